```python
import math
import jax, jax.numpy as jnp
from jax import lax
import numpy as np

D_MODEL = 1024
BATCH = 8
SEQ = 4096
DEPTH = 1

MEM_LEN = 256
HEAD_DIM = 64
N_ATTN_HEADS = 8
ATTN_WIDTH = N_ATTN_HEADS * HEAD_DIM
CONV_WIDTH = D_MODEL - ATTN_WIDTH
N_CONV_GROUPS = CONV_WIDTH // HEAD_DIM
MIX_WIDTH = ATTN_WIDTH + CONV_WIDTH
IN_PROJ_COLS = 3 * ATTN_WIDTH + 3 * CONV_WIDTH
DILATED_PATTERNS = ((128, 1), (512, 4), (2048, 16))
SEQ_PAD_MULT = max(w for w, _ in DILATED_PATTERNS)
N_BUCKETS = 32
BUCKET_MAX_EXACT = N_BUCKETS // 2
BUCKET_MAX_DISTANCE = 2048
SHORT_CONV_K = 3
FFN_CONV_K = 3
D_FF = 2816
N_MEM_HEADS = 4
MEM_HEAD_DIM = D_MODEL // N_MEM_HEADS
EPS = 1e-6

kernel_name = "hymba_dilated_shortconv_convffn_memxattn"


def rms_norm(x, g):
    xf = x.astype(jnp.float32)
    y = xf * lax.rsqrt(jnp.mean(xf * xf, axis=-1, keepdims=True) + EPS)
    return (y * g.astype(jnp.float32)).astype(x.dtype)


def causal_dwconv(u, w):
    k_width = w.shape[0]
    s = u.shape[1]
    up = jnp.pad(u, ((0, 0), (k_width - 1, 0), (0, 0)))
    out = up[:, 0:s, :] * w[0]
    for k in range(1, k_width):
        out = out + up[:, k:k + s, :] * w[k]
    return out


def t5_bucket(distance):
    d = jnp.maximum(distance, 1).astype(jnp.float32)
    large = BUCKET_MAX_EXACT + (
        jnp.log(d / BUCKET_MAX_EXACT) / math.log(BUCKET_MAX_DISTANCE / BUCKET_MAX_EXACT)
        * (N_BUCKETS - BUCKET_MAX_EXACT)).astype(jnp.int32)
    large = jnp.minimum(large, N_BUCKETS - 1)
    return jnp.where(distance < BUCKET_MAX_EXACT, distance, large)


def dilated_window_attention(q, k, v, rel_bias, window, dilation):
    b, h, sp, dh = q.shape
    w = window // dilation
    n_sub = sp // dilation
    nb = n_sub // w

    def to_blocks(t):
        t = t.reshape(b, h, n_sub, dilation, -1).transpose(0, 1, 3, 2, 4)
        return t.reshape(b, h, dilation, nb, w, t.shape[-1])

    def from_blocks(t):
        t = t.reshape(b, h, dilation, n_sub, -1).transpose(0, 1, 3, 2, 4)
        return t.reshape(b, h, sp, t.shape[-1])

    def with_prev(t):
        prev = jnp.pad(t, ((0, 0), (0, 0), (0, 0), (1, 0), (0, 0), (0, 0)))[:, :, :, :-1]
        return jnp.concatenate([prev, t], axis=4)

    qb = to_blocks(q)
    kk = with_prev(to_blocks(k))
    vv = with_prev(to_blocks(v))

    logits = jnp.einsum('bhrnid,bhrnjd->bhrnij', qb, kk).astype(jnp.float32)
    qi = jnp.arange(w)[:, None]
    kj = jnp.arange(2 * w)[None, :]
    steps = qi + w - kj
    valid_local = (steps >= 0) & (steps <= w)
    block_idx = jnp.arange(nb)[:, None, None]
    valid = valid_local[None] & ((block_idx > 0) | (kj >= w)[None])
    bucket = t5_bucket(jnp.clip(steps, 0, w) * dilation)
    bias = rel_bias.astype(jnp.float32)[:, bucket]
    logits = logits + bias[None, :, None, None]
    logits = jnp.where(valid, logits, -jnp.inf)
    m = jnp.max(logits, axis=-1, keepdims=True)
    p = jnp.exp(logits - m)
    s = jnp.sum(p, axis=-1, keepdims=True)
    o = jnp.einsum('bhrnij,bhrnjd->bhrnid', p, vv.astype(jnp.float32)) / s
    return from_blocks(o), from_blocks(m), from_blocks(s)


def hybrid_mixer(h, rel_bias, w_in, w_short_conv, g_attn_out, g_conv_out, w_out):
    b, s, _ = h.shape
    proj = h @ w_in
    q, k, v, gate_b, gate_c, x_in = jnp.split(
        proj, [ATTN_WIDTH, 2 * ATTN_WIDTH, 3 * ATTN_WIDTH,
               3 * ATTN_WIDTH + CONV_WIDTH, 3 * ATTN_WIDTH + 2 * CONV_WIDTH], axis=-1)

    sp = ((s + SEQ_PAD_MULT - 1) // SEQ_PAD_MULT) * SEQ_PAD_MULT

    def heads(t):
        t = t.reshape(b, s, N_ATTN_HEADS, HEAD_DIM).transpose(0, 2, 1, 3)
        return jnp.pad(t, ((0, 0), (0, 0), (0, sp - s), (0, 0)))

    qh = heads(q) * (HEAD_DIM ** -0.5)
    kh, vh = heads(k), heads(v)
    branches = [dilated_window_attention(qh, kh, vh, rel_bias, w, d) for (w, d) in DILATED_PATTERNS]
    m_all = branches[0][1]
    for _, m_i, _ in branches[1:]:
        m_all = jnp.maximum(m_all, m_i)
    num = jnp.zeros_like(branches[0][0])
    den = jnp.zeros_like(m_all)
    for o_i, m_i, s_i in branches:
        wt = s_i * jnp.exp(m_i - m_all)
        num = num + wt * o_i
        den = den + wt
    attn = (num / den)[:, :, :s].transpose(0, 2, 1, 3).reshape(b, s, ATTN_WIDTH).astype(h.dtype)

    conv = gate_b * causal_dwconv(gate_c * x_in, w_short_conv)

    mixed = jnp.concatenate([rms_norm(attn, g_attn_out), rms_norm(conv, g_conv_out)], axis=-1)
    return mixed @ w_out


def memory_cross_attention(h, mem_n, w_xq, w_xk, w_xv, w_xo):
    b, s, _ = h.shape
    q = (h @ w_xq).reshape(b, s, N_MEM_HEADS, MEM_HEAD_DIM)
    k = (mem_n @ w_xk).reshape(b, MEM_LEN, N_MEM_HEADS, MEM_HEAD_DIM)
    v = (mem_n @ w_xv).reshape(b, MEM_LEN, N_MEM_HEADS, MEM_HEAD_DIM)
    logits = jnp.einsum('bshd,bmhd->bhsm', q, k).astype(jnp.float32) * (MEM_HEAD_DIM ** -0.5)
    p = jax.nn.softmax(logits, axis=-1)
    o = jnp.einsum('bhsm,bmhd->bshd', p, v.astype(jnp.float32)).astype(h.dtype)
    return o.reshape(b, s, D_MODEL) @ w_xo


def conv_ffn(h, w_up, w_ffn_conv, b_ffn_conv, w_down):
    up = causal_dwconv(h @ w_up, w_ffn_conv) + b_ffn_conv
    gate, val = jnp.split(up, 2, axis=-1)
    return (jax.nn.silu(gate) * val) @ w_down


def _fwd_setup_inputs(seed: int = 0) -> dict:
    key = jax.random.key(seed)
    ks = iter(jax.random.split(key, 32))
    f32 = jnp.float32
    L = DEPTH

    def dense(shape, fan_in):
        return jax.random.normal(next(ks), shape, f32) * fan_in ** -0.5

    def gain(shape):
        return 1.0 + 0.02 * jax.random.normal(next(ks), shape, f32)

    return {
        "x": jax.random.normal(next(ks), (BATCH, SEQ, D_MODEL), f32),
        "mem": jax.random.normal(next(ks), (BATCH, MEM_LEN, D_MODEL), f32),
        "rel_bias": 0.2 * jax.random.normal(next(ks), (N_ATTN_HEADS, N_BUCKETS), f32),
        "g_mix": gain((L, D_MODEL)),
        "w_in": dense((L, D_MODEL, IN_PROJ_COLS), D_MODEL),
        "w_short_conv": dense((L, SHORT_CONV_K, CONV_WIDTH), SHORT_CONV_K),
        "g_attn_out": gain((L, ATTN_WIDTH)),
        "g_conv_out": gain((L, CONV_WIDTH)),
        "w_out": dense((L, MIX_WIDTH, D_MODEL), MIX_WIDTH),
        "g_xattn": gain((L, D_MODEL)),
        "g_mem": gain((L, D_MODEL)),
        "w_xq": dense((L, D_MODEL, D_MODEL), D_MODEL),
        "w_xk": dense((L, D_MODEL, D_MODEL), D_MODEL),
        "w_xv": dense((L, D_MODEL, D_MODEL), D_MODEL),
        "w_xo": dense((L, D_MODEL, D_MODEL), D_MODEL),
        "g_ffn": gain((L, D_MODEL)),
        "w_up": dense((L, D_MODEL, 2 * D_FF), D_MODEL),
        "w_ffn_conv": dense((L, FFN_CONV_K, 2 * D_FF), FFN_CONV_K),
        "b_ffn_conv": 0.02 * jax.random.normal(next(ks), (L, 2 * D_FF), f32),
        "w_down": dense((L, D_FF, D_MODEL), D_FF),
        "g_final": gain((D_MODEL,)),
    }


def _fwd_reference(x, mem, rel_bias, g_mix, w_in, w_short_conv, g_attn_out, g_conv_out, w_out,
              g_xattn, g_mem, w_xq, w_xk, w_xv, w_xo, g_ffn, w_up, w_ffn_conv, b_ffn_conv,
              w_down, g_final):
    for l in range(DEPTH):
        x = x + hybrid_mixer(rms_norm(x, g_mix[l]), rel_bias, w_in[l], w_short_conv[l],
                             g_attn_out[l], g_conv_out[l], w_out[l])
        x = x + memory_cross_attention(rms_norm(x, g_xattn[l]), rms_norm(mem, g_mem[l]),
                                       w_xq[l], w_xk[l], w_xv[l], w_xo[l])
        x = x + conv_ffn(rms_norm(x, g_ffn[l]), w_up[l], w_ffn_conv[l], b_ffn_conv[l], w_down[l])
    return rms_norm(x, g_final)


import jax as _jax
import jax.numpy as _jnp

TWIN_FORMAT = 'train_step'
FWD_PARAMS = ['x', 'mem', 'rel_bias', 'g_mix', 'w_in', 'w_short_conv', 'g_attn_out', 'g_conv_out', 'w_out', 'g_xattn', 'g_mem', 'w_xq', 'w_xk', 'w_xv', 'w_xo', 'g_ffn', 'w_up', 'w_ffn_conv', 'b_ffn_conv', 'w_down', 'g_final']
TWIN_WEIGHTS = ['rel_bias', 'g_mix', 'w_in', 'w_short_conv', 'g_attn_out', 'g_conv_out', 'w_out', 'g_xattn', 'g_mem', 'w_xq', 'w_xk', 'w_xv', 'w_xo', 'g_ffn', 'w_up', 'w_ffn_conv', 'b_ffn_conv', 'w_down', 'g_final']
TWIN_DIFF_INPUT = 'x'
TWIN_INPUTS = ['x', 'mem', 'rel_bias', 'g_mix', 'w_in', 'w_short_conv', 'g_attn_out', 'g_conv_out', 'w_out', 'g_xattn', 'g_mem', 'w_xq', 'w_xk', 'w_xv', 'w_xo', 'g_ffn', 'w_up', 'w_ffn_conv', 'b_ffn_conv', 'w_down', 'g_final', 'loss_target', 'm_rel_bias', 'm_g_mix', 'm_w_in', 'm_w_short_conv', 'm_g_attn_out', 'm_g_conv_out', 'm_w_out', 'm_g_xattn', 'm_g_mem', 'm_w_xq', 'm_w_xk', 'm_w_xv', 'm_w_xo', 'm_g_ffn', 'm_w_up', 'm_w_ffn_conv', 'm_b_ffn_conv', 'm_w_down', 'm_g_final', 'v_rel_bias', 'v_g_mix', 'v_w_in', 'v_w_short_conv', 'v_g_attn_out', 'v_g_conv_out', 'v_w_out', 'v_g_xattn', 'v_g_mem', 'v_w_xq', 'v_w_xk', 'v_w_xv', 'v_w_xo', 'v_g_ffn', 'v_w_up', 'v_w_ffn_conv', 'v_b_ffn_conv', 'v_w_down', 'v_g_final']
TWIN_OUTPUTS = ['loss', 'grad_x', 'grad_rel_bias', 'grad_g_mix', 'grad_w_in', 'grad_w_short_conv', 'grad_g_attn_out', 'grad_g_conv_out', 'grad_w_out', 'grad_g_xattn', 'grad_g_mem', 'grad_w_xq', 'grad_w_xk', 'grad_w_xv', 'grad_w_xo', 'grad_g_ffn', 'grad_w_up', 'grad_w_ffn_conv', 'grad_b_ffn_conv', 'grad_w_down', 'grad_g_final', 'delta_rel_bias', 'delta_g_mix', 'delta_w_in', 'delta_w_short_conv', 'delta_g_attn_out', 'delta_g_conv_out', 'delta_w_out', 'delta_g_xattn', 'delta_g_mem', 'delta_w_xq', 'delta_w_xk', 'delta_w_xv', 'delta_w_xo', 'delta_g_ffn', 'delta_w_up', 'delta_w_ffn_conv', 'delta_b_ffn_conv', 'delta_w_down', 'delta_g_final', 'new_m_rel_bias', 'new_m_g_mix', 'new_m_w_in', 'new_m_w_short_conv', 'new_m_g_attn_out', 'new_m_g_conv_out', 'new_m_w_out', 'new_m_g_xattn', 'new_m_g_mem', 'new_m_w_xq', 'new_m_w_xk', 'new_m_w_xv', 'new_m_w_xo', 'new_m_g_ffn', 'new_m_w_up', 'new_m_w_ffn_conv', 'new_m_b_ffn_conv', 'new_m_w_down', 'new_m_g_final', 'new_v_rel_bias', 'new_v_g_mix', 'new_v_w_in', 'new_v_w_short_conv', 'new_v_g_attn_out', 'new_v_g_conv_out', 'new_v_w_out', 'new_v_g_xattn', 'new_v_g_mem', 'new_v_w_xq', 'new_v_w_xk', 'new_v_w_xv', 'new_v_w_xo', 'new_v_g_ffn', 'new_v_w_up', 'new_v_w_ffn_conv', 'new_v_b_ffn_conv', 'new_v_w_down', 'new_v_g_final']
TWIN_LEAF_KINDS = {'loss': 'loss', 'grad_x': 'grad_x', 'grad_rel_bias': 'grad_w', 'grad_g_mix': 'grad_w', 'grad_w_in': 'grad_w', 'grad_w_short_conv': 'grad_w', 'grad_g_attn_out': 'grad_w', 'grad_g_conv_out': 'grad_w', 'grad_w_out': 'grad_w', 'grad_g_xattn': 'grad_w', 'grad_g_mem': 'grad_w', 'grad_w_xq': 'grad_w', 'grad_w_xk': 'grad_w', 'grad_w_xv': 'grad_w', 'grad_w_xo': 'grad_w', 'grad_g_ffn': 'grad_w', 'grad_w_up': 'grad_w', 'grad_w_ffn_conv': 'grad_w', 'grad_b_ffn_conv': 'grad_w', 'grad_w_down': 'grad_w', 'grad_g_final': 'grad_w', 'delta_rel_bias': 'delta_w', 'delta_g_mix': 'delta_w', 'delta_w_in': 'delta_w', 'delta_w_short_conv': 'delta_w', 'delta_g_attn_out': 'delta_w', 'delta_g_conv_out': 'delta_w', 'delta_w_out': 'delta_w', 'delta_g_xattn': 'delta_w', 'delta_g_mem': 'delta_w', 'delta_w_xq': 'delta_w', 'delta_w_xk': 'delta_w', 'delta_w_xv': 'delta_w', 'delta_w_xo': 'delta_w', 'delta_g_ffn': 'delta_w', 'delta_w_up': 'delta_w', 'delta_w_ffn_conv': 'delta_w', 'delta_b_ffn_conv': 'delta_w', 'delta_w_down': 'delta_w', 'delta_g_final': 'delta_w', 'new_m_rel_bias': 'new_m', 'new_m_g_mix': 'new_m', 'new_m_w_in': 'new_m', 'new_m_w_short_conv': 'new_m', 'new_m_g_attn_out': 'new_m', 'new_m_g_conv_out': 'new_m', 'new_m_w_out': 'new_m', 'new_m_g_xattn': 'new_m', 'new_m_g_mem': 'new_m', 'new_m_w_xq': 'new_m', 'new_m_w_xk': 'new_m', 'new_m_w_xv': 'new_m', 'new_m_w_xo': 'new_m', 'new_m_g_ffn': 'new_m', 'new_m_w_up': 'new_m', 'new_m_w_ffn_conv': 'new_m', 'new_m_b_ffn_conv': 'new_m', 'new_m_w_down': 'new_m', 'new_m_g_final': 'new_m', 'new_v_rel_bias': 'new_v', 'new_v_g_mix': 'new_v', 'new_v_w_in': 'new_v', 'new_v_w_short_conv': 'new_v', 'new_v_g_attn_out': 'new_v', 'new_v_g_conv_out': 'new_v', 'new_v_w_out': 'new_v', 'new_v_g_xattn': 'new_v', 'new_v_g_mem': 'new_v', 'new_v_w_xq': 'new_v', 'new_v_w_xk': 'new_v', 'new_v_w_xv': 'new_v', 'new_v_w_xo': 'new_v', 'new_v_g_ffn': 'new_v', 'new_v_w_up': 'new_v', 'new_v_w_ffn_conv': 'new_v', 'new_v_b_ffn_conv': 'new_v', 'new_v_w_down': 'new_v', 'new_v_g_final': 'new_v'}


def _forward(args):
    return _fwd_reference(*[args[k] for k in FWD_PARAMS])


def _output_shape():
    def fwd():
        inp = _fwd_setup_inputs(0)
        return _fwd_reference(*[inp[k] for k in FWD_PARAMS])
    out = _jax.eval_shape(fwd)
    return out.shape, out.dtype

N_MICROBATCH = 1
ADAM_LR = 0.001
ADAM_B1 = 0.9
ADAM_B2 = 0.999
ADAM_EPS = 1e-08
ADAM_WD = 0.01
ADAM_STEP = 10
PER_EXAMPLE_BATCH_AXIS = {'x': 0, 'mem': 0, 'loss_target': 0}
SHARED_INPUTS = []
_WEIGHT_DTYPES = {'rel_bias': _jnp.float32, 'g_mix': _jnp.float32, 'w_in': _jnp.float32, 'w_short_conv': _jnp.float32, 'g_attn_out': _jnp.float32, 'g_conv_out': _jnp.float32, 'w_out': _jnp.float32, 'g_xattn': _jnp.float32, 'g_mem': _jnp.float32, 'w_xq': _jnp.float32, 'w_xk': _jnp.float32, 'w_xv': _jnp.float32, 'w_xo': _jnp.float32, 'g_ffn': _jnp.float32, 'w_up': _jnp.float32, 'w_ffn_conv': _jnp.float32, 'b_ffn_conv': _jnp.float32, 'w_down': _jnp.float32, 'g_final': _jnp.float32}
MOMENT_SCALE = {'rel_bias': 2.359782e-01, 'g_mix': 2.287294e-01, 'w_in': 1.353783e-01, 'w_short_conv': 1.404939e-01, 'g_attn_out': 1.295556e-01, 'g_conv_out': 1.596682e-01, 'w_out': 1.362941e-01, 'g_xattn': 1.491750e-02, 'g_mem': 2.181285e-02, 'w_xq': 1.417182e-02, 'w_xk': 1.418633e-02, 'w_xv': 1.442255e-02, 'w_xo': 1.435163e-02, 'g_ffn': 1.038242e-01, 'w_up': 4.274379e-02, 'w_ffn_conv': 4.348129e-02, 'b_ffn_conv': 4.190328e-02, 'w_down': 6.991783e-02, 'g_final': 3.203353e+01}


def _to_microbatches(a, axis):
    t = _jnp.moveaxis(a, axis, 0)
    t = t.reshape((N_MICROBATCH, t.shape[0] // N_MICROBATCH) + t.shape[1:])
    return _jnp.moveaxis(t, 1, axis + 1)


def setup_inputs(seed: int = 0) -> dict:
    inp = _fwd_setup_inputs(seed)
    key = _jax.random.fold_in(_jax.random.key(seed), 7919)
    shape, _ = _output_shape()
    out = dict(inp)
    out["loss_target"] = _jax.random.normal(_jax.random.fold_in(key, 0), shape, _jnp.float32)
    for i, name in enumerate(TWIN_WEIGHTS):
        w = inp[name].astype(_jnp.float32)
        if MOMENT_SCALE is None:
            s = _jnp.sqrt(_jnp.mean(_jnp.square(w)) + 1e-30)
        else:
            s = MOMENT_SCALE[name]
        km, kv = _jax.random.split(_jax.random.fold_in(key, i + 1))
        out[name] = w
        out["m_" + name] = s * _jax.random.normal(km, w.shape, _jnp.float32)
        out["v_" + name] = (s * s) * _jax.random.uniform(kv, w.shape, _jnp.float32, 0.5, 1.5)
    if N_MICROBATCH > 1:
        for name, axis in PER_EXAMPLE_BATCH_AXIS.items():
            out[name] = _to_microbatches(out[name], axis)
    return {'x': out['x'], 'mem': out['mem'], 'rel_bias': out['rel_bias'], 'g_mix': out['g_mix'], 'w_in': out['w_in'], 'w_short_conv': out['w_short_conv'], 'g_attn_out': out['g_attn_out'], 'g_conv_out': out['g_conv_out'], 'w_out': out['w_out'], 'g_xattn': out['g_xattn'], 'g_mem': out['g_mem'], 'w_xq': out['w_xq'], 'w_xk': out['w_xk'], 'w_xv': out['w_xv'], 'w_xo': out['w_xo'], 'g_ffn': out['g_ffn'], 'w_up': out['w_up'], 'w_ffn_conv': out['w_ffn_conv'], 'b_ffn_conv': out['b_ffn_conv'], 'w_down': out['w_down'], 'g_final': out['g_final'], 'loss_target': out['loss_target'], 'm_rel_bias': out['m_rel_bias'], 'm_g_mix': out['m_g_mix'], 'm_w_in': out['m_w_in'], 'm_w_short_conv': out['m_w_short_conv'], 'm_g_attn_out': out['m_g_attn_out'], 'm_g_conv_out': out['m_g_conv_out'], 'm_w_out': out['m_w_out'], 'm_g_xattn': out['m_g_xattn'], 'm_g_mem': out['m_g_mem'], 'm_w_xq': out['m_w_xq'], 'm_w_xk': out['m_w_xk'], 'm_w_xv': out['m_w_xv'], 'm_w_xo': out['m_w_xo'], 'm_g_ffn': out['m_g_ffn'], 'm_w_up': out['m_w_up'], 'm_w_ffn_conv': out['m_w_ffn_conv'], 'm_b_ffn_conv': out['m_b_ffn_conv'], 'm_w_down': out['m_w_down'], 'm_g_final': out['m_g_final'], 'v_rel_bias': out['v_rel_bias'], 'v_g_mix': out['v_g_mix'], 'v_w_in': out['v_w_in'], 'v_w_short_conv': out['v_w_short_conv'], 'v_g_attn_out': out['v_g_attn_out'], 'v_g_conv_out': out['v_g_conv_out'], 'v_w_out': out['v_w_out'], 'v_g_xattn': out['v_g_xattn'], 'v_g_mem': out['v_g_mem'], 'v_w_xq': out['v_w_xq'], 'v_w_xk': out['v_w_xk'], 'v_w_xv': out['v_w_xv'], 'v_w_xo': out['v_w_xo'], 'v_g_ffn': out['v_g_ffn'], 'v_w_up': out['v_w_up'], 'v_w_ffn_conv': out['v_w_ffn_conv'], 'v_b_ffn_conv': out['v_b_ffn_conv'], 'v_w_down': out['v_w_down'], 'v_g_final': out['v_g_final']}


def _loss(weights, diff, rest, loss_target):
    with _jax.named_scope("forward"):
        args = {**rest, TWIN_DIFF_INPUT: diff, **{k: w.astype(_WEIGHT_DTYPES[k]) for k, w in weights.items()}}
        y = _forward(args)
    with _jax.named_scope("loss_head"):
        err = _jnp.square(y.astype(_jnp.float32) - loss_target)
        return 0.5 * _jnp.sum(_jnp.mean(err, axis=-1)) if err.ndim else 0.5 * err


def _adamw(w, g, m, v):
    m = ADAM_B1 * m + (1.0 - ADAM_B1) * g
    v = ADAM_B2 * v + (1.0 - ADAM_B2) * _jnp.square(g)
    m_hat = m / (1.0 - ADAM_B1 ** ADAM_STEP)
    v_hat = v / (1.0 - ADAM_B2 ** ADAM_STEP)
    delta = -ADAM_LR * (m_hat / (_jnp.sqrt(v_hat) + ADAM_EPS) + ADAM_WD * w)
    return delta, m, v


def reference(x, mem, rel_bias, g_mix, w_in, w_short_conv, g_attn_out, g_conv_out, w_out, g_xattn, g_mem, w_xq, w_xk, w_xv, w_xo, g_ffn, w_up, w_ffn_conv, b_ffn_conv, w_down, g_final, loss_target, m_rel_bias, m_g_mix, m_w_in, m_w_short_conv, m_g_attn_out, m_g_conv_out, m_w_out, m_g_xattn, m_g_mem, m_w_xq, m_w_xk, m_w_xv, m_w_xo, m_g_ffn, m_w_up, m_w_ffn_conv, m_b_ffn_conv, m_w_down, m_g_final, v_rel_bias, v_g_mix, v_w_in, v_w_short_conv, v_g_attn_out, v_g_conv_out, v_w_out, v_g_xattn, v_g_mem, v_w_xq, v_w_xk, v_w_xv, v_w_xo, v_g_ffn, v_w_up, v_w_ffn_conv, v_b_ffn_conv, v_w_down, v_g_final):
    given = dict(x=x, mem=mem, rel_bias=rel_bias, g_mix=g_mix, w_in=w_in, w_short_conv=w_short_conv, g_attn_out=g_attn_out, g_conv_out=g_conv_out, w_out=w_out, g_xattn=g_xattn, g_mem=g_mem, w_xq=w_xq, w_xk=w_xk, w_xv=w_xv, w_xo=w_xo, g_ffn=g_ffn, w_up=w_up, w_ffn_conv=w_ffn_conv, b_ffn_conv=b_ffn_conv, w_down=w_down, g_final=g_final, loss_target=loss_target, m_rel_bias=m_rel_bias, m_g_mix=m_g_mix, m_w_in=m_w_in, m_w_short_conv=m_w_short_conv, m_g_attn_out=m_g_attn_out, m_g_conv_out=m_g_conv_out, m_w_out=m_w_out, m_g_xattn=m_g_xattn, m_g_mem=m_g_mem, m_w_xq=m_w_xq, m_w_xk=m_w_xk, m_w_xv=m_w_xv, m_w_xo=m_w_xo, m_g_ffn=m_g_ffn, m_w_up=m_w_up, m_w_ffn_conv=m_w_ffn_conv, m_b_ffn_conv=m_b_ffn_conv, m_w_down=m_w_down, m_g_final=m_g_final, v_rel_bias=v_rel_bias, v_g_mix=v_g_mix, v_w_in=v_w_in, v_w_short_conv=v_w_short_conv, v_g_attn_out=v_g_attn_out, v_g_conv_out=v_g_conv_out, v_w_out=v_w_out, v_g_xattn=v_g_xattn, v_g_mem=v_g_mem, v_w_xq=v_w_xq, v_w_xk=v_w_xk, v_w_xv=v_w_xv, v_w_xo=v_w_xo, v_g_ffn=v_g_ffn, v_w_up=v_w_up, v_w_ffn_conv=v_w_ffn_conv, v_b_ffn_conv=v_b_ffn_conv, v_w_down=v_w_down, v_g_final=v_g_final)
    weights = {n: given[n] for n in TWIN_WEIGHTS}
    shared = {n: given[n] for n in SHARED_INPUTS}
    per_example = {n: given[n] for n in ['x', 'mem']}
    grad_fn = _jax.value_and_grad(_loss, argnums=(0, 1))

    def one_microbatch(ex, loss_target):
        ex = dict(ex)
        diff = ex.pop(TWIN_DIFF_INPUT)
        return grad_fn(weights, diff, {**shared, **ex}, loss_target)

    if N_MICROBATCH == 1:
        loss, (grad_w, grad_x) = one_microbatch(per_example, given["loss_target"])
    else:
        def body(carry, xs):
            loss_sum, grad_sum = carry
            l_k, (gw_k, gx_k) = one_microbatch(xs[0], xs[1])
            with _jax.named_scope("update"):
                return (loss_sum + l_k, _jax.tree.map(_jnp.add, grad_sum, gw_k)), gx_k

        init = (_jnp.zeros((), _jnp.float32), _jax.tree.map(_jnp.zeros_like, weights))
        (loss, grad_w), grad_x = _jax.lax.scan(body, init, (per_example, given["loss_target"]))
    with _jax.named_scope("update"):
        delta_w, new_m, new_v = {}, {}, {}
        for n in TWIN_WEIGHTS:
            delta_w[n], new_m[n], new_v[n] = _adamw(weights[n], grad_w[n], given["m_" + n], given["v_" + n])
    return (loss, grad_x, *[grad_w[n] for n in TWIN_WEIGHTS], *[delta_w[n] for n in TWIN_WEIGHTS],
            *[new_m[n] for n in TWIN_WEIGHTS], *[new_v[n] for n in TWIN_WEIGHTS])
```

```python
import functools
import math

import numpy as np
import jax
import jax.numpy as jnp
from jax import lax
from jax.experimental import pallas as pl
from jax.experimental.pallas import tpu as pltpu

F32 = jnp.float32
BF16 = jnp.bfloat16
MESH = pl.DeviceIdType.MESH

N_DEV = 8
D_MODEL = 1024
ATTN_W = 512
CONV_W = 512
N_HEADS = 8
HEAD_DIM = 64
WIN = 128
DILATIONS = (1, 4, 16)
N_BUCKETS = 32
BUCKET_MAX_EXACT = 16
BUCKET_MAX_DISTANCE = 2048
N_MEM_HEADS = 4
MEM_HEAD_DIM = 256
D_FF = 2816
IN_COLS = 3072
IN_CHUNK = IN_COLS // N_DEV
UP_CHUNK = 2 * D_FF // N_DEV
EPS = 1e-6

ADAM_LR = 0.001
ADAM_B1 = 0.9
ADAM_B2 = 0.999
ADAM_EPS = 1e-08
ADAM_WD = 0.01
ADAM_STEP = 10

SUBLANES = 8
TM = 512
TM_FFN = 256
VMEM_LIMIT = 56 * 1024 * 1024

ROW_RELB, ROW_GMIX, ROW_GXATTN, ROW_GMEM, ROW_GFFN, ROW_GFINAL, ROW_GAC = 0, 8, 16, 24, 32, 40, 48
ROW_WSC, ROW_BFC, ROW_WFC, SMALL_ROWS = 56, 64, 72, 96


def _cparams(n_grid):
    return pltpu.CompilerParams(dimension_semantics=("arbitrary",) * n_grid, vmem_limit_bytes=VMEM_LIMIT)


def _full(shape):
    nd = len(shape)
    return pl.BlockSpec(tuple(shape), lambda *_: (0,) * nd)


def _rms(x):
    r = lax.rsqrt(jnp.mean(x * x, axis=-1, keepdims=True) + EPS)
    return x * r, r


def _rms_bwd(xh, r, g, dy):
    dxh = dy * g
    return r * (dxh - xh * jnp.mean(dxh * xh, axis=-1, keepdims=True))


def _shift_down(u, halo, k):
    ru = pltpu.roll(u, k, 0)
    rh = pltpu.roll(halo, k, 0)
    row = lax.broadcasted_iota(jnp.int32, rh.shape, 0)
    head = jnp.where(row < k, rh, ru[0:SUBLANES])
    return jnp.concatenate([head, ru[SUBLANES:]], axis=0)


def _shift_up(u, halo, k):
    tm = u.shape[0]
    ru = pltpu.roll(u, tm - k, 0)
    rh = pltpu.roll(halo, SUBLANES - k, 0)
    row = lax.broadcasted_iota(jnp.int32, rh.shape, 0)
    tail = jnp.where(row >= SUBLANES - k, rh, ru[tm - SUBLANES:])
    return jnp.concatenate([ru[:tm - SUBLANES], tail], axis=0)


def _causal_conv3(u, halo, w_ref):
    return (_shift_down(u, halo, 2) * w_ref[0:1, :] + _shift_down(u, halo, 1) * w_ref[1:2, :]) + u * w_ref[2:3, :]


def _dot(a, b):
    return jnp.dot(a, b, preferred_element_type=F32)


def _dot_nt(a, b):
    return lax.dot_general(a, b, (((1,), (1,)), ((), ())), preferred_element_type=F32)


def _dot_tn(a, b):
    return lax.dot_general(a, b, (((0,), (0,)), ((), ())), preferred_element_type=F32)


def _sigmoid(x):
    return 1.0 / (1.0 + jnp.exp(-x))


def _bucket_tables():
    qi = np.arange(WIN)[:, None]
    kj = np.arange(2 * WIN)[None, :]
    steps = np.clip(qi + WIN - kj, 0, WIN)
    out = []
    for d in DILATIONS:
        dist = steps * d
        dd = np.maximum(dist, 1).astype(np.float32)
        large = BUCKET_MAX_EXACT + (
            np.log(dd / np.float32(BUCKET_MAX_EXACT)) / np.float32(math.log(BUCKET_MAX_DISTANCE / BUCKET_MAX_EXACT))
            * np.float32(N_BUCKETS - BUCKET_MAX_EXACT)).astype(np.int32)
        large = np.minimum(large, N_BUCKETS - 1)
        out.append(np.where(dist < BUCKET_MAX_EXACT, dist, large).astype(np.int32))
    return jnp.asarray(np.stack(out))


def _bias_fwd(rel_bias, buckets):
    def body(rb_ref, bk_ref, o_ref):
        for p in range(3):
            bk = bk_ref[p]
            for h in range(N_HEADS):
                acc = jnp.zeros((WIN, 2 * WIN), F32)
                for b in range(N_BUCKETS):
                    acc = jnp.where(bk == b, rb_ref[h, b], acc)
                o_ref[p, h] = acc

    return pl.pallas_call(
        body, name="bias_fwd",
        out_shape=jax.ShapeDtypeStruct((3, N_HEADS, WIN, 2 * WIN), F32),
        in_specs=[pl.BlockSpec(memory_space=pltpu.SMEM), pl.BlockSpec(memory_space=pltpu.VMEM)],
        out_specs=pl.BlockSpec(memory_space=pltpu.VMEM),
    )(rel_bias, buckets)


def _bias_bwd(dbias, buckets):
    def body(db_ref, bk_ref, o_ref):
        lane = lax.broadcasted_iota(jnp.int32, (1, D_MODEL), 1)
        rows = []
        for h in range(N_HEADS):
            row = jnp.zeros((1, D_MODEL), F32)
            for b in range(N_BUCKETS):
                tot = jnp.zeros((1, 1), F32)
                for p in range(3):
                    sel = jnp.where(bk_ref[p] == b, db_ref[p, h], 0.0)
                    tot = tot + jnp.sum(jnp.sum(sel, axis=0, keepdims=True), axis=1, keepdims=True)
                row = jnp.where(lane == b, tot, row)
            rows.append(row)
        o_ref[...] = jnp.concatenate(rows, axis=0)

    return pl.pallas_call(
        body, name="bias_bwd",
        out_shape=jax.ShapeDtypeStruct((N_HEADS, D_MODEL), F32),
        in_specs=[pl.BlockSpec(memory_space=pltpu.VMEM), pl.BlockSpec(memory_space=pltpu.VMEM)],
        out_specs=pl.BlockSpec(memory_space=pltpu.VMEM),
    )(dbias, buckets)


def _rms_proj(x, g_mix, w_in_g):
    s = x.shape[0]

    def body(x_ref, g_ref, w_ref, h_ref, q_ref, k_ref, v_ref, gb_ref, gc_ref, xi_ref):
        xh, _ = _rms(x_ref[...])
        h = (xh * g_ref[...]).astype(BF16)
        h_ref[...] = h
        proj = jnp.concatenate([_dot(h, w_ref[j]) for j in range(N_DEV)], axis=1)
        q_ref[...] = (proj[:, 0:512] * (HEAD_DIM ** -0.5)).astype(BF16)
        k_ref[...] = proj[:, 512:1024].astype(BF16)
        v_ref[...] = proj[:, 1024:1536].astype(BF16)
        gb_ref[...] = proj[:, 1536:2048]
        gc_ref[...] = proj[:, 2048:2560]
        xi_ref[...] = proj[:, 2560:3072]

    row = lambda n: pl.BlockSpec((TM, n), lambda i: (i, 0))
    return pl.pallas_call(
        body, name="rms_proj", grid=(s // TM,),
        out_shape=[jax.ShapeDtypeStruct((s, D_MODEL), BF16)] + [jax.ShapeDtypeStruct((s, 512), BF16)] * 3
        + [jax.ShapeDtypeStruct((s, 512), F32)] * 3,
        in_specs=[row(D_MODEL), _full(g_mix.shape), _full(w_in_g.shape)],
        out_specs=[row(D_MODEL)] + [row(512)] * 6,
        compiler_params=_cparams(1),
    )(x, g_mix, w_in_g)


def _band_mask(blk):
    qi = lax.broadcasted_iota(jnp.int32, (WIN, 2 * WIN), 0)
    kj = lax.broadcasted_iota(jnp.int32, (WIN, 2 * WIN), 1)
    steps = qi + WIN - kj
    return (steps >= 0) & (steps <= WIN) & (kj >= jnp.where(blk > 0, 0, WIN))


def _swa_fwd(qv, kv, vv, bias, dil):
    rows = qv.shape[0]
    nb = rows // WIN

    def body(q_ref, kp_ref, kc_ref, vp_ref, vc_ref, b_ref, num_ref, m_ref, s_ref):
        valid = _band_mask(pl.program_id(1))
        for h in range(N_HEADS):
            sl = slice(h * HEAD_DIM, (h + 1) * HEAD_DIM)
            kh = jnp.concatenate([kp_ref[:, sl], kc_ref[:, sl]], axis=0)
            vh = jnp.concatenate([vp_ref[:, sl], vc_ref[:, sl]], axis=0)
            lg = _dot_nt(q_ref[:, sl], kh) + b_ref[h]
            lg = jnp.where(valid, lg, -jnp.inf)
            m = jnp.max(lg, axis=-1, keepdims=True)
            p = jnp.exp(lg - m)
            num_ref[:, sl] = _dot(p.astype(BF16), vh)
            m_ref[:, sl] = jnp.broadcast_to(m, (WIN, HEAD_DIM))
            s_ref[:, sl] = jnp.broadcast_to(jnp.sum(p, axis=-1, keepdims=True), (WIN, HEAD_DIM))

    cur = pl.BlockSpec((WIN, 512), lambda r, b: (b, r))
    prev = pl.BlockSpec((WIN, 512), lambda r, b: (jnp.maximum(b - 1, 0), r))
    return pl.pallas_call(
        body, name=f"swa_fwd_d{dil}", grid=(dil, nb),
        out_shape=[jax.ShapeDtypeStruct(qv.shape, F32)] * 3,
        in_specs=[cur, prev, cur, prev, cur, _full(bias.shape)],
        out_specs=[cur] * 3,
        compiler_params=_cparams(2),
    )(qv, kv, kv, vv, vv, bias)


def _mix_out(branches, gb, gc, xi, x, w_sc, g_a, g_c, w_out):
    s = x.shape[0]
    tb = TM // SUBLANES

    def body(n1, m1, s1, n2, m2, s2, n3, m3, s3, gb_ref, gc_ref, xi_ref, gch_ref, xih_ref, x_ref, wsc_ref,
             ga_ref, gcv_ref, wout_ref, attn_ref, lse_ref, mixed_ref, x1_ref):
        i = pl.program_id(0)
        m_all = jnp.maximum(jnp.maximum(m1[...], m2[...]), m3[...])
        e1, e2, e3 = jnp.exp(m1[...] - m_all), jnp.exp(m2[...] - m_all), jnp.exp(m3[...] - m_all)
        den = (e1 * s1[...] + e2 * s2[...]) + e3 * s3[...]
        num = (e1 * n1[...] + e2 * n2[...]) + e3 * n3[...]
        attn = num / den
        attn_ref[...] = attn
        lse_ref[...] = m_all + jnp.log(den)
        xa, _ = _rms(attn)
        u = gc_ref[...] * xi_ref[...]
        uh = jnp.where(i > 0, gch_ref[...] * xih_ref[...], 0.0)
        conv = gb_ref[...] * _causal_conv3(u, uh, wsc_ref)
        xc, _ = _rms(conv)
        mixed = jnp.concatenate([xa * ga_ref[...], xc * gcv_ref[...]], axis=1).astype(BF16)
        mixed_ref[...] = mixed
        x1_ref[...] = x_ref[...] + _dot(mixed, wout_ref[...])

    row = lambda n: pl.BlockSpec((TM, n), lambda i: (i, 0))
    halo = pl.BlockSpec((SUBLANES, 512), lambda i: (jnp.maximum(i * tb - 1, 0), 0))
    flat = [a for br in branches for a in br]
    return pl.pallas_call(
        body, name="mix_out", grid=(s // TM,),
        out_shape=[jax.ShapeDtypeStruct((s, 512), F32)] * 2
        + [jax.ShapeDtypeStruct((s, D_MODEL), BF16), jax.ShapeDtypeStruct((s, D_MODEL), F32)],
        in_specs=[row(512)] * 12 + [halo, halo, row(D_MODEL), _full(w_sc.shape), _full(g_a.shape),
                                    _full(g_c.shape), _full(w_out.shape)],
        out_specs=[row(512), row(512), row(D_MODEL), row(D_MODEL)],
        compiler_params=_cparams(1),
    )(*flat, gb, gc, xi, gc, xi, x, w_sc, g_a, g_c, w_out)


def _mem_kv(mem, g_mem, w_xk, w_xv):
    def body(mem_ref, g_ref, wk_ref, wv_ref, mn_ref, k_ref, v_ref):
        xh, _ = _rms(mem_ref[...])
        mn = (xh * g_ref[...]).astype(BF16)
        mn_ref[...] = mn
        k_ref[...] = _dot(mn, wk_ref[...]).astype(BF16)
        v_ref[...] = _dot(mn, wv_ref[...]).astype(BF16)

    vm = pl.BlockSpec(memory_space=pltpu.VMEM)
    return pl.pallas_call(
        body, name="mem_kv",
        out_shape=[jax.ShapeDtypeStruct(mem.shape, BF16)] * 3,
        in_specs=[vm] * 4, out_specs=[vm] * 3,
        compiler_params=pltpu.CompilerParams(vmem_limit_bytes=VMEM_LIMIT),
    )(mem, g_mem, w_xk, w_xv)


def _xattn_fwd(x1, g, w_xq, k, v, w_xo):
    s = x1.shape[0]

    def body(x1_ref, g_ref, wq_ref, k_ref, v_ref, wo_ref, h2_ref, q_ref, o_ref, x2_ref):
        x1v = x1_ref[...]
        xh, _ = _rms(x1v)
        h2 = (xh * g_ref[...]).astype(BF16)
        h2_ref[...] = h2
        qb = _dot(h2, wq_ref[...]).astype(BF16)
        q_ref[...] = qb
        outs = []
        for h in range(N_MEM_HEADS):
            sl = slice(h * MEM_HEAD_DIM, (h + 1) * MEM_HEAD_DIM)
            lg = _dot_nt(qb[:, sl], k_ref[:, sl]) * (MEM_HEAD_DIM ** -0.5)
            p = jnp.exp(lg - jnp.max(lg, axis=-1, keepdims=True))
            p = p / jnp.sum(p, axis=-1, keepdims=True)
            outs.append(_dot(p.astype(BF16), v_ref[:, sl]))
        o = jnp.concatenate(outs, axis=1).astype(BF16)
        o_ref[...] = o
        x2_ref[...] = x1v + _dot(o, wo_ref[...])

    row = pl.BlockSpec((TM, D_MODEL), lambda i: (i, 0))
    return pl.pallas_call(
        body, name="xattn_fwd", grid=(s // TM,),
        out_shape=[jax.ShapeDtypeStruct((s, D_MODEL), BF16)] * 3 + [jax.ShapeDtypeStruct((s, D_MODEL), F32)],
        in_specs=[row, _full(g.shape), _full(w_xq.shape), _full(k.shape), _full(v.shape), _full(w_xo.shape)],
        out_specs=[row] * 4,
        compiler_params=_cparams(1),
    )(x1, g, w_xq, k, v, w_xo)


def _ffn_up(x2, g, w_up_g):
    s = x2.shape[0]

    def body(x_ref, g_ref, w_ref, h_ref, up_ref, h_scr):
        @pl.when(pl.program_id(1) == 0)
        def _():
            xh, _ = _rms(x_ref[...])
            h = (xh * g_ref[...]).astype(BF16)
            h_scr[...] = h
            h_ref[...] = h

        up_ref[0] = _dot(h_scr[...], w_ref[0])

    return pl.pallas_call(
        body, name="ffn_up", grid=(s // TM, N_DEV),
        out_shape=[jax.ShapeDtypeStruct((s, D_MODEL), BF16), jax.ShapeDtypeStruct((N_DEV, s, UP_CHUNK), F32)],
        in_specs=[pl.BlockSpec((TM, D_MODEL), lambda i, j: (i, 0)), _full(g.shape),
                  pl.BlockSpec((1, D_MODEL, UP_CHUNK), lambda i, j: (j, 0, 0))],
        out_specs=[pl.BlockSpec((TM, D_MODEL), lambda i, j: (i, 0)),
                   pl.BlockSpec((1, TM, UP_CHUNK), lambda i, j: (j, i, 0))],
        scratch_shapes=[pltpu.VMEM((TM, D_MODEL), BF16)],
        compiler_params=_cparams(2),
    )(x2, g, w_up_g)


def _ffn_conv(up_ref, uph_ref, wfc_ref, bfc_ref, i, j):
    u = up_ref[j]
    uh = jnp.where(i > 0, uph_ref[j], 0.0)
    u2 = _shift_down(u, uh, 2)
    u1 = _shift_down(u, uh, 1)
    w = wfc_ref[j]
    c = ((u2 * w[0:1, :] + u1 * w[1:2, :]) + u * w[2:3, :]) + bfc_ref[j]
    return c, u2, u1, u


def _ffn_tail(up, w_fc, b_fc, w_down_g, x2, g_final, target):
    s = x2.shape[0]
    tb = TM_FFN // SUBLANES
    half = N_DEV // 2

    def body(up_ref, uph_ref, wfc_ref, bfc_ref, wd_ref, x2_ref, gf_ref, t_ref, act_ref, dx3_ref, loss_ref, dgf_ref):
        i = pl.program_id(0)

        @pl.when(i == 0)
        def _():
            loss_ref[...] = jnp.zeros_like(loss_ref)
            dgf_ref[...] = jnp.zeros_like(dgf_ref)

        down = jnp.zeros((TM_FFN, D_MODEL), F32)
        for j in range(half):
            cg = _ffn_conv(up_ref, uph_ref, wfc_ref, bfc_ref, i, j)[0]
            cv = _ffn_conv(up_ref, uph_ref, wfc_ref, bfc_ref, i, j + half)[0]
            a = ((cg * _sigmoid(cg)) * cv).astype(BF16)
            act_ref[j] = a
            down = down + _dot(a, wd_ref[j])
        x3 = x2_ref[...] + down
        xh, r = _rms(x3)
        gf = gf_ref[...]
        e = xh * gf - t_ref[...]
        loss_ref[...] += 0.5 * jnp.sum(jnp.sum(e * e, axis=1, keepdims=True), axis=0, keepdims=True) / D_MODEL
        dy = e * (1.0 / D_MODEL)
        dgf_ref[0:1, :] += jnp.sum(dy * xh, axis=0, keepdims=True)
        dx3_ref[...] = _rms_bwd(xh, r, gf, dy)

    row = pl.BlockSpec((TM_FFN, D_MODEL), lambda i: (i, 0))
    cur = pl.BlockSpec((N_DEV, TM_FFN, UP_CHUNK), lambda i: (0, i, 0))
    halo = pl.BlockSpec((N_DEV, SUBLANES, UP_CHUNK), lambda i: (0, jnp.maximum(i * tb - 1, 0), 0))
    return pl.pallas_call(
        body, name="ffn_tail", grid=(s // TM_FFN,),
        out_shape=[jax.ShapeDtypeStruct((half, s, UP_CHUNK), BF16), jax.ShapeDtypeStruct((s, D_MODEL), F32),
                   jax.ShapeDtypeStruct((SUBLANES, 128), F32), jax.ShapeDtypeStruct((SUBLANES, D_MODEL), F32)],
        in_specs=[cur, halo, _full(w_fc.shape), _full(b_fc.shape), _full(w_down_g.shape), row,
                  _full(g_final.shape), row],
        out_specs=[pl.BlockSpec((half, TM_FFN, UP_CHUNK), lambda i: (0, i, 0)), row,
                   _full((SUBLANES, 128)), _full((SUBLANES, D_MODEL))],
        compiler_params=_cparams(1),
    )(up, up, w_fc, b_fc, w_down_g, x2, g_final, target)


def _ffn_down_bwd(dx3, up, w_fc, b_fc, w_down_g):
    s = dx3.shape[0]
    tb = TM_FFN // SUBLANES
    half = N_DEV // 2

    def body(dx3_ref, up_ref, uph_ref, wfc_ref, bfc_ref, wd_ref, dc_ref, dwfc_ref, dbfc_ref):
        i = pl.program_id(0)

        @pl.when(i == 0)
        def _():
            dwfc_ref[...] = jnp.zeros_like(dwfc_ref)
            dbfc_ref[...] = jnp.zeros_like(dbfc_ref)

        dxb = dx3_ref[...].astype(BF16)

        def small_grads(j, dc, u2, u1, u):
            dc_ref[j] = dc
            dbfc_ref[j:j + 1, :] += jnp.sum(dc, axis=0, keepdims=True)
            dwfc_ref[0, j:j + 1, :] += jnp.sum(dc * u2, axis=0, keepdims=True)
            dwfc_ref[1, j:j + 1, :] += jnp.sum(dc * u1, axis=0, keepdims=True)
            dwfc_ref[2, j:j + 1, :] += jnp.sum(dc * u, axis=0, keepdims=True)

        for j in range(half):
            dact = _dot_nt(dxb, wd_ref[j])
            cg, g2, g1, g0 = _ffn_conv(up_ref, uph_ref, wfc_ref, bfc_ref, i, j)
            cv, v2, v1, v0 = _ffn_conv(up_ref, uph_ref, wfc_ref, bfc_ref, i, j + half)
            sg = _sigmoid(cg)
            small_grads(j + half, dact * (cg * sg), v2, v1, v0)
            small_grads(j, (dact * cv) * (sg * (1.0 + cg * (1.0 - sg))), g2, g1, g0)

    row = pl.BlockSpec((TM_FFN, D_MODEL), lambda i: (i, 0))
    cur = pl.BlockSpec((N_DEV, TM_FFN, UP_CHUNK), lambda i: (0, i, 0))
    halo = pl.BlockSpec((N_DEV, SUBLANES, UP_CHUNK), lambda i: (0, jnp.maximum(i * tb - 1, 0), 0))
    return pl.pallas_call(
        body, name="ffn_down_bwd", grid=(s // TM_FFN,),
        out_shape=[jax.ShapeDtypeStruct((N_DEV, s, UP_CHUNK), F32), jax.ShapeDtypeStruct((3, N_DEV, UP_CHUNK), F32),
                   jax.ShapeDtypeStruct((N_DEV, UP_CHUNK), F32)],
        in_specs=[row, cur, halo, _full(w_fc.shape), _full(b_fc.shape), _full(w_down_g.shape)],
        out_specs=[cur, _full((3, N_DEV, UP_CHUNK)), _full((N_DEV, UP_CHUNK))],
        compiler_params=_cparams(1),
    )(dx3, up, up, w_fc, b_fc, w_down_g)


def _ffn_up_bwd(dc, w_fc, w_up_g, x2, g, dx3):
    s = x2.shape[0]
    tb = TM_FFN // SUBLANES
    last = s // SUBLANES - 1
    n_tiles = s // TM_FFN

    def body(dc_ref, dch_ref, wfc_ref, wup_ref, x2_ref, g_ref, dx3_ref, dup_ref, dx2_ref, dg_ref):
        i = pl.program_id(0)

        @pl.when(i == 0)
        def _():
            dg_ref[...] = jnp.zeros_like(dg_ref)

        dh = jnp.zeros((TM_FFN, D_MODEL), F32)
        for j in range(N_DEV):
            d0 = dc_ref[j]
            dn = jnp.where(i < n_tiles - 1, dch_ref[j], 0.0)
            w = wfc_ref[j]
            du = ((d0 * w[2:3, :] + _shift_up(d0, dn, 1) * w[1:2, :]) + _shift_up(d0, dn, 2) * w[0:1, :]).astype(BF16)
            dup_ref[j] = du
            dh = dh + _dot_nt(du, wup_ref[j])
        xh, r = _rms(x2_ref[...])
        dg_ref[0:1, :] += jnp.sum(dh * xh, axis=0, keepdims=True)
        dx2_ref[...] = dx3_ref[...] + _rms_bwd(xh, r, g_ref[...], dh)

    row = pl.BlockSpec((TM_FFN, D_MODEL), lambda i: (i, 0))
    cur = pl.BlockSpec((N_DEV, TM_FFN, UP_CHUNK), lambda i: (0, i, 0))
    nxt = pl.BlockSpec((N_DEV, SUBLANES, UP_CHUNK), lambda i: (0, jnp.minimum((i + 1) * tb, last), 0))
    return pl.pallas_call(
        body, name="ffn_up_bwd", grid=(n_tiles,),
        out_shape=[jax.ShapeDtypeStruct((N_DEV, s, UP_CHUNK), BF16), jax.ShapeDtypeStruct((s, D_MODEL), F32),
                   jax.ShapeDtypeStruct((SUBLANES, D_MODEL), F32)],
        in_specs=[cur, nxt, _full(w_fc.shape), _full(w_up_g.shape), row, _full(g.shape), row],
        out_specs=[cur, row, _full((SUBLANES, D_MODEL))],
        compiler_params=_cparams(1),
    )(dc, dc, w_fc, w_up_g, x2, g, dx3)


def _xattn_bwd(dx2, o, q, k, v, w_xo, w_xq, x1, g):
    s = x1.shape[0]

    def body(dx2_ref, o_ref, q_ref, k_ref, v_ref, wo_ref, wq_ref, x1_ref, g_ref, dq_ref, dx1_ref, dk_ref, dv_ref, dg_ref):
        @pl.when(pl.program_id(0) == 0)
        def _():
            dk_ref[...] = jnp.zeros_like(dk_ref)
            dv_ref[...] = jnp.zeros_like(dv_ref)
            dg_ref[...] = jnp.zeros_like(dg_ref)

        dx2v = dx2_ref[...]
        do = _dot_nt(dx2v.astype(BF16), wo_ref[...])
        dqs = []
        for h in range(N_MEM_HEADS):
            sl = slice(h * MEM_HEAD_DIM, (h + 1) * MEM_HEAD_DIM)
            qh, kh, vh = q_ref[:, sl], k_ref[:, sl], v_ref[:, sl]
            lg = _dot_nt(qh, kh) * (MEM_HEAD_DIM ** -0.5)
            p = jnp.exp(lg - jnp.max(lg, axis=-1, keepdims=True))
            p = p / jnp.sum(p, axis=-1, keepdims=True)
            doh = do[:, sl].astype(BF16)
            dp = _dot_nt(doh, vh)
            ds = (p * (dp - jnp.sum(p * dp, axis=-1, keepdims=True)) * (MEM_HEAD_DIM ** -0.5)).astype(BF16)
            dqs.append(_dot(ds, kh))
            dk_ref[:, sl] += _dot_tn(ds, qh)
            dv_ref[:, sl] += _dot_tn(p.astype(BF16), doh)
        dq = jnp.concatenate(dqs, axis=1).astype(BF16)
        dq_ref[...] = dq
        dh2 = _dot_nt(dq, wq_ref[...])
        xh, r = _rms(x1_ref[...])
        dg_ref[0:1, :] += jnp.sum(dh2 * xh, axis=0, keepdims=True)
        dx1_ref[...] = dx2v + _rms_bwd(xh, r, g_ref[...], dh2)

    row = pl.BlockSpec((TM, D_MODEL), lambda i: (i, 0))
    return pl.pallas_call(
        body, name="xattn_bwd", grid=(s // TM,),
        out_shape=[jax.ShapeDtypeStruct((s, D_MODEL), BF16), jax.ShapeDtypeStruct((s, D_MODEL), F32),
                   jax.ShapeDtypeStruct(k.shape, F32), jax.ShapeDtypeStruct(k.shape, F32),
                   jax.ShapeDtypeStruct((SUBLANES, D_MODEL), F32)],
        in_specs=[row, row, row, _full(k.shape), _full(v.shape), _full(w_xo.shape), _full(w_xq.shape), row,
                  _full(g.shape)],
        out_specs=[row, row, _full(k.shape), _full(k.shape), _full((SUBLANES, D_MODEL))],
        compiler_params=_cparams(1),
    )(dx2, o, q, k, v, w_xo, w_xq, x1, g)


def _mem_kv_bwd(dk, dv, mem_n, mem, w_xk, w_xv):
    def body(dk_ref, dv_ref, mn_ref, mem_ref, wk_ref, wv_ref, dwk_ref, dwv_ref, dg_ref):
        dkb, dvb = dk_ref[...].astype(BF16), dv_ref[...].astype(BF16)
        mn = mn_ref[...]
        dwk_ref[...] = _dot_tn(mn, dkb).astype(BF16)
        dwv_ref[...] = _dot_tn(mn, dvb).astype(BF16)
        dmn = _dot_nt(dkb, wk_ref[...]) + _dot_nt(dvb, wv_ref[...])
        xh, _ = _rms(mem_ref[...])
        dg_ref[...] = jnp.zeros_like(dg_ref)
        dg_ref[0:1, :] = jnp.sum(dmn * xh, axis=0, keepdims=True)

    vm = pl.BlockSpec(memory_space=pltpu.VMEM)
    return pl.pallas_call(
        body, name="mem_kv_bwd",
        out_shape=[jax.ShapeDtypeStruct(w_xk.shape, BF16), jax.ShapeDtypeStruct(w_xv.shape, BF16),
                   jax.ShapeDtypeStruct((SUBLANES, D_MODEL), F32)],
        in_specs=[vm] * 6, out_specs=[vm] * 3,
        compiler_params=pltpu.CompilerParams(vmem_limit_bytes=VMEM_LIMIT),
    )(dk, dv, mem_n, mem, w_xk, w_xv)


def _mix_out_bwd(dx1, w_out, attn, gb, gc, xi, w_sc, g_a, g_c):
    s = dx1.shape[0]
    tb = TM // SUBLANES

    def body(dx1_ref, wout_ref, attn_ref, gb_ref, gc_ref, xi_ref, gch_ref, xih_ref, wsc_ref, ga_ref, gcv_ref,
             dattn_ref, dd_ref, dgb_ref, dcv_ref, dga_ref, dgc_ref, dwsc_ref):
        i = pl.program_id(0)

        @pl.when(i == 0)
        def _():
            dga_ref[...] = jnp.zeros_like(dga_ref)
            dgc_ref[...] = jnp.zeros_like(dgc_ref)
            dwsc_ref[...] = jnp.zeros_like(dwsc_ref)

        dmixed = _dot_nt(dx1_ref[...].astype(BF16), wout_ref[...])
        da, dcn = dmixed[:, :ATTN_W], dmixed[:, ATTN_W:]
        attn = attn_ref[...]
        xa, ra = _rms(attn)
        dga_ref[0:1, :] += jnp.sum(da * xa, axis=0, keepdims=True)
        dattn = _rms_bwd(xa, ra, ga_ref[...], da)
        dattn_ref[...] = dattn
        prod = dattn * attn
        for h in range(N_HEADS):
            sl = slice(h * HEAD_DIM, (h + 1) * HEAD_DIM)
            dd_ref[:, sl] = jnp.broadcast_to(jnp.sum(prod[:, sl], axis=-1, keepdims=True), (TM, HEAD_DIM))
        gbv = gb_ref[...]
        u = gc_ref[...] * xi_ref[...]
        uh = jnp.where(i > 0, gch_ref[...] * xih_ref[...], 0.0)
        u2, u1 = _shift_down(u, uh, 2), _shift_down(u, uh, 1)
        cv = (u2 * wsc_ref[0:1, :] + u1 * wsc_ref[1:2, :]) + u * wsc_ref[2:3, :]
        xc, rc = _rms(gbv * cv)
        dgc_ref[0:1, :] += jnp.sum(dcn * xc, axis=0, keepdims=True)
        dconv = _rms_bwd(xc, rc, gcv_ref[...], dcn)
        dgb_ref[...] = dconv * cv
        dcv = dconv * gbv
        dcv_ref[...] = dcv
        dwsc_ref[0:1, :] += jnp.sum(dcv * u2, axis=0, keepdims=True)
        dwsc_ref[1:2, :] += jnp.sum(dcv * u1, axis=0, keepdims=True)
        dwsc_ref[2:3, :] += jnp.sum(dcv * u, axis=0, keepdims=True)

    row = lambda n: pl.BlockSpec((TM, n), lambda i: (i, 0))
    halo = pl.BlockSpec((SUBLANES, 512), lambda i: (jnp.maximum(i * tb - 1, 0), 0))
    acc = _full((SUBLANES, 512))
    return pl.pallas_call(
        body, name="mix_out_bwd", grid=(s // TM,),
        out_shape=[jax.ShapeDtypeStruct((s, 512), F32)] * 4 + [jax.ShapeDtypeStruct((SUBLANES, 512), F32)] * 3,
        in_specs=[row(D_MODEL), _full(w_out.shape), row(512), row(512), row(512), row(512), halo, halo,
                  _full(w_sc.shape), _full(g_a.shape), _full(g_c.shape)],
        out_specs=[row(512)] * 4 + [acc] * 3,
        compiler_params=_cparams(1),
    )(dx1, w_out, attn, gb, gc, xi, gc, xi, w_sc, g_a, g_c)


def _swa_bwd(qv, kv, vv, dov, lsev, ddv, bias, dil):
    rows = qv.shape[0]
    nb = rows // WIN

    def body(q_ref, qn_ref, kp_ref, kc_ref, vp_ref, vc_ref, do_ref, don_ref, lse_ref, lsen_ref, dd_ref, ddn_ref,
             b_ref, dq_ref, dk_ref, dv_ref, db_ref):
        r, b = pl.program_id(0), pl.program_id(1)

        @pl.when((r == 0) & (b == 0))
        def _():
            db_ref[...] = jnp.zeros_like(db_ref)

        valid = _band_mask(b)
        qi = lax.broadcasted_iota(jnp.int32, (WIN, WIN), 0)
        kj = lax.broadcasted_iota(jnp.int32, (WIN, WIN), 1)
        valid_n = kj >= qi + jnp.where(b + 1 < nb, 0, WIN)
        for h in range(N_HEADS):
            sl = slice(h * HEAD_DIM, (h + 1) * HEAD_DIM)
            col = slice(h * HEAD_DIM, h * HEAD_DIM + 1)
            qh, kc, vc = q_ref[:, sl], kc_ref[:, sl], vc_ref[:, sl]
            kh = jnp.concatenate([kp_ref[:, sl], kc], axis=0)
            vh = jnp.concatenate([vp_ref[:, sl], vc], axis=0)
            doh = do_ref[:, sl].astype(BF16)
            lg = jnp.where(valid, _dot_nt(qh, kh) + b_ref[h], -jnp.inf)
            p = jnp.exp(lg - lse_ref[:, col])
            ds = p * (_dot_nt(doh, vh) - dd_ref[:, col])
            db_ref[h] += ds
            dsb = ds.astype(BF16)
            dq_ref[:, sl] = _dot(dsb, kh)
            dk = _dot_tn(dsb[:, WIN:], qh)
            dv = _dot_tn(p[:, WIN:].astype(BF16), doh)
            qn = qn_ref[:, sl]
            don = don_ref[:, sl].astype(BF16)
            lgn = jnp.where(valid_n, _dot_nt(qn, kc) + b_ref[h][:, :WIN], -jnp.inf)
            pn = jnp.exp(lgn - lsen_ref[:, col])
            dsn = pn * (_dot_nt(don, vc) - ddn_ref[:, col])
            dk_ref[:, sl] = dk + _dot_tn(dsn.astype(BF16), qn)
            dv_ref[:, sl] = dv + _dot_tn(pn.astype(BF16), don)

    cur = pl.BlockSpec((WIN, 512), lambda r, b: (b, r))
    prev = pl.BlockSpec((WIN, 512), lambda r, b: (jnp.maximum(b - 1, 0), r))
    nxt = pl.BlockSpec((WIN, 512), lambda r, b: (jnp.minimum(b + 1, nb - 1), r))
    return pl.pallas_call(
        body, name=f"swa_bwd_d{dil}", grid=(dil, nb),
        out_shape=[jax.ShapeDtypeStruct(qv.shape, F32)] * 3 + [jax.ShapeDtypeStruct(bias.shape, F32)],
        in_specs=[cur, nxt, prev, cur, prev, cur, cur, nxt, cur, nxt, cur, nxt, _full(bias.shape)],
        out_specs=[cur] * 3 + [_full(bias.shape)],
        compiler_params=_cparams(2),
    )(qv, qv, kv, kv, vv, vv, dov, dov, lsev, lsev, ddv, ddv, bias)


def _in_proj_bwd(dqs, dks, dvs, dgb, dcv, gc, xi, w_sc, w_in_g, x, g_mix, dx1):
    s = x.shape[0]
    tb = TM // SUBLANES
    last = s // SUBLANES - 1
    n_tiles = s // TM

    def body(dq1, dq2, dq3, dk1, dk2, dk3, dv1, dv2, dv3, dgb_ref, dcv_ref, dcvn_ref, gc_ref, xi_ref, wsc_ref,
             win_ref, x_ref, g_ref, dx1_ref, dproj_ref, gx_ref, dg_ref):
        i = pl.program_id(0)

        @pl.when(i == 0)
        def _():
            dg_ref[...] = jnp.zeros_like(dg_ref)

        d0 = dcv_ref[...]
        dn = jnp.where(i < n_tiles - 1, dcvn_ref[...], 0.0)
        du = (d0 * wsc_ref[2:3, :] + _shift_up(d0, dn, 1) * wsc_ref[1:2, :]) + _shift_up(d0, dn, 2) * wsc_ref[0:1, :]
        dq = ((dq1[...] + dq2[...]) + dq3[...]) * (HEAD_DIM ** -0.5)
        dk = (dk1[...] + dk2[...]) + dk3[...]
        dv = (dv1[...] + dv2[...]) + dv3[...]
        dproj = jnp.concatenate([dq, dk, dv, dgb_ref[...], du * xi_ref[...], du * gc_ref[...]], axis=1).astype(BF16)
        dproj_ref[...] = dproj
        dh = jnp.zeros((TM, D_MODEL), F32)
        for j in range(N_DEV):
            dh = dh + _dot_nt(dproj[:, j * IN_CHUNK:(j + 1) * IN_CHUNK], win_ref[j])
        xh, r = _rms(x_ref[...])
        dg_ref[0:1, :] += jnp.sum(dh * xh, axis=0, keepdims=True)
        gx_ref[...] = dx1_ref[...] + _rms_bwd(xh, r, g_ref[...], dh)

    row = lambda n: pl.BlockSpec((TM, n), lambda i: (i, 0))
    nxt = pl.BlockSpec((SUBLANES, 512), lambda i: (jnp.minimum((i + 1) * tb, last), 0))
    return pl.pallas_call(
        body, name="in_proj_bwd", grid=(n_tiles,),
        out_shape=[jax.ShapeDtypeStruct((s, IN_COLS), BF16), jax.ShapeDtypeStruct((s, D_MODEL), F32),
                   jax.ShapeDtypeStruct((SUBLANES, D_MODEL), F32)],
        in_specs=[row(512)] * 11 + [nxt, row(512), row(512), _full(w_sc.shape), _full(w_in_g.shape),
                                    row(D_MODEL), _full(g_mix.shape), row(D_MODEL)],
        out_specs=[row(IN_COLS), row(D_MODEL), _full((SUBLANES, D_MODEL))],
        compiler_params=_cparams(1),
    )(*dqs, *dks, *dvs, dgb, dcv, dcv, gc, xi, w_sc, w_in_g, x, g_mix, dx1)


def _dw(a, b, name, a_chunked=False, b_chunked=False, n_chunks=1, chunk_cols=None):
    ts = TM
    if a_chunked:
        nj, s, kk = a.shape
        nn = b.shape[1]
        a_spec = pl.BlockSpec((1, ts, kk), lambda j, t: (j, t, 0))
        b_spec = pl.BlockSpec((ts, nn), lambda j, t: (t, 0))
    elif b_chunked:
        nj, s, nn = b.shape
        kk = a.shape[1]
        a_spec = pl.BlockSpec((ts, kk), lambda j, t: (t, 0))
        b_spec = pl.BlockSpec((1, ts, nn), lambda j, t: (j, t, 0))
    else:
        s, kk = a.shape
        nj, nn = (n_chunks, chunk_cols) if chunk_cols else (1, b.shape[1])
        a_spec = pl.BlockSpec((ts, kk), lambda j, t: (t, 0))
        b_spec = pl.BlockSpec((ts, nn), lambda j, t: (t, j))
    n_steps = s // ts

    def body(a_ref, b_ref, o_ref, acc):
        t = pl.program_id(1)

        @pl.when(t == 0)
        def _():
            acc[...] = jnp.zeros_like(acc)

        av = (a_ref[0] if a_chunked else a_ref[...]).astype(BF16)
        bv = (b_ref[0] if b_chunked else b_ref[...]).astype(BF16)
        acc[...] += _dot_tn(av, bv)

        @pl.when(t == n_steps - 1)
        def _():
            o_ref[0] = acc[...].astype(BF16)

    return pl.pallas_call(
        body, name=name, grid=(nj, n_steps),
        out_shape=jax.ShapeDtypeStruct((nj, kk, nn), BF16),
        in_specs=[a_spec, b_spec],
        out_specs=pl.BlockSpec((1, kk, nn), lambda j, t: (j, 0, 0)),
        scratch_shapes=[pltpu.VMEM((kk, nn), F32)],
        compiler_params=_cparams(2),
    )(a, b)


def _adamw_math(w, g, m, v):
    m2 = ADAM_B1 * m + (1.0 - ADAM_B1) * g
    v2 = ADAM_B2 * v + (1.0 - ADAM_B2) * (g * g)
    m_hat = m2 / (1.0 - ADAM_B1 ** ADAM_STEP)
    v_hat = v2 / (1.0 - ADAM_B2 ** ADAM_STEP)
    delta = -ADAM_LR * (m_hat / (jnp.sqrt(v_hat) + ADAM_EPS) + ADAM_WD * w)
    return delta, m2, v2


def _adamw_big(name, w, parts, m, v):
    rr, cc = w.shape
    tr = rr // 4 if rr >= 512 else rr

    def body(w_ref, p_ref, m_ref, v_ref, g_ref, d_ref, nm_ref, nv_ref):
        g = p_ref[0].astype(F32)
        for i in range(1, N_DEV):
            g = g + p_ref[i].astype(F32)
        g_ref[...] = g
        d_ref[...], nm_ref[...], nv_ref[...] = _adamw_math(w_ref[...], g, m_ref[...], v_ref[...])

    row = pl.BlockSpec((tr, cc), lambda i: (i, 0))
    return pl.pallas_call(
        body, name=name, grid=(rr // tr,),
        out_shape=[jax.ShapeDtypeStruct((rr, cc), F32)] * 4,
        in_specs=[row, pl.BlockSpec((N_DEV, tr, cc), lambda i: (0, i, 0)), row, row],
        out_specs=[row] * 4,
        compiler_params=_cparams(1),
    )(w, parts, m, v)


def _small_slices():
    return [
        (slice(ROW_RELB, ROW_RELB + 8), slice(0, N_BUCKETS)),
        (slice(ROW_GMIX, ROW_GMIX + 1), slice(0, D_MODEL)),
        (slice(ROW_GAC, ROW_GAC + 1), slice(0, ATTN_W)),
        (slice(ROW_GAC, ROW_GAC + 1), slice(ATTN_W, D_MODEL)),
        (slice(ROW_GXATTN, ROW_GXATTN + 1), slice(0, D_MODEL)),
        (slice(ROW_GMEM, ROW_GMEM + 1), slice(0, D_MODEL)),
        (slice(ROW_GFFN, ROW_GFFN + 1), slice(0, D_MODEL)),
        (slice(ROW_BFC, ROW_BFC + 8), slice(0, UP_CHUNK)),
        (slice(ROW_GFINAL, ROW_GFINAL + 1), slice(0, D_MODEL)),
    ]


def _adamw_small(parts, wmv):
    slices = _small_slices()
    n = len(slices)

    def body(*refs):
        p_ref = refs[0]
        ins = refs[1:1 + 3 * n]
        g_ref = refs[1 + 3 * n]
        outs = refs[2 + 3 * n:]
        g = p_ref[0]
        for i in range(1, N_DEV):
            g = g + p_ref[i]
        g_ref[...] = g
        for a, (rs, ls) in enumerate(slices):
            ga = g[rs, ls]
            outs[4 * a][...] = ga
            outs[4 * a + 1][...], outs[4 * a + 2][...], outs[4 * a + 3][...] = _adamw_math(
                ins[3 * a][...], ga, ins[3 * a + 1][...], ins[3 * a + 2][...])

    vm = pl.BlockSpec(memory_space=pltpu.VMEM)
    flat = [t for trip in wmv for t in trip]
    out_shape = [jax.ShapeDtypeStruct((SMALL_ROWS, D_MODEL), F32)]
    for w, _, _ in wmv:
        out_shape += [jax.ShapeDtypeStruct(w.shape, F32)] * 4
    res = pl.pallas_call(
        body, name="adamw_small", out_shape=out_shape,
        in_specs=[vm] * (1 + 3 * n), out_specs=[vm] * len(out_shape),
    )(parts, *flat)
    return res[0], [res[1 + 4 * a:5 + 4 * a] for a in range(n)]


def _adamw_shards(items):
    n = len(items)

    def body(*refs):
        for a in range(n):
            w_ref, g_ref, m_ref, v_ref = refs[4 * a:4 * a + 4]
            d_ref, nm_ref, nv_ref = refs[4 * n + 3 * a:4 * n + 3 * a + 3]
            d_ref[...], nm_ref[...], nv_ref[...] = _adamw_math(w_ref[...], g_ref[...], m_ref[...], v_ref[...])

    vm = pl.BlockSpec(memory_space=pltpu.VMEM)
    out_shape = []
    for w, _, _, _ in items:
        out_shape += [jax.ShapeDtypeStruct(w.shape, F32)] * 3
    res = pl.pallas_call(
        body, name="adamw_shards", out_shape=out_shape, in_specs=[vm] * (4 * n), out_specs=[vm] * (3 * n),
    )(*[t for it in items for t in it])
    return [res[3 * a:3 * a + 3] for a in range(n)]


def _mesh_pos():
    return lax.axis_index("x"), lax.axis_index("y"), lax.axis_index("c")


def _dev_index(p):
    return 4 * p[0] + 2 * p[1] + p[2]


def _all_gather(shards):
    n = len(shards)

    def body(*refs):
        ins, outs = refs[:n], refs[n:2 * n]
        send_sems, recv_sems, loc_sems = refs[2 * n:]
        x, y, c = _mesh_pos()
        me, sib = (x, y, c), (x, y, 1 - c)
        chips = [(1 - x, y), (x, 1 - y), (1 - x, 1 - y)]

        def cp(a, k, block, to, src=None):
            dst = outs[a].at[_dev_index(block)]
            return pltpu.make_async_remote_copy(
                src_ref=dst if src is None else src, dst_ref=dst, send_sem=send_sems.at[a, k],
                recv_sem=recv_sems.at[a, k], device_id=to, device_id_type=MESH)

        mine = [pltpu.make_async_copy(ins[a], outs[a].at[_dev_index(me)], loc_sems.at[a]) for a in range(n)]
        for m_ in mine:
            m_.start()
        first = []
        for a in range(n):
            first.append(cp(a, 0, me, sib, src=ins[a]))
            first += [cp(a, 1 + j, me, (*chip, c), src=ins[a]) for j, chip in enumerate(chips)]
        for f in first:
            f.start()
        passed = []
        for a in range(n):
            for j, chip in enumerate(chips):
                cp(a, 1 + j, (*chip, c), me).wait_recv()
                fwd = cp(a, 4 + j, (*chip, c), sib)
                fwd.start()
                passed.append(fwd)
        for a in range(n):
            cp(a, 0, sib, me).wait_recv()
            for j, chip in enumerate(chips):
                cp(a, 4 + j, (*chip, 1 - c), me).wait_recv()
        for f in first + passed:
            f.wait_send()
        for m_ in mine:
            m_.wait()

    hbm = pl.BlockSpec(memory_space=pltpu.HBM)
    return pl.pallas_call(
        body, name="all_gather_weights",
        out_shape=[jax.ShapeDtypeStruct((N_DEV,) + a.shape, a.dtype) for a in shards],
        in_specs=[hbm] * n, out_specs=[hbm] * n,
        scratch_shapes=[pltpu.SemaphoreType.DMA((n, 7)), pltpu.SemaphoreType.DMA((n, 7)),
                        pltpu.SemaphoreType.DMA((n,))],
    )(*shards)


def _scatter_grads(grads, small):
    n = len(grads)

    def body(*refs):
        ins, small_in = refs[:n], refs[n]
        outs, small_out = refs[n + 1:2 * n + 1], refs[2 * n + 1]
        send_sems, recv_sems, loc_sems = refs[2 * n + 2:]
        x, y, c = _mesh_pos()
        me = (x, y, c)
        peers = [((1 - x) if k & 4 else x, (1 - y) if k & 2 else y, (1 - c) if k & 1 else c) for k in range(1, 8)]

        def cp(a, k, peer, sender):
            if a < n:
                src, dst = ins[a].at[_dev_index(peer)], outs[a].at[_dev_index(sender)]
            else:
                src, dst = small_in, small_out.at[_dev_index(sender)]
            return pltpu.make_async_remote_copy(
                src_ref=src, dst_ref=dst, send_sem=send_sems.at[a, k], recv_sem=recv_sems.at[a, k],
                device_id=peer, device_id_type=MESH)

        mine = [pltpu.make_async_copy(ins[a].at[_dev_index(me)], outs[a].at[_dev_index(me)], loc_sems.at[a])
                for a in range(n)]
        mine.append(pltpu.make_async_copy(small_in, small_out.at[_dev_index(me)], loc_sems.at[n]))
        for m_ in mine:
            m_.start()
        sends = [cp(a, k, peer, me) for a in range(n + 1) for k, peer in enumerate(peers)]
        for s_ in sends:
            s_.start()
        for a in range(n + 1):
            for k, peer in enumerate(peers):
                cp(a, k, me, peer).wait_recv()
        for s_ in sends:
            s_.wait_send()
        for m_ in mine:
            m_.wait()

    hbm = pl.BlockSpec(memory_space=pltpu.HBM)
    res = pl.pallas_call(
        body, name="scatter_grads",
        out_shape=[jax.ShapeDtypeStruct(g.shape, g.dtype) for g in grads]
        + [jax.ShapeDtypeStruct((N_DEV,) + small.shape, small.dtype)],
        in_specs=[hbm] * (n + 1), out_specs=[hbm] * (n + 1),
        scratch_shapes=[pltpu.SemaphoreType.DMA((n + 1, 7)), pltpu.SemaphoreType.DMA((n + 1, 7)),
                        pltpu.SemaphoreType.DMA((n + 1,))],
    )(*grads, small)
    return res[:n], res[n]


def _views(a, dil):
    s = a.shape[0]
    return a.reshape(s // dil, dil * a.shape[1])


def _local_step(x, mem, target, rel_bias, g_mix, w_in_g, w_sc, g_a, g_c, w_out, g_xattn, g_mem, w_xq, w_xk, w_xv,
                w_xo, g_ffn, w_up_g, w_fc, b_fc, w_down_g, g_final):
    s = x.shape[0]
    buckets = _bucket_tables()
    bias = _bias_fwd(rel_bias, buckets)

    h1, q, k, v, gb, gc, xi = _rms_proj(x, g_mix, w_in_g)
    branches = []
    for p, dil in enumerate(DILATIONS):
        outs = _swa_fwd(_views(q, dil), _views(k, dil), _views(v, dil), bias[p], dil)
        branches.append([o.reshape(s, ATTN_W) for o in outs])
    attn, lse, mixed, x1 = _mix_out(branches, gb, gc, xi, x, w_sc, g_a, g_c, w_out)
    mem_n, mk, mv = _mem_kv(mem, g_mem, w_xk, w_xv)
    h2, xq, xo, x2 = _xattn_fwd(x1, g_xattn, w_xq, mk, mv, w_xo)
    h3, up = _ffn_up(x2, g_ffn, w_up_g)
    act, dx3, loss_acc, dg_final = _ffn_tail(up, w_fc, b_fc, w_down_g, x2, g_final, target)

    dc, dw_fc, db_fc = _ffn_down_bwd(dx3, up, w_fc, b_fc, w_down_g)
    gw_down = _dw(act, dx3, "dw_down", a_chunked=True)
    dup, dx2, dg_ffn = _ffn_up_bwd(dc, w_fc, w_up_g, x2, g_ffn, dx3)
    gw_up = _dw(h3, dup, "dw_up", b_chunked=True)
    dxq, dx1, dmk, dmv, dg_xattn = _xattn_bwd(dx2, xo, xq, mk, mv, w_xo, w_xq, x1, g_xattn)
    gw_xo = _dw(xo, dx2, "dw_xo")[0]
    gw_xq = _dw(h2, dxq, "dw_xq")[0]
    gw_xk, gw_xv, dg_mem = _mem_kv_bwd(dmk, dmv, mem_n, mem, w_xk, w_xv)
    dattn, dd, dgb, dcv, dg_a, dg_c, dw_sc = _mix_out_bwd(dx1, w_out, attn, gb, gc, xi, w_sc, g_a, g_c)
    gw_out = _dw(mixed, dx1, "dw_out")[0]
    dqs, dks, dvs, dbias = [], [], [], []
    for p, dil in enumerate(DILATIONS):
        dq_p, dk_p, dv_p, db_p = _swa_bwd(_views(q, dil), _views(k, dil), _views(v, dil), _views(dattn, dil),
                                          _views(lse, dil), _views(dd, dil), bias[p], dil)
        dqs.append(dq_p.reshape(s, ATTN_W))
        dks.append(dk_p.reshape(s, ATTN_W))
        dvs.append(dv_p.reshape(s, ATTN_W))
        dbias.append(db_p)
    d_relb = _bias_bwd(jnp.stack(dbias), buckets)
    dproj, grad_x, dg_mix = _in_proj_bwd(dqs, dks, dvs, dgb, dcv, gc, xi, w_sc, w_in_g, x, g_mix, dx1)
    gw_in = _dw(h1, dproj, "dw_in", n_chunks=N_DEV, chunk_cols=IN_CHUNK)

    big = dict(w_in=gw_in, w_out=gw_out, w_xq=gw_xq, w_xk=gw_xk, w_xv=gw_xv, w_xo=gw_xo, w_up=gw_up, w_down=gw_down)
    pad = lambda a: jnp.pad(a, ((0, 0), (0, D_MODEL - a.shape[1])))
    small = jnp.concatenate([
        d_relb, dg_mix, dg_xattn, dg_mem, dg_ffn, dg_final, jnp.concatenate([dg_a, dg_c], axis=1),
        pad(dw_sc), pad(db_fc), pad(dw_fc.reshape(3 * N_DEV, UP_CHUNK))], axis=0)
    return loss_acc[0, 0], grad_x, big, small


def kernel(x, mem, rel_bias, g_mix, w_in, w_short_conv, g_attn_out, g_conv_out, w_out, g_xattn, g_mem, w_xq, w_xk, w_xv, w_xo, g_ffn, w_up, w_ffn_conv, b_ffn_conv, w_down, g_final, loss_target, m_rel_bias, m_g_mix, m_w_in, m_w_short_conv, m_g_attn_out, m_g_conv_out, m_w_out, m_g_xattn, m_g_mem, m_w_xq, m_w_xk, m_w_xv, m_w_xo, m_g_ffn, m_w_up, m_w_ffn_conv, m_b_ffn_conv, m_w_down, m_g_final, v_rel_bias, v_g_mix, v_w_in, v_w_short_conv, v_g_attn_out, v_g_conv_out, v_w_out, v_g_xattn, v_g_mem, v_w_xq, v_w_xk, v_w_xv, v_w_xo, v_g_ffn, v_w_up, v_w_ffn_conv, v_b_ffn_conv, v_w_down, v_g_final):
    me = _dev_index(_mesh_pos())

    big_names = ["w_in", "w_out", "w_xq", "w_xk", "w_xv", "w_xo", "w_up", "w_down"]
    big_w = dict(w_in=w_in[0], w_out=w_out[0], w_xq=w_xq[0], w_xk=w_xk[0], w_xv=w_xv[0], w_xo=w_xo[0],
                 w_up=w_up[0], w_down=w_down[0])
    big_m = dict(w_in=m_w_in[0], w_out=m_w_out[0], w_xq=m_w_xq[0], w_xk=m_w_xk[0], w_xv=m_w_xv[0], w_xo=m_w_xo[0],
                 w_up=m_w_up[0], w_down=m_w_down[0])
    big_v = dict(w_in=v_w_in[0], w_out=v_w_out[0], w_xq=v_w_xq[0], w_xk=v_w_xk[0], w_xv=v_w_xv[0], w_xo=v_w_xo[0],
                 w_up=v_w_up[0], w_down=v_w_down[0])
    gathered = _all_gather([big_w[n].astype(BF16) for n in big_names] + [w_short_conv[0], w_ffn_conv[0]])
    gw = dict(zip(big_names, gathered[:8]))
    w_sc_full = gathered[8].transpose(1, 0, 2).reshape(3, CONV_W)
    w_fc_full = gathered[9]
    sq = lambda a: a.reshape(D_MODEL, D_MODEL)

    loss_part, grad_x, big_g, small_g = _local_step(
        x[0], mem[0], loss_target[0], rel_bias, g_mix, gw["w_in"], w_sc_full, g_attn_out, g_conv_out,
        sq(gw["w_out"]), g_xattn, g_mem, sq(gw["w_xq"]), sq(gw["w_xk"]), sq(gw["w_xv"]), sq(gw["w_xo"]), g_ffn,
        gw["w_up"], w_fc_full, b_ffn_conv.reshape(N_DEV, 1, UP_CHUNK),
        gw["w_down"].reshape(N_DEV // 2, UP_CHUNK, D_MODEL), g_final.reshape(1, D_MODEL))
    loss = lax.psum(loss_part, ("x", "y", "c"))

    shard_shape = {n: big_w[n].shape for n in big_names}
    send = [big_g[n].reshape((N_DEV,) + shard_shape[n]) for n in big_names]
    recv, small_parts = _scatter_grads(send, small_g)

    big_out = {}
    for n, parts in zip(big_names, recv):
        res = _adamw_big("adamw_" + n, big_w[n], parts, big_m[n], big_v[n])
        big_out[n] = [r[None] for r in res]

    as_rows = lambda a: a.reshape(N_DEV, UP_CHUNK)
    row1 = lambda a: a.reshape(1, D_MODEL)
    small_names = ["rel_bias", "g_mix", "g_attn_out", "g_conv_out", "g_xattn", "g_mem", "g_ffn", "b_ffn_conv", "g_final"]
    wmv = [
        (rel_bias, m_rel_bias, v_rel_bias), (g_mix, m_g_mix, v_g_mix), (g_attn_out, m_g_attn_out, v_g_attn_out),
        (g_conv_out, m_g_conv_out, v_g_conv_out), (g_xattn, m_g_xattn, v_g_xattn), (g_mem, m_g_mem, v_g_mem),
        (g_ffn, m_g_ffn, v_g_ffn), (as_rows(b_ffn_conv), as_rows(m_b_ffn_conv), as_rows(v_b_ffn_conv)),
        (row1(g_final), row1(m_g_final), row1(v_g_final))]
    g_packed, small_res = _adamw_small(small_parts, wmv)
    small_out = dict(zip(small_names, small_res))
    small_out["b_ffn_conv"] = [a.reshape(1, 2 * D_FF) for a in small_out["b_ffn_conv"]]
    small_out["g_final"] = [a.reshape(D_MODEL) for a in small_out["g_final"]]

    g_wsc = lax.dynamic_slice(g_packed[ROW_WSC:ROW_WSC + 3, 0:CONV_W], (0, me * HEAD_DIM), (3, HEAD_DIM))
    g_wfc = lax.dynamic_slice(g_packed[ROW_WFC:ROW_WFC + 3 * N_DEV, 0:UP_CHUNK].reshape(3, N_DEV, UP_CHUNK),
                              (0, me, 0), (3, 1, UP_CHUNK)).reshape(3, UP_CHUNK)
    shard_res = _adamw_shards([(w_short_conv[0], g_wsc, m_w_short_conv[0], v_w_short_conv[0]),
                               (w_ffn_conv[0], g_wfc, m_w_ffn_conv[0], v_w_ffn_conv[0])])
    small_out["w_short_conv"] = [g_wsc[None]] + [a[None] for a in shard_res[0]]
    small_out["w_ffn_conv"] = [g_wfc[None]] + [a[None] for a in shard_res[1]]

    order = ["rel_bias", "g_mix", "w_in", "w_short_conv", "g_attn_out", "g_conv_out", "w_out", "g_xattn", "g_mem",
             "w_xq", "w_xk", "w_xv", "w_xo", "g_ffn", "w_up", "w_ffn_conv", "b_ffn_conv", "w_down", "g_final"]
    allp = {**big_out, **small_out}
    outs = [loss, grad_x[None]]
    for kind in range(4):
        outs += [allp[n][kind] for n in order]
    return tuple(outs)
```

```python
import functools
import math

import numpy as np
import jax
import jax.numpy as jnp
from jax import lax
from jax.experimental import pallas as pl
from jax.experimental.pallas import tpu as pltpu

F32 = jnp.float32
BF16 = jnp.bfloat16
MESH = pl.DeviceIdType.MESH

N_DEV = 8
D_MODEL = 1024
ATTN_W = 512
CONV_W = 512
N_HEADS = 8
HEAD_DIM = 64
WIN = 128
DILATIONS = (1, 4, 16)
N_BUCKETS = 32
BUCKET_MAX_EXACT = 16
BUCKET_MAX_DISTANCE = 2048
N_MEM_HEADS = 4
MEM_HEAD_DIM = 256
D_FF = 2816
IN_COLS = 3072
IN_CHUNK = IN_COLS // N_DEV
UP_CHUNK = 2 * D_FF // N_DEV
EPS = 1e-6

ADAM_LR = 0.001
ADAM_B1 = 0.9
ADAM_B2 = 0.999
ADAM_EPS = 1e-08
ADAM_WD = 0.01
ADAM_STEP = 10

SUBLANES = 8
TM = 512
TM_FFN = 256
VMEM_LIMIT = 56 * 1024 * 1024

ROW_RELB, ROW_GMIX, ROW_GXATTN, ROW_GMEM, ROW_GFFN, ROW_GFINAL, ROW_GAC = 0, 8, 16, 24, 32, 40, 48
ROW_WSC, ROW_BFC, ROW_WFC, SMALL_ROWS = 56, 64, 72, 96


def _cparams(n_grid):
    return pltpu.CompilerParams(dimension_semantics=("arbitrary",) * n_grid, vmem_limit_bytes=VMEM_LIMIT)


def _full(shape):
    nd = len(shape)
    return pl.BlockSpec(tuple(shape), lambda *_: (0,) * nd)


ANY_SPEC = pl.BlockSpec(memory_space=pl.ANY)
HBM_SPEC = pl.BlockSpec(memory_space=pltpu.HBM)
SEM_SPEC = pl.BlockSpec(memory_space=pltpu.SEMAPHORE)
VMEM_SPEC = pl.BlockSpec(memory_space=pltpu.VMEM)
SMEM_SPEC = pl.BlockSpec(memory_space=pltpu.SMEM)
DATAFLOW = pltpu.SideEffectType.DATAFLOW_SIDE_EFFECTING


def _rms(x):
    r = lax.rsqrt(jnp.mean(x * x, axis=-1, keepdims=True) + EPS)
    return x * r, r


def _rms_bwd(xh, r, g, dy):
    dxh = dy * g
    return r * (dxh - xh * jnp.mean(dxh * xh, axis=-1, keepdims=True))


def _shift_down(u, halo, k):
    ru = pltpu.roll(u, k, 0)
    rh = pltpu.roll(halo, k, 0)
    row = lax.broadcasted_iota(jnp.int32, rh.shape, 0)
    head = jnp.where(row < k, rh, ru[0:SUBLANES])
    return jnp.concatenate([head, ru[SUBLANES:]], axis=0)


def _shift_up(u, halo, k):
    tm = u.shape[0]
    ru = pltpu.roll(u, tm - k, 0)
    rh = pltpu.roll(halo, SUBLANES - k, 0)
    row = lax.broadcasted_iota(jnp.int32, rh.shape, 0)
    tail = jnp.where(row >= SUBLANES - k, rh, ru[tm - SUBLANES:])
    return jnp.concatenate([ru[:tm - SUBLANES], tail], axis=0)


def _causal_conv3(u, halo, w_ref):
    return (_shift_down(u, halo, 2) * w_ref[0:1, :] + _shift_down(u, halo, 1) * w_ref[1:2, :]) + u * w_ref[2:3, :]


def _dot(a, b):
    return jnp.dot(a, b, preferred_element_type=F32)


def _dot_nt(a, b):
    return lax.dot_general(a, b, (((1,), (1,)), ((), ())), preferred_element_type=F32)


def _dot_tn(a, b):
    return lax.dot_general(a, b, (((0,), (0,)), ((), ())), preferred_element_type=F32)


def _sigmoid(x):
    return 1.0 / (1.0 + jnp.exp(-x))


def _bucket_tables():
    qi = np.arange(WIN)[:, None]
    kj = np.arange(2 * WIN)[None, :]
    steps = np.clip(qi + WIN - kj, 0, WIN)
    out = []
    for d in DILATIONS:
        dist = steps * d
        dd = np.maximum(dist, 1).astype(np.float32)
        large = BUCKET_MAX_EXACT + (
            np.log(dd / np.float32(BUCKET_MAX_EXACT)) / np.float32(math.log(BUCKET_MAX_DISTANCE / BUCKET_MAX_EXACT))
            * np.float32(N_BUCKETS - BUCKET_MAX_EXACT)).astype(np.int32)
        large = np.minimum(large, N_BUCKETS - 1)
        out.append(np.where(dist < BUCKET_MAX_EXACT, dist, large).astype(np.int32))
    return jnp.asarray(np.stack(out))


def _bias_fwd(rel_bias, buckets):
    def body(rb_ref, bk_ref, o_ref):
        for p in range(3):
            bk = bk_ref[p]
            for h in range(N_HEADS):
                acc = jnp.zeros((WIN, 2 * WIN), F32)
                for b in range(N_BUCKETS):
                    acc = jnp.where(bk == b, rb_ref[h, b], acc)
                o_ref[p, h] = acc

    return pl.pallas_call(
        body, name="bias_fwd",
        out_shape=jax.ShapeDtypeStruct((3, N_HEADS, WIN, 2 * WIN), F32),
        in_specs=[pl.BlockSpec(memory_space=pltpu.SMEM), pl.BlockSpec(memory_space=pltpu.VMEM)],
        out_specs=pl.BlockSpec(memory_space=pltpu.VMEM),
    )(rel_bias, buckets)


def _bias_bwd(dbias, buckets):
    def body(db_ref, bk_ref, o_ref):
        lane = lax.broadcasted_iota(jnp.int32, (1, D_MODEL), 1)
        rows = []
        for h in range(N_HEADS):
            row = jnp.zeros((1, D_MODEL), F32)
            for b in range(N_BUCKETS):
                tot = jnp.zeros((1, 1), F32)
                for p in range(3):
                    sel = jnp.where(bk_ref[p] == b, db_ref[p, h], 0.0)
                    tot = tot + jnp.sum(jnp.sum(sel, axis=0, keepdims=True), axis=1, keepdims=True)
                row = jnp.where(lane == b, tot, row)
            rows.append(row)
        o_ref[...] = jnp.concatenate(rows, axis=0)

    return pl.pallas_call(
        body, name="bias_bwd",
        out_shape=jax.ShapeDtypeStruct((N_HEADS, D_MODEL), F32),
        in_specs=[pl.BlockSpec(memory_space=pltpu.VMEM), pl.BlockSpec(memory_space=pltpu.VMEM)],
        out_specs=pl.BlockSpec(memory_space=pltpu.VMEM),
    )(dbias, buckets)


def _rms_proj(x, g_mix, w_in_g, dep):
    s = x.shape[0]

    def body(x_ref, g_ref, w_ref, dep_ref, h_ref, q_ref, k_ref, v_ref, gb_ref, gc_ref, xi_ref):
        xh, _ = _rms(x_ref[...])
        h = (xh * g_ref[...]).astype(BF16)
        h_ref[...] = h
        proj = jnp.concatenate([_dot(h, w_ref[j]) for j in range(N_DEV)], axis=1)
        q_ref[...] = (proj[:, 0:512] * (HEAD_DIM ** -0.5)).astype(BF16)
        k_ref[...] = proj[:, 512:1024].astype(BF16)
        v_ref[...] = proj[:, 1024:1536].astype(BF16)
        gb_ref[...] = proj[:, 1536:2048]
        gc_ref[...] = proj[:, 2048:2560]
        xi_ref[...] = proj[:, 2560:3072]

    row = lambda n: pl.BlockSpec((TM, n), lambda i: (i, 0))
    return pl.pallas_call(
        body, name="rms_proj", grid=(s // TM,),
        out_shape=[jax.ShapeDtypeStruct((s, D_MODEL), BF16)] + [jax.ShapeDtypeStruct((s, 512), BF16)] * 3
        + [jax.ShapeDtypeStruct((s, 512), F32)] * 3,
        in_specs=[row(D_MODEL), _full(g_mix.shape), _full(w_in_g.shape), ANY_SPEC],
        out_specs=[row(D_MODEL)] + [row(512)] * 6,
        compiler_params=_cparams(1),
    )(x, g_mix, w_in_g, dep)


def _band_mask(blk):
    qi = lax.broadcasted_iota(jnp.int32, (WIN, 2 * WIN), 0)
    kj = lax.broadcasted_iota(jnp.int32, (WIN, 2 * WIN), 1)
    steps = qi + WIN - kj
    return (steps >= 0) & (steps <= WIN) & (kj >= jnp.where(blk > 0, 0, WIN))


def _swa_fwd(qv, kv, vv, bias, dil):
    rows = qv.shape[0]
    nb = rows // WIN

    def body(q_ref, kp_ref, kc_ref, vp_ref, vc_ref, b_ref, num_ref, m_ref, s_ref):
        valid = _band_mask(pl.program_id(1))
        for h in range(N_HEADS):
            sl = slice(h * HEAD_DIM, (h + 1) * HEAD_DIM)
            kh = jnp.concatenate([kp_ref[:, sl], kc_ref[:, sl]], axis=0)
            vh = jnp.concatenate([vp_ref[:, sl], vc_ref[:, sl]], axis=0)
            lg = _dot_nt(q_ref[:, sl], kh) + b_ref[h]
            lg = jnp.where(valid, lg, -jnp.inf)
            m = jnp.max(lg, axis=-1, keepdims=True)
            p = jnp.exp(lg - m)
            num_ref[:, sl] = _dot(p.astype(BF16), vh)
            m_ref[:, sl] = jnp.broadcast_to(m, (WIN, HEAD_DIM))
            s_ref[:, sl] = jnp.broadcast_to(jnp.sum(p, axis=-1, keepdims=True), (WIN, HEAD_DIM))

    cur = pl.BlockSpec((WIN, 512), lambda r, b: (b, r))
    prev = pl.BlockSpec((WIN, 512), lambda r, b: (jnp.maximum(b - 1, 0), r))
    return pl.pallas_call(
        body, name=f"swa_fwd_d{dil}", grid=(dil, nb),
        out_shape=[jax.ShapeDtypeStruct(qv.shape, F32)] * 3,
        in_specs=[cur, prev, cur, prev, cur, _full(bias.shape)],
        out_specs=[cur] * 3,
        compiler_params=_cparams(2),
    )(qv, kv, kv, vv, vv, bias)


def _mix_out(branches, gb, gc, xi, x, w_sc, g_a, g_c, w_out):
    s = x.shape[0]
    tb = TM // SUBLANES

    def body(n1, m1, s1, n2, m2, s2, n3, m3, s3, gb_ref, gc_ref, xi_ref, gch_ref, xih_ref, x_ref, wsc_ref,
             ga_ref, gcv_ref, wout_ref, attn_ref, lse_ref, mixed_ref, x1_ref):
        i = pl.program_id(0)
        m_all = jnp.maximum(jnp.maximum(m1[...], m2[...]), m3[...])
        e1, e2, e3 = jnp.exp(m1[...] - m_all), jnp.exp(m2[...] - m_all), jnp.exp(m3[...] - m_all)
        den = (e1 * s1[...] + e2 * s2[...]) + e3 * s3[...]
        num = (e1 * n1[...] + e2 * n2[...]) + e3 * n3[...]
        attn = num / den
        attn_ref[...] = attn
        lse_ref[...] = m_all + jnp.log(den)
        xa, _ = _rms(attn)
        u = gc_ref[...] * xi_ref[...]
        uh = jnp.where(i > 0, gch_ref[...] * xih_ref[...], 0.0)
        conv = gb_ref[...] * _causal_conv3(u, uh, wsc_ref)
        xc, _ = _rms(conv)
        mixed = jnp.concatenate([xa * ga_ref[...], xc * gcv_ref[...]], axis=1).astype(BF16)
        mixed_ref[...] = mixed
        x1_ref[...] = x_ref[...] + _dot(mixed, wout_ref[...])

    row = lambda n: pl.BlockSpec((TM, n), lambda i: (i, 0))
    halo = pl.BlockSpec((SUBLANES, 512), lambda i: (jnp.maximum(i * tb - 1, 0), 0))
    flat = [a for br in branches for a in br]
    return pl.pallas_call(
        body, name="mix_out", grid=(s // TM,),
        out_shape=[jax.ShapeDtypeStruct((s, 512), F32)] * 2
        + [jax.ShapeDtypeStruct((s, D_MODEL), BF16), jax.ShapeDtypeStruct((s, D_MODEL), F32)],
        in_specs=[row(512)] * 12 + [halo, halo, row(D_MODEL), _full(w_sc.shape), _full(g_a.shape),
                                    _full(g_c.shape), _full(w_out.shape)],
        out_specs=[row(512), row(512), row(D_MODEL), row(D_MODEL)],
        compiler_params=_cparams(1),
    )(*flat, gb, gc, xi, gc, xi, x, w_sc, g_a, g_c, w_out)


def _mem_kv(mem, g_mem, w_xk, w_xv):
    def body(mem_ref, g_ref, wk_ref, wv_ref, mn_ref, k_ref, v_ref):
        xh, _ = _rms(mem_ref[...])
        mn = (xh * g_ref[...]).astype(BF16)
        mn_ref[...] = mn
        k_ref[...] = _dot(mn, wk_ref[...]).astype(BF16)
        v_ref[...] = _dot(mn, wv_ref[...]).astype(BF16)

    vm = pl.BlockSpec(memory_space=pltpu.VMEM)
    return pl.pallas_call(
        body, name="mem_kv",
        out_shape=[jax.ShapeDtypeStruct(mem.shape, BF16)] * 3,
        in_specs=[vm] * 4, out_specs=[vm] * 3,
        compiler_params=pltpu.CompilerParams(vmem_limit_bytes=VMEM_LIMIT),
    )(mem, g_mem, w_xk, w_xv)


def _xattn_fwd(x1, g, w_xq, k, v, w_xo):
    s = x1.shape[0]

    def body(x1_ref, g_ref, wq_ref, k_ref, v_ref, wo_ref, h2_ref, q_ref, o_ref, x2_ref):
        x1v = x1_ref[...]
        xh, _ = _rms(x1v)
        h2 = (xh * g_ref[...]).astype(BF16)
        h2_ref[...] = h2
        qb = _dot(h2, wq_ref[...]).astype(BF16)
        q_ref[...] = qb
        outs = []
        for h in range(N_MEM_HEADS):
            sl = slice(h * MEM_HEAD_DIM, (h + 1) * MEM_HEAD_DIM)
            lg = _dot_nt(qb[:, sl], k_ref[:, sl]) * (MEM_HEAD_DIM ** -0.5)
            p = jnp.exp(lg - jnp.max(lg, axis=-1, keepdims=True))
            p = p / jnp.sum(p, axis=-1, keepdims=True)
            outs.append(_dot(p.astype(BF16), v_ref[:, sl]))
        o = jnp.concatenate(outs, axis=1).astype(BF16)
        o_ref[...] = o
        x2_ref[...] = x1v + _dot(o, wo_ref[...])

    row = pl.BlockSpec((TM, D_MODEL), lambda i: (i, 0))
    return pl.pallas_call(
        body, name="xattn_fwd", grid=(s // TM,),
        out_shape=[jax.ShapeDtypeStruct((s, D_MODEL), BF16)] * 3 + [jax.ShapeDtypeStruct((s, D_MODEL), F32)],
        in_specs=[row, _full(g.shape), _full(w_xq.shape), _full(k.shape), _full(v.shape), _full(w_xo.shape)],
        out_specs=[row] * 4,
        compiler_params=_cparams(1),
    )(x1, g, w_xq, k, v, w_xo)


def _ffn_up(x2, g, w_up_g):
    s = x2.shape[0]

    def body(x_ref, g_ref, w_ref, h_ref, up_ref, h_scr):
        @pl.when(pl.program_id(1) == 0)
        def _():
            xh, _ = _rms(x_ref[...])
            h = (xh * g_ref[...]).astype(BF16)
            h_scr[...] = h
            h_ref[...] = h

        up_ref[0] = _dot(h_scr[...], w_ref[0])

    return pl.pallas_call(
        body, name="ffn_up", grid=(s // TM, N_DEV),
        out_shape=[jax.ShapeDtypeStruct((s, D_MODEL), BF16), jax.ShapeDtypeStruct((N_DEV, s, UP_CHUNK), F32)],
        in_specs=[pl.BlockSpec((TM, D_MODEL), lambda i, j: (i, 0)), _full(g.shape),
                  pl.BlockSpec((1, D_MODEL, UP_CHUNK), lambda i, j: (j, 0, 0))],
        out_specs=[pl.BlockSpec((TM, D_MODEL), lambda i, j: (i, 0)),
                   pl.BlockSpec((1, TM, UP_CHUNK), lambda i, j: (j, i, 0))],
        scratch_shapes=[pltpu.VMEM((TM, D_MODEL), BF16)],
        compiler_params=_cparams(2),
    )(x2, g, w_up_g)


def _ffn_conv(up_ref, uph_ref, wfc_ref, bfc_ref, i, j):
    u = up_ref[j]
    uh = jnp.where(i > 0, uph_ref[j], 0.0)
    u2 = _shift_down(u, uh, 2)
    u1 = _shift_down(u, uh, 1)
    w = wfc_ref[j]
    c = ((u2 * w[0:1, :] + u1 * w[1:2, :]) + u * w[2:3, :]) + bfc_ref[j]
    return c, u2, u1, u


def _ffn_tail(up, w_fc, b_fc, w_down_g, x2, g_final, target):
    s = x2.shape[0]
    tb = TM_FFN // SUBLANES
    half = N_DEV // 2

    def body(up_ref, uph_ref, wfc_ref, bfc_ref, wd_ref, x2_ref, gf_ref, t_ref, act_ref, dx3_ref, loss_ref, dgf_ref):
        i = pl.program_id(0)

        @pl.when(i == 0)
        def _():
            loss_ref[...] = jnp.zeros_like(loss_ref)
            dgf_ref[...] = jnp.zeros_like(dgf_ref)

        down = jnp.zeros((TM_FFN, D_MODEL), F32)
        for j in range(half):
            cg = _ffn_conv(up_ref, uph_ref, wfc_ref, bfc_ref, i, j)[0]
            cv = _ffn_conv(up_ref, uph_ref, wfc_ref, bfc_ref, i, j + half)[0]
            a = ((cg * _sigmoid(cg)) * cv).astype(BF16)
            act_ref[j] = a
            down = down + _dot(a, wd_ref[j])
        x3 = x2_ref[...] + down
        xh, r = _rms(x3)
        gf = gf_ref[...]
        e = xh * gf - t_ref[...]
        loss_ref[...] += 0.5 * jnp.sum(jnp.sum(e * e, axis=1, keepdims=True), axis=0, keepdims=True) / D_MODEL
        dy = e * (1.0 / D_MODEL)
        dgf_ref[0:1, :] += jnp.sum(dy * xh, axis=0, keepdims=True)
        dx3_ref[...] = _rms_bwd(xh, r, gf, dy)

    row = pl.BlockSpec((TM_FFN, D_MODEL), lambda i: (i, 0))
    cur = pl.BlockSpec((N_DEV, TM_FFN, UP_CHUNK), lambda i: (0, i, 0))
    halo = pl.BlockSpec((N_DEV, SUBLANES, UP_CHUNK), lambda i: (0, jnp.maximum(i * tb - 1, 0), 0))
    return pl.pallas_call(
        body, name="ffn_tail", grid=(s // TM_FFN,),
        out_shape=[jax.ShapeDtypeStruct((half, s, UP_CHUNK), BF16), jax.ShapeDtypeStruct((s, D_MODEL), F32),
                   jax.ShapeDtypeStruct((SUBLANES, 128), F32), jax.ShapeDtypeStruct((SUBLANES, D_MODEL), F32)],
        in_specs=[cur, halo, _full(w_fc.shape), _full(b_fc.shape), _full(w_down_g.shape), row,
                  _full(g_final.shape), row],
        out_specs=[pl.BlockSpec((half, TM_FFN, UP_CHUNK), lambda i: (0, i, 0)), row,
                   _full((SUBLANES, 128)), _full((SUBLANES, D_MODEL))],
        compiler_params=_cparams(1),
    )(up, up, w_fc, b_fc, w_down_g, x2, g_final, target)


def _ffn_down_bwd(dx3, up, w_fc, b_fc, w_down_g):
    s = dx3.shape[0]
    tb = TM_FFN // SUBLANES
    half = N_DEV // 2

    def body(dx3_ref, up_ref, uph_ref, wfc_ref, bfc_ref, wd_ref, dc_ref, dwfc_ref, dbfc_ref):
        i = pl.program_id(0)

        @pl.when(i == 0)
        def _():
            dwfc_ref[...] = jnp.zeros_like(dwfc_ref)
            dbfc_ref[...] = jnp.zeros_like(dbfc_ref)

        dxb = dx3_ref[...].astype(BF16)

        def small_grads(j, dc, u2, u1, u):
            dc_ref[j] = dc
            dbfc_ref[j:j + 1, :] += jnp.sum(dc, axis=0, keepdims=True)
            dwfc_ref[0, j:j + 1, :] += jnp.sum(dc * u2, axis=0, keepdims=True)
            dwfc_ref[1, j:j + 1, :] += jnp.sum(dc * u1, axis=0, keepdims=True)
            dwfc_ref[2, j:j + 1, :] += jnp.sum(dc * u, axis=0, keepdims=True)

        for j in range(half):
            dact = _dot_nt(dxb, wd_ref[j])
            cg, g2, g1, g0 = _ffn_conv(up_ref, uph_ref, wfc_ref, bfc_ref, i, j)
            cv, v2, v1, v0 = _ffn_conv(up_ref, uph_ref, wfc_ref, bfc_ref, i, j + half)
            sg = _sigmoid(cg)
            small_grads(j + half, dact * (cg * sg), v2, v1, v0)
            small_grads(j, (dact * cv) * (sg * (1.0 + cg * (1.0 - sg))), g2, g1, g0)

    row = pl.BlockSpec((TM_FFN, D_MODEL), lambda i: (i, 0))
    cur = pl.BlockSpec((N_DEV, TM_FFN, UP_CHUNK), lambda i: (0, i, 0))
    halo = pl.BlockSpec((N_DEV, SUBLANES, UP_CHUNK), lambda i: (0, jnp.maximum(i * tb - 1, 0), 0))
    return pl.pallas_call(
        body, name="ffn_down_bwd", grid=(s // TM_FFN,),
        out_shape=[jax.ShapeDtypeStruct((N_DEV, s, UP_CHUNK), F32), jax.ShapeDtypeStruct((3, N_DEV, UP_CHUNK), F32),
                   jax.ShapeDtypeStruct((N_DEV, UP_CHUNK), F32)],
        in_specs=[row, cur, halo, _full(w_fc.shape), _full(b_fc.shape), _full(w_down_g.shape)],
        out_specs=[cur, _full((3, N_DEV, UP_CHUNK)), _full((N_DEV, UP_CHUNK))],
        compiler_params=_cparams(1),
    )(dx3, up, up, w_fc, b_fc, w_down_g)


def _ffn_up_bwd(dc, w_fc, w_up_g, x2, g, dx3):
    s = x2.shape[0]
    tb = TM_FFN // SUBLANES
    last = s // SUBLANES - 1
    n_tiles = s // TM_FFN

    def body(dc_ref, dch_ref, wfc_ref, wup_ref, x2_ref, g_ref, dx3_ref, dup_ref, dx2_ref, dg_ref):
        i = pl.program_id(0)

        @pl.when(i == 0)
        def _():
            dg_ref[...] = jnp.zeros_like(dg_ref)

        dh = jnp.zeros((TM_FFN, D_MODEL), F32)
        for j in range(N_DEV):
            d0 = dc_ref[j]
            dn = jnp.where(i < n_tiles - 1, dch_ref[j], 0.0)
            w = wfc_ref[j]
            du = ((d0 * w[2:3, :] + _shift_up(d0, dn, 1) * w[1:2, :]) + _shift_up(d0, dn, 2) * w[0:1, :]).astype(BF16)
            dup_ref[j] = du
            dh = dh + _dot_nt(du, wup_ref[j])
        xh, r = _rms(x2_ref[...])
        dg_ref[0:1, :] += jnp.sum(dh * xh, axis=0, keepdims=True)
        dx2_ref[...] = dx3_ref[...] + _rms_bwd(xh, r, g_ref[...], dh)

    row = pl.BlockSpec((TM_FFN, D_MODEL), lambda i: (i, 0))
    cur = pl.BlockSpec((N_DEV, TM_FFN, UP_CHUNK), lambda i: (0, i, 0))
    nxt = pl.BlockSpec((N_DEV, SUBLANES, UP_CHUNK), lambda i: (0, jnp.minimum((i + 1) * tb, last), 0))
    return pl.pallas_call(
        body, name="ffn_up_bwd", grid=(n_tiles,),
        out_shape=[jax.ShapeDtypeStruct((N_DEV, s, UP_CHUNK), BF16), jax.ShapeDtypeStruct((s, D_MODEL), F32),
                   jax.ShapeDtypeStruct((SUBLANES, D_MODEL), F32)],
        in_specs=[cur, nxt, _full(w_fc.shape), _full(w_up_g.shape), row, _full(g.shape), row],
        out_specs=[cur, row, _full((SUBLANES, D_MODEL))],
        compiler_params=_cparams(1),
    )(dc, dc, w_fc, w_up_g, x2, g, dx3)


def _xattn_bwd(dx2, o, q, k, v, w_xo, w_xq, x1, g, dep):
    s = x1.shape[0]

    def body(dx2_ref, o_ref, q_ref, k_ref, v_ref, wo_ref, wq_ref, x1_ref, g_ref, dep_ref, dq_ref, dx1_ref, dk_ref,
             dv_ref, dg_ref):
        @pl.when(pl.program_id(0) == 0)
        def _():
            dk_ref[...] = jnp.zeros_like(dk_ref)
            dv_ref[...] = jnp.zeros_like(dv_ref)
            dg_ref[...] = jnp.zeros_like(dg_ref)

        dx2v = dx2_ref[...]
        do = _dot_nt(dx2v.astype(BF16), wo_ref[...])
        dqs = []
        for h in range(N_MEM_HEADS):
            sl = slice(h * MEM_HEAD_DIM, (h + 1) * MEM_HEAD_DIM)
            qh, kh, vh = q_ref[:, sl], k_ref[:, sl], v_ref[:, sl]
            lg = _dot_nt(qh, kh) * (MEM_HEAD_DIM ** -0.5)
            p = jnp.exp(lg - jnp.max(lg, axis=-1, keepdims=True))
            p = p / jnp.sum(p, axis=-1, keepdims=True)
            doh = do[:, sl].astype(BF16)
            dp = _dot_nt(doh, vh)
            ds = (p * (dp - jnp.sum(p * dp, axis=-1, keepdims=True)) * (MEM_HEAD_DIM ** -0.5)).astype(BF16)
            dqs.append(_dot(ds, kh))
            dk_ref[:, sl] += _dot_tn(ds, qh)
            dv_ref[:, sl] += _dot_tn(p.astype(BF16), doh)
        dq = jnp.concatenate(dqs, axis=1).astype(BF16)
        dq_ref[...] = dq
        dh2 = _dot_nt(dq, wq_ref[...])
        xh, r = _rms(x1_ref[...])
        dg_ref[0:1, :] += jnp.sum(dh2 * xh, axis=0, keepdims=True)
        dx1_ref[...] = dx2v + _rms_bwd(xh, r, g_ref[...], dh2)

    row = pl.BlockSpec((TM, D_MODEL), lambda i: (i, 0))
    return pl.pallas_call(
        body, name="xattn_bwd", grid=(s // TM,),
        out_shape=[jax.ShapeDtypeStruct((s, D_MODEL), BF16), jax.ShapeDtypeStruct((s, D_MODEL), F32),
                   jax.ShapeDtypeStruct(k.shape, F32), jax.ShapeDtypeStruct(k.shape, F32),
                   jax.ShapeDtypeStruct((SUBLANES, D_MODEL), F32)],
        in_specs=[row, row, row, _full(k.shape), _full(v.shape), _full(w_xo.shape), _full(w_xq.shape), row,
                  _full(g.shape), ANY_SPEC],
        out_specs=[row, row, _full(k.shape), _full(k.shape), _full((SUBLANES, D_MODEL))],
        compiler_params=_cparams(1),
    )(dx2, o, q, k, v, w_xo, w_xq, x1, g, dep)


def _mem_kv_bwd(dk, dv, mem_n, mem, w_xk, w_xv):
    def body(dk_ref, dv_ref, mn_ref, mem_ref, wk_ref, wv_ref, dwk_ref, dwv_ref, dg_ref):
        dkb, dvb = dk_ref[...].astype(BF16), dv_ref[...].astype(BF16)
        mn = mn_ref[...]
        dwk_ref[...] = _dot_tn(mn, dkb).astype(BF16)
        dwv_ref[...] = _dot_tn(mn, dvb).astype(BF16)
        dmn = _dot_nt(dkb, wk_ref[...]) + _dot_nt(dvb, wv_ref[...])
        xh, _ = _rms(mem_ref[...])
        dg_ref[...] = jnp.zeros_like(dg_ref)
        dg_ref[0:1, :] = jnp.sum(dmn * xh, axis=0, keepdims=True)

    vm = pl.BlockSpec(memory_space=pltpu.VMEM)
    return pl.pallas_call(
        body, name="mem_kv_bwd",
        out_shape=[jax.ShapeDtypeStruct(w_xk.shape, BF16), jax.ShapeDtypeStruct(w_xv.shape, BF16),
                   jax.ShapeDtypeStruct((SUBLANES, D_MODEL), F32)],
        in_specs=[vm] * 6, out_specs=[vm] * 3,
        compiler_params=pltpu.CompilerParams(vmem_limit_bytes=VMEM_LIMIT),
    )(dk, dv, mem_n, mem, w_xk, w_xv)


def _mix_out_bwd(dx1, w_out, attn, gb, gc, xi, w_sc, g_a, g_c, dep):
    s = dx1.shape[0]
    tb = TM // SUBLANES

    def body(dx1_ref, wout_ref, attn_ref, gb_ref, gc_ref, xi_ref, gch_ref, xih_ref, wsc_ref, ga_ref, gcv_ref, dep_ref,
             dattn_ref, dd_ref, dgb_ref, dcv_ref, dga_ref, dgc_ref, dwsc_ref):
        i = pl.program_id(0)

        @pl.when(i == 0)
        def _():
            dga_ref[...] = jnp.zeros_like(dga_ref)
            dgc_ref[...] = jnp.zeros_like(dgc_ref)
            dwsc_ref[...] = jnp.zeros_like(dwsc_ref)

        dmixed = _dot_nt(dx1_ref[...].astype(BF16), wout_ref[...])
        da, dcn = dmixed[:, :ATTN_W], dmixed[:, ATTN_W:]
        attn = attn_ref[...]
        xa, ra = _rms(attn)
        dga_ref[0:1, :] += jnp.sum(da * xa, axis=0, keepdims=True)
        dattn = _rms_bwd(xa, ra, ga_ref[...], da)
        dattn_ref[...] = dattn
        prod = dattn * attn
        for h in range(N_HEADS):
            sl = slice(h * HEAD_DIM, (h + 1) * HEAD_DIM)
            dd_ref[:, sl] = jnp.broadcast_to(jnp.sum(prod[:, sl], axis=-1, keepdims=True), (TM, HEAD_DIM))
        gbv = gb_ref[...]
        u = gc_ref[...] * xi_ref[...]
        uh = jnp.where(i > 0, gch_ref[...] * xih_ref[...], 0.0)
        u2, u1 = _shift_down(u, uh, 2), _shift_down(u, uh, 1)
        cv = (u2 * wsc_ref[0:1, :] + u1 * wsc_ref[1:2, :]) + u * wsc_ref[2:3, :]
        xc, rc = _rms(gbv * cv)
        dgc_ref[0:1, :] += jnp.sum(dcn * xc, axis=0, keepdims=True)
        dconv = _rms_bwd(xc, rc, gcv_ref[...], dcn)
        dgb_ref[...] = dconv * cv
        dcv = dconv * gbv
        dcv_ref[...] = dcv
        dwsc_ref[0:1, :] += jnp.sum(dcv * u2, axis=0, keepdims=True)
        dwsc_ref[1:2, :] += jnp.sum(dcv * u1, axis=0, keepdims=True)
        dwsc_ref[2:3, :] += jnp.sum(dcv * u, axis=0, keepdims=True)

    row = lambda n: pl.BlockSpec((TM, n), lambda i: (i, 0))
    halo = pl.BlockSpec((SUBLANES, 512), lambda i: (jnp.maximum(i * tb - 1, 0), 0))
    acc = _full((SUBLANES, 512))
    return pl.pallas_call(
        body, name="mix_out_bwd", grid=(s // TM,),
        out_shape=[jax.ShapeDtypeStruct((s, 512), F32)] * 4 + [jax.ShapeDtypeStruct((SUBLANES, 512), F32)] * 3,
        in_specs=[row(D_MODEL), _full(w_out.shape), row(512), row(512), row(512), row(512), halo, halo,
                  _full(w_sc.shape), _full(g_a.shape), _full(g_c.shape), ANY_SPEC],
        out_specs=[row(512)] * 4 + [acc] * 3,
        compiler_params=_cparams(1),
    )(dx1, w_out, attn, gb, gc, xi, gc, xi, w_sc, g_a, g_c, dep)


def _swa_bwd(qv, kv, vv, dov, lsev, ddv, bias, dil, dep):
    rows = qv.shape[0]
    nb = rows // WIN

    def body(q_ref, qn_ref, kp_ref, kc_ref, vp_ref, vc_ref, do_ref, don_ref, lse_ref, lsen_ref, dd_ref, ddn_ref,
             b_ref, dep_ref, dq_ref, dk_ref, dv_ref, db_ref):
        r, b = pl.program_id(0), pl.program_id(1)

        @pl.when((r == 0) & (b == 0))
        def _():
            db_ref[...] = jnp.zeros_like(db_ref)

        valid = _band_mask(b)
        qi = lax.broadcasted_iota(jnp.int32, (WIN, WIN), 0)
        kj = lax.broadcasted_iota(jnp.int32, (WIN, WIN), 1)
        valid_n = kj >= qi + jnp.where(b + 1 < nb, 0, WIN)
        for h in range(N_HEADS):
            sl = slice(h * HEAD_DIM, (h + 1) * HEAD_DIM)
            col = slice(h * HEAD_DIM, h * HEAD_DIM + 1)
            qh, kc, vc = q_ref[:, sl], kc_ref[:, sl], vc_ref[:, sl]
            kh = jnp.concatenate([kp_ref[:, sl], kc], axis=0)
            vh = jnp.concatenate([vp_ref[:, sl], vc], axis=0)
            doh = do_ref[:, sl].astype(BF16)
            lg = jnp.where(valid, _dot_nt(qh, kh) + b_ref[h], -jnp.inf)
            p = jnp.exp(lg - lse_ref[:, col])
            ds = p * (_dot_nt(doh, vh) - dd_ref[:, col])
            db_ref[h] += ds
            dsb = ds.astype(BF16)
            dq_ref[:, sl] = _dot(dsb, kh)
            dk = _dot_tn(dsb[:, WIN:], qh)
            dv = _dot_tn(p[:, WIN:].astype(BF16), doh)
            qn = qn_ref[:, sl]
            don = don_ref[:, sl].astype(BF16)
            lgn = jnp.where(valid_n, _dot_nt(qn, kc) + b_ref[h][:, :WIN], -jnp.inf)
            pn = jnp.exp(lgn - lsen_ref[:, col])
            dsn = pn * (_dot_nt(don, vc) - ddn_ref[:, col])
            dk_ref[:, sl] = dk + _dot_tn(dsn.astype(BF16), qn)
            dv_ref[:, sl] = dv + _dot_tn(pn.astype(BF16), don)

    cur = pl.BlockSpec((WIN, 512), lambda r, b: (b, r))
    prev = pl.BlockSpec((WIN, 512), lambda r, b: (jnp.maximum(b - 1, 0), r))
    nxt = pl.BlockSpec((WIN, 512), lambda r, b: (jnp.minimum(b + 1, nb - 1), r))
    return pl.pallas_call(
        body, name=f"swa_bwd_d{dil}", grid=(dil, nb),
        out_shape=[jax.ShapeDtypeStruct(qv.shape, F32)] * 3 + [jax.ShapeDtypeStruct(bias.shape, F32)],
        in_specs=[cur, nxt, prev, cur, prev, cur, cur, nxt, cur, nxt, cur, nxt, _full(bias.shape), ANY_SPEC],
        out_specs=[cur] * 3 + [_full(bias.shape)],
        compiler_params=_cparams(2),
    )(qv, qv, kv, kv, vv, vv, dov, dov, lsev, lsev, ddv, ddv, bias, dep)


def _in_proj_bwd(dqs, dks, dvs, dgb, dcv, gc, xi, w_sc, w_in_g, x, g_mix, dx1):
    s = x.shape[0]
    tb = TM // SUBLANES
    last = s // SUBLANES - 1
    n_tiles = s // TM

    def body(dq1, dq2, dq3, dk1, dk2, dk3, dv1, dv2, dv3, dgb_ref, dcv_ref, dcvn_ref, gc_ref, xi_ref, wsc_ref,
             win_ref, x_ref, g_ref, dx1_ref, dproj_ref, gx_ref, dg_ref):
        i = pl.program_id(0)

        @pl.when(i == 0)
        def _():
            dg_ref[...] = jnp.zeros_like(dg_ref)

        d0 = dcv_ref[...]
        dn = jnp.where(i < n_tiles - 1, dcvn_ref[...], 0.0)
        du = (d0 * wsc_ref[2:3, :] + _shift_up(d0, dn, 1) * wsc_ref[1:2, :]) + _shift_up(d0, dn, 2) * wsc_ref[0:1, :]
        dq = ((dq1[...] + dq2[...]) + dq3[...]) * (HEAD_DIM ** -0.5)
        dk = (dk1[...] + dk2[...]) + dk3[...]
        dv = (dv1[...] + dv2[...]) + dv3[...]
        dproj = jnp.concatenate([dq, dk, dv, dgb_ref[...], du * xi_ref[...], du * gc_ref[...]], axis=1).astype(BF16)
        dproj_ref[...] = dproj
        dh = jnp.zeros((TM, D_MODEL), F32)
        for j in range(N_DEV):
            dh = dh + _dot_nt(dproj[:, j * IN_CHUNK:(j + 1) * IN_CHUNK], win_ref[j])
        xh, r = _rms(x_ref[...])
        dg_ref[0:1, :] += jnp.sum(dh * xh, axis=0, keepdims=True)
        gx_ref[...] = dx1_ref[...] + _rms_bwd(xh, r, g_ref[...], dh)

    row = lambda n: pl.BlockSpec((TM, n), lambda i: (i, 0))
    nxt = pl.BlockSpec((SUBLANES, 512), lambda i: (jnp.minimum((i + 1) * tb, last), 0))
    return pl.pallas_call(
        body, name="in_proj_bwd", grid=(n_tiles,),
        out_shape=[jax.ShapeDtypeStruct((s, IN_COLS), BF16), jax.ShapeDtypeStruct((s, D_MODEL), F32),
                   jax.ShapeDtypeStruct((SUBLANES, D_MODEL), F32)],
        in_specs=[row(512)] * 11 + [nxt, row(512), row(512), _full(w_sc.shape), _full(w_in_g.shape),
                                    row(D_MODEL), _full(g_mix.shape), row(D_MODEL)],
        out_specs=[row(IN_COLS), row(D_MODEL), _full((SUBLANES, D_MODEL))],
        compiler_params=_cparams(1),
    )(*dqs, *dks, *dvs, dgb, dcv, dcv, gc, xi, w_sc, w_in_g, x, g_mix, dx1)


def _dw(a, b, name, a_chunked=False, b_chunked=False, n_chunks=1, chunk_cols=None):
    ts = TM
    if a_chunked:
        nj, s, kk = a.shape
        nn = b.shape[1]
        a_spec = pl.BlockSpec((1, ts, kk), lambda j, t: (j, t, 0))
        b_spec = pl.BlockSpec((ts, nn), lambda j, t: (t, 0))
    elif b_chunked:
        nj, s, nn = b.shape
        kk = a.shape[1]
        a_spec = pl.BlockSpec((ts, kk), lambda j, t: (t, 0))
        b_spec = pl.BlockSpec((1, ts, nn), lambda j, t: (j, t, 0))
    else:
        s, kk = a.shape
        nj, nn = (n_chunks, chunk_cols) if chunk_cols else (1, b.shape[1])
        a_spec = pl.BlockSpec((ts, kk), lambda j, t: (t, 0))
        b_spec = pl.BlockSpec((ts, nn), lambda j, t: (t, j))
    n_steps = s // ts

    def body(a_ref, b_ref, o_ref, acc):
        t = pl.program_id(1)

        @pl.when(t == 0)
        def _():
            acc[...] = jnp.zeros_like(acc)

        av = (a_ref[0] if a_chunked else a_ref[...]).astype(BF16)
        bv = (b_ref[0] if b_chunked else b_ref[...]).astype(BF16)
        acc[...] += _dot_tn(av, bv)

        @pl.when(t == n_steps - 1)
        def _():
            o_ref[0] = acc[...].astype(BF16)

    return pl.pallas_call(
        body, name=name, grid=(nj, n_steps),
        out_shape=jax.ShapeDtypeStruct((nj, kk, nn), BF16),
        in_specs=[a_spec, b_spec],
        out_specs=pl.BlockSpec((1, kk, nn), lambda j, t: (j, 0, 0)),
        scratch_shapes=[pltpu.VMEM((kk, nn), F32)],
        compiler_params=_cparams(2),
    )(a, b)


def _adamw_math(w, g, m, v):
    m2 = ADAM_B1 * m + (1.0 - ADAM_B1) * g
    v2 = ADAM_B2 * v + (1.0 - ADAM_B2) * (g * g)
    m_hat = m2 / (1.0 - ADAM_B1 ** ADAM_STEP)
    v_hat = v2 / (1.0 - ADAM_B2 ** ADAM_STEP)
    delta = -ADAM_LR * (m_hat / (jnp.sqrt(v_hat) + ADAM_EPS) + ADAM_WD * w)
    return delta, m2, v2


def _sum_parts(me, own, p_ref):
    g = None
    for i in range(N_DEV):
        part = jnp.where(me == i, own.astype(F32), p_ref[i].astype(F32))
        g = part if g is None else g + part
    return g


def _adamw_big(name, w, own, parts, m, v, me_arr):
    rr, cc = w.shape
    tr = rr // 4 if rr >= 512 else rr

    def body(me_ref, w_ref, own_ref, p_ref, m_ref, v_ref, g_ref, d_ref, nm_ref, nv_ref):
        g = _sum_parts(me_ref[0], own_ref[...], p_ref)
        g_ref[...] = g
        d_ref[...], nm_ref[...], nv_ref[...] = _adamw_math(w_ref[...], g, m_ref[...], v_ref[...])

    row = pl.BlockSpec((tr, cc), lambda i: (i, 0))
    return pl.pallas_call(
        body, name=name, grid=(rr // tr,),
        out_shape=[jax.ShapeDtypeStruct((rr, cc), F32)] * 4,
        in_specs=[SMEM_SPEC, row, row, pl.BlockSpec((N_DEV, tr, cc), lambda i: (0, i, 0)), row, row],
        out_specs=[row] * 4,
        compiler_params=_cparams(1),
    )(me_arr, w, own, parts, m, v)


def _small_slices():
    return [
        (slice(ROW_RELB, ROW_RELB + 8), slice(0, N_BUCKETS)),
        (slice(ROW_GMIX, ROW_GMIX + 1), slice(0, D_MODEL)),
        (slice(ROW_GAC, ROW_GAC + 1), slice(0, ATTN_W)),
        (slice(ROW_GAC, ROW_GAC + 1), slice(ATTN_W, D_MODEL)),
        (slice(ROW_GXATTN, ROW_GXATTN + 1), slice(0, D_MODEL)),
        (slice(ROW_GMEM, ROW_GMEM + 1), slice(0, D_MODEL)),
        (slice(ROW_GFFN, ROW_GFFN + 1), slice(0, D_MODEL)),
        (slice(ROW_BFC, ROW_BFC + 8), slice(0, UP_CHUNK)),
        (slice(ROW_GFINAL, ROW_GFINAL + 1), slice(0, D_MODEL)),
    ]


def _adamw_small(own, parts, wmv, me_arr):
    slices = _small_slices()
    n = len(slices)

    def body(*refs):
        me_ref, own_ref, p_ref = refs[:3]
        ins = refs[3:3 + 3 * n]
        g_ref = refs[3 + 3 * n]
        outs = refs[4 + 3 * n:]
        g = _sum_parts(me_ref[0], own_ref[...], p_ref)
        g_ref[...] = g
        for a, (rs, ls) in enumerate(slices):
            ga = g[rs, ls]
            outs[4 * a][...] = ga
            outs[4 * a + 1][...], outs[4 * a + 2][...], outs[4 * a + 3][...] = _adamw_math(
                ins[3 * a][...], ga, ins[3 * a + 1][...], ins[3 * a + 2][...])

    vm = pl.BlockSpec(memory_space=pltpu.VMEM)
    flat = [t for trip in wmv for t in trip]
    out_shape = [jax.ShapeDtypeStruct((SMALL_ROWS, D_MODEL), F32)]
    for w, _, _ in wmv:
        out_shape += [jax.ShapeDtypeStruct(w.shape, F32)] * 4
    res = pl.pallas_call(
        body, name="adamw_small", out_shape=out_shape,
        in_specs=[SMEM_SPEC] + [vm] * (2 + 3 * n), out_specs=[vm] * len(out_shape),
    )(me_arr, own, parts, *flat)
    return res[0], [res[1 + 4 * a:5 + 4 * a] for a in range(n)]


def _adamw_shards(items):
    n = len(items)

    def body(*refs):
        for a in range(n):
            w_ref, g_ref, m_ref, v_ref = refs[4 * a:4 * a + 4]
            d_ref, nm_ref, nv_ref = refs[4 * n + 3 * a:4 * n + 3 * a + 3]
            d_ref[...], nm_ref[...], nv_ref[...] = _adamw_math(w_ref[...], g_ref[...], m_ref[...], v_ref[...])

    vm = pl.BlockSpec(memory_space=pltpu.VMEM)
    out_shape = []
    for w, _, _, _ in items:
        out_shape += [jax.ShapeDtypeStruct(w.shape, F32)] * 3
    res = pl.pallas_call(
        body, name="adamw_shards", out_shape=out_shape, in_specs=[vm] * (4 * n), out_specs=[vm] * (3 * n),
    )(*[t for it in items for t in it])
    return [res[3 * a:3 * a + 3] for a in range(n)]


def _mesh_pos():
    return lax.axis_index("x"), lax.axis_index("y"), lax.axis_index("c")


def _dev_index(p):
    return 4 * p[0] + 2 * p[1] + p[2]


def _all_gather(shards):
    n = len(shards)

    def body(*refs):
        ins, outs = refs[:n], refs[n:2 * n]
        send_sems, recv_sems, loc_sems = refs[2 * n:]
        x, y, c = _mesh_pos()
        me, sib = (x, y, c), (x, y, 1 - c)
        chips = [(1 - x, y), (x, 1 - y), (1 - x, 1 - y)]

        def cp(a, k, block, to, src=None):
            dst = outs[a].at[_dev_index(block)]
            return pltpu.make_async_remote_copy(
                src_ref=dst if src is None else src, dst_ref=dst, send_sem=send_sems.at[a, k],
                recv_sem=recv_sems.at[a, k], device_id=to, device_id_type=MESH)

        mine = [pltpu.make_async_copy(ins[a], outs[a].at[_dev_index(me)], loc_sems.at[a]) for a in range(n)]
        for m_ in mine:
            m_.start()
        first = []
        for a in range(n):
            first.append(cp(a, 0, me, sib, src=ins[a]))
            first += [cp(a, 1 + j, me, (*chip, c), src=ins[a]) for j, chip in enumerate(chips)]
        for f in first:
            f.start()
        passed = []
        for a in range(n):
            for j, chip in enumerate(chips):
                cp(a, 1 + j, (*chip, c), me).wait_recv()
                fwd = cp(a, 4 + j, (*chip, c), sib)
                fwd.start()
                passed.append(fwd)
        for a in range(n):
            cp(a, 0, sib, me).wait_recv()
            for j, chip in enumerate(chips):
                cp(a, 4 + j, (*chip, 1 - c), me).wait_recv()
        for f in first + passed:
            f.wait_send()
        for m_ in mine:
            m_.wait()

    hbm = pl.BlockSpec(memory_space=pltpu.HBM)
    return pl.pallas_call(
        body, name="all_gather_weights",
        out_shape=[jax.ShapeDtypeStruct((N_DEV,) + a.shape, a.dtype) for a in shards],
        in_specs=[hbm] * n, out_specs=[hbm] * n,
        scratch_shapes=[pltpu.SemaphoreType.DMA((n, 7)), pltpu.SemaphoreType.DMA((n, 7)),
                        pltpu.SemaphoreType.DMA((n,))],
    )(*shards)


def _peers():
    x, y, c = _mesh_pos()
    return (x, y, c), [((1 - x) if k & 4 else x, (1 - y) if k & 2 else y, (1 - c) if k & 1 else c)
                       for k in range(1, 8)]


def _exchange_copy(src_ref, land_ref, whole, send_sems, recv_sems, a, k, peer, slot):
    src = src_ref if whole else src_ref.at[_dev_index(peer)]
    return pltpu.make_async_remote_copy(
        src_ref=src, dst_ref=land_ref.at[slot], send_sem=send_sems.at[7 * a + k], recv_sem=recv_sems.at[7 * a + k],
        device_id=peer, device_id_type=MESH)


def _exchange_start(name, srcs, whole):
    n = len(srcs)
    lands = [lax.empty(((N_DEV,) + s.shape) if w else s.shape, s.dtype) for s, w in zip(srcs, whole)]

    def body(*refs):
        src_refs, land_refs = refs[:n], refs[n:2 * n]
        send_sems, recv_sems, token = refs[2 * n], refs[2 * n + 1], refs[-1]
        me, peers = _peers()
        for a in range(n):
            for k, peer in enumerate(peers):
                _exchange_copy(src_refs[a], land_refs[a], whole[a], send_sems, recv_sems, a, k, peer,
                               _dev_index(me)).start()
        token[...] = jnp.zeros_like(token)

    res = pl.pallas_call(
        body, name=name,
        out_shape=(pltpu.SemaphoreType.DMA((7 * n,)), pltpu.SemaphoreType.DMA((7 * n,)),
                   *[pltpu.HBM(a.shape, a.dtype) for a in srcs], *[pltpu.HBM(a.shape, a.dtype) for a in lands],
                   jax.ShapeDtypeStruct((SUBLANES, 128), F32)),
        in_specs=[HBM_SPEC] * (2 * n),
        out_specs=(SEM_SPEC, SEM_SPEC, *([HBM_SPEC] * (2 * n)), VMEM_SPEC),
        input_output_aliases={i: 2 + i for i in range(2 * n)},
        compiler_params=pltpu.CompilerParams(has_side_effects=DATAFLOW),
    )(*[pltpu.with_memory_space_constraint(a, pltpu.HBM) for a in srcs],
      *[pltpu.with_memory_space_constraint(a, pltpu.HBM) for a in lands])
    return res[0], res[1], list(res[2:2 + n]), list(res[2 + n:2 + 2 * n]), res[-1]


def _exchange_wait(name, started, whole, after):
    send_sems, recv_sems, srcs, lands, _ = started
    n = len(srcs)

    def body(*refs):
        src_refs, land_refs = refs[:n], refs[n:2 * n]
        send_sems, recv_sems = refs[2 * n], refs[2 * n + 1]
        _, peers = _peers()
        for a in range(n):
            for k, peer in enumerate(peers):
                cp = _exchange_copy(src_refs[a], land_refs[a], whole[a], send_sems, recv_sems, a, k, peer,
                                    _dev_index(peer))
                cp.wait_send()
                cp.wait_recv()

    res = pl.pallas_call(
        body, name=name,
        out_shape=[pltpu.HBM(a.shape, a.dtype) for a in srcs + lands],
        in_specs=[HBM_SPEC] * (2 * n) + [SEM_SPEC, SEM_SPEC, ANY_SPEC],
        out_specs=[HBM_SPEC] * (2 * n),
        input_output_aliases={i: i for i in range(2 * n)},
        compiler_params=pltpu.CompilerParams(has_side_effects=DATAFLOW),
    )(*srcs, *lands, send_sems, recv_sems, after)
    return list(res[n:])


def _views(a, dil):
    s = a.shape[0]
    return a.reshape(s // dil, dil * a.shape[1])


def _local_step(x, mem, target, rel_bias, g_mix, w_in_g, w_sc, g_a, g_c, g_xattn, g_mem, g_ffn, w_fc, b_fc, g_final,
                dep, late_weights, emit):
    s = x.shape[0]
    buckets = _bucket_tables()
    bias = _bias_fwd(rel_bias, buckets)

    h1, q, k, v, gb, gc, xi = _rms_proj(x, g_mix, w_in_g, dep)
    branches = []
    for p, dil in enumerate(DILATIONS):
        outs = _swa_fwd(_views(q, dil), _views(k, dil), _views(v, dil), bias[p], dil)
        branches.append([o.reshape(s, ATTN_W) for o in outs])
    lw = late_weights(branches[-1][0])
    w_out, w_xq, w_xk, w_xv, w_xo, w_up_g, w_down_g = (lw[n] for n in ("w_out", "w_xq", "w_xk", "w_xv", "w_xo",
                                                                       "w_up", "w_down"))
    attn, lse, mixed, x1 = _mix_out(branches, gb, gc, xi, x, w_sc, g_a, g_c, w_out)
    mem_n, mk, mv = _mem_kv(mem, g_mem, w_xk, w_xv)
    h2, xq, xo, x2 = _xattn_fwd(x1, g_xattn, w_xq, mk, mv, w_xo)
    h3, up = _ffn_up(x2, g_ffn, w_up_g)
    act, dx3, loss_acc, dg_final = _ffn_tail(up, w_fc, b_fc, w_down_g, x2, g_final, target)

    dc, dw_fc, db_fc = _ffn_down_bwd(dx3, up, w_fc, b_fc, w_down_g)
    gw_down = _dw(act, dx3, "dw_down", a_chunked=True)
    dup, dx2, dg_ffn = _ffn_up_bwd(dc, w_fc, w_up_g, x2, g_ffn, dx3)
    gw_up = _dw(h3, dup, "dw_up", b_chunked=True)
    tok = emit(dict(w_down=gw_down, w_up=gw_up))
    dxq, dx1, dmk, dmv, dg_xattn = _xattn_bwd(dx2, xo, xq, mk, mv, w_xo, w_xq, x1, g_xattn, tok)
    gw_xo = _dw(xo, dx2, "dw_xo")[0]
    gw_xq = _dw(h2, dxq, "dw_xq")[0]
    gw_xk, gw_xv, dg_mem = _mem_kv_bwd(dmk, dmv, mem_n, mem, w_xk, w_xv)
    tok = emit(dict(w_xo=gw_xo, w_xq=gw_xq, w_xk=gw_xk, w_xv=gw_xv))
    dattn, dd, dgb, dcv, dg_a, dg_c, dw_sc = _mix_out_bwd(dx1, w_out, attn, gb, gc, xi, w_sc, g_a, g_c, tok)
    gw_out = _dw(mixed, dx1, "dw_out")[0]
    tok = emit(dict(w_out=gw_out))
    dqs, dks, dvs, dbias = [], [], [], []
    for p, dil in enumerate(DILATIONS):
        dq_p, dk_p, dv_p, db_p = _swa_bwd(_views(q, dil), _views(k, dil), _views(v, dil), _views(dattn, dil),
                                          _views(lse, dil), _views(dd, dil), bias[p], dil, tok)
        dqs.append(dq_p.reshape(s, ATTN_W))
        dks.append(dk_p.reshape(s, ATTN_W))
        dvs.append(dv_p.reshape(s, ATTN_W))
        dbias.append(db_p)
    d_relb = _bias_bwd(jnp.stack(dbias), buckets)
    dproj, grad_x, dg_mix = _in_proj_bwd(dqs, dks, dvs, dgb, dcv, gc, xi, w_sc, w_in_g, x, g_mix, dx1)
    gw_in = _dw(h1, dproj, "dw_in", n_chunks=N_DEV, chunk_cols=IN_CHUNK)
    emit(dict(w_in=gw_in))

    pad = lambda a: jnp.pad(a, ((0, 0), (0, D_MODEL - a.shape[1])))
    small = jnp.concatenate([
        d_relb, dg_mix, dg_xattn, dg_mem, dg_ffn, dg_final, jnp.concatenate([dg_a, dg_c], axis=1),
        pad(dw_sc), pad(db_fc), pad(dw_fc.reshape(3 * N_DEV, UP_CHUNK))], axis=0)
    return loss_acc[0, 0], grad_x, small


def kernel(x, mem, rel_bias, g_mix, w_in, w_short_conv, g_attn_out, g_conv_out, w_out, g_xattn, g_mem, w_xq, w_xk, w_xv, w_xo, g_ffn, w_up, w_ffn_conv, b_ffn_conv, w_down, g_final, loss_target, m_rel_bias, m_g_mix, m_w_in, m_w_short_conv, m_g_attn_out, m_g_conv_out, m_w_out, m_g_xattn, m_g_mem, m_w_xq, m_w_xk, m_w_xv, m_w_xo, m_g_ffn, m_w_up, m_w_ffn_conv, m_b_ffn_conv, m_w_down, m_g_final, v_rel_bias, v_g_mix, v_w_in, v_w_short_conv, v_g_attn_out, v_g_conv_out, v_w_out, v_g_xattn, v_g_mem, v_w_xq, v_w_xk, v_w_xv, v_w_xo, v_g_ffn, v_w_up, v_w_ffn_conv, v_b_ffn_conv, v_w_down, v_g_final):
    me = _dev_index(_mesh_pos())
    me_arr = me.reshape(1).astype(jnp.int32)

    big_names = ["w_in", "w_out", "w_xq", "w_xk", "w_xv", "w_xo", "w_up", "w_down"]
    late_names = big_names[1:]
    big_w = dict(w_in=w_in[0], w_out=w_out[0], w_xq=w_xq[0], w_xk=w_xk[0], w_xv=w_xv[0], w_xo=w_xo[0],
                 w_up=w_up[0], w_down=w_down[0])
    big_m = dict(w_in=m_w_in[0], w_out=m_w_out[0], w_xq=m_w_xq[0], w_xk=m_w_xk[0], w_xv=m_w_xv[0], w_xo=m_w_xo[0],
                 w_up=m_w_up[0], w_down=m_w_down[0])
    big_v = dict(w_in=v_w_in[0], w_out=v_w_out[0], w_xq=v_w_xq[0], w_xk=v_w_xk[0], w_xv=v_w_xv[0], w_xo=v_w_xo[0],
                 w_up=v_w_up[0], w_down=v_w_down[0])
    shard_shape = {n: big_w[n].shape for n in big_names}

    w_in_g, w_sc_g, w_fc_full = _all_gather([big_w["w_in"].astype(BF16), w_short_conv[0], w_ffn_conv[0]])
    w_sc_full = w_sc_g.transpose(1, 0, 2).reshape(3, CONV_W)
    late_shards = [big_w[n].astype(BF16) for n in late_names]
    ag = _exchange_start("gather_weights_start", late_shards, [True] * len(late_names))

    def late_weights(after):
        lands = _exchange_wait("gather_weights_wait", ag, [True] * len(late_names), after)
        full = {n: lax.dynamic_update_index_in_dim(land, shard, me, 0)
                for n, land, shard in zip(late_names, lands, late_shards)}
        out = {n: full[n].reshape(D_MODEL, D_MODEL) for n in ("w_out", "w_xq", "w_xk", "w_xv", "w_xo")}
        out["w_up"] = full["w_up"]
        out["w_down"] = full["w_down"].reshape(N_DEV // 2, UP_CHUNK, D_MODEL)
        return out

    sent = []

    def emit(grads):
        names = list(grads)
        blocks = [grads[n].reshape((N_DEV,) + shard_shape[n]) for n in names]
        own = [lax.dynamic_index_in_dim(b, me, 0, keepdims=False) for b in blocks]
        started = _exchange_start("scatter_" + "_".join(names) + "_start", blocks, [False] * len(names))
        sent.append((names, own, started))
        return started[-1]

    loss_part, grad_x, small_g = _local_step(
        x[0], mem[0], loss_target[0], rel_bias, g_mix, w_in_g, w_sc_full, g_attn_out, g_conv_out, g_xattn, g_mem,
        g_ffn, w_fc_full, b_ffn_conv.reshape(N_DEV, 1, UP_CHUNK), g_final.reshape(1, D_MODEL), ag[-1],
        late_weights, emit)
    loss = lax.psum(loss_part, ("x", "y", "c"))

    small_started = _exchange_start("gather_small_start", [small_g], [True])
    small_parts = _exchange_wait("gather_small_wait", small_started, [True], grad_x)[0]
    own_g, recv_g = {}, {}
    for names, own, started in sent:
        lands = _exchange_wait("scatter_" + "_".join(names) + "_wait", started, [False] * len(names), small_parts)
        own_g.update(zip(names, own))
        recv_g.update(zip(names, lands))

    big_out = {}
    for n in big_names:
        res = _adamw_big("adamw_" + n, big_w[n], own_g[n], recv_g[n], big_m[n], big_v[n], me_arr)
        big_out[n] = [r[None] for r in res]

    as_rows = lambda a: a.reshape(N_DEV, UP_CHUNK)
    row1 = lambda a: a.reshape(1, D_MODEL)
    small_names = ["rel_bias", "g_mix", "g_attn_out", "g_conv_out", "g_xattn", "g_mem", "g_ffn", "b_ffn_conv", "g_final"]
    wmv = [
        (rel_bias, m_rel_bias, v_rel_bias), (g_mix, m_g_mix, v_g_mix), (g_attn_out, m_g_attn_out, v_g_attn_out),
        (g_conv_out, m_g_conv_out, v_g_conv_out), (g_xattn, m_g_xattn, v_g_xattn), (g_mem, m_g_mem, v_g_mem),
        (g_ffn, m_g_ffn, v_g_ffn), (as_rows(b_ffn_conv), as_rows(m_b_ffn_conv), as_rows(v_b_ffn_conv)),
        (row1(g_final), row1(m_g_final), row1(v_g_final))]
    g_packed, small_res = _adamw_small(small_g, small_parts, wmv, me_arr)
    small_out = dict(zip(small_names, small_res))
    small_out["b_ffn_conv"] = [a.reshape(1, 2 * D_FF) for a in small_out["b_ffn_conv"]]
    small_out["g_final"] = [a.reshape(D_MODEL) for a in small_out["g_final"]]

    g_wsc = lax.dynamic_slice(g_packed[ROW_WSC:ROW_WSC + 3, 0:CONV_W], (0, me * HEAD_DIM), (3, HEAD_DIM))
    g_wfc = lax.dynamic_slice(g_packed[ROW_WFC:ROW_WFC + 3 * N_DEV, 0:UP_CHUNK].reshape(3, N_DEV, UP_CHUNK),
                              (0, me, 0), (3, 1, UP_CHUNK)).reshape(3, UP_CHUNK)
    shard_res = _adamw_shards([(w_short_conv[0], g_wsc, m_w_short_conv[0], v_w_short_conv[0]),
                               (w_ffn_conv[0], g_wfc, m_w_ffn_conv[0], v_w_ffn_conv[0])])
    small_out["w_short_conv"] = [g_wsc[None]] + [a[None] for a in shard_res[0]]
    small_out["w_ffn_conv"] = [g_wfc[None]] + [a[None] for a in shard_res[1]]

    order = ["rel_bias", "g_mix", "w_in", "w_short_conv", "g_attn_out", "g_conv_out", "w_out", "g_xattn", "g_mem",
             "w_xq", "w_xk", "w_xv", "w_xo", "g_ffn", "w_up", "w_ffn_conv", "b_ffn_conv", "w_down", "g_final"]
    allp = {**big_out, **small_out}
    outs = [loss, grad_x[None]]
    for kind in range(4):
        outs += [allp[n][kind] for n in order]
    return tuple(outs)
```

```python
import functools
import math

import numpy as np
import jax
import jax.numpy as jnp
from jax import lax
from jax.experimental import pallas as pl
from jax.experimental.pallas import tpu as pltpu

F32 = jnp.float32
BF16 = jnp.bfloat16
MESH = pl.DeviceIdType.MESH

N_DEV = 8
D_MODEL = 1024
ATTN_W = 512
CONV_W = 512
N_HEADS = 8
HEAD_DIM = 64
WIN = 128
DILATIONS = (1, 4, 16)
N_BUCKETS = 32
BUCKET_MAX_EXACT = 16
BUCKET_MAX_DISTANCE = 2048
N_MEM_HEADS = 4
MEM_HEAD_DIM = 256
D_FF = 2816
IN_COLS = 3072
IN_CHUNK = IN_COLS // N_DEV
UP_CHUNK = 2 * D_FF // N_DEV
EPS = 1e-6

ADAM_LR = 0.001
ADAM_B1 = 0.9
ADAM_B2 = 0.999
ADAM_EPS = 1e-08
ADAM_WD = 0.01
ADAM_STEP = 10

SUBLANES = 8
TM = 512
TM_FFN = 256
VMEM_LIMIT = 56 * 1024 * 1024

ROW_RELB, ROW_GMIX, ROW_GXATTN, ROW_GMEM, ROW_GFFN, ROW_GFINAL, ROW_GAC = 0, 8, 16, 24, 32, 40, 48
ROW_WSC, ROW_BFC, ROW_WFC, SMALL_ROWS = 56, 64, 72, 96


def _cparams(n_grid):
    return pltpu.CompilerParams(dimension_semantics=("arbitrary",) * n_grid, vmem_limit_bytes=VMEM_LIMIT)


def _full(shape):
    nd = len(shape)
    return pl.BlockSpec(tuple(shape), lambda *_: (0,) * nd)


ANY_SPEC = pl.BlockSpec(memory_space=pl.ANY)
HBM_SPEC = pl.BlockSpec(memory_space=pltpu.HBM)
SEM_SPEC = pl.BlockSpec(memory_space=pltpu.SEMAPHORE)
VMEM_SPEC = pl.BlockSpec(memory_space=pltpu.VMEM)
SMEM_SPEC = pl.BlockSpec(memory_space=pltpu.SMEM)
DATAFLOW = pltpu.SideEffectType.DATAFLOW_SIDE_EFFECTING


def _rms(x):
    r = lax.rsqrt(jnp.mean(x * x, axis=-1, keepdims=True) + EPS)
    return x * r, r


def _rms_bwd(xh, r, g, dy):
    dxh = dy * g
    return r * (dxh - xh * jnp.mean(dxh * xh, axis=-1, keepdims=True))


def _shift_down(u, halo, k):
    ru = pltpu.roll(u, k, 0)
    rh = pltpu.roll(halo, k, 0)
    row = lax.broadcasted_iota(jnp.int32, rh.shape, 0)
    head = jnp.where(row < k, rh, ru[0:SUBLANES])
    return jnp.concatenate([head, ru[SUBLANES:]], axis=0)


def _shift_up(u, halo, k):
    tm = u.shape[0]
    ru = pltpu.roll(u, tm - k, 0)
    rh = pltpu.roll(halo, SUBLANES - k, 0)
    row = lax.broadcasted_iota(jnp.int32, rh.shape, 0)
    tail = jnp.where(row >= SUBLANES - k, rh, ru[tm - SUBLANES:])
    return jnp.concatenate([ru[:tm - SUBLANES], tail], axis=0)


def _causal_conv3(u, halo, w_ref):
    return (_shift_down(u, halo, 2) * w_ref[0:1, :] + _shift_down(u, halo, 1) * w_ref[1:2, :]) + u * w_ref[2:3, :]


def _dot(a, b):
    return jnp.dot(a, b, preferred_element_type=F32)


def _dot_nt(a, b):
    return lax.dot_general(a, b, (((1,), (1,)), ((), ())), preferred_element_type=F32)


def _dot_tn(a, b):
    return lax.dot_general(a, b, (((0,), (0,)), ((), ())), preferred_element_type=F32)


def _sigmoid(x):
    return 1.0 / (1.0 + jnp.exp(-x))


def _bucket_tables():
    qi = np.arange(WIN)[:, None]
    kj = np.arange(2 * WIN)[None, :]
    steps = np.clip(qi + WIN - kj, 0, WIN)
    out = []
    for d in DILATIONS:
        dist = steps * d
        dd = np.maximum(dist, 1).astype(np.float32)
        large = BUCKET_MAX_EXACT + (
            np.log(dd / np.float32(BUCKET_MAX_EXACT)) / np.float32(math.log(BUCKET_MAX_DISTANCE / BUCKET_MAX_EXACT))
            * np.float32(N_BUCKETS - BUCKET_MAX_EXACT)).astype(np.int32)
        large = np.minimum(large, N_BUCKETS - 1)
        out.append(np.where(dist < BUCKET_MAX_EXACT, dist, large).astype(np.int32))
    return jnp.asarray(np.stack(out))


def _bias_fwd(rel_bias, buckets):
    def body(rb_ref, bk_ref, o_ref):
        for p in range(3):
            bk = bk_ref[p]
            for h in range(N_HEADS):
                acc = jnp.zeros((WIN, 2 * WIN), F32)
                for b in range(N_BUCKETS):
                    acc = jnp.where(bk == b, rb_ref[h, b], acc)
                o_ref[p, h] = acc

    return pl.pallas_call(
        body, name="bias_fwd",
        out_shape=jax.ShapeDtypeStruct((3, N_HEADS, WIN, 2 * WIN), F32),
        in_specs=[pl.BlockSpec(memory_space=pltpu.SMEM), pl.BlockSpec(memory_space=pltpu.VMEM)],
        out_specs=pl.BlockSpec(memory_space=pltpu.VMEM),
    )(rel_bias, buckets)


def _bias_bwd(dbias, buckets):
    def body(db_ref, bk_ref, o_ref):
        lane = lax.broadcasted_iota(jnp.int32, (1, D_MODEL), 1)
        rows = []
        for h in range(N_HEADS):
            row = jnp.zeros((1, D_MODEL), F32)
            for b in range(N_BUCKETS):
                tot = jnp.zeros((1, 1), F32)
                for p in range(3):
                    sel = jnp.where(bk_ref[p] == b, db_ref[p, h], 0.0)
                    tot = tot + jnp.sum(jnp.sum(sel, axis=0, keepdims=True), axis=1, keepdims=True)
                row = jnp.where(lane == b, tot, row)
            rows.append(row)
        o_ref[...] = jnp.concatenate(rows, axis=0)

    return pl.pallas_call(
        body, name="bias_bwd",
        out_shape=jax.ShapeDtypeStruct((N_HEADS, D_MODEL), F32),
        in_specs=[pl.BlockSpec(memory_space=pltpu.VMEM), pl.BlockSpec(memory_space=pltpu.VMEM)],
        out_specs=pl.BlockSpec(memory_space=pltpu.VMEM),
    )(dbias, buckets)


def _rms_proj(x, g_mix, w_in_g, dep):
    s = x.shape[0]

    def body(x_ref, g_ref, w_ref, dep_ref, h_ref, q_ref, k_ref, v_ref, gb_ref, gc_ref, xi_ref):
        xh, _ = _rms(x_ref[...])
        h = (xh * g_ref[...]).astype(BF16)
        h_ref[...] = h
        proj = jnp.concatenate([_dot(h, w_ref[j]) for j in range(N_DEV)], axis=1)
        q_ref[...] = (proj[:, 0:512] * (HEAD_DIM ** -0.5)).astype(BF16)
        k_ref[...] = proj[:, 512:1024].astype(BF16)
        v_ref[...] = proj[:, 1024:1536].astype(BF16)
        gb_ref[...] = proj[:, 1536:2048]
        gc_ref[...] = proj[:, 2048:2560]
        xi_ref[...] = proj[:, 2560:3072]

    row = lambda n: pl.BlockSpec((TM, n), lambda i: (i, 0))
    return pl.pallas_call(
        body, name="rms_proj", grid=(s // TM,),
        out_shape=[jax.ShapeDtypeStruct((s, D_MODEL), BF16)] + [jax.ShapeDtypeStruct((s, 512), BF16)] * 3
        + [jax.ShapeDtypeStruct((s, 512), F32)] * 3,
        in_specs=[row(D_MODEL), _full(g_mix.shape), _full(w_in_g.shape), ANY_SPEC],
        out_specs=[row(D_MODEL)] + [row(512)] * 6,
        compiler_params=_cparams(1),
    )(x, g_mix, w_in_g, dep)


def _band_mask(blk):
    qi = lax.broadcasted_iota(jnp.int32, (WIN, 2 * WIN), 0)
    kj = lax.broadcasted_iota(jnp.int32, (WIN, 2 * WIN), 1)
    steps = qi + WIN - kj
    return (steps >= 0) & (steps <= WIN) & (kj >= jnp.where(blk > 0, 0, WIN))


def _swa_fwd(qv, kv, vv, bias, dil):
    rows = qv.shape[0]
    nb = rows // WIN

    def body(q_ref, kp_ref, kc_ref, vp_ref, vc_ref, b_ref, num_ref, m_ref, s_ref):
        valid = _band_mask(pl.program_id(1))
        for h in range(N_HEADS):
            sl = slice(h * HEAD_DIM, (h + 1) * HEAD_DIM)
            kh = jnp.concatenate([kp_ref[:, sl], kc_ref[:, sl]], axis=0)
            vh = jnp.concatenate([vp_ref[:, sl], vc_ref[:, sl]], axis=0)
            lg = _dot_nt(q_ref[:, sl], kh) + b_ref[h]
            lg = jnp.where(valid, lg, -jnp.inf)
            m = jnp.max(lg, axis=-1, keepdims=True)
            p = jnp.exp(lg - m)
            num_ref[:, sl] = _dot(p.astype(BF16), vh)
            m_ref[:, sl] = jnp.broadcast_to(m, (WIN, HEAD_DIM))
            s_ref[:, sl] = jnp.broadcast_to(jnp.sum(p, axis=-1, keepdims=True), (WIN, HEAD_DIM))

    cur = pl.BlockSpec((WIN, 512), lambda r, b: (b, r))
    prev = pl.BlockSpec((WIN, 512), lambda r, b: (jnp.maximum(b - 1, 0), r))
    return pl.pallas_call(
        body, name=f"swa_fwd_d{dil}", grid=(dil, nb),
        out_shape=[jax.ShapeDtypeStruct(qv.shape, F32)] * 3,
        in_specs=[cur, prev, cur, prev, cur, _full(bias.shape)],
        out_specs=[cur] * 3,
        compiler_params=_cparams(2),
    )(qv, kv, kv, vv, vv, bias)


def _mix_out(branches, gb, gc, xi, x, w_sc, g_a, g_c, w_out):
    s = x.shape[0]
    tb = TM // SUBLANES

    def body(n1, m1, s1, n2, m2, s2, n3, m3, s3, gb_ref, gc_ref, xi_ref, gch_ref, xih_ref, x_ref, wsc_ref,
             ga_ref, gcv_ref, wout_ref, attn_ref, lse_ref, mixed_ref, x1_ref):
        i = pl.program_id(0)
        m_all = jnp.maximum(jnp.maximum(m1[...], m2[...]), m3[...])
        e1, e2, e3 = jnp.exp(m1[...] - m_all), jnp.exp(m2[...] - m_all), jnp.exp(m3[...] - m_all)
        den = (e1 * s1[...] + e2 * s2[...]) + e3 * s3[...]
        num = (e1 * n1[...] + e2 * n2[...]) + e3 * n3[...]
        attn = num / den
        attn_ref[...] = attn
        lse_ref[...] = m_all + jnp.log(den)
        xa, _ = _rms(attn)
        u = gc_ref[...] * xi_ref[...]
        uh = jnp.where(i > 0, gch_ref[...] * xih_ref[...], 0.0)
        conv = gb_ref[...] * _causal_conv3(u, uh, wsc_ref)
        xc, _ = _rms(conv)
        mixed = jnp.concatenate([xa * ga_ref[...], xc * gcv_ref[...]], axis=1).astype(BF16)
        mixed_ref[...] = mixed
        x1_ref[...] = x_ref[...] + _dot(mixed, wout_ref[...])

    row = lambda n: pl.BlockSpec((TM, n), lambda i: (i, 0))
    halo = pl.BlockSpec((SUBLANES, 512), lambda i: (jnp.maximum(i * tb - 1, 0), 0))
    flat = [a for br in branches for a in br]
    return pl.pallas_call(
        body, name="mix_out", grid=(s // TM,),
        out_shape=[jax.ShapeDtypeStruct((s, 512), F32)] * 2
        + [jax.ShapeDtypeStruct((s, D_MODEL), BF16), jax.ShapeDtypeStruct((s, D_MODEL), F32)],
        in_specs=[row(512)] * 12 + [halo, halo, row(D_MODEL), _full(w_sc.shape), _full(g_a.shape),
                                    _full(g_c.shape), _full(w_out.shape)],
        out_specs=[row(512), row(512), row(D_MODEL), row(D_MODEL)],
        compiler_params=_cparams(1),
    )(*flat, gb, gc, xi, gc, xi, x, w_sc, g_a, g_c, w_out)


def _mem_kv(mem, g_mem, w_xk, w_xv):
    def body(mem_ref, g_ref, wk_ref, wv_ref, mn_ref, k_ref, v_ref):
        xh, _ = _rms(mem_ref[...])
        mn = (xh * g_ref[...]).astype(BF16)
        mn_ref[...] = mn
        k_ref[...] = _dot(mn, wk_ref[...]).astype(BF16)
        v_ref[...] = _dot(mn, wv_ref[...]).astype(BF16)

    vm = pl.BlockSpec(memory_space=pltpu.VMEM)
    return pl.pallas_call(
        body, name="mem_kv",
        out_shape=[jax.ShapeDtypeStruct(mem.shape, BF16)] * 3,
        in_specs=[vm] * 4, out_specs=[vm] * 3,
        compiler_params=pltpu.CompilerParams(vmem_limit_bytes=VMEM_LIMIT),
    )(mem, g_mem, w_xk, w_xv)


def _xattn_fwd(x1, g, w_xq, k, v, w_xo):
    s = x1.shape[0]

    def body(x1_ref, g_ref, wq_ref, k_ref, v_ref, wo_ref, h2_ref, q_ref, o_ref, x2_ref):
        x1v = x1_ref[...]
        xh, _ = _rms(x1v)
        h2 = (xh * g_ref[...]).astype(BF16)
        h2_ref[...] = h2
        qb = _dot(h2, wq_ref[...]).astype(BF16)
        q_ref[...] = qb
        outs = []
        for h in range(N_MEM_HEADS):
            sl = slice(h * MEM_HEAD_DIM, (h + 1) * MEM_HEAD_DIM)
            lg = _dot_nt(qb[:, sl], k_ref[:, sl]) * (MEM_HEAD_DIM ** -0.5)
            p = jnp.exp(lg - jnp.max(lg, axis=-1, keepdims=True))
            p = p / jnp.sum(p, axis=-1, keepdims=True)
            outs.append(_dot(p.astype(BF16), v_ref[:, sl]))
        o = jnp.concatenate(outs, axis=1).astype(BF16)
        o_ref[...] = o
        x2_ref[...] = x1v + _dot(o, wo_ref[...])

    row = pl.BlockSpec((TM, D_MODEL), lambda i: (i, 0))
    return pl.pallas_call(
        body, name="xattn_fwd", grid=(s // TM,),
        out_shape=[jax.ShapeDtypeStruct((s, D_MODEL), BF16)] * 3 + [jax.ShapeDtypeStruct((s, D_MODEL), F32)],
        in_specs=[row, _full(g.shape), _full(w_xq.shape), _full(k.shape), _full(v.shape), _full(w_xo.shape)],
        out_specs=[row] * 4,
        compiler_params=_cparams(1),
    )(x1, g, w_xq, k, v, w_xo)


def _ffn_up(x2, g, w_up_g):
    s = x2.shape[0]

    def body(x_ref, g_ref, w_ref, h_ref, up_ref, h_scr):
        @pl.when(pl.program_id(1) == 0)
        def _():
            xh, _ = _rms(x_ref[...])
            h = (xh * g_ref[...]).astype(BF16)
            h_scr[...] = h
            h_ref[...] = h

        up_ref[0] = _dot(h_scr[...], w_ref[0])

    return pl.pallas_call(
        body, name="ffn_up", grid=(s // TM, N_DEV),
        out_shape=[jax.ShapeDtypeStruct((s, D_MODEL), BF16), jax.ShapeDtypeStruct((N_DEV, s, UP_CHUNK), F32)],
        in_specs=[pl.BlockSpec((TM, D_MODEL), lambda i, j: (i, 0)), _full(g.shape),
                  pl.BlockSpec((1, D_MODEL, UP_CHUNK), lambda i, j: (j, 0, 0))],
        out_specs=[pl.BlockSpec((TM, D_MODEL), lambda i, j: (i, 0)),
                   pl.BlockSpec((1, TM, UP_CHUNK), lambda i, j: (j, i, 0))],
        scratch_shapes=[pltpu.VMEM((TM, D_MODEL), BF16)],
        compiler_params=_cparams(2),
    )(x2, g, w_up_g)


def _ffn_conv(up_ref, uph_ref, wfc_ref, bfc_ref, i, j):
    u = up_ref[j]
    uh = jnp.where(i > 0, uph_ref[j], 0.0)
    u2 = _shift_down(u, uh, 2)
    u1 = _shift_down(u, uh, 1)
    w = wfc_ref[j]
    c = ((u2 * w[0:1, :] + u1 * w[1:2, :]) + u * w[2:3, :]) + bfc_ref[j]
    return c, u2, u1, u


def _ffn_tail(up, w_fc, b_fc, w_down_g, x2, g_final, target):
    s = x2.shape[0]
    tb = TM_FFN // SUBLANES
    half = N_DEV // 2

    def body(up_ref, uph_ref, wfc_ref, bfc_ref, wd_ref, x2_ref, gf_ref, t_ref, act_ref, dx3_ref, loss_ref, dgf_ref):
        i = pl.program_id(0)

        @pl.when(i == 0)
        def _():
            loss_ref[...] = jnp.zeros_like(loss_ref)
            dgf_ref[...] = jnp.zeros_like(dgf_ref)

        down = jnp.zeros((TM_FFN, D_MODEL), F32)
        for j in range(half):
            cg = _ffn_conv(up_ref, uph_ref, wfc_ref, bfc_ref, i, j)[0]
            cv = _ffn_conv(up_ref, uph_ref, wfc_ref, bfc_ref, i, j + half)[0]
            a = ((cg * _sigmoid(cg)) * cv).astype(BF16)
            act_ref[j] = a
            down = down + _dot(a, wd_ref[j])
        x3 = x2_ref[...] + down
        xh, r = _rms(x3)
        gf = gf_ref[...]
        e = xh * gf - t_ref[...]
        loss_ref[...] += 0.5 * jnp.sum(jnp.sum(e * e, axis=1, keepdims=True), axis=0, keepdims=True) / D_MODEL
        dy = e * (1.0 / D_MODEL)
        dgf_ref[0:1, :] += jnp.sum(dy * xh, axis=0, keepdims=True)
        dx3_ref[...] = _rms_bwd(xh, r, gf, dy)

    row = pl.BlockSpec((TM_FFN, D_MODEL), lambda i: (i, 0))
    cur = pl.BlockSpec((N_DEV, TM_FFN, UP_CHUNK), lambda i: (0, i, 0))
    halo = pl.BlockSpec((N_DEV, SUBLANES, UP_CHUNK), lambda i: (0, jnp.maximum(i * tb - 1, 0), 0))
    return pl.pallas_call(
        body, name="ffn_tail", grid=(s // TM_FFN,),
        out_shape=[jax.ShapeDtypeStruct((half, s, UP_CHUNK), BF16), jax.ShapeDtypeStruct((s, D_MODEL), F32),
                   jax.ShapeDtypeStruct((SUBLANES, 128), F32), jax.ShapeDtypeStruct((SUBLANES, D_MODEL), F32)],
        in_specs=[cur, halo, _full(w_fc.shape), _full(b_fc.shape), _full(w_down_g.shape), row,
                  _full(g_final.shape), row],
        out_specs=[pl.BlockSpec((half, TM_FFN, UP_CHUNK), lambda i: (0, i, 0)), row,
                   _full((SUBLANES, 128)), _full((SUBLANES, D_MODEL))],
        compiler_params=_cparams(1),
    )(up, up, w_fc, b_fc, w_down_g, x2, g_final, target)


def _ffn_down_bwd(dx3, up, w_fc, b_fc, w_down_g):
    s = dx3.shape[0]
    tb = TM_FFN // SUBLANES
    half = N_DEV // 2

    def body(dx3_ref, up_ref, uph_ref, wfc_ref, bfc_ref, wd_ref, dc_ref, dwfc_ref, dbfc_ref):
        i = pl.program_id(0)

        @pl.when(i == 0)
        def _():
            dwfc_ref[...] = jnp.zeros_like(dwfc_ref)
            dbfc_ref[...] = jnp.zeros_like(dbfc_ref)

        dxb = dx3_ref[...].astype(BF16)

        def small_grads(j, dc, u2, u1, u):
            dc_ref[j] = dc
            dbfc_ref[j:j + 1, :] += jnp.sum(dc, axis=0, keepdims=True)
            dwfc_ref[0, j:j + 1, :] += jnp.sum(dc * u2, axis=0, keepdims=True)
            dwfc_ref[1, j:j + 1, :] += jnp.sum(dc * u1, axis=0, keepdims=True)
            dwfc_ref[2, j:j + 1, :] += jnp.sum(dc * u, axis=0, keepdims=True)

        for j in range(half):
            dact = _dot_nt(dxb, wd_ref[j])
            cg, g2, g1, g0 = _ffn_conv(up_ref, uph_ref, wfc_ref, bfc_ref, i, j)
            cv, v2, v1, v0 = _ffn_conv(up_ref, uph_ref, wfc_ref, bfc_ref, i, j + half)
            sg = _sigmoid(cg)
            small_grads(j + half, dact * (cg * sg), v2, v1, v0)
            small_grads(j, (dact * cv) * (sg * (1.0 + cg * (1.0 - sg))), g2, g1, g0)

    row = pl.BlockSpec((TM_FFN, D_MODEL), lambda i: (i, 0))
    cur = pl.BlockSpec((N_DEV, TM_FFN, UP_CHUNK), lambda i: (0, i, 0))
    halo = pl.BlockSpec((N_DEV, SUBLANES, UP_CHUNK), lambda i: (0, jnp.maximum(i * tb - 1, 0), 0))
    return pl.pallas_call(
        body, name="ffn_down_bwd", grid=(s // TM_FFN,),
        out_shape=[jax.ShapeDtypeStruct((N_DEV, s, UP_CHUNK), F32), jax.ShapeDtypeStruct((3, N_DEV, UP_CHUNK), F32),
                   jax.ShapeDtypeStruct((N_DEV, UP_CHUNK), F32)],
        in_specs=[row, cur, halo, _full(w_fc.shape), _full(b_fc.shape), _full(w_down_g.shape)],
        out_specs=[cur, _full((3, N_DEV, UP_CHUNK)), _full((N_DEV, UP_CHUNK))],
        compiler_params=_cparams(1),
    )(dx3, up, up, w_fc, b_fc, w_down_g)


def _ffn_up_bwd(dc, w_fc, w_up_g, x2, g, dx3):
    s = x2.shape[0]
    tb = TM_FFN // SUBLANES
    last = s // SUBLANES - 1
    n_tiles = s // TM_FFN

    def body(dc_ref, dch_ref, wfc_ref, wup_ref, x2_ref, g_ref, dx3_ref, dup_ref, dx2_ref, dg_ref):
        i = pl.program_id(0)

        @pl.when(i == 0)
        def _():
            dg_ref[...] = jnp.zeros_like(dg_ref)

        dh = jnp.zeros((TM_FFN, D_MODEL), F32)
        for j in range(N_DEV):
            d0 = dc_ref[j]
            dn = jnp.where(i < n_tiles - 1, dch_ref[j], 0.0)
            w = wfc_ref[j]
            du = ((d0 * w[2:3, :] + _shift_up(d0, dn, 1) * w[1:2, :]) + _shift_up(d0, dn, 2) * w[0:1, :]).astype(BF16)
            dup_ref[j] = du
            dh = dh + _dot_nt(du, wup_ref[j])
        xh, r = _rms(x2_ref[...])
        dg_ref[0:1, :] += jnp.sum(dh * xh, axis=0, keepdims=True)
        dx2_ref[...] = dx3_ref[...] + _rms_bwd(xh, r, g_ref[...], dh)

    row = pl.BlockSpec((TM_FFN, D_MODEL), lambda i: (i, 0))
    cur = pl.BlockSpec((N_DEV, TM_FFN, UP_CHUNK), lambda i: (0, i, 0))
    nxt = pl.BlockSpec((N_DEV, SUBLANES, UP_CHUNK), lambda i: (0, jnp.minimum((i + 1) * tb, last), 0))
    return pl.pallas_call(
        body, name="ffn_up_bwd", grid=(n_tiles,),
        out_shape=[jax.ShapeDtypeStruct((N_DEV, s, UP_CHUNK), BF16), jax.ShapeDtypeStruct((s, D_MODEL), F32),
                   jax.ShapeDtypeStruct((SUBLANES, D_MODEL), F32)],
        in_specs=[cur, nxt, _full(w_fc.shape), _full(w_up_g.shape), row, _full(g.shape), row],
        out_specs=[cur, row, _full((SUBLANES, D_MODEL))],
        compiler_params=_cparams(1),
    )(dc, dc, w_fc, w_up_g, x2, g, dx3)


def _xattn_bwd(dx2, o, q, k, v, w_xo, w_xq, x1, g, dep):
    s = x1.shape[0]

    def body(dx2_ref, o_ref, q_ref, k_ref, v_ref, wo_ref, wq_ref, x1_ref, g_ref, dep_ref, dq_ref, dx1_ref, dk_ref,
             dv_ref, dg_ref):
        @pl.when(pl.program_id(0) == 0)
        def _():
            dk_ref[...] = jnp.zeros_like(dk_ref)
            dv_ref[...] = jnp.zeros_like(dv_ref)
            dg_ref[...] = jnp.zeros_like(dg_ref)

        dx2v = dx2_ref[...]
        do = _dot_nt(dx2v.astype(BF16), wo_ref[...])
        dqs = []
        for h in range(N_MEM_HEADS):
            sl = slice(h * MEM_HEAD_DIM, (h + 1) * MEM_HEAD_DIM)
            qh, kh, vh = q_ref[:, sl], k_ref[:, sl], v_ref[:, sl]
            lg = _dot_nt(qh, kh) * (MEM_HEAD_DIM ** -0.5)
            p = jnp.exp(lg - jnp.max(lg, axis=-1, keepdims=True))
            p = p / jnp.sum(p, axis=-1, keepdims=True)
            doh = do[:, sl].astype(BF16)
            dp = _dot_nt(doh, vh)
            ds = (p * (dp - jnp.sum(p * dp, axis=-1, keepdims=True)) * (MEM_HEAD_DIM ** -0.5)).astype(BF16)
            dqs.append(_dot(ds, kh))
            dk_ref[:, sl] += _dot_tn(ds, qh)
            dv_ref[:, sl] += _dot_tn(p.astype(BF16), doh)
        dq = jnp.concatenate(dqs, axis=1).astype(BF16)
        dq_ref[...] = dq
        dh2 = _dot_nt(dq, wq_ref[...])
        xh, r = _rms(x1_ref[...])
        dg_ref[0:1, :] += jnp.sum(dh2 * xh, axis=0, keepdims=True)
        dx1_ref[...] = dx2v + _rms_bwd(xh, r, g_ref[...], dh2)

    row = pl.BlockSpec((TM, D_MODEL), lambda i: (i, 0))
    return pl.pallas_call(
        body, name="xattn_bwd", grid=(s // TM,),
        out_shape=[jax.ShapeDtypeStruct((s, D_MODEL), BF16), jax.ShapeDtypeStruct((s, D_MODEL), F32),
                   jax.ShapeDtypeStruct(k.shape, F32), jax.ShapeDtypeStruct(k.shape, F32),
                   jax.ShapeDtypeStruct((SUBLANES, D_MODEL), F32)],
        in_specs=[row, row, row, _full(k.shape), _full(v.shape), _full(w_xo.shape), _full(w_xq.shape), row,
                  _full(g.shape), ANY_SPEC],
        out_specs=[row, row, _full(k.shape), _full(k.shape), _full((SUBLANES, D_MODEL))],
        compiler_params=_cparams(1),
    )(dx2, o, q, k, v, w_xo, w_xq, x1, g, dep)


def _mem_kv_bwd(dk, dv, mem_n, mem, w_xk, w_xv):
    def body(dk_ref, dv_ref, mn_ref, mem_ref, wk_ref, wv_ref, dwk_ref, dwv_ref, dg_ref):
        dkb, dvb = dk_ref[...].astype(BF16), dv_ref[...].astype(BF16)
        mn = mn_ref[...]
        dwk_ref[...] = _dot_tn(mn, dkb).astype(BF16)
        dwv_ref[...] = _dot_tn(mn, dvb).astype(BF16)
        dmn = _dot_nt(dkb, wk_ref[...]) + _dot_nt(dvb, wv_ref[...])
        xh, _ = _rms(mem_ref[...])
        dg_ref[...] = jnp.zeros_like(dg_ref)
        dg_ref[0:1, :] = jnp.sum(dmn * xh, axis=0, keepdims=True)

    vm = pl.BlockSpec(memory_space=pltpu.VMEM)
    return pl.pallas_call(
        body, name="mem_kv_bwd",
        out_shape=[jax.ShapeDtypeStruct(w_xk.shape, BF16), jax.ShapeDtypeStruct(w_xv.shape, BF16),
                   jax.ShapeDtypeStruct((SUBLANES, D_MODEL), F32)],
        in_specs=[vm] * 6, out_specs=[vm] * 3,
        compiler_params=pltpu.CompilerParams(vmem_limit_bytes=VMEM_LIMIT),
    )(dk, dv, mem_n, mem, w_xk, w_xv)


def _mix_out_bwd(dx1, w_out, attn, gb, gc, xi, w_sc, g_a, g_c, dep):
    s = dx1.shape[0]
    tb = TM // SUBLANES

    def body(dx1_ref, wout_ref, attn_ref, gb_ref, gc_ref, xi_ref, gch_ref, xih_ref, wsc_ref, ga_ref, gcv_ref, dep_ref,
             dattn_ref, dd_ref, dgb_ref, dcv_ref, dga_ref, dgc_ref, dwsc_ref):
        i = pl.program_id(0)

        @pl.when(i == 0)
        def _():
            dga_ref[...] = jnp.zeros_like(dga_ref)
            dgc_ref[...] = jnp.zeros_like(dgc_ref)
            dwsc_ref[...] = jnp.zeros_like(dwsc_ref)

        dmixed = _dot_nt(dx1_ref[...].astype(BF16), wout_ref[...])
        da, dcn = dmixed[:, :ATTN_W], dmixed[:, ATTN_W:]
        attn = attn_ref[...]
        xa, ra = _rms(attn)
        dga_ref[0:1, :] += jnp.sum(da * xa, axis=0, keepdims=True)
        dattn = _rms_bwd(xa, ra, ga_ref[...], da)
        dattn_ref[...] = dattn
        prod = dattn * attn
        for h in range(N_HEADS):
            sl = slice(h * HEAD_DIM, (h + 1) * HEAD_DIM)
            dd_ref[:, sl] = jnp.broadcast_to(jnp.sum(prod[:, sl], axis=-1, keepdims=True), (TM, HEAD_DIM))
        gbv = gb_ref[...]
        u = gc_ref[...] * xi_ref[...]
        uh = jnp.where(i > 0, gch_ref[...] * xih_ref[...], 0.0)
        u2, u1 = _shift_down(u, uh, 2), _shift_down(u, uh, 1)
        cv = (u2 * wsc_ref[0:1, :] + u1 * wsc_ref[1:2, :]) + u * wsc_ref[2:3, :]
        xc, rc = _rms(gbv * cv)
        dgc_ref[0:1, :] += jnp.sum(dcn * xc, axis=0, keepdims=True)
        dconv = _rms_bwd(xc, rc, gcv_ref[...], dcn)
        dgb_ref[...] = dconv * cv
        dcv = dconv * gbv
        dcv_ref[...] = dcv
        dwsc_ref[0:1, :] += jnp.sum(dcv * u2, axis=0, keepdims=True)
        dwsc_ref[1:2, :] += jnp.sum(dcv * u1, axis=0, keepdims=True)
        dwsc_ref[2:3, :] += jnp.sum(dcv * u, axis=0, keepdims=True)

    row = lambda n: pl.BlockSpec((TM, n), lambda i: (i, 0))
    halo = pl.BlockSpec((SUBLANES, 512), lambda i: (jnp.maximum(i * tb - 1, 0), 0))
    acc = _full((SUBLANES, 512))
    return pl.pallas_call(
        body, name="mix_out_bwd", grid=(s // TM,),
        out_shape=[jax.ShapeDtypeStruct((s, 512), F32)] * 4 + [jax.ShapeDtypeStruct((SUBLANES, 512), F32)] * 3,
        in_specs=[row(D_MODEL), _full(w_out.shape), row(512), row(512), row(512), row(512), halo, halo,
                  _full(w_sc.shape), _full(g_a.shape), _full(g_c.shape), ANY_SPEC],
        out_specs=[row(512)] * 4 + [acc] * 3,
        compiler_params=_cparams(1),
    )(dx1, w_out, attn, gb, gc, xi, gc, xi, w_sc, g_a, g_c, dep)


def _swa_bwd(qv, kv, vv, dov, lsev, ddv, bias, dil, dep):
    rows = qv.shape[0]
    nb = rows // WIN

    def body(q_ref, qn_ref, kp_ref, kc_ref, vp_ref, vc_ref, do_ref, don_ref, lse_ref, lsen_ref, dd_ref, ddn_ref,
             b_ref, dep_ref, dq_ref, dk_ref, dv_ref, db_ref):
        r, b = pl.program_id(0), pl.program_id(1)

        @pl.when((r == 0) & (b == 0))
        def _():
            db_ref[...] = jnp.zeros_like(db_ref)

        valid = _band_mask(b)
        qi = lax.broadcasted_iota(jnp.int32, (WIN, WIN), 0)
        kj = lax.broadcasted_iota(jnp.int32, (WIN, WIN), 1)
        valid_n = kj >= qi + jnp.where(b + 1 < nb, 0, WIN)
        for h in range(N_HEADS):
            sl = slice(h * HEAD_DIM, (h + 1) * HEAD_DIM)
            col = slice(h * HEAD_DIM, h * HEAD_DIM + 1)
            qh, kc, vc = q_ref[:, sl], kc_ref[:, sl], vc_ref[:, sl]
            kh = jnp.concatenate([kp_ref[:, sl], kc], axis=0)
            vh = jnp.concatenate([vp_ref[:, sl], vc], axis=0)
            doh = do_ref[:, sl].astype(BF16)
            lg = jnp.where(valid, _dot_nt(qh, kh) + b_ref[h], -jnp.inf)
            p = jnp.exp(lg - lse_ref[:, col])
            ds = p * (_dot_nt(doh, vh) - dd_ref[:, col])
            db_ref[h] += ds
            dsb = ds.astype(BF16)
            dq_ref[:, sl] = _dot(dsb, kh)
            dk = _dot_tn(dsb[:, WIN:], qh)
            dv = _dot_tn(p[:, WIN:].astype(BF16), doh)
            qn = qn_ref[:, sl]
            don = don_ref[:, sl].astype(BF16)
            lgn = jnp.where(valid_n, _dot_nt(qn, kc) + b_ref[h][:, :WIN], -jnp.inf)
            pn = jnp.exp(lgn - lsen_ref[:, col])
            dsn = pn * (_dot_nt(don, vc) - ddn_ref[:, col])
            dk_ref[:, sl] = dk + _dot_tn(dsn.astype(BF16), qn)
            dv_ref[:, sl] = dv + _dot_tn(pn.astype(BF16), don)

    cur = pl.BlockSpec((WIN, 512), lambda r, b: (b, r))
    prev = pl.BlockSpec((WIN, 512), lambda r, b: (jnp.maximum(b - 1, 0), r))
    nxt = pl.BlockSpec((WIN, 512), lambda r, b: (jnp.minimum(b + 1, nb - 1), r))
    return pl.pallas_call(
        body, name=f"swa_bwd_d{dil}", grid=(dil, nb),
        out_shape=[jax.ShapeDtypeStruct(qv.shape, F32)] * 3 + [jax.ShapeDtypeStruct(bias.shape, F32)],
        in_specs=[cur, nxt, prev, cur, prev, cur, cur, nxt, cur, nxt, cur, nxt, _full(bias.shape), ANY_SPEC],
        out_specs=[cur] * 3 + [_full(bias.shape)],
        compiler_params=_cparams(2),
    )(qv, qv, kv, kv, vv, vv, dov, dov, lsev, lsev, ddv, ddv, bias, dep)


def _in_proj_bwd(dqs, dks, dvs, dgb, dcv, gc, xi, w_sc, w_in_g, x, g_mix, dx1):
    s = x.shape[0]
    tb = TM // SUBLANES
    last = s // SUBLANES - 1
    n_tiles = s // TM

    def body(dq1, dq2, dq3, dk1, dk2, dk3, dv1, dv2, dv3, dgb_ref, dcv_ref, dcvn_ref, gc_ref, xi_ref, wsc_ref,
             win_ref, x_ref, g_ref, dx1_ref, dproj_ref, gx_ref, dg_ref):
        i = pl.program_id(0)

        @pl.when(i == 0)
        def _():
            dg_ref[...] = jnp.zeros_like(dg_ref)

        d0 = dcv_ref[...]
        dn = jnp.where(i < n_tiles - 1, dcvn_ref[...], 0.0)
        du = (d0 * wsc_ref[2:3, :] + _shift_up(d0, dn, 1) * wsc_ref[1:2, :]) + _shift_up(d0, dn, 2) * wsc_ref[0:1, :]
        dq = ((dq1[...] + dq2[...]) + dq3[...]) * (HEAD_DIM ** -0.5)
        dk = (dk1[...] + dk2[...]) + dk3[...]
        dv = (dv1[...] + dv2[...]) + dv3[...]
        dproj = jnp.concatenate([dq, dk, dv, dgb_ref[...], du * xi_ref[...], du * gc_ref[...]], axis=1).astype(BF16)
        dproj_ref[...] = dproj
        dh = jnp.zeros((TM, D_MODEL), F32)
        for j in range(N_DEV):
            dh = dh + _dot_nt(dproj[:, j * IN_CHUNK:(j + 1) * IN_CHUNK], win_ref[j])
        xh, r = _rms(x_ref[...])
        dg_ref[0:1, :] += jnp.sum(dh * xh, axis=0, keepdims=True)
        gx_ref[...] = dx1_ref[...] + _rms_bwd(xh, r, g_ref[...], dh)

    row = lambda n: pl.BlockSpec((TM, n), lambda i: (i, 0))
    nxt = pl.BlockSpec((SUBLANES, 512), lambda i: (jnp.minimum((i + 1) * tb, last), 0))
    return pl.pallas_call(
        body, name="in_proj_bwd", grid=(n_tiles,),
        out_shape=[jax.ShapeDtypeStruct((s, IN_COLS), BF16), jax.ShapeDtypeStruct((s, D_MODEL), F32),
                   jax.ShapeDtypeStruct((SUBLANES, D_MODEL), F32)],
        in_specs=[row(512)] * 11 + [nxt, row(512), row(512), _full(w_sc.shape), _full(w_in_g.shape),
                                    row(D_MODEL), _full(g_mix.shape), row(D_MODEL)],
        out_specs=[row(IN_COLS), row(D_MODEL), _full((SUBLANES, D_MODEL))],
        compiler_params=_cparams(1),
    )(*dqs, *dks, *dvs, dgb, dcv, dcv, gc, xi, w_sc, w_in_g, x, g_mix, dx1)


def _dw(a, b, name, a_chunked=False, b_chunked=False, n_chunks=1, chunk_cols=None):
    ts = TM
    if a_chunked:
        nj, s, kk = a.shape
        nn = b.shape[1]
        a_spec = pl.BlockSpec((1, ts, kk), lambda j, t: (j, t, 0))
        b_spec = pl.BlockSpec((ts, nn), lambda j, t: (t, 0))
    elif b_chunked:
        nj, s, nn = b.shape
        kk = a.shape[1]
        a_spec = pl.BlockSpec((ts, kk), lambda j, t: (t, 0))
        b_spec = pl.BlockSpec((1, ts, nn), lambda j, t: (j, t, 0))
    else:
        s, kk = a.shape
        nj, nn = (n_chunks, chunk_cols) if chunk_cols else (1, b.shape[1])
        a_spec = pl.BlockSpec((ts, kk), lambda j, t: (t, 0))
        b_spec = pl.BlockSpec((ts, nn), lambda j, t: (t, j))
    n_steps = s // ts

    def body(a_ref, b_ref, o_ref, acc):
        t = pl.program_id(1)

        @pl.when(t == 0)
        def _():
            acc[...] = jnp.zeros_like(acc)

        av = (a_ref[0] if a_chunked else a_ref[...]).astype(BF16)
        bv = (b_ref[0] if b_chunked else b_ref[...]).astype(BF16)
        acc[...] += _dot_tn(av, bv)

        @pl.when(t == n_steps - 1)
        def _():
            o_ref[0] = acc[...].astype(BF16)

    return pl.pallas_call(
        body, name=name, grid=(nj, n_steps),
        out_shape=jax.ShapeDtypeStruct((nj, kk, nn), BF16),
        in_specs=[a_spec, b_spec],
        out_specs=pl.BlockSpec((1, kk, nn), lambda j, t: (j, 0, 0)),
        scratch_shapes=[pltpu.VMEM((kk, nn), F32)],
        compiler_params=_cparams(2),
    )(a, b)


def _adamw_math(w, g, m, v):
    m2 = ADAM_B1 * m + (1.0 - ADAM_B1) * g
    v2 = ADAM_B2 * v + (1.0 - ADAM_B2) * (g * g)
    m_hat = m2 / (1.0 - ADAM_B1 ** ADAM_STEP)
    v_hat = v2 / (1.0 - ADAM_B2 ** ADAM_STEP)
    delta = -ADAM_LR * (m_hat / (jnp.sqrt(v_hat) + ADAM_EPS) + ADAM_WD * w)
    return delta, m2, v2


def _sum_parts(me, own, p_ref):
    g = None
    for i in range(N_DEV):
        part = jnp.where(me == i, own.astype(F32), p_ref[i].astype(F32))
        g = part if g is None else g + part
    return g


def _adamw_big(name, w, own, parts, m, v, me_arr):
    rr, cc = w.shape
    tr = rr // 4 if rr >= 512 else rr

    def body(me_ref, w_ref, own_ref, p_ref, m_ref, v_ref, g_ref, d_ref, nm_ref, nv_ref):
        g = _sum_parts(me_ref[0], own_ref[...], p_ref)
        g_ref[...] = g
        d_ref[...], nm_ref[...], nv_ref[...] = _adamw_math(w_ref[...], g, m_ref[...], v_ref[...])

    row = pl.BlockSpec((tr, cc), lambda i: (i, 0))
    return pl.pallas_call(
        body, name=name, grid=(rr // tr,),
        out_shape=[jax.ShapeDtypeStruct((rr, cc), F32)] * 4,
        in_specs=[SMEM_SPEC, row, row, pl.BlockSpec((N_DEV, tr, cc), lambda i: (0, i, 0)), row, row],
        out_specs=[row] * 4,
        compiler_params=_cparams(1),
    )(me_arr, w, own, parts, m, v)


def _small_slices():
    return [
        (slice(ROW_RELB, ROW_RELB + 8), slice(0, N_BUCKETS)),
        (slice(ROW_GMIX, ROW_GMIX + 1), slice(0, D_MODEL)),
        (slice(ROW_GAC, ROW_GAC + 1), slice(0, ATTN_W)),
        (slice(ROW_GAC, ROW_GAC + 1), slice(ATTN_W, D_MODEL)),
        (slice(ROW_GXATTN, ROW_GXATTN + 1), slice(0, D_MODEL)),
        (slice(ROW_GMEM, ROW_GMEM + 1), slice(0, D_MODEL)),
        (slice(ROW_GFFN, ROW_GFFN + 1), slice(0, D_MODEL)),
        (slice(ROW_BFC, ROW_BFC + 8), slice(0, UP_CHUNK)),
        (slice(ROW_GFINAL, ROW_GFINAL + 1), slice(0, D_MODEL)),
    ]


def _adamw_small(own, parts, wmv, me_arr):
    slices = _small_slices()
    n = len(slices)

    def body(*refs):
        me_ref, own_ref, p_ref = refs[:3]
        ins = refs[3:3 + 3 * n]
        g_ref = refs[3 + 3 * n]
        outs = refs[4 + 3 * n:]
        g = _sum_parts(me_ref[0], own_ref[...], p_ref)
        g_ref[...] = g
        for a, (rs, ls) in enumerate(slices):
            ga = g[rs, ls]
            outs[4 * a][...] = ga
            outs[4 * a + 1][...], outs[4 * a + 2][...], outs[4 * a + 3][...] = _adamw_math(
                ins[3 * a][...], ga, ins[3 * a + 1][...], ins[3 * a + 2][...])

    vm = pl.BlockSpec(memory_space=pltpu.VMEM)
    flat = [t for trip in wmv for t in trip]
    out_shape = [jax.ShapeDtypeStruct((SMALL_ROWS, D_MODEL), F32)]
    for w, _, _ in wmv:
        out_shape += [jax.ShapeDtypeStruct(w.shape, F32)] * 4
    res = pl.pallas_call(
        body, name="adamw_small", out_shape=out_shape,
        in_specs=[SMEM_SPEC] + [vm] * (2 + 3 * n), out_specs=[vm] * len(out_shape),
    )(me_arr, own, parts, *flat)
    return res[0], [res[1 + 4 * a:5 + 4 * a] for a in range(n)]


def _adamw_shards(items):
    n = len(items)

    def body(*refs):
        for a in range(n):
            w_ref, g_ref, m_ref, v_ref = refs[4 * a:4 * a + 4]
            d_ref, nm_ref, nv_ref = refs[4 * n + 3 * a:4 * n + 3 * a + 3]
            d_ref[...], nm_ref[...], nv_ref[...] = _adamw_math(w_ref[...], g_ref[...], m_ref[...], v_ref[...])

    vm = pl.BlockSpec(memory_space=pltpu.VMEM)
    out_shape = []
    for w, _, _, _ in items:
        out_shape += [jax.ShapeDtypeStruct(w.shape, F32)] * 3
    res = pl.pallas_call(
        body, name="adamw_shards", out_shape=out_shape, in_specs=[vm] * (4 * n), out_specs=[vm] * (3 * n),
    )(*[t for it in items for t in it])
    return [res[3 * a:3 * a + 3] for a in range(n)]


def _mesh_pos():
    return lax.axis_index("x"), lax.axis_index("y"), lax.axis_index("c")


def _dev_index(p):
    return 4 * p[0] + 2 * p[1] + p[2]


def _all_gather(shards):
    n = len(shards)

    def body(*refs):
        ins, outs = refs[:n], refs[n:2 * n]
        send_sems, recv_sems, loc_sems = refs[2 * n:]
        x, y, c = _mesh_pos()
        me, sib = (x, y, c), (x, y, 1 - c)
        chips = [(1 - x, y), (x, 1 - y), (1 - x, 1 - y)]

        def cp(a, k, block, to, src=None):
            dst = outs[a].at[_dev_index(block)]
            return pltpu.make_async_remote_copy(
                src_ref=dst if src is None else src, dst_ref=dst, send_sem=send_sems.at[a, k],
                recv_sem=recv_sems.at[a, k], device_id=to, device_id_type=MESH)

        mine = [pltpu.make_async_copy(ins[a], outs[a].at[_dev_index(me)], loc_sems.at[a]) for a in range(n)]
        for m_ in mine:
            m_.start()
        first = []
        for a in range(n):
            first.append(cp(a, 0, me, sib, src=ins[a]))
            first += [cp(a, 1 + j, me, (*chip, c), src=ins[a]) for j, chip in enumerate(chips)]
        for f in first:
            f.start()
        passed = []
        for a in range(n):
            for j, chip in enumerate(chips):
                cp(a, 1 + j, (*chip, c), me).wait_recv()
                fwd = cp(a, 4 + j, (*chip, c), sib)
                fwd.start()
                passed.append(fwd)
        for a in range(n):
            cp(a, 0, sib, me).wait_recv()
            for j, chip in enumerate(chips):
                cp(a, 4 + j, (*chip, 1 - c), me).wait_recv()
        for f in first + passed:
            f.wait_send()
        for m_ in mine:
            m_.wait()

    hbm = pl.BlockSpec(memory_space=pltpu.HBM)
    return pl.pallas_call(
        body, name="all_gather_weights",
        out_shape=[jax.ShapeDtypeStruct((N_DEV,) + a.shape, a.dtype) for a in shards],
        in_specs=[hbm] * n, out_specs=[hbm] * n,
        scratch_shapes=[pltpu.SemaphoreType.DMA((n, 7)), pltpu.SemaphoreType.DMA((n, 7)),
                        pltpu.SemaphoreType.DMA((n,))],
    )(*shards)


def _peers():
    x, y, c = _mesh_pos()
    return (x, y, c), [((1 - x) if k & 4 else x, (1 - y) if k & 2 else y, (1 - c) if k & 1 else c)
                       for k in range(1, 8)]


def _exchange_copy(src_ref, land_ref, whole, send_sems, recv_sems, a, k, peer, slot):
    src = src_ref if whole else src_ref.at[_dev_index(peer)]
    return pltpu.make_async_remote_copy(
        src_ref=src, dst_ref=land_ref.at[slot], send_sem=send_sems.at[7 * a + k], recv_sem=recv_sems.at[7 * a + k],
        device_id=peer, device_id_type=MESH)


def _exchange_start(name, srcs, whole, dep):
    n = len(srcs)
    lands = [lax.empty(((N_DEV,) + s.shape) if w else s.shape, s.dtype) for s, w in zip(srcs, whole)]

    def body(*refs):
        src_refs, land_refs = refs[:n], refs[n:2 * n]
        send_sems, recv_sems, token = refs[2 * n + 1], refs[2 * n + 2], refs[-1]
        me, peers = _peers()
        for a in range(n):
            for k, peer in enumerate(peers):
                _exchange_copy(src_refs[a], land_refs[a], whole[a], send_sems, recv_sems, a, k, peer,
                               _dev_index(me)).start()
        token[...] = jnp.zeros_like(token)

    res = pl.pallas_call(
        body, name=name,
        out_shape=(pltpu.SemaphoreType.DMA((7 * n,)), pltpu.SemaphoreType.DMA((7 * n,)),
                   *[pltpu.HBM(a.shape, a.dtype) for a in srcs], *[pltpu.HBM(a.shape, a.dtype) for a in lands],
                   jax.ShapeDtypeStruct((SUBLANES, 128), F32)),
        in_specs=[HBM_SPEC] * (2 * n) + [ANY_SPEC],
        out_specs=(SEM_SPEC, SEM_SPEC, *([HBM_SPEC] * (2 * n)), VMEM_SPEC),
        input_output_aliases={i: 2 + i for i in range(2 * n)},
        compiler_params=pltpu.CompilerParams(has_side_effects=DATAFLOW),
    )(*[pltpu.with_memory_space_constraint(a, pltpu.HBM) for a in srcs],
      *[pltpu.with_memory_space_constraint(a, pltpu.HBM) for a in lands], dep)
    return res[0], res[1], list(res[2:2 + n]), list(res[2 + n:2 + 2 * n]), res[-1]


def _exchange_wait(name, started, whole, after):
    send_sems, recv_sems, srcs, lands, _ = started
    n = len(srcs)

    def body(*refs):
        src_refs, land_refs = refs[:n], refs[n:2 * n]
        send_sems, recv_sems = refs[2 * n], refs[2 * n + 1]
        _, peers = _peers()
        for a in range(n):
            for k, peer in enumerate(peers):
                cp = _exchange_copy(src_refs[a], land_refs[a], whole[a], send_sems, recv_sems, a, k, peer,
                                    _dev_index(peer))
                cp.wait_send()
                cp.wait_recv()

    res = pl.pallas_call(
        body, name=name,
        out_shape=[pltpu.HBM(a.shape, a.dtype) for a in srcs + lands],
        in_specs=[HBM_SPEC] * (2 * n) + [SEM_SPEC, SEM_SPEC, ANY_SPEC],
        out_specs=[HBM_SPEC] * (2 * n),
        input_output_aliases={i: i for i in range(2 * n)},
        compiler_params=pltpu.CompilerParams(has_side_effects=DATAFLOW),
    )(*srcs, *lands, send_sems, recv_sems, after)
    return list(res[n:])


def _views(a, dil):
    s = a.shape[0]
    return a.reshape(s // dil, dil * a.shape[1])


def _local_step(x, mem, target, rel_bias, g_mix, w_in_g, w_sc, g_a, g_c, g_xattn, g_mem, g_ffn, w_fc, b_fc, g_final,
                dep, late_weights, emit, emit_small):
    s = x.shape[0]
    buckets = _bucket_tables()
    bias = _bias_fwd(rel_bias, buckets)

    h1, q, k, v, gb, gc, xi = _rms_proj(x, g_mix, w_in_g, dep)
    branches = []
    for p, dil in enumerate(DILATIONS):
        outs = _swa_fwd(_views(q, dil), _views(k, dil), _views(v, dil), bias[p], dil)
        branches.append([o.reshape(s, ATTN_W) for o in outs])
    lw = late_weights(branches[-1][0])
    w_out, w_xq, w_xk, w_xv, w_xo, w_up_g, w_down_g = (lw[n] for n in ("w_out", "w_xq", "w_xk", "w_xv", "w_xo",
                                                                       "w_up", "w_down"))
    attn, lse, mixed, x1 = _mix_out(branches, gb, gc, xi, x, w_sc, g_a, g_c, w_out)
    mem_n, mk, mv = _mem_kv(mem, g_mem, w_xk, w_xv)
    h2, xq, xo, x2 = _xattn_fwd(x1, g_xattn, w_xq, mk, mv, w_xo)
    h3, up = _ffn_up(x2, g_ffn, w_up_g)
    act, dx3, loss_acc, dg_final = _ffn_tail(up, w_fc, b_fc, w_down_g, x2, g_final, target)

    dc, dw_fc, db_fc = _ffn_down_bwd(dx3, up, w_fc, b_fc, w_down_g)
    gw_down = _dw(act, dx3, "dw_down", a_chunked=True)
    dup, dx2, dg_ffn = _ffn_up_bwd(dc, w_fc, w_up_g, x2, g_ffn, dx3)
    gw_up = _dw(h3, dup, "dw_up", b_chunked=True)
    tok = emit(dict(w_down=gw_down, w_up=gw_up))
    dxq, dx1, dmk, dmv, dg_xattn = _xattn_bwd(dx2, xo, xq, mk, mv, w_xo, w_xq, x1, g_xattn, tok)
    gw_xo = _dw(xo, dx2, "dw_xo")[0]
    gw_xq = _dw(h2, dxq, "dw_xq")[0]
    gw_xk, gw_xv, dg_mem = _mem_kv_bwd(dmk, dmv, mem_n, mem, w_xk, w_xv)
    tok = emit(dict(w_xo=gw_xo, w_xq=gw_xq, w_xk=gw_xk, w_xv=gw_xv))
    dattn, dd, dgb, dcv, dg_a, dg_c, dw_sc = _mix_out_bwd(dx1, w_out, attn, gb, gc, xi, w_sc, g_a, g_c, tok)
    gw_out = _dw(mixed, dx1, "dw_out")[0]
    tok = emit(dict(w_out=gw_out))
    dqs, dks, dvs, dbias = [], [], [], []
    for p, dil in enumerate(DILATIONS):
        dq_p, dk_p, dv_p, db_p = _swa_bwd(_views(q, dil), _views(k, dil), _views(v, dil), _views(dattn, dil),
                                          _views(lse, dil), _views(dd, dil), bias[p], dil, tok)
        dqs.append(dq_p.reshape(s, ATTN_W))
        dks.append(dk_p.reshape(s, ATTN_W))
        dvs.append(dv_p.reshape(s, ATTN_W))
        dbias.append(db_p)
    d_relb = _bias_bwd(jnp.stack(dbias), buckets)
    dproj, grad_x, dg_mix = _in_proj_bwd(dqs, dks, dvs, dgb, dcv, gc, xi, w_sc, w_in_g, x, g_mix, dx1)
    pad = lambda a: jnp.pad(a, ((0, 0), (0, D_MODEL - a.shape[1])))
    small = jnp.concatenate([
        d_relb, dg_mix, dg_xattn, dg_mem, dg_ffn, dg_final, jnp.concatenate([dg_a, dg_c], axis=1),
        pad(dw_sc), pad(db_fc), pad(dw_fc.reshape(3 * N_DEV, UP_CHUNK))], axis=0)
    emit_small(small)
    gw_in = _dw(h1, dproj, "dw_in", n_chunks=N_DEV, chunk_cols=IN_CHUNK)
    emit(dict(w_in=gw_in))
    return loss_acc[0, 0], grad_x


def kernel(x, mem, rel_bias, g_mix, w_in, w_short_conv, g_attn_out, g_conv_out, w_out, g_xattn, g_mem, w_xq, w_xk, w_xv, w_xo, g_ffn, w_up, w_ffn_conv, b_ffn_conv, w_down, g_final, loss_target, m_rel_bias, m_g_mix, m_w_in, m_w_short_conv, m_g_attn_out, m_g_conv_out, m_w_out, m_g_xattn, m_g_mem, m_w_xq, m_w_xk, m_w_xv, m_w_xo, m_g_ffn, m_w_up, m_w_ffn_conv, m_b_ffn_conv, m_w_down, m_g_final, v_rel_bias, v_g_mix, v_w_in, v_w_short_conv, v_g_attn_out, v_g_conv_out, v_w_out, v_g_xattn, v_g_mem, v_w_xq, v_w_xk, v_w_xv, v_w_xo, v_g_ffn, v_w_up, v_w_ffn_conv, v_b_ffn_conv, v_w_down, v_g_final):
    me = _dev_index(_mesh_pos())
    me_arr = me.reshape(1).astype(jnp.int32)

    big_names = ["w_in", "w_out", "w_xq", "w_xk", "w_xv", "w_xo", "w_up", "w_down"]
    late_names = big_names[1:]
    big_w = dict(w_in=w_in[0], w_out=w_out[0], w_xq=w_xq[0], w_xk=w_xk[0], w_xv=w_xv[0], w_xo=w_xo[0],
                 w_up=w_up[0], w_down=w_down[0])
    big_m = dict(w_in=m_w_in[0], w_out=m_w_out[0], w_xq=m_w_xq[0], w_xk=m_w_xk[0], w_xv=m_w_xv[0], w_xo=m_w_xo[0],
                 w_up=m_w_up[0], w_down=m_w_down[0])
    big_v = dict(w_in=v_w_in[0], w_out=v_w_out[0], w_xq=v_w_xq[0], w_xk=v_w_xk[0], w_xv=v_w_xv[0], w_xo=v_w_xo[0],
                 w_up=v_w_up[0], w_down=v_w_down[0])
    shard_shape = {n: big_w[n].shape for n in big_names}

    w_in_g, w_sc_g, w_fc_full = _all_gather([big_w["w_in"].astype(BF16), w_short_conv[0], w_ffn_conv[0]])
    w_sc_full = w_sc_g.transpose(1, 0, 2).reshape(3, CONV_W)
    late_shards = [big_w[n].astype(BF16) for n in late_names]
    ag = _exchange_start("gather_weights_start", late_shards, [True] * len(late_names), w_in_g)

    def late_weights(after):
        lands = _exchange_wait("gather_weights_wait", ag, [True] * len(late_names), after)
        full = {n: lax.dynamic_update_index_in_dim(land, shard, me, 0)
                for n, land, shard in zip(late_names, lands, late_shards)}
        out = {n: full[n].reshape(D_MODEL, D_MODEL) for n in ("w_out", "w_xq", "w_xk", "w_xv", "w_xo")}
        out["w_up"] = full["w_up"]
        out["w_down"] = full["w_down"].reshape(N_DEV // 2, UP_CHUNK, D_MODEL)
        return out

    sent = []

    def emit(grads):
        names = list(grads)
        blocks = [grads[n].reshape((N_DEV,) + shard_shape[n]) for n in names]
        own = [lax.dynamic_index_in_dim(b, me, 0, keepdims=False) for b in blocks]
        started = _exchange_start("scatter_" + "_".join(names) + "_start", blocks, [False] * len(names), me_arr)
        sent.append((names, own, started))
        return started[-1]

    def emit_small(small):
        sent_small.append((small, _exchange_start("gather_small_start", [small], [True], me_arr)))

    sent_small = []
    loss_part, grad_x = _local_step(
        x[0], mem[0], loss_target[0], rel_bias, g_mix, w_in_g, w_sc_full, g_attn_out, g_conv_out, g_xattn, g_mem,
        g_ffn, w_fc_full, b_ffn_conv.reshape(N_DEV, 1, UP_CHUNK), g_final.reshape(1, D_MODEL), ag[-1],
        late_weights, emit, emit_small)
    loss = lax.psum(loss_part, ("x", "y", "c"))

    small_g, small_started = sent_small[0]
    after = sent[-1][2][-1]
    small_parts = _exchange_wait("gather_small_wait", small_started, [True], after)[0]
    big_out = {}
    after = small_parts
    for names, own, started in sent:
        lands = _exchange_wait("scatter_" + "_".join(names) + "_wait", started, [False] * len(names), after)
        for n, own_n, land in zip(names, own, lands):
            res = _adamw_big("adamw_" + n, big_w[n], own_n, land, big_m[n], big_v[n], me_arr)
            big_out[n] = [r[None] for r in res]
            after = res[0]

    as_rows = lambda a: a.reshape(N_DEV, UP_CHUNK)
    row1 = lambda a: a.reshape(1, D_MODEL)
    small_names = ["rel_bias", "g_mix", "g_attn_out", "g_conv_out", "g_xattn", "g_mem", "g_ffn", "b_ffn_conv", "g_final"]
    wmv = [
        (rel_bias, m_rel_bias, v_rel_bias), (g_mix, m_g_mix, v_g_mix), (g_attn_out, m_g_attn_out, v_g_attn_out),
        (g_conv_out, m_g_conv_out, v_g_conv_out), (g_xattn, m_g_xattn, v_g_xattn), (g_mem, m_g_mem, v_g_mem),
        (g_ffn, m_g_ffn, v_g_ffn), (as_rows(b_ffn_conv), as_rows(m_b_ffn_conv), as_rows(v_b_ffn_conv)),
        (row1(g_final), row1(m_g_final), row1(v_g_final))]
    g_packed, small_res = _adamw_small(small_g, small_parts, wmv, me_arr)
    small_out = dict(zip(small_names, small_res))
    small_out["b_ffn_conv"] = [a.reshape(1, 2 * D_FF) for a in small_out["b_ffn_conv"]]
    small_out["g_final"] = [a.reshape(D_MODEL) for a in small_out["g_final"]]

    g_wsc = lax.dynamic_slice(g_packed[ROW_WSC:ROW_WSC + 3, 0:CONV_W], (0, me * HEAD_DIM), (3, HEAD_DIM))
    g_wfc = lax.dynamic_slice(g_packed[ROW_WFC:ROW_WFC + 3 * N_DEV, 0:UP_CHUNK].reshape(3, N_DEV, UP_CHUNK),
                              (0, me, 0), (3, 1, UP_CHUNK)).reshape(3, UP_CHUNK)
    shard_res = _adamw_shards([(w_short_conv[0], g_wsc, m_w_short_conv[0], v_w_short_conv[0]),
                               (w_ffn_conv[0], g_wfc, m_w_ffn_conv[0], v_w_ffn_conv[0])])
    small_out["w_short_conv"] = [g_wsc[None]] + [a[None] for a in shard_res[0]]
    small_out["w_ffn_conv"] = [g_wfc[None]] + [a[None] for a in shard_res[1]]

    order = ["rel_bias", "g_mix", "w_in", "w_short_conv", "g_attn_out", "g_conv_out", "w_out", "g_xattn", "g_mem",
             "w_xq", "w_xk", "w_xv", "w_xo", "g_ffn", "w_up", "w_ffn_conv", "b_ffn_conv", "w_down", "g_final"]
    allp = {**big_out, **small_out}
    outs = [loss, grad_x[None]]
    for kind in range(4):
        outs += [allp[n][kind] for n in order]
    return tuple(outs)
```

```python
import functools
import math

import numpy as np
import jax
import jax.numpy as jnp
from jax import lax
from jax.experimental import pallas as pl
from jax.experimental.pallas import tpu as pltpu

F32 = jnp.float32
BF16 = jnp.bfloat16
MESH = pl.DeviceIdType.MESH

N_DEV = 8
D_MODEL = 1024
ATTN_W = 512
CONV_W = 512
N_HEADS = 8
HEAD_DIM = 64
WIN = 128
DILATIONS = (1, 4, 16)
N_BUCKETS = 32
BUCKET_MAX_EXACT = 16
BUCKET_MAX_DISTANCE = 2048
N_MEM_HEADS = 4
MEM_HEAD_DIM = 256
D_FF = 2816
IN_COLS = 3072
IN_CHUNK = IN_COLS // N_DEV
UP_CHUNK = 2 * D_FF // N_DEV
EPS = 1e-6

ADAM_LR = 0.001
ADAM_B1 = 0.9
ADAM_B2 = 0.999
ADAM_EPS = 1e-08
ADAM_WD = 0.01
ADAM_STEP = 10

SUBLANES = 8
TM = 512
TM_FFN = 256
VMEM_LIMIT = 56 * 1024 * 1024

ROW_RELB, ROW_GMIX, ROW_GXATTN, ROW_GMEM, ROW_GFFN, ROW_GFINAL, ROW_GAC = 0, 8, 16, 24, 32, 40, 48
ROW_WSC, ROW_BFC, ROW_WFC, SMALL_ROWS = 56, 64, 72, 96


def _cparams(n_grid):
    return pltpu.CompilerParams(dimension_semantics=("arbitrary",) * n_grid, vmem_limit_bytes=VMEM_LIMIT)


def _full(shape):
    nd = len(shape)
    return pl.BlockSpec(tuple(shape), lambda *_: (0,) * nd)


ANY_SPEC = pl.BlockSpec(memory_space=pl.ANY)
HBM_SPEC = pl.BlockSpec(memory_space=pltpu.HBM)
SEM_SPEC = pl.BlockSpec(memory_space=pltpu.SEMAPHORE)
VMEM_SPEC = pl.BlockSpec(memory_space=pltpu.VMEM)
SMEM_SPEC = pl.BlockSpec(memory_space=pltpu.SMEM)
DATAFLOW = pltpu.SideEffectType.DATAFLOW_SIDE_EFFECTING


def _rms(x):
    r = lax.rsqrt(jnp.mean(x * x, axis=-1, keepdims=True) + EPS)
    return x * r, r


def _rms_bwd(xh, r, g, dy):
    dxh = dy * g
    return r * (dxh - xh * jnp.mean(dxh * xh, axis=-1, keepdims=True))


def _shift_down(u, halo, k):
    ru = pltpu.roll(u, k, 0)
    rh = pltpu.roll(halo, k, 0)
    row = lax.broadcasted_iota(jnp.int32, rh.shape, 0)
    head = jnp.where(row < k, rh, ru[0:SUBLANES])
    return jnp.concatenate([head, ru[SUBLANES:]], axis=0)


def _shift_up(u, halo, k):
    tm = u.shape[0]
    ru = pltpu.roll(u, tm - k, 0)
    rh = pltpu.roll(halo, SUBLANES - k, 0)
    row = lax.broadcasted_iota(jnp.int32, rh.shape, 0)
    tail = jnp.where(row >= SUBLANES - k, rh, ru[tm - SUBLANES:])
    return jnp.concatenate([ru[:tm - SUBLANES], tail], axis=0)


def _causal_conv3(u, halo, w_ref):
    return (_shift_down(u, halo, 2) * w_ref[0:1, :] + _shift_down(u, halo, 1) * w_ref[1:2, :]) + u * w_ref[2:3, :]


def _dot(a, b):
    return jnp.dot(a, b, preferred_element_type=F32)


def _dot_nt(a, b):
    return lax.dot_general(a, b, (((1,), (1,)), ((), ())), preferred_element_type=F32)


def _dot_tn(a, b):
    return lax.dot_general(a, b, (((0,), (0,)), ((), ())), preferred_element_type=F32)


def _sigmoid(x):
    return 1.0 / (1.0 + jnp.exp(-x))


def _bucket_tables():
    qi = np.arange(WIN)[:, None]
    kj = np.arange(2 * WIN)[None, :]
    steps = np.clip(qi + WIN - kj, 0, WIN)
    out = []
    for d in DILATIONS:
        dist = steps * d
        dd = np.maximum(dist, 1).astype(np.float32)
        large = BUCKET_MAX_EXACT + (
            np.log(dd / np.float32(BUCKET_MAX_EXACT)) / np.float32(math.log(BUCKET_MAX_DISTANCE / BUCKET_MAX_EXACT))
            * np.float32(N_BUCKETS - BUCKET_MAX_EXACT)).astype(np.int32)
        large = np.minimum(large, N_BUCKETS - 1)
        out.append(np.where(dist < BUCKET_MAX_EXACT, dist, large).astype(np.int32))
    return jnp.asarray(np.stack(out))


def _bias_fwd(rel_bias, buckets):
    def body(rb_ref, bk_ref, o_ref):
        for p in range(3):
            bk = bk_ref[p]
            for h in range(N_HEADS):
                acc = jnp.zeros((WIN, 2 * WIN), F32)
                for b in range(N_BUCKETS):
                    acc = jnp.where(bk == b, rb_ref[h, b], acc)
                o_ref[p, h] = acc

    return pl.pallas_call(
        body, name="bias_fwd",
        out_shape=jax.ShapeDtypeStruct((3, N_HEADS, WIN, 2 * WIN), F32),
        in_specs=[pl.BlockSpec(memory_space=pltpu.SMEM), pl.BlockSpec(memory_space=pltpu.VMEM)],
        out_specs=pl.BlockSpec(memory_space=pltpu.VMEM),
    )(rel_bias, buckets)


def _bias_bwd(dbias, buckets):
    def body(db_ref, bk_ref, o_ref):
        lane = lax.broadcasted_iota(jnp.int32, (1, D_MODEL), 1)
        rows = []
        for h in range(N_HEADS):
            row = jnp.zeros((1, D_MODEL), F32)
            for b in range(N_BUCKETS):
                tot = jnp.zeros((1, 1), F32)
                for p in range(3):
                    sel = jnp.where(bk_ref[p] == b, db_ref[p, h], 0.0)
                    tot = tot + jnp.sum(jnp.sum(sel, axis=0, keepdims=True), axis=1, keepdims=True)
                row = jnp.where(lane == b, tot, row)
            rows.append(row)
        o_ref[...] = jnp.concatenate(rows, axis=0)

    return pl.pallas_call(
        body, name="bias_bwd",
        out_shape=jax.ShapeDtypeStruct((N_HEADS, D_MODEL), F32),
        in_specs=[pl.BlockSpec(memory_space=pltpu.VMEM), pl.BlockSpec(memory_space=pltpu.VMEM)],
        out_specs=pl.BlockSpec(memory_space=pltpu.VMEM),
    )(dbias, buckets)


def _rms_proj(x, g_mix, w_in_g, dep):
    s = x.shape[0]

    def body(x_ref, g_ref, w_ref, dep_ref, h_ref, q_ref, k_ref, v_ref, gb_ref, gc_ref, xi_ref):
        xh, _ = _rms(x_ref[...])
        h = (xh * g_ref[...]).astype(BF16)
        h_ref[...] = h
        proj = jnp.concatenate([_dot(h, w_ref[j]) for j in range(N_DEV)], axis=1)
        q_ref[...] = (proj[:, 0:512] * (HEAD_DIM ** -0.5)).astype(BF16)
        k_ref[...] = proj[:, 512:1024].astype(BF16)
        v_ref[...] = proj[:, 1024:1536].astype(BF16)
        gb_ref[...] = proj[:, 1536:2048]
        gc_ref[...] = proj[:, 2048:2560]
        xi_ref[...] = proj[:, 2560:3072]

    row = lambda n: pl.BlockSpec((TM, n), lambda i: (i, 0))
    return pl.pallas_call(
        body, name="rms_proj", grid=(s // TM,),
        out_shape=[jax.ShapeDtypeStruct((s, D_MODEL), BF16)] + [jax.ShapeDtypeStruct((s, 512), BF16)] * 3
        + [jax.ShapeDtypeStruct((s, 512), F32)] * 3,
        in_specs=[row(D_MODEL), _full(g_mix.shape), _full(w_in_g.shape), ANY_SPEC],
        out_specs=[row(D_MODEL)] + [row(512)] * 6,
        compiler_params=_cparams(1),
    )(x, g_mix, w_in_g, dep)


def _band_mask(blk):
    qi = lax.broadcasted_iota(jnp.int32, (WIN, 2 * WIN), 0)
    kj = lax.broadcasted_iota(jnp.int32, (WIN, 2 * WIN), 1)
    steps = qi + WIN - kj
    return (steps >= 0) & (steps <= WIN) & (kj >= jnp.where(blk > 0, 0, WIN))


def _swa_fwd(qv, kv, vv, bias, dil):
    rows = qv.shape[0]
    nb = rows // WIN

    def body(q_ref, kp_ref, kc_ref, vp_ref, vc_ref, b_ref, num_ref, m_ref, s_ref):
        valid = _band_mask(pl.program_id(1))
        for h in range(N_HEADS):
            sl = slice(h * HEAD_DIM, (h + 1) * HEAD_DIM)
            kh = jnp.concatenate([kp_ref[:, sl], kc_ref[:, sl]], axis=0)
            vh = jnp.concatenate([vp_ref[:, sl], vc_ref[:, sl]], axis=0)
            lg = _dot_nt(q_ref[:, sl], kh) + b_ref[h]
            lg = jnp.where(valid, lg, -jnp.inf)
            m = jnp.max(lg, axis=-1, keepdims=True)
            p = jnp.exp(lg - m)
            num_ref[:, sl] = _dot(p.astype(BF16), vh)
            m_ref[:, sl] = jnp.broadcast_to(m, (WIN, HEAD_DIM))
            s_ref[:, sl] = jnp.broadcast_to(jnp.sum(p, axis=-1, keepdims=True), (WIN, HEAD_DIM))

    cur = pl.BlockSpec((WIN, 512), lambda r, b: (b, r))
    prev = pl.BlockSpec((WIN, 512), lambda r, b: (jnp.maximum(b - 1, 0), r))
    return pl.pallas_call(
        body, name=f"swa_fwd_d{dil}", grid=(dil, nb),
        out_shape=[jax.ShapeDtypeStruct(qv.shape, F32)] * 3,
        in_specs=[cur, prev, cur, prev, cur, _full(bias.shape)],
        out_specs=[cur] * 3,
        compiler_params=_cparams(2),
    )(qv, kv, kv, vv, vv, bias)


def _mix_out(branches, gb, gc, xi, x, w_sc, g_a, g_c, w_out):
    s = x.shape[0]
    tb = TM // SUBLANES

    def body(n1, m1, s1, n2, m2, s2, n3, m3, s3, gb_ref, gc_ref, xi_ref, gch_ref, xih_ref, x_ref, wsc_ref,
             ga_ref, gcv_ref, wout_ref, attn_ref, lse_ref, mixed_ref, x1_ref):
        i = pl.program_id(0)
        m_all = jnp.maximum(jnp.maximum(m1[...], m2[...]), m3[...])
        e1, e2, e3 = jnp.exp(m1[...] - m_all), jnp.exp(m2[...] - m_all), jnp.exp(m3[...] - m_all)
        den = (e1 * s1[...] + e2 * s2[...]) + e3 * s3[...]
        num = (e1 * n1[...] + e2 * n2[...]) + e3 * n3[...]
        attn = num / den
        attn_ref[...] = attn
        lse_ref[...] = m_all + jnp.log(den)
        xa, _ = _rms(attn)
        u = gc_ref[...] * xi_ref[...]
        uh = jnp.where(i > 0, gch_ref[...] * xih_ref[...], 0.0)
        conv = gb_ref[...] * _causal_conv3(u, uh, wsc_ref)
        xc, _ = _rms(conv)
        mixed = jnp.concatenate([xa * ga_ref[...], xc * gcv_ref[...]], axis=1).astype(BF16)
        mixed_ref[...] = mixed
        x1_ref[...] = x_ref[...] + _dot(mixed, wout_ref[...])

    row = lambda n: pl.BlockSpec((TM, n), lambda i: (i, 0))
    halo = pl.BlockSpec((SUBLANES, 512), lambda i: (jnp.maximum(i * tb - 1, 0), 0))
    flat = [a for br in branches for a in br]
    return pl.pallas_call(
        body, name="mix_out", grid=(s // TM,),
        out_shape=[jax.ShapeDtypeStruct((s, 512), F32)] * 2
        + [jax.ShapeDtypeStruct((s, D_MODEL), BF16), jax.ShapeDtypeStruct((s, D_MODEL), F32)],
        in_specs=[row(512)] * 12 + [halo, halo, row(D_MODEL), _full(w_sc.shape), _full(g_a.shape),
                                    _full(g_c.shape), _full(w_out.shape)],
        out_specs=[row(512), row(512), row(D_MODEL), row(D_MODEL)],
        compiler_params=_cparams(1),
    )(*flat, gb, gc, xi, gc, xi, x, w_sc, g_a, g_c, w_out)


def _mem_kv(mem, g_mem, w_xk, w_xv):
    def body(mem_ref, g_ref, wk_ref, wv_ref, mn_ref, k_ref, v_ref):
        xh, _ = _rms(mem_ref[...])
        mn = (xh * g_ref[...]).astype(BF16)
        mn_ref[...] = mn
        k_ref[...] = _dot(mn, wk_ref[...]).astype(BF16)
        v_ref[...] = _dot(mn, wv_ref[...]).astype(BF16)

    vm = pl.BlockSpec(memory_space=pltpu.VMEM)
    return pl.pallas_call(
        body, name="mem_kv",
        out_shape=[jax.ShapeDtypeStruct(mem.shape, BF16)] * 3,
        in_specs=[vm] * 4, out_specs=[vm] * 3,
        compiler_params=pltpu.CompilerParams(vmem_limit_bytes=VMEM_LIMIT),
    )(mem, g_mem, w_xk, w_xv)


def _xattn_fwd(x1, g, w_xq, k, v, w_xo):
    s = x1.shape[0]

    def body(x1_ref, g_ref, wq_ref, k_ref, v_ref, wo_ref, h2_ref, q_ref, o_ref, x2_ref):
        x1v = x1_ref[...]
        xh, _ = _rms(x1v)
        h2 = (xh * g_ref[...]).astype(BF16)
        h2_ref[...] = h2
        qb = _dot(h2, wq_ref[...]).astype(BF16)
        q_ref[...] = qb
        outs = []
        for h in range(N_MEM_HEADS):
            sl = slice(h * MEM_HEAD_DIM, (h + 1) * MEM_HEAD_DIM)
            lg = _dot_nt(qb[:, sl], k_ref[:, sl]) * (MEM_HEAD_DIM ** -0.5)
            p = jnp.exp(lg - jnp.max(lg, axis=-1, keepdims=True))
            p = p / jnp.sum(p, axis=-1, keepdims=True)
            outs.append(_dot(p.astype(BF16), v_ref[:, sl]))
        o = jnp.concatenate(outs, axis=1).astype(BF16)
        o_ref[...] = o
        x2_ref[...] = x1v + _dot(o, wo_ref[...])

    row = pl.BlockSpec((TM, D_MODEL), lambda i: (i, 0))
    return pl.pallas_call(
        body, name="xattn_fwd", grid=(s // TM,),
        out_shape=[jax.ShapeDtypeStruct((s, D_MODEL), BF16)] * 3 + [jax.ShapeDtypeStruct((s, D_MODEL), F32)],
        in_specs=[row, _full(g.shape), _full(w_xq.shape), _full(k.shape), _full(v.shape), _full(w_xo.shape)],
        out_specs=[row] * 4,
        compiler_params=_cparams(1),
    )(x1, g, w_xq, k, v, w_xo)


def _ffn_up(x2, g, w_up_g):
    s = x2.shape[0]

    def body(x_ref, g_ref, w_ref, h_ref, up_ref, h_scr):
        @pl.when(pl.program_id(1) == 0)
        def _():
            xh, _ = _rms(x_ref[...])
            h = (xh * g_ref[...]).astype(BF16)
            h_scr[...] = h
            h_ref[...] = h

        up_ref[0] = _dot(h_scr[...], w_ref[0])

    return pl.pallas_call(
        body, name="ffn_up", grid=(s // TM, N_DEV),
        out_shape=[jax.ShapeDtypeStruct((s, D_MODEL), BF16), jax.ShapeDtypeStruct((N_DEV, s, UP_CHUNK), F32)],
        in_specs=[pl.BlockSpec((TM, D_MODEL), lambda i, j: (i, 0)), _full(g.shape),
                  pl.BlockSpec((1, D_MODEL, UP_CHUNK), lambda i, j: (j, 0, 0))],
        out_specs=[pl.BlockSpec((TM, D_MODEL), lambda i, j: (i, 0)),
                   pl.BlockSpec((1, TM, UP_CHUNK), lambda i, j: (j, i, 0))],
        scratch_shapes=[pltpu.VMEM((TM, D_MODEL), BF16)],
        compiler_params=_cparams(2),
    )(x2, g, w_up_g)


def _ffn_conv(up_ref, uph_ref, wfc_ref, bfc_ref, i, j):
    u = up_ref[j]
    uh = jnp.where(i > 0, uph_ref[j], 0.0)
    u2 = _shift_down(u, uh, 2)
    u1 = _shift_down(u, uh, 1)
    w = wfc_ref[j]
    c = ((u2 * w[0:1, :] + u1 * w[1:2, :]) + u * w[2:3, :]) + bfc_ref[j]
    return c, u2, u1, u


def _ffn_tail(up, w_fc, b_fc, w_down_g, x2, g_final, target):
    s = x2.shape[0]
    tb = TM_FFN // SUBLANES
    half = N_DEV // 2

    def body(up_ref, uph_ref, wfc_ref, bfc_ref, wd_ref, x2_ref, gf_ref, t_ref, act_ref, dx3_ref, loss_ref, dgf_ref):
        i = pl.program_id(0)

        @pl.when(i == 0)
        def _():
            loss_ref[...] = jnp.zeros_like(loss_ref)
            dgf_ref[...] = jnp.zeros_like(dgf_ref)

        down = jnp.zeros((TM_FFN, D_MODEL), F32)
        for j in range(half):
            cg = _ffn_conv(up_ref, uph_ref, wfc_ref, bfc_ref, i, j)[0]
            cv = _ffn_conv(up_ref, uph_ref, wfc_ref, bfc_ref, i, j + half)[0]
            a = ((cg * _sigmoid(cg)) * cv).astype(BF16)
            act_ref[j] = a
            down = down + _dot(a, wd_ref[j])
        x3 = x2_ref[...] + down
        xh, r = _rms(x3)
        gf = gf_ref[...]
        e = xh * gf - t_ref[...]
        loss_ref[...] += 0.5 * jnp.sum(jnp.sum(e * e, axis=1, keepdims=True), axis=0, keepdims=True) / D_MODEL
        dy = e * (1.0 / D_MODEL)
        dgf_ref[0:1, :] += jnp.sum(dy * xh, axis=0, keepdims=True)
        dx3_ref[...] = _rms_bwd(xh, r, gf, dy)

    row = pl.BlockSpec((TM_FFN, D_MODEL), lambda i: (i, 0))
    cur = pl.BlockSpec((N_DEV, TM_FFN, UP_CHUNK), lambda i: (0, i, 0))
    halo = pl.BlockSpec((N_DEV, SUBLANES, UP_CHUNK), lambda i: (0, jnp.maximum(i * tb - 1, 0), 0))
    return pl.pallas_call(
        body, name="ffn_tail", grid=(s // TM_FFN,),
        out_shape=[jax.ShapeDtypeStruct((half, s, UP_CHUNK), BF16), jax.ShapeDtypeStruct((s, D_MODEL), F32),
                   jax.ShapeDtypeStruct((SUBLANES, 128), F32), jax.ShapeDtypeStruct((SUBLANES, D_MODEL), F32)],
        in_specs=[cur, halo, _full(w_fc.shape), _full(b_fc.shape), _full(w_down_g.shape), row,
                  _full(g_final.shape), row],
        out_specs=[pl.BlockSpec((half, TM_FFN, UP_CHUNK), lambda i: (0, i, 0)), row,
                   _full((SUBLANES, 128)), _full((SUBLANES, D_MODEL))],
        compiler_params=_cparams(1),
    )(up, up, w_fc, b_fc, w_down_g, x2, g_final, target)


def _ffn_down_bwd(dx3, up, w_fc, b_fc, w_down_g):
    s = dx3.shape[0]
    tb = TM_FFN // SUBLANES
    half = N_DEV // 2

    def body(dx3_ref, up_ref, uph_ref, wfc_ref, bfc_ref, wd_ref, dc_ref, dwfc_ref, dbfc_ref):
        i = pl.program_id(0)

        @pl.when(i == 0)
        def _():
            dwfc_ref[...] = jnp.zeros_like(dwfc_ref)
            dbfc_ref[...] = jnp.zeros_like(dbfc_ref)

        dxb = dx3_ref[...].astype(BF16)

        def small_grads(j, dc, u2, u1, u):
            dc_ref[j] = dc
            dbfc_ref[j:j + 1, :] += jnp.sum(dc, axis=0, keepdims=True)
            dwfc_ref[0, j:j + 1, :] += jnp.sum(dc * u2, axis=0, keepdims=True)
            dwfc_ref[1, j:j + 1, :] += jnp.sum(dc * u1, axis=0, keepdims=True)
            dwfc_ref[2, j:j + 1, :] += jnp.sum(dc * u, axis=0, keepdims=True)

        for j in range(half):
            dact = _dot_nt(dxb, wd_ref[j])
            cg, g2, g1, g0 = _ffn_conv(up_ref, uph_ref, wfc_ref, bfc_ref, i, j)
            cv, v2, v1, v0 = _ffn_conv(up_ref, uph_ref, wfc_ref, bfc_ref, i, j + half)
            sg = _sigmoid(cg)
            small_grads(j + half, dact * (cg * sg), v2, v1, v0)
            small_grads(j, (dact * cv) * (sg * (1.0 + cg * (1.0 - sg))), g2, g1, g0)

    row = pl.BlockSpec((TM_FFN, D_MODEL), lambda i: (i, 0))
    cur = pl.BlockSpec((N_DEV, TM_FFN, UP_CHUNK), lambda i: (0, i, 0))
    halo = pl.BlockSpec((N_DEV, SUBLANES, UP_CHUNK), lambda i: (0, jnp.maximum(i * tb - 1, 0), 0))
    return pl.pallas_call(
        body, name="ffn_down_bwd", grid=(s // TM_FFN,),
        out_shape=[jax.ShapeDtypeStruct((N_DEV, s, UP_CHUNK), F32), jax.ShapeDtypeStruct((3, N_DEV, UP_CHUNK), F32),
                   jax.ShapeDtypeStruct((N_DEV, UP_CHUNK), F32)],
        in_specs=[row, cur, halo, _full(w_fc.shape), _full(b_fc.shape), _full(w_down_g.shape)],
        out_specs=[cur, _full((3, N_DEV, UP_CHUNK)), _full((N_DEV, UP_CHUNK))],
        compiler_params=_cparams(1),
    )(dx3, up, up, w_fc, b_fc, w_down_g)


def _ffn_up_bwd(dc, w_fc, w_up_g, x2, g, dx3):
    s = x2.shape[0]
    tb = TM_FFN // SUBLANES
    last = s // SUBLANES - 1
    n_tiles = s // TM_FFN

    def body(dc_ref, dch_ref, wfc_ref, wup_ref, x2_ref, g_ref, dx3_ref, dup_ref, dx2_ref, dg_ref):
        i = pl.program_id(0)

        @pl.when(i == 0)
        def _():
            dg_ref[...] = jnp.zeros_like(dg_ref)

        dh = jnp.zeros((TM_FFN, D_MODEL), F32)
        for j in range(N_DEV):
            d0 = dc_ref[j]
            dn = jnp.where(i < n_tiles - 1, dch_ref[j], 0.0)
            w = wfc_ref[j]
            du = ((d0 * w[2:3, :] + _shift_up(d0, dn, 1) * w[1:2, :]) + _shift_up(d0, dn, 2) * w[0:1, :]).astype(BF16)
            dup_ref[j] = du
            dh = dh + _dot_nt(du, wup_ref[j])
        xh, r = _rms(x2_ref[...])
        dg_ref[0:1, :] += jnp.sum(dh * xh, axis=0, keepdims=True)
        dx2_ref[...] = dx3_ref[...] + _rms_bwd(xh, r, g_ref[...], dh)

    row = pl.BlockSpec((TM_FFN, D_MODEL), lambda i: (i, 0))
    cur = pl.BlockSpec((N_DEV, TM_FFN, UP_CHUNK), lambda i: (0, i, 0))
    nxt = pl.BlockSpec((N_DEV, SUBLANES, UP_CHUNK), lambda i: (0, jnp.minimum((i + 1) * tb, last), 0))
    return pl.pallas_call(
        body, name="ffn_up_bwd", grid=(n_tiles,),
        out_shape=[jax.ShapeDtypeStruct((N_DEV, s, UP_CHUNK), BF16), jax.ShapeDtypeStruct((s, D_MODEL), F32),
                   jax.ShapeDtypeStruct((SUBLANES, D_MODEL), F32)],
        in_specs=[cur, nxt, _full(w_fc.shape), _full(w_up_g.shape), row, _full(g.shape), row],
        out_specs=[cur, row, _full((SUBLANES, D_MODEL))],
        compiler_params=_cparams(1),
    )(dc, dc, w_fc, w_up_g, x2, g, dx3)


def _xattn_bwd(dx2, o, q, k, v, w_xo, w_xq, x1, g, dep):
    s = x1.shape[0]

    def body(dx2_ref, o_ref, q_ref, k_ref, v_ref, wo_ref, wq_ref, x1_ref, g_ref, dep_ref, dq_ref, dx1_ref, dk_ref,
             dv_ref, dg_ref):
        @pl.when(pl.program_id(0) == 0)
        def _():
            dk_ref[...] = jnp.zeros_like(dk_ref)
            dv_ref[...] = jnp.zeros_like(dv_ref)
            dg_ref[...] = jnp.zeros_like(dg_ref)

        dx2v = dx2_ref[...]
        do = _dot_nt(dx2v.astype(BF16), wo_ref[...])
        dqs = []
        for h in range(N_MEM_HEADS):
            sl = slice(h * MEM_HEAD_DIM, (h + 1) * MEM_HEAD_DIM)
            qh, kh, vh = q_ref[:, sl], k_ref[:, sl], v_ref[:, sl]
            lg = _dot_nt(qh, kh) * (MEM_HEAD_DIM ** -0.5)
            p = jnp.exp(lg - jnp.max(lg, axis=-1, keepdims=True))
            p = p / jnp.sum(p, axis=-1, keepdims=True)
            doh = do[:, sl].astype(BF16)
            dp = _dot_nt(doh, vh)
            ds = (p * (dp - jnp.sum(p * dp, axis=-1, keepdims=True)) * (MEM_HEAD_DIM ** -0.5)).astype(BF16)
            dqs.append(_dot(ds, kh))
            dk_ref[:, sl] += _dot_tn(ds, qh)
            dv_ref[:, sl] += _dot_tn(p.astype(BF16), doh)
        dq = jnp.concatenate(dqs, axis=1).astype(BF16)
        dq_ref[...] = dq
        dh2 = _dot_nt(dq, wq_ref[...])
        xh, r = _rms(x1_ref[...])
        dg_ref[0:1, :] += jnp.sum(dh2 * xh, axis=0, keepdims=True)
        dx1_ref[...] = dx2v + _rms_bwd(xh, r, g_ref[...], dh2)

    row = pl.BlockSpec((TM, D_MODEL), lambda i: (i, 0))
    return pl.pallas_call(
        body, name="xattn_bwd", grid=(s // TM,),
        out_shape=[jax.ShapeDtypeStruct((s, D_MODEL), BF16), jax.ShapeDtypeStruct((s, D_MODEL), F32),
                   jax.ShapeDtypeStruct(k.shape, F32), jax.ShapeDtypeStruct(k.shape, F32),
                   jax.ShapeDtypeStruct((SUBLANES, D_MODEL), F32)],
        in_specs=[row, row, row, _full(k.shape), _full(v.shape), _full(w_xo.shape), _full(w_xq.shape), row,
                  _full(g.shape), ANY_SPEC],
        out_specs=[row, row, _full(k.shape), _full(k.shape), _full((SUBLANES, D_MODEL))],
        compiler_params=_cparams(1),
    )(dx2, o, q, k, v, w_xo, w_xq, x1, g, dep)


def _mem_kv_bwd(dk, dv, mem_n, mem, w_xk, w_xv):
    def body(dk_ref, dv_ref, mn_ref, mem_ref, wk_ref, wv_ref, dwk_ref, dwv_ref, dg_ref):
        dkb, dvb = dk_ref[...].astype(BF16), dv_ref[...].astype(BF16)
        mn = mn_ref[...]
        dwk_ref[...] = _dot_tn(mn, dkb).astype(BF16)
        dwv_ref[...] = _dot_tn(mn, dvb).astype(BF16)
        dmn = _dot_nt(dkb, wk_ref[...]) + _dot_nt(dvb, wv_ref[...])
        xh, _ = _rms(mem_ref[...])
        dg_ref[...] = jnp.zeros_like(dg_ref)
        dg_ref[0:1, :] = jnp.sum(dmn * xh, axis=0, keepdims=True)

    vm = pl.BlockSpec(memory_space=pltpu.VMEM)
    return pl.pallas_call(
        body, name="mem_kv_bwd",
        out_shape=[jax.ShapeDtypeStruct(w_xk.shape, BF16), jax.ShapeDtypeStruct(w_xv.shape, BF16),
                   jax.ShapeDtypeStruct((SUBLANES, D_MODEL), F32)],
        in_specs=[vm] * 6, out_specs=[vm] * 3,
        compiler_params=pltpu.CompilerParams(vmem_limit_bytes=VMEM_LIMIT),
    )(dk, dv, mem_n, mem, w_xk, w_xv)


def _mix_out_bwd(dx1, w_out, attn, gb, gc, xi, w_sc, g_a, g_c, dep):
    s = dx1.shape[0]
    tb = TM // SUBLANES

    def body(dx1_ref, wout_ref, attn_ref, gb_ref, gc_ref, xi_ref, gch_ref, xih_ref, wsc_ref, ga_ref, gcv_ref, dep_ref,
             dattn_ref, dd_ref, dgb_ref, dcv_ref, dga_ref, dgc_ref, dwsc_ref):
        i = pl.program_id(0)

        @pl.when(i == 0)
        def _():
            dga_ref[...] = jnp.zeros_like(dga_ref)
            dgc_ref[...] = jnp.zeros_like(dgc_ref)
            dwsc_ref[...] = jnp.zeros_like(dwsc_ref)

        dmixed = _dot_nt(dx1_ref[...].astype(BF16), wout_ref[...])
        da, dcn = dmixed[:, :ATTN_W], dmixed[:, ATTN_W:]
        attn = attn_ref[...]
        xa, ra = _rms(attn)
        dga_ref[0:1, :] += jnp.sum(da * xa, axis=0, keepdims=True)
        dattn = _rms_bwd(xa, ra, ga_ref[...], da)
        dattn_ref[...] = dattn
        prod = dattn * attn
        for h in range(N_HEADS):
            sl = slice(h * HEAD_DIM, (h + 1) * HEAD_DIM)
            dd_ref[:, sl] = jnp.broadcast_to(jnp.sum(prod[:, sl], axis=-1, keepdims=True), (TM, HEAD_DIM))
        gbv = gb_ref[...]
        u = gc_ref[...] * xi_ref[...]
        uh = jnp.where(i > 0, gch_ref[...] * xih_ref[...], 0.0)
        u2, u1 = _shift_down(u, uh, 2), _shift_down(u, uh, 1)
        cv = (u2 * wsc_ref[0:1, :] + u1 * wsc_ref[1:2, :]) + u * wsc_ref[2:3, :]
        xc, rc = _rms(gbv * cv)
        dgc_ref[0:1, :] += jnp.sum(dcn * xc, axis=0, keepdims=True)
        dconv = _rms_bwd(xc, rc, gcv_ref[...], dcn)
        dgb_ref[...] = dconv * cv
        dcv = dconv * gbv
        dcv_ref[...] = dcv
        dwsc_ref[0:1, :] += jnp.sum(dcv * u2, axis=0, keepdims=True)
        dwsc_ref[1:2, :] += jnp.sum(dcv * u1, axis=0, keepdims=True)
        dwsc_ref[2:3, :] += jnp.sum(dcv * u, axis=0, keepdims=True)

    row = lambda n: pl.BlockSpec((TM, n), lambda i: (i, 0))
    halo = pl.BlockSpec((SUBLANES, 512), lambda i: (jnp.maximum(i * tb - 1, 0), 0))
    acc = _full((SUBLANES, 512))
    return pl.pallas_call(
        body, name="mix_out_bwd", grid=(s // TM,),
        out_shape=[jax.ShapeDtypeStruct((s, 512), F32)] * 4 + [jax.ShapeDtypeStruct((SUBLANES, 512), F32)] * 3,
        in_specs=[row(D_MODEL), _full(w_out.shape), row(512), row(512), row(512), row(512), halo, halo,
                  _full(w_sc.shape), _full(g_a.shape), _full(g_c.shape), ANY_SPEC],
        out_specs=[row(512)] * 4 + [acc] * 3,
        compiler_params=_cparams(1),
    )(dx1, w_out, attn, gb, gc, xi, gc, xi, w_sc, g_a, g_c, dep)


def _swa_bwd(qv, kv, vv, dov, lsev, ddv, bias, dil, dep):
    rows = qv.shape[0]
    nb = rows // WIN

    def body(q_ref, qn_ref, kp_ref, kc_ref, vp_ref, vc_ref, do_ref, don_ref, lse_ref, lsen_ref, dd_ref, ddn_ref,
             b_ref, dep_ref, dq_ref, dk_ref, dv_ref, db_ref):
        r, b = pl.program_id(0), pl.program_id(1)

        @pl.when((r == 0) & (b == 0))
        def _():
            db_ref[...] = jnp.zeros_like(db_ref)

        valid = _band_mask(b)
        qi = lax.broadcasted_iota(jnp.int32, (WIN, WIN), 0)
        kj = lax.broadcasted_iota(jnp.int32, (WIN, WIN), 1)
        valid_n = kj >= qi + jnp.where(b + 1 < nb, 0, WIN)
        for h in range(N_HEADS):
            sl = slice(h * HEAD_DIM, (h + 1) * HEAD_DIM)
            col = slice(h * HEAD_DIM, h * HEAD_DIM + 1)
            qh, kc, vc = q_ref[:, sl], kc_ref[:, sl], vc_ref[:, sl]
            kh = jnp.concatenate([kp_ref[:, sl], kc], axis=0)
            vh = jnp.concatenate([vp_ref[:, sl], vc], axis=0)
            doh = do_ref[:, sl].astype(BF16)
            lg = jnp.where(valid, _dot_nt(qh, kh) + b_ref[h], -jnp.inf)
            p = jnp.exp(lg - lse_ref[:, col])
            ds = p * (_dot_nt(doh, vh) - dd_ref[:, col])
            db_ref[h] += ds
            dsb = ds.astype(BF16)
            dq_ref[:, sl] = _dot(dsb, kh)
            dk = _dot_tn(dsb[:, WIN:], qh)
            dv = _dot_tn(p[:, WIN:].astype(BF16), doh)
            qn = qn_ref[:, sl]
            don = don_ref[:, sl].astype(BF16)
            lgn = jnp.where(valid_n, _dot_nt(qn, kc) + b_ref[h][:, :WIN], -jnp.inf)
            pn = jnp.exp(lgn - lsen_ref[:, col])
            dsn = pn * (_dot_nt(don, vc) - ddn_ref[:, col])
            dk_ref[:, sl] = dk + _dot_tn(dsn.astype(BF16), qn)
            dv_ref[:, sl] = dv + _dot_tn(pn.astype(BF16), don)

    cur = pl.BlockSpec((WIN, 512), lambda r, b: (b, r))
    prev = pl.BlockSpec((WIN, 512), lambda r, b: (jnp.maximum(b - 1, 0), r))
    nxt = pl.BlockSpec((WIN, 512), lambda r, b: (jnp.minimum(b + 1, nb - 1), r))
    return pl.pallas_call(
        body, name=f"swa_bwd_d{dil}", grid=(dil, nb),
        out_shape=[jax.ShapeDtypeStruct(qv.shape, F32)] * 3 + [jax.ShapeDtypeStruct(bias.shape, F32)],
        in_specs=[cur, nxt, prev, cur, prev, cur, cur, nxt, cur, nxt, cur, nxt, _full(bias.shape), ANY_SPEC],
        out_specs=[cur] * 3 + [_full(bias.shape)],
        compiler_params=_cparams(2),
    )(qv, qv, kv, kv, vv, vv, dov, dov, lsev, lsev, ddv, ddv, bias, dep)


def _in_proj_bwd(dqs, dks, dvs, dgb, dcv, gc, xi, w_sc, w_in_g, x, g_mix, dx1):
    s = x.shape[0]
    tb = TM // SUBLANES
    last = s // SUBLANES - 1
    n_tiles = s // TM

    def body(dq1, dq2, dq3, dk1, dk2, dk3, dv1, dv2, dv3, dgb_ref, dcv_ref, dcvn_ref, gc_ref, xi_ref, wsc_ref,
             win_ref, x_ref, g_ref, dx1_ref, dproj_ref, gx_ref, dg_ref):
        i = pl.program_id(0)

        @pl.when(i == 0)
        def _():
            dg_ref[...] = jnp.zeros_like(dg_ref)

        d0 = dcv_ref[...]
        dn = jnp.where(i < n_tiles - 1, dcvn_ref[...], 0.0)
        du = (d0 * wsc_ref[2:3, :] + _shift_up(d0, dn, 1) * wsc_ref[1:2, :]) + _shift_up(d0, dn, 2) * wsc_ref[0:1, :]
        dq = ((dq1[...] + dq2[...]) + dq3[...]) * (HEAD_DIM ** -0.5)
        dk = (dk1[...] + dk2[...]) + dk3[...]
        dv = (dv1[...] + dv2[...]) + dv3[...]
        dproj = jnp.concatenate([dq, dk, dv, dgb_ref[...], du * xi_ref[...], du * gc_ref[...]], axis=1).astype(BF16)
        dproj_ref[...] = dproj
        dh = jnp.zeros((TM, D_MODEL), F32)
        for j in range(N_DEV):
            dh = dh + _dot_nt(dproj[:, j * IN_CHUNK:(j + 1) * IN_CHUNK], win_ref[j])
        xh, r = _rms(x_ref[...])
        dg_ref[0:1, :] += jnp.sum(dh * xh, axis=0, keepdims=True)
        gx_ref[...] = dx1_ref[...] + _rms_bwd(xh, r, g_ref[...], dh)

    row = lambda n: pl.BlockSpec((TM, n), lambda i: (i, 0))
    nxt = pl.BlockSpec((SUBLANES, 512), lambda i: (jnp.minimum((i + 1) * tb, last), 0))
    return pl.pallas_call(
        body, name="in_proj_bwd", grid=(n_tiles,),
        out_shape=[jax.ShapeDtypeStruct((s, IN_COLS), BF16), jax.ShapeDtypeStruct((s, D_MODEL), F32),
                   jax.ShapeDtypeStruct((SUBLANES, D_MODEL), F32)],
        in_specs=[row(512)] * 11 + [nxt, row(512), row(512), _full(w_sc.shape), _full(w_in_g.shape),
                                    row(D_MODEL), _full(g_mix.shape), row(D_MODEL)],
        out_specs=[row(IN_COLS), row(D_MODEL), _full((SUBLANES, D_MODEL))],
        compiler_params=_cparams(1),
    )(*dqs, *dks, *dvs, dgb, dcv, dcv, gc, xi, w_sc, w_in_g, x, g_mix, dx1)


def _dw(a, b, dep, name, a_chunked=False, b_chunked=False, n_chunks=1, chunk_cols=None):
    ts = TM
    if a_chunked:
        nj, s, kk = a.shape
        nn = b.shape[1]
        a_spec = pl.BlockSpec((1, ts, kk), lambda j, t: (j, t, 0))
        b_spec = pl.BlockSpec((ts, nn), lambda j, t: (t, 0))
    elif b_chunked:
        nj, s, nn = b.shape
        kk = a.shape[1]
        a_spec = pl.BlockSpec((ts, kk), lambda j, t: (t, 0))
        b_spec = pl.BlockSpec((1, ts, nn), lambda j, t: (j, t, 0))
    else:
        s, kk = a.shape
        nj, nn = (n_chunks, chunk_cols) if chunk_cols else (1, b.shape[1])
        a_spec = pl.BlockSpec((ts, kk), lambda j, t: (t, 0))
        b_spec = pl.BlockSpec((ts, nn), lambda j, t: (t, j))
    n_steps = s // ts

    def body(a_ref, b_ref, dep_ref, o_ref, acc):
        t = pl.program_id(1)

        @pl.when(t == 0)
        def _():
            acc[...] = jnp.zeros_like(acc)

        av = (a_ref[0] if a_chunked else a_ref[...]).astype(BF16)
        bv = (b_ref[0] if b_chunked else b_ref[...]).astype(BF16)
        acc[...] += _dot_tn(av, bv)

        @pl.when(t == n_steps - 1)
        def _():
            o_ref[0] = acc[...].astype(BF16)

    return pl.pallas_call(
        body, name=name, grid=(nj, n_steps),
        out_shape=jax.ShapeDtypeStruct((nj, kk, nn), BF16),
        in_specs=[a_spec, b_spec, ANY_SPEC],
        out_specs=pl.BlockSpec((1, kk, nn), lambda j, t: (j, 0, 0)),
        scratch_shapes=[pltpu.VMEM((kk, nn), F32)],
        compiler_params=_cparams(2),
    )(a, b, dep)


def _adamw_math(w, g, m, v):
    m2 = ADAM_B1 * m + (1.0 - ADAM_B1) * g
    v2 = ADAM_B2 * v + (1.0 - ADAM_B2) * (g * g)
    m_hat = m2 / (1.0 - ADAM_B1 ** ADAM_STEP)
    v_hat = v2 / (1.0 - ADAM_B2 ** ADAM_STEP)
    delta = -ADAM_LR * (m_hat / (jnp.sqrt(v_hat) + ADAM_EPS) + ADAM_WD * w)
    return delta, m2, v2


def _sum_parts(me, own, p_ref):
    g = None
    for i in range(N_DEV):
        part = jnp.where(me == i, own.astype(F32), p_ref[i].astype(F32))
        g = part if g is None else g + part
    return g


def _adamw_big(name, w, own, parts, m, v, me_arr):
    rr, cc = w.shape
    tr = rr // 4 if rr >= 512 else rr

    def body(me_ref, w_ref, own_ref, p_ref, m_ref, v_ref, g_ref, d_ref, nm_ref, nv_ref):
        g = _sum_parts(me_ref[0], own_ref[...], p_ref)
        g_ref[...] = g
        d_ref[...], nm_ref[...], nv_ref[...] = _adamw_math(w_ref[...], g, m_ref[...], v_ref[...])

    row = pl.BlockSpec((tr, cc), lambda i: (i, 0))
    return pl.pallas_call(
        body, name=name, grid=(rr // tr,),
        out_shape=[jax.ShapeDtypeStruct((rr, cc), F32)] * 4,
        in_specs=[SMEM_SPEC, row, row, pl.BlockSpec((N_DEV, tr, cc), lambda i: (0, i, 0)), row, row],
        out_specs=[row] * 4,
        compiler_params=_cparams(1),
    )(me_arr, w, own, parts, m, v)


def _small_slices():
    return [
        (slice(ROW_RELB, ROW_RELB + 8), slice(0, N_BUCKETS)),
        (slice(ROW_GMIX, ROW_GMIX + 1), slice(0, D_MODEL)),
        (slice(ROW_GAC, ROW_GAC + 1), slice(0, ATTN_W)),
        (slice(ROW_GAC, ROW_GAC + 1), slice(ATTN_W, D_MODEL)),
        (slice(ROW_GXATTN, ROW_GXATTN + 1), slice(0, D_MODEL)),
        (slice(ROW_GMEM, ROW_GMEM + 1), slice(0, D_MODEL)),
        (slice(ROW_GFFN, ROW_GFFN + 1), slice(0, D_MODEL)),
        (slice(ROW_BFC, ROW_BFC + 8), slice(0, UP_CHUNK)),
        (slice(ROW_GFINAL, ROW_GFINAL + 1), slice(0, D_MODEL)),
    ]


def _adamw_small(own, parts, wmv, me_arr):
    slices = _small_slices()
    n = len(slices)

    def body(*refs):
        me_ref, own_ref, p_ref = refs[:3]
        ins = refs[3:3 + 3 * n]
        g_ref = refs[3 + 3 * n]
        outs = refs[4 + 3 * n:]
        g = _sum_parts(me_ref[0], own_ref[...], p_ref)
        g_ref[...] = g
        for a, (rs, ls) in enumerate(slices):
            ga = g[rs, ls]
            outs[4 * a][...] = ga
            outs[4 * a + 1][...], outs[4 * a + 2][...], outs[4 * a + 3][...] = _adamw_math(
                ins[3 * a][...], ga, ins[3 * a + 1][...], ins[3 * a + 2][...])

    vm = pl.BlockSpec(memory_space=pltpu.VMEM)
    flat = [t for trip in wmv for t in trip]
    out_shape = [jax.ShapeDtypeStruct((SMALL_ROWS, D_MODEL), F32)]
    for w, _, _ in wmv:
        out_shape += [jax.ShapeDtypeStruct(w.shape, F32)] * 4
    res = pl.pallas_call(
        body, name="adamw_small", out_shape=out_shape,
        in_specs=[SMEM_SPEC] + [vm] * (2 + 3 * n), out_specs=[vm] * len(out_shape),
    )(me_arr, own, parts, *flat)
    return res[0], [res[1 + 4 * a:5 + 4 * a] for a in range(n)]


def _adamw_shards(items):
    n = len(items)

    def body(*refs):
        for a in range(n):
            w_ref, g_ref, m_ref, v_ref = refs[4 * a:4 * a + 4]
            d_ref, nm_ref, nv_ref = refs[4 * n + 3 * a:4 * n + 3 * a + 3]
            d_ref[...], nm_ref[...], nv_ref[...] = _adamw_math(w_ref[...], g_ref[...], m_ref[...], v_ref[...])

    vm = pl.BlockSpec(memory_space=pltpu.VMEM)
    out_shape = []
    for w, _, _, _ in items:
        out_shape += [jax.ShapeDtypeStruct(w.shape, F32)] * 3
    res = pl.pallas_call(
        body, name="adamw_shards", out_shape=out_shape, in_specs=[vm] * (4 * n), out_specs=[vm] * (3 * n),
    )(*[t for it in items for t in it])
    return [res[3 * a:3 * a + 3] for a in range(n)]


def _mesh_pos():
    return lax.axis_index("x"), lax.axis_index("y"), lax.axis_index("c")


def _dev_index(p):
    return 4 * p[0] + 2 * p[1] + p[2]


def _all_gather(shards):
    n = len(shards)

    def body(*refs):
        ins, outs = refs[:n], refs[n:2 * n]
        send_sems, recv_sems, loc_sems = refs[2 * n:]
        x, y, c = _mesh_pos()
        me, sib = (x, y, c), (x, y, 1 - c)
        chips = [(1 - x, y), (x, 1 - y), (1 - x, 1 - y)]

        def cp(a, k, block, to, src=None):
            dst = outs[a].at[_dev_index(block)]
            return pltpu.make_async_remote_copy(
                src_ref=dst if src is None else src, dst_ref=dst, send_sem=send_sems.at[a, k],
                recv_sem=recv_sems.at[a, k], device_id=to, device_id_type=MESH)

        mine = [pltpu.make_async_copy(ins[a], outs[a].at[_dev_index(me)], loc_sems.at[a]) for a in range(n)]
        for m_ in mine:
            m_.start()
        first = []
        for a in range(n):
            first.append(cp(a, 0, me, sib, src=ins[a]))
            first += [cp(a, 1 + j, me, (*chip, c), src=ins[a]) for j, chip in enumerate(chips)]
        for f in first:
            f.start()
        passed = []
        for a in range(n):
            for j, chip in enumerate(chips):
                cp(a, 1 + j, (*chip, c), me).wait_recv()
                fwd = cp(a, 4 + j, (*chip, c), sib)
                fwd.start()
                passed.append(fwd)
        for a in range(n):
            cp(a, 0, sib, me).wait_recv()
            for j, chip in enumerate(chips):
                cp(a, 4 + j, (*chip, 1 - c), me).wait_recv()
        for f in first + passed:
            f.wait_send()
        for m_ in mine:
            m_.wait()

    hbm = pl.BlockSpec(memory_space=pltpu.HBM)
    return pl.pallas_call(
        body, name="all_gather_weights",
        out_shape=[jax.ShapeDtypeStruct((N_DEV,) + a.shape, a.dtype) for a in shards],
        in_specs=[hbm] * n, out_specs=[hbm] * n,
        scratch_shapes=[pltpu.SemaphoreType.DMA((n, 7)), pltpu.SemaphoreType.DMA((n, 7)),
                        pltpu.SemaphoreType.DMA((n,))],
    )(*shards)


def _peers():
    x, y, c = _mesh_pos()
    return (x, y, c), [((1 - x) if k & 4 else x, (1 - y) if k & 2 else y, (1 - c) if k & 1 else c)
                       for k in range(1, 8)]


def _exchange_copy(src_ref, land_ref, whole, send_sems, recv_sems, a, k, peer, slot):
    src = src_ref if whole else src_ref.at[_dev_index(peer)]
    return pltpu.make_async_remote_copy(
        src_ref=src, dst_ref=land_ref.at[slot], send_sem=send_sems.at[7 * a + k], recv_sem=recv_sems.at[7 * a + k],
        device_id=peer, device_id_type=MESH)


def _exchange_start(name, srcs, whole, dep):
    n = len(srcs)
    lands = [lax.empty(((N_DEV,) + s.shape) if w else s.shape, s.dtype) for s, w in zip(srcs, whole)]

    def body(*refs):
        src_refs, land_refs = refs[:n], refs[n:2 * n]
        send_sems, recv_sems, token = refs[2 * n + 1], refs[2 * n + 2], refs[-1]
        me, peers = _peers()
        for a in range(n):
            for k, peer in enumerate(peers):
                _exchange_copy(src_refs[a], land_refs[a], whole[a], send_sems, recv_sems, a, k, peer,
                               _dev_index(me)).start()
        token[...] = jnp.zeros_like(token)

    res = pl.pallas_call(
        body, name=name,
        out_shape=(pltpu.SemaphoreType.DMA((7 * n,)), pltpu.SemaphoreType.DMA((7 * n,)),
                   *[pltpu.HBM(a.shape, a.dtype) for a in srcs], *[pltpu.HBM(a.shape, a.dtype) for a in lands],
                   jax.ShapeDtypeStruct((SUBLANES, 128), F32)),
        in_specs=[HBM_SPEC] * (2 * n) + [ANY_SPEC],
        out_specs=(SEM_SPEC, SEM_SPEC, *([HBM_SPEC] * (2 * n)), VMEM_SPEC),
        input_output_aliases={i: 2 + i for i in range(2 * n)},
        compiler_params=pltpu.CompilerParams(has_side_effects=DATAFLOW),
    )(*[pltpu.with_memory_space_constraint(a, pltpu.HBM) for a in srcs],
      *[pltpu.with_memory_space_constraint(a, pltpu.HBM) for a in lands], dep)
    return res[0], res[1], list(res[2:2 + n]), list(res[2 + n:2 + 2 * n]), res[-1]


def _exchange_wait(name, started, whole, after, which=None):
    send_sems, recv_sems, srcs, lands, _ = started
    which = list(range(len(srcs))) if which is None else which
    srcs, lands = [srcs[a] for a in which], [lands[a] for a in which]
    n = len(srcs)

    def body(*refs):
        src_refs, land_refs = refs[:n], refs[n:2 * n]
        send_sems, recv_sems = refs[2 * n], refs[2 * n + 1]
        _, peers = _peers()
        for i, a in enumerate(which):
            for k, peer in enumerate(peers):
                cp = _exchange_copy(src_refs[i], land_refs[i], whole[a], send_sems, recv_sems, a, k, peer,
                                    _dev_index(peer))
                cp.wait_send()
                cp.wait_recv()

    res = pl.pallas_call(
        body, name=name,
        out_shape=[pltpu.HBM(a.shape, a.dtype) for a in srcs + lands],
        in_specs=[HBM_SPEC] * (2 * n) + [SEM_SPEC, SEM_SPEC, ANY_SPEC],
        out_specs=[HBM_SPEC] * (2 * n),
        input_output_aliases={i: i for i in range(2 * n)},
        compiler_params=pltpu.CompilerParams(has_side_effects=DATAFLOW),
    )(*srcs, *lands, send_sems, recv_sems, after)
    return list(res[n:])


def _views(a, dil):
    s = a.shape[0]
    return a.reshape(s // dil, dil * a.shape[1])


def _local_step(x, mem, target, rel_bias, g_mix, w_in_g, w_sc, g_a, g_c, g_xattn, g_mem, g_ffn, w_fc, b_fc, g_final,
                dep, late_weights, emit, emit_small):
    s = x.shape[0]
    buckets = _bucket_tables()
    bias = _bias_fwd(rel_bias, buckets)

    h1, q, k, v, gb, gc, xi = _rms_proj(x, g_mix, w_in_g, dep)
    branches = []
    for p, dil in enumerate(DILATIONS):
        outs = _swa_fwd(_views(q, dil), _views(k, dil), _views(v, dil), bias[p], dil)
        branches.append([o.reshape(s, ATTN_W) for o in outs])
    w_out = late_weights(["w_out"], branches[-1][0])["w_out"]
    attn, lse, mixed, x1 = _mix_out(branches, gb, gc, xi, x, w_sc, g_a, g_c, w_out)
    lw = late_weights(["w_xq", "w_xk", "w_xv", "w_xo"], x1)
    w_xq, w_xk, w_xv, w_xo = lw["w_xq"], lw["w_xk"], lw["w_xv"], lw["w_xo"]
    mem_n, mk, mv = _mem_kv(mem, g_mem, w_xk, w_xv)
    h2, xq, xo, x2 = _xattn_fwd(x1, g_xattn, w_xq, mk, mv, w_xo)
    w_up_g = late_weights(["w_up"], x2)["w_up"]
    h3, up = _ffn_up(x2, g_ffn, w_up_g)
    w_down_g = late_weights(["w_down"], up)["w_down"]
    act, dx3, loss_acc, dg_final = _ffn_tail(up, w_fc, b_fc, w_down_g, x2, g_final, target)

    dc, dw_fc, db_fc = _ffn_down_bwd(dx3, up, w_fc, b_fc, w_down_g)
    gw_down = _dw(act, dx3, dep, "dw_down", a_chunked=True)
    dup, dx2, dg_ffn = _ffn_up_bwd(dc, w_fc, w_up_g, x2, g_ffn, dx3)
    gw_up = _dw(h3, dup, dep, "dw_up", b_chunked=True)
    tok = emit(dict(w_down=gw_down, w_up=gw_up))
    dxq, dx1, dmk, dmv, dg_xattn = _xattn_bwd(dx2, xo, xq, mk, mv, w_xo, w_xq, x1, g_xattn, tok)
    gw_xo = _dw(xo, dx2, tok, "dw_xo")[0]
    gw_xq = _dw(h2, dxq, tok, "dw_xq")[0]
    gw_xk, gw_xv, dg_mem = _mem_kv_bwd(dmk, dmv, mem_n, mem, w_xk, w_xv)
    tok = emit(dict(w_xo=gw_xo, w_xq=gw_xq, w_xk=gw_xk, w_xv=gw_xv))
    dattn, dd, dgb, dcv, dg_a, dg_c, dw_sc = _mix_out_bwd(dx1, w_out, attn, gb, gc, xi, w_sc, g_a, g_c, tok)
    gw_out = _dw(mixed, dx1, tok, "dw_out")[0]
    tok = emit(dict(w_out=gw_out))
    dqs, dks, dvs, dbias = [], [], [], []
    for p, dil in enumerate(DILATIONS):
        dq_p, dk_p, dv_p, db_p = _swa_bwd(_views(q, dil), _views(k, dil), _views(v, dil), _views(dattn, dil),
                                          _views(lse, dil), _views(dd, dil), bias[p], dil, tok)
        dqs.append(dq_p.reshape(s, ATTN_W))
        dks.append(dk_p.reshape(s, ATTN_W))
        dvs.append(dv_p.reshape(s, ATTN_W))
        dbias.append(db_p)
    d_relb = _bias_bwd(jnp.stack(dbias), buckets)
    dproj, grad_x, dg_mix = _in_proj_bwd(dqs, dks, dvs, dgb, dcv, gc, xi, w_sc, w_in_g, x, g_mix, dx1)
    pad = lambda a: jnp.pad(a, ((0, 0), (0, D_MODEL - a.shape[1])))
    small = jnp.concatenate([
        d_relb, dg_mix, dg_xattn, dg_mem, dg_ffn, dg_final, jnp.concatenate([dg_a, dg_c], axis=1),
        pad(dw_sc), pad(db_fc), pad(dw_fc.reshape(3 * N_DEV, UP_CHUNK))], axis=0)
    tok = emit_small(small)
    gw_in = _dw(h1, dproj, tok, "dw_in", n_chunks=N_DEV, chunk_cols=IN_CHUNK)
    emit(dict(w_in=gw_in))
    return loss_acc[0, 0], grad_x


def kernel(x, mem, rel_bias, g_mix, w_in, w_short_conv, g_attn_out, g_conv_out, w_out, g_xattn, g_mem, w_xq, w_xk, w_xv, w_xo, g_ffn, w_up, w_ffn_conv, b_ffn_conv, w_down, g_final, loss_target, m_rel_bias, m_g_mix, m_w_in, m_w_short_conv, m_g_attn_out, m_g_conv_out, m_w_out, m_g_xattn, m_g_mem, m_w_xq, m_w_xk, m_w_xv, m_w_xo, m_g_ffn, m_w_up, m_w_ffn_conv, m_b_ffn_conv, m_w_down, m_g_final, v_rel_bias, v_g_mix, v_w_in, v_w_short_conv, v_g_attn_out, v_g_conv_out, v_w_out, v_g_xattn, v_g_mem, v_w_xq, v_w_xk, v_w_xv, v_w_xo, v_g_ffn, v_w_up, v_w_ffn_conv, v_b_ffn_conv, v_w_down, v_g_final):
    me = _dev_index(_mesh_pos())
    me_arr = me.reshape(1).astype(jnp.int32)

    big_names = ["w_in", "w_out", "w_xq", "w_xk", "w_xv", "w_xo", "w_up", "w_down"]
    late_names = big_names[1:]
    big_w = dict(w_in=w_in[0], w_out=w_out[0], w_xq=w_xq[0], w_xk=w_xk[0], w_xv=w_xv[0], w_xo=w_xo[0],
                 w_up=w_up[0], w_down=w_down[0])
    big_m = dict(w_in=m_w_in[0], w_out=m_w_out[0], w_xq=m_w_xq[0], w_xk=m_w_xk[0], w_xv=m_w_xv[0], w_xo=m_w_xo[0],
                 w_up=m_w_up[0], w_down=m_w_down[0])
    big_v = dict(w_in=v_w_in[0], w_out=v_w_out[0], w_xq=v_w_xq[0], w_xk=v_w_xk[0], w_xv=v_w_xv[0], w_xo=v_w_xo[0],
                 w_up=v_w_up[0], w_down=v_w_down[0])
    shard_shape = {n: big_w[n].shape for n in big_names}

    w_in_g, w_sc_g, w_fc_full = _all_gather([big_w["w_in"].astype(BF16), w_short_conv[0], w_ffn_conv[0]])
    w_sc_full = w_sc_g.transpose(1, 0, 2).reshape(3, CONV_W)
    late_shards = [big_w[n].astype(BF16) for n in late_names]
    ag = _exchange_start("gather_weights_start", late_shards, [True] * len(late_names), w_in_g)

    def late_weights(names, after):
        which = [late_names.index(n) for n in names]
        lands = _exchange_wait("gather_" + "_".join(names) + "_wait", ag, [True] * len(late_names), after, which)
        out = {}
        for n, a, land in zip(names, which, lands):
            full = lax.dynamic_update_index_in_dim(land, late_shards[a], me, 0)
            if n == "w_up":
                out[n] = full
            elif n == "w_down":
                out[n] = full.reshape(N_DEV // 2, UP_CHUNK, D_MODEL)
            else:
                out[n] = full.reshape(D_MODEL, D_MODEL)
        return out

    sent = []

    def emit(grads):
        names = list(grads)
        blocks = [grads[n].reshape((N_DEV,) + shard_shape[n]) for n in names]
        own = [lax.dynamic_index_in_dim(b, me, 0, keepdims=False) for b in blocks]
        started = _exchange_start("scatter_" + "_".join(names) + "_start", blocks, [False] * len(names), me_arr)
        sent.append((names, own, started))
        return started[-1]

    def emit_small(small):
        sent_small.append((small, _exchange_start("gather_small_start", [small], [True], me_arr)))
        return sent_small[0][1][-1]

    sent_small = []
    loss_part, grad_x = _local_step(
        x[0], mem[0], loss_target[0], rel_bias, g_mix, w_in_g, w_sc_full, g_attn_out, g_conv_out, g_xattn, g_mem,
        g_ffn, w_fc_full, b_ffn_conv.reshape(N_DEV, 1, UP_CHUNK), g_final.reshape(1, D_MODEL), ag[-1],
        late_weights, emit, emit_small)
    loss = lax.psum(loss_part, ("x", "y", "c"))

    small_g, small_started = sent_small[0]
    after = sent[-1][2][-1]
    small_parts = _exchange_wait("gather_small_wait", small_started, [True], after)[0]
    big_out = {}
    after = small_parts
    for names, own, started in sent:
        lands = _exchange_wait("scatter_" + "_".join(names) + "_wait", started, [False] * len(names), after)
        for n, own_n, land in zip(names, own, lands):
            res = _adamw_big("adamw_" + n, big_w[n], own_n, land, big_m[n], big_v[n], me_arr)
            big_out[n] = [r[None] for r in res]
            after = res[0]

    as_rows = lambda a: a.reshape(N_DEV, UP_CHUNK)
    row1 = lambda a: a.reshape(1, D_MODEL)
    small_names = ["rel_bias", "g_mix", "g_attn_out", "g_conv_out", "g_xattn", "g_mem", "g_ffn", "b_ffn_conv", "g_final"]
    wmv = [
        (rel_bias, m_rel_bias, v_rel_bias), (g_mix, m_g_mix, v_g_mix), (g_attn_out, m_g_attn_out, v_g_attn_out),
        (g_conv_out, m_g_conv_out, v_g_conv_out), (g_xattn, m_g_xattn, v_g_xattn), (g_mem, m_g_mem, v_g_mem),
        (g_ffn, m_g_ffn, v_g_ffn), (as_rows(b_ffn_conv), as_rows(m_b_ffn_conv), as_rows(v_b_ffn_conv)),
        (row1(g_final), row1(m_g_final), row1(v_g_final))]
    g_packed, small_res = _adamw_small(small_g, small_parts, wmv, me_arr)
    small_out = dict(zip(small_names, small_res))
    small_out["b_ffn_conv"] = [a.reshape(1, 2 * D_FF) for a in small_out["b_ffn_conv"]]
    small_out["g_final"] = [a.reshape(D_MODEL) for a in small_out["g_final"]]

    g_wsc = lax.dynamic_slice(g_packed[ROW_WSC:ROW_WSC + 3, 0:CONV_W], (0, me * HEAD_DIM), (3, HEAD_DIM))
    g_wfc = lax.dynamic_slice(g_packed[ROW_WFC:ROW_WFC + 3 * N_DEV, 0:UP_CHUNK].reshape(3, N_DEV, UP_CHUNK),
                              (0, me, 0), (3, 1, UP_CHUNK)).reshape(3, UP_CHUNK)
    shard_res = _adamw_shards([(w_short_conv[0], g_wsc, m_w_short_conv[0], v_w_short_conv[0]),
                               (w_ffn_conv[0], g_wfc, m_w_ffn_conv[0], v_w_ffn_conv[0])])
    small_out["w_short_conv"] = [g_wsc[None]] + [a[None] for a in shard_res[0]]
    small_out["w_ffn_conv"] = [g_wfc[None]] + [a[None] for a in shard_res[1]]

    order = ["rel_bias", "g_mix", "w_in", "w_short_conv", "g_attn_out", "g_conv_out", "w_out", "g_xattn", "g_mem",
             "w_xq", "w_xk", "w_xv", "w_xo", "g_ffn", "w_up", "w_ffn_conv", "b_ffn_conv", "w_down", "g_final"]
    allp = {**big_out, **small_out}
    outs = [loss, grad_x[None]]
    for kind in range(4):
        outs += [allp[n][kind] for n in order]
    return tuple(outs)
```

```python
import functools
import math

import numpy as np
import jax
import jax.numpy as jnp
from jax import lax
from jax.experimental import pallas as pl
from jax.experimental.pallas import tpu as pltpu

F32 = jnp.float32
BF16 = jnp.bfloat16
MESH = pl.DeviceIdType.MESH

N_DEV = 8
D_MODEL = 1024
ATTN_W = 512
CONV_W = 512
N_HEADS = 8
HEAD_DIM = 64
WIN = 128
DILATIONS = (1, 4, 16)
N_BUCKETS = 32
BUCKET_MAX_EXACT = 16
BUCKET_MAX_DISTANCE = 2048
N_MEM_HEADS = 4
MEM_HEAD_DIM = 256
D_FF = 2816
IN_COLS = 3072
IN_CHUNK = IN_COLS // N_DEV
UP_CHUNK = 2 * D_FF // N_DEV
EPS = 1e-6

ADAM_LR = 0.001
ADAM_B1 = 0.9
ADAM_B2 = 0.999
ADAM_EPS = 1e-08
ADAM_WD = 0.01
ADAM_STEP = 10

SUBLANES = 8
LANES = 128
TM = 512
TM_FFN = 256
TS_DW = 2048
VMEM_LIMIT = 56 * 1024 * 1024

ROW_RELB, ROW_GMIX, ROW_GXATTN, ROW_GMEM, ROW_GFFN, ROW_GFINAL, ROW_GAC = 0, 8, 16, 24, 32, 40, 48
ROW_WSC, ROW_BFC, ROW_WFC, SMALL_ROWS = 56, 64, 72, 96


def _cparams(n_grid):
    return pltpu.CompilerParams(dimension_semantics=("arbitrary",) * n_grid, vmem_limit_bytes=VMEM_LIMIT)


def _full(shape):
    nd = len(shape)
    return pl.BlockSpec(tuple(shape), lambda *_: (0,) * nd)


ANY_SPEC = pl.BlockSpec(memory_space=pl.ANY)
HBM_SPEC = pl.BlockSpec(memory_space=pltpu.HBM)
SEM_SPEC = pl.BlockSpec(memory_space=pltpu.SEMAPHORE)
VMEM_SPEC = pl.BlockSpec(memory_space=pltpu.VMEM)
SMEM_SPEC = pl.BlockSpec(memory_space=pltpu.SMEM)
DATAFLOW = pltpu.SideEffectType.DATAFLOW_SIDE_EFFECTING


def _rms(x):
    r = lax.rsqrt(jnp.mean(x * x, axis=-1, keepdims=True) + EPS)
    return x * r, r


def _rms_bwd(xh, r, g, dy):
    dxh = dy * g
    return r * (dxh - xh * jnp.mean(dxh * xh, axis=-1, keepdims=True))


def _shift_down(u, halo, k):
    ru = pltpu.roll(u, k, 0)
    rh = pltpu.roll(halo, k, 0)
    row = lax.broadcasted_iota(jnp.int32, rh.shape, 0)
    head = jnp.where(row < k, rh, ru[0:SUBLANES])
    return jnp.concatenate([head, ru[SUBLANES:]], axis=0)


def _shift_up(u, halo, k):
    tm = u.shape[0]
    ru = pltpu.roll(u, tm - k, 0)
    rh = pltpu.roll(halo, SUBLANES - k, 0)
    row = lax.broadcasted_iota(jnp.int32, rh.shape, 0)
    tail = jnp.where(row >= SUBLANES - k, rh, ru[tm - SUBLANES:])
    return jnp.concatenate([ru[:tm - SUBLANES], tail], axis=0)


def _causal_conv3(u, halo, w_ref):
    return (_shift_down(u, halo, 2) * w_ref[0:1, :] + _shift_down(u, halo, 1) * w_ref[1:2, :]) + u * w_ref[2:3, :]


def _dot(a, b):
    return jnp.dot(a, b, preferred_element_type=F32)


def _dot_nt(a, b):
    return lax.dot_general(a, b, (((1,), (1,)), ((), ())), preferred_element_type=F32)


def _dot_tn(a, b):
    return lax.dot_general(a, b, (((0,), (0,)), ((), ())), preferred_element_type=F32)


def _sigmoid(x):
    return 1.0 / (1.0 + jnp.exp(-x))


def _bucket_tables():
    qi = np.arange(WIN)[:, None]
    kj = np.arange(2 * WIN)[None, :]
    steps = np.clip(qi + WIN - kj, 0, WIN)
    out = []
    for d in DILATIONS:
        dist = steps * d
        dd = np.maximum(dist, 1).astype(np.float32)
        large = BUCKET_MAX_EXACT + (
            np.log(dd / np.float32(BUCKET_MAX_EXACT)) / np.float32(math.log(BUCKET_MAX_DISTANCE / BUCKET_MAX_EXACT))
            * np.float32(N_BUCKETS - BUCKET_MAX_EXACT)).astype(np.int32)
        large = np.minimum(large, N_BUCKETS - 1)
        out.append(np.where(dist < BUCKET_MAX_EXACT, dist, large).astype(np.int32))
    return jnp.asarray(np.stack(out))


def _bias_fwd(rel_bias, buckets):
    def body(rb_ref, bk_ref, o_ref):
        for p in range(3):
            bk = bk_ref[p]
            for h in range(N_HEADS):
                acc = jnp.zeros((WIN, 2 * WIN), F32)
                for b in range(N_BUCKETS):
                    acc = jnp.where(bk == b, rb_ref[h, b], acc)
                o_ref[p, h] = acc

    return pl.pallas_call(
        body, name="bias_fwd",
        out_shape=jax.ShapeDtypeStruct((3, N_HEADS, WIN, 2 * WIN), F32),
        in_specs=[pl.BlockSpec(memory_space=pltpu.SMEM), pl.BlockSpec(memory_space=pltpu.VMEM)],
        out_specs=pl.BlockSpec(memory_space=pltpu.VMEM),
    )(rel_bias, buckets)


def _bias_bwd(dbias, buckets):
    def body(db_ref, bk_ref, o_ref):
        lane = lax.broadcasted_iota(jnp.int32, (1, D_MODEL), 1)
        rows = []
        for h in range(N_HEADS):
            row = jnp.zeros((1, D_MODEL), F32)
            for b in range(N_BUCKETS):
                tot = jnp.zeros((1, 1), F32)
                for p in range(3):
                    sel = jnp.where(bk_ref[p] == b, db_ref[p, h], 0.0)
                    tot = tot + jnp.sum(jnp.sum(sel, axis=0, keepdims=True), axis=1, keepdims=True)
                row = jnp.where(lane == b, tot, row)
            rows.append(row)
        o_ref[...] = jnp.concatenate(rows, axis=0)

    return pl.pallas_call(
        body, name="bias_bwd",
        out_shape=jax.ShapeDtypeStruct((N_HEADS, D_MODEL), F32),
        in_specs=[pl.BlockSpec(memory_space=pltpu.VMEM), pl.BlockSpec(memory_space=pltpu.VMEM)],
        out_specs=pl.BlockSpec(memory_space=pltpu.VMEM),
    )(dbias, buckets)


def _spread(val, scr_ref, out_refs, dtype):
    out_refs[0][...] = val.astype(dtype)
    n_blk = val.shape[1] // LANES
    for c in range(n_blk):
        scr_ref[c] = val[:, c * LANES:(c + 1) * LANES]
    for o_ref, d in zip(out_refs[1:], DILATIONS[1:]):
        for r in range(d):
            for c in range(n_blk):
                o_ref[r, :, c * LANES:(c + 1) * LANES] = scr_ref.at[c][pl.ds(r, TM // d, stride=d), :].astype(dtype)


def _gather_classes(blk_ref, scr_ref, d):
    n_blk = blk_ref.shape[2] // LANES
    for r in range(d):
        for c in range(n_blk):
            scr_ref.at[c][pl.ds(r, TM // d, stride=d), :] = blk_ref[r, :, c * LANES:(c + 1) * LANES].astype(F32)
    return jnp.concatenate([scr_ref[c] for c in range(n_blk)], axis=1)


def _class_specs(cols):
    return [pl.BlockSpec((TM, cols), lambda i: (i, 0))] + [
        pl.BlockSpec((d, TM // d, cols), lambda i: (0, i, 0)) for d in DILATIONS[1:]]


def _class_shapes(s, cols, dtype):
    return [jax.ShapeDtypeStruct((s, cols), dtype)] + [
        jax.ShapeDtypeStruct((d, s // d, cols), dtype) for d in DILATIONS[1:]]


def _rms_proj(x, g_mix, w_in_g, dep):
    s = x.shape[0]

    def body(x_ref, g_ref, w_ref, dep_ref, h_ref, q1, q4, q16, k1, k4, k16, v1, v4, v16, gb_ref, gc_ref, xi_ref, scr):
        xh, _ = _rms(x_ref[...])
        h = (xh * g_ref[...]).astype(BF16)
        h_ref[...] = h
        proj = jnp.concatenate([_dot(h, w_ref[j]) for j in range(N_DEV)], axis=1)
        _spread(proj[:, 0:512] * (HEAD_DIM ** -0.5), scr, (q1, q4, q16), BF16)
        _spread(proj[:, 512:1024], scr, (k1, k4, k16), BF16)
        _spread(proj[:, 1024:1536], scr, (v1, v4, v16), BF16)
        gb_ref[...] = proj[:, 1536:2048]
        gc_ref[...] = proj[:, 2048:2560]
        xi_ref[...] = proj[:, 2560:3072]

    row = lambda n: pl.BlockSpec((TM, n), lambda i: (i, 0))
    res = pl.pallas_call(
        body, name="rms_proj", grid=(s // TM,),
        out_shape=[jax.ShapeDtypeStruct((s, D_MODEL), BF16)] + _class_shapes(s, 512, BF16) * 3
        + [jax.ShapeDtypeStruct((s, 512), F32)] * 3,
        in_specs=[row(D_MODEL), _full(g_mix.shape), _full(w_in_g.shape), ANY_SPEC],
        out_specs=[row(D_MODEL)] + _class_specs(512) * 3 + [row(512)] * 3,
        scratch_shapes=[pltpu.VMEM((512 // LANES, TM, LANES), F32)],
        compiler_params=_cparams(1),
    )(x, g_mix, w_in_g, dep)
    return res[0], res[1:4], res[4:7], res[7:10], res[10], res[11], res[12]


def _band_mask(blk):
    qi = lax.broadcasted_iota(jnp.int32, (WIN, 2 * WIN), 0)
    kj = lax.broadcasted_iota(jnp.int32, (WIN, 2 * WIN), 1)
    steps = qi + WIN - kj
    return (steps >= 0) & (steps <= WIN) & (kj >= jnp.where(blk > 0, 0, WIN))


def _swa_fwd(qc, kc, vc, bias, dil):
    nb = qc.shape[1] // WIN

    def body(q_ref, kp_ref, kc_ref, vp_ref, vc_ref, b_ref, o_ref, lse_ref):
        valid = _band_mask(pl.program_id(1))
        for h in range(N_HEADS):
            sl = slice(h * HEAD_DIM, (h + 1) * HEAD_DIM)
            kh = jnp.concatenate([kp_ref[0, :, sl], kc_ref[0, :, sl]], axis=0)
            vh = jnp.concatenate([vp_ref[0, :, sl], vc_ref[0, :, sl]], axis=0)
            lg = _dot_nt(q_ref[0, :, sl], kh) + b_ref[h]
            lg = jnp.where(valid, lg, -jnp.inf)
            m = jnp.max(lg, axis=-1, keepdims=True)
            p = jnp.exp(lg - m)
            den = jnp.sum(p, axis=-1, keepdims=True)
            o_ref[0, :, sl] = _dot(p.astype(BF16), vh) / den
            lse_ref[0, :, sl] = jnp.broadcast_to(m + jnp.log(den), (WIN, HEAD_DIM))

    cur = pl.BlockSpec((1, WIN, 512), lambda r, b: (r, b, 0))
    prev = pl.BlockSpec((1, WIN, 512), lambda r, b: (r, jnp.maximum(b - 1, 0), 0))
    return pl.pallas_call(
        body, name=f"swa_fwd_d{dil}", grid=(dil, nb),
        out_shape=[jax.ShapeDtypeStruct(qc.shape, F32)] * 2,
        in_specs=[cur, prev, cur, prev, cur, _full(bias.shape)],
        out_specs=[cur] * 2,
        compiler_params=_cparams(2),
    )(qc, kc, kc, vc, vc, bias)


def _mix_out(branches, gb, gc, xi, x, w_sc, g_a, g_c, w_out):
    s = x.shape[0]
    tb = TM // SUBLANES

    def body(o1, l1, o4, l4, o16, l16, gb_ref, gc_ref, xi_ref, gch_ref, xih_ref, x_ref, wsc_ref,
             ga_ref, gcv_ref, wout_ref, attn_ref, lse1, lse4, lse16, mixed_ref, x1_ref, scr_a, scr_b, scr_c, scr_d):
        i = pl.program_id(0)
        la, lb, lc = l1[...], _gather_classes(l4, scr_a, 4), _gather_classes(l16, scr_b, 16)
        m_all = jnp.maximum(jnp.maximum(la, lb), lc)
        ea, eb, ec = jnp.exp(la - m_all), jnp.exp(lb - m_all), jnp.exp(lc - m_all)
        den = (ea + eb) + ec
        num = (ea * o1[...] + eb * _gather_classes(o4, scr_c, 4)) + ec * _gather_classes(o16, scr_d, 16)
        attn = num / den
        attn_ref[...] = attn
        _spread(m_all + jnp.log(den), scr_a, (lse1, lse4, lse16), F32)
        xa, _ = _rms(attn)
        u = gc_ref[...] * xi_ref[...]
        uh = jnp.where(i > 0, gch_ref[...] * xih_ref[...], 0.0)
        conv = gb_ref[...] * _causal_conv3(u, uh, wsc_ref)
        xc, _ = _rms(conv)
        mixed = jnp.concatenate([xa * ga_ref[...], xc * gcv_ref[...]], axis=1).astype(BF16)
        mixed_ref[...] = mixed
        x1_ref[...] = x_ref[...] + _dot(mixed, wout_ref[...])

    row = lambda n: pl.BlockSpec((TM, n), lambda i: (i, 0))
    halo = pl.BlockSpec((SUBLANES, 512), lambda i: (jnp.maximum(i * tb - 1, 0), 0))
    cs = _class_specs(512)
    flat = [a for br in branches for a in br]
    res = pl.pallas_call(
        body, name="mix_out", grid=(s // TM,),
        out_shape=[jax.ShapeDtypeStruct((s, 512), F32)] + _class_shapes(s, 512, F32)
        + [jax.ShapeDtypeStruct((s, D_MODEL), BF16), jax.ShapeDtypeStruct((s, D_MODEL), F32)],
        in_specs=[cs[0], cs[0], cs[1], cs[1], cs[2], cs[2], row(512), row(512), row(512), halo, halo,
                  row(D_MODEL), _full(w_sc.shape), _full(g_a.shape), _full(g_c.shape), _full(w_out.shape)],
        out_specs=[row(512)] + cs + [row(D_MODEL), row(D_MODEL)],
        scratch_shapes=[pltpu.VMEM((512 // LANES, TM, LANES), F32)] * 4,
        compiler_params=_cparams(1),
    )(*flat, gb, gc, xi, gc, xi, x, w_sc, g_a, g_c, w_out)
    return res[0], res[1:4], res[4], res[5]


def _mem_kv(mem, g_mem, w_xk, w_xv):
    def body(mem_ref, g_ref, wk_ref, wv_ref, mn_ref, k_ref, v_ref):
        xh, _ = _rms(mem_ref[...])
        mn = (xh * g_ref[...]).astype(BF16)
        mn_ref[...] = mn
        k_ref[...] = _dot(mn, wk_ref[...]).astype(BF16)
        v_ref[...] = _dot(mn, wv_ref[...]).astype(BF16)

    vm = pl.BlockSpec(memory_space=pltpu.VMEM)
    return pl.pallas_call(
        body, name="mem_kv",
        out_shape=[jax.ShapeDtypeStruct(mem.shape, BF16)] * 3,
        in_specs=[vm] * 4, out_specs=[vm] * 3,
        compiler_params=pltpu.CompilerParams(vmem_limit_bytes=VMEM_LIMIT),
    )(mem, g_mem, w_xk, w_xv)


def _xattn_fwd(x1, g, w_xq, k, v, w_xo):
    s = x1.shape[0]

    def body(x1_ref, g_ref, wq_ref, k_ref, v_ref, wo_ref, h2_ref, q_ref, o_ref, x2_ref):
        x1v = x1_ref[...]
        xh, _ = _rms(x1v)
        h2 = (xh * g_ref[...]).astype(BF16)
        h2_ref[...] = h2
        qb = _dot(h2, wq_ref[...]).astype(BF16)
        q_ref[...] = qb
        outs = []
        for h in range(N_MEM_HEADS):
            sl = slice(h * MEM_HEAD_DIM, (h + 1) * MEM_HEAD_DIM)
            lg = _dot_nt(qb[:, sl], k_ref[:, sl]) * (MEM_HEAD_DIM ** -0.5)
            p = jnp.exp(lg - jnp.max(lg, axis=-1, keepdims=True))
            p = p / jnp.sum(p, axis=-1, keepdims=True)
            outs.append(_dot(p.astype(BF16), v_ref[:, sl]))
        o = jnp.concatenate(outs, axis=1).astype(BF16)
        o_ref[...] = o
        x2_ref[...] = x1v + _dot(o, wo_ref[...])

    row = pl.BlockSpec((TM, D_MODEL), lambda i: (i, 0))
    return pl.pallas_call(
        body, name="xattn_fwd", grid=(s // TM,),
        out_shape=[jax.ShapeDtypeStruct((s, D_MODEL), BF16)] * 3 + [jax.ShapeDtypeStruct((s, D_MODEL), F32)],
        in_specs=[row, _full(g.shape), _full(w_xq.shape), _full(k.shape), _full(v.shape), _full(w_xo.shape)],
        out_specs=[row] * 4,
        compiler_params=_cparams(1),
    )(x1, g, w_xq, k, v, w_xo)


def _ffn_up(x2, g, w_up_g):
    s = x2.shape[0]

    def body(x_ref, g_ref, w_ref, h_ref, up_ref, h_scr):
        @pl.when(pl.program_id(1) == 0)
        def _():
            xh, _ = _rms(x_ref[...])
            h = (xh * g_ref[...]).astype(BF16)
            h_scr[...] = h
            h_ref[...] = h

        up_ref[0] = _dot(h_scr[...], w_ref[0])

    return pl.pallas_call(
        body, name="ffn_up", grid=(s // TM, N_DEV),
        out_shape=[jax.ShapeDtypeStruct((s, D_MODEL), BF16), jax.ShapeDtypeStruct((N_DEV, s, UP_CHUNK), F32)],
        in_specs=[pl.BlockSpec((TM, D_MODEL), lambda i, j: (i, 0)), _full(g.shape),
                  pl.BlockSpec((1, D_MODEL, UP_CHUNK), lambda i, j: (j, 0, 0))],
        out_specs=[pl.BlockSpec((TM, D_MODEL), lambda i, j: (i, 0)),
                   pl.BlockSpec((1, TM, UP_CHUNK), lambda i, j: (j, i, 0))],
        scratch_shapes=[pltpu.VMEM((TM, D_MODEL), BF16)],
        compiler_params=_cparams(2),
    )(x2, g, w_up_g)


def _ffn_conv(up_ref, uph_ref, wfc_ref, bfc_ref, i, j):
    u = up_ref[j]
    uh = jnp.where(i > 0, uph_ref[j], 0.0)
    u2 = _shift_down(u, uh, 2)
    u1 = _shift_down(u, uh, 1)
    w = wfc_ref[j]
    c = ((u2 * w[0:1, :] + u1 * w[1:2, :]) + u * w[2:3, :]) + bfc_ref[j]
    return c, u2, u1, u


def _ffn_tail(up, w_fc, b_fc, w_down_g, x2, g_final, target):
    s = x2.shape[0]
    tb = TM_FFN // SUBLANES
    half = N_DEV // 2

    def body(up_ref, uph_ref, wfc_ref, bfc_ref, wd_ref, x2_ref, gf_ref, t_ref, act_ref, dx3_ref, loss_ref, dgf_ref):
        i = pl.program_id(0)

        @pl.when(i == 0)
        def _():
            loss_ref[...] = jnp.zeros_like(loss_ref)
            dgf_ref[...] = jnp.zeros_like(dgf_ref)

        down = jnp.zeros((TM_FFN, D_MODEL), F32)
        for j in range(half):
            cg = _ffn_conv(up_ref, uph_ref, wfc_ref, bfc_ref, i, j)[0]
            cv = _ffn_conv(up_ref, uph_ref, wfc_ref, bfc_ref, i, j + half)[0]
            a = ((cg * _sigmoid(cg)) * cv).astype(BF16)
            act_ref[j] = a
            down = down + _dot(a, wd_ref[j])
        x3 = x2_ref[...] + down
        xh, r = _rms(x3)
        gf = gf_ref[...]
        e = xh * gf - t_ref[...]
        loss_ref[...] += 0.5 * jnp.sum(jnp.sum(e * e, axis=1, keepdims=True), axis=0, keepdims=True) / D_MODEL
        dy = e * (1.0 / D_MODEL)
        dgf_ref[0:1, :] += jnp.sum(dy * xh, axis=0, keepdims=True)
        dx3_ref[...] = _rms_bwd(xh, r, gf, dy)

    row = pl.BlockSpec((TM_FFN, D_MODEL), lambda i: (i, 0))
    cur = pl.BlockSpec((N_DEV, TM_FFN, UP_CHUNK), lambda i: (0, i, 0))
    halo = pl.BlockSpec((N_DEV, SUBLANES, UP_CHUNK), lambda i: (0, jnp.maximum(i * tb - 1, 0), 0))
    return pl.pallas_call(
        body, name="ffn_tail", grid=(s // TM_FFN,),
        out_shape=[jax.ShapeDtypeStruct((half, s, UP_CHUNK), BF16), jax.ShapeDtypeStruct((s, D_MODEL), F32),
                   jax.ShapeDtypeStruct((SUBLANES, 128), F32), jax.ShapeDtypeStruct((SUBLANES, D_MODEL), F32)],
        in_specs=[cur, halo, _full(w_fc.shape), _full(b_fc.shape), _full(w_down_g.shape), row,
                  _full(g_final.shape), row],
        out_specs=[pl.BlockSpec((half, TM_FFN, UP_CHUNK), lambda i: (0, i, 0)), row,
                   _full((SUBLANES, 128)), _full((SUBLANES, D_MODEL))],
        compiler_params=_cparams(1),
    )(up, up, w_fc, b_fc, w_down_g, x2, g_final, target)


def _ffn_down_bwd(dx3, up, w_fc, b_fc, w_down_g):
    s = dx3.shape[0]
    tb = TM_FFN // SUBLANES
    half = N_DEV // 2

    def body(dx3_ref, up_ref, uph_ref, wfc_ref, bfc_ref, wd_ref, dc_ref, dwfc_ref, dbfc_ref):
        i = pl.program_id(0)

        @pl.when(i == 0)
        def _():
            dwfc_ref[...] = jnp.zeros_like(dwfc_ref)
            dbfc_ref[...] = jnp.zeros_like(dbfc_ref)

        dxb = dx3_ref[...].astype(BF16)

        def small_grads(j, dc, u2, u1, u):
            dc_ref[j] = dc
            dbfc_ref[j:j + 1, :] += jnp.sum(dc, axis=0, keepdims=True)
            dwfc_ref[0, j:j + 1, :] += jnp.sum(dc * u2, axis=0, keepdims=True)
            dwfc_ref[1, j:j + 1, :] += jnp.sum(dc * u1, axis=0, keepdims=True)
            dwfc_ref[2, j:j + 1, :] += jnp.sum(dc * u, axis=0, keepdims=True)

        for j in range(half):
            dact = _dot_nt(dxb, wd_ref[j])
            cg, g2, g1, g0 = _ffn_conv(up_ref, uph_ref, wfc_ref, bfc_ref, i, j)
            cv, v2, v1, v0 = _ffn_conv(up_ref, uph_ref, wfc_ref, bfc_ref, i, j + half)
            sg = _sigmoid(cg)
            small_grads(j + half, dact * (cg * sg), v2, v1, v0)
            small_grads(j, (dact * cv) * (sg * (1.0 + cg * (1.0 - sg))), g2, g1, g0)

    row = pl.BlockSpec((TM_FFN, D_MODEL), lambda i: (i, 0))
    cur = pl.BlockSpec((N_DEV, TM_FFN, UP_CHUNK), lambda i: (0, i, 0))
    halo = pl.BlockSpec((N_DEV, SUBLANES, UP_CHUNK), lambda i: (0, jnp.maximum(i * tb - 1, 0), 0))
    return pl.pallas_call(
        body, name="ffn_down_bwd", grid=(s // TM_FFN,),
        out_shape=[jax.ShapeDtypeStruct((N_DEV, s, UP_CHUNK), F32), jax.ShapeDtypeStruct((3, N_DEV, UP_CHUNK), F32),
                   jax.ShapeDtypeStruct((N_DEV, UP_CHUNK), F32)],
        in_specs=[row, cur, halo, _full(w_fc.shape), _full(b_fc.shape), _full(w_down_g.shape)],
        out_specs=[cur, _full((3, N_DEV, UP_CHUNK)), _full((N_DEV, UP_CHUNK))],
        compiler_params=_cparams(1),
    )(dx3, up, up, w_fc, b_fc, w_down_g)


def _ffn_up_bwd(dc, w_fc, w_up_g, x2, g, dx3):
    s = x2.shape[0]
    tb = TM_FFN // SUBLANES
    last = s // SUBLANES - 1
    n_tiles = s // TM_FFN

    def body(dc_ref, dch_ref, wfc_ref, wup_ref, x2_ref, g_ref, dx3_ref, dup_ref, dx2_ref, dg_ref):
        i = pl.program_id(0)

        @pl.when(i == 0)
        def _():
            dg_ref[...] = jnp.zeros_like(dg_ref)

        dh = jnp.zeros((TM_FFN, D_MODEL), F32)
        for j in range(N_DEV):
            d0 = dc_ref[j]
            dn = jnp.where(i < n_tiles - 1, dch_ref[j], 0.0)
            w = wfc_ref[j]
            du = ((d0 * w[2:3, :] + _shift_up(d0, dn, 1) * w[1:2, :]) + _shift_up(d0, dn, 2) * w[0:1, :]).astype(BF16)
            dup_ref[j] = du
            dh = dh + _dot_nt(du, wup_ref[j])
        xh, r = _rms(x2_ref[...])
        dg_ref[0:1, :] += jnp.sum(dh * xh, axis=0, keepdims=True)
        dx2_ref[...] = dx3_ref[...] + _rms_bwd(xh, r, g_ref[...], dh)

    row = pl.BlockSpec((TM_FFN, D_MODEL), lambda i: (i, 0))
    cur = pl.BlockSpec((N_DEV, TM_FFN, UP_CHUNK), lambda i: (0, i, 0))
    nxt = pl.BlockSpec((N_DEV, SUBLANES, UP_CHUNK), lambda i: (0, jnp.minimum((i + 1) * tb, last), 0))
    return pl.pallas_call(
        body, name="ffn_up_bwd", grid=(n_tiles,),
        out_shape=[jax.ShapeDtypeStruct((N_DEV, s, UP_CHUNK), BF16), jax.ShapeDtypeStruct((s, D_MODEL), F32),
                   jax.ShapeDtypeStruct((SUBLANES, D_MODEL), F32)],
        in_specs=[cur, nxt, _full(w_fc.shape), _full(w_up_g.shape), row, _full(g.shape), row],
        out_specs=[cur, row, _full((SUBLANES, D_MODEL))],
        compiler_params=_cparams(1),
    )(dc, dc, w_fc, w_up_g, x2, g, dx3)


def _xattn_bwd(dx2, o, q, k, v, w_xo, w_xq, x1, g, dep):
    s = x1.shape[0]

    def body(dx2_ref, o_ref, q_ref, k_ref, v_ref, wo_ref, wq_ref, x1_ref, g_ref, dep_ref, dq_ref, dx1_ref, dk_ref,
             dv_ref, dg_ref):
        @pl.when(pl.program_id(0) == 0)
        def _():
            dk_ref[...] = jnp.zeros_like(dk_ref)
            dv_ref[...] = jnp.zeros_like(dv_ref)
            dg_ref[...] = jnp.zeros_like(dg_ref)

        dx2v = dx2_ref[...]
        do = _dot_nt(dx2v.astype(BF16), wo_ref[...])
        dqs = []
        for h in range(N_MEM_HEADS):
            sl = slice(h * MEM_HEAD_DIM, (h + 1) * MEM_HEAD_DIM)
            qh, kh, vh = q_ref[:, sl], k_ref[:, sl], v_ref[:, sl]
            lg = _dot_nt(qh, kh) * (MEM_HEAD_DIM ** -0.5)
            p = jnp.exp(lg - jnp.max(lg, axis=-1, keepdims=True))
            p = p / jnp.sum(p, axis=-1, keepdims=True)
            doh = do[:, sl].astype(BF16)
            dp = _dot_nt(doh, vh)
            ds = (p * (dp - jnp.sum(p * dp, axis=-1, keepdims=True)) * (MEM_HEAD_DIM ** -0.5)).astype(BF16)
            dqs.append(_dot(ds, kh))
            dk_ref[:, sl] += _dot_tn(ds, qh)
            dv_ref[:, sl] += _dot_tn(p.astype(BF16), doh)
        dq = jnp.concatenate(dqs, axis=1).astype(BF16)
        dq_ref[...] = dq
        dh2 = _dot_nt(dq, wq_ref[...])
        xh, r = _rms(x1_ref[...])
        dg_ref[0:1, :] += jnp.sum(dh2 * xh, axis=0, keepdims=True)
        dx1_ref[...] = dx2v + _rms_bwd(xh, r, g_ref[...], dh2)

    row = pl.BlockSpec((TM, D_MODEL), lambda i: (i, 0))
    return pl.pallas_call(
        body, name="xattn_bwd", grid=(s // TM,),
        out_shape=[jax.ShapeDtypeStruct((s, D_MODEL), BF16), jax.ShapeDtypeStruct((s, D_MODEL), F32),
                   jax.ShapeDtypeStruct(k.shape, F32), jax.ShapeDtypeStruct(k.shape, F32),
                   jax.ShapeDtypeStruct((SUBLANES, D_MODEL), F32)],
        in_specs=[row, row, row, _full(k.shape), _full(v.shape), _full(w_xo.shape), _full(w_xq.shape), row,
                  _full(g.shape), ANY_SPEC],
        out_specs=[row, row, _full(k.shape), _full(k.shape), _full((SUBLANES, D_MODEL))],
        compiler_params=_cparams(1),
    )(dx2, o, q, k, v, w_xo, w_xq, x1, g, dep)


def _mem_kv_bwd(dk, dv, mem_n, mem, w_xk, w_xv):
    def body(dk_ref, dv_ref, mn_ref, mem_ref, wk_ref, wv_ref, dwk_ref, dwv_ref, dg_ref):
        dkb, dvb = dk_ref[...].astype(BF16), dv_ref[...].astype(BF16)
        mn = mn_ref[...]
        dwk_ref[...] = _dot_tn(mn, dkb).astype(BF16)
        dwv_ref[...] = _dot_tn(mn, dvb).astype(BF16)
        dmn = _dot_nt(dkb, wk_ref[...]) + _dot_nt(dvb, wv_ref[...])
        xh, _ = _rms(mem_ref[...])
        dg_ref[...] = jnp.zeros_like(dg_ref)
        dg_ref[0:1, :] = jnp.sum(dmn * xh, axis=0, keepdims=True)

    vm = pl.BlockSpec(memory_space=pltpu.VMEM)
    return pl.pallas_call(
        body, name="mem_kv_bwd",
        out_shape=[jax.ShapeDtypeStruct(w_xk.shape, BF16), jax.ShapeDtypeStruct(w_xv.shape, BF16),
                   jax.ShapeDtypeStruct((SUBLANES, D_MODEL), F32)],
        in_specs=[vm] * 6, out_specs=[vm] * 3,
        compiler_params=pltpu.CompilerParams(vmem_limit_bytes=VMEM_LIMIT),
    )(dk, dv, mem_n, mem, w_xk, w_xv)


def _mix_out_bwd(dx1, w_out, attn, gb, gc, xi, w_sc, g_a, g_c, dep):
    s = dx1.shape[0]
    tb = TM // SUBLANES

    def body(dx1_ref, wout_ref, attn_ref, gb_ref, gc_ref, xi_ref, gch_ref, xih_ref, wsc_ref, ga_ref, gcv_ref, dep_ref,
             da1, da4, da16, dd1, dd4, dd16, dgb_ref, dcv_ref, dga_ref, dgc_ref, dwsc_ref, scr):
        i = pl.program_id(0)

        @pl.when(i == 0)
        def _():
            dga_ref[...] = jnp.zeros_like(dga_ref)
            dgc_ref[...] = jnp.zeros_like(dgc_ref)
            dwsc_ref[...] = jnp.zeros_like(dwsc_ref)

        dmixed = _dot_nt(dx1_ref[...].astype(BF16), wout_ref[...])
        da, dcn = dmixed[:, :ATTN_W], dmixed[:, ATTN_W:]
        attn = attn_ref[...]
        xa, ra = _rms(attn)
        dga_ref[0:1, :] += jnp.sum(da * xa, axis=0, keepdims=True)
        dattn = _rms_bwd(xa, ra, ga_ref[...], da)
        _spread(dattn, scr, (da1, da4, da16), F32)
        prod = dattn * attn
        dd = jnp.concatenate(
            [jnp.broadcast_to(jnp.sum(prod[:, h * HEAD_DIM:(h + 1) * HEAD_DIM], axis=-1, keepdims=True),
                              (TM, HEAD_DIM)) for h in range(N_HEADS)], axis=1)
        _spread(dd, scr, (dd1, dd4, dd16), F32)
        gbv = gb_ref[...]
        u = gc_ref[...] * xi_ref[...]
        uh = jnp.where(i > 0, gch_ref[...] * xih_ref[...], 0.0)
        u2, u1 = _shift_down(u, uh, 2), _shift_down(u, uh, 1)
        cv = (u2 * wsc_ref[0:1, :] + u1 * wsc_ref[1:2, :]) + u * wsc_ref[2:3, :]
        xc, rc = _rms(gbv * cv)
        dgc_ref[0:1, :] += jnp.sum(dcn * xc, axis=0, keepdims=True)
        dconv = _rms_bwd(xc, rc, gcv_ref[...], dcn)
        dgb_ref[...] = dconv * cv
        dcv = dconv * gbv
        dcv_ref[...] = dcv
        dwsc_ref[0:1, :] += jnp.sum(dcv * u2, axis=0, keepdims=True)
        dwsc_ref[1:2, :] += jnp.sum(dcv * u1, axis=0, keepdims=True)
        dwsc_ref[2:3, :] += jnp.sum(dcv * u, axis=0, keepdims=True)

    row = lambda n: pl.BlockSpec((TM, n), lambda i: (i, 0))
    halo = pl.BlockSpec((SUBLANES, 512), lambda i: (jnp.maximum(i * tb - 1, 0), 0))
    acc = _full((SUBLANES, 512))
    res = pl.pallas_call(
        body, name="mix_out_bwd", grid=(s // TM,),
        out_shape=_class_shapes(s, 512, F32) * 2 + [jax.ShapeDtypeStruct((s, 512), F32)] * 2
        + [jax.ShapeDtypeStruct((SUBLANES, 512), F32)] * 3,
        in_specs=[row(D_MODEL), _full(w_out.shape), row(512), row(512), row(512), row(512), halo, halo,
                  _full(w_sc.shape), _full(g_a.shape), _full(g_c.shape), ANY_SPEC],
        out_specs=_class_specs(512) * 2 + [row(512)] * 2 + [acc] * 3,
        scratch_shapes=[pltpu.VMEM((512 // LANES, TM, LANES), F32)],
        compiler_params=_cparams(1),
    )(dx1, w_out, attn, gb, gc, xi, gc, xi, w_sc, g_a, g_c, dep)
    return res[0:3], res[3:6], res[6], res[7], res[8], res[9], res[10]


def _swa_bwd(qc, kc, vc, doc, lsec, ddc, bias, dil, dep):
    nb = qc.shape[1] // WIN

    def body(q_ref, qn_ref, kp_ref, kc_ref, vp_ref, vc_ref, do_ref, don_ref, lse_ref, lsen_ref, dd_ref, ddn_ref,
             b_ref, dep_ref, dq_ref, dk_ref, dv_ref, db_ref):
        r, b = pl.program_id(0), pl.program_id(1)

        @pl.when((r == 0) & (b == 0))
        def _():
            db_ref[...] = jnp.zeros_like(db_ref)

        valid = _band_mask(b)
        qi = lax.broadcasted_iota(jnp.int32, (WIN, WIN), 0)
        kj = lax.broadcasted_iota(jnp.int32, (WIN, WIN), 1)
        valid_n = kj >= qi + jnp.where(b + 1 < nb, 0, WIN)
        for h in range(N_HEADS):
            sl = slice(h * HEAD_DIM, (h + 1) * HEAD_DIM)
            col = slice(h * HEAD_DIM, h * HEAD_DIM + 1)
            qh, kc_h, vc_h = q_ref[0, :, sl], kc_ref[0, :, sl], vc_ref[0, :, sl]
            kh = jnp.concatenate([kp_ref[0, :, sl], kc_h], axis=0)
            vh = jnp.concatenate([vp_ref[0, :, sl], vc_h], axis=0)
            doh = do_ref[0, :, sl].astype(BF16)
            lg = jnp.where(valid, _dot_nt(qh, kh) + b_ref[h], -jnp.inf)
            p = jnp.exp(lg - lse_ref[0, :, col])
            ds = p * (_dot_nt(doh, vh) - dd_ref[0, :, col])
            db_ref[h] += ds
            dsb = ds.astype(BF16)
            dq_ref[0, :, sl] = _dot(dsb, kh)
            dk = _dot_tn(dsb[:, WIN:], qh)
            dv = _dot_tn(p[:, WIN:].astype(BF16), doh)
            qn = qn_ref[0, :, sl]
            don = don_ref[0, :, sl].astype(BF16)
            lgn = jnp.where(valid_n, _dot_nt(qn, kc_h) + b_ref[h][:, :WIN], -jnp.inf)
            pn = jnp.exp(lgn - lsen_ref[0, :, col])
            dsn = pn * (_dot_nt(don, vc_h) - ddn_ref[0, :, col])
            dk_ref[0, :, sl] = dk + _dot_tn(dsn.astype(BF16), qn)
            dv_ref[0, :, sl] = dv + _dot_tn(pn.astype(BF16), don)

    cur = pl.BlockSpec((1, WIN, 512), lambda r, b: (r, b, 0))
    prev = pl.BlockSpec((1, WIN, 512), lambda r, b: (r, jnp.maximum(b - 1, 0), 0))
    nxt = pl.BlockSpec((1, WIN, 512), lambda r, b: (r, jnp.minimum(b + 1, nb - 1), 0))
    return pl.pallas_call(
        body, name=f"swa_bwd_d{dil}", grid=(dil, nb),
        out_shape=[jax.ShapeDtypeStruct(qc.shape, F32)] * 3 + [jax.ShapeDtypeStruct(bias.shape, F32)],
        in_specs=[cur, nxt, prev, cur, prev, cur, cur, nxt, cur, nxt, cur, nxt, _full(bias.shape), ANY_SPEC],
        out_specs=[cur] * 3 + [_full(bias.shape)],
        compiler_params=_cparams(2),
    )(qc, qc, kc, kc, vc, vc, doc, doc, lsec, lsec, ddc, ddc, bias, dep)


def _in_proj_bwd(dqs, dks, dvs, dgb, dcv, gc, xi, w_sc, w_in_g, x, g_mix, dx1):
    s = x.shape[0]
    tb = TM // SUBLANES
    last = s // SUBLANES - 1
    n_tiles = s // TM

    def body(dq1, dq4, dq16, dk1, dk4, dk16, dv1, dv4, dv16, dgb_ref, dcv_ref, dcvn_ref, gc_ref, xi_ref, wsc_ref,
             win_ref, x_ref, g_ref, dx1_ref, dproj_ref, gx_ref, dg_ref, scr_a, scr_b):
        i = pl.program_id(0)

        @pl.when(i == 0)
        def _():
            dg_ref[...] = jnp.zeros_like(dg_ref)

        d0 = dcv_ref[...]
        dn = jnp.where(i < n_tiles - 1, dcvn_ref[...], 0.0)
        du = (d0 * wsc_ref[2:3, :] + _shift_up(d0, dn, 1) * wsc_ref[1:2, :]) + _shift_up(d0, dn, 2) * wsc_ref[0:1, :]
        merge = lambda a, b4, b16: (a[...] + _gather_classes(b4, scr_a, 4)) + _gather_classes(b16, scr_b, 16)
        dq = merge(dq1, dq4, dq16) * (HEAD_DIM ** -0.5)
        dk = merge(dk1, dk4, dk16)
        dv = merge(dv1, dv4, dv16)
        dproj = jnp.concatenate([dq, dk, dv, dgb_ref[...], du * xi_ref[...], du * gc_ref[...]], axis=1).astype(BF16)
        dproj_ref[...] = dproj
        dh = jnp.zeros((TM, D_MODEL), F32)
        for j in range(N_DEV):
            dh = dh + _dot_nt(dproj[:, j * IN_CHUNK:(j + 1) * IN_CHUNK], win_ref[j])
        xh, r = _rms(x_ref[...])
        dg_ref[0:1, :] += jnp.sum(dh * xh, axis=0, keepdims=True)
        gx_ref[...] = dx1_ref[...] + _rms_bwd(xh, r, g_ref[...], dh)

    row = lambda n: pl.BlockSpec((TM, n), lambda i: (i, 0))
    nxt = pl.BlockSpec((SUBLANES, 512), lambda i: (jnp.minimum((i + 1) * tb, last), 0))
    return pl.pallas_call(
        body, name="in_proj_bwd", grid=(n_tiles,),
        out_shape=[jax.ShapeDtypeStruct((s, IN_COLS), BF16), jax.ShapeDtypeStruct((s, D_MODEL), F32),
                   jax.ShapeDtypeStruct((SUBLANES, D_MODEL), F32)],
        in_specs=_class_specs(512) * 3 + [row(512), row(512), nxt, row(512), row(512), _full(w_sc.shape),
                                          _full(w_in_g.shape), row(D_MODEL), _full(g_mix.shape), row(D_MODEL)],
        out_specs=[row(IN_COLS), row(D_MODEL), _full((SUBLANES, D_MODEL))],
        scratch_shapes=[pltpu.VMEM((512 // LANES, TM, LANES), F32)] * 2,
        compiler_params=_cparams(1),
    )(*dqs, *dks, *dvs, dgb, dcv, dcv, gc, xi, w_sc, w_in_g, x, g_mix, dx1)


def _dw(a, b, dep, name, a_chunked=False, b_chunked=False, n_chunks=1, chunk_cols=None):
    ts = TS_DW
    if a_chunked:
        nj, s, kk = a.shape
        nn = b.shape[1]
        a_spec = pl.BlockSpec((1, ts, kk), lambda j, t: (j, t, 0))
        b_spec = pl.BlockSpec((ts, nn), lambda j, t: (t, 0))
    elif b_chunked:
        nj, s, nn = b.shape
        kk = a.shape[1]
        a_spec = pl.BlockSpec((ts, kk), lambda j, t: (t, 0))
        b_spec = pl.BlockSpec((1, ts, nn), lambda j, t: (j, t, 0))
    else:
        s, kk = a.shape
        nj, nn = (n_chunks, chunk_cols) if chunk_cols else (1, b.shape[1])
        a_spec = pl.BlockSpec((ts, kk), lambda j, t: (t, 0))
        b_spec = pl.BlockSpec((ts, nn), lambda j, t: (t, j))
    n_steps = s // ts

    def body(a_ref, b_ref, dep_ref, o_ref, acc):
        t = pl.program_id(1)

        @pl.when(t == 0)
        def _():
            acc[...] = jnp.zeros_like(acc)

        av = (a_ref[0] if a_chunked else a_ref[...]).astype(BF16)
        bv = (b_ref[0] if b_chunked else b_ref[...]).astype(BF16)
        acc[...] += _dot_tn(av, bv)

        @pl.when(t == n_steps - 1)
        def _():
            o_ref[0] = acc[...].astype(BF16)

    return pl.pallas_call(
        body, name=name, grid=(nj, n_steps),
        out_shape=jax.ShapeDtypeStruct((nj, kk, nn), BF16),
        in_specs=[a_spec, b_spec, ANY_SPEC],
        out_specs=pl.BlockSpec((1, kk, nn), lambda j, t: (j, 0, 0)),
        scratch_shapes=[pltpu.VMEM((kk, nn), F32)],
        compiler_params=_cparams(2),
    )(a, b, dep)


def _adamw_math(w, g, m, v):
    m2 = ADAM_B1 * m + (1.0 - ADAM_B1) * g
    v2 = ADAM_B2 * v + (1.0 - ADAM_B2) * (g * g)
    m_hat = m2 / (1.0 - ADAM_B1 ** ADAM_STEP)
    v_hat = v2 / (1.0 - ADAM_B2 ** ADAM_STEP)
    delta = -ADAM_LR * (m_hat / (jnp.sqrt(v_hat) + ADAM_EPS) + ADAM_WD * w)
    return delta, m2, v2


def _sum_parts(me, own, p_ref):
    g = None
    for i in range(N_DEV):
        part = jnp.where(me == i, own.astype(F32), p_ref[i].astype(F32))
        g = part if g is None else g + part
    return g


def _adamw_big(name, w, own, parts, m, v, me_arr):
    rr, cc = w.shape
    tr = rr // 4 if rr >= 512 else rr

    def body(me_ref, w_ref, own_ref, p_ref, m_ref, v_ref, g_ref, d_ref, nm_ref, nv_ref):
        g = _sum_parts(me_ref[0], own_ref[...], p_ref)
        g_ref[...] = g
        d_ref[...], nm_ref[...], nv_ref[...] = _adamw_math(w_ref[...], g, m_ref[...], v_ref[...])

    row = pl.BlockSpec((tr, cc), lambda i: (i, 0))
    return pl.pallas_call(
        body, name=name, grid=(rr // tr,),
        out_shape=[jax.ShapeDtypeStruct((rr, cc), F32)] * 4,
        in_specs=[SMEM_SPEC, row, row, pl.BlockSpec((N_DEV, tr, cc), lambda i: (0, i, 0)), row, row],
        out_specs=[row] * 4,
        compiler_params=_cparams(1),
    )(me_arr, w, own, parts, m, v)


def _small_slices():
    return [
        (slice(ROW_RELB, ROW_RELB + 8), slice(0, N_BUCKETS)),
        (slice(ROW_GMIX, ROW_GMIX + 1), slice(0, D_MODEL)),
        (slice(ROW_GAC, ROW_GAC + 1), slice(0, ATTN_W)),
        (slice(ROW_GAC, ROW_GAC + 1), slice(ATTN_W, D_MODEL)),
        (slice(ROW_GXATTN, ROW_GXATTN + 1), slice(0, D_MODEL)),
        (slice(ROW_GMEM, ROW_GMEM + 1), slice(0, D_MODEL)),
        (slice(ROW_GFFN, ROW_GFFN + 1), slice(0, D_MODEL)),
        (slice(ROW_BFC, ROW_BFC + 8), slice(0, UP_CHUNK)),
        (slice(ROW_GFINAL, ROW_GFINAL + 1), slice(0, D_MODEL)),
    ]


def _adamw_small(own, parts, wmv, me_arr):
    slices = _small_slices()
    n = len(slices)

    def body(*refs):
        me_ref, own_ref, p_ref = refs[:3]
        ins = refs[3:3 + 3 * n]
        g_ref = refs[3 + 3 * n]
        outs = refs[4 + 3 * n:]
        g = _sum_parts(me_ref[0], own_ref[...], p_ref)
        g_ref[...] = g
        for a, (rs, ls) in enumerate(slices):
            ga = g[rs, ls]
            outs[4 * a][...] = ga
            outs[4 * a + 1][...], outs[4 * a + 2][...], outs[4 * a + 3][...] = _adamw_math(
                ins[3 * a][...], ga, ins[3 * a + 1][...], ins[3 * a + 2][...])

    vm = pl.BlockSpec(memory_space=pltpu.VMEM)
    flat = [t for trip in wmv for t in trip]
    out_shape = [jax.ShapeDtypeStruct((SMALL_ROWS, D_MODEL), F32)]
    for w, _, _ in wmv:
        out_shape += [jax.ShapeDtypeStruct(w.shape, F32)] * 4
    res = pl.pallas_call(
        body, name="adamw_small", out_shape=out_shape,
        in_specs=[SMEM_SPEC] + [vm] * (2 + 3 * n), out_specs=[vm] * len(out_shape),
    )(me_arr, own, parts, *flat)
    return res[0], [res[1 + 4 * a:5 + 4 * a] for a in range(n)]


def _adamw_shards(items):
    n = len(items)

    def body(*refs):
        for a in range(n):
            w_ref, g_ref, m_ref, v_ref = refs[4 * a:4 * a + 4]
            d_ref, nm_ref, nv_ref = refs[4 * n + 3 * a:4 * n + 3 * a + 3]
            d_ref[...], nm_ref[...], nv_ref[...] = _adamw_math(w_ref[...], g_ref[...], m_ref[...], v_ref[...])

    vm = pl.BlockSpec(memory_space=pltpu.VMEM)
    out_shape = []
    for w, _, _, _ in items:
        out_shape += [jax.ShapeDtypeStruct(w.shape, F32)] * 3
    res = pl.pallas_call(
        body, name="adamw_shards", out_shape=out_shape, in_specs=[vm] * (4 * n), out_specs=[vm] * (3 * n),
    )(*[t for it in items for t in it])
    return [res[3 * a:3 * a + 3] for a in range(n)]


def _mesh_pos():
    return lax.axis_index("x"), lax.axis_index("y"), lax.axis_index("c")


def _dev_index(p):
    return 4 * p[0] + 2 * p[1] + p[2]


def _all_gather(shards):
    n = len(shards)

    def body(*refs):
        ins, outs = refs[:n], refs[n:2 * n]
        send_sems, recv_sems, loc_sems = refs[2 * n:]
        x, y, c = _mesh_pos()
        me, sib = (x, y, c), (x, y, 1 - c)
        chips = [(1 - x, y), (x, 1 - y), (1 - x, 1 - y)]

        def cp(a, k, block, to, src=None):
            dst = outs[a].at[_dev_index(block)]
            return pltpu.make_async_remote_copy(
                src_ref=dst if src is None else src, dst_ref=dst, send_sem=send_sems.at[a, k],
                recv_sem=recv_sems.at[a, k], device_id=to, device_id_type=MESH)

        mine = [pltpu.make_async_copy(ins[a], outs[a].at[_dev_index(me)], loc_sems.at[a]) for a in range(n)]
        for m_ in mine:
            m_.start()
        first = []
        for a in range(n):
            first.append(cp(a, 0, me, sib, src=ins[a]))
            first += [cp(a, 1 + j, me, (*chip, c), src=ins[a]) for j, chip in enumerate(chips)]
        for f in first:
            f.start()
        passed = []
        for a in range(n):
            for j, chip in enumerate(chips):
                cp(a, 1 + j, (*chip, c), me).wait_recv()
                fwd = cp(a, 4 + j, (*chip, c), sib)
                fwd.start()
                passed.append(fwd)
        for a in range(n):
            cp(a, 0, sib, me).wait_recv()
            for j, chip in enumerate(chips):
                cp(a, 4 + j, (*chip, 1 - c), me).wait_recv()
        for f in first + passed:
            f.wait_send()
        for m_ in mine:
            m_.wait()

    hbm = pl.BlockSpec(memory_space=pltpu.HBM)
    return pl.pallas_call(
        body, name="all_gather_weights",
        out_shape=[jax.ShapeDtypeStruct((N_DEV,) + a.shape, a.dtype) for a in shards],
        in_specs=[hbm] * n, out_specs=[hbm] * n,
        scratch_shapes=[pltpu.SemaphoreType.DMA((n, 7)), pltpu.SemaphoreType.DMA((n, 7)),
                        pltpu.SemaphoreType.DMA((n,))],
    )(*shards)


def _peers():
    x, y, c = _mesh_pos()
    return (x, y, c), [((1 - x) if k & 4 else x, (1 - y) if k & 2 else y, (1 - c) if k & 1 else c)
                       for k in range(1, 8)]


def _exchange_copy(src_ref, land_ref, whole, send_sems, recv_sems, a, k, peer, slot):
    src = src_ref if whole else src_ref.at[_dev_index(peer)]
    return pltpu.make_async_remote_copy(
        src_ref=src, dst_ref=land_ref.at[slot], send_sem=send_sems.at[7 * a + k], recv_sem=recv_sems.at[7 * a + k],
        device_id=peer, device_id_type=MESH)


def _exchange_start(name, srcs, whole, dep):
    n = len(srcs)
    lands = [lax.empty(((N_DEV,) + s.shape) if w else s.shape, s.dtype) for s, w in zip(srcs, whole)]

    def body(*refs):
        src_refs, land_refs = refs[:n], refs[n:2 * n]
        send_sems, recv_sems, token = refs[2 * n + 1], refs[2 * n + 2], refs[-1]
        me, peers = _peers()
        for a in range(n):
            for k, peer in enumerate(peers):
                _exchange_copy(src_refs[a], land_refs[a], whole[a], send_sems, recv_sems, a, k, peer,
                               _dev_index(me)).start()
        token[...] = jnp.zeros_like(token)

    res = pl.pallas_call(
        body, name=name,
        out_shape=(pltpu.SemaphoreType.DMA((7 * n,)), pltpu.SemaphoreType.DMA((7 * n,)),
                   *[pltpu.HBM(a.shape, a.dtype) for a in srcs], *[pltpu.HBM(a.shape, a.dtype) for a in lands],
                   jax.ShapeDtypeStruct((SUBLANES, 128), F32)),
        in_specs=[HBM_SPEC] * (2 * n) + [ANY_SPEC],
        out_specs=(SEM_SPEC, SEM_SPEC, *([HBM_SPEC] * (2 * n)), VMEM_SPEC),
        input_output_aliases={i: 2 + i for i in range(2 * n)},
        compiler_params=pltpu.CompilerParams(has_side_effects=DATAFLOW),
    )(*[pltpu.with_memory_space_constraint(a, pltpu.HBM) for a in srcs],
      *[pltpu.with_memory_space_constraint(a, pltpu.HBM) for a in lands], dep)
    return res[0], res[1], list(res[2:2 + n]), list(res[2 + n:2 + 2 * n]), res[-1]


def _exchange_wait(name, started, whole, after, which=None):
    send_sems, recv_sems, srcs, lands, _ = started
    which = list(range(len(srcs))) if which is None else which
    srcs, lands = [srcs[a] for a in which], [lands[a] for a in which]
    n = len(srcs)

    def body(*refs):
        src_refs, land_refs = refs[:n], refs[n:2 * n]
        send_sems, recv_sems = refs[2 * n], refs[2 * n + 1]
        _, peers = _peers()
        for i, a in enumerate(which):
            for k, peer in enumerate(peers):
                cp = _exchange_copy(src_refs[i], land_refs[i], whole[a], send_sems, recv_sems, a, k, peer,
                                    _dev_index(peer))
                cp.wait_send()
                cp.wait_recv()

    res = pl.pallas_call(
        body, name=name,
        out_shape=[pltpu.HBM(a.shape, a.dtype) for a in srcs + lands],
        in_specs=[HBM_SPEC] * (2 * n) + [SEM_SPEC, SEM_SPEC, ANY_SPEC],
        out_specs=[HBM_SPEC] * (2 * n),
        input_output_aliases={i: i for i in range(2 * n)},
        compiler_params=pltpu.CompilerParams(has_side_effects=DATAFLOW),
    )(*srcs, *lands, send_sems, recv_sems, after)
    return list(res[n:])


def _local_step(x, mem, target, rel_bias, g_mix, w_in_g, w_sc, g_a, g_c, g_xattn, g_mem, g_ffn, w_fc, b_fc, g_final,
                dep, late_weights, emit, emit_small):
    s = x.shape[0]
    buckets = _bucket_tables()
    bias = _bias_fwd(rel_bias, buckets)

    h1, qs, ks, vs, gb, gc, xi = _rms_proj(x, g_mix, w_in_g, dep)
    qs, ks, vs = ([a[0][None]] + list(a[1:]) for a in (qs, ks, vs))
    branches = []
    for p, dil in enumerate(DILATIONS):
        o_p, lse_p = _swa_fwd(qs[p], ks[p], vs[p], bias[p], dil)
        branches.append([o_p[0], lse_p[0]] if dil == 1 else [o_p, lse_p])
    w_out = late_weights(["w_out"], branches[-1][0])["w_out"]
    attn, lses, mixed, x1 = _mix_out(branches, gb, gc, xi, x, w_sc, g_a, g_c, w_out)
    lw = late_weights(["w_xq", "w_xk", "w_xv", "w_xo"], x1)
    w_xq, w_xk, w_xv, w_xo = lw["w_xq"], lw["w_xk"], lw["w_xv"], lw["w_xo"]
    mem_n, mk, mv = _mem_kv(mem, g_mem, w_xk, w_xv)
    h2, xq, xo, x2 = _xattn_fwd(x1, g_xattn, w_xq, mk, mv, w_xo)
    w_up_g = late_weights(["w_up"], x2)["w_up"]
    h3, up = _ffn_up(x2, g_ffn, w_up_g)
    w_down_g = late_weights(["w_down"], up)["w_down"]
    act, dx3, loss_acc, dg_final = _ffn_tail(up, w_fc, b_fc, w_down_g, x2, g_final, target)

    dc, dw_fc, db_fc = _ffn_down_bwd(dx3, up, w_fc, b_fc, w_down_g)
    gw_down = _dw(act, dx3, dep, "dw_down", a_chunked=True)
    dup, dx2, dg_ffn = _ffn_up_bwd(dc, w_fc, w_up_g, x2, g_ffn, dx3)
    gw_up = _dw(h3, dup, dep, "dw_up", b_chunked=True)
    tok = emit(dict(w_down=gw_down, w_up=gw_up))
    dxq, dx1, dmk, dmv, dg_xattn = _xattn_bwd(dx2, xo, xq, mk, mv, w_xo, w_xq, x1, g_xattn, tok)
    gw_xo = _dw(xo, dx2, tok, "dw_xo")[0]
    gw_xq = _dw(h2, dxq, tok, "dw_xq")[0]
    gw_xk, gw_xv, dg_mem = _mem_kv_bwd(dmk, dmv, mem_n, mem, w_xk, w_xv)
    tok = emit(dict(w_xo=gw_xo, w_xq=gw_xq, w_xk=gw_xk, w_xv=gw_xv))
    dattns, dds, dgb, dcv, dg_a, dg_c, dw_sc = _mix_out_bwd(dx1, w_out, attn, gb, gc, xi, w_sc, g_a, g_c, tok)
    first = lambda a: [a[0][None]] + list(a[1:])
    dattns, dds, lses = first(dattns), first(dds), first(lses)
    gw_out = _dw(mixed, dx1, tok, "dw_out")[0]
    tok = emit(dict(w_out=gw_out))
    dqs, dks, dvs, dbias = [], [], [], []
    for p, dil in enumerate(DILATIONS):
        dq_p, dk_p, dv_p, db_p = _swa_bwd(qs[p], ks[p], vs[p], dattns[p], lses[p], dds[p], bias[p], dil, tok)
        dqs.append(dq_p[0] if dil == 1 else dq_p)
        dks.append(dk_p[0] if dil == 1 else dk_p)
        dvs.append(dv_p[0] if dil == 1 else dv_p)
        dbias.append(db_p)
    d_relb = _bias_bwd(jnp.stack(dbias), buckets)
    dproj, grad_x, dg_mix = _in_proj_bwd(dqs, dks, dvs, dgb, dcv, gc, xi, w_sc, w_in_g, x, g_mix, dx1)
    pad = lambda a: jnp.pad(a, ((0, 0), (0, D_MODEL - a.shape[1])))
    small = jnp.concatenate([
        d_relb, dg_mix, dg_xattn, dg_mem, dg_ffn, dg_final, jnp.concatenate([dg_a, dg_c], axis=1),
        pad(dw_sc), pad(db_fc), pad(dw_fc.reshape(3 * N_DEV, UP_CHUNK))], axis=0)
    tok = emit_small(small)
    gw_in = _dw(h1, dproj, tok, "dw_in", n_chunks=N_DEV, chunk_cols=IN_CHUNK)
    emit(dict(w_in=gw_in))
    return loss_acc[0, 0], grad_x


def kernel(x, mem, rel_bias, g_mix, w_in, w_short_conv, g_attn_out, g_conv_out, w_out, g_xattn, g_mem, w_xq, w_xk, w_xv, w_xo, g_ffn, w_up, w_ffn_conv, b_ffn_conv, w_down, g_final, loss_target, m_rel_bias, m_g_mix, m_w_in, m_w_short_conv, m_g_attn_out, m_g_conv_out, m_w_out, m_g_xattn, m_g_mem, m_w_xq, m_w_xk, m_w_xv, m_w_xo, m_g_ffn, m_w_up, m_w_ffn_conv, m_b_ffn_conv, m_w_down, m_g_final, v_rel_bias, v_g_mix, v_w_in, v_w_short_conv, v_g_attn_out, v_g_conv_out, v_w_out, v_g_xattn, v_g_mem, v_w_xq, v_w_xk, v_w_xv, v_w_xo, v_g_ffn, v_w_up, v_w_ffn_conv, v_b_ffn_conv, v_w_down, v_g_final):
    me = _dev_index(_mesh_pos())
    me_arr = me.reshape(1).astype(jnp.int32)

    big_names = ["w_in", "w_out", "w_xq", "w_xk", "w_xv", "w_xo", "w_up", "w_down"]
    late_names = big_names[1:]
    big_w = dict(w_in=w_in[0], w_out=w_out[0], w_xq=w_xq[0], w_xk=w_xk[0], w_xv=w_xv[0], w_xo=w_xo[0],
                 w_up=w_up[0], w_down=w_down[0])
    big_m = dict(w_in=m_w_in[0], w_out=m_w_out[0], w_xq=m_w_xq[0], w_xk=m_w_xk[0], w_xv=m_w_xv[0], w_xo=m_w_xo[0],
                 w_up=m_w_up[0], w_down=m_w_down[0])
    big_v = dict(w_in=v_w_in[0], w_out=v_w_out[0], w_xq=v_w_xq[0], w_xk=v_w_xk[0], w_xv=v_w_xv[0], w_xo=v_w_xo[0],
                 w_up=v_w_up[0], w_down=v_w_down[0])
    shard_shape = {n: big_w[n].shape for n in big_names}

    w_in_g, w_sc_g, w_fc_full = _all_gather([big_w["w_in"].astype(BF16), w_short_conv[0], w_ffn_conv[0]])
    w_sc_full = w_sc_g.transpose(1, 0, 2).reshape(3, CONV_W)
    late_shards = [big_w[n].astype(BF16) for n in late_names]
    ag = _exchange_start("gather_weights_start", late_shards, [True] * len(late_names), w_in_g)

    def late_weights(names, after):
        which = [late_names.index(n) for n in names]
        lands = _exchange_wait("gather_" + "_".join(names) + "_wait", ag, [True] * len(late_names), after, which)
        out = {}
        for n, a, land in zip(names, which, lands):
            full = lax.dynamic_update_index_in_dim(land, late_shards[a], me, 0)
            if n == "w_up":
                out[n] = full
            elif n == "w_down":
                out[n] = full.reshape(N_DEV // 2, UP_CHUNK, D_MODEL)
            else:
                out[n] = full.reshape(D_MODEL, D_MODEL)
        return out

    sent = []

    def emit(grads):
        names = list(grads)
        blocks = [grads[n].reshape((N_DEV,) + shard_shape[n]) for n in names]
        own = [lax.dynamic_index_in_dim(b, me, 0, keepdims=False) for b in blocks]
        started = _exchange_start("scatter_" + "_".join(names) + "_start", blocks, [False] * len(names), me_arr)
        sent.append((names, own, started))
        return started[-1]

    def emit_small(small):
        sent_small.append((small, _exchange_start("gather_small_start", [small], [True], me_arr)))
        return sent_small[0][1][-1]

    sent_small = []
    loss_part, grad_x = _local_step(
        x[0], mem[0], loss_target[0], rel_bias, g_mix, w_in_g, w_sc_full, g_attn_out, g_conv_out, g_xattn, g_mem,
        g_ffn, w_fc_full, b_ffn_conv.reshape(N_DEV, 1, UP_CHUNK), g_final.reshape(1, D_MODEL), ag[-1],
        late_weights, emit, emit_small)
    loss = lax.psum(loss_part, ("x", "y", "c"))

    small_g, small_started = sent_small[0]
    after = sent[-1][2][-1]
    small_parts = _exchange_wait("gather_small_wait", small_started, [True], after)[0]
    big_out = {}
    after = small_parts
    for names, own, started in sent:
        lands = _exchange_wait("scatter_" + "_".join(names) + "_wait", started, [False] * len(names), after)
        for n, own_n, land in zip(names, own, lands):
            res = _adamw_big("adamw_" + n, big_w[n], own_n, land, big_m[n], big_v[n], me_arr)
            big_out[n] = [r[None] for r in res]
            after = res[0]

    as_rows = lambda a: a.reshape(N_DEV, UP_CHUNK)
    row1 = lambda a: a.reshape(1, D_MODEL)
    small_names = ["rel_bias", "g_mix", "g_attn_out", "g_conv_out", "g_xattn", "g_mem", "g_ffn", "b_ffn_conv", "g_final"]
    wmv = [
        (rel_bias, m_rel_bias, v_rel_bias), (g_mix, m_g_mix, v_g_mix), (g_attn_out, m_g_attn_out, v_g_attn_out),
        (g_conv_out, m_g_conv_out, v_g_conv_out), (g_xattn, m_g_xattn, v_g_xattn), (g_mem, m_g_mem, v_g_mem),
        (g_ffn, m_g_ffn, v_g_ffn), (as_rows(b_ffn_conv), as_rows(m_b_ffn_conv), as_rows(v_b_ffn_conv)),
        (row1(g_final), row1(m_g_final), row1(v_g_final))]
    g_packed, small_res = _adamw_small(small_g, small_parts, wmv, me_arr)
    small_out = dict(zip(small_names, small_res))
    small_out["b_ffn_conv"] = [a.reshape(1, 2 * D_FF) for a in small_out["b_ffn_conv"]]
    small_out["g_final"] = [a.reshape(D_MODEL) for a in small_out["g_final"]]

    g_wsc = lax.dynamic_slice(g_packed[ROW_WSC:ROW_WSC + 3, 0:CONV_W], (0, me * HEAD_DIM), (3, HEAD_DIM))
    g_wfc = lax.dynamic_slice(g_packed[ROW_WFC:ROW_WFC + 3 * N_DEV, 0:UP_CHUNK].reshape(3, N_DEV, UP_CHUNK),
                              (0, me, 0), (3, 1, UP_CHUNK)).reshape(3, UP_CHUNK)
    shard_res = _adamw_shards([(w_short_conv[0], g_wsc, m_w_short_conv[0], v_w_short_conv[0]),
                               (w_ffn_conv[0], g_wfc, m_w_ffn_conv[0], v_w_ffn_conv[0])])
    small_out["w_short_conv"] = [g_wsc[None]] + [a[None] for a in shard_res[0]]
    small_out["w_ffn_conv"] = [g_wfc[None]] + [a[None] for a in shard_res[1]]

    order = ["rel_bias", "g_mix", "w_in", "w_short_conv", "g_attn_out", "g_conv_out", "w_out", "g_xattn", "g_mem",
             "w_xq", "w_xk", "w_xv", "w_xo", "g_ffn", "w_up", "w_ffn_conv", "b_ffn_conv", "w_down", "g_final"]
    allp = {**big_out, **small_out}
    outs = [loss, grad_x[None]]
    for kind in range(4):
        outs += [allp[n][kind] for n in order]
    return tuple(outs)
```

```python
import functools
import math

import numpy as np
import jax
import jax.numpy as jnp
from jax import lax
from jax.experimental import pallas as pl
from jax.experimental.pallas import tpu as pltpu

F32 = jnp.float32
BF16 = jnp.bfloat16
MESH = pl.DeviceIdType.MESH

N_DEV = 8
D_MODEL = 1024
ATTN_W = 512
CONV_W = 512
N_HEADS = 8
HEAD_DIM = 64
WIN = 128
DILATIONS = (1, 4, 16)
N_BUCKETS = 32
BUCKET_MAX_EXACT = 16
BUCKET_MAX_DISTANCE = 2048
N_MEM_HEADS = 4
MEM_HEAD_DIM = 256
D_FF = 2816
IN_COLS = 3072
IN_CHUNK = IN_COLS // N_DEV
UP_CHUNK = 2 * D_FF // N_DEV
EPS = 1e-6

ADAM_LR = 0.001
ADAM_B1 = 0.9
ADAM_B2 = 0.999
ADAM_EPS = 1e-08
ADAM_WD = 0.01
ADAM_STEP = 10

SUBLANES = 8
LANES = 128
HALO = 16
TM = 512
TM_FFN = 256
TS_DW = 2048
VMEM_LIMIT = 56 * 1024 * 1024

ROW_RELB, ROW_GMIX, ROW_GXATTN, ROW_GMEM, ROW_GFFN, ROW_GFINAL, ROW_GAC = 0, 8, 16, 24, 32, 40, 48
ROW_WSC, ROW_BFC, ROW_WFC, SMALL_ROWS = 56, 64, 72, 96


def _cparams(n_grid):
    return pltpu.CompilerParams(dimension_semantics=("arbitrary",) * n_grid, vmem_limit_bytes=VMEM_LIMIT)


def _full(shape):
    nd = len(shape)
    return pl.BlockSpec(tuple(shape), lambda *_: (0,) * nd)


def _resident(shape):
    nd = len(shape)
    return pl.BlockSpec(tuple(shape), lambda *_: (0,) * nd, pipeline_mode=pl.Buffered(1))


ANY_SPEC = pl.BlockSpec(memory_space=pl.ANY)
HBM_SPEC = pl.BlockSpec(memory_space=pltpu.HBM)
SEM_SPEC = pl.BlockSpec(memory_space=pltpu.SEMAPHORE)
VMEM_SPEC = pl.BlockSpec(memory_space=pltpu.VMEM)
SMEM_SPEC = pl.BlockSpec(memory_space=pltpu.SMEM)
DATAFLOW = pltpu.SideEffectType.DATAFLOW_SIDE_EFFECTING


def _rms(x):
    r = lax.rsqrt(jnp.mean(x * x, axis=-1, keepdims=True) + EPS)
    return x * r, r


def _rms_bwd(xh, r, g, dy):
    dxh = dy * g
    return r * (dxh - xh * jnp.mean(dxh * xh, axis=-1, keepdims=True))


def _shift_down(u, halo, k):
    ru = pltpu.roll(u, k, 0)
    rh = pltpu.roll(halo, k, 0)
    row = lax.broadcasted_iota(jnp.int32, rh.shape, 0)
    head = jnp.where(row < k, rh, ru[0:SUBLANES])
    return jnp.concatenate([head, ru[SUBLANES:]], axis=0)


def _shift_up(u, halo, k):
    tm = u.shape[0]
    ru = pltpu.roll(u, tm - k, 0)
    rh = pltpu.roll(halo, SUBLANES - k, 0)
    row = lax.broadcasted_iota(jnp.int32, rh.shape, 0)
    tail = jnp.where(row >= SUBLANES - k, rh, ru[tm - SUBLANES:])
    return jnp.concatenate([ru[:tm - SUBLANES], tail], axis=0)


def _causal_conv3(u, halo, w_ref):
    return (_shift_down(u, halo, 2) * w_ref[0:1, :] + _shift_down(u, halo, 1) * w_ref[1:2, :]) + u * w_ref[2:3, :]


def _dot(a, b):
    return jnp.dot(a, b, preferred_element_type=F32)


def _dot_nt(a, b):
    return lax.dot_general(a, b, (((1,), (1,)), ((), ())), preferred_element_type=F32)


def _dot_tn(a, b):
    return lax.dot_general(a, b, (((0,), (0,)), ((), ())), preferred_element_type=F32)


def _sigmoid(x):
    return 1.0 / (1.0 + jnp.exp(-x))


def _bucket_tables():
    qi = np.arange(WIN)[:, None]
    kj = np.arange(2 * WIN)[None, :]
    steps = np.clip(qi + WIN - kj, 0, WIN)
    out = []
    for d in DILATIONS:
        dist = steps * d
        dd = np.maximum(dist, 1).astype(np.float32)
        large = BUCKET_MAX_EXACT + (
            np.log(dd / np.float32(BUCKET_MAX_EXACT)) / np.float32(math.log(BUCKET_MAX_DISTANCE / BUCKET_MAX_EXACT))
            * np.float32(N_BUCKETS - BUCKET_MAX_EXACT)).astype(np.int32)
        large = np.minimum(large, N_BUCKETS - 1)
        out.append(np.where(dist < BUCKET_MAX_EXACT, dist, large).astype(np.int32))
    return jnp.asarray(np.stack(out))


def _bias_fwd(rel_bias, buckets):
    def body(rb_ref, bk_ref, o_ref):
        for p in range(3):
            bk = bk_ref[p]
            for h in range(N_HEADS):
                acc = jnp.zeros((WIN, 2 * WIN), F32)
                for b in range(N_BUCKETS):
                    acc = jnp.where(bk == b, rb_ref[h, b], acc)
                o_ref[p, h] = acc

    return pl.pallas_call(
        body, name="bias_fwd",
        out_shape=jax.ShapeDtypeStruct((3, N_HEADS, WIN, 2 * WIN), F32),
        in_specs=[pl.BlockSpec(memory_space=pltpu.SMEM), pl.BlockSpec(memory_space=pltpu.VMEM)],
        out_specs=pl.BlockSpec(memory_space=pltpu.VMEM),
    )(rel_bias, buckets)


def _bias_bwd(dbias, buckets):
    def body(db_ref, bk_ref, o_ref):
        lane = lax.broadcasted_iota(jnp.int32, (1, D_MODEL), 1)
        rows = []
        for h in range(N_HEADS):
            row = jnp.zeros((1, D_MODEL), F32)
            for b in range(N_BUCKETS):
                tot = jnp.zeros((1, 1), F32)
                for p in range(3):
                    sel = jnp.where(bk_ref[p] == b, db_ref[p, h], 0.0)
                    tot = tot + jnp.sum(jnp.sum(sel, axis=0, keepdims=True), axis=1, keepdims=True)
                row = jnp.where(lane == b, tot, row)
            rows.append(row)
        o_ref[...] = jnp.concatenate(rows, axis=0)

    return pl.pallas_call(
        body, name="bias_bwd",
        out_shape=jax.ShapeDtypeStruct((N_HEADS, D_MODEL), F32),
        in_specs=[pl.BlockSpec(memory_space=pltpu.VMEM), pl.BlockSpec(memory_space=pltpu.VMEM)],
        out_specs=pl.BlockSpec(memory_space=pltpu.VMEM),
    )(dbias, buckets)


def _spread(val, scr_ref, out_refs, dtype):
    out_refs[0][...] = val.astype(dtype)
    n_blk = val.shape[1] // LANES
    for c in range(n_blk):
        scr_ref[c] = val[:, c * LANES:(c + 1) * LANES]
    for o_ref, d in zip(out_refs[1:], DILATIONS[1:]):
        for r in range(d):
            for c in range(n_blk):
                o_ref[r, :, c * LANES:(c + 1) * LANES] = scr_ref.at[c][pl.ds(r, TM // d, stride=d), :].astype(dtype)


def _gather_classes(blk_ref, scr_ref, d):
    n_blk = blk_ref.shape[2] // LANES
    for r in range(d):
        for c in range(n_blk):
            scr_ref.at[c][pl.ds(r, TM // d, stride=d), :] = blk_ref[r, :, c * LANES:(c + 1) * LANES].astype(F32)
    return jnp.concatenate([scr_ref[c] for c in range(n_blk)], axis=1)


def _class_specs(cols):
    return [pl.BlockSpec((TM, cols), lambda i: (i, 0))] + [
        pl.BlockSpec((d, TM // d, cols), lambda i: (0, i, 0)) for d in DILATIONS[1:]]


def _class_shapes(s, cols, dtype):
    return [jax.ShapeDtypeStruct((s, cols), dtype)] + [
        jax.ShapeDtypeStruct((d, s // d, cols), dtype) for d in DILATIONS[1:]]


def _rms_proj(x, g_mix, w_in_g, dep):
    s = x.shape[0]

    def body(x_ref, g_ref, w_ref, dep_ref, h_ref, q1, q4, q16, k1, k4, k16, v1, v4, v16, gb_ref, gc_ref, xi_ref, scr):
        xh, _ = _rms(x_ref[...])
        h = (xh * g_ref[...]).astype(BF16)
        h_ref[...] = h
        proj = jnp.concatenate([_dot(h, w_ref[j]) for j in range(N_DEV)], axis=1)
        _spread(proj[:, 0:512] * (HEAD_DIM ** -0.5), scr, (q1, q4, q16), BF16)
        _spread(proj[:, 512:1024], scr, (k1, k4, k16), BF16)
        _spread(proj[:, 1024:1536], scr, (v1, v4, v16), BF16)
        gb_ref[...] = proj[:, 1536:2048]
        gc_ref[...] = proj[:, 2048:2560]
        xi_ref[...] = proj[:, 2560:3072]

    row = lambda n: pl.BlockSpec((TM, n), lambda i: (i, 0))
    res = pl.pallas_call(
        body, name="rms_proj", grid=(s // TM,),
        out_shape=[jax.ShapeDtypeStruct((s, D_MODEL), BF16)] + _class_shapes(s, 512, BF16) * 3
        + [jax.ShapeDtypeStruct((s, 512), F32)] * 3,
        in_specs=[row(D_MODEL), _full(g_mix.shape), _full(w_in_g.shape), ANY_SPEC],
        out_specs=[row(D_MODEL)] + _class_specs(512) * 3 + [row(512)] * 3,
        scratch_shapes=[pltpu.VMEM((512 // LANES, TM, LANES), F32)],
        compiler_params=_cparams(1),
    )(x, g_mix, w_in_g, dep)
    return res[0], res[1:4], res[4:7], res[7:10], res[10], res[11], res[12]


def _band_mask(blk):
    qi = lax.broadcasted_iota(jnp.int32, (WIN, 2 * WIN), 0)
    kj = lax.broadcasted_iota(jnp.int32, (WIN, 2 * WIN), 1)
    steps = qi + WIN - kj
    return (steps >= 0) & (steps <= WIN) & (kj >= jnp.where(blk > 0, 0, WIN))


def _swa_fwd(qc, kc, vc, bias, dil):
    nb = qc.shape[1] // WIN

    def body(q_ref, kp_ref, kc_ref, vp_ref, vc_ref, b_ref, o_ref, lse_ref):
        valid = _band_mask(pl.program_id(1))
        for h in range(N_HEADS):
            sl = slice(h * HEAD_DIM, (h + 1) * HEAD_DIM)
            kh = jnp.concatenate([kp_ref[0, :, sl], kc_ref[0, :, sl]], axis=0)
            vh = jnp.concatenate([vp_ref[0, :, sl], vc_ref[0, :, sl]], axis=0)
            lg = _dot_nt(q_ref[0, :, sl], kh) + b_ref[h]
            lg = jnp.where(valid, lg, -jnp.inf)
            m = jnp.max(lg, axis=-1, keepdims=True)
            p = jnp.exp(lg - m)
            den = jnp.sum(p, axis=-1, keepdims=True)
            o_ref[0, :, sl] = _dot(p.astype(BF16), vh) / den
            lse_ref[0, :, sl] = jnp.broadcast_to(m + jnp.log(den), (WIN, HEAD_DIM))

    cur = pl.BlockSpec((1, WIN, 512), lambda r, b: (r, b, 0))
    prev = pl.BlockSpec((1, WIN, 512), lambda r, b: (r, jnp.maximum(b - 1, 0), 0))
    return pl.pallas_call(
        body, name=f"swa_fwd_d{dil}", grid=(dil, nb),
        out_shape=[jax.ShapeDtypeStruct(qc.shape, F32)] * 2,
        in_specs=[cur, prev, cur, prev, cur, _full(bias.shape)],
        out_specs=[cur] * 2,
        compiler_params=_cparams(2),
    )(qc, kc, kc, vc, vc, bias)


def _mix_out(branches, gb, gc, xi, x, w_sc, g_a, g_c, w_out):
    s = x.shape[0]
    tb = TM // SUBLANES

    def body(o1, l1, o4, l4, o16, l16, gb_ref, gc_ref, xi_ref, gch_ref, xih_ref, x_ref, wsc_ref,
             ga_ref, gcv_ref, wout_ref, attn_ref, lse1, lse4, lse16, mixed_ref, x1_ref, scr_a, scr_b, scr_c, scr_d):
        i = pl.program_id(0)
        la, lb, lc = l1[...], _gather_classes(l4, scr_a, 4), _gather_classes(l16, scr_b, 16)
        m_all = jnp.maximum(jnp.maximum(la, lb), lc)
        ea, eb, ec = jnp.exp(la - m_all), jnp.exp(lb - m_all), jnp.exp(lc - m_all)
        den = (ea + eb) + ec
        num = (ea * o1[...] + eb * _gather_classes(o4, scr_c, 4)) + ec * _gather_classes(o16, scr_d, 16)
        attn = num / den
        attn_ref[...] = attn
        _spread(m_all + jnp.log(den), scr_a, (lse1, lse4, lse16), F32)
        xa, _ = _rms(attn)
        u = gc_ref[...] * xi_ref[...]
        uh = jnp.where(i > 0, gch_ref[...] * xih_ref[...], 0.0)
        conv = gb_ref[...] * _causal_conv3(u, uh, wsc_ref)
        xc, _ = _rms(conv)
        mixed = jnp.concatenate([xa * ga_ref[...], xc * gcv_ref[...]], axis=1).astype(BF16)
        mixed_ref[...] = mixed
        x1_ref[...] = x_ref[...] + _dot(mixed, wout_ref[...])

    row = lambda n: pl.BlockSpec((TM, n), lambda i: (i, 0))
    halo = pl.BlockSpec((SUBLANES, 512), lambda i: (jnp.maximum(i * tb - 1, 0), 0))
    cs = _class_specs(512)
    flat = [a for br in branches for a in br]
    res = pl.pallas_call(
        body, name="mix_out", grid=(s // TM,),
        out_shape=[jax.ShapeDtypeStruct((s, 512), F32)] + _class_shapes(s, 512, F32)
        + [jax.ShapeDtypeStruct((s, D_MODEL), BF16), jax.ShapeDtypeStruct((s, D_MODEL), F32)],
        in_specs=[cs[0], cs[0], cs[1], cs[1], cs[2], cs[2], row(512), row(512), row(512), halo, halo,
                  row(D_MODEL), _full(w_sc.shape), _full(g_a.shape), _full(g_c.shape), _full(w_out.shape)],
        out_specs=[row(512)] + cs + [row(D_MODEL), row(D_MODEL)],
        scratch_shapes=[pltpu.VMEM((512 // LANES, TM, LANES), F32)] * 4,
        compiler_params=_cparams(1),
    )(*flat, gb, gc, xi, gc, xi, x, w_sc, g_a, g_c, w_out)
    return res[0], res[1:4], res[4], res[5]


def _mem_kv(mem, g_mem, w_xk, w_xv):
    def body(mem_ref, g_ref, wk_ref, wv_ref, mn_ref, k_ref, v_ref):
        xh, _ = _rms(mem_ref[...])
        mn = (xh * g_ref[...]).astype(BF16)
        mn_ref[...] = mn
        k_ref[...] = _dot(mn, wk_ref[...]).astype(BF16)
        v_ref[...] = _dot(mn, wv_ref[...]).astype(BF16)

    vm = pl.BlockSpec(memory_space=pltpu.VMEM)
    return pl.pallas_call(
        body, name="mem_kv",
        out_shape=[jax.ShapeDtypeStruct(mem.shape, BF16)] * 3,
        in_specs=[vm] * 4, out_specs=[vm] * 3,
        compiler_params=pltpu.CompilerParams(vmem_limit_bytes=VMEM_LIMIT),
    )(mem, g_mem, w_xk, w_xv)


def _xattn_fwd(x1, g, w_xq, k, v, w_xo):
    s = x1.shape[0]

    def body(x1_ref, g_ref, wq_ref, k_ref, v_ref, wo_ref, h2_ref, q_ref, o_ref, x2_ref):
        x1v = x1_ref[...]
        xh, _ = _rms(x1v)
        h2 = (xh * g_ref[...]).astype(BF16)
        h2_ref[...] = h2
        qb = _dot(h2, wq_ref[...]).astype(BF16)
        q_ref[...] = qb
        outs = []
        for h in range(N_MEM_HEADS):
            sl = slice(h * MEM_HEAD_DIM, (h + 1) * MEM_HEAD_DIM)
            lg = _dot_nt(qb[:, sl], k_ref[:, sl]) * (MEM_HEAD_DIM ** -0.5)
            p = jnp.exp(lg - jnp.max(lg, axis=-1, keepdims=True))
            p = p / jnp.sum(p, axis=-1, keepdims=True)
            outs.append(_dot(p.astype(BF16), v_ref[:, sl]))
        o = jnp.concatenate(outs, axis=1).astype(BF16)
        o_ref[...] = o
        x2_ref[...] = x1v + _dot(o, wo_ref[...])

    row = pl.BlockSpec((TM, D_MODEL), lambda i: (i, 0))
    return pl.pallas_call(
        body, name="xattn_fwd", grid=(s // TM,),
        out_shape=[jax.ShapeDtypeStruct((s, D_MODEL), BF16)] * 3 + [jax.ShapeDtypeStruct((s, D_MODEL), F32)],
        in_specs=[row, _full(g.shape), _full(w_xq.shape), _full(k.shape), _full(v.shape), _full(w_xo.shape)],
        out_specs=[row] * 4,
        compiler_params=_cparams(1),
    )(x1, g, w_xq, k, v, w_xo)


def _ffn_conv(h_ext, wup_ref, wfc_ref, bfc_ref, j):
    u = _dot(h_ext, wup_ref[j])
    u2, u1 = pltpu.roll(u, 2, 0), pltpu.roll(u, 1, 0)
    w = wfc_ref[j]
    c = ((u2 * w[0:1, :] + u1 * w[1:2, :]) + u * w[2:3, :]) + bfc_ref[j]
    return c[HALO:], u2[HALO:], u1[HALO:], u[HALO:]


def _ffn_fwd(x2, g, w_up_g, w_fc, b_fc, w_down_g, g_final, target):
    s = x2.shape[0]
    tb = TM_FFN // HALO
    half = N_DEV // 2

    def body(x_ref, xp_ref, g_ref, wup_ref, wfc_ref, bfc_ref, wd_ref, gf_ref, t_ref, h_ref, act_ref, dx3_ref,
             loss_ref, dgf_ref):
        i = pl.program_id(0)

        @pl.when(i == 0)
        def _():
            loss_ref[...] = jnp.zeros_like(loss_ref)
            dgf_ref[...] = jnp.zeros_like(dgf_ref)

        x2v = x_ref[...]
        gv = g_ref[...]
        h = (_rms(x2v)[0] * gv).astype(BF16)
        h_ref[...] = h
        hp = jnp.where(i > 0, _rms(xp_ref[...])[0] * gv, 0.0).astype(BF16)
        h_ext = jnp.concatenate([hp, h], axis=0)
        down = jnp.zeros((TM_FFN, D_MODEL), F32)
        for j in range(half):
            cg = _ffn_conv(h_ext, wup_ref, wfc_ref, bfc_ref, j)[0]
            cv = _ffn_conv(h_ext, wup_ref, wfc_ref, bfc_ref, j + half)[0]
            a = ((cg * _sigmoid(cg)) * cv).astype(BF16)
            act_ref[j] = a
            down = down + _dot(a, wd_ref[j])
        x3 = x2v + down
        xh, r = _rms(x3)
        gf = gf_ref[...]
        e = xh * gf - t_ref[...]
        loss_ref[...] += 0.5 * jnp.sum(jnp.sum(e * e, axis=1, keepdims=True), axis=0, keepdims=True) / D_MODEL
        dy = e * (1.0 / D_MODEL)
        dgf_ref[0:1, :] += jnp.sum(dy * xh, axis=0, keepdims=True)
        dx3_ref[...] = _rms_bwd(xh, r, gf, dy)

    row = pl.BlockSpec((TM_FFN, D_MODEL), lambda i: (i, 0))
    prev = pl.BlockSpec((HALO, D_MODEL), lambda i: (jnp.maximum(i * tb - 1, 0), 0))
    return pl.pallas_call(
        body, name="ffn_fwd", grid=(s // TM_FFN,),
        out_shape=[jax.ShapeDtypeStruct((s, D_MODEL), BF16), jax.ShapeDtypeStruct((half, s, UP_CHUNK), BF16),
                   jax.ShapeDtypeStruct((s, D_MODEL), F32), jax.ShapeDtypeStruct((SUBLANES, 128), F32),
                   jax.ShapeDtypeStruct((SUBLANES, D_MODEL), F32)],
        in_specs=[row, prev, _full(g.shape), _resident(w_up_g.shape), _full(w_fc.shape), _full(b_fc.shape),
                  _resident(w_down_g.shape), _full(g_final.shape), row],
        out_specs=[row, pl.BlockSpec((half, TM_FFN, UP_CHUNK), lambda i: (0, i, 0)), row,
                   _full((SUBLANES, 128)), _full((SUBLANES, D_MODEL))],
        compiler_params=_cparams(1),
    )(x2, x2, g, w_up_g, w_fc, b_fc, w_down_g, g_final, target)


def _ffn_bwd(dx3, h3, x2, g, w_up_g, w_fc, b_fc, w_down_g):
    s = x2.shape[0]
    tb = TM_FFN // HALO
    last = s // HALO - 1
    n_tiles = s // TM_FFN
    half = N_DEV // 2
    n_ext = TM_FFN + HALO

    def body(dx_ref, dxn_ref, hp_ref, h_ref, hn_ref, x2_ref, g_ref, wup_ref, wfc_ref, bfc_ref, wd_ref,
             dup_ref, dx2_ref, dg_ref, dwfc_ref, dbfc_ref):
        i = pl.program_id(0)

        @pl.when(i == 0)
        def _():
            dg_ref[...] = jnp.zeros_like(dg_ref)
            dwfc_ref[...] = jnp.zeros_like(dwfc_ref)
            dbfc_ref[...] = jnp.zeros_like(dbfc_ref)

        dxv = dx_ref[...]
        dxn = jnp.where(i < n_tiles - 1, dxn_ref[...], 0.0)
        dx_ext = jnp.concatenate([dxv, dxn], axis=0).astype(BF16)
        hp = jnp.where(i > 0, hp_ref[...], jnp.zeros_like(hp_ref))
        h_ext = jnp.concatenate([hp, h_ref[...], hn_ref[...]], axis=0)
        dh = jnp.zeros((TM_FFN, D_MODEL), F32)
        for j in range(half):
            cg, g2, g1, g0 = _ffn_conv(h_ext, wup_ref, wfc_ref, bfc_ref, j)
            cv, v2, v1, v0 = _ffn_conv(h_ext, wup_ref, wfc_ref, bfc_ref, j + half)
            dact = _dot_nt(dx_ext, wd_ref[j])
            sg = _sigmoid(cg)
            parts = ((j + half, dact * (cg * sg), v2, v1, v0),
                     (j, (dact * cv) * (sg * (1.0 + cg * (1.0 - sg))), g2, g1, g0))
            for jj, dc, u2, u1, u0 in parts:
                dct = dc[:TM_FFN]
                dbfc_ref[jj:jj + 1, :] += jnp.sum(dct, axis=0, keepdims=True)
                dwfc_ref[0, jj:jj + 1, :] += jnp.sum(dct * u2[:TM_FFN], axis=0, keepdims=True)
                dwfc_ref[1, jj:jj + 1, :] += jnp.sum(dct * u1[:TM_FFN], axis=0, keepdims=True)
                dwfc_ref[2, jj:jj + 1, :] += jnp.sum(dct * u0[:TM_FFN], axis=0, keepdims=True)
                w = wfc_ref[jj]
                du = ((dct * w[2:3, :] + pltpu.roll(dc, n_ext - 1, 0)[:TM_FFN] * w[1:2, :])
                      + pltpu.roll(dc, n_ext - 2, 0)[:TM_FFN] * w[0:1, :]).astype(BF16)
                dup_ref[jj] = du
                dh = dh + _dot_nt(du, wup_ref[jj])
        xh, r = _rms(x2_ref[...])
        dg_ref[0:1, :] += jnp.sum(dh * xh, axis=0, keepdims=True)
        dx2_ref[...] = dxv + _rms_bwd(xh, r, g_ref[...], dh)

    row = pl.BlockSpec((TM_FFN, D_MODEL), lambda i: (i, 0))
    prev = pl.BlockSpec((HALO, D_MODEL), lambda i: (jnp.maximum(i * tb - 1, 0), 0))
    nxt = pl.BlockSpec((HALO, D_MODEL), lambda i: (jnp.minimum((i + 1) * tb, last), 0))
    return pl.pallas_call(
        body, name="ffn_bwd", grid=(n_tiles,),
        out_shape=[jax.ShapeDtypeStruct((N_DEV, s, UP_CHUNK), BF16), jax.ShapeDtypeStruct((s, D_MODEL), F32),
                   jax.ShapeDtypeStruct((SUBLANES, D_MODEL), F32), jax.ShapeDtypeStruct((3, N_DEV, UP_CHUNK), F32),
                   jax.ShapeDtypeStruct((N_DEV, UP_CHUNK), F32)],
        in_specs=[row, nxt, prev, row, nxt, row, _full(g.shape), _resident(w_up_g.shape), _full(w_fc.shape),
                  _full(b_fc.shape), _resident(w_down_g.shape)],
        out_specs=[pl.BlockSpec((N_DEV, TM_FFN, UP_CHUNK), lambda i: (0, i, 0)), row, _full((SUBLANES, D_MODEL)),
                   _full((3, N_DEV, UP_CHUNK)), _full((N_DEV, UP_CHUNK))],
        compiler_params=_cparams(1),
    )(dx3, dx3, h3, h3, h3, x2, g, w_up_g, w_fc, b_fc, w_down_g)


def _xattn_bwd(dx2, o, q, k, v, w_xo, w_xq, x1, g, dep):
    s = x1.shape[0]

    def body(dx2_ref, o_ref, q_ref, k_ref, v_ref, wo_ref, wq_ref, x1_ref, g_ref, dep_ref, dq_ref, dx1_ref, dk_ref,
             dv_ref, dg_ref):
        @pl.when(pl.program_id(0) == 0)
        def _():
            dk_ref[...] = jnp.zeros_like(dk_ref)
            dv_ref[...] = jnp.zeros_like(dv_ref)
            dg_ref[...] = jnp.zeros_like(dg_ref)

        dx2v = dx2_ref[...]
        do = _dot_nt(dx2v.astype(BF16), wo_ref[...])
        dqs = []
        for h in range(N_MEM_HEADS):
            sl = slice(h * MEM_HEAD_DIM, (h + 1) * MEM_HEAD_DIM)
            qh, kh, vh = q_ref[:, sl], k_ref[:, sl], v_ref[:, sl]
            lg = _dot_nt(qh, kh) * (MEM_HEAD_DIM ** -0.5)
            p = jnp.exp(lg - jnp.max(lg, axis=-1, keepdims=True))
            p = p / jnp.sum(p, axis=-1, keepdims=True)
            doh = do[:, sl].astype(BF16)
            dp = _dot_nt(doh, vh)
            ds = (p * (dp - jnp.sum(p * dp, axis=-1, keepdims=True)) * (MEM_HEAD_DIM ** -0.5)).astype(BF16)
            dqs.append(_dot(ds, kh))
            dk_ref[:, sl] += _dot_tn(ds, qh)
            dv_ref[:, sl] += _dot_tn(p.astype(BF16), doh)
        dq = jnp.concatenate(dqs, axis=1).astype(BF16)
        dq_ref[...] = dq
        dh2 = _dot_nt(dq, wq_ref[...])
        xh, r = _rms(x1_ref[...])
        dg_ref[0:1, :] += jnp.sum(dh2 * xh, axis=0, keepdims=True)
        dx1_ref[...] = dx2v + _rms_bwd(xh, r, g_ref[...], dh2)

    row = pl.BlockSpec((TM, D_MODEL), lambda i: (i, 0))
    return pl.pallas_call(
        body, name="xattn_bwd", grid=(s // TM,),
        out_shape=[jax.ShapeDtypeStruct((s, D_MODEL), BF16), jax.ShapeDtypeStruct((s, D_MODEL), F32),
                   jax.ShapeDtypeStruct(k.shape, F32), jax.ShapeDtypeStruct(k.shape, F32),
                   jax.ShapeDtypeStruct((SUBLANES, D_MODEL), F32)],
        in_specs=[row, row, row, _full(k.shape), _full(v.shape), _full(w_xo.shape), _full(w_xq.shape), row,
                  _full(g.shape), ANY_SPEC],
        out_specs=[row, row, _full(k.shape), _full(k.shape), _full((SUBLANES, D_MODEL))],
        compiler_params=_cparams(1),
    )(dx2, o, q, k, v, w_xo, w_xq, x1, g, dep)


def _mem_kv_bwd(dk, dv, mem_n, mem, w_xk, w_xv):
    def body(dk_ref, dv_ref, mn_ref, mem_ref, wk_ref, wv_ref, dwk_ref, dwv_ref, dg_ref):
        dkb, dvb = dk_ref[...].astype(BF16), dv_ref[...].astype(BF16)
        mn = mn_ref[...]
        dwk_ref[...] = _dot_tn(mn, dkb).astype(BF16)
        dwv_ref[...] = _dot_tn(mn, dvb).astype(BF16)
        dmn = _dot_nt(dkb, wk_ref[...]) + _dot_nt(dvb, wv_ref[...])
        xh, _ = _rms(mem_ref[...])
        dg_ref[...] = jnp.zeros_like(dg_ref)
        dg_ref[0:1, :] = jnp.sum(dmn * xh, axis=0, keepdims=True)

    vm = pl.BlockSpec(memory_space=pltpu.VMEM)
    return pl.pallas_call(
        body, name="mem_kv_bwd",
        out_shape=[jax.ShapeDtypeStruct(w_xk.shape, BF16), jax.ShapeDtypeStruct(w_xv.shape, BF16),
                   jax.ShapeDtypeStruct((SUBLANES, D_MODEL), F32)],
        in_specs=[vm] * 6, out_specs=[vm] * 3,
        compiler_params=pltpu.CompilerParams(vmem_limit_bytes=VMEM_LIMIT),
    )(dk, dv, mem_n, mem, w_xk, w_xv)


def _mix_out_bwd(dx1, w_out, attn, gb, gc, xi, w_sc, g_a, g_c, dep):
    s = dx1.shape[0]
    tb = TM // SUBLANES

    def body(dx1_ref, wout_ref, attn_ref, gb_ref, gc_ref, xi_ref, gch_ref, xih_ref, wsc_ref, ga_ref, gcv_ref, dep_ref,
             da1, da4, da16, dd1, dd4, dd16, dgb_ref, dcv_ref, dga_ref, dgc_ref, dwsc_ref, scr):
        i = pl.program_id(0)

        @pl.when(i == 0)
        def _():
            dga_ref[...] = jnp.zeros_like(dga_ref)
            dgc_ref[...] = jnp.zeros_like(dgc_ref)
            dwsc_ref[...] = jnp.zeros_like(dwsc_ref)

        dmixed = _dot_nt(dx1_ref[...].astype(BF16), wout_ref[...])
        da, dcn = dmixed[:, :ATTN_W], dmixed[:, ATTN_W:]
        attn = attn_ref[...]
        xa, ra = _rms(attn)
        dga_ref[0:1, :] += jnp.sum(da * xa, axis=0, keepdims=True)
        dattn = _rms_bwd(xa, ra, ga_ref[...], da)
        _spread(dattn, scr, (da1, da4, da16), F32)
        prod = dattn * attn
        dd = jnp.concatenate(
            [jnp.broadcast_to(jnp.sum(prod[:, h * HEAD_DIM:(h + 1) * HEAD_DIM], axis=-1, keepdims=True),
                              (TM, HEAD_DIM)) for h in range(N_HEADS)], axis=1)
        _spread(dd, scr, (dd1, dd4, dd16), F32)
        gbv = gb_ref[...]
        u = gc_ref[...] * xi_ref[...]
        uh = jnp.where(i > 0, gch_ref[...] * xih_ref[...], 0.0)
        u2, u1 = _shift_down(u, uh, 2), _shift_down(u, uh, 1)
        cv = (u2 * wsc_ref[0:1, :] + u1 * wsc_ref[1:2, :]) + u * wsc_ref[2:3, :]
        xc, rc = _rms(gbv * cv)
        dgc_ref[0:1, :] += jnp.sum(dcn * xc, axis=0, keepdims=True)
        dconv = _rms_bwd(xc, rc, gcv_ref[...], dcn)
        dgb_ref[...] = dconv * cv
        dcv = dconv * gbv
        dcv_ref[...] = dcv
        dwsc_ref[0:1, :] += jnp.sum(dcv * u2, axis=0, keepdims=True)
        dwsc_ref[1:2, :] += jnp.sum(dcv * u1, axis=0, keepdims=True)
        dwsc_ref[2:3, :] += jnp.sum(dcv * u, axis=0, keepdims=True)

    row = lambda n: pl.BlockSpec((TM, n), lambda i: (i, 0))
    halo = pl.BlockSpec((SUBLANES, 512), lambda i: (jnp.maximum(i * tb - 1, 0), 0))
    acc = _full((SUBLANES, 512))
    res = pl.pallas_call(
        body, name="mix_out_bwd", grid=(s // TM,),
        out_shape=_class_shapes(s, 512, F32) * 2 + [jax.ShapeDtypeStruct((s, 512), F32)] * 2
        + [jax.ShapeDtypeStruct((SUBLANES, 512), F32)] * 3,
        in_specs=[row(D_MODEL), _full(w_out.shape), row(512), row(512), row(512), row(512), halo, halo,
                  _full(w_sc.shape), _full(g_a.shape), _full(g_c.shape), ANY_SPEC],
        out_specs=_class_specs(512) * 2 + [row(512)] * 2 + [acc] * 3,
        scratch_shapes=[pltpu.VMEM((512 // LANES, TM, LANES), F32)],
        compiler_params=_cparams(1),
    )(dx1, w_out, attn, gb, gc, xi, gc, xi, w_sc, g_a, g_c, dep)
    return res[0:3], res[3:6], res[6], res[7], res[8], res[9], res[10]


def _swa_bwd(qc, kc, vc, doc, lsec, ddc, bias, dil, dep):
    nb = qc.shape[1] // WIN

    def body(q_ref, qn_ref, kp_ref, kc_ref, vp_ref, vc_ref, do_ref, don_ref, lse_ref, lsen_ref, dd_ref, ddn_ref,
             b_ref, dep_ref, dq_ref, dk_ref, dv_ref, db_ref):
        r, b = pl.program_id(0), pl.program_id(1)

        @pl.when((r == 0) & (b == 0))
        def _():
            db_ref[...] = jnp.zeros_like(db_ref)

        valid = _band_mask(b)
        qi = lax.broadcasted_iota(jnp.int32, (WIN, WIN), 0)
        kj = lax.broadcasted_iota(jnp.int32, (WIN, WIN), 1)
        valid_n = kj >= qi + jnp.where(b + 1 < nb, 0, WIN)
        for h in range(N_HEADS):
            sl = slice(h * HEAD_DIM, (h + 1) * HEAD_DIM)
            col = slice(h * HEAD_DIM, h * HEAD_DIM + 1)
            qh, kc_h, vc_h = q_ref[0, :, sl], kc_ref[0, :, sl], vc_ref[0, :, sl]
            kh = jnp.concatenate([kp_ref[0, :, sl], kc_h], axis=0)
            vh = jnp.concatenate([vp_ref[0, :, sl], vc_h], axis=0)
            doh = do_ref[0, :, sl].astype(BF16)
            lg = jnp.where(valid, _dot_nt(qh, kh) + b_ref[h], -jnp.inf)
            p = jnp.exp(lg - lse_ref[0, :, col])
            ds = p * (_dot_nt(doh, vh) - dd_ref[0, :, col])
            db_ref[h] += ds
            dsb = ds.astype(BF16)
            dq_ref[0, :, sl] = _dot(dsb, kh)
            dk = _dot_tn(dsb[:, WIN:], qh)
            dv = _dot_tn(p[:, WIN:].astype(BF16), doh)
            qn = qn_ref[0, :, sl]
            don = don_ref[0, :, sl].astype(BF16)
            lgn = jnp.where(valid_n, _dot_nt(qn, kc_h) + b_ref[h][:, :WIN], -jnp.inf)
            pn = jnp.exp(lgn - lsen_ref[0, :, col])
            dsn = pn * (_dot_nt(don, vc_h) - ddn_ref[0, :, col])
            dk_ref[0, :, sl] = dk + _dot_tn(dsn.astype(BF16), qn)
            dv_ref[0, :, sl] = dv + _dot_tn(pn.astype(BF16), don)

    cur = pl.BlockSpec((1, WIN, 512), lambda r, b: (r, b, 0))
    prev = pl.BlockSpec((1, WIN, 512), lambda r, b: (r, jnp.maximum(b - 1, 0), 0))
    nxt = pl.BlockSpec((1, WIN, 512), lambda r, b: (r, jnp.minimum(b + 1, nb - 1), 0))
    return pl.pallas_call(
        body, name=f"swa_bwd_d{dil}", grid=(dil, nb),
        out_shape=[jax.ShapeDtypeStruct(qc.shape, F32)] * 3 + [jax.ShapeDtypeStruct(bias.shape, F32)],
        in_specs=[cur, nxt, prev, cur, prev, cur, cur, nxt, cur, nxt, cur, nxt, _full(bias.shape), ANY_SPEC],
        out_specs=[cur] * 3 + [_full(bias.shape)],
        compiler_params=_cparams(2),
    )(qc, qc, kc, kc, vc, vc, doc, doc, lsec, lsec, ddc, ddc, bias, dep)


def _in_proj_bwd(dqs, dks, dvs, dgb, dcv, gc, xi, w_sc, w_in_g, x, g_mix, dx1):
    s = x.shape[0]
    tb = TM // SUBLANES
    last = s // SUBLANES - 1
    n_tiles = s // TM

    def body(dq1, dq4, dq16, dk1, dk4, dk16, dv1, dv4, dv16, dgb_ref, dcv_ref, dcvn_ref, gc_ref, xi_ref, wsc_ref,
             win_ref, x_ref, g_ref, dx1_ref, dproj_ref, gx_ref, dg_ref, scr_a, scr_b):
        i = pl.program_id(0)

        @pl.when(i == 0)
        def _():
            dg_ref[...] = jnp.zeros_like(dg_ref)

        d0 = dcv_ref[...]
        dn = jnp.where(i < n_tiles - 1, dcvn_ref[...], 0.0)
        du = (d0 * wsc_ref[2:3, :] + _shift_up(d0, dn, 1) * wsc_ref[1:2, :]) + _shift_up(d0, dn, 2) * wsc_ref[0:1, :]
        merge = lambda a, b4, b16: (a[...] + _gather_classes(b4, scr_a, 4)) + _gather_classes(b16, scr_b, 16)
        dq = merge(dq1, dq4, dq16) * (HEAD_DIM ** -0.5)
        dk = merge(dk1, dk4, dk16)
        dv = merge(dv1, dv4, dv16)
        dproj = jnp.concatenate([dq, dk, dv, dgb_ref[...], du * xi_ref[...], du * gc_ref[...]], axis=1).astype(BF16)
        dproj_ref[...] = dproj
        dh = jnp.zeros((TM, D_MODEL), F32)
        for j in range(N_DEV):
            dh = dh + _dot_nt(dproj[:, j * IN_CHUNK:(j + 1) * IN_CHUNK], win_ref[j])
        xh, r = _rms(x_ref[...])
        dg_ref[0:1, :] += jnp.sum(dh * xh, axis=0, keepdims=True)
        gx_ref[...] = dx1_ref[...] + _rms_bwd(xh, r, g_ref[...], dh)

    row = lambda n: pl.BlockSpec((TM, n), lambda i: (i, 0))
    nxt = pl.BlockSpec((SUBLANES, 512), lambda i: (jnp.minimum((i + 1) * tb, last), 0))
    return pl.pallas_call(
        body, name="in_proj_bwd", grid=(n_tiles,),
        out_shape=[jax.ShapeDtypeStruct((s, IN_COLS), BF16), jax.ShapeDtypeStruct((s, D_MODEL), F32),
                   jax.ShapeDtypeStruct((SUBLANES, D_MODEL), F32)],
        in_specs=_class_specs(512) * 3 + [row(512), row(512), nxt, row(512), row(512), _full(w_sc.shape),
                                          _full(w_in_g.shape), row(D_MODEL), _full(g_mix.shape), row(D_MODEL)],
        out_specs=[row(IN_COLS), row(D_MODEL), _full((SUBLANES, D_MODEL))],
        scratch_shapes=[pltpu.VMEM((512 // LANES, TM, LANES), F32)] * 2,
        compiler_params=_cparams(1),
    )(*dqs, *dks, *dvs, dgb, dcv, dcv, gc, xi, w_sc, w_in_g, x, g_mix, dx1)


def _dw(a, b, dep, name, a_chunked=False, b_chunked=False, n_chunks=1, chunk_cols=None):
    ts = TS_DW
    if a_chunked:
        nj, s, kk = a.shape
        nn = b.shape[1]
        a_spec = pl.BlockSpec((1, ts, kk), lambda j, t: (j, t, 0))
        b_spec = pl.BlockSpec((ts, nn), lambda j, t: (t, 0))
    elif b_chunked:
        nj, s, nn = b.shape
        kk = a.shape[1]
        a_spec = pl.BlockSpec((ts, kk), lambda j, t: (t, 0))
        b_spec = pl.BlockSpec((1, ts, nn), lambda j, t: (j, t, 0))
    else:
        s, kk = a.shape
        nj, nn = (n_chunks, chunk_cols) if chunk_cols else (1, b.shape[1])
        a_spec = pl.BlockSpec((ts, kk), lambda j, t: (t, 0))
        b_spec = pl.BlockSpec((ts, nn), lambda j, t: (t, j))
    n_steps = s // ts

    def body(a_ref, b_ref, dep_ref, o_ref, acc):
        t = pl.program_id(1)

        @pl.when(t == 0)
        def _():
            acc[...] = jnp.zeros_like(acc)

        av = (a_ref[0] if a_chunked else a_ref[...]).astype(BF16)
        bv = (b_ref[0] if b_chunked else b_ref[...]).astype(BF16)
        acc[...] += _dot_tn(av, bv)

        @pl.when(t == n_steps - 1)
        def _():
            o_ref[0] = acc[...].astype(BF16)

    return pl.pallas_call(
        body, name=name, grid=(nj, n_steps),
        out_shape=jax.ShapeDtypeStruct((nj, kk, nn), BF16),
        in_specs=[a_spec, b_spec, ANY_SPEC],
        out_specs=pl.BlockSpec((1, kk, nn), lambda j, t: (j, 0, 0)),
        scratch_shapes=[pltpu.VMEM((kk, nn), F32)],
        compiler_params=_cparams(2),
    )(a, b, dep)


def _adamw_math(w, g, m, v):
    m2 = ADAM_B1 * m + (1.0 - ADAM_B1) * g
    v2 = ADAM_B2 * v + (1.0 - ADAM_B2) * (g * g)
    m_hat = m2 / (1.0 - ADAM_B1 ** ADAM_STEP)
    v_hat = v2 / (1.0 - ADAM_B2 ** ADAM_STEP)
    delta = -ADAM_LR * (m_hat / (jnp.sqrt(v_hat) + ADAM_EPS) + ADAM_WD * w)
    return delta, m2, v2


def _sum_parts(me, own, p_ref):
    g = None
    for i in range(N_DEV):
        part = jnp.where(me == i, own.astype(F32), p_ref[i].astype(F32))
        g = part if g is None else g + part
    return g


def _adamw_big(name, w, own, parts, m, v, me_arr):
    rr, cc = w.shape
    tr = rr // 4 if rr >= 512 else rr

    def body(me_ref, w_ref, own_ref, p_ref, m_ref, v_ref, g_ref, d_ref, nm_ref, nv_ref):
        g = _sum_parts(me_ref[0], own_ref[...], p_ref)
        g_ref[...] = g
        d_ref[...], nm_ref[...], nv_ref[...] = _adamw_math(w_ref[...], g, m_ref[...], v_ref[...])

    row = pl.BlockSpec((tr, cc), lambda i: (i, 0))
    return pl.pallas_call(
        body, name=name, grid=(rr // tr,),
        out_shape=[jax.ShapeDtypeStruct((rr, cc), F32)] * 4,
        in_specs=[SMEM_SPEC, row, row, pl.BlockSpec((N_DEV, tr, cc), lambda i: (0, i, 0)), row, row],
        out_specs=[row] * 4,
        compiler_params=_cparams(1),
    )(me_arr, w, own, parts, m, v)


def _small_slices():
    return [
        (slice(ROW_RELB, ROW_RELB + 8), slice(0, N_BUCKETS)),
        (slice(ROW_GMIX, ROW_GMIX + 1), slice(0, D_MODEL)),
        (slice(ROW_GAC, ROW_GAC + 1), slice(0, ATTN_W)),
        (slice(ROW_GAC, ROW_GAC + 1), slice(ATTN_W, D_MODEL)),
        (slice(ROW_GXATTN, ROW_GXATTN + 1), slice(0, D_MODEL)),
        (slice(ROW_GMEM, ROW_GMEM + 1), slice(0, D_MODEL)),
        (slice(ROW_GFFN, ROW_GFFN + 1), slice(0, D_MODEL)),
        (slice(ROW_BFC, ROW_BFC + 8), slice(0, UP_CHUNK)),
        (slice(ROW_GFINAL, ROW_GFINAL + 1), slice(0, D_MODEL)),
    ]


def _adamw_small(own, parts, wmv, me_arr):
    slices = _small_slices()
    n = len(slices)

    def body(*refs):
        me_ref, own_ref, p_ref = refs[:3]
        ins = refs[3:3 + 3 * n]
        g_ref = refs[3 + 3 * n]
        outs = refs[4 + 3 * n:]
        g = _sum_parts(me_ref[0], own_ref[...], p_ref)
        g_ref[...] = g
        for a, (rs, ls) in enumerate(slices):
            ga = g[rs, ls]
            outs[4 * a][...] = ga
            outs[4 * a + 1][...], outs[4 * a + 2][...], outs[4 * a + 3][...] = _adamw_math(
                ins[3 * a][...], ga, ins[3 * a + 1][...], ins[3 * a + 2][...])

    vm = pl.BlockSpec(memory_space=pltpu.VMEM)
    flat = [t for trip in wmv for t in trip]
    out_shape = [jax.ShapeDtypeStruct((SMALL_ROWS, D_MODEL), F32)]
    for w, _, _ in wmv:
        out_shape += [jax.ShapeDtypeStruct(w.shape, F32)] * 4
    res = pl.pallas_call(
        body, name="adamw_small", out_shape=out_shape,
        in_specs=[SMEM_SPEC] + [vm] * (2 + 3 * n), out_specs=[vm] * len(out_shape),
    )(me_arr, own, parts, *flat)
    return res[0], [res[1 + 4 * a:5 + 4 * a] for a in range(n)]


def _adamw_shards(items):
    n = len(items)

    def body(*refs):
        for a in range(n):
            w_ref, g_ref, m_ref, v_ref = refs[4 * a:4 * a + 4]
            d_ref, nm_ref, nv_ref = refs[4 * n + 3 * a:4 * n + 3 * a + 3]
            d_ref[...], nm_ref[...], nv_ref[...] = _adamw_math(w_ref[...], g_ref[...], m_ref[...], v_ref[...])

    vm = pl.BlockSpec(memory_space=pltpu.VMEM)
    out_shape = []
    for w, _, _, _ in items:
        out_shape += [jax.ShapeDtypeStruct(w.shape, F32)] * 3
    res = pl.pallas_call(
        body, name="adamw_shards", out_shape=out_shape, in_specs=[vm] * (4 * n), out_specs=[vm] * (3 * n),
    )(*[t for it in items for t in it])
    return [res[3 * a:3 * a + 3] for a in range(n)]


def _mesh_pos():
    return lax.axis_index("x"), lax.axis_index("y"), lax.axis_index("c")


def _dev_index(p):
    return 4 * p[0] + 2 * p[1] + p[2]


def _all_gather(shards):
    n = len(shards)

    def body(*refs):
        ins, outs = refs[:n], refs[n:2 * n]
        send_sems, recv_sems, loc_sems = refs[2 * n:]
        x, y, c = _mesh_pos()
        me, sib = (x, y, c), (x, y, 1 - c)
        chips = [(1 - x, y), (x, 1 - y), (1 - x, 1 - y)]

        def cp(a, k, block, to, src=None):
            dst = outs[a].at[_dev_index(block)]
            return pltpu.make_async_remote_copy(
                src_ref=dst if src is None else src, dst_ref=dst, send_sem=send_sems.at[a, k],
                recv_sem=recv_sems.at[a, k], device_id=to, device_id_type=MESH)

        mine = [pltpu.make_async_copy(ins[a], outs[a].at[_dev_index(me)], loc_sems.at[a]) for a in range(n)]
        for m_ in mine:
            m_.start()
        first = []
        for a in range(n):
            first.append(cp(a, 0, me, sib, src=ins[a]))
            first += [cp(a, 1 + j, me, (*chip, c), src=ins[a]) for j, chip in enumerate(chips)]
        for f in first:
            f.start()
        passed = []
        for a in range(n):
            for j, chip in enumerate(chips):
                cp(a, 1 + j, (*chip, c), me).wait_recv()
                fwd = cp(a, 4 + j, (*chip, c), sib)
                fwd.start()
                passed.append(fwd)
        for a in range(n):
            cp(a, 0, sib, me).wait_recv()
            for j, chip in enumerate(chips):
                cp(a, 4 + j, (*chip, 1 - c), me).wait_recv()
        for f in first + passed:
            f.wait_send()
        for m_ in mine:
            m_.wait()

    hbm = pl.BlockSpec(memory_space=pltpu.HBM)
    return pl.pallas_call(
        body, name="all_gather_weights",
        out_shape=[jax.ShapeDtypeStruct((N_DEV,) + a.shape, a.dtype) for a in shards],
        in_specs=[hbm] * n, out_specs=[hbm] * n,
        scratch_shapes=[pltpu.SemaphoreType.DMA((n, 7)), pltpu.SemaphoreType.DMA((n, 7)),
                        pltpu.SemaphoreType.DMA((n,))],
    )(*shards)


def _peers():
    x, y, c = _mesh_pos()
    return (x, y, c), [((1 - x) if k & 4 else x, (1 - y) if k & 2 else y, (1 - c) if k & 1 else c)
                       for k in range(1, 8)]


def _exchange_copy(src_ref, land_ref, whole, send_sems, recv_sems, a, k, peer, slot):
    src = src_ref if whole else src_ref.at[_dev_index(peer)]
    return pltpu.make_async_remote_copy(
        src_ref=src, dst_ref=land_ref.at[slot], send_sem=send_sems.at[7 * a + k], recv_sem=recv_sems.at[7 * a + k],
        device_id=peer, device_id_type=MESH)


def _exchange_start(name, srcs, whole, dep):
    n = len(srcs)
    lands = [lax.empty(((N_DEV,) + s.shape) if w else s.shape, s.dtype) for s, w in zip(srcs, whole)]

    def body(*refs):
        src_refs, land_refs = refs[:n], refs[n:2 * n]
        send_sems, recv_sems, token = refs[2 * n + 1], refs[2 * n + 2], refs[-1]
        me, peers = _peers()
        for a in range(n):
            for k, peer in enumerate(peers):
                _exchange_copy(src_refs[a], land_refs[a], whole[a], send_sems, recv_sems, a, k, peer,
                               _dev_index(me)).start()
        token[...] = jnp.zeros_like(token)

    res = pl.pallas_call(
        body, name=name,
        out_shape=(pltpu.SemaphoreType.DMA((7 * n,)), pltpu.SemaphoreType.DMA((7 * n,)),
                   *[pltpu.HBM(a.shape, a.dtype) for a in srcs], *[pltpu.HBM(a.shape, a.dtype) for a in lands],
                   jax.ShapeDtypeStruct((SUBLANES, 128), F32)),
        in_specs=[HBM_SPEC] * (2 * n) + [ANY_SPEC],
        out_specs=(SEM_SPEC, SEM_SPEC, *([HBM_SPEC] * (2 * n)), VMEM_SPEC),
        input_output_aliases={i: 2 + i for i in range(2 * n)},
        compiler_params=pltpu.CompilerParams(has_side_effects=DATAFLOW),
    )(*[pltpu.with_memory_space_constraint(a, pltpu.HBM) for a in srcs],
      *[pltpu.with_memory_space_constraint(a, pltpu.HBM) for a in lands], dep)
    return res[0], res[1], list(res[2:2 + n]), list(res[2 + n:2 + 2 * n]), res[-1]


def _exchange_wait(name, started, whole, after, which=None):
    send_sems, recv_sems, srcs, lands, _ = started
    which = list(range(len(srcs))) if which is None else which
    srcs, lands = [srcs[a] for a in which], [lands[a] for a in which]
    n = len(srcs)

    def body(*refs):
        src_refs, land_refs = refs[:n], refs[n:2 * n]
        send_sems, recv_sems = refs[2 * n], refs[2 * n + 1]
        _, peers = _peers()
        for i, a in enumerate(which):
            for k, peer in enumerate(peers):
                cp = _exchange_copy(src_refs[i], land_refs[i], whole[a], send_sems, recv_sems, a, k, peer,
                                    _dev_index(peer))
                cp.wait_send()
                cp.wait_recv()

    res = pl.pallas_call(
        body, name=name,
        out_shape=[pltpu.HBM(a.shape, a.dtype) for a in srcs + lands],
        in_specs=[HBM_SPEC] * (2 * n) + [SEM_SPEC, SEM_SPEC, ANY_SPEC],
        out_specs=[HBM_SPEC] * (2 * n),
        input_output_aliases={i: i for i in range(2 * n)},
        compiler_params=pltpu.CompilerParams(has_side_effects=DATAFLOW),
    )(*srcs, *lands, send_sems, recv_sems, after)
    return list(res[n:])


def _local_step(x, mem, target, rel_bias, g_mix, w_in_g, w_sc, g_a, g_c, g_xattn, g_mem, g_ffn, w_fc, b_fc, g_final,
                dep, late_weights, emit, emit_small):
    s = x.shape[0]
    buckets = _bucket_tables()
    bias = _bias_fwd(rel_bias, buckets)

    h1, qs, ks, vs, gb, gc, xi = _rms_proj(x, g_mix, w_in_g, dep)
    qs, ks, vs = ([a[0][None]] + list(a[1:]) for a in (qs, ks, vs))
    branches = []
    for p, dil in enumerate(DILATIONS):
        o_p, lse_p = _swa_fwd(qs[p], ks[p], vs[p], bias[p], dil)
        branches.append([o_p[0], lse_p[0]] if dil == 1 else [o_p, lse_p])
    w_out = late_weights(["w_out"], branches[-1][0])["w_out"]
    attn, lses, mixed, x1 = _mix_out(branches, gb, gc, xi, x, w_sc, g_a, g_c, w_out)
    lw = late_weights(["w_xq", "w_xk", "w_xv", "w_xo"], x1)
    w_xq, w_xk, w_xv, w_xo = lw["w_xq"], lw["w_xk"], lw["w_xv"], lw["w_xo"]
    mem_n, mk, mv = _mem_kv(mem, g_mem, w_xk, w_xv)
    h2, xq, xo, x2 = _xattn_fwd(x1, g_xattn, w_xq, mk, mv, w_xo)
    lw = late_weights(["w_up", "w_down"], x2)
    w_up_g, w_down_g = lw["w_up"], lw["w_down"]
    h3, act, dx3, loss_acc, dg_final = _ffn_fwd(x2, g_ffn, w_up_g, w_fc, b_fc, w_down_g, g_final, target)

    gw_down = _dw(act, dx3, dep, "dw_down", a_chunked=True)
    dup, dx2, dg_ffn, dw_fc, db_fc = _ffn_bwd(dx3, h3, x2, g_ffn, w_up_g, w_fc, b_fc, w_down_g)
    gw_up = _dw(h3, dup, dep, "dw_up", b_chunked=True)
    tok = emit(dict(w_down=gw_down, w_up=gw_up))
    dxq, dx1, dmk, dmv, dg_xattn = _xattn_bwd(dx2, xo, xq, mk, mv, w_xo, w_xq, x1, g_xattn, tok)
    gw_xo = _dw(xo, dx2, tok, "dw_xo")[0]
    gw_xq = _dw(h2, dxq, tok, "dw_xq")[0]
    gw_xk, gw_xv, dg_mem = _mem_kv_bwd(dmk, dmv, mem_n, mem, w_xk, w_xv)
    tok = emit(dict(w_xo=gw_xo, w_xq=gw_xq, w_xk=gw_xk, w_xv=gw_xv))
    dattns, dds, dgb, dcv, dg_a, dg_c, dw_sc = _mix_out_bwd(dx1, w_out, attn, gb, gc, xi, w_sc, g_a, g_c, tok)
    first = lambda a: [a[0][None]] + list(a[1:])
    dattns, dds, lses = first(dattns), first(dds), first(lses)
    gw_out = _dw(mixed, dx1, tok, "dw_out")[0]
    tok = emit(dict(w_out=gw_out))
    dqs, dks, dvs, dbias = [], [], [], []
    for p, dil in enumerate(DILATIONS):
        dq_p, dk_p, dv_p, db_p = _swa_bwd(qs[p], ks[p], vs[p], dattns[p], lses[p], dds[p], bias[p], dil, tok)
        dqs.append(dq_p[0] if dil == 1 else dq_p)
        dks.append(dk_p[0] if dil == 1 else dk_p)
        dvs.append(dv_p[0] if dil == 1 else dv_p)
        dbias.append(db_p)
    d_relb = _bias_bwd(jnp.stack(dbias), buckets)
    dproj, grad_x, dg_mix = _in_proj_bwd(dqs, dks, dvs, dgb, dcv, gc, xi, w_sc, w_in_g, x, g_mix, dx1)
    pad = lambda a: jnp.pad(a, ((0, 0), (0, D_MODEL - a.shape[1])))
    small = jnp.concatenate([
        d_relb, dg_mix, dg_xattn, dg_mem, dg_ffn, dg_final, jnp.concatenate([dg_a, dg_c], axis=1),
        pad(dw_sc), pad(db_fc), pad(dw_fc.reshape(3 * N_DEV, UP_CHUNK))], axis=0)
    tok = emit_small(small)
    gw_in = _dw(h1, dproj, tok, "dw_in", n_chunks=N_DEV, chunk_cols=IN_CHUNK)
    emit(dict(w_in=gw_in))
    return loss_acc[0, 0], grad_x


def kernel(x, mem, rel_bias, g_mix, w_in, w_short_conv, g_attn_out, g_conv_out, w_out, g_xattn, g_mem, w_xq, w_xk, w_xv, w_xo, g_ffn, w_up, w_ffn_conv, b_ffn_conv, w_down, g_final, loss_target, m_rel_bias, m_g_mix, m_w_in, m_w_short_conv, m_g_attn_out, m_g_conv_out, m_w_out, m_g_xattn, m_g_mem, m_w_xq, m_w_xk, m_w_xv, m_w_xo, m_g_ffn, m_w_up, m_w_ffn_conv, m_b_ffn_conv, m_w_down, m_g_final, v_rel_bias, v_g_mix, v_w_in, v_w_short_conv, v_g_attn_out, v_g_conv_out, v_w_out, v_g_xattn, v_g_mem, v_w_xq, v_w_xk, v_w_xv, v_w_xo, v_g_ffn, v_w_up, v_w_ffn_conv, v_b_ffn_conv, v_w_down, v_g_final):
    me = _dev_index(_mesh_pos())
    me_arr = me.reshape(1).astype(jnp.int32)

    big_names = ["w_in", "w_out", "w_xq", "w_xk", "w_xv", "w_xo", "w_up", "w_down"]
    late_names = big_names[1:]
    big_w = dict(w_in=w_in[0], w_out=w_out[0], w_xq=w_xq[0], w_xk=w_xk[0], w_xv=w_xv[0], w_xo=w_xo[0],
                 w_up=w_up[0], w_down=w_down[0])
    big_m = dict(w_in=m_w_in[0], w_out=m_w_out[0], w_xq=m_w_xq[0], w_xk=m_w_xk[0], w_xv=m_w_xv[0], w_xo=m_w_xo[0],
                 w_up=m_w_up[0], w_down=m_w_down[0])
    big_v = dict(w_in=v_w_in[0], w_out=v_w_out[0], w_xq=v_w_xq[0], w_xk=v_w_xk[0], w_xv=v_w_xv[0], w_xo=v_w_xo[0],
                 w_up=v_w_up[0], w_down=v_w_down[0])
    shard_shape = {n: big_w[n].shape for n in big_names}

    w_in_g, w_sc_g, w_fc_full = _all_gather([big_w["w_in"].astype(BF16), w_short_conv[0], w_ffn_conv[0]])
    w_sc_full = w_sc_g.transpose(1, 0, 2).reshape(3, CONV_W)
    late_shards = [big_w[n].astype(BF16) for n in late_names]
    ag = _exchange_start("gather_weights_start", late_shards, [True] * len(late_names), w_in_g)

    def late_weights(names, after):
        which = [late_names.index(n) for n in names]
        lands = _exchange_wait("gather_" + "_".join(names) + "_wait", ag, [True] * len(late_names), after, which)
        out = {}
        for n, a, land in zip(names, which, lands):
            full = lax.dynamic_update_index_in_dim(land, late_shards[a], me, 0)
            if n == "w_up":
                out[n] = full
            elif n == "w_down":
                out[n] = full.reshape(N_DEV // 2, UP_CHUNK, D_MODEL)
            else:
                out[n] = full.reshape(D_MODEL, D_MODEL)
        return out

    sent = []

    def emit(grads):
        names = list(grads)
        blocks = [grads[n].reshape((N_DEV,) + shard_shape[n]) for n in names]
        own = [lax.dynamic_index_in_dim(b, me, 0, keepdims=False) for b in blocks]
        started = _exchange_start("scatter_" + "_".join(names) + "_start", blocks, [False] * len(names), me_arr)
        sent.append((names, own, started))
        return started[-1]

    def emit_small(small):
        sent_small.append((small, _exchange_start("gather_small_start", [small], [True], me_arr)))
        return sent_small[0][1][-1]

    sent_small = []
    loss_part, grad_x = _local_step(
        x[0], mem[0], loss_target[0], rel_bias, g_mix, w_in_g, w_sc_full, g_attn_out, g_conv_out, g_xattn, g_mem,
        g_ffn, w_fc_full, b_ffn_conv.reshape(N_DEV, 1, UP_CHUNK), g_final.reshape(1, D_MODEL), ag[-1],
        late_weights, emit, emit_small)
    loss = lax.psum(loss_part, ("x", "y", "c"))

    small_g, small_started = sent_small[0]
    after = sent[-1][2][-1]
    small_parts = _exchange_wait("gather_small_wait", small_started, [True], after)[0]
    big_out = {}
    after = small_parts
    for names, own, started in sent:
        lands = _exchange_wait("scatter_" + "_".join(names) + "_wait", started, [False] * len(names), after)
        for n, own_n, land in zip(names, own, lands):
            res = _adamw_big("adamw_" + n, big_w[n], own_n, land, big_m[n], big_v[n], me_arr)
            big_out[n] = [r[None] for r in res]
            after = res[0]

    as_rows = lambda a: a.reshape(N_DEV, UP_CHUNK)
    row1 = lambda a: a.reshape(1, D_MODEL)
    small_names = ["rel_bias", "g_mix", "g_attn_out", "g_conv_out", "g_xattn", "g_mem", "g_ffn", "b_ffn_conv", "g_final"]
    wmv = [
        (rel_bias, m_rel_bias, v_rel_bias), (g_mix, m_g_mix, v_g_mix), (g_attn_out, m_g_attn_out, v_g_attn_out),
        (g_conv_out, m_g_conv_out, v_g_conv_out), (g_xattn, m_g_xattn, v_g_xattn), (g_mem, m_g_mem, v_g_mem),
        (g_ffn, m_g_ffn, v_g_ffn), (as_rows(b_ffn_conv), as_rows(m_b_ffn_conv), as_rows(v_b_ffn_conv)),
        (row1(g_final), row1(m_g_final), row1(v_g_final))]
    g_packed, small_res = _adamw_small(small_g, small_parts, wmv, me_arr)
    small_out = dict(zip(small_names, small_res))
    small_out["b_ffn_conv"] = [a.reshape(1, 2 * D_FF) for a in small_out["b_ffn_conv"]]
    small_out["g_final"] = [a.reshape(D_MODEL) for a in small_out["g_final"]]

    g_wsc = lax.dynamic_slice(g_packed[ROW_WSC:ROW_WSC + 3, 0:CONV_W], (0, me * HEAD_DIM), (3, HEAD_DIM))
    g_wfc = lax.dynamic_slice(g_packed[ROW_WFC:ROW_WFC + 3 * N_DEV, 0:UP_CHUNK].reshape(3, N_DEV, UP_CHUNK),
                              (0, me, 0), (3, 1, UP_CHUNK)).reshape(3, UP_CHUNK)
    shard_res = _adamw_shards([(w_short_conv[0], g_wsc, m_w_short_conv[0], v_w_short_conv[0]),
                               (w_ffn_conv[0], g_wfc, m_w_ffn_conv[0], v_w_ffn_conv[0])])
    small_out["w_short_conv"] = [g_wsc[None]] + [a[None] for a in shard_res[0]]
    small_out["w_ffn_conv"] = [g_wfc[None]] + [a[None] for a in shard_res[1]]

    order = ["rel_bias", "g_mix", "w_in", "w_short_conv", "g_attn_out", "g_conv_out", "w_out", "g_xattn", "g_mem",
             "w_xq", "w_xk", "w_xv", "w_xo", "g_ffn", "w_up", "w_ffn_conv", "b_ffn_conv", "w_down", "g_final"]
    allp = {**big_out, **small_out}
    outs = [loss, grad_x[None]]
    for kind in range(4):
        outs += [allp[n][kind] for n in order]
    return tuple(outs)
```

```python
import functools
import math

import numpy as np
import jax
import jax.numpy as jnp
from jax import lax
from jax.experimental import pallas as pl
from jax.experimental.pallas import tpu as pltpu

F32 = jnp.float32
BF16 = jnp.bfloat16
MESH = pl.DeviceIdType.MESH

N_DEV = 8
D_MODEL = 1024
ATTN_W = 512
CONV_W = 512
N_HEADS = 8
HEAD_DIM = 64
WIN = 128
DILATIONS = (1, 4, 16)
N_BUCKETS = 32
BUCKET_MAX_EXACT = 16
BUCKET_MAX_DISTANCE = 2048
N_MEM_HEADS = 4
MEM_HEAD_DIM = 256
D_FF = 2816
IN_COLS = 3072
IN_CHUNK = IN_COLS // N_DEV
UP_CHUNK = 2 * D_FF // N_DEV
EPS = 1e-6

ADAM_LR = 0.001
ADAM_B1 = 0.9
ADAM_B2 = 0.999
ADAM_EPS = 1e-08
ADAM_WD = 0.01
ADAM_STEP = 10

SUBLANES = 8
LANES = 128
HALO = 16
TM = 512
TM_FFN = 256
TS_DW = 2048
VMEM_LIMIT = 56 * 1024 * 1024

ROW_RELB, ROW_GMIX, ROW_GXATTN, ROW_GMEM, ROW_GFFN, ROW_GFINAL, ROW_GAC = 0, 8, 16, 24, 32, 40, 48
ROW_WSC, ROW_BFC, ROW_WFC, SMALL_ROWS = 56, 64, 72, 96


def _cparams(n_grid):
    return pltpu.CompilerParams(dimension_semantics=("arbitrary",) * n_grid, vmem_limit_bytes=VMEM_LIMIT)


def _full(shape):
    nd = len(shape)
    return pl.BlockSpec(tuple(shape), lambda *_: (0,) * nd)


def _resident(shape):
    nd = len(shape)
    return pl.BlockSpec(tuple(shape), lambda *_: (0,) * nd, pipeline_mode=pl.Buffered(1))


ANY_SPEC = pl.BlockSpec(memory_space=pl.ANY)
HBM_SPEC = pl.BlockSpec(memory_space=pltpu.HBM)
SEM_SPEC = pl.BlockSpec(memory_space=pltpu.SEMAPHORE)
VMEM_SPEC = pl.BlockSpec(memory_space=pltpu.VMEM)
SMEM_SPEC = pl.BlockSpec(memory_space=pltpu.SMEM)
DATAFLOW = pltpu.SideEffectType.DATAFLOW_SIDE_EFFECTING


def _rms(x):
    r = lax.rsqrt(jnp.mean(x * x, axis=-1, keepdims=True) + EPS)
    return x * r, r


def _rms_bwd(xh, r, g, dy):
    dxh = dy * g
    return r * (dxh - xh * jnp.mean(dxh * xh, axis=-1, keepdims=True))


def _shift_down(u, halo, k):
    ru = pltpu.roll(u, k, 0)
    rh = pltpu.roll(halo, k, 0)
    row = lax.broadcasted_iota(jnp.int32, rh.shape, 0)
    head = jnp.where(row < k, rh, ru[0:SUBLANES])
    return jnp.concatenate([head, ru[SUBLANES:]], axis=0)


def _shift_up(u, halo, k):
    tm = u.shape[0]
    ru = pltpu.roll(u, tm - k, 0)
    rh = pltpu.roll(halo, SUBLANES - k, 0)
    row = lax.broadcasted_iota(jnp.int32, rh.shape, 0)
    tail = jnp.where(row >= SUBLANES - k, rh, ru[tm - SUBLANES:])
    return jnp.concatenate([ru[:tm - SUBLANES], tail], axis=0)


def _causal_conv3(u, halo, w_ref):
    return (_shift_down(u, halo, 2) * w_ref[0:1, :] + _shift_down(u, halo, 1) * w_ref[1:2, :]) + u * w_ref[2:3, :]


def _dot(a, b):
    return jnp.dot(a, b, preferred_element_type=F32)


def _dot_nt(a, b):
    return lax.dot_general(a, b, (((1,), (1,)), ((), ())), preferred_element_type=F32)


def _dot_tn(a, b):
    return lax.dot_general(a, b, (((0,), (0,)), ((), ())), preferred_element_type=F32)


def _sigmoid(x):
    return 1.0 / (1.0 + jnp.exp(-x))


def _bucket_tables():
    qi = np.arange(WIN)[:, None]
    kj = np.arange(2 * WIN)[None, :]
    steps = np.clip(qi + WIN - kj, 0, WIN)
    out = []
    for d in DILATIONS:
        dist = steps * d
        dd = np.maximum(dist, 1).astype(np.float32)
        large = BUCKET_MAX_EXACT + (
            np.log(dd / np.float32(BUCKET_MAX_EXACT)) / np.float32(math.log(BUCKET_MAX_DISTANCE / BUCKET_MAX_EXACT))
            * np.float32(N_BUCKETS - BUCKET_MAX_EXACT)).astype(np.int32)
        large = np.minimum(large, N_BUCKETS - 1)
        out.append(np.where(dist < BUCKET_MAX_EXACT, dist, large).astype(np.int32))
    return jnp.asarray(np.stack(out))


def _bias_fwd(rel_bias, buckets):
    def body(rb_ref, bk_ref, o_ref):
        for p in range(3):
            bk = bk_ref[p]
            for h in range(N_HEADS):
                acc = jnp.zeros((WIN, 2 * WIN), F32)
                for b in range(N_BUCKETS):
                    acc = jnp.where(bk == b, rb_ref[h, b], acc)
                o_ref[p, h] = acc

    return pl.pallas_call(
        body, name="bias_fwd",
        out_shape=jax.ShapeDtypeStruct((3, N_HEADS, WIN, 2 * WIN), F32),
        in_specs=[pl.BlockSpec(memory_space=pltpu.SMEM), pl.BlockSpec(memory_space=pltpu.VMEM)],
        out_specs=pl.BlockSpec(memory_space=pltpu.VMEM),
    )(rel_bias, buckets)


def _bias_bwd(dbias, buckets):
    def body(db_ref, bk_ref, o_ref):
        lane = lax.broadcasted_iota(jnp.int32, (1, D_MODEL), 1)
        rows = []
        for h in range(N_HEADS):
            row = jnp.zeros((1, D_MODEL), F32)
            for b in range(N_BUCKETS):
                tot = jnp.zeros((1, 1), F32)
                for p in range(3):
                    sel = jnp.where(bk_ref[p] == b, db_ref[p, h], 0.0)
                    tot = tot + jnp.sum(jnp.sum(sel, axis=0, keepdims=True), axis=1, keepdims=True)
                row = jnp.where(lane == b, tot, row)
            rows.append(row)
        o_ref[...] = jnp.concatenate(rows, axis=0)

    return pl.pallas_call(
        body, name="bias_bwd",
        out_shape=jax.ShapeDtypeStruct((N_HEADS, D_MODEL), F32),
        in_specs=[pl.BlockSpec(memory_space=pltpu.VMEM), pl.BlockSpec(memory_space=pltpu.VMEM)],
        out_specs=pl.BlockSpec(memory_space=pltpu.VMEM),
    )(dbias, buckets)


def _spread(val, scr_ref, out_refs, dtype):
    out_refs[0][...] = val.astype(dtype)
    n_blk = val.shape[1] // LANES
    for c in range(n_blk):
        scr_ref[c] = val[:, c * LANES:(c + 1) * LANES]
    for o_ref, d in zip(out_refs[1:], DILATIONS[1:]):
        for r in range(d):
            for c in range(n_blk):
                o_ref[r, :, c * LANES:(c + 1) * LANES] = scr_ref.at[c][pl.ds(r, TM // d, stride=d), :].astype(dtype)


def _gather_classes(blk_ref, scr_ref, d):
    n_blk = blk_ref.shape[2] // LANES
    for r in range(d):
        for c in range(n_blk):
            scr_ref.at[c][pl.ds(r, TM // d, stride=d), :] = blk_ref[r, :, c * LANES:(c + 1) * LANES].astype(F32)
    return jnp.concatenate([scr_ref[c] for c in range(n_blk)], axis=1)


def _class_specs(cols):
    return [pl.BlockSpec((TM, cols), lambda i: (i, 0))] + [
        pl.BlockSpec((d, TM // d, cols), lambda i: (0, i, 0)) for d in DILATIONS[1:]]


def _class_shapes(s, cols, dtype):
    return [jax.ShapeDtypeStruct((s, cols), dtype)] + [
        jax.ShapeDtypeStruct((d, s // d, cols), dtype) for d in DILATIONS[1:]]


def _rms_proj(x, g_mix, w_in_g, dep):
    s = x.shape[0]

    def body(x_ref, g_ref, w_ref, dep_ref, h_ref, q1, q4, q16, k1, k4, k16, v1, v4, v16, gb_ref, gc_ref, xi_ref, scr):
        xh, _ = _rms(x_ref[...])
        h = (xh * g_ref[...]).astype(BF16)
        h_ref[...] = h
        proj = jnp.concatenate([_dot(h, w_ref[j]) for j in range(N_DEV)], axis=1)
        _spread(proj[:, 0:512] * (HEAD_DIM ** -0.5), scr, (q1, q4, q16), BF16)
        _spread(proj[:, 512:1024], scr, (k1, k4, k16), BF16)
        _spread(proj[:, 1024:1536], scr, (v1, v4, v16), BF16)
        gb_ref[...] = proj[:, 1536:2048]
        gc_ref[...] = proj[:, 2048:2560]
        xi_ref[...] = proj[:, 2560:3072]

    row = lambda n: pl.BlockSpec((TM, n), lambda i: (i, 0))
    res = pl.pallas_call(
        body, name="rms_proj", grid=(s // TM,),
        out_shape=[jax.ShapeDtypeStruct((s, D_MODEL), BF16)] + _class_shapes(s, 512, BF16) * 3
        + [jax.ShapeDtypeStruct((s, 512), F32)] * 3,
        in_specs=[row(D_MODEL), _full(g_mix.shape), _full(w_in_g.shape), ANY_SPEC],
        out_specs=[row(D_MODEL)] + _class_specs(512) * 3 + [row(512)] * 3,
        scratch_shapes=[pltpu.VMEM((512 // LANES, TM, LANES), F32)],
        compiler_params=_cparams(1),
    )(x, g_mix, w_in_g, dep)
    return res[0], res[1:4], res[4:7], res[7:10], res[10], res[11], res[12]


def _pair_split(x2):
    lane = lax.broadcasted_iota(jnp.int32, x2.shape, 1)
    zero = jnp.zeros_like(x2)
    return jnp.where(lane < HEAD_DIM, x2, zero), jnp.where(lane >= HEAD_DIM, x2, zero)


def _pair_join(even, odd):
    lane = lax.broadcasted_iota(jnp.int32, (even.shape[0], LANES), 1)
    return jnp.where(lane < HEAD_DIM, even, odd)


def _band_mask(blk):
    qi = lax.broadcasted_iota(jnp.int32, (WIN, 2 * WIN), 0)
    kj = lax.broadcasted_iota(jnp.int32, (WIN, 2 * WIN), 1)
    steps = qi + WIN - kj
    return (steps >= 0) & (steps <= WIN) & (kj >= jnp.where(blk > 0, 0, WIN))


def _swa_fwd(qc, kc, vc, bias, dil):
    nb = qc.shape[1] // WIN

    def body(q_ref, kp_ref, kc_ref, vp_ref, vc_ref, b_ref, o_ref, lse_ref, s_scr, p_scr):
        for a in range(N_HEADS // 2):
            sl = slice(a * LANES, (a + 1) * LANES)
            k2 = jnp.concatenate([kp_ref[0, :, sl], kc_ref[0, :, sl]], axis=0)
            for e, qh in enumerate(_pair_split(q_ref[0, :, sl])):
                s_scr[2 * a + e] = _dot_nt(qh, k2)
        lg = jnp.where(_band_mask(pl.program_id(1)), s_scr[...] + b_ref[...], -jnp.inf)
        m = jnp.max(lg, axis=-1, keepdims=True)
        p = jnp.exp(lg - m)
        den = jnp.sum(p, axis=-1, keepdims=True)
        p_scr[...] = p.astype(BF16)
        lse = m + jnp.log(den)
        for a in range(N_HEADS // 2):
            sl = slice(a * LANES, (a + 1) * LANES)
            v_even, v_odd = _pair_split(jnp.concatenate([vp_ref[0, :, sl], vc_ref[0, :, sl]], axis=0))
            o2 = _dot(p_scr[2 * a], v_even) + _dot(p_scr[2 * a + 1], v_odd)
            o_ref[0, :, sl] = o2 / _pair_join(den[2 * a], den[2 * a + 1])
            lse_ref[0, :, sl] = _pair_join(lse[2 * a], lse[2 * a + 1])

    cur = pl.BlockSpec((1, WIN, 512), lambda r, b: (r, b, 0))
    prev = pl.BlockSpec((1, WIN, 512), lambda r, b: (r, jnp.maximum(b - 1, 0), 0))
    return pl.pallas_call(
        body, name=f"swa_fwd_d{dil}", grid=(dil, nb),
        out_shape=[jax.ShapeDtypeStruct(qc.shape, F32)] * 2,
        in_specs=[cur, prev, cur, prev, cur, _full(bias.shape)],
        out_specs=[cur] * 2,
        scratch_shapes=[pltpu.VMEM((N_HEADS, WIN, 2 * WIN), F32), pltpu.VMEM((N_HEADS, WIN, 2 * WIN), BF16)],
        compiler_params=_cparams(2),
    )(qc, kc, kc, vc, vc, bias)


def _mix_out(branches, gb, gc, xi, x, w_sc, g_a, g_c, w_out):
    s = x.shape[0]
    tb = TM // SUBLANES

    def body(o1, l1, o4, l4, o16, l16, gb_ref, gc_ref, xi_ref, gch_ref, xih_ref, x_ref, wsc_ref,
             ga_ref, gcv_ref, wout_ref, attn_ref, lse1, lse4, lse16, mixed_ref, x1_ref, scr_a, scr_b, scr_c, scr_d):
        i = pl.program_id(0)
        la, lb, lc = l1[...], _gather_classes(l4, scr_a, 4), _gather_classes(l16, scr_b, 16)
        m_all = jnp.maximum(jnp.maximum(la, lb), lc)
        ea, eb, ec = jnp.exp(la - m_all), jnp.exp(lb - m_all), jnp.exp(lc - m_all)
        den = (ea + eb) + ec
        num = (ea * o1[...] + eb * _gather_classes(o4, scr_c, 4)) + ec * _gather_classes(o16, scr_d, 16)
        attn = num / den
        attn_ref[...] = attn
        _spread(m_all + jnp.log(den), scr_a, (lse1, lse4, lse16), F32)
        xa, _ = _rms(attn)
        u = gc_ref[...] * xi_ref[...]
        uh = jnp.where(i > 0, gch_ref[...] * xih_ref[...], 0.0)
        conv = gb_ref[...] * _causal_conv3(u, uh, wsc_ref)
        xc, _ = _rms(conv)
        mixed = jnp.concatenate([xa * ga_ref[...], xc * gcv_ref[...]], axis=1).astype(BF16)
        mixed_ref[...] = mixed
        x1_ref[...] = x_ref[...] + _dot(mixed, wout_ref[...])

    row = lambda n: pl.BlockSpec((TM, n), lambda i: (i, 0))
    halo = pl.BlockSpec((SUBLANES, 512), lambda i: (jnp.maximum(i * tb - 1, 0), 0))
    cs = _class_specs(512)
    flat = [a for br in branches for a in br]
    res = pl.pallas_call(
        body, name="mix_out", grid=(s // TM,),
        out_shape=[jax.ShapeDtypeStruct((s, 512), F32)] + _class_shapes(s, 512, F32)
        + [jax.ShapeDtypeStruct((s, D_MODEL), BF16), jax.ShapeDtypeStruct((s, D_MODEL), F32)],
        in_specs=[cs[0], cs[0], cs[1], cs[1], cs[2], cs[2], row(512), row(512), row(512), halo, halo,
                  row(D_MODEL), _full(w_sc.shape), _full(g_a.shape), _full(g_c.shape), _full(w_out.shape)],
        out_specs=[row(512)] + cs + [row(D_MODEL), row(D_MODEL)],
        scratch_shapes=[pltpu.VMEM((512 // LANES, TM, LANES), F32)] * 4,
        compiler_params=_cparams(1),
    )(*flat, gb, gc, xi, gc, xi, x, w_sc, g_a, g_c, w_out)
    return res[0], res[1:4], res[4], res[5]


def _mem_kv(mem, g_mem, w_xk, w_xv):
    def body(mem_ref, g_ref, wk_ref, wv_ref, mn_ref, k_ref, v_ref):
        xh, _ = _rms(mem_ref[...])
        mn = (xh * g_ref[...]).astype(BF16)
        mn_ref[...] = mn
        k_ref[...] = _dot(mn, wk_ref[...]).astype(BF16)
        v_ref[...] = _dot(mn, wv_ref[...]).astype(BF16)

    vm = pl.BlockSpec(memory_space=pltpu.VMEM)
    return pl.pallas_call(
        body, name="mem_kv",
        out_shape=[jax.ShapeDtypeStruct(mem.shape, BF16)] * 3,
        in_specs=[vm] * 4, out_specs=[vm] * 3,
        compiler_params=pltpu.CompilerParams(vmem_limit_bytes=VMEM_LIMIT),
    )(mem, g_mem, w_xk, w_xv)


def _xattn_fwd(x1, g, w_xq, k, v, w_xo):
    s = x1.shape[0]

    def body(x1_ref, g_ref, wq_ref, k_ref, v_ref, wo_ref, h2_ref, q_ref, o_ref, x2_ref):
        x1v = x1_ref[...]
        xh, _ = _rms(x1v)
        h2 = (xh * g_ref[...]).astype(BF16)
        h2_ref[...] = h2
        qb = _dot(h2, wq_ref[...]).astype(BF16)
        q_ref[...] = qb
        outs = []
        for h in range(N_MEM_HEADS):
            sl = slice(h * MEM_HEAD_DIM, (h + 1) * MEM_HEAD_DIM)
            lg = _dot_nt(qb[:, sl], k_ref[:, sl]) * (MEM_HEAD_DIM ** -0.5)
            p = jnp.exp(lg - jnp.max(lg, axis=-1, keepdims=True))
            p = p / jnp.sum(p, axis=-1, keepdims=True)
            outs.append(_dot(p.astype(BF16), v_ref[:, sl]))
        o = jnp.concatenate(outs, axis=1).astype(BF16)
        o_ref[...] = o
        x2_ref[...] = x1v + _dot(o, wo_ref[...])

    row = pl.BlockSpec((TM, D_MODEL), lambda i: (i, 0))
    return pl.pallas_call(
        body, name="xattn_fwd", grid=(s // TM,),
        out_shape=[jax.ShapeDtypeStruct((s, D_MODEL), BF16)] * 3 + [jax.ShapeDtypeStruct((s, D_MODEL), F32)],
        in_specs=[row, _full(g.shape), _full(w_xq.shape), _full(k.shape), _full(v.shape), _full(w_xo.shape)],
        out_specs=[row] * 4,
        compiler_params=_cparams(1),
    )(x1, g, w_xq, k, v, w_xo)


def _ffn_conv(h_ext, wup_ref, wfc_ref, bfc_ref, j):
    u = _dot(h_ext, wup_ref[j])
    u2, u1 = pltpu.roll(u, 2, 0), pltpu.roll(u, 1, 0)
    w = wfc_ref[j]
    c = ((u2 * w[0:1, :] + u1 * w[1:2, :]) + u * w[2:3, :]) + bfc_ref[j]
    return c[HALO:], u2[HALO:], u1[HALO:], u[HALO:]


def _ffn_fwd(x2, g, w_up_g, w_fc, b_fc, w_down_g, g_final, target):
    s = x2.shape[0]
    tb = TM_FFN // HALO
    half = N_DEV // 2

    def body(x_ref, xp_ref, g_ref, wup_ref, wfc_ref, bfc_ref, wd_ref, gf_ref, t_ref, h_ref, act_ref, dx3_ref,
             loss_ref, dgf_ref):
        i = pl.program_id(0)

        @pl.when(i == 0)
        def _():
            loss_ref[...] = jnp.zeros_like(loss_ref)
            dgf_ref[...] = jnp.zeros_like(dgf_ref)

        x2v = x_ref[...]
        gv = g_ref[...]
        h = (_rms(x2v)[0] * gv).astype(BF16)
        h_ref[...] = h
        hp = jnp.where(i > 0, _rms(xp_ref[...])[0] * gv, 0.0).astype(BF16)
        h_ext = jnp.concatenate([hp, h], axis=0)
        down = jnp.zeros((TM_FFN, D_MODEL), F32)
        for j in range(half):
            cg = _ffn_conv(h_ext, wup_ref, wfc_ref, bfc_ref, j)[0]
            cv = _ffn_conv(h_ext, wup_ref, wfc_ref, bfc_ref, j + half)[0]
            a = ((cg * _sigmoid(cg)) * cv).astype(BF16)
            act_ref[j] = a
            down = down + _dot(a, wd_ref[j])
        x3 = x2v + down
        xh, r = _rms(x3)
        gf = gf_ref[...]
        e = xh * gf - t_ref[...]
        loss_ref[...] += 0.5 * jnp.sum(jnp.sum(e * e, axis=1, keepdims=True), axis=0, keepdims=True) / D_MODEL
        dy = e * (1.0 / D_MODEL)
        dgf_ref[0:1, :] += jnp.sum(dy * xh, axis=0, keepdims=True)
        dx3_ref[...] = _rms_bwd(xh, r, gf, dy)

    row = pl.BlockSpec((TM_FFN, D_MODEL), lambda i: (i, 0))
    prev = pl.BlockSpec((HALO, D_MODEL), lambda i: (jnp.maximum(i * tb - 1, 0), 0))
    return pl.pallas_call(
        body, name="ffn_fwd", grid=(s // TM_FFN,),
        out_shape=[jax.ShapeDtypeStruct((s, D_MODEL), BF16), jax.ShapeDtypeStruct((half, s, UP_CHUNK), BF16),
                   jax.ShapeDtypeStruct((s, D_MODEL), F32), jax.ShapeDtypeStruct((SUBLANES, 128), F32),
                   jax.ShapeDtypeStruct((SUBLANES, D_MODEL), F32)],
        in_specs=[row, prev, _full(g.shape), _resident(w_up_g.shape), _full(w_fc.shape), _full(b_fc.shape),
                  _resident(w_down_g.shape), _full(g_final.shape), row],
        out_specs=[row, pl.BlockSpec((half, TM_FFN, UP_CHUNK), lambda i: (0, i, 0)), row,
                   _full((SUBLANES, 128)), _full((SUBLANES, D_MODEL))],
        compiler_params=_cparams(1),
    )(x2, x2, g, w_up_g, w_fc, b_fc, w_down_g, g_final, target)


def _ffn_bwd(dx3, h3, x2, g, w_up_g, w_fc, b_fc, w_down_g):
    s = x2.shape[0]
    tb = TM_FFN // HALO
    last = s // HALO - 1
    n_tiles = s // TM_FFN
    half = N_DEV // 2
    n_ext = TM_FFN + HALO

    def body(dx_ref, dxn_ref, hp_ref, h_ref, hn_ref, x2_ref, g_ref, wup_ref, wfc_ref, bfc_ref, wd_ref,
             dup_ref, dx2_ref, dg_ref, dwfc_ref, dbfc_ref):
        i = pl.program_id(0)

        @pl.when(i == 0)
        def _():
            dg_ref[...] = jnp.zeros_like(dg_ref)
            dwfc_ref[...] = jnp.zeros_like(dwfc_ref)
            dbfc_ref[...] = jnp.zeros_like(dbfc_ref)

        dxv = dx_ref[...]
        dxn = jnp.where(i < n_tiles - 1, dxn_ref[...], 0.0)
        dx_ext = jnp.concatenate([dxv, dxn], axis=0).astype(BF16)
        hp = jnp.where(i > 0, hp_ref[...], jnp.zeros_like(hp_ref))
        h_ext = jnp.concatenate([hp, h_ref[...], hn_ref[...]], axis=0)
        dh = jnp.zeros((TM_FFN, D_MODEL), F32)
        for j in range(half):
            cg, g2, g1, g0 = _ffn_conv(h_ext, wup_ref, wfc_ref, bfc_ref, j)
            cv, v2, v1, v0 = _ffn_conv(h_ext, wup_ref, wfc_ref, bfc_ref, j + half)
            dact = _dot_nt(dx_ext, wd_ref[j])
            sg = _sigmoid(cg)
            parts = ((j + half, dact * (cg * sg), v2, v1, v0),
                     (j, (dact * cv) * (sg * (1.0 + cg * (1.0 - sg))), g2, g1, g0))
            for jj, dc, u2, u1, u0 in parts:
                dct = dc[:TM_FFN]
                dbfc_ref[jj:jj + 1, :] += jnp.sum(dct, axis=0, keepdims=True)
                dwfc_ref[0, jj:jj + 1, :] += jnp.sum(dct * u2[:TM_FFN], axis=0, keepdims=True)
                dwfc_ref[1, jj:jj + 1, :] += jnp.sum(dct * u1[:TM_FFN], axis=0, keepdims=True)
                dwfc_ref[2, jj:jj + 1, :] += jnp.sum(dct * u0[:TM_FFN], axis=0, keepdims=True)
                w = wfc_ref[jj]
                du = ((dct * w[2:3, :] + pltpu.roll(dc, n_ext - 1, 0)[:TM_FFN] * w[1:2, :])
                      + pltpu.roll(dc, n_ext - 2, 0)[:TM_FFN] * w[0:1, :]).astype(BF16)
                dup_ref[jj] = du
                dh = dh + _dot_nt(du, wup_ref[jj])
        xh, r = _rms(x2_ref[...])
        dg_ref[0:1, :] += jnp.sum(dh * xh, axis=0, keepdims=True)
        dx2_ref[...] = dxv + _rms_bwd(xh, r, g_ref[...], dh)

    row = pl.BlockSpec((TM_FFN, D_MODEL), lambda i: (i, 0))
    prev = pl.BlockSpec((HALO, D_MODEL), lambda i: (jnp.maximum(i * tb - 1, 0), 0))
    nxt = pl.BlockSpec((HALO, D_MODEL), lambda i: (jnp.minimum((i + 1) * tb, last), 0))
    return pl.pallas_call(
        body, name="ffn_bwd", grid=(n_tiles,),
        out_shape=[jax.ShapeDtypeStruct((N_DEV, s, UP_CHUNK), BF16), jax.ShapeDtypeStruct((s, D_MODEL), F32),
                   jax.ShapeDtypeStruct((SUBLANES, D_MODEL), F32), jax.ShapeDtypeStruct((3, N_DEV, UP_CHUNK), F32),
                   jax.ShapeDtypeStruct((N_DEV, UP_CHUNK), F32)],
        in_specs=[row, nxt, prev, row, nxt, row, _full(g.shape), _resident(w_up_g.shape), _full(w_fc.shape),
                  _full(b_fc.shape), _resident(w_down_g.shape)],
        out_specs=[pl.BlockSpec((N_DEV, TM_FFN, UP_CHUNK), lambda i: (0, i, 0)), row, _full((SUBLANES, D_MODEL)),
                   _full((3, N_DEV, UP_CHUNK)), _full((N_DEV, UP_CHUNK))],
        compiler_params=_cparams(1),
    )(dx3, dx3, h3, h3, h3, x2, g, w_up_g, w_fc, b_fc, w_down_g)


def _xattn_bwd(dx2, o, q, k, v, w_xo, w_xq, x1, g, dep):
    s = x1.shape[0]

    def body(dx2_ref, o_ref, q_ref, k_ref, v_ref, wo_ref, wq_ref, x1_ref, g_ref, dep_ref, dq_ref, dx1_ref, dk_ref,
             dv_ref, dg_ref):
        @pl.when(pl.program_id(0) == 0)
        def _():
            dk_ref[...] = jnp.zeros_like(dk_ref)
            dv_ref[...] = jnp.zeros_like(dv_ref)
            dg_ref[...] = jnp.zeros_like(dg_ref)

        dx2v = dx2_ref[...]
        do = _dot_nt(dx2v.astype(BF16), wo_ref[...])
        dqs = []
        for h in range(N_MEM_HEADS):
            sl = slice(h * MEM_HEAD_DIM, (h + 1) * MEM_HEAD_DIM)
            qh, kh, vh = q_ref[:, sl], k_ref[:, sl], v_ref[:, sl]
            lg = _dot_nt(qh, kh) * (MEM_HEAD_DIM ** -0.5)
            p = jnp.exp(lg - jnp.max(lg, axis=-1, keepdims=True))
            p = p / jnp.sum(p, axis=-1, keepdims=True)
            doh = do[:, sl].astype(BF16)
            dp = _dot_nt(doh, vh)
            ds = (p * (dp - jnp.sum(p * dp, axis=-1, keepdims=True)) * (MEM_HEAD_DIM ** -0.5)).astype(BF16)
            dqs.append(_dot(ds, kh))
            dk_ref[:, sl] += _dot_tn(ds, qh)
            dv_ref[:, sl] += _dot_tn(p.astype(BF16), doh)
        dq = jnp.concatenate(dqs, axis=1).astype(BF16)
        dq_ref[...] = dq
        dh2 = _dot_nt(dq, wq_ref[...])
        xh, r = _rms(x1_ref[...])
        dg_ref[0:1, :] += jnp.sum(dh2 * xh, axis=0, keepdims=True)
        dx1_ref[...] = dx2v + _rms_bwd(xh, r, g_ref[...], dh2)

    row = pl.BlockSpec((TM, D_MODEL), lambda i: (i, 0))
    return pl.pallas_call(
        body, name="xattn_bwd", grid=(s // TM,),
        out_shape=[jax.ShapeDtypeStruct((s, D_MODEL), BF16), jax.ShapeDtypeStruct((s, D_MODEL), F32),
                   jax.ShapeDtypeStruct(k.shape, F32), jax.ShapeDtypeStruct(k.shape, F32),
                   jax.ShapeDtypeStruct((SUBLANES, D_MODEL), F32)],
        in_specs=[row, row, row, _full(k.shape), _full(v.shape), _full(w_xo.shape), _full(w_xq.shape), row,
                  _full(g.shape), ANY_SPEC],
        out_specs=[row, row, _full(k.shape), _full(k.shape), _full((SUBLANES, D_MODEL))],
        compiler_params=_cparams(1),
    )(dx2, o, q, k, v, w_xo, w_xq, x1, g, dep)


def _mem_kv_bwd(dk, dv, mem_n, mem, w_xk, w_xv):
    def body(dk_ref, dv_ref, mn_ref, mem_ref, wk_ref, wv_ref, dwk_ref, dwv_ref, dg_ref):
        dkb, dvb = dk_ref[...].astype(BF16), dv_ref[...].astype(BF16)
        mn = mn_ref[...]
        dwk_ref[...] = _dot_tn(mn, dkb).astype(BF16)
        dwv_ref[...] = _dot_tn(mn, dvb).astype(BF16)
        dmn = _dot_nt(dkb, wk_ref[...]) + _dot_nt(dvb, wv_ref[...])
        xh, _ = _rms(mem_ref[...])
        dg_ref[...] = jnp.zeros_like(dg_ref)
        dg_ref[0:1, :] = jnp.sum(dmn * xh, axis=0, keepdims=True)

    vm = pl.BlockSpec(memory_space=pltpu.VMEM)
    return pl.pallas_call(
        body, name="mem_kv_bwd",
        out_shape=[jax.ShapeDtypeStruct(w_xk.shape, BF16), jax.ShapeDtypeStruct(w_xv.shape, BF16),
                   jax.ShapeDtypeStruct((SUBLANES, D_MODEL), F32)],
        in_specs=[vm] * 6, out_specs=[vm] * 3,
        compiler_params=pltpu.CompilerParams(vmem_limit_bytes=VMEM_LIMIT),
    )(dk, dv, mem_n, mem, w_xk, w_xv)


def _mix_out_bwd(dx1, w_out, attn, gb, gc, xi, w_sc, g_a, g_c, dep):
    s = dx1.shape[0]
    tb = TM // SUBLANES

    def body(dx1_ref, wout_ref, attn_ref, gb_ref, gc_ref, xi_ref, gch_ref, xih_ref, wsc_ref, ga_ref, gcv_ref, dep_ref,
             da1, da4, da16, dd1, dd4, dd16, dgb_ref, dcv_ref, dga_ref, dgc_ref, dwsc_ref, scr):
        i = pl.program_id(0)

        @pl.when(i == 0)
        def _():
            dga_ref[...] = jnp.zeros_like(dga_ref)
            dgc_ref[...] = jnp.zeros_like(dgc_ref)
            dwsc_ref[...] = jnp.zeros_like(dwsc_ref)

        dmixed = _dot_nt(dx1_ref[...].astype(BF16), wout_ref[...])
        da, dcn = dmixed[:, :ATTN_W], dmixed[:, ATTN_W:]
        attn = attn_ref[...]
        xa, ra = _rms(attn)
        dga_ref[0:1, :] += jnp.sum(da * xa, axis=0, keepdims=True)
        dattn = _rms_bwd(xa, ra, ga_ref[...], da)
        _spread(dattn, scr, (da1, da4, da16), F32)
        prod = dattn * attn
        dd = jnp.concatenate(
            [jnp.broadcast_to(jnp.sum(prod[:, h * HEAD_DIM:(h + 1) * HEAD_DIM], axis=-1, keepdims=True),
                              (TM, HEAD_DIM)) for h in range(N_HEADS)], axis=1)
        _spread(dd, scr, (dd1, dd4, dd16), F32)
        gbv = gb_ref[...]
        u = gc_ref[...] * xi_ref[...]
        uh = jnp.where(i > 0, gch_ref[...] * xih_ref[...], 0.0)
        u2, u1 = _shift_down(u, uh, 2), _shift_down(u, uh, 1)
        cv = (u2 * wsc_ref[0:1, :] + u1 * wsc_ref[1:2, :]) + u * wsc_ref[2:3, :]
        xc, rc = _rms(gbv * cv)
        dgc_ref[0:1, :] += jnp.sum(dcn * xc, axis=0, keepdims=True)
        dconv = _rms_bwd(xc, rc, gcv_ref[...], dcn)
        dgb_ref[...] = dconv * cv
        dcv = dconv * gbv
        dcv_ref[...] = dcv
        dwsc_ref[0:1, :] += jnp.sum(dcv * u2, axis=0, keepdims=True)
        dwsc_ref[1:2, :] += jnp.sum(dcv * u1, axis=0, keepdims=True)
        dwsc_ref[2:3, :] += jnp.sum(dcv * u, axis=0, keepdims=True)

    row = lambda n: pl.BlockSpec((TM, n), lambda i: (i, 0))
    halo = pl.BlockSpec((SUBLANES, 512), lambda i: (jnp.maximum(i * tb - 1, 0), 0))
    acc = _full((SUBLANES, 512))
    res = pl.pallas_call(
        body, name="mix_out_bwd", grid=(s // TM,),
        out_shape=_class_shapes(s, 512, F32) * 2 + [jax.ShapeDtypeStruct((s, 512), F32)] * 2
        + [jax.ShapeDtypeStruct((SUBLANES, 512), F32)] * 3,
        in_specs=[row(D_MODEL), _full(w_out.shape), row(512), row(512), row(512), row(512), halo, halo,
                  _full(w_sc.shape), _full(g_a.shape), _full(g_c.shape), ANY_SPEC],
        out_specs=_class_specs(512) * 2 + [row(512)] * 2 + [acc] * 3,
        scratch_shapes=[pltpu.VMEM((512 // LANES, TM, LANES), F32)],
        compiler_params=_cparams(1),
    )(dx1, w_out, attn, gb, gc, xi, gc, xi, w_sc, g_a, g_c, dep)
    return res[0:3], res[3:6], res[6], res[7], res[8], res[9], res[10]


def _swa_bwd(qc, kc, vc, doc, lsec, ddc, bias, dil, dep):
    nb = qc.shape[1] // WIN

    def body(q_ref, qn_ref, kp_ref, kc_ref, vp_ref, vc_ref, do_ref, don_ref, lse_ref, lsen_ref, dd_ref, ddn_ref,
             b_ref, dep_ref, dq_ref, dk_ref, dv_ref, db_ref, s_scr, dp_scr, sn_scr, dpn_scr, ds_scr, p_scr, dsn_scr,
             pn_scr):
        r, b = pl.program_id(0), pl.program_id(1)

        @pl.when((r == 0) & (b == 0))
        def _():
            db_ref[...] = jnp.zeros_like(db_ref)

        pairs = [slice(a * LANES, (a + 1) * LANES) for a in range(N_HEADS // 2)]
        for a, sl in enumerate(pairs):
            k2 = jnp.concatenate([kp_ref[0, :, sl], kc_ref[0, :, sl]], axis=0)
            v2 = jnp.concatenate([vp_ref[0, :, sl], vc_ref[0, :, sl]], axis=0)
            q_eo = _pair_split(q_ref[0, :, sl])
            do_eo = _pair_split(do_ref[0, :, sl].astype(BF16))
            qn_eo = _pair_split(qn_ref[0, :, sl])
            don_eo = _pair_split(don_ref[0, :, sl].astype(BF16))
            for e in range(2):
                s_scr[2 * a + e] = _dot_nt(q_eo[e], k2)
                dp_scr[2 * a + e] = _dot_nt(do_eo[e], v2)
                sn_scr[2 * a + e] = _dot_nt(qn_eo[e], kc_ref[0, :, sl])
                dpn_scr[2 * a + e] = _dot_nt(don_eo[e], vc_ref[0, :, sl])
        per_head = lambda ref: jnp.stack([ref[0, :, h * HEAD_DIM:h * HEAD_DIM + 1] for h in range(N_HEADS)])
        bias = b_ref[...]
        p = jnp.exp(jnp.where(_band_mask(b), s_scr[...] + bias, -jnp.inf) - per_head(lse_ref))
        ds = p * (dp_scr[...] - per_head(dd_ref))
        db_ref[...] += ds
        ds_scr[...] = ds.astype(BF16)
        p_scr[...] = p.astype(BF16)
        qi = lax.broadcasted_iota(jnp.int32, (WIN, WIN), 0)
        kj = lax.broadcasted_iota(jnp.int32, (WIN, WIN), 1)
        valid_n = kj >= qi + jnp.where(b + 1 < nb, 0, WIN)
        pn = jnp.exp(jnp.where(valid_n, sn_scr[...] + bias[:, :, :WIN], -jnp.inf) - per_head(lsen_ref))
        dsn_scr[...] = (pn * (dpn_scr[...] - per_head(ddn_ref))).astype(BF16)
        pn_scr[...] = pn.astype(BF16)
        for a, sl in enumerate(pairs):
            k_eo = _pair_split(jnp.concatenate([kp_ref[0, :, sl], kc_ref[0, :, sl]], axis=0))
            q_eo = _pair_split(q_ref[0, :, sl])
            do_eo = _pair_split(do_ref[0, :, sl].astype(BF16))
            qn_eo = _pair_split(qn_ref[0, :, sl])
            don_eo = _pair_split(don_ref[0, :, sl].astype(BF16))
            dq, dk, dv = None, None, None
            for e in range(2):
                h = 2 * a + e
                terms = (_dot(ds_scr[h], k_eo[e]),
                         _dot_tn(ds_scr[h, :, WIN:], q_eo[e]) + _dot_tn(dsn_scr[h], qn_eo[e]),
                         _dot_tn(p_scr[h, :, WIN:], do_eo[e]) + _dot_tn(pn_scr[h], don_eo[e]))
                dq, dk, dv = terms if e == 0 else (dq + terms[0], dk + terms[1], dv + terms[2])
            dq_ref[0, :, sl] = dq
            dk_ref[0, :, sl] = dk
            dv_ref[0, :, sl] = dv

    cur = pl.BlockSpec((1, WIN, 512), lambda r, b: (r, b, 0))
    prev = pl.BlockSpec((1, WIN, 512), lambda r, b: (r, jnp.maximum(b - 1, 0), 0))
    nxt = pl.BlockSpec((1, WIN, 512), lambda r, b: (r, jnp.minimum(b + 1, nb - 1), 0))
    wide, narrow = (N_HEADS, WIN, 2 * WIN), (N_HEADS, WIN, WIN)
    return pl.pallas_call(
        body, name=f"swa_bwd_d{dil}", grid=(dil, nb),
        out_shape=[jax.ShapeDtypeStruct(qc.shape, F32)] * 3 + [jax.ShapeDtypeStruct(bias.shape, F32)],
        in_specs=[cur, nxt, prev, cur, prev, cur, cur, nxt, cur, nxt, cur, nxt, _full(bias.shape), ANY_SPEC],
        out_specs=[cur] * 3 + [_full(bias.shape)],
        scratch_shapes=[pltpu.VMEM(wide, F32), pltpu.VMEM(wide, F32), pltpu.VMEM(narrow, F32),
                        pltpu.VMEM(narrow, F32), pltpu.VMEM(wide, BF16), pltpu.VMEM(wide, BF16),
                        pltpu.VMEM(narrow, BF16), pltpu.VMEM(narrow, BF16)],
        compiler_params=_cparams(2),
    )(qc, qc, kc, kc, vc, vc, doc, doc, lsec, lsec, ddc, ddc, bias, dep)


def _in_proj_bwd(dqs, dks, dvs, dgb, dcv, gc, xi, w_sc, w_in_g, x, g_mix, dx1):
    s = x.shape[0]
    tb = TM // SUBLANES
    last = s // SUBLANES - 1
    n_tiles = s // TM

    def body(dq1, dq4, dq16, dk1, dk4, dk16, dv1, dv4, dv16, dgb_ref, dcv_ref, dcvn_ref, gc_ref, xi_ref, wsc_ref,
             win_ref, x_ref, g_ref, dx1_ref, dproj_ref, gx_ref, dg_ref, scr_a, scr_b):
        i = pl.program_id(0)

        @pl.when(i == 0)
        def _():
            dg_ref[...] = jnp.zeros_like(dg_ref)

        d0 = dcv_ref[...]
        dn = jnp.where(i < n_tiles - 1, dcvn_ref[...], 0.0)
        du = (d0 * wsc_ref[2:3, :] + _shift_up(d0, dn, 1) * wsc_ref[1:2, :]) + _shift_up(d0, dn, 2) * wsc_ref[0:1, :]
        merge = lambda a, b4, b16: (a[...] + _gather_classes(b4, scr_a, 4)) + _gather_classes(b16, scr_b, 16)
        dq = merge(dq1, dq4, dq16) * (HEAD_DIM ** -0.5)
        dk = merge(dk1, dk4, dk16)
        dv = merge(dv1, dv4, dv16)
        dproj = jnp.concatenate([dq, dk, dv, dgb_ref[...], du * xi_ref[...], du * gc_ref[...]], axis=1).astype(BF16)
        dproj_ref[...] = dproj
        dh = jnp.zeros((TM, D_MODEL), F32)
        for j in range(N_DEV):
            dh = dh + _dot_nt(dproj[:, j * IN_CHUNK:(j + 1) * IN_CHUNK], win_ref[j])
        xh, r = _rms(x_ref[...])
        dg_ref[0:1, :] += jnp.sum(dh * xh, axis=0, keepdims=True)
        gx_ref[...] = dx1_ref[...] + _rms_bwd(xh, r, g_ref[...], dh)

    row = lambda n: pl.BlockSpec((TM, n), lambda i: (i, 0))
    nxt = pl.BlockSpec((SUBLANES, 512), lambda i: (jnp.minimum((i + 1) * tb, last), 0))
    return pl.pallas_call(
        body, name="in_proj_bwd", grid=(n_tiles,),
        out_shape=[jax.ShapeDtypeStruct((s, IN_COLS), BF16), jax.ShapeDtypeStruct((s, D_MODEL), F32),
                   jax.ShapeDtypeStruct((SUBLANES, D_MODEL), F32)],
        in_specs=_class_specs(512) * 3 + [row(512), row(512), nxt, row(512), row(512), _full(w_sc.shape),
                                          _full(w_in_g.shape), row(D_MODEL), _full(g_mix.shape), row(D_MODEL)],
        out_specs=[row(IN_COLS), row(D_MODEL), _full((SUBLANES, D_MODEL))],
        scratch_shapes=[pltpu.VMEM((512 // LANES, TM, LANES), F32)] * 2,
        compiler_params=_cparams(1),
    )(*dqs, *dks, *dvs, dgb, dcv, dcv, gc, xi, w_sc, w_in_g, x, g_mix, dx1)


def _dw(a, b, dep, name, a_chunked=False, b_chunked=False, n_chunks=1, chunk_cols=None):
    ts = TS_DW
    if a_chunked:
        nj, s, kk = a.shape
        nn = b.shape[1]
        a_spec = pl.BlockSpec((1, ts, kk), lambda j, t: (j, t, 0))
        b_spec = pl.BlockSpec((ts, nn), lambda j, t: (t, 0))
    elif b_chunked:
        nj, s, nn = b.shape
        kk = a.shape[1]
        a_spec = pl.BlockSpec((ts, kk), lambda j, t: (t, 0))
        b_spec = pl.BlockSpec((1, ts, nn), lambda j, t: (j, t, 0))
    else:
        s, kk = a.shape
        nj, nn = (n_chunks, chunk_cols) if chunk_cols else (1, b.shape[1])
        a_spec = pl.BlockSpec((ts, kk), lambda j, t: (t, 0))
        b_spec = pl.BlockSpec((ts, nn), lambda j, t: (t, j))
    n_steps = s // ts

    def body(a_ref, b_ref, dep_ref, o_ref, acc):
        t = pl.program_id(1)

        @pl.when(t == 0)
        def _():
            acc[...] = jnp.zeros_like(acc)

        av = (a_ref[0] if a_chunked else a_ref[...]).astype(BF16)
        bv = (b_ref[0] if b_chunked else b_ref[...]).astype(BF16)
        acc[...] += _dot_tn(av, bv)

        @pl.when(t == n_steps - 1)
        def _():
            o_ref[0] = acc[...].astype(BF16)

    return pl.pallas_call(
        body, name=name, grid=(nj, n_steps),
        out_shape=jax.ShapeDtypeStruct((nj, kk, nn), BF16),
        in_specs=[a_spec, b_spec, ANY_SPEC],
        out_specs=pl.BlockSpec((1, kk, nn), lambda j, t: (j, 0, 0)),
        scratch_shapes=[pltpu.VMEM((kk, nn), F32)],
        compiler_params=_cparams(2),
    )(a, b, dep)


def _adamw_math(w, g, m, v):
    m2 = ADAM_B1 * m + (1.0 - ADAM_B1) * g
    v2 = ADAM_B2 * v + (1.0 - ADAM_B2) * (g * g)
    m_hat = m2 / (1.0 - ADAM_B1 ** ADAM_STEP)
    v_hat = v2 / (1.0 - ADAM_B2 ** ADAM_STEP)
    delta = -ADAM_LR * (m_hat / (jnp.sqrt(v_hat) + ADAM_EPS) + ADAM_WD * w)
    return delta, m2, v2


def _sum_parts(me, own, p_ref):
    g = None
    for i in range(N_DEV):
        part = jnp.where(me == i, own.astype(F32), p_ref[i].astype(F32))
        g = part if g is None else g + part
    return g


def _adamw_big(name, w, own, parts, m, v, me_arr):
    rr, cc = w.shape
    tr = rr // 4 if rr >= 512 else rr

    def body(me_ref, w_ref, own_ref, p_ref, m_ref, v_ref, g_ref, d_ref, nm_ref, nv_ref):
        g = _sum_parts(me_ref[0], own_ref[...], p_ref)
        g_ref[...] = g
        d_ref[...], nm_ref[...], nv_ref[...] = _adamw_math(w_ref[...], g, m_ref[...], v_ref[...])

    row = pl.BlockSpec((tr, cc), lambda i: (i, 0))
    return pl.pallas_call(
        body, name=name, grid=(rr // tr,),
        out_shape=[jax.ShapeDtypeStruct((rr, cc), F32)] * 4,
        in_specs=[SMEM_SPEC, row, row, pl.BlockSpec((N_DEV, tr, cc), lambda i: (0, i, 0)), row, row],
        out_specs=[row] * 4,
        compiler_params=_cparams(1),
    )(me_arr, w, own, parts, m, v)


def _small_slices():
    return [
        (slice(ROW_RELB, ROW_RELB + 8), slice(0, N_BUCKETS)),
        (slice(ROW_GMIX, ROW_GMIX + 1), slice(0, D_MODEL)),
        (slice(ROW_GAC, ROW_GAC + 1), slice(0, ATTN_W)),
        (slice(ROW_GAC, ROW_GAC + 1), slice(ATTN_W, D_MODEL)),
        (slice(ROW_GXATTN, ROW_GXATTN + 1), slice(0, D_MODEL)),
        (slice(ROW_GMEM, ROW_GMEM + 1), slice(0, D_MODEL)),
        (slice(ROW_GFFN, ROW_GFFN + 1), slice(0, D_MODEL)),
        (slice(ROW_BFC, ROW_BFC + 8), slice(0, UP_CHUNK)),
        (slice(ROW_GFINAL, ROW_GFINAL + 1), slice(0, D_MODEL)),
    ]


def _adamw_small(own, parts, wmv, me_arr):
    slices = _small_slices()
    n = len(slices)

    def body(*refs):
        me_ref, own_ref, p_ref = refs[:3]
        ins = refs[3:3 + 3 * n]
        g_ref = refs[3 + 3 * n]
        outs = refs[4 + 3 * n:]
        g = _sum_parts(me_ref[0], own_ref[...], p_ref)
        g_ref[...] = g
        for a, (rs, ls) in enumerate(slices):
            ga = g[rs, ls]
            outs[4 * a][...] = ga
            outs[4 * a + 1][...], outs[4 * a + 2][...], outs[4 * a + 3][...] = _adamw_math(
                ins[3 * a][...], ga, ins[3 * a + 1][...], ins[3 * a + 2][...])

    vm = pl.BlockSpec(memory_space=pltpu.VMEM)
    flat = [t for trip in wmv for t in trip]
    out_shape = [jax.ShapeDtypeStruct((SMALL_ROWS, D_MODEL), F32)]
    for w, _, _ in wmv:
        out_shape += [jax.ShapeDtypeStruct(w.shape, F32)] * 4
    res = pl.pallas_call(
        body, name="adamw_small", out_shape=out_shape,
        in_specs=[SMEM_SPEC] + [vm] * (2 + 3 * n), out_specs=[vm] * len(out_shape),
    )(me_arr, own, parts, *flat)
    return res[0], [res[1 + 4 * a:5 + 4 * a] for a in range(n)]


def _adamw_shards(items):
    n = len(items)

    def body(*refs):
        for a in range(n):
            w_ref, g_ref, m_ref, v_ref = refs[4 * a:4 * a + 4]
            d_ref, nm_ref, nv_ref = refs[4 * n + 3 * a:4 * n + 3 * a + 3]
            d_ref[...], nm_ref[...], nv_ref[...] = _adamw_math(w_ref[...], g_ref[...], m_ref[...], v_ref[...])

    vm = pl.BlockSpec(memory_space=pltpu.VMEM)
    out_shape = []
    for w, _, _, _ in items:
        out_shape += [jax.ShapeDtypeStruct(w.shape, F32)] * 3
    res = pl.pallas_call(
        body, name="adamw_shards", out_shape=out_shape, in_specs=[vm] * (4 * n), out_specs=[vm] * (3 * n),
    )(*[t for it in items for t in it])
    return [res[3 * a:3 * a + 3] for a in range(n)]


def _mesh_pos():
    return lax.axis_index("x"), lax.axis_index("y"), lax.axis_index("c")


def _dev_index(p):
    return 4 * p[0] + 2 * p[1] + p[2]


def _all_gather(shards):
    n = len(shards)

    def body(*refs):
        ins, outs = refs[:n], refs[n:2 * n]
        send_sems, recv_sems, loc_sems = refs[2 * n:]
        x, y, c = _mesh_pos()
        me, sib = (x, y, c), (x, y, 1 - c)
        chips = [(1 - x, y), (x, 1 - y), (1 - x, 1 - y)]

        def cp(a, k, block, to, src=None):
            dst = outs[a].at[_dev_index(block)]
            return pltpu.make_async_remote_copy(
                src_ref=dst if src is None else src, dst_ref=dst, send_sem=send_sems.at[a, k],
                recv_sem=recv_sems.at[a, k], device_id=to, device_id_type=MESH)

        mine = [pltpu.make_async_copy(ins[a], outs[a].at[_dev_index(me)], loc_sems.at[a]) for a in range(n)]
        for m_ in mine:
            m_.start()
        first = []
        for a in range(n):
            first.append(cp(a, 0, me, sib, src=ins[a]))
            first += [cp(a, 1 + j, me, (*chip, c), src=ins[a]) for j, chip in enumerate(chips)]
        for f in first:
            f.start()
        passed = []
        for a in range(n):
            for j, chip in enumerate(chips):
                cp(a, 1 + j, (*chip, c), me).wait_recv()
                fwd = cp(a, 4 + j, (*chip, c), sib)
                fwd.start()
                passed.append(fwd)
        for a in range(n):
            cp(a, 0, sib, me).wait_recv()
            for j, chip in enumerate(chips):
                cp(a, 4 + j, (*chip, 1 - c), me).wait_recv()
        for f in first + passed:
            f.wait_send()
        for m_ in mine:
            m_.wait()

    hbm = pl.BlockSpec(memory_space=pltpu.HBM)
    return pl.pallas_call(
        body, name="all_gather_weights",
        out_shape=[jax.ShapeDtypeStruct((N_DEV,) + a.shape, a.dtype) for a in shards],
        in_specs=[hbm] * n, out_specs=[hbm] * n,
        scratch_shapes=[pltpu.SemaphoreType.DMA((n, 7)), pltpu.SemaphoreType.DMA((n, 7)),
                        pltpu.SemaphoreType.DMA((n,))],
    )(*shards)


def _peers():
    x, y, c = _mesh_pos()
    return (x, y, c), [((1 - x) if k & 4 else x, (1 - y) if k & 2 else y, (1 - c) if k & 1 else c)
                       for k in range(1, 8)]


def _exchange_copy(src_ref, land_ref, whole, send_sems, recv_sems, a, k, peer, slot):
    src = src_ref if whole else src_ref.at[_dev_index(peer)]
    return pltpu.make_async_remote_copy(
        src_ref=src, dst_ref=land_ref.at[slot], send_sem=send_sems.at[7 * a + k], recv_sem=recv_sems.at[7 * a + k],
        device_id=peer, device_id_type=MESH)


def _exchange_start(name, srcs, whole, dep):
    n = len(srcs)
    lands = [lax.empty(((N_DEV,) + s.shape) if w else s.shape, s.dtype) for s, w in zip(srcs, whole)]

    def body(*refs):
        src_refs, land_refs = refs[:n], refs[n:2 * n]
        send_sems, recv_sems, token = refs[2 * n + 1], refs[2 * n + 2], refs[-1]
        me, peers = _peers()
        for a in range(n):
            for k, peer in enumerate(peers):
                _exchange_copy(src_refs[a], land_refs[a], whole[a], send_sems, recv_sems, a, k, peer,
                               _dev_index(me)).start()
        token[...] = jnp.zeros_like(token)

    res = pl.pallas_call(
        body, name=name,
        out_shape=(pltpu.SemaphoreType.DMA((7 * n,)), pltpu.SemaphoreType.DMA((7 * n,)),
                   *[pltpu.HBM(a.shape, a.dtype) for a in srcs], *[pltpu.HBM(a.shape, a.dtype) for a in lands],
                   jax.ShapeDtypeStruct((SUBLANES, 128), F32)),
        in_specs=[HBM_SPEC] * (2 * n) + [ANY_SPEC],
        out_specs=(SEM_SPEC, SEM_SPEC, *([HBM_SPEC] * (2 * n)), VMEM_SPEC),
        input_output_aliases={i: 2 + i for i in range(2 * n)},
        compiler_params=pltpu.CompilerParams(has_side_effects=DATAFLOW),
    )(*[pltpu.with_memory_space_constraint(a, pltpu.HBM) for a in srcs],
      *[pltpu.with_memory_space_constraint(a, pltpu.HBM) for a in lands], dep)
    return res[0], res[1], list(res[2:2 + n]), list(res[2 + n:2 + 2 * n]), res[-1]


def _exchange_wait(name, started, whole, after, which=None):
    send_sems, recv_sems, srcs, lands, _ = started
    which = list(range(len(srcs))) if which is None else which
    srcs, lands = [srcs[a] for a in which], [lands[a] for a in which]
    n = len(srcs)

    def body(*refs):
        src_refs, land_refs = refs[:n], refs[n:2 * n]
        send_sems, recv_sems = refs[2 * n], refs[2 * n + 1]
        _, peers = _peers()
        for i, a in enumerate(which):
            for k, peer in enumerate(peers):
                cp = _exchange_copy(src_refs[i], land_refs[i], whole[a], send_sems, recv_sems, a, k, peer,
                                    _dev_index(peer))
                cp.wait_send()
                cp.wait_recv()

    res = pl.pallas_call(
        body, name=name,
        out_shape=[pltpu.HBM(a.shape, a.dtype) for a in srcs + lands],
        in_specs=[HBM_SPEC] * (2 * n) + [SEM_SPEC, SEM_SPEC, ANY_SPEC],
        out_specs=[HBM_SPEC] * (2 * n),
        input_output_aliases={i: i for i in range(2 * n)},
        compiler_params=pltpu.CompilerParams(has_side_effects=DATAFLOW),
    )(*srcs, *lands, send_sems, recv_sems, after)
    return list(res[n:])


def _local_step(x, mem, target, rel_bias, g_mix, w_in_g, w_sc, g_a, g_c, g_xattn, g_mem, g_ffn, w_fc, b_fc, g_final,
                dep, late_weights, emit, emit_small):
    s = x.shape[0]
    buckets = _bucket_tables()
    bias = _bias_fwd(rel_bias, buckets)

    h1, qs, ks, vs, gb, gc, xi = _rms_proj(x, g_mix, w_in_g, dep)
    qs, ks, vs = ([a[0][None]] + list(a[1:]) for a in (qs, ks, vs))
    branches = []
    for p, dil in enumerate(DILATIONS):
        o_p, lse_p = _swa_fwd(qs[p], ks[p], vs[p], bias[p], dil)
        branches.append([o_p[0], lse_p[0]] if dil == 1 else [o_p, lse_p])
    w_out = late_weights(["w_out"], branches[-1][0])["w_out"]
    attn, lses, mixed, x1 = _mix_out(branches, gb, gc, xi, x, w_sc, g_a, g_c, w_out)
    lw = late_weights(["w_xq", "w_xk", "w_xv", "w_xo"], x1)
    w_xq, w_xk, w_xv, w_xo = lw["w_xq"], lw["w_xk"], lw["w_xv"], lw["w_xo"]
    mem_n, mk, mv = _mem_kv(mem, g_mem, w_xk, w_xv)
    h2, xq, xo, x2 = _xattn_fwd(x1, g_xattn, w_xq, mk, mv, w_xo)
    lw = late_weights(["w_up", "w_down"], x2)
    w_up_g, w_down_g = lw["w_up"], lw["w_down"]
    h3, act, dx3, loss_acc, dg_final = _ffn_fwd(x2, g_ffn, w_up_g, w_fc, b_fc, w_down_g, g_final, target)

    gw_down = _dw(act, dx3, dep, "dw_down", a_chunked=True)
    dup, dx2, dg_ffn, dw_fc, db_fc = _ffn_bwd(dx3, h3, x2, g_ffn, w_up_g, w_fc, b_fc, w_down_g)
    gw_up = _dw(h3, dup, dep, "dw_up", b_chunked=True)
    tok = emit(dict(w_down=gw_down, w_up=gw_up))
    dxq, dx1, dmk, dmv, dg_xattn = _xattn_bwd(dx2, xo, xq, mk, mv, w_xo, w_xq, x1, g_xattn, tok)
    gw_xo = _dw(xo, dx2, tok, "dw_xo")[0]
    gw_xq = _dw(h2, dxq, tok, "dw_xq")[0]
    gw_xk, gw_xv, dg_mem = _mem_kv_bwd(dmk, dmv, mem_n, mem, w_xk, w_xv)
    tok = emit(dict(w_xo=gw_xo, w_xq=gw_xq, w_xk=gw_xk, w_xv=gw_xv))
    dattns, dds, dgb, dcv, dg_a, dg_c, dw_sc = _mix_out_bwd(dx1, w_out, attn, gb, gc, xi, w_sc, g_a, g_c, tok)
    first = lambda a: [a[0][None]] + list(a[1:])
    dattns, dds, lses = first(dattns), first(dds), first(lses)
    gw_out = _dw(mixed, dx1, tok, "dw_out")[0]
    tok = emit(dict(w_out=gw_out))
    dqs, dks, dvs, dbias = [], [], [], []
    for p, dil in enumerate(DILATIONS):
        dq_p, dk_p, dv_p, db_p = _swa_bwd(qs[p], ks[p], vs[p], dattns[p], lses[p], dds[p], bias[p], dil, tok)
        dqs.append(dq_p[0] if dil == 1 else dq_p)
        dks.append(dk_p[0] if dil == 1 else dk_p)
        dvs.append(dv_p[0] if dil == 1 else dv_p)
        dbias.append(db_p)
    d_relb = _bias_bwd(jnp.stack(dbias), buckets)
    dproj, grad_x, dg_mix = _in_proj_bwd(dqs, dks, dvs, dgb, dcv, gc, xi, w_sc, w_in_g, x, g_mix, dx1)
    pad = lambda a: jnp.pad(a, ((0, 0), (0, D_MODEL - a.shape[1])))
    small = jnp.concatenate([
        d_relb, dg_mix, dg_xattn, dg_mem, dg_ffn, dg_final, jnp.concatenate([dg_a, dg_c], axis=1),
        pad(dw_sc), pad(db_fc), pad(dw_fc.reshape(3 * N_DEV, UP_CHUNK))], axis=0)
    tok = emit_small(small)
    gw_in = _dw(h1, dproj, tok, "dw_in", n_chunks=N_DEV, chunk_cols=IN_CHUNK)
    emit(dict(w_in=gw_in))
    return loss_acc[0, 0], grad_x


def kernel(x, mem, rel_bias, g_mix, w_in, w_short_conv, g_attn_out, g_conv_out, w_out, g_xattn, g_mem, w_xq, w_xk, w_xv, w_xo, g_ffn, w_up, w_ffn_conv, b_ffn_conv, w_down, g_final, loss_target, m_rel_bias, m_g_mix, m_w_in, m_w_short_conv, m_g_attn_out, m_g_conv_out, m_w_out, m_g_xattn, m_g_mem, m_w_xq, m_w_xk, m_w_xv, m_w_xo, m_g_ffn, m_w_up, m_w_ffn_conv, m_b_ffn_conv, m_w_down, m_g_final, v_rel_bias, v_g_mix, v_w_in, v_w_short_conv, v_g_attn_out, v_g_conv_out, v_w_out, v_g_xattn, v_g_mem, v_w_xq, v_w_xk, v_w_xv, v_w_xo, v_g_ffn, v_w_up, v_w_ffn_conv, v_b_ffn_conv, v_w_down, v_g_final):
    me = _dev_index(_mesh_pos())
    me_arr = me.reshape(1).astype(jnp.int32)

    big_names = ["w_in", "w_out", "w_xq", "w_xk", "w_xv", "w_xo", "w_up", "w_down"]
    late_names = big_names[1:]
    big_w = dict(w_in=w_in[0], w_out=w_out[0], w_xq=w_xq[0], w_xk=w_xk[0], w_xv=w_xv[0], w_xo=w_xo[0],
                 w_up=w_up[0], w_down=w_down[0])
    big_m = dict(w_in=m_w_in[0], w_out=m_w_out[0], w_xq=m_w_xq[0], w_xk=m_w_xk[0], w_xv=m_w_xv[0], w_xo=m_w_xo[0],
                 w_up=m_w_up[0], w_down=m_w_down[0])
    big_v = dict(w_in=v_w_in[0], w_out=v_w_out[0], w_xq=v_w_xq[0], w_xk=v_w_xk[0], w_xv=v_w_xv[0], w_xo=v_w_xo[0],
                 w_up=v_w_up[0], w_down=v_w_down[0])
    shard_shape = {n: big_w[n].shape for n in big_names}

    w_in_g, w_sc_g, w_fc_full = _all_gather([big_w["w_in"].astype(BF16), w_short_conv[0], w_ffn_conv[0]])
    w_sc_full = w_sc_g.transpose(1, 0, 2).reshape(3, CONV_W)
    late_shards = [big_w[n].astype(BF16) for n in late_names]
    ag = _exchange_start("gather_weights_start", late_shards, [True] * len(late_names), w_in_g)

    def late_weights(names, after):
        which = [late_names.index(n) for n in names]
        lands = _exchange_wait("gather_" + "_".join(names) + "_wait", ag, [True] * len(late_names), after, which)
        out = {}
        for n, a, land in zip(names, which, lands):
            full = lax.dynamic_update_index_in_dim(land, late_shards[a], me, 0)
            if n == "w_up":
                out[n] = full
            elif n == "w_down":
                out[n] = full.reshape(N_DEV // 2, UP_CHUNK, D_MODEL)
            else:
                out[n] = full.reshape(D_MODEL, D_MODEL)
        return out

    sent = []

    def emit(grads):
        names = list(grads)
        blocks = [grads[n].reshape((N_DEV,) + shard_shape[n]) for n in names]
        own = [lax.dynamic_index_in_dim(b, me, 0, keepdims=False) for b in blocks]
        started = _exchange_start("scatter_" + "_".join(names) + "_start", blocks, [False] * len(names), me_arr)
        sent.append((names, own, started))
        return started[-1]

    def emit_small(small):
        sent_small.append((small, _exchange_start("gather_small_start", [small], [True], me_arr)))
        return sent_small[0][1][-1]

    sent_small = []
    loss_part, grad_x = _local_step(
        x[0], mem[0], loss_target[0], rel_bias, g_mix, w_in_g, w_sc_full, g_attn_out, g_conv_out, g_xattn, g_mem,
        g_ffn, w_fc_full, b_ffn_conv.reshape(N_DEV, 1, UP_CHUNK), g_final.reshape(1, D_MODEL), ag[-1],
        late_weights, emit, emit_small)
    loss = lax.psum(loss_part, ("x", "y", "c"))

    small_g, small_started = sent_small[0]
    after = sent[-1][2][-1]
    small_parts = _exchange_wait("gather_small_wait", small_started, [True], after)[0]
    big_out = {}
    after = small_parts
    for names, own, started in sent:
        lands = _exchange_wait("scatter_" + "_".join(names) + "_wait", started, [False] * len(names), after)
        for n, own_n, land in zip(names, own, lands):
            res = _adamw_big("adamw_" + n, big_w[n], own_n, land, big_m[n], big_v[n], me_arr)
            big_out[n] = [r[None] for r in res]
            after = res[0]

    as_rows = lambda a: a.reshape(N_DEV, UP_CHUNK)
    row1 = lambda a: a.reshape(1, D_MODEL)
    small_names = ["rel_bias", "g_mix", "g_attn_out", "g_conv_out", "g_xattn", "g_mem", "g_ffn", "b_ffn_conv", "g_final"]
    wmv = [
        (rel_bias, m_rel_bias, v_rel_bias), (g_mix, m_g_mix, v_g_mix), (g_attn_out, m_g_attn_out, v_g_attn_out),
        (g_conv_out, m_g_conv_out, v_g_conv_out), (g_xattn, m_g_xattn, v_g_xattn), (g_mem, m_g_mem, v_g_mem),
        (g_ffn, m_g_ffn, v_g_ffn), (as_rows(b_ffn_conv), as_rows(m_b_ffn_conv), as_rows(v_b_ffn_conv)),
        (row1(g_final), row1(m_g_final), row1(v_g_final))]
    g_packed, small_res = _adamw_small(small_g, small_parts, wmv, me_arr)
    small_out = dict(zip(small_names, small_res))
    small_out["b_ffn_conv"] = [a.reshape(1, 2 * D_FF) for a in small_out["b_ffn_conv"]]
    small_out["g_final"] = [a.reshape(D_MODEL) for a in small_out["g_final"]]

    g_wsc = lax.dynamic_slice(g_packed[ROW_WSC:ROW_WSC + 3, 0:CONV_W], (0, me * HEAD_DIM), (3, HEAD_DIM))
    g_wfc = lax.dynamic_slice(g_packed[ROW_WFC:ROW_WFC + 3 * N_DEV, 0:UP_CHUNK].reshape(3, N_DEV, UP_CHUNK),
                              (0, me, 0), (3, 1, UP_CHUNK)).reshape(3, UP_CHUNK)
    shard_res = _adamw_shards([(w_short_conv[0], g_wsc, m_w_short_conv[0], v_w_short_conv[0]),
                               (w_ffn_conv[0], g_wfc, m_w_ffn_conv[0], v_w_ffn_conv[0])])
    small_out["w_short_conv"] = [g_wsc[None]] + [a[None] for a in shard_res[0]]
    small_out["w_ffn_conv"] = [g_wfc[None]] + [a[None] for a in shard_res[1]]

    order = ["rel_bias", "g_mix", "w_in", "w_short_conv", "g_attn_out", "g_conv_out", "w_out", "g_xattn", "g_mem",
             "w_xq", "w_xk", "w_xv", "w_xo", "g_ffn", "w_up", "w_ffn_conv", "b_ffn_conv", "w_down", "g_final"]
    allp = {**big_out, **small_out}
    outs = [loss, grad_x[None]]
    for kind in range(4):
        outs += [allp[n][kind] for n in order]
    return tuple(outs)
```

```python
import functools
import math

import numpy as np
import jax
import jax.numpy as jnp
from jax import lax
from jax.experimental import pallas as pl
from jax.experimental.pallas import tpu as pltpu

F32 = jnp.float32
BF16 = jnp.bfloat16
MESH = pl.DeviceIdType.MESH

N_DEV = 8
D_MODEL = 1024
ATTN_W = 512
CONV_W = 512
N_HEADS = 8
HEAD_DIM = 64
WIN = 128
DILATIONS = (1, 4, 16)
N_BUCKETS = 32
BUCKET_MAX_EXACT = 16
BUCKET_MAX_DISTANCE = 2048
N_MEM_HEADS = 4
MEM_HEAD_DIM = 256
D_FF = 2816
IN_COLS = 3072
IN_CHUNK = IN_COLS // N_DEV
UP_CHUNK = 2 * D_FF // N_DEV
EPS = 1e-6

ADAM_LR = 0.001
ADAM_B1 = 0.9
ADAM_B2 = 0.999
ADAM_EPS = 1e-08
ADAM_WD = 0.01
ADAM_STEP = 10

SUBLANES = 8
LANES = 128
HALO = 16
TM = 512
TM_FFN = 256
TS_DW = 4096
VMEM_LIMIT = 56 * 1024 * 1024

ROW_RELB, ROW_GMIX, ROW_GXATTN, ROW_GMEM, ROW_GFFN, ROW_GFINAL, ROW_GAC = 0, 8, 16, 24, 32, 40, 48
ROW_WSC, ROW_BFC, ROW_WFC, SMALL_ROWS = 56, 64, 72, 96


def _cparams(n_grid):
    return pltpu.CompilerParams(dimension_semantics=("arbitrary",) * n_grid, vmem_limit_bytes=VMEM_LIMIT)


def _full(shape):
    nd = len(shape)
    return pl.BlockSpec(tuple(shape), lambda *_: (0,) * nd)


def _resident(shape):
    nd = len(shape)
    return pl.BlockSpec(tuple(shape), lambda *_: (0,) * nd, pipeline_mode=pl.Buffered(1))


ANY_SPEC = pl.BlockSpec(memory_space=pl.ANY)
HBM_SPEC = pl.BlockSpec(memory_space=pltpu.HBM)
SEM_SPEC = pl.BlockSpec(memory_space=pltpu.SEMAPHORE)
VMEM_SPEC = pl.BlockSpec(memory_space=pltpu.VMEM)
SMEM_SPEC = pl.BlockSpec(memory_space=pltpu.SMEM)
DATAFLOW = pltpu.SideEffectType.DATAFLOW_SIDE_EFFECTING


def _rms(x):
    r = lax.rsqrt(jnp.mean(x * x, axis=-1, keepdims=True) + EPS)
    return x * r, r


def _rms_bwd(xh, r, g, dy):
    dxh = dy * g
    return r * (dxh - xh * jnp.mean(dxh * xh, axis=-1, keepdims=True))


def _shift_down(u, halo, k):
    ru = pltpu.roll(u, k, 0)
    rh = pltpu.roll(halo, k, 0)
    row = lax.broadcasted_iota(jnp.int32, rh.shape, 0)
    head = jnp.where(row < k, rh, ru[0:SUBLANES])
    return jnp.concatenate([head, ru[SUBLANES:]], axis=0)


def _shift_up(u, halo, k):
    tm = u.shape[0]
    ru = pltpu.roll(u, tm - k, 0)
    rh = pltpu.roll(halo, SUBLANES - k, 0)
    row = lax.broadcasted_iota(jnp.int32, rh.shape, 0)
    tail = jnp.where(row >= SUBLANES - k, rh, ru[tm - SUBLANES:])
    return jnp.concatenate([ru[:tm - SUBLANES], tail], axis=0)


def _causal_conv3(u, halo, w_ref):
    return (_shift_down(u, halo, 2) * w_ref[0:1, :] + _shift_down(u, halo, 1) * w_ref[1:2, :]) + u * w_ref[2:3, :]


def _dot(a, b):
    return jnp.dot(a, b, preferred_element_type=F32)


def _dot_nt(a, b):
    return lax.dot_general(a, b, (((1,), (1,)), ((), ())), preferred_element_type=F32)


def _dot_tn(a, b):
    return lax.dot_general(a, b, (((0,), (0,)), ((), ())), preferred_element_type=F32)


def _sigmoid(x):
    return 1.0 / (1.0 + jnp.exp(-x))


def _bucket_tables():
    qi = np.arange(WIN)[:, None]
    kj = np.arange(2 * WIN)[None, :]
    steps = np.clip(qi + WIN - kj, 0, WIN)
    out = []
    for d in DILATIONS:
        dist = steps * d
        dd = np.maximum(dist, 1).astype(np.float32)
        large = BUCKET_MAX_EXACT + (
            np.log(dd / np.float32(BUCKET_MAX_EXACT)) / np.float32(math.log(BUCKET_MAX_DISTANCE / BUCKET_MAX_EXACT))
            * np.float32(N_BUCKETS - BUCKET_MAX_EXACT)).astype(np.int32)
        large = np.minimum(large, N_BUCKETS - 1)
        out.append(np.where(dist < BUCKET_MAX_EXACT, dist, large).astype(np.int32))
    return jnp.asarray(np.stack(out))


def _bias_fwd(rel_bias, buckets):
    def body(rb_ref, bk_ref, o_ref):
        for p in range(3):
            bk = bk_ref[p]
            for h in range(N_HEADS):
                acc = jnp.zeros((WIN, 2 * WIN), F32)
                for b in range(N_BUCKETS):
                    acc = jnp.where(bk == b, rb_ref[h, b], acc)
                o_ref[p, h] = acc

    return pl.pallas_call(
        body, name="bias_fwd",
        out_shape=jax.ShapeDtypeStruct((3, N_HEADS, WIN, 2 * WIN), F32),
        in_specs=[pl.BlockSpec(memory_space=pltpu.SMEM), pl.BlockSpec(memory_space=pltpu.VMEM)],
        out_specs=pl.BlockSpec(memory_space=pltpu.VMEM),
    )(rel_bias, buckets)


def _bias_bwd(dbias, buckets):
    def body(db_ref, bk_ref, o_ref):
        lane = lax.broadcasted_iota(jnp.int32, (1, D_MODEL), 1)
        rows = []
        for h in range(N_HEADS):
            row = jnp.zeros((1, D_MODEL), F32)
            for b in range(N_BUCKETS):
                tot = jnp.zeros((1, 1), F32)
                for p in range(3):
                    sel = jnp.where(bk_ref[p] == b, db_ref[p, h], 0.0)
                    tot = tot + jnp.sum(jnp.sum(sel, axis=0, keepdims=True), axis=1, keepdims=True)
                row = jnp.where(lane == b, tot, row)
            rows.append(row)
        o_ref[...] = jnp.concatenate(rows, axis=0)

    return pl.pallas_call(
        body, name="bias_bwd",
        out_shape=jax.ShapeDtypeStruct((N_HEADS, D_MODEL), F32),
        in_specs=[pl.BlockSpec(memory_space=pltpu.VMEM), pl.BlockSpec(memory_space=pltpu.VMEM)],
        out_specs=pl.BlockSpec(memory_space=pltpu.VMEM),
    )(dbias, buckets)


def _spread(val, scr_ref, out_refs, dtype):
    out_refs[0][...] = val.astype(dtype)
    n_blk = val.shape[1] // LANES
    for c in range(n_blk):
        scr_ref[c] = val[:, c * LANES:(c + 1) * LANES]
    for o_ref, d in zip(out_refs[1:], DILATIONS[1:]):
        for r in range(d):
            for c in range(n_blk):
                o_ref[r, :, c * LANES:(c + 1) * LANES] = scr_ref.at[c][pl.ds(r, TM // d, stride=d), :].astype(dtype)


def _gather_classes(blk_ref, scr_ref, d):
    n_blk = blk_ref.shape[2] // LANES
    for r in range(d):
        for c in range(n_blk):
            scr_ref.at[c][pl.ds(r, TM // d, stride=d), :] = blk_ref[r, :, c * LANES:(c + 1) * LANES].astype(F32)
    return jnp.concatenate([scr_ref[c] for c in range(n_blk)], axis=1)


def _class_specs(cols):
    return [pl.BlockSpec((TM, cols), lambda i: (i, 0))] + [
        pl.BlockSpec((d, TM // d, cols), lambda i: (0, i, 0)) for d in DILATIONS[1:]]


def _class_shapes(s, cols, dtype):
    return [jax.ShapeDtypeStruct((s, cols), dtype)] + [
        jax.ShapeDtypeStruct((d, s // d, cols), dtype) for d in DILATIONS[1:]]


def _rms_proj(x, g_mix, w_in_g, dep):
    s = x.shape[0]

    def body(x_ref, g_ref, w_ref, dep_ref, h_ref, q1, q4, q16, k1, k4, k16, v1, v4, v16, gb_ref, gc_ref, xi_ref, scr):
        xh, _ = _rms(x_ref[...])
        h = (xh * g_ref[...]).astype(BF16)
        h_ref[...] = h
        proj = jnp.concatenate([_dot(h, w_ref[j]) for j in range(N_DEV)], axis=1)
        _spread(proj[:, 0:512] * (HEAD_DIM ** -0.5), scr, (q1, q4, q16), BF16)
        _spread(proj[:, 512:1024], scr, (k1, k4, k16), BF16)
        _spread(proj[:, 1024:1536], scr, (v1, v4, v16), BF16)
        gb_ref[...] = proj[:, 1536:2048]
        gc_ref[...] = proj[:, 2048:2560]
        xi_ref[...] = proj[:, 2560:3072]

    row = lambda n: pl.BlockSpec((TM, n), lambda i: (i, 0))
    res = pl.pallas_call(
        body, name="rms_proj", grid=(s // TM,),
        out_shape=[jax.ShapeDtypeStruct((s, D_MODEL), BF16)] + _class_shapes(s, 512, BF16) * 3
        + [jax.ShapeDtypeStruct((s, 512), F32)] * 3,
        in_specs=[row(D_MODEL), _full(g_mix.shape), _full(w_in_g.shape), ANY_SPEC],
        out_specs=[row(D_MODEL)] + _class_specs(512) * 3 + [row(512)] * 3,
        scratch_shapes=[pltpu.VMEM((512 // LANES, TM, LANES), F32)],
        compiler_params=_cparams(1),
    )(x, g_mix, w_in_g, dep)
    return res[0], res[1:4], res[4:7], res[7:10], res[10], res[11], res[12]


def _pair_split(x2):
    lane = lax.broadcasted_iota(jnp.int32, x2.shape, 1)
    zero = jnp.zeros_like(x2)
    return jnp.where(lane < HEAD_DIM, x2, zero), jnp.where(lane >= HEAD_DIM, x2, zero)


def _pair_join(even, odd):
    lane = lax.broadcasted_iota(jnp.int32, (even.shape[0], LANES), 1)
    return jnp.where(lane < HEAD_DIM, even, odd)


def _band_mask(blk):
    qi = lax.broadcasted_iota(jnp.int32, (WIN, 2 * WIN), 0)
    kj = lax.broadcasted_iota(jnp.int32, (WIN, 2 * WIN), 1)
    steps = qi + WIN - kj
    return (steps >= 0) & (steps <= WIN) & (kj >= jnp.where(blk > 0, 0, WIN))


def _swa_fwd(qc, kc, vc, bias, dil):
    nb = qc.shape[1] // WIN

    def body(q_ref, kp_ref, kc_ref, vp_ref, vc_ref, b_ref, o_ref, lse_ref, s_scr, p_scr):
        for a in range(N_HEADS // 2):
            sl = slice(a * LANES, (a + 1) * LANES)
            k2 = jnp.concatenate([kp_ref[0, :, sl], kc_ref[0, :, sl]], axis=0)
            for e, qh in enumerate(_pair_split(q_ref[0, :, sl])):
                s_scr[2 * a + e] = _dot_nt(qh, k2)
        lg = jnp.where(_band_mask(pl.program_id(1)), s_scr[...] + b_ref[...], -jnp.inf)
        m = jnp.max(lg, axis=-1, keepdims=True)
        p = jnp.exp(lg - m)
        den = jnp.sum(p, axis=-1, keepdims=True)
        p_scr[...] = p.astype(BF16)
        lse = m + jnp.log(den)
        for a in range(N_HEADS // 2):
            sl = slice(a * LANES, (a + 1) * LANES)
            v_even, v_odd = _pair_split(jnp.concatenate([vp_ref[0, :, sl], vc_ref[0, :, sl]], axis=0))
            o2 = _dot(p_scr[2 * a], v_even) + _dot(p_scr[2 * a + 1], v_odd)
            o_ref[0, :, sl] = o2 / _pair_join(den[2 * a], den[2 * a + 1])
            lse_ref[0, :, sl] = _pair_join(lse[2 * a], lse[2 * a + 1])

    cur = pl.BlockSpec((1, WIN, 512), lambda r, b: (r, b, 0))
    prev = pl.BlockSpec((1, WIN, 512), lambda r, b: (r, jnp.maximum(b - 1, 0), 0))
    return pl.pallas_call(
        body, name=f"swa_fwd_d{dil}", grid=(dil, nb),
        out_shape=[jax.ShapeDtypeStruct(qc.shape, F32)] * 2,
        in_specs=[cur, prev, cur, prev, cur, _full(bias.shape)],
        out_specs=[cur] * 2,
        scratch_shapes=[pltpu.VMEM((N_HEADS, WIN, 2 * WIN), F32), pltpu.VMEM((N_HEADS, WIN, 2 * WIN), BF16)],
        compiler_params=_cparams(2),
    )(qc, kc, kc, vc, vc, bias)


def _mix_out(branches, gb, gc, xi, x, w_sc, g_a, g_c, w_out):
    s = x.shape[0]
    tb = TM // SUBLANES

    def body(o1, l1, o4, l4, o16, l16, gb_ref, gc_ref, xi_ref, gch_ref, xih_ref, x_ref, wsc_ref,
             ga_ref, gcv_ref, wout_ref, attn_ref, lse1, lse4, lse16, mixed_ref, x1_ref, scr_a, scr_b, scr_c, scr_d):
        i = pl.program_id(0)
        la, lb, lc = l1[...], _gather_classes(l4, scr_a, 4), _gather_classes(l16, scr_b, 16)
        m_all = jnp.maximum(jnp.maximum(la, lb), lc)
        ea, eb, ec = jnp.exp(la - m_all), jnp.exp(lb - m_all), jnp.exp(lc - m_all)
        den = (ea + eb) + ec
        num = (ea * o1[...] + eb * _gather_classes(o4, scr_c, 4)) + ec * _gather_classes(o16, scr_d, 16)
        attn = num / den
        attn_ref[...] = attn
        _spread(m_all + jnp.log(den), scr_a, (lse1, lse4, lse16), F32)
        xa, _ = _rms(attn)
        u = gc_ref[...] * xi_ref[...]
        uh = jnp.where(i > 0, gch_ref[...] * xih_ref[...], 0.0)
        conv = gb_ref[...] * _causal_conv3(u, uh, wsc_ref)
        xc, _ = _rms(conv)
        mixed = jnp.concatenate([xa * ga_ref[...], xc * gcv_ref[...]], axis=1).astype(BF16)
        mixed_ref[...] = mixed
        x1_ref[...] = x_ref[...] + _dot(mixed, wout_ref[...])

    row = lambda n: pl.BlockSpec((TM, n), lambda i: (i, 0))
    halo = pl.BlockSpec((SUBLANES, 512), lambda i: (jnp.maximum(i * tb - 1, 0), 0))
    cs = _class_specs(512)
    flat = [a for br in branches for a in br]
    res = pl.pallas_call(
        body, name="mix_out", grid=(s // TM,),
        out_shape=[jax.ShapeDtypeStruct((s, 512), F32)] + _class_shapes(s, 512, F32)
        + [jax.ShapeDtypeStruct((s, D_MODEL), BF16), jax.ShapeDtypeStruct((s, D_MODEL), F32)],
        in_specs=[cs[0], cs[0], cs[1], cs[1], cs[2], cs[2], row(512), row(512), row(512), halo, halo,
                  row(D_MODEL), _full(w_sc.shape), _full(g_a.shape), _full(g_c.shape), _full(w_out.shape)],
        out_specs=[row(512)] + cs + [row(D_MODEL), row(D_MODEL)],
        scratch_shapes=[pltpu.VMEM((512 // LANES, TM, LANES), F32)] * 4,
        compiler_params=_cparams(1),
    )(*flat, gb, gc, xi, gc, xi, x, w_sc, g_a, g_c, w_out)
    return res[0], res[1:4], res[4], res[5]


def _mem_kv(mem, g_mem, w_xk, w_xv):
    def body(mem_ref, g_ref, wk_ref, wv_ref, mn_ref, k_ref, v_ref):
        xh, _ = _rms(mem_ref[...])
        mn = (xh * g_ref[...]).astype(BF16)
        mn_ref[...] = mn
        k_ref[...] = _dot(mn, wk_ref[...]).astype(BF16)
        v_ref[...] = _dot(mn, wv_ref[...]).astype(BF16)

    vm = pl.BlockSpec(memory_space=pltpu.VMEM)
    return pl.pallas_call(
        body, name="mem_kv",
        out_shape=[jax.ShapeDtypeStruct(mem.shape, BF16)] * 3,
        in_specs=[vm] * 4, out_specs=[vm] * 3,
        compiler_params=pltpu.CompilerParams(vmem_limit_bytes=VMEM_LIMIT),
    )(mem, g_mem, w_xk, w_xv)


def _xattn_fwd(x1, g, w_xq, k, v, w_xo):
    s = x1.shape[0]

    def body(x1_ref, g_ref, wq_ref, k_ref, v_ref, wo_ref, h2_ref, q_ref, o_ref, x2_ref):
        x1v = x1_ref[...]
        xh, _ = _rms(x1v)
        h2 = (xh * g_ref[...]).astype(BF16)
        h2_ref[...] = h2
        qb = _dot(h2, wq_ref[...]).astype(BF16)
        q_ref[...] = qb
        outs = []
        for h in range(N_MEM_HEADS):
            sl = slice(h * MEM_HEAD_DIM, (h + 1) * MEM_HEAD_DIM)
            lg = _dot_nt(qb[:, sl], k_ref[:, sl]) * (MEM_HEAD_DIM ** -0.5)
            p = jnp.exp(lg - jnp.max(lg, axis=-1, keepdims=True))
            p = p / jnp.sum(p, axis=-1, keepdims=True)
            outs.append(_dot(p.astype(BF16), v_ref[:, sl]))
        o = jnp.concatenate(outs, axis=1).astype(BF16)
        o_ref[...] = o
        x2_ref[...] = x1v + _dot(o, wo_ref[...])

    row = pl.BlockSpec((TM, D_MODEL), lambda i: (i, 0))
    return pl.pallas_call(
        body, name="xattn_fwd", grid=(s // TM,),
        out_shape=[jax.ShapeDtypeStruct((s, D_MODEL), BF16)] * 3 + [jax.ShapeDtypeStruct((s, D_MODEL), F32)],
        in_specs=[row, _full(g.shape), _full(w_xq.shape), _full(k.shape), _full(v.shape), _full(w_xo.shape)],
        out_specs=[row] * 4,
        compiler_params=_cparams(1),
    )(x1, g, w_xq, k, v, w_xo)


def _ffn_conv(h_ext, wup_ref, wfc_ref, bfc_ref, j):
    u = _dot(h_ext, wup_ref[j])
    w = wfc_ref[j]
    c = ((pltpu.roll(u, 2, 0) * w[0:1, :] + pltpu.roll(u, 1, 0) * w[1:2, :]) + u * w[2:3, :]) + bfc_ref[j]
    return c[HALO:]


def _ffn_fwd(x2, g, w_up_g, w_fc, b_fc, w_down_g, g_final, target):
    s = x2.shape[0]
    tb = TM_FFN // HALO
    half = N_DEV // 2

    def body(x_ref, xp_ref, g_ref, wup_ref, wfc_ref, bfc_ref, wd_ref, gf_ref, t_ref, h_ref, c_ref, act_ref, dx3_ref,
             loss_ref, dgf_ref):
        i = pl.program_id(0)

        @pl.when(i == 0)
        def _():
            loss_ref[...] = jnp.zeros_like(loss_ref)
            dgf_ref[...] = jnp.zeros_like(dgf_ref)

        x2v = x_ref[...]
        gv = g_ref[...]
        h = (_rms(x2v)[0] * gv).astype(BF16)
        h_ref[...] = h
        hp = jnp.where(i > 0, _rms(xp_ref[...])[0] * gv, 0.0).astype(BF16)
        h_ext = jnp.concatenate([hp, h], axis=0)
        down = jnp.zeros((TM_FFN, D_MODEL), F32)
        for j in range(half):
            cg = _ffn_conv(h_ext, wup_ref, wfc_ref, bfc_ref, j)
            cv = _ffn_conv(h_ext, wup_ref, wfc_ref, bfc_ref, j + half)
            c_ref[j] = cg
            c_ref[j + half] = cv
            a = ((cg * _sigmoid(cg)) * cv).astype(BF16)
            act_ref[j] = a
            down = down + _dot(a, wd_ref[j])
        x3 = x2v + down
        xh, r = _rms(x3)
        gf = gf_ref[...]
        e = xh * gf - t_ref[...]
        loss_ref[...] += 0.5 * jnp.sum(jnp.sum(e * e, axis=1, keepdims=True), axis=0, keepdims=True) / D_MODEL
        dy = e * (1.0 / D_MODEL)
        dgf_ref[0:1, :] += jnp.sum(dy * xh, axis=0, keepdims=True)
        dx3_ref[...] = _rms_bwd(xh, r, gf, dy)

    row = pl.BlockSpec((TM_FFN, D_MODEL), lambda i: (i, 0))
    prev = pl.BlockSpec((HALO, D_MODEL), lambda i: (jnp.maximum(i * tb - 1, 0), 0))
    return pl.pallas_call(
        body, name="ffn_fwd", grid=(s // TM_FFN,),
        out_shape=[jax.ShapeDtypeStruct((s, D_MODEL), BF16), jax.ShapeDtypeStruct((N_DEV, s, UP_CHUNK), F32),
                   jax.ShapeDtypeStruct((half, s, UP_CHUNK), BF16),
                   jax.ShapeDtypeStruct((s, D_MODEL), F32), jax.ShapeDtypeStruct((SUBLANES, 128), F32),
                   jax.ShapeDtypeStruct((SUBLANES, D_MODEL), F32)],
        in_specs=[row, prev, _full(g.shape), _resident(w_up_g.shape), _full(w_fc.shape), _full(b_fc.shape),
                  _resident(w_down_g.shape), _full(g_final.shape), row],
        out_specs=[row, pl.BlockSpec((N_DEV, TM_FFN, UP_CHUNK), lambda i: (0, i, 0)),
                   pl.BlockSpec((half, TM_FFN, UP_CHUNK), lambda i: (0, i, 0)), row,
                   _full((SUBLANES, 128)), _full((SUBLANES, D_MODEL))],
        compiler_params=_cparams(1),
    )(x2, x2, g, w_up_g, w_fc, b_fc, w_down_g, g_final, target)


def _ffn_bwd(dx3, h3, conv, x2, g, w_up_g, w_fc, w_down_g):
    s = x2.shape[0]
    tb = TM_FFN // HALO
    last = s // HALO - 1
    n_tiles = s // TM_FFN
    half = N_DEV // 2
    n_ext = TM_FFN + HALO

    def body(dx_ref, dxn_ref, h_ref, c_ref, cn_ref, x2_ref, g_ref, wup_ref, wfc_ref, wd_ref,
             dup_ref, dx2_ref, dg_ref, dwfc_ref, dbfc_ref):
        i = pl.program_id(0)

        @pl.when(i == 0)
        def _():
            dg_ref[...] = jnp.zeros_like(dg_ref)
            dwfc_ref[...] = jnp.zeros_like(dwfc_ref)
            dbfc_ref[...] = jnp.zeros_like(dbfc_ref)

        dxv = dx_ref[...]
        dxn = jnp.where(i < n_tiles - 1, dxn_ref[...], 0.0)
        dx_ext = jnp.concatenate([dxv, dxn], axis=0).astype(BF16)
        h = h_ref[...]
        dh = jnp.zeros((TM_FFN, D_MODEL), F32)
        for j in range(half):
            cg = jnp.concatenate([c_ref[j], cn_ref[j]], axis=0)
            cv = jnp.concatenate([c_ref[j + half], cn_ref[j + half]], axis=0)
            dact = _dot_nt(dx_ext, wd_ref[j])
            sg = _sigmoid(cg)
            parts = ((j + half, dact * (cg * sg)), (j, (dact * cv) * (sg * (1.0 + cg * (1.0 - sg)))))
            for jj, dc in parts:
                u = _dot(h, wup_ref[jj])
                dc0, dc1, dc2 = dc[:TM_FFN], pltpu.roll(dc, n_ext - 1, 0)[:TM_FFN], pltpu.roll(dc, n_ext - 2, 0)[:TM_FFN]
                dbfc_ref[jj:jj + 1, :] += jnp.sum(dc0, axis=0, keepdims=True)
                dwfc_ref[0, jj:jj + 1, :] += jnp.sum(dc2 * u, axis=0, keepdims=True)
                dwfc_ref[1, jj:jj + 1, :] += jnp.sum(dc1 * u, axis=0, keepdims=True)
                dwfc_ref[2, jj:jj + 1, :] += jnp.sum(dc0 * u, axis=0, keepdims=True)
                w = wfc_ref[jj]
                du = ((dc0 * w[2:3, :] + dc1 * w[1:2, :]) + dc2 * w[0:1, :]).astype(BF16)
                dup_ref[jj] = du
                dh = dh + _dot_nt(du, wup_ref[jj])
        xh, r = _rms(x2_ref[...])
        dg_ref[0:1, :] += jnp.sum(dh * xh, axis=0, keepdims=True)
        dx2_ref[...] = dxv + _rms_bwd(xh, r, g_ref[...], dh)

    row = pl.BlockSpec((TM_FFN, D_MODEL), lambda i: (i, 0))
    nxt = pl.BlockSpec((HALO, D_MODEL), lambda i: (jnp.minimum((i + 1) * tb, last), 0))
    cur_c = pl.BlockSpec((N_DEV, TM_FFN, UP_CHUNK), lambda i: (0, i, 0))
    nxt_c = pl.BlockSpec((N_DEV, HALO, UP_CHUNK), lambda i: (0, jnp.minimum((i + 1) * tb, last), 0))
    return pl.pallas_call(
        body, name="ffn_bwd", grid=(n_tiles,),
        out_shape=[jax.ShapeDtypeStruct((N_DEV, s, UP_CHUNK), BF16), jax.ShapeDtypeStruct((s, D_MODEL), F32),
                   jax.ShapeDtypeStruct((SUBLANES, D_MODEL), F32), jax.ShapeDtypeStruct((3, N_DEV, UP_CHUNK), F32),
                   jax.ShapeDtypeStruct((N_DEV, UP_CHUNK), F32)],
        in_specs=[row, nxt, row, cur_c, nxt_c, row, _full(g.shape), _resident(w_up_g.shape), _full(w_fc.shape),
                  _resident(w_down_g.shape)],
        out_specs=[cur_c, row, _full((SUBLANES, D_MODEL)), _full((3, N_DEV, UP_CHUNK)), _full((N_DEV, UP_CHUNK))],
        compiler_params=_cparams(1),
    )(dx3, dx3, h3, conv, conv, x2, g, w_up_g, w_fc, w_down_g)


def _xattn_bwd(dx2, o, q, k, v, w_xo, w_xq, x1, g, dep):
    s = x1.shape[0]

    def body(dx2_ref, o_ref, q_ref, k_ref, v_ref, wo_ref, wq_ref, x1_ref, g_ref, dep_ref, dq_ref, dx1_ref, dk_ref,
             dv_ref, dg_ref):
        @pl.when(pl.program_id(0) == 0)
        def _():
            dk_ref[...] = jnp.zeros_like(dk_ref)
            dv_ref[...] = jnp.zeros_like(dv_ref)
            dg_ref[...] = jnp.zeros_like(dg_ref)

        dx2v = dx2_ref[...]
        do = _dot_nt(dx2v.astype(BF16), wo_ref[...])
        dqs = []
        for h in range(N_MEM_HEADS):
            sl = slice(h * MEM_HEAD_DIM, (h + 1) * MEM_HEAD_DIM)
            qh, kh, vh = q_ref[:, sl], k_ref[:, sl], v_ref[:, sl]
            lg = _dot_nt(qh, kh) * (MEM_HEAD_DIM ** -0.5)
            p = jnp.exp(lg - jnp.max(lg, axis=-1, keepdims=True))
            p = p / jnp.sum(p, axis=-1, keepdims=True)
            doh = do[:, sl].astype(BF16)
            dp = _dot_nt(doh, vh)
            ds = (p * (dp - jnp.sum(p * dp, axis=-1, keepdims=True)) * (MEM_HEAD_DIM ** -0.5)).astype(BF16)
            dqs.append(_dot(ds, kh))
            dk_ref[:, sl] += _dot_tn(ds, qh)
            dv_ref[:, sl] += _dot_tn(p.astype(BF16), doh)
        dq = jnp.concatenate(dqs, axis=1).astype(BF16)
        dq_ref[...] = dq
        dh2 = _dot_nt(dq, wq_ref[...])
        xh, r = _rms(x1_ref[...])
        dg_ref[0:1, :] += jnp.sum(dh2 * xh, axis=0, keepdims=True)
        dx1_ref[...] = dx2v + _rms_bwd(xh, r, g_ref[...], dh2)

    row = pl.BlockSpec((TM, D_MODEL), lambda i: (i, 0))
    return pl.pallas_call(
        body, name="xattn_bwd", grid=(s // TM,),
        out_shape=[jax.ShapeDtypeStruct((s, D_MODEL), BF16), jax.ShapeDtypeStruct((s, D_MODEL), F32),
                   jax.ShapeDtypeStruct(k.shape, F32), jax.ShapeDtypeStruct(k.shape, F32),
                   jax.ShapeDtypeStruct((SUBLANES, D_MODEL), F32)],
        in_specs=[row, row, row, _full(k.shape), _full(v.shape), _full(w_xo.shape), _full(w_xq.shape), row,
                  _full(g.shape), ANY_SPEC],
        out_specs=[row, row, _full(k.shape), _full(k.shape), _full((SUBLANES, D_MODEL))],
        compiler_params=_cparams(1),
    )(dx2, o, q, k, v, w_xo, w_xq, x1, g, dep)


def _mem_kv_bwd(dk, dv, mem_n, mem, w_xk, w_xv):
    def body(dk_ref, dv_ref, mn_ref, mem_ref, wk_ref, wv_ref, dwk_ref, dwv_ref, dg_ref):
        dkb, dvb = dk_ref[...].astype(BF16), dv_ref[...].astype(BF16)
        mn = mn_ref[...]
        dwk_ref[...] = _dot_tn(mn, dkb).astype(BF16)
        dwv_ref[...] = _dot_tn(mn, dvb).astype(BF16)
        dmn = _dot_nt(dkb, wk_ref[...]) + _dot_nt(dvb, wv_ref[...])
        xh, _ = _rms(mem_ref[...])
        dg_ref[...] = jnp.zeros_like(dg_ref)
        dg_ref[0:1, :] = jnp.sum(dmn * xh, axis=0, keepdims=True)

    vm = pl.BlockSpec(memory_space=pltpu.VMEM)
    return pl.pallas_call(
        body, name="mem_kv_bwd",
        out_shape=[jax.ShapeDtypeStruct(w_xk.shape, BF16), jax.ShapeDtypeStruct(w_xv.shape, BF16),
                   jax.ShapeDtypeStruct((SUBLANES, D_MODEL), F32)],
        in_specs=[vm] * 6, out_specs=[vm] * 3,
        compiler_params=pltpu.CompilerParams(vmem_limit_bytes=VMEM_LIMIT),
    )(dk, dv, mem_n, mem, w_xk, w_xv)


def _mix_out_bwd(dx1, w_out, attn, gb, gc, xi, w_sc, g_a, g_c, dep):
    s = dx1.shape[0]
    tb = TM // SUBLANES

    def body(dx1_ref, wout_ref, attn_ref, gb_ref, gc_ref, xi_ref, gch_ref, xih_ref, wsc_ref, ga_ref, gcv_ref, dep_ref,
             da1, da4, da16, dd1, dd4, dd16, dgb_ref, dcv_ref, dga_ref, dgc_ref, dwsc_ref, scr):
        i = pl.program_id(0)

        @pl.when(i == 0)
        def _():
            dga_ref[...] = jnp.zeros_like(dga_ref)
            dgc_ref[...] = jnp.zeros_like(dgc_ref)
            dwsc_ref[...] = jnp.zeros_like(dwsc_ref)

        dmixed = _dot_nt(dx1_ref[...].astype(BF16), wout_ref[...])
        da, dcn = dmixed[:, :ATTN_W], dmixed[:, ATTN_W:]
        attn = attn_ref[...]
        xa, ra = _rms(attn)
        dga_ref[0:1, :] += jnp.sum(da * xa, axis=0, keepdims=True)
        dattn = _rms_bwd(xa, ra, ga_ref[...], da)
        _spread(dattn, scr, (da1, da4, da16), F32)
        prod = dattn * attn
        dd = jnp.concatenate(
            [jnp.broadcast_to(jnp.sum(prod[:, h * HEAD_DIM:(h + 1) * HEAD_DIM], axis=-1, keepdims=True),
                              (TM, HEAD_DIM)) for h in range(N_HEADS)], axis=1)
        _spread(dd, scr, (dd1, dd4, dd16), F32)
        gbv = gb_ref[...]
        u = gc_ref[...] * xi_ref[...]
        uh = jnp.where(i > 0, gch_ref[...] * xih_ref[...], 0.0)
        u2, u1 = _shift_down(u, uh, 2), _shift_down(u, uh, 1)
        cv = (u2 * wsc_ref[0:1, :] + u1 * wsc_ref[1:2, :]) + u * wsc_ref[2:3, :]
        xc, rc = _rms(gbv * cv)
        dgc_ref[0:1, :] += jnp.sum(dcn * xc, axis=0, keepdims=True)
        dconv = _rms_bwd(xc, rc, gcv_ref[...], dcn)
        dgb_ref[...] = dconv * cv
        dcv = dconv * gbv
        dcv_ref[...] = dcv
        dwsc_ref[0:1, :] += jnp.sum(dcv * u2, axis=0, keepdims=True)
        dwsc_ref[1:2, :] += jnp.sum(dcv * u1, axis=0, keepdims=True)
        dwsc_ref[2:3, :] += jnp.sum(dcv * u, axis=0, keepdims=True)

    row = lambda n: pl.BlockSpec((TM, n), lambda i: (i, 0))
    halo = pl.BlockSpec((SUBLANES, 512), lambda i: (jnp.maximum(i * tb - 1, 0), 0))
    acc = _full((SUBLANES, 512))
    res = pl.pallas_call(
        body, name="mix_out_bwd", grid=(s // TM,),
        out_shape=_class_shapes(s, 512, F32) * 2 + [jax.ShapeDtypeStruct((s, 512), F32)] * 2
        + [jax.ShapeDtypeStruct((SUBLANES, 512), F32)] * 3,
        in_specs=[row(D_MODEL), _full(w_out.shape), row(512), row(512), row(512), row(512), halo, halo,
                  _full(w_sc.shape), _full(g_a.shape), _full(g_c.shape), ANY_SPEC],
        out_specs=_class_specs(512) * 2 + [row(512)] * 2 + [acc] * 3,
        scratch_shapes=[pltpu.VMEM((512 // LANES, TM, LANES), F32)],
        compiler_params=_cparams(1),
    )(dx1, w_out, attn, gb, gc, xi, gc, xi, w_sc, g_a, g_c, dep)
    return res[0:3], res[3:6], res[6], res[7], res[8], res[9], res[10]


def _swa_bwd(qc, kc, vc, doc, lsec, ddc, bias, dil, dep):
    nb = qc.shape[1] // WIN

    def body(q_ref, qn_ref, kp_ref, kc_ref, vp_ref, vc_ref, do_ref, don_ref, lse_ref, lsen_ref, dd_ref, ddn_ref,
             b_ref, dep_ref, dq_ref, dk_ref, dv_ref, db_ref, s_scr, dp_scr, sn_scr, dpn_scr, ds_scr, p_scr, dsn_scr,
             pn_scr):
        r, b = pl.program_id(0), pl.program_id(1)

        @pl.when((r == 0) & (b == 0))
        def _():
            db_ref[...] = jnp.zeros_like(db_ref)

        pairs = [slice(a * LANES, (a + 1) * LANES) for a in range(N_HEADS // 2)]
        for a, sl in enumerate(pairs):
            k2 = jnp.concatenate([kp_ref[0, :, sl], kc_ref[0, :, sl]], axis=0)
            v2 = jnp.concatenate([vp_ref[0, :, sl], vc_ref[0, :, sl]], axis=0)
            q_eo = _pair_split(q_ref[0, :, sl])
            do_eo = _pair_split(do_ref[0, :, sl].astype(BF16))
            qn_eo = _pair_split(qn_ref[0, :, sl])
            don_eo = _pair_split(don_ref[0, :, sl].astype(BF16))
            for e in range(2):
                s_scr[2 * a + e] = _dot_nt(q_eo[e], k2)
                dp_scr[2 * a + e] = _dot_nt(do_eo[e], v2)
                sn_scr[2 * a + e] = _dot_nt(qn_eo[e], kc_ref[0, :, sl])
                dpn_scr[2 * a + e] = _dot_nt(don_eo[e], vc_ref[0, :, sl])
        per_head = lambda ref: jnp.stack([ref[0, :, h * HEAD_DIM:h * HEAD_DIM + 1] for h in range(N_HEADS)])
        bias = b_ref[...]
        p = jnp.exp(jnp.where(_band_mask(b), s_scr[...] + bias, -jnp.inf) - per_head(lse_ref))
        ds = p * (dp_scr[...] - per_head(dd_ref))
        db_ref[...] += ds
        ds_scr[...] = ds.astype(BF16)
        p_scr[...] = p.astype(BF16)
        qi = lax.broadcasted_iota(jnp.int32, (WIN, WIN), 0)
        kj = lax.broadcasted_iota(jnp.int32, (WIN, WIN), 1)
        valid_n = kj >= qi + jnp.where(b + 1 < nb, 0, WIN)
        pn = jnp.exp(jnp.where(valid_n, sn_scr[...] + bias[:, :, :WIN], -jnp.inf) - per_head(lsen_ref))
        dsn_scr[...] = (pn * (dpn_scr[...] - per_head(ddn_ref))).astype(BF16)
        pn_scr[...] = pn.astype(BF16)
        for a, sl in enumerate(pairs):
            k_eo = _pair_split(jnp.concatenate([kp_ref[0, :, sl], kc_ref[0, :, sl]], axis=0))
            q_eo = _pair_split(q_ref[0, :, sl])
            do_eo = _pair_split(do_ref[0, :, sl].astype(BF16))
            qn_eo = _pair_split(qn_ref[0, :, sl])
            don_eo = _pair_split(don_ref[0, :, sl].astype(BF16))
            dq, dk, dv = None, None, None
            for e in range(2):
                h = 2 * a + e
                terms = (_dot(ds_scr[h], k_eo[e]),
                         _dot_tn(ds_scr[h, :, WIN:], q_eo[e]) + _dot_tn(dsn_scr[h], qn_eo[e]),
                         _dot_tn(p_scr[h, :, WIN:], do_eo[e]) + _dot_tn(pn_scr[h], don_eo[e]))
                dq, dk, dv = terms if e == 0 else (dq + terms[0], dk + terms[1], dv + terms[2])
            dq_ref[0, :, sl] = dq
            dk_ref[0, :, sl] = dk
            dv_ref[0, :, sl] = dv

    cur = pl.BlockSpec((1, WIN, 512), lambda r, b: (r, b, 0))
    prev = pl.BlockSpec((1, WIN, 512), lambda r, b: (r, jnp.maximum(b - 1, 0), 0))
    nxt = pl.BlockSpec((1, WIN, 512), lambda r, b: (r, jnp.minimum(b + 1, nb - 1), 0))
    wide, narrow = (N_HEADS, WIN, 2 * WIN), (N_HEADS, WIN, WIN)
    return pl.pallas_call(
        body, name=f"swa_bwd_d{dil}", grid=(dil, nb),
        out_shape=[jax.ShapeDtypeStruct(qc.shape, F32)] * 3 + [jax.ShapeDtypeStruct(bias.shape, F32)],
        in_specs=[cur, nxt, prev, cur, prev, cur, cur, nxt, cur, nxt, cur, nxt, _full(bias.shape), ANY_SPEC],
        out_specs=[cur] * 3 + [_full(bias.shape)],
        scratch_shapes=[pltpu.VMEM(wide, F32), pltpu.VMEM(wide, F32), pltpu.VMEM(narrow, F32),
                        pltpu.VMEM(narrow, F32), pltpu.VMEM(wide, BF16), pltpu.VMEM(wide, BF16),
                        pltpu.VMEM(narrow, BF16), pltpu.VMEM(narrow, BF16)],
        compiler_params=_cparams(2),
    )(qc, qc, kc, kc, vc, vc, doc, doc, lsec, lsec, ddc, ddc, bias, dep)


def _in_proj_bwd(dqs, dks, dvs, dgb, dcv, gc, xi, w_sc, w_in_g, x, g_mix, dx1):
    s = x.shape[0]
    tb = TM // SUBLANES
    last = s // SUBLANES - 1
    n_tiles = s // TM

    def body(dq1, dq4, dq16, dk1, dk4, dk16, dv1, dv4, dv16, dgb_ref, dcv_ref, dcvn_ref, gc_ref, xi_ref, wsc_ref,
             win_ref, x_ref, g_ref, dx1_ref, dproj_ref, gx_ref, dg_ref, scr_a, scr_b):
        i = pl.program_id(0)

        @pl.when(i == 0)
        def _():
            dg_ref[...] = jnp.zeros_like(dg_ref)

        d0 = dcv_ref[...]
        dn = jnp.where(i < n_tiles - 1, dcvn_ref[...], 0.0)
        du = (d0 * wsc_ref[2:3, :] + _shift_up(d0, dn, 1) * wsc_ref[1:2, :]) + _shift_up(d0, dn, 2) * wsc_ref[0:1, :]
        merge = lambda a, b4, b16: (a[...] + _gather_classes(b4, scr_a, 4)) + _gather_classes(b16, scr_b, 16)
        dq = merge(dq1, dq4, dq16) * (HEAD_DIM ** -0.5)
        dk = merge(dk1, dk4, dk16)
        dv = merge(dv1, dv4, dv16)
        dproj = jnp.concatenate([dq, dk, dv, dgb_ref[...], du * xi_ref[...], du * gc_ref[...]], axis=1).astype(BF16)
        dproj_ref[...] = dproj
        dh = jnp.zeros((TM, D_MODEL), F32)
        for j in range(N_DEV):
            dh = dh + _dot_nt(dproj[:, j * IN_CHUNK:(j + 1) * IN_CHUNK], win_ref[j])
        xh, r = _rms(x_ref[...])
        dg_ref[0:1, :] += jnp.sum(dh * xh, axis=0, keepdims=True)
        gx_ref[...] = dx1_ref[...] + _rms_bwd(xh, r, g_ref[...], dh)

    row = lambda n: pl.BlockSpec((TM, n), lambda i: (i, 0))
    nxt = pl.BlockSpec((SUBLANES, 512), lambda i: (jnp.minimum((i + 1) * tb, last), 0))
    return pl.pallas_call(
        body, name="in_proj_bwd", grid=(n_tiles,),
        out_shape=[jax.ShapeDtypeStruct((s, IN_COLS), BF16), jax.ShapeDtypeStruct((s, D_MODEL), F32),
                   jax.ShapeDtypeStruct((SUBLANES, D_MODEL), F32)],
        in_specs=_class_specs(512) * 3 + [row(512), row(512), nxt, row(512), row(512), _full(w_sc.shape),
                                          _full(w_in_g.shape), row(D_MODEL), _full(g_mix.shape), row(D_MODEL)],
        out_specs=[row(IN_COLS), row(D_MODEL), _full((SUBLANES, D_MODEL))],
        scratch_shapes=[pltpu.VMEM((512 // LANES, TM, LANES), F32)] * 2,
        compiler_params=_cparams(1),
    )(*dqs, *dks, *dvs, dgb, dcv, dcv, gc, xi, w_sc, w_in_g, x, g_mix, dx1)


def _dw(a, b, dep, name, a_chunked=False, b_chunked=False, n_chunks=1, chunk_cols=None):
    ts = TS_DW
    if a_chunked:
        nj, s, kk = a.shape
        nn = b.shape[1]
        a_spec = pl.BlockSpec((1, ts, kk), lambda j, t: (j, t, 0))
        b_spec = pl.BlockSpec((ts, nn), lambda j, t: (t, 0))
    elif b_chunked:
        nj, s, nn = b.shape
        kk = a.shape[1]
        a_spec = pl.BlockSpec((ts, kk), lambda j, t: (t, 0))
        b_spec = pl.BlockSpec((1, ts, nn), lambda j, t: (j, t, 0))
    else:
        s, kk = a.shape
        nj, nn = (n_chunks, chunk_cols) if chunk_cols else (1, b.shape[1])
        a_spec = pl.BlockSpec((ts, kk), lambda j, t: (t, 0))
        b_spec = pl.BlockSpec((ts, nn), lambda j, t: (t, j))
    n_steps = s // ts

    def body(a_ref, b_ref, dep_ref, o_ref, acc):
        t = pl.program_id(1)

        @pl.when(t == 0)
        def _():
            acc[...] = jnp.zeros_like(acc)

        av = (a_ref[0] if a_chunked else a_ref[...]).astype(BF16)
        bv = (b_ref[0] if b_chunked else b_ref[...]).astype(BF16)
        acc[...] += _dot_tn(av, bv)

        @pl.when(t == n_steps - 1)
        def _():
            o_ref[0] = acc[...].astype(BF16)

    return pl.pallas_call(
        body, name=name, grid=(nj, n_steps),
        out_shape=jax.ShapeDtypeStruct((nj, kk, nn), BF16),
        in_specs=[a_spec, b_spec, ANY_SPEC],
        out_specs=pl.BlockSpec((1, kk, nn), lambda j, t: (j, 0, 0)),
        scratch_shapes=[pltpu.VMEM((kk, nn), F32)],
        compiler_params=_cparams(2),
    )(a, b, dep)


def _adamw_math(w, g, m, v):
    m2 = ADAM_B1 * m + (1.0 - ADAM_B1) * g
    v2 = ADAM_B2 * v + (1.0 - ADAM_B2) * (g * g)
    m_hat = m2 / (1.0 - ADAM_B1 ** ADAM_STEP)
    v_hat = v2 / (1.0 - ADAM_B2 ** ADAM_STEP)
    delta = -ADAM_LR * (m_hat / (jnp.sqrt(v_hat) + ADAM_EPS) + ADAM_WD * w)
    return delta, m2, v2


def _sum_parts(me, own, p_ref):
    g = None
    for i in range(N_DEV):
        part = jnp.where(me == i, own.astype(F32), p_ref[i].astype(F32))
        g = part if g is None else g + part
    return g


def _adamw_big(name, w, own, parts, m, v, me_arr):
    rr, cc = w.shape
    tr = rr // 4 if rr >= 512 else rr

    def body(me_ref, w_ref, own_ref, p_ref, m_ref, v_ref, g_ref, d_ref, nm_ref, nv_ref):
        g = _sum_parts(me_ref[0], own_ref[...], p_ref)
        g_ref[...] = g
        d_ref[...], nm_ref[...], nv_ref[...] = _adamw_math(w_ref[...], g, m_ref[...], v_ref[...])

    row = pl.BlockSpec((tr, cc), lambda i: (i, 0))
    return pl.pallas_call(
        body, name=name, grid=(rr // tr,),
        out_shape=[jax.ShapeDtypeStruct((rr, cc), F32)] * 4,
        in_specs=[SMEM_SPEC, row, row, pl.BlockSpec((N_DEV, tr, cc), lambda i: (0, i, 0)), row, row],
        out_specs=[row] * 4,
        compiler_params=_cparams(1),
    )(me_arr, w, own, parts, m, v)


def _small_slices():
    return [
        (slice(ROW_RELB, ROW_RELB + 8), slice(0, N_BUCKETS)),
        (slice(ROW_GMIX, ROW_GMIX + 1), slice(0, D_MODEL)),
        (slice(ROW_GAC, ROW_GAC + 1), slice(0, ATTN_W)),
        (slice(ROW_GAC, ROW_GAC + 1), slice(ATTN_W, D_MODEL)),
        (slice(ROW_GXATTN, ROW_GXATTN + 1), slice(0, D_MODEL)),
        (slice(ROW_GMEM, ROW_GMEM + 1), slice(0, D_MODEL)),
        (slice(ROW_GFFN, ROW_GFFN + 1), slice(0, D_MODEL)),
        (slice(ROW_BFC, ROW_BFC + 8), slice(0, UP_CHUNK)),
        (slice(ROW_GFINAL, ROW_GFINAL + 1), slice(0, D_MODEL)),
    ]


def _adamw_small(own, parts, wmv, me_arr):
    slices = _small_slices()
    n = len(slices)

    def body(*refs):
        me_ref, own_ref, p_ref = refs[:3]
        ins = refs[3:3 + 3 * n]
        g_ref = refs[3 + 3 * n]
        outs = refs[4 + 3 * n:]
        g = _sum_parts(me_ref[0], own_ref[...], p_ref)
        g_ref[...] = g
        for a, (rs, ls) in enumerate(slices):
            ga = g[rs, ls]
            outs[4 * a][...] = ga
            outs[4 * a + 1][...], outs[4 * a + 2][...], outs[4 * a + 3][...] = _adamw_math(
                ins[3 * a][...], ga, ins[3 * a + 1][...], ins[3 * a + 2][...])

    vm = pl.BlockSpec(memory_space=pltpu.VMEM)
    flat = [t for trip in wmv for t in trip]
    out_shape = [jax.ShapeDtypeStruct((SMALL_ROWS, D_MODEL), F32)]
    for w, _, _ in wmv:
        out_shape += [jax.ShapeDtypeStruct(w.shape, F32)] * 4
    res = pl.pallas_call(
        body, name="adamw_small", out_shape=out_shape,
        in_specs=[SMEM_SPEC] + [vm] * (2 + 3 * n), out_specs=[vm] * len(out_shape),
    )(me_arr, own, parts, *flat)
    return res[0], [res[1 + 4 * a:5 + 4 * a] for a in range(n)]


def _adamw_shards(items):
    n = len(items)

    def body(*refs):
        for a in range(n):
            w_ref, g_ref, m_ref, v_ref = refs[4 * a:4 * a + 4]
            d_ref, nm_ref, nv_ref = refs[4 * n + 3 * a:4 * n + 3 * a + 3]
            d_ref[...], nm_ref[...], nv_ref[...] = _adamw_math(w_ref[...], g_ref[...], m_ref[...], v_ref[...])

    vm = pl.BlockSpec(memory_space=pltpu.VMEM)
    out_shape = []
    for w, _, _, _ in items:
        out_shape += [jax.ShapeDtypeStruct(w.shape, F32)] * 3
    res = pl.pallas_call(
        body, name="adamw_shards", out_shape=out_shape, in_specs=[vm] * (4 * n), out_specs=[vm] * (3 * n),
    )(*[t for it in items for t in it])
    return [res[3 * a:3 * a + 3] for a in range(n)]


def _mesh_pos():
    return lax.axis_index("x"), lax.axis_index("y"), lax.axis_index("c")


def _dev_index(p):
    return 4 * p[0] + 2 * p[1] + p[2]


def _all_gather(shards):
    n = len(shards)

    def body(*refs):
        ins, outs = refs[:n], refs[n:2 * n]
        send_sems, recv_sems, loc_sems = refs[2 * n:]
        x, y, c = _mesh_pos()
        me, sib = (x, y, c), (x, y, 1 - c)
        chips = [(1 - x, y), (x, 1 - y), (1 - x, 1 - y)]

        def cp(a, k, block, to, src=None):
            dst = outs[a].at[_dev_index(block)]
            return pltpu.make_async_remote_copy(
                src_ref=dst if src is None else src, dst_ref=dst, send_sem=send_sems.at[a, k],
                recv_sem=recv_sems.at[a, k], device_id=to, device_id_type=MESH)

        mine = [pltpu.make_async_copy(ins[a], outs[a].at[_dev_index(me)], loc_sems.at[a]) for a in range(n)]
        for m_ in mine:
            m_.start()
        first = []
        for a in range(n):
            first.append(cp(a, 0, me, sib, src=ins[a]))
            first += [cp(a, 1 + j, me, (*chip, c), src=ins[a]) for j, chip in enumerate(chips)]
        for f in first:
            f.start()
        passed = []
        for a in range(n):
            for j, chip in enumerate(chips):
                cp(a, 1 + j, (*chip, c), me).wait_recv()
                fwd = cp(a, 4 + j, (*chip, c), sib)
                fwd.start()
                passed.append(fwd)
        for a in range(n):
            cp(a, 0, sib, me).wait_recv()
            for j, chip in enumerate(chips):
                cp(a, 4 + j, (*chip, 1 - c), me).wait_recv()
        for f in first + passed:
            f.wait_send()
        for m_ in mine:
            m_.wait()

    hbm = pl.BlockSpec(memory_space=pltpu.HBM)
    return pl.pallas_call(
        body, name="all_gather_weights",
        out_shape=[jax.ShapeDtypeStruct((N_DEV,) + a.shape, a.dtype) for a in shards],
        in_specs=[hbm] * n, out_specs=[hbm] * n,
        scratch_shapes=[pltpu.SemaphoreType.DMA((n, 7)), pltpu.SemaphoreType.DMA((n, 7)),
                        pltpu.SemaphoreType.DMA((n,))],
    )(*shards)


def _peers():
    x, y, c = _mesh_pos()
    return (x, y, c), [((1 - x) if k & 4 else x, (1 - y) if k & 2 else y, (1 - c) if k & 1 else c)
                       for k in range(1, 8)]


def _exchange_copy(src_ref, land_ref, whole, send_sems, recv_sems, a, k, peer, slot):
    src = src_ref if whole else src_ref.at[_dev_index(peer)]
    return pltpu.make_async_remote_copy(
        src_ref=src, dst_ref=land_ref.at[slot], send_sem=send_sems.at[7 * a + k], recv_sem=recv_sems.at[7 * a + k],
        device_id=peer, device_id_type=MESH)


def _exchange_start(name, srcs, whole, dep):
    n = len(srcs)
    lands = [lax.empty(((N_DEV,) + s.shape) if w else s.shape, s.dtype) for s, w in zip(srcs, whole)]

    def body(*refs):
        src_refs, land_refs = refs[:n], refs[n:2 * n]
        send_sems, recv_sems, token = refs[2 * n + 1], refs[2 * n + 2], refs[-1]
        me, peers = _peers()
        for a in range(n):
            for k, peer in enumerate(peers):
                _exchange_copy(src_refs[a], land_refs[a], whole[a], send_sems, recv_sems, a, k, peer,
                               _dev_index(me)).start()
        token[...] = jnp.zeros_like(token)

    res = pl.pallas_call(
        body, name=name,
        out_shape=(pltpu.SemaphoreType.DMA((7 * n,)), pltpu.SemaphoreType.DMA((7 * n,)),
                   *[pltpu.HBM(a.shape, a.dtype) for a in srcs], *[pltpu.HBM(a.shape, a.dtype) for a in lands],
                   jax.ShapeDtypeStruct((SUBLANES, 128), F32)),
        in_specs=[HBM_SPEC] * (2 * n) + [ANY_SPEC],
        out_specs=(SEM_SPEC, SEM_SPEC, *([HBM_SPEC] * (2 * n)), VMEM_SPEC),
        input_output_aliases={i: 2 + i for i in range(2 * n)},
        compiler_params=pltpu.CompilerParams(has_side_effects=DATAFLOW),
    )(*[pltpu.with_memory_space_constraint(a, pltpu.HBM) for a in srcs],
      *[pltpu.with_memory_space_constraint(a, pltpu.HBM) for a in lands], dep)
    return res[0], res[1], list(res[2:2 + n]), list(res[2 + n:2 + 2 * n]), res[-1]


def _exchange_wait(name, started, whole, after, which=None):
    send_sems, recv_sems, srcs, lands, _ = started
    which = list(range(len(srcs))) if which is None else which
    srcs, lands = [srcs[a] for a in which], [lands[a] for a in which]
    n = len(srcs)

    def body(*refs):
        src_refs, land_refs = refs[:n], refs[n:2 * n]
        send_sems, recv_sems = refs[2 * n], refs[2 * n + 1]
        _, peers = _peers()
        for i, a in enumerate(which):
            for k, peer in enumerate(peers):
                cp = _exchange_copy(src_refs[i], land_refs[i], whole[a], send_sems, recv_sems, a, k, peer,
                                    _dev_index(peer))
                cp.wait_send()
                cp.wait_recv()

    res = pl.pallas_call(
        body, name=name,
        out_shape=[pltpu.HBM(a.shape, a.dtype) for a in srcs + lands],
        in_specs=[HBM_SPEC] * (2 * n) + [SEM_SPEC, SEM_SPEC, ANY_SPEC],
        out_specs=[HBM_SPEC] * (2 * n),
        input_output_aliases={i: i for i in range(2 * n)},
        compiler_params=pltpu.CompilerParams(has_side_effects=DATAFLOW),
    )(*srcs, *lands, send_sems, recv_sems, after)
    return list(res[n:])


def _local_step(x, mem, target, rel_bias, g_mix, w_in_g, w_sc, g_a, g_c, g_xattn, g_mem, g_ffn, w_fc, b_fc, g_final,
                dep, late_weights, emit, emit_small):
    s = x.shape[0]
    buckets = _bucket_tables()
    bias = _bias_fwd(rel_bias, buckets)

    h1, qs, ks, vs, gb, gc, xi = _rms_proj(x, g_mix, w_in_g, dep)
    qs, ks, vs = ([a[0][None]] + list(a[1:]) for a in (qs, ks, vs))
    branches = []
    for p, dil in enumerate(DILATIONS):
        o_p, lse_p = _swa_fwd(qs[p], ks[p], vs[p], bias[p], dil)
        branches.append([o_p[0], lse_p[0]] if dil == 1 else [o_p, lse_p])
    w_out = late_weights(["w_out"], branches[-1][0])["w_out"]
    attn, lses, mixed, x1 = _mix_out(branches, gb, gc, xi, x, w_sc, g_a, g_c, w_out)
    lw = late_weights(["w_xq", "w_xk", "w_xv", "w_xo"], x1)
    w_xq, w_xk, w_xv, w_xo = lw["w_xq"], lw["w_xk"], lw["w_xv"], lw["w_xo"]
    mem_n, mk, mv = _mem_kv(mem, g_mem, w_xk, w_xv)
    h2, xq, xo, x2 = _xattn_fwd(x1, g_xattn, w_xq, mk, mv, w_xo)
    lw = late_weights(["w_up", "w_down"], x2)
    w_up_g, w_down_g = lw["w_up"], lw["w_down"]
    h3, conv, act, dx3, loss_acc, dg_final = _ffn_fwd(x2, g_ffn, w_up_g, w_fc, b_fc, w_down_g, g_final, target)

    gw_down = _dw(act, dx3, dep, "dw_down", a_chunked=True)
    dup, dx2, dg_ffn, dw_fc, db_fc = _ffn_bwd(dx3, h3, conv, x2, g_ffn, w_up_g, w_fc, w_down_g)
    gw_up = _dw(h3, dup, dep, "dw_up", b_chunked=True)
    tok = emit(dict(w_down=gw_down, w_up=gw_up))
    dxq, dx1, dmk, dmv, dg_xattn = _xattn_bwd(dx2, xo, xq, mk, mv, w_xo, w_xq, x1, g_xattn, tok)
    gw_xo = _dw(xo, dx2, tok, "dw_xo")[0]
    gw_xq = _dw(h2, dxq, tok, "dw_xq")[0]
    gw_xk, gw_xv, dg_mem = _mem_kv_bwd(dmk, dmv, mem_n, mem, w_xk, w_xv)
    tok = emit(dict(w_xo=gw_xo, w_xq=gw_xq, w_xk=gw_xk, w_xv=gw_xv))
    dattns, dds, dgb, dcv, dg_a, dg_c, dw_sc = _mix_out_bwd(dx1, w_out, attn, gb, gc, xi, w_sc, g_a, g_c, tok)
    first = lambda a: [a[0][None]] + list(a[1:])
    dattns, dds, lses = first(dattns), first(dds), first(lses)
    gw_out = _dw(mixed, dx1, tok, "dw_out")[0]
    tok = emit(dict(w_out=gw_out))
    dqs, dks, dvs, dbias = [], [], [], []
    for p, dil in enumerate(DILATIONS):
        dq_p, dk_p, dv_p, db_p = _swa_bwd(qs[p], ks[p], vs[p], dattns[p], lses[p], dds[p], bias[p], dil, tok)
        dqs.append(dq_p[0] if dil == 1 else dq_p)
        dks.append(dk_p[0] if dil == 1 else dk_p)
        dvs.append(dv_p[0] if dil == 1 else dv_p)
        dbias.append(db_p)
    d_relb = _bias_bwd(jnp.stack(dbias), buckets)
    dproj, grad_x, dg_mix = _in_proj_bwd(dqs, dks, dvs, dgb, dcv, gc, xi, w_sc, w_in_g, x, g_mix, dx1)
    pad = lambda a: jnp.pad(a, ((0, 0), (0, D_MODEL - a.shape[1])))
    small = jnp.concatenate([
        d_relb, dg_mix, dg_xattn, dg_mem, dg_ffn, dg_final, jnp.concatenate([dg_a, dg_c], axis=1),
        pad(dw_sc), pad(db_fc), pad(dw_fc.reshape(3 * N_DEV, UP_CHUNK))], axis=0)
    tok = emit_small(small)
    gw_in = _dw(h1, dproj, tok, "dw_in", n_chunks=N_DEV, chunk_cols=IN_CHUNK)
    emit(dict(w_in=gw_in))
    return loss_acc[0, 0], grad_x


def kernel(x, mem, rel_bias, g_mix, w_in, w_short_conv, g_attn_out, g_conv_out, w_out, g_xattn, g_mem, w_xq, w_xk, w_xv, w_xo, g_ffn, w_up, w_ffn_conv, b_ffn_conv, w_down, g_final, loss_target, m_rel_bias, m_g_mix, m_w_in, m_w_short_conv, m_g_attn_out, m_g_conv_out, m_w_out, m_g_xattn, m_g_mem, m_w_xq, m_w_xk, m_w_xv, m_w_xo, m_g_ffn, m_w_up, m_w_ffn_conv, m_b_ffn_conv, m_w_down, m_g_final, v_rel_bias, v_g_mix, v_w_in, v_w_short_conv, v_g_attn_out, v_g_conv_out, v_w_out, v_g_xattn, v_g_mem, v_w_xq, v_w_xk, v_w_xv, v_w_xo, v_g_ffn, v_w_up, v_w_ffn_conv, v_b_ffn_conv, v_w_down, v_g_final):
    me = _dev_index(_mesh_pos())
    me_arr = me.reshape(1).astype(jnp.int32)

    big_names = ["w_in", "w_out", "w_xq", "w_xk", "w_xv", "w_xo", "w_up", "w_down"]
    late_names = big_names[1:]
    big_w = dict(w_in=w_in[0], w_out=w_out[0], w_xq=w_xq[0], w_xk=w_xk[0], w_xv=w_xv[0], w_xo=w_xo[0],
                 w_up=w_up[0], w_down=w_down[0])
    big_m = dict(w_in=m_w_in[0], w_out=m_w_out[0], w_xq=m_w_xq[0], w_xk=m_w_xk[0], w_xv=m_w_xv[0], w_xo=m_w_xo[0],
                 w_up=m_w_up[0], w_down=m_w_down[0])
    big_v = dict(w_in=v_w_in[0], w_out=v_w_out[0], w_xq=v_w_xq[0], w_xk=v_w_xk[0], w_xv=v_w_xv[0], w_xo=v_w_xo[0],
                 w_up=v_w_up[0], w_down=v_w_down[0])
    shard_shape = {n: big_w[n].shape for n in big_names}

    w_in_g, w_sc_g, w_fc_full = _all_gather([big_w["w_in"].astype(BF16), w_short_conv[0], w_ffn_conv[0]])
    w_sc_full = w_sc_g.transpose(1, 0, 2).reshape(3, CONV_W)
    late_shards = [big_w[n].astype(BF16) for n in late_names]
    ag = _exchange_start("gather_weights_start", late_shards, [True] * len(late_names), w_in_g)

    def late_weights(names, after):
        which = [late_names.index(n) for n in names]
        lands = _exchange_wait("gather_" + "_".join(names) + "_wait", ag, [True] * len(late_names), after, which)
        out = {}
        for n, a, land in zip(names, which, lands):
            full = lax.dynamic_update_index_in_dim(land, late_shards[a], me, 0)
            if n == "w_up":
                out[n] = full
            elif n == "w_down":
                out[n] = full.reshape(N_DEV // 2, UP_CHUNK, D_MODEL)
            else:
                out[n] = full.reshape(D_MODEL, D_MODEL)
        return out

    sent = []

    def emit(grads):
        names = list(grads)
        blocks = [grads[n].reshape((N_DEV,) + shard_shape[n]) for n in names]
        own = [lax.dynamic_index_in_dim(b, me, 0, keepdims=False) for b in blocks]
        started = _exchange_start("scatter_" + "_".join(names) + "_start", blocks, [False] * len(names), me_arr)
        sent.append((names, own, started))
        return started[-1]

    def emit_small(small):
        sent_small.append((small, _exchange_start("gather_small_start", [small], [True], me_arr)))
        return sent_small[0][1][-1]

    sent_small = []
    loss_part, grad_x = _local_step(
        x[0], mem[0], loss_target[0], rel_bias, g_mix, w_in_g, w_sc_full, g_attn_out, g_conv_out, g_xattn, g_mem,
        g_ffn, w_fc_full, b_ffn_conv.reshape(N_DEV, 1, UP_CHUNK), g_final.reshape(1, D_MODEL), ag[-1],
        late_weights, emit, emit_small)
    loss = lax.psum(loss_part, ("x", "y", "c"))

    small_g, small_started = sent_small[0]
    after = sent[-1][2][-1]
    small_parts = _exchange_wait("gather_small_wait", small_started, [True], after)[0]
    big_out = {}
    after = small_parts
    for names, own, started in sent:
        lands = _exchange_wait("scatter_" + "_".join(names) + "_wait", started, [False] * len(names), after)
        for n, own_n, land in zip(names, own, lands):
            res = _adamw_big("adamw_" + n, big_w[n], own_n, land, big_m[n], big_v[n], me_arr)
            big_out[n] = [r[None] for r in res]
            after = res[0]

    as_rows = lambda a: a.reshape(N_DEV, UP_CHUNK)
    row1 = lambda a: a.reshape(1, D_MODEL)
    small_names = ["rel_bias", "g_mix", "g_attn_out", "g_conv_out", "g_xattn", "g_mem", "g_ffn", "b_ffn_conv", "g_final"]
    wmv = [
        (rel_bias, m_rel_bias, v_rel_bias), (g_mix, m_g_mix, v_g_mix), (g_attn_out, m_g_attn_out, v_g_attn_out),
        (g_conv_out, m_g_conv_out, v_g_conv_out), (g_xattn, m_g_xattn, v_g_xattn), (g_mem, m_g_mem, v_g_mem),
        (g_ffn, m_g_ffn, v_g_ffn), (as_rows(b_ffn_conv), as_rows(m_b_ffn_conv), as_rows(v_b_ffn_conv)),
        (row1(g_final), row1(m_g_final), row1(v_g_final))]
    g_packed, small_res = _adamw_small(small_g, small_parts, wmv, me_arr)
    small_out = dict(zip(small_names, small_res))
    small_out["b_ffn_conv"] = [a.reshape(1, 2 * D_FF) for a in small_out["b_ffn_conv"]]
    small_out["g_final"] = [a.reshape(D_MODEL) for a in small_out["g_final"]]

    g_wsc = lax.dynamic_slice(g_packed[ROW_WSC:ROW_WSC + 3, 0:CONV_W], (0, me * HEAD_DIM), (3, HEAD_DIM))
    g_wfc = lax.dynamic_slice(g_packed[ROW_WFC:ROW_WFC + 3 * N_DEV, 0:UP_CHUNK].reshape(3, N_DEV, UP_CHUNK),
                              (0, me, 0), (3, 1, UP_CHUNK)).reshape(3, UP_CHUNK)
    shard_res = _adamw_shards([(w_short_conv[0], g_wsc, m_w_short_conv[0], v_w_short_conv[0]),
                               (w_ffn_conv[0], g_wfc, m_w_ffn_conv[0], v_w_ffn_conv[0])])
    small_out["w_short_conv"] = [g_wsc[None]] + [a[None] for a in shard_res[0]]
    small_out["w_ffn_conv"] = [g_wfc[None]] + [a[None] for a in shard_res[1]]

    order = ["rel_bias", "g_mix", "w_in", "w_short_conv", "g_attn_out", "g_conv_out", "w_out", "g_xattn", "g_mem",
             "w_xq", "w_xk", "w_xv", "w_xo", "g_ffn", "w_up", "w_ffn_conv", "b_ffn_conv", "w_down", "g_final"]
    allp = {**big_out, **small_out}
    outs = [loss, grad_x[None]]
    for kind in range(4):
        outs += [allp[n][kind] for n in order]
    return tuple(outs)
```

```python
import functools
import math

import numpy as np
import jax
import jax.numpy as jnp
from jax import lax
from jax.experimental import pallas as pl
from jax.experimental.pallas import tpu as pltpu

F32 = jnp.float32
BF16 = jnp.bfloat16
MESH = pl.DeviceIdType.MESH

N_DEV = 8
D_MODEL = 1024
ATTN_W = 512
CONV_W = 512
N_HEADS = 8
HEAD_DIM = 64
WIN = 128
DILATIONS = (1, 4, 16)
N_BUCKETS = 32
BUCKET_MAX_EXACT = 16
BUCKET_MAX_DISTANCE = 2048
N_MEM_HEADS = 4
MEM_HEAD_DIM = 256
D_FF = 2816
IN_COLS = 3072
IN_CHUNK = IN_COLS // N_DEV
UP_CHUNK = 2 * D_FF // N_DEV
EPS = 1e-6

ADAM_LR = 0.001
ADAM_B1 = 0.9
ADAM_B2 = 0.999
ADAM_EPS = 1e-08
ADAM_WD = 0.01
ADAM_STEP = 10

SUBLANES = 8
LANES = 128
HALO = 16
TM = 512
TM_FFN = 256
TS_DW = 4096
VMEM_LIMIT = 56 * 1024 * 1024

ROW_RELB, ROW_GMIX, ROW_GXATTN, ROW_GMEM, ROW_GFFN, ROW_GFINAL, ROW_GAC = 0, 8, 16, 24, 32, 40, 48
ROW_WSC, ROW_BFC, ROW_WFC, ROW_LOSS, SMALL_ROWS = 56, 64, 72, 96, 104


def _cparams(n_grid):
    return pltpu.CompilerParams(dimension_semantics=("arbitrary",) * n_grid, vmem_limit_bytes=VMEM_LIMIT)


def _full(shape):
    nd = len(shape)
    return pl.BlockSpec(tuple(shape), lambda *_: (0,) * nd)


def _resident(shape):
    nd = len(shape)
    return pl.BlockSpec(tuple(shape), lambda *_: (0,) * nd, pipeline_mode=pl.Buffered(1))


ANY_SPEC = pl.BlockSpec(memory_space=pl.ANY)
HBM_SPEC = pl.BlockSpec(memory_space=pltpu.HBM)
SEM_SPEC = pl.BlockSpec(memory_space=pltpu.SEMAPHORE)
VMEM_SPEC = pl.BlockSpec(memory_space=pltpu.VMEM)
SMEM_SPEC = pl.BlockSpec(memory_space=pltpu.SMEM)
DATAFLOW = pltpu.SideEffectType.DATAFLOW_SIDE_EFFECTING


def _rms(x):
    r = lax.rsqrt(jnp.mean(x * x, axis=-1, keepdims=True) + EPS)
    return x * r, r


def _rms_bwd(xh, r, g, dy):
    dxh = dy * g
    return r * (dxh - xh * jnp.mean(dxh * xh, axis=-1, keepdims=True))


def _shift_down(u, halo, k):
    ru = pltpu.roll(u, k, 0)
    rh = pltpu.roll(halo, k, 0)
    row = lax.broadcasted_iota(jnp.int32, rh.shape, 0)
    head = jnp.where(row < k, rh, ru[0:SUBLANES])
    return jnp.concatenate([head, ru[SUBLANES:]], axis=0)


def _shift_up(u, halo, k):
    tm = u.shape[0]
    ru = pltpu.roll(u, tm - k, 0)
    rh = pltpu.roll(halo, SUBLANES - k, 0)
    row = lax.broadcasted_iota(jnp.int32, rh.shape, 0)
    tail = jnp.where(row >= SUBLANES - k, rh, ru[tm - SUBLANES:])
    return jnp.concatenate([ru[:tm - SUBLANES], tail], axis=0)


def _causal_conv3(u, halo, w_ref):
    return (_shift_down(u, halo, 2) * w_ref[0:1, :] + _shift_down(u, halo, 1) * w_ref[1:2, :]) + u * w_ref[2:3, :]


def _dot(a, b):
    return jnp.dot(a, b, preferred_element_type=F32)


def _dot_nt(a, b):
    return lax.dot_general(a, b, (((1,), (1,)), ((), ())), preferred_element_type=F32)


def _dot_tn(a, b):
    return lax.dot_general(a, b, (((0,), (0,)), ((), ())), preferred_element_type=F32)


def _sigmoid(x):
    return 1.0 / (1.0 + jnp.exp(-x))


def _bucket_tables():
    qi = np.arange(WIN)[:, None]
    kj = np.arange(2 * WIN)[None, :]
    steps = np.clip(qi + WIN - kj, 0, WIN)
    out = []
    for d in DILATIONS:
        dist = steps * d
        dd = np.maximum(dist, 1).astype(np.float32)
        large = BUCKET_MAX_EXACT + (
            np.log(dd / np.float32(BUCKET_MAX_EXACT)) / np.float32(math.log(BUCKET_MAX_DISTANCE / BUCKET_MAX_EXACT))
            * np.float32(N_BUCKETS - BUCKET_MAX_EXACT)).astype(np.int32)
        large = np.minimum(large, N_BUCKETS - 1)
        out.append(np.where(dist < BUCKET_MAX_EXACT, dist, large).astype(np.int32))
    return jnp.asarray(np.stack(out))


def _bias_fwd(rel_bias, buckets):
    def body(rb_ref, bk_ref, o_ref):
        for p in range(3):
            bk = bk_ref[p]
            for h in range(N_HEADS):
                acc = jnp.zeros((WIN, 2 * WIN), F32)
                for b in range(N_BUCKETS):
                    acc = jnp.where(bk == b, rb_ref[h, b], acc)
                o_ref[p, h] = acc

    return pl.pallas_call(
        body, name="bias_fwd",
        out_shape=jax.ShapeDtypeStruct((3, N_HEADS, WIN, 2 * WIN), F32),
        in_specs=[pl.BlockSpec(memory_space=pltpu.SMEM), pl.BlockSpec(memory_space=pltpu.VMEM)],
        out_specs=pl.BlockSpec(memory_space=pltpu.VMEM),
    )(rel_bias, buckets)


def _bias_bwd(dbias, buckets):
    def body(db_ref, bk_ref, o_ref):
        lane = lax.broadcasted_iota(jnp.int32, (1, D_MODEL), 1)
        rows = []
        for h in range(N_HEADS):
            row = jnp.zeros((1, D_MODEL), F32)
            for b in range(N_BUCKETS):
                tot = jnp.zeros((1, 1), F32)
                for p in range(3):
                    sel = jnp.where(bk_ref[p] == b, db_ref[p, h], 0.0)
                    tot = tot + jnp.sum(jnp.sum(sel, axis=0, keepdims=True), axis=1, keepdims=True)
                row = jnp.where(lane == b, tot, row)
            rows.append(row)
        o_ref[...] = jnp.concatenate(rows, axis=0)

    return pl.pallas_call(
        body, name="bias_bwd",
        out_shape=jax.ShapeDtypeStruct((N_HEADS, D_MODEL), F32),
        in_specs=[pl.BlockSpec(memory_space=pltpu.VMEM), pl.BlockSpec(memory_space=pltpu.VMEM)],
        out_specs=pl.BlockSpec(memory_space=pltpu.VMEM),
    )(dbias, buckets)


def _spread(val, scr_ref, out_refs, dtype):
    out_refs[0][...] = val.astype(dtype)
    n_blk = val.shape[1] // LANES
    for c in range(n_blk):
        scr_ref[c] = val[:, c * LANES:(c + 1) * LANES]
    for o_ref, d in zip(out_refs[1:], DILATIONS[1:]):
        for r in range(d):
            for c in range(n_blk):
                o_ref[r, :, c * LANES:(c + 1) * LANES] = scr_ref.at[c][pl.ds(r, TM // d, stride=d), :].astype(dtype)


def _gather_classes(blk_ref, scr_ref, d):
    n_blk = blk_ref.shape[2] // LANES
    for r in range(d):
        for c in range(n_blk):
            scr_ref.at[c][pl.ds(r, TM // d, stride=d), :] = blk_ref[r, :, c * LANES:(c + 1) * LANES].astype(F32)
    return jnp.concatenate([scr_ref[c] for c in range(n_blk)], axis=1)


def _class_specs(cols):
    return [pl.BlockSpec((TM, cols), lambda i: (i, 0))] + [
        pl.BlockSpec((d, TM // d, cols), lambda i: (0, i, 0)) for d in DILATIONS[1:]]


def _class_shapes(s, cols, dtype):
    return [jax.ShapeDtypeStruct((s, cols), dtype)] + [
        jax.ShapeDtypeStruct((d, s // d, cols), dtype) for d in DILATIONS[1:]]


def _rms_proj(x, g_mix, w_in_g, dep):
    s = x.shape[0]

    def body(x_ref, g_ref, w_ref, dep_ref, h_ref, q1, q4, q16, k1, k4, k16, v1, v4, v16, gb_ref, gc_ref, xi_ref, scr):
        xh, _ = _rms(x_ref[...])
        h = (xh * g_ref[...]).astype(BF16)
        h_ref[...] = h
        proj = jnp.concatenate([_dot(h, w_ref[j]) for j in range(N_DEV)], axis=1)
        _spread(proj[:, 0:512] * (HEAD_DIM ** -0.5), scr, (q1, q4, q16), BF16)
        _spread(proj[:, 512:1024], scr, (k1, k4, k16), BF16)
        _spread(proj[:, 1024:1536], scr, (v1, v4, v16), BF16)
        gb_ref[...] = proj[:, 1536:2048]
        gc_ref[...] = proj[:, 2048:2560]
        xi_ref[...] = proj[:, 2560:3072]

    row = lambda n: pl.BlockSpec((TM, n), lambda i: (i, 0))
    res = pl.pallas_call(
        body, name="rms_proj", grid=(s // TM,),
        out_shape=[jax.ShapeDtypeStruct((s, D_MODEL), BF16)] + _class_shapes(s, 512, BF16) * 3
        + [jax.ShapeDtypeStruct((s, 512), F32)] * 3,
        in_specs=[row(D_MODEL), _full(g_mix.shape), _full(w_in_g.shape), ANY_SPEC],
        out_specs=[row(D_MODEL)] + _class_specs(512) * 3 + [row(512)] * 3,
        scratch_shapes=[pltpu.VMEM((512 // LANES, TM, LANES), F32)],
        compiler_params=_cparams(1),
    )(x, g_mix, w_in_g, dep)
    return res[0], res[1:4], res[4:7], res[7:10], res[10], res[11], res[12]


def _pair_split(x2):
    lane = lax.broadcasted_iota(jnp.int32, x2.shape, 1)
    zero = jnp.zeros_like(x2)
    return jnp.where(lane < HEAD_DIM, x2, zero), jnp.where(lane >= HEAD_DIM, x2, zero)


def _pair_join(even, odd):
    lane = lax.broadcasted_iota(jnp.int32, (even.shape[0], LANES), 1)
    return jnp.where(lane < HEAD_DIM, even, odd)


def _band_mask(first):
    qi = lax.broadcasted_iota(jnp.int32, (WIN, 2 * WIN), 0)
    kj = lax.broadcasted_iota(jnp.int32, (WIN, 2 * WIN), 1)
    steps = qi + WIN - kj
    return (steps >= 0) & (steps <= WIN) & (kj >= jnp.where(first, WIN, 0))


def _swa_fwd(qc, kc, vc, bias, dil):
    nb = qc.shape[1] // (2 * WIN)

    def body(q_ref, kp_ref, kc_ref, vp_ref, vc_ref, b_ref, o_ref, lse_ref, s_scr, p_scr):
        b = pl.program_id(1)
        pairs = [slice(a * LANES, (a + 1) * LANES) for a in range(N_HEADS // 2)]
        for sub in range(2):
            rows = slice(sub * WIN, (sub + 1) * WIN)

            def keys(prev_ref, cur_ref, sl):
                if sub == 0:
                    return jnp.concatenate([prev_ref[0, :, sl], cur_ref[0, 0:WIN, sl]], axis=0)
                return cur_ref[0, :, sl]

            for a, sl in enumerate(pairs):
                k2 = keys(kp_ref, kc_ref, sl)
                for e, qh in enumerate(_pair_split(q_ref[0, rows, sl])):
                    s_scr[sub, 2 * a + e] = _dot_nt(qh, k2)
            first = (b == 0) if sub == 0 else False
            lg = jnp.where(_band_mask(first), s_scr[sub] + b_ref[...], -jnp.inf)
            m = jnp.max(lg, axis=-1, keepdims=True)
            p = jnp.exp(lg - m)
            den = jnp.sum(p, axis=-1, keepdims=True)
            p_scr[sub] = p.astype(BF16)
            lse = m + jnp.log(den)
            for a, sl in enumerate(pairs):
                v_even, v_odd = _pair_split(keys(vp_ref, vc_ref, sl))
                o2 = _dot(p_scr[sub, 2 * a], v_even) + _dot(p_scr[sub, 2 * a + 1], v_odd)
                o_ref[0, rows, sl] = o2 / _pair_join(den[2 * a], den[2 * a + 1])
                lse_ref[0, rows, sl] = _pair_join(lse[2 * a], lse[2 * a + 1])

    cur = pl.BlockSpec((1, 2 * WIN, 512), lambda r, b: (r, b, 0))
    prev = pl.BlockSpec((1, WIN, 512), lambda r, b: (r, jnp.maximum(2 * b - 1, 0), 0))
    return pl.pallas_call(
        body, name=f"swa_fwd_d{dil}", grid=(dil, nb),
        out_shape=[jax.ShapeDtypeStruct(qc.shape, F32)] * 2,
        in_specs=[cur, prev, cur, prev, cur, _full(bias.shape)],
        out_specs=[cur] * 2,
        scratch_shapes=[pltpu.VMEM((2, N_HEADS, WIN, 2 * WIN), F32), pltpu.VMEM((2, N_HEADS, WIN, 2 * WIN), BF16)],
        compiler_params=_cparams(2),
    )(qc, kc, kc, vc, vc, bias)


def _mix_out(branches, gb, gc, xi, x, w_sc, g_a, g_c, w_out):
    s = x.shape[0]
    tb = TM // SUBLANES

    def body(o1, l1, o4, l4, o16, l16, gb_ref, gc_ref, xi_ref, gch_ref, xih_ref, x_ref, wsc_ref,
             ga_ref, gcv_ref, wout_ref, attn_ref, lse1, lse4, lse16, mixed_ref, x1_ref, scr_a, scr_b, scr_c, scr_d):
        i = pl.program_id(0)
        la, lb, lc = l1[...], _gather_classes(l4, scr_a, 4), _gather_classes(l16, scr_b, 16)
        m_all = jnp.maximum(jnp.maximum(la, lb), lc)
        ea, eb, ec = jnp.exp(la - m_all), jnp.exp(lb - m_all), jnp.exp(lc - m_all)
        den = (ea + eb) + ec
        num = (ea * o1[...] + eb * _gather_classes(o4, scr_c, 4)) + ec * _gather_classes(o16, scr_d, 16)
        attn = num / den
        attn_ref[...] = attn
        _spread(m_all + jnp.log(den), scr_a, (lse1, lse4, lse16), F32)
        xa, _ = _rms(attn)
        u = gc_ref[...] * xi_ref[...]
        uh = jnp.where(i > 0, gch_ref[...] * xih_ref[...], 0.0)
        conv = gb_ref[...] * _causal_conv3(u, uh, wsc_ref)
        xc, _ = _rms(conv)
        mixed = jnp.concatenate([xa * ga_ref[...], xc * gcv_ref[...]], axis=1).astype(BF16)
        mixed_ref[...] = mixed
        x1_ref[...] = x_ref[...] + _dot(mixed, wout_ref[...])

    row = lambda n: pl.BlockSpec((TM, n), lambda i: (i, 0))
    halo = pl.BlockSpec((SUBLANES, 512), lambda i: (jnp.maximum(i * tb - 1, 0), 0))
    cs = _class_specs(512)
    flat = [a for br in branches for a in br]
    res = pl.pallas_call(
        body, name="mix_out", grid=(s // TM,),
        out_shape=[jax.ShapeDtypeStruct((s, 512), F32)] + _class_shapes(s, 512, F32)
        + [jax.ShapeDtypeStruct((s, D_MODEL), BF16), jax.ShapeDtypeStruct((s, D_MODEL), F32)],
        in_specs=[cs[0], cs[0], cs[1], cs[1], cs[2], cs[2], row(512), row(512), row(512), halo, halo,
                  row(D_MODEL), _full(w_sc.shape), _full(g_a.shape), _full(g_c.shape), _full(w_out.shape)],
        out_specs=[row(512)] + cs + [row(D_MODEL), row(D_MODEL)],
        scratch_shapes=[pltpu.VMEM((512 // LANES, TM, LANES), F32)] * 4,
        compiler_params=_cparams(1),
    )(*flat, gb, gc, xi, gc, xi, x, w_sc, g_a, g_c, w_out)
    return res[0], res[1:4], res[4], res[5]


def _mem_kv(mem, g_mem, w_xk, w_xv):
    def body(mem_ref, g_ref, wk_ref, wv_ref, mn_ref, k_ref, v_ref):
        xh, _ = _rms(mem_ref[...])
        mn = (xh * g_ref[...]).astype(BF16)
        mn_ref[...] = mn
        k_ref[...] = _dot(mn, wk_ref[...]).astype(BF16)
        v_ref[...] = _dot(mn, wv_ref[...]).astype(BF16)

    vm = pl.BlockSpec(memory_space=pltpu.VMEM)
    return pl.pallas_call(
        body, name="mem_kv",
        out_shape=[jax.ShapeDtypeStruct(mem.shape, BF16)] * 3,
        in_specs=[vm] * 4, out_specs=[vm] * 3,
        compiler_params=pltpu.CompilerParams(vmem_limit_bytes=VMEM_LIMIT),
    )(mem, g_mem, w_xk, w_xv)


def _xattn_fwd(x1, g, w_xq, k, v, w_xo):
    s = x1.shape[0]

    def body(x1_ref, g_ref, wq_ref, k_ref, v_ref, wo_ref, h2_ref, q_ref, o_ref, x2_ref):
        x1v = x1_ref[...]
        xh, _ = _rms(x1v)
        h2 = (xh * g_ref[...]).astype(BF16)
        h2_ref[...] = h2
        qb = _dot(h2, wq_ref[...]).astype(BF16)
        q_ref[...] = qb
        outs = []
        for h in range(N_MEM_HEADS):
            sl = slice(h * MEM_HEAD_DIM, (h + 1) * MEM_HEAD_DIM)
            lg = _dot_nt(qb[:, sl], k_ref[:, sl]) * (MEM_HEAD_DIM ** -0.5)
            p = jnp.exp(lg - jnp.max(lg, axis=-1, keepdims=True))
            p = p / jnp.sum(p, axis=-1, keepdims=True)
            outs.append(_dot(p.astype(BF16), v_ref[:, sl]))
        o = jnp.concatenate(outs, axis=1).astype(BF16)
        o_ref[...] = o
        x2_ref[...] = x1v + _dot(o, wo_ref[...])

    row = pl.BlockSpec((TM, D_MODEL), lambda i: (i, 0))
    return pl.pallas_call(
        body, name="xattn_fwd", grid=(s // TM,),
        out_shape=[jax.ShapeDtypeStruct((s, D_MODEL), BF16)] * 3 + [jax.ShapeDtypeStruct((s, D_MODEL), F32)],
        in_specs=[row, _full(g.shape), _full(w_xq.shape), _full(k.shape), _full(v.shape), _full(w_xo.shape)],
        out_specs=[row] * 4,
        compiler_params=_cparams(1),
    )(x1, g, w_xq, k, v, w_xo)


def _ffn_conv(h_ext, wup_ref, wfc_ref, bfc_ref, j):
    u = _dot(h_ext, wup_ref[j])
    w = wfc_ref[j]
    c = ((pltpu.roll(u, 2, 0) * w[0:1, :] + pltpu.roll(u, 1, 0) * w[1:2, :]) + u * w[2:3, :]) + bfc_ref[j]
    return c[HALO:]


def _ffn_fwd(x2, g, w_up_g, w_fc, b_fc, w_down_g, g_final, target):
    s = x2.shape[0]
    tb = TM_FFN // HALO
    half = N_DEV // 2

    def body(x_ref, xp_ref, g_ref, wup_ref, wfc_ref, bfc_ref, wd_ref, gf_ref, t_ref, h_ref, c_ref, act_ref, dx3_ref,
             loss_ref, dgf_ref):
        i = pl.program_id(0)

        @pl.when(i == 0)
        def _():
            loss_ref[...] = jnp.zeros_like(loss_ref)
            dgf_ref[...] = jnp.zeros_like(dgf_ref)

        x2v = x_ref[...]
        gv = g_ref[...]
        h = (_rms(x2v)[0] * gv).astype(BF16)
        h_ref[...] = h
        hp = jnp.where(i > 0, _rms(xp_ref[...])[0] * gv, 0.0).astype(BF16)
        h_ext = jnp.concatenate([hp, h], axis=0)
        down = jnp.zeros((TM_FFN, D_MODEL), F32)
        for j in range(half):
            cg = _ffn_conv(h_ext, wup_ref, wfc_ref, bfc_ref, j)
            cv = _ffn_conv(h_ext, wup_ref, wfc_ref, bfc_ref, j + half)
            c_ref[j] = cg
            c_ref[j + half] = cv
            a = ((cg * _sigmoid(cg)) * cv).astype(BF16)
            act_ref[j] = a
            down = down + _dot(a, wd_ref[j])
        x3 = x2v + down
        xh, r = _rms(x3)
        gf = gf_ref[...]
        e = xh * gf - t_ref[...]
        loss_ref[...] += 0.5 * jnp.sum(jnp.sum(e * e, axis=1, keepdims=True), axis=0, keepdims=True) / D_MODEL
        dy = e * (1.0 / D_MODEL)
        dgf_ref[0:1, :] += jnp.sum(dy * xh, axis=0, keepdims=True)
        dx3_ref[...] = _rms_bwd(xh, r, gf, dy)

    row = pl.BlockSpec((TM_FFN, D_MODEL), lambda i: (i, 0))
    prev = pl.BlockSpec((HALO, D_MODEL), lambda i: (jnp.maximum(i * tb - 1, 0), 0))
    return pl.pallas_call(
        body, name="ffn_fwd", grid=(s // TM_FFN,),
        out_shape=[jax.ShapeDtypeStruct((s, D_MODEL), BF16), jax.ShapeDtypeStruct((N_DEV, s, UP_CHUNK), F32),
                   jax.ShapeDtypeStruct((half, s, UP_CHUNK), BF16),
                   jax.ShapeDtypeStruct((s, D_MODEL), F32), jax.ShapeDtypeStruct((SUBLANES, 128), F32),
                   jax.ShapeDtypeStruct((SUBLANES, D_MODEL), F32)],
        in_specs=[row, prev, _full(g.shape), _resident(w_up_g.shape), _full(w_fc.shape), _full(b_fc.shape),
                  _resident(w_down_g.shape), _full(g_final.shape), row],
        out_specs=[row, pl.BlockSpec((N_DEV, TM_FFN, UP_CHUNK), lambda i: (0, i, 0)),
                   pl.BlockSpec((half, TM_FFN, UP_CHUNK), lambda i: (0, i, 0)), row,
                   _full((SUBLANES, 128)), _full((SUBLANES, D_MODEL))],
        compiler_params=_cparams(1),
    )(x2, x2, g, w_up_g, w_fc, b_fc, w_down_g, g_final, target)


def _ffn_bwd(dx3, h3, conv, x2, g, w_up_g, w_fc, w_down_g):
    s = x2.shape[0]
    tb = TM_FFN // HALO
    last = s // HALO - 1
    n_tiles = s // TM_FFN
    half = N_DEV // 2
    n_ext = TM_FFN + HALO

    def body(dx_ref, dxn_ref, h_ref, c_ref, cn_ref, x2_ref, g_ref, wup_ref, wfc_ref, wd_ref,
             dup_ref, dx2_ref, dg_ref, dwfc_ref, dbfc_ref):
        i = pl.program_id(0)

        @pl.when(i == 0)
        def _():
            dg_ref[...] = jnp.zeros_like(dg_ref)
            dwfc_ref[...] = jnp.zeros_like(dwfc_ref)
            dbfc_ref[...] = jnp.zeros_like(dbfc_ref)

        dxv = dx_ref[...]
        dxn = jnp.where(i < n_tiles - 1, dxn_ref[...], 0.0)
        dx_ext = jnp.concatenate([dxv, dxn], axis=0).astype(BF16)
        h = h_ref[...]
        dh = jnp.zeros((TM_FFN, D_MODEL), F32)
        for j in range(half):
            cg = jnp.concatenate([c_ref[j], cn_ref[j]], axis=0)
            cv = jnp.concatenate([c_ref[j + half], cn_ref[j + half]], axis=0)
            dact = _dot_nt(dx_ext, wd_ref[j])
            sg = _sigmoid(cg)
            parts = ((j + half, dact * (cg * sg)), (j, (dact * cv) * (sg * (1.0 + cg * (1.0 - sg)))))
            for jj, dc in parts:
                u = _dot(h, wup_ref[jj])
                dc0, dc1, dc2 = dc[:TM_FFN], pltpu.roll(dc, n_ext - 1, 0)[:TM_FFN], pltpu.roll(dc, n_ext - 2, 0)[:TM_FFN]
                dbfc_ref[jj:jj + 1, :] += jnp.sum(dc0, axis=0, keepdims=True)
                dwfc_ref[0, jj:jj + 1, :] += jnp.sum(dc2 * u, axis=0, keepdims=True)
                dwfc_ref[1, jj:jj + 1, :] += jnp.sum(dc1 * u, axis=0, keepdims=True)
                dwfc_ref[2, jj:jj + 1, :] += jnp.sum(dc0 * u, axis=0, keepdims=True)
                w = wfc_ref[jj]
                du = ((dc0 * w[2:3, :] + dc1 * w[1:2, :]) + dc2 * w[0:1, :]).astype(BF16)
                dup_ref[jj] = du
                dh = dh + _dot_nt(du, wup_ref[jj])
        xh, r = _rms(x2_ref[...])
        dg_ref[0:1, :] += jnp.sum(dh * xh, axis=0, keepdims=True)
        dx2_ref[...] = dxv + _rms_bwd(xh, r, g_ref[...], dh)

    row = pl.BlockSpec((TM_FFN, D_MODEL), lambda i: (i, 0))
    nxt = pl.BlockSpec((HALO, D_MODEL), lambda i: (jnp.minimum((i + 1) * tb, last), 0))
    cur_c = pl.BlockSpec((N_DEV, TM_FFN, UP_CHUNK), lambda i: (0, i, 0))
    nxt_c = pl.BlockSpec((N_DEV, HALO, UP_CHUNK), lambda i: (0, jnp.minimum((i + 1) * tb, last), 0))
    return pl.pallas_call(
        body, name="ffn_bwd", grid=(n_tiles,),
        out_shape=[jax.ShapeDtypeStruct((N_DEV, s, UP_CHUNK), BF16), jax.ShapeDtypeStruct((s, D_MODEL), F32),
                   jax.ShapeDtypeStruct((SUBLANES, D_MODEL), F32), jax.ShapeDtypeStruct((3, N_DEV, UP_CHUNK), F32),
                   jax.ShapeDtypeStruct((N_DEV, UP_CHUNK), F32)],
        in_specs=[row, nxt, row, cur_c, nxt_c, row, _full(g.shape), _resident(w_up_g.shape), _full(w_fc.shape),
                  _resident(w_down_g.shape)],
        out_specs=[cur_c, row, _full((SUBLANES, D_MODEL)), _full((3, N_DEV, UP_CHUNK)), _full((N_DEV, UP_CHUNK))],
        compiler_params=_cparams(1),
    )(dx3, dx3, h3, conv, conv, x2, g, w_up_g, w_fc, w_down_g)


def _xattn_bwd(dx2, o, q, k, v, w_xo, w_xq, x1, g, dep):
    s = x1.shape[0]

    def body(dx2_ref, o_ref, q_ref, k_ref, v_ref, wo_ref, wq_ref, x1_ref, g_ref, dep_ref, dq_ref, dx1_ref, dk_ref,
             dv_ref, dg_ref):
        @pl.when(pl.program_id(0) == 0)
        def _():
            dk_ref[...] = jnp.zeros_like(dk_ref)
            dv_ref[...] = jnp.zeros_like(dv_ref)
            dg_ref[...] = jnp.zeros_like(dg_ref)

        dx2v = dx2_ref[...]
        do = _dot_nt(dx2v.astype(BF16), wo_ref[...])
        dqs = []
        for h in range(N_MEM_HEADS):
            sl = slice(h * MEM_HEAD_DIM, (h + 1) * MEM_HEAD_DIM)
            qh, kh, vh = q_ref[:, sl], k_ref[:, sl], v_ref[:, sl]
            lg = _dot_nt(qh, kh) * (MEM_HEAD_DIM ** -0.5)
            p = jnp.exp(lg - jnp.max(lg, axis=-1, keepdims=True))
            p = p / jnp.sum(p, axis=-1, keepdims=True)
            doh = do[:, sl].astype(BF16)
            dp = _dot_nt(doh, vh)
            ds = (p * (dp - jnp.sum(p * dp, axis=-1, keepdims=True)) * (MEM_HEAD_DIM ** -0.5)).astype(BF16)
            dqs.append(_dot(ds, kh))
            dk_ref[:, sl] += _dot_tn(ds, qh)
            dv_ref[:, sl] += _dot_tn(p.astype(BF16), doh)
        dq = jnp.concatenate(dqs, axis=1).astype(BF16)
        dq_ref[...] = dq
        dh2 = _dot_nt(dq, wq_ref[...])
        xh, r = _rms(x1_ref[...])
        dg_ref[0:1, :] += jnp.sum(dh2 * xh, axis=0, keepdims=True)
        dx1_ref[...] = dx2v + _rms_bwd(xh, r, g_ref[...], dh2)

    row = pl.BlockSpec((TM, D_MODEL), lambda i: (i, 0))
    return pl.pallas_call(
        body, name="xattn_bwd", grid=(s // TM,),
        out_shape=[jax.ShapeDtypeStruct((s, D_MODEL), BF16), jax.ShapeDtypeStruct((s, D_MODEL), F32),
                   jax.ShapeDtypeStruct(k.shape, F32), jax.ShapeDtypeStruct(k.shape, F32),
                   jax.ShapeDtypeStruct((SUBLANES, D_MODEL), F32)],
        in_specs=[row, row, row, _full(k.shape), _full(v.shape), _full(w_xo.shape), _full(w_xq.shape), row,
                  _full(g.shape), ANY_SPEC],
        out_specs=[row, row, _full(k.shape), _full(k.shape), _full((SUBLANES, D_MODEL))],
        compiler_params=_cparams(1),
    )(dx2, o, q, k, v, w_xo, w_xq, x1, g, dep)


def _mem_kv_bwd(dk, dv, mem_n, mem, w_xk, w_xv):
    def body(dk_ref, dv_ref, mn_ref, mem_ref, wk_ref, wv_ref, dwk_ref, dwv_ref, dg_ref):
        dkb, dvb = dk_ref[...].astype(BF16), dv_ref[...].astype(BF16)
        mn = mn_ref[...]
        dwk_ref[...] = _dot_tn(mn, dkb).astype(BF16)
        dwv_ref[...] = _dot_tn(mn, dvb).astype(BF16)
        dmn = _dot_nt(dkb, wk_ref[...]) + _dot_nt(dvb, wv_ref[...])
        xh, _ = _rms(mem_ref[...])
        dg_ref[...] = jnp.zeros_like(dg_ref)
        dg_ref[0:1, :] = jnp.sum(dmn * xh, axis=0, keepdims=True)

    vm = pl.BlockSpec(memory_space=pltpu.VMEM)
    return pl.pallas_call(
        body, name="mem_kv_bwd",
        out_shape=[jax.ShapeDtypeStruct(w_xk.shape, BF16), jax.ShapeDtypeStruct(w_xv.shape, BF16),
                   jax.ShapeDtypeStruct((SUBLANES, D_MODEL), F32)],
        in_specs=[vm] * 6, out_specs=[vm] * 3,
        compiler_params=pltpu.CompilerParams(vmem_limit_bytes=VMEM_LIMIT),
    )(dk, dv, mem_n, mem, w_xk, w_xv)


def _mix_out_bwd(dx1, w_out, attn, gb, gc, xi, w_sc, g_a, g_c, dep):
    s = dx1.shape[0]
    tb = TM // SUBLANES

    def body(dx1_ref, wout_ref, attn_ref, gb_ref, gc_ref, xi_ref, gch_ref, xih_ref, wsc_ref, ga_ref, gcv_ref, dep_ref,
             da1, da4, da16, dd1, dd4, dd16, dgb_ref, dcv_ref, dga_ref, dgc_ref, dwsc_ref, scr):
        i = pl.program_id(0)

        @pl.when(i == 0)
        def _():
            dga_ref[...] = jnp.zeros_like(dga_ref)
            dgc_ref[...] = jnp.zeros_like(dgc_ref)
            dwsc_ref[...] = jnp.zeros_like(dwsc_ref)

        dmixed = _dot_nt(dx1_ref[...].astype(BF16), wout_ref[...])
        da, dcn = dmixed[:, :ATTN_W], dmixed[:, ATTN_W:]
        attn = attn_ref[...]
        xa, ra = _rms(attn)
        dga_ref[0:1, :] += jnp.sum(da * xa, axis=0, keepdims=True)
        dattn = _rms_bwd(xa, ra, ga_ref[...], da)
        _spread(dattn, scr, (da1, da4, da16), F32)
        prod = dattn * attn
        dd = jnp.concatenate(
            [jnp.broadcast_to(jnp.sum(prod[:, h * HEAD_DIM:(h + 1) * HEAD_DIM], axis=-1, keepdims=True),
                              (TM, HEAD_DIM)) for h in range(N_HEADS)], axis=1)
        _spread(dd, scr, (dd1, dd4, dd16), F32)
        gbv = gb_ref[...]
        u = gc_ref[...] * xi_ref[...]
        uh = jnp.where(i > 0, gch_ref[...] * xih_ref[...], 0.0)
        u2, u1 = _shift_down(u, uh, 2), _shift_down(u, uh, 1)
        cv = (u2 * wsc_ref[0:1, :] + u1 * wsc_ref[1:2, :]) + u * wsc_ref[2:3, :]
        xc, rc = _rms(gbv * cv)
        dgc_ref[0:1, :] += jnp.sum(dcn * xc, axis=0, keepdims=True)
        dconv = _rms_bwd(xc, rc, gcv_ref[...], dcn)
        dgb_ref[...] = dconv * cv
        dcv = dconv * gbv
        dcv_ref[...] = dcv
        dwsc_ref[0:1, :] += jnp.sum(dcv * u2, axis=0, keepdims=True)
        dwsc_ref[1:2, :] += jnp.sum(dcv * u1, axis=0, keepdims=True)
        dwsc_ref[2:3, :] += jnp.sum(dcv * u, axis=0, keepdims=True)

    row = lambda n: pl.BlockSpec((TM, n), lambda i: (i, 0))
    halo = pl.BlockSpec((SUBLANES, 512), lambda i: (jnp.maximum(i * tb - 1, 0), 0))
    acc = _full((SUBLANES, 512))
    res = pl.pallas_call(
        body, name="mix_out_bwd", grid=(s // TM,),
        out_shape=_class_shapes(s, 512, F32) * 2 + [jax.ShapeDtypeStruct((s, 512), F32)] * 2
        + [jax.ShapeDtypeStruct((SUBLANES, 512), F32)] * 3,
        in_specs=[row(D_MODEL), _full(w_out.shape), row(512), row(512), row(512), row(512), halo, halo,
                  _full(w_sc.shape), _full(g_a.shape), _full(g_c.shape), ANY_SPEC],
        out_specs=_class_specs(512) * 2 + [row(512)] * 2 + [acc] * 3,
        scratch_shapes=[pltpu.VMEM((512 // LANES, TM, LANES), F32)],
        compiler_params=_cparams(1),
    )(dx1, w_out, attn, gb, gc, xi, gc, xi, w_sc, g_a, g_c, dep)
    return res[0:3], res[3:6], res[6], res[7], res[8], res[9], res[10]


def _swa_bwd(qc, kc, vc, doc, lsec, ddc, bias, dil, dep):
    n128 = qc.shape[1] // WIN
    nb = n128 // 2

    def body(q_ref, qn_ref, kp_ref, kc_ref, vp_ref, vc_ref, do_ref, don_ref, lse_ref, lsen_ref, dd_ref, ddn_ref,
             b_ref, dep_ref, dq_ref, dk_ref, dv_ref, db_ref, s_scr, dp_scr, sn_scr, dpn_scr, ds_scr, p_scr, dsn_scr,
             pn_scr):
        r, b = pl.program_id(0), pl.program_id(1)

        @pl.when((r == 0) & (b == 0))
        def _():
            db_ref[...] = jnp.zeros_like(db_ref)

        pairs = [slice(a * LANES, (a + 1) * LANES) for a in range(N_HEADS // 2)]
        blk_a, blk_b = slice(0, WIN), slice(WIN, 2 * WIN)
        per_head = lambda ref, rows: jnp.stack([ref[0, rows, h * HEAD_DIM:h * HEAD_DIM + 1] for h in range(N_HEADS)])
        for a, sl in enumerate(pairs):
            k_pa = jnp.concatenate([kp_ref[0, :, sl], kc_ref[0, blk_a, sl]], axis=0)
            v_pa = jnp.concatenate([vp_ref[0, :, sl], vc_ref[0, blk_a, sl]], axis=0)
            for sub, (rows, k2, v2) in enumerate(((blk_a, k_pa, v_pa), (blk_b, kc_ref[0, :, sl], vc_ref[0, :, sl]))):
                q_eo = _pair_split(q_ref[0, rows, sl])
                do_eo = _pair_split(do_ref[0, rows, sl].astype(BF16))
                for e in range(2):
                    s_scr[sub, 2 * a + e] = _dot_nt(q_eo[e], k2)
                    dp_scr[sub, 2 * a + e] = _dot_nt(do_eo[e], v2)
            qn_eo = _pair_split(qn_ref[0, :, sl])
            don_eo = _pair_split(don_ref[0, :, sl].astype(BF16))
            for e in range(2):
                sn_scr[2 * a + e] = _dot_nt(qn_eo[e], kc_ref[0, blk_b, sl])
                dpn_scr[2 * a + e] = _dot_nt(don_eo[e], vc_ref[0, blk_b, sl])
        bias = b_ref[...]
        for sub, rows in enumerate((blk_a, blk_b)):
            first = (b == 0) if sub == 0 else False
            p = jnp.exp(jnp.where(_band_mask(first), s_scr[sub] + bias, -jnp.inf) - per_head(lse_ref, rows))
            ds = p * (dp_scr[sub] - per_head(dd_ref, rows))
            db_ref[...] += ds
            ds_scr[sub] = ds.astype(BF16)
            p_scr[sub] = p.astype(BF16)
        qi = lax.broadcasted_iota(jnp.int32, (WIN, WIN), 0)
        kj = lax.broadcasted_iota(jnp.int32, (WIN, WIN), 1)
        valid_n = kj >= qi + jnp.where(b + 1 < nb, 0, WIN)
        every = slice(0, WIN)
        pn = jnp.exp(jnp.where(valid_n, sn_scr[...] + bias[:, :, :WIN], -jnp.inf) - per_head(lsen_ref, every))
        dsn_scr[...] = (pn * (dpn_scr[...] - per_head(ddn_ref, every))).astype(BF16)
        pn_scr[...] = pn.astype(BF16)
        for a, sl in enumerate(pairs):
            k_pa = _pair_split(jnp.concatenate([kp_ref[0, :, sl], kc_ref[0, blk_a, sl]], axis=0))
            k_ab = _pair_split(kc_ref[0, :, sl])
            qa_eo, qb_eo = _pair_split(q_ref[0, blk_a, sl]), _pair_split(q_ref[0, blk_b, sl])
            doa_eo = _pair_split(do_ref[0, blk_a, sl].astype(BF16))
            dob_eo = _pair_split(do_ref[0, blk_b, sl].astype(BF16))
            qn_eo = _pair_split(qn_ref[0, :, sl])
            don_eo = _pair_split(don_ref[0, :, sl].astype(BF16))
            acc = None
            for e in range(2):
                h = 2 * a + e
                terms = (_dot(ds_scr[0, h], k_pa[e]),
                         _dot(ds_scr[1, h], k_ab[e]),
                         _dot_tn(ds_scr[0, h, :, WIN:], qa_eo[e]) + _dot_tn(ds_scr[1, h, :, :WIN], qb_eo[e]),
                         _dot_tn(ds_scr[1, h, :, WIN:], qb_eo[e]) + _dot_tn(dsn_scr[h], qn_eo[e]),
                         _dot_tn(p_scr[0, h, :, WIN:], doa_eo[e]) + _dot_tn(p_scr[1, h, :, :WIN], dob_eo[e]),
                         _dot_tn(p_scr[1, h, :, WIN:], dob_eo[e]) + _dot_tn(pn_scr[h], don_eo[e]))
                acc = terms if acc is None else tuple(x + y for x, y in zip(acc, terms))
            dq_ref[0, blk_a, sl], dq_ref[0, blk_b, sl] = acc[0], acc[1]
            dk_ref[0, blk_a, sl], dk_ref[0, blk_b, sl] = acc[2], acc[3]
            dv_ref[0, blk_a, sl], dv_ref[0, blk_b, sl] = acc[4], acc[5]

    cur = pl.BlockSpec((1, 2 * WIN, 512), lambda r, b: (r, b, 0))
    prev = pl.BlockSpec((1, WIN, 512), lambda r, b: (r, jnp.maximum(2 * b - 1, 0), 0))
    nxt = pl.BlockSpec((1, WIN, 512), lambda r, b: (r, jnp.minimum(2 * b + 2, n128 - 1), 0))
    wide, narrow = (2, N_HEADS, WIN, 2 * WIN), (N_HEADS, WIN, WIN)
    return pl.pallas_call(
        body, name=f"swa_bwd_d{dil}", grid=(dil, nb),
        out_shape=[jax.ShapeDtypeStruct(qc.shape, F32)] * 3 + [jax.ShapeDtypeStruct(bias.shape, F32)],
        in_specs=[cur, nxt, prev, cur, prev, cur, cur, nxt, cur, nxt, cur, nxt, _full(bias.shape), ANY_SPEC],
        out_specs=[cur] * 3 + [_full(bias.shape)],
        scratch_shapes=[pltpu.VMEM(wide, F32), pltpu.VMEM(wide, F32), pltpu.VMEM(narrow, F32),
                        pltpu.VMEM(narrow, F32), pltpu.VMEM(wide, BF16), pltpu.VMEM(wide, BF16),
                        pltpu.VMEM(narrow, BF16), pltpu.VMEM(narrow, BF16)],
        compiler_params=_cparams(2),
    )(qc, qc, kc, kc, vc, vc, doc, doc, lsec, lsec, ddc, ddc, bias, dep)


def _in_proj_bwd(dqs, dks, dvs, dgb, dcv, gc, xi, w_sc, w_in_g, x, g_mix, dx1):
    s = x.shape[0]
    tb = TM // SUBLANES
    last = s // SUBLANES - 1
    n_tiles = s // TM

    def body(dq1, dq4, dq16, dk1, dk4, dk16, dv1, dv4, dv16, dgb_ref, dcv_ref, dcvn_ref, gc_ref, xi_ref, wsc_ref,
             win_ref, x_ref, g_ref, dx1_ref, dproj_ref, gx_ref, dg_ref, scr_a, scr_b):
        i = pl.program_id(0)

        @pl.when(i == 0)
        def _():
            dg_ref[...] = jnp.zeros_like(dg_ref)

        d0 = dcv_ref[...]
        dn = jnp.where(i < n_tiles - 1, dcvn_ref[...], 0.0)
        du = (d0 * wsc_ref[2:3, :] + _shift_up(d0, dn, 1) * wsc_ref[1:2, :]) + _shift_up(d0, dn, 2) * wsc_ref[0:1, :]
        merge = lambda a, b4, b16: (a[...] + _gather_classes(b4, scr_a, 4)) + _gather_classes(b16, scr_b, 16)
        dq = merge(dq1, dq4, dq16) * (HEAD_DIM ** -0.5)
        dk = merge(dk1, dk4, dk16)
        dv = merge(dv1, dv4, dv16)
        dproj = jnp.concatenate([dq, dk, dv, dgb_ref[...], du * xi_ref[...], du * gc_ref[...]], axis=1).astype(BF16)
        dproj_ref[...] = dproj
        dh = jnp.zeros((TM, D_MODEL), F32)
        for j in range(N_DEV):
            dh = dh + _dot_nt(dproj[:, j * IN_CHUNK:(j + 1) * IN_CHUNK], win_ref[j])
        xh, r = _rms(x_ref[...])
        dg_ref[0:1, :] += jnp.sum(dh * xh, axis=0, keepdims=True)
        gx_ref[...] = dx1_ref[...] + _rms_bwd(xh, r, g_ref[...], dh)

    row = lambda n: pl.BlockSpec((TM, n), lambda i: (i, 0))
    nxt = pl.BlockSpec((SUBLANES, 512), lambda i: (jnp.minimum((i + 1) * tb, last), 0))
    return pl.pallas_call(
        body, name="in_proj_bwd", grid=(n_tiles,),
        out_shape=[jax.ShapeDtypeStruct((s, IN_COLS), BF16), jax.ShapeDtypeStruct((s, D_MODEL), F32),
                   jax.ShapeDtypeStruct((SUBLANES, D_MODEL), F32)],
        in_specs=_class_specs(512) * 3 + [row(512), row(512), nxt, row(512), row(512), _full(w_sc.shape),
                                          _full(w_in_g.shape), row(D_MODEL), _full(g_mix.shape), row(D_MODEL)],
        out_specs=[row(IN_COLS), row(D_MODEL), _full((SUBLANES, D_MODEL))],
        scratch_shapes=[pltpu.VMEM((512 // LANES, TM, LANES), F32)] * 2,
        compiler_params=_cparams(1),
    )(*dqs, *dks, *dvs, dgb, dcv, dcv, gc, xi, w_sc, w_in_g, x, g_mix, dx1)


def _dw(a, b, dep, name, a_chunked=False, b_chunked=False, n_chunks=1, chunk_cols=None):
    ts = TS_DW if (a_chunked or b_chunked or chunk_cols) else TS_DW // 2
    if a_chunked:
        nj, s, kk = a.shape
        nn = b.shape[1]
        a_spec = pl.BlockSpec((1, ts, kk), lambda j, t: (j, t, 0))
        b_spec = pl.BlockSpec((ts, nn), lambda j, t: (t, 0))
    elif b_chunked:
        nj, s, nn = b.shape
        kk = a.shape[1]
        a_spec = pl.BlockSpec((ts, kk), lambda j, t: (t, 0))
        b_spec = pl.BlockSpec((1, ts, nn), lambda j, t: (j, t, 0))
    else:
        s, kk = a.shape
        nj, nn = (n_chunks, chunk_cols) if chunk_cols else (1, b.shape[1])
        a_spec = pl.BlockSpec((ts, kk), lambda j, t: (t, 0))
        b_spec = pl.BlockSpec((ts, nn), lambda j, t: (t, j))
    n_steps = s // ts

    def body(a_ref, b_ref, dep_ref, o_ref, acc):
        t = pl.program_id(1)

        @pl.when(t == 0)
        def _():
            acc[...] = jnp.zeros_like(acc)

        av = (a_ref[0] if a_chunked else a_ref[...]).astype(BF16)
        bv = (b_ref[0] if b_chunked else b_ref[...]).astype(BF16)
        acc[...] += _dot_tn(av, bv)

        @pl.when(t == n_steps - 1)
        def _():
            o_ref[0] = acc[...].astype(BF16)

    return pl.pallas_call(
        body, name=name, grid=(nj, n_steps),
        out_shape=jax.ShapeDtypeStruct((nj, kk, nn), BF16),
        in_specs=[a_spec, b_spec, ANY_SPEC],
        out_specs=pl.BlockSpec((1, kk, nn), lambda j, t: (j, 0, 0)),
        scratch_shapes=[pltpu.VMEM((kk, nn), F32)],
        compiler_params=_cparams(2),
    )(a, b, dep)


def _adamw_math(w, g, m, v):
    m2 = ADAM_B1 * m + (1.0 - ADAM_B1) * g
    v2 = ADAM_B2 * v + (1.0 - ADAM_B2) * (g * g)
    m_hat = m2 / (1.0 - ADAM_B1 ** ADAM_STEP)
    v_hat = v2 / (1.0 - ADAM_B2 ** ADAM_STEP)
    delta = -ADAM_LR * (m_hat / (jnp.sqrt(v_hat) + ADAM_EPS) + ADAM_WD * w)
    return delta, m2, v2


def _sum_parts(me, own, p_ref):
    g = None
    for i in range(N_DEV):
        part = jnp.where(me == i, own.astype(F32), p_ref[i].astype(F32))
        g = part if g is None else g + part
    return g


def _adamw_big(name, w, own, parts, m, v, me_arr):
    rr, cc = w.shape
    tr = rr // 4 if rr >= 512 else rr

    def body(me_ref, w_ref, own_ref, p_ref, m_ref, v_ref, g_ref, d_ref, nm_ref, nv_ref):
        g = _sum_parts(me_ref[0], own_ref[...], p_ref)
        g_ref[...] = g
        d_ref[...], nm_ref[...], nv_ref[...] = _adamw_math(w_ref[...], g, m_ref[...], v_ref[...])

    row = pl.BlockSpec((tr, cc), lambda i: (i, 0))
    return pl.pallas_call(
        body, name=name, grid=(rr // tr,),
        out_shape=[jax.ShapeDtypeStruct((rr, cc), F32)] * 4,
        in_specs=[SMEM_SPEC, row, row, pl.BlockSpec((N_DEV, tr, cc), lambda i: (0, i, 0)), row, row],
        out_specs=[row] * 4,
        compiler_params=_cparams(1),
    )(me_arr, w, own, parts, m, v)


def _small_slices():
    return [
        (slice(ROW_RELB, ROW_RELB + 8), slice(0, N_BUCKETS)),
        (slice(ROW_GMIX, ROW_GMIX + 1), slice(0, D_MODEL)),
        (slice(ROW_GAC, ROW_GAC + 1), slice(0, ATTN_W)),
        (slice(ROW_GAC, ROW_GAC + 1), slice(ATTN_W, D_MODEL)),
        (slice(ROW_GXATTN, ROW_GXATTN + 1), slice(0, D_MODEL)),
        (slice(ROW_GMEM, ROW_GMEM + 1), slice(0, D_MODEL)),
        (slice(ROW_GFFN, ROW_GFFN + 1), slice(0, D_MODEL)),
        (slice(ROW_BFC, ROW_BFC + 8), slice(0, UP_CHUNK)),
        (slice(ROW_GFINAL, ROW_GFINAL + 1), slice(0, D_MODEL)),
    ]


def _adamw_small(own, parts, wmv, me_arr):
    slices = _small_slices()
    n = len(slices)

    def body(*refs):
        me_ref, own_ref, p_ref = refs[:3]
        ins = refs[3:3 + 3 * n]
        g_ref = refs[3 + 3 * n]
        outs = refs[4 + 3 * n:]
        g = _sum_parts(me_ref[0], own_ref[...], p_ref)
        g_ref[...] = g
        for a, (rs, ls) in enumerate(slices):
            ga = g[rs, ls]
            outs[4 * a][...] = ga
            outs[4 * a + 1][...], outs[4 * a + 2][...], outs[4 * a + 3][...] = _adamw_math(
                ins[3 * a][...], ga, ins[3 * a + 1][...], ins[3 * a + 2][...])

    vm = pl.BlockSpec(memory_space=pltpu.VMEM)
    flat = [t for trip in wmv for t in trip]
    out_shape = [jax.ShapeDtypeStruct((SMALL_ROWS, D_MODEL), F32)]
    for w, _, _ in wmv:
        out_shape += [jax.ShapeDtypeStruct(w.shape, F32)] * 4
    res = pl.pallas_call(
        body, name="adamw_small", out_shape=out_shape,
        in_specs=[SMEM_SPEC] + [vm] * (2 + 3 * n), out_specs=[vm] * len(out_shape),
    )(me_arr, own, parts, *flat)
    return res[0], [res[1 + 4 * a:5 + 4 * a] for a in range(n)]


def _adamw_shards(items):
    n = len(items)

    def body(*refs):
        for a in range(n):
            w_ref, g_ref, m_ref, v_ref = refs[4 * a:4 * a + 4]
            d_ref, nm_ref, nv_ref = refs[4 * n + 3 * a:4 * n + 3 * a + 3]
            d_ref[...], nm_ref[...], nv_ref[...] = _adamw_math(w_ref[...], g_ref[...], m_ref[...], v_ref[...])

    vm = pl.BlockSpec(memory_space=pltpu.VMEM)
    out_shape = []
    for w, _, _, _ in items:
        out_shape += [jax.ShapeDtypeStruct(w.shape, F32)] * 3
    res = pl.pallas_call(
        body, name="adamw_shards", out_shape=out_shape, in_specs=[vm] * (4 * n), out_specs=[vm] * (3 * n),
    )(*[t for it in items for t in it])
    return [res[3 * a:3 * a + 3] for a in range(n)]


def _mesh_pos():
    return lax.axis_index("x"), lax.axis_index("y"), lax.axis_index("c")


def _dev_index(p):
    return 4 * p[0] + 2 * p[1] + p[2]


def _all_gather(shards):
    n = len(shards)

    def body(*refs):
        ins, outs = refs[:n], refs[n:2 * n]
        send_sems, recv_sems, loc_sems = refs[2 * n:]
        x, y, c = _mesh_pos()
        me, sib = (x, y, c), (x, y, 1 - c)
        chips = [(1 - x, y), (x, 1 - y), (1 - x, 1 - y)]

        def cp(a, k, block, to, src=None):
            dst = outs[a].at[_dev_index(block)]
            return pltpu.make_async_remote_copy(
                src_ref=dst if src is None else src, dst_ref=dst, send_sem=send_sems.at[a, k],
                recv_sem=recv_sems.at[a, k], device_id=to, device_id_type=MESH)

        mine = [pltpu.make_async_copy(ins[a], outs[a].at[_dev_index(me)], loc_sems.at[a]) for a in range(n)]
        for m_ in mine:
            m_.start()
        first = []
        for a in range(n):
            first.append(cp(a, 0, me, sib, src=ins[a]))
            first += [cp(a, 1 + j, me, (*chip, c), src=ins[a]) for j, chip in enumerate(chips)]
        for f in first:
            f.start()
        passed = []
        for a in range(n):
            for j, chip in enumerate(chips):
                cp(a, 1 + j, (*chip, c), me).wait_recv()
                fwd = cp(a, 4 + j, (*chip, c), sib)
                fwd.start()
                passed.append(fwd)
        for a in range(n):
            cp(a, 0, sib, me).wait_recv()
            for j, chip in enumerate(chips):
                cp(a, 4 + j, (*chip, 1 - c), me).wait_recv()
        for f in first + passed:
            f.wait_send()
        for m_ in mine:
            m_.wait()

    hbm = pl.BlockSpec(memory_space=pltpu.HBM)
    return pl.pallas_call(
        body, name="all_gather_weights",
        out_shape=[jax.ShapeDtypeStruct((N_DEV,) + a.shape, a.dtype) for a in shards],
        in_specs=[hbm] * n, out_specs=[hbm] * n,
        scratch_shapes=[pltpu.SemaphoreType.DMA((n, 7)), pltpu.SemaphoreType.DMA((n, 7)),
                        pltpu.SemaphoreType.DMA((n,))],
    )(*shards)


def _peers():
    x, y, c = _mesh_pos()
    return (x, y, c), [((1 - x) if k & 4 else x, (1 - y) if k & 2 else y, (1 - c) if k & 1 else c)
                       for k in range(1, 8)]


def _exchange_copy(src_ref, land_ref, whole, send_sems, recv_sems, a, k, peer, slot):
    src = src_ref if whole else src_ref.at[_dev_index(peer)]
    return pltpu.make_async_remote_copy(
        src_ref=src, dst_ref=land_ref.at[slot], send_sem=send_sems.at[7 * a + k], recv_sem=recv_sems.at[7 * a + k],
        device_id=peer, device_id_type=MESH)


def _exchange_start(name, srcs, whole, dep):
    n = len(srcs)
    lands = [lax.empty(((N_DEV,) + s.shape) if w else s.shape, s.dtype) for s, w in zip(srcs, whole)]

    def body(*refs):
        src_refs, land_refs = refs[:n], refs[n:2 * n]
        send_sems, recv_sems, token = refs[2 * n + 1], refs[2 * n + 2], refs[-1]
        me, peers = _peers()
        for a in range(n):
            for k, peer in enumerate(peers):
                _exchange_copy(src_refs[a], land_refs[a], whole[a], send_sems, recv_sems, a, k, peer,
                               _dev_index(me)).start()
        token[...] = jnp.zeros_like(token)

    res = pl.pallas_call(
        body, name=name,
        out_shape=(pltpu.SemaphoreType.DMA((7 * n,)), pltpu.SemaphoreType.DMA((7 * n,)),
                   *[pltpu.HBM(a.shape, a.dtype) for a in srcs], *[pltpu.HBM(a.shape, a.dtype) for a in lands],
                   jax.ShapeDtypeStruct((SUBLANES, 128), F32)),
        in_specs=[HBM_SPEC] * (2 * n) + [ANY_SPEC],
        out_specs=(SEM_SPEC, SEM_SPEC, *([HBM_SPEC] * (2 * n)), VMEM_SPEC),
        input_output_aliases={i: 2 + i for i in range(2 * n)},
        compiler_params=pltpu.CompilerParams(has_side_effects=DATAFLOW),
    )(*[pltpu.with_memory_space_constraint(a, pltpu.HBM) for a in srcs],
      *[pltpu.with_memory_space_constraint(a, pltpu.HBM) for a in lands], dep)
    return res[0], res[1], list(res[2:2 + n]), list(res[2 + n:2 + 2 * n]), res[-1]


def _exchange_wait(name, started, whole, after, which=None):
    send_sems, recv_sems, srcs, lands, _ = started
    which = list(range(len(srcs))) if which is None else which
    srcs, lands = [srcs[a] for a in which], [lands[a] for a in which]
    n = len(srcs)

    def body(*refs):
        src_refs, land_refs = refs[:n], refs[n:2 * n]
        send_sems, recv_sems = refs[2 * n], refs[2 * n + 1]
        _, peers = _peers()
        for i, a in enumerate(which):
            for k, peer in enumerate(peers):
                cp = _exchange_copy(src_refs[i], land_refs[i], whole[a], send_sems, recv_sems, a, k, peer,
                                    _dev_index(peer))
                cp.wait_send()
                cp.wait_recv()

    res = pl.pallas_call(
        body, name=name,
        out_shape=[pltpu.HBM(a.shape, a.dtype) for a in srcs + lands],
        in_specs=[HBM_SPEC] * (2 * n) + [SEM_SPEC, SEM_SPEC, ANY_SPEC],
        out_specs=[HBM_SPEC] * (2 * n),
        input_output_aliases={i: i for i in range(2 * n)},
        compiler_params=pltpu.CompilerParams(has_side_effects=DATAFLOW),
    )(*srcs, *lands, send_sems, recv_sems, after)
    return list(res[n:])


def _local_step(x, mem, target, rel_bias, g_mix, w_in_g, w_sc, g_a, g_c, g_xattn, g_mem, g_ffn, w_fc, b_fc, g_final,
                dep, late_weights, emit, emit_small):
    s = x.shape[0]
    buckets = _bucket_tables()
    bias = _bias_fwd(rel_bias, buckets)

    h1, qs, ks, vs, gb, gc, xi = _rms_proj(x, g_mix, w_in_g, dep)
    qs, ks, vs = ([a[0][None]] + list(a[1:]) for a in (qs, ks, vs))
    branches = []
    for p, dil in enumerate(DILATIONS):
        o_p, lse_p = _swa_fwd(qs[p], ks[p], vs[p], bias[p], dil)
        branches.append([o_p[0], lse_p[0]] if dil == 1 else [o_p, lse_p])
    w_out = late_weights(["w_out"], branches[-1][0])["w_out"]
    attn, lses, mixed, x1 = _mix_out(branches, gb, gc, xi, x, w_sc, g_a, g_c, w_out)
    lw = late_weights(["w_xq", "w_xk", "w_xv", "w_xo"], x1)
    w_xq, w_xk, w_xv, w_xo = lw["w_xq"], lw["w_xk"], lw["w_xv"], lw["w_xo"]
    mem_n, mk, mv = _mem_kv(mem, g_mem, w_xk, w_xv)
    h2, xq, xo, x2 = _xattn_fwd(x1, g_xattn, w_xq, mk, mv, w_xo)
    lw = late_weights(["w_up", "w_down"], x2)
    w_up_g, w_down_g = lw["w_up"], lw["w_down"]
    h3, conv, act, dx3, loss_acc, dg_final = _ffn_fwd(x2, g_ffn, w_up_g, w_fc, b_fc, w_down_g, g_final, target)

    gw_down = _dw(act, dx3, dep, "dw_down", a_chunked=True)
    dup, dx2, dg_ffn, dw_fc, db_fc = _ffn_bwd(dx3, h3, conv, x2, g_ffn, w_up_g, w_fc, w_down_g)
    gw_up = _dw(h3, dup, dep, "dw_up", b_chunked=True)
    tok = emit(dict(w_down=gw_down, w_up=gw_up))
    dxq, dx1, dmk, dmv, dg_xattn = _xattn_bwd(dx2, xo, xq, mk, mv, w_xo, w_xq, x1, g_xattn, tok)
    gw_xo = _dw(xo, dx2, tok, "dw_xo")[0]
    gw_xq = _dw(h2, dxq, tok, "dw_xq")[0]
    gw_xk, gw_xv, dg_mem = _mem_kv_bwd(dmk, dmv, mem_n, mem, w_xk, w_xv)
    tok = emit(dict(w_xo=gw_xo, w_xq=gw_xq, w_xk=gw_xk, w_xv=gw_xv))
    dattns, dds, dgb, dcv, dg_a, dg_c, dw_sc = _mix_out_bwd(dx1, w_out, attn, gb, gc, xi, w_sc, g_a, g_c, tok)
    first = lambda a: [a[0][None]] + list(a[1:])
    dattns, dds, lses = first(dattns), first(dds), first(lses)
    gw_out = _dw(mixed, dx1, tok, "dw_out")[0]
    tok = emit(dict(w_out=gw_out))
    dqs, dks, dvs, dbias = [], [], [], []
    for p, dil in enumerate(DILATIONS):
        dq_p, dk_p, dv_p, db_p = _swa_bwd(qs[p], ks[p], vs[p], dattns[p], lses[p], dds[p], bias[p], dil, tok)
        dqs.append(dq_p[0] if dil == 1 else dq_p)
        dks.append(dk_p[0] if dil == 1 else dk_p)
        dvs.append(dv_p[0] if dil == 1 else dv_p)
        dbias.append(db_p)
    d_relb = _bias_bwd(jnp.stack(dbias), buckets)
    dproj, grad_x, dg_mix = _in_proj_bwd(dqs, dks, dvs, dgb, dcv, gc, xi, w_sc, w_in_g, x, g_mix, dx1)
    pad = lambda a: jnp.pad(a, ((0, 0), (0, D_MODEL - a.shape[1])))
    small = jnp.concatenate([
        d_relb, dg_mix, dg_xattn, dg_mem, dg_ffn, dg_final, jnp.concatenate([dg_a, dg_c], axis=1),
        pad(dw_sc), pad(db_fc), pad(dw_fc.reshape(3 * N_DEV, UP_CHUNK)), pad(loss_acc)], axis=0)
    tok = emit_small(small)
    gw_in = _dw(h1, dproj, tok, "dw_in", n_chunks=N_DEV, chunk_cols=IN_CHUNK)
    emit(dict(w_in=gw_in))
    return grad_x


def kernel(x, mem, rel_bias, g_mix, w_in, w_short_conv, g_attn_out, g_conv_out, w_out, g_xattn, g_mem, w_xq, w_xk, w_xv, w_xo, g_ffn, w_up, w_ffn_conv, b_ffn_conv, w_down, g_final, loss_target, m_rel_bias, m_g_mix, m_w_in, m_w_short_conv, m_g_attn_out, m_g_conv_out, m_w_out, m_g_xattn, m_g_mem, m_w_xq, m_w_xk, m_w_xv, m_w_xo, m_g_ffn, m_w_up, m_w_ffn_conv, m_b_ffn_conv, m_w_down, m_g_final, v_rel_bias, v_g_mix, v_w_in, v_w_short_conv, v_g_attn_out, v_g_conv_out, v_w_out, v_g_xattn, v_g_mem, v_w_xq, v_w_xk, v_w_xv, v_w_xo, v_g_ffn, v_w_up, v_w_ffn_conv, v_b_ffn_conv, v_w_down, v_g_final):
    me = _dev_index(_mesh_pos())
    me_arr = me.reshape(1).astype(jnp.int32)

    big_names = ["w_in", "w_out", "w_xq", "w_xk", "w_xv", "w_xo", "w_up", "w_down"]
    late_names = big_names[1:]
    big_w = dict(w_in=w_in[0], w_out=w_out[0], w_xq=w_xq[0], w_xk=w_xk[0], w_xv=w_xv[0], w_xo=w_xo[0],
                 w_up=w_up[0], w_down=w_down[0])
    big_m = dict(w_in=m_w_in[0], w_out=m_w_out[0], w_xq=m_w_xq[0], w_xk=m_w_xk[0], w_xv=m_w_xv[0], w_xo=m_w_xo[0],
                 w_up=m_w_up[0], w_down=m_w_down[0])
    big_v = dict(w_in=v_w_in[0], w_out=v_w_out[0], w_xq=v_w_xq[0], w_xk=v_w_xk[0], w_xv=v_w_xv[0], w_xo=v_w_xo[0],
                 w_up=v_w_up[0], w_down=v_w_down[0])
    shard_shape = {n: big_w[n].shape for n in big_names}

    w_in_g, w_sc_g, w_fc_full = _all_gather([big_w["w_in"].astype(BF16), w_short_conv[0], w_ffn_conv[0]])
    w_sc_full = w_sc_g.transpose(1, 0, 2).reshape(3, CONV_W)
    late_shards = [big_w[n].astype(BF16) for n in late_names]
    ag = _exchange_start("gather_weights_start", late_shards, [True] * len(late_names), w_in_g)

    def late_weights(names, after):
        which = [late_names.index(n) for n in names]
        lands = _exchange_wait("gather_" + "_".join(names) + "_wait", ag, [True] * len(late_names), after, which)
        out = {}
        for n, a, land in zip(names, which, lands):
            full = lax.dynamic_update_index_in_dim(land, late_shards[a], me, 0)
            if n == "w_up":
                out[n] = full
            elif n == "w_down":
                out[n] = full.reshape(N_DEV // 2, UP_CHUNK, D_MODEL)
            else:
                out[n] = full.reshape(D_MODEL, D_MODEL)
        return out

    sent = []

    def emit(grads):
        names = list(grads)
        blocks = [grads[n].reshape((N_DEV,) + shard_shape[n]) for n in names]
        own = [lax.dynamic_index_in_dim(b, me, 0, keepdims=False) for b in blocks]
        started = _exchange_start("scatter_" + "_".join(names) + "_start", blocks, [False] * len(names), me_arr)
        sent.append((names, own, started))
        return started[-1]

    def emit_small(small):
        sent_small.append((small, _exchange_start("gather_small_start", [small], [True], me_arr)))
        return sent_small[0][1][-1]

    sent_small = []
    grad_x = _local_step(
        x[0], mem[0], loss_target[0], rel_bias, g_mix, w_in_g, w_sc_full, g_attn_out, g_conv_out, g_xattn, g_mem,
        g_ffn, w_fc_full, b_ffn_conv.reshape(N_DEV, 1, UP_CHUNK), g_final.reshape(1, D_MODEL), ag[-1],
        late_weights, emit, emit_small)

    small_g, small_started = sent_small[0]
    after = sent[-1][2][-1]
    small_parts = _exchange_wait("gather_small_wait", small_started, [True], after)[0]
    big_out = {}
    after = small_parts
    for names, own, started in sent:
        lands = _exchange_wait("scatter_" + "_".join(names) + "_wait", started, [False] * len(names), after)
        for n, own_n, land in zip(names, own, lands):
            res = _adamw_big("adamw_" + n, big_w[n], own_n, land, big_m[n], big_v[n], me_arr)
            big_out[n] = [r[None] for r in res]
            after = res[0]

    as_rows = lambda a: a.reshape(N_DEV, UP_CHUNK)
    row1 = lambda a: a.reshape(1, D_MODEL)
    small_names = ["rel_bias", "g_mix", "g_attn_out", "g_conv_out", "g_xattn", "g_mem", "g_ffn", "b_ffn_conv", "g_final"]
    wmv = [
        (rel_bias, m_rel_bias, v_rel_bias), (g_mix, m_g_mix, v_g_mix), (g_attn_out, m_g_attn_out, v_g_attn_out),
        (g_conv_out, m_g_conv_out, v_g_conv_out), (g_xattn, m_g_xattn, v_g_xattn), (g_mem, m_g_mem, v_g_mem),
        (g_ffn, m_g_ffn, v_g_ffn), (as_rows(b_ffn_conv), as_rows(m_b_ffn_conv), as_rows(v_b_ffn_conv)),
        (row1(g_final), row1(m_g_final), row1(v_g_final))]
    g_packed, small_res = _adamw_small(small_g, small_parts, wmv, me_arr)
    small_out = dict(zip(small_names, small_res))
    loss = g_packed[ROW_LOSS, 0]
    small_out["b_ffn_conv"] = [a.reshape(1, 2 * D_FF) for a in small_out["b_ffn_conv"]]
    small_out["g_final"] = [a.reshape(D_MODEL) for a in small_out["g_final"]]

    g_wsc = lax.dynamic_slice(g_packed[ROW_WSC:ROW_WSC + 3, 0:CONV_W], (0, me * HEAD_DIM), (3, HEAD_DIM))
    g_wfc = lax.dynamic_slice(g_packed[ROW_WFC:ROW_WFC + 3 * N_DEV, 0:UP_CHUNK].reshape(3, N_DEV, UP_CHUNK),
                              (0, me, 0), (3, 1, UP_CHUNK)).reshape(3, UP_CHUNK)
    shard_res = _adamw_shards([(w_short_conv[0], g_wsc, m_w_short_conv[0], v_w_short_conv[0]),
                               (w_ffn_conv[0], g_wfc, m_w_ffn_conv[0], v_w_ffn_conv[0])])
    small_out["w_short_conv"] = [g_wsc[None]] + [a[None] for a in shard_res[0]]
    small_out["w_ffn_conv"] = [g_wfc[None]] + [a[None] for a in shard_res[1]]

    order = ["rel_bias", "g_mix", "w_in", "w_short_conv", "g_attn_out", "g_conv_out", "w_out", "g_xattn", "g_mem",
             "w_xq", "w_xk", "w_xv", "w_xo", "g_ffn", "w_up", "w_ffn_conv", "b_ffn_conv", "w_down", "g_final"]
    allp = {**big_out, **small_out}
    outs = [loss, grad_x[None]]
    for kind in range(4):
        outs += [allp[n][kind] for n in order]
    return tuple(outs)
```

```python
import functools
import math

import numpy as np
import jax
import jax.numpy as jnp
from jax import lax
from jax.experimental import pallas as pl
from jax.experimental.pallas import tpu as pltpu

F32 = jnp.float32
BF16 = jnp.bfloat16
MESH = pl.DeviceIdType.MESH

N_DEV = 8
D_MODEL = 1024
ATTN_W = 512
CONV_W = 512
N_HEADS = 8
HEAD_DIM = 64
WIN = 128
DILATIONS = (1, 4, 16)
N_BUCKETS = 32
BUCKET_MAX_EXACT = 16
BUCKET_MAX_DISTANCE = 2048
N_MEM_HEADS = 4
MEM_HEAD_DIM = 256
D_FF = 2816
IN_COLS = 3072
IN_CHUNK = IN_COLS // N_DEV
UP_CHUNK = 2 * D_FF // N_DEV
EPS = 1e-6

ADAM_LR = 0.001
ADAM_B1 = 0.9
ADAM_B2 = 0.999
ADAM_EPS = 1e-08
ADAM_WD = 0.01
ADAM_STEP = 10

SUBLANES = 8
LANES = 128
HALO = 16
TM = 512
TM_FFN = 256
TS_DW = 4096
VMEM_LIMIT = 56 * 1024 * 1024

ROW_RELB, ROW_GMIX, ROW_GXATTN, ROW_GMEM, ROW_GFFN, ROW_GFINAL, ROW_GAC = 0, 8, 16, 24, 32, 40, 48
ROW_WSC, ROW_BFC, ROW_WFC, ROW_LOSS, SMALL_ROWS = 56, 64, 72, 96, 104


def _cparams(n_grid):
    return pltpu.CompilerParams(dimension_semantics=("arbitrary",) * n_grid, vmem_limit_bytes=VMEM_LIMIT)


def _full(shape):
    nd = len(shape)
    return pl.BlockSpec(tuple(shape), lambda *_: (0,) * nd)


def _resident(shape):
    nd = len(shape)
    return pl.BlockSpec(tuple(shape), lambda *_: (0,) * nd, pipeline_mode=pl.Buffered(1))


ANY_SPEC = pl.BlockSpec(memory_space=pl.ANY)
HBM_SPEC = pl.BlockSpec(memory_space=pltpu.HBM)
SEM_SPEC = pl.BlockSpec(memory_space=pltpu.SEMAPHORE)
VMEM_SPEC = pl.BlockSpec(memory_space=pltpu.VMEM)
SMEM_SPEC = pl.BlockSpec(memory_space=pltpu.SMEM)
DATAFLOW = pltpu.SideEffectType.DATAFLOW_SIDE_EFFECTING


def _rms(x):
    r = lax.rsqrt(jnp.mean(x * x, axis=-1, keepdims=True) + EPS)
    return x * r, r


def _rms_bwd(xh, r, g, dy):
    dxh = dy * g
    return r * (dxh - xh * jnp.mean(dxh * xh, axis=-1, keepdims=True))


def _shift_down(u, halo, k):
    ru = pltpu.roll(u, k, 0)
    rh = pltpu.roll(halo, k, 0)
    row = lax.broadcasted_iota(jnp.int32, rh.shape, 0)
    head = jnp.where(row < k, rh, ru[0:SUBLANES])
    return jnp.concatenate([head, ru[SUBLANES:]], axis=0)


def _shift_up(u, halo, k):
    tm = u.shape[0]
    ru = pltpu.roll(u, tm - k, 0)
    rh = pltpu.roll(halo, SUBLANES - k, 0)
    row = lax.broadcasted_iota(jnp.int32, rh.shape, 0)
    tail = jnp.where(row >= SUBLANES - k, rh, ru[tm - SUBLANES:])
    return jnp.concatenate([ru[:tm - SUBLANES], tail], axis=0)


def _causal_conv3(u, halo, w_ref):
    return (_shift_down(u, halo, 2) * w_ref[0:1, :] + _shift_down(u, halo, 1) * w_ref[1:2, :]) + u * w_ref[2:3, :]


def _dot(a, b):
    return jnp.dot(a, b, preferred_element_type=F32)


def _dot_nt(a, b):
    return lax.dot_general(a, b, (((1,), (1,)), ((), ())), preferred_element_type=F32)


def _dot_tn(a, b):
    return lax.dot_general(a, b, (((0,), (0,)), ((), ())), preferred_element_type=F32)


def _sigmoid(x):
    return 1.0 / (1.0 + jnp.exp(-x))


def _bucket_tables():
    qi = np.arange(WIN)[:, None]
    kj = np.arange(2 * WIN)[None, :]
    steps = np.clip(qi + WIN - kj, 0, WIN)
    out = []
    for d in DILATIONS:
        dist = steps * d
        dd = np.maximum(dist, 1).astype(np.float32)
        large = BUCKET_MAX_EXACT + (
            np.log(dd / np.float32(BUCKET_MAX_EXACT)) / np.float32(math.log(BUCKET_MAX_DISTANCE / BUCKET_MAX_EXACT))
            * np.float32(N_BUCKETS - BUCKET_MAX_EXACT)).astype(np.int32)
        large = np.minimum(large, N_BUCKETS - 1)
        out.append(np.where(dist < BUCKET_MAX_EXACT, dist, large).astype(np.int32))
    return jnp.asarray(np.stack(out))


def _bias_fwd(rel_bias, buckets):
    def body(rb_ref, bk_ref, o_ref):
        for p in range(3):
            bk = bk_ref[p]
            for h in range(N_HEADS):
                acc = jnp.zeros((WIN, 2 * WIN), F32)
                for b in range(N_BUCKETS):
                    acc = jnp.where(bk == b, rb_ref[h, b], acc)
                o_ref[p, h] = acc

    return pl.pallas_call(
        body, name="bias_fwd",
        out_shape=jax.ShapeDtypeStruct((3, N_HEADS, WIN, 2 * WIN), F32),
        in_specs=[pl.BlockSpec(memory_space=pltpu.SMEM), pl.BlockSpec(memory_space=pltpu.VMEM)],
        out_specs=pl.BlockSpec(memory_space=pltpu.VMEM),
    )(rel_bias, buckets)


def _bias_bwd(dbias, buckets):
    def body(db_ref, bk_ref, o_ref):
        lane = lax.broadcasted_iota(jnp.int32, (1, D_MODEL), 1)
        rows = []
        for h in range(N_HEADS):
            row = jnp.zeros((1, D_MODEL), F32)
            for b in range(N_BUCKETS):
                tot = jnp.zeros((1, 1), F32)
                for p in range(3):
                    sel = jnp.where(bk_ref[p] == b, db_ref[p, h], 0.0)
                    tot = tot + jnp.sum(jnp.sum(sel, axis=0, keepdims=True), axis=1, keepdims=True)
                row = jnp.where(lane == b, tot, row)
            rows.append(row)
        o_ref[...] = jnp.concatenate(rows, axis=0)

    return pl.pallas_call(
        body, name="bias_bwd",
        out_shape=jax.ShapeDtypeStruct((N_HEADS, D_MODEL), F32),
        in_specs=[pl.BlockSpec(memory_space=pltpu.VMEM), pl.BlockSpec(memory_space=pltpu.VMEM)],
        out_specs=pl.BlockSpec(memory_space=pltpu.VMEM),
    )(dbias, buckets)


def _spread(val, scr_ref, out_refs, dtype):
    out_refs[0][...] = val.astype(dtype)
    n_blk = val.shape[1] // LANES
    for c in range(n_blk):
        scr_ref[c] = val[:, c * LANES:(c + 1) * LANES]
    for o_ref, d in zip(out_refs[1:], DILATIONS[1:]):
        for r in range(d):
            for c in range(n_blk):
                o_ref[r, :, c * LANES:(c + 1) * LANES] = scr_ref.at[c][pl.ds(r, TM // d, stride=d), :].astype(dtype)


def _gather_classes(blk_ref, scr_ref, d):
    n_blk = blk_ref.shape[2] // LANES
    for r in range(d):
        for c in range(n_blk):
            scr_ref.at[c][pl.ds(r, TM // d, stride=d), :] = blk_ref[r, :, c * LANES:(c + 1) * LANES].astype(F32)
    return jnp.concatenate([scr_ref[c] for c in range(n_blk)], axis=1)


def _class_specs(cols):
    return [pl.BlockSpec((TM, cols), lambda i: (i, 0))] + [
        pl.BlockSpec((d, TM // d, cols), lambda i: (0, i, 0)) for d in DILATIONS[1:]]


def _class_shapes(s, cols, dtype):
    return [jax.ShapeDtypeStruct((s, cols), dtype)] + [
        jax.ShapeDtypeStruct((d, s // d, cols), dtype) for d in DILATIONS[1:]]


def _rms_proj(x, g_mix, w_in_g, dep):
    s = x.shape[0]

    def body(x_ref, g_ref, w_ref, dep_ref, h_ref, q1, q4, q16, k1, k4, k16, v1, v4, v16, gb_ref, gc_ref, xi_ref, scr):
        xh, _ = _rms(x_ref[...])
        h = (xh * g_ref[...]).astype(BF16)
        h_ref[...] = h
        proj = jnp.concatenate([_dot(h, w_ref[j]) for j in range(N_DEV)], axis=1)
        _spread(proj[:, 0:512] * (HEAD_DIM ** -0.5), scr, (q1, q4, q16), BF16)
        _spread(proj[:, 512:1024], scr, (k1, k4, k16), BF16)
        _spread(proj[:, 1024:1536], scr, (v1, v4, v16), BF16)
        gb_ref[...] = proj[:, 1536:2048]
        gc_ref[...] = proj[:, 2048:2560]
        xi_ref[...] = proj[:, 2560:3072]

    row = lambda n: pl.BlockSpec((TM, n), lambda i: (i, 0))
    res = pl.pallas_call(
        body, name="rms_proj", grid=(s // TM,),
        out_shape=[jax.ShapeDtypeStruct((s, D_MODEL), BF16)] + _class_shapes(s, 512, BF16) * 3
        + [jax.ShapeDtypeStruct((s, 512), F32)] * 3,
        in_specs=[row(D_MODEL), _full(g_mix.shape), _full(w_in_g.shape), ANY_SPEC],
        out_specs=[row(D_MODEL)] + _class_specs(512) * 3 + [row(512)] * 3,
        scratch_shapes=[pltpu.VMEM((512 // LANES, TM, LANES), F32)],
        compiler_params=_cparams(1),
    )(x, g_mix, w_in_g, dep)
    return res[0], res[1:4], res[4:7], res[7:10], res[10], res[11], res[12]


def _pair_split(x2):
    lane = lax.broadcasted_iota(jnp.int32, x2.shape, 1)
    zero = jnp.zeros_like(x2)
    return jnp.where(lane < HEAD_DIM, x2, zero), jnp.where(lane >= HEAD_DIM, x2, zero)


def _pair_join(even, odd):
    lane = lax.broadcasted_iota(jnp.int32, (even.shape[0], LANES), 1)
    return jnp.where(lane < HEAD_DIM, even, odd)


def _band_mask(first):
    qi = lax.broadcasted_iota(jnp.int32, (WIN, 2 * WIN), 0)
    kj = lax.broadcasted_iota(jnp.int32, (WIN, 2 * WIN), 1)
    steps = qi + WIN - kj
    return (steps >= 0) & (steps <= WIN) & (kj >= jnp.where(first, WIN, 0))


def _swa_fwd(qc, kc, vc, bias, dil, dep):
    nb = qc.shape[1] // (2 * WIN)

    def body(q_ref, kp_ref, kc_ref, vp_ref, vc_ref, b_ref, dep_ref, o_ref, lse_ref, s_scr, p_scr):
        b = pl.program_id(1)
        pairs = [slice(a * LANES, (a + 1) * LANES) for a in range(N_HEADS // 2)]
        for sub in range(2):
            rows = slice(sub * WIN, (sub + 1) * WIN)

            def keys(prev_ref, cur_ref, sl):
                if sub == 0:
                    return jnp.concatenate([prev_ref[0, :, sl], cur_ref[0, 0:WIN, sl]], axis=0)
                return cur_ref[0, :, sl]

            for a, sl in enumerate(pairs):
                k2 = keys(kp_ref, kc_ref, sl)
                for e, qh in enumerate(_pair_split(q_ref[0, rows, sl])):
                    s_scr[sub, 2 * a + e] = _dot_nt(qh, k2)
            first = (b == 0) if sub == 0 else False
            lg = jnp.where(_band_mask(first), s_scr[sub] + b_ref[...], -jnp.inf)
            m = jnp.max(lg, axis=-1, keepdims=True)
            p = jnp.exp(lg - m)
            den = jnp.sum(p, axis=-1, keepdims=True)
            p_scr[sub] = p.astype(BF16)
            lse = m + jnp.log(den)
            for a, sl in enumerate(pairs):
                v_even, v_odd = _pair_split(keys(vp_ref, vc_ref, sl))
                o2 = _dot(p_scr[sub, 2 * a], v_even) + _dot(p_scr[sub, 2 * a + 1], v_odd)
                o_ref[0, rows, sl] = o2 / _pair_join(den[2 * a], den[2 * a + 1])
                lse_ref[0, rows, sl] = _pair_join(lse[2 * a], lse[2 * a + 1])

    cur = pl.BlockSpec((1, 2 * WIN, 512), lambda r, b: (r, b, 0))
    prev = pl.BlockSpec((1, WIN, 512), lambda r, b: (r, jnp.maximum(2 * b - 1, 0), 0))
    return pl.pallas_call(
        body, name=f"swa_fwd_d{dil}", grid=(dil, nb),
        out_shape=[jax.ShapeDtypeStruct(qc.shape, F32)] * 2,
        in_specs=[cur, prev, cur, prev, cur, _full(bias.shape), ANY_SPEC],
        out_specs=[cur] * 2,
        scratch_shapes=[pltpu.VMEM((2, N_HEADS, WIN, 2 * WIN), F32), pltpu.VMEM((2, N_HEADS, WIN, 2 * WIN), BF16)],
        compiler_params=_cparams(2),
    )(qc, kc, kc, vc, vc, bias, dep)


def _mix_out(branches, gb, gc, xi, x, w_sc, g_a, g_c, w_out):
    s = x.shape[0]
    tb = TM // SUBLANES

    def body(o1, l1, o4, l4, o16, l16, gb_ref, gc_ref, xi_ref, gch_ref, xih_ref, x_ref, wsc_ref,
             ga_ref, gcv_ref, wout_ref, attn_ref, lse1, lse4, lse16, mixed_ref, x1_ref, scr_a, scr_b, scr_c, scr_d):
        i = pl.program_id(0)
        la, lb, lc = l1[...], _gather_classes(l4, scr_a, 4), _gather_classes(l16, scr_b, 16)
        m_all = jnp.maximum(jnp.maximum(la, lb), lc)
        ea, eb, ec = jnp.exp(la - m_all), jnp.exp(lb - m_all), jnp.exp(lc - m_all)
        den = (ea + eb) + ec
        num = (ea * o1[...] + eb * _gather_classes(o4, scr_c, 4)) + ec * _gather_classes(o16, scr_d, 16)
        attn = num / den
        attn_ref[...] = attn
        _spread(m_all + jnp.log(den), scr_a, (lse1, lse4, lse16), F32)
        xa, _ = _rms(attn)
        u = gc_ref[...] * xi_ref[...]
        uh = jnp.where(i > 0, gch_ref[...] * xih_ref[...], 0.0)
        conv = gb_ref[...] * _causal_conv3(u, uh, wsc_ref)
        xc, _ = _rms(conv)
        mixed = jnp.concatenate([xa * ga_ref[...], xc * gcv_ref[...]], axis=1).astype(BF16)
        mixed_ref[...] = mixed
        x1_ref[...] = x_ref[...] + _dot(mixed, wout_ref[...])

    row = lambda n: pl.BlockSpec((TM, n), lambda i: (i, 0))
    halo = pl.BlockSpec((SUBLANES, 512), lambda i: (jnp.maximum(i * tb - 1, 0), 0))
    cs = _class_specs(512)
    flat = [a for br in branches for a in br]
    res = pl.pallas_call(
        body, name="mix_out", grid=(s // TM,),
        out_shape=[jax.ShapeDtypeStruct((s, 512), F32)] + _class_shapes(s, 512, F32)
        + [jax.ShapeDtypeStruct((s, D_MODEL), BF16), jax.ShapeDtypeStruct((s, D_MODEL), F32)],
        in_specs=[cs[0], cs[0], cs[1], cs[1], cs[2], cs[2], row(512), row(512), row(512), halo, halo,
                  row(D_MODEL), _full(w_sc.shape), _full(g_a.shape), _full(g_c.shape), _full(w_out.shape)],
        out_specs=[row(512)] + cs + [row(D_MODEL), row(D_MODEL)],
        scratch_shapes=[pltpu.VMEM((512 // LANES, TM, LANES), F32)] * 4,
        compiler_params=_cparams(1),
    )(*flat, gb, gc, xi, gc, xi, x, w_sc, g_a, g_c, w_out)
    return res[0], res[1:4], res[4], res[5]


def _mem_kv(mem, g_mem, w_xk, w_xv):
    def body(mem_ref, g_ref, wk_ref, wv_ref, mn_ref, k_ref, v_ref):
        xh, _ = _rms(mem_ref[...])
        mn = (xh * g_ref[...]).astype(BF16)
        mn_ref[...] = mn
        k_ref[...] = _dot(mn, wk_ref[...]).astype(BF16)
        v_ref[...] = _dot(mn, wv_ref[...]).astype(BF16)

    vm = pl.BlockSpec(memory_space=pltpu.VMEM)
    return pl.pallas_call(
        body, name="mem_kv",
        out_shape=[jax.ShapeDtypeStruct(mem.shape, BF16)] * 3,
        in_specs=[vm] * 4, out_specs=[vm] * 3,
        compiler_params=pltpu.CompilerParams(vmem_limit_bytes=VMEM_LIMIT),
    )(mem, g_mem, w_xk, w_xv)


def _xattn_fwd(x1, g, w_xq, k, v, w_xo, dep):
    s = x1.shape[0]

    def body(x1_ref, g_ref, wq_ref, k_ref, v_ref, wo_ref, dep_ref, h2_ref, q_ref, o_ref, x2_ref):
        x1v = x1_ref[...]
        xh, _ = _rms(x1v)
        h2 = (xh * g_ref[...]).astype(BF16)
        h2_ref[...] = h2
        qb = _dot(h2, wq_ref[...]).astype(BF16)
        q_ref[...] = qb
        outs = []
        for h in range(N_MEM_HEADS):
            sl = slice(h * MEM_HEAD_DIM, (h + 1) * MEM_HEAD_DIM)
            lg = _dot_nt(qb[:, sl], k_ref[:, sl]) * (MEM_HEAD_DIM ** -0.5)
            p = jnp.exp(lg - jnp.max(lg, axis=-1, keepdims=True))
            p = p / jnp.sum(p, axis=-1, keepdims=True)
            outs.append(_dot(p.astype(BF16), v_ref[:, sl]))
        o = jnp.concatenate(outs, axis=1).astype(BF16)
        o_ref[...] = o
        x2_ref[...] = x1v + _dot(o, wo_ref[...])

    row = pl.BlockSpec((TM, D_MODEL), lambda i: (i, 0))
    return pl.pallas_call(
        body, name="xattn_fwd", grid=(s // TM,),
        out_shape=[jax.ShapeDtypeStruct((s, D_MODEL), BF16)] * 3 + [jax.ShapeDtypeStruct((s, D_MODEL), F32)],
        in_specs=[row, _full(g.shape), _full(w_xq.shape), _full(k.shape), _full(v.shape), _full(w_xo.shape), ANY_SPEC],
        out_specs=[row] * 4,
        compiler_params=_cparams(1),
    )(x1, g, w_xq, k, v, w_xo, dep)


def _ffn_conv(h_ext, wup_ref, wfc_ref, bfc_ref, j):
    u = _dot(h_ext, wup_ref[j])
    w = wfc_ref[j]
    c = ((pltpu.roll(u, 2, 0) * w[0:1, :] + pltpu.roll(u, 1, 0) * w[1:2, :]) + u * w[2:3, :]) + bfc_ref[j]
    return c[HALO:]


def _ffn_fwd(x2, g, w_up_g, w_fc, b_fc, w_down_g, g_final, target):
    s = x2.shape[0]
    tb = TM_FFN // HALO
    half = N_DEV // 2

    def body(x_ref, xp_ref, g_ref, wup_ref, wfc_ref, bfc_ref, wd_ref, gf_ref, t_ref, h_ref, c_ref, act_ref, dx3_ref,
             loss_ref, dgf_ref):
        i = pl.program_id(0)

        @pl.when(i == 0)
        def _():
            loss_ref[...] = jnp.zeros_like(loss_ref)
            dgf_ref[...] = jnp.zeros_like(dgf_ref)

        x2v = x_ref[...]
        gv = g_ref[...]
        h = (_rms(x2v)[0] * gv).astype(BF16)
        h_ref[...] = h
        hp = jnp.where(i > 0, _rms(xp_ref[...])[0] * gv, 0.0).astype(BF16)
        h_ext = jnp.concatenate([hp, h], axis=0)
        down = jnp.zeros((TM_FFN, D_MODEL), F32)
        for j in range(half):
            cg = _ffn_conv(h_ext, wup_ref, wfc_ref, bfc_ref, j)
            cv = _ffn_conv(h_ext, wup_ref, wfc_ref, bfc_ref, j + half)
            c_ref[j] = cg
            c_ref[j + half] = cv
            a = ((cg * _sigmoid(cg)) * cv).astype(BF16)
            act_ref[j] = a
            down = down + _dot(a, wd_ref[j])
        x3 = x2v + down
        xh, r = _rms(x3)
        gf = gf_ref[...]
        e = xh * gf - t_ref[...]
        loss_ref[...] += 0.5 * jnp.sum(jnp.sum(e * e, axis=1, keepdims=True), axis=0, keepdims=True) / D_MODEL
        dy = e * (1.0 / D_MODEL)
        dgf_ref[0:1, :] += jnp.sum(dy * xh, axis=0, keepdims=True)
        dx3_ref[...] = _rms_bwd(xh, r, gf, dy)

    row = pl.BlockSpec((TM_FFN, D_MODEL), lambda i: (i, 0))
    prev = pl.BlockSpec((HALO, D_MODEL), lambda i: (jnp.maximum(i * tb - 1, 0), 0))
    return pl.pallas_call(
        body, name="ffn_fwd", grid=(s // TM_FFN,),
        out_shape=[jax.ShapeDtypeStruct((s, D_MODEL), BF16), jax.ShapeDtypeStruct((N_DEV, s, UP_CHUNK), F32),
                   jax.ShapeDtypeStruct((half, s, UP_CHUNK), BF16),
                   jax.ShapeDtypeStruct((s, D_MODEL), F32), jax.ShapeDtypeStruct((SUBLANES, 128), F32),
                   jax.ShapeDtypeStruct((SUBLANES, D_MODEL), F32)],
        in_specs=[row, prev, _full(g.shape), _resident(w_up_g.shape), _full(w_fc.shape), _full(b_fc.shape),
                  _resident(w_down_g.shape), _full(g_final.shape), row],
        out_specs=[row, pl.BlockSpec((N_DEV, TM_FFN, UP_CHUNK), lambda i: (0, i, 0)),
                   pl.BlockSpec((half, TM_FFN, UP_CHUNK), lambda i: (0, i, 0)), row,
                   _full((SUBLANES, 128)), _full((SUBLANES, D_MODEL))],
        compiler_params=_cparams(1),
    )(x2, x2, g, w_up_g, w_fc, b_fc, w_down_g, g_final, target)


def _ffn_bwd(dx3, h3, conv, x2, g, w_up_g, w_fc, w_down_g):
    s = x2.shape[0]
    tb = TM_FFN // HALO
    last = s // HALO - 1
    n_tiles = s // TM_FFN
    half = N_DEV // 2
    n_ext = TM_FFN + HALO

    def body(dx_ref, dxn_ref, h_ref, c_ref, cn_ref, x2_ref, g_ref, wup_ref, wfc_ref, wd_ref,
             dup_ref, dx2_ref, dg_ref, dwfc_ref, dbfc_ref):
        i = pl.program_id(0)

        @pl.when(i == 0)
        def _():
            dg_ref[...] = jnp.zeros_like(dg_ref)
            dwfc_ref[...] = jnp.zeros_like(dwfc_ref)
            dbfc_ref[...] = jnp.zeros_like(dbfc_ref)

        dxv = dx_ref[...]
        dxn = jnp.where(i < n_tiles - 1, dxn_ref[...], 0.0)
        dx_ext = jnp.concatenate([dxv, dxn], axis=0).astype(BF16)
        h = h_ref[...]
        dh = jnp.zeros((TM_FFN, D_MODEL), F32)
        for j in range(half):
            cg = jnp.concatenate([c_ref[j], cn_ref[j]], axis=0)
            cv = jnp.concatenate([c_ref[j + half], cn_ref[j + half]], axis=0)
            dact = _dot_nt(dx_ext, wd_ref[j])
            sg = _sigmoid(cg)
            parts = ((j + half, dact * (cg * sg)), (j, (dact * cv) * (sg * (1.0 + cg * (1.0 - sg)))))
            for jj, dc in parts:
                u = _dot(h, wup_ref[jj])
                dc0, dc1, dc2 = dc[:TM_FFN], pltpu.roll(dc, n_ext - 1, 0)[:TM_FFN], pltpu.roll(dc, n_ext - 2, 0)[:TM_FFN]
                dbfc_ref[jj:jj + 1, :] += jnp.sum(dc0, axis=0, keepdims=True)
                dwfc_ref[0, jj:jj + 1, :] += jnp.sum(dc2 * u, axis=0, keepdims=True)
                dwfc_ref[1, jj:jj + 1, :] += jnp.sum(dc1 * u, axis=0, keepdims=True)
                dwfc_ref[2, jj:jj + 1, :] += jnp.sum(dc0 * u, axis=0, keepdims=True)
                w = wfc_ref[jj]
                du = ((dc0 * w[2:3, :] + dc1 * w[1:2, :]) + dc2 * w[0:1, :]).astype(BF16)
                dup_ref[jj] = du
                dh = dh + _dot_nt(du, wup_ref[jj])
        xh, r = _rms(x2_ref[...])
        dg_ref[0:1, :] += jnp.sum(dh * xh, axis=0, keepdims=True)
        dx2_ref[...] = dxv + _rms_bwd(xh, r, g_ref[...], dh)

    row = pl.BlockSpec((TM_FFN, D_MODEL), lambda i: (i, 0))
    nxt = pl.BlockSpec((HALO, D_MODEL), lambda i: (jnp.minimum((i + 1) * tb, last), 0))
    cur_c = pl.BlockSpec((N_DEV, TM_FFN, UP_CHUNK), lambda i: (0, i, 0))
    nxt_c = pl.BlockSpec((N_DEV, HALO, UP_CHUNK), lambda i: (0, jnp.minimum((i + 1) * tb, last), 0))
    return pl.pallas_call(
        body, name="ffn_bwd", grid=(n_tiles,),
        out_shape=[jax.ShapeDtypeStruct((N_DEV, s, UP_CHUNK), BF16), jax.ShapeDtypeStruct((s, D_MODEL), F32),
                   jax.ShapeDtypeStruct((SUBLANES, D_MODEL), F32), jax.ShapeDtypeStruct((3, N_DEV, UP_CHUNK), F32),
                   jax.ShapeDtypeStruct((N_DEV, UP_CHUNK), F32)],
        in_specs=[row, nxt, row, cur_c, nxt_c, row, _full(g.shape), _resident(w_up_g.shape), _full(w_fc.shape),
                  _resident(w_down_g.shape)],
        out_specs=[cur_c, row, _full((SUBLANES, D_MODEL)), _full((3, N_DEV, UP_CHUNK)), _full((N_DEV, UP_CHUNK))],
        compiler_params=_cparams(1),
    )(dx3, dx3, h3, conv, conv, x2, g, w_up_g, w_fc, w_down_g)


def _xattn_bwd(dx2, o, q, k, v, w_xo, w_xq, x1, g, dep):
    s = x1.shape[0]

    def body(dx2_ref, o_ref, q_ref, k_ref, v_ref, wo_ref, wq_ref, x1_ref, g_ref, dep_ref, dq_ref, dx1_ref, dk_ref,
             dv_ref, dg_ref):
        @pl.when(pl.program_id(0) == 0)
        def _():
            dk_ref[...] = jnp.zeros_like(dk_ref)
            dv_ref[...] = jnp.zeros_like(dv_ref)
            dg_ref[...] = jnp.zeros_like(dg_ref)

        dx2v = dx2_ref[...]
        do = _dot_nt(dx2v.astype(BF16), wo_ref[...])
        dqs = []
        for h in range(N_MEM_HEADS):
            sl = slice(h * MEM_HEAD_DIM, (h + 1) * MEM_HEAD_DIM)
            qh, kh, vh = q_ref[:, sl], k_ref[:, sl], v_ref[:, sl]
            lg = _dot_nt(qh, kh) * (MEM_HEAD_DIM ** -0.5)
            p = jnp.exp(lg - jnp.max(lg, axis=-1, keepdims=True))
            p = p / jnp.sum(p, axis=-1, keepdims=True)
            doh = do[:, sl].astype(BF16)
            dp = _dot_nt(doh, vh)
            ds = (p * (dp - jnp.sum(p * dp, axis=-1, keepdims=True)) * (MEM_HEAD_DIM ** -0.5)).astype(BF16)
            dqs.append(_dot(ds, kh))
            dk_ref[:, sl] += _dot_tn(ds, qh)
            dv_ref[:, sl] += _dot_tn(p.astype(BF16), doh)
        dq = jnp.concatenate(dqs, axis=1).astype(BF16)
        dq_ref[...] = dq
        dh2 = _dot_nt(dq, wq_ref[...])
        xh, r = _rms(x1_ref[...])
        dg_ref[0:1, :] += jnp.sum(dh2 * xh, axis=0, keepdims=True)
        dx1_ref[...] = dx2v + _rms_bwd(xh, r, g_ref[...], dh2)

    row = pl.BlockSpec((TM, D_MODEL), lambda i: (i, 0))
    return pl.pallas_call(
        body, name="xattn_bwd", grid=(s // TM,),
        out_shape=[jax.ShapeDtypeStruct((s, D_MODEL), BF16), jax.ShapeDtypeStruct((s, D_MODEL), F32),
                   jax.ShapeDtypeStruct(k.shape, F32), jax.ShapeDtypeStruct(k.shape, F32),
                   jax.ShapeDtypeStruct((SUBLANES, D_MODEL), F32)],
        in_specs=[row, row, row, _full(k.shape), _full(v.shape), _full(w_xo.shape), _full(w_xq.shape), row,
                  _full(g.shape), ANY_SPEC],
        out_specs=[row, row, _full(k.shape), _full(k.shape), _full((SUBLANES, D_MODEL))],
        compiler_params=_cparams(1),
    )(dx2, o, q, k, v, w_xo, w_xq, x1, g, dep)


def _mem_kv_bwd(dk, dv, mem_n, mem, w_xk, w_xv):
    def body(dk_ref, dv_ref, mn_ref, mem_ref, wk_ref, wv_ref, dwk_ref, dwv_ref, dg_ref):
        dkb, dvb = dk_ref[...].astype(BF16), dv_ref[...].astype(BF16)
        mn = mn_ref[...]
        dwk_ref[...] = _dot_tn(mn, dkb).astype(BF16)
        dwv_ref[...] = _dot_tn(mn, dvb).astype(BF16)
        dmn = _dot_nt(dkb, wk_ref[...]) + _dot_nt(dvb, wv_ref[...])
        xh, _ = _rms(mem_ref[...])
        dg_ref[...] = jnp.zeros_like(dg_ref)
        dg_ref[0:1, :] = jnp.sum(dmn * xh, axis=0, keepdims=True)

    vm = pl.BlockSpec(memory_space=pltpu.VMEM)
    return pl.pallas_call(
        body, name="mem_kv_bwd",
        out_shape=[jax.ShapeDtypeStruct(w_xk.shape, BF16), jax.ShapeDtypeStruct(w_xv.shape, BF16),
                   jax.ShapeDtypeStruct((SUBLANES, D_MODEL), F32)],
        in_specs=[vm] * 6, out_specs=[vm] * 3,
        compiler_params=pltpu.CompilerParams(vmem_limit_bytes=VMEM_LIMIT),
    )(dk, dv, mem_n, mem, w_xk, w_xv)


def _mix_out_bwd(dx1, w_out, attn, gb, gc, xi, w_sc, g_a, g_c, dep):
    s = dx1.shape[0]
    tb = TM // SUBLANES

    def body(dx1_ref, wout_ref, attn_ref, gb_ref, gc_ref, xi_ref, gch_ref, xih_ref, wsc_ref, ga_ref, gcv_ref, dep_ref,
             da1, da4, da16, dd1, dd4, dd16, dgb_ref, dcv_ref, dga_ref, dgc_ref, dwsc_ref, scr):
        i = pl.program_id(0)

        @pl.when(i == 0)
        def _():
            dga_ref[...] = jnp.zeros_like(dga_ref)
            dgc_ref[...] = jnp.zeros_like(dgc_ref)
            dwsc_ref[...] = jnp.zeros_like(dwsc_ref)

        dmixed = _dot_nt(dx1_ref[...].astype(BF16), wout_ref[...])
        da, dcn = dmixed[:, :ATTN_W], dmixed[:, ATTN_W:]
        attn = attn_ref[...]
        xa, ra = _rms(attn)
        dga_ref[0:1, :] += jnp.sum(da * xa, axis=0, keepdims=True)
        dattn = _rms_bwd(xa, ra, ga_ref[...], da)
        _spread(dattn, scr, (da1, da4, da16), F32)
        prod = dattn * attn
        dd = jnp.concatenate(
            [jnp.broadcast_to(jnp.sum(prod[:, h * HEAD_DIM:(h + 1) * HEAD_DIM], axis=-1, keepdims=True),
                              (TM, HEAD_DIM)) for h in range(N_HEADS)], axis=1)
        _spread(dd, scr, (dd1, dd4, dd16), F32)
        gbv = gb_ref[...]
        u = gc_ref[...] * xi_ref[...]
        uh = jnp.where(i > 0, gch_ref[...] * xih_ref[...], 0.0)
        u2, u1 = _shift_down(u, uh, 2), _shift_down(u, uh, 1)
        cv = (u2 * wsc_ref[0:1, :] + u1 * wsc_ref[1:2, :]) + u * wsc_ref[2:3, :]
        xc, rc = _rms(gbv * cv)
        dgc_ref[0:1, :] += jnp.sum(dcn * xc, axis=0, keepdims=True)
        dconv = _rms_bwd(xc, rc, gcv_ref[...], dcn)
        dgb_ref[...] = dconv * cv
        dcv = dconv * gbv
        dcv_ref[...] = dcv
        dwsc_ref[0:1, :] += jnp.sum(dcv * u2, axis=0, keepdims=True)
        dwsc_ref[1:2, :] += jnp.sum(dcv * u1, axis=0, keepdims=True)
        dwsc_ref[2:3, :] += jnp.sum(dcv * u, axis=0, keepdims=True)

    row = lambda n: pl.BlockSpec((TM, n), lambda i: (i, 0))
    halo = pl.BlockSpec((SUBLANES, 512), lambda i: (jnp.maximum(i * tb - 1, 0), 0))
    acc = _full((SUBLANES, 512))
    res = pl.pallas_call(
        body, name="mix_out_bwd", grid=(s // TM,),
        out_shape=_class_shapes(s, 512, F32) * 2 + [jax.ShapeDtypeStruct((s, 512), F32)] * 2
        + [jax.ShapeDtypeStruct((SUBLANES, 512), F32)] * 3,
        in_specs=[row(D_MODEL), _full(w_out.shape), row(512), row(512), row(512), row(512), halo, halo,
                  _full(w_sc.shape), _full(g_a.shape), _full(g_c.shape), ANY_SPEC],
        out_specs=_class_specs(512) * 2 + [row(512)] * 2 + [acc] * 3,
        scratch_shapes=[pltpu.VMEM((512 // LANES, TM, LANES), F32)],
        compiler_params=_cparams(1),
    )(dx1, w_out, attn, gb, gc, xi, gc, xi, w_sc, g_a, g_c, dep)
    return res[0:3], res[3:6], res[6], res[7], res[8], res[9], res[10]


def _swa_bwd(qc, kc, vc, doc, lsec, ddc, bias, dil, dep):
    n128 = qc.shape[1] // WIN
    nb = n128 // 2

    def body(q_ref, qn_ref, kp_ref, kc_ref, vp_ref, vc_ref, do_ref, don_ref, lse_ref, lsen_ref, dd_ref, ddn_ref,
             b_ref, dep_ref, dq_ref, dk_ref, dv_ref, db_ref, s_scr, dp_scr, sn_scr, dpn_scr, ds_scr, p_scr, dsn_scr,
             pn_scr):
        r, b = pl.program_id(0), pl.program_id(1)

        @pl.when((r == 0) & (b == 0))
        def _():
            db_ref[...] = jnp.zeros_like(db_ref)

        pairs = [slice(a * LANES, (a + 1) * LANES) for a in range(N_HEADS // 2)]
        blk_a, blk_b = slice(0, WIN), slice(WIN, 2 * WIN)
        per_head = lambda ref, rows: jnp.stack([ref[0, rows, h * HEAD_DIM:h * HEAD_DIM + 1] for h in range(N_HEADS)])
        for a, sl in enumerate(pairs):
            k_pa = jnp.concatenate([kp_ref[0, :, sl], kc_ref[0, blk_a, sl]], axis=0)
            v_pa = jnp.concatenate([vp_ref[0, :, sl], vc_ref[0, blk_a, sl]], axis=0)
            for sub, (rows, k2, v2) in enumerate(((blk_a, k_pa, v_pa), (blk_b, kc_ref[0, :, sl], vc_ref[0, :, sl]))):
                q_eo = _pair_split(q_ref[0, rows, sl])
                do_eo = _pair_split(do_ref[0, rows, sl].astype(BF16))
                for e in range(2):
                    s_scr[sub, 2 * a + e] = _dot_nt(q_eo[e], k2)
                    dp_scr[sub, 2 * a + e] = _dot_nt(do_eo[e], v2)
            qn_eo = _pair_split(qn_ref[0, :, sl])
            don_eo = _pair_split(don_ref[0, :, sl].astype(BF16))
            for e in range(2):
                sn_scr[2 * a + e] = _dot_nt(qn_eo[e], kc_ref[0, blk_b, sl])
                dpn_scr[2 * a + e] = _dot_nt(don_eo[e], vc_ref[0, blk_b, sl])
        bias = b_ref[...]
        for sub, rows in enumerate((blk_a, blk_b)):
            first = (b == 0) if sub == 0 else False
            p = jnp.exp(jnp.where(_band_mask(first), s_scr[sub] + bias, -jnp.inf) - per_head(lse_ref, rows))
            ds = p * (dp_scr[sub] - per_head(dd_ref, rows))
            db_ref[...] += ds
            ds_scr[sub] = ds.astype(BF16)
            p_scr[sub] = p.astype(BF16)
        qi = lax.broadcasted_iota(jnp.int32, (WIN, WIN), 0)
        kj = lax.broadcasted_iota(jnp.int32, (WIN, WIN), 1)
        valid_n = kj >= qi + jnp.where(b + 1 < nb, 0, WIN)
        every = slice(0, WIN)
        pn = jnp.exp(jnp.where(valid_n, sn_scr[...] + bias[:, :, :WIN], -jnp.inf) - per_head(lsen_ref, every))
        dsn_scr[...] = (pn * (dpn_scr[...] - per_head(ddn_ref, every))).astype(BF16)
        pn_scr[...] = pn.astype(BF16)
        for a, sl in enumerate(pairs):
            k_pa = _pair_split(jnp.concatenate([kp_ref[0, :, sl], kc_ref[0, blk_a, sl]], axis=0))
            k_ab = _pair_split(kc_ref[0, :, sl])
            qa_eo, qb_eo = _pair_split(q_ref[0, blk_a, sl]), _pair_split(q_ref[0, blk_b, sl])
            doa_eo = _pair_split(do_ref[0, blk_a, sl].astype(BF16))
            dob_eo = _pair_split(do_ref[0, blk_b, sl].astype(BF16))
            qn_eo = _pair_split(qn_ref[0, :, sl])
            don_eo = _pair_split(don_ref[0, :, sl].astype(BF16))
            acc = None
            for e in range(2):
                h = 2 * a + e
                terms = (_dot(ds_scr[0, h], k_pa[e]),
                         _dot(ds_scr[1, h], k_ab[e]),
                         _dot_tn(ds_scr[0, h, :, WIN:], qa_eo[e]) + _dot_tn(ds_scr[1, h, :, :WIN], qb_eo[e]),
                         _dot_tn(ds_scr[1, h, :, WIN:], qb_eo[e]) + _dot_tn(dsn_scr[h], qn_eo[e]),
                         _dot_tn(p_scr[0, h, :, WIN:], doa_eo[e]) + _dot_tn(p_scr[1, h, :, :WIN], dob_eo[e]),
                         _dot_tn(p_scr[1, h, :, WIN:], dob_eo[e]) + _dot_tn(pn_scr[h], don_eo[e]))
                acc = terms if acc is None else tuple(x + y for x, y in zip(acc, terms))
            dq_ref[0, blk_a, sl], dq_ref[0, blk_b, sl] = acc[0], acc[1]
            dk_ref[0, blk_a, sl], dk_ref[0, blk_b, sl] = acc[2], acc[3]
            dv_ref[0, blk_a, sl], dv_ref[0, blk_b, sl] = acc[4], acc[5]

    cur = pl.BlockSpec((1, 2 * WIN, 512), lambda r, b: (r, b, 0))
    prev = pl.BlockSpec((1, WIN, 512), lambda r, b: (r, jnp.maximum(2 * b - 1, 0), 0))
    nxt = pl.BlockSpec((1, WIN, 512), lambda r, b: (r, jnp.minimum(2 * b + 2, n128 - 1), 0))
    wide, narrow = (2, N_HEADS, WIN, 2 * WIN), (N_HEADS, WIN, WIN)
    return pl.pallas_call(
        body, name=f"swa_bwd_d{dil}", grid=(dil, nb),
        out_shape=[jax.ShapeDtypeStruct(qc.shape, F32)] * 3 + [jax.ShapeDtypeStruct(bias.shape, F32)],
        in_specs=[cur, nxt, prev, cur, prev, cur, cur, nxt, cur, nxt, cur, nxt, _full(bias.shape), ANY_SPEC],
        out_specs=[cur] * 3 + [_full(bias.shape)],
        scratch_shapes=[pltpu.VMEM(wide, F32), pltpu.VMEM(wide, F32), pltpu.VMEM(narrow, F32),
                        pltpu.VMEM(narrow, F32), pltpu.VMEM(wide, BF16), pltpu.VMEM(wide, BF16),
                        pltpu.VMEM(narrow, BF16), pltpu.VMEM(narrow, BF16)],
        compiler_params=_cparams(2),
    )(qc, qc, kc, kc, vc, vc, doc, doc, lsec, lsec, ddc, ddc, bias, dep)


def _in_proj_bwd(dqs, dks, dvs, dgb, dcv, gc, xi, w_sc, w_in_g, x, g_mix, dx1):
    s = x.shape[0]
    tb = TM // SUBLANES
    last = s // SUBLANES - 1
    n_tiles = s // TM

    def body(dq1, dq4, dq16, dk1, dk4, dk16, dv1, dv4, dv16, dgb_ref, dcv_ref, dcvn_ref, gc_ref, xi_ref, wsc_ref,
             win_ref, x_ref, g_ref, dx1_ref, dproj_ref, gx_ref, dg_ref, scr_a, scr_b):
        i = pl.program_id(0)

        @pl.when(i == 0)
        def _():
            dg_ref[...] = jnp.zeros_like(dg_ref)

        d0 = dcv_ref[...]
        dn = jnp.where(i < n_tiles - 1, dcvn_ref[...], 0.0)
        du = (d0 * wsc_ref[2:3, :] + _shift_up(d0, dn, 1) * wsc_ref[1:2, :]) + _shift_up(d0, dn, 2) * wsc_ref[0:1, :]
        merge = lambda a, b4, b16: (a[...] + _gather_classes(b4, scr_a, 4)) + _gather_classes(b16, scr_b, 16)
        dq = merge(dq1, dq4, dq16) * (HEAD_DIM ** -0.5)
        dk = merge(dk1, dk4, dk16)
        dv = merge(dv1, dv4, dv16)
        dproj = jnp.concatenate([dq, dk, dv, dgb_ref[...], du * xi_ref[...], du * gc_ref[...]], axis=1).astype(BF16)
        dproj_ref[...] = dproj
        dh = jnp.zeros((TM, D_MODEL), F32)
        for j in range(N_DEV):
            dh = dh + _dot_nt(dproj[:, j * IN_CHUNK:(j + 1) * IN_CHUNK], win_ref[j])
        xh, r = _rms(x_ref[...])
        dg_ref[0:1, :] += jnp.sum(dh * xh, axis=0, keepdims=True)
        gx_ref[...] = dx1_ref[...] + _rms_bwd(xh, r, g_ref[...], dh)

    row = lambda n: pl.BlockSpec((TM, n), lambda i: (i, 0))
    nxt = pl.BlockSpec((SUBLANES, 512), lambda i: (jnp.minimum((i + 1) * tb, last), 0))
    return pl.pallas_call(
        body, name="in_proj_bwd", grid=(n_tiles,),
        out_shape=[jax.ShapeDtypeStruct((s, IN_COLS), BF16), jax.ShapeDtypeStruct((s, D_MODEL), F32),
                   jax.ShapeDtypeStruct((SUBLANES, D_MODEL), F32)],
        in_specs=_class_specs(512) * 3 + [row(512), row(512), nxt, row(512), row(512), _full(w_sc.shape),
                                          _full(w_in_g.shape), row(D_MODEL), _full(g_mix.shape), row(D_MODEL)],
        out_specs=[row(IN_COLS), row(D_MODEL), _full((SUBLANES, D_MODEL))],
        scratch_shapes=[pltpu.VMEM((512 // LANES, TM, LANES), F32)] * 2,
        compiler_params=_cparams(1),
    )(*dqs, *dks, *dvs, dgb, dcv, dcv, gc, xi, w_sc, w_in_g, x, g_mix, dx1)


def _dw(a, b, dep, name, a_chunked=False, b_chunked=False, n_chunks=1, chunk_cols=None):
    ts = TS_DW if (a_chunked or b_chunked or chunk_cols) else TS_DW // 2
    if a_chunked:
        nj, s, kk = a.shape
        nn = b.shape[1]
        a_spec = pl.BlockSpec((1, ts, kk), lambda j, t: (j, t, 0))
        b_spec = pl.BlockSpec((ts, nn), lambda j, t: (t, 0))
    elif b_chunked:
        nj, s, nn = b.shape
        kk = a.shape[1]
        a_spec = pl.BlockSpec((ts, kk), lambda j, t: (t, 0))
        b_spec = pl.BlockSpec((1, ts, nn), lambda j, t: (j, t, 0))
    else:
        s, kk = a.shape
        nj, nn = (n_chunks, chunk_cols) if chunk_cols else (1, b.shape[1])
        a_spec = pl.BlockSpec((ts, kk), lambda j, t: (t, 0))
        b_spec = pl.BlockSpec((ts, nn), lambda j, t: (t, j))
    n_steps = s // ts

    def body(a_ref, b_ref, dep_ref, o_ref, acc):
        t = pl.program_id(1)

        @pl.when(t == 0)
        def _():
            acc[...] = jnp.zeros_like(acc)

        av = (a_ref[0] if a_chunked else a_ref[...]).astype(BF16)
        bv = (b_ref[0] if b_chunked else b_ref[...]).astype(BF16)
        acc[...] += _dot_tn(av, bv)

        @pl.when(t == n_steps - 1)
        def _():
            o_ref[0] = acc[...].astype(BF16)

    return pl.pallas_call(
        body, name=name, grid=(nj, n_steps),
        out_shape=jax.ShapeDtypeStruct((nj, kk, nn), BF16),
        in_specs=[a_spec, b_spec, ANY_SPEC],
        out_specs=pl.BlockSpec((1, kk, nn), lambda j, t: (j, 0, 0)),
        scratch_shapes=[pltpu.VMEM((kk, nn), F32)],
        compiler_params=_cparams(2),
    )(a, b, dep)


def _adamw_math(w, g, m, v):
    m2 = ADAM_B1 * m + (1.0 - ADAM_B1) * g
    v2 = ADAM_B2 * v + (1.0 - ADAM_B2) * (g * g)
    m_hat = m2 / (1.0 - ADAM_B1 ** ADAM_STEP)
    v_hat = v2 / (1.0 - ADAM_B2 ** ADAM_STEP)
    delta = -ADAM_LR * (m_hat / (jnp.sqrt(v_hat) + ADAM_EPS) + ADAM_WD * w)
    return delta, m2, v2


def _sum_parts(me, own, p_ref):
    g = None
    for i in range(N_DEV):
        part = jnp.where(me == i, own.astype(F32), p_ref[i].astype(F32))
        g = part if g is None else g + part
    return g


def _adamw_big(name, w, own, parts, m, v, me_arr):
    rr, cc = w.shape
    tr = rr // 4 if rr >= 512 else rr

    def body(me_ref, w_ref, own_ref, p_ref, m_ref, v_ref, g_ref, d_ref, nm_ref, nv_ref):
        g = _sum_parts(me_ref[0], own_ref[...], p_ref)
        g_ref[...] = g
        d_ref[...], nm_ref[...], nv_ref[...] = _adamw_math(w_ref[...], g, m_ref[...], v_ref[...])

    row = pl.BlockSpec((tr, cc), lambda i: (i, 0))
    return pl.pallas_call(
        body, name=name, grid=(rr // tr,),
        out_shape=[jax.ShapeDtypeStruct((rr, cc), F32)] * 4,
        in_specs=[SMEM_SPEC, row, row, pl.BlockSpec((N_DEV, tr, cc), lambda i: (0, i, 0)), row, row],
        out_specs=[row] * 4,
        compiler_params=_cparams(1),
    )(me_arr, w, own, parts, m, v)


def _small_slices():
    return [
        (slice(ROW_RELB, ROW_RELB + 8), slice(0, N_BUCKETS)),
        (slice(ROW_GMIX, ROW_GMIX + 1), slice(0, D_MODEL)),
        (slice(ROW_GAC, ROW_GAC + 1), slice(0, ATTN_W)),
        (slice(ROW_GAC, ROW_GAC + 1), slice(ATTN_W, D_MODEL)),
        (slice(ROW_GXATTN, ROW_GXATTN + 1), slice(0, D_MODEL)),
        (slice(ROW_GMEM, ROW_GMEM + 1), slice(0, D_MODEL)),
        (slice(ROW_GFFN, ROW_GFFN + 1), slice(0, D_MODEL)),
        (slice(ROW_BFC, ROW_BFC + 8), slice(0, UP_CHUNK)),
        (slice(ROW_GFINAL, ROW_GFINAL + 1), slice(0, D_MODEL)),
    ]


def _adamw_small(own, parts, wmv, me_arr):
    slices = _small_slices()
    n = len(slices)

    def body(*refs):
        me_ref, own_ref, p_ref = refs[:3]
        ins = refs[3:3 + 3 * n]
        g_ref = refs[3 + 3 * n]
        outs = refs[4 + 3 * n:]
        g = _sum_parts(me_ref[0], own_ref[...], p_ref)
        g_ref[...] = g
        for a, (rs, ls) in enumerate(slices):
            ga = g[rs, ls]
            outs[4 * a][...] = ga
            outs[4 * a + 1][...], outs[4 * a + 2][...], outs[4 * a + 3][...] = _adamw_math(
                ins[3 * a][...], ga, ins[3 * a + 1][...], ins[3 * a + 2][...])

    vm = pl.BlockSpec(memory_space=pltpu.VMEM)
    flat = [t for trip in wmv for t in trip]
    out_shape = [jax.ShapeDtypeStruct((SMALL_ROWS, D_MODEL), F32)]
    for w, _, _ in wmv:
        out_shape += [jax.ShapeDtypeStruct(w.shape, F32)] * 4
    res = pl.pallas_call(
        body, name="adamw_small", out_shape=out_shape,
        in_specs=[SMEM_SPEC] + [vm] * (2 + 3 * n), out_specs=[vm] * len(out_shape),
    )(me_arr, own, parts, *flat)
    return res[0], [res[1 + 4 * a:5 + 4 * a] for a in range(n)]


def _adamw_shards(items):
    n = len(items)

    def body(*refs):
        for a in range(n):
            w_ref, g_ref, m_ref, v_ref = refs[4 * a:4 * a + 4]
            d_ref, nm_ref, nv_ref = refs[4 * n + 3 * a:4 * n + 3 * a + 3]
            d_ref[...], nm_ref[...], nv_ref[...] = _adamw_math(w_ref[...], g_ref[...], m_ref[...], v_ref[...])

    vm = pl.BlockSpec(memory_space=pltpu.VMEM)
    out_shape = []
    for w, _, _, _ in items:
        out_shape += [jax.ShapeDtypeStruct(w.shape, F32)] * 3
    res = pl.pallas_call(
        body, name="adamw_shards", out_shape=out_shape, in_specs=[vm] * (4 * n), out_specs=[vm] * (3 * n),
    )(*[t for it in items for t in it])
    return [res[3 * a:3 * a + 3] for a in range(n)]


def _mesh_pos():
    return lax.axis_index("x"), lax.axis_index("y"), lax.axis_index("c")


def _dev_index(p):
    return 4 * p[0] + 2 * p[1] + p[2]


def _all_gather(shards):
    n = len(shards)

    def body(*refs):
        ins, outs = refs[:n], refs[n:2 * n]
        send_sems, recv_sems, loc_sems = refs[2 * n:]
        x, y, c = _mesh_pos()
        me, sib = (x, y, c), (x, y, 1 - c)
        chips = [(1 - x, y), (x, 1 - y), (1 - x, 1 - y)]

        def cp(a, k, block, to, src=None):
            dst = outs[a].at[_dev_index(block)]
            return pltpu.make_async_remote_copy(
                src_ref=dst if src is None else src, dst_ref=dst, send_sem=send_sems.at[a, k],
                recv_sem=recv_sems.at[a, k], device_id=to, device_id_type=MESH)

        mine = [pltpu.make_async_copy(ins[a], outs[a].at[_dev_index(me)], loc_sems.at[a]) for a in range(n)]
        for m_ in mine:
            m_.start()
        first = []
        for a in range(n):
            first.append(cp(a, 0, me, sib, src=ins[a]))
            first += [cp(a, 1 + j, me, (*chip, c), src=ins[a]) for j, chip in enumerate(chips)]
        for f in first:
            f.start()
        passed = []
        for a in range(n):
            for j, chip in enumerate(chips):
                cp(a, 1 + j, (*chip, c), me).wait_recv()
                fwd = cp(a, 4 + j, (*chip, c), sib)
                fwd.start()
                passed.append(fwd)
        for a in range(n):
            cp(a, 0, sib, me).wait_recv()
            for j, chip in enumerate(chips):
                cp(a, 4 + j, (*chip, 1 - c), me).wait_recv()
        for f in first + passed:
            f.wait_send()
        for m_ in mine:
            m_.wait()

    hbm = pl.BlockSpec(memory_space=pltpu.HBM)
    return pl.pallas_call(
        body, name="all_gather_weights",
        out_shape=[jax.ShapeDtypeStruct((N_DEV,) + a.shape, a.dtype) for a in shards],
        in_specs=[hbm] * n, out_specs=[hbm] * n,
        scratch_shapes=[pltpu.SemaphoreType.DMA((n, 7)), pltpu.SemaphoreType.DMA((n, 7)),
                        pltpu.SemaphoreType.DMA((n,))],
    )(*shards)


def _peers():
    x, y, c = _mesh_pos()
    return (x, y, c), [((1 - x) if k & 4 else x, (1 - y) if k & 2 else y, (1 - c) if k & 1 else c)
                       for k in range(1, 8)]


def _exchange_copy(src_ref, land_ref, whole, send_sems, recv_sems, a, k, peer, slot):
    src = src_ref if whole else src_ref.at[_dev_index(peer)]
    return pltpu.make_async_remote_copy(
        src_ref=src, dst_ref=land_ref.at[slot], send_sem=send_sems.at[7 * a + k], recv_sem=recv_sems.at[7 * a + k],
        device_id=peer, device_id_type=MESH)


def _exchange_start(name, srcs, whole, dep):
    n = len(srcs)
    lands = [lax.empty(((N_DEV,) + s.shape) if w else s.shape, s.dtype) for s, w in zip(srcs, whole)]

    def body(*refs):
        src_refs, land_refs = refs[:n], refs[n:2 * n]
        send_sems, recv_sems, token = refs[2 * n + 1], refs[2 * n + 2], refs[-1]
        me, peers = _peers()
        for a in range(n):
            for k, peer in enumerate(peers):
                _exchange_copy(src_refs[a], land_refs[a], whole[a], send_sems, recv_sems, a, k, peer,
                               _dev_index(me)).start()
        token[...] = jnp.zeros_like(token)

    res = pl.pallas_call(
        body, name=name,
        out_shape=(pltpu.SemaphoreType.DMA((7 * n,)), pltpu.SemaphoreType.DMA((7 * n,)),
                   *[pltpu.HBM(a.shape, a.dtype) for a in srcs], *[pltpu.HBM(a.shape, a.dtype) for a in lands],
                   jax.ShapeDtypeStruct((SUBLANES, 128), F32)),
        in_specs=[HBM_SPEC] * (2 * n) + [ANY_SPEC],
        out_specs=(SEM_SPEC, SEM_SPEC, *([HBM_SPEC] * (2 * n)), VMEM_SPEC),
        input_output_aliases={i: 2 + i for i in range(2 * n)},
        compiler_params=pltpu.CompilerParams(has_side_effects=DATAFLOW),
    )(*[pltpu.with_memory_space_constraint(a, pltpu.HBM) for a in srcs],
      *[pltpu.with_memory_space_constraint(a, pltpu.HBM) for a in lands], dep)
    return res[0], res[1], list(res[2:2 + n]), list(res[2 + n:2 + 2 * n]), res[-1]


def _exchange_wait(name, started, whole, after, which=None):
    send_sems, recv_sems, srcs, lands, _ = started
    which = list(range(len(srcs))) if which is None else which
    srcs, lands = [srcs[a] for a in which], [lands[a] for a in which]
    n = len(srcs)

    def body(*refs):
        src_refs, land_refs = refs[:n], refs[n:2 * n]
        send_sems, recv_sems = refs[2 * n], refs[2 * n + 1]
        _, peers = _peers()
        for i, a in enumerate(which):
            for k, peer in enumerate(peers):
                cp = _exchange_copy(src_refs[i], land_refs[i], whole[a], send_sems, recv_sems, a, k, peer,
                                    _dev_index(peer))
                cp.wait_send()
                cp.wait_recv()

    res = pl.pallas_call(
        body, name=name,
        out_shape=[pltpu.HBM(a.shape, a.dtype) for a in srcs + lands],
        in_specs=[HBM_SPEC] * (2 * n) + [SEM_SPEC, SEM_SPEC, ANY_SPEC],
        out_specs=[HBM_SPEC] * (2 * n),
        input_output_aliases={i: i for i in range(2 * n)},
        compiler_params=pltpu.CompilerParams(has_side_effects=DATAFLOW),
    )(*srcs, *lands, send_sems, recv_sems, after)
    return list(res[n:])


def _gather_start(name, shards, dep):
    n = len(shards)
    lands = [lax.empty((N_DEV,) + a.shape, a.dtype) for a in shards]

    def body(*refs):
        src_refs, land_refs = refs[:n], refs[n:2 * n]
        send_sems, recv_sems, token = refs[2 * n + 1], refs[2 * n + 2], refs[-1]
        x, y, c = _mesh_pos()
        peers = [(x, y, 1 - c), (1 - x, y, c), (x, 1 - y, c), (1 - x, 1 - y, c)]
        for a in range(n):
            for k, peer in enumerate(peers):
                pltpu.make_async_remote_copy(
                    src_ref=src_refs[a], dst_ref=land_refs[a].at[_dev_index((x, y, c))], send_sem=send_sems.at[4 * a + k],
                    recv_sem=recv_sems.at[4 * a + k], device_id=peer, device_id_type=MESH).start()
        token[...] = jnp.zeros_like(token)

    res = pl.pallas_call(
        body, name=name,
        out_shape=(pltpu.SemaphoreType.DMA((4 * n,)), pltpu.SemaphoreType.DMA((4 * n,)),
                   *[pltpu.HBM(a.shape, a.dtype) for a in shards], *[pltpu.HBM(a.shape, a.dtype) for a in lands],
                   jax.ShapeDtypeStruct((SUBLANES, 128), F32)),
        in_specs=[HBM_SPEC] * (2 * n) + [ANY_SPEC],
        out_specs=(SEM_SPEC, SEM_SPEC, *([HBM_SPEC] * (2 * n)), VMEM_SPEC),
        input_output_aliases={i: 2 + i for i in range(2 * n)},
        compiler_params=pltpu.CompilerParams(has_side_effects=DATAFLOW),
    )(*[pltpu.with_memory_space_constraint(a, pltpu.HBM) for a in shards],
      *[pltpu.with_memory_space_constraint(a, pltpu.HBM) for a in lands], dep)
    return res[0], res[1], list(res[2:2 + n]), list(res[2 + n:2 + 2 * n]), res[-1]


def _gather_forward(name, send_sems, recv_sems, lands, which, after):
    n = len(which)

    def body(*refs):
        land_refs = refs[:n]
        send_sems, recv_sems = refs[n], refs[n + 1]
        fsend, frecv, token = refs[n + 3], refs[n + 4], refs[-1]
        x, y, c = _mesh_pos()
        chips = [(1 - x, y), (x, 1 - y), (1 - x, 1 - y)]
        for i, a in enumerate(which):
            for j, chip in enumerate(chips):
                block = land_refs[i].at[_dev_index((*chip, c))]
                pltpu.make_async_remote_copy(
                    src_ref=block, dst_ref=block, send_sem=send_sems.at[4 * a + 1 + j], recv_sem=recv_sems.at[4 * a + 1 + j],
                    device_id=(*chip, c), device_id_type=MESH).wait_recv()
                pltpu.make_async_remote_copy(
                    src_ref=block, dst_ref=block, send_sem=fsend.at[3 * i + j], recv_sem=frecv.at[3 * i + j],
                    device_id=(x, y, 1 - c), device_id_type=MESH).start()
        token[...] = jnp.zeros_like(token)

    res = pl.pallas_call(
        body, name=name,
        out_shape=(pltpu.SemaphoreType.DMA((3 * n,)), pltpu.SemaphoreType.DMA((3 * n,)),
                   *[pltpu.HBM(a.shape, a.dtype) for a in lands], jax.ShapeDtypeStruct((SUBLANES, 128), F32)),
        in_specs=[HBM_SPEC] * n + [SEM_SPEC, SEM_SPEC, ANY_SPEC],
        out_specs=(SEM_SPEC, SEM_SPEC, *([HBM_SPEC] * n), VMEM_SPEC),
        input_output_aliases={i: 2 + i for i in range(n)},
        compiler_params=pltpu.CompilerParams(has_side_effects=DATAFLOW),
    )(*lands, send_sems, recv_sems, after)
    return res[0], res[1], list(res[2:2 + n]), res[-1]


def _gather_wait(name, send_sems, recv_sems, fsend, frecv, srcs, lands, which, after):
    n = len(which)

    def body(*refs):
        land_refs = refs[n:2 * n]
        send_sems, recv_sems, fsend, frecv = refs[2 * n:2 * n + 4]
        x, y, c = _mesh_pos()
        sib = (x, y, 1 - c)
        chips = [(1 - x, y), (x, 1 - y), (1 - x, 1 - y)]
        for i, a in enumerate(which):
            def cp(slot, ssem, rsem):
                block = land_refs[i].at[_dev_index(slot)]
                return pltpu.make_async_remote_copy(src_ref=block, dst_ref=block, send_sem=ssem, recv_sem=rsem,
                                                    device_id=sib, device_id_type=MESH)
            cp(sib, send_sems.at[4 * a], recv_sems.at[4 * a]).wait_recv()
            for j, chip in enumerate(chips):
                cp((*chip, 1 - c), fsend.at[3 * i + j], frecv.at[3 * i + j]).wait_recv()
            for k in range(4):
                cp(sib, send_sems.at[4 * a + k], recv_sems.at[4 * a + k]).wait_send()
            for j in range(3):
                cp(sib, fsend.at[3 * i + j], frecv.at[3 * i + j]).wait_send()

    res = pl.pallas_call(
        body, name=name,
        out_shape=[pltpu.HBM(a.shape, a.dtype) for a in srcs + lands],
        in_specs=[HBM_SPEC] * (2 * n) + [SEM_SPEC] * 4 + [ANY_SPEC],
        out_specs=[HBM_SPEC] * (2 * n),
        input_output_aliases={i: i for i in range(2 * n)},
        compiler_params=pltpu.CompilerParams(has_side_effects=DATAFLOW),
    )(*srcs, *lands, send_sems, recv_sems, fsend, frecv, after)
    return list(res[n:])


def _local_step(x, mem, target, rel_bias, g_mix, w_in_g, w_sc, g_a, g_c, g_xattn, g_mem, g_ffn, w_fc, b_fc, g_final,
                dep, forward_weights, late_weights, emit, emit_small):
    s = x.shape[0]
    buckets = _bucket_tables()
    bias = _bias_fwd(rel_bias, buckets)

    h1, qs, ks, vs, gb, gc, xi = _rms_proj(x, g_mix, w_in_g, dep)
    qs, ks, vs = ([a[0][None]] + list(a[1:]) for a in (qs, ks, vs))
    group1, group2 = ["w_out", "w_xq", "w_xk", "w_xv", "w_xo"], ["w_up", "w_down"]
    tok = forward_weights(group1, h1)
    branches = []
    for p, dil in enumerate(DILATIONS):
        o_p, lse_p = _swa_fwd(qs[p], ks[p], vs[p], bias[p], dil, tok)
        branches.append([o_p[0], lse_p[0]] if dil == 1 else [o_p, lse_p])
    lw = late_weights(group1, branches[-1][0])
    w_out, w_xq, w_xk, w_xv, w_xo = (lw[n] for n in group1)
    attn, lses, mixed, x1 = _mix_out(branches, gb, gc, xi, x, w_sc, g_a, g_c, w_out)
    tok = forward_weights(group2, x1)
    mem_n, mk, mv = _mem_kv(mem, g_mem, w_xk, w_xv)
    h2, xq, xo, x2 = _xattn_fwd(x1, g_xattn, w_xq, mk, mv, w_xo, tok)
    lw = late_weights(group2, x2)
    w_up_g, w_down_g = lw["w_up"], lw["w_down"]
    h3, conv, act, dx3, loss_acc, dg_final = _ffn_fwd(x2, g_ffn, w_up_g, w_fc, b_fc, w_down_g, g_final, target)

    gw_down = _dw(act, dx3, dep, "dw_down", a_chunked=True)
    dup, dx2, dg_ffn, dw_fc, db_fc = _ffn_bwd(dx3, h3, conv, x2, g_ffn, w_up_g, w_fc, w_down_g)
    gw_up = _dw(h3, dup, dep, "dw_up", b_chunked=True)
    tok = emit(dict(w_down=gw_down, w_up=gw_up))
    dxq, dx1, dmk, dmv, dg_xattn = _xattn_bwd(dx2, xo, xq, mk, mv, w_xo, w_xq, x1, g_xattn, tok)
    gw_xo = _dw(xo, dx2, tok, "dw_xo")[0]
    gw_xq = _dw(h2, dxq, tok, "dw_xq")[0]
    gw_xk, gw_xv, dg_mem = _mem_kv_bwd(dmk, dmv, mem_n, mem, w_xk, w_xv)
    tok = emit(dict(w_xo=gw_xo, w_xq=gw_xq, w_xk=gw_xk, w_xv=gw_xv))
    dattns, dds, dgb, dcv, dg_a, dg_c, dw_sc = _mix_out_bwd(dx1, w_out, attn, gb, gc, xi, w_sc, g_a, g_c, tok)
    first = lambda a: [a[0][None]] + list(a[1:])
    dattns, dds, lses = first(dattns), first(dds), first(lses)
    gw_out = _dw(mixed, dx1, tok, "dw_out")[0]
    tok = emit(dict(w_out=gw_out))
    dqs, dks, dvs, dbias = [], [], [], []
    for p, dil in enumerate(DILATIONS):
        dq_p, dk_p, dv_p, db_p = _swa_bwd(qs[p], ks[p], vs[p], dattns[p], lses[p], dds[p], bias[p], dil, tok)
        dqs.append(dq_p[0] if dil == 1 else dq_p)
        dks.append(dk_p[0] if dil == 1 else dk_p)
        dvs.append(dv_p[0] if dil == 1 else dv_p)
        dbias.append(db_p)
    d_relb = _bias_bwd(jnp.stack(dbias), buckets)
    dproj, grad_x, dg_mix = _in_proj_bwd(dqs, dks, dvs, dgb, dcv, gc, xi, w_sc, w_in_g, x, g_mix, dx1)
    pad = lambda a: jnp.pad(a, ((0, 0), (0, D_MODEL - a.shape[1])))
    small = jnp.concatenate([
        d_relb, dg_mix, dg_xattn, dg_mem, dg_ffn, dg_final, jnp.concatenate([dg_a, dg_c], axis=1),
        pad(dw_sc), pad(db_fc), pad(dw_fc.reshape(3 * N_DEV, UP_CHUNK)), pad(loss_acc)], axis=0)
    tok = emit_small(small)
    gw_in = _dw(h1, dproj, tok, "dw_in", n_chunks=N_DEV, chunk_cols=IN_CHUNK)
    emit(dict(w_in=gw_in))
    return grad_x


def kernel(x, mem, rel_bias, g_mix, w_in, w_short_conv, g_attn_out, g_conv_out, w_out, g_xattn, g_mem, w_xq, w_xk, w_xv, w_xo, g_ffn, w_up, w_ffn_conv, b_ffn_conv, w_down, g_final, loss_target, m_rel_bias, m_g_mix, m_w_in, m_w_short_conv, m_g_attn_out, m_g_conv_out, m_w_out, m_g_xattn, m_g_mem, m_w_xq, m_w_xk, m_w_xv, m_w_xo, m_g_ffn, m_w_up, m_w_ffn_conv, m_b_ffn_conv, m_w_down, m_g_final, v_rel_bias, v_g_mix, v_w_in, v_w_short_conv, v_g_attn_out, v_g_conv_out, v_w_out, v_g_xattn, v_g_mem, v_w_xq, v_w_xk, v_w_xv, v_w_xo, v_g_ffn, v_w_up, v_w_ffn_conv, v_b_ffn_conv, v_w_down, v_g_final):
    me = _dev_index(_mesh_pos())
    me_arr = me.reshape(1).astype(jnp.int32)

    big_names = ["w_in", "w_out", "w_xq", "w_xk", "w_xv", "w_xo", "w_up", "w_down"]
    late_names = big_names[1:]
    big_w = dict(w_in=w_in[0], w_out=w_out[0], w_xq=w_xq[0], w_xk=w_xk[0], w_xv=w_xv[0], w_xo=w_xo[0],
                 w_up=w_up[0], w_down=w_down[0])
    big_m = dict(w_in=m_w_in[0], w_out=m_w_out[0], w_xq=m_w_xq[0], w_xk=m_w_xk[0], w_xv=m_w_xv[0], w_xo=m_w_xo[0],
                 w_up=m_w_up[0], w_down=m_w_down[0])
    big_v = dict(w_in=v_w_in[0], w_out=v_w_out[0], w_xq=v_w_xq[0], w_xk=v_w_xk[0], w_xv=v_w_xv[0], w_xo=v_w_xo[0],
                 w_up=v_w_up[0], w_down=v_w_down[0])
    shard_shape = {n: big_w[n].shape for n in big_names}

    w_in_g, w_sc_g, w_fc_full = _all_gather([big_w["w_in"].astype(BF16), w_short_conv[0], w_ffn_conv[0]])
    w_sc_full = w_sc_g.transpose(1, 0, 2).reshape(3, CONV_W)
    late_shards = [big_w[n].astype(BF16) for n in late_names]
    ag_send, ag_recv, ag_srcs, ag_lands, ag_token = _gather_start("gather_weights_start", late_shards, w_in_g)
    forwarded = {}

    def forward_weights(names, after):
        which = [late_names.index(n) for n in names]
        fsend, frecv, lands, token = _gather_forward("gather_" + "_".join(names) + "_forward", ag_send, ag_recv,
                                                     [ag_lands[a] for a in which], which, after)
        forwarded[tuple(names)] = (fsend, frecv, lands)
        return token

    def late_weights(names, after):
        which = [late_names.index(n) for n in names]
        fsend, frecv, lands = forwarded[tuple(names)]
        lands = _gather_wait("gather_" + "_".join(names) + "_wait", ag_send, ag_recv, fsend, frecv,
                             [ag_srcs[a] for a in which], lands, which, after)
        out = {}
        for n, a, land in zip(names, which, lands):
            full = lax.dynamic_update_index_in_dim(land, late_shards[a], me, 0)
            if n == "w_up":
                out[n] = full
            elif n == "w_down":
                out[n] = full.reshape(N_DEV // 2, UP_CHUNK, D_MODEL)
            else:
                out[n] = full.reshape(D_MODEL, D_MODEL)
        return out

    sent = []

    def emit(grads):
        names = list(grads)
        blocks = [grads[n].reshape((N_DEV,) + shard_shape[n]) for n in names]
        own = [lax.dynamic_index_in_dim(b, me, 0, keepdims=False) for b in blocks]
        started = _exchange_start("scatter_" + "_".join(names) + "_start", blocks, [False] * len(names), me_arr)
        sent.append((names, own, started))
        return started[-1]

    def emit_small(small):
        sent_small.append((small, _exchange_start("gather_small_start", [small], [True], me_arr)))
        return sent_small[0][1][-1]

    sent_small = []
    grad_x = _local_step(
        x[0], mem[0], loss_target[0], rel_bias, g_mix, w_in_g, w_sc_full, g_attn_out, g_conv_out, g_xattn, g_mem,
        g_ffn, w_fc_full, b_ffn_conv.reshape(N_DEV, 1, UP_CHUNK), g_final.reshape(1, D_MODEL), ag_token,
        forward_weights, late_weights, emit, emit_small)

    small_g, small_started = sent_small[0]
    after = sent[-1][2][-1]
    small_parts = _exchange_wait("gather_small_wait", small_started, [True], after)[0]
    big_out = {}
    after = small_parts
    for names, own, started in sent:
        lands = _exchange_wait("scatter_" + "_".join(names) + "_wait", started, [False] * len(names), after)
        for n, own_n, land in zip(names, own, lands):
            res = _adamw_big("adamw_" + n, big_w[n], own_n, land, big_m[n], big_v[n], me_arr)
            big_out[n] = [r[None] for r in res]
            after = res[0]

    as_rows = lambda a: a.reshape(N_DEV, UP_CHUNK)
    row1 = lambda a: a.reshape(1, D_MODEL)
    small_names = ["rel_bias", "g_mix", "g_attn_out", "g_conv_out", "g_xattn", "g_mem", "g_ffn", "b_ffn_conv", "g_final"]
    wmv = [
        (rel_bias, m_rel_bias, v_rel_bias), (g_mix, m_g_mix, v_g_mix), (g_attn_out, m_g_attn_out, v_g_attn_out),
        (g_conv_out, m_g_conv_out, v_g_conv_out), (g_xattn, m_g_xattn, v_g_xattn), (g_mem, m_g_mem, v_g_mem),
        (g_ffn, m_g_ffn, v_g_ffn), (as_rows(b_ffn_conv), as_rows(m_b_ffn_conv), as_rows(v_b_ffn_conv)),
        (row1(g_final), row1(m_g_final), row1(v_g_final))]
    g_packed, small_res = _adamw_small(small_g, small_parts, wmv, me_arr)
    small_out = dict(zip(small_names, small_res))
    loss = g_packed[ROW_LOSS, 0]
    small_out["b_ffn_conv"] = [a.reshape(1, 2 * D_FF) for a in small_out["b_ffn_conv"]]
    small_out["g_final"] = [a.reshape(D_MODEL) for a in small_out["g_final"]]

    g_wsc = lax.dynamic_slice(g_packed[ROW_WSC:ROW_WSC + 3, 0:CONV_W], (0, me * HEAD_DIM), (3, HEAD_DIM))
    g_wfc = lax.dynamic_slice(g_packed[ROW_WFC:ROW_WFC + 3 * N_DEV, 0:UP_CHUNK].reshape(3, N_DEV, UP_CHUNK),
                              (0, me, 0), (3, 1, UP_CHUNK)).reshape(3, UP_CHUNK)
    shard_res = _adamw_shards([(w_short_conv[0], g_wsc, m_w_short_conv[0], v_w_short_conv[0]),
                               (w_ffn_conv[0], g_wfc, m_w_ffn_conv[0], v_w_ffn_conv[0])])
    small_out["w_short_conv"] = [g_wsc[None]] + [a[None] for a in shard_res[0]]
    small_out["w_ffn_conv"] = [g_wfc[None]] + [a[None] for a in shard_res[1]]

    order = ["rel_bias", "g_mix", "w_in", "w_short_conv", "g_attn_out", "g_conv_out", "w_out", "g_xattn", "g_mem",
             "w_xq", "w_xk", "w_xv", "w_xo", "g_ffn", "w_up", "w_ffn_conv", "b_ffn_conv", "w_down", "g_final"]
    allp = {**big_out, **small_out}
    outs = [loss, grad_x[None]]
    for kind in range(4):
        outs += [allp[n][kind] for n in order]
    return tuple(outs)
```

```python
import functools
import math

import numpy as np
import jax
import jax.numpy as jnp
from jax import lax
from jax.experimental import pallas as pl
from jax.experimental.pallas import tpu as pltpu

F32 = jnp.float32
BF16 = jnp.bfloat16
MESH = pl.DeviceIdType.MESH

N_DEV = 8
D_MODEL = 1024
ATTN_W = 512
CONV_W = 512
N_HEADS = 8
HEAD_DIM = 64
WIN = 128
DILATIONS = (1, 4, 16)
N_BUCKETS = 32
BUCKET_MAX_EXACT = 16
BUCKET_MAX_DISTANCE = 2048
N_MEM_HEADS = 4
MEM_HEAD_DIM = 256
D_FF = 2816
IN_COLS = 3072
IN_CHUNK = IN_COLS // N_DEV
UP_CHUNK = 2 * D_FF // N_DEV
EPS = 1e-6

ADAM_LR = 0.001
ADAM_B1 = 0.9
ADAM_B2 = 0.999
ADAM_EPS = 1e-08
ADAM_WD = 0.01
ADAM_STEP = 10

SUBLANES = 8
LANES = 128
HALO = 16
TM = 512
TM_FFN = 256
TS_DW = 4096
VMEM_LIMIT = 56 * 1024 * 1024

ROW_RELB, ROW_GMIX, ROW_GXATTN, ROW_GMEM, ROW_GFFN, ROW_GFINAL, ROW_GAC = 0, 8, 16, 24, 32, 40, 48
ROW_WSC, ROW_BFC, ROW_WFC, ROW_LOSS, SMALL_ROWS = 56, 64, 72, 96, 104


def _cparams(n_grid):
    return pltpu.CompilerParams(dimension_semantics=("arbitrary",) * n_grid, vmem_limit_bytes=VMEM_LIMIT)


def _full(shape):
    nd = len(shape)
    return pl.BlockSpec(tuple(shape), lambda *_: (0,) * nd)


def _resident(shape):
    nd = len(shape)
    return pl.BlockSpec(tuple(shape), lambda *_: (0,) * nd, pipeline_mode=pl.Buffered(1))


ANY_SPEC = pl.BlockSpec(memory_space=pl.ANY)
HBM_SPEC = pl.BlockSpec(memory_space=pltpu.HBM)
SEM_SPEC = pl.BlockSpec(memory_space=pltpu.SEMAPHORE)
VMEM_SPEC = pl.BlockSpec(memory_space=pltpu.VMEM)
SMEM_SPEC = pl.BlockSpec(memory_space=pltpu.SMEM)
DATAFLOW = pltpu.SideEffectType.DATAFLOW_SIDE_EFFECTING


def _rms(x):
    r = lax.rsqrt(jnp.mean(x * x, axis=-1, keepdims=True) + EPS)
    return x * r, r


def _rms_bwd(xh, r, g, dy):
    dxh = dy * g
    return r * (dxh - xh * jnp.mean(dxh * xh, axis=-1, keepdims=True))


def _shift_down(u, halo, k):
    ru = pltpu.roll(u, k, 0)
    rh = pltpu.roll(halo, k, 0)
    row = lax.broadcasted_iota(jnp.int32, rh.shape, 0)
    head = jnp.where(row < k, rh, ru[0:SUBLANES])
    return jnp.concatenate([head, ru[SUBLANES:]], axis=0)


def _shift_up(u, halo, k):
    tm = u.shape[0]
    ru = pltpu.roll(u, tm - k, 0)
    rh = pltpu.roll(halo, SUBLANES - k, 0)
    row = lax.broadcasted_iota(jnp.int32, rh.shape, 0)
    tail = jnp.where(row >= SUBLANES - k, rh, ru[tm - SUBLANES:])
    return jnp.concatenate([ru[:tm - SUBLANES], tail], axis=0)


def _causal_conv3(u, halo, w_ref):
    return (_shift_down(u, halo, 2) * w_ref[0:1, :] + _shift_down(u, halo, 1) * w_ref[1:2, :]) + u * w_ref[2:3, :]


def _dot(a, b):
    return jnp.dot(a, b, preferred_element_type=F32)


def _dot_nt(a, b):
    return lax.dot_general(a, b, (((1,), (1,)), ((), ())), preferred_element_type=F32)


def _dot_tn(a, b):
    return lax.dot_general(a, b, (((0,), (0,)), ((), ())), preferred_element_type=F32)


def _sigmoid(x):
    return 1.0 / (1.0 + jnp.exp(-x))


def _bucket_tables():
    qi = np.arange(WIN)[:, None]
    kj = np.arange(2 * WIN)[None, :]
    steps = np.clip(qi + WIN - kj, 0, WIN)
    out = []
    for d in DILATIONS:
        dist = steps * d
        dd = np.maximum(dist, 1).astype(np.float32)
        large = BUCKET_MAX_EXACT + (
            np.log(dd / np.float32(BUCKET_MAX_EXACT)) / np.float32(math.log(BUCKET_MAX_DISTANCE / BUCKET_MAX_EXACT))
            * np.float32(N_BUCKETS - BUCKET_MAX_EXACT)).astype(np.int32)
        large = np.minimum(large, N_BUCKETS - 1)
        out.append(np.where(dist < BUCKET_MAX_EXACT, dist, large).astype(np.int32))
    return jnp.asarray(np.stack(out))


def _bias_fwd(rel_bias, buckets):
    def body(rb_ref, bk_ref, o_ref):
        for p in range(3):
            bk = bk_ref[p]
            for h in range(N_HEADS):
                acc = jnp.zeros((WIN, 2 * WIN), F32)
                for b in range(N_BUCKETS):
                    acc = jnp.where(bk == b, rb_ref[h, b], acc)
                o_ref[p, h] = acc

    return pl.pallas_call(
        body, name="bias_fwd",
        out_shape=jax.ShapeDtypeStruct((3, N_HEADS, WIN, 2 * WIN), F32),
        in_specs=[pl.BlockSpec(memory_space=pltpu.SMEM), pl.BlockSpec(memory_space=pltpu.VMEM)],
        out_specs=pl.BlockSpec(memory_space=pltpu.VMEM),
    )(rel_bias, buckets)


def _bias_bwd(dbias, buckets):
    def body(db_ref, bk_ref, o_ref):
        lane = lax.broadcasted_iota(jnp.int32, (1, D_MODEL), 1)
        rows = []
        for h in range(N_HEADS):
            row = jnp.zeros((1, D_MODEL), F32)
            for b in range(N_BUCKETS):
                tot = jnp.zeros((1, 1), F32)
                for p in range(3):
                    sel = jnp.where(bk_ref[p] == b, db_ref[p, h], 0.0)
                    tot = tot + jnp.sum(jnp.sum(sel, axis=0, keepdims=True), axis=1, keepdims=True)
                row = jnp.where(lane == b, tot, row)
            rows.append(row)
        o_ref[...] = jnp.concatenate(rows, axis=0)

    return pl.pallas_call(
        body, name="bias_bwd",
        out_shape=jax.ShapeDtypeStruct((N_HEADS, D_MODEL), F32),
        in_specs=[pl.BlockSpec(memory_space=pltpu.VMEM), pl.BlockSpec(memory_space=pltpu.VMEM)],
        out_specs=pl.BlockSpec(memory_space=pltpu.VMEM),
    )(dbias, buckets)


def _spread(val, scr_ref, out_refs, dtype):
    out_refs[0][...] = val.astype(dtype)
    n_blk = val.shape[1] // LANES
    for c in range(n_blk):
        scr_ref[c] = val[:, c * LANES:(c + 1) * LANES]
    for o_ref, d in zip(out_refs[1:], DILATIONS[1:]):
        for r in range(d):
            for c in range(n_blk):
                o_ref[r, :, c * LANES:(c + 1) * LANES] = scr_ref.at[c][pl.ds(r, TM // d, stride=d), :].astype(dtype)


def _gather_classes(blk_ref, scr_ref, d):
    n_blk = blk_ref.shape[2] // LANES
    for r in range(d):
        for c in range(n_blk):
            scr_ref.at[c][pl.ds(r, TM // d, stride=d), :] = blk_ref[r, :, c * LANES:(c + 1) * LANES].astype(F32)
    return jnp.concatenate([scr_ref[c] for c in range(n_blk)], axis=1)


def _class_specs(cols):
    return [pl.BlockSpec((TM, cols), lambda i: (i, 0))] + [
        pl.BlockSpec((d, TM // d, cols), lambda i: (0, i, 0)) for d in DILATIONS[1:]]


def _class_shapes(s, cols, dtype):
    return [jax.ShapeDtypeStruct((s, cols), dtype)] + [
        jax.ShapeDtypeStruct((d, s // d, cols), dtype) for d in DILATIONS[1:]]


def _rms_proj(x, g_mix, w_in_g, dep):
    s = x.shape[0]

    def body(x_ref, g_ref, w_ref, dep_ref, h_ref, q1, q4, q16, k1, k4, k16, v1, v4, v16, gb_ref, gc_ref, xi_ref, scr):
        xh, _ = _rms(x_ref[...])
        h = (xh * g_ref[...]).astype(BF16)
        h_ref[...] = h
        proj = jnp.concatenate([_dot(h, w_ref[j]) for j in range(N_DEV)], axis=1)
        _spread(proj[:, 0:512] * (HEAD_DIM ** -0.5), scr, (q1, q4, q16), BF16)
        _spread(proj[:, 512:1024], scr, (k1, k4, k16), BF16)
        _spread(proj[:, 1024:1536], scr, (v1, v4, v16), BF16)
        gb_ref[...] = proj[:, 1536:2048]
        gc_ref[...] = proj[:, 2048:2560]
        xi_ref[...] = proj[:, 2560:3072]

    row = lambda n: pl.BlockSpec((TM, n), lambda i: (i, 0))
    res = pl.pallas_call(
        body, name="rms_proj", grid=(s // TM,),
        out_shape=[jax.ShapeDtypeStruct((s, D_MODEL), BF16)] + _class_shapes(s, 512, BF16) * 3
        + [jax.ShapeDtypeStruct((s, 512), F32)] * 3,
        in_specs=[row(D_MODEL), _full(g_mix.shape), _full(w_in_g.shape), ANY_SPEC],
        out_specs=[row(D_MODEL)] + _class_specs(512) * 3 + [row(512)] * 3,
        scratch_shapes=[pltpu.VMEM((512 // LANES, TM, LANES), F32)],
        compiler_params=_cparams(1),
    )(x, g_mix, w_in_g, dep)
    return res[0], res[1:4], res[4:7], res[7:10], res[10], res[11], res[12]


def _pair_split(x2):
    lane = lax.broadcasted_iota(jnp.int32, x2.shape, 1)
    zero = jnp.zeros_like(x2)
    return jnp.where(lane < HEAD_DIM, x2, zero), jnp.where(lane >= HEAD_DIM, x2, zero)


def _pair_join(even, odd):
    lane = lax.broadcasted_iota(jnp.int32, (even.shape[0], LANES), 1)
    return jnp.where(lane < HEAD_DIM, even, odd)


def _band_mask(first):
    qi = lax.broadcasted_iota(jnp.int32, (WIN, 2 * WIN), 0)
    kj = lax.broadcasted_iota(jnp.int32, (WIN, 2 * WIN), 1)
    steps = qi + WIN - kj
    return (steps >= 0) & (steps <= WIN) & (kj >= jnp.where(first, WIN, 0))


def _swa_fwd(qc, kc, vc, bias, dil, dep):
    nb = qc.shape[1] // (2 * WIN)

    def body(q_ref, kp_ref, kc_ref, vp_ref, vc_ref, b_ref, dep_ref, o_ref, lse_ref, s_scr, p_scr):
        b = pl.program_id(1)
        pairs = [slice(a * LANES, (a + 1) * LANES) for a in range(N_HEADS // 2)]
        for sub in range(2):
            rows = slice(sub * WIN, (sub + 1) * WIN)

            def keys(prev_ref, cur_ref, sl):
                if sub == 0:
                    return jnp.concatenate([prev_ref[0, :, sl], cur_ref[0, 0:WIN, sl]], axis=0)
                return cur_ref[0, :, sl]

            for a, sl in enumerate(pairs):
                k2 = keys(kp_ref, kc_ref, sl)
                for e, qh in enumerate(_pair_split(q_ref[0, rows, sl])):
                    s_scr[sub, 2 * a + e] = _dot_nt(qh, k2)
            first = (b == 0) if sub == 0 else False
            lg = jnp.where(_band_mask(first), s_scr[sub] + b_ref[...], -jnp.inf)
            m = jnp.max(lg, axis=-1, keepdims=True)
            p = jnp.exp(lg - m)
            den = jnp.sum(p, axis=-1, keepdims=True)
            p_scr[sub] = p.astype(BF16)
            lse = m + jnp.log(den)
            for a, sl in enumerate(pairs):
                v_even, v_odd = _pair_split(keys(vp_ref, vc_ref, sl))
                o2 = _dot(p_scr[sub, 2 * a], v_even) + _dot(p_scr[sub, 2 * a + 1], v_odd)
                o_ref[0, rows, sl] = o2 / _pair_join(den[2 * a], den[2 * a + 1])
                lse_ref[0, rows, sl] = _pair_join(lse[2 * a], lse[2 * a + 1])

    cur = pl.BlockSpec((1, 2 * WIN, 512), lambda r, b: (r, b, 0))
    prev = pl.BlockSpec((1, WIN, 512), lambda r, b: (r, jnp.maximum(2 * b - 1, 0), 0))
    return pl.pallas_call(
        body, name=f"swa_fwd_d{dil}", grid=(dil, nb),
        out_shape=[jax.ShapeDtypeStruct(qc.shape, F32)] * 2,
        in_specs=[cur, prev, cur, prev, cur, _full(bias.shape), ANY_SPEC],
        out_specs=[cur] * 2,
        scratch_shapes=[pltpu.VMEM((2, N_HEADS, WIN, 2 * WIN), F32), pltpu.VMEM((2, N_HEADS, WIN, 2 * WIN), BF16)],
        compiler_params=_cparams(2),
    )(qc, kc, kc, vc, vc, bias, dep)


def _mix_out(branches, gb, gc, xi, x, w_sc, g_a, g_c, w_out):
    s = x.shape[0]
    tb = TM // SUBLANES

    def body(o1, l1, o4, l4, o16, l16, gb_ref, gc_ref, xi_ref, gch_ref, xih_ref, x_ref, wsc_ref,
             ga_ref, gcv_ref, wout_ref, attn_ref, lse1, lse4, lse16, mixed_ref, x1_ref, scr_a, scr_b, scr_c, scr_d):
        i = pl.program_id(0)
        la, lb, lc = l1[...], _gather_classes(l4, scr_a, 4), _gather_classes(l16, scr_b, 16)
        m_all = jnp.maximum(jnp.maximum(la, lb), lc)
        ea, eb, ec = jnp.exp(la - m_all), jnp.exp(lb - m_all), jnp.exp(lc - m_all)
        den = (ea + eb) + ec
        num = (ea * o1[...] + eb * _gather_classes(o4, scr_c, 4)) + ec * _gather_classes(o16, scr_d, 16)
        attn = num / den
        attn_ref[...] = attn
        _spread(m_all + jnp.log(den), scr_a, (lse1, lse4, lse16), F32)
        xa, _ = _rms(attn)
        u = gc_ref[...] * xi_ref[...]
        uh = jnp.where(i > 0, gch_ref[...] * xih_ref[...], 0.0)
        conv = gb_ref[...] * _causal_conv3(u, uh, wsc_ref)
        xc, _ = _rms(conv)
        mixed = jnp.concatenate([xa * ga_ref[...], xc * gcv_ref[...]], axis=1).astype(BF16)
        mixed_ref[...] = mixed
        x1_ref[...] = x_ref[...] + _dot(mixed, wout_ref[...])

    row = lambda n: pl.BlockSpec((TM, n), lambda i: (i, 0))
    halo = pl.BlockSpec((SUBLANES, 512), lambda i: (jnp.maximum(i * tb - 1, 0), 0))
    cs = _class_specs(512)
    flat = [a for br in branches for a in br]
    res = pl.pallas_call(
        body, name="mix_out", grid=(s // TM,),
        out_shape=[jax.ShapeDtypeStruct((s, 512), F32)] + _class_shapes(s, 512, F32)
        + [jax.ShapeDtypeStruct((s, D_MODEL), BF16), jax.ShapeDtypeStruct((s, D_MODEL), F32)],
        in_specs=[cs[0], cs[0], cs[1], cs[1], cs[2], cs[2], row(512), row(512), row(512), halo, halo,
                  row(D_MODEL), _full(w_sc.shape), _full(g_a.shape), _full(g_c.shape), _full(w_out.shape)],
        out_specs=[row(512)] + cs + [row(D_MODEL), row(D_MODEL)],
        scratch_shapes=[pltpu.VMEM((512 // LANES, TM, LANES), F32)] * 4,
        compiler_params=_cparams(1),
    )(*flat, gb, gc, xi, gc, xi, x, w_sc, g_a, g_c, w_out)
    return res[0], res[1:4], res[4], res[5]


def _mem_kv(mem, g_mem, w_xk, w_xv):
    def body(mem_ref, g_ref, wk_ref, wv_ref, mn_ref, k_ref, v_ref):
        xh, _ = _rms(mem_ref[...])
        mn = (xh * g_ref[...]).astype(BF16)
        mn_ref[...] = mn
        k_ref[...] = _dot(mn, wk_ref[...]).astype(BF16)
        v_ref[...] = _dot(mn, wv_ref[...]).astype(BF16)

    vm = pl.BlockSpec(memory_space=pltpu.VMEM)
    return pl.pallas_call(
        body, name="mem_kv",
        out_shape=[jax.ShapeDtypeStruct(mem.shape, BF16)] * 3,
        in_specs=[vm] * 4, out_specs=[vm] * 3,
        compiler_params=pltpu.CompilerParams(vmem_limit_bytes=VMEM_LIMIT),
    )(mem, g_mem, w_xk, w_xv)


def _xattn_fwd(x1, g, w_xq, k, v, w_xo, dep):
    s = x1.shape[0]

    def body(x1_ref, g_ref, wq_ref, k_ref, v_ref, wo_ref, dep_ref, h2_ref, q_ref, o_ref, x2_ref):
        x1v = x1_ref[...]
        xh, _ = _rms(x1v)
        h2 = (xh * g_ref[...]).astype(BF16)
        h2_ref[...] = h2
        qb = _dot(h2, wq_ref[...]).astype(BF16)
        q_ref[...] = qb
        outs = []
        for h in range(N_MEM_HEADS):
            sl = slice(h * MEM_HEAD_DIM, (h + 1) * MEM_HEAD_DIM)
            lg = _dot_nt(qb[:, sl], k_ref[:, sl]) * (MEM_HEAD_DIM ** -0.5)
            p = jnp.exp(lg - jnp.max(lg, axis=-1, keepdims=True))
            p = p / jnp.sum(p, axis=-1, keepdims=True)
            outs.append(_dot(p.astype(BF16), v_ref[:, sl]))
        o = jnp.concatenate(outs, axis=1).astype(BF16)
        o_ref[...] = o
        x2_ref[...] = x1v + _dot(o, wo_ref[...])

    row = pl.BlockSpec((TM, D_MODEL), lambda i: (i, 0))
    return pl.pallas_call(
        body, name="xattn_fwd", grid=(s // TM,),
        out_shape=[jax.ShapeDtypeStruct((s, D_MODEL), BF16)] * 3 + [jax.ShapeDtypeStruct((s, D_MODEL), F32)],
        in_specs=[row, _full(g.shape), _full(w_xq.shape), _full(k.shape), _full(v.shape), _full(w_xo.shape), ANY_SPEC],
        out_specs=[row] * 4,
        compiler_params=_cparams(1),
    )(x1, g, w_xq, k, v, w_xo, dep)


def _ffn_conv(h_ext, wup_ref, wfc_ref, bfc_ref, j):
    u = _dot_nt(h_ext, wup_ref[j])
    w = wfc_ref[j]
    c = ((pltpu.roll(u, 2, 0) * w[0:1, :] + pltpu.roll(u, 1, 0) * w[1:2, :]) + u * w[2:3, :]) + bfc_ref[j]
    return c[HALO:]


def _ffn_fwd(x2, g, w_up_g, w_fc, b_fc, w_down_g, g_final, target):
    s = x2.shape[0]
    tb = TM_FFN // HALO
    half = N_DEV // 2

    def body(x_ref, xp_ref, g_ref, wup_ref, wfc_ref, bfc_ref, wd_ref, gf_ref, t_ref, h_ref, c_ref, act_ref, dx3_ref,
             loss_ref, dgf_ref):
        i = pl.program_id(0)

        @pl.when(i == 0)
        def _():
            loss_ref[...] = jnp.zeros_like(loss_ref)
            dgf_ref[...] = jnp.zeros_like(dgf_ref)

        x2v = x_ref[...]
        gv = g_ref[...]
        h = (_rms(x2v)[0] * gv).astype(BF16)
        h_ref[...] = h
        hp = jnp.where(i > 0, _rms(xp_ref[...])[0] * gv, 0.0).astype(BF16)
        h_ext = jnp.concatenate([hp, h], axis=0)
        down = jnp.zeros((TM_FFN, D_MODEL), F32)
        for j in range(half):
            cg = _ffn_conv(h_ext, wup_ref, wfc_ref, bfc_ref, j)
            cv = _ffn_conv(h_ext, wup_ref, wfc_ref, bfc_ref, j + half)
            c_ref[j] = cg
            c_ref[j + half] = cv
            a = ((cg * _sigmoid(cg)) * cv).astype(BF16)
            act_ref[j] = a
            down = down + _dot(a, wd_ref[j])
        x3 = x2v + down
        xh, r = _rms(x3)
        gf = gf_ref[...]
        e = xh * gf - t_ref[...]
        loss_ref[...] += 0.5 * jnp.sum(jnp.sum(e * e, axis=1, keepdims=True), axis=0, keepdims=True) / D_MODEL
        dy = e * (1.0 / D_MODEL)
        dgf_ref[0:1, :] += jnp.sum(dy * xh, axis=0, keepdims=True)
        dx3_ref[...] = _rms_bwd(xh, r, gf, dy)

    row = pl.BlockSpec((TM_FFN, D_MODEL), lambda i: (i, 0))
    prev = pl.BlockSpec((HALO, D_MODEL), lambda i: (jnp.maximum(i * tb - 1, 0), 0))
    return pl.pallas_call(
        body, name="ffn_fwd", grid=(s // TM_FFN,),
        out_shape=[jax.ShapeDtypeStruct((s, D_MODEL), BF16), jax.ShapeDtypeStruct((N_DEV, s, UP_CHUNK), F32),
                   jax.ShapeDtypeStruct((half, s, UP_CHUNK), BF16),
                   jax.ShapeDtypeStruct((s, D_MODEL), F32), jax.ShapeDtypeStruct((SUBLANES, 128), F32),
                   jax.ShapeDtypeStruct((SUBLANES, D_MODEL), F32)],
        in_specs=[row, prev, _full(g.shape), _resident(w_up_g.shape), _full(w_fc.shape), _full(b_fc.shape),
                  _resident(w_down_g.shape), _full(g_final.shape), row],
        out_specs=[row, pl.BlockSpec((N_DEV, TM_FFN, UP_CHUNK), lambda i: (0, i, 0)),
                   pl.BlockSpec((half, TM_FFN, UP_CHUNK), lambda i: (0, i, 0)), row,
                   _full((SUBLANES, 128)), _full((SUBLANES, D_MODEL))],
        compiler_params=_cparams(1),
    )(x2, x2, g, w_up_g, w_fc, b_fc, w_down_g, g_final, target)


def _ffn_bwd(dx3, h3, conv, x2, g, w_up_g, w_fc, w_down_g):
    s = x2.shape[0]
    tb = TM_FFN // HALO
    last = s // HALO - 1
    n_tiles = s // TM_FFN
    half = N_DEV // 2
    n_ext = TM_FFN + HALO

    def body(dx_ref, dxn_ref, h_ref, c_ref, cn_ref, x2_ref, g_ref, wup_ref, wfc_ref, wd_ref,
             dup_ref, dx2_ref, dg_ref, dwfc_ref, dbfc_ref):
        i = pl.program_id(0)

        @pl.when(i == 0)
        def _():
            dg_ref[...] = jnp.zeros_like(dg_ref)
            dwfc_ref[...] = jnp.zeros_like(dwfc_ref)
            dbfc_ref[...] = jnp.zeros_like(dbfc_ref)

        dxv = dx_ref[...]
        dxn = jnp.where(i < n_tiles - 1, dxn_ref[...], 0.0)
        dx_ext = jnp.concatenate([dxv, dxn], axis=0).astype(BF16)
        h = h_ref[...]
        dh = jnp.zeros((TM_FFN, D_MODEL), F32)
        for j in range(half):
            cg = jnp.concatenate([c_ref[j], cn_ref[j]], axis=0)
            cv = jnp.concatenate([c_ref[j + half], cn_ref[j + half]], axis=0)
            dact = _dot_nt(dx_ext, wd_ref[j])
            sg = _sigmoid(cg)
            parts = ((j + half, dact * (cg * sg)), (j, (dact * cv) * (sg * (1.0 + cg * (1.0 - sg)))))
            for jj, dc in parts:
                u = _dot_nt(h, wup_ref[jj])
                dc0, dc1, dc2 = dc[:TM_FFN], pltpu.roll(dc, n_ext - 1, 0)[:TM_FFN], pltpu.roll(dc, n_ext - 2, 0)[:TM_FFN]
                dbfc_ref[jj:jj + 1, :] += jnp.sum(dc0, axis=0, keepdims=True)
                dwfc_ref[0, jj:jj + 1, :] += jnp.sum(dc2 * u, axis=0, keepdims=True)
                dwfc_ref[1, jj:jj + 1, :] += jnp.sum(dc1 * u, axis=0, keepdims=True)
                dwfc_ref[2, jj:jj + 1, :] += jnp.sum(dc0 * u, axis=0, keepdims=True)
                w = wfc_ref[jj]
                du = ((dc0 * w[2:3, :] + dc1 * w[1:2, :]) + dc2 * w[0:1, :]).astype(BF16)
                dup_ref[jj] = du
                dh = dh + _dot(du, wup_ref[jj])
        xh, r = _rms(x2_ref[...])
        dg_ref[0:1, :] += jnp.sum(dh * xh, axis=0, keepdims=True)
        dx2_ref[...] = dxv + _rms_bwd(xh, r, g_ref[...], dh)

    row = pl.BlockSpec((TM_FFN, D_MODEL), lambda i: (i, 0))
    nxt = pl.BlockSpec((HALO, D_MODEL), lambda i: (jnp.minimum((i + 1) * tb, last), 0))
    cur_c = pl.BlockSpec((N_DEV, TM_FFN, UP_CHUNK), lambda i: (0, i, 0))
    nxt_c = pl.BlockSpec((N_DEV, HALO, UP_CHUNK), lambda i: (0, jnp.minimum((i + 1) * tb, last), 0))
    return pl.pallas_call(
        body, name="ffn_bwd", grid=(n_tiles,),
        out_shape=[jax.ShapeDtypeStruct((N_DEV, s, UP_CHUNK), BF16), jax.ShapeDtypeStruct((s, D_MODEL), F32),
                   jax.ShapeDtypeStruct((SUBLANES, D_MODEL), F32), jax.ShapeDtypeStruct((3, N_DEV, UP_CHUNK), F32),
                   jax.ShapeDtypeStruct((N_DEV, UP_CHUNK), F32)],
        in_specs=[row, nxt, row, cur_c, nxt_c, row, _full(g.shape), _resident(w_up_g.shape), _full(w_fc.shape),
                  _resident(w_down_g.shape)],
        out_specs=[cur_c, row, _full((SUBLANES, D_MODEL)), _full((3, N_DEV, UP_CHUNK)), _full((N_DEV, UP_CHUNK))],
        compiler_params=_cparams(1),
    )(dx3, dx3, h3, conv, conv, x2, g, w_up_g, w_fc, w_down_g)


def _xattn_bwd(dx2, o, q, k, v, w_xo, w_xq, x1, g, dep):
    s = x1.shape[0]

    def body(dx2_ref, o_ref, q_ref, k_ref, v_ref, wo_ref, wq_ref, x1_ref, g_ref, dep_ref, dq_ref, dx1_ref, dk_ref,
             dv_ref, dg_ref):
        @pl.when(pl.program_id(0) == 0)
        def _():
            dk_ref[...] = jnp.zeros_like(dk_ref)
            dv_ref[...] = jnp.zeros_like(dv_ref)
            dg_ref[...] = jnp.zeros_like(dg_ref)

        dx2v = dx2_ref[...]
        do = _dot_nt(dx2v.astype(BF16), wo_ref[...])
        dqs = []
        for h in range(N_MEM_HEADS):
            sl = slice(h * MEM_HEAD_DIM, (h + 1) * MEM_HEAD_DIM)
            qh, kh, vh = q_ref[:, sl], k_ref[:, sl], v_ref[:, sl]
            lg = _dot_nt(qh, kh) * (MEM_HEAD_DIM ** -0.5)
            p = jnp.exp(lg - jnp.max(lg, axis=-1, keepdims=True))
            p = p / jnp.sum(p, axis=-1, keepdims=True)
            doh = do[:, sl].astype(BF16)
            dp = _dot_nt(doh, vh)
            ds = (p * (dp - jnp.sum(p * dp, axis=-1, keepdims=True)) * (MEM_HEAD_DIM ** -0.5)).astype(BF16)
            dqs.append(_dot(ds, kh))
            dk_ref[:, sl] += _dot_tn(ds, qh)
            dv_ref[:, sl] += _dot_tn(p.astype(BF16), doh)
        dq = jnp.concatenate(dqs, axis=1).astype(BF16)
        dq_ref[...] = dq
        dh2 = _dot_nt(dq, wq_ref[...])
        xh, r = _rms(x1_ref[...])
        dg_ref[0:1, :] += jnp.sum(dh2 * xh, axis=0, keepdims=True)
        dx1_ref[...] = dx2v + _rms_bwd(xh, r, g_ref[...], dh2)

    row = pl.BlockSpec((TM, D_MODEL), lambda i: (i, 0))
    return pl.pallas_call(
        body, name="xattn_bwd", grid=(s // TM,),
        out_shape=[jax.ShapeDtypeStruct((s, D_MODEL), BF16), jax.ShapeDtypeStruct((s, D_MODEL), F32),
                   jax.ShapeDtypeStruct(k.shape, F32), jax.ShapeDtypeStruct(k.shape, F32),
                   jax.ShapeDtypeStruct((SUBLANES, D_MODEL), F32)],
        in_specs=[row, row, row, _full(k.shape), _full(v.shape), _full(w_xo.shape), _full(w_xq.shape), row,
                  _full(g.shape), ANY_SPEC],
        out_specs=[row, row, _full(k.shape), _full(k.shape), _full((SUBLANES, D_MODEL))],
        compiler_params=_cparams(1),
    )(dx2, o, q, k, v, w_xo, w_xq, x1, g, dep)


def _mem_kv_bwd(dk, dv, mem_n, mem, w_xk, w_xv):
    def body(dk_ref, dv_ref, mn_ref, mem_ref, wk_ref, wv_ref, dwk_ref, dwv_ref, dg_ref):
        dkb, dvb = dk_ref[...].astype(BF16), dv_ref[...].astype(BF16)
        mn = mn_ref[...]
        dwk_ref[...] = _dot_tn(mn, dkb).astype(BF16)
        dwv_ref[...] = _dot_tn(mn, dvb).astype(BF16)
        dmn = _dot_nt(dkb, wk_ref[...]) + _dot_nt(dvb, wv_ref[...])
        xh, _ = _rms(mem_ref[...])
        dg_ref[...] = jnp.zeros_like(dg_ref)
        dg_ref[0:1, :] = jnp.sum(dmn * xh, axis=0, keepdims=True)

    vm = pl.BlockSpec(memory_space=pltpu.VMEM)
    return pl.pallas_call(
        body, name="mem_kv_bwd",
        out_shape=[jax.ShapeDtypeStruct(w_xk.shape, BF16), jax.ShapeDtypeStruct(w_xv.shape, BF16),
                   jax.ShapeDtypeStruct((SUBLANES, D_MODEL), F32)],
        in_specs=[vm] * 6, out_specs=[vm] * 3,
        compiler_params=pltpu.CompilerParams(vmem_limit_bytes=VMEM_LIMIT),
    )(dk, dv, mem_n, mem, w_xk, w_xv)


def _mix_out_bwd(dx1, w_out, attn, gb, gc, xi, w_sc, g_a, g_c, dep):
    s = dx1.shape[0]
    tb = TM // SUBLANES

    def body(dx1_ref, wout_ref, attn_ref, gb_ref, gc_ref, xi_ref, gch_ref, xih_ref, wsc_ref, ga_ref, gcv_ref, dep_ref,
             da1, da4, da16, dd1, dd4, dd16, dgb_ref, dcv_ref, dga_ref, dgc_ref, dwsc_ref, scr):
        i = pl.program_id(0)

        @pl.when(i == 0)
        def _():
            dga_ref[...] = jnp.zeros_like(dga_ref)
            dgc_ref[...] = jnp.zeros_like(dgc_ref)
            dwsc_ref[...] = jnp.zeros_like(dwsc_ref)

        dmixed = _dot_nt(dx1_ref[...].astype(BF16), wout_ref[...])
        da, dcn = dmixed[:, :ATTN_W], dmixed[:, ATTN_W:]
        attn = attn_ref[...]
        xa, ra = _rms(attn)
        dga_ref[0:1, :] += jnp.sum(da * xa, axis=0, keepdims=True)
        dattn = _rms_bwd(xa, ra, ga_ref[...], da)
        _spread(dattn, scr, (da1, da4, da16), F32)
        prod = dattn * attn
        dd = jnp.concatenate(
            [jnp.broadcast_to(jnp.sum(prod[:, h * HEAD_DIM:(h + 1) * HEAD_DIM], axis=-1, keepdims=True),
                              (TM, HEAD_DIM)) for h in range(N_HEADS)], axis=1)
        _spread(dd, scr, (dd1, dd4, dd16), F32)
        gbv = gb_ref[...]
        u = gc_ref[...] * xi_ref[...]
        uh = jnp.where(i > 0, gch_ref[...] * xih_ref[...], 0.0)
        u2, u1 = _shift_down(u, uh, 2), _shift_down(u, uh, 1)
        cv = (u2 * wsc_ref[0:1, :] + u1 * wsc_ref[1:2, :]) + u * wsc_ref[2:3, :]
        xc, rc = _rms(gbv * cv)
        dgc_ref[0:1, :] += jnp.sum(dcn * xc, axis=0, keepdims=True)
        dconv = _rms_bwd(xc, rc, gcv_ref[...], dcn)
        dgb_ref[...] = dconv * cv
        dcv = dconv * gbv
        dcv_ref[...] = dcv
        dwsc_ref[0:1, :] += jnp.sum(dcv * u2, axis=0, keepdims=True)
        dwsc_ref[1:2, :] += jnp.sum(dcv * u1, axis=0, keepdims=True)
        dwsc_ref[2:3, :] += jnp.sum(dcv * u, axis=0, keepdims=True)

    row = lambda n: pl.BlockSpec((TM, n), lambda i: (i, 0))
    halo = pl.BlockSpec((SUBLANES, 512), lambda i: (jnp.maximum(i * tb - 1, 0), 0))
    acc = _full((SUBLANES, 512))
    res = pl.pallas_call(
        body, name="mix_out_bwd", grid=(s // TM,),
        out_shape=_class_shapes(s, 512, F32) * 2 + [jax.ShapeDtypeStruct((s, 512), F32)] * 2
        + [jax.ShapeDtypeStruct((SUBLANES, 512), F32)] * 3,
        in_specs=[row(D_MODEL), _full(w_out.shape), row(512), row(512), row(512), row(512), halo, halo,
                  _full(w_sc.shape), _full(g_a.shape), _full(g_c.shape), ANY_SPEC],
        out_specs=_class_specs(512) * 2 + [row(512)] * 2 + [acc] * 3,
        scratch_shapes=[pltpu.VMEM((512 // LANES, TM, LANES), F32)],
        compiler_params=_cparams(1),
    )(dx1, w_out, attn, gb, gc, xi, gc, xi, w_sc, g_a, g_c, dep)
    return res[0:3], res[3:6], res[6], res[7], res[8], res[9], res[10]


def _swa_bwd(qc, kc, vc, doc, lsec, ddc, bias, dil, dep):
    n128 = qc.shape[1] // WIN
    nb = n128 // 2

    def body(q_ref, qn_ref, kp_ref, kc_ref, vp_ref, vc_ref, do_ref, don_ref, lse_ref, lsen_ref, dd_ref, ddn_ref,
             b_ref, dep_ref, dq_ref, dk_ref, dv_ref, db_ref, s_scr, dp_scr, sn_scr, dpn_scr, ds_scr, p_scr, dsn_scr,
             pn_scr):
        r, b = pl.program_id(0), pl.program_id(1)

        @pl.when((r == 0) & (b == 0))
        def _():
            db_ref[...] = jnp.zeros_like(db_ref)

        pairs = [slice(a * LANES, (a + 1) * LANES) for a in range(N_HEADS // 2)]
        blk_a, blk_b = slice(0, WIN), slice(WIN, 2 * WIN)
        per_head = lambda ref, rows: jnp.stack([ref[0, rows, h * HEAD_DIM:h * HEAD_DIM + 1] for h in range(N_HEADS)])
        for a, sl in enumerate(pairs):
            k_pa = jnp.concatenate([kp_ref[0, :, sl], kc_ref[0, blk_a, sl]], axis=0)
            v_pa = jnp.concatenate([vp_ref[0, :, sl], vc_ref[0, blk_a, sl]], axis=0)
            for sub, (rows, k2, v2) in enumerate(((blk_a, k_pa, v_pa), (blk_b, kc_ref[0, :, sl], vc_ref[0, :, sl]))):
                q_eo = _pair_split(q_ref[0, rows, sl])
                do_eo = _pair_split(do_ref[0, rows, sl].astype(BF16))
                for e in range(2):
                    s_scr[sub, 2 * a + e] = _dot_nt(q_eo[e], k2)
                    dp_scr[sub, 2 * a + e] = _dot_nt(do_eo[e], v2)
            qn_eo = _pair_split(qn_ref[0, :, sl])
            don_eo = _pair_split(don_ref[0, :, sl].astype(BF16))
            for e in range(2):
                sn_scr[2 * a + e] = _dot_nt(qn_eo[e], kc_ref[0, blk_b, sl])
                dpn_scr[2 * a + e] = _dot_nt(don_eo[e], vc_ref[0, blk_b, sl])
        bias = b_ref[...]
        for sub, rows in enumerate((blk_a, blk_b)):
            first = (b == 0) if sub == 0 else False
            p = jnp.exp(jnp.where(_band_mask(first), s_scr[sub] + bias, -jnp.inf) - per_head(lse_ref, rows))
            ds = p * (dp_scr[sub] - per_head(dd_ref, rows))
            db_ref[...] += ds
            ds_scr[sub] = ds.astype(BF16)
            p_scr[sub] = p.astype(BF16)
        qi = lax.broadcasted_iota(jnp.int32, (WIN, WIN), 0)
        kj = lax.broadcasted_iota(jnp.int32, (WIN, WIN), 1)
        valid_n = kj >= qi + jnp.where(b + 1 < nb, 0, WIN)
        every = slice(0, WIN)
        pn = jnp.exp(jnp.where(valid_n, sn_scr[...] + bias[:, :, :WIN], -jnp.inf) - per_head(lsen_ref, every))
        dsn_scr[...] = (pn * (dpn_scr[...] - per_head(ddn_ref, every))).astype(BF16)
        pn_scr[...] = pn.astype(BF16)
        for a, sl in enumerate(pairs):
            k_pa = _pair_split(jnp.concatenate([kp_ref[0, :, sl], kc_ref[0, blk_a, sl]], axis=0))
            k_ab = _pair_split(kc_ref[0, :, sl])
            qa_eo, qb_eo = _pair_split(q_ref[0, blk_a, sl]), _pair_split(q_ref[0, blk_b, sl])
            doa_eo = _pair_split(do_ref[0, blk_a, sl].astype(BF16))
            dob_eo = _pair_split(do_ref[0, blk_b, sl].astype(BF16))
            qn_eo = _pair_split(qn_ref[0, :, sl])
            don_eo = _pair_split(don_ref[0, :, sl].astype(BF16))
            acc = None
            for e in range(2):
                h = 2 * a + e
                terms = (_dot(ds_scr[0, h], k_pa[e]),
                         _dot(ds_scr[1, h], k_ab[e]),
                         _dot_tn(ds_scr[0, h, :, WIN:], qa_eo[e]) + _dot_tn(ds_scr[1, h, :, :WIN], qb_eo[e]),
                         _dot_tn(ds_scr[1, h, :, WIN:], qb_eo[e]) + _dot_tn(dsn_scr[h], qn_eo[e]),
                         _dot_tn(p_scr[0, h, :, WIN:], doa_eo[e]) + _dot_tn(p_scr[1, h, :, :WIN], dob_eo[e]),
                         _dot_tn(p_scr[1, h, :, WIN:], dob_eo[e]) + _dot_tn(pn_scr[h], don_eo[e]))
                acc = terms if acc is None else tuple(x + y for x, y in zip(acc, terms))
            dq_ref[0, blk_a, sl], dq_ref[0, blk_b, sl] = acc[0], acc[1]
            dk_ref[0, blk_a, sl], dk_ref[0, blk_b, sl] = acc[2], acc[3]
            dv_ref[0, blk_a, sl], dv_ref[0, blk_b, sl] = acc[4], acc[5]

    cur = pl.BlockSpec((1, 2 * WIN, 512), lambda r, b: (r, b, 0))
    prev = pl.BlockSpec((1, WIN, 512), lambda r, b: (r, jnp.maximum(2 * b - 1, 0), 0))
    nxt = pl.BlockSpec((1, WIN, 512), lambda r, b: (r, jnp.minimum(2 * b + 2, n128 - 1), 0))
    wide, narrow = (2, N_HEADS, WIN, 2 * WIN), (N_HEADS, WIN, WIN)
    return pl.pallas_call(
        body, name=f"swa_bwd_d{dil}", grid=(dil, nb),
        out_shape=[jax.ShapeDtypeStruct(qc.shape, F32)] * 3 + [jax.ShapeDtypeStruct(bias.shape, F32)],
        in_specs=[cur, nxt, prev, cur, prev, cur, cur, nxt, cur, nxt, cur, nxt, _full(bias.shape), ANY_SPEC],
        out_specs=[cur] * 3 + [_full(bias.shape)],
        scratch_shapes=[pltpu.VMEM(wide, F32), pltpu.VMEM(wide, F32), pltpu.VMEM(narrow, F32),
                        pltpu.VMEM(narrow, F32), pltpu.VMEM(wide, BF16), pltpu.VMEM(wide, BF16),
                        pltpu.VMEM(narrow, BF16), pltpu.VMEM(narrow, BF16)],
        compiler_params=_cparams(2),
    )(qc, qc, kc, kc, vc, vc, doc, doc, lsec, lsec, ddc, ddc, bias, dep)


def _in_proj_bwd(dqs, dks, dvs, dgb, dcv, gc, xi, w_sc, w_in_g, x, g_mix, dx1):
    s = x.shape[0]
    tb = TM // SUBLANES
    last = s // SUBLANES - 1
    n_tiles = s // TM

    def body(dq1, dq4, dq16, dk1, dk4, dk16, dv1, dv4, dv16, dgb_ref, dcv_ref, dcvn_ref, gc_ref, xi_ref, wsc_ref,
             win_ref, x_ref, g_ref, dx1_ref, dproj_ref, gx_ref, dg_ref, scr_a, scr_b):
        i = pl.program_id(0)

        @pl.when(i == 0)
        def _():
            dg_ref[...] = jnp.zeros_like(dg_ref)

        d0 = dcv_ref[...]
        dn = jnp.where(i < n_tiles - 1, dcvn_ref[...], 0.0)
        du = (d0 * wsc_ref[2:3, :] + _shift_up(d0, dn, 1) * wsc_ref[1:2, :]) + _shift_up(d0, dn, 2) * wsc_ref[0:1, :]
        merge = lambda a, b4, b16: (a[...] + _gather_classes(b4, scr_a, 4)) + _gather_classes(b16, scr_b, 16)
        dq = merge(dq1, dq4, dq16) * (HEAD_DIM ** -0.5)
        dk = merge(dk1, dk4, dk16)
        dv = merge(dv1, dv4, dv16)
        dproj = jnp.concatenate([dq, dk, dv, dgb_ref[...], du * xi_ref[...], du * gc_ref[...]], axis=1).astype(BF16)
        dproj_ref[...] = dproj
        dh = jnp.zeros((TM, D_MODEL), F32)
        for j in range(N_DEV):
            dh = dh + _dot_nt(dproj[:, j * IN_CHUNK:(j + 1) * IN_CHUNK], win_ref[j])
        xh, r = _rms(x_ref[...])
        dg_ref[0:1, :] += jnp.sum(dh * xh, axis=0, keepdims=True)
        gx_ref[...] = dx1_ref[...] + _rms_bwd(xh, r, g_ref[...], dh)

    row = lambda n: pl.BlockSpec((TM, n), lambda i: (i, 0))
    nxt = pl.BlockSpec((SUBLANES, 512), lambda i: (jnp.minimum((i + 1) * tb, last), 0))
    return pl.pallas_call(
        body, name="in_proj_bwd", grid=(n_tiles,),
        out_shape=[jax.ShapeDtypeStruct((s, IN_COLS), BF16), jax.ShapeDtypeStruct((s, D_MODEL), F32),
                   jax.ShapeDtypeStruct((SUBLANES, D_MODEL), F32)],
        in_specs=_class_specs(512) * 3 + [row(512), row(512), nxt, row(512), row(512), _full(w_sc.shape),
                                          _full(w_in_g.shape), row(D_MODEL), _full(g_mix.shape), row(D_MODEL)],
        out_specs=[row(IN_COLS), row(D_MODEL), _full((SUBLANES, D_MODEL))],
        scratch_shapes=[pltpu.VMEM((512 // LANES, TM, LANES), F32)] * 2,
        compiler_params=_cparams(1),
    )(*dqs, *dks, *dvs, dgb, dcv, dcv, gc, xi, w_sc, w_in_g, x, g_mix, dx1)


def _dw(a, b, dep, name, a_chunked=False, b_chunked=False, n_chunks=1, chunk_cols=None):
    ts = TS_DW if (a_chunked or b_chunked or chunk_cols) else TS_DW // 2
    if a_chunked:
        nj, s, kk = a.shape
        nn = b.shape[1]
        a_spec = pl.BlockSpec((1, ts, kk), lambda j, t: (j, t, 0))
        b_spec = pl.BlockSpec((ts, nn), lambda j, t: (t, 0))
    elif b_chunked:
        nj, s, nn = b.shape
        kk = a.shape[1]
        a_spec = pl.BlockSpec((ts, kk), lambda j, t: (t, 0))
        b_spec = pl.BlockSpec((1, ts, nn), lambda j, t: (j, t, 0))
    else:
        s, kk = a.shape
        nj, nn = (n_chunks, chunk_cols) if chunk_cols else (1, b.shape[1])
        a_spec = pl.BlockSpec((ts, kk), lambda j, t: (t, 0))
        b_spec = pl.BlockSpec((ts, nn), lambda j, t: (t, j))
    n_steps = s // ts

    def body(a_ref, b_ref, dep_ref, o_ref, acc):
        t = pl.program_id(1)

        @pl.when(t == 0)
        def _():
            acc[...] = jnp.zeros_like(acc)

        av = (a_ref[0] if a_chunked else a_ref[...]).astype(BF16)
        bv = (b_ref[0] if b_chunked else b_ref[...]).astype(BF16)
        acc[...] += _dot_tn(av, bv)

        @pl.when(t == n_steps - 1)
        def _():
            o_ref[0] = acc[...].astype(BF16)

    return pl.pallas_call(
        body, name=name, grid=(nj, n_steps),
        out_shape=jax.ShapeDtypeStruct((nj, kk, nn), BF16),
        in_specs=[a_spec, b_spec, ANY_SPEC],
        out_specs=pl.BlockSpec((1, kk, nn), lambda j, t: (j, 0, 0)),
        scratch_shapes=[pltpu.VMEM((kk, nn), F32)],
        compiler_params=_cparams(2),
    )(a, b, dep)


def _adamw_math(w, g, m, v):
    m2 = ADAM_B1 * m + (1.0 - ADAM_B1) * g
    v2 = ADAM_B2 * v + (1.0 - ADAM_B2) * (g * g)
    m_hat = m2 / (1.0 - ADAM_B1 ** ADAM_STEP)
    v_hat = v2 / (1.0 - ADAM_B2 ** ADAM_STEP)
    delta = -ADAM_LR * (m_hat / (jnp.sqrt(v_hat) + ADAM_EPS) + ADAM_WD * w)
    return delta, m2, v2


def _sum_parts(me, own, p_ref):
    g = None
    for i in range(N_DEV):
        part = jnp.where(me == i, own.astype(F32), p_ref[i].astype(F32))
        g = part if g is None else g + part
    return g


def _adamw_big(name, w, sent, parts, m, v, me_arr):
    rr, cc = w.shape
    tr = rr // 4 if rr >= 512 else rr

    def body(me_ref, w_ref, own_ref, p_ref, m_ref, v_ref, g_ref, d_ref, nm_ref, nv_ref):
        g = _sum_parts(me_ref[0], own_ref[0], p_ref)
        g_ref[...] = g
        d_ref[...], nm_ref[...], nv_ref[...] = _adamw_math(w_ref[...], g, m_ref[...], v_ref[...])

    row = pl.BlockSpec((tr, cc), lambda i, me: (i, 0))
    return pl.pallas_call(
        body, name=name,
        grid_spec=pltpu.PrefetchScalarGridSpec(
            num_scalar_prefetch=1, grid=(rr // tr,),
            in_specs=[row, pl.BlockSpec((1, tr, cc), lambda i, me: (me[0], i, 0)),
                      pl.BlockSpec((N_DEV, tr, cc), lambda i, me: (0, i, 0)), row, row],
            out_specs=[row] * 4),
        out_shape=[jax.ShapeDtypeStruct((rr, cc), F32)] * 4,
        compiler_params=_cparams(1),
    )(me_arr, w, sent, parts, m, v)


def _small_slices():
    return [
        (slice(ROW_RELB, ROW_RELB + 8), slice(0, N_BUCKETS)),
        (slice(ROW_GMIX, ROW_GMIX + 1), slice(0, D_MODEL)),
        (slice(ROW_GAC, ROW_GAC + 1), slice(0, ATTN_W)),
        (slice(ROW_GAC, ROW_GAC + 1), slice(ATTN_W, D_MODEL)),
        (slice(ROW_GXATTN, ROW_GXATTN + 1), slice(0, D_MODEL)),
        (slice(ROW_GMEM, ROW_GMEM + 1), slice(0, D_MODEL)),
        (slice(ROW_GFFN, ROW_GFFN + 1), slice(0, D_MODEL)),
        (slice(ROW_BFC, ROW_BFC + 8), slice(0, UP_CHUNK)),
        (slice(ROW_GFINAL, ROW_GFINAL + 1), slice(0, D_MODEL)),
    ]


def _adamw_small(own, parts, wmv, me_arr):
    slices = _small_slices()
    n = len(slices)

    def body(*refs):
        me_ref, own_ref, p_ref = refs[:3]
        ins = refs[3:3 + 3 * n]
        g_ref = refs[3 + 3 * n]
        outs = refs[4 + 3 * n:]
        g = _sum_parts(me_ref[0], own_ref[...], p_ref)
        g_ref[...] = g
        for a, (rs, ls) in enumerate(slices):
            ga = g[rs, ls]
            outs[4 * a][...] = ga
            outs[4 * a + 1][...], outs[4 * a + 2][...], outs[4 * a + 3][...] = _adamw_math(
                ins[3 * a][...], ga, ins[3 * a + 1][...], ins[3 * a + 2][...])

    vm = pl.BlockSpec(memory_space=pltpu.VMEM)
    flat = [t for trip in wmv for t in trip]
    out_shape = [jax.ShapeDtypeStruct((SMALL_ROWS, D_MODEL), F32)]
    for w, _, _ in wmv:
        out_shape += [jax.ShapeDtypeStruct(w.shape, F32)] * 4
    res = pl.pallas_call(
        body, name="adamw_small", out_shape=out_shape,
        in_specs=[SMEM_SPEC] + [vm] * (2 + 3 * n), out_specs=[vm] * len(out_shape),
    )(me_arr, own, parts, *flat)
    return res[0], [res[1 + 4 * a:5 + 4 * a] for a in range(n)]


def _adamw_shards(items):
    n = len(items)

    def body(*refs):
        for a in range(n):
            w_ref, g_ref, m_ref, v_ref = refs[4 * a:4 * a + 4]
            d_ref, nm_ref, nv_ref = refs[4 * n + 3 * a:4 * n + 3 * a + 3]
            d_ref[...], nm_ref[...], nv_ref[...] = _adamw_math(w_ref[...], g_ref[...], m_ref[...], v_ref[...])

    vm = pl.BlockSpec(memory_space=pltpu.VMEM)
    out_shape = []
    for w, _, _, _ in items:
        out_shape += [jax.ShapeDtypeStruct(w.shape, F32)] * 3
    res = pl.pallas_call(
        body, name="adamw_shards", out_shape=out_shape, in_specs=[vm] * (4 * n), out_specs=[vm] * (3 * n),
    )(*[t for it in items for t in it])
    return [res[3 * a:3 * a + 3] for a in range(n)]


def _mesh_pos():
    return lax.axis_index("x"), lax.axis_index("y"), lax.axis_index("c")


def _dev_index(p):
    return 4 * p[0] + 2 * p[1] + p[2]


def _all_gather(shards):
    n = len(shards)

    def body(*refs):
        ins, outs = refs[:n], refs[n:2 * n]
        send_sems, recv_sems, loc_sems = refs[2 * n:]
        x, y, c = _mesh_pos()
        me, sib = (x, y, c), (x, y, 1 - c)
        chips = [(1 - x, y), (x, 1 - y), (1 - x, 1 - y)]

        def cp(a, k, block, to, src=None):
            dst = outs[a].at[_dev_index(block)]
            return pltpu.make_async_remote_copy(
                src_ref=dst if src is None else src, dst_ref=dst, send_sem=send_sems.at[a, k],
                recv_sem=recv_sems.at[a, k], device_id=to, device_id_type=MESH)

        mine = [pltpu.make_async_copy(ins[a], outs[a].at[_dev_index(me)], loc_sems.at[a]) for a in range(n)]
        for m_ in mine:
            m_.start()
        first = []
        for a in range(n):
            first.append(cp(a, 0, me, sib, src=ins[a]))
            first += [cp(a, 1 + j, me, (*chip, c), src=ins[a]) for j, chip in enumerate(chips)]
        for f in first:
            f.start()
        passed = []
        for a in range(n):
            for j, chip in enumerate(chips):
                cp(a, 1 + j, (*chip, c), me).wait_recv()
                fwd = cp(a, 4 + j, (*chip, c), sib)
                fwd.start()
                passed.append(fwd)
        for a in range(n):
            cp(a, 0, sib, me).wait_recv()
            for j, chip in enumerate(chips):
                cp(a, 4 + j, (*chip, 1 - c), me).wait_recv()
        for f in first + passed:
            f.wait_send()
        for m_ in mine:
            m_.wait()

    hbm = pl.BlockSpec(memory_space=pltpu.HBM)
    return pl.pallas_call(
        body, name="all_gather_weights",
        out_shape=[jax.ShapeDtypeStruct((N_DEV,) + a.shape, a.dtype) for a in shards],
        in_specs=[hbm] * n, out_specs=[hbm] * n,
        scratch_shapes=[pltpu.SemaphoreType.DMA((n, 7)), pltpu.SemaphoreType.DMA((n, 7)),
                        pltpu.SemaphoreType.DMA((n,))],
    )(*shards)


def _peers():
    x, y, c = _mesh_pos()
    return (x, y, c), [((1 - x) if k & 4 else x, (1 - y) if k & 2 else y, (1 - c) if k & 1 else c)
                       for k in range(1, 8)]


def _exchange_copy(src_ref, land_ref, whole, send_sems, recv_sems, a, k, peer, slot):
    src = src_ref if whole else src_ref.at[_dev_index(peer)]
    return pltpu.make_async_remote_copy(
        src_ref=src, dst_ref=land_ref.at[slot], send_sem=send_sems.at[7 * a + k], recv_sem=recv_sems.at[7 * a + k],
        device_id=peer, device_id_type=MESH)


def _exchange_start(name, srcs, whole, dep):
    n = len(srcs)
    lands = [lax.empty(((N_DEV,) + s.shape) if w else s.shape, s.dtype) for s, w in zip(srcs, whole)]

    def body(*refs):
        src_refs, land_refs = refs[:n], refs[n:2 * n]
        send_sems, recv_sems, token = refs[2 * n + 1], refs[2 * n + 2], refs[-1]
        me, peers = _peers()
        for a in range(n):
            for k, peer in enumerate(peers):
                _exchange_copy(src_refs[a], land_refs[a], whole[a], send_sems, recv_sems, a, k, peer,
                               _dev_index(me)).start()
        token[...] = jnp.zeros_like(token)

    res = pl.pallas_call(
        body, name=name,
        out_shape=(pltpu.SemaphoreType.DMA((7 * n,)), pltpu.SemaphoreType.DMA((7 * n,)),
                   *[pltpu.HBM(a.shape, a.dtype) for a in srcs], *[pltpu.HBM(a.shape, a.dtype) for a in lands],
                   jax.ShapeDtypeStruct((SUBLANES, 128), F32)),
        in_specs=[HBM_SPEC] * (2 * n) + [ANY_SPEC],
        out_specs=(SEM_SPEC, SEM_SPEC, *([HBM_SPEC] * (2 * n)), VMEM_SPEC),
        input_output_aliases={i: 2 + i for i in range(2 * n)},
        compiler_params=pltpu.CompilerParams(has_side_effects=DATAFLOW),
    )(*[pltpu.with_memory_space_constraint(a, pltpu.HBM) for a in srcs],
      *[pltpu.with_memory_space_constraint(a, pltpu.HBM) for a in lands], dep)
    return res[0], res[1], list(res[2:2 + n]), list(res[2 + n:2 + 2 * n]), res[-1]


def _exchange_wait(name, started, whole, after, which=None):
    send_sems, recv_sems, srcs, lands, _ = started
    which = list(range(len(srcs))) if which is None else which
    srcs, lands = [srcs[a] for a in which], [lands[a] for a in which]
    n = len(srcs)

    def body(*refs):
        src_refs, land_refs = refs[:n], refs[n:2 * n]
        send_sems, recv_sems = refs[2 * n], refs[2 * n + 1]
        _, peers = _peers()
        for i, a in enumerate(which):
            for k, peer in enumerate(peers):
                cp = _exchange_copy(src_refs[i], land_refs[i], whole[a], send_sems, recv_sems, a, k, peer,
                                    _dev_index(peer))
                cp.wait_send()
                cp.wait_recv()

    res = pl.pallas_call(
        body, name=name,
        out_shape=[pltpu.HBM(a.shape, a.dtype) for a in srcs + lands],
        in_specs=[HBM_SPEC] * (2 * n) + [SEM_SPEC, SEM_SPEC, ANY_SPEC],
        out_specs=[HBM_SPEC] * (2 * n),
        input_output_aliases={i: i for i in range(2 * n)},
        compiler_params=pltpu.CompilerParams(has_side_effects=DATAFLOW),
    )(*srcs, *lands, send_sems, recv_sems, after)
    return list(res[:n]), list(res[n:])


def _gather_start(name, shards, dep):
    n = len(shards)
    lands = [lax.empty((N_DEV,) + a.shape, a.dtype) for a in shards]

    def body(*refs):
        src_refs, land_refs = refs[:n], refs[n:2 * n]
        send_sems, recv_sems, token = refs[2 * n + 1], refs[2 * n + 2], refs[-1]
        x, y, c = _mesh_pos()
        peers = [(x, y, 1 - c), (1 - x, y, c), (x, 1 - y, c), (1 - x, 1 - y, c)]
        for a in range(n):
            for k, peer in enumerate(peers):
                pltpu.make_async_remote_copy(
                    src_ref=src_refs[a], dst_ref=land_refs[a].at[_dev_index((x, y, c))], send_sem=send_sems.at[4 * a + k],
                    recv_sem=recv_sems.at[4 * a + k], device_id=peer, device_id_type=MESH).start()
        token[...] = jnp.zeros_like(token)

    res = pl.pallas_call(
        body, name=name,
        out_shape=(pltpu.SemaphoreType.DMA((4 * n,)), pltpu.SemaphoreType.DMA((4 * n,)),
                   *[pltpu.HBM(a.shape, a.dtype) for a in shards], *[pltpu.HBM(a.shape, a.dtype) for a in lands],
                   jax.ShapeDtypeStruct((SUBLANES, 128), F32)),
        in_specs=[HBM_SPEC] * (2 * n) + [ANY_SPEC],
        out_specs=(SEM_SPEC, SEM_SPEC, *([HBM_SPEC] * (2 * n)), VMEM_SPEC),
        input_output_aliases={i: 2 + i for i in range(2 * n)},
        compiler_params=pltpu.CompilerParams(has_side_effects=DATAFLOW),
    )(*[pltpu.with_memory_space_constraint(a, pltpu.HBM) for a in shards],
      *[pltpu.with_memory_space_constraint(a, pltpu.HBM) for a in lands], dep)
    return res[0], res[1], list(res[2:2 + n]), list(res[2 + n:2 + 2 * n]), res[-1]


def _gather_forward(name, send_sems, recv_sems, lands, which, after):
    n = len(which)

    def body(*refs):
        land_refs = refs[:n]
        send_sems, recv_sems = refs[n], refs[n + 1]
        fsend, frecv, token = refs[n + 3], refs[n + 4], refs[-1]
        x, y, c = _mesh_pos()
        chips = [(1 - x, y), (x, 1 - y), (1 - x, 1 - y)]
        for i, a in enumerate(which):
            for j, chip in enumerate(chips):
                block = land_refs[i].at[_dev_index((*chip, c))]
                pltpu.make_async_remote_copy(
                    src_ref=block, dst_ref=block, send_sem=send_sems.at[4 * a + 1 + j], recv_sem=recv_sems.at[4 * a + 1 + j],
                    device_id=(*chip, c), device_id_type=MESH).wait_recv()
                pltpu.make_async_remote_copy(
                    src_ref=block, dst_ref=block, send_sem=fsend.at[3 * i + j], recv_sem=frecv.at[3 * i + j],
                    device_id=(x, y, 1 - c), device_id_type=MESH).start()
        token[...] = jnp.zeros_like(token)

    res = pl.pallas_call(
        body, name=name,
        out_shape=(pltpu.SemaphoreType.DMA((3 * n,)), pltpu.SemaphoreType.DMA((3 * n,)),
                   *[pltpu.HBM(a.shape, a.dtype) for a in lands], jax.ShapeDtypeStruct((SUBLANES, 128), F32)),
        in_specs=[HBM_SPEC] * n + [SEM_SPEC, SEM_SPEC, ANY_SPEC],
        out_specs=(SEM_SPEC, SEM_SPEC, *([HBM_SPEC] * n), VMEM_SPEC),
        input_output_aliases={i: 2 + i for i in range(n)},
        compiler_params=pltpu.CompilerParams(has_side_effects=DATAFLOW),
    )(*lands, send_sems, recv_sems, after)
    return res[0], res[1], list(res[2:2 + n]), res[-1]


def _gather_wait(name, send_sems, recv_sems, fsend, frecv, srcs, lands, which, after):
    n = len(which)

    def body(*refs):
        land_refs = refs[n:2 * n]
        send_sems, recv_sems, fsend, frecv = refs[2 * n:2 * n + 4]
        x, y, c = _mesh_pos()
        sib = (x, y, 1 - c)
        chips = [(1 - x, y), (x, 1 - y), (1 - x, 1 - y)]
        for i, a in enumerate(which):
            def cp(slot, ssem, rsem):
                block = land_refs[i].at[_dev_index(slot)]
                return pltpu.make_async_remote_copy(src_ref=block, dst_ref=block, send_sem=ssem, recv_sem=rsem,
                                                    device_id=sib, device_id_type=MESH)
            cp(sib, send_sems.at[4 * a], recv_sems.at[4 * a]).wait_recv()
            for j, chip in enumerate(chips):
                cp((*chip, 1 - c), fsend.at[3 * i + j], frecv.at[3 * i + j]).wait_recv()
            for k in range(4):
                cp(sib, send_sems.at[4 * a + k], recv_sems.at[4 * a + k]).wait_send()
            for j in range(3):
                cp(sib, fsend.at[3 * i + j], frecv.at[3 * i + j]).wait_send()

    res = pl.pallas_call(
        body, name=name,
        out_shape=[pltpu.HBM(a.shape, a.dtype) for a in srcs + lands],
        in_specs=[HBM_SPEC] * (2 * n) + [SEM_SPEC] * 4 + [ANY_SPEC],
        out_specs=[HBM_SPEC] * (2 * n),
        input_output_aliases={i: i for i in range(2 * n)},
        compiler_params=pltpu.CompilerParams(has_side_effects=DATAFLOW),
    )(*srcs, *lands, send_sems, recv_sems, fsend, frecv, after)
    return list(res[n:])


def _local_step(x, mem, target, rel_bias, g_mix, w_in_g, w_sc, g_a, g_c, g_xattn, g_mem, g_ffn, w_fc, b_fc, g_final,
                dep, forward_weights, late_weights, emit, emit_small):
    s = x.shape[0]
    buckets = _bucket_tables()
    bias = _bias_fwd(rel_bias, buckets)

    h1, qs, ks, vs, gb, gc, xi = _rms_proj(x, g_mix, w_in_g, dep)
    qs, ks, vs = ([a[0][None]] + list(a[1:]) for a in (qs, ks, vs))
    group1, group2 = ["w_out", "w_xq", "w_xk", "w_xv", "w_xo"], ["w_up", "w_down"]
    tok = forward_weights(group1, h1)
    branches = []
    for p, dil in enumerate(DILATIONS):
        o_p, lse_p = _swa_fwd(qs[p], ks[p], vs[p], bias[p], dil, tok)
        branches.append([o_p[0], lse_p[0]] if dil == 1 else [o_p, lse_p])
    lw = late_weights(group1, branches[-1][0])
    w_out, w_xq, w_xk, w_xv, w_xo = (lw[n] for n in group1)
    attn, lses, mixed, x1 = _mix_out(branches, gb, gc, xi, x, w_sc, g_a, g_c, w_out)
    tok = forward_weights(group2, x1)
    mem_n, mk, mv = _mem_kv(mem, g_mem, w_xk, w_xv)
    h2, xq, xo, x2 = _xattn_fwd(x1, g_xattn, w_xq, mk, mv, w_xo, tok)
    lw = late_weights(group2, x2)
    w_up_g, w_down_g = lw["w_up"], lw["w_down"]
    h3, conv, act, dx3, loss_acc, dg_final = _ffn_fwd(x2, g_ffn, w_up_g, w_fc, b_fc, w_down_g, g_final, target)

    gw_down = _dw(act, dx3, dep, "dw_down", a_chunked=True)
    dup, dx2, dg_ffn, dw_fc, db_fc = _ffn_bwd(dx3, h3, conv, x2, g_ffn, w_up_g, w_fc, w_down_g)
    gw_up = _dw(dup, h3, dep, "dw_up", a_chunked=True)
    tok = emit(dict(w_down=gw_down, w_up=gw_up))
    dxq, dx1, dmk, dmv, dg_xattn = _xattn_bwd(dx2, xo, xq, mk, mv, w_xo, w_xq, x1, g_xattn, tok)
    gw_xo = _dw(xo, dx2, tok, "dw_xo")[0]
    gw_xq = _dw(h2, dxq, tok, "dw_xq")[0]
    gw_xk, gw_xv, dg_mem = _mem_kv_bwd(dmk, dmv, mem_n, mem, w_xk, w_xv)
    tok = emit(dict(w_xo=gw_xo, w_xq=gw_xq, w_xk=gw_xk, w_xv=gw_xv))
    dattns, dds, dgb, dcv, dg_a, dg_c, dw_sc = _mix_out_bwd(dx1, w_out, attn, gb, gc, xi, w_sc, g_a, g_c, tok)
    first = lambda a: [a[0][None]] + list(a[1:])
    dattns, dds, lses = first(dattns), first(dds), first(lses)
    gw_out = _dw(mixed, dx1, tok, "dw_out")[0]
    tok = emit(dict(w_out=gw_out))
    dqs, dks, dvs, dbias = [], [], [], []
    for p, dil in enumerate(DILATIONS):
        dq_p, dk_p, dv_p, db_p = _swa_bwd(qs[p], ks[p], vs[p], dattns[p], lses[p], dds[p], bias[p], dil, tok)
        dqs.append(dq_p[0] if dil == 1 else dq_p)
        dks.append(dk_p[0] if dil == 1 else dk_p)
        dvs.append(dv_p[0] if dil == 1 else dv_p)
        dbias.append(db_p)
    d_relb = _bias_bwd(jnp.stack(dbias), buckets)
    dproj, grad_x, dg_mix = _in_proj_bwd(dqs, dks, dvs, dgb, dcv, gc, xi, w_sc, w_in_g, x, g_mix, dx1)
    pad = lambda a: jnp.pad(a, ((0, 0), (0, D_MODEL - a.shape[1])))
    small = jnp.concatenate([
        d_relb, dg_mix, dg_xattn, dg_mem, dg_ffn, dg_final, jnp.concatenate([dg_a, dg_c], axis=1),
        pad(dw_sc), pad(db_fc), pad(dw_fc.reshape(3 * N_DEV, UP_CHUNK)), pad(loss_acc)], axis=0)
    tok = emit_small(small)
    gw_in = _dw(h1, dproj, tok, "dw_in", n_chunks=N_DEV, chunk_cols=IN_CHUNK)
    emit(dict(w_in=gw_in))
    return grad_x


def kernel(x, mem, rel_bias, g_mix, w_in, w_short_conv, g_attn_out, g_conv_out, w_out, g_xattn, g_mem, w_xq, w_xk, w_xv, w_xo, g_ffn, w_up, w_ffn_conv, b_ffn_conv, w_down, g_final, loss_target, m_rel_bias, m_g_mix, m_w_in, m_w_short_conv, m_g_attn_out, m_g_conv_out, m_w_out, m_g_xattn, m_g_mem, m_w_xq, m_w_xk, m_w_xv, m_w_xo, m_g_ffn, m_w_up, m_w_ffn_conv, m_b_ffn_conv, m_w_down, m_g_final, v_rel_bias, v_g_mix, v_w_in, v_w_short_conv, v_g_attn_out, v_g_conv_out, v_w_out, v_g_xattn, v_g_mem, v_w_xq, v_w_xk, v_w_xv, v_w_xo, v_g_ffn, v_w_up, v_w_ffn_conv, v_b_ffn_conv, v_w_down, v_g_final):
    me = _dev_index(_mesh_pos())
    me_arr = me.reshape(1).astype(jnp.int32)

    big_names = ["w_in", "w_out", "w_xq", "w_xk", "w_xv", "w_xo", "w_up", "w_down"]
    late_names = big_names[1:]
    big_w = dict(w_in=w_in[0], w_out=w_out[0], w_xq=w_xq[0], w_xk=w_xk[0], w_xv=w_xv[0], w_xo=w_xo[0],
                 w_up=w_up[0].T, w_down=w_down[0])
    big_m = dict(w_in=m_w_in[0], w_out=m_w_out[0], w_xq=m_w_xq[0], w_xk=m_w_xk[0], w_xv=m_w_xv[0], w_xo=m_w_xo[0],
                 w_up=m_w_up[0].T, w_down=m_w_down[0])
    big_v = dict(w_in=v_w_in[0], w_out=v_w_out[0], w_xq=v_w_xq[0], w_xk=v_w_xk[0], w_xv=v_w_xv[0], w_xo=v_w_xo[0],
                 w_up=v_w_up[0].T, w_down=v_w_down[0])
    shard_shape = {n: big_w[n].shape for n in big_names}

    w_in_g, w_sc_g, w_fc_full = _all_gather([big_w["w_in"].astype(BF16), w_short_conv[0], w_ffn_conv[0]])
    w_sc_full = w_sc_g.transpose(1, 0, 2).reshape(3, CONV_W)
    late_shards = [big_w[n].astype(BF16) for n in late_names]
    ag_send, ag_recv, ag_srcs, ag_lands, ag_token = _gather_start("gather_weights_start", late_shards, w_in_g)
    forwarded = {}

    def forward_weights(names, after):
        which = [late_names.index(n) for n in names]
        fsend, frecv, lands, token = _gather_forward("gather_" + "_".join(names) + "_forward", ag_send, ag_recv,
                                                     [ag_lands[a] for a in which], which, after)
        forwarded[tuple(names)] = (fsend, frecv, lands)
        return token

    def late_weights(names, after):
        which = [late_names.index(n) for n in names]
        fsend, frecv, lands = forwarded[tuple(names)]
        lands = _gather_wait("gather_" + "_".join(names) + "_wait", ag_send, ag_recv, fsend, frecv,
                             [ag_srcs[a] for a in which], lands, which, after)
        out = {}
        for n, a, land in zip(names, which, lands):
            full = lax.dynamic_update_index_in_dim(land, late_shards[a], me, 0)
            if n == "w_up":
                out[n] = full
            elif n == "w_down":
                out[n] = full.reshape(N_DEV // 2, UP_CHUNK, D_MODEL)
            else:
                out[n] = full.reshape(D_MODEL, D_MODEL)
        return out

    sent = []

    def emit(grads):
        names = list(grads)
        blocks = [grads[n].reshape((N_DEV,) + shard_shape[n]) for n in names]
        started = _exchange_start("scatter_" + "_".join(names) + "_start", blocks, [False] * len(names), me_arr)
        sent.append((names, started))
        return started[-1]

    def emit_small(small):
        sent_small.append((small, _exchange_start("gather_small_start", [small], [True], me_arr)))
        return sent_small[0][1][-1]

    sent_small = []
    grad_x = _local_step(
        x[0], mem[0], loss_target[0], rel_bias, g_mix, w_in_g, w_sc_full, g_attn_out, g_conv_out, g_xattn, g_mem,
        g_ffn, w_fc_full, b_ffn_conv.reshape(N_DEV, 1, UP_CHUNK), g_final.reshape(1, D_MODEL), ag_token,
        forward_weights, late_weights, emit, emit_small)

    small_g, small_started = sent_small[0]
    after = sent[-1][1][-1]
    small_parts = _exchange_wait("gather_small_wait", small_started, [True], after)[1][0]
    big_out = {}
    after = small_parts
    for names, started in sent:
        blocks, lands = _exchange_wait("scatter_" + "_".join(names) + "_wait", started, [False] * len(names), after)
        for n, block, land in zip(names, blocks, lands):
            res = _adamw_big("adamw_" + n, big_w[n], block, land, big_m[n], big_v[n], me_arr)
            big_out[n] = [(r.T if n == "w_up" else r)[None] for r in res]
            after = res[0]

    as_rows = lambda a: a.reshape(N_DEV, UP_CHUNK)
    row1 = lambda a: a.reshape(1, D_MODEL)
    small_names = ["rel_bias", "g_mix", "g_attn_out", "g_conv_out", "g_xattn", "g_mem", "g_ffn", "b_ffn_conv", "g_final"]
    wmv = [
        (rel_bias, m_rel_bias, v_rel_bias), (g_mix, m_g_mix, v_g_mix), (g_attn_out, m_g_attn_out, v_g_attn_out),
        (g_conv_out, m_g_conv_out, v_g_conv_out), (g_xattn, m_g_xattn, v_g_xattn), (g_mem, m_g_mem, v_g_mem),
        (g_ffn, m_g_ffn, v_g_ffn), (as_rows(b_ffn_conv), as_rows(m_b_ffn_conv), as_rows(v_b_ffn_conv)),
        (row1(g_final), row1(m_g_final), row1(v_g_final))]
    g_packed, small_res = _adamw_small(small_g, small_parts, wmv, me_arr)
    small_out = dict(zip(small_names, small_res))
    loss = g_packed[ROW_LOSS, 0]
    small_out["b_ffn_conv"] = [a.reshape(1, 2 * D_FF) for a in small_out["b_ffn_conv"]]
    small_out["g_final"] = [a.reshape(D_MODEL) for a in small_out["g_final"]]

    g_wsc = lax.dynamic_slice(g_packed[ROW_WSC:ROW_WSC + 3, 0:CONV_W], (0, me * HEAD_DIM), (3, HEAD_DIM))
    g_wfc = lax.dynamic_slice(g_packed[ROW_WFC:ROW_WFC + 3 * N_DEV, 0:UP_CHUNK].reshape(3, N_DEV, UP_CHUNK),
                              (0, me, 0), (3, 1, UP_CHUNK)).reshape(3, UP_CHUNK)
    shard_res = _adamw_shards([(w_short_conv[0], g_wsc, m_w_short_conv[0], v_w_short_conv[0]),
                               (w_ffn_conv[0], g_wfc, m_w_ffn_conv[0], v_w_ffn_conv[0])])
    small_out["w_short_conv"] = [g_wsc[None]] + [a[None] for a in shard_res[0]]
    small_out["w_ffn_conv"] = [g_wfc[None]] + [a[None] for a in shard_res[1]]

    order = ["rel_bias", "g_mix", "w_in", "w_short_conv", "g_attn_out", "g_conv_out", "w_out", "g_xattn", "g_mem",
             "w_xq", "w_xk", "w_xv", "w_xo", "g_ffn", "w_up", "w_ffn_conv", "b_ffn_conv", "w_down", "g_final"]
    allp = {**big_out, **small_out}
    outs = [loss, grad_x[None]]
    for kind in range(4):
        outs += [allp[n][kind] for n in order]
    return tuple(outs)
```

```python
import functools
import math

import numpy as np
import jax
import jax.numpy as jnp
from jax import lax
from jax.experimental import pallas as pl
from jax.experimental.pallas import tpu as pltpu

F32 = jnp.float32
BF16 = jnp.bfloat16
MESH = pl.DeviceIdType.MESH

N_DEV = 8
D_MODEL = 1024
ATTN_W = 512
CONV_W = 512
N_HEADS = 8
HEAD_DIM = 64
WIN = 128
DILATIONS = (1, 4, 16)
N_BUCKETS = 32
BUCKET_MAX_EXACT = 16
BUCKET_MAX_DISTANCE = 2048
N_MEM_HEADS = 4
MEM_HEAD_DIM = 256
D_FF = 2816
IN_COLS = 3072
IN_CHUNK = IN_COLS // N_DEV
UP_CHUNK = 2 * D_FF // N_DEV
EPS = 1e-6

ADAM_LR = 0.001
ADAM_B1 = 0.9
ADAM_B2 = 0.999
ADAM_EPS = 1e-08
ADAM_WD = 0.01
ADAM_STEP = 10

SUBLANES = 8
LANES = 128
HALO = 16
TM = 512
TM_FFN = 256
TS_DW = 4096
VMEM_LIMIT = 56 * 1024 * 1024

ROW_RELB, ROW_GMIX, ROW_GXATTN, ROW_GMEM, ROW_GFFN, ROW_GFINAL, ROW_GAC = 0, 8, 16, 24, 32, 40, 48
ROW_WSC, ROW_BFC, ROW_WFC, ROW_LOSS, SMALL_ROWS = 56, 64, 72, 96, 104


def _cparams(n_grid):
    return pltpu.CompilerParams(dimension_semantics=("arbitrary",) * n_grid, vmem_limit_bytes=VMEM_LIMIT)


def _full(shape):
    nd = len(shape)
    return pl.BlockSpec(tuple(shape), lambda *_: (0,) * nd)


def _resident(shape):
    nd = len(shape)
    return pl.BlockSpec(tuple(shape), lambda *_: (0,) * nd, pipeline_mode=pl.Buffered(1))


ANY_SPEC = pl.BlockSpec(memory_space=pl.ANY)
HBM_SPEC = pl.BlockSpec(memory_space=pltpu.HBM)
SEM_SPEC = pl.BlockSpec(memory_space=pltpu.SEMAPHORE)
VMEM_SPEC = pl.BlockSpec(memory_space=pltpu.VMEM)
SMEM_SPEC = pl.BlockSpec(memory_space=pltpu.SMEM)
DATAFLOW = pltpu.SideEffectType.DATAFLOW_SIDE_EFFECTING


def _rms(x):
    r = lax.rsqrt(jnp.mean(x * x, axis=-1, keepdims=True) + EPS)
    return x * r, r


def _rms_bwd(xh, r, g, dy):
    dxh = dy * g
    return r * (dxh - xh * jnp.mean(dxh * xh, axis=-1, keepdims=True))


def _shift_down(u, halo, k):
    ru = pltpu.roll(u, k, 0)
    rh = pltpu.roll(halo, k, 0)
    row = lax.broadcasted_iota(jnp.int32, rh.shape, 0)
    head = jnp.where(row < k, rh, ru[0:SUBLANES])
    return jnp.concatenate([head, ru[SUBLANES:]], axis=0)


def _shift_up(u, halo, k):
    tm = u.shape[0]
    ru = pltpu.roll(u, tm - k, 0)
    rh = pltpu.roll(halo, SUBLANES - k, 0)
    row = lax.broadcasted_iota(jnp.int32, rh.shape, 0)
    tail = jnp.where(row >= SUBLANES - k, rh, ru[tm - SUBLANES:])
    return jnp.concatenate([ru[:tm - SUBLANES], tail], axis=0)


def _causal_conv3(u, halo, w_ref):
    return (_shift_down(u, halo, 2) * w_ref[0:1, :] + _shift_down(u, halo, 1) * w_ref[1:2, :]) + u * w_ref[2:3, :]


def _dot(a, b):
    return jnp.dot(a, b, preferred_element_type=F32)


def _dot_nt(a, b):
    return lax.dot_general(a, b, (((1,), (1,)), ((), ())), preferred_element_type=F32)


def _dot_tn(a, b):
    return lax.dot_general(a, b, (((0,), (0,)), ((), ())), preferred_element_type=F32)


def _sigmoid(x):
    return 0.5 * jnp.tanh(0.5 * x) + 0.5


def _bucket_tables():
    qi = np.arange(WIN)[:, None]
    kj = np.arange(2 * WIN)[None, :]
    steps = np.clip(qi + WIN - kj, 0, WIN)
    out = []
    for d in DILATIONS:
        dist = steps * d
        dd = np.maximum(dist, 1).astype(np.float32)
        large = BUCKET_MAX_EXACT + (
            np.log(dd / np.float32(BUCKET_MAX_EXACT)) / np.float32(math.log(BUCKET_MAX_DISTANCE / BUCKET_MAX_EXACT))
            * np.float32(N_BUCKETS - BUCKET_MAX_EXACT)).astype(np.int32)
        large = np.minimum(large, N_BUCKETS - 1)
        out.append(np.where(dist < BUCKET_MAX_EXACT, dist, large).astype(np.int32))
    return np.stack(out)


def _bias_fwd(rel_bias, buckets):
    present = [sorted(set(buckets[p].ravel().tolist())) for p in range(3)]

    def body(rb_ref, bk_ref, o_ref):
        for p in range(3):
            bk = bk_ref[p]
            for h in range(N_HEADS):
                acc = jnp.zeros((WIN, 2 * WIN), F32)
                for b in present[p]:
                    acc = jnp.where(bk == b, rb_ref[h, b], acc)
                o_ref[p, h] = acc

    return pl.pallas_call(
        body, name="bias_fwd",
        out_shape=jax.ShapeDtypeStruct((3, N_HEADS, WIN, 2 * WIN), F32),
        in_specs=[pl.BlockSpec(memory_space=pltpu.SMEM), pl.BlockSpec(memory_space=pltpu.VMEM)],
        out_specs=pl.BlockSpec(memory_space=pltpu.VMEM),
    )(rel_bias, jnp.asarray(buckets))


def _bias_bwd(dbias, buckets):
    present = [set(buckets[p].ravel().tolist()) for p in range(3)]

    def body(db_ref, bk_ref, o_ref):
        lane = lax.broadcasted_iota(jnp.int32, (1, D_MODEL), 1)
        rows = []
        for h in range(N_HEADS):
            row = jnp.zeros((1, D_MODEL), F32)
            for b in range(N_BUCKETS):
                tot = jnp.zeros((1, 1), F32)
                for p in (p for p in range(3) if b in present[p]):
                    sel = jnp.where(bk_ref[p] == b, db_ref[p, h], 0.0)
                    tot = tot + jnp.sum(jnp.sum(sel, axis=0, keepdims=True), axis=1, keepdims=True)
                row = jnp.where(lane == b, tot, row)
            rows.append(row)
        o_ref[...] = jnp.concatenate(rows, axis=0)

    return pl.pallas_call(
        body, name="bias_bwd",
        out_shape=jax.ShapeDtypeStruct((N_HEADS, D_MODEL), F32),
        in_specs=[pl.BlockSpec(memory_space=pltpu.VMEM), pl.BlockSpec(memory_space=pltpu.VMEM)],
        out_specs=pl.BlockSpec(memory_space=pltpu.VMEM),
    )(dbias, jnp.asarray(buckets))


def _spread(val, scr_ref, out_refs, dtype):
    out_refs[0][...] = val.astype(dtype)
    n_blk = val.shape[1] // LANES
    for c in range(n_blk):
        scr_ref[c] = val[:, c * LANES:(c + 1) * LANES]
    for o_ref, d in zip(out_refs[1:], DILATIONS[1:]):
        for r in range(d):
            for c in range(n_blk):
                o_ref[r, :, c * LANES:(c + 1) * LANES] = scr_ref.at[c][pl.ds(r, TM // d, stride=d), :].astype(dtype)


def _gather_classes(blk_ref, scr_ref, d):
    n_blk = blk_ref.shape[2] // LANES
    for r in range(d):
        for c in range(n_blk):
            scr_ref.at[c][pl.ds(r, TM // d, stride=d), :] = blk_ref[r, :, c * LANES:(c + 1) * LANES].astype(F32)
    return jnp.concatenate([scr_ref[c] for c in range(n_blk)], axis=1)


def _class_specs(cols):
    return [pl.BlockSpec((TM, cols), lambda i: (i, 0))] + [
        pl.BlockSpec((d, TM // d, cols), lambda i: (0, i, 0)) for d in DILATIONS[1:]]


def _class_shapes(s, cols, dtype):
    return [jax.ShapeDtypeStruct((s, cols), dtype)] + [
        jax.ShapeDtypeStruct((d, s // d, cols), dtype) for d in DILATIONS[1:]]


def _rms_proj(x, g_mix, w_in_g, dep):
    s = x.shape[0]

    def body(x_ref, g_ref, w_ref, dep_ref, h_ref, q1, q4, q16, k1, k4, k16, v1, v4, v16, gb_ref, gc_ref, xi_ref, scr):
        xh, _ = _rms(x_ref[...])
        h = (xh * g_ref[...]).astype(BF16)
        h_ref[...] = h
        proj = jnp.concatenate([_dot(h, w_ref[j]) for j in range(N_DEV)], axis=1)
        _spread(proj[:, 0:512] * (HEAD_DIM ** -0.5), scr, (q1, q4, q16), BF16)
        _spread(proj[:, 512:1024], scr, (k1, k4, k16), BF16)
        _spread(proj[:, 1024:1536], scr, (v1, v4, v16), BF16)
        gb_ref[...] = proj[:, 1536:2048]
        gc_ref[...] = proj[:, 2048:2560]
        xi_ref[...] = proj[:, 2560:3072]

    row = lambda n: pl.BlockSpec((TM, n), lambda i: (i, 0))
    res = pl.pallas_call(
        body, name="rms_proj", grid=(s // TM,),
        out_shape=[jax.ShapeDtypeStruct((s, D_MODEL), BF16)] + _class_shapes(s, 512, BF16) * 3
        + [jax.ShapeDtypeStruct((s, 512), F32)] * 3,
        in_specs=[row(D_MODEL), _full(g_mix.shape), _full(w_in_g.shape), ANY_SPEC],
        out_specs=[row(D_MODEL)] + _class_specs(512) * 3 + [row(512)] * 3,
        scratch_shapes=[pltpu.VMEM((512 // LANES, TM, LANES), F32)],
        compiler_params=_cparams(1),
    )(x, g_mix, w_in_g, dep)
    return res[0], res[1:4], res[4:7], res[7:10], res[10], res[11], res[12]


def _pair_split(x2):
    lane = lax.broadcasted_iota(jnp.int32, x2.shape, 1)
    zero = jnp.zeros_like(x2)
    return jnp.where(lane < HEAD_DIM, x2, zero), jnp.where(lane >= HEAD_DIM, x2, zero)


def _pair_join(even, odd):
    lane = lax.broadcasted_iota(jnp.int32, (even.shape[0], LANES), 1)
    return jnp.where(lane < HEAD_DIM, even, odd)


def _band_mask(first):
    qi = lax.broadcasted_iota(jnp.int32, (WIN, 2 * WIN), 0)
    kj = lax.broadcasted_iota(jnp.int32, (WIN, 2 * WIN), 1)
    steps = qi + WIN - kj
    return (steps >= 0) & (steps <= WIN) & (kj >= jnp.where(first, WIN, 0))


def _swa_fwd(qc, kc, vc, bias, dil, dep):
    nb = qc.shape[1] // (2 * WIN)

    def body(q_ref, kp_ref, kc_ref, vp_ref, vc_ref, b_ref, dep_ref, o_ref, lse_ref, s_scr, p_scr):
        b = pl.program_id(1)
        pairs = [slice(a * LANES, (a + 1) * LANES) for a in range(N_HEADS // 2)]
        for sub in range(2):
            rows = slice(sub * WIN, (sub + 1) * WIN)

            def keys(prev_ref, cur_ref, sl):
                if sub == 0:
                    return jnp.concatenate([prev_ref[0, :, sl], cur_ref[0, 0:WIN, sl]], axis=0)
                return cur_ref[0, :, sl]

            for a, sl in enumerate(pairs):
                k2 = keys(kp_ref, kc_ref, sl)
                for e, qh in enumerate(_pair_split(q_ref[0, rows, sl])):
                    s_scr[sub, 2 * a + e] = _dot_nt(qh, k2)
            first = (b == 0) if sub == 0 else False
            lg = jnp.where(_band_mask(first), s_scr[sub] + b_ref[...], -jnp.inf)
            m = jnp.max(lg, axis=-1, keepdims=True)
            p = jnp.exp(lg - m)
            den = jnp.sum(p, axis=-1, keepdims=True)
            p_scr[sub] = p.astype(BF16)
            lse = m + jnp.log(den)
            for a, sl in enumerate(pairs):
                v_even, v_odd = _pair_split(keys(vp_ref, vc_ref, sl))
                o2 = _dot(p_scr[sub, 2 * a], v_even) + _dot(p_scr[sub, 2 * a + 1], v_odd)
                o_ref[0, rows, sl] = o2 / _pair_join(den[2 * a], den[2 * a + 1])
                lse_ref[0, rows, sl] = _pair_join(lse[2 * a], lse[2 * a + 1])

    cur = pl.BlockSpec((1, 2 * WIN, 512), lambda r, b: (r, b, 0))
    prev = pl.BlockSpec((1, WIN, 512), lambda r, b: (r, jnp.maximum(2 * b - 1, 0), 0))
    return pl.pallas_call(
        body, name=f"swa_fwd_d{dil}", grid=(dil, nb),
        out_shape=[jax.ShapeDtypeStruct(qc.shape, F32)] * 2,
        in_specs=[cur, prev, cur, prev, cur, _full(bias.shape), ANY_SPEC],
        out_specs=[cur] * 2,
        scratch_shapes=[pltpu.VMEM((2, N_HEADS, WIN, 2 * WIN), F32), pltpu.VMEM((2, N_HEADS, WIN, 2 * WIN), BF16)],
        compiler_params=_cparams(2),
    )(qc, kc, kc, vc, vc, bias, dep)


def _mix_out(branches, gb, gc, xi, x, w_sc, g_a, g_c, w_out):
    s = x.shape[0]
    tb = TM // SUBLANES

    def body(o1, l1, o4, l4, o16, l16, gb_ref, gc_ref, xi_ref, gch_ref, xih_ref, x_ref, wsc_ref,
             ga_ref, gcv_ref, wout_ref, attn_ref, lse1, lse4, lse16, mixed_ref, x1_ref, scr_a, scr_b, scr_c, scr_d):
        i = pl.program_id(0)
        la, lb, lc = l1[...], _gather_classes(l4, scr_a, 4), _gather_classes(l16, scr_b, 16)
        m_all = jnp.maximum(jnp.maximum(la, lb), lc)
        ea, eb, ec = jnp.exp(la - m_all), jnp.exp(lb - m_all), jnp.exp(lc - m_all)
        den = (ea + eb) + ec
        num = (ea * o1[...] + eb * _gather_classes(o4, scr_c, 4)) + ec * _gather_classes(o16, scr_d, 16)
        attn = num / den
        attn_ref[...] = attn
        _spread(m_all + jnp.log(den), scr_a, (lse1, lse4, lse16), F32)
        xa, _ = _rms(attn)
        u = gc_ref[...] * xi_ref[...]
        uh = jnp.where(i > 0, gch_ref[...] * xih_ref[...], 0.0)
        conv = gb_ref[...] * _causal_conv3(u, uh, wsc_ref)
        xc, _ = _rms(conv)
        mixed = jnp.concatenate([xa * ga_ref[...], xc * gcv_ref[...]], axis=1).astype(BF16)
        mixed_ref[...] = mixed
        x1_ref[...] = x_ref[...] + _dot(mixed, wout_ref[...])

    row = lambda n: pl.BlockSpec((TM, n), lambda i: (i, 0))
    halo = pl.BlockSpec((SUBLANES, 512), lambda i: (jnp.maximum(i * tb - 1, 0), 0))
    cs = _class_specs(512)
    flat = [a for br in branches for a in br]
    res = pl.pallas_call(
        body, name="mix_out", grid=(s // TM,),
        out_shape=[jax.ShapeDtypeStruct((s, 512), F32)] + _class_shapes(s, 512, F32)
        + [jax.ShapeDtypeStruct((s, D_MODEL), BF16), jax.ShapeDtypeStruct((s, D_MODEL), F32)],
        in_specs=[cs[0], cs[0], cs[1], cs[1], cs[2], cs[2], row(512), row(512), row(512), halo, halo,
                  row(D_MODEL), _full(w_sc.shape), _full(g_a.shape), _full(g_c.shape), _full(w_out.shape)],
        out_specs=[row(512)] + cs + [row(D_MODEL), row(D_MODEL)],
        scratch_shapes=[pltpu.VMEM((512 // LANES, TM, LANES), F32)] * 4,
        compiler_params=_cparams(1),
    )(*flat, gb, gc, xi, gc, xi, x, w_sc, g_a, g_c, w_out)
    return res[0], res[1:4], res[4], res[5]


def _mem_kv(mem, g_mem, w_xk, w_xv):
    def body(mem_ref, g_ref, wk_ref, wv_ref, mn_ref, k_ref, v_ref):
        xh, _ = _rms(mem_ref[...])
        mn = (xh * g_ref[...]).astype(BF16)
        mn_ref[...] = mn
        k_ref[...] = _dot(mn, wk_ref[...]).astype(BF16)
        v_ref[...] = _dot(mn, wv_ref[...]).astype(BF16)

    vm = pl.BlockSpec(memory_space=pltpu.VMEM)
    return pl.pallas_call(
        body, name="mem_kv",
        out_shape=[jax.ShapeDtypeStruct(mem.shape, BF16)] * 3,
        in_specs=[vm] * 4, out_specs=[vm] * 3,
        compiler_params=pltpu.CompilerParams(vmem_limit_bytes=VMEM_LIMIT),
    )(mem, g_mem, w_xk, w_xv)


def _xattn_fwd(x1, g, w_xq, k, v, w_xo, dep):
    s = x1.shape[0]

    def body(x1_ref, g_ref, wq_ref, k_ref, v_ref, wo_ref, dep_ref, h2_ref, q_ref, o_ref, x2_ref):
        x1v = x1_ref[...]
        xh, _ = _rms(x1v)
        h2 = (xh * g_ref[...]).astype(BF16)
        h2_ref[...] = h2
        qb = _dot(h2, wq_ref[...]).astype(BF16)
        q_ref[...] = qb
        outs = []
        for h in range(N_MEM_HEADS):
            sl = slice(h * MEM_HEAD_DIM, (h + 1) * MEM_HEAD_DIM)
            lg = _dot_nt(qb[:, sl], k_ref[:, sl]) * (MEM_HEAD_DIM ** -0.5)
            p = jnp.exp(lg - jnp.max(lg, axis=-1, keepdims=True))
            p = p / jnp.sum(p, axis=-1, keepdims=True)
            outs.append(_dot(p.astype(BF16), v_ref[:, sl]))
        o = jnp.concatenate(outs, axis=1).astype(BF16)
        o_ref[...] = o
        x2_ref[...] = x1v + _dot(o, wo_ref[...])

    row = pl.BlockSpec((TM, D_MODEL), lambda i: (i, 0))
    return pl.pallas_call(
        body, name="xattn_fwd", grid=(s // TM,),
        out_shape=[jax.ShapeDtypeStruct((s, D_MODEL), BF16)] * 3 + [jax.ShapeDtypeStruct((s, D_MODEL), F32)],
        in_specs=[row, _full(g.shape), _full(w_xq.shape), _full(k.shape), _full(v.shape), _full(w_xo.shape), ANY_SPEC],
        out_specs=[row] * 4,
        compiler_params=_cparams(1),
    )(x1, g, w_xq, k, v, w_xo, dep)


def _ffn_conv(h_ext, wup_ref, wfc_ref, bfc_ref, j):
    u = _dot_nt(h_ext, wup_ref[j])
    w = wfc_ref[j]
    c = ((pltpu.roll(u, 2, 0) * w[0:1, :] + pltpu.roll(u, 1, 0) * w[1:2, :]) + u * w[2:3, :]) + bfc_ref[j]
    return c[HALO:]


def _ffn_fwd(x2, g, w_up_g, w_fc, b_fc, w_down_g, g_final, target):
    s = x2.shape[0]
    tb = TM_FFN // HALO
    half = N_DEV // 2

    def body(x_ref, xp_ref, g_ref, wup_ref, wfc_ref, bfc_ref, wd_ref, gf_ref, t_ref, h_ref, c_ref, act_ref, dx3_ref,
             loss_ref, dgf_ref):
        i = pl.program_id(0)

        @pl.when(i == 0)
        def _():
            loss_ref[...] = jnp.zeros_like(loss_ref)
            dgf_ref[...] = jnp.zeros_like(dgf_ref)

        x2v = x_ref[...]
        gv = g_ref[...]
        h = (_rms(x2v)[0] * gv).astype(BF16)
        h_ref[...] = h
        hp = jnp.where(i > 0, _rms(xp_ref[...])[0] * gv, 0.0).astype(BF16)
        h_ext = jnp.concatenate([hp, h], axis=0)
        down = jnp.zeros((TM_FFN, D_MODEL), F32)
        for j in range(half):
            cg = _ffn_conv(h_ext, wup_ref, wfc_ref, bfc_ref, j)
            cv = _ffn_conv(h_ext, wup_ref, wfc_ref, bfc_ref, j + half)
            c_ref[j] = cg
            c_ref[j + half] = cv
            a = ((cg * _sigmoid(cg)) * cv).astype(BF16)
            act_ref[j] = a
            down = down + _dot(a, wd_ref[j])
        x3 = x2v + down
        xh, r = _rms(x3)
        gf = gf_ref[...]
        e = xh * gf - t_ref[...]
        loss_ref[...] += 0.5 * jnp.sum(jnp.sum(e * e, axis=1, keepdims=True), axis=0, keepdims=True) / D_MODEL
        dy = e * (1.0 / D_MODEL)
        dgf_ref[0:1, :] += jnp.sum(dy * xh, axis=0, keepdims=True)
        dx3_ref[...] = _rms_bwd(xh, r, gf, dy)

    row = pl.BlockSpec((TM_FFN, D_MODEL), lambda i: (i, 0))
    prev = pl.BlockSpec((HALO, D_MODEL), lambda i: (jnp.maximum(i * tb - 1, 0), 0))
    return pl.pallas_call(
        body, name="ffn_fwd", grid=(s // TM_FFN,),
        out_shape=[jax.ShapeDtypeStruct((s, D_MODEL), BF16), jax.ShapeDtypeStruct((N_DEV, s, UP_CHUNK), F32),
                   jax.ShapeDtypeStruct((half, s, UP_CHUNK), BF16),
                   jax.ShapeDtypeStruct((s, D_MODEL), F32), jax.ShapeDtypeStruct((SUBLANES, 128), F32),
                   jax.ShapeDtypeStruct((SUBLANES, D_MODEL), F32)],
        in_specs=[row, prev, _full(g.shape), _resident(w_up_g.shape), _full(w_fc.shape), _full(b_fc.shape),
                  _resident(w_down_g.shape), _full(g_final.shape), row],
        out_specs=[row, pl.BlockSpec((N_DEV, TM_FFN, UP_CHUNK), lambda i: (0, i, 0)),
                   pl.BlockSpec((half, TM_FFN, UP_CHUNK), lambda i: (0, i, 0)), row,
                   _full((SUBLANES, 128)), _full((SUBLANES, D_MODEL))],
        compiler_params=_cparams(1),
    )(x2, x2, g, w_up_g, w_fc, b_fc, w_down_g, g_final, target)


def _ffn_bwd(dx3, h3, conv, x2, g, w_up_g, w_fc, w_down_g):
    s = x2.shape[0]
    tb = TM_FFN // HALO
    last = s // HALO - 1
    n_tiles = s // TM_FFN
    half = N_DEV // 2
    n_ext = TM_FFN + HALO

    def body(dx_ref, dxn_ref, h_ref, c_ref, cn_ref, x2_ref, g_ref, wup_ref, wfc_ref, wd_ref,
             dup_ref, dx2_ref, dg_ref, dwfc_ref, dbfc_ref):
        i = pl.program_id(0)

        @pl.when(i == 0)
        def _():
            dg_ref[...] = jnp.zeros_like(dg_ref)
            dwfc_ref[...] = jnp.zeros_like(dwfc_ref)
            dbfc_ref[...] = jnp.zeros_like(dbfc_ref)

        dxv = dx_ref[...]
        dxn = jnp.where(i < n_tiles - 1, dxn_ref[...], 0.0)
        dx_ext = jnp.concatenate([dxv, dxn], axis=0).astype(BF16)
        h = h_ref[...]
        dh = jnp.zeros((TM_FFN, D_MODEL), F32)
        for j in range(half):
            cg = jnp.concatenate([c_ref[j], cn_ref[j]], axis=0)
            cv = jnp.concatenate([c_ref[j + half], cn_ref[j + half]], axis=0)
            dact = _dot_nt(dx_ext, wd_ref[j])
            sg = _sigmoid(cg)
            parts = ((j + half, dact * (cg * sg)), (j, (dact * cv) * (sg * (1.0 + cg * (1.0 - sg)))))
            for jj, dc in parts:
                u = _dot_nt(h, wup_ref[jj])
                dc0, dc1, dc2 = dc[:TM_FFN], pltpu.roll(dc, n_ext - 1, 0)[:TM_FFN], pltpu.roll(dc, n_ext - 2, 0)[:TM_FFN]
                dbfc_ref[jj:jj + 1, :] += jnp.sum(dc0, axis=0, keepdims=True)
                dwfc_ref[0, jj:jj + 1, :] += jnp.sum(dc2 * u, axis=0, keepdims=True)
                dwfc_ref[1, jj:jj + 1, :] += jnp.sum(dc1 * u, axis=0, keepdims=True)
                dwfc_ref[2, jj:jj + 1, :] += jnp.sum(dc0 * u, axis=0, keepdims=True)
                w = wfc_ref[jj]
                du = ((dc0 * w[2:3, :] + dc1 * w[1:2, :]) + dc2 * w[0:1, :]).astype(BF16)
                dup_ref[jj] = du
                dh = dh + _dot(du, wup_ref[jj])
        xh, r = _rms(x2_ref[...])
        dg_ref[0:1, :] += jnp.sum(dh * xh, axis=0, keepdims=True)
        dx2_ref[...] = dxv + _rms_bwd(xh, r, g_ref[...], dh)

    row = pl.BlockSpec((TM_FFN, D_MODEL), lambda i: (i, 0))
    nxt = pl.BlockSpec((HALO, D_MODEL), lambda i: (jnp.minimum((i + 1) * tb, last), 0))
    cur_c = pl.BlockSpec((N_DEV, TM_FFN, UP_CHUNK), lambda i: (0, i, 0))
    nxt_c = pl.BlockSpec((N_DEV, HALO, UP_CHUNK), lambda i: (0, jnp.minimum((i + 1) * tb, last), 0))
    return pl.pallas_call(
        body, name="ffn_bwd", grid=(n_tiles,),
        out_shape=[jax.ShapeDtypeStruct((N_DEV, s, UP_CHUNK), BF16), jax.ShapeDtypeStruct((s, D_MODEL), F32),
                   jax.ShapeDtypeStruct((SUBLANES, D_MODEL), F32), jax.ShapeDtypeStruct((3, N_DEV, UP_CHUNK), F32),
                   jax.ShapeDtypeStruct((N_DEV, UP_CHUNK), F32)],
        in_specs=[row, nxt, row, cur_c, nxt_c, row, _full(g.shape), _resident(w_up_g.shape), _full(w_fc.shape),
                  _resident(w_down_g.shape)],
        out_specs=[cur_c, row, _full((SUBLANES, D_MODEL)), _full((3, N_DEV, UP_CHUNK)), _full((N_DEV, UP_CHUNK))],
        compiler_params=_cparams(1),
    )(dx3, dx3, h3, conv, conv, x2, g, w_up_g, w_fc, w_down_g)


def _xattn_bwd(dx2, o, q, k, v, w_xo, w_xq, x1, g, dep):
    s = x1.shape[0]

    def body(dx2_ref, o_ref, q_ref, k_ref, v_ref, wo_ref, wq_ref, x1_ref, g_ref, dep_ref, dq_ref, dx1_ref, dk_ref,
             dv_ref, dg_ref):
        @pl.when(pl.program_id(0) == 0)
        def _():
            dk_ref[...] = jnp.zeros_like(dk_ref)
            dv_ref[...] = jnp.zeros_like(dv_ref)
            dg_ref[...] = jnp.zeros_like(dg_ref)

        dx2v = dx2_ref[...]
        do = _dot_nt(dx2v.astype(BF16), wo_ref[...])
        dqs = []
        for h in range(N_MEM_HEADS):
            sl = slice(h * MEM_HEAD_DIM, (h + 1) * MEM_HEAD_DIM)
            qh, kh, vh = q_ref[:, sl], k_ref[:, sl], v_ref[:, sl]
            lg = _dot_nt(qh, kh) * (MEM_HEAD_DIM ** -0.5)
            p = jnp.exp(lg - jnp.max(lg, axis=-1, keepdims=True))
            p = p / jnp.sum(p, axis=-1, keepdims=True)
            doh = do[:, sl].astype(BF16)
            dp = _dot_nt(doh, vh)
            ds = (p * (dp - jnp.sum(p * dp, axis=-1, keepdims=True)) * (MEM_HEAD_DIM ** -0.5)).astype(BF16)
            dqs.append(_dot(ds, kh))
            dk_ref[:, sl] += _dot_tn(ds, qh)
            dv_ref[:, sl] += _dot_tn(p.astype(BF16), doh)
        dq = jnp.concatenate(dqs, axis=1).astype(BF16)
        dq_ref[...] = dq
        dh2 = _dot_nt(dq, wq_ref[...])
        xh, r = _rms(x1_ref[...])
        dg_ref[0:1, :] += jnp.sum(dh2 * xh, axis=0, keepdims=True)
        dx1_ref[...] = dx2v + _rms_bwd(xh, r, g_ref[...], dh2)

    row = pl.BlockSpec((TM, D_MODEL), lambda i: (i, 0))
    return pl.pallas_call(
        body, name="xattn_bwd", grid=(s // TM,),
        out_shape=[jax.ShapeDtypeStruct((s, D_MODEL), BF16), jax.ShapeDtypeStruct((s, D_MODEL), F32),
                   jax.ShapeDtypeStruct(k.shape, F32), jax.ShapeDtypeStruct(k.shape, F32),
                   jax.ShapeDtypeStruct((SUBLANES, D_MODEL), F32)],
        in_specs=[row, row, row, _full(k.shape), _full(v.shape), _full(w_xo.shape), _full(w_xq.shape), row,
                  _full(g.shape), ANY_SPEC],
        out_specs=[row, row, _full(k.shape), _full(k.shape), _full((SUBLANES, D_MODEL))],
        compiler_params=_cparams(1),
    )(dx2, o, q, k, v, w_xo, w_xq, x1, g, dep)


def _mem_kv_bwd(dk, dv, mem_n, mem, w_xk, w_xv):
    def body(dk_ref, dv_ref, mn_ref, mem_ref, wk_ref, wv_ref, dwk_ref, dwv_ref, dg_ref):
        dkb, dvb = dk_ref[...].astype(BF16), dv_ref[...].astype(BF16)
        mn = mn_ref[...]
        dwk_ref[...] = _dot_tn(mn, dkb).astype(BF16)
        dwv_ref[...] = _dot_tn(mn, dvb).astype(BF16)
        dmn = _dot_nt(dkb, wk_ref[...]) + _dot_nt(dvb, wv_ref[...])
        xh, _ = _rms(mem_ref[...])
        dg_ref[...] = jnp.zeros_like(dg_ref)
        dg_ref[0:1, :] = jnp.sum(dmn * xh, axis=0, keepdims=True)

    vm = pl.BlockSpec(memory_space=pltpu.VMEM)
    return pl.pallas_call(
        body, name="mem_kv_bwd",
        out_shape=[jax.ShapeDtypeStruct(w_xk.shape, BF16), jax.ShapeDtypeStruct(w_xv.shape, BF16),
                   jax.ShapeDtypeStruct((SUBLANES, D_MODEL), F32)],
        in_specs=[vm] * 6, out_specs=[vm] * 3,
        compiler_params=pltpu.CompilerParams(vmem_limit_bytes=VMEM_LIMIT),
    )(dk, dv, mem_n, mem, w_xk, w_xv)


def _mix_out_bwd(dx1, w_out, attn, gb, gc, xi, w_sc, g_a, g_c, dep):
    s = dx1.shape[0]
    tb = TM // SUBLANES

    def body(dx1_ref, wout_ref, attn_ref, gb_ref, gc_ref, xi_ref, gch_ref, xih_ref, wsc_ref, ga_ref, gcv_ref, dep_ref,
             da1, da4, da16, dd1, dd4, dd16, dgb_ref, dcv_ref, dga_ref, dgc_ref, dwsc_ref, scr):
        i = pl.program_id(0)

        @pl.when(i == 0)
        def _():
            dga_ref[...] = jnp.zeros_like(dga_ref)
            dgc_ref[...] = jnp.zeros_like(dgc_ref)
            dwsc_ref[...] = jnp.zeros_like(dwsc_ref)

        dmixed = _dot_nt(dx1_ref[...].astype(BF16), wout_ref[...])
        da, dcn = dmixed[:, :ATTN_W], dmixed[:, ATTN_W:]
        attn = attn_ref[...]
        xa, ra = _rms(attn)
        dga_ref[0:1, :] += jnp.sum(da * xa, axis=0, keepdims=True)
        dattn = _rms_bwd(xa, ra, ga_ref[...], da)
        _spread(dattn, scr, (da1, da4, da16), BF16)
        prod = dattn * attn
        dd = jnp.concatenate(
            [jnp.broadcast_to(jnp.sum(prod[:, h * HEAD_DIM:(h + 1) * HEAD_DIM], axis=-1, keepdims=True),
                              (TM, HEAD_DIM)) for h in range(N_HEADS)], axis=1)
        _spread(dd, scr, (dd1, dd4, dd16), F32)
        gbv = gb_ref[...]
        u = gc_ref[...] * xi_ref[...]
        uh = jnp.where(i > 0, gch_ref[...] * xih_ref[...], 0.0)
        u2, u1 = _shift_down(u, uh, 2), _shift_down(u, uh, 1)
        cv = (u2 * wsc_ref[0:1, :] + u1 * wsc_ref[1:2, :]) + u * wsc_ref[2:3, :]
        xc, rc = _rms(gbv * cv)
        dgc_ref[0:1, :] += jnp.sum(dcn * xc, axis=0, keepdims=True)
        dconv = _rms_bwd(xc, rc, gcv_ref[...], dcn)
        dgb_ref[...] = (dconv * cv).astype(BF16)
        dcv = dconv * gbv
        dcv_ref[...] = dcv
        dwsc_ref[0:1, :] += jnp.sum(dcv * u2, axis=0, keepdims=True)
        dwsc_ref[1:2, :] += jnp.sum(dcv * u1, axis=0, keepdims=True)
        dwsc_ref[2:3, :] += jnp.sum(dcv * u, axis=0, keepdims=True)

    row = lambda n: pl.BlockSpec((TM, n), lambda i: (i, 0))
    halo = pl.BlockSpec((SUBLANES, 512), lambda i: (jnp.maximum(i * tb - 1, 0), 0))
    acc = _full((SUBLANES, 512))
    res = pl.pallas_call(
        body, name="mix_out_bwd", grid=(s // TM,),
        out_shape=_class_shapes(s, 512, BF16) + _class_shapes(s, 512, F32)
        + [jax.ShapeDtypeStruct((s, 512), BF16), jax.ShapeDtypeStruct((s, 512), F32)]
        + [jax.ShapeDtypeStruct((SUBLANES, 512), F32)] * 3,
        in_specs=[row(D_MODEL), _full(w_out.shape), row(512), row(512), row(512), row(512), halo, halo,
                  _full(w_sc.shape), _full(g_a.shape), _full(g_c.shape), ANY_SPEC],
        out_specs=_class_specs(512) * 2 + [row(512)] * 2 + [acc] * 3,
        scratch_shapes=[pltpu.VMEM((512 // LANES, TM, LANES), F32)],
        compiler_params=_cparams(1),
    )(dx1, w_out, attn, gb, gc, xi, gc, xi, w_sc, g_a, g_c, dep)
    return res[0:3], res[3:6], res[6], res[7], res[8], res[9], res[10]


def _swa_bwd(qc, kc, vc, doc, lsec, ddc, bias, dil, dep):
    n128 = qc.shape[1] // WIN
    nb = n128 // 2

    def body(q_ref, qn_ref, kp_ref, kc_ref, vp_ref, vc_ref, do_ref, don_ref, lse_ref, lsen_ref, dd_ref, ddn_ref,
             b_ref, dep_ref, dq_ref, dk_ref, dv_ref, db_ref, s_scr, dp_scr, sn_scr, dpn_scr, ds_scr, p_scr, dsn_scr,
             pn_scr):
        r, b = pl.program_id(0), pl.program_id(1)

        @pl.when((r == 0) & (b == 0))
        def _():
            db_ref[...] = jnp.zeros_like(db_ref)

        pairs = [slice(a * LANES, (a + 1) * LANES) for a in range(N_HEADS // 2)]
        blk_a, blk_b = slice(0, WIN), slice(WIN, 2 * WIN)
        per_head = lambda ref, rows: jnp.stack([ref[0, rows, h * HEAD_DIM:h * HEAD_DIM + 1] for h in range(N_HEADS)])
        for a, sl in enumerate(pairs):
            k_pa = jnp.concatenate([kp_ref[0, :, sl], kc_ref[0, blk_a, sl]], axis=0)
            v_pa = jnp.concatenate([vp_ref[0, :, sl], vc_ref[0, blk_a, sl]], axis=0)
            for sub, (rows, k2, v2) in enumerate(((blk_a, k_pa, v_pa), (blk_b, kc_ref[0, :, sl], vc_ref[0, :, sl]))):
                q_eo = _pair_split(q_ref[0, rows, sl])
                do_eo = _pair_split(do_ref[0, rows, sl].astype(BF16))
                for e in range(2):
                    s_scr[sub, 2 * a + e] = _dot_nt(q_eo[e], k2)
                    dp_scr[sub, 2 * a + e] = _dot_nt(do_eo[e], v2)
            qn_eo = _pair_split(qn_ref[0, :, sl])
            don_eo = _pair_split(don_ref[0, :, sl].astype(BF16))
            for e in range(2):
                sn_scr[2 * a + e] = _dot_nt(qn_eo[e], kc_ref[0, blk_b, sl])
                dpn_scr[2 * a + e] = _dot_nt(don_eo[e], vc_ref[0, blk_b, sl])
        bias = b_ref[...]
        for sub, rows in enumerate((blk_a, blk_b)):
            first = (b == 0) if sub == 0 else False
            p = jnp.exp(jnp.where(_band_mask(first), s_scr[sub] + bias, -jnp.inf) - per_head(lse_ref, rows))
            ds = p * (dp_scr[sub] - per_head(dd_ref, rows))
            db_ref[...] += ds
            ds_scr[sub] = ds.astype(BF16)
            p_scr[sub] = p.astype(BF16)
        qi = lax.broadcasted_iota(jnp.int32, (WIN, WIN), 0)
        kj = lax.broadcasted_iota(jnp.int32, (WIN, WIN), 1)
        valid_n = kj >= qi + jnp.where(b + 1 < nb, 0, WIN)
        every = slice(0, WIN)
        pn = jnp.exp(jnp.where(valid_n, sn_scr[...] + bias[:, :, :WIN], -jnp.inf) - per_head(lsen_ref, every))
        dsn_scr[...] = (pn * (dpn_scr[...] - per_head(ddn_ref, every))).astype(BF16)
        pn_scr[...] = pn.astype(BF16)
        for a, sl in enumerate(pairs):
            k_pa = _pair_split(jnp.concatenate([kp_ref[0, :, sl], kc_ref[0, blk_a, sl]], axis=0))
            k_ab = _pair_split(kc_ref[0, :, sl])
            qa_eo, qb_eo = _pair_split(q_ref[0, blk_a, sl]), _pair_split(q_ref[0, blk_b, sl])
            doa_eo = _pair_split(do_ref[0, blk_a, sl].astype(BF16))
            dob_eo = _pair_split(do_ref[0, blk_b, sl].astype(BF16))
            qn_eo = _pair_split(qn_ref[0, :, sl])
            don_eo = _pair_split(don_ref[0, :, sl].astype(BF16))
            acc = None
            for e in range(2):
                h = 2 * a + e
                terms = (_dot(ds_scr[0, h], k_pa[e]),
                         _dot(ds_scr[1, h], k_ab[e]),
                         _dot_tn(ds_scr[0, h, :, WIN:], qa_eo[e]) + _dot_tn(ds_scr[1, h, :, :WIN], qb_eo[e]),
                         _dot_tn(ds_scr[1, h, :, WIN:], qb_eo[e]) + _dot_tn(dsn_scr[h], qn_eo[e]),
                         _dot_tn(p_scr[0, h, :, WIN:], doa_eo[e]) + _dot_tn(p_scr[1, h, :, :WIN], dob_eo[e]),
                         _dot_tn(p_scr[1, h, :, WIN:], dob_eo[e]) + _dot_tn(pn_scr[h], don_eo[e]))
                acc = terms if acc is None else tuple(x + y for x, y in zip(acc, terms))
            acc = [t.astype(BF16) for t in acc]
            dq_ref[0, blk_a, sl], dq_ref[0, blk_b, sl] = acc[0], acc[1]
            dk_ref[0, blk_a, sl], dk_ref[0, blk_b, sl] = acc[2], acc[3]
            dv_ref[0, blk_a, sl], dv_ref[0, blk_b, sl] = acc[4], acc[5]

    cur = pl.BlockSpec((1, 2 * WIN, 512), lambda r, b: (r, b, 0))
    prev = pl.BlockSpec((1, WIN, 512), lambda r, b: (r, jnp.maximum(2 * b - 1, 0), 0))
    nxt = pl.BlockSpec((1, WIN, 512), lambda r, b: (r, jnp.minimum(2 * b + 2, n128 - 1), 0))
    wide, narrow = (2, N_HEADS, WIN, 2 * WIN), (N_HEADS, WIN, WIN)
    return pl.pallas_call(
        body, name=f"swa_bwd_d{dil}", grid=(dil, nb),
        out_shape=[jax.ShapeDtypeStruct(qc.shape, BF16)] * 3 + [jax.ShapeDtypeStruct(bias.shape, F32)],
        in_specs=[cur, nxt, prev, cur, prev, cur, cur, nxt, cur, nxt, cur, nxt, _full(bias.shape), ANY_SPEC],
        out_specs=[cur] * 3 + [_full(bias.shape)],
        scratch_shapes=[pltpu.VMEM(wide, F32), pltpu.VMEM(wide, F32), pltpu.VMEM(narrow, F32),
                        pltpu.VMEM(narrow, F32), pltpu.VMEM(wide, BF16), pltpu.VMEM(wide, BF16),
                        pltpu.VMEM(narrow, BF16), pltpu.VMEM(narrow, BF16)],
        compiler_params=_cparams(2),
    )(qc, qc, kc, kc, vc, vc, doc, doc, lsec, lsec, ddc, ddc, bias, dep)


def _in_proj_bwd(dqs, dks, dvs, dgb, dcv, gc, xi, w_sc, w_in_g, x, g_mix, dx1):
    s = x.shape[0]
    tb = TM // SUBLANES
    last = s // SUBLANES - 1
    n_tiles = s // TM

    def body(dq1, dq4, dq16, dk1, dk4, dk16, dv1, dv4, dv16, dgb_ref, dcv_ref, dcvn_ref, gc_ref, xi_ref, wsc_ref,
             win_ref, x_ref, g_ref, dx1_ref, dproj_ref, gx_ref, dg_ref, scr_a, scr_b):
        i = pl.program_id(0)

        @pl.when(i == 0)
        def _():
            dg_ref[...] = jnp.zeros_like(dg_ref)

        d0 = dcv_ref[...]
        dn = jnp.where(i < n_tiles - 1, dcvn_ref[...], 0.0)
        du = (d0 * wsc_ref[2:3, :] + _shift_up(d0, dn, 1) * wsc_ref[1:2, :]) + _shift_up(d0, dn, 2) * wsc_ref[0:1, :]
        merge = lambda a, b4, b16: ((a[...].astype(F32) + _gather_classes(b4, scr_a, 4))
                                    + _gather_classes(b16, scr_b, 16))
        dq = merge(dq1, dq4, dq16) * (HEAD_DIM ** -0.5)
        dk = merge(dk1, dk4, dk16)
        dv = merge(dv1, dv4, dv16)
        dproj = jnp.concatenate([dq, dk, dv, dgb_ref[...].astype(F32), du * xi_ref[...], du * gc_ref[...]],
                                axis=1).astype(BF16)
        dproj_ref[...] = dproj
        dh = jnp.zeros((TM, D_MODEL), F32)
        for j in range(N_DEV):
            dh = dh + _dot_nt(dproj[:, j * IN_CHUNK:(j + 1) * IN_CHUNK], win_ref[j])
        xh, r = _rms(x_ref[...])
        dg_ref[0:1, :] += jnp.sum(dh * xh, axis=0, keepdims=True)
        gx_ref[...] = dx1_ref[...] + _rms_bwd(xh, r, g_ref[...], dh)

    row = lambda n: pl.BlockSpec((TM, n), lambda i: (i, 0))
    nxt = pl.BlockSpec((SUBLANES, 512), lambda i: (jnp.minimum((i + 1) * tb, last), 0))
    return pl.pallas_call(
        body, name="in_proj_bwd", grid=(n_tiles,),
        out_shape=[jax.ShapeDtypeStruct((s, IN_COLS), BF16), jax.ShapeDtypeStruct((s, D_MODEL), F32),
                   jax.ShapeDtypeStruct((SUBLANES, D_MODEL), F32)],
        in_specs=_class_specs(512) * 3 + [row(512), row(512), nxt, row(512), row(512), _full(w_sc.shape),
                                          _full(w_in_g.shape), row(D_MODEL), _full(g_mix.shape), row(D_MODEL)],
        out_specs=[row(IN_COLS), row(D_MODEL), _full((SUBLANES, D_MODEL))],
        scratch_shapes=[pltpu.VMEM((512 // LANES, TM, LANES), F32)] * 2,
        compiler_params=_cparams(1),
    )(*dqs, *dks, *dvs, dgb, dcv, dcv, gc, xi, w_sc, w_in_g, x, g_mix, dx1)


def _dw(a, b, dep, name, a_chunked=False, b_chunked=False, n_chunks=1, chunk_cols=None):
    ts = TS_DW if (a_chunked or b_chunked or chunk_cols) else TS_DW // 2
    if a_chunked:
        nj, s, kk = a.shape
        nn = b.shape[1]
        a_spec = pl.BlockSpec((1, ts, kk), lambda j, t: (j, t, 0))
        b_spec = pl.BlockSpec((ts, nn), lambda j, t: (t, 0))
    elif b_chunked:
        nj, s, nn = b.shape
        kk = a.shape[1]
        a_spec = pl.BlockSpec((ts, kk), lambda j, t: (t, 0))
        b_spec = pl.BlockSpec((1, ts, nn), lambda j, t: (j, t, 0))
    else:
        s, kk = a.shape
        nj, nn = (n_chunks, chunk_cols) if chunk_cols else (1, b.shape[1])
        a_spec = pl.BlockSpec((ts, kk), lambda j, t: (t, 0))
        b_spec = pl.BlockSpec((ts, nn), lambda j, t: (t, j))
    n_steps = s // ts

    def body(a_ref, b_ref, dep_ref, o_ref, acc):
        t = pl.program_id(1)

        @pl.when(t == 0)
        def _():
            acc[...] = jnp.zeros_like(acc)

        av = (a_ref[0] if a_chunked else a_ref[...]).astype(BF16)
        bv = (b_ref[0] if b_chunked else b_ref[...]).astype(BF16)
        acc[...] += _dot_tn(av, bv)

        @pl.when(t == n_steps - 1)
        def _():
            o_ref[0] = acc[...].astype(BF16)

    return pl.pallas_call(
        body, name=name, grid=(nj, n_steps),
        out_shape=jax.ShapeDtypeStruct((nj, kk, nn), BF16),
        in_specs=[a_spec, b_spec, ANY_SPEC],
        out_specs=pl.BlockSpec((1, kk, nn), lambda j, t: (j, 0, 0)),
        scratch_shapes=[pltpu.VMEM((kk, nn), F32)],
        compiler_params=_cparams(2),
    )(a, b, dep)


def _adamw_math(w, g, m, v):
    m2 = ADAM_B1 * m + (1.0 - ADAM_B1) * g
    v2 = ADAM_B2 * v + (1.0 - ADAM_B2) * (g * g)
    m_hat = m2 / (1.0 - ADAM_B1 ** ADAM_STEP)
    v_hat = v2 / (1.0 - ADAM_B2 ** ADAM_STEP)
    delta = -ADAM_LR * (m_hat / (jnp.sqrt(v_hat) + ADAM_EPS) + ADAM_WD * w)
    return delta, m2, v2


def _sum_parts(me, own, p_ref):
    g = None
    for i in range(N_DEV):
        part = jnp.where(me == i, own.astype(F32), p_ref[i].astype(F32))
        g = part if g is None else g + part
    return g


def _adamw_big(name, w, sent, parts, m, v, me_arr):
    rr, cc = w.shape
    tr = rr // 4 if rr >= 512 else rr

    def body(me_ref, w_ref, own_ref, p_ref, m_ref, v_ref, g_ref, d_ref, nm_ref, nv_ref):
        g = own_ref[0].astype(F32)
        for k in range(1, N_DEV):
            g = g + p_ref[(me_ref[0] + k) % N_DEV].astype(F32)
        g_ref[...] = g
        d_ref[...], nm_ref[...], nv_ref[...] = _adamw_math(w_ref[...], g, m_ref[...], v_ref[...])

    row = pl.BlockSpec((tr, cc), lambda i, me: (i, 0))
    return pl.pallas_call(
        body, name=name,
        grid_spec=pltpu.PrefetchScalarGridSpec(
            num_scalar_prefetch=1, grid=(rr // tr,),
            in_specs=[row, pl.BlockSpec((1, tr, cc), lambda i, me: (me[0], i, 0)),
                      pl.BlockSpec((N_DEV, tr, cc), lambda i, me: (0, i, 0)), row, row],
            out_specs=[row] * 4),
        out_shape=[jax.ShapeDtypeStruct((rr, cc), F32)] * 4,
        compiler_params=_cparams(1),
    )(me_arr, w, sent, parts, m, v)


def _small_slices():
    return [
        (slice(ROW_RELB, ROW_RELB + 8), slice(0, N_BUCKETS)),
        (slice(ROW_GMIX, ROW_GMIX + 1), slice(0, D_MODEL)),
        (slice(ROW_GAC, ROW_GAC + 1), slice(0, ATTN_W)),
        (slice(ROW_GAC, ROW_GAC + 1), slice(ATTN_W, D_MODEL)),
        (slice(ROW_GXATTN, ROW_GXATTN + 1), slice(0, D_MODEL)),
        (slice(ROW_GMEM, ROW_GMEM + 1), slice(0, D_MODEL)),
        (slice(ROW_GFFN, ROW_GFFN + 1), slice(0, D_MODEL)),
        (slice(ROW_BFC, ROW_BFC + 8), slice(0, UP_CHUNK)),
        (slice(ROW_GFINAL, ROW_GFINAL + 1), slice(0, D_MODEL)),
    ]


def _adamw_small(own, parts, wmv, me_arr):
    slices = _small_slices()
    n = len(slices)

    def body(*refs):
        me_ref, own_ref, p_ref = refs[:3]
        ins = refs[3:3 + 3 * n]
        g_ref = refs[3 + 3 * n]
        outs = refs[4 + 3 * n:]
        g = _sum_parts(me_ref[0], own_ref[...], p_ref)
        g_ref[...] = g
        for a, (rs, ls) in enumerate(slices):
            ga = g[rs, ls]
            outs[4 * a][...] = ga
            outs[4 * a + 1][...], outs[4 * a + 2][...], outs[4 * a + 3][...] = _adamw_math(
                ins[3 * a][...], ga, ins[3 * a + 1][...], ins[3 * a + 2][...])

    vm = pl.BlockSpec(memory_space=pltpu.VMEM)
    flat = [t for trip in wmv for t in trip]
    out_shape = [jax.ShapeDtypeStruct((SMALL_ROWS, D_MODEL), F32)]
    for w, _, _ in wmv:
        out_shape += [jax.ShapeDtypeStruct(w.shape, F32)] * 4
    res = pl.pallas_call(
        body, name="adamw_small", out_shape=out_shape,
        in_specs=[SMEM_SPEC] + [vm] * (2 + 3 * n), out_specs=[vm] * len(out_shape),
    )(me_arr, own, parts, *flat)
    return res[0], [res[1 + 4 * a:5 + 4 * a] for a in range(n)]


def _adamw_shards(items):
    n = len(items)

    def body(*refs):
        for a in range(n):
            w_ref, g_ref, m_ref, v_ref = refs[4 * a:4 * a + 4]
            d_ref, nm_ref, nv_ref = refs[4 * n + 3 * a:4 * n + 3 * a + 3]
            d_ref[...], nm_ref[...], nv_ref[...] = _adamw_math(w_ref[...], g_ref[...], m_ref[...], v_ref[...])

    vm = pl.BlockSpec(memory_space=pltpu.VMEM)
    out_shape = []
    for w, _, _, _ in items:
        out_shape += [jax.ShapeDtypeStruct(w.shape, F32)] * 3
    res = pl.pallas_call(
        body, name="adamw_shards", out_shape=out_shape, in_specs=[vm] * (4 * n), out_specs=[vm] * (3 * n),
    )(*[t for it in items for t in it])
    return [res[3 * a:3 * a + 3] for a in range(n)]


def _mesh_pos():
    return lax.axis_index("x"), lax.axis_index("y"), lax.axis_index("c")


def _dev_index(p):
    return 4 * p[0] + 2 * p[1] + p[2]


def _all_gather(shards):
    n = len(shards)

    def body(*refs):
        ins, outs = refs[:n], refs[n:2 * n]
        send_sems, recv_sems, loc_sems = refs[2 * n:]
        x, y, c = _mesh_pos()
        me, sib = (x, y, c), (x, y, 1 - c)
        chips = [(1 - x, y), (x, 1 - y), (1 - x, 1 - y)]

        def cp(a, k, block, to, src=None):
            dst = outs[a].at[_dev_index(block)]
            return pltpu.make_async_remote_copy(
                src_ref=dst if src is None else src, dst_ref=dst, send_sem=send_sems.at[a, k],
                recv_sem=recv_sems.at[a, k], device_id=to, device_id_type=MESH)

        mine = [pltpu.make_async_copy(ins[a], outs[a].at[_dev_index(me)], loc_sems.at[a]) for a in range(n)]
        for m_ in mine:
            m_.start()
        first = []
        for a in range(n):
            first.append(cp(a, 0, me, sib, src=ins[a]))
            first += [cp(a, 1 + j, me, (*chip, c), src=ins[a]) for j, chip in enumerate(chips)]
        for f in first:
            f.start()
        passed = []
        for a in range(n):
            for j, chip in enumerate(chips):
                cp(a, 1 + j, (*chip, c), me).wait_recv()
                fwd = cp(a, 4 + j, (*chip, c), sib)
                fwd.start()
                passed.append(fwd)
        for a in range(n):
            cp(a, 0, sib, me).wait_recv()
            for j, chip in enumerate(chips):
                cp(a, 4 + j, (*chip, 1 - c), me).wait_recv()
        for f in first + passed:
            f.wait_send()
        for m_ in mine:
            m_.wait()

    hbm = pl.BlockSpec(memory_space=pltpu.HBM)
    return pl.pallas_call(
        body, name="all_gather_weights",
        out_shape=[jax.ShapeDtypeStruct((N_DEV,) + a.shape, a.dtype) for a in shards],
        in_specs=[hbm] * n, out_specs=[hbm] * n,
        scratch_shapes=[pltpu.SemaphoreType.DMA((n, 7)), pltpu.SemaphoreType.DMA((n, 7)),
                        pltpu.SemaphoreType.DMA((n,))],
    )(*shards)


def _peers():
    x, y, c = _mesh_pos()
    return (x, y, c), [((1 - x) if k & 4 else x, (1 - y) if k & 2 else y, (1 - c) if k & 1 else c)
                       for k in range(1, 8)]


def _exchange_copy(src_ref, land_ref, whole, send_sems, recv_sems, a, k, peer, slot):
    src = src_ref if whole else src_ref.at[_dev_index(peer)]
    return pltpu.make_async_remote_copy(
        src_ref=src, dst_ref=land_ref.at[slot], send_sem=send_sems.at[7 * a + k], recv_sem=recv_sems.at[7 * a + k],
        device_id=peer, device_id_type=MESH)


def _exchange_start(name, srcs, whole, dep):
    n = len(srcs)
    lands = [lax.empty(((N_DEV,) + s.shape) if w else s.shape, s.dtype) for s, w in zip(srcs, whole)]

    def body(*refs):
        src_refs, land_refs = refs[:n], refs[n:2 * n]
        send_sems, recv_sems, token = refs[2 * n + 1], refs[2 * n + 2], refs[-1]
        me, peers = _peers()
        for a in range(n):
            for k, peer in enumerate(peers):
                _exchange_copy(src_refs[a], land_refs[a], whole[a], send_sems, recv_sems, a, k, peer,
                               _dev_index(me)).start()
        token[...] = jnp.zeros_like(token)

    res = pl.pallas_call(
        body, name=name,
        out_shape=(pltpu.SemaphoreType.DMA((7 * n,)), pltpu.SemaphoreType.DMA((7 * n,)),
                   *[pltpu.HBM(a.shape, a.dtype) for a in srcs], *[pltpu.HBM(a.shape, a.dtype) for a in lands],
                   jax.ShapeDtypeStruct((SUBLANES, 128), F32)),
        in_specs=[HBM_SPEC] * (2 * n) + [ANY_SPEC],
        out_specs=(SEM_SPEC, SEM_SPEC, *([HBM_SPEC] * (2 * n)), VMEM_SPEC),
        input_output_aliases={i: 2 + i for i in range(2 * n)},
        compiler_params=pltpu.CompilerParams(has_side_effects=DATAFLOW),
    )(*[pltpu.with_memory_space_constraint(a, pltpu.HBM) for a in srcs],
      *[pltpu.with_memory_space_constraint(a, pltpu.HBM) for a in lands], dep)
    return res[0], res[1], list(res[2:2 + n]), list(res[2 + n:2 + 2 * n]), res[-1]


def _exchange_wait(name, started, whole, after, which=None):
    send_sems, recv_sems, srcs, lands, _ = started
    which = list(range(len(srcs))) if which is None else which
    srcs, lands = [srcs[a] for a in which], [lands[a] for a in which]
    n = len(srcs)

    def body(*refs):
        src_refs, land_refs = refs[:n], refs[n:2 * n]
        send_sems, recv_sems = refs[2 * n], refs[2 * n + 1]
        _, peers = _peers()
        for i, a in enumerate(which):
            for k, peer in enumerate(peers):
                cp = _exchange_copy(src_refs[i], land_refs[i], whole[a], send_sems, recv_sems, a, k, peer,
                                    _dev_index(peer))
                cp.wait_send()
                cp.wait_recv()

    res = pl.pallas_call(
        body, name=name,
        out_shape=[pltpu.HBM(a.shape, a.dtype) for a in srcs + lands],
        in_specs=[HBM_SPEC] * (2 * n) + [SEM_SPEC, SEM_SPEC, ANY_SPEC],
        out_specs=[HBM_SPEC] * (2 * n),
        input_output_aliases={i: i for i in range(2 * n)},
        compiler_params=pltpu.CompilerParams(has_side_effects=DATAFLOW),
    )(*srcs, *lands, send_sems, recv_sems, after)
    return list(res[:n]), list(res[n:])


def _gather_start(name, shards, dep):
    n = len(shards)
    lands = [lax.empty((N_DEV,) + a.shape, a.dtype) for a in shards]

    def body(*refs):
        src_refs, land_refs = refs[:n], refs[n:2 * n]
        send_sems, recv_sems, token = refs[2 * n + 1], refs[2 * n + 2], refs[-1]
        x, y, c = _mesh_pos()
        peers = [(x, y, 1 - c), (1 - x, y, c), (x, 1 - y, c), (1 - x, 1 - y, c)]
        for a in range(n):
            for k, peer in enumerate(peers):
                pltpu.make_async_remote_copy(
                    src_ref=src_refs[a], dst_ref=land_refs[a].at[_dev_index((x, y, c))], send_sem=send_sems.at[4 * a + k],
                    recv_sem=recv_sems.at[4 * a + k], device_id=peer, device_id_type=MESH).start()
        token[...] = jnp.zeros_like(token)

    res = pl.pallas_call(
        body, name=name,
        out_shape=(pltpu.SemaphoreType.DMA((4 * n,)), pltpu.SemaphoreType.DMA((4 * n,)),
                   *[pltpu.HBM(a.shape, a.dtype) for a in shards], *[pltpu.HBM(a.shape, a.dtype) for a in lands],
                   jax.ShapeDtypeStruct((SUBLANES, 128), F32)),
        in_specs=[HBM_SPEC] * (2 * n) + [ANY_SPEC],
        out_specs=(SEM_SPEC, SEM_SPEC, *([HBM_SPEC] * (2 * n)), VMEM_SPEC),
        input_output_aliases={i: 2 + i for i in range(2 * n)},
        compiler_params=pltpu.CompilerParams(has_side_effects=DATAFLOW),
    )(*[pltpu.with_memory_space_constraint(a, pltpu.HBM) for a in shards],
      *[pltpu.with_memory_space_constraint(a, pltpu.HBM) for a in lands], dep)
    return res[0], res[1], list(res[2:2 + n]), list(res[2 + n:2 + 2 * n]), res[-1]


def _gather_forward(name, send_sems, recv_sems, lands, which, after):
    n = len(which)

    def body(*refs):
        land_refs = refs[:n]
        send_sems, recv_sems = refs[n], refs[n + 1]
        fsend, frecv, token = refs[n + 3], refs[n + 4], refs[-1]
        x, y, c = _mesh_pos()
        chips = [(1 - x, y), (x, 1 - y), (1 - x, 1 - y)]
        for i, a in enumerate(which):
            for j, chip in enumerate(chips):
                block = land_refs[i].at[_dev_index((*chip, c))]
                pltpu.make_async_remote_copy(
                    src_ref=block, dst_ref=block, send_sem=send_sems.at[4 * a + 1 + j], recv_sem=recv_sems.at[4 * a + 1 + j],
                    device_id=(*chip, c), device_id_type=MESH).wait_recv()
                pltpu.make_async_remote_copy(
                    src_ref=block, dst_ref=block, send_sem=fsend.at[3 * i + j], recv_sem=frecv.at[3 * i + j],
                    device_id=(x, y, 1 - c), device_id_type=MESH).start()
        token[...] = jnp.zeros_like(token)

    res = pl.pallas_call(
        body, name=name,
        out_shape=(pltpu.SemaphoreType.DMA((3 * n,)), pltpu.SemaphoreType.DMA((3 * n,)),
                   *[pltpu.HBM(a.shape, a.dtype) for a in lands], jax.ShapeDtypeStruct((SUBLANES, 128), F32)),
        in_specs=[HBM_SPEC] * n + [SEM_SPEC, SEM_SPEC, ANY_SPEC],
        out_specs=(SEM_SPEC, SEM_SPEC, *([HBM_SPEC] * n), VMEM_SPEC),
        input_output_aliases={i: 2 + i for i in range(n)},
        compiler_params=pltpu.CompilerParams(has_side_effects=DATAFLOW),
    )(*lands, send_sems, recv_sems, after)
    return res[0], res[1], list(res[2:2 + n]), res[-1]


def _gather_wait(name, send_sems, recv_sems, fsend, frecv, srcs, lands, which, after):
    n = len(which)

    def body(*refs):
        land_refs = refs[n:2 * n]
        send_sems, recv_sems, fsend, frecv = refs[2 * n:2 * n + 4]
        x, y, c = _mesh_pos()
        sib = (x, y, 1 - c)
        chips = [(1 - x, y), (x, 1 - y), (1 - x, 1 - y)]
        for i, a in enumerate(which):
            def cp(slot, ssem, rsem):
                block = land_refs[i].at[_dev_index(slot)]
                return pltpu.make_async_remote_copy(src_ref=block, dst_ref=block, send_sem=ssem, recv_sem=rsem,
                                                    device_id=sib, device_id_type=MESH)
            cp(sib, send_sems.at[4 * a], recv_sems.at[4 * a]).wait_recv()
            for j, chip in enumerate(chips):
                cp((*chip, 1 - c), fsend.at[3 * i + j], frecv.at[3 * i + j]).wait_recv()
            for k in range(4):
                cp(sib, send_sems.at[4 * a + k], recv_sems.at[4 * a + k]).wait_send()
            for j in range(3):
                cp(sib, fsend.at[3 * i + j], frecv.at[3 * i + j]).wait_send()

    res = pl.pallas_call(
        body, name=name,
        out_shape=[pltpu.HBM(a.shape, a.dtype) for a in srcs + lands],
        in_specs=[HBM_SPEC] * (2 * n) + [SEM_SPEC] * 4 + [ANY_SPEC],
        out_specs=[HBM_SPEC] * (2 * n),
        input_output_aliases={i: i for i in range(2 * n)},
        compiler_params=pltpu.CompilerParams(has_side_effects=DATAFLOW),
    )(*srcs, *lands, send_sems, recv_sems, fsend, frecv, after)
    return list(res[n:])


def _local_step(x, mem, target, rel_bias, g_mix, w_in_g, w_sc, g_a, g_c, g_xattn, g_mem, g_ffn, w_fc, b_fc, g_final,
                dep, forward_weights, late_weights, emit, emit_small):
    s = x.shape[0]
    buckets = _bucket_tables()
    bias = _bias_fwd(rel_bias, buckets)

    h1, qs, ks, vs, gb, gc, xi = _rms_proj(x, g_mix, w_in_g, dep)
    qs, ks, vs = ([a[0][None]] + list(a[1:]) for a in (qs, ks, vs))
    group1, group2 = ["w_out", "w_xq", "w_xk", "w_xv", "w_xo"], ["w_up", "w_down"]
    tok = forward_weights(group1, h1)
    branches = []
    for p, dil in enumerate(DILATIONS):
        o_p, lse_p = _swa_fwd(qs[p], ks[p], vs[p], bias[p], dil, tok)
        branches.append([o_p[0], lse_p[0]] if dil == 1 else [o_p, lse_p])
    lw = late_weights(group1, branches[-1][0])
    w_out, w_xq, w_xk, w_xv, w_xo = (lw[n] for n in group1)
    attn, lses, mixed, x1 = _mix_out(branches, gb, gc, xi, x, w_sc, g_a, g_c, w_out)
    tok = forward_weights(group2, x1)
    mem_n, mk, mv = _mem_kv(mem, g_mem, w_xk, w_xv)
    h2, xq, xo, x2 = _xattn_fwd(x1, g_xattn, w_xq, mk, mv, w_xo, tok)
    lw = late_weights(group2, x2)
    w_up_g, w_down_g = lw["w_up"], lw["w_down"]
    h3, conv, act, dx3, loss_acc, dg_final = _ffn_fwd(x2, g_ffn, w_up_g, w_fc, b_fc, w_down_g, g_final, target)

    gw_down = _dw(act, dx3, dep, "dw_down", a_chunked=True)
    dup, dx2, dg_ffn, dw_fc, db_fc = _ffn_bwd(dx3, h3, conv, x2, g_ffn, w_up_g, w_fc, w_down_g)
    gw_up = _dw(dup, h3, dep, "dw_up", a_chunked=True)
    tok = emit(dict(w_down=gw_down, w_up=gw_up))
    dxq, dx1, dmk, dmv, dg_xattn = _xattn_bwd(dx2, xo, xq, mk, mv, w_xo, w_xq, x1, g_xattn, tok)
    gw_xo = _dw(xo, dx2, tok, "dw_xo")[0]
    gw_xq = _dw(h2, dxq, tok, "dw_xq")[0]
    gw_xk, gw_xv, dg_mem = _mem_kv_bwd(dmk, dmv, mem_n, mem, w_xk, w_xv)
    tok = emit(dict(w_xo=gw_xo, w_xq=gw_xq, w_xk=gw_xk, w_xv=gw_xv))
    dattns, dds, dgb, dcv, dg_a, dg_c, dw_sc = _mix_out_bwd(dx1, w_out, attn, gb, gc, xi, w_sc, g_a, g_c, tok)
    first = lambda a: [a[0][None]] + list(a[1:])
    dattns, dds, lses = first(dattns), first(dds), first(lses)
    gw_out = _dw(mixed, dx1, tok, "dw_out")[0]
    tok = emit(dict(w_out=gw_out))
    dqs, dks, dvs, dbias = [], [], [], []
    for p, dil in enumerate(DILATIONS):
        dq_p, dk_p, dv_p, db_p = _swa_bwd(qs[p], ks[p], vs[p], dattns[p], lses[p], dds[p], bias[p], dil, tok)
        dqs.append(dq_p[0] if dil == 1 else dq_p)
        dks.append(dk_p[0] if dil == 1 else dk_p)
        dvs.append(dv_p[0] if dil == 1 else dv_p)
        dbias.append(db_p)
    d_relb = _bias_bwd(jnp.stack(dbias), buckets)
    dproj, grad_x, dg_mix = _in_proj_bwd(dqs, dks, dvs, dgb, dcv, gc, xi, w_sc, w_in_g, x, g_mix, dx1)
    pad = lambda a: jnp.pad(a, ((0, 0), (0, D_MODEL - a.shape[1])))
    small = jnp.concatenate([
        d_relb, dg_mix, dg_xattn, dg_mem, dg_ffn, dg_final, jnp.concatenate([dg_a, dg_c], axis=1),
        pad(dw_sc), pad(db_fc), pad(dw_fc.reshape(3 * N_DEV, UP_CHUNK)), pad(loss_acc)], axis=0)
    tok = emit_small(small)
    gw_in = _dw(h1, dproj, tok, "dw_in", n_chunks=N_DEV, chunk_cols=IN_CHUNK)
    emit(dict(w_in=gw_in))
    return grad_x


def kernel(x, mem, rel_bias, g_mix, w_in, w_short_conv, g_attn_out, g_conv_out, w_out, g_xattn, g_mem, w_xq, w_xk, w_xv, w_xo, g_ffn, w_up, w_ffn_conv, b_ffn_conv, w_down, g_final, loss_target, m_rel_bias, m_g_mix, m_w_in, m_w_short_conv, m_g_attn_out, m_g_conv_out, m_w_out, m_g_xattn, m_g_mem, m_w_xq, m_w_xk, m_w_xv, m_w_xo, m_g_ffn, m_w_up, m_w_ffn_conv, m_b_ffn_conv, m_w_down, m_g_final, v_rel_bias, v_g_mix, v_w_in, v_w_short_conv, v_g_attn_out, v_g_conv_out, v_w_out, v_g_xattn, v_g_mem, v_w_xq, v_w_xk, v_w_xv, v_w_xo, v_g_ffn, v_w_up, v_w_ffn_conv, v_b_ffn_conv, v_w_down, v_g_final):
    me = _dev_index(_mesh_pos())
    me_arr = me.reshape(1).astype(jnp.int32)

    big_names = ["w_in", "w_out", "w_xq", "w_xk", "w_xv", "w_xo", "w_up", "w_down"]
    late_names = big_names[1:]
    big_w = dict(w_in=w_in[0], w_out=w_out[0], w_xq=w_xq[0], w_xk=w_xk[0], w_xv=w_xv[0], w_xo=w_xo[0],
                 w_up=w_up[0].T, w_down=w_down[0])
    big_m = dict(w_in=m_w_in[0], w_out=m_w_out[0], w_xq=m_w_xq[0], w_xk=m_w_xk[0], w_xv=m_w_xv[0], w_xo=m_w_xo[0],
                 w_up=m_w_up[0].T, w_down=m_w_down[0])
    big_v = dict(w_in=v_w_in[0], w_out=v_w_out[0], w_xq=v_w_xq[0], w_xk=v_w_xk[0], w_xv=v_w_xv[0], w_xo=v_w_xo[0],
                 w_up=v_w_up[0].T, w_down=v_w_down[0])
    shard_shape = {n: big_w[n].shape for n in big_names}

    w_in_g, w_sc_g, w_fc_full = _all_gather([big_w["w_in"].astype(BF16), w_short_conv[0], w_ffn_conv[0]])
    w_sc_full = w_sc_g.transpose(1, 0, 2).reshape(3, CONV_W)
    late_shards = [big_w[n].astype(BF16) for n in late_names]
    ag_send, ag_recv, ag_srcs, ag_lands, ag_token = _gather_start("gather_weights_start", late_shards, w_in_g)
    forwarded = {}

    def forward_weights(names, after):
        which = [late_names.index(n) for n in names]
        fsend, frecv, lands, token = _gather_forward("gather_" + "_".join(names) + "_forward", ag_send, ag_recv,
                                                     [ag_lands[a] for a in which], which, after)
        forwarded[tuple(names)] = (fsend, frecv, lands)
        return token

    def late_weights(names, after):
        which = [late_names.index(n) for n in names]
        fsend, frecv, lands = forwarded[tuple(names)]
        lands = _gather_wait("gather_" + "_".join(names) + "_wait", ag_send, ag_recv, fsend, frecv,
                             [ag_srcs[a] for a in which], lands, which, after)
        out = {}
        for n, a, land in zip(names, which, lands):
            full = lax.dynamic_update_index_in_dim(land, late_shards[a], me, 0)
            if n == "w_up":
                out[n] = full
            elif n == "w_down":
                out[n] = full.reshape(N_DEV // 2, UP_CHUNK, D_MODEL)
            else:
                out[n] = full.reshape(D_MODEL, D_MODEL)
        return out

    sent = []

    def emit(grads):
        names = list(grads)
        blocks = [grads[n].reshape((N_DEV,) + shard_shape[n]) for n in names]
        started = _exchange_start("scatter_" + "_".join(names) + "_start", blocks, [False] * len(names), me_arr)
        sent.append((names, started))
        return started[-1]

    def emit_small(small):
        sent_small.append((small, _exchange_start("gather_small_start", [small], [True], me_arr)))
        return sent_small[0][1][-1]

    sent_small = []
    grad_x = _local_step(
        x[0], mem[0], loss_target[0], rel_bias, g_mix, w_in_g, w_sc_full, g_attn_out, g_conv_out, g_xattn, g_mem,
        g_ffn, w_fc_full, b_ffn_conv.reshape(N_DEV, 1, UP_CHUNK), g_final.reshape(1, D_MODEL), ag_token,
        forward_weights, late_weights, emit, emit_small)

    small_g, small_started = sent_small[0]
    after = sent[-1][1][-1]
    small_parts = _exchange_wait("gather_small_wait", small_started, [True], after)[1][0]
    big_out = {}
    after = small_parts
    for names, started in sent:
        blocks, lands = _exchange_wait("scatter_" + "_".join(names) + "_wait", started, [False] * len(names), after)
        for n, block, land in zip(names, blocks, lands):
            res = _adamw_big("adamw_" + n, big_w[n], block, land, big_m[n], big_v[n], me_arr)
            big_out[n] = [(r.T if n == "w_up" else r)[None] for r in res]
            after = res[0]

    as_rows = lambda a: a.reshape(N_DEV, UP_CHUNK)
    row1 = lambda a: a.reshape(1, D_MODEL)
    small_names = ["rel_bias", "g_mix", "g_attn_out", "g_conv_out", "g_xattn", "g_mem", "g_ffn", "b_ffn_conv", "g_final"]
    wmv = [
        (rel_bias, m_rel_bias, v_rel_bias), (g_mix, m_g_mix, v_g_mix), (g_attn_out, m_g_attn_out, v_g_attn_out),
        (g_conv_out, m_g_conv_out, v_g_conv_out), (g_xattn, m_g_xattn, v_g_xattn), (g_mem, m_g_mem, v_g_mem),
        (g_ffn, m_g_ffn, v_g_ffn), (as_rows(b_ffn_conv), as_rows(m_b_ffn_conv), as_rows(v_b_ffn_conv)),
        (row1(g_final), row1(m_g_final), row1(v_g_final))]
    g_packed, small_res = _adamw_small(small_g, small_parts, wmv, me_arr)
    small_out = dict(zip(small_names, small_res))
    loss = g_packed[ROW_LOSS, 0]
    small_out["b_ffn_conv"] = [a.reshape(1, 2 * D_FF) for a in small_out["b_ffn_conv"]]
    small_out["g_final"] = [a.reshape(D_MODEL) for a in small_out["g_final"]]

    g_wsc = lax.dynamic_slice(g_packed[ROW_WSC:ROW_WSC + 3, 0:CONV_W], (0, me * HEAD_DIM), (3, HEAD_DIM))
    g_wfc = lax.dynamic_slice(g_packed[ROW_WFC:ROW_WFC + 3 * N_DEV, 0:UP_CHUNK].reshape(3, N_DEV, UP_CHUNK),
                              (0, me, 0), (3, 1, UP_CHUNK)).reshape(3, UP_CHUNK)
    shard_res = _adamw_shards([(w_short_conv[0], g_wsc, m_w_short_conv[0], v_w_short_conv[0]),
                               (w_ffn_conv[0], g_wfc, m_w_ffn_conv[0], v_w_ffn_conv[0])])
    small_out["w_short_conv"] = [g_wsc[None]] + [a[None] for a in shard_res[0]]
    small_out["w_ffn_conv"] = [g_wfc[None]] + [a[None] for a in shard_res[1]]

    order = ["rel_bias", "g_mix", "w_in", "w_short_conv", "g_attn_out", "g_conv_out", "w_out", "g_xattn", "g_mem",
             "w_xq", "w_xk", "w_xv", "w_xo", "g_ffn", "w_up", "w_ffn_conv", "b_ffn_conv", "w_down", "g_final"]
    allp = {**big_out, **small_out}
    outs = [loss, grad_x[None]]
    for kind in range(4):
        outs += [allp[n][kind] for n in order]
    return tuple(outs)
```

```python
import functools
import math

import numpy as np
import jax
import jax.numpy as jnp
from jax import lax
from jax.experimental import pallas as pl
from jax.experimental.pallas import tpu as pltpu

F32 = jnp.float32
BF16 = jnp.bfloat16
MESH = pl.DeviceIdType.MESH

N_DEV = 8
D_MODEL = 1024
ATTN_W = 512
CONV_W = 512
N_HEADS = 8
HEAD_DIM = 64
WIN = 128
DILATIONS = (1, 4, 16)
N_BUCKETS = 32
BUCKET_MAX_EXACT = 16
BUCKET_MAX_DISTANCE = 2048
N_MEM_HEADS = 4
MEM_HEAD_DIM = 256
D_FF = 2816
IN_COLS = 3072
IN_CHUNK = IN_COLS // N_DEV
UP_CHUNK = 2 * D_FF // N_DEV
EPS = 1e-6

ADAM_LR = 0.001
ADAM_B1 = 0.9
ADAM_B2 = 0.999
ADAM_EPS = 1e-08
ADAM_WD = 0.01
ADAM_STEP = 10

SUBLANES = 8
LANES = 128
HALO = 16
TM = 512
TM_FFN = 512
TS_DW = 4096
VMEM_LIMIT = 56 * 1024 * 1024

ROW_RELB, ROW_GMIX, ROW_GXATTN, ROW_GMEM, ROW_GFFN, ROW_GFINAL, ROW_GAC = 0, 8, 16, 24, 32, 40, 48
ROW_WSC, ROW_BFC, ROW_WFC, ROW_LOSS, SMALL_ROWS = 56, 64, 72, 96, 104


def _cparams(n_grid):
    return pltpu.CompilerParams(dimension_semantics=("arbitrary",) * n_grid, vmem_limit_bytes=VMEM_LIMIT)


def _full(shape):
    nd = len(shape)
    return pl.BlockSpec(tuple(shape), lambda *_: (0,) * nd)


def _resident(shape):
    nd = len(shape)
    return pl.BlockSpec(tuple(shape), lambda *_: (0,) * nd, pipeline_mode=pl.Buffered(1))


ANY_SPEC = pl.BlockSpec(memory_space=pl.ANY)
HBM_SPEC = pl.BlockSpec(memory_space=pltpu.HBM)
SEM_SPEC = pl.BlockSpec(memory_space=pltpu.SEMAPHORE)
VMEM_SPEC = pl.BlockSpec(memory_space=pltpu.VMEM)
SMEM_SPEC = pl.BlockSpec(memory_space=pltpu.SMEM)
DATAFLOW = pltpu.SideEffectType.DATAFLOW_SIDE_EFFECTING


def _rms(x):
    r = lax.rsqrt(jnp.mean(x * x, axis=-1, keepdims=True) + EPS)
    return x * r, r


def _rms_bwd(xh, r, g, dy):
    dxh = dy * g
    return r * (dxh - xh * jnp.mean(dxh * xh, axis=-1, keepdims=True))


def _shift_down(u, halo, k):
    ru = pltpu.roll(u, k, 0)
    rh = pltpu.roll(halo, k, 0)
    row = lax.broadcasted_iota(jnp.int32, rh.shape, 0)
    head = jnp.where(row < k, rh, ru[0:SUBLANES])
    return jnp.concatenate([head, ru[SUBLANES:]], axis=0)


def _shift_up(u, halo, k):
    tm = u.shape[0]
    ru = pltpu.roll(u, tm - k, 0)
    rh = pltpu.roll(halo, SUBLANES - k, 0)
    row = lax.broadcasted_iota(jnp.int32, rh.shape, 0)
    tail = jnp.where(row >= SUBLANES - k, rh, ru[tm - SUBLANES:])
    return jnp.concatenate([ru[:tm - SUBLANES], tail], axis=0)


def _causal_conv3(u, halo, w_ref):
    return (_shift_down(u, halo, 2) * w_ref[0:1, :] + _shift_down(u, halo, 1) * w_ref[1:2, :]) + u * w_ref[2:3, :]


def _dot(a, b):
    return jnp.dot(a, b, preferred_element_type=F32)


def _dot_nt(a, b):
    return lax.dot_general(a, b, (((1,), (1,)), ((), ())), preferred_element_type=F32)


def _dot_tn(a, b):
    return lax.dot_general(a, b, (((0,), (0,)), ((), ())), preferred_element_type=F32)


def _sigmoid(x):
    return 0.5 * jnp.tanh(0.5 * x) + 0.5


def _bucket_tables():
    qi = np.arange(WIN)[:, None]
    kj = np.arange(2 * WIN)[None, :]
    steps = np.clip(qi + WIN - kj, 0, WIN)
    out = []
    for d in DILATIONS:
        dist = steps * d
        dd = np.maximum(dist, 1).astype(np.float32)
        large = BUCKET_MAX_EXACT + (
            np.log(dd / np.float32(BUCKET_MAX_EXACT)) / np.float32(math.log(BUCKET_MAX_DISTANCE / BUCKET_MAX_EXACT))
            * np.float32(N_BUCKETS - BUCKET_MAX_EXACT)).astype(np.int32)
        large = np.minimum(large, N_BUCKETS - 1)
        out.append(np.where(dist < BUCKET_MAX_EXACT, dist, large).astype(np.int32))
    return np.stack(out)


def _bias_fwd(rel_bias, buckets):
    present = [sorted(set(buckets[p].ravel().tolist())) for p in range(3)]

    def body(rb_ref, bk_ref, o_ref):
        for p in range(3):
            bk = bk_ref[p]
            for h in range(N_HEADS):
                acc = jnp.zeros((WIN, 2 * WIN), F32)
                for b in present[p]:
                    acc = jnp.where(bk == b, rb_ref[h, b], acc)
                o_ref[p, h] = acc

    return pl.pallas_call(
        body, name="bias_fwd",
        out_shape=jax.ShapeDtypeStruct((3, N_HEADS, WIN, 2 * WIN), F32),
        in_specs=[pl.BlockSpec(memory_space=pltpu.SMEM), pl.BlockSpec(memory_space=pltpu.VMEM)],
        out_specs=pl.BlockSpec(memory_space=pltpu.VMEM),
    )(rel_bias, jnp.asarray(buckets))


def _bias_bwd(dbias, buckets):
    present = [set(buckets[p].ravel().tolist()) for p in range(3)]

    def body(db_ref, bk_ref, o_ref):
        lane = lax.broadcasted_iota(jnp.int32, (1, D_MODEL), 1)
        rows = []
        for h in range(N_HEADS):
            row = jnp.zeros((1, D_MODEL), F32)
            for b in range(N_BUCKETS):
                tot = jnp.zeros((1, 1), F32)
                for p in (p for p in range(3) if b in present[p]):
                    sel = jnp.where(bk_ref[p] == b, db_ref[p, h], 0.0)
                    tot = tot + jnp.sum(jnp.sum(sel, axis=0, keepdims=True), axis=1, keepdims=True)
                row = jnp.where(lane == b, tot, row)
            rows.append(row)
        o_ref[...] = jnp.concatenate(rows, axis=0)

    return pl.pallas_call(
        body, name="bias_bwd",
        out_shape=jax.ShapeDtypeStruct((N_HEADS, D_MODEL), F32),
        in_specs=[pl.BlockSpec(memory_space=pltpu.VMEM), pl.BlockSpec(memory_space=pltpu.VMEM)],
        out_specs=pl.BlockSpec(memory_space=pltpu.VMEM),
    )(dbias, jnp.asarray(buckets))


def _spread(val, scr_ref, out_refs, dtype):
    out_refs[0][...] = val.astype(dtype)
    n_blk = val.shape[1] // LANES
    for c in range(n_blk):
        scr_ref[c] = val[:, c * LANES:(c + 1) * LANES]
    for o_ref, d in zip(out_refs[1:], DILATIONS[1:]):
        for r in range(d):
            for c in range(n_blk):
                o_ref[r, :, c * LANES:(c + 1) * LANES] = scr_ref.at[c][pl.ds(r, TM // d, stride=d), :].astype(dtype)


def _gather_classes(blk_ref, scr_ref, d):
    n_blk = blk_ref.shape[2] // LANES
    for r in range(d):
        for c in range(n_blk):
            scr_ref.at[c][pl.ds(r, TM // d, stride=d), :] = blk_ref[r, :, c * LANES:(c + 1) * LANES].astype(F32)
    return jnp.concatenate([scr_ref[c] for c in range(n_blk)], axis=1)


def _class_specs(cols):
    return [pl.BlockSpec((TM, cols), lambda i: (i, 0))] + [
        pl.BlockSpec((d, TM // d, cols), lambda i: (0, i, 0)) for d in DILATIONS[1:]]


def _class_shapes(s, cols, dtype):
    return [jax.ShapeDtypeStruct((s, cols), dtype)] + [
        jax.ShapeDtypeStruct((d, s // d, cols), dtype) for d in DILATIONS[1:]]


def _rms_proj(x, g_mix, w_in_g, dep):
    s = x.shape[0]

    def body(x_ref, g_ref, w_ref, dep_ref, h_ref, q1, q4, q16, k1, k4, k16, v1, v4, v16, gb_ref, gc_ref, xi_ref, scr):
        xh, _ = _rms(x_ref[...])
        h = (xh * g_ref[...]).astype(BF16)
        h_ref[...] = h
        proj = jnp.concatenate([_dot(h, w_ref[j]) for j in range(N_DEV)], axis=1)
        _spread(proj[:, 0:512] * (HEAD_DIM ** -0.5), scr, (q1, q4, q16), BF16)
        _spread(proj[:, 512:1024], scr, (k1, k4, k16), BF16)
        _spread(proj[:, 1024:1536], scr, (v1, v4, v16), BF16)
        gb_ref[...] = proj[:, 1536:2048]
        gc_ref[...] = proj[:, 2048:2560]
        xi_ref[...] = proj[:, 2560:3072]

    row = lambda n: pl.BlockSpec((TM, n), lambda i: (i, 0))
    res = pl.pallas_call(
        body, name="rms_proj", grid=(s // TM,),
        out_shape=[jax.ShapeDtypeStruct((s, D_MODEL), BF16)] + _class_shapes(s, 512, BF16) * 3
        + [jax.ShapeDtypeStruct((s, 512), F32)] * 3,
        in_specs=[row(D_MODEL), _full(g_mix.shape), _full(w_in_g.shape), ANY_SPEC],
        out_specs=[row(D_MODEL)] + _class_specs(512) * 3 + [row(512)] * 3,
        scratch_shapes=[pltpu.VMEM((512 // LANES, TM, LANES), F32)],
        compiler_params=_cparams(1),
    )(x, g_mix, w_in_g, dep)
    return res[0], res[1:4], res[4:7], res[7:10], res[10], res[11], res[12]


def _pair_split(x2):
    lane = lax.broadcasted_iota(jnp.int32, x2.shape, 1)
    zero = jnp.zeros_like(x2)
    return jnp.where(lane < HEAD_DIM, x2, zero), jnp.where(lane >= HEAD_DIM, x2, zero)


def _pair_join(even, odd):
    lane = lax.broadcasted_iota(jnp.int32, (even.shape[0], LANES), 1)
    return jnp.where(lane < HEAD_DIM, even, odd)


def _band_mask(first):
    qi = lax.broadcasted_iota(jnp.int32, (WIN, 2 * WIN), 0)
    kj = lax.broadcasted_iota(jnp.int32, (WIN, 2 * WIN), 1)
    steps = qi + WIN - kj
    return (steps >= 0) & (steps <= WIN) & (kj >= jnp.where(first, WIN, 0))


def _swa_fwd(qc, kc, vc, bias, dil, dep):
    nb = qc.shape[1] // (2 * WIN)

    def body(q_ref, kp_ref, kc_ref, vp_ref, vc_ref, b_ref, dep_ref, o_ref, lse_ref, s_scr, p_scr):
        b = pl.program_id(1)
        pairs = [slice(a * LANES, (a + 1) * LANES) for a in range(N_HEADS // 2)]
        for sub in range(2):
            rows = slice(sub * WIN, (sub + 1) * WIN)

            def keys(prev_ref, cur_ref, sl):
                if sub == 0:
                    return jnp.concatenate([prev_ref[0, :, sl], cur_ref[0, 0:WIN, sl]], axis=0)
                return cur_ref[0, :, sl]

            for a, sl in enumerate(pairs):
                k2 = keys(kp_ref, kc_ref, sl)
                for e, qh in enumerate(_pair_split(q_ref[0, rows, sl])):
                    s_scr[sub, 2 * a + e] = _dot_nt(qh, k2)
            first = (b == 0) if sub == 0 else False
            lg = jnp.where(_band_mask(first), s_scr[sub] + b_ref[...], -jnp.inf)
            m = jnp.max(lg, axis=-1, keepdims=True)
            p = jnp.exp(lg - m)
            den = jnp.sum(p, axis=-1, keepdims=True)
            p_scr[sub] = p.astype(BF16)
            lse = m + jnp.log(den)
            for a, sl in enumerate(pairs):
                v_even, v_odd = _pair_split(keys(vp_ref, vc_ref, sl))
                o2 = _dot(p_scr[sub, 2 * a], v_even) + _dot(p_scr[sub, 2 * a + 1], v_odd)
                o_ref[0, rows, sl] = o2 / _pair_join(den[2 * a], den[2 * a + 1])
                lse_ref[0, rows, sl] = _pair_join(lse[2 * a], lse[2 * a + 1])

    cur = pl.BlockSpec((1, 2 * WIN, 512), lambda r, b: (r, b, 0))
    prev = pl.BlockSpec((1, WIN, 512), lambda r, b: (r, jnp.maximum(2 * b - 1, 0), 0))
    return pl.pallas_call(
        body, name=f"swa_fwd_d{dil}", grid=(dil, nb),
        out_shape=[jax.ShapeDtypeStruct(qc.shape, F32)] * 2,
        in_specs=[cur, prev, cur, prev, cur, _full(bias.shape), ANY_SPEC],
        out_specs=[cur] * 2,
        scratch_shapes=[pltpu.VMEM((2, N_HEADS, WIN, 2 * WIN), F32), pltpu.VMEM((2, N_HEADS, WIN, 2 * WIN), BF16)],
        compiler_params=_cparams(2),
    )(qc, kc, kc, vc, vc, bias, dep)


def _mix_out(branches, gb, gc, xi, x, w_sc, g_a, g_c, w_out):
    s = x.shape[0]
    tb = TM // SUBLANES

    def body(o1, l1, o4, l4, o16, l16, gb_ref, gc_ref, xi_ref, gch_ref, xih_ref, x_ref, wsc_ref,
             ga_ref, gcv_ref, wout_ref, attn_ref, lse1, lse4, lse16, mixed_ref, x1_ref, scr_a, scr_b, scr_c, scr_d):
        i = pl.program_id(0)
        la, lb, lc = l1[...], _gather_classes(l4, scr_a, 4), _gather_classes(l16, scr_b, 16)
        m_all = jnp.maximum(jnp.maximum(la, lb), lc)
        ea, eb, ec = jnp.exp(la - m_all), jnp.exp(lb - m_all), jnp.exp(lc - m_all)
        den = (ea + eb) + ec
        num = (ea * o1[...] + eb * _gather_classes(o4, scr_c, 4)) + ec * _gather_classes(o16, scr_d, 16)
        attn = num / den
        attn_ref[...] = attn
        _spread(m_all + jnp.log(den), scr_a, (lse1, lse4, lse16), F32)
        xa, _ = _rms(attn)
        u = gc_ref[...] * xi_ref[...]
        uh = jnp.where(i > 0, gch_ref[...] * xih_ref[...], 0.0)
        conv = gb_ref[...] * _causal_conv3(u, uh, wsc_ref)
        xc, _ = _rms(conv)
        mixed = jnp.concatenate([xa * ga_ref[...], xc * gcv_ref[...]], axis=1).astype(BF16)
        mixed_ref[...] = mixed
        x1_ref[...] = x_ref[...] + _dot(mixed, wout_ref[...])

    row = lambda n: pl.BlockSpec((TM, n), lambda i: (i, 0))
    halo = pl.BlockSpec((SUBLANES, 512), lambda i: (jnp.maximum(i * tb - 1, 0), 0))
    cs = _class_specs(512)
    flat = [a for br in branches for a in br]
    res = pl.pallas_call(
        body, name="mix_out", grid=(s // TM,),
        out_shape=[jax.ShapeDtypeStruct((s, 512), F32)] + _class_shapes(s, 512, F32)
        + [jax.ShapeDtypeStruct((s, D_MODEL), BF16), jax.ShapeDtypeStruct((s, D_MODEL), F32)],
        in_specs=[cs[0], cs[0], cs[1], cs[1], cs[2], cs[2], row(512), row(512), row(512), halo, halo,
                  row(D_MODEL), _full(w_sc.shape), _full(g_a.shape), _full(g_c.shape), _full(w_out.shape)],
        out_specs=[row(512)] + cs + [row(D_MODEL), row(D_MODEL)],
        scratch_shapes=[pltpu.VMEM((512 // LANES, TM, LANES), F32)] * 4,
        compiler_params=_cparams(1),
    )(*flat, gb, gc, xi, gc, xi, x, w_sc, g_a, g_c, w_out)
    return res[0], res[1:4], res[4], res[5]


def _mem_kv(mem, g_mem, w_xk, w_xv):
    def body(mem_ref, g_ref, wk_ref, wv_ref, mn_ref, k_ref, v_ref):
        xh, _ = _rms(mem_ref[...])
        mn = (xh * g_ref[...]).astype(BF16)
        mn_ref[...] = mn
        k_ref[...] = _dot(mn, wk_ref[...]).astype(BF16)
        v_ref[...] = _dot(mn, wv_ref[...]).astype(BF16)

    vm = pl.BlockSpec(memory_space=pltpu.VMEM)
    return pl.pallas_call(
        body, name="mem_kv",
        out_shape=[jax.ShapeDtypeStruct(mem.shape, BF16)] * 3,
        in_specs=[vm] * 4, out_specs=[vm] * 3,
        compiler_params=pltpu.CompilerParams(vmem_limit_bytes=VMEM_LIMIT),
    )(mem, g_mem, w_xk, w_xv)


def _xattn_fwd(x1, g, w_xq, k, v, w_xo, dep):
    s = x1.shape[0]

    def body(x1_ref, g_ref, wq_ref, k_ref, v_ref, wo_ref, dep_ref, h2_ref, q_ref, o_ref, x2_ref):
        x1v = x1_ref[...]
        xh, _ = _rms(x1v)
        h2 = (xh * g_ref[...]).astype(BF16)
        h2_ref[...] = h2
        qb = _dot(h2, wq_ref[...]).astype(BF16)
        q_ref[...] = qb
        outs = []
        for h in range(N_MEM_HEADS):
            sl = slice(h * MEM_HEAD_DIM, (h + 1) * MEM_HEAD_DIM)
            lg = _dot_nt(qb[:, sl], k_ref[:, sl]) * (MEM_HEAD_DIM ** -0.5)
            p = jnp.exp(lg - jnp.max(lg, axis=-1, keepdims=True))
            p = p / jnp.sum(p, axis=-1, keepdims=True)
            outs.append(_dot(p.astype(BF16), v_ref[:, sl]))
        o = jnp.concatenate(outs, axis=1).astype(BF16)
        o_ref[...] = o
        x2_ref[...] = x1v + _dot(o, wo_ref[...])

    row = pl.BlockSpec((TM, D_MODEL), lambda i: (i, 0))
    return pl.pallas_call(
        body, name="xattn_fwd", grid=(s // TM,),
        out_shape=[jax.ShapeDtypeStruct((s, D_MODEL), BF16)] * 3 + [jax.ShapeDtypeStruct((s, D_MODEL), F32)],
        in_specs=[row, _full(g.shape), _full(w_xq.shape), _full(k.shape), _full(v.shape), _full(w_xo.shape), ANY_SPEC],
        out_specs=[row] * 4,
        compiler_params=_cparams(1),
    )(x1, g, w_xq, k, v, w_xo, dep)


def _ffn_conv(h_ext, w_up_t, w, b):
    u = _dot_nt(h_ext, w_up_t)
    c = ((pltpu.roll(u, 2, 0) * w[0:1, :] + pltpu.roll(u, 1, 0) * w[1:2, :]) + u * w[2:3, :]) + b
    return c[HALO:]


def _ffn_fwd(x2, g, w_up_g, w_fc, b_fc, w_down_g, g_final, target):
    s = x2.shape[0]
    tb = TM_FFN // HALO
    half = N_DEV // 2

    def body(x_ref, xp_ref, g_ref, wup_ref, wfc_ref, bfc_ref, wd_ref, gf_ref, t_ref, h_ref, c_ref, act_ref, dx3_ref,
             loss_ref, dgf_ref, hext_scr, down_scr):
        i, j = pl.program_id(0), pl.program_id(1)

        @pl.when((i == 0) & (j == 0))
        def _():
            loss_ref[...] = jnp.zeros_like(loss_ref)
            dgf_ref[...] = jnp.zeros_like(dgf_ref)

        @pl.when(j == 0)
        def _():
            gv = g_ref[...]
            h = (_rms(x_ref[...])[0] * gv).astype(BF16)
            h_ref[...] = h
            hp = jnp.where(i > 0, _rms(xp_ref[...])[0] * gv, 0.0).astype(BF16)
            hext_scr[...] = jnp.concatenate([hp, h], axis=0)
            down_scr[...] = jnp.zeros_like(down_scr)

        h_ext = hext_scr[...]
        cg = _ffn_conv(h_ext, wup_ref[0, 0], wfc_ref[0, 0], bfc_ref[0, 0])
        cv = _ffn_conv(h_ext, wup_ref[1, 0], wfc_ref[1, 0], bfc_ref[1, 0])
        c_ref[0, 0] = cg
        c_ref[1, 0] = cv
        a = ((cg * _sigmoid(cg)) * cv).astype(BF16)
        act_ref[0] = a
        down_scr[...] += _dot(a, wd_ref[0])

        @pl.when(j == half - 1)
        def _():
            x3 = x_ref[...] + down_scr[...]
            xh, r = _rms(x3)
            gf = gf_ref[...]
            e = xh * gf - t_ref[...]
            loss_ref[...] += 0.5 * jnp.sum(jnp.sum(e * e, axis=1, keepdims=True), axis=0, keepdims=True) / D_MODEL
            dy = e * (1.0 / D_MODEL)
            dgf_ref[0:1, :] += jnp.sum(dy * xh, axis=0, keepdims=True)
            dx3_ref[...] = _rms_bwd(xh, r, gf, dy)

    row = pl.BlockSpec((TM_FFN, D_MODEL), lambda i, j: (i, 0))
    prev = pl.BlockSpec((HALO, D_MODEL), lambda i, j: (jnp.maximum(i * tb - 1, 0), 0))
    pair = lambda *tail: pl.BlockSpec((2, 1) + tail, lambda i, j: (0, j) + (0,) * len(tail))
    return pl.pallas_call(
        body, name="ffn_fwd", grid=(s // TM_FFN, half),
        out_shape=[jax.ShapeDtypeStruct((s, D_MODEL), BF16), jax.ShapeDtypeStruct((2, half, s, UP_CHUNK), F32),
                   jax.ShapeDtypeStruct((half, s, UP_CHUNK), BF16),
                   jax.ShapeDtypeStruct((s, D_MODEL), F32), jax.ShapeDtypeStruct((SUBLANES, 128), F32),
                   jax.ShapeDtypeStruct((SUBLANES, D_MODEL), F32)],
        in_specs=[row, prev, _full(g.shape), pair(UP_CHUNK, D_MODEL), pair(3, UP_CHUNK), pair(1, UP_CHUNK),
                  pl.BlockSpec((1, UP_CHUNK, D_MODEL), lambda i, j: (j, 0, 0)), _full(g_final.shape), row],
        out_specs=[row, pl.BlockSpec((2, 1, TM_FFN, UP_CHUNK), lambda i, j: (0, j, i, 0)),
                   pl.BlockSpec((1, TM_FFN, UP_CHUNK), lambda i, j: (j, i, 0)), row,
                   _full((SUBLANES, 128)), _full((SUBLANES, D_MODEL))],
        scratch_shapes=[pltpu.VMEM((HALO + TM_FFN, D_MODEL), BF16), pltpu.VMEM((TM_FFN, D_MODEL), F32)],
        compiler_params=_cparams(2),
    )(x2, x2, g, w_up_g, w_fc, b_fc, w_down_g, g_final, target)


def _ffn_bwd(dx3, h3, conv, x2, g, w_up_g, w_fc, w_down_g):
    s = x2.shape[0]
    tb = TM_FFN // HALO
    last = s // HALO - 1
    n_tiles = s // TM_FFN
    half = N_DEV // 2
    n_ext = TM_FFN + HALO

    def body(dx_ref, dxn_ref, h_ref, c_ref, cn_ref, x2_ref, g_ref, wup_ref, wfc_ref, wd_ref,
             dup_ref, dx2_ref, dg_ref, dwfc_ref, dbfc_ref, dh_scr):
        i, j = pl.program_id(0), pl.program_id(1)

        @pl.when((i == 0) & (j == 0))
        def _():
            dg_ref[...] = jnp.zeros_like(dg_ref)
            dwfc_ref[...] = jnp.zeros_like(dwfc_ref)
            dbfc_ref[...] = jnp.zeros_like(dbfc_ref)

        @pl.when(j == 0)
        def _():
            dh_scr[...] = jnp.zeros_like(dh_scr)

        dxv = dx_ref[...]
        dxn = jnp.where(i < n_tiles - 1, dxn_ref[...], 0.0)
        dx_ext = jnp.concatenate([dxv, dxn], axis=0).astype(BF16)
        h = h_ref[...]
        cg = jnp.concatenate([c_ref[0, 0], cn_ref[0, 0]], axis=0)
        cv = jnp.concatenate([c_ref[1, 0], cn_ref[1, 0]], axis=0)
        dact = _dot_nt(dx_ext, wd_ref[0])
        sg = _sigmoid(cg)
        for k, dc in ((1, dact * (cg * sg)), (0, (dact * cv) * (sg * (1.0 + cg * (1.0 - sg))))):
            jj = j + half * k
            w_up_t = wup_ref[k, 0]
            u = _dot_nt(h, w_up_t)
            dc0, dc1, dc2 = dc[:TM_FFN], pltpu.roll(dc, n_ext - 1, 0)[:TM_FFN], pltpu.roll(dc, n_ext - 2, 0)[:TM_FFN]
            dbfc_ref[jj, 0:1, :] += jnp.sum(dc0, axis=0, keepdims=True)
            dwfc_ref[0, jj, 0:1, :] += jnp.sum(dc2 * u, axis=0, keepdims=True)
            dwfc_ref[1, jj, 0:1, :] += jnp.sum(dc1 * u, axis=0, keepdims=True)
            dwfc_ref[2, jj, 0:1, :] += jnp.sum(dc0 * u, axis=0, keepdims=True)
            w = wfc_ref[k, 0]
            du = ((dc0 * w[2:3, :] + dc1 * w[1:2, :]) + dc2 * w[0:1, :]).astype(BF16)
            dup_ref[k, 0] = du
            dh_scr[...] += _dot(du, w_up_t)

        @pl.when(j == half - 1)
        def _():
            xh, r = _rms(x2_ref[...])
            dh = dh_scr[...]
            dg_ref[0:1, :] += jnp.sum(dh * xh, axis=0, keepdims=True)
            dx2_ref[...] = dxv + _rms_bwd(xh, r, g_ref[...], dh)

    row = pl.BlockSpec((TM_FFN, D_MODEL), lambda i, j: (i, 0))
    nxt = pl.BlockSpec((HALO, D_MODEL), lambda i, j: (jnp.minimum((i + 1) * tb, last), 0))
    pair = lambda *tail: pl.BlockSpec((2, 1) + tail, lambda i, j: (0, j) + (0,) * len(tail))
    cur_c = pl.BlockSpec((2, 1, TM_FFN, UP_CHUNK), lambda i, j: (0, j, i, 0))
    nxt_c = pl.BlockSpec((2, 1, HALO, UP_CHUNK), lambda i, j: (0, j, jnp.minimum((i + 1) * tb, last), 0))
    return pl.pallas_call(
        body, name="ffn_bwd", grid=(n_tiles, half),
        out_shape=[jax.ShapeDtypeStruct((2, half, s, UP_CHUNK), BF16), jax.ShapeDtypeStruct((s, D_MODEL), F32),
                   jax.ShapeDtypeStruct((SUBLANES, D_MODEL), F32),
                   jax.ShapeDtypeStruct((3, N_DEV, SUBLANES, UP_CHUNK), F32),
                   jax.ShapeDtypeStruct((N_DEV, SUBLANES, UP_CHUNK), F32)],
        in_specs=[row, nxt, row, cur_c, nxt_c, row, _full(g.shape), pair(UP_CHUNK, D_MODEL), pair(3, UP_CHUNK),
                  pl.BlockSpec((1, UP_CHUNK, D_MODEL), lambda i, j: (j, 0, 0))],
        out_specs=[cur_c, row, _full((SUBLANES, D_MODEL)), _full((3, N_DEV, SUBLANES, UP_CHUNK)),
                   _full((N_DEV, SUBLANES, UP_CHUNK))],
        scratch_shapes=[pltpu.VMEM((TM_FFN, D_MODEL), F32)],
        compiler_params=_cparams(2),
    )(dx3, dx3, h3, conv, conv, x2, g, w_up_g, w_fc, w_down_g)


def _xattn_bwd(dx2, o, q, k, v, w_xo, w_xq, x1, g, dep):
    s = x1.shape[0]

    def body(dx2_ref, o_ref, q_ref, k_ref, v_ref, wo_ref, wq_ref, x1_ref, g_ref, dep_ref, dq_ref, dx1_ref, dk_ref,
             dv_ref, dg_ref):
        @pl.when(pl.program_id(0) == 0)
        def _():
            dk_ref[...] = jnp.zeros_like(dk_ref)
            dv_ref[...] = jnp.zeros_like(dv_ref)
            dg_ref[...] = jnp.zeros_like(dg_ref)

        dx2v = dx2_ref[...]
        do = _dot_nt(dx2v.astype(BF16), wo_ref[...])
        dqs = []
        for h in range(N_MEM_HEADS):
            sl = slice(h * MEM_HEAD_DIM, (h + 1) * MEM_HEAD_DIM)
            qh, kh, vh = q_ref[:, sl], k_ref[:, sl], v_ref[:, sl]
            lg = _dot_nt(qh, kh) * (MEM_HEAD_DIM ** -0.5)
            p = jnp.exp(lg - jnp.max(lg, axis=-1, keepdims=True))
            p = p / jnp.sum(p, axis=-1, keepdims=True)
            doh = do[:, sl].astype(BF16)
            dp = _dot_nt(doh, vh)
            ds = (p * (dp - jnp.sum(p * dp, axis=-1, keepdims=True)) * (MEM_HEAD_DIM ** -0.5)).astype(BF16)
            dqs.append(_dot(ds, kh))
            dk_ref[:, sl] += _dot_tn(ds, qh)
            dv_ref[:, sl] += _dot_tn(p.astype(BF16), doh)
        dq = jnp.concatenate(dqs, axis=1).astype(BF16)
        dq_ref[...] = dq
        dh2 = _dot_nt(dq, wq_ref[...])
        xh, r = _rms(x1_ref[...])
        dg_ref[0:1, :] += jnp.sum(dh2 * xh, axis=0, keepdims=True)
        dx1_ref[...] = dx2v + _rms_bwd(xh, r, g_ref[...], dh2)

    row = pl.BlockSpec((TM, D_MODEL), lambda i: (i, 0))
    return pl.pallas_call(
        body, name="xattn_bwd", grid=(s // TM,),
        out_shape=[jax.ShapeDtypeStruct((s, D_MODEL), BF16), jax.ShapeDtypeStruct((s, D_MODEL), F32),
                   jax.ShapeDtypeStruct(k.shape, F32), jax.ShapeDtypeStruct(k.shape, F32),
                   jax.ShapeDtypeStruct((SUBLANES, D_MODEL), F32)],
        in_specs=[row, row, row, _full(k.shape), _full(v.shape), _full(w_xo.shape), _full(w_xq.shape), row,
                  _full(g.shape), ANY_SPEC],
        out_specs=[row, row, _full(k.shape), _full(k.shape), _full((SUBLANES, D_MODEL))],
        compiler_params=_cparams(1),
    )(dx2, o, q, k, v, w_xo, w_xq, x1, g, dep)


def _mem_kv_bwd(dk, dv, mem_n, mem, w_xk, w_xv):
    def body(dk_ref, dv_ref, mn_ref, mem_ref, wk_ref, wv_ref, dwk_ref, dwv_ref, dg_ref):
        dkb, dvb = dk_ref[...].astype(BF16), dv_ref[...].astype(BF16)
        mn = mn_ref[...]
        dwk_ref[...] = _dot_tn(mn, dkb).astype(BF16)
        dwv_ref[...] = _dot_tn(mn, dvb).astype(BF16)
        dmn = _dot_nt(dkb, wk_ref[...]) + _dot_nt(dvb, wv_ref[...])
        xh, _ = _rms(mem_ref[...])
        dg_ref[...] = jnp.zeros_like(dg_ref)
        dg_ref[0:1, :] = jnp.sum(dmn * xh, axis=0, keepdims=True)

    vm = pl.BlockSpec(memory_space=pltpu.VMEM)
    return pl.pallas_call(
        body, name="mem_kv_bwd",
        out_shape=[jax.ShapeDtypeStruct(w_xk.shape, BF16), jax.ShapeDtypeStruct(w_xv.shape, BF16),
                   jax.ShapeDtypeStruct((SUBLANES, D_MODEL), F32)],
        in_specs=[vm] * 6, out_specs=[vm] * 3,
        compiler_params=pltpu.CompilerParams(vmem_limit_bytes=VMEM_LIMIT),
    )(dk, dv, mem_n, mem, w_xk, w_xv)


def _mix_out_bwd(dx1, w_out, attn, gb, gc, xi, w_sc, g_a, g_c, dep):
    s = dx1.shape[0]
    tb = TM // SUBLANES

    def body(dx1_ref, wout_ref, attn_ref, gb_ref, gc_ref, xi_ref, gch_ref, xih_ref, wsc_ref, ga_ref, gcv_ref, dep_ref,
             da1, da4, da16, dd1, dd4, dd16, dgb_ref, dcv_ref, dga_ref, dgc_ref, dwsc_ref, scr):
        i = pl.program_id(0)

        @pl.when(i == 0)
        def _():
            dga_ref[...] = jnp.zeros_like(dga_ref)
            dgc_ref[...] = jnp.zeros_like(dgc_ref)
            dwsc_ref[...] = jnp.zeros_like(dwsc_ref)

        dmixed = _dot_nt(dx1_ref[...].astype(BF16), wout_ref[...])
        da, dcn = dmixed[:, :ATTN_W], dmixed[:, ATTN_W:]
        attn = attn_ref[...]
        xa, ra = _rms(attn)
        dga_ref[0:1, :] += jnp.sum(da * xa, axis=0, keepdims=True)
        dattn = _rms_bwd(xa, ra, ga_ref[...], da)
        _spread(dattn, scr, (da1, da4, da16), BF16)
        prod = dattn * attn
        dd = jnp.concatenate(
            [jnp.broadcast_to(jnp.sum(prod[:, h * HEAD_DIM:(h + 1) * HEAD_DIM], axis=-1, keepdims=True),
                              (TM, HEAD_DIM)) for h in range(N_HEADS)], axis=1)
        _spread(dd, scr, (dd1, dd4, dd16), F32)
        gbv = gb_ref[...]
        u = gc_ref[...] * xi_ref[...]
        uh = jnp.where(i > 0, gch_ref[...] * xih_ref[...], 0.0)
        u2, u1 = _shift_down(u, uh, 2), _shift_down(u, uh, 1)
        cv = (u2 * wsc_ref[0:1, :] + u1 * wsc_ref[1:2, :]) + u * wsc_ref[2:3, :]
        xc, rc = _rms(gbv * cv)
        dgc_ref[0:1, :] += jnp.sum(dcn * xc, axis=0, keepdims=True)
        dconv = _rms_bwd(xc, rc, gcv_ref[...], dcn)
        dgb_ref[...] = (dconv * cv).astype(BF16)
        dcv = dconv * gbv
        dcv_ref[...] = dcv
        dwsc_ref[0:1, :] += jnp.sum(dcv * u2, axis=0, keepdims=True)
        dwsc_ref[1:2, :] += jnp.sum(dcv * u1, axis=0, keepdims=True)
        dwsc_ref[2:3, :] += jnp.sum(dcv * u, axis=0, keepdims=True)

    row = lambda n: pl.BlockSpec((TM, n), lambda i: (i, 0))
    halo = pl.BlockSpec((SUBLANES, 512), lambda i: (jnp.maximum(i * tb - 1, 0), 0))
    acc = _full((SUBLANES, 512))
    res = pl.pallas_call(
        body, name="mix_out_bwd", grid=(s // TM,),
        out_shape=_class_shapes(s, 512, BF16) + _class_shapes(s, 512, F32)
        + [jax.ShapeDtypeStruct((s, 512), BF16), jax.ShapeDtypeStruct((s, 512), F32)]
        + [jax.ShapeDtypeStruct((SUBLANES, 512), F32)] * 3,
        in_specs=[row(D_MODEL), _full(w_out.shape), row(512), row(512), row(512), row(512), halo, halo,
                  _full(w_sc.shape), _full(g_a.shape), _full(g_c.shape), ANY_SPEC],
        out_specs=_class_specs(512) * 2 + [row(512)] * 2 + [acc] * 3,
        scratch_shapes=[pltpu.VMEM((512 // LANES, TM, LANES), F32)],
        compiler_params=_cparams(1),
    )(dx1, w_out, attn, gb, gc, xi, gc, xi, w_sc, g_a, g_c, dep)
    return res[0:3], res[3:6], res[6], res[7], res[8], res[9], res[10]


def _swa_bwd(qc, kc, vc, doc, lsec, ddc, bias, dil, dep):
    n128 = qc.shape[1] // WIN
    nb = n128 // 2

    def body(q_ref, qn_ref, kp_ref, kc_ref, vp_ref, vc_ref, do_ref, don_ref, lse_ref, lsen_ref, dd_ref, ddn_ref,
             b_ref, dep_ref, dq_ref, dk_ref, dv_ref, db_ref, s_scr, dp_scr, sn_scr, dpn_scr, ds_scr, p_scr, dsn_scr,
             pn_scr):
        r, b = pl.program_id(0), pl.program_id(1)

        @pl.when((r == 0) & (b == 0))
        def _():
            db_ref[...] = jnp.zeros_like(db_ref)

        pairs = [slice(a * LANES, (a + 1) * LANES) for a in range(N_HEADS // 2)]
        blk_a, blk_b = slice(0, WIN), slice(WIN, 2 * WIN)
        per_head = lambda ref, rows: jnp.stack([ref[0, rows, h * HEAD_DIM:h * HEAD_DIM + 1] for h in range(N_HEADS)])
        for a, sl in enumerate(pairs):
            k_pa = jnp.concatenate([kp_ref[0, :, sl], kc_ref[0, blk_a, sl]], axis=0)
            v_pa = jnp.concatenate([vp_ref[0, :, sl], vc_ref[0, blk_a, sl]], axis=0)
            for sub, (rows, k2, v2) in enumerate(((blk_a, k_pa, v_pa), (blk_b, kc_ref[0, :, sl], vc_ref[0, :, sl]))):
                q_eo = _pair_split(q_ref[0, rows, sl])
                do_eo = _pair_split(do_ref[0, rows, sl].astype(BF16))
                for e in range(2):
                    s_scr[sub, 2 * a + e] = _dot_nt(q_eo[e], k2)
                    dp_scr[sub, 2 * a + e] = _dot_nt(do_eo[e], v2)
            qn_eo = _pair_split(qn_ref[0, :, sl])
            don_eo = _pair_split(don_ref[0, :, sl].astype(BF16))
            for e in range(2):
                sn_scr[2 * a + e] = _dot_nt(qn_eo[e], kc_ref[0, blk_b, sl])
                dpn_scr[2 * a + e] = _dot_nt(don_eo[e], vc_ref[0, blk_b, sl])
        bias = b_ref[...]
        for sub, rows in enumerate((blk_a, blk_b)):
            first = (b == 0) if sub == 0 else False
            p = jnp.exp(jnp.where(_band_mask(first), s_scr[sub] + bias, -jnp.inf) - per_head(lse_ref, rows))
            ds = p * (dp_scr[sub] - per_head(dd_ref, rows))
            db_ref[...] += ds
            ds_scr[sub] = ds.astype(BF16)
            p_scr[sub] = p.astype(BF16)
        qi = lax.broadcasted_iota(jnp.int32, (WIN, WIN), 0)
        kj = lax.broadcasted_iota(jnp.int32, (WIN, WIN), 1)
        valid_n = kj >= qi + jnp.where(b + 1 < nb, 0, WIN)
        every = slice(0, WIN)
        pn = jnp.exp(jnp.where(valid_n, sn_scr[...] + bias[:, :, :WIN], -jnp.inf) - per_head(lsen_ref, every))
        dsn_scr[...] = (pn * (dpn_scr[...] - per_head(ddn_ref, every))).astype(BF16)
        pn_scr[...] = pn.astype(BF16)
        for a, sl in enumerate(pairs):
            k_pa = _pair_split(jnp.concatenate([kp_ref[0, :, sl], kc_ref[0, blk_a, sl]], axis=0))
            k_ab = _pair_split(kc_ref[0, :, sl])
            qa_eo, qb_eo = _pair_split(q_ref[0, blk_a, sl]), _pair_split(q_ref[0, blk_b, sl])
            doa_eo = _pair_split(do_ref[0, blk_a, sl].astype(BF16))
            dob_eo = _pair_split(do_ref[0, blk_b, sl].astype(BF16))
            qn_eo = _pair_split(qn_ref[0, :, sl])
            don_eo = _pair_split(don_ref[0, :, sl].astype(BF16))
            acc = None
            for e in range(2):
                h = 2 * a + e
                terms = (_dot(ds_scr[0, h], k_pa[e]),
                         _dot(ds_scr[1, h], k_ab[e]),
                         _dot_tn(ds_scr[0, h, :, WIN:], qa_eo[e]) + _dot_tn(ds_scr[1, h, :, :WIN], qb_eo[e]),
                         _dot_tn(ds_scr[1, h, :, WIN:], qb_eo[e]) + _dot_tn(dsn_scr[h], qn_eo[e]),
                         _dot_tn(p_scr[0, h, :, WIN:], doa_eo[e]) + _dot_tn(p_scr[1, h, :, :WIN], dob_eo[e]),
                         _dot_tn(p_scr[1, h, :, WIN:], dob_eo[e]) + _dot_tn(pn_scr[h], don_eo[e]))
                acc = terms if acc is None else tuple(x + y for x, y in zip(acc, terms))
            acc = [t.astype(BF16) for t in acc]
            dq_ref[0, blk_a, sl], dq_ref[0, blk_b, sl] = acc[0], acc[1]
            dk_ref[0, blk_a, sl], dk_ref[0, blk_b, sl] = acc[2], acc[3]
            dv_ref[0, blk_a, sl], dv_ref[0, blk_b, sl] = acc[4], acc[5]

    cur = pl.BlockSpec((1, 2 * WIN, 512), lambda r, b: (r, b, 0))
    prev = pl.BlockSpec((1, WIN, 512), lambda r, b: (r, jnp.maximum(2 * b - 1, 0), 0))
    nxt = pl.BlockSpec((1, WIN, 512), lambda r, b: (r, jnp.minimum(2 * b + 2, n128 - 1), 0))
    wide, narrow = (2, N_HEADS, WIN, 2 * WIN), (N_HEADS, WIN, WIN)
    return pl.pallas_call(
        body, name=f"swa_bwd_d{dil}", grid=(dil, nb),
        out_shape=[jax.ShapeDtypeStruct(qc.shape, BF16)] * 3 + [jax.ShapeDtypeStruct(bias.shape, F32)],
        in_specs=[cur, nxt, prev, cur, prev, cur, cur, nxt, cur, nxt, cur, nxt, _full(bias.shape), ANY_SPEC],
        out_specs=[cur] * 3 + [_full(bias.shape)],
        scratch_shapes=[pltpu.VMEM(wide, F32), pltpu.VMEM(wide, F32), pltpu.VMEM(narrow, F32),
                        pltpu.VMEM(narrow, F32), pltpu.VMEM(wide, BF16), pltpu.VMEM(wide, BF16),
                        pltpu.VMEM(narrow, BF16), pltpu.VMEM(narrow, BF16)],
        compiler_params=_cparams(2),
    )(qc, qc, kc, kc, vc, vc, doc, doc, lsec, lsec, ddc, ddc, bias, dep)


def _in_proj_bwd(dqs, dks, dvs, dgb, dcv, gc, xi, w_sc, w_in_g, x, g_mix, dx1):
    s = x.shape[0]
    tb = TM // SUBLANES
    last = s // SUBLANES - 1
    n_tiles = s // TM

    def body(dq1, dq4, dq16, dk1, dk4, dk16, dv1, dv4, dv16, dgb_ref, dcv_ref, dcvn_ref, gc_ref, xi_ref, wsc_ref,
             win_ref, x_ref, g_ref, dx1_ref, dproj_ref, gx_ref, dg_ref, scr_a, scr_b):
        i = pl.program_id(0)

        @pl.when(i == 0)
        def _():
            dg_ref[...] = jnp.zeros_like(dg_ref)

        d0 = dcv_ref[...]
        dn = jnp.where(i < n_tiles - 1, dcvn_ref[...], 0.0)
        du = (d0 * wsc_ref[2:3, :] + _shift_up(d0, dn, 1) * wsc_ref[1:2, :]) + _shift_up(d0, dn, 2) * wsc_ref[0:1, :]
        merge = lambda a, b4, b16: ((a[...].astype(F32) + _gather_classes(b4, scr_a, 4))
                                    + _gather_classes(b16, scr_b, 16))
        dq = merge(dq1, dq4, dq16) * (HEAD_DIM ** -0.5)
        dk = merge(dk1, dk4, dk16)
        dv = merge(dv1, dv4, dv16)
        dproj = jnp.concatenate([dq, dk, dv, dgb_ref[...].astype(F32), du * xi_ref[...], du * gc_ref[...]],
                                axis=1).astype(BF16)
        dproj_ref[...] = dproj
        dh = jnp.zeros((TM, D_MODEL), F32)
        for j in range(N_DEV):
            dh = dh + _dot_nt(dproj[:, j * IN_CHUNK:(j + 1) * IN_CHUNK], win_ref[j])
        xh, r = _rms(x_ref[...])
        dg_ref[0:1, :] += jnp.sum(dh * xh, axis=0, keepdims=True)
        gx_ref[...] = dx1_ref[...] + _rms_bwd(xh, r, g_ref[...], dh)

    row = lambda n: pl.BlockSpec((TM, n), lambda i: (i, 0))
    nxt = pl.BlockSpec((SUBLANES, 512), lambda i: (jnp.minimum((i + 1) * tb, last), 0))
    return pl.pallas_call(
        body, name="in_proj_bwd", grid=(n_tiles,),
        out_shape=[jax.ShapeDtypeStruct((s, IN_COLS), BF16), jax.ShapeDtypeStruct((s, D_MODEL), F32),
                   jax.ShapeDtypeStruct((SUBLANES, D_MODEL), F32)],
        in_specs=_class_specs(512) * 3 + [row(512), row(512), nxt, row(512), row(512), _full(w_sc.shape),
                                          _full(w_in_g.shape), row(D_MODEL), _full(g_mix.shape), row(D_MODEL)],
        out_specs=[row(IN_COLS), row(D_MODEL), _full((SUBLANES, D_MODEL))],
        scratch_shapes=[pltpu.VMEM((512 // LANES, TM, LANES), F32)] * 2,
        compiler_params=_cparams(1),
    )(*dqs, *dks, *dvs, dgb, dcv, dcv, gc, xi, w_sc, w_in_g, x, g_mix, dx1)


def _dw(a, b, dep, name, a_chunked=False, b_chunked=False, n_chunks=1, chunk_cols=None):
    ts = TS_DW if (a_chunked or b_chunked or chunk_cols) else TS_DW // 2
    if a_chunked:
        nj, s, kk = a.shape
        nn = b.shape[1]
        a_spec = pl.BlockSpec((1, ts, kk), lambda j, t: (j, t, 0))
        b_spec = pl.BlockSpec((ts, nn), lambda j, t: (t, 0))
    elif b_chunked:
        nj, s, nn = b.shape
        kk = a.shape[1]
        a_spec = pl.BlockSpec((ts, kk), lambda j, t: (t, 0))
        b_spec = pl.BlockSpec((1, ts, nn), lambda j, t: (j, t, 0))
    else:
        s, kk = a.shape
        nj, nn = (n_chunks, chunk_cols) if chunk_cols else (1, b.shape[1])
        a_spec = pl.BlockSpec((ts, kk), lambda j, t: (t, 0))
        b_spec = pl.BlockSpec((ts, nn), lambda j, t: (t, j))
    n_steps = s // ts

    def body(a_ref, b_ref, dep_ref, o_ref, acc):
        t = pl.program_id(1)

        @pl.when(t == 0)
        def _():
            acc[...] = jnp.zeros_like(acc)

        av = (a_ref[0] if a_chunked else a_ref[...]).astype(BF16)
        bv = (b_ref[0] if b_chunked else b_ref[...]).astype(BF16)
        acc[...] += _dot_tn(av, bv)

        @pl.when(t == n_steps - 1)
        def _():
            o_ref[0] = acc[...].astype(BF16)

    return pl.pallas_call(
        body, name=name, grid=(nj, n_steps),
        out_shape=jax.ShapeDtypeStruct((nj, kk, nn), BF16),
        in_specs=[a_spec, b_spec, ANY_SPEC],
        out_specs=pl.BlockSpec((1, kk, nn), lambda j, t: (j, 0, 0)),
        scratch_shapes=[pltpu.VMEM((kk, nn), F32)],
        compiler_params=_cparams(2),
    )(a, b, dep)


def _adamw_math(w, g, m, v):
    m2 = ADAM_B1 * m + (1.0 - ADAM_B1) * g
    v2 = ADAM_B2 * v + (1.0 - ADAM_B2) * (g * g)
    m_hat = m2 / (1.0 - ADAM_B1 ** ADAM_STEP)
    v_hat = v2 / (1.0 - ADAM_B2 ** ADAM_STEP)
    delta = -ADAM_LR * (m_hat / (jnp.sqrt(v_hat) + ADAM_EPS) + ADAM_WD * w)
    return delta, m2, v2


def _sum_parts(me, own, p_ref):
    g = None
    for i in range(N_DEV):
        part = jnp.where(me == i, own.astype(F32), p_ref[i].astype(F32))
        g = part if g is None else g + part
    return g


def _adamw_big(name, w, sent, parts, m, v, me_arr):
    rr, cc = w.shape
    tr = rr // 4 if rr >= 512 else rr

    def body(me_ref, w_ref, own_ref, p_ref, m_ref, v_ref, g_ref, d_ref, nm_ref, nv_ref):
        g = own_ref[0].astype(F32)
        for k in range(1, N_DEV):
            g = g + p_ref[(me_ref[0] + k) % N_DEV].astype(F32)
        g_ref[...] = g
        d_ref[...], nm_ref[...], nv_ref[...] = _adamw_math(w_ref[...], g, m_ref[...], v_ref[...])

    row = pl.BlockSpec((tr, cc), lambda i, me: (i, 0))
    return pl.pallas_call(
        body, name=name,
        grid_spec=pltpu.PrefetchScalarGridSpec(
            num_scalar_prefetch=1, grid=(rr // tr,),
            in_specs=[row, pl.BlockSpec((1, tr, cc), lambda i, me: (me[0], i, 0)),
                      pl.BlockSpec((N_DEV, tr, cc), lambda i, me: (0, i, 0)), row, row],
            out_specs=[row] * 4),
        out_shape=[jax.ShapeDtypeStruct((rr, cc), F32)] * 4,
        compiler_params=_cparams(1),
    )(me_arr, w, sent, parts, m, v)


def _small_slices():
    return [
        (slice(ROW_RELB, ROW_RELB + 8), slice(0, N_BUCKETS)),
        (slice(ROW_GMIX, ROW_GMIX + 1), slice(0, D_MODEL)),
        (slice(ROW_GAC, ROW_GAC + 1), slice(0, ATTN_W)),
        (slice(ROW_GAC, ROW_GAC + 1), slice(ATTN_W, D_MODEL)),
        (slice(ROW_GXATTN, ROW_GXATTN + 1), slice(0, D_MODEL)),
        (slice(ROW_GMEM, ROW_GMEM + 1), slice(0, D_MODEL)),
        (slice(ROW_GFFN, ROW_GFFN + 1), slice(0, D_MODEL)),
        (slice(ROW_BFC, ROW_BFC + 8), slice(0, UP_CHUNK)),
        (slice(ROW_GFINAL, ROW_GFINAL + 1), slice(0, D_MODEL)),
    ]


def _adamw_small(own, parts, wmv, me_arr):
    slices = _small_slices()
    n = len(slices)

    def body(*refs):
        me_ref, own_ref, p_ref = refs[:3]
        ins = refs[3:3 + 3 * n]
        g_ref = refs[3 + 3 * n]
        outs = refs[4 + 3 * n:]
        g = _sum_parts(me_ref[0], own_ref[...], p_ref)
        g_ref[...] = g
        for a, (rs, ls) in enumerate(slices):
            ga = g[rs, ls]
            outs[4 * a][...] = ga
            outs[4 * a + 1][...], outs[4 * a + 2][...], outs[4 * a + 3][...] = _adamw_math(
                ins[3 * a][...], ga, ins[3 * a + 1][...], ins[3 * a + 2][...])

    vm = pl.BlockSpec(memory_space=pltpu.VMEM)
    flat = [t for trip in wmv for t in trip]
    out_shape = [jax.ShapeDtypeStruct((SMALL_ROWS, D_MODEL), F32)]
    for w, _, _ in wmv:
        out_shape += [jax.ShapeDtypeStruct(w.shape, F32)] * 4
    res = pl.pallas_call(
        body, name="adamw_small", out_shape=out_shape,
        in_specs=[SMEM_SPEC] + [vm] * (2 + 3 * n), out_specs=[vm] * len(out_shape),
    )(me_arr, own, parts, *flat)
    return res[0], [res[1 + 4 * a:5 + 4 * a] for a in range(n)]


def _adamw_shards(items):
    n = len(items)

    def body(*refs):
        for a in range(n):
            w_ref, g_ref, m_ref, v_ref = refs[4 * a:4 * a + 4]
            d_ref, nm_ref, nv_ref = refs[4 * n + 3 * a:4 * n + 3 * a + 3]
            d_ref[...], nm_ref[...], nv_ref[...] = _adamw_math(w_ref[...], g_ref[...], m_ref[...], v_ref[...])

    vm = pl.BlockSpec(memory_space=pltpu.VMEM)
    out_shape = []
    for w, _, _, _ in items:
        out_shape += [jax.ShapeDtypeStruct(w.shape, F32)] * 3
    res = pl.pallas_call(
        body, name="adamw_shards", out_shape=out_shape, in_specs=[vm] * (4 * n), out_specs=[vm] * (3 * n),
    )(*[t for it in items for t in it])
    return [res[3 * a:3 * a + 3] for a in range(n)]


def _mesh_pos():
    return lax.axis_index("x"), lax.axis_index("y"), lax.axis_index("c")


def _dev_index(p):
    return 4 * p[0] + 2 * p[1] + p[2]


def _all_gather(shards):
    n = len(shards)

    def body(*refs):
        ins, outs = refs[:n], refs[n:2 * n]
        send_sems, recv_sems, loc_sems = refs[2 * n:]
        x, y, c = _mesh_pos()
        me, sib = (x, y, c), (x, y, 1 - c)
        chips = [(1 - x, y), (x, 1 - y), (1 - x, 1 - y)]

        def cp(a, k, block, to, src=None):
            dst = outs[a].at[_dev_index(block)]
            return pltpu.make_async_remote_copy(
                src_ref=dst if src is None else src, dst_ref=dst, send_sem=send_sems.at[a, k],
                recv_sem=recv_sems.at[a, k], device_id=to, device_id_type=MESH)

        mine = [pltpu.make_async_copy(ins[a], outs[a].at[_dev_index(me)], loc_sems.at[a]) for a in range(n)]
        for m_ in mine:
            m_.start()
        first = []
        for a in range(n):
            first.append(cp(a, 0, me, sib, src=ins[a]))
            first += [cp(a, 1 + j, me, (*chip, c), src=ins[a]) for j, chip in enumerate(chips)]
        for f in first:
            f.start()
        passed = []
        for a in range(n):
            for j, chip in enumerate(chips):
                cp(a, 1 + j, (*chip, c), me).wait_recv()
                fwd = cp(a, 4 + j, (*chip, c), sib)
                fwd.start()
                passed.append(fwd)
        for a in range(n):
            cp(a, 0, sib, me).wait_recv()
            for j, chip in enumerate(chips):
                cp(a, 4 + j, (*chip, 1 - c), me).wait_recv()
        for f in first + passed:
            f.wait_send()
        for m_ in mine:
            m_.wait()

    hbm = pl.BlockSpec(memory_space=pltpu.HBM)
    return pl.pallas_call(
        body, name="all_gather_weights",
        out_shape=[jax.ShapeDtypeStruct((N_DEV,) + a.shape, a.dtype) for a in shards],
        in_specs=[hbm] * n, out_specs=[hbm] * n,
        scratch_shapes=[pltpu.SemaphoreType.DMA((n, 7)), pltpu.SemaphoreType.DMA((n, 7)),
                        pltpu.SemaphoreType.DMA((n,))],
    )(*shards)


def _peers():
    x, y, c = _mesh_pos()
    return (x, y, c), [((1 - x) if k & 4 else x, (1 - y) if k & 2 else y, (1 - c) if k & 1 else c)
                       for k in range(1, 8)]


def _exchange_copy(src_ref, land_ref, whole, send_sems, recv_sems, a, k, peer, slot):
    src = src_ref if whole else src_ref.at[_dev_index(peer)]
    return pltpu.make_async_remote_copy(
        src_ref=src, dst_ref=land_ref.at[slot], send_sem=send_sems.at[7 * a + k], recv_sem=recv_sems.at[7 * a + k],
        device_id=peer, device_id_type=MESH)


def _exchange_start(name, srcs, whole, dep):
    n = len(srcs)
    lands = [lax.empty(((N_DEV,) + s.shape) if w else s.shape, s.dtype) for s, w in zip(srcs, whole)]

    def body(*refs):
        src_refs, land_refs = refs[:n], refs[n:2 * n]
        send_sems, recv_sems, token = refs[2 * n + 1], refs[2 * n + 2], refs[-1]
        me, peers = _peers()
        for a in range(n):
            for k, peer in enumerate(peers):
                _exchange_copy(src_refs[a], land_refs[a], whole[a], send_sems, recv_sems, a, k, peer,
                               _dev_index(me)).start()
        token[...] = jnp.zeros_like(token)

    res = pl.pallas_call(
        body, name=name,
        out_shape=(pltpu.SemaphoreType.DMA((7 * n,)), pltpu.SemaphoreType.DMA((7 * n,)),
                   *[pltpu.HBM(a.shape, a.dtype) for a in srcs], *[pltpu.HBM(a.shape, a.dtype) for a in lands],
                   jax.ShapeDtypeStruct((SUBLANES, 128), F32)),
        in_specs=[HBM_SPEC] * (2 * n) + [ANY_SPEC],
        out_specs=(SEM_SPEC, SEM_SPEC, *([HBM_SPEC] * (2 * n)), VMEM_SPEC),
        input_output_aliases={i: 2 + i for i in range(2 * n)},
        compiler_params=pltpu.CompilerParams(has_side_effects=DATAFLOW),
    )(*[pltpu.with_memory_space_constraint(a, pltpu.HBM) for a in srcs],
      *[pltpu.with_memory_space_constraint(a, pltpu.HBM) for a in lands], dep)
    return res[0], res[1], list(res[2:2 + n]), list(res[2 + n:2 + 2 * n]), res[-1]


def _exchange_wait(name, started, whole, after, which=None):
    send_sems, recv_sems, srcs, lands, _ = started
    which = list(range(len(srcs))) if which is None else which
    srcs, lands = [srcs[a] for a in which], [lands[a] for a in which]
    n = len(srcs)

    def body(*refs):
        src_refs, land_refs = refs[:n], refs[n:2 * n]
        send_sems, recv_sems = refs[2 * n], refs[2 * n + 1]
        _, peers = _peers()
        for i, a in enumerate(which):
            for k, peer in enumerate(peers):
                cp = _exchange_copy(src_refs[i], land_refs[i], whole[a], send_sems, recv_sems, a, k, peer,
                                    _dev_index(peer))
                cp.wait_send()
                cp.wait_recv()

    res = pl.pallas_call(
        body, name=name,
        out_shape=[pltpu.HBM(a.shape, a.dtype) for a in srcs + lands],
        in_specs=[HBM_SPEC] * (2 * n) + [SEM_SPEC, SEM_SPEC, ANY_SPEC],
        out_specs=[HBM_SPEC] * (2 * n),
        input_output_aliases={i: i for i in range(2 * n)},
        compiler_params=pltpu.CompilerParams(has_side_effects=DATAFLOW),
    )(*srcs, *lands, send_sems, recv_sems, after)
    return list(res[:n]), list(res[n:])


def _gather_start(name, shards, dep):
    n = len(shards)
    lands = [lax.empty((N_DEV,) + a.shape, a.dtype) for a in shards]

    def body(*refs):
        src_refs, land_refs = refs[:n], refs[n:2 * n]
        send_sems, recv_sems, token = refs[2 * n + 1], refs[2 * n + 2], refs[-1]
        x, y, c = _mesh_pos()
        peers = [(x, y, 1 - c), (1 - x, y, c), (x, 1 - y, c), (1 - x, 1 - y, c)]
        for a in range(n):
            for k, peer in enumerate(peers):
                pltpu.make_async_remote_copy(
                    src_ref=src_refs[a], dst_ref=land_refs[a].at[_dev_index((x, y, c))], send_sem=send_sems.at[4 * a + k],
                    recv_sem=recv_sems.at[4 * a + k], device_id=peer, device_id_type=MESH).start()
        token[...] = jnp.zeros_like(token)

    res = pl.pallas_call(
        body, name=name,
        out_shape=(pltpu.SemaphoreType.DMA((4 * n,)), pltpu.SemaphoreType.DMA((4 * n,)),
                   *[pltpu.HBM(a.shape, a.dtype) for a in shards], *[pltpu.HBM(a.shape, a.dtype) for a in lands],
                   jax.ShapeDtypeStruct((SUBLANES, 128), F32)),
        in_specs=[HBM_SPEC] * (2 * n) + [ANY_SPEC],
        out_specs=(SEM_SPEC, SEM_SPEC, *([HBM_SPEC] * (2 * n)), VMEM_SPEC),
        input_output_aliases={i: 2 + i for i in range(2 * n)},
        compiler_params=pltpu.CompilerParams(has_side_effects=DATAFLOW),
    )(*[pltpu.with_memory_space_constraint(a, pltpu.HBM) for a in shards],
      *[pltpu.with_memory_space_constraint(a, pltpu.HBM) for a in lands], dep)
    return res[0], res[1], list(res[2:2 + n]), list(res[2 + n:2 + 2 * n]), res[-1]


def _gather_forward(name, send_sems, recv_sems, lands, which, after):
    n = len(which)

    def body(*refs):
        land_refs = refs[:n]
        send_sems, recv_sems = refs[n], refs[n + 1]
        fsend, frecv, token = refs[n + 3], refs[n + 4], refs[-1]
        x, y, c = _mesh_pos()
        chips = [(1 - x, y), (x, 1 - y), (1 - x, 1 - y)]
        for i, a in enumerate(which):
            for j, chip in enumerate(chips):
                block = land_refs[i].at[_dev_index((*chip, c))]
                pltpu.make_async_remote_copy(
                    src_ref=block, dst_ref=block, send_sem=send_sems.at[4 * a + 1 + j], recv_sem=recv_sems.at[4 * a + 1 + j],
                    device_id=(*chip, c), device_id_type=MESH).wait_recv()
                pltpu.make_async_remote_copy(
                    src_ref=block, dst_ref=block, send_sem=fsend.at[3 * i + j], recv_sem=frecv.at[3 * i + j],
                    device_id=(x, y, 1 - c), device_id_type=MESH).start()
        token[...] = jnp.zeros_like(token)

    res = pl.pallas_call(
        body, name=name,
        out_shape=(pltpu.SemaphoreType.DMA((3 * n,)), pltpu.SemaphoreType.DMA((3 * n,)),
                   *[pltpu.HBM(a.shape, a.dtype) for a in lands], jax.ShapeDtypeStruct((SUBLANES, 128), F32)),
        in_specs=[HBM_SPEC] * n + [SEM_SPEC, SEM_SPEC, ANY_SPEC],
        out_specs=(SEM_SPEC, SEM_SPEC, *([HBM_SPEC] * n), VMEM_SPEC),
        input_output_aliases={i: 2 + i for i in range(n)},
        compiler_params=pltpu.CompilerParams(has_side_effects=DATAFLOW),
    )(*lands, send_sems, recv_sems, after)
    return res[0], res[1], list(res[2:2 + n]), res[-1]


def _gather_wait(name, send_sems, recv_sems, fsend, frecv, srcs, lands, which, after):
    n = len(which)

    def body(*refs):
        land_refs = refs[n:2 * n]
        send_sems, recv_sems, fsend, frecv = refs[2 * n:2 * n + 4]
        x, y, c = _mesh_pos()
        sib = (x, y, 1 - c)
        chips = [(1 - x, y), (x, 1 - y), (1 - x, 1 - y)]
        for i, a in enumerate(which):
            def cp(slot, ssem, rsem):
                block = land_refs[i].at[_dev_index(slot)]
                return pltpu.make_async_remote_copy(src_ref=block, dst_ref=block, send_sem=ssem, recv_sem=rsem,
                                                    device_id=sib, device_id_type=MESH)
            cp(sib, send_sems.at[4 * a], recv_sems.at[4 * a]).wait_recv()
            for j, chip in enumerate(chips):
                cp((*chip, 1 - c), fsend.at[3 * i + j], frecv.at[3 * i + j]).wait_recv()
            for k in range(4):
                cp(sib, send_sems.at[4 * a + k], recv_sems.at[4 * a + k]).wait_send()
            for j in range(3):
                cp(sib, fsend.at[3 * i + j], frecv.at[3 * i + j]).wait_send()

    res = pl.pallas_call(
        body, name=name,
        out_shape=[pltpu.HBM(a.shape, a.dtype) for a in srcs + lands],
        in_specs=[HBM_SPEC] * (2 * n) + [SEM_SPEC] * 4 + [ANY_SPEC],
        out_specs=[HBM_SPEC] * (2 * n),
        input_output_aliases={i: i for i in range(2 * n)},
        compiler_params=pltpu.CompilerParams(has_side_effects=DATAFLOW),
    )(*srcs, *lands, send_sems, recv_sems, fsend, frecv, after)
    return list(res[n:])


def _local_step(x, mem, target, rel_bias, g_mix, w_in_g, w_sc, g_a, g_c, g_xattn, g_mem, g_ffn, w_fc, b_fc, g_final,
                dep, forward_weights, late_weights, emit, emit_small):
    s = x.shape[0]
    buckets = _bucket_tables()
    bias = _bias_fwd(rel_bias, buckets)

    h1, qs, ks, vs, gb, gc, xi = _rms_proj(x, g_mix, w_in_g, dep)
    qs, ks, vs = ([a[0][None]] + list(a[1:]) for a in (qs, ks, vs))
    group1, group2 = ["w_out", "w_xq", "w_xk", "w_xv", "w_xo"], ["w_up", "w_down"]
    tok = forward_weights(group1, h1)
    branches = []
    for p, dil in enumerate(DILATIONS):
        o_p, lse_p = _swa_fwd(qs[p], ks[p], vs[p], bias[p], dil, tok)
        branches.append([o_p[0], lse_p[0]] if dil == 1 else [o_p, lse_p])
    lw = late_weights(group1, branches[-1][0])
    w_out, w_xq, w_xk, w_xv, w_xo = (lw[n] for n in group1)
    attn, lses, mixed, x1 = _mix_out(branches, gb, gc, xi, x, w_sc, g_a, g_c, w_out)
    tok = forward_weights(group2, x1)
    mem_n, mk, mv = _mem_kv(mem, g_mem, w_xk, w_xv)
    h2, xq, xo, x2 = _xattn_fwd(x1, g_xattn, w_xq, mk, mv, w_xo, tok)
    lw = late_weights(group2, x2)
    w_up_g, w_down_g = lw["w_up"], lw["w_down"]
    half = N_DEV // 2
    w_up_p = w_up_g.reshape(2, half, UP_CHUNK, D_MODEL)
    w_fc_p, b_fc_p = w_fc.reshape(2, half, 3, UP_CHUNK), b_fc.reshape(2, half, 1, UP_CHUNK)
    h3, conv, act, dx3, loss_acc, dg_final = _ffn_fwd(x2, g_ffn, w_up_p, w_fc_p, b_fc_p, w_down_g, g_final, target)

    gw_down = _dw(act, dx3, dep, "dw_down", a_chunked=True)
    dup, dx2, dg_ffn, dw_fc, db_fc = _ffn_bwd(dx3, h3, conv, x2, g_ffn, w_up_p, w_fc_p, w_down_g)
    dw_fc, db_fc = dw_fc[:, :, 0, :], db_fc[:, 0, :]
    gw_up = _dw(dup.reshape(N_DEV, s, UP_CHUNK), h3, dep, "dw_up", a_chunked=True)
    tok = emit(dict(w_down=gw_down, w_up=gw_up))
    dxq, dx1, dmk, dmv, dg_xattn = _xattn_bwd(dx2, xo, xq, mk, mv, w_xo, w_xq, x1, g_xattn, tok)
    gw_xo = _dw(xo, dx2, tok, "dw_xo")[0]
    gw_xq = _dw(h2, dxq, tok, "dw_xq")[0]
    gw_xk, gw_xv, dg_mem = _mem_kv_bwd(dmk, dmv, mem_n, mem, w_xk, w_xv)
    tok = emit(dict(w_xo=gw_xo, w_xq=gw_xq, w_xk=gw_xk, w_xv=gw_xv))
    dattns, dds, dgb, dcv, dg_a, dg_c, dw_sc = _mix_out_bwd(dx1, w_out, attn, gb, gc, xi, w_sc, g_a, g_c, tok)
    first = lambda a: [a[0][None]] + list(a[1:])
    dattns, dds, lses = first(dattns), first(dds), first(lses)
    gw_out = _dw(mixed, dx1, tok, "dw_out")[0]
    tok = emit(dict(w_out=gw_out))
    dqs, dks, dvs, dbias = [], [], [], []
    for p, dil in enumerate(DILATIONS):
        dq_p, dk_p, dv_p, db_p = _swa_bwd(qs[p], ks[p], vs[p], dattns[p], lses[p], dds[p], bias[p], dil, tok)
        dqs.append(dq_p[0] if dil == 1 else dq_p)
        dks.append(dk_p[0] if dil == 1 else dk_p)
        dvs.append(dv_p[0] if dil == 1 else dv_p)
        dbias.append(db_p)
    d_relb = _bias_bwd(jnp.stack(dbias), buckets)
    dproj, grad_x, dg_mix = _in_proj_bwd(dqs, dks, dvs, dgb, dcv, gc, xi, w_sc, w_in_g, x, g_mix, dx1)
    pad = lambda a: jnp.pad(a, ((0, 0), (0, D_MODEL - a.shape[1])))
    small = jnp.concatenate([
        d_relb, dg_mix, dg_xattn, dg_mem, dg_ffn, dg_final, jnp.concatenate([dg_a, dg_c], axis=1),
        pad(dw_sc), pad(db_fc), pad(dw_fc.reshape(3 * N_DEV, UP_CHUNK)), pad(loss_acc)], axis=0)
    tok = emit_small(small)
    gw_in = _dw(h1, dproj, tok, "dw_in", n_chunks=N_DEV, chunk_cols=IN_CHUNK)
    emit(dict(w_in=gw_in))
    return grad_x


def kernel(x, mem, rel_bias, g_mix, w_in, w_short_conv, g_attn_out, g_conv_out, w_out, g_xattn, g_mem, w_xq, w_xk, w_xv, w_xo, g_ffn, w_up, w_ffn_conv, b_ffn_conv, w_down, g_final, loss_target, m_rel_bias, m_g_mix, m_w_in, m_w_short_conv, m_g_attn_out, m_g_conv_out, m_w_out, m_g_xattn, m_g_mem, m_w_xq, m_w_xk, m_w_xv, m_w_xo, m_g_ffn, m_w_up, m_w_ffn_conv, m_b_ffn_conv, m_w_down, m_g_final, v_rel_bias, v_g_mix, v_w_in, v_w_short_conv, v_g_attn_out, v_g_conv_out, v_w_out, v_g_xattn, v_g_mem, v_w_xq, v_w_xk, v_w_xv, v_w_xo, v_g_ffn, v_w_up, v_w_ffn_conv, v_b_ffn_conv, v_w_down, v_g_final):
    me = _dev_index(_mesh_pos())
    me_arr = me.reshape(1).astype(jnp.int32)

    big_names = ["w_in", "w_out", "w_xq", "w_xk", "w_xv", "w_xo", "w_up", "w_down"]
    late_names = big_names[1:]
    big_w = dict(w_in=w_in[0], w_out=w_out[0], w_xq=w_xq[0], w_xk=w_xk[0], w_xv=w_xv[0], w_xo=w_xo[0],
                 w_up=w_up[0].T, w_down=w_down[0])
    big_m = dict(w_in=m_w_in[0], w_out=m_w_out[0], w_xq=m_w_xq[0], w_xk=m_w_xk[0], w_xv=m_w_xv[0], w_xo=m_w_xo[0],
                 w_up=m_w_up[0].T, w_down=m_w_down[0])
    big_v = dict(w_in=v_w_in[0], w_out=v_w_out[0], w_xq=v_w_xq[0], w_xk=v_w_xk[0], w_xv=v_w_xv[0], w_xo=v_w_xo[0],
                 w_up=v_w_up[0].T, w_down=v_w_down[0])
    shard_shape = {n: big_w[n].shape for n in big_names}

    w_in_g, w_sc_g, w_fc_full = _all_gather([big_w["w_in"].astype(BF16), w_short_conv[0], w_ffn_conv[0]])
    w_sc_full = w_sc_g.transpose(1, 0, 2).reshape(3, CONV_W)
    late_shards = [big_w[n].astype(BF16) for n in late_names]
    ag_send, ag_recv, ag_srcs, ag_lands, ag_token = _gather_start("gather_weights_start", late_shards, w_in_g)
    forwarded = {}

    def forward_weights(names, after):
        which = [late_names.index(n) for n in names]
        fsend, frecv, lands, token = _gather_forward("gather_" + "_".join(names) + "_forward", ag_send, ag_recv,
                                                     [ag_lands[a] for a in which], which, after)
        forwarded[tuple(names)] = (fsend, frecv, lands)
        return token

    def late_weights(names, after):
        which = [late_names.index(n) for n in names]
        fsend, frecv, lands = forwarded[tuple(names)]
        lands = _gather_wait("gather_" + "_".join(names) + "_wait", ag_send, ag_recv, fsend, frecv,
                             [ag_srcs[a] for a in which], lands, which, after)
        out = {}
        for n, a, land in zip(names, which, lands):
            full = lax.dynamic_update_index_in_dim(land, late_shards[a], me, 0)
            if n == "w_up":
                out[n] = full
            elif n == "w_down":
                out[n] = full.reshape(N_DEV // 2, UP_CHUNK, D_MODEL)
            else:
                out[n] = full.reshape(D_MODEL, D_MODEL)
        return out

    sent = []

    def emit(grads):
        names = list(grads)
        blocks = [grads[n].reshape((N_DEV,) + shard_shape[n]) for n in names]
        started = _exchange_start("scatter_" + "_".join(names) + "_start", blocks, [False] * len(names), me_arr)
        sent.append((names, started))
        return started[-1]

    def emit_small(small):
        sent_small.append((small, _exchange_start("gather_small_start", [small], [True], me_arr)))
        return sent_small[0][1][-1]

    sent_small = []
    grad_x = _local_step(
        x[0], mem[0], loss_target[0], rel_bias, g_mix, w_in_g, w_sc_full, g_attn_out, g_conv_out, g_xattn, g_mem,
        g_ffn, w_fc_full, b_ffn_conv.reshape(N_DEV, 1, UP_CHUNK), g_final.reshape(1, D_MODEL), ag_token,
        forward_weights, late_weights, emit, emit_small)

    small_g, small_started = sent_small[0]
    after = sent[-1][1][-1]
    small_parts = _exchange_wait("gather_small_wait", small_started, [True], after)[1][0]
    big_out = {}
    after = small_parts
    for names, started in sent:
        blocks, lands = _exchange_wait("scatter_" + "_".join(names) + "_wait", started, [False] * len(names), after)
        for n, block, land in zip(names, blocks, lands):
            res = _adamw_big("adamw_" + n, big_w[n], block, land, big_m[n], big_v[n], me_arr)
            big_out[n] = [(r.T if n == "w_up" else r)[None] for r in res]
            after = res[0]

    as_rows = lambda a: a.reshape(N_DEV, UP_CHUNK)
    row1 = lambda a: a.reshape(1, D_MODEL)
    small_names = ["rel_bias", "g_mix", "g_attn_out", "g_conv_out", "g_xattn", "g_mem", "g_ffn", "b_ffn_conv", "g_final"]
    wmv = [
        (rel_bias, m_rel_bias, v_rel_bias), (g_mix, m_g_mix, v_g_mix), (g_attn_out, m_g_attn_out, v_g_attn_out),
        (g_conv_out, m_g_conv_out, v_g_conv_out), (g_xattn, m_g_xattn, v_g_xattn), (g_mem, m_g_mem, v_g_mem),
        (g_ffn, m_g_ffn, v_g_ffn), (as_rows(b_ffn_conv), as_rows(m_b_ffn_conv), as_rows(v_b_ffn_conv)),
        (row1(g_final), row1(m_g_final), row1(v_g_final))]
    g_packed, small_res = _adamw_small(small_g, small_parts, wmv, me_arr)
    small_out = dict(zip(small_names, small_res))
    loss = g_packed[ROW_LOSS, 0]
    small_out["b_ffn_conv"] = [a.reshape(1, 2 * D_FF) for a in small_out["b_ffn_conv"]]
    small_out["g_final"] = [a.reshape(D_MODEL) for a in small_out["g_final"]]

    g_wsc = lax.dynamic_slice(g_packed[ROW_WSC:ROW_WSC + 3, 0:CONV_W], (0, me * HEAD_DIM), (3, HEAD_DIM))
    g_wfc = lax.dynamic_slice(g_packed[ROW_WFC:ROW_WFC + 3 * N_DEV, 0:UP_CHUNK].reshape(3, N_DEV, UP_CHUNK),
                              (0, me, 0), (3, 1, UP_CHUNK)).reshape(3, UP_CHUNK)
    shard_res = _adamw_shards([(w_short_conv[0], g_wsc, m_w_short_conv[0], v_w_short_conv[0]),
                               (w_ffn_conv[0], g_wfc, m_w_ffn_conv[0], v_w_ffn_conv[0])])
    small_out["w_short_conv"] = [g_wsc[None]] + [a[None] for a in shard_res[0]]
    small_out["w_ffn_conv"] = [g_wfc[None]] + [a[None] for a in shard_res[1]]

    order = ["rel_bias", "g_mix", "w_in", "w_short_conv", "g_attn_out", "g_conv_out", "w_out", "g_xattn", "g_mem",
             "w_xq", "w_xk", "w_xv", "w_xo", "g_ffn", "w_up", "w_ffn_conv", "b_ffn_conv", "w_down", "g_final"]
    allp = {**big_out, **small_out}
    outs = [loss, grad_x[None]]
    for kind in range(4):
        outs += [allp[n][kind] for n in order]
    return tuple(outs)
```

```python
import math

import numpy as np
import jax
import jax.numpy as jnp
from jax import lax
from jax.experimental import pallas as pl
from jax.experimental.pallas import tpu as pltpu

F32 = jnp.float32
BF16 = jnp.bfloat16
MESH = pl.DeviceIdType.MESH

N_DEV = 8
D_MODEL = 1024
ATTN_W = 512
CONV_W = 512
N_HEADS = 8
HEAD_DIM = 64
WIN = 128
DILATIONS = (1, 4, 16)
N_BUCKETS = 32
BUCKET_MAX_EXACT = 16
BUCKET_MAX_DISTANCE = 2048
N_MEM_HEADS = 4
MEM_HEAD_DIM = 256
D_FF = 2816
IN_COLS = 3072
IN_CHUNK = IN_COLS // N_DEV
UP_CHUNK = 2 * D_FF // N_DEV
EPS = 1e-6

ADAM_LR = 0.001
ADAM_B1 = 0.9
ADAM_B2 = 0.999
ADAM_EPS = 1e-08
ADAM_WD = 0.01
ADAM_STEP = 10

SUBLANES = 8
LANES = 128
HALO = 16
TM = 512
TM_FFN = 256
TS_DW = 4096
SWA_BLOCKS = 4
VMEM_LIMIT = 56 * 1024 * 1024

ROW_RELB, ROW_GMIX, ROW_GXATTN, ROW_GMEM, ROW_GFFN, ROW_GFINAL, ROW_GAC = 0, 8, 16, 24, 32, 40, 48
ROW_WSC, ROW_BFC, ROW_WFC, ROW_LOSS, SMALL_ROWS = 56, 64, 72, 96, 104


def _cparams(n_grid):
    return pltpu.CompilerParams(dimension_semantics=("arbitrary",) * n_grid, vmem_limit_bytes=VMEM_LIMIT)


def _full(shape):
    nd = len(shape)
    return pl.BlockSpec(tuple(shape), lambda *_: (0,) * nd)


def _resident(shape):
    nd = len(shape)
    return pl.BlockSpec(tuple(shape), lambda *_: (0,) * nd, pipeline_mode=pl.Buffered(1))


ANY_SPEC = pl.BlockSpec(memory_space=pl.ANY)
HBM_SPEC = pl.BlockSpec(memory_space=pltpu.HBM)
SEM_SPEC = pl.BlockSpec(memory_space=pltpu.SEMAPHORE)
VMEM_SPEC = pl.BlockSpec(memory_space=pltpu.VMEM)
SMEM_SPEC = pl.BlockSpec(memory_space=pltpu.SMEM)
DATAFLOW = pltpu.SideEffectType.DATAFLOW_SIDE_EFFECTING


def _rms(x):
    r = lax.rsqrt(jnp.mean(x * x, axis=-1, keepdims=True) + EPS)
    return x * r, r


def _rms_bwd(xh, r, g, dy):
    dxh = dy * g
    return r * (dxh - xh * jnp.mean(dxh * xh, axis=-1, keepdims=True))


def _shift_down(u, halo, k):
    ru = pltpu.roll(u, k, 0)
    rh = pltpu.roll(halo, k, 0)
    row = lax.broadcasted_iota(jnp.int32, rh.shape, 0)
    head = jnp.where(row < k, rh, ru[0:SUBLANES])
    return jnp.concatenate([head, ru[SUBLANES:]], axis=0)


def _shift_up(u, halo, k):
    tm = u.shape[0]
    ru = pltpu.roll(u, tm - k, 0)
    rh = pltpu.roll(halo, SUBLANES - k, 0)
    row = lax.broadcasted_iota(jnp.int32, rh.shape, 0)
    tail = jnp.where(row >= SUBLANES - k, rh, ru[tm - SUBLANES:])
    return jnp.concatenate([ru[:tm - SUBLANES], tail], axis=0)


def _causal_conv3(u, halo, w_ref):
    return (_shift_down(u, halo, 2) * w_ref[0:1, :] + _shift_down(u, halo, 1) * w_ref[1:2, :]) + u * w_ref[2:3, :]


def _dot(a, b):
    return jnp.dot(a, b, preferred_element_type=F32)


def _dot_nt(a, b):
    return lax.dot_general(a, b, (((1,), (1,)), ((), ())), preferred_element_type=F32)


def _dot_tn(a, b):
    return lax.dot_general(a, b, (((0,), (0,)), ((), ())), preferred_element_type=F32)


def _sigmoid(x):
    return 0.5 * jnp.tanh(0.5 * x) + 0.5


def _bucket_tables():
    qi = np.arange(WIN)[:, None]
    kj = np.arange(2 * WIN)[None, :]
    steps = np.clip(qi + WIN - kj, 0, WIN)
    out = []
    for d in DILATIONS:
        dist = steps * d
        dd = np.maximum(dist, 1).astype(np.float32)
        large = BUCKET_MAX_EXACT + (
            np.log(dd / np.float32(BUCKET_MAX_EXACT)) / np.float32(math.log(BUCKET_MAX_DISTANCE / BUCKET_MAX_EXACT))
            * np.float32(N_BUCKETS - BUCKET_MAX_EXACT)).astype(np.int32)
        large = np.minimum(large, N_BUCKETS - 1)
        out.append(np.where(dist < BUCKET_MAX_EXACT, dist, large).astype(np.int32))
    return np.stack(out)


def _bias_fwd(rel_bias, buckets):
    present = [sorted(set(buckets[p].ravel().tolist())) for p in range(3)]

    def body(rb_ref, bk_ref, o_ref):
        for p in range(3):
            bk = bk_ref[p]
            for h in range(N_HEADS):
                acc = jnp.zeros((WIN, 2 * WIN), F32)
                for b in present[p]:
                    acc = jnp.where(bk == b, rb_ref[h, b], acc)
                o_ref[p, h] = acc

    return pl.pallas_call(
        body, name="bias_fwd",
        out_shape=jax.ShapeDtypeStruct((3, N_HEADS, WIN, 2 * WIN), F32),
        in_specs=[pl.BlockSpec(memory_space=pltpu.SMEM), pl.BlockSpec(memory_space=pltpu.VMEM)],
        out_specs=pl.BlockSpec(memory_space=pltpu.VMEM),
    )(rel_bias, jnp.asarray(buckets))


def _bias_bwd(dbias, buckets):
    present = [set(buckets[p].ravel().tolist()) for p in range(3)]

    def body(db_ref, bk_ref, o_ref):
        lane = lax.broadcasted_iota(jnp.int32, (1, D_MODEL), 1)
        rows = []
        for h in range(N_HEADS):
            row = jnp.zeros((1, D_MODEL), F32)
            for b in range(N_BUCKETS):
                tot = jnp.zeros((1, 1), F32)
                for p in (p for p in range(3) if b in present[p]):
                    sel = jnp.where(bk_ref[p] == b, db_ref[p, h], 0.0)
                    tot = tot + jnp.sum(jnp.sum(sel, axis=0, keepdims=True), axis=1, keepdims=True)
                row = jnp.where(lane == b, tot, row)
            rows.append(row)
        o_ref[...] = jnp.concatenate(rows, axis=0)

    return pl.pallas_call(
        body, name="bias_bwd",
        out_shape=jax.ShapeDtypeStruct((N_HEADS, D_MODEL), F32),
        in_specs=[pl.BlockSpec(memory_space=pltpu.VMEM), pl.BlockSpec(memory_space=pltpu.VMEM)],
        out_specs=pl.BlockSpec(memory_space=pltpu.VMEM),
    )(dbias, jnp.asarray(buckets))


def _spread(val, scr_ref, out_refs, dtype):
    out_refs[0][...] = val.astype(dtype)
    n_blk = val.shape[1] // LANES
    for c in range(n_blk):
        scr_ref[c] = val[:, c * LANES:(c + 1) * LANES]
    for o_ref, d in zip(out_refs[1:], DILATIONS[1:]):
        for r in range(d):
            for c in range(n_blk):
                o_ref[r, :, c * LANES:(c + 1) * LANES] = scr_ref.at[c][pl.ds(r, TM // d, stride=d), :].astype(dtype)


def _gather_classes(blk_ref, scr_ref, d):
    n_blk = blk_ref.shape[2] // LANES
    for r in range(d):
        for c in range(n_blk):
            scr_ref.at[c][pl.ds(r, TM // d, stride=d), :] = blk_ref[r, :, c * LANES:(c + 1) * LANES].astype(F32)
    return jnp.concatenate([scr_ref[c] for c in range(n_blk)], axis=1)


def _class_specs(cols):
    return [pl.BlockSpec((TM, cols), lambda i: (i, 0))] + [
        pl.BlockSpec((d, TM // d, cols), lambda i: (0, i, 0)) for d in DILATIONS[1:]]


def _class_shapes(s, cols, dtype):
    return [jax.ShapeDtypeStruct((s, cols), dtype)] + [
        jax.ShapeDtypeStruct((d, s // d, cols), dtype) for d in DILATIONS[1:]]


def _rms_proj(x, g_mix, w_in_g, dep):
    s = x.shape[0]

    def body(x_ref, g_ref, w_ref, dep_ref, h_ref, q1, q4, q16, k1, k4, k16, v1, v4, v16, gb_ref, gc_ref, xi_ref, scr):
        xh, _ = _rms(x_ref[...])
        h = (xh * g_ref[...]).astype(BF16)
        h_ref[...] = h
        proj = jnp.concatenate([_dot(h, w_ref[j]) for j in range(N_DEV)], axis=1)
        _spread(proj[:, 0:512] * (HEAD_DIM ** -0.5), scr, (q1, q4, q16), BF16)
        _spread(proj[:, 512:1024], scr, (k1, k4, k16), BF16)
        _spread(proj[:, 1024:1536], scr, (v1, v4, v16), BF16)
        gb_ref[...] = proj[:, 1536:2048]
        gc_ref[...] = proj[:, 2048:2560]
        xi_ref[...] = proj[:, 2560:3072]

    row = lambda n: pl.BlockSpec((TM, n), lambda i: (i, 0))
    res = pl.pallas_call(
        body, name="rms_proj", grid=(s // TM,),
        out_shape=[jax.ShapeDtypeStruct((s, D_MODEL), BF16)] + _class_shapes(s, 512, BF16) * 3
        + [jax.ShapeDtypeStruct((s, 512), F32)] * 3,
        in_specs=[row(D_MODEL), _full(g_mix.shape), _full(w_in_g.shape), ANY_SPEC],
        out_specs=[row(D_MODEL)] + _class_specs(512) * 3 + [row(512)] * 3,
        scratch_shapes=[pltpu.VMEM((512 // LANES, TM, LANES), F32)],
        compiler_params=_cparams(1),
    )(x, g_mix, w_in_g, dep)
    return res[0], res[1:4], res[4:7], res[7:10], res[10], res[11], res[12]


def _pair_split(x2):
    lane = lax.broadcasted_iota(jnp.int32, x2.shape, 1)
    zero = jnp.zeros_like(x2)
    return jnp.where(lane < HEAD_DIM, x2, zero), jnp.where(lane >= HEAD_DIM, x2, zero)


def _pair_join(even, odd):
    lane = lax.broadcasted_iota(jnp.int32, (even.shape[0], LANES), 1)
    return jnp.where(lane < HEAD_DIM, even, odd)


def _band_mask(first):
    qi = lax.broadcasted_iota(jnp.int32, (WIN, 2 * WIN), 0)
    kj = lax.broadcasted_iota(jnp.int32, (WIN, 2 * WIN), 1)
    steps = qi + WIN - kj
    return (steps >= 0) & (steps <= WIN) & (kj >= jnp.where(first, WIN, 0))


def _swa_fwd(qc, kc, vc, bias, dil, dep):
    nsub = min(SWA_BLOCKS, qc.shape[1] // WIN)
    nb = qc.shape[1] // (nsub * WIN)

    def body(q_ref, kp_ref, kc_ref, vp_ref, vc_ref, b_ref, dep_ref, o_ref, lse_ref, s_scr, p_scr):
        b = pl.program_id(1)
        pairs = [slice(a * LANES, (a + 1) * LANES) for a in range(N_HEADS // 2)]
        for sub in range(nsub):
            rows = slice(sub * WIN, (sub + 1) * WIN)

            def keys(prev_ref, cur_ref, sl):
                if sub == 0:
                    return jnp.concatenate([prev_ref[0, :, sl], cur_ref[0, 0:WIN, sl]], axis=0)
                return cur_ref[0, (sub - 1) * WIN:(sub + 1) * WIN, sl]

            for a, sl in enumerate(pairs):
                k2 = keys(kp_ref, kc_ref, sl)
                for e, qh in enumerate(_pair_split(q_ref[0, rows, sl])):
                    s_scr[sub, 2 * a + e] = _dot_nt(qh, k2)
            first = (b == 0) if sub == 0 else False
            lg = jnp.where(_band_mask(first), s_scr[sub] + b_ref[...], -jnp.inf)
            m = jnp.max(lg, axis=-1, keepdims=True)
            p = jnp.exp(lg - m)
            den = jnp.sum(p, axis=-1, keepdims=True)
            p_scr[sub] = p.astype(BF16)
            lse = m + jnp.log(den)
            for a, sl in enumerate(pairs):
                v_even, v_odd = _pair_split(keys(vp_ref, vc_ref, sl))
                o2 = _dot(p_scr[sub, 2 * a], v_even) + _dot(p_scr[sub, 2 * a + 1], v_odd)
                o_ref[0, rows, sl] = o2 / _pair_join(den[2 * a], den[2 * a + 1])
                lse_ref[0, rows, sl] = _pair_join(lse[2 * a], lse[2 * a + 1])

    cur = pl.BlockSpec((1, nsub * WIN, 512), lambda r, b: (r, b, 0))
    prev = pl.BlockSpec((1, WIN, 512), lambda r, b: (r, jnp.maximum(nsub * b - 1, 0), 0))
    wide = (nsub, N_HEADS, WIN, 2 * WIN)
    return pl.pallas_call(
        body, name=f"swa_fwd_d{dil}", grid=(dil, nb),
        out_shape=[jax.ShapeDtypeStruct(qc.shape, F32)] * 2,
        in_specs=[cur, prev, cur, prev, cur, _full(bias.shape), ANY_SPEC],
        out_specs=[cur] * 2,
        scratch_shapes=[pltpu.VMEM(wide, F32), pltpu.VMEM(wide, BF16)],
        compiler_params=_cparams(2),
    )(qc, kc, kc, vc, vc, bias, dep)


def _mix_out(branches, gb, gc, xi, x, w_sc, g_a, g_c, w_out):
    s = x.shape[0]
    tb = TM // SUBLANES

    def body(o1, l1, o4, l4, o16, l16, gb_ref, gc_ref, xi_ref, gch_ref, xih_ref, x_ref, wsc_ref,
             ga_ref, gcv_ref, wout_ref, attn_ref, lse1, lse4, lse16, mixed_ref, x1_ref, scr_a, scr_b, scr_c, scr_d):
        i = pl.program_id(0)
        la, lb, lc = l1[...], _gather_classes(l4, scr_a, 4), _gather_classes(l16, scr_b, 16)
        m_all = jnp.maximum(jnp.maximum(la, lb), lc)
        ea, eb, ec = jnp.exp(la - m_all), jnp.exp(lb - m_all), jnp.exp(lc - m_all)
        den = (ea + eb) + ec
        num = (ea * o1[...] + eb * _gather_classes(o4, scr_c, 4)) + ec * _gather_classes(o16, scr_d, 16)
        attn = num / den
        attn_ref[...] = attn
        _spread(m_all + jnp.log(den), scr_a, (lse1, lse4, lse16), F32)
        xa, _ = _rms(attn)
        u = gc_ref[...] * xi_ref[...]
        uh = jnp.where(i > 0, gch_ref[...] * xih_ref[...], 0.0)
        conv = gb_ref[...] * _causal_conv3(u, uh, wsc_ref)
        xc, _ = _rms(conv)
        mixed = jnp.concatenate([xa * ga_ref[...], xc * gcv_ref[...]], axis=1).astype(BF16)
        mixed_ref[...] = mixed
        x1_ref[...] = x_ref[...] + _dot(mixed, wout_ref[...])

    row = lambda n: pl.BlockSpec((TM, n), lambda i: (i, 0))
    halo = pl.BlockSpec((SUBLANES, 512), lambda i: (jnp.maximum(i * tb - 1, 0), 0))
    cs = _class_specs(512)
    flat = [a for br in branches for a in br]
    res = pl.pallas_call(
        body, name="mix_out", grid=(s // TM,),
        out_shape=[jax.ShapeDtypeStruct((s, 512), F32)] + _class_shapes(s, 512, F32)
        + [jax.ShapeDtypeStruct((s, D_MODEL), BF16), jax.ShapeDtypeStruct((s, D_MODEL), F32)],
        in_specs=[cs[0], cs[0], cs[1], cs[1], cs[2], cs[2], row(512), row(512), row(512), halo, halo,
                  row(D_MODEL), _full(w_sc.shape), _full(g_a.shape), _full(g_c.shape), _full(w_out.shape)],
        out_specs=[row(512)] + cs + [row(D_MODEL), row(D_MODEL)],
        scratch_shapes=[pltpu.VMEM((512 // LANES, TM, LANES), F32)] * 4,
        compiler_params=_cparams(1),
    )(*flat, gb, gc, xi, gc, xi, x, w_sc, g_a, g_c, w_out)
    return res[0], res[1:4], res[4], res[5]


def _mem_kv(mem, g_mem, w_xk, w_xv):
    def body(mem_ref, g_ref, wk_ref, wv_ref, mn_ref, k_ref, v_ref):
        xh, _ = _rms(mem_ref[...])
        mn = (xh * g_ref[...]).astype(BF16)
        mn_ref[...] = mn
        k_ref[...] = _dot(mn, wk_ref[...]).astype(BF16)
        v_ref[...] = _dot(mn, wv_ref[...]).astype(BF16)

    vm = pl.BlockSpec(memory_space=pltpu.VMEM)
    return pl.pallas_call(
        body, name="mem_kv",
        out_shape=[jax.ShapeDtypeStruct(mem.shape, BF16)] * 3,
        in_specs=[vm] * 4, out_specs=[vm] * 3,
        compiler_params=pltpu.CompilerParams(vmem_limit_bytes=VMEM_LIMIT),
    )(mem, g_mem, w_xk, w_xv)


def _xattn_fwd(x1, g, w_xq, k, v, w_xo, dep):
    s = x1.shape[0]

    def body(x1_ref, g_ref, wq_ref, k_ref, v_ref, wo_ref, dep_ref, h2_ref, q_ref, o_ref, x2_ref):
        x1v = x1_ref[...]
        xh, _ = _rms(x1v)
        h2 = (xh * g_ref[...]).astype(BF16)
        h2_ref[...] = h2
        qb = _dot(h2, wq_ref[...]).astype(BF16)
        q_ref[...] = qb
        outs = []
        for h in range(N_MEM_HEADS):
            sl = slice(h * MEM_HEAD_DIM, (h + 1) * MEM_HEAD_DIM)
            lg = _dot_nt(qb[:, sl], k_ref[:, sl]) * (MEM_HEAD_DIM ** -0.5)
            p = jnp.exp(lg - jnp.max(lg, axis=-1, keepdims=True))
            p = p / jnp.sum(p, axis=-1, keepdims=True)
            outs.append(_dot(p.astype(BF16), v_ref[:, sl]))
        o = jnp.concatenate(outs, axis=1).astype(BF16)
        o_ref[...] = o
        x2_ref[...] = x1v + _dot(o, wo_ref[...])

    row = pl.BlockSpec((TM, D_MODEL), lambda i: (i, 0))
    return pl.pallas_call(
        body, name="xattn_fwd", grid=(s // TM,),
        out_shape=[jax.ShapeDtypeStruct((s, D_MODEL), BF16)] * 3 + [jax.ShapeDtypeStruct((s, D_MODEL), F32)],
        in_specs=[row, _full(g.shape), _full(w_xq.shape), _full(k.shape), _full(v.shape), _full(w_xo.shape), ANY_SPEC],
        out_specs=[row] * 4,
        compiler_params=_cparams(1),
    )(x1, g, w_xq, k, v, w_xo, dep)


def _ffn_conv(h_ext, wup_ref, wfc_ref, bfc_ref, j):
    u = _dot_nt(h_ext, wup_ref[j])
    w = wfc_ref[j]
    c = ((pltpu.roll(u, 2, 0) * w[0:1, :] + pltpu.roll(u, 1, 0) * w[1:2, :]) + u * w[2:3, :]) + bfc_ref[j]
    return c[HALO:]


def _ffn_fwd(x2, g, w_up_g, w_fc, b_fc, w_down_g, g_final, target):
    s = x2.shape[0]
    tb = TM_FFN // HALO
    half = N_DEV // 2

    def body(x_ref, xp_ref, g_ref, wup_ref, wfc_ref, bfc_ref, wd_ref, gf_ref, t_ref, h_ref, c_ref, act_ref, dx3_ref,
             loss_ref, dgf_ref):
        i = pl.program_id(0)

        @pl.when(i == 0)
        def _():
            loss_ref[...] = jnp.zeros_like(loss_ref)
            dgf_ref[...] = jnp.zeros_like(dgf_ref)

        x2v = x_ref[...]
        gv = g_ref[...]
        h = (_rms(x2v)[0] * gv).astype(BF16)
        h_ref[...] = h
        hp = jnp.where(i > 0, _rms(xp_ref[...])[0] * gv, 0.0).astype(BF16)
        h_ext = jnp.concatenate([hp, h], axis=0)
        down = jnp.zeros((TM_FFN, D_MODEL), F32)
        for j in range(half):
            cg = _ffn_conv(h_ext, wup_ref, wfc_ref, bfc_ref, j)
            cv = _ffn_conv(h_ext, wup_ref, wfc_ref, bfc_ref, j + half)
            c_ref[j] = cg
            c_ref[j + half] = cv
            a = ((cg * _sigmoid(cg)) * cv).astype(BF16)
            act_ref[j] = a
            down = down + _dot(a, wd_ref[j])
        x3 = x2v + down
        xh, r = _rms(x3)
        gf = gf_ref[...]
        e = xh * gf - t_ref[...]
        loss_ref[...] += 0.5 * jnp.sum(jnp.sum(e * e, axis=1, keepdims=True), axis=0, keepdims=True) / D_MODEL
        dy = e * (1.0 / D_MODEL)
        dgf_ref[0:1, :] += jnp.sum(dy * xh, axis=0, keepdims=True)
        dx3_ref[...] = _rms_bwd(xh, r, gf, dy)

    row = pl.BlockSpec((TM_FFN, D_MODEL), lambda i: (i, 0))
    prev = pl.BlockSpec((HALO, D_MODEL), lambda i: (jnp.maximum(i * tb - 1, 0), 0))
    return pl.pallas_call(
        body, name="ffn_fwd", grid=(s // TM_FFN,),
        out_shape=[jax.ShapeDtypeStruct((s, D_MODEL), BF16), jax.ShapeDtypeStruct((N_DEV, s, UP_CHUNK), F32),
                   jax.ShapeDtypeStruct((half, s, UP_CHUNK), BF16),
                   jax.ShapeDtypeStruct((s, D_MODEL), F32), jax.ShapeDtypeStruct((SUBLANES, 128), F32),
                   jax.ShapeDtypeStruct((SUBLANES, D_MODEL), F32)],
        in_specs=[row, prev, _full(g.shape), _resident(w_up_g.shape), _full(w_fc.shape), _full(b_fc.shape),
                  _resident(w_down_g.shape), _full(g_final.shape), row],
        out_specs=[row, pl.BlockSpec((N_DEV, TM_FFN, UP_CHUNK), lambda i: (0, i, 0)),
                   pl.BlockSpec((half, TM_FFN, UP_CHUNK), lambda i: (0, i, 0)), row,
                   _full((SUBLANES, 128)), _full((SUBLANES, D_MODEL))],
        compiler_params=_cparams(1),
    )(x2, x2, g, w_up_g, w_fc, b_fc, w_down_g, g_final, target)


def _ffn_bwd(dx3, h3, conv, x2, g, w_up_g, w_fc, w_down_g):
    s = x2.shape[0]
    tb = TM_FFN // HALO
    last = s // HALO - 1
    n_tiles = s // TM_FFN
    half = N_DEV // 2
    n_ext = TM_FFN + HALO

    def body(dx_ref, dxn_ref, h_ref, c_ref, cn_ref, x2_ref, g_ref, wup_ref, wfc_ref, wd_ref,
             dup_ref, dx2_ref, dg_ref, dwfc_ref, dbfc_ref):
        i = pl.program_id(0)

        @pl.when(i == 0)
        def _():
            dg_ref[...] = jnp.zeros_like(dg_ref)
            dwfc_ref[...] = jnp.zeros_like(dwfc_ref)
            dbfc_ref[...] = jnp.zeros_like(dbfc_ref)

        dxv = dx_ref[...]
        dxn = jnp.where(i < n_tiles - 1, dxn_ref[...], 0.0)
        dx_ext = jnp.concatenate([dxv, dxn], axis=0).astype(BF16)
        h = h_ref[...]
        dh = jnp.zeros((TM_FFN, D_MODEL), F32)
        for j in range(half):
            cg = jnp.concatenate([c_ref[j], cn_ref[j]], axis=0)
            cv = jnp.concatenate([c_ref[j + half], cn_ref[j + half]], axis=0)
            dact = _dot_nt(dx_ext, wd_ref[j])
            sg = _sigmoid(cg)
            parts = ((j + half, dact * (cg * sg)), (j, (dact * cv) * (sg * (1.0 + cg * (1.0 - sg)))))
            for jj, dc in parts:
                u = _dot_nt(h, wup_ref[jj])
                dc0, dc1, dc2 = dc[:TM_FFN], pltpu.roll(dc, n_ext - 1, 0)[:TM_FFN], pltpu.roll(dc, n_ext - 2, 0)[:TM_FFN]
                dbfc_ref[jj:jj + 1, :] += jnp.sum(dc0, axis=0, keepdims=True)
                dwfc_ref[0, jj:jj + 1, :] += jnp.sum(dc2 * u, axis=0, keepdims=True)
                dwfc_ref[1, jj:jj + 1, :] += jnp.sum(dc1 * u, axis=0, keepdims=True)
                dwfc_ref[2, jj:jj + 1, :] += jnp.sum(dc0 * u, axis=0, keepdims=True)
                w = wfc_ref[jj]
                du = ((dc0 * w[2:3, :] + dc1 * w[1:2, :]) + dc2 * w[0:1, :]).astype(BF16)
                dup_ref[jj] = du
                dh = dh + _dot(du, wup_ref[jj])
        xh, r = _rms(x2_ref[...])
        dg_ref[0:1, :] += jnp.sum(dh * xh, axis=0, keepdims=True)
        dx2_ref[...] = dxv + _rms_bwd(xh, r, g_ref[...], dh)

    row = pl.BlockSpec((TM_FFN, D_MODEL), lambda i: (i, 0))
    nxt = pl.BlockSpec((HALO, D_MODEL), lambda i: (jnp.minimum((i + 1) * tb, last), 0))
    cur_c = pl.BlockSpec((N_DEV, TM_FFN, UP_CHUNK), lambda i: (0, i, 0))
    nxt_c = pl.BlockSpec((N_DEV, HALO, UP_CHUNK), lambda i: (0, jnp.minimum((i + 1) * tb, last), 0))
    return pl.pallas_call(
        body, name="ffn_bwd", grid=(n_tiles,),
        out_shape=[jax.ShapeDtypeStruct((N_DEV, s, UP_CHUNK), BF16), jax.ShapeDtypeStruct((s, D_MODEL), F32),
                   jax.ShapeDtypeStruct((SUBLANES, D_MODEL), F32), jax.ShapeDtypeStruct((3, N_DEV, UP_CHUNK), F32),
                   jax.ShapeDtypeStruct((N_DEV, UP_CHUNK), F32)],
        in_specs=[row, nxt, row, cur_c, nxt_c, row, _full(g.shape), _resident(w_up_g.shape), _full(w_fc.shape),
                  _resident(w_down_g.shape)],
        out_specs=[cur_c, row, _full((SUBLANES, D_MODEL)), _full((3, N_DEV, UP_CHUNK)), _full((N_DEV, UP_CHUNK))],
        compiler_params=_cparams(1),
    )(dx3, dx3, h3, conv, conv, x2, g, w_up_g, w_fc, w_down_g)


def _xattn_bwd(dx2, o, q, k, v, w_xo, w_xq, x1, g, dep):
    s = x1.shape[0]

    def body(dx2_ref, o_ref, q_ref, k_ref, v_ref, wo_ref, wq_ref, x1_ref, g_ref, dep_ref, dq_ref, dx1_ref, dk_ref,
             dv_ref, dg_ref):
        @pl.when(pl.program_id(0) == 0)
        def _():
            dk_ref[...] = jnp.zeros_like(dk_ref)
            dv_ref[...] = jnp.zeros_like(dv_ref)
            dg_ref[...] = jnp.zeros_like(dg_ref)

        dx2v = dx2_ref[...]
        do = _dot_nt(dx2v.astype(BF16), wo_ref[...])
        dqs = []
        for h in range(N_MEM_HEADS):
            sl = slice(h * MEM_HEAD_DIM, (h + 1) * MEM_HEAD_DIM)
            qh, kh, vh = q_ref[:, sl], k_ref[:, sl], v_ref[:, sl]
            lg = _dot_nt(qh, kh) * (MEM_HEAD_DIM ** -0.5)
            p = jnp.exp(lg - jnp.max(lg, axis=-1, keepdims=True))
            p = p / jnp.sum(p, axis=-1, keepdims=True)
            doh = do[:, sl].astype(BF16)
            dp = _dot_nt(doh, vh)
            ds = (p * (dp - jnp.sum(p * dp, axis=-1, keepdims=True)) * (MEM_HEAD_DIM ** -0.5)).astype(BF16)
            dqs.append(_dot(ds, kh))
            dk_ref[:, sl] += _dot_tn(ds, qh)
            dv_ref[:, sl] += _dot_tn(p.astype(BF16), doh)
        dq = jnp.concatenate(dqs, axis=1).astype(BF16)
        dq_ref[...] = dq
        dh2 = _dot_nt(dq, wq_ref[...])
        xh, r = _rms(x1_ref[...])
        dg_ref[0:1, :] += jnp.sum(dh2 * xh, axis=0, keepdims=True)
        dx1_ref[...] = dx2v + _rms_bwd(xh, r, g_ref[...], dh2)

    row = pl.BlockSpec((TM, D_MODEL), lambda i: (i, 0))
    return pl.pallas_call(
        body, name="xattn_bwd", grid=(s // TM,),
        out_shape=[jax.ShapeDtypeStruct((s, D_MODEL), BF16), jax.ShapeDtypeStruct((s, D_MODEL), F32),
                   jax.ShapeDtypeStruct(k.shape, F32), jax.ShapeDtypeStruct(k.shape, F32),
                   jax.ShapeDtypeStruct((SUBLANES, D_MODEL), F32)],
        in_specs=[row, row, row, _full(k.shape), _full(v.shape), _full(w_xo.shape), _full(w_xq.shape), row,
                  _full(g.shape), ANY_SPEC],
        out_specs=[row, row, _full(k.shape), _full(k.shape), _full((SUBLANES, D_MODEL))],
        compiler_params=_cparams(1),
    )(dx2, o, q, k, v, w_xo, w_xq, x1, g, dep)


def _mem_kv_bwd(dk, dv, mem_n, mem, w_xk, w_xv):
    def body(dk_ref, dv_ref, mn_ref, mem_ref, wk_ref, wv_ref, dwk_ref, dwv_ref, dg_ref):
        dkb, dvb = dk_ref[...].astype(BF16), dv_ref[...].astype(BF16)
        mn = mn_ref[...]
        dwk_ref[...] = _dot_tn(mn, dkb).astype(BF16)
        dwv_ref[...] = _dot_tn(mn, dvb).astype(BF16)
        dmn = _dot_nt(dkb, wk_ref[...]) + _dot_nt(dvb, wv_ref[...])
        xh, _ = _rms(mem_ref[...])
        dg_ref[...] = jnp.zeros_like(dg_ref)
        dg_ref[0:1, :] = jnp.sum(dmn * xh, axis=0, keepdims=True)

    vm = pl.BlockSpec(memory_space=pltpu.VMEM)
    return pl.pallas_call(
        body, name="mem_kv_bwd",
        out_shape=[jax.ShapeDtypeStruct(w_xk.shape, BF16), jax.ShapeDtypeStruct(w_xv.shape, BF16),
                   jax.ShapeDtypeStruct((SUBLANES, D_MODEL), F32)],
        in_specs=[vm] * 6, out_specs=[vm] * 3,
        compiler_params=pltpu.CompilerParams(vmem_limit_bytes=VMEM_LIMIT),
    )(dk, dv, mem_n, mem, w_xk, w_xv)


def _mix_out_bwd(dx1, w_out, attn, gb, gc, xi, w_sc, g_a, g_c, dep):
    s = dx1.shape[0]
    tb = TM // SUBLANES

    def body(dx1_ref, wout_ref, attn_ref, gb_ref, gc_ref, xi_ref, gch_ref, xih_ref, wsc_ref, ga_ref, gcv_ref, dep_ref,
             da1, da4, da16, dd1, dd4, dd16, dgb_ref, dcv_ref, dga_ref, dgc_ref, dwsc_ref, scr):
        i = pl.program_id(0)

        @pl.when(i == 0)
        def _():
            dga_ref[...] = jnp.zeros_like(dga_ref)
            dgc_ref[...] = jnp.zeros_like(dgc_ref)
            dwsc_ref[...] = jnp.zeros_like(dwsc_ref)

        dmixed = _dot_nt(dx1_ref[...].astype(BF16), wout_ref[...])
        da, dcn = dmixed[:, :ATTN_W], dmixed[:, ATTN_W:]
        attn = attn_ref[...]
        xa, ra = _rms(attn)
        dga_ref[0:1, :] += jnp.sum(da * xa, axis=0, keepdims=True)
        dattn = _rms_bwd(xa, ra, ga_ref[...], da)
        _spread(dattn, scr, (da1, da4, da16), BF16)
        prod = dattn * attn
        dd = jnp.concatenate(
            [jnp.broadcast_to(jnp.sum(prod[:, h * HEAD_DIM:(h + 1) * HEAD_DIM], axis=-1, keepdims=True),
                              (TM, HEAD_DIM)) for h in range(N_HEADS)], axis=1)
        _spread(dd, scr, (dd1, dd4, dd16), F32)
        gbv = gb_ref[...]
        u = gc_ref[...] * xi_ref[...]
        uh = jnp.where(i > 0, gch_ref[...] * xih_ref[...], 0.0)
        u2, u1 = _shift_down(u, uh, 2), _shift_down(u, uh, 1)
        cv = (u2 * wsc_ref[0:1, :] + u1 * wsc_ref[1:2, :]) + u * wsc_ref[2:3, :]
        xc, rc = _rms(gbv * cv)
        dgc_ref[0:1, :] += jnp.sum(dcn * xc, axis=0, keepdims=True)
        dconv = _rms_bwd(xc, rc, gcv_ref[...], dcn)
        dgb_ref[...] = (dconv * cv).astype(BF16)
        dcv = dconv * gbv
        dcv_ref[...] = dcv
        dwsc_ref[0:1, :] += jnp.sum(dcv * u2, axis=0, keepdims=True)
        dwsc_ref[1:2, :] += jnp.sum(dcv * u1, axis=0, keepdims=True)
        dwsc_ref[2:3, :] += jnp.sum(dcv * u, axis=0, keepdims=True)

    row = lambda n: pl.BlockSpec((TM, n), lambda i: (i, 0))
    halo = pl.BlockSpec((SUBLANES, 512), lambda i: (jnp.maximum(i * tb - 1, 0), 0))
    acc = _full((SUBLANES, 512))
    res = pl.pallas_call(
        body, name="mix_out_bwd", grid=(s // TM,),
        out_shape=_class_shapes(s, 512, BF16) + _class_shapes(s, 512, F32)
        + [jax.ShapeDtypeStruct((s, 512), BF16), jax.ShapeDtypeStruct((s, 512), F32)]
        + [jax.ShapeDtypeStruct((SUBLANES, 512), F32)] * 3,
        in_specs=[row(D_MODEL), _full(w_out.shape), row(512), row(512), row(512), row(512), halo, halo,
                  _full(w_sc.shape), _full(g_a.shape), _full(g_c.shape), ANY_SPEC],
        out_specs=_class_specs(512) * 2 + [row(512)] * 2 + [acc] * 3,
        scratch_shapes=[pltpu.VMEM((512 // LANES, TM, LANES), F32)],
        compiler_params=_cparams(1),
    )(dx1, w_out, attn, gb, gc, xi, gc, xi, w_sc, g_a, g_c, dep)
    return res[0:3], res[3:6], res[6], res[7], res[8], res[9], res[10]


def _swa_bwd(qc, kc, vc, doc, lsec, ddc, bias, dil, dep):
    n128 = qc.shape[1] // WIN
    nsub = min(SWA_BLOCKS, n128)
    nb = n128 // nsub

    def body(q_ref, qn_ref, kp_ref, kc_ref, vp_ref, vc_ref, do_ref, don_ref, lse_ref, lsen_ref, dd_ref, ddn_ref,
             b_ref, dep_ref, dq_ref, dk_ref, dv_ref, db_ref, s_scr, dp_scr, sn_scr, dpn_scr, ds_scr, p_scr, dsn_scr,
             pn_scr):
        r, b = pl.program_id(0), pl.program_id(1)

        @pl.when((r == 0) & (b == 0))
        def _():
            db_ref[...] = jnp.zeros_like(db_ref)

        pairs = [slice(a * LANES, (a + 1) * LANES) for a in range(N_HEADS // 2)]
        blk = [slice(t * WIN, (t + 1) * WIN) for t in range(nsub)]
        last = blk[nsub - 1]
        per_head = lambda ref, rows: jnp.stack([ref[0, rows, h * HEAD_DIM:h * HEAD_DIM + 1] for h in range(N_HEADS)])

        def keys(prev_ref, cur_ref, t, sl):
            if t == 0:
                return jnp.concatenate([prev_ref[0, :, sl], cur_ref[0, blk[0], sl]], axis=0)
            return cur_ref[0, (t - 1) * WIN:(t + 1) * WIN, sl]

        for a, sl in enumerate(pairs):
            for t in range(nsub):
                k2, v2 = keys(kp_ref, kc_ref, t, sl), keys(vp_ref, vc_ref, t, sl)
                q_eo = _pair_split(q_ref[0, blk[t], sl])
                do_eo = _pair_split(do_ref[0, blk[t], sl].astype(BF16))
                for e in range(2):
                    s_scr[t, 2 * a + e] = _dot_nt(q_eo[e], k2)
                    dp_scr[t, 2 * a + e] = _dot_nt(do_eo[e], v2)
            qn_eo = _pair_split(qn_ref[0, :, sl])
            don_eo = _pair_split(don_ref[0, :, sl].astype(BF16))
            for e in range(2):
                sn_scr[2 * a + e] = _dot_nt(qn_eo[e], kc_ref[0, last, sl])
                dpn_scr[2 * a + e] = _dot_nt(don_eo[e], vc_ref[0, last, sl])
        bias = b_ref[...]
        for t in range(nsub):
            first = (b == 0) if t == 0 else False
            p = jnp.exp(jnp.where(_band_mask(first), s_scr[t] + bias, -jnp.inf) - per_head(lse_ref, blk[t]))
            ds = p * (dp_scr[t] - per_head(dd_ref, blk[t]))
            db_ref[...] += ds
            ds_scr[t] = ds.astype(BF16)
            p_scr[t] = p.astype(BF16)
        qi = lax.broadcasted_iota(jnp.int32, (WIN, WIN), 0)
        kj = lax.broadcasted_iota(jnp.int32, (WIN, WIN), 1)
        valid_n = kj >= qi + jnp.where(b + 1 < nb, 0, WIN)
        every = slice(0, WIN)
        pn = jnp.exp(jnp.where(valid_n, sn_scr[...] + bias[:, :, :WIN], -jnp.inf) - per_head(lsen_ref, every))
        dsn_scr[...] = (pn * (dpn_scr[...] - per_head(ddn_ref, every))).astype(BF16)
        pn_scr[...] = pn.astype(BF16)
        for a, sl in enumerate(pairs):
            q_eo = [_pair_split(q_ref[0, blk[t], sl]) for t in range(nsub)] + [_pair_split(qn_ref[0, :, sl])]
            do_eo = [_pair_split(do_ref[0, blk[t], sl].astype(BF16)) for t in range(nsub)]
            do_eo.append(_pair_split(don_ref[0, :, sl].astype(BF16)))
            for t in range(nsub):
                k_eo = _pair_split(keys(kp_ref, kc_ref, t, sl))
                dq, dk, dv = None, None, None
                for e in range(2):
                    h = 2 * a + e
                    ds_next = ds_scr[t + 1, h, :, :WIN] if t + 1 < nsub else dsn_scr[h]
                    p_next = p_scr[t + 1, h, :, :WIN] if t + 1 < nsub else pn_scr[h]
                    terms = (_dot(ds_scr[t, h], k_eo[e]),
                             _dot_tn(ds_scr[t, h, :, WIN:], q_eo[t][e]) + _dot_tn(ds_next, q_eo[t + 1][e]),
                             _dot_tn(p_scr[t, h, :, WIN:], do_eo[t][e]) + _dot_tn(p_next, do_eo[t + 1][e]))
                    dq, dk, dv = terms if e == 0 else (dq + terms[0], dk + terms[1], dv + terms[2])
                dq_ref[0, blk[t], sl] = dq.astype(BF16)
                dk_ref[0, blk[t], sl] = dk.astype(BF16)
                dv_ref[0, blk[t], sl] = dv.astype(BF16)

    cur = pl.BlockSpec((1, nsub * WIN, 512), lambda r, b: (r, b, 0))
    prev = pl.BlockSpec((1, WIN, 512), lambda r, b: (r, jnp.maximum(nsub * b - 1, 0), 0))
    nxt = pl.BlockSpec((1, WIN, 512), lambda r, b: (r, jnp.minimum(nsub * b + nsub, n128 - 1), 0))
    wide, narrow = (nsub, N_HEADS, WIN, 2 * WIN), (N_HEADS, WIN, WIN)
    return pl.pallas_call(
        body, name=f"swa_bwd_d{dil}", grid=(dil, nb),
        out_shape=[jax.ShapeDtypeStruct(qc.shape, BF16)] * 3 + [jax.ShapeDtypeStruct(bias.shape, F32)],
        in_specs=[cur, nxt, prev, cur, prev, cur, cur, nxt, cur, nxt, cur, nxt, _full(bias.shape), ANY_SPEC],
        out_specs=[cur] * 3 + [_full(bias.shape)],
        scratch_shapes=[pltpu.VMEM(wide, F32), pltpu.VMEM(wide, F32), pltpu.VMEM(narrow, F32),
                        pltpu.VMEM(narrow, F32), pltpu.VMEM(wide, BF16), pltpu.VMEM(wide, BF16),
                        pltpu.VMEM(narrow, BF16), pltpu.VMEM(narrow, BF16)],
        compiler_params=_cparams(2),
    )(qc, qc, kc, kc, vc, vc, doc, doc, lsec, lsec, ddc, ddc, bias, dep)


def _in_proj_bwd(dqs, dks, dvs, dgb, dcv, gc, xi, w_sc, w_in_g, x, g_mix, dx1):
    s = x.shape[0]
    tb = TM // SUBLANES
    last = s // SUBLANES - 1
    n_tiles = s // TM

    def body(dq1, dq4, dq16, dk1, dk4, dk16, dv1, dv4, dv16, dgb_ref, dcv_ref, dcvn_ref, gc_ref, xi_ref, wsc_ref,
             win_ref, x_ref, g_ref, dx1_ref, dproj_ref, gx_ref, dg_ref, scr_a, scr_b):
        i = pl.program_id(0)

        @pl.when(i == 0)
        def _():
            dg_ref[...] = jnp.zeros_like(dg_ref)

        d0 = dcv_ref[...]
        dn = jnp.where(i < n_tiles - 1, dcvn_ref[...], 0.0)
        du = (d0 * wsc_ref[2:3, :] + _shift_up(d0, dn, 1) * wsc_ref[1:2, :]) + _shift_up(d0, dn, 2) * wsc_ref[0:1, :]
        merge = lambda a, b4, b16: ((a[...].astype(F32) + _gather_classes(b4, scr_a, 4))
                                    + _gather_classes(b16, scr_b, 16))
        dq = merge(dq1, dq4, dq16) * (HEAD_DIM ** -0.5)
        dk = merge(dk1, dk4, dk16)
        dv = merge(dv1, dv4, dv16)
        dproj = jnp.concatenate([dq, dk, dv, dgb_ref[...].astype(F32), du * xi_ref[...], du * gc_ref[...]],
                                axis=1).astype(BF16)
        dproj_ref[...] = dproj
        dh = jnp.zeros((TM, D_MODEL), F32)
        for j in range(N_DEV):
            dh = dh + _dot_nt(dproj[:, j * IN_CHUNK:(j + 1) * IN_CHUNK], win_ref[j])
        xh, r = _rms(x_ref[...])
        dg_ref[0:1, :] += jnp.sum(dh * xh, axis=0, keepdims=True)
        gx_ref[...] = dx1_ref[...] + _rms_bwd(xh, r, g_ref[...], dh)

    row = lambda n: pl.BlockSpec((TM, n), lambda i: (i, 0))
    nxt = pl.BlockSpec((SUBLANES, 512), lambda i: (jnp.minimum((i + 1) * tb, last), 0))
    return pl.pallas_call(
        body, name="in_proj_bwd", grid=(n_tiles,),
        out_shape=[jax.ShapeDtypeStruct((s, IN_COLS), BF16), jax.ShapeDtypeStruct((s, D_MODEL), F32),
                   jax.ShapeDtypeStruct((SUBLANES, D_MODEL), F32)],
        in_specs=_class_specs(512) * 3 + [row(512), row(512), nxt, row(512), row(512), _full(w_sc.shape),
                                          _full(w_in_g.shape), row(D_MODEL), _full(g_mix.shape), row(D_MODEL)],
        out_specs=[row(IN_COLS), row(D_MODEL), _full((SUBLANES, D_MODEL))],
        scratch_shapes=[pltpu.VMEM((512 // LANES, TM, LANES), F32)] * 2,
        compiler_params=_cparams(1),
    )(*dqs, *dks, *dvs, dgb, dcv, dcv, gc, xi, w_sc, w_in_g, x, g_mix, dx1)


def _dw(a, b, dep, name, a_chunked=False, b_chunked=False, n_chunks=1, chunk_cols=None):
    ts = TS_DW if (a_chunked or b_chunked or chunk_cols) else TS_DW // 2
    if a_chunked:
        nj, s, kk = a.shape
        nn = b.shape[1]
        a_spec = pl.BlockSpec((1, ts, kk), lambda j, t: (j, t, 0))
        b_spec = pl.BlockSpec((ts, nn), lambda j, t: (t, 0))
    elif b_chunked:
        nj, s, nn = b.shape
        kk = a.shape[1]
        a_spec = pl.BlockSpec((ts, kk), lambda j, t: (t, 0))
        b_spec = pl.BlockSpec((1, ts, nn), lambda j, t: (j, t, 0))
    else:
        s, kk = a.shape
        nj, nn = (n_chunks, chunk_cols) if chunk_cols else (1, b.shape[1])
        a_spec = pl.BlockSpec((ts, kk), lambda j, t: (t, 0))
        b_spec = pl.BlockSpec((ts, nn), lambda j, t: (t, j))
    n_steps = s // ts

    def body(a_ref, b_ref, dep_ref, o_ref, acc):
        t = pl.program_id(1)

        @pl.when(t == 0)
        def _():
            acc[...] = jnp.zeros_like(acc)

        av = (a_ref[0] if a_chunked else a_ref[...]).astype(BF16)
        bv = (b_ref[0] if b_chunked else b_ref[...]).astype(BF16)
        acc[...] += _dot_tn(av, bv)

        @pl.when(t == n_steps - 1)
        def _():
            o_ref[0] = acc[...].astype(BF16)

    return pl.pallas_call(
        body, name=name, grid=(nj, n_steps),
        out_shape=jax.ShapeDtypeStruct((nj, kk, nn), BF16),
        in_specs=[a_spec, b_spec, ANY_SPEC],
        out_specs=pl.BlockSpec((1, kk, nn), lambda j, t: (j, 0, 0)),
        scratch_shapes=[pltpu.VMEM((kk, nn), F32)],
        compiler_params=_cparams(2),
    )(a, b, dep)


def _adamw_math(w, g, m, v):
    m2 = ADAM_B1 * m + (1.0 - ADAM_B1) * g
    v2 = ADAM_B2 * v + (1.0 - ADAM_B2) * (g * g)
    m_hat = m2 / (1.0 - ADAM_B1 ** ADAM_STEP)
    v_hat = v2 / (1.0 - ADAM_B2 ** ADAM_STEP)
    delta = -ADAM_LR * (m_hat / (jnp.sqrt(v_hat) + ADAM_EPS) + ADAM_WD * w)
    return delta, m2, v2


def _sum_parts(me, own, p_ref):
    g = None
    for i in range(N_DEV):
        part = jnp.where(me == i, own.astype(F32), p_ref[i].astype(F32))
        g = part if g is None else g + part
    return g


def _adamw_big(name, w, sent, parts, m, v, me_arr):
    rr, cc = w.shape
    tr = rr // 4 if rr >= 512 else rr

    def body(me_ref, w_ref, own_ref, p_ref, m_ref, v_ref, g_ref, d_ref, nm_ref, nv_ref):
        g = own_ref[0].astype(F32)
        for k in range(1, N_DEV):
            g = g + p_ref[(me_ref[0] + k) % N_DEV].astype(F32)
        g_ref[...] = g
        d_ref[...], nm_ref[...], nv_ref[...] = _adamw_math(w_ref[...], g, m_ref[...], v_ref[...])

    row = pl.BlockSpec((tr, cc), lambda i, me: (i, 0))
    return pl.pallas_call(
        body, name=name,
        grid_spec=pltpu.PrefetchScalarGridSpec(
            num_scalar_prefetch=1, grid=(rr // tr,),
            in_specs=[row, pl.BlockSpec((1, tr, cc), lambda i, me: (me[0], i, 0)),
                      pl.BlockSpec((N_DEV, tr, cc), lambda i, me: (0, i, 0)), row, row],
            out_specs=[row] * 4),
        out_shape=[jax.ShapeDtypeStruct((rr, cc), F32)] * 4,
        compiler_params=_cparams(1),
    )(me_arr, w, sent, parts, m, v)


def _small_slices():
    return [
        (slice(ROW_RELB, ROW_RELB + 8), slice(0, N_BUCKETS)),
        (slice(ROW_GMIX, ROW_GMIX + 1), slice(0, D_MODEL)),
        (slice(ROW_GAC, ROW_GAC + 1), slice(0, ATTN_W)),
        (slice(ROW_GAC, ROW_GAC + 1), slice(ATTN_W, D_MODEL)),
        (slice(ROW_GXATTN, ROW_GXATTN + 1), slice(0, D_MODEL)),
        (slice(ROW_GMEM, ROW_GMEM + 1), slice(0, D_MODEL)),
        (slice(ROW_GFFN, ROW_GFFN + 1), slice(0, D_MODEL)),
        (slice(ROW_BFC, ROW_BFC + 8), slice(0, UP_CHUNK)),
        (slice(ROW_GFINAL, ROW_GFINAL + 1), slice(0, D_MODEL)),
    ]


def _adamw_small(own, parts, wmv, me_arr):
    slices = _small_slices()
    n = len(slices)

    def body(*refs):
        me_ref, own_ref, p_ref = refs[:3]
        ins = refs[3:3 + 3 * n]
        g_ref = refs[3 + 3 * n]
        outs = refs[4 + 3 * n:]
        g = _sum_parts(me_ref[0], own_ref[...], p_ref)
        g_ref[...] = g
        for a, (rs, ls) in enumerate(slices):
            ga = g[rs, ls]
            outs[4 * a][...] = ga
            outs[4 * a + 1][...], outs[4 * a + 2][...], outs[4 * a + 3][...] = _adamw_math(
                ins[3 * a][...], ga, ins[3 * a + 1][...], ins[3 * a + 2][...])

    vm = pl.BlockSpec(memory_space=pltpu.VMEM)
    flat = [t for trip in wmv for t in trip]
    out_shape = [jax.ShapeDtypeStruct((SMALL_ROWS, D_MODEL), F32)]
    for w, _, _ in wmv:
        out_shape += [jax.ShapeDtypeStruct(w.shape, F32)] * 4
    res = pl.pallas_call(
        body, name="adamw_small", out_shape=out_shape,
        in_specs=[SMEM_SPEC] + [vm] * (2 + 3 * n), out_specs=[vm] * len(out_shape),
    )(me_arr, own, parts, *flat)
    return res[0], [res[1 + 4 * a:5 + 4 * a] for a in range(n)]


def _adamw_shards(items):
    n = len(items)

    def body(*refs):
        for a in range(n):
            w_ref, g_ref, m_ref, v_ref = refs[4 * a:4 * a + 4]
            d_ref, nm_ref, nv_ref = refs[4 * n + 3 * a:4 * n + 3 * a + 3]
            d_ref[...], nm_ref[...], nv_ref[...] = _adamw_math(w_ref[...], g_ref[...], m_ref[...], v_ref[...])

    vm = pl.BlockSpec(memory_space=pltpu.VMEM)
    out_shape = []
    for w, _, _, _ in items:
        out_shape += [jax.ShapeDtypeStruct(w.shape, F32)] * 3
    res = pl.pallas_call(
        body, name="adamw_shards", out_shape=out_shape, in_specs=[vm] * (4 * n), out_specs=[vm] * (3 * n),
    )(*[t for it in items for t in it])
    return [res[3 * a:3 * a + 3] for a in range(n)]


def _mesh_pos():
    return lax.axis_index("x"), lax.axis_index("y"), lax.axis_index("c")


def _dev_index(p):
    return 4 * p[0] + 2 * p[1] + p[2]


def _all_gather(shards):
    n = len(shards)

    def body(*refs):
        ins, outs = refs[:n], refs[n:2 * n]
        send_sems, recv_sems, loc_sems = refs[2 * n:]
        x, y, c = _mesh_pos()
        me, sib = (x, y, c), (x, y, 1 - c)
        chips = [(1 - x, y), (x, 1 - y), (1 - x, 1 - y)]

        def cp(a, k, block, to, src=None):
            dst = outs[a].at[_dev_index(block)]
            return pltpu.make_async_remote_copy(
                src_ref=dst if src is None else src, dst_ref=dst, send_sem=send_sems.at[a, k],
                recv_sem=recv_sems.at[a, k], device_id=to, device_id_type=MESH)

        mine = [pltpu.make_async_copy(ins[a], outs[a].at[_dev_index(me)], loc_sems.at[a]) for a in range(n)]
        for m_ in mine:
            m_.start()
        first = []
        for a in range(n):
            first.append(cp(a, 0, me, sib, src=ins[a]))
            first += [cp(a, 1 + j, me, (*chip, c), src=ins[a]) for j, chip in enumerate(chips)]
        for f in first:
            f.start()
        passed = []
        for a in range(n):
            for j, chip in enumerate(chips):
                cp(a, 1 + j, (*chip, c), me).wait_recv()
                fwd = cp(a, 4 + j, (*chip, c), sib)
                fwd.start()
                passed.append(fwd)
        for a in range(n):
            cp(a, 0, sib, me).wait_recv()
            for j, chip in enumerate(chips):
                cp(a, 4 + j, (*chip, 1 - c), me).wait_recv()
        for f in first + passed:
            f.wait_send()
        for m_ in mine:
            m_.wait()

    hbm = pl.BlockSpec(memory_space=pltpu.HBM)
    return pl.pallas_call(
        body, name="all_gather_weights",
        out_shape=[jax.ShapeDtypeStruct((N_DEV,) + a.shape, a.dtype) for a in shards],
        in_specs=[hbm] * n, out_specs=[hbm] * n,
        scratch_shapes=[pltpu.SemaphoreType.DMA((n, 7)), pltpu.SemaphoreType.DMA((n, 7)),
                        pltpu.SemaphoreType.DMA((n,))],
    )(*shards)


def _peers():
    x, y, c = _mesh_pos()
    return (x, y, c), [((1 - x) if k & 4 else x, (1 - y) if k & 2 else y, (1 - c) if k & 1 else c)
                       for k in range(1, 8)]


def _exchange_copy(src_ref, land_ref, whole, send_sems, recv_sems, a, k, peer, slot):
    src = src_ref if whole else src_ref.at[_dev_index(peer)]
    return pltpu.make_async_remote_copy(
        src_ref=src, dst_ref=land_ref.at[slot], send_sem=send_sems.at[7 * a + k], recv_sem=recv_sems.at[7 * a + k],
        device_id=peer, device_id_type=MESH)


def _exchange_start(name, srcs, whole, dep):
    n = len(srcs)
    lands = [lax.empty(((N_DEV,) + s.shape) if w else s.shape, s.dtype) for s, w in zip(srcs, whole)]

    def body(*refs):
        src_refs, land_refs = refs[:n], refs[n:2 * n]
        send_sems, recv_sems, token = refs[2 * n + 1], refs[2 * n + 2], refs[-1]
        me, peers = _peers()
        for a in range(n):
            for k, peer in enumerate(peers):
                _exchange_copy(src_refs[a], land_refs[a], whole[a], send_sems, recv_sems, a, k, peer,
                               _dev_index(me)).start()
        token[...] = jnp.zeros_like(token)

    res = pl.pallas_call(
        body, name=name,
        out_shape=(pltpu.SemaphoreType.DMA((7 * n,)), pltpu.SemaphoreType.DMA((7 * n,)),
                   *[pltpu.HBM(a.shape, a.dtype) for a in srcs], *[pltpu.HBM(a.shape, a.dtype) for a in lands],
                   jax.ShapeDtypeStruct((SUBLANES, 128), F32)),
        in_specs=[HBM_SPEC] * (2 * n) + [ANY_SPEC],
        out_specs=(SEM_SPEC, SEM_SPEC, *([HBM_SPEC] * (2 * n)), VMEM_SPEC),
        input_output_aliases={i: 2 + i for i in range(2 * n)},
        compiler_params=pltpu.CompilerParams(has_side_effects=DATAFLOW),
    )(*[pltpu.with_memory_space_constraint(a, pltpu.HBM) for a in srcs],
      *[pltpu.with_memory_space_constraint(a, pltpu.HBM) for a in lands], dep)
    return res[0], res[1], list(res[2:2 + n]), list(res[2 + n:2 + 2 * n]), res[-1]


def _exchange_wait(name, started, whole, after, which=None):
    send_sems, recv_sems, srcs, lands, _ = started
    which = list(range(len(srcs))) if which is None else which
    srcs, lands = [srcs[a] for a in which], [lands[a] for a in which]
    n = len(srcs)

    def body(*refs):
        src_refs, land_refs = refs[:n], refs[n:2 * n]
        send_sems, recv_sems = refs[2 * n], refs[2 * n + 1]
        _, peers = _peers()
        for i, a in enumerate(which):
            for k, peer in enumerate(peers):
                cp = _exchange_copy(src_refs[i], land_refs[i], whole[a], send_sems, recv_sems, a, k, peer,
                                    _dev_index(peer))
                cp.wait_send()
                cp.wait_recv()

    res = pl.pallas_call(
        body, name=name,
        out_shape=[pltpu.HBM(a.shape, a.dtype) for a in srcs + lands],
        in_specs=[HBM_SPEC] * (2 * n) + [SEM_SPEC, SEM_SPEC, ANY_SPEC],
        out_specs=[HBM_SPEC] * (2 * n),
        input_output_aliases={i: i for i in range(2 * n)},
        compiler_params=pltpu.CompilerParams(has_side_effects=DATAFLOW),
    )(*srcs, *lands, send_sems, recv_sems, after)
    return list(res[:n]), list(res[n:])


def _gather_start(name, shards, dep):
    n = len(shards)
    lands = [lax.empty((N_DEV,) + a.shape, a.dtype) for a in shards]

    def body(*refs):
        src_refs, land_refs = refs[:n], refs[n:2 * n]
        send_sems, recv_sems, token = refs[2 * n + 1], refs[2 * n + 2], refs[-1]
        x, y, c = _mesh_pos()
        peers = [(x, y, 1 - c), (1 - x, y, c), (x, 1 - y, c), (1 - x, 1 - y, c)]
        for a in range(n):
            for k, peer in enumerate(peers):
                pltpu.make_async_remote_copy(
                    src_ref=src_refs[a], dst_ref=land_refs[a].at[_dev_index((x, y, c))], send_sem=send_sems.at[4 * a + k],
                    recv_sem=recv_sems.at[4 * a + k], device_id=peer, device_id_type=MESH).start()
        token[...] = jnp.zeros_like(token)

    res = pl.pallas_call(
        body, name=name,
        out_shape=(pltpu.SemaphoreType.DMA((4 * n,)), pltpu.SemaphoreType.DMA((4 * n,)),
                   *[pltpu.HBM(a.shape, a.dtype) for a in shards], *[pltpu.HBM(a.shape, a.dtype) for a in lands],
                   jax.ShapeDtypeStruct((SUBLANES, 128), F32)),
        in_specs=[HBM_SPEC] * (2 * n) + [ANY_SPEC],
        out_specs=(SEM_SPEC, SEM_SPEC, *([HBM_SPEC] * (2 * n)), VMEM_SPEC),
        input_output_aliases={i: 2 + i for i in range(2 * n)},
        compiler_params=pltpu.CompilerParams(has_side_effects=DATAFLOW),
    )(*[pltpu.with_memory_space_constraint(a, pltpu.HBM) for a in shards],
      *[pltpu.with_memory_space_constraint(a, pltpu.HBM) for a in lands], dep)
    return res[0], res[1], list(res[2:2 + n]), list(res[2 + n:2 + 2 * n]), res[-1]


def _gather_forward(name, send_sems, recv_sems, lands, which, after):
    n = len(which)

    def body(*refs):
        land_refs = refs[:n]
        send_sems, recv_sems = refs[n], refs[n + 1]
        fsend, frecv, token = refs[n + 3], refs[n + 4], refs[-1]
        x, y, c = _mesh_pos()
        chips = [(1 - x, y), (x, 1 - y), (1 - x, 1 - y)]
        for i, a in enumerate(which):
            for j, chip in enumerate(chips):
                block = land_refs[i].at[_dev_index((*chip, c))]
                pltpu.make_async_remote_copy(
                    src_ref=block, dst_ref=block, send_sem=send_sems.at[4 * a + 1 + j], recv_sem=recv_sems.at[4 * a + 1 + j],
                    device_id=(*chip, c), device_id_type=MESH).wait_recv()
                pltpu.make_async_remote_copy(
                    src_ref=block, dst_ref=block, send_sem=fsend.at[3 * i + j], recv_sem=frecv.at[3 * i + j],
                    device_id=(x, y, 1 - c), device_id_type=MESH).start()
        token[...] = jnp.zeros_like(token)

    res = pl.pallas_call(
        body, name=name,
        out_shape=(pltpu.SemaphoreType.DMA((3 * n,)), pltpu.SemaphoreType.DMA((3 * n,)),
                   *[pltpu.HBM(a.shape, a.dtype) for a in lands], jax.ShapeDtypeStruct((SUBLANES, 128), F32)),
        in_specs=[HBM_SPEC] * n + [SEM_SPEC, SEM_SPEC, ANY_SPEC],
        out_specs=(SEM_SPEC, SEM_SPEC, *([HBM_SPEC] * n), VMEM_SPEC),
        input_output_aliases={i: 2 + i for i in range(n)},
        compiler_params=pltpu.CompilerParams(has_side_effects=DATAFLOW),
    )(*lands, send_sems, recv_sems, after)
    return res[0], res[1], list(res[2:2 + n]), res[-1]


def _gather_wait(name, send_sems, recv_sems, fsend, frecv, srcs, lands, which, after):
    n = len(which)

    def body(*refs):
        land_refs = refs[n:2 * n]
        send_sems, recv_sems, fsend, frecv = refs[2 * n:2 * n + 4]
        x, y, c = _mesh_pos()
        sib = (x, y, 1 - c)
        chips = [(1 - x, y), (x, 1 - y), (1 - x, 1 - y)]
        for i, a in enumerate(which):
            def cp(slot, ssem, rsem):
                block = land_refs[i].at[_dev_index(slot)]
                return pltpu.make_async_remote_copy(src_ref=block, dst_ref=block, send_sem=ssem, recv_sem=rsem,
                                                    device_id=sib, device_id_type=MESH)
            cp(sib, send_sems.at[4 * a], recv_sems.at[4 * a]).wait_recv()
            for j, chip in enumerate(chips):
                cp((*chip, 1 - c), fsend.at[3 * i + j], frecv.at[3 * i + j]).wait_recv()
            for k in range(4):
                cp(sib, send_sems.at[4 * a + k], recv_sems.at[4 * a + k]).wait_send()
            for j in range(3):
                cp(sib, fsend.at[3 * i + j], frecv.at[3 * i + j]).wait_send()

    res = pl.pallas_call(
        body, name=name,
        out_shape=[pltpu.HBM(a.shape, a.dtype) for a in srcs + lands],
        in_specs=[HBM_SPEC] * (2 * n) + [SEM_SPEC] * 4 + [ANY_SPEC],
        out_specs=[HBM_SPEC] * (2 * n),
        input_output_aliases={i: i for i in range(2 * n)},
        compiler_params=pltpu.CompilerParams(has_side_effects=DATAFLOW),
    )(*srcs, *lands, send_sems, recv_sems, fsend, frecv, after)
    return list(res[n:])


def _local_step(x, mem, target, rel_bias, g_mix, w_in_g, w_sc, g_a, g_c, g_xattn, g_mem, g_ffn, w_fc, b_fc, g_final,
                dep, forward_weights, late_weights, emit, emit_small):
    s = x.shape[0]
    buckets = _bucket_tables()
    bias = _bias_fwd(rel_bias, buckets)

    h1, qs, ks, vs, gb, gc, xi = _rms_proj(x, g_mix, w_in_g, dep)
    qs, ks, vs = ([a[0][None]] + list(a[1:]) for a in (qs, ks, vs))
    group1, group2 = ["w_out", "w_xq", "w_xk", "w_xv", "w_xo"], ["w_up", "w_down"]
    tok = forward_weights(group1, h1)
    branches = []
    for p, dil in enumerate(DILATIONS):
        o_p, lse_p = _swa_fwd(qs[p], ks[p], vs[p], bias[p], dil, tok)
        branches.append([o_p[0], lse_p[0]] if dil == 1 else [o_p, lse_p])
    lw = late_weights(group1, branches[-1][0])
    w_out, w_xq, w_xk, w_xv, w_xo = (lw[n] for n in group1)
    attn, lses, mixed, x1 = _mix_out(branches, gb, gc, xi, x, w_sc, g_a, g_c, w_out)
    tok = forward_weights(group2, x1)
    mem_n, mk, mv = _mem_kv(mem, g_mem, w_xk, w_xv)
    h2, xq, xo, x2 = _xattn_fwd(x1, g_xattn, w_xq, mk, mv, w_xo, tok)
    lw = late_weights(group2, x2)
    w_up_g, w_down_g = lw["w_up"], lw["w_down"]
    h3, conv, act, dx3, loss_acc, dg_final = _ffn_fwd(x2, g_ffn, w_up_g, w_fc, b_fc, w_down_g, g_final, target)

    gw_down = _dw(act, dx3, dep, "dw_down", a_chunked=True)
    dup, dx2, dg_ffn, dw_fc, db_fc = _ffn_bwd(dx3, h3, conv, x2, g_ffn, w_up_g, w_fc, w_down_g)
    gw_up = _dw(dup, h3, dep, "dw_up", a_chunked=True)
    tok = emit(dict(w_down=gw_down, w_up=gw_up))
    dxq, dx1, dmk, dmv, dg_xattn = _xattn_bwd(dx2, xo, xq, mk, mv, w_xo, w_xq, x1, g_xattn, tok)
    gw_xo = _dw(xo, dx2, tok, "dw_xo")[0]
    gw_xq = _dw(h2, dxq, tok, "dw_xq")[0]
    gw_xk, gw_xv, dg_mem = _mem_kv_bwd(dmk, dmv, mem_n, mem, w_xk, w_xv)
    tok = emit(dict(w_xo=gw_xo, w_xq=gw_xq, w_xk=gw_xk, w_xv=gw_xv))
    dattns, dds, dgb, dcv, dg_a, dg_c, dw_sc = _mix_out_bwd(dx1, w_out, attn, gb, gc, xi, w_sc, g_a, g_c, tok)
    first = lambda a: [a[0][None]] + list(a[1:])
    dattns, dds, lses = first(dattns), first(dds), first(lses)
    gw_out = _dw(mixed, dx1, tok, "dw_out")[0]
    tok = emit(dict(w_out=gw_out))
    dqs, dks, dvs, dbias = [], [], [], []
    for p, dil in enumerate(DILATIONS):
        dq_p, dk_p, dv_p, db_p = _swa_bwd(qs[p], ks[p], vs[p], dattns[p], lses[p], dds[p], bias[p], dil, tok)
        dqs.append(dq_p[0] if dil == 1 else dq_p)
        dks.append(dk_p[0] if dil == 1 else dk_p)
        dvs.append(dv_p[0] if dil == 1 else dv_p)
        dbias.append(db_p)
    d_relb = _bias_bwd(jnp.stack(dbias), buckets)
    dproj, grad_x, dg_mix = _in_proj_bwd(dqs, dks, dvs, dgb, dcv, gc, xi, w_sc, w_in_g, x, g_mix, dx1)
    pad = lambda a: jnp.pad(a, ((0, 0), (0, D_MODEL - a.shape[1])))
    small = jnp.concatenate([
        d_relb, dg_mix, dg_xattn, dg_mem, dg_ffn, dg_final, jnp.concatenate([dg_a, dg_c], axis=1),
        pad(dw_sc), pad(db_fc), pad(dw_fc.reshape(3 * N_DEV, UP_CHUNK)), pad(loss_acc)], axis=0)
    tok = emit_small(small)
    gw_in = _dw(h1, dproj, tok, "dw_in", n_chunks=N_DEV, chunk_cols=IN_CHUNK)
    emit(dict(w_in=gw_in))
    return grad_x


def kernel(x, mem, rel_bias, g_mix, w_in, w_short_conv, g_attn_out, g_conv_out, w_out, g_xattn, g_mem, w_xq, w_xk, w_xv, w_xo, g_ffn, w_up, w_ffn_conv, b_ffn_conv, w_down, g_final, loss_target, m_rel_bias, m_g_mix, m_w_in, m_w_short_conv, m_g_attn_out, m_g_conv_out, m_w_out, m_g_xattn, m_g_mem, m_w_xq, m_w_xk, m_w_xv, m_w_xo, m_g_ffn, m_w_up, m_w_ffn_conv, m_b_ffn_conv, m_w_down, m_g_final, v_rel_bias, v_g_mix, v_w_in, v_w_short_conv, v_g_attn_out, v_g_conv_out, v_w_out, v_g_xattn, v_g_mem, v_w_xq, v_w_xk, v_w_xv, v_w_xo, v_g_ffn, v_w_up, v_w_ffn_conv, v_b_ffn_conv, v_w_down, v_g_final):
    me = _dev_index(_mesh_pos())
    me_arr = me.reshape(1).astype(jnp.int32)

    big_names = ["w_in", "w_out", "w_xq", "w_xk", "w_xv", "w_xo", "w_up", "w_down"]
    late_names = big_names[1:]
    big_w = dict(w_in=w_in[0], w_out=w_out[0], w_xq=w_xq[0], w_xk=w_xk[0], w_xv=w_xv[0], w_xo=w_xo[0],
                 w_up=w_up[0].T, w_down=w_down[0])
    big_m = dict(w_in=m_w_in[0], w_out=m_w_out[0], w_xq=m_w_xq[0], w_xk=m_w_xk[0], w_xv=m_w_xv[0], w_xo=m_w_xo[0],
                 w_up=m_w_up[0].T, w_down=m_w_down[0])
    big_v = dict(w_in=v_w_in[0], w_out=v_w_out[0], w_xq=v_w_xq[0], w_xk=v_w_xk[0], w_xv=v_w_xv[0], w_xo=v_w_xo[0],
                 w_up=v_w_up[0].T, w_down=v_w_down[0])
    shard_shape = {n: big_w[n].shape for n in big_names}

    w_in_g, w_sc_g, w_fc_full = _all_gather([big_w["w_in"].astype(BF16), w_short_conv[0], w_ffn_conv[0]])
    w_sc_full = w_sc_g.transpose(1, 0, 2).reshape(3, CONV_W)
    late_shards = [big_w[n].astype(BF16) for n in late_names]
    ag_send, ag_recv, ag_srcs, ag_lands, ag_token = _gather_start("gather_weights_start", late_shards, w_in_g)
    forwarded = {}

    def forward_weights(names, after):
        which = [late_names.index(n) for n in names]
        fsend, frecv, lands, token = _gather_forward("gather_" + "_".join(names) + "_forward", ag_send, ag_recv,
                                                     [ag_lands[a] for a in which], which, after)
        forwarded[tuple(names)] = (fsend, frecv, lands)
        return token

    def late_weights(names, after):
        which = [late_names.index(n) for n in names]
        fsend, frecv, lands = forwarded[tuple(names)]
        lands = _gather_wait("gather_" + "_".join(names) + "_wait", ag_send, ag_recv, fsend, frecv,
                             [ag_srcs[a] for a in which], lands, which, after)
        out = {}
        for n, a, land in zip(names, which, lands):
            full = lax.dynamic_update_index_in_dim(land, late_shards[a], me, 0)
            if n == "w_up":
                out[n] = full
            elif n == "w_down":
                out[n] = full.reshape(N_DEV // 2, UP_CHUNK, D_MODEL)
            else:
                out[n] = full.reshape(D_MODEL, D_MODEL)
        return out

    sent = []

    def emit(grads):
        names = list(grads)
        blocks = [grads[n].reshape((N_DEV,) + shard_shape[n]) for n in names]
        started = _exchange_start("scatter_" + "_".join(names) + "_start", blocks, [False] * len(names), me_arr)
        sent.append((names, started))
        return started[-1]

    def emit_small(small):
        sent_small.append((small, _exchange_start("gather_small_start", [small], [True], me_arr)))
        return sent_small[0][1][-1]

    sent_small = []
    grad_x = _local_step(
        x[0], mem[0], loss_target[0], rel_bias, g_mix, w_in_g, w_sc_full, g_attn_out, g_conv_out, g_xattn, g_mem,
        g_ffn, w_fc_full, b_ffn_conv.reshape(N_DEV, 1, UP_CHUNK), g_final.reshape(1, D_MODEL), ag_token,
        forward_weights, late_weights, emit, emit_small)

    small_g, small_started = sent_small[0]
    after = sent[-1][1][-1]
    small_parts = _exchange_wait("gather_small_wait", small_started, [True], after)[1][0]
    big_out = {}
    after = small_parts
    for names, started in sent:
        blocks, lands = _exchange_wait("scatter_" + "_".join(names) + "_wait", started, [False] * len(names), after)
        for n, block, land in zip(names, blocks, lands):
            res = _adamw_big("adamw_" + n, big_w[n], block, land, big_m[n], big_v[n], me_arr)
            big_out[n] = [(r.T if n == "w_up" else r)[None] for r in res]
            after = res[0]

    as_rows = lambda a: a.reshape(N_DEV, UP_CHUNK)
    row1 = lambda a: a.reshape(1, D_MODEL)
    small_names = ["rel_bias", "g_mix", "g_attn_out", "g_conv_out", "g_xattn", "g_mem", "g_ffn", "b_ffn_conv", "g_final"]
    wmv = [
        (rel_bias, m_rel_bias, v_rel_bias), (g_mix, m_g_mix, v_g_mix), (g_attn_out, m_g_attn_out, v_g_attn_out),
        (g_conv_out, m_g_conv_out, v_g_conv_out), (g_xattn, m_g_xattn, v_g_xattn), (g_mem, m_g_mem, v_g_mem),
        (g_ffn, m_g_ffn, v_g_ffn), (as_rows(b_ffn_conv), as_rows(m_b_ffn_conv), as_rows(v_b_ffn_conv)),
        (row1(g_final), row1(m_g_final), row1(v_g_final))]
    g_packed, small_res = _adamw_small(small_g, small_parts, wmv, me_arr)
    small_out = dict(zip(small_names, small_res))
    loss = g_packed[ROW_LOSS, 0]
    small_out["b_ffn_conv"] = [a.reshape(1, 2 * D_FF) for a in small_out["b_ffn_conv"]]
    small_out["g_final"] = [a.reshape(D_MODEL) for a in small_out["g_final"]]

    g_wsc = lax.dynamic_slice(g_packed[ROW_WSC:ROW_WSC + 3, 0:CONV_W], (0, me * HEAD_DIM), (3, HEAD_DIM))
    g_wfc = lax.dynamic_slice(g_packed[ROW_WFC:ROW_WFC + 3 * N_DEV, 0:UP_CHUNK].reshape(3, N_DEV, UP_CHUNK),
                              (0, me, 0), (3, 1, UP_CHUNK)).reshape(3, UP_CHUNK)
    shard_res = _adamw_shards([(w_short_conv[0], g_wsc, m_w_short_conv[0], v_w_short_conv[0]),
                               (w_ffn_conv[0], g_wfc, m_w_ffn_conv[0], v_w_ffn_conv[0])])
    small_out["w_short_conv"] = [g_wsc[None]] + [a[None] for a in shard_res[0]]
    small_out["w_ffn_conv"] = [g_wfc[None]] + [a[None] for a in shard_res[1]]

    order = ["rel_bias", "g_mix", "w_in", "w_short_conv", "g_attn_out", "g_conv_out", "w_out", "g_xattn", "g_mem",
             "w_xq", "w_xk", "w_xv", "w_xo", "g_ffn", "w_up", "w_ffn_conv", "b_ffn_conv", "w_down", "g_final"]
    allp = {**big_out, **small_out}
    outs = [loss, grad_x[None]]
    for kind in range(4):
        outs += [allp[n][kind] for n in order]
    return tuple(outs)
```

```python
import math

import numpy as np
import jax
import jax.numpy as jnp
from jax import lax
from jax.experimental import pallas as pl
from jax.experimental.pallas import tpu as pltpu

F32 = jnp.float32
BF16 = jnp.bfloat16
MESH = pl.DeviceIdType.MESH

N_DEV = 8
D_MODEL = 1024
ATTN_W = 512
CONV_W = 512
N_HEADS = 8
HEAD_DIM = 64
WIN = 128
DILATIONS = (1, 4, 16)
N_BUCKETS = 32
BUCKET_MAX_EXACT = 16
BUCKET_MAX_DISTANCE = 2048
N_MEM_HEADS = 4
MEM_HEAD_DIM = 256
D_FF = 2816
IN_COLS = 3072
IN_CHUNK = IN_COLS // N_DEV
UP_CHUNK = 2 * D_FF // N_DEV
EPS = 1e-6

ADAM_LR = 0.001
ADAM_B1 = 0.9
ADAM_B2 = 0.999
ADAM_EPS = 1e-08
ADAM_WD = 0.01
ADAM_STEP = 10

SUBLANES = 8
LANES = 128
HALO = 16
TM = 512
TM_FFN = 256
TS_DW = 4096
SWA_BLOCKS = 8
VMEM_LIMIT = 56 * 1024 * 1024

ROW_RELB, ROW_GMIX, ROW_GXATTN, ROW_GMEM, ROW_GFFN, ROW_GFINAL, ROW_GAC = 0, 8, 16, 24, 32, 40, 48
ROW_WSC, ROW_BFC, ROW_WFC, ROW_LOSS, SMALL_ROWS = 56, 64, 72, 96, 104


def _cparams(n_grid):
    return pltpu.CompilerParams(dimension_semantics=("arbitrary",) * n_grid, vmem_limit_bytes=VMEM_LIMIT)


def _full(shape):
    nd = len(shape)
    return pl.BlockSpec(tuple(shape), lambda *_: (0,) * nd)


def _resident(shape):
    nd = len(shape)
    return pl.BlockSpec(tuple(shape), lambda *_: (0,) * nd, pipeline_mode=pl.Buffered(1))


ANY_SPEC = pl.BlockSpec(memory_space=pl.ANY)
HBM_SPEC = pl.BlockSpec(memory_space=pltpu.HBM)
SEM_SPEC = pl.BlockSpec(memory_space=pltpu.SEMAPHORE)
VMEM_SPEC = pl.BlockSpec(memory_space=pltpu.VMEM)
SMEM_SPEC = pl.BlockSpec(memory_space=pltpu.SMEM)
DATAFLOW = pltpu.SideEffectType.DATAFLOW_SIDE_EFFECTING


def _rms(x):
    r = lax.rsqrt(jnp.mean(x * x, axis=-1, keepdims=True) + EPS)
    return x * r, r


def _rms_bwd(xh, r, g, dy):
    dxh = dy * g
    return r * (dxh - xh * jnp.mean(dxh * xh, axis=-1, keepdims=True))


def _shift_down(u, halo, k):
    ru = pltpu.roll(u, k, 0)
    rh = pltpu.roll(halo, k, 0)
    row = lax.broadcasted_iota(jnp.int32, rh.shape, 0)
    head = jnp.where(row < k, rh, ru[0:SUBLANES])
    return jnp.concatenate([head, ru[SUBLANES:]], axis=0)


def _shift_up(u, halo, k):
    tm = u.shape[0]
    ru = pltpu.roll(u, tm - k, 0)
    rh = pltpu.roll(halo, SUBLANES - k, 0)
    row = lax.broadcasted_iota(jnp.int32, rh.shape, 0)
    tail = jnp.where(row >= SUBLANES - k, rh, ru[tm - SUBLANES:])
    return jnp.concatenate([ru[:tm - SUBLANES], tail], axis=0)


def _causal_conv3(u, halo, w_ref):
    return (_shift_down(u, halo, 2) * w_ref[0:1, :] + _shift_down(u, halo, 1) * w_ref[1:2, :]) + u * w_ref[2:3, :]


def _dot(a, b):
    return jnp.dot(a, b, preferred_element_type=F32)


def _dot_nt(a, b):
    return lax.dot_general(a, b, (((1,), (1,)), ((), ())), preferred_element_type=F32)


def _dot_tn(a, b):
    return lax.dot_general(a, b, (((0,), (0,)), ((), ())), preferred_element_type=F32)


def _sigmoid(x):
    return 0.5 * jnp.tanh(0.5 * x) + 0.5


def _bucket_tables():
    qi = np.arange(WIN)[:, None]
    kj = np.arange(2 * WIN)[None, :]
    steps = np.clip(qi + WIN - kj, 0, WIN)
    out = []
    for d in DILATIONS:
        dist = steps * d
        dd = np.maximum(dist, 1).astype(np.float32)
        large = BUCKET_MAX_EXACT + (
            np.log(dd / np.float32(BUCKET_MAX_EXACT)) / np.float32(math.log(BUCKET_MAX_DISTANCE / BUCKET_MAX_EXACT))
            * np.float32(N_BUCKETS - BUCKET_MAX_EXACT)).astype(np.int32)
        large = np.minimum(large, N_BUCKETS - 1)
        out.append(np.where(dist < BUCKET_MAX_EXACT, dist, large).astype(np.int32))
    return np.stack(out)


def _bias_fwd(rel_bias, buckets):
    present = [sorted(set(buckets[p].ravel().tolist())) for p in range(3)]

    def body(rb_ref, bk_ref, o_ref):
        for p in range(3):
            bk = bk_ref[p]
            for h in range(N_HEADS):
                acc = jnp.zeros((WIN, 2 * WIN), F32)
                for b in present[p]:
                    acc = jnp.where(bk == b, rb_ref[h, b], acc)
                o_ref[p, h] = acc

    return pl.pallas_call(
        body, name="bias_fwd",
        out_shape=jax.ShapeDtypeStruct((3, N_HEADS, WIN, 2 * WIN), F32),
        in_specs=[pl.BlockSpec(memory_space=pltpu.SMEM), pl.BlockSpec(memory_space=pltpu.VMEM)],
        out_specs=pl.BlockSpec(memory_space=pltpu.VMEM),
    )(rel_bias, jnp.asarray(buckets))


def _bias_bwd(dbias, buckets):
    present = [set(buckets[p].ravel().tolist()) for p in range(3)]

    def body(db_ref, bk_ref, o_ref):
        lane = lax.broadcasted_iota(jnp.int32, (1, D_MODEL), 1)
        rows = []
        for h in range(N_HEADS):
            row = jnp.zeros((1, D_MODEL), F32)
            for b in range(N_BUCKETS):
                tot = jnp.zeros((1, 1), F32)
                for p in (p for p in range(3) if b in present[p]):
                    sel = jnp.where(bk_ref[p] == b, db_ref[p, h], 0.0)
                    tot = tot + jnp.sum(jnp.sum(sel, axis=0, keepdims=True), axis=1, keepdims=True)
                row = jnp.where(lane == b, tot, row)
            rows.append(row)
        o_ref[...] = jnp.concatenate(rows, axis=0)

    return pl.pallas_call(
        body, name="bias_bwd",
        out_shape=jax.ShapeDtypeStruct((N_HEADS, D_MODEL), F32),
        in_specs=[pl.BlockSpec(memory_space=pltpu.VMEM), pl.BlockSpec(memory_space=pltpu.VMEM)],
        out_specs=pl.BlockSpec(memory_space=pltpu.VMEM),
    )(dbias, jnp.asarray(buckets))


def _spread(val, scr_ref, out_refs, dtype):
    out_refs[0][...] = val.astype(dtype)
    n_blk = val.shape[1] // LANES
    for c in range(n_blk):
        scr_ref[c] = val[:, c * LANES:(c + 1) * LANES]
    for o_ref, d in zip(out_refs[1:], DILATIONS[1:]):
        for r in range(d):
            for c in range(n_blk):
                o_ref[r, :, c * LANES:(c + 1) * LANES] = scr_ref.at[c][pl.ds(r, TM // d, stride=d), :].astype(dtype)


def _gather_classes(blk_ref, scr_ref, d):
    n_blk = blk_ref.shape[2] // LANES
    for r in range(d):
        for c in range(n_blk):
            scr_ref.at[c][pl.ds(r, TM // d, stride=d), :] = blk_ref[r, :, c * LANES:(c + 1) * LANES].astype(F32)
    return jnp.concatenate([scr_ref[c] for c in range(n_blk)], axis=1)


def _class_specs(cols):
    return [pl.BlockSpec((TM, cols), lambda i: (i, 0))] + [
        pl.BlockSpec((d, TM // d, cols), lambda i: (0, i, 0)) for d in DILATIONS[1:]]


def _class_shapes(s, cols, dtype):
    return [jax.ShapeDtypeStruct((s, cols), dtype)] + [
        jax.ShapeDtypeStruct((d, s // d, cols), dtype) for d in DILATIONS[1:]]


def _rms_proj(x, g_mix, w_in_g, dep):
    s = x.shape[0]

    def body(x_ref, g_ref, w_ref, dep_ref, h_ref, q1, q4, q16, k1, k4, k16, v1, v4, v16, gb_ref, gc_ref, xi_ref, scr):
        xh, _ = _rms(x_ref[...])
        h = (xh * g_ref[...]).astype(BF16)
        h_ref[...] = h
        proj = jnp.concatenate([_dot(h, w_ref[j]) for j in range(N_DEV)], axis=1)
        _spread(proj[:, 0:512] * (HEAD_DIM ** -0.5), scr, (q1, q4, q16), BF16)
        _spread(proj[:, 512:1024], scr, (k1, k4, k16), BF16)
        _spread(proj[:, 1024:1536], scr, (v1, v4, v16), BF16)
        gb_ref[...] = proj[:, 1536:2048]
        gc_ref[...] = proj[:, 2048:2560]
        xi_ref[...] = proj[:, 2560:3072]

    row = lambda n: pl.BlockSpec((TM, n), lambda i: (i, 0))
    res = pl.pallas_call(
        body, name="rms_proj", grid=(s // TM,),
        out_shape=[jax.ShapeDtypeStruct((s, D_MODEL), BF16)] + _class_shapes(s, 512, BF16) * 3
        + [jax.ShapeDtypeStruct((s, 512), F32)] * 3,
        in_specs=[row(D_MODEL), _full(g_mix.shape), _full(w_in_g.shape), ANY_SPEC],
        out_specs=[row(D_MODEL)] + _class_specs(512) * 3 + [row(512)] * 3,
        scratch_shapes=[pltpu.VMEM((512 // LANES, TM, LANES), F32)],
        compiler_params=_cparams(1),
    )(x, g_mix, w_in_g, dep)
    return res[0], res[1:4], res[4:7], res[7:10], res[10], res[11], res[12]


def _pair_split(x2):
    lane = lax.broadcasted_iota(jnp.int32, x2.shape, 1)
    zero = jnp.zeros_like(x2)
    return jnp.where(lane < HEAD_DIM, x2, zero), jnp.where(lane >= HEAD_DIM, x2, zero)


def _pair_join(even, odd):
    lane = lax.broadcasted_iota(jnp.int32, (even.shape[0], LANES), 1)
    return jnp.where(lane < HEAD_DIM, even, odd)


def _band_mask(first):
    qi = lax.broadcasted_iota(jnp.int32, (WIN, 2 * WIN), 0)
    kj = lax.broadcasted_iota(jnp.int32, (WIN, 2 * WIN), 1)
    steps = qi + WIN - kj
    return (steps >= 0) & (steps <= WIN) & (kj >= jnp.where(first, WIN, 0))


def _swa_fwd(qc, kc, vc, bias, dil, dep):
    nsub = min(SWA_BLOCKS, qc.shape[1] // WIN)
    nb = qc.shape[1] // (nsub * WIN)

    def body(q_ref, kp_ref, kc_ref, vp_ref, vc_ref, b_ref, dep_ref, o_ref, lse_ref, s_scr, p_scr):
        b = pl.program_id(1)
        pairs = [slice(a * LANES, (a + 1) * LANES) for a in range(N_HEADS // 2)]
        for sub in range(nsub):
            rows = slice(sub * WIN, (sub + 1) * WIN)

            def keys(prev_ref, cur_ref, sl):
                if sub == 0:
                    return jnp.concatenate([prev_ref[0, :, sl], cur_ref[0, 0:WIN, sl]], axis=0)
                return cur_ref[0, (sub - 1) * WIN:(sub + 1) * WIN, sl]

            for a, sl in enumerate(pairs):
                k2 = keys(kp_ref, kc_ref, sl)
                for e, qh in enumerate(_pair_split(q_ref[0, rows, sl])):
                    s_scr[sub, 2 * a + e] = _dot_nt(qh, k2)
            first = (b == 0) if sub == 0 else False
            lg = jnp.where(_band_mask(first), s_scr[sub] + b_ref[...], -jnp.inf)
            m = jnp.max(lg, axis=-1, keepdims=True)
            p = jnp.exp(lg - m)
            den = jnp.sum(p, axis=-1, keepdims=True)
            p_scr[sub] = p.astype(BF16)
            lse = m + jnp.log(den)
            for a, sl in enumerate(pairs):
                v_even, v_odd = _pair_split(keys(vp_ref, vc_ref, sl))
                o2 = _dot(p_scr[sub, 2 * a], v_even) + _dot(p_scr[sub, 2 * a + 1], v_odd)
                o_ref[0, rows, sl] = o2 / _pair_join(den[2 * a], den[2 * a + 1])
                lse_ref[0, rows, sl] = _pair_join(lse[2 * a], lse[2 * a + 1])

    cur = pl.BlockSpec((1, nsub * WIN, 512), lambda r, b: (r, b, 0))
    prev = pl.BlockSpec((1, WIN, 512), lambda r, b: (r, jnp.maximum(nsub * b - 1, 0), 0))
    wide = (nsub, N_HEADS, WIN, 2 * WIN)
    return pl.pallas_call(
        body, name=f"swa_fwd_d{dil}", grid=(dil, nb),
        out_shape=[jax.ShapeDtypeStruct(qc.shape, F32)] * 2,
        in_specs=[cur, prev, cur, prev, cur, _full(bias.shape), ANY_SPEC],
        out_specs=[cur] * 2,
        scratch_shapes=[pltpu.VMEM(wide, F32), pltpu.VMEM(wide, BF16)],
        compiler_params=_cparams(2),
    )(qc, kc, kc, vc, vc, bias, dep)


def _mix_out(branches, gb, gc, xi, x, w_sc, g_a, g_c, w_out):
    s = x.shape[0]
    tb = TM // SUBLANES

    def body(o1, l1, o4, l4, o16, l16, gb_ref, gc_ref, xi_ref, gch_ref, xih_ref, x_ref, wsc_ref,
             ga_ref, gcv_ref, wout_ref, attn_ref, lse1, lse4, lse16, mixed_ref, x1_ref, scr_a, scr_b, scr_c, scr_d):
        i = pl.program_id(0)
        la, lb, lc = l1[...], _gather_classes(l4, scr_a, 4), _gather_classes(l16, scr_b, 16)
        m_all = jnp.maximum(jnp.maximum(la, lb), lc)
        ea, eb, ec = jnp.exp(la - m_all), jnp.exp(lb - m_all), jnp.exp(lc - m_all)
        den = (ea + eb) + ec
        num = (ea * o1[...] + eb * _gather_classes(o4, scr_c, 4)) + ec * _gather_classes(o16, scr_d, 16)
        attn = num / den
        attn_ref[...] = attn
        _spread(m_all + jnp.log(den), scr_a, (lse1, lse4, lse16), F32)
        xa, _ = _rms(attn)
        u = gc_ref[...] * xi_ref[...]
        uh = jnp.where(i > 0, gch_ref[...] * xih_ref[...], 0.0)
        conv = gb_ref[...] * _causal_conv3(u, uh, wsc_ref)
        xc, _ = _rms(conv)
        mixed = jnp.concatenate([xa * ga_ref[...], xc * gcv_ref[...]], axis=1).astype(BF16)
        mixed_ref[...] = mixed
        x1_ref[...] = x_ref[...] + _dot(mixed, wout_ref[...])

    row = lambda n: pl.BlockSpec((TM, n), lambda i: (i, 0))
    halo = pl.BlockSpec((SUBLANES, 512), lambda i: (jnp.maximum(i * tb - 1, 0), 0))
    cs = _class_specs(512)
    flat = [a for br in branches for a in br]
    res = pl.pallas_call(
        body, name="mix_out", grid=(s // TM,),
        out_shape=[jax.ShapeDtypeStruct((s, 512), F32)] + _class_shapes(s, 512, F32)
        + [jax.ShapeDtypeStruct((s, D_MODEL), BF16), jax.ShapeDtypeStruct((s, D_MODEL), F32)],
        in_specs=[cs[0], cs[0], cs[1], cs[1], cs[2], cs[2], row(512), row(512), row(512), halo, halo,
                  row(D_MODEL), _full(w_sc.shape), _full(g_a.shape), _full(g_c.shape), _full(w_out.shape)],
        out_specs=[row(512)] + cs + [row(D_MODEL), row(D_MODEL)],
        scratch_shapes=[pltpu.VMEM((512 // LANES, TM, LANES), F32)] * 4,
        compiler_params=_cparams(1),
    )(*flat, gb, gc, xi, gc, xi, x, w_sc, g_a, g_c, w_out)
    return res[0], res[1:4], res[4], res[5]


def _mem_kv(mem, g_mem, w_xk, w_xv):
    def body(mem_ref, g_ref, wk_ref, wv_ref, mn_ref, k_ref, v_ref):
        xh, _ = _rms(mem_ref[...])
        mn = (xh * g_ref[...]).astype(BF16)
        mn_ref[...] = mn
        k_ref[...] = _dot(mn, wk_ref[...]).astype(BF16)
        v_ref[...] = _dot(mn, wv_ref[...]).astype(BF16)

    vm = pl.BlockSpec(memory_space=pltpu.VMEM)
    return pl.pallas_call(
        body, name="mem_kv",
        out_shape=[jax.ShapeDtypeStruct(mem.shape, BF16)] * 3,
        in_specs=[vm] * 4, out_specs=[vm] * 3,
        compiler_params=pltpu.CompilerParams(vmem_limit_bytes=VMEM_LIMIT),
    )(mem, g_mem, w_xk, w_xv)


def _xattn_fwd(x1, g, w_xq, k, v, w_xo, dep):
    s = x1.shape[0]

    def body(x1_ref, g_ref, wq_ref, k_ref, v_ref, wo_ref, dep_ref, h2_ref, q_ref, o_ref, x2_ref):
        x1v = x1_ref[...]
        xh, _ = _rms(x1v)
        h2 = (xh * g_ref[...]).astype(BF16)
        h2_ref[...] = h2
        qb = _dot(h2, wq_ref[...]).astype(BF16)
        q_ref[...] = qb
        outs = []
        for h in range(N_MEM_HEADS):
            sl = slice(h * MEM_HEAD_DIM, (h + 1) * MEM_HEAD_DIM)
            lg = _dot_nt(qb[:, sl], k_ref[:, sl]) * (MEM_HEAD_DIM ** -0.5)
            p = jnp.exp(lg - jnp.max(lg, axis=-1, keepdims=True))
            p = p / jnp.sum(p, axis=-1, keepdims=True)
            outs.append(_dot(p.astype(BF16), v_ref[:, sl]))
        o = jnp.concatenate(outs, axis=1).astype(BF16)
        o_ref[...] = o
        x2_ref[...] = x1v + _dot(o, wo_ref[...])

    row = pl.BlockSpec((TM, D_MODEL), lambda i: (i, 0))
    return pl.pallas_call(
        body, name="xattn_fwd", grid=(s // TM,),
        out_shape=[jax.ShapeDtypeStruct((s, D_MODEL), BF16)] * 3 + [jax.ShapeDtypeStruct((s, D_MODEL), F32)],
        in_specs=[row, _full(g.shape), _full(w_xq.shape), _full(k.shape), _full(v.shape), _full(w_xo.shape), ANY_SPEC],
        out_specs=[row] * 4,
        compiler_params=_cparams(1),
    )(x1, g, w_xq, k, v, w_xo, dep)


def _ffn_conv(h_ext, wup_ref, wfc_ref, bfc_ref, j):
    u = _dot_nt(h_ext, wup_ref[j])
    w = wfc_ref[j]
    c = ((pltpu.roll(u, 2, 0) * w[0:1, :] + pltpu.roll(u, 1, 0) * w[1:2, :]) + u * w[2:3, :]) + bfc_ref[j]
    return c[HALO:]


def _ffn_fwd(x2, g, w_up_g, w_fc, b_fc, w_down_g, g_final, target):
    s = x2.shape[0]
    tb = TM_FFN // HALO
    half = N_DEV // 2

    def body(x_ref, xp_ref, g_ref, wup_ref, wfc_ref, bfc_ref, wd_ref, gf_ref, t_ref, h_ref, c_ref, act_ref, dx3_ref,
             loss_ref, dgf_ref):
        i = pl.program_id(0)

        @pl.when(i == 0)
        def _():
            loss_ref[...] = jnp.zeros_like(loss_ref)
            dgf_ref[...] = jnp.zeros_like(dgf_ref)

        x2v = x_ref[...]
        gv = g_ref[...]
        h = (_rms(x2v)[0] * gv).astype(BF16)
        h_ref[...] = h
        hp = jnp.where(i > 0, _rms(xp_ref[...])[0] * gv, 0.0).astype(BF16)
        h_ext = jnp.concatenate([hp, h], axis=0)
        down = jnp.zeros((TM_FFN, D_MODEL), F32)
        for j in range(half):
            cg = _ffn_conv(h_ext, wup_ref, wfc_ref, bfc_ref, j)
            cv = _ffn_conv(h_ext, wup_ref, wfc_ref, bfc_ref, j + half)
            c_ref[j] = cg
            c_ref[j + half] = cv
            a = ((cg * _sigmoid(cg)) * cv).astype(BF16)
            act_ref[j] = a
            down = down + _dot(a, wd_ref[j])
        x3 = x2v + down
        xh, r = _rms(x3)
        gf = gf_ref[...]
        e = xh * gf - t_ref[...]
        loss_ref[...] += 0.5 * jnp.sum(jnp.sum(e * e, axis=1, keepdims=True), axis=0, keepdims=True) / D_MODEL
        dy = e * (1.0 / D_MODEL)
        dgf_ref[0:1, :] += jnp.sum(dy * xh, axis=0, keepdims=True)
        dx3_ref[...] = _rms_bwd(xh, r, gf, dy)

    row = pl.BlockSpec((TM_FFN, D_MODEL), lambda i: (i, 0))
    prev = pl.BlockSpec((HALO, D_MODEL), lambda i: (jnp.maximum(i * tb - 1, 0), 0))
    return pl.pallas_call(
        body, name="ffn_fwd", grid=(s // TM_FFN,),
        out_shape=[jax.ShapeDtypeStruct((s, D_MODEL), BF16), jax.ShapeDtypeStruct((N_DEV, s, UP_CHUNK), F32),
                   jax.ShapeDtypeStruct((half, s, UP_CHUNK), BF16),
                   jax.ShapeDtypeStruct((s, D_MODEL), F32), jax.ShapeDtypeStruct((SUBLANES, 128), F32),
                   jax.ShapeDtypeStruct((SUBLANES, D_MODEL), F32)],
        in_specs=[row, prev, _full(g.shape), _resident(w_up_g.shape), _full(w_fc.shape), _full(b_fc.shape),
                  _resident(w_down_g.shape), _full(g_final.shape), row],
        out_specs=[row, pl.BlockSpec((N_DEV, TM_FFN, UP_CHUNK), lambda i: (0, i, 0)),
                   pl.BlockSpec((half, TM_FFN, UP_CHUNK), lambda i: (0, i, 0)), row,
                   _full((SUBLANES, 128)), _full((SUBLANES, D_MODEL))],
        compiler_params=_cparams(1),
    )(x2, x2, g, w_up_g, w_fc, b_fc, w_down_g, g_final, target)


def _ffn_bwd(dx3, h3, conv, x2, g, w_up_g, w_fc, w_down_g):
    s = x2.shape[0]
    tb = TM_FFN // HALO
    last = s // HALO - 1
    n_tiles = s // TM_FFN
    half = N_DEV // 2
    n_ext = TM_FFN + HALO

    def body(dx_ref, dxn_ref, h_ref, c_ref, cn_ref, x2_ref, g_ref, wup_ref, wfc_ref, wd_ref,
             dup_ref, dx2_ref, dg_ref, dwfc_ref, dbfc_ref):
        i = pl.program_id(0)

        @pl.when(i == 0)
        def _():
            dg_ref[...] = jnp.zeros_like(dg_ref)
            dwfc_ref[...] = jnp.zeros_like(dwfc_ref)
            dbfc_ref[...] = jnp.zeros_like(dbfc_ref)

        dxv = dx_ref[...]
        dxn = jnp.where(i < n_tiles - 1, dxn_ref[...], 0.0)
        dx_ext = jnp.concatenate([dxv, dxn], axis=0).astype(BF16)
        h = h_ref[...]
        dh = jnp.zeros((TM_FFN, D_MODEL), F32)
        for j in range(half):
            cg = jnp.concatenate([c_ref[j], cn_ref[j]], axis=0)
            cv = jnp.concatenate([c_ref[j + half], cn_ref[j + half]], axis=0)
            dact = _dot_nt(dx_ext, wd_ref[j])
            sg = _sigmoid(cg)
            parts = ((j + half, dact * (cg * sg)), (j, (dact * cv) * (sg * (1.0 + cg * (1.0 - sg)))))
            for jj, dc in parts:
                u = _dot_nt(h, wup_ref[jj])
                dc0, dc1, dc2 = dc[:TM_FFN], pltpu.roll(dc, n_ext - 1, 0)[:TM_FFN], pltpu.roll(dc, n_ext - 2, 0)[:TM_FFN]
                dbfc_ref[jj:jj + 1, :] += jnp.sum(dc0, axis=0, keepdims=True)
                dwfc_ref[0, jj:jj + 1, :] += jnp.sum(dc2 * u, axis=0, keepdims=True)
                dwfc_ref[1, jj:jj + 1, :] += jnp.sum(dc1 * u, axis=0, keepdims=True)
                dwfc_ref[2, jj:jj + 1, :] += jnp.sum(dc0 * u, axis=0, keepdims=True)
                w = wfc_ref[jj]
                du = ((dc0 * w[2:3, :] + dc1 * w[1:2, :]) + dc2 * w[0:1, :]).astype(BF16)
                dup_ref[jj] = du
                dh = dh + _dot(du, wup_ref[jj])
        xh, r = _rms(x2_ref[...])
        dg_ref[0:1, :] += jnp.sum(dh * xh, axis=0, keepdims=True)
        dx2_ref[...] = dxv + _rms_bwd(xh, r, g_ref[...], dh)

    row = pl.BlockSpec((TM_FFN, D_MODEL), lambda i: (i, 0))
    nxt = pl.BlockSpec((HALO, D_MODEL), lambda i: (jnp.minimum((i + 1) * tb, last), 0))
    cur_c = pl.BlockSpec((N_DEV, TM_FFN, UP_CHUNK), lambda i: (0, i, 0))
    nxt_c = pl.BlockSpec((N_DEV, HALO, UP_CHUNK), lambda i: (0, jnp.minimum((i + 1) * tb, last), 0))
    return pl.pallas_call(
        body, name="ffn_bwd", grid=(n_tiles,),
        out_shape=[jax.ShapeDtypeStruct((N_DEV, s, UP_CHUNK), BF16), jax.ShapeDtypeStruct((s, D_MODEL), F32),
                   jax.ShapeDtypeStruct((SUBLANES, D_MODEL), F32), jax.ShapeDtypeStruct((3, N_DEV, UP_CHUNK), F32),
                   jax.ShapeDtypeStruct((N_DEV, UP_CHUNK), F32)],
        in_specs=[row, nxt, row, cur_c, nxt_c, row, _full(g.shape), _resident(w_up_g.shape), _full(w_fc.shape),
                  _resident(w_down_g.shape)],
        out_specs=[cur_c, row, _full((SUBLANES, D_MODEL)), _full((3, N_DEV, UP_CHUNK)), _full((N_DEV, UP_CHUNK))],
        compiler_params=_cparams(1),
    )(dx3, dx3, h3, conv, conv, x2, g, w_up_g, w_fc, w_down_g)


def _xattn_bwd(dx2, o, q, k, v, w_xo, w_xq, x1, g, dep):
    s = x1.shape[0]

    def body(dx2_ref, o_ref, q_ref, k_ref, v_ref, wo_ref, wq_ref, x1_ref, g_ref, dep_ref, dq_ref, dx1_ref, dk_ref,
             dv_ref, dg_ref):
        @pl.when(pl.program_id(0) == 0)
        def _():
            dk_ref[...] = jnp.zeros_like(dk_ref)
            dv_ref[...] = jnp.zeros_like(dv_ref)
            dg_ref[...] = jnp.zeros_like(dg_ref)

        dx2v = dx2_ref[...]
        do = _dot_nt(dx2v.astype(BF16), wo_ref[...])
        dqs = []
        for h in range(N_MEM_HEADS):
            sl = slice(h * MEM_HEAD_DIM, (h + 1) * MEM_HEAD_DIM)
            qh, kh, vh = q_ref[:, sl], k_ref[:, sl], v_ref[:, sl]
            lg = _dot_nt(qh, kh) * (MEM_HEAD_DIM ** -0.5)
            p = jnp.exp(lg - jnp.max(lg, axis=-1, keepdims=True))
            p = p / jnp.sum(p, axis=-1, keepdims=True)
            doh = do[:, sl].astype(BF16)
            dp = _dot_nt(doh, vh)
            ds = (p * (dp - jnp.sum(p * dp, axis=-1, keepdims=True)) * (MEM_HEAD_DIM ** -0.5)).astype(BF16)
            dqs.append(_dot(ds, kh))
            dk_ref[:, sl] += _dot_tn(ds, qh)
            dv_ref[:, sl] += _dot_tn(p.astype(BF16), doh)
        dq = jnp.concatenate(dqs, axis=1).astype(BF16)
        dq_ref[...] = dq
        dh2 = _dot_nt(dq, wq_ref[...])
        xh, r = _rms(x1_ref[...])
        dg_ref[0:1, :] += jnp.sum(dh2 * xh, axis=0, keepdims=True)
        dx1_ref[...] = dx2v + _rms_bwd(xh, r, g_ref[...], dh2)

    row = pl.BlockSpec((TM, D_MODEL), lambda i: (i, 0))
    return pl.pallas_call(
        body, name="xattn_bwd", grid=(s // TM,),
        out_shape=[jax.ShapeDtypeStruct((s, D_MODEL), BF16), jax.ShapeDtypeStruct((s, D_MODEL), F32),
                   jax.ShapeDtypeStruct(k.shape, F32), jax.ShapeDtypeStruct(k.shape, F32),
                   jax.ShapeDtypeStruct((SUBLANES, D_MODEL), F32)],
        in_specs=[row, row, row, _full(k.shape), _full(v.shape), _full(w_xo.shape), _full(w_xq.shape), row,
                  _full(g.shape), ANY_SPEC],
        out_specs=[row, row, _full(k.shape), _full(k.shape), _full((SUBLANES, D_MODEL))],
        compiler_params=_cparams(1),
    )(dx2, o, q, k, v, w_xo, w_xq, x1, g, dep)


def _mem_kv_bwd(dk, dv, mem_n, mem, w_xk, w_xv):
    def body(dk_ref, dv_ref, mn_ref, mem_ref, wk_ref, wv_ref, dwk_ref, dwv_ref, dg_ref):
        dkb, dvb = dk_ref[...].astype(BF16), dv_ref[...].astype(BF16)
        mn = mn_ref[...]
        dwk_ref[...] = _dot_tn(mn, dkb).astype(BF16)
        dwv_ref[...] = _dot_tn(mn, dvb).astype(BF16)
        dmn = _dot_nt(dkb, wk_ref[...]) + _dot_nt(dvb, wv_ref[...])
        xh, _ = _rms(mem_ref[...])
        dg_ref[...] = jnp.zeros_like(dg_ref)
        dg_ref[0:1, :] = jnp.sum(dmn * xh, axis=0, keepdims=True)

    vm = pl.BlockSpec(memory_space=pltpu.VMEM)
    return pl.pallas_call(
        body, name="mem_kv_bwd",
        out_shape=[jax.ShapeDtypeStruct(w_xk.shape, BF16), jax.ShapeDtypeStruct(w_xv.shape, BF16),
                   jax.ShapeDtypeStruct((SUBLANES, D_MODEL), F32)],
        in_specs=[vm] * 6, out_specs=[vm] * 3,
        compiler_params=pltpu.CompilerParams(vmem_limit_bytes=VMEM_LIMIT),
    )(dk, dv, mem_n, mem, w_xk, w_xv)


def _mix_out_bwd(dx1, w_out, attn, gb, gc, xi, w_sc, g_a, g_c, dep):
    s = dx1.shape[0]
    tb = TM // SUBLANES

    def body(dx1_ref, wout_ref, attn_ref, gb_ref, gc_ref, xi_ref, gch_ref, xih_ref, wsc_ref, ga_ref, gcv_ref, dep_ref,
             da1, da4, da16, dd1, dd4, dd16, dgb_ref, dcv_ref, dga_ref, dgc_ref, dwsc_ref, scr):
        i = pl.program_id(0)

        @pl.when(i == 0)
        def _():
            dga_ref[...] = jnp.zeros_like(dga_ref)
            dgc_ref[...] = jnp.zeros_like(dgc_ref)
            dwsc_ref[...] = jnp.zeros_like(dwsc_ref)

        dmixed = _dot_nt(dx1_ref[...].astype(BF16), wout_ref[...])
        da, dcn = dmixed[:, :ATTN_W], dmixed[:, ATTN_W:]
        attn = attn_ref[...]
        xa, ra = _rms(attn)
        dga_ref[0:1, :] += jnp.sum(da * xa, axis=0, keepdims=True)
        dattn = _rms_bwd(xa, ra, ga_ref[...], da)
        _spread(dattn, scr, (da1, da4, da16), BF16)
        prod = dattn * attn
        dd = jnp.concatenate(
            [jnp.broadcast_to(jnp.sum(prod[:, h * HEAD_DIM:(h + 1) * HEAD_DIM], axis=-1, keepdims=True),
                              (TM, HEAD_DIM)) for h in range(N_HEADS)], axis=1)
        _spread(dd, scr, (dd1, dd4, dd16), F32)
        gbv = gb_ref[...]
        u = gc_ref[...] * xi_ref[...]
        uh = jnp.where(i > 0, gch_ref[...] * xih_ref[...], 0.0)
        u2, u1 = _shift_down(u, uh, 2), _shift_down(u, uh, 1)
        cv = (u2 * wsc_ref[0:1, :] + u1 * wsc_ref[1:2, :]) + u * wsc_ref[2:3, :]
        xc, rc = _rms(gbv * cv)
        dgc_ref[0:1, :] += jnp.sum(dcn * xc, axis=0, keepdims=True)
        dconv = _rms_bwd(xc, rc, gcv_ref[...], dcn)
        dgb_ref[...] = (dconv * cv).astype(BF16)
        dcv = dconv * gbv
        dcv_ref[...] = dcv
        dwsc_ref[0:1, :] += jnp.sum(dcv * u2, axis=0, keepdims=True)
        dwsc_ref[1:2, :] += jnp.sum(dcv * u1, axis=0, keepdims=True)
        dwsc_ref[2:3, :] += jnp.sum(dcv * u, axis=0, keepdims=True)

    row = lambda n: pl.BlockSpec((TM, n), lambda i: (i, 0))
    halo = pl.BlockSpec((SUBLANES, 512), lambda i: (jnp.maximum(i * tb - 1, 0), 0))
    acc = _full((SUBLANES, 512))
    res = pl.pallas_call(
        body, name="mix_out_bwd", grid=(s // TM,),
        out_shape=_class_shapes(s, 512, BF16) + _class_shapes(s, 512, F32)
        + [jax.ShapeDtypeStruct((s, 512), BF16), jax.ShapeDtypeStruct((s, 512), F32)]
        + [jax.ShapeDtypeStruct((SUBLANES, 512), F32)] * 3,
        in_specs=[row(D_MODEL), _full(w_out.shape), row(512), row(512), row(512), row(512), halo, halo,
                  _full(w_sc.shape), _full(g_a.shape), _full(g_c.shape), ANY_SPEC],
        out_specs=_class_specs(512) * 2 + [row(512)] * 2 + [acc] * 3,
        scratch_shapes=[pltpu.VMEM((512 // LANES, TM, LANES), F32)],
        compiler_params=_cparams(1),
    )(dx1, w_out, attn, gb, gc, xi, gc, xi, w_sc, g_a, g_c, dep)
    return res[0:3], res[3:6], res[6], res[7], res[8], res[9], res[10]


def _swa_bwd(qc, kc, vc, doc, lsec, ddc, bias, dil, dep):
    n128 = qc.shape[1] // WIN
    nsub = min(SWA_BLOCKS, n128)
    nb = n128 // nsub

    def body(q_ref, qn_ref, kp_ref, kc_ref, vp_ref, vc_ref, do_ref, don_ref, lse_ref, lsen_ref, dd_ref, ddn_ref,
             b_ref, dep_ref, dq_ref, dk_ref, dv_ref, db_ref, s_scr, dp_scr, sn_scr, dpn_scr, ds_scr, p_scr, dsn_scr,
             pn_scr):
        r, b = pl.program_id(0), pl.program_id(1)

        @pl.when((r == 0) & (b == 0))
        def _():
            db_ref[...] = jnp.zeros_like(db_ref)

        pairs = [slice(a * LANES, (a + 1) * LANES) for a in range(N_HEADS // 2)]
        blk = [slice(t * WIN, (t + 1) * WIN) for t in range(nsub)]
        last = blk[nsub - 1]
        per_head = lambda ref, rows: jnp.stack([ref[0, rows, h * HEAD_DIM:h * HEAD_DIM + 1] for h in range(N_HEADS)])

        def keys(prev_ref, cur_ref, t, sl):
            if t == 0:
                return jnp.concatenate([prev_ref[0, :, sl], cur_ref[0, blk[0], sl]], axis=0)
            return cur_ref[0, (t - 1) * WIN:(t + 1) * WIN, sl]

        for a, sl in enumerate(pairs):
            for t in range(nsub):
                k2, v2 = keys(kp_ref, kc_ref, t, sl), keys(vp_ref, vc_ref, t, sl)
                q_eo = _pair_split(q_ref[0, blk[t], sl])
                do_eo = _pair_split(do_ref[0, blk[t], sl].astype(BF16))
                for e in range(2):
                    s_scr[t, 2 * a + e] = _dot_nt(q_eo[e], k2)
                    dp_scr[t, 2 * a + e] = _dot_nt(do_eo[e], v2)
            qn_eo = _pair_split(qn_ref[0, :, sl])
            don_eo = _pair_split(don_ref[0, :, sl].astype(BF16))
            for e in range(2):
                sn_scr[2 * a + e] = _dot_nt(qn_eo[e], kc_ref[0, last, sl])
                dpn_scr[2 * a + e] = _dot_nt(don_eo[e], vc_ref[0, last, sl])
        bias = b_ref[...]
        for t in range(nsub):
            first = (b == 0) if t == 0 else False
            p = jnp.exp(jnp.where(_band_mask(first), s_scr[t] + bias, -jnp.inf) - per_head(lse_ref, blk[t]))
            ds = p * (dp_scr[t] - per_head(dd_ref, blk[t]))
            db_ref[...] += ds
            ds_scr[t] = ds.astype(BF16)
            p_scr[t] = p.astype(BF16)
        qi = lax.broadcasted_iota(jnp.int32, (WIN, WIN), 0)
        kj = lax.broadcasted_iota(jnp.int32, (WIN, WIN), 1)
        valid_n = kj >= qi + jnp.where(b + 1 < nb, 0, WIN)
        every = slice(0, WIN)
        pn = jnp.exp(jnp.where(valid_n, sn_scr[...] + bias[:, :, :WIN], -jnp.inf) - per_head(lsen_ref, every))
        dsn_scr[...] = (pn * (dpn_scr[...] - per_head(ddn_ref, every))).astype(BF16)
        pn_scr[...] = pn.astype(BF16)
        for a, sl in enumerate(pairs):
            q_eo = [_pair_split(q_ref[0, blk[t], sl]) for t in range(nsub)] + [_pair_split(qn_ref[0, :, sl])]
            do_eo = [_pair_split(do_ref[0, blk[t], sl].astype(BF16)) for t in range(nsub)]
            do_eo.append(_pair_split(don_ref[0, :, sl].astype(BF16)))
            for t in range(nsub):
                k_eo = _pair_split(keys(kp_ref, kc_ref, t, sl))
                dq, dk, dv = None, None, None
                for e in range(2):
                    h = 2 * a + e
                    ds_next = ds_scr[t + 1, h, :, :WIN] if t + 1 < nsub else dsn_scr[h]
                    p_next = p_scr[t + 1, h, :, :WIN] if t + 1 < nsub else pn_scr[h]
                    terms = (_dot(ds_scr[t, h], k_eo[e]),
                             _dot_tn(ds_scr[t, h, :, WIN:], q_eo[t][e]) + _dot_tn(ds_next, q_eo[t + 1][e]),
                             _dot_tn(p_scr[t, h, :, WIN:], do_eo[t][e]) + _dot_tn(p_next, do_eo[t + 1][e]))
                    dq, dk, dv = terms if e == 0 else (dq + terms[0], dk + terms[1], dv + terms[2])
                dq_ref[0, blk[t], sl] = dq.astype(BF16)
                dk_ref[0, blk[t], sl] = dk.astype(BF16)
                dv_ref[0, blk[t], sl] = dv.astype(BF16)

    cur = pl.BlockSpec((1, nsub * WIN, 512), lambda r, b: (r, b, 0))
    prev = pl.BlockSpec((1, WIN, 512), lambda r, b: (r, jnp.maximum(nsub * b - 1, 0), 0))
    nxt = pl.BlockSpec((1, WIN, 512), lambda r, b: (r, jnp.minimum(nsub * b + nsub, n128 - 1), 0))
    wide, narrow = (nsub, N_HEADS, WIN, 2 * WIN), (N_HEADS, WIN, WIN)
    return pl.pallas_call(
        body, name=f"swa_bwd_d{dil}", grid=(dil, nb),
        out_shape=[jax.ShapeDtypeStruct(qc.shape, BF16)] * 3 + [jax.ShapeDtypeStruct(bias.shape, F32)],
        in_specs=[cur, nxt, prev, cur, prev, cur, cur, nxt, cur, nxt, cur, nxt, _full(bias.shape), ANY_SPEC],
        out_specs=[cur] * 3 + [_full(bias.shape)],
        scratch_shapes=[pltpu.VMEM(wide, F32), pltpu.VMEM(wide, F32), pltpu.VMEM(narrow, F32),
                        pltpu.VMEM(narrow, F32), pltpu.VMEM(wide, BF16), pltpu.VMEM(wide, BF16),
                        pltpu.VMEM(narrow, BF16), pltpu.VMEM(narrow, BF16)],
        compiler_params=_cparams(2),
    )(qc, qc, kc, kc, vc, vc, doc, doc, lsec, lsec, ddc, ddc, bias, dep)


def _in_proj_bwd(dqs, dks, dvs, dgb, dcv, gc, xi, w_sc, w_in_g, x, g_mix, dx1):
    s = x.shape[0]
    tb = TM // SUBLANES
    last = s // SUBLANES - 1
    n_tiles = s // TM

    def body(dq1, dq4, dq16, dk1, dk4, dk16, dv1, dv4, dv16, dgb_ref, dcv_ref, dcvn_ref, gc_ref, xi_ref, wsc_ref,
             win_ref, x_ref, g_ref, dx1_ref, dproj_ref, gx_ref, dg_ref, scr_a, scr_b):
        i = pl.program_id(0)

        @pl.when(i == 0)
        def _():
            dg_ref[...] = jnp.zeros_like(dg_ref)

        d0 = dcv_ref[...]
        dn = jnp.where(i < n_tiles - 1, dcvn_ref[...], 0.0)
        du = (d0 * wsc_ref[2:3, :] + _shift_up(d0, dn, 1) * wsc_ref[1:2, :]) + _shift_up(d0, dn, 2) * wsc_ref[0:1, :]
        merge = lambda a, b4, b16: ((a[...].astype(F32) + _gather_classes(b4, scr_a, 4))
                                    + _gather_classes(b16, scr_b, 16))
        dq = merge(dq1, dq4, dq16) * (HEAD_DIM ** -0.5)
        dk = merge(dk1, dk4, dk16)
        dv = merge(dv1, dv4, dv16)
        dproj = jnp.concatenate([dq, dk, dv, dgb_ref[...].astype(F32), du * xi_ref[...], du * gc_ref[...]],
                                axis=1).astype(BF16)
        dproj_ref[...] = dproj
        dh = jnp.zeros((TM, D_MODEL), F32)
        for j in range(N_DEV):
            dh = dh + _dot_nt(dproj[:, j * IN_CHUNK:(j + 1) * IN_CHUNK], win_ref[j])
        xh, r = _rms(x_ref[...])
        dg_ref[0:1, :] += jnp.sum(dh * xh, axis=0, keepdims=True)
        gx_ref[...] = dx1_ref[...] + _rms_bwd(xh, r, g_ref[...], dh)

    row = lambda n: pl.BlockSpec((TM, n), lambda i: (i, 0))
    nxt = pl.BlockSpec((SUBLANES, 512), lambda i: (jnp.minimum((i + 1) * tb, last), 0))
    return pl.pallas_call(
        body, name="in_proj_bwd", grid=(n_tiles,),
        out_shape=[jax.ShapeDtypeStruct((s, IN_COLS), BF16), jax.ShapeDtypeStruct((s, D_MODEL), F32),
                   jax.ShapeDtypeStruct((SUBLANES, D_MODEL), F32)],
        in_specs=_class_specs(512) * 3 + [row(512), row(512), nxt, row(512), row(512), _full(w_sc.shape),
                                          _full(w_in_g.shape), row(D_MODEL), _full(g_mix.shape), row(D_MODEL)],
        out_specs=[row(IN_COLS), row(D_MODEL), _full((SUBLANES, D_MODEL))],
        scratch_shapes=[pltpu.VMEM((512 // LANES, TM, LANES), F32)] * 2,
        compiler_params=_cparams(1),
    )(*dqs, *dks, *dvs, dgb, dcv, dcv, gc, xi, w_sc, w_in_g, x, g_mix, dx1)


def _dw(a, b, dep, name, a_chunked=False, b_chunked=False, n_chunks=1, chunk_cols=None):
    ts = TS_DW if (a_chunked or b_chunked or chunk_cols) else TS_DW // 2
    if a_chunked:
        nj, s, kk = a.shape
        nn = b.shape[1]
        a_spec = pl.BlockSpec((1, ts, kk), lambda j, t: (j, t, 0))
        b_spec = pl.BlockSpec((ts, nn), lambda j, t: (t, 0))
    elif b_chunked:
        nj, s, nn = b.shape
        kk = a.shape[1]
        a_spec = pl.BlockSpec((ts, kk), lambda j, t: (t, 0))
        b_spec = pl.BlockSpec((1, ts, nn), lambda j, t: (j, t, 0))
    else:
        s, kk = a.shape
        nj, nn = (n_chunks, chunk_cols) if chunk_cols else (1, b.shape[1])
        a_spec = pl.BlockSpec((ts, kk), lambda j, t: (t, 0))
        b_spec = pl.BlockSpec((ts, nn), lambda j, t: (t, j))
    n_steps = s // ts

    def body(a_ref, b_ref, dep_ref, o_ref, acc):
        t = pl.program_id(1)

        @pl.when(t == 0)
        def _():
            acc[...] = jnp.zeros_like(acc)

        av = (a_ref[0] if a_chunked else a_ref[...]).astype(BF16)
        bv = (b_ref[0] if b_chunked else b_ref[...]).astype(BF16)
        acc[...] += _dot_tn(av, bv)

        @pl.when(t == n_steps - 1)
        def _():
            o_ref[0] = acc[...].astype(BF16)

    return pl.pallas_call(
        body, name=name, grid=(nj, n_steps),
        out_shape=jax.ShapeDtypeStruct((nj, kk, nn), BF16),
        in_specs=[a_spec, b_spec, ANY_SPEC],
        out_specs=pl.BlockSpec((1, kk, nn), lambda j, t: (j, 0, 0)),
        scratch_shapes=[pltpu.VMEM((kk, nn), F32)],
        compiler_params=_cparams(2),
    )(a, b, dep)


def _adamw_math(w, g, m, v):
    m2 = ADAM_B1 * m + (1.0 - ADAM_B1) * g
    v2 = ADAM_B2 * v + (1.0 - ADAM_B2) * (g * g)
    m_hat = m2 / (1.0 - ADAM_B1 ** ADAM_STEP)
    v_hat = v2 / (1.0 - ADAM_B2 ** ADAM_STEP)
    delta = -ADAM_LR * (m_hat / (jnp.sqrt(v_hat) + ADAM_EPS) + ADAM_WD * w)
    return delta, m2, v2


def _sum_parts(me, own, p_ref):
    g = None
    for i in range(N_DEV):
        part = jnp.where(me == i, own.astype(F32), p_ref[i].astype(F32))
        g = part if g is None else g + part
    return g


def _adamw_big(name, w, sent, parts, m, v, me_arr):
    rr, cc = w.shape
    tr = rr // 4 if rr >= 512 else rr

    def body(me_ref, w_ref, own_ref, p_ref, m_ref, v_ref, g_ref, d_ref, nm_ref, nv_ref):
        g = own_ref[0].astype(F32)
        for k in range(1, N_DEV):
            g = g + p_ref[(me_ref[0] + k) % N_DEV].astype(F32)
        g_ref[...] = g
        d_ref[...], nm_ref[...], nv_ref[...] = _adamw_math(w_ref[...], g, m_ref[...], v_ref[...])

    row = pl.BlockSpec((tr, cc), lambda i, me: (i, 0))
    return pl.pallas_call(
        body, name=name,
        grid_spec=pltpu.PrefetchScalarGridSpec(
            num_scalar_prefetch=1, grid=(rr // tr,),
            in_specs=[row, pl.BlockSpec((1, tr, cc), lambda i, me: (me[0], i, 0)),
                      pl.BlockSpec((N_DEV, tr, cc), lambda i, me: (0, i, 0)), row, row],
            out_specs=[row] * 4),
        out_shape=[jax.ShapeDtypeStruct((rr, cc), F32)] * 4,
        compiler_params=_cparams(1),
    )(me_arr, w, sent, parts, m, v)


def _small_slices():
    return [
        (slice(ROW_RELB, ROW_RELB + 8), slice(0, N_BUCKETS)),
        (slice(ROW_GMIX, ROW_GMIX + 1), slice(0, D_MODEL)),
        (slice(ROW_GAC, ROW_GAC + 1), slice(0, ATTN_W)),
        (slice(ROW_GAC, ROW_GAC + 1), slice(ATTN_W, D_MODEL)),
        (slice(ROW_GXATTN, ROW_GXATTN + 1), slice(0, D_MODEL)),
        (slice(ROW_GMEM, ROW_GMEM + 1), slice(0, D_MODEL)),
        (slice(ROW_GFFN, ROW_GFFN + 1), slice(0, D_MODEL)),
        (slice(ROW_BFC, ROW_BFC + 8), slice(0, UP_CHUNK)),
        (slice(ROW_GFINAL, ROW_GFINAL + 1), slice(0, D_MODEL)),
    ]


def _adamw_small(own, parts, wmv, me_arr):
    slices = _small_slices()
    n = len(slices)

    def body(*refs):
        me_ref, own_ref, p_ref = refs[:3]
        ins = refs[3:3 + 3 * n]
        g_ref = refs[3 + 3 * n]
        outs = refs[4 + 3 * n:]
        g = _sum_parts(me_ref[0], own_ref[...], p_ref)
        g_ref[...] = g
        for a, (rs, ls) in enumerate(slices):
            ga = g[rs, ls]
            outs[4 * a][...] = ga
            outs[4 * a + 1][...], outs[4 * a + 2][...], outs[4 * a + 3][...] = _adamw_math(
                ins[3 * a][...], ga, ins[3 * a + 1][...], ins[3 * a + 2][...])

    vm = pl.BlockSpec(memory_space=pltpu.VMEM)
    flat = [t for trip in wmv for t in trip]
    out_shape = [jax.ShapeDtypeStruct((SMALL_ROWS, D_MODEL), F32)]
    for w, _, _ in wmv:
        out_shape += [jax.ShapeDtypeStruct(w.shape, F32)] * 4
    res = pl.pallas_call(
        body, name="adamw_small", out_shape=out_shape,
        in_specs=[SMEM_SPEC] + [vm] * (2 + 3 * n), out_specs=[vm] * len(out_shape),
    )(me_arr, own, parts, *flat)
    return res[0], [res[1 + 4 * a:5 + 4 * a] for a in range(n)]


def _adamw_shards(items):
    n = len(items)

    def body(*refs):
        for a in range(n):
            w_ref, g_ref, m_ref, v_ref = refs[4 * a:4 * a + 4]
            d_ref, nm_ref, nv_ref = refs[4 * n + 3 * a:4 * n + 3 * a + 3]
            d_ref[...], nm_ref[...], nv_ref[...] = _adamw_math(w_ref[...], g_ref[...], m_ref[...], v_ref[...])

    vm = pl.BlockSpec(memory_space=pltpu.VMEM)
    out_shape = []
    for w, _, _, _ in items:
        out_shape += [jax.ShapeDtypeStruct(w.shape, F32)] * 3
    res = pl.pallas_call(
        body, name="adamw_shards", out_shape=out_shape, in_specs=[vm] * (4 * n), out_specs=[vm] * (3 * n),
    )(*[t for it in items for t in it])
    return [res[3 * a:3 * a + 3] for a in range(n)]


def _mesh_pos():
    return lax.axis_index("x"), lax.axis_index("y"), lax.axis_index("c")


def _dev_index(p):
    return 4 * p[0] + 2 * p[1] + p[2]


def _all_gather(shards):
    n = len(shards)

    def body(*refs):
        ins, outs = refs[:n], refs[n:2 * n]
        send_sems, recv_sems, loc_sems = refs[2 * n:]
        x, y, c = _mesh_pos()
        me, sib = (x, y, c), (x, y, 1 - c)
        chips = [(1 - x, y), (x, 1 - y), (1 - x, 1 - y)]

        def cp(a, k, block, to, src=None):
            dst = outs[a].at[_dev_index(block)]
            return pltpu.make_async_remote_copy(
                src_ref=dst if src is None else src, dst_ref=dst, send_sem=send_sems.at[a, k],
                recv_sem=recv_sems.at[a, k], device_id=to, device_id_type=MESH)

        mine = [pltpu.make_async_copy(ins[a], outs[a].at[_dev_index(me)], loc_sems.at[a]) for a in range(n)]
        for m_ in mine:
            m_.start()
        first = []
        for a in range(n):
            first.append(cp(a, 0, me, sib, src=ins[a]))
            first += [cp(a, 1 + j, me, (*chip, c), src=ins[a]) for j, chip in enumerate(chips)]
        for f in first:
            f.start()
        passed = []
        for a in range(n):
            for j, chip in enumerate(chips):
                cp(a, 1 + j, (*chip, c), me).wait_recv()
                fwd = cp(a, 4 + j, (*chip, c), sib)
                fwd.start()
                passed.append(fwd)
        for a in range(n):
            cp(a, 0, sib, me).wait_recv()
            for j, chip in enumerate(chips):
                cp(a, 4 + j, (*chip, 1 - c), me).wait_recv()
        for f in first + passed:
            f.wait_send()
        for m_ in mine:
            m_.wait()

    hbm = pl.BlockSpec(memory_space=pltpu.HBM)
    return pl.pallas_call(
        body, name="all_gather_weights",
        out_shape=[jax.ShapeDtypeStruct((N_DEV,) + a.shape, a.dtype) for a in shards],
        in_specs=[hbm] * n, out_specs=[hbm] * n,
        scratch_shapes=[pltpu.SemaphoreType.DMA((n, 7)), pltpu.SemaphoreType.DMA((n, 7)),
                        pltpu.SemaphoreType.DMA((n,))],
    )(*shards)


def _peers():
    x, y, c = _mesh_pos()
    return (x, y, c), [((1 - x) if k & 4 else x, (1 - y) if k & 2 else y, (1 - c) if k & 1 else c)
                       for k in range(1, 8)]


def _exchange_copy(src_ref, land_ref, whole, send_sems, recv_sems, a, k, peer, slot):
    src = src_ref if whole else src_ref.at[_dev_index(peer)]
    return pltpu.make_async_remote_copy(
        src_ref=src, dst_ref=land_ref.at[slot], send_sem=send_sems.at[7 * a + k], recv_sem=recv_sems.at[7 * a + k],
        device_id=peer, device_id_type=MESH)


def _exchange_start(name, srcs, whole, dep):
    n = len(srcs)
    lands = [lax.empty(((N_DEV,) + s.shape) if w else s.shape, s.dtype) for s, w in zip(srcs, whole)]

    def body(*refs):
        src_refs, land_refs = refs[:n], refs[n:2 * n]
        send_sems, recv_sems, token = refs[2 * n + 1], refs[2 * n + 2], refs[-1]
        me, peers = _peers()
        for a in range(n):
            for k, peer in enumerate(peers):
                _exchange_copy(src_refs[a], land_refs[a], whole[a], send_sems, recv_sems, a, k, peer,
                               _dev_index(me)).start()
        token[...] = jnp.zeros_like(token)

    res = pl.pallas_call(
        body, name=name,
        out_shape=(pltpu.SemaphoreType.DMA((7 * n,)), pltpu.SemaphoreType.DMA((7 * n,)),
                   *[pltpu.HBM(a.shape, a.dtype) for a in srcs], *[pltpu.HBM(a.shape, a.dtype) for a in lands],
                   jax.ShapeDtypeStruct((SUBLANES, 128), F32)),
        in_specs=[HBM_SPEC] * (2 * n) + [ANY_SPEC],
        out_specs=(SEM_SPEC, SEM_SPEC, *([HBM_SPEC] * (2 * n)), VMEM_SPEC),
        input_output_aliases={i: 2 + i for i in range(2 * n)},
        compiler_params=pltpu.CompilerParams(has_side_effects=DATAFLOW),
    )(*[pltpu.with_memory_space_constraint(a, pltpu.HBM) for a in srcs],
      *[pltpu.with_memory_space_constraint(a, pltpu.HBM) for a in lands], dep)
    return res[0], res[1], list(res[2:2 + n]), list(res[2 + n:2 + 2 * n]), res[-1]


def _exchange_wait(name, started, whole, after, which=None):
    send_sems, recv_sems, srcs, lands, _ = started
    which = list(range(len(srcs))) if which is None else which
    srcs, lands = [srcs[a] for a in which], [lands[a] for a in which]
    n = len(srcs)

    def body(*refs):
        src_refs, land_refs = refs[:n], refs[n:2 * n]
        send_sems, recv_sems = refs[2 * n], refs[2 * n + 1]
        _, peers = _peers()
        for i, a in enumerate(which):
            for k, peer in enumerate(peers):
                cp = _exchange_copy(src_refs[i], land_refs[i], whole[a], send_sems, recv_sems, a, k, peer,
                                    _dev_index(peer))
                cp.wait_send()
                cp.wait_recv()

    res = pl.pallas_call(
        body, name=name,
        out_shape=[pltpu.HBM(a.shape, a.dtype) for a in srcs + lands],
        in_specs=[HBM_SPEC] * (2 * n) + [SEM_SPEC, SEM_SPEC, ANY_SPEC],
        out_specs=[HBM_SPEC] * (2 * n),
        input_output_aliases={i: i for i in range(2 * n)},
        compiler_params=pltpu.CompilerParams(has_side_effects=DATAFLOW),
    )(*srcs, *lands, send_sems, recv_sems, after)
    return list(res[:n]), list(res[n:])


def _gather_start(name, shards, dep):
    n = len(shards)
    lands = [lax.empty((N_DEV,) + a.shape, a.dtype) for a in shards]

    def body(*refs):
        src_refs, land_refs = refs[:n], refs[n:2 * n]
        send_sems, recv_sems, token = refs[2 * n + 1], refs[2 * n + 2], refs[-1]
        x, y, c = _mesh_pos()
        peers = [(x, y, 1 - c), (1 - x, y, c), (x, 1 - y, c), (1 - x, 1 - y, c)]
        for a in range(n):
            for k, peer in enumerate(peers):
                pltpu.make_async_remote_copy(
                    src_ref=src_refs[a], dst_ref=land_refs[a].at[_dev_index((x, y, c))], send_sem=send_sems.at[4 * a + k],
                    recv_sem=recv_sems.at[4 * a + k], device_id=peer, device_id_type=MESH).start()
        token[...] = jnp.zeros_like(token)

    res = pl.pallas_call(
        body, name=name,
        out_shape=(pltpu.SemaphoreType.DMA((4 * n,)), pltpu.SemaphoreType.DMA((4 * n,)),
                   *[pltpu.HBM(a.shape, a.dtype) for a in shards], *[pltpu.HBM(a.shape, a.dtype) for a in lands],
                   jax.ShapeDtypeStruct((SUBLANES, 128), F32)),
        in_specs=[HBM_SPEC] * (2 * n) + [ANY_SPEC],
        out_specs=(SEM_SPEC, SEM_SPEC, *([HBM_SPEC] * (2 * n)), VMEM_SPEC),
        input_output_aliases={i: 2 + i for i in range(2 * n)},
        compiler_params=pltpu.CompilerParams(has_side_effects=DATAFLOW),
    )(*[pltpu.with_memory_space_constraint(a, pltpu.HBM) for a in shards],
      *[pltpu.with_memory_space_constraint(a, pltpu.HBM) for a in lands], dep)
    return res[0], res[1], list(res[2:2 + n]), list(res[2 + n:2 + 2 * n]), res[-1]


def _gather_forward(name, send_sems, recv_sems, lands, which, after):
    n = len(which)

    def body(*refs):
        land_refs = refs[:n]
        send_sems, recv_sems = refs[n], refs[n + 1]
        fsend, frecv, token = refs[n + 3], refs[n + 4], refs[-1]
        x, y, c = _mesh_pos()
        chips = [(1 - x, y), (x, 1 - y), (1 - x, 1 - y)]
        for i, a in enumerate(which):
            for j, chip in enumerate(chips):
                block = land_refs[i].at[_dev_index((*chip, c))]
                pltpu.make_async_remote_copy(
                    src_ref=block, dst_ref=block, send_sem=send_sems.at[4 * a + 1 + j], recv_sem=recv_sems.at[4 * a + 1 + j],
                    device_id=(*chip, c), device_id_type=MESH).wait_recv()
                pltpu.make_async_remote_copy(
                    src_ref=block, dst_ref=block, send_sem=fsend.at[3 * i + j], recv_sem=frecv.at[3 * i + j],
                    device_id=(x, y, 1 - c), device_id_type=MESH).start()
        token[...] = jnp.zeros_like(token)

    res = pl.pallas_call(
        body, name=name,
        out_shape=(pltpu.SemaphoreType.DMA((3 * n,)), pltpu.SemaphoreType.DMA((3 * n,)),
                   *[pltpu.HBM(a.shape, a.dtype) for a in lands], jax.ShapeDtypeStruct((SUBLANES, 128), F32)),
        in_specs=[HBM_SPEC] * n + [SEM_SPEC, SEM_SPEC, ANY_SPEC],
        out_specs=(SEM_SPEC, SEM_SPEC, *([HBM_SPEC] * n), VMEM_SPEC),
        input_output_aliases={i: 2 + i for i in range(n)},
        compiler_params=pltpu.CompilerParams(has_side_effects=DATAFLOW),
    )(*lands, send_sems, recv_sems, after)
    return res[0], res[1], list(res[2:2 + n]), res[-1]


def _gather_wait(name, send_sems, recv_sems, fsend, frecv, srcs, lands, which, after):
    n = len(which)

    def body(*refs):
        land_refs = refs[n:2 * n]
        send_sems, recv_sems, fsend, frecv = refs[2 * n:2 * n + 4]
        x, y, c = _mesh_pos()
        sib = (x, y, 1 - c)
        chips = [(1 - x, y), (x, 1 - y), (1 - x, 1 - y)]
        for i, a in enumerate(which):
            def cp(slot, ssem, rsem):
                block = land_refs[i].at[_dev_index(slot)]
                return pltpu.make_async_remote_copy(src_ref=block, dst_ref=block, send_sem=ssem, recv_sem=rsem,
                                                    device_id=sib, device_id_type=MESH)
            cp(sib, send_sems.at[4 * a], recv_sems.at[4 * a]).wait_recv()
            for j, chip in enumerate(chips):
                cp((*chip, 1 - c), fsend.at[3 * i + j], frecv.at[3 * i + j]).wait_recv()
            for k in range(4):
                cp(sib, send_sems.at[4 * a + k], recv_sems.at[4 * a + k]).wait_send()
            for j in range(3):
                cp(sib, fsend.at[3 * i + j], frecv.at[3 * i + j]).wait_send()

    res = pl.pallas_call(
        body, name=name,
        out_shape=[pltpu.HBM(a.shape, a.dtype) for a in srcs + lands],
        in_specs=[HBM_SPEC] * (2 * n) + [SEM_SPEC] * 4 + [ANY_SPEC],
        out_specs=[HBM_SPEC] * (2 * n),
        input_output_aliases={i: i for i in range(2 * n)},
        compiler_params=pltpu.CompilerParams(has_side_effects=DATAFLOW),
    )(*srcs, *lands, send_sems, recv_sems, fsend, frecv, after)
    return list(res[n:])


def _local_step(x, mem, target, rel_bias, g_mix, w_in_g, w_sc, g_a, g_c, g_xattn, g_mem, g_ffn, w_fc, b_fc, g_final,
                dep, forward_weights, late_weights, emit, emit_small):
    s = x.shape[0]
    buckets = _bucket_tables()
    bias = _bias_fwd(rel_bias, buckets)

    h1, qs, ks, vs, gb, gc, xi = _rms_proj(x, g_mix, w_in_g, dep)
    qs, ks, vs = ([a[0][None]] + list(a[1:]) for a in (qs, ks, vs))
    group1, group2 = ["w_out", "w_xq", "w_xk", "w_xv", "w_xo"], ["w_up", "w_down"]
    tok = forward_weights(group1, h1)
    branches = []
    for p, dil in enumerate(DILATIONS):
        o_p, lse_p = _swa_fwd(qs[p], ks[p], vs[p], bias[p], dil, tok)
        branches.append([o_p[0], lse_p[0]] if dil == 1 else [o_p, lse_p])
    lw = late_weights(group1, branches[-1][0])
    w_out, w_xq, w_xk, w_xv, w_xo = (lw[n] for n in group1)
    attn, lses, mixed, x1 = _mix_out(branches, gb, gc, xi, x, w_sc, g_a, g_c, w_out)
    tok = forward_weights(group2, x1)
    mem_n, mk, mv = _mem_kv(mem, g_mem, w_xk, w_xv)
    h2, xq, xo, x2 = _xattn_fwd(x1, g_xattn, w_xq, mk, mv, w_xo, tok)
    lw = late_weights(group2, x2)
    w_up_g, w_down_g = lw["w_up"], lw["w_down"]
    h3, conv, act, dx3, loss_acc, dg_final = _ffn_fwd(x2, g_ffn, w_up_g, w_fc, b_fc, w_down_g, g_final, target)

    gw_down = _dw(act, dx3, dep, "dw_down", a_chunked=True)
    dup, dx2, dg_ffn, dw_fc, db_fc = _ffn_bwd(dx3, h3, conv, x2, g_ffn, w_up_g, w_fc, w_down_g)
    gw_up = _dw(dup, h3, dep, "dw_up", a_chunked=True)
    tok = emit(dict(w_down=gw_down, w_up=gw_up))
    dxq, dx1, dmk, dmv, dg_xattn = _xattn_bwd(dx2, xo, xq, mk, mv, w_xo, w_xq, x1, g_xattn, tok)
    gw_xo = _dw(xo, dx2, tok, "dw_xo")[0]
    gw_xq = _dw(h2, dxq, tok, "dw_xq")[0]
    gw_xk, gw_xv, dg_mem = _mem_kv_bwd(dmk, dmv, mem_n, mem, w_xk, w_xv)
    tok = emit(dict(w_xo=gw_xo, w_xq=gw_xq, w_xk=gw_xk, w_xv=gw_xv))
    dattns, dds, dgb, dcv, dg_a, dg_c, dw_sc = _mix_out_bwd(dx1, w_out, attn, gb, gc, xi, w_sc, g_a, g_c, tok)
    first = lambda a: [a[0][None]] + list(a[1:])
    dattns, dds, lses = first(dattns), first(dds), first(lses)
    gw_out = _dw(mixed, dx1, tok, "dw_out")[0]
    tok = emit(dict(w_out=gw_out))
    dqs, dks, dvs, dbias = [], [], [], []
    for p, dil in enumerate(DILATIONS):
        dq_p, dk_p, dv_p, db_p = _swa_bwd(qs[p], ks[p], vs[p], dattns[p], lses[p], dds[p], bias[p], dil, tok)
        dqs.append(dq_p[0] if dil == 1 else dq_p)
        dks.append(dk_p[0] if dil == 1 else dk_p)
        dvs.append(dv_p[0] if dil == 1 else dv_p)
        dbias.append(db_p)
    d_relb = _bias_bwd(jnp.stack(dbias), buckets)
    dproj, grad_x, dg_mix = _in_proj_bwd(dqs, dks, dvs, dgb, dcv, gc, xi, w_sc, w_in_g, x, g_mix, dx1)
    pad = lambda a: jnp.pad(a, ((0, 0), (0, D_MODEL - a.shape[1])))
    small = jnp.concatenate([
        d_relb, dg_mix, dg_xattn, dg_mem, dg_ffn, dg_final, jnp.concatenate([dg_a, dg_c], axis=1),
        pad(dw_sc), pad(db_fc), pad(dw_fc.reshape(3 * N_DEV, UP_CHUNK)), pad(loss_acc)], axis=0)
    tok = emit_small(small)
    gw_in = _dw(h1, dproj, tok, "dw_in", n_chunks=N_DEV, chunk_cols=IN_CHUNK)
    emit(dict(w_in=gw_in))
    return grad_x


def kernel(x, mem, rel_bias, g_mix, w_in, w_short_conv, g_attn_out, g_conv_out, w_out, g_xattn, g_mem, w_xq, w_xk, w_xv, w_xo, g_ffn, w_up, w_ffn_conv, b_ffn_conv, w_down, g_final, loss_target, m_rel_bias, m_g_mix, m_w_in, m_w_short_conv, m_g_attn_out, m_g_conv_out, m_w_out, m_g_xattn, m_g_mem, m_w_xq, m_w_xk, m_w_xv, m_w_xo, m_g_ffn, m_w_up, m_w_ffn_conv, m_b_ffn_conv, m_w_down, m_g_final, v_rel_bias, v_g_mix, v_w_in, v_w_short_conv, v_g_attn_out, v_g_conv_out, v_w_out, v_g_xattn, v_g_mem, v_w_xq, v_w_xk, v_w_xv, v_w_xo, v_g_ffn, v_w_up, v_w_ffn_conv, v_b_ffn_conv, v_w_down, v_g_final):
    me = _dev_index(_mesh_pos())
    me_arr = me.reshape(1).astype(jnp.int32)

    big_names = ["w_in", "w_out", "w_xq", "w_xk", "w_xv", "w_xo", "w_up", "w_down"]
    late_names = big_names[1:]
    big_w = dict(w_in=w_in[0], w_out=w_out[0], w_xq=w_xq[0], w_xk=w_xk[0], w_xv=w_xv[0], w_xo=w_xo[0],
                 w_up=w_up[0].T, w_down=w_down[0])
    big_m = dict(w_in=m_w_in[0], w_out=m_w_out[0], w_xq=m_w_xq[0], w_xk=m_w_xk[0], w_xv=m_w_xv[0], w_xo=m_w_xo[0],
                 w_up=m_w_up[0].T, w_down=m_w_down[0])
    big_v = dict(w_in=v_w_in[0], w_out=v_w_out[0], w_xq=v_w_xq[0], w_xk=v_w_xk[0], w_xv=v_w_xv[0], w_xo=v_w_xo[0],
                 w_up=v_w_up[0].T, w_down=v_w_down[0])
    shard_shape = {n: big_w[n].shape for n in big_names}

    w_in_g, w_sc_g, w_fc_full = _all_gather([big_w["w_in"].astype(BF16), w_short_conv[0], w_ffn_conv[0]])
    w_sc_full = w_sc_g.transpose(1, 0, 2).reshape(3, CONV_W)
    late_shards = [big_w[n].astype(BF16) for n in late_names]
    ag_send, ag_recv, ag_srcs, ag_lands, ag_token = _gather_start("gather_weights_start", late_shards, w_in_g)
    forwarded = {}

    def forward_weights(names, after):
        which = [late_names.index(n) for n in names]
        fsend, frecv, lands, token = _gather_forward("gather_" + "_".join(names) + "_forward", ag_send, ag_recv,
                                                     [ag_lands[a] for a in which], which, after)
        forwarded[tuple(names)] = (fsend, frecv, lands)
        return token

    def late_weights(names, after):
        which = [late_names.index(n) for n in names]
        fsend, frecv, lands = forwarded[tuple(names)]
        lands = _gather_wait("gather_" + "_".join(names) + "_wait", ag_send, ag_recv, fsend, frecv,
                             [ag_srcs[a] for a in which], lands, which, after)
        out = {}
        for n, a, land in zip(names, which, lands):
            full = lax.dynamic_update_index_in_dim(land, late_shards[a], me, 0)
            if n == "w_up":
                out[n] = full
            elif n == "w_down":
                out[n] = full.reshape(N_DEV // 2, UP_CHUNK, D_MODEL)
            else:
                out[n] = full.reshape(D_MODEL, D_MODEL)
        return out

    sent = []

    def emit(grads):
        names = list(grads)
        blocks = [grads[n].reshape((N_DEV,) + shard_shape[n]) for n in names]
        started = _exchange_start("scatter_" + "_".join(names) + "_start", blocks, [False] * len(names), me_arr)
        sent.append((names, started))
        return started[-1]

    def emit_small(small):
        sent_small.append((small, _exchange_start("gather_small_start", [small], [True], me_arr)))
        return sent_small[0][1][-1]

    sent_small = []
    grad_x = _local_step(
        x[0], mem[0], loss_target[0], rel_bias, g_mix, w_in_g, w_sc_full, g_attn_out, g_conv_out, g_xattn, g_mem,
        g_ffn, w_fc_full, b_ffn_conv.reshape(N_DEV, 1, UP_CHUNK), g_final.reshape(1, D_MODEL), ag_token,
        forward_weights, late_weights, emit, emit_small)

    small_g, small_started = sent_small[0]
    after = sent[-1][1][-1]
    small_parts = _exchange_wait("gather_small_wait", small_started, [True], after)[1][0]
    big_out = {}
    after = small_parts
    for names, started in sent:
        blocks, lands = _exchange_wait("scatter_" + "_".join(names) + "_wait", started, [False] * len(names), after)
        for n, block, land in zip(names, blocks, lands):
            res = _adamw_big("adamw_" + n, big_w[n], block, land, big_m[n], big_v[n], me_arr)
            big_out[n] = [(r.T if n == "w_up" else r)[None] for r in res]
            after = res[0]

    as_rows = lambda a: a.reshape(N_DEV, UP_CHUNK)
    row1 = lambda a: a.reshape(1, D_MODEL)
    small_names = ["rel_bias", "g_mix", "g_attn_out", "g_conv_out", "g_xattn", "g_mem", "g_ffn", "b_ffn_conv", "g_final"]
    wmv = [
        (rel_bias, m_rel_bias, v_rel_bias), (g_mix, m_g_mix, v_g_mix), (g_attn_out, m_g_attn_out, v_g_attn_out),
        (g_conv_out, m_g_conv_out, v_g_conv_out), (g_xattn, m_g_xattn, v_g_xattn), (g_mem, m_g_mem, v_g_mem),
        (g_ffn, m_g_ffn, v_g_ffn), (as_rows(b_ffn_conv), as_rows(m_b_ffn_conv), as_rows(v_b_ffn_conv)),
        (row1(g_final), row1(m_g_final), row1(v_g_final))]
    g_packed, small_res = _adamw_small(small_g, small_parts, wmv, me_arr)
    small_out = dict(zip(small_names, small_res))
    loss = g_packed[ROW_LOSS, 0]
    small_out["b_ffn_conv"] = [a.reshape(1, 2 * D_FF) for a in small_out["b_ffn_conv"]]
    small_out["g_final"] = [a.reshape(D_MODEL) for a in small_out["g_final"]]

    g_wsc = lax.dynamic_slice(g_packed[ROW_WSC:ROW_WSC + 3, 0:CONV_W], (0, me * HEAD_DIM), (3, HEAD_DIM))
    g_wfc = lax.dynamic_slice(g_packed[ROW_WFC:ROW_WFC + 3 * N_DEV, 0:UP_CHUNK].reshape(3, N_DEV, UP_CHUNK),
                              (0, me, 0), (3, 1, UP_CHUNK)).reshape(3, UP_CHUNK)
    shard_res = _adamw_shards([(w_short_conv[0], g_wsc, m_w_short_conv[0], v_w_short_conv[0]),
                               (w_ffn_conv[0], g_wfc, m_w_ffn_conv[0], v_w_ffn_conv[0])])
    small_out["w_short_conv"] = [g_wsc[None]] + [a[None] for a in shard_res[0]]
    small_out["w_ffn_conv"] = [g_wfc[None]] + [a[None] for a in shard_res[1]]

    order = ["rel_bias", "g_mix", "w_in", "w_short_conv", "g_attn_out", "g_conv_out", "w_out", "g_xattn", "g_mem",
             "w_xq", "w_xk", "w_xv", "w_xo", "g_ffn", "w_up", "w_ffn_conv", "b_ffn_conv", "w_down", "g_final"]
    allp = {**big_out, **small_out}
    outs = [loss, grad_x[None]]
    for kind in range(4):
        outs += [allp[n][kind] for n in order]
    return tuple(outs)
```

```python
import math

import numpy as np
import jax
import jax.numpy as jnp
from jax import lax
from jax.experimental import pallas as pl
from jax.experimental.pallas import tpu as pltpu

F32 = jnp.float32
BF16 = jnp.bfloat16
MESH = pl.DeviceIdType.MESH

N_DEV = 8
D_MODEL = 1024
ATTN_W = 512
CONV_W = 512
N_HEADS = 8
HEAD_DIM = 64
WIN = 128
DILATIONS = (1, 4, 16)
N_BUCKETS = 32
BUCKET_MAX_EXACT = 16
BUCKET_MAX_DISTANCE = 2048
N_MEM_HEADS = 4
MEM_HEAD_DIM = 256
D_FF = 2816
IN_COLS = 3072
IN_CHUNK = IN_COLS // N_DEV
UP_CHUNK = 2 * D_FF // N_DEV
EPS = 1e-6

ADAM_LR = 0.001
ADAM_B1 = 0.9
ADAM_B2 = 0.999
ADAM_EPS = 1e-08
ADAM_WD = 0.01
ADAM_STEP = 10

SUBLANES = 8
LANES = 128
HALO = 16
TM = 512
TM_FFN = 256
TS_DW = 4096
SWA_BLOCKS = 8
VMEM_LIMIT = 56 * 1024 * 1024

ROW_RELB, ROW_GMIX, ROW_GXATTN, ROW_GMEM, ROW_GFFN, ROW_GFINAL, ROW_GAC = 0, 8, 16, 24, 32, 40, 48
ROW_WSC, ROW_BFC, ROW_WFC, ROW_LOSS, SMALL_ROWS = 56, 64, 72, 96, 104


def _cparams(n_grid):
    return pltpu.CompilerParams(dimension_semantics=("arbitrary",) * n_grid, vmem_limit_bytes=VMEM_LIMIT)


def _full(shape):
    nd = len(shape)
    return pl.BlockSpec(tuple(shape), lambda *_: (0,) * nd)


def _resident(shape):
    nd = len(shape)
    return pl.BlockSpec(tuple(shape), lambda *_: (0,) * nd, pipeline_mode=pl.Buffered(1))


ANY_SPEC = pl.BlockSpec(memory_space=pl.ANY)
HBM_SPEC = pl.BlockSpec(memory_space=pltpu.HBM)
SEM_SPEC = pl.BlockSpec(memory_space=pltpu.SEMAPHORE)
VMEM_SPEC = pl.BlockSpec(memory_space=pltpu.VMEM)
SMEM_SPEC = pl.BlockSpec(memory_space=pltpu.SMEM)
DATAFLOW = pltpu.SideEffectType.DATAFLOW_SIDE_EFFECTING


def _rms(x):
    r = lax.rsqrt(jnp.mean(x * x, axis=-1, keepdims=True) + EPS)
    return x * r, r


def _rms_bwd(xh, r, g, dy):
    dxh = dy * g
    return r * (dxh - xh * jnp.mean(dxh * xh, axis=-1, keepdims=True))


def _shift_down(u, halo, k):
    ru = pltpu.roll(u, k, 0)
    rh = pltpu.roll(halo, k, 0)
    row = lax.broadcasted_iota(jnp.int32, rh.shape, 0)
    head = jnp.where(row < k, rh, ru[0:SUBLANES])
    return jnp.concatenate([head, ru[SUBLANES:]], axis=0)


def _shift_up(u, halo, k):
    tm = u.shape[0]
    ru = pltpu.roll(u, tm - k, 0)
    rh = pltpu.roll(halo, SUBLANES - k, 0)
    row = lax.broadcasted_iota(jnp.int32, rh.shape, 0)
    tail = jnp.where(row >= SUBLANES - k, rh, ru[tm - SUBLANES:])
    return jnp.concatenate([ru[:tm - SUBLANES], tail], axis=0)


def _causal_conv3(u, halo, w_ref):
    return (_shift_down(u, halo, 2) * w_ref[0:1, :] + _shift_down(u, halo, 1) * w_ref[1:2, :]) + u * w_ref[2:3, :]


def _dot(a, b):
    return jnp.dot(a, b, preferred_element_type=F32)


def _dot_nt(a, b):
    return lax.dot_general(a, b, (((1,), (1,)), ((), ())), preferred_element_type=F32)


def _dot_tn(a, b):
    return lax.dot_general(a, b, (((0,), (0,)), ((), ())), preferred_element_type=F32)


def _sigmoid(x):
    return 0.5 * jnp.tanh(0.5 * x) + 0.5


def _bucket_tables():
    qi = np.arange(WIN)[:, None]
    kj = np.arange(2 * WIN)[None, :]
    steps = np.clip(qi + WIN - kj, 0, WIN)
    out = []
    for d in DILATIONS:
        dist = steps * d
        dd = np.maximum(dist, 1).astype(np.float32)
        large = BUCKET_MAX_EXACT + (
            np.log(dd / np.float32(BUCKET_MAX_EXACT)) / np.float32(math.log(BUCKET_MAX_DISTANCE / BUCKET_MAX_EXACT))
            * np.float32(N_BUCKETS - BUCKET_MAX_EXACT)).astype(np.int32)
        large = np.minimum(large, N_BUCKETS - 1)
        out.append(np.where(dist < BUCKET_MAX_EXACT, dist, large).astype(np.int32))
    return np.stack(out)


def _bias_fwd(rel_bias, buckets):
    present = [sorted(set(buckets[p].ravel().tolist())) for p in range(3)]

    def body(rb_ref, bk_ref, o_ref):
        for p in range(3):
            bk = bk_ref[p]
            for h in range(N_HEADS):
                acc = jnp.zeros((WIN, 2 * WIN), F32)
                for b in present[p]:
                    acc = jnp.where(bk == b, rb_ref[h, b], acc)
                o_ref[p, h] = acc

    return pl.pallas_call(
        body, name="bias_fwd",
        out_shape=jax.ShapeDtypeStruct((3, N_HEADS, WIN, 2 * WIN), F32),
        in_specs=[pl.BlockSpec(memory_space=pltpu.SMEM), pl.BlockSpec(memory_space=pltpu.VMEM)],
        out_specs=pl.BlockSpec(memory_space=pltpu.VMEM),
    )(rel_bias, jnp.asarray(buckets))


def _bias_bwd(dbias, buckets):
    present = [set(buckets[p].ravel().tolist()) for p in range(3)]

    def body(db_ref, bk_ref, o_ref):
        lane = lax.broadcasted_iota(jnp.int32, (1, D_MODEL), 1)
        rows = []
        for h in range(N_HEADS):
            row = jnp.zeros((1, D_MODEL), F32)
            for b in range(N_BUCKETS):
                tot = jnp.zeros((1, 1), F32)
                for p in (p for p in range(3) if b in present[p]):
                    sel = jnp.where(bk_ref[p] == b, db_ref[p, h], 0.0)
                    tot = tot + jnp.sum(jnp.sum(sel, axis=0, keepdims=True), axis=1, keepdims=True)
                row = jnp.where(lane == b, tot, row)
            rows.append(row)
        o_ref[...] = jnp.concatenate(rows, axis=0)

    return pl.pallas_call(
        body, name="bias_bwd",
        out_shape=jax.ShapeDtypeStruct((N_HEADS, D_MODEL), F32),
        in_specs=[pl.BlockSpec(memory_space=pltpu.VMEM), pl.BlockSpec(memory_space=pltpu.VMEM)],
        out_specs=pl.BlockSpec(memory_space=pltpu.VMEM),
    )(dbias, jnp.asarray(buckets))


def _spread(val, scr_ref, out_refs, dtype):
    out_refs[0][...] = val.astype(dtype)
    n_blk = val.shape[1] // LANES
    for c in range(n_blk):
        scr_ref[c] = val[:, c * LANES:(c + 1) * LANES]
    for o_ref, d in zip(out_refs[1:], DILATIONS[1:]):
        for r in range(d):
            for c in range(n_blk):
                o_ref[r, :, c * LANES:(c + 1) * LANES] = scr_ref.at[c][pl.ds(r, TM // d, stride=d), :].astype(dtype)


def _gather_classes(blk_ref, scr_ref, d):
    n_blk = blk_ref.shape[2] // LANES
    for r in range(d):
        for c in range(n_blk):
            scr_ref.at[c][pl.ds(r, TM // d, stride=d), :] = blk_ref[r, :, c * LANES:(c + 1) * LANES].astype(F32)
    return jnp.concatenate([scr_ref[c] for c in range(n_blk)], axis=1)


def _class_specs(cols):
    return [pl.BlockSpec((TM, cols), lambda i: (i, 0))] + [
        pl.BlockSpec((d, TM // d, cols), lambda i: (0, i, 0)) for d in DILATIONS[1:]]


def _class_shapes(s, cols, dtype):
    return [jax.ShapeDtypeStruct((s, cols), dtype)] + [
        jax.ShapeDtypeStruct((d, s // d, cols), dtype) for d in DILATIONS[1:]]


def _rms_proj(x, g_mix, w_in_g, dep):
    s = x.shape[0]

    def body(x_ref, g_ref, w_ref, dep_ref, h_ref, q1, q4, q16, k1, k4, k16, v1, v4, v16, gb_ref, gc_ref, xi_ref, scr):
        xh, _ = _rms(x_ref[...])
        h = (xh * g_ref[...]).astype(BF16)
        h_ref[...] = h
        proj = jnp.concatenate([_dot(h, w_ref[j]) for j in range(N_DEV)], axis=1)
        _spread(proj[:, 0:512] * (HEAD_DIM ** -0.5), scr, (q1, q4, q16), BF16)
        _spread(proj[:, 512:1024], scr, (k1, k4, k16), BF16)
        _spread(proj[:, 1024:1536], scr, (v1, v4, v16), BF16)
        gb_ref[...] = proj[:, 1536:2048]
        gc_ref[...] = proj[:, 2048:2560]
        xi_ref[...] = proj[:, 2560:3072]

    row = lambda n: pl.BlockSpec((TM, n), lambda i: (i, 0))
    res = pl.pallas_call(
        body, name="rms_proj", grid=(s // TM,),
        out_shape=[jax.ShapeDtypeStruct((s, D_MODEL), BF16)] + _class_shapes(s, 512, BF16) * 3
        + [jax.ShapeDtypeStruct((s, 512), F32)] * 3,
        in_specs=[row(D_MODEL), _full(g_mix.shape), _full(w_in_g.shape), ANY_SPEC],
        out_specs=[row(D_MODEL)] + _class_specs(512) * 3 + [row(512)] * 3,
        scratch_shapes=[pltpu.VMEM((512 // LANES, TM, LANES), F32)],
        compiler_params=_cparams(1),
    )(x, g_mix, w_in_g, dep)
    return res[0], res[1:4], res[4:7], res[7:10], res[10], res[11], res[12]


def _pair_split(x2):
    lane = lax.broadcasted_iota(jnp.int32, x2.shape, 1)
    zero = jnp.zeros_like(x2)
    return jnp.where(lane < HEAD_DIM, x2, zero), jnp.where(lane >= HEAD_DIM, x2, zero)


def _pair_join(even, odd):
    lane = lax.broadcasted_iota(jnp.int32, (even.shape[0], LANES), 1)
    return jnp.where(lane < HEAD_DIM, even, odd)


def _band_mask(first):
    qi = lax.broadcasted_iota(jnp.int32, (WIN, 2 * WIN), 0)
    kj = lax.broadcasted_iota(jnp.int32, (WIN, 2 * WIN), 1)
    steps = qi + WIN - kj
    return (steps >= 0) & (steps <= WIN) & (kj >= jnp.where(first, WIN, 0))


def _swa_steps(qc, dil):
    n128 = qc.shape[1] // WIN
    nsub = min(SWA_BLOCKS, n128)
    nb = n128 // nsub
    ncls = min(dil, SWA_BLOCKS // nsub) if nb == 1 else 1
    return nsub, nb, ncls


def _swa_fwd(qc, kc, vc, bias, dil, dep):
    nsub, nb, ncls = _swa_steps(qc, dil)
    whole = nb == 1

    def body(q_ref, kp_ref, kc_ref, vp_ref, vc_ref, b_ref, dep_ref, o_ref, lse_ref, s_scr, p_scr):
        b = pl.program_id(1)
        pairs = [slice(a * LANES, (a + 1) * LANES) for a in range(N_HEADS // 2)]
        for c, t in [(c, t) for c in range(ncls) for t in range(nsub)]:
            i = c * nsub + t
            rows = slice(t * WIN, (t + 1) * WIN)
            alone = whole and t == 0
            cols = slice(WIN, 2 * WIN) if alone else slice(0, 2 * WIN)

            def keys(prev_ref, cur_ref, sl):
                if alone:
                    return cur_ref[c, rows, sl]
                if t == 0:
                    return jnp.concatenate([prev_ref[c, :, sl], cur_ref[c, rows, sl]], axis=0)
                return cur_ref[c, (t - 1) * WIN:(t + 1) * WIN, sl]

            for a, sl in enumerate(pairs):
                k2 = keys(kp_ref, kc_ref, sl)
                for e, qh in enumerate(_pair_split(q_ref[c, rows, sl])):
                    s_scr[i, 2 * a + e, :, cols] = _dot_nt(qh, k2)
            first = (b == 0) if t == 0 else False
            lg = jnp.where(_band_mask(first)[:, cols], s_scr[i, :, :, cols] + b_ref[:, :, cols], -jnp.inf)
            m = jnp.max(lg, axis=-1, keepdims=True)
            p = jnp.exp(lg - m)
            den = jnp.sum(p, axis=-1, keepdims=True)
            p_scr[i, :, :, cols] = p.astype(BF16)
            lse = m + jnp.log(den)
            for a, sl in enumerate(pairs):
                v_even, v_odd = _pair_split(keys(vp_ref, vc_ref, sl))
                o2 = _dot(p_scr[i, 2 * a, :, cols], v_even) + _dot(p_scr[i, 2 * a + 1, :, cols], v_odd)
                o_ref[c, rows, sl] = o2 / _pair_join(den[2 * a], den[2 * a + 1])
                lse_ref[c, rows, sl] = _pair_join(lse[2 * a], lse[2 * a + 1])

    cur = pl.BlockSpec((ncls, nsub * WIN, 512), lambda r, b: (r, b, 0))
    prev = pl.BlockSpec((ncls, WIN, 512), lambda r, b: (r, jnp.maximum(nsub * b - 1, 0), 0))
    wide = (ncls * nsub, N_HEADS, WIN, 2 * WIN)
    return pl.pallas_call(
        body, name=f"swa_fwd_d{dil}", grid=(dil // ncls, nb),
        out_shape=[jax.ShapeDtypeStruct(qc.shape, F32)] * 2,
        in_specs=[cur, prev, cur, prev, cur, _full(bias.shape), ANY_SPEC],
        out_specs=[cur] * 2,
        scratch_shapes=[pltpu.VMEM(wide, F32), pltpu.VMEM(wide, BF16)],
        compiler_params=_cparams(2),
    )(qc, kc, kc, vc, vc, bias, dep)


def _mix_out(branches, gb, gc, xi, x, w_sc, g_a, g_c, w_out):
    s = x.shape[0]
    tb = TM // SUBLANES

    def body(o1, l1, o4, l4, o16, l16, gb_ref, gc_ref, xi_ref, gch_ref, xih_ref, x_ref, wsc_ref,
             ga_ref, gcv_ref, wout_ref, attn_ref, lse1, lse4, lse16, mixed_ref, x1_ref, scr_a, scr_b, scr_c, scr_d):
        i = pl.program_id(0)
        la, lb, lc = l1[...], _gather_classes(l4, scr_a, 4), _gather_classes(l16, scr_b, 16)
        m_all = jnp.maximum(jnp.maximum(la, lb), lc)
        ea, eb, ec = jnp.exp(la - m_all), jnp.exp(lb - m_all), jnp.exp(lc - m_all)
        den = (ea + eb) + ec
        num = (ea * o1[...] + eb * _gather_classes(o4, scr_c, 4)) + ec * _gather_classes(o16, scr_d, 16)
        attn = num / den
        attn_ref[...] = attn
        _spread(m_all + jnp.log(den), scr_a, (lse1, lse4, lse16), F32)
        xa, _ = _rms(attn)
        u = gc_ref[...] * xi_ref[...]
        uh = jnp.where(i > 0, gch_ref[...] * xih_ref[...], 0.0)
        conv = gb_ref[...] * _causal_conv3(u, uh, wsc_ref)
        xc, _ = _rms(conv)
        mixed = jnp.concatenate([xa * ga_ref[...], xc * gcv_ref[...]], axis=1).astype(BF16)
        mixed_ref[...] = mixed
        x1_ref[...] = x_ref[...] + _dot(mixed, wout_ref[...])

    row = lambda n: pl.BlockSpec((TM, n), lambda i: (i, 0))
    halo = pl.BlockSpec((SUBLANES, 512), lambda i: (jnp.maximum(i * tb - 1, 0), 0))
    cs = _class_specs(512)
    flat = [a for br in branches for a in br]
    res = pl.pallas_call(
        body, name="mix_out", grid=(s // TM,),
        out_shape=[jax.ShapeDtypeStruct((s, 512), F32)] + _class_shapes(s, 512, F32)
        + [jax.ShapeDtypeStruct((s, D_MODEL), BF16), jax.ShapeDtypeStruct((s, D_MODEL), F32)],
        in_specs=[cs[0], cs[0], cs[1], cs[1], cs[2], cs[2], row(512), row(512), row(512), halo, halo,
                  row(D_MODEL), _full(w_sc.shape), _full(g_a.shape), _full(g_c.shape), _full(w_out.shape)],
        out_specs=[row(512)] + cs + [row(D_MODEL), row(D_MODEL)],
        scratch_shapes=[pltpu.VMEM((512 // LANES, TM, LANES), F32)] * 4,
        compiler_params=_cparams(1),
    )(*flat, gb, gc, xi, gc, xi, x, w_sc, g_a, g_c, w_out)
    return res[0], res[1:4], res[4], res[5]


def _mem_kv(mem, g_mem, w_xk, w_xv):
    def body(mem_ref, g_ref, wk_ref, wv_ref, mn_ref, k_ref, v_ref):
        xh, _ = _rms(mem_ref[...])
        mn = (xh * g_ref[...]).astype(BF16)
        mn_ref[...] = mn
        k_ref[...] = _dot(mn, wk_ref[...]).astype(BF16)
        v_ref[...] = _dot(mn, wv_ref[...]).astype(BF16)

    vm = pl.BlockSpec(memory_space=pltpu.VMEM)
    return pl.pallas_call(
        body, name="mem_kv",
        out_shape=[jax.ShapeDtypeStruct(mem.shape, BF16)] * 3,
        in_specs=[vm] * 4, out_specs=[vm] * 3,
        compiler_params=pltpu.CompilerParams(vmem_limit_bytes=VMEM_LIMIT),
    )(mem, g_mem, w_xk, w_xv)


def _xattn_fwd(x1, g, w_xq, k, v, w_xo, dep):
    s = x1.shape[0]

    def body(x1_ref, g_ref, wq_ref, k_ref, v_ref, wo_ref, dep_ref, h2_ref, q_ref, o_ref, x2_ref):
        x1v = x1_ref[...]
        xh, _ = _rms(x1v)
        h2 = (xh * g_ref[...]).astype(BF16)
        h2_ref[...] = h2
        qb = _dot(h2, wq_ref[...]).astype(BF16)
        q_ref[...] = qb
        outs = []
        for h in range(N_MEM_HEADS):
            sl = slice(h * MEM_HEAD_DIM, (h + 1) * MEM_HEAD_DIM)
            lg = _dot_nt(qb[:, sl], k_ref[:, sl]) * (MEM_HEAD_DIM ** -0.5)
            p = jnp.exp(lg - jnp.max(lg, axis=-1, keepdims=True))
            p = p / jnp.sum(p, axis=-1, keepdims=True)
            outs.append(_dot(p.astype(BF16), v_ref[:, sl]))
        o = jnp.concatenate(outs, axis=1).astype(BF16)
        o_ref[...] = o
        x2_ref[...] = x1v + _dot(o, wo_ref[...])

    row = pl.BlockSpec((TM, D_MODEL), lambda i: (i, 0))
    return pl.pallas_call(
        body, name="xattn_fwd", grid=(s // TM,),
        out_shape=[jax.ShapeDtypeStruct((s, D_MODEL), BF16)] * 3 + [jax.ShapeDtypeStruct((s, D_MODEL), F32)],
        in_specs=[row, _full(g.shape), _full(w_xq.shape), _full(k.shape), _full(v.shape), _full(w_xo.shape), ANY_SPEC],
        out_specs=[row] * 4,
        compiler_params=_cparams(1),
    )(x1, g, w_xq, k, v, w_xo, dep)


def _ffn_conv(h_ext, wup_ref, wfc_ref, bfc_ref, j):
    u = _dot_nt(h_ext, wup_ref[j])
    w = wfc_ref[j]
    c = ((pltpu.roll(u, 2, 0) * w[0:1, :] + pltpu.roll(u, 1, 0) * w[1:2, :]) + u * w[2:3, :]) + bfc_ref[j]
    return c[HALO:]


def _ffn_fwd(x2, g, w_up_g, w_fc, b_fc, w_down_g, g_final, target):
    s = x2.shape[0]
    tb = TM_FFN // HALO
    half = N_DEV // 2

    def body(x_ref, xp_ref, g_ref, wup_ref, wfc_ref, bfc_ref, wd_ref, gf_ref, t_ref, h_ref, c_ref, act_ref, dx3_ref,
             loss_ref, dgf_ref):
        i = pl.program_id(0)

        @pl.when(i == 0)
        def _():
            loss_ref[...] = jnp.zeros_like(loss_ref)
            dgf_ref[...] = jnp.zeros_like(dgf_ref)

        x2v = x_ref[...]
        gv = g_ref[...]
        h = (_rms(x2v)[0] * gv).astype(BF16)
        h_ref[...] = h
        hp = jnp.where(i > 0, _rms(xp_ref[...])[0] * gv, 0.0).astype(BF16)
        h_ext = jnp.concatenate([hp, h], axis=0)
        down = jnp.zeros((TM_FFN, D_MODEL), F32)
        for j in range(half):
            cg = _ffn_conv(h_ext, wup_ref, wfc_ref, bfc_ref, j)
            cv = _ffn_conv(h_ext, wup_ref, wfc_ref, bfc_ref, j + half)
            c_ref[j] = cg
            c_ref[j + half] = cv
            a = ((cg * _sigmoid(cg)) * cv).astype(BF16)
            act_ref[j] = a
            down = down + _dot(a, wd_ref[j])
        x3 = x2v + down
        xh, r = _rms(x3)
        gf = gf_ref[...]
        e = xh * gf - t_ref[...]
        loss_ref[...] += 0.5 * jnp.sum(jnp.sum(e * e, axis=1, keepdims=True), axis=0, keepdims=True) / D_MODEL
        dy = e * (1.0 / D_MODEL)
        dgf_ref[0:1, :] += jnp.sum(dy * xh, axis=0, keepdims=True)
        dx3_ref[...] = _rms_bwd(xh, r, gf, dy)

    row = pl.BlockSpec((TM_FFN, D_MODEL), lambda i: (i, 0))
    prev = pl.BlockSpec((HALO, D_MODEL), lambda i: (jnp.maximum(i * tb - 1, 0), 0))
    return pl.pallas_call(
        body, name="ffn_fwd", grid=(s // TM_FFN,),
        out_shape=[jax.ShapeDtypeStruct((s, D_MODEL), BF16), jax.ShapeDtypeStruct((N_DEV, s, UP_CHUNK), F32),
                   jax.ShapeDtypeStruct((half, s, UP_CHUNK), BF16),
                   jax.ShapeDtypeStruct((s, D_MODEL), F32), jax.ShapeDtypeStruct((SUBLANES, 128), F32),
                   jax.ShapeDtypeStruct((SUBLANES, D_MODEL), F32)],
        in_specs=[row, prev, _full(g.shape), _resident(w_up_g.shape), _full(w_fc.shape), _full(b_fc.shape),
                  _resident(w_down_g.shape), _full(g_final.shape), row],
        out_specs=[row, pl.BlockSpec((N_DEV, TM_FFN, UP_CHUNK), lambda i: (0, i, 0)),
                   pl.BlockSpec((half, TM_FFN, UP_CHUNK), lambda i: (0, i, 0)), row,
                   _full((SUBLANES, 128)), _full((SUBLANES, D_MODEL))],
        compiler_params=_cparams(1),
    )(x2, x2, g, w_up_g, w_fc, b_fc, w_down_g, g_final, target)


def _ffn_bwd(dx3, h3, conv, x2, g, w_up_g, w_fc, w_down_g):
    s = x2.shape[0]
    tb = TM_FFN // HALO
    last = s // HALO - 1
    n_tiles = s // TM_FFN
    half = N_DEV // 2
    n_ext = TM_FFN + HALO

    def body(dx_ref, dxn_ref, h_ref, c_ref, cn_ref, x2_ref, g_ref, wup_ref, wfc_ref, wd_ref,
             dup_ref, dx2_ref, dg_ref, dwfc_ref, dbfc_ref):
        i = pl.program_id(0)

        @pl.when(i == 0)
        def _():
            dg_ref[...] = jnp.zeros_like(dg_ref)
            dwfc_ref[...] = jnp.zeros_like(dwfc_ref)
            dbfc_ref[...] = jnp.zeros_like(dbfc_ref)

        dxv = dx_ref[...]
        dxn = jnp.where(i < n_tiles - 1, dxn_ref[...], 0.0)
        dx_ext = jnp.concatenate([dxv, dxn], axis=0).astype(BF16)
        h = h_ref[...]
        dh = jnp.zeros((TM_FFN, D_MODEL), F32)
        for j in range(half):
            cg = jnp.concatenate([c_ref[j], cn_ref[j]], axis=0)
            cv = jnp.concatenate([c_ref[j + half], cn_ref[j + half]], axis=0)
            dact = _dot_nt(dx_ext, wd_ref[j])
            sg = _sigmoid(cg)
            parts = ((j + half, dact * (cg * sg)), (j, (dact * cv) * (sg * (1.0 + cg * (1.0 - sg)))))
            for jj, dc in parts:
                u = _dot_nt(h, wup_ref[jj])
                dc0, dc1, dc2 = dc[:TM_FFN], pltpu.roll(dc, n_ext - 1, 0)[:TM_FFN], pltpu.roll(dc, n_ext - 2, 0)[:TM_FFN]
                dbfc_ref[jj:jj + 1, :] += jnp.sum(dc0, axis=0, keepdims=True)
                dwfc_ref[0, jj:jj + 1, :] += jnp.sum(dc2 * u, axis=0, keepdims=True)
                dwfc_ref[1, jj:jj + 1, :] += jnp.sum(dc1 * u, axis=0, keepdims=True)
                dwfc_ref[2, jj:jj + 1, :] += jnp.sum(dc0 * u, axis=0, keepdims=True)
                w = wfc_ref[jj]
                du = ((dc0 * w[2:3, :] + dc1 * w[1:2, :]) + dc2 * w[0:1, :]).astype(BF16)
                dup_ref[jj] = du
                dh = dh + _dot(du, wup_ref[jj])
        xh, r = _rms(x2_ref[...])
        dg_ref[0:1, :] += jnp.sum(dh * xh, axis=0, keepdims=True)
        dx2_ref[...] = dxv + _rms_bwd(xh, r, g_ref[...], dh)

    row = pl.BlockSpec((TM_FFN, D_MODEL), lambda i: (i, 0))
    nxt = pl.BlockSpec((HALO, D_MODEL), lambda i: (jnp.minimum((i + 1) * tb, last), 0))
    cur_c = pl.BlockSpec((N_DEV, TM_FFN, UP_CHUNK), lambda i: (0, i, 0))
    nxt_c = pl.BlockSpec((N_DEV, HALO, UP_CHUNK), lambda i: (0, jnp.minimum((i + 1) * tb, last), 0))
    return pl.pallas_call(
        body, name="ffn_bwd", grid=(n_tiles,),
        out_shape=[jax.ShapeDtypeStruct((N_DEV, s, UP_CHUNK), BF16), jax.ShapeDtypeStruct((s, D_MODEL), F32),
                   jax.ShapeDtypeStruct((SUBLANES, D_MODEL), F32), jax.ShapeDtypeStruct((3, N_DEV, UP_CHUNK), F32),
                   jax.ShapeDtypeStruct((N_DEV, UP_CHUNK), F32)],
        in_specs=[row, nxt, row, cur_c, nxt_c, row, _full(g.shape), _resident(w_up_g.shape), _full(w_fc.shape),
                  _resident(w_down_g.shape)],
        out_specs=[cur_c, row, _full((SUBLANES, D_MODEL)), _full((3, N_DEV, UP_CHUNK)), _full((N_DEV, UP_CHUNK))],
        compiler_params=_cparams(1),
    )(dx3, dx3, h3, conv, conv, x2, g, w_up_g, w_fc, w_down_g)


def _xattn_bwd(dx2, o, q, k, v, w_xo, w_xq, x1, g, dep):
    s = x1.shape[0]

    def body(dx2_ref, o_ref, q_ref, k_ref, v_ref, wo_ref, wq_ref, x1_ref, g_ref, dep_ref, dq_ref, dx1_ref, dk_ref,
             dv_ref, dg_ref):
        @pl.when(pl.program_id(0) == 0)
        def _():
            dk_ref[...] = jnp.zeros_like(dk_ref)
            dv_ref[...] = jnp.zeros_like(dv_ref)
            dg_ref[...] = jnp.zeros_like(dg_ref)

        dx2v = dx2_ref[...]
        do = _dot_nt(dx2v.astype(BF16), wo_ref[...])
        dqs = []
        for h in range(N_MEM_HEADS):
            sl = slice(h * MEM_HEAD_DIM, (h + 1) * MEM_HEAD_DIM)
            qh, kh, vh = q_ref[:, sl], k_ref[:, sl], v_ref[:, sl]
            lg = _dot_nt(qh, kh) * (MEM_HEAD_DIM ** -0.5)
            p = jnp.exp(lg - jnp.max(lg, axis=-1, keepdims=True))
            p = p / jnp.sum(p, axis=-1, keepdims=True)
            doh = do[:, sl].astype(BF16)
            dp = _dot_nt(doh, vh)
            ds = (p * (dp - jnp.sum(p * dp, axis=-1, keepdims=True)) * (MEM_HEAD_DIM ** -0.5)).astype(BF16)
            dqs.append(_dot(ds, kh))
            dk_ref[:, sl] += _dot_tn(ds, qh)
            dv_ref[:, sl] += _dot_tn(p.astype(BF16), doh)
        dq = jnp.concatenate(dqs, axis=1).astype(BF16)
        dq_ref[...] = dq
        dh2 = _dot_nt(dq, wq_ref[...])
        xh, r = _rms(x1_ref[...])
        dg_ref[0:1, :] += jnp.sum(dh2 * xh, axis=0, keepdims=True)
        dx1_ref[...] = dx2v + _rms_bwd(xh, r, g_ref[...], dh2)

    row = pl.BlockSpec((TM, D_MODEL), lambda i: (i, 0))
    return pl.pallas_call(
        body, name="xattn_bwd", grid=(s // TM,),
        out_shape=[jax.ShapeDtypeStruct((s, D_MODEL), BF16), jax.ShapeDtypeStruct((s, D_MODEL), F32),
                   jax.ShapeDtypeStruct(k.shape, F32), jax.ShapeDtypeStruct(k.shape, F32),
                   jax.ShapeDtypeStruct((SUBLANES, D_MODEL), F32)],
        in_specs=[row, row, row, _full(k.shape), _full(v.shape), _full(w_xo.shape), _full(w_xq.shape), row,
                  _full(g.shape), ANY_SPEC],
        out_specs=[row, row, _full(k.shape), _full(k.shape), _full((SUBLANES, D_MODEL))],
        compiler_params=_cparams(1),
    )(dx2, o, q, k, v, w_xo, w_xq, x1, g, dep)


def _mem_kv_bwd(dk, dv, mem_n, mem, w_xk, w_xv):
    def body(dk_ref, dv_ref, mn_ref, mem_ref, wk_ref, wv_ref, dwk_ref, dwv_ref, dg_ref):
        dkb, dvb = dk_ref[...].astype(BF16), dv_ref[...].astype(BF16)
        mn = mn_ref[...]
        dwk_ref[...] = _dot_tn(mn, dkb).astype(BF16)
        dwv_ref[...] = _dot_tn(mn, dvb).astype(BF16)
        dmn = _dot_nt(dkb, wk_ref[...]) + _dot_nt(dvb, wv_ref[...])
        xh, _ = _rms(mem_ref[...])
        dg_ref[...] = jnp.zeros_like(dg_ref)
        dg_ref[0:1, :] = jnp.sum(dmn * xh, axis=0, keepdims=True)

    vm = pl.BlockSpec(memory_space=pltpu.VMEM)
    return pl.pallas_call(
        body, name="mem_kv_bwd",
        out_shape=[jax.ShapeDtypeStruct(w_xk.shape, BF16), jax.ShapeDtypeStruct(w_xv.shape, BF16),
                   jax.ShapeDtypeStruct((SUBLANES, D_MODEL), F32)],
        in_specs=[vm] * 6, out_specs=[vm] * 3,
        compiler_params=pltpu.CompilerParams(vmem_limit_bytes=VMEM_LIMIT),
    )(dk, dv, mem_n, mem, w_xk, w_xv)


def _mix_out_bwd(dx1, w_out, attn, gb, gc, xi, w_sc, g_a, g_c, dep):
    s = dx1.shape[0]
    tb = TM // SUBLANES

    def body(dx1_ref, wout_ref, attn_ref, gb_ref, gc_ref, xi_ref, gch_ref, xih_ref, wsc_ref, ga_ref, gcv_ref, dep_ref,
             da1, da4, da16, dd1, dd4, dd16, dgb_ref, dcv_ref, dga_ref, dgc_ref, dwsc_ref, scr):
        i = pl.program_id(0)

        @pl.when(i == 0)
        def _():
            dga_ref[...] = jnp.zeros_like(dga_ref)
            dgc_ref[...] = jnp.zeros_like(dgc_ref)
            dwsc_ref[...] = jnp.zeros_like(dwsc_ref)

        dmixed = _dot_nt(dx1_ref[...].astype(BF16), wout_ref[...])
        da, dcn = dmixed[:, :ATTN_W], dmixed[:, ATTN_W:]
        attn = attn_ref[...]
        xa, ra = _rms(attn)
        dga_ref[0:1, :] += jnp.sum(da * xa, axis=0, keepdims=True)
        dattn = _rms_bwd(xa, ra, ga_ref[...], da)
        _spread(dattn, scr, (da1, da4, da16), BF16)
        prod = dattn * attn
        dd = jnp.concatenate(
            [jnp.broadcast_to(jnp.sum(prod[:, h * HEAD_DIM:(h + 1) * HEAD_DIM], axis=-1, keepdims=True),
                              (TM, HEAD_DIM)) for h in range(N_HEADS)], axis=1)
        _spread(dd, scr, (dd1, dd4, dd16), F32)
        gbv = gb_ref[...]
        u = gc_ref[...] * xi_ref[...]
        uh = jnp.where(i > 0, gch_ref[...] * xih_ref[...], 0.0)
        u2, u1 = _shift_down(u, uh, 2), _shift_down(u, uh, 1)
        cv = (u2 * wsc_ref[0:1, :] + u1 * wsc_ref[1:2, :]) + u * wsc_ref[2:3, :]
        xc, rc = _rms(gbv * cv)
        dgc_ref[0:1, :] += jnp.sum(dcn * xc, axis=0, keepdims=True)
        dconv = _rms_bwd(xc, rc, gcv_ref[...], dcn)
        dgb_ref[...] = (dconv * cv).astype(BF16)
        dcv = dconv * gbv
        dcv_ref[...] = dcv
        dwsc_ref[0:1, :] += jnp.sum(dcv * u2, axis=0, keepdims=True)
        dwsc_ref[1:2, :] += jnp.sum(dcv * u1, axis=0, keepdims=True)
        dwsc_ref[2:3, :] += jnp.sum(dcv * u, axis=0, keepdims=True)

    row = lambda n: pl.BlockSpec((TM, n), lambda i: (i, 0))
    halo = pl.BlockSpec((SUBLANES, 512), lambda i: (jnp.maximum(i * tb - 1, 0), 0))
    acc = _full((SUBLANES, 512))
    res = pl.pallas_call(
        body, name="mix_out_bwd", grid=(s // TM,),
        out_shape=_class_shapes(s, 512, BF16) + _class_shapes(s, 512, F32)
        + [jax.ShapeDtypeStruct((s, 512), BF16), jax.ShapeDtypeStruct((s, 512), F32)]
        + [jax.ShapeDtypeStruct((SUBLANES, 512), F32)] * 3,
        in_specs=[row(D_MODEL), _full(w_out.shape), row(512), row(512), row(512), row(512), halo, halo,
                  _full(w_sc.shape), _full(g_a.shape), _full(g_c.shape), ANY_SPEC],
        out_specs=_class_specs(512) * 2 + [row(512)] * 2 + [acc] * 3,
        scratch_shapes=[pltpu.VMEM((512 // LANES, TM, LANES), F32)],
        compiler_params=_cparams(1),
    )(dx1, w_out, attn, gb, gc, xi, gc, xi, w_sc, g_a, g_c, dep)
    return res[0:3], res[3:6], res[6], res[7], res[8], res[9], res[10]


def _swa_bwd(qc, kc, vc, doc, lsec, ddc, bias, dil, dep):
    nsub, nb, ncls = _swa_steps(qc, dil)
    n128 = nsub * nb
    whole = nb == 1

    def body(q_ref, qn_ref, kp_ref, kc_ref, vp_ref, vc_ref, do_ref, don_ref, lse_ref, lsen_ref, dd_ref, ddn_ref,
             b_ref, dep_ref, dq_ref, dk_ref, dv_ref, db_ref, s_scr, dp_scr, sn_scr, dpn_scr, ds_scr, p_scr, dsn_scr,
             pn_scr):
        r, b = pl.program_id(0), pl.program_id(1)

        @pl.when((r == 0) & (b == 0))
        def _():
            db_ref[...] = jnp.zeros_like(db_ref)

        pairs = [slice(a * LANES, (a + 1) * LANES) for a in range(N_HEADS // 2)]
        blk = [slice(t * WIN, (t + 1) * WIN) for t in range(nsub)]
        last = blk[nsub - 1]
        cols = lambda t: slice(WIN, 2 * WIN) if whole and t == 0 else slice(0, 2 * WIN)
        per_head = lambda ref, c, rows: jnp.stack(
            [ref[c, rows, h * HEAD_DIM:h * HEAD_DIM + 1] for h in range(N_HEADS)])

        def keys(prev_ref, cur_ref, c, t, sl):
            if whole and t == 0:
                return cur_ref[c, blk[0], sl]
            if t == 0:
                return jnp.concatenate([prev_ref[c, :, sl], cur_ref[c, blk[0], sl]], axis=0)
            return cur_ref[c, (t - 1) * WIN:(t + 1) * WIN, sl]

        for a, sl in enumerate(pairs):
            for c, t in [(c, t) for c in range(ncls) for t in range(nsub)]:
                k2, v2 = keys(kp_ref, kc_ref, c, t, sl), keys(vp_ref, vc_ref, c, t, sl)
                q_eo = _pair_split(q_ref[c, blk[t], sl])
                do_eo = _pair_split(do_ref[c, blk[t], sl].astype(BF16))
                for e in range(2):
                    s_scr[c * nsub + t, 2 * a + e, :, cols(t)] = _dot_nt(q_eo[e], k2)
                    dp_scr[c * nsub + t, 2 * a + e, :, cols(t)] = _dot_nt(do_eo[e], v2)
            if not whole:
                qn_eo = _pair_split(qn_ref[0, :, sl])
                don_eo = _pair_split(don_ref[0, :, sl].astype(BF16))
                for e in range(2):
                    sn_scr[2 * a + e] = _dot_nt(qn_eo[e], kc_ref[0, last, sl])
                    dpn_scr[2 * a + e] = _dot_nt(don_eo[e], vc_ref[0, last, sl])
        for c, t in [(c, t) for c in range(ncls) for t in range(nsub)]:
            i, cl = c * nsub + t, cols(t)
            first = (b == 0) if t == 0 else False
            lg = jnp.where(_band_mask(first)[:, cl], s_scr[i, :, :, cl] + b_ref[:, :, cl], -jnp.inf)
            p = jnp.exp(lg - per_head(lse_ref, c, blk[t]))
            ds = p * (dp_scr[i, :, :, cl] - per_head(dd_ref, c, blk[t]))
            db_ref[:, :, cl] += ds
            ds_scr[i, :, :, cl] = ds.astype(BF16)
            p_scr[i, :, :, cl] = p.astype(BF16)
        if not whole:
            qi = lax.broadcasted_iota(jnp.int32, (WIN, WIN), 0)
            kj = lax.broadcasted_iota(jnp.int32, (WIN, WIN), 1)
            valid_n = kj >= qi + jnp.where(b + 1 < nb, 0, WIN)
            every = slice(0, WIN)
            lgn = jnp.where(valid_n, sn_scr[...] + b_ref[:, :, :WIN], -jnp.inf)
            pn = jnp.exp(lgn - per_head(lsen_ref, 0, every))
            dsn_scr[...] = (pn * (dpn_scr[...] - per_head(ddn_ref, 0, every))).astype(BF16)
            pn_scr[...] = pn.astype(BF16)
        for a, sl in enumerate(pairs):
            for c in range(ncls):
                q_eo = [_pair_split(q_ref[c, blk[t], sl]) for t in range(nsub)]
                do_eo = [_pair_split(do_ref[c, blk[t], sl].astype(BF16)) for t in range(nsub)]
                if not whole:
                    q_eo.append(_pair_split(qn_ref[0, :, sl]))
                    do_eo.append(_pair_split(don_ref[0, :, sl].astype(BF16)))
                for t in range(nsub):
                    i = c * nsub + t
                    k_eo = _pair_split(keys(kp_ref, kc_ref, c, t, sl))
                    dq, dk, dv = None, None, None
                    for e in range(2):
                        h = 2 * a + e
                        terms = [_dot(ds_scr[i, h, :, cols(t)], k_eo[e]),
                                 _dot_tn(ds_scr[i, h, :, WIN:], q_eo[t][e]),
                                 _dot_tn(p_scr[i, h, :, WIN:], do_eo[t][e])]
                        if t + 1 < nsub or not whole:
                            ds_next = ds_scr[i + 1, h, :, :WIN] if t + 1 < nsub else dsn_scr[h]
                            p_next = p_scr[i + 1, h, :, :WIN] if t + 1 < nsub else pn_scr[h]
                            terms[1] += _dot_tn(ds_next, q_eo[t + 1][e])
                            terms[2] += _dot_tn(p_next, do_eo[t + 1][e])
                        dq, dk, dv = terms if e == 0 else (dq + terms[0], dk + terms[1], dv + terms[2])
                    dq_ref[c, blk[t], sl] = dq.astype(BF16)
                    dk_ref[c, blk[t], sl] = dk.astype(BF16)
                    dv_ref[c, blk[t], sl] = dv.astype(BF16)

    cur = pl.BlockSpec((ncls, nsub * WIN, 512), lambda r, b: (r, b, 0))
    prev = pl.BlockSpec((ncls, WIN, 512), lambda r, b: (r, jnp.maximum(nsub * b - 1, 0), 0))
    nxt = pl.BlockSpec((ncls, WIN, 512), lambda r, b: (r, jnp.minimum(nsub * b + nsub, n128 - 1), 0))
    wide, narrow = (ncls * nsub, N_HEADS, WIN, 2 * WIN), (N_HEADS, WIN, WIN)
    return pl.pallas_call(
        body, name=f"swa_bwd_d{dil}", grid=(dil // ncls, nb),
        out_shape=[jax.ShapeDtypeStruct(qc.shape, BF16)] * 3 + [jax.ShapeDtypeStruct(bias.shape, F32)],
        in_specs=[cur, nxt, prev, cur, prev, cur, cur, nxt, cur, nxt, cur, nxt, _full(bias.shape), ANY_SPEC],
        out_specs=[cur] * 3 + [_full(bias.shape)],
        scratch_shapes=[pltpu.VMEM(wide, F32), pltpu.VMEM(wide, F32), pltpu.VMEM(narrow, F32),
                        pltpu.VMEM(narrow, F32), pltpu.VMEM(wide, BF16), pltpu.VMEM(wide, BF16),
                        pltpu.VMEM(narrow, BF16), pltpu.VMEM(narrow, BF16)],
        compiler_params=_cparams(2),
    )(qc, qc, kc, kc, vc, vc, doc, doc, lsec, lsec, ddc, ddc, bias, dep)


def _in_proj_bwd(dqs, dks, dvs, dgb, dcv, gc, xi, w_sc, w_in_g, x, g_mix, dx1):
    s = x.shape[0]
    tb = TM // SUBLANES
    last = s // SUBLANES - 1
    n_tiles = s // TM

    def body(dq1, dq4, dq16, dk1, dk4, dk16, dv1, dv4, dv16, dgb_ref, dcv_ref, dcvn_ref, gc_ref, xi_ref, wsc_ref,
             win_ref, x_ref, g_ref, dx1_ref, dproj_ref, gx_ref, dg_ref, scr_a, scr_b):
        i = pl.program_id(0)

        @pl.when(i == 0)
        def _():
            dg_ref[...] = jnp.zeros_like(dg_ref)

        d0 = dcv_ref[...]
        dn = jnp.where(i < n_tiles - 1, dcvn_ref[...], 0.0)
        du = (d0 * wsc_ref[2:3, :] + _shift_up(d0, dn, 1) * wsc_ref[1:2, :]) + _shift_up(d0, dn, 2) * wsc_ref[0:1, :]
        merge = lambda a, b4, b16: ((a[...].astype(F32) + _gather_classes(b4, scr_a, 4))
                                    + _gather_classes(b16, scr_b, 16))
        dq = merge(dq1, dq4, dq16) * (HEAD_DIM ** -0.5)
        dk = merge(dk1, dk4, dk16)
        dv = merge(dv1, dv4, dv16)
        dproj = jnp.concatenate([dq, dk, dv, dgb_ref[...].astype(F32), du * xi_ref[...], du * gc_ref[...]],
                                axis=1).astype(BF16)
        dproj_ref[...] = dproj
        dh = jnp.zeros((TM, D_MODEL), F32)
        for j in range(N_DEV):
            dh = dh + _dot_nt(dproj[:, j * IN_CHUNK:(j + 1) * IN_CHUNK], win_ref[j])
        xh, r = _rms(x_ref[...])
        dg_ref[0:1, :] += jnp.sum(dh * xh, axis=0, keepdims=True)
        gx_ref[...] = dx1_ref[...] + _rms_bwd(xh, r, g_ref[...], dh)

    row = lambda n: pl.BlockSpec((TM, n), lambda i: (i, 0))
    nxt = pl.BlockSpec((SUBLANES, 512), lambda i: (jnp.minimum((i + 1) * tb, last), 0))
    return pl.pallas_call(
        body, name="in_proj_bwd", grid=(n_tiles,),
        out_shape=[jax.ShapeDtypeStruct((s, IN_COLS), BF16), jax.ShapeDtypeStruct((s, D_MODEL), F32),
                   jax.ShapeDtypeStruct((SUBLANES, D_MODEL), F32)],
        in_specs=_class_specs(512) * 3 + [row(512), row(512), nxt, row(512), row(512), _full(w_sc.shape),
                                          _full(w_in_g.shape), row(D_MODEL), _full(g_mix.shape), row(D_MODEL)],
        out_specs=[row(IN_COLS), row(D_MODEL), _full((SUBLANES, D_MODEL))],
        scratch_shapes=[pltpu.VMEM((512 // LANES, TM, LANES), F32)] * 2,
        compiler_params=_cparams(1),
    )(*dqs, *dks, *dvs, dgb, dcv, dcv, gc, xi, w_sc, w_in_g, x, g_mix, dx1)


def _dw(a, b, dep, name, a_chunked=False, b_chunked=False, n_chunks=1, chunk_cols=None):
    ts = TS_DW if (a_chunked or b_chunked or chunk_cols) else TS_DW // 2
    if a_chunked:
        nj, s, kk = a.shape
        nn = b.shape[1]
        a_spec = pl.BlockSpec((1, ts, kk), lambda j, t: (j, t, 0))
        b_spec = pl.BlockSpec((ts, nn), lambda j, t: (t, 0))
    elif b_chunked:
        nj, s, nn = b.shape
        kk = a.shape[1]
        a_spec = pl.BlockSpec((ts, kk), lambda j, t: (t, 0))
        b_spec = pl.BlockSpec((1, ts, nn), lambda j, t: (j, t, 0))
    else:
        s, kk = a.shape
        nj, nn = (n_chunks, chunk_cols) if chunk_cols else (1, b.shape[1])
        a_spec = pl.BlockSpec((ts, kk), lambda j, t: (t, 0))
        b_spec = pl.BlockSpec((ts, nn), lambda j, t: (t, j))
    n_steps = s // ts

    def body(a_ref, b_ref, dep_ref, o_ref, acc):
        t = pl.program_id(1)

        @pl.when(t == 0)
        def _():
            acc[...] = jnp.zeros_like(acc)

        av = (a_ref[0] if a_chunked else a_ref[...]).astype(BF16)
        bv = (b_ref[0] if b_chunked else b_ref[...]).astype(BF16)
        acc[...] += _dot_tn(av, bv)

        @pl.when(t == n_steps - 1)
        def _():
            o_ref[0] = acc[...].astype(BF16)

    return pl.pallas_call(
        body, name=name, grid=(nj, n_steps),
        out_shape=jax.ShapeDtypeStruct((nj, kk, nn), BF16),
        in_specs=[a_spec, b_spec, ANY_SPEC],
        out_specs=pl.BlockSpec((1, kk, nn), lambda j, t: (j, 0, 0)),
        scratch_shapes=[pltpu.VMEM((kk, nn), F32)],
        compiler_params=_cparams(2),
    )(a, b, dep)


def _adamw_math(w, g, m, v):
    m2 = ADAM_B1 * m + (1.0 - ADAM_B1) * g
    v2 = ADAM_B2 * v + (1.0 - ADAM_B2) * (g * g)
    m_hat = m2 / (1.0 - ADAM_B1 ** ADAM_STEP)
    v_hat = v2 / (1.0 - ADAM_B2 ** ADAM_STEP)
    delta = -ADAM_LR * (m_hat / (jnp.sqrt(v_hat) + ADAM_EPS) + ADAM_WD * w)
    return delta, m2, v2


def _sum_parts(me, own, p_ref):
    g = None
    for i in range(N_DEV):
        part = jnp.where(me == i, own.astype(F32), p_ref[i].astype(F32))
        g = part if g is None else g + part
    return g


def _adamw_big(name, w, sent, parts, m, v, me_arr):
    rr, cc = w.shape
    tr = rr // 4 if rr >= 512 else rr

    def body(me_ref, w_ref, own_ref, p_ref, m_ref, v_ref, g_ref, d_ref, nm_ref, nv_ref):
        g = own_ref[0].astype(F32)
        for k in range(1, N_DEV):
            g = g + p_ref[(me_ref[0] + k) % N_DEV].astype(F32)
        g_ref[...] = g
        d_ref[...], nm_ref[...], nv_ref[...] = _adamw_math(w_ref[...], g, m_ref[...], v_ref[...])

    row = pl.BlockSpec((tr, cc), lambda i, me: (i, 0))
    return pl.pallas_call(
        body, name=name,
        grid_spec=pltpu.PrefetchScalarGridSpec(
            num_scalar_prefetch=1, grid=(rr // tr,),
            in_specs=[row, pl.BlockSpec((1, tr, cc), lambda i, me: (me[0], i, 0)),
                      pl.BlockSpec((N_DEV, tr, cc), lambda i, me: (0, i, 0)), row, row],
            out_specs=[row] * 4),
        out_shape=[jax.ShapeDtypeStruct((rr, cc), F32)] * 4,
        compiler_params=_cparams(1),
    )(me_arr, w, sent, parts, m, v)


def _small_slices():
    return [
        (slice(ROW_RELB, ROW_RELB + 8), slice(0, N_BUCKETS)),
        (slice(ROW_GMIX, ROW_GMIX + 1), slice(0, D_MODEL)),
        (slice(ROW_GAC, ROW_GAC + 1), slice(0, ATTN_W)),
        (slice(ROW_GAC, ROW_GAC + 1), slice(ATTN_W, D_MODEL)),
        (slice(ROW_GXATTN, ROW_GXATTN + 1), slice(0, D_MODEL)),
        (slice(ROW_GMEM, ROW_GMEM + 1), slice(0, D_MODEL)),
        (slice(ROW_GFFN, ROW_GFFN + 1), slice(0, D_MODEL)),
        (slice(ROW_BFC, ROW_BFC + 8), slice(0, UP_CHUNK)),
        (slice(ROW_GFINAL, ROW_GFINAL + 1), slice(0, D_MODEL)),
    ]


def _adamw_small(own, parts, wmv, me_arr):
    slices = _small_slices()
    n = len(slices)

    def body(*refs):
        me_ref, own_ref, p_ref = refs[:3]
        ins = refs[3:3 + 3 * n]
        g_ref = refs[3 + 3 * n]
        outs = refs[4 + 3 * n:]
        g = _sum_parts(me_ref[0], own_ref[...], p_ref)
        g_ref[...] = g
        for a, (rs, ls) in enumerate(slices):
            ga = g[rs, ls]
            outs[4 * a][...] = ga
            outs[4 * a + 1][...], outs[4 * a + 2][...], outs[4 * a + 3][...] = _adamw_math(
                ins[3 * a][...], ga, ins[3 * a + 1][...], ins[3 * a + 2][...])

    vm = pl.BlockSpec(memory_space=pltpu.VMEM)
    flat = [t for trip in wmv for t in trip]
    out_shape = [jax.ShapeDtypeStruct((SMALL_ROWS, D_MODEL), F32)]
    for w, _, _ in wmv:
        out_shape += [jax.ShapeDtypeStruct(w.shape, F32)] * 4
    res = pl.pallas_call(
        body, name="adamw_small", out_shape=out_shape,
        in_specs=[SMEM_SPEC] + [vm] * (2 + 3 * n), out_specs=[vm] * len(out_shape),
    )(me_arr, own, parts, *flat)
    return res[0], [res[1 + 4 * a:5 + 4 * a] for a in range(n)]


def _adamw_shards(items):
    n = len(items)

    def body(*refs):
        for a in range(n):
            w_ref, g_ref, m_ref, v_ref = refs[4 * a:4 * a + 4]
            d_ref, nm_ref, nv_ref = refs[4 * n + 3 * a:4 * n + 3 * a + 3]
            d_ref[...], nm_ref[...], nv_ref[...] = _adamw_math(w_ref[...], g_ref[...], m_ref[...], v_ref[...])

    vm = pl.BlockSpec(memory_space=pltpu.VMEM)
    out_shape = []
    for w, _, _, _ in items:
        out_shape += [jax.ShapeDtypeStruct(w.shape, F32)] * 3
    res = pl.pallas_call(
        body, name="adamw_shards", out_shape=out_shape, in_specs=[vm] * (4 * n), out_specs=[vm] * (3 * n),
    )(*[t for it in items for t in it])
    return [res[3 * a:3 * a + 3] for a in range(n)]


def _mesh_pos():
    return lax.axis_index("x"), lax.axis_index("y"), lax.axis_index("c")


def _dev_index(p):
    return 4 * p[0] + 2 * p[1] + p[2]


def _all_gather(shards):
    n = len(shards)

    def body(*refs):
        ins, outs = refs[:n], refs[n:2 * n]
        send_sems, recv_sems, loc_sems = refs[2 * n:]
        x, y, c = _mesh_pos()
        me, sib = (x, y, c), (x, y, 1 - c)
        chips = [(1 - x, y), (x, 1 - y), (1 - x, 1 - y)]

        def cp(a, k, block, to, src=None):
            dst = outs[a].at[_dev_index(block)]
            return pltpu.make_async_remote_copy(
                src_ref=dst if src is None else src, dst_ref=dst, send_sem=send_sems.at[a, k],
                recv_sem=recv_sems.at[a, k], device_id=to, device_id_type=MESH)

        mine = [pltpu.make_async_copy(ins[a], outs[a].at[_dev_index(me)], loc_sems.at[a]) for a in range(n)]
        for m_ in mine:
            m_.start()
        first = []
        for a in range(n):
            first.append(cp(a, 0, me, sib, src=ins[a]))
            first += [cp(a, 1 + j, me, (*chip, c), src=ins[a]) for j, chip in enumerate(chips)]
        for f in first:
            f.start()
        passed = []
        for a in range(n):
            for j, chip in enumerate(chips):
                cp(a, 1 + j, (*chip, c), me).wait_recv()
                fwd = cp(a, 4 + j, (*chip, c), sib)
                fwd.start()
                passed.append(fwd)
        for a in range(n):
            cp(a, 0, sib, me).wait_recv()
            for j, chip in enumerate(chips):
                cp(a, 4 + j, (*chip, 1 - c), me).wait_recv()
        for f in first + passed:
            f.wait_send()
        for m_ in mine:
            m_.wait()

    hbm = pl.BlockSpec(memory_space=pltpu.HBM)
    return pl.pallas_call(
        body, name="all_gather_weights",
        out_shape=[jax.ShapeDtypeStruct((N_DEV,) + a.shape, a.dtype) for a in shards],
        in_specs=[hbm] * n, out_specs=[hbm] * n,
        scratch_shapes=[pltpu.SemaphoreType.DMA((n, 7)), pltpu.SemaphoreType.DMA((n, 7)),
                        pltpu.SemaphoreType.DMA((n,))],
    )(*shards)


def _peers():
    x, y, c = _mesh_pos()
    return (x, y, c), [((1 - x) if k & 4 else x, (1 - y) if k & 2 else y, (1 - c) if k & 1 else c)
                       for k in range(1, 8)]


def _exchange_copy(src_ref, land_ref, whole, send_sems, recv_sems, a, k, peer, slot):
    src = src_ref if whole else src_ref.at[_dev_index(peer)]
    return pltpu.make_async_remote_copy(
        src_ref=src, dst_ref=land_ref.at[slot], send_sem=send_sems.at[7 * a + k], recv_sem=recv_sems.at[7 * a + k],
        device_id=peer, device_id_type=MESH)


def _exchange_start(name, srcs, whole, dep):
    n = len(srcs)
    lands = [lax.empty(((N_DEV,) + s.shape) if w else s.shape, s.dtype) for s, w in zip(srcs, whole)]

    def body(*refs):
        src_refs, land_refs = refs[:n], refs[n:2 * n]
        send_sems, recv_sems, token = refs[2 * n + 1], refs[2 * n + 2], refs[-1]
        me, peers = _peers()
        for a in range(n):
            for k, peer in enumerate(peers):
                _exchange_copy(src_refs[a], land_refs[a], whole[a], send_sems, recv_sems, a, k, peer,
                               _dev_index(me)).start()
        token[...] = jnp.zeros_like(token)

    res = pl.pallas_call(
        body, name=name,
        out_shape=(pltpu.SemaphoreType.DMA((7 * n,)), pltpu.SemaphoreType.DMA((7 * n,)),
                   *[pltpu.HBM(a.shape, a.dtype) for a in srcs], *[pltpu.HBM(a.shape, a.dtype) for a in lands],
                   jax.ShapeDtypeStruct((SUBLANES, 128), F32)),
        in_specs=[HBM_SPEC] * (2 * n) + [ANY_SPEC],
        out_specs=(SEM_SPEC, SEM_SPEC, *([HBM_SPEC] * (2 * n)), VMEM_SPEC),
        input_output_aliases={i: 2 + i for i in range(2 * n)},
        compiler_params=pltpu.CompilerParams(has_side_effects=DATAFLOW),
    )(*[pltpu.with_memory_space_constraint(a, pltpu.HBM) for a in srcs],
      *[pltpu.with_memory_space_constraint(a, pltpu.HBM) for a in lands], dep)
    return res[0], res[1], list(res[2:2 + n]), list(res[2 + n:2 + 2 * n]), res[-1]


def _exchange_wait(name, started, whole, after, which=None):
    send_sems, recv_sems, srcs, lands, _ = started
    which = list(range(len(srcs))) if which is None else which
    srcs, lands = [srcs[a] for a in which], [lands[a] for a in which]
    n = len(srcs)

    def body(*refs):
        src_refs, land_refs = refs[:n], refs[n:2 * n]
        send_sems, recv_sems = refs[2 * n], refs[2 * n + 1]
        _, peers = _peers()
        for i, a in enumerate(which):
            for k, peer in enumerate(peers):
                cp = _exchange_copy(src_refs[i], land_refs[i], whole[a], send_sems, recv_sems, a, k, peer,
                                    _dev_index(peer))
                cp.wait_send()
                cp.wait_recv()

    res = pl.pallas_call(
        body, name=name,
        out_shape=[pltpu.HBM(a.shape, a.dtype) for a in srcs + lands],
        in_specs=[HBM_SPEC] * (2 * n) + [SEM_SPEC, SEM_SPEC, ANY_SPEC],
        out_specs=[HBM_SPEC] * (2 * n),
        input_output_aliases={i: i for i in range(2 * n)},
        compiler_params=pltpu.CompilerParams(has_side_effects=DATAFLOW),
    )(*srcs, *lands, send_sems, recv_sems, after)
    return list(res[:n]), list(res[n:])


def _gather_start(name, shards, dep):
    n = len(shards)
    lands = [lax.empty((N_DEV,) + a.shape, a.dtype) for a in shards]

    def body(*refs):
        src_refs, land_refs = refs[:n], refs[n:2 * n]
        send_sems, recv_sems, token = refs[2 * n + 1], refs[2 * n + 2], refs[-1]
        x, y, c = _mesh_pos()
        peers = [(x, y, 1 - c), (1 - x, y, c), (x, 1 - y, c), (1 - x, 1 - y, c)]
        for a in range(n):
            for k, peer in enumerate(peers):
                pltpu.make_async_remote_copy(
                    src_ref=src_refs[a], dst_ref=land_refs[a].at[_dev_index((x, y, c))], send_sem=send_sems.at[4 * a + k],
                    recv_sem=recv_sems.at[4 * a + k], device_id=peer, device_id_type=MESH).start()
        token[...] = jnp.zeros_like(token)

    res = pl.pallas_call(
        body, name=name,
        out_shape=(pltpu.SemaphoreType.DMA((4 * n,)), pltpu.SemaphoreType.DMA((4 * n,)),
                   *[pltpu.HBM(a.shape, a.dtype) for a in shards], *[pltpu.HBM(a.shape, a.dtype) for a in lands],
                   jax.ShapeDtypeStruct((SUBLANES, 128), F32)),
        in_specs=[HBM_SPEC] * (2 * n) + [ANY_SPEC],
        out_specs=(SEM_SPEC, SEM_SPEC, *([HBM_SPEC] * (2 * n)), VMEM_SPEC),
        input_output_aliases={i: 2 + i for i in range(2 * n)},
        compiler_params=pltpu.CompilerParams(has_side_effects=DATAFLOW),
    )(*[pltpu.with_memory_space_constraint(a, pltpu.HBM) for a in shards],
      *[pltpu.with_memory_space_constraint(a, pltpu.HBM) for a in lands], dep)
    return res[0], res[1], list(res[2:2 + n]), list(res[2 + n:2 + 2 * n]), res[-1]


def _gather_forward(name, send_sems, recv_sems, lands, which, after):
    n = len(which)

    def body(*refs):
        land_refs = refs[:n]
        send_sems, recv_sems = refs[n], refs[n + 1]
        fsend, frecv, token = refs[n + 3], refs[n + 4], refs[-1]
        x, y, c = _mesh_pos()
        chips = [(1 - x, y), (x, 1 - y), (1 - x, 1 - y)]
        for i, a in enumerate(which):
            for j, chip in enumerate(chips):
                block = land_refs[i].at[_dev_index((*chip, c))]
                pltpu.make_async_remote_copy(
                    src_ref=block, dst_ref=block, send_sem=send_sems.at[4 * a + 1 + j], recv_sem=recv_sems.at[4 * a + 1 + j],
                    device_id=(*chip, c), device_id_type=MESH).wait_recv()
                pltpu.make_async_remote_copy(
                    src_ref=block, dst_ref=block, send_sem=fsend.at[3 * i + j], recv_sem=frecv.at[3 * i + j],
                    device_id=(x, y, 1 - c), device_id_type=MESH).start()
        token[...] = jnp.zeros_like(token)

    res = pl.pallas_call(
        body, name=name,
        out_shape=(pltpu.SemaphoreType.DMA((3 * n,)), pltpu.SemaphoreType.DMA((3 * n,)),
                   *[pltpu.HBM(a.shape, a.dtype) for a in lands], jax.ShapeDtypeStruct((SUBLANES, 128), F32)),
        in_specs=[HBM_SPEC] * n + [SEM_SPEC, SEM_SPEC, ANY_SPEC],
        out_specs=(SEM_SPEC, SEM_SPEC, *([HBM_SPEC] * n), VMEM_SPEC),
        input_output_aliases={i: 2 + i for i in range(n)},
        compiler_params=pltpu.CompilerParams(has_side_effects=DATAFLOW),
    )(*lands, send_sems, recv_sems, after)
    return res[0], res[1], list(res[2:2 + n]), res[-1]


def _gather_wait(name, send_sems, recv_sems, fsend, frecv, srcs, lands, which, after):
    n = len(which)

    def body(*refs):
        land_refs = refs[n:2 * n]
        send_sems, recv_sems, fsend, frecv = refs[2 * n:2 * n + 4]
        x, y, c = _mesh_pos()
        sib = (x, y, 1 - c)
        chips = [(1 - x, y), (x, 1 - y), (1 - x, 1 - y)]
        for i, a in enumerate(which):
            def cp(slot, ssem, rsem):
                block = land_refs[i].at[_dev_index(slot)]
                return pltpu.make_async_remote_copy(src_ref=block, dst_ref=block, send_sem=ssem, recv_sem=rsem,
                                                    device_id=sib, device_id_type=MESH)
            cp(sib, send_sems.at[4 * a], recv_sems.at[4 * a]).wait_recv()
            for j, chip in enumerate(chips):
                cp((*chip, 1 - c), fsend.at[3 * i + j], frecv.at[3 * i + j]).wait_recv()
            for k in range(4):
                cp(sib, send_sems.at[4 * a + k], recv_sems.at[4 * a + k]).wait_send()
            for j in range(3):
                cp(sib, fsend.at[3 * i + j], frecv.at[3 * i + j]).wait_send()

    res = pl.pallas_call(
        body, name=name,
        out_shape=[pltpu.HBM(a.shape, a.dtype) for a in srcs + lands],
        in_specs=[HBM_SPEC] * (2 * n) + [SEM_SPEC] * 4 + [ANY_SPEC],
        out_specs=[HBM_SPEC] * (2 * n),
        input_output_aliases={i: i for i in range(2 * n)},
        compiler_params=pltpu.CompilerParams(has_side_effects=DATAFLOW),
    )(*srcs, *lands, send_sems, recv_sems, fsend, frecv, after)
    return list(res[n:])


def _local_step(x, mem, target, rel_bias, g_mix, w_in_g, w_sc, g_a, g_c, g_xattn, g_mem, g_ffn, w_fc, b_fc, g_final,
                dep, forward_weights, late_weights, emit, emit_small):
    s = x.shape[0]
    buckets = _bucket_tables()
    bias = _bias_fwd(rel_bias, buckets)

    h1, qs, ks, vs, gb, gc, xi = _rms_proj(x, g_mix, w_in_g, dep)
    qs, ks, vs = ([a[0][None]] + list(a[1:]) for a in (qs, ks, vs))
    group1, group2 = ["w_out", "w_xq", "w_xk", "w_xv", "w_xo"], ["w_up", "w_down"]
    tok = forward_weights(group1, h1)
    branches = []
    for p, dil in enumerate(DILATIONS):
        o_p, lse_p = _swa_fwd(qs[p], ks[p], vs[p], bias[p], dil, tok)
        branches.append([o_p[0], lse_p[0]] if dil == 1 else [o_p, lse_p])
    lw = late_weights(group1, branches[-1][0])
    w_out, w_xq, w_xk, w_xv, w_xo = (lw[n] for n in group1)
    attn, lses, mixed, x1 = _mix_out(branches, gb, gc, xi, x, w_sc, g_a, g_c, w_out)
    tok = forward_weights(group2, x1)
    mem_n, mk, mv = _mem_kv(mem, g_mem, w_xk, w_xv)
    h2, xq, xo, x2 = _xattn_fwd(x1, g_xattn, w_xq, mk, mv, w_xo, tok)
    lw = late_weights(group2, x2)
    w_up_g, w_down_g = lw["w_up"], lw["w_down"]
    h3, conv, act, dx3, loss_acc, dg_final = _ffn_fwd(x2, g_ffn, w_up_g, w_fc, b_fc, w_down_g, g_final, target)

    gw_down = _dw(act, dx3, dep, "dw_down", a_chunked=True)
    dup, dx2, dg_ffn, dw_fc, db_fc = _ffn_bwd(dx3, h3, conv, x2, g_ffn, w_up_g, w_fc, w_down_g)
    gw_up = _dw(dup, h3, dep, "dw_up", a_chunked=True)
    tok = emit(dict(w_down=gw_down, w_up=gw_up))
    dxq, dx1, dmk, dmv, dg_xattn = _xattn_bwd(dx2, xo, xq, mk, mv, w_xo, w_xq, x1, g_xattn, tok)
    gw_xo = _dw(xo, dx2, tok, "dw_xo")[0]
    gw_xq = _dw(h2, dxq, tok, "dw_xq")[0]
    gw_xk, gw_xv, dg_mem = _mem_kv_bwd(dmk, dmv, mem_n, mem, w_xk, w_xv)
    tok = emit(dict(w_xo=gw_xo, w_xq=gw_xq, w_xk=gw_xk, w_xv=gw_xv))
    dattns, dds, dgb, dcv, dg_a, dg_c, dw_sc = _mix_out_bwd(dx1, w_out, attn, gb, gc, xi, w_sc, g_a, g_c, tok)
    first = lambda a: [a[0][None]] + list(a[1:])
    dattns, dds, lses = first(dattns), first(dds), first(lses)
    gw_out = _dw(mixed, dx1, tok, "dw_out")[0]
    tok = emit(dict(w_out=gw_out))
    dqs, dks, dvs, dbias = [], [], [], []
    for p, dil in enumerate(DILATIONS):
        dq_p, dk_p, dv_p, db_p = _swa_bwd(qs[p], ks[p], vs[p], dattns[p], lses[p], dds[p], bias[p], dil, tok)
        dqs.append(dq_p[0] if dil == 1 else dq_p)
        dks.append(dk_p[0] if dil == 1 else dk_p)
        dvs.append(dv_p[0] if dil == 1 else dv_p)
        dbias.append(db_p)
    d_relb = _bias_bwd(jnp.stack(dbias), buckets)
    dproj, grad_x, dg_mix = _in_proj_bwd(dqs, dks, dvs, dgb, dcv, gc, xi, w_sc, w_in_g, x, g_mix, dx1)
    pad = lambda a: jnp.pad(a, ((0, 0), (0, D_MODEL - a.shape[1])))
    small = jnp.concatenate([
        d_relb, dg_mix, dg_xattn, dg_mem, dg_ffn, dg_final, jnp.concatenate([dg_a, dg_c], axis=1),
        pad(dw_sc), pad(db_fc), pad(dw_fc.reshape(3 * N_DEV, UP_CHUNK)), pad(loss_acc)], axis=0)
    tok = emit_small(small)
    gw_in = _dw(h1, dproj, tok, "dw_in", n_chunks=N_DEV, chunk_cols=IN_CHUNK)
    emit(dict(w_in=gw_in))
    return grad_x


def kernel(x, mem, rel_bias, g_mix, w_in, w_short_conv, g_attn_out, g_conv_out, w_out, g_xattn, g_mem, w_xq, w_xk, w_xv, w_xo, g_ffn, w_up, w_ffn_conv, b_ffn_conv, w_down, g_final, loss_target, m_rel_bias, m_g_mix, m_w_in, m_w_short_conv, m_g_attn_out, m_g_conv_out, m_w_out, m_g_xattn, m_g_mem, m_w_xq, m_w_xk, m_w_xv, m_w_xo, m_g_ffn, m_w_up, m_w_ffn_conv, m_b_ffn_conv, m_w_down, m_g_final, v_rel_bias, v_g_mix, v_w_in, v_w_short_conv, v_g_attn_out, v_g_conv_out, v_w_out, v_g_xattn, v_g_mem, v_w_xq, v_w_xk, v_w_xv, v_w_xo, v_g_ffn, v_w_up, v_w_ffn_conv, v_b_ffn_conv, v_w_down, v_g_final):
    me = _dev_index(_mesh_pos())
    me_arr = me.reshape(1).astype(jnp.int32)

    big_names = ["w_in", "w_out", "w_xq", "w_xk", "w_xv", "w_xo", "w_up", "w_down"]
    late_names = big_names[1:]
    big_w = dict(w_in=w_in[0], w_out=w_out[0], w_xq=w_xq[0], w_xk=w_xk[0], w_xv=w_xv[0], w_xo=w_xo[0],
                 w_up=w_up[0].T, w_down=w_down[0])
    big_m = dict(w_in=m_w_in[0], w_out=m_w_out[0], w_xq=m_w_xq[0], w_xk=m_w_xk[0], w_xv=m_w_xv[0], w_xo=m_w_xo[0],
                 w_up=m_w_up[0].T, w_down=m_w_down[0])
    big_v = dict(w_in=v_w_in[0], w_out=v_w_out[0], w_xq=v_w_xq[0], w_xk=v_w_xk[0], w_xv=v_w_xv[0], w_xo=v_w_xo[0],
                 w_up=v_w_up[0].T, w_down=v_w_down[0])
    shard_shape = {n: big_w[n].shape for n in big_names}

    w_in_g, w_sc_g, w_fc_full = _all_gather([big_w["w_in"].astype(BF16), w_short_conv[0], w_ffn_conv[0]])
    w_sc_full = w_sc_g.transpose(1, 0, 2).reshape(3, CONV_W)
    late_shards = [big_w[n].astype(BF16) for n in late_names]
    ag_send, ag_recv, ag_srcs, ag_lands, ag_token = _gather_start("gather_weights_start", late_shards, w_in_g)
    forwarded = {}

    def forward_weights(names, after):
        which = [late_names.index(n) for n in names]
        fsend, frecv, lands, token = _gather_forward("gather_" + "_".join(names) + "_forward", ag_send, ag_recv,
                                                     [ag_lands[a] for a in which], which, after)
        forwarded[tuple(names)] = (fsend, frecv, lands)
        return token

    def late_weights(names, after):
        which = [late_names.index(n) for n in names]
        fsend, frecv, lands = forwarded[tuple(names)]
        lands = _gather_wait("gather_" + "_".join(names) + "_wait", ag_send, ag_recv, fsend, frecv,
                             [ag_srcs[a] for a in which], lands, which, after)
        out = {}
        for n, a, land in zip(names, which, lands):
            full = lax.dynamic_update_index_in_dim(land, late_shards[a], me, 0)
            if n == "w_up":
                out[n] = full
            elif n == "w_down":
                out[n] = full.reshape(N_DEV // 2, UP_CHUNK, D_MODEL)
            else:
                out[n] = full.reshape(D_MODEL, D_MODEL)
        return out

    sent = []

    def emit(grads):
        names = list(grads)
        blocks = [grads[n].reshape((N_DEV,) + shard_shape[n]) for n in names]
        started = _exchange_start("scatter_" + "_".join(names) + "_start", blocks, [False] * len(names), me_arr)
        sent.append((names, started))
        return started[-1]

    def emit_small(small):
        sent_small.append((small, _exchange_start("gather_small_start", [small], [True], me_arr)))
        return sent_small[0][1][-1]

    sent_small = []
    grad_x = _local_step(
        x[0], mem[0], loss_target[0], rel_bias, g_mix, w_in_g, w_sc_full, g_attn_out, g_conv_out, g_xattn, g_mem,
        g_ffn, w_fc_full, b_ffn_conv.reshape(N_DEV, 1, UP_CHUNK), g_final.reshape(1, D_MODEL), ag_token,
        forward_weights, late_weights, emit, emit_small)

    small_g, small_started = sent_small[0]
    after = sent[-1][1][-1]
    small_parts = _exchange_wait("gather_small_wait", small_started, [True], after)[1][0]
    big_out = {}
    after = small_parts
    for names, started in sent:
        blocks, lands = _exchange_wait("scatter_" + "_".join(names) + "_wait", started, [False] * len(names), after)
        for n, block, land in zip(names, blocks, lands):
            res = _adamw_big("adamw_" + n, big_w[n], block, land, big_m[n], big_v[n], me_arr)
            big_out[n] = [(r.T if n == "w_up" else r)[None] for r in res]
            after = res[0]

    as_rows = lambda a: a.reshape(N_DEV, UP_CHUNK)
    row1 = lambda a: a.reshape(1, D_MODEL)
    small_names = ["rel_bias", "g_mix", "g_attn_out", "g_conv_out", "g_xattn", "g_mem", "g_ffn", "b_ffn_conv", "g_final"]
    wmv = [
        (rel_bias, m_rel_bias, v_rel_bias), (g_mix, m_g_mix, v_g_mix), (g_attn_out, m_g_attn_out, v_g_attn_out),
        (g_conv_out, m_g_conv_out, v_g_conv_out), (g_xattn, m_g_xattn, v_g_xattn), (g_mem, m_g_mem, v_g_mem),
        (g_ffn, m_g_ffn, v_g_ffn), (as_rows(b_ffn_conv), as_rows(m_b_ffn_conv), as_rows(v_b_ffn_conv)),
        (row1(g_final), row1(m_g_final), row1(v_g_final))]
    g_packed, small_res = _adamw_small(small_g, small_parts, wmv, me_arr)
    small_out = dict(zip(small_names, small_res))
    loss = g_packed[ROW_LOSS, 0]
    small_out["b_ffn_conv"] = [a.reshape(1, 2 * D_FF) for a in small_out["b_ffn_conv"]]
    small_out["g_final"] = [a.reshape(D_MODEL) for a in small_out["g_final"]]

    g_wsc = lax.dynamic_slice(g_packed[ROW_WSC:ROW_WSC + 3, 0:CONV_W], (0, me * HEAD_DIM), (3, HEAD_DIM))
    g_wfc = lax.dynamic_slice(g_packed[ROW_WFC:ROW_WFC + 3 * N_DEV, 0:UP_CHUNK].reshape(3, N_DEV, UP_CHUNK),
                              (0, me, 0), (3, 1, UP_CHUNK)).reshape(3, UP_CHUNK)
    shard_res = _adamw_shards([(w_short_conv[0], g_wsc, m_w_short_conv[0], v_w_short_conv[0]),
                               (w_ffn_conv[0], g_wfc, m_w_ffn_conv[0], v_w_ffn_conv[0])])
    small_out["w_short_conv"] = [g_wsc[None]] + [a[None] for a in shard_res[0]]
    small_out["w_ffn_conv"] = [g_wfc[None]] + [a[None] for a in shard_res[1]]

    order = ["rel_bias", "g_mix", "w_in", "w_short_conv", "g_attn_out", "g_conv_out", "w_out", "g_xattn", "g_mem",
             "w_xq", "w_xk", "w_xv", "w_xo", "g_ffn", "w_up", "w_ffn_conv", "b_ffn_conv", "w_down", "g_final"]
    allp = {**big_out, **small_out}
    outs = [loss, grad_x[None]]
    for kind in range(4):
        outs += [allp[n][kind] for n in order]
    return tuple(outs)
```

```python
import math

import numpy as np
import jax
import jax.numpy as jnp
from jax import lax
from jax.experimental import pallas as pl
from jax.experimental.pallas import tpu as pltpu

F32 = jnp.float32
BF16 = jnp.bfloat16
MESH = pl.DeviceIdType.MESH

N_DEV = 8
D_MODEL = 1024
ATTN_W = 512
CONV_W = 512
N_HEADS = 8
HEAD_DIM = 64
WIN = 128
DILATIONS = (1, 4, 16)
N_BUCKETS = 32
BUCKET_MAX_EXACT = 16
BUCKET_MAX_DISTANCE = 2048
N_MEM_HEADS = 4
MEM_HEAD_DIM = 256
D_FF = 2816
IN_COLS = 3072
IN_CHUNK = IN_COLS // N_DEV
UP_CHUNK = 2 * D_FF // N_DEV
EPS = 1e-6

ADAM_LR = 0.001
ADAM_B1 = 0.9
ADAM_B2 = 0.999
ADAM_EPS = 1e-08
ADAM_WD = 0.01
ADAM_STEP = 10

SUBLANES = 8
LANES = 128
HALO = 16
TM = 512
TM_FFN = 256
TS_DW = 4096
SWA_BLOCKS = 8
VMEM_LIMIT = 56 * 1024 * 1024

ROW_RELB, ROW_GMIX, ROW_GXATTN, ROW_GMEM, ROW_GFFN, ROW_GFINAL, ROW_GAC = 0, 8, 16, 24, 32, 40, 48
ROW_WSC, ROW_BFC, ROW_WFC, ROW_LOSS, SMALL_ROWS = 56, 64, 72, 96, 104


def _cparams(n_grid):
    return pltpu.CompilerParams(dimension_semantics=("arbitrary",) * n_grid, vmem_limit_bytes=VMEM_LIMIT)


def _full(shape):
    nd = len(shape)
    return pl.BlockSpec(tuple(shape), lambda *_: (0,) * nd)


def _resident(shape):
    nd = len(shape)
    return pl.BlockSpec(tuple(shape), lambda *_: (0,) * nd, pipeline_mode=pl.Buffered(1))


ANY_SPEC = pl.BlockSpec(memory_space=pl.ANY)
HBM_SPEC = pl.BlockSpec(memory_space=pltpu.HBM)
SEM_SPEC = pl.BlockSpec(memory_space=pltpu.SEMAPHORE)
VMEM_SPEC = pl.BlockSpec(memory_space=pltpu.VMEM)
SMEM_SPEC = pl.BlockSpec(memory_space=pltpu.SMEM)
DATAFLOW = pltpu.SideEffectType.DATAFLOW_SIDE_EFFECTING


def _rms(x):
    r = lax.rsqrt(jnp.mean(x * x, axis=-1, keepdims=True) + EPS)
    return x * r, r


def _rms_bwd(xh, r, g, dy):
    dxh = dy * g
    return r * (dxh - xh * jnp.mean(dxh * xh, axis=-1, keepdims=True))


def _shift_down(u, halo, k):
    ru = pltpu.roll(u, k, 0)
    rh = pltpu.roll(halo, k, 0)
    row = lax.broadcasted_iota(jnp.int32, rh.shape, 0)
    head = jnp.where(row < k, rh, ru[0:SUBLANES])
    return jnp.concatenate([head, ru[SUBLANES:]], axis=0)


def _shift_up(u, halo, k):
    tm = u.shape[0]
    ru = pltpu.roll(u, tm - k, 0)
    rh = pltpu.roll(halo, SUBLANES - k, 0)
    row = lax.broadcasted_iota(jnp.int32, rh.shape, 0)
    tail = jnp.where(row >= SUBLANES - k, rh, ru[tm - SUBLANES:])
    return jnp.concatenate([ru[:tm - SUBLANES], tail], axis=0)


def _causal_conv3(u, halo, w_ref):
    return (_shift_down(u, halo, 2) * w_ref[0:1, :] + _shift_down(u, halo, 1) * w_ref[1:2, :]) + u * w_ref[2:3, :]


def _dot(a, b):
    return jnp.dot(a, b, preferred_element_type=F32)


def _dot_nt(a, b):
    return lax.dot_general(a, b, (((1,), (1,)), ((), ())), preferred_element_type=F32)


def _dot_tn(a, b):
    return lax.dot_general(a, b, (((0,), (0,)), ((), ())), preferred_element_type=F32)


def _sigmoid(x):
    return 0.5 * jnp.tanh(0.5 * x) + 0.5


def _bucket_tables():
    qi = np.arange(WIN)[:, None]
    kj = np.arange(2 * WIN)[None, :]
    steps = np.clip(qi + WIN - kj, 0, WIN)
    out = []
    for d in DILATIONS:
        dist = steps * d
        dd = np.maximum(dist, 1).astype(np.float32)
        large = BUCKET_MAX_EXACT + (
            np.log(dd / np.float32(BUCKET_MAX_EXACT)) / np.float32(math.log(BUCKET_MAX_DISTANCE / BUCKET_MAX_EXACT))
            * np.float32(N_BUCKETS - BUCKET_MAX_EXACT)).astype(np.int32)
        large = np.minimum(large, N_BUCKETS - 1)
        out.append(np.where(dist < BUCKET_MAX_EXACT, dist, large).astype(np.int32))
    return np.stack(out)


def _band_mask():
    qi = lax.broadcasted_iota(jnp.int32, (WIN, 2 * WIN), 0)
    kj = lax.broadcasted_iota(jnp.int32, (WIN, 2 * WIN), 1)
    steps = qi + WIN - kj
    return (steps >= 0) & (steps <= WIN)


def _bias_fwd(rel_bias, buckets):
    present = [sorted(set(buckets[p].ravel().tolist())) for p in range(3)]

    def body(rb_ref, bk_ref, o_ref):
        band = _band_mask()
        for p in range(3):
            bk = bk_ref[p]
            for h in range(N_HEADS):
                acc = jnp.zeros((WIN, 2 * WIN), F32)
                for b in present[p]:
                    acc = jnp.where(bk == b, rb_ref[h, b], acc)
                o_ref[p, h] = jnp.where(band, acc, -jnp.inf)

    return pl.pallas_call(
        body, name="bias_fwd",
        out_shape=jax.ShapeDtypeStruct((3, N_HEADS, WIN, 2 * WIN), F32),
        in_specs=[pl.BlockSpec(memory_space=pltpu.SMEM), pl.BlockSpec(memory_space=pltpu.VMEM)],
        out_specs=pl.BlockSpec(memory_space=pltpu.VMEM),
    )(rel_bias, jnp.asarray(buckets))


def _bias_bwd(dbias, buckets):
    present = [set(buckets[p].ravel().tolist()) for p in range(3)]

    def body(db_ref, bk_ref, o_ref):
        lane = lax.broadcasted_iota(jnp.int32, (1, D_MODEL), 1)
        rows = []
        for h in range(N_HEADS):
            row = jnp.zeros((1, D_MODEL), F32)
            for b in range(N_BUCKETS):
                tot = jnp.zeros((1, 1), F32)
                for p in (p for p in range(3) if b in present[p]):
                    sel = jnp.where(bk_ref[p] == b, db_ref[p, h], 0.0)
                    tot = tot + jnp.sum(jnp.sum(sel, axis=0, keepdims=True), axis=1, keepdims=True)
                row = jnp.where(lane == b, tot, row)
            rows.append(row)
        o_ref[...] = jnp.concatenate(rows, axis=0)

    return pl.pallas_call(
        body, name="bias_bwd",
        out_shape=jax.ShapeDtypeStruct((N_HEADS, D_MODEL), F32),
        in_specs=[pl.BlockSpec(memory_space=pltpu.VMEM), pl.BlockSpec(memory_space=pltpu.VMEM)],
        out_specs=pl.BlockSpec(memory_space=pltpu.VMEM),
    )(dbias, jnp.asarray(buckets))


def _spread(val, scr_ref, out_refs, dtype):
    out_refs[0][...] = val.astype(dtype)
    n_blk = val.shape[1] // LANES
    for c in range(n_blk):
        scr_ref[c] = val[:, c * LANES:(c + 1) * LANES]
    for o_ref, d in zip(out_refs[1:], DILATIONS[1:]):
        for r in range(d):
            for c in range(n_blk):
                o_ref[r, :, c * LANES:(c + 1) * LANES] = scr_ref.at[c][pl.ds(r, TM // d, stride=d), :].astype(dtype)


def _gather_classes(blk_ref, scr_ref, d):
    n_blk = blk_ref.shape[2] // LANES
    for r in range(d):
        for c in range(n_blk):
            scr_ref.at[c][pl.ds(r, TM // d, stride=d), :] = blk_ref[r, :, c * LANES:(c + 1) * LANES].astype(F32)
    return jnp.concatenate([scr_ref[c] for c in range(n_blk)], axis=1)


def _class_specs(cols):
    return [pl.BlockSpec((TM, cols), lambda i: (i, 0))] + [
        pl.BlockSpec((d, TM // d, cols), lambda i: (0, i, 0)) for d in DILATIONS[1:]]


def _class_shapes(s, cols, dtype):
    return [jax.ShapeDtypeStruct((s, cols), dtype)] + [
        jax.ShapeDtypeStruct((d, s // d, cols), dtype) for d in DILATIONS[1:]]


def _rms_proj(x, g_mix, w_in_g, dep):
    s = x.shape[0]

    def body(x_ref, g_ref, w_ref, dep_ref, h_ref, q1, q4, q16, k1, k4, k16, v1, v4, v16, gb_ref, gc_ref, xi_ref, scr):
        xh, _ = _rms(x_ref[...])
        h = (xh * g_ref[...]).astype(BF16)
        h_ref[...] = h
        proj = jnp.concatenate([_dot(h, w_ref[j]) for j in range(N_DEV)], axis=1)
        _spread(proj[:, 0:512] * (HEAD_DIM ** -0.5), scr, (q1, q4, q16), BF16)
        _spread(proj[:, 512:1024], scr, (k1, k4, k16), BF16)
        _spread(proj[:, 1024:1536], scr, (v1, v4, v16), BF16)
        gb_ref[...] = proj[:, 1536:2048]
        gc_ref[...] = proj[:, 2048:2560]
        xi_ref[...] = proj[:, 2560:3072]

    row = lambda n: pl.BlockSpec((TM, n), lambda i: (i, 0))
    res = pl.pallas_call(
        body, name="rms_proj", grid=(s // TM,),
        out_shape=[jax.ShapeDtypeStruct((s, D_MODEL), BF16)] + _class_shapes(s, 512, BF16) * 3
        + [jax.ShapeDtypeStruct((s, 512), F32)] * 3,
        in_specs=[row(D_MODEL), _full(g_mix.shape), _full(w_in_g.shape), ANY_SPEC],
        out_specs=[row(D_MODEL)] + _class_specs(512) * 3 + [row(512)] * 3,
        scratch_shapes=[pltpu.VMEM((512 // LANES, TM, LANES), F32)],
        compiler_params=_cparams(1),
    )(x, g_mix, w_in_g, dep)
    return res[0], res[1:4], res[4:7], res[7:10], res[10], res[11], res[12]


def _pair_split(x2):
    lane = lax.broadcasted_iota(jnp.int32, x2.shape, 1)
    zero = jnp.zeros_like(x2)
    return jnp.where(lane < HEAD_DIM, x2, zero), jnp.where(lane >= HEAD_DIM, x2, zero)


def _pair_join(even, odd):
    lane = lax.broadcasted_iota(jnp.int32, (even.shape[0], LANES), 1)
    return jnp.where(lane < HEAD_DIM, even, odd)


def _swa_steps(qc, dil):
    n128 = qc.shape[1] // WIN
    nsub = min(SWA_BLOCKS, n128)
    nb = n128 // nsub
    ncls = min(dil, SWA_BLOCKS // nsub) if nb == 1 else 1
    return nsub, nb, ncls


def _swa_fwd(qc, kc, vc, bias, dil, dep):
    nsub, nb, ncls = _swa_steps(qc, dil)
    whole = nb == 1

    def body(q_ref, kp_ref, kc_ref, vp_ref, vc_ref, b_ref, dep_ref, o_ref, lse_ref, s_scr, p_scr):
        no_prev = (pl.program_id(1) == 0) & (lax.broadcasted_iota(jnp.int32, (WIN, 2 * WIN), 1) < WIN)
        pairs = [slice(a * LANES, (a + 1) * LANES) for a in range(N_HEADS // 2)]
        for c, t in [(c, t) for c in range(ncls) for t in range(nsub)]:
            i = c * nsub + t
            rows = slice(t * WIN, (t + 1) * WIN)
            alone = whole and t == 0
            cols = slice(WIN, 2 * WIN) if alone else slice(0, 2 * WIN)

            def keys(prev_ref, cur_ref, sl):
                if alone:
                    return cur_ref[c, rows, sl]
                if t == 0:
                    return jnp.concatenate([prev_ref[c, :, sl], cur_ref[c, rows, sl]], axis=0)
                return cur_ref[c, (t - 1) * WIN:(t + 1) * WIN, sl]

            for a, sl in enumerate(pairs):
                k2 = keys(kp_ref, kc_ref, sl)
                for e, qh in enumerate(_pair_split(q_ref[c, rows, sl])):
                    s_scr[i, 2 * a + e, :, cols] = _dot_nt(qh, k2)
            den, lse = [], []
            for h in range(N_HEADS):
                lg = s_scr[i, h, :, cols] + b_ref[h, :, cols]
                if t == 0 and not whole:
                    lg = jnp.where(no_prev, -jnp.inf, lg)
                m = jnp.max(lg, axis=-1, keepdims=True)
                p = jnp.exp(lg - m)
                den.append(jnp.sum(p, axis=-1, keepdims=True))
                p_scr[i, h, :, cols] = p.astype(BF16)
                lse.append(m + jnp.log(den[h]))
            for a, sl in enumerate(pairs):
                v_even, v_odd = _pair_split(keys(vp_ref, vc_ref, sl))
                o2 = _dot(p_scr[i, 2 * a, :, cols], v_even) + _dot(p_scr[i, 2 * a + 1, :, cols], v_odd)
                o_ref[c, rows, sl] = o2 / _pair_join(den[2 * a], den[2 * a + 1])
                lse_ref[c, rows, sl] = _pair_join(lse[2 * a], lse[2 * a + 1])

    cur = pl.BlockSpec((ncls, nsub * WIN, 512), lambda r, b: (r, b, 0))
    prev = pl.BlockSpec((ncls, WIN, 512), lambda r, b: (r, jnp.maximum(nsub * b - 1, 0), 0))
    wide = (ncls * nsub, N_HEADS, WIN, 2 * WIN)
    return pl.pallas_call(
        body, name=f"swa_fwd_d{dil}", grid=(dil // ncls, nb),
        out_shape=[jax.ShapeDtypeStruct(qc.shape, F32)] * 2,
        in_specs=[cur, prev, cur, prev, cur, _full(bias.shape), ANY_SPEC],
        out_specs=[cur] * 2,
        scratch_shapes=[pltpu.VMEM(wide, F32), pltpu.VMEM(wide, BF16)],
        compiler_params=_cparams(2),
    )(qc, kc, kc, vc, vc, bias, dep)


def _mix_out(branches, gb, gc, xi, x, w_sc, g_a, g_c, w_out):
    s = x.shape[0]
    tb = TM // SUBLANES

    def body(o1, l1, o4, l4, o16, l16, gb_ref, gc_ref, xi_ref, gch_ref, xih_ref, x_ref, wsc_ref,
             ga_ref, gcv_ref, wout_ref, attn_ref, lse1, lse4, lse16, mixed_ref, x1_ref, scr_a, scr_b, scr_c, scr_d):
        i = pl.program_id(0)
        la, lb, lc = l1[...], _gather_classes(l4, scr_a, 4), _gather_classes(l16, scr_b, 16)
        m_all = jnp.maximum(jnp.maximum(la, lb), lc)
        ea, eb, ec = jnp.exp(la - m_all), jnp.exp(lb - m_all), jnp.exp(lc - m_all)
        den = (ea + eb) + ec
        num = (ea * o1[...] + eb * _gather_classes(o4, scr_c, 4)) + ec * _gather_classes(o16, scr_d, 16)
        attn = num / den
        attn_ref[...] = attn
        _spread(m_all + jnp.log(den), scr_a, (lse1, lse4, lse16), F32)
        xa, _ = _rms(attn)
        u = gc_ref[...] * xi_ref[...]
        uh = jnp.where(i > 0, gch_ref[...] * xih_ref[...], 0.0)
        conv = gb_ref[...] * _causal_conv3(u, uh, wsc_ref)
        xc, _ = _rms(conv)
        mixed = jnp.concatenate([xa * ga_ref[...], xc * gcv_ref[...]], axis=1).astype(BF16)
        mixed_ref[...] = mixed
        x1_ref[...] = x_ref[...] + _dot(mixed, wout_ref[...])

    row = lambda n: pl.BlockSpec((TM, n), lambda i: (i, 0))
    halo = pl.BlockSpec((SUBLANES, 512), lambda i: (jnp.maximum(i * tb - 1, 0), 0))
    cs = _class_specs(512)
    flat = [a for br in branches for a in br]
    res = pl.pallas_call(
        body, name="mix_out", grid=(s // TM,),
        out_shape=[jax.ShapeDtypeStruct((s, 512), F32)] + _class_shapes(s, 512, F32)
        + [jax.ShapeDtypeStruct((s, D_MODEL), BF16), jax.ShapeDtypeStruct((s, D_MODEL), F32)],
        in_specs=[cs[0], cs[0], cs[1], cs[1], cs[2], cs[2], row(512), row(512), row(512), halo, halo,
                  row(D_MODEL), _full(w_sc.shape), _full(g_a.shape), _full(g_c.shape), _full(w_out.shape)],
        out_specs=[row(512)] + cs + [row(D_MODEL), row(D_MODEL)],
        scratch_shapes=[pltpu.VMEM((512 // LANES, TM, LANES), F32)] * 4,
        compiler_params=_cparams(1),
    )(*flat, gb, gc, xi, gc, xi, x, w_sc, g_a, g_c, w_out)
    return res[0], res[1:4], res[4], res[5]


def _mem_kv(mem, g_mem, w_xk, w_xv):
    def body(mem_ref, g_ref, wk_ref, wv_ref, mn_ref, k_ref, v_ref):
        xh, _ = _rms(mem_ref[...])
        mn = (xh * g_ref[...]).astype(BF16)
        mn_ref[...] = mn
        k_ref[...] = _dot(mn, wk_ref[...]).astype(BF16)
        v_ref[...] = _dot(mn, wv_ref[...]).astype(BF16)

    vm = pl.BlockSpec(memory_space=pltpu.VMEM)
    return pl.pallas_call(
        body, name="mem_kv",
        out_shape=[jax.ShapeDtypeStruct(mem.shape, BF16)] * 3,
        in_specs=[vm] * 4, out_specs=[vm] * 3,
        compiler_params=pltpu.CompilerParams(vmem_limit_bytes=VMEM_LIMIT),
    )(mem, g_mem, w_xk, w_xv)


def _xattn_fwd(x1, g, w_xq, k, v, w_xo, dep):
    s = x1.shape[0]

    def body(x1_ref, g_ref, wq_ref, k_ref, v_ref, wo_ref, dep_ref, h2_ref, q_ref, o_ref, x2_ref):
        x1v = x1_ref[...]
        xh, _ = _rms(x1v)
        h2 = (xh * g_ref[...]).astype(BF16)
        h2_ref[...] = h2
        qb = _dot(h2, wq_ref[...]).astype(BF16)
        q_ref[...] = qb
        outs = []
        for h in range(N_MEM_HEADS):
            sl = slice(h * MEM_HEAD_DIM, (h + 1) * MEM_HEAD_DIM)
            lg = _dot_nt(qb[:, sl], k_ref[:, sl]) * (MEM_HEAD_DIM ** -0.5)
            p = jnp.exp(lg - jnp.max(lg, axis=-1, keepdims=True))
            p = p / jnp.sum(p, axis=-1, keepdims=True)
            outs.append(_dot(p.astype(BF16), v_ref[:, sl]))
        o = jnp.concatenate(outs, axis=1).astype(BF16)
        o_ref[...] = o
        x2_ref[...] = x1v + _dot(o, wo_ref[...])

    row = pl.BlockSpec((TM, D_MODEL), lambda i: (i, 0))
    return pl.pallas_call(
        body, name="xattn_fwd", grid=(s // TM,),
        out_shape=[jax.ShapeDtypeStruct((s, D_MODEL), BF16)] * 3 + [jax.ShapeDtypeStruct((s, D_MODEL), F32)],
        in_specs=[row, _full(g.shape), _full(w_xq.shape), _full(k.shape), _full(v.shape), _full(w_xo.shape), ANY_SPEC],
        out_specs=[row] * 4,
        compiler_params=_cparams(1),
    )(x1, g, w_xq, k, v, w_xo, dep)


def _ffn_conv(h_ext, wup_ref, wfc_ref, bfc_ref, j):
    u = _dot_nt(h_ext, wup_ref[j])
    w = wfc_ref[j]
    c = ((pltpu.roll(u, 2, 0) * w[0:1, :] + pltpu.roll(u, 1, 0) * w[1:2, :]) + u * w[2:3, :]) + bfc_ref[j]
    return c[HALO:]


def _ffn_fwd(x2, g, w_up_g, w_fc, b_fc, w_down_g, g_final, target):
    s = x2.shape[0]
    tb = TM_FFN // HALO
    half = N_DEV // 2

    def body(x_ref, xp_ref, g_ref, wup_ref, wfc_ref, bfc_ref, wd_ref, gf_ref, t_ref, h_ref, c_ref, act_ref, dx3_ref,
             loss_ref, dgf_ref):
        i = pl.program_id(0)

        @pl.when(i == 0)
        def _():
            loss_ref[...] = jnp.zeros_like(loss_ref)
            dgf_ref[...] = jnp.zeros_like(dgf_ref)

        x2v = x_ref[...]
        gv = g_ref[...]
        h = (_rms(x2v)[0] * gv).astype(BF16)
        h_ref[...] = h
        hp = jnp.where(i > 0, _rms(xp_ref[...])[0] * gv, 0.0).astype(BF16)
        h_ext = jnp.concatenate([hp, h], axis=0)
        down = jnp.zeros((TM_FFN, D_MODEL), F32)
        for j in range(half):
            cg = _ffn_conv(h_ext, wup_ref, wfc_ref, bfc_ref, j)
            cv = _ffn_conv(h_ext, wup_ref, wfc_ref, bfc_ref, j + half)
            c_ref[j] = cg
            c_ref[j + half] = cv
            a = ((cg * _sigmoid(cg)) * cv).astype(BF16)
            act_ref[j] = a
            down = down + _dot(a, wd_ref[j])
        x3 = x2v + down
        xh, r = _rms(x3)
        gf = gf_ref[...]
        e = xh * gf - t_ref[...]
        loss_ref[...] += 0.5 * jnp.sum(jnp.sum(e * e, axis=1, keepdims=True), axis=0, keepdims=True) / D_MODEL
        dy = e * (1.0 / D_MODEL)
        dgf_ref[0:1, :] += jnp.sum(dy * xh, axis=0, keepdims=True)
        dx3_ref[...] = _rms_bwd(xh, r, gf, dy)

    row = pl.BlockSpec((TM_FFN, D_MODEL), lambda i: (i, 0))
    prev = pl.BlockSpec((HALO, D_MODEL), lambda i: (jnp.maximum(i * tb - 1, 0), 0))
    return pl.pallas_call(
        body, name="ffn_fwd", grid=(s // TM_FFN,),
        out_shape=[jax.ShapeDtypeStruct((s, D_MODEL), BF16), jax.ShapeDtypeStruct((N_DEV, s, UP_CHUNK), F32),
                   jax.ShapeDtypeStruct((half, s, UP_CHUNK), BF16),
                   jax.ShapeDtypeStruct((s, D_MODEL), F32), jax.ShapeDtypeStruct((SUBLANES, 128), F32),
                   jax.ShapeDtypeStruct((SUBLANES, D_MODEL), F32)],
        in_specs=[row, prev, _full(g.shape), _resident(w_up_g.shape), _full(w_fc.shape), _full(b_fc.shape),
                  _resident(w_down_g.shape), _full(g_final.shape), row],
        out_specs=[row, pl.BlockSpec((N_DEV, TM_FFN, UP_CHUNK), lambda i: (0, i, 0)),
                   pl.BlockSpec((half, TM_FFN, UP_CHUNK), lambda i: (0, i, 0)), row,
                   _full((SUBLANES, 128)), _full((SUBLANES, D_MODEL))],
        compiler_params=_cparams(1),
    )(x2, x2, g, w_up_g, w_fc, b_fc, w_down_g, g_final, target)


def _ffn_bwd(dx3, h3, conv, x2, g, w_up_g, w_fc, w_down_g):
    s = x2.shape[0]
    tb = TM_FFN // HALO
    last = s // HALO - 1
    n_tiles = s // TM_FFN
    half = N_DEV // 2
    n_ext = TM_FFN + HALO

    def body(dx_ref, dxn_ref, h_ref, c_ref, cn_ref, x2_ref, g_ref, wup_ref, wfc_ref, wd_ref,
             dup_ref, dx2_ref, dg_ref, dwfc_ref, dbfc_ref):
        i = pl.program_id(0)

        @pl.when(i == 0)
        def _():
            dg_ref[...] = jnp.zeros_like(dg_ref)
            dwfc_ref[...] = jnp.zeros_like(dwfc_ref)
            dbfc_ref[...] = jnp.zeros_like(dbfc_ref)

        dxv = dx_ref[...]
        dxn = jnp.where(i < n_tiles - 1, dxn_ref[...], 0.0)
        dx_ext = jnp.concatenate([dxv, dxn], axis=0).astype(BF16)
        h = h_ref[...]
        dh = jnp.zeros((TM_FFN, D_MODEL), F32)
        for j in range(half):
            cg = jnp.concatenate([c_ref[j], cn_ref[j]], axis=0)
            cv = jnp.concatenate([c_ref[j + half], cn_ref[j + half]], axis=0)
            dact = _dot_nt(dx_ext, wd_ref[j])
            sg = _sigmoid(cg)
            parts = ((j + half, dact * (cg * sg)), (j, (dact * cv) * (sg * (1.0 + cg * (1.0 - sg)))))
            for jj, dc in parts:
                u = _dot_nt(h, wup_ref[jj])
                dc0, dc1, dc2 = dc[:TM_FFN], pltpu.roll(dc, n_ext - 1, 0)[:TM_FFN], pltpu.roll(dc, n_ext - 2, 0)[:TM_FFN]
                dbfc_ref[jj:jj + 1, :] += jnp.sum(dc0, axis=0, keepdims=True)
                dwfc_ref[0, jj:jj + 1, :] += jnp.sum(dc2 * u, axis=0, keepdims=True)
                dwfc_ref[1, jj:jj + 1, :] += jnp.sum(dc1 * u, axis=0, keepdims=True)
                dwfc_ref[2, jj:jj + 1, :] += jnp.sum(dc0 * u, axis=0, keepdims=True)
                w = wfc_ref[jj]
                du = ((dc0 * w[2:3, :] + dc1 * w[1:2, :]) + dc2 * w[0:1, :]).astype(BF16)
                dup_ref[jj] = du
                dh = dh + _dot(du, wup_ref[jj])
        xh, r = _rms(x2_ref[...])
        dg_ref[0:1, :] += jnp.sum(dh * xh, axis=0, keepdims=True)
        dx2_ref[...] = dxv + _rms_bwd(xh, r, g_ref[...], dh)

    row = pl.BlockSpec((TM_FFN, D_MODEL), lambda i: (i, 0))
    nxt = pl.BlockSpec((HALO, D_MODEL), lambda i: (jnp.minimum((i + 1) * tb, last), 0))
    cur_c = pl.BlockSpec((N_DEV, TM_FFN, UP_CHUNK), lambda i: (0, i, 0))
    nxt_c = pl.BlockSpec((N_DEV, HALO, UP_CHUNK), lambda i: (0, jnp.minimum((i + 1) * tb, last), 0))
    return pl.pallas_call(
        body, name="ffn_bwd", grid=(n_tiles,),
        out_shape=[jax.ShapeDtypeStruct((N_DEV, s, UP_CHUNK), BF16), jax.ShapeDtypeStruct((s, D_MODEL), F32),
                   jax.ShapeDtypeStruct((SUBLANES, D_MODEL), F32), jax.ShapeDtypeStruct((3, N_DEV, UP_CHUNK), F32),
                   jax.ShapeDtypeStruct((N_DEV, UP_CHUNK), F32)],
        in_specs=[row, nxt, row, cur_c, nxt_c, row, _full(g.shape), _resident(w_up_g.shape), _full(w_fc.shape),
                  _resident(w_down_g.shape)],
        out_specs=[cur_c, row, _full((SUBLANES, D_MODEL)), _full((3, N_DEV, UP_CHUNK)), _full((N_DEV, UP_CHUNK))],
        compiler_params=_cparams(1),
    )(dx3, dx3, h3, conv, conv, x2, g, w_up_g, w_fc, w_down_g)


def _xattn_bwd(dx2, o, q, k, v, w_xo, w_xq, x1, g, dep):
    s = x1.shape[0]

    def body(dx2_ref, o_ref, q_ref, k_ref, v_ref, wo_ref, wq_ref, x1_ref, g_ref, dep_ref, dq_ref, dx1_ref, dk_ref,
             dv_ref, dg_ref):
        @pl.when(pl.program_id(0) == 0)
        def _():
            dk_ref[...] = jnp.zeros_like(dk_ref)
            dv_ref[...] = jnp.zeros_like(dv_ref)
            dg_ref[...] = jnp.zeros_like(dg_ref)

        dx2v = dx2_ref[...]
        do = _dot_nt(dx2v.astype(BF16), wo_ref[...])
        dqs = []
        for h in range(N_MEM_HEADS):
            sl = slice(h * MEM_HEAD_DIM, (h + 1) * MEM_HEAD_DIM)
            qh, kh, vh = q_ref[:, sl], k_ref[:, sl], v_ref[:, sl]
            lg = _dot_nt(qh, kh) * (MEM_HEAD_DIM ** -0.5)
            p = jnp.exp(lg - jnp.max(lg, axis=-1, keepdims=True))
            p = p / jnp.sum(p, axis=-1, keepdims=True)
            doh = do[:, sl].astype(BF16)
            dp = _dot_nt(doh, vh)
            ds = (p * (dp - jnp.sum(p * dp, axis=-1, keepdims=True)) * (MEM_HEAD_DIM ** -0.5)).astype(BF16)
            dqs.append(_dot(ds, kh))
            dk_ref[:, sl] += _dot_tn(ds, qh)
            dv_ref[:, sl] += _dot_tn(p.astype(BF16), doh)
        dq = jnp.concatenate(dqs, axis=1).astype(BF16)
        dq_ref[...] = dq
        dh2 = _dot_nt(dq, wq_ref[...])
        xh, r = _rms(x1_ref[...])
        dg_ref[0:1, :] += jnp.sum(dh2 * xh, axis=0, keepdims=True)
        dx1_ref[...] = dx2v + _rms_bwd(xh, r, g_ref[...], dh2)

    row = pl.BlockSpec((TM, D_MODEL), lambda i: (i, 0))
    return pl.pallas_call(
        body, name="xattn_bwd", grid=(s // TM,),
        out_shape=[jax.ShapeDtypeStruct((s, D_MODEL), BF16), jax.ShapeDtypeStruct((s, D_MODEL), F32),
                   jax.ShapeDtypeStruct(k.shape, F32), jax.ShapeDtypeStruct(k.shape, F32),
                   jax.ShapeDtypeStruct((SUBLANES, D_MODEL), F32)],
        in_specs=[row, row, row, _full(k.shape), _full(v.shape), _full(w_xo.shape), _full(w_xq.shape), row,
                  _full(g.shape), ANY_SPEC],
        out_specs=[row, row, _full(k.shape), _full(k.shape), _full((SUBLANES, D_MODEL))],
        compiler_params=_cparams(1),
    )(dx2, o, q, k, v, w_xo, w_xq, x1, g, dep)


def _mem_kv_bwd(dk, dv, mem_n, mem, w_xk, w_xv):
    def body(dk_ref, dv_ref, mn_ref, mem_ref, wk_ref, wv_ref, dwk_ref, dwv_ref, dg_ref):
        dkb, dvb = dk_ref[...].astype(BF16), dv_ref[...].astype(BF16)
        mn = mn_ref[...]
        dwk_ref[...] = _dot_tn(mn, dkb).astype(BF16)
        dwv_ref[...] = _dot_tn(mn, dvb).astype(BF16)
        dmn = _dot_nt(dkb, wk_ref[...]) + _dot_nt(dvb, wv_ref[...])
        xh, _ = _rms(mem_ref[...])
        dg_ref[...] = jnp.zeros_like(dg_ref)
        dg_ref[0:1, :] = jnp.sum(dmn * xh, axis=0, keepdims=True)

    vm = pl.BlockSpec(memory_space=pltpu.VMEM)
    return pl.pallas_call(
        body, name="mem_kv_bwd",
        out_shape=[jax.ShapeDtypeStruct(w_xk.shape, BF16), jax.ShapeDtypeStruct(w_xv.shape, BF16),
                   jax.ShapeDtypeStruct((SUBLANES, D_MODEL), F32)],
        in_specs=[vm] * 6, out_specs=[vm] * 3,
        compiler_params=pltpu.CompilerParams(vmem_limit_bytes=VMEM_LIMIT),
    )(dk, dv, mem_n, mem, w_xk, w_xv)


def _mix_out_bwd(dx1, w_out, attn, gb, gc, xi, w_sc, g_a, g_c, dep):
    s = dx1.shape[0]
    tb = TM // SUBLANES

    def body(dx1_ref, wout_ref, attn_ref, gb_ref, gc_ref, xi_ref, gch_ref, xih_ref, wsc_ref, ga_ref, gcv_ref, dep_ref,
             da1, da4, da16, dd1, dd4, dd16, dgb_ref, dcv_ref, dga_ref, dgc_ref, dwsc_ref, scr):
        i = pl.program_id(0)

        @pl.when(i == 0)
        def _():
            dga_ref[...] = jnp.zeros_like(dga_ref)
            dgc_ref[...] = jnp.zeros_like(dgc_ref)
            dwsc_ref[...] = jnp.zeros_like(dwsc_ref)

        dmixed = _dot_nt(dx1_ref[...].astype(BF16), wout_ref[...])
        da, dcn = dmixed[:, :ATTN_W], dmixed[:, ATTN_W:]
        attn = attn_ref[...]
        xa, ra = _rms(attn)
        dga_ref[0:1, :] += jnp.sum(da * xa, axis=0, keepdims=True)
        dattn = _rms_bwd(xa, ra, ga_ref[...], da)
        _spread(dattn, scr, (da1, da4, da16), BF16)
        prod = dattn * attn
        dd = jnp.concatenate(
            [jnp.broadcast_to(jnp.sum(prod[:, h * HEAD_DIM:(h + 1) * HEAD_DIM], axis=-1, keepdims=True),
                              (TM, HEAD_DIM)) for h in range(N_HEADS)], axis=1)
        _spread(dd, scr, (dd1, dd4, dd16), F32)
        gbv = gb_ref[...]
        u = gc_ref[...] * xi_ref[...]
        uh = jnp.where(i > 0, gch_ref[...] * xih_ref[...], 0.0)
        u2, u1 = _shift_down(u, uh, 2), _shift_down(u, uh, 1)
        cv = (u2 * wsc_ref[0:1, :] + u1 * wsc_ref[1:2, :]) + u * wsc_ref[2:3, :]
        xc, rc = _rms(gbv * cv)
        dgc_ref[0:1, :] += jnp.sum(dcn * xc, axis=0, keepdims=True)
        dconv = _rms_bwd(xc, rc, gcv_ref[...], dcn)
        dgb_ref[...] = (dconv * cv).astype(BF16)
        dcv = dconv * gbv
        dcv_ref[...] = dcv
        dwsc_ref[0:1, :] += jnp.sum(dcv * u2, axis=0, keepdims=True)
        dwsc_ref[1:2, :] += jnp.sum(dcv * u1, axis=0, keepdims=True)
        dwsc_ref[2:3, :] += jnp.sum(dcv * u, axis=0, keepdims=True)

    row = lambda n: pl.BlockSpec((TM, n), lambda i: (i, 0))
    halo = pl.BlockSpec((SUBLANES, 512), lambda i: (jnp.maximum(i * tb - 1, 0), 0))
    acc = _full((SUBLANES, 512))
    res = pl.pallas_call(
        body, name="mix_out_bwd", grid=(s // TM,),
        out_shape=_class_shapes(s, 512, BF16) + _class_shapes(s, 512, F32)
        + [jax.ShapeDtypeStruct((s, 512), BF16), jax.ShapeDtypeStruct((s, 512), F32)]
        + [jax.ShapeDtypeStruct((SUBLANES, 512), F32)] * 3,
        in_specs=[row(D_MODEL), _full(w_out.shape), row(512), row(512), row(512), row(512), halo, halo,
                  _full(w_sc.shape), _full(g_a.shape), _full(g_c.shape), ANY_SPEC],
        out_specs=_class_specs(512) * 2 + [row(512)] * 2 + [acc] * 3,
        scratch_shapes=[pltpu.VMEM((512 // LANES, TM, LANES), F32)],
        compiler_params=_cparams(1),
    )(dx1, w_out, attn, gb, gc, xi, gc, xi, w_sc, g_a, g_c, dep)
    return res[0:3], res[3:6], res[6], res[7], res[8], res[9], res[10]


def _swa_bwd(qc, kc, vc, doc, lsec, ddc, bias, dil, dep):
    nsub, nb, ncls = _swa_steps(qc, dil)
    n128 = nsub * nb
    whole = nb == 1

    def body(q_ref, qn_ref, kp_ref, kc_ref, vp_ref, vc_ref, do_ref, don_ref, lse_ref, lsen_ref, dd_ref, ddn_ref,
             b_ref, dep_ref, dq_ref, dk_ref, dv_ref, db_ref, s_scr, dp_scr, sn_scr, dpn_scr, ds_scr, p_scr, dsn_scr,
             pn_scr):
        r, b = pl.program_id(0), pl.program_id(1)

        @pl.when((r == 0) & (b == 0))
        def _():
            db_ref[...] = jnp.zeros_like(db_ref)

        pairs = [slice(a * LANES, (a + 1) * LANES) for a in range(N_HEADS // 2)]
        blk = [slice(t * WIN, (t + 1) * WIN) for t in range(nsub)]
        last = blk[nsub - 1]
        cols = lambda t: slice(WIN, 2 * WIN) if whole and t == 0 else slice(0, 2 * WIN)
        of_head = lambda ref, c, rows, h: ref[c, rows, h * HEAD_DIM:h * HEAD_DIM + 1]
        no_prev = (b == 0) & (lax.broadcasted_iota(jnp.int32, (WIN, 2 * WIN), 1) < WIN)

        def keys(prev_ref, cur_ref, c, t, sl):
            if whole and t == 0:
                return cur_ref[c, blk[0], sl]
            if t == 0:
                return jnp.concatenate([prev_ref[c, :, sl], cur_ref[c, blk[0], sl]], axis=0)
            return cur_ref[c, (t - 1) * WIN:(t + 1) * WIN, sl]

        for a, sl in enumerate(pairs):
            for c, t in [(c, t) for c in range(ncls) for t in range(nsub)]:
                k2, v2 = keys(kp_ref, kc_ref, c, t, sl), keys(vp_ref, vc_ref, c, t, sl)
                q_eo = _pair_split(q_ref[c, blk[t], sl])
                do_eo = _pair_split(do_ref[c, blk[t], sl].astype(BF16))
                for e in range(2):
                    s_scr[c * nsub + t, 2 * a + e, :, cols(t)] = _dot_nt(q_eo[e], k2)
                    dp_scr[c * nsub + t, 2 * a + e, :, cols(t)] = _dot_nt(do_eo[e], v2)
            if not whole:
                qn_eo = _pair_split(qn_ref[0, :, sl])
                don_eo = _pair_split(don_ref[0, :, sl].astype(BF16))
                for e in range(2):
                    sn_scr[2 * a + e] = _dot_nt(qn_eo[e], kc_ref[0, last, sl])
                    dpn_scr[2 * a + e] = _dot_nt(don_eo[e], vc_ref[0, last, sl])
        for c, t, h in [(c, t, h) for c in range(ncls) for t in range(nsub) for h in range(N_HEADS)]:
            i, cl = c * nsub + t, cols(t)
            lg = s_scr[i, h, :, cl] + b_ref[h, :, cl]
            if t == 0 and not whole:
                lg = jnp.where(no_prev, -jnp.inf, lg)
            p = jnp.exp(lg - of_head(lse_ref, c, blk[t], h))
            ds = p * (dp_scr[i, h, :, cl] - of_head(dd_ref, c, blk[t], h))
            db_ref[h, :, cl] += ds
            ds_scr[i, h, :, cl] = ds.astype(BF16)
            p_scr[i, h, :, cl] = p.astype(BF16)
        if not whole:
            every = slice(0, WIN)
            for h in range(N_HEADS):
                lgn = jnp.where(b + 1 < nb, sn_scr[h] + b_ref[h, :, :WIN], -jnp.inf)
                pn = jnp.exp(lgn - of_head(lsen_ref, 0, every, h))
                dsn_scr[h] = (pn * (dpn_scr[h] - of_head(ddn_ref, 0, every, h))).astype(BF16)
                pn_scr[h] = pn.astype(BF16)
        for a, sl in enumerate(pairs):
            for c in range(ncls):
                q_eo = [_pair_split(q_ref[c, blk[t], sl]) for t in range(nsub)]
                do_eo = [_pair_split(do_ref[c, blk[t], sl].astype(BF16)) for t in range(nsub)]
                if not whole:
                    q_eo.append(_pair_split(qn_ref[0, :, sl]))
                    do_eo.append(_pair_split(don_ref[0, :, sl].astype(BF16)))
                for t in range(nsub):
                    i = c * nsub + t
                    k_eo = _pair_split(keys(kp_ref, kc_ref, c, t, sl))
                    dq, dk, dv = None, None, None
                    for e in range(2):
                        h = 2 * a + e
                        terms = [_dot(ds_scr[i, h, :, cols(t)], k_eo[e]),
                                 _dot_tn(ds_scr[i, h, :, WIN:], q_eo[t][e]),
                                 _dot_tn(p_scr[i, h, :, WIN:], do_eo[t][e])]
                        if t + 1 < nsub or not whole:
                            ds_next = ds_scr[i + 1, h, :, :WIN] if t + 1 < nsub else dsn_scr[h]
                            p_next = p_scr[i + 1, h, :, :WIN] if t + 1 < nsub else pn_scr[h]
                            terms[1] += _dot_tn(ds_next, q_eo[t + 1][e])
                            terms[2] += _dot_tn(p_next, do_eo[t + 1][e])
                        dq, dk, dv = terms if e == 0 else (dq + terms[0], dk + terms[1], dv + terms[2])
                    dq_ref[c, blk[t], sl] = dq.astype(BF16)
                    dk_ref[c, blk[t], sl] = dk.astype(BF16)
                    dv_ref[c, blk[t], sl] = dv.astype(BF16)

    cur = pl.BlockSpec((ncls, nsub * WIN, 512), lambda r, b: (r, b, 0))
    prev = pl.BlockSpec((ncls, WIN, 512), lambda r, b: (r, jnp.maximum(nsub * b - 1, 0), 0))
    nxt = pl.BlockSpec((ncls, WIN, 512), lambda r, b: (r, jnp.minimum(nsub * b + nsub, n128 - 1), 0))
    wide, narrow = (ncls * nsub, N_HEADS, WIN, 2 * WIN), (N_HEADS, WIN, WIN)
    return pl.pallas_call(
        body, name=f"swa_bwd_d{dil}", grid=(dil // ncls, nb),
        out_shape=[jax.ShapeDtypeStruct(qc.shape, BF16)] * 3 + [jax.ShapeDtypeStruct(bias.shape, F32)],
        in_specs=[cur, nxt, prev, cur, prev, cur, cur, nxt, cur, nxt, cur, nxt, _full(bias.shape), ANY_SPEC],
        out_specs=[cur] * 3 + [_full(bias.shape)],
        scratch_shapes=[pltpu.VMEM(wide, F32), pltpu.VMEM(wide, F32), pltpu.VMEM(narrow, F32),
                        pltpu.VMEM(narrow, F32), pltpu.VMEM(wide, BF16), pltpu.VMEM(wide, BF16),
                        pltpu.VMEM(narrow, BF16), pltpu.VMEM(narrow, BF16)],
        compiler_params=_cparams(2),
    )(qc, qc, kc, kc, vc, vc, doc, doc, lsec, lsec, ddc, ddc, bias, dep)


def _in_proj_bwd(dqs, dks, dvs, dgb, dcv, gc, xi, w_sc, w_in_g, x, g_mix, dx1):
    s = x.shape[0]
    tb = TM // SUBLANES
    last = s // SUBLANES - 1
    n_tiles = s // TM

    def body(dq1, dq4, dq16, dk1, dk4, dk16, dv1, dv4, dv16, dgb_ref, dcv_ref, dcvn_ref, gc_ref, xi_ref, wsc_ref,
             win_ref, x_ref, g_ref, dx1_ref, dproj_ref, gx_ref, dg_ref, scr_a, scr_b):
        i = pl.program_id(0)

        @pl.when(i == 0)
        def _():
            dg_ref[...] = jnp.zeros_like(dg_ref)

        d0 = dcv_ref[...]
        dn = jnp.where(i < n_tiles - 1, dcvn_ref[...], 0.0)
        du = (d0 * wsc_ref[2:3, :] + _shift_up(d0, dn, 1) * wsc_ref[1:2, :]) + _shift_up(d0, dn, 2) * wsc_ref[0:1, :]
        merge = lambda a, b4, b16: ((a[...].astype(F32) + _gather_classes(b4, scr_a, 4))
                                    + _gather_classes(b16, scr_b, 16))
        dq = merge(dq1, dq4, dq16) * (HEAD_DIM ** -0.5)
        dk = merge(dk1, dk4, dk16)
        dv = merge(dv1, dv4, dv16)
        dproj = jnp.concatenate([dq, dk, dv, dgb_ref[...].astype(F32), du * xi_ref[...], du * gc_ref[...]],
                                axis=1).astype(BF16)
        dproj_ref[...] = dproj
        dh = jnp.zeros((TM, D_MODEL), F32)
        for j in range(N_DEV):
            dh = dh + _dot_nt(dproj[:, j * IN_CHUNK:(j + 1) * IN_CHUNK], win_ref[j])
        xh, r = _rms(x_ref[...])
        dg_ref[0:1, :] += jnp.sum(dh * xh, axis=0, keepdims=True)
        gx_ref[...] = dx1_ref[...] + _rms_bwd(xh, r, g_ref[...], dh)

    row = lambda n: pl.BlockSpec((TM, n), lambda i: (i, 0))
    nxt = pl.BlockSpec((SUBLANES, 512), lambda i: (jnp.minimum((i + 1) * tb, last), 0))
    return pl.pallas_call(
        body, name="in_proj_bwd", grid=(n_tiles,),
        out_shape=[jax.ShapeDtypeStruct((s, IN_COLS), BF16), jax.ShapeDtypeStruct((s, D_MODEL), F32),
                   jax.ShapeDtypeStruct((SUBLANES, D_MODEL), F32)],
        in_specs=_class_specs(512) * 3 + [row(512), row(512), nxt, row(512), row(512), _full(w_sc.shape),
                                          _full(w_in_g.shape), row(D_MODEL), _full(g_mix.shape), row(D_MODEL)],
        out_specs=[row(IN_COLS), row(D_MODEL), _full((SUBLANES, D_MODEL))],
        scratch_shapes=[pltpu.VMEM((512 // LANES, TM, LANES), F32)] * 2,
        compiler_params=_cparams(1),
    )(*dqs, *dks, *dvs, dgb, dcv, dcv, gc, xi, w_sc, w_in_g, x, g_mix, dx1)


def _dw(a, b, dep, name, a_chunked=False, b_chunked=False, n_chunks=1, chunk_cols=None):
    ts = TS_DW if (a_chunked or b_chunked or chunk_cols) else TS_DW // 2
    if a_chunked:
        nj, s, kk = a.shape
        nn = b.shape[1]
        a_spec = pl.BlockSpec((1, ts, kk), lambda j, t: (j, t, 0))
        b_spec = pl.BlockSpec((ts, nn), lambda j, t: (t, 0))
    elif b_chunked:
        nj, s, nn = b.shape
        kk = a.shape[1]
        a_spec = pl.BlockSpec((ts, kk), lambda j, t: (t, 0))
        b_spec = pl.BlockSpec((1, ts, nn), lambda j, t: (j, t, 0))
    else:
        s, kk = a.shape
        nj, nn = (n_chunks, chunk_cols) if chunk_cols else (1, b.shape[1])
        a_spec = pl.BlockSpec((ts, kk), lambda j, t: (t, 0))
        b_spec = pl.BlockSpec((ts, nn), lambda j, t: (t, j))
    n_steps = s // ts

    def body(a_ref, b_ref, dep_ref, o_ref, acc):
        t = pl.program_id(1)

        @pl.when(t == 0)
        def _():
            acc[...] = jnp.zeros_like(acc)

        av = (a_ref[0] if a_chunked else a_ref[...]).astype(BF16)
        bv = (b_ref[0] if b_chunked else b_ref[...]).astype(BF16)
        acc[...] += _dot_tn(av, bv)

        @pl.when(t == n_steps - 1)
        def _():
            o_ref[0] = acc[...].astype(BF16)

    return pl.pallas_call(
        body, name=name, grid=(nj, n_steps),
        out_shape=jax.ShapeDtypeStruct((nj, kk, nn), BF16),
        in_specs=[a_spec, b_spec, ANY_SPEC],
        out_specs=pl.BlockSpec((1, kk, nn), lambda j, t: (j, 0, 0)),
        scratch_shapes=[pltpu.VMEM((kk, nn), F32)],
        compiler_params=_cparams(2),
    )(a, b, dep)


def _adamw_math(w, g, m, v):
    m2 = ADAM_B1 * m + (1.0 - ADAM_B1) * g
    v2 = ADAM_B2 * v + (1.0 - ADAM_B2) * (g * g)
    m_hat = m2 / (1.0 - ADAM_B1 ** ADAM_STEP)
    v_hat = v2 / (1.0 - ADAM_B2 ** ADAM_STEP)
    delta = -ADAM_LR * (m_hat / (jnp.sqrt(v_hat) + ADAM_EPS) + ADAM_WD * w)
    return delta, m2, v2


def _sum_parts(me, own, p_ref):
    g = None
    for i in range(N_DEV):
        part = jnp.where(me == i, own.astype(F32), p_ref[i].astype(F32))
        g = part if g is None else g + part
    return g


def _adamw_big(name, w, sent, parts, m, v, me_arr):
    rr, cc = w.shape
    tr = rr // 4 if rr >= 512 else rr

    def body(me_ref, w_ref, own_ref, p_ref, m_ref, v_ref, g_ref, d_ref, nm_ref, nv_ref):
        g = own_ref[0].astype(F32)
        for k in range(1, N_DEV):
            g = g + p_ref[(me_ref[0] + k) % N_DEV].astype(F32)
        g_ref[...] = g
        d_ref[...], nm_ref[...], nv_ref[...] = _adamw_math(w_ref[...], g, m_ref[...], v_ref[...])

    row = pl.BlockSpec((tr, cc), lambda i, me: (i, 0))
    return pl.pallas_call(
        body, name=name,
        grid_spec=pltpu.PrefetchScalarGridSpec(
            num_scalar_prefetch=1, grid=(rr // tr,),
            in_specs=[row, pl.BlockSpec((1, tr, cc), lambda i, me: (me[0], i, 0)),
                      pl.BlockSpec((N_DEV, tr, cc), lambda i, me: (0, i, 0)), row, row],
            out_specs=[row] * 4),
        out_shape=[jax.ShapeDtypeStruct((rr, cc), F32)] * 4,
        compiler_params=_cparams(1),
    )(me_arr, w, sent, parts, m, v)


def _small_slices():
    return [
        (slice(ROW_RELB, ROW_RELB + 8), slice(0, N_BUCKETS)),
        (slice(ROW_GMIX, ROW_GMIX + 1), slice(0, D_MODEL)),
        (slice(ROW_GAC, ROW_GAC + 1), slice(0, ATTN_W)),
        (slice(ROW_GAC, ROW_GAC + 1), slice(ATTN_W, D_MODEL)),
        (slice(ROW_GXATTN, ROW_GXATTN + 1), slice(0, D_MODEL)),
        (slice(ROW_GMEM, ROW_GMEM + 1), slice(0, D_MODEL)),
        (slice(ROW_GFFN, ROW_GFFN + 1), slice(0, D_MODEL)),
        (slice(ROW_BFC, ROW_BFC + 8), slice(0, UP_CHUNK)),
        (slice(ROW_GFINAL, ROW_GFINAL + 1), slice(0, D_MODEL)),
    ]


def _adamw_small(own, parts, wmv, me_arr):
    slices = _small_slices()
    n = len(slices)

    def body(*refs):
        me_ref, own_ref, p_ref = refs[:3]
        ins = refs[3:3 + 3 * n]
        g_ref = refs[3 + 3 * n]
        outs = refs[4 + 3 * n:]
        g = _sum_parts(me_ref[0], own_ref[...], p_ref)
        g_ref[...] = g
        for a, (rs, ls) in enumerate(slices):
            ga = g[rs, ls]
            outs[4 * a][...] = ga
            outs[4 * a + 1][...], outs[4 * a + 2][...], outs[4 * a + 3][...] = _adamw_math(
                ins[3 * a][...], ga, ins[3 * a + 1][...], ins[3 * a + 2][...])

    vm = pl.BlockSpec(memory_space=pltpu.VMEM)
    flat = [t for trip in wmv for t in trip]
    out_shape = [jax.ShapeDtypeStruct((SMALL_ROWS, D_MODEL), F32)]
    for w, _, _ in wmv:
        out_shape += [jax.ShapeDtypeStruct(w.shape, F32)] * 4
    res = pl.pallas_call(
        body, name="adamw_small", out_shape=out_shape,
        in_specs=[SMEM_SPEC] + [vm] * (2 + 3 * n), out_specs=[vm] * len(out_shape),
    )(me_arr, own, parts, *flat)
    return res[0], [res[1 + 4 * a:5 + 4 * a] for a in range(n)]


def _adamw_shards(items):
    n = len(items)

    def body(*refs):
        for a in range(n):
            w_ref, g_ref, m_ref, v_ref = refs[4 * a:4 * a + 4]
            d_ref, nm_ref, nv_ref = refs[4 * n + 3 * a:4 * n + 3 * a + 3]
            d_ref[...], nm_ref[...], nv_ref[...] = _adamw_math(w_ref[...], g_ref[...], m_ref[...], v_ref[...])

    vm = pl.BlockSpec(memory_space=pltpu.VMEM)
    out_shape = []
    for w, _, _, _ in items:
        out_shape += [jax.ShapeDtypeStruct(w.shape, F32)] * 3
    res = pl.pallas_call(
        body, name="adamw_shards", out_shape=out_shape, in_specs=[vm] * (4 * n), out_specs=[vm] * (3 * n),
    )(*[t for it in items for t in it])
    return [res[3 * a:3 * a + 3] for a in range(n)]


def _mesh_pos():
    return lax.axis_index("x"), lax.axis_index("y"), lax.axis_index("c")


def _dev_index(p):
    return 4 * p[0] + 2 * p[1] + p[2]


def _all_gather(shards):
    n = len(shards)

    def body(*refs):
        ins, outs = refs[:n], refs[n:2 * n]
        send_sems, recv_sems, loc_sems = refs[2 * n:]
        x, y, c = _mesh_pos()
        me, sib = (x, y, c), (x, y, 1 - c)
        chips = [(1 - x, y), (x, 1 - y), (1 - x, 1 - y)]

        def cp(a, k, block, to, src=None):
            dst = outs[a].at[_dev_index(block)]
            return pltpu.make_async_remote_copy(
                src_ref=dst if src is None else src, dst_ref=dst, send_sem=send_sems.at[a, k],
                recv_sem=recv_sems.at[a, k], device_id=to, device_id_type=MESH)

        mine = [pltpu.make_async_copy(ins[a], outs[a].at[_dev_index(me)], loc_sems.at[a]) for a in range(n)]
        for m_ in mine:
            m_.start()
        first = []
        for a in range(n):
            first.append(cp(a, 0, me, sib, src=ins[a]))
            first += [cp(a, 1 + j, me, (*chip, c), src=ins[a]) for j, chip in enumerate(chips)]
        for f in first:
            f.start()
        passed = []
        for a in range(n):
            for j, chip in enumerate(chips):
                cp(a, 1 + j, (*chip, c), me).wait_recv()
                fwd = cp(a, 4 + j, (*chip, c), sib)
                fwd.start()
                passed.append(fwd)
        for a in range(n):
            cp(a, 0, sib, me).wait_recv()
            for j, chip in enumerate(chips):
                cp(a, 4 + j, (*chip, 1 - c), me).wait_recv()
        for f in first + passed:
            f.wait_send()
        for m_ in mine:
            m_.wait()

    hbm = pl.BlockSpec(memory_space=pltpu.HBM)
    return pl.pallas_call(
        body, name="all_gather_weights",
        out_shape=[jax.ShapeDtypeStruct((N_DEV,) + a.shape, a.dtype) for a in shards],
        in_specs=[hbm] * n, out_specs=[hbm] * n,
        scratch_shapes=[pltpu.SemaphoreType.DMA((n, 7)), pltpu.SemaphoreType.DMA((n, 7)),
                        pltpu.SemaphoreType.DMA((n,))],
    )(*shards)


def _peers():
    x, y, c = _mesh_pos()
    return (x, y, c), [((1 - x) if k & 4 else x, (1 - y) if k & 2 else y, (1 - c) if k & 1 else c)
                       for k in range(1, 8)]


def _exchange_copy(src_ref, land_ref, whole, send_sems, recv_sems, a, k, peer, slot):
    src = src_ref if whole else src_ref.at[_dev_index(peer)]
    return pltpu.make_async_remote_copy(
        src_ref=src, dst_ref=land_ref.at[slot], send_sem=send_sems.at[7 * a + k], recv_sem=recv_sems.at[7 * a + k],
        device_id=peer, device_id_type=MESH)


def _exchange_start(name, srcs, whole, dep):
    n = len(srcs)
    lands = [lax.empty(((N_DEV,) + s.shape) if w else s.shape, s.dtype) for s, w in zip(srcs, whole)]

    def body(*refs):
        src_refs, land_refs = refs[:n], refs[n:2 * n]
        send_sems, recv_sems, token = refs[2 * n + 1], refs[2 * n + 2], refs[-1]
        me, peers = _peers()
        for a in range(n):
            for k, peer in enumerate(peers):
                _exchange_copy(src_refs[a], land_refs[a], whole[a], send_sems, recv_sems, a, k, peer,
                               _dev_index(me)).start()
        token[...] = jnp.zeros_like(token)

    res = pl.pallas_call(
        body, name=name,
        out_shape=(pltpu.SemaphoreType.DMA((7 * n,)), pltpu.SemaphoreType.DMA((7 * n,)),
                   *[pltpu.HBM(a.shape, a.dtype) for a in srcs], *[pltpu.HBM(a.shape, a.dtype) for a in lands],
                   jax.ShapeDtypeStruct((SUBLANES, 128), F32)),
        in_specs=[HBM_SPEC] * (2 * n) + [ANY_SPEC],
        out_specs=(SEM_SPEC, SEM_SPEC, *([HBM_SPEC] * (2 * n)), VMEM_SPEC),
        input_output_aliases={i: 2 + i for i in range(2 * n)},
        compiler_params=pltpu.CompilerParams(has_side_effects=DATAFLOW),
    )(*[pltpu.with_memory_space_constraint(a, pltpu.HBM) for a in srcs],
      *[pltpu.with_memory_space_constraint(a, pltpu.HBM) for a in lands], dep)
    return res[0], res[1], list(res[2:2 + n]), list(res[2 + n:2 + 2 * n]), res[-1]


def _exchange_wait(name, started, whole, after, which=None):
    send_sems, recv_sems, srcs, lands, _ = started
    which = list(range(len(srcs))) if which is None else which
    srcs, lands = [srcs[a] for a in which], [lands[a] for a in which]
    n = len(srcs)

    def body(*refs):
        src_refs, land_refs = refs[:n], refs[n:2 * n]
        send_sems, recv_sems = refs[2 * n], refs[2 * n + 1]
        _, peers = _peers()
        for i, a in enumerate(which):
            for k, peer in enumerate(peers):
                cp = _exchange_copy(src_refs[i], land_refs[i], whole[a], send_sems, recv_sems, a, k, peer,
                                    _dev_index(peer))
                cp.wait_send()
                cp.wait_recv()

    res = pl.pallas_call(
        body, name=name,
        out_shape=[pltpu.HBM(a.shape, a.dtype) for a in srcs + lands],
        in_specs=[HBM_SPEC] * (2 * n) + [SEM_SPEC, SEM_SPEC, ANY_SPEC],
        out_specs=[HBM_SPEC] * (2 * n),
        input_output_aliases={i: i for i in range(2 * n)},
        compiler_params=pltpu.CompilerParams(has_side_effects=DATAFLOW),
    )(*srcs, *lands, send_sems, recv_sems, after)
    return list(res[:n]), list(res[n:])


def _gather_start(name, shards, dep):
    n = len(shards)
    lands = [lax.empty((N_DEV,) + a.shape, a.dtype) for a in shards]

    def body(*refs):
        src_refs, land_refs = refs[:n], refs[n:2 * n]
        send_sems, recv_sems, token = refs[2 * n + 1], refs[2 * n + 2], refs[-1]
        x, y, c = _mesh_pos()
        peers = [(x, y, 1 - c), (1 - x, y, c), (x, 1 - y, c), (1 - x, 1 - y, c)]
        for a in range(n):
            for k, peer in enumerate(peers):
                pltpu.make_async_remote_copy(
                    src_ref=src_refs[a], dst_ref=land_refs[a].at[_dev_index((x, y, c))], send_sem=send_sems.at[4 * a + k],
                    recv_sem=recv_sems.at[4 * a + k], device_id=peer, device_id_type=MESH).start()
        token[...] = jnp.zeros_like(token)

    res = pl.pallas_call(
        body, name=name,
        out_shape=(pltpu.SemaphoreType.DMA((4 * n,)), pltpu.SemaphoreType.DMA((4 * n,)),
                   *[pltpu.HBM(a.shape, a.dtype) for a in shards], *[pltpu.HBM(a.shape, a.dtype) for a in lands],
                   jax.ShapeDtypeStruct((SUBLANES, 128), F32)),
        in_specs=[HBM_SPEC] * (2 * n) + [ANY_SPEC],
        out_specs=(SEM_SPEC, SEM_SPEC, *([HBM_SPEC] * (2 * n)), VMEM_SPEC),
        input_output_aliases={i: 2 + i for i in range(2 * n)},
        compiler_params=pltpu.CompilerParams(has_side_effects=DATAFLOW),
    )(*[pltpu.with_memory_space_constraint(a, pltpu.HBM) for a in shards],
      *[pltpu.with_memory_space_constraint(a, pltpu.HBM) for a in lands], dep)
    return res[0], res[1], list(res[2:2 + n]), list(res[2 + n:2 + 2 * n]), res[-1]


def _gather_forward(name, send_sems, recv_sems, lands, which, after):
    n = len(which)

    def body(*refs):
        land_refs = refs[:n]
        send_sems, recv_sems = refs[n], refs[n + 1]
        fsend, frecv, token = refs[n + 3], refs[n + 4], refs[-1]
        x, y, c = _mesh_pos()
        chips = [(1 - x, y), (x, 1 - y), (1 - x, 1 - y)]
        for i, a in enumerate(which):
            for j, chip in enumerate(chips):
                block = land_refs[i].at[_dev_index((*chip, c))]
                pltpu.make_async_remote_copy(
                    src_ref=block, dst_ref=block, send_sem=send_sems.at[4 * a + 1 + j], recv_sem=recv_sems.at[4 * a + 1 + j],
                    device_id=(*chip, c), device_id_type=MESH).wait_recv()
                pltpu.make_async_remote_copy(
                    src_ref=block, dst_ref=block, send_sem=fsend.at[3 * i + j], recv_sem=frecv.at[3 * i + j],
                    device_id=(x, y, 1 - c), device_id_type=MESH).start()
        token[...] = jnp.zeros_like(token)

    res = pl.pallas_call(
        body, name=name,
        out_shape=(pltpu.SemaphoreType.DMA((3 * n,)), pltpu.SemaphoreType.DMA((3 * n,)),
                   *[pltpu.HBM(a.shape, a.dtype) for a in lands], jax.ShapeDtypeStruct((SUBLANES, 128), F32)),
        in_specs=[HBM_SPEC] * n + [SEM_SPEC, SEM_SPEC, ANY_SPEC],
        out_specs=(SEM_SPEC, SEM_SPEC, *([HBM_SPEC] * n), VMEM_SPEC),
        input_output_aliases={i: 2 + i for i in range(n)},
        compiler_params=pltpu.CompilerParams(has_side_effects=DATAFLOW),
    )(*lands, send_sems, recv_sems, after)
    return res[0], res[1], list(res[2:2 + n]), res[-1]


def _gather_wait(name, send_sems, recv_sems, fsend, frecv, srcs, lands, which, after):
    n = len(which)

    def body(*refs):
        land_refs = refs[n:2 * n]
        send_sems, recv_sems, fsend, frecv = refs[2 * n:2 * n + 4]
        x, y, c = _mesh_pos()
        sib = (x, y, 1 - c)
        chips = [(1 - x, y), (x, 1 - y), (1 - x, 1 - y)]
        for i, a in enumerate(which):
            def cp(slot, ssem, rsem):
                block = land_refs[i].at[_dev_index(slot)]
                return pltpu.make_async_remote_copy(src_ref=block, dst_ref=block, send_sem=ssem, recv_sem=rsem,
                                                    device_id=sib, device_id_type=MESH)
            cp(sib, send_sems.at[4 * a], recv_sems.at[4 * a]).wait_recv()
            for j, chip in enumerate(chips):
                cp((*chip, 1 - c), fsend.at[3 * i + j], frecv.at[3 * i + j]).wait_recv()
            for k in range(4):
                cp(sib, send_sems.at[4 * a + k], recv_sems.at[4 * a + k]).wait_send()
            for j in range(3):
                cp(sib, fsend.at[3 * i + j], frecv.at[3 * i + j]).wait_send()

    res = pl.pallas_call(
        body, name=name,
        out_shape=[pltpu.HBM(a.shape, a.dtype) for a in srcs + lands],
        in_specs=[HBM_SPEC] * (2 * n) + [SEM_SPEC] * 4 + [ANY_SPEC],
        out_specs=[HBM_SPEC] * (2 * n),
        input_output_aliases={i: i for i in range(2 * n)},
        compiler_params=pltpu.CompilerParams(has_side_effects=DATAFLOW),
    )(*srcs, *lands, send_sems, recv_sems, fsend, frecv, after)
    return list(res[n:])


def _local_step(x, mem, target, rel_bias, g_mix, w_in_g, w_sc, g_a, g_c, g_xattn, g_mem, g_ffn, w_fc, b_fc, g_final,
                dep, forward_weights, late_weights, emit, emit_small):
    s = x.shape[0]
    buckets = _bucket_tables()
    bias = _bias_fwd(rel_bias, buckets)

    h1, qs, ks, vs, gb, gc, xi = _rms_proj(x, g_mix, w_in_g, dep)
    qs, ks, vs = ([a[0][None]] + list(a[1:]) for a in (qs, ks, vs))
    group1, group2 = ["w_out", "w_xq", "w_xk", "w_xv", "w_xo"], ["w_up", "w_down"]
    tok = forward_weights(group1, h1)
    branches = []
    for p, dil in enumerate(DILATIONS):
        o_p, lse_p = _swa_fwd(qs[p], ks[p], vs[p], bias[p], dil, tok)
        branches.append([o_p[0], lse_p[0]] if dil == 1 else [o_p, lse_p])
    lw = late_weights(group1, branches[-1][0])
    w_out, w_xq, w_xk, w_xv, w_xo = (lw[n] for n in group1)
    attn, lses, mixed, x1 = _mix_out(branches, gb, gc, xi, x, w_sc, g_a, g_c, w_out)
    tok = forward_weights(group2, x1)
    mem_n, mk, mv = _mem_kv(mem, g_mem, w_xk, w_xv)
    h2, xq, xo, x2 = _xattn_fwd(x1, g_xattn, w_xq, mk, mv, w_xo, tok)
    lw = late_weights(group2, x2)
    w_up_g, w_down_g = lw["w_up"], lw["w_down"]
    h3, conv, act, dx3, loss_acc, dg_final = _ffn_fwd(x2, g_ffn, w_up_g, w_fc, b_fc, w_down_g, g_final, target)

    gw_down = _dw(act, dx3, dep, "dw_down", a_chunked=True)
    dup, dx2, dg_ffn, dw_fc, db_fc = _ffn_bwd(dx3, h3, conv, x2, g_ffn, w_up_g, w_fc, w_down_g)
    gw_up = _dw(dup, h3, dep, "dw_up", a_chunked=True)
    tok = emit(dict(w_down=gw_down, w_up=gw_up))
    dxq, dx1, dmk, dmv, dg_xattn = _xattn_bwd(dx2, xo, xq, mk, mv, w_xo, w_xq, x1, g_xattn, tok)
    gw_xo = _dw(xo, dx2, tok, "dw_xo")[0]
    gw_xq = _dw(h2, dxq, tok, "dw_xq")[0]
    gw_xk, gw_xv, dg_mem = _mem_kv_bwd(dmk, dmv, mem_n, mem, w_xk, w_xv)
    tok = emit(dict(w_xo=gw_xo, w_xq=gw_xq, w_xk=gw_xk, w_xv=gw_xv))
    dattns, dds, dgb, dcv, dg_a, dg_c, dw_sc = _mix_out_bwd(dx1, w_out, attn, gb, gc, xi, w_sc, g_a, g_c, tok)
    first = lambda a: [a[0][None]] + list(a[1:])
    dattns, dds, lses = first(dattns), first(dds), first(lses)
    gw_out = _dw(mixed, dx1, tok, "dw_out")[0]
    tok = emit(dict(w_out=gw_out))
    dqs, dks, dvs, dbias = [], [], [], []
    for p, dil in enumerate(DILATIONS):
        dq_p, dk_p, dv_p, db_p = _swa_bwd(qs[p], ks[p], vs[p], dattns[p], lses[p], dds[p], bias[p], dil, tok)
        dqs.append(dq_p[0] if dil == 1 else dq_p)
        dks.append(dk_p[0] if dil == 1 else dk_p)
        dvs.append(dv_p[0] if dil == 1 else dv_p)
        dbias.append(db_p)
    d_relb = _bias_bwd(jnp.stack(dbias), buckets)
    dproj, grad_x, dg_mix = _in_proj_bwd(dqs, dks, dvs, dgb, dcv, gc, xi, w_sc, w_in_g, x, g_mix, dx1)
    pad = lambda a: jnp.pad(a, ((0, 0), (0, D_MODEL - a.shape[1])))
    small = jnp.concatenate([
        d_relb, dg_mix, dg_xattn, dg_mem, dg_ffn, dg_final, jnp.concatenate([dg_a, dg_c], axis=1),
        pad(dw_sc), pad(db_fc), pad(dw_fc.reshape(3 * N_DEV, UP_CHUNK)), pad(loss_acc)], axis=0)
    tok = emit_small(small)
    gw_in = _dw(h1, dproj, tok, "dw_in", n_chunks=N_DEV, chunk_cols=IN_CHUNK)
    emit(dict(w_in=gw_in))
    return grad_x


def kernel(x, mem, rel_bias, g_mix, w_in, w_short_conv, g_attn_out, g_conv_out, w_out, g_xattn, g_mem, w_xq, w_xk, w_xv, w_xo, g_ffn, w_up, w_ffn_conv, b_ffn_conv, w_down, g_final, loss_target, m_rel_bias, m_g_mix, m_w_in, m_w_short_conv, m_g_attn_out, m_g_conv_out, m_w_out, m_g_xattn, m_g_mem, m_w_xq, m_w_xk, m_w_xv, m_w_xo, m_g_ffn, m_w_up, m_w_ffn_conv, m_b_ffn_conv, m_w_down, m_g_final, v_rel_bias, v_g_mix, v_w_in, v_w_short_conv, v_g_attn_out, v_g_conv_out, v_w_out, v_g_xattn, v_g_mem, v_w_xq, v_w_xk, v_w_xv, v_w_xo, v_g_ffn, v_w_up, v_w_ffn_conv, v_b_ffn_conv, v_w_down, v_g_final):
    me = _dev_index(_mesh_pos())
    me_arr = me.reshape(1).astype(jnp.int32)

    big_names = ["w_in", "w_out", "w_xq", "w_xk", "w_xv", "w_xo", "w_up", "w_down"]
    late_names = big_names[1:]
    big_w = dict(w_in=w_in[0], w_out=w_out[0], w_xq=w_xq[0], w_xk=w_xk[0], w_xv=w_xv[0], w_xo=w_xo[0],
                 w_up=w_up[0].T, w_down=w_down[0])
    big_m = dict(w_in=m_w_in[0], w_out=m_w_out[0], w_xq=m_w_xq[0], w_xk=m_w_xk[0], w_xv=m_w_xv[0], w_xo=m_w_xo[0],
                 w_up=m_w_up[0].T, w_down=m_w_down[0])
    big_v = dict(w_in=v_w_in[0], w_out=v_w_out[0], w_xq=v_w_xq[0], w_xk=v_w_xk[0], w_xv=v_w_xv[0], w_xo=v_w_xo[0],
                 w_up=v_w_up[0].T, w_down=v_w_down[0])
    shard_shape = {n: big_w[n].shape for n in big_names}

    w_in_g, w_sc_g, w_fc_full = _all_gather([big_w["w_in"].astype(BF16), w_short_conv[0], w_ffn_conv[0]])
    w_sc_full = w_sc_g.transpose(1, 0, 2).reshape(3, CONV_W)
    late_shards = [big_w[n].astype(BF16) for n in late_names]
    ag_send, ag_recv, ag_srcs, ag_lands, ag_token = _gather_start("gather_weights_start", late_shards, w_in_g)
    forwarded = {}

    def forward_weights(names, after):
        which = [late_names.index(n) for n in names]
        fsend, frecv, lands, token = _gather_forward("gather_" + "_".join(names) + "_forward", ag_send, ag_recv,
                                                     [ag_lands[a] for a in which], which, after)
        forwarded[tuple(names)] = (fsend, frecv, lands)
        return token

    def late_weights(names, after):
        which = [late_names.index(n) for n in names]
        fsend, frecv, lands = forwarded[tuple(names)]
        lands = _gather_wait("gather_" + "_".join(names) + "_wait", ag_send, ag_recv, fsend, frecv,
                             [ag_srcs[a] for a in which], lands, which, after)
        out = {}
        for n, a, land in zip(names, which, lands):
            full = lax.dynamic_update_index_in_dim(land, late_shards[a], me, 0)
            if n == "w_up":
                out[n] = full
            elif n == "w_down":
                out[n] = full.reshape(N_DEV // 2, UP_CHUNK, D_MODEL)
            else:
                out[n] = full.reshape(D_MODEL, D_MODEL)
        return out

    sent = []

    def emit(grads):
        names = list(grads)
        blocks = [grads[n].reshape((N_DEV,) + shard_shape[n]) for n in names]
        started = _exchange_start("scatter_" + "_".join(names) + "_start", blocks, [False] * len(names), me_arr)
        sent.append((names, started))
        return started[-1]

    def emit_small(small):
        sent_small.append((small, _exchange_start("gather_small_start", [small], [True], me_arr)))
        return sent_small[0][1][-1]

    sent_small = []
    grad_x = _local_step(
        x[0], mem[0], loss_target[0], rel_bias, g_mix, w_in_g, w_sc_full, g_attn_out, g_conv_out, g_xattn, g_mem,
        g_ffn, w_fc_full, b_ffn_conv.reshape(N_DEV, 1, UP_CHUNK), g_final.reshape(1, D_MODEL), ag_token,
        forward_weights, late_weights, emit, emit_small)

    small_g, small_started = sent_small[0]
    after = sent[-1][1][-1]
    small_parts = _exchange_wait("gather_small_wait", small_started, [True], after)[1][0]
    big_out = {}
    after = small_parts
    for names, started in sent:
        blocks, lands = _exchange_wait("scatter_" + "_".join(names) + "_wait", started, [False] * len(names), after)
        for n, block, land in zip(names, blocks, lands):
            res = _adamw_big("adamw_" + n, big_w[n], block, land, big_m[n], big_v[n], me_arr)
            big_out[n] = [(r.T if n == "w_up" else r)[None] for r in res]
            after = res[0]

    as_rows = lambda a: a.reshape(N_DEV, UP_CHUNK)
    row1 = lambda a: a.reshape(1, D_MODEL)
    small_names = ["rel_bias", "g_mix", "g_attn_out", "g_conv_out", "g_xattn", "g_mem", "g_ffn", "b_ffn_conv", "g_final"]
    wmv = [
        (rel_bias, m_rel_bias, v_rel_bias), (g_mix, m_g_mix, v_g_mix), (g_attn_out, m_g_attn_out, v_g_attn_out),
        (g_conv_out, m_g_conv_out, v_g_conv_out), (g_xattn, m_g_xattn, v_g_xattn), (g_mem, m_g_mem, v_g_mem),
        (g_ffn, m_g_ffn, v_g_ffn), (as_rows(b_ffn_conv), as_rows(m_b_ffn_conv), as_rows(v_b_ffn_conv)),
        (row1(g_final), row1(m_g_final), row1(v_g_final))]
    g_packed, small_res = _adamw_small(small_g, small_parts, wmv, me_arr)
    small_out = dict(zip(small_names, small_res))
    loss = g_packed[ROW_LOSS, 0]
    small_out["b_ffn_conv"] = [a.reshape(1, 2 * D_FF) for a in small_out["b_ffn_conv"]]
    small_out["g_final"] = [a.reshape(D_MODEL) for a in small_out["g_final"]]

    g_wsc = lax.dynamic_slice(g_packed[ROW_WSC:ROW_WSC + 3, 0:CONV_W], (0, me * HEAD_DIM), (3, HEAD_DIM))
    g_wfc = lax.dynamic_slice(g_packed[ROW_WFC:ROW_WFC + 3 * N_DEV, 0:UP_CHUNK].reshape(3, N_DEV, UP_CHUNK),
                              (0, me, 0), (3, 1, UP_CHUNK)).reshape(3, UP_CHUNK)
    shard_res = _adamw_shards([(w_short_conv[0], g_wsc, m_w_short_conv[0], v_w_short_conv[0]),
                               (w_ffn_conv[0], g_wfc, m_w_ffn_conv[0], v_w_ffn_conv[0])])
    small_out["w_short_conv"] = [g_wsc[None]] + [a[None] for a in shard_res[0]]
    small_out["w_ffn_conv"] = [g_wfc[None]] + [a[None] for a in shard_res[1]]

    order = ["rel_bias", "g_mix", "w_in", "w_short_conv", "g_attn_out", "g_conv_out", "w_out", "g_xattn", "g_mem",
             "w_xq", "w_xk", "w_xv", "w_xo", "g_ffn", "w_up", "w_ffn_conv", "b_ffn_conv", "w_down", "g_final"]
    allp = {**big_out, **small_out}
    outs = [loss, grad_x[None]]
    for kind in range(4):
        outs += [allp[n][kind] for n in order]
    return tuple(outs)
```

```python
import math

import numpy as np
import jax
import jax.numpy as jnp
from jax import lax
from jax.experimental import pallas as pl
from jax.experimental.pallas import tpu as pltpu

F32 = jnp.float32
BF16 = jnp.bfloat16
MESH = pl.DeviceIdType.MESH

N_DEV = 8
D_MODEL = 1024
ATTN_W = 512
CONV_W = 512
N_HEADS = 8
HEAD_DIM = 64
WIN = 128
DILATIONS = (1, 4, 16)
N_BUCKETS = 32
BUCKET_MAX_EXACT = 16
BUCKET_MAX_DISTANCE = 2048
N_MEM_HEADS = 4
MEM_HEAD_DIM = 256
D_FF = 2816
IN_COLS = 3072
IN_CHUNK = IN_COLS // N_DEV
UP_CHUNK = 2 * D_FF // N_DEV
EPS = 1e-6

ADAM_LR = 0.001
ADAM_B1 = 0.9
ADAM_B2 = 0.999
ADAM_EPS = 1e-08
ADAM_WD = 0.01
ADAM_STEP = 10

SUBLANES = 8
LANES = 128
HALO = 16
TM = 512
TM_FFN = 256
TS_DW = 4096
SWA_BLOCKS = 8
VMEM_LIMIT = 56 * 1024 * 1024

ROW_RELB, ROW_GMIX, ROW_GXATTN, ROW_GMEM, ROW_GFFN, ROW_GFINAL, ROW_GAC = 0, 8, 16, 24, 32, 40, 48
ROW_WSC, ROW_BFC, ROW_WFC, ROW_LOSS, SMALL_ROWS = 56, 64, 72, 96, 104


def _cparams(n_grid):
    return pltpu.CompilerParams(dimension_semantics=("arbitrary",) * n_grid, vmem_limit_bytes=VMEM_LIMIT)


def _full(shape):
    nd = len(shape)
    return pl.BlockSpec(tuple(shape), lambda *_: (0,) * nd)


def _resident(shape):
    nd = len(shape)
    return pl.BlockSpec(tuple(shape), lambda *_: (0,) * nd, pipeline_mode=pl.Buffered(1))


ANY_SPEC = pl.BlockSpec(memory_space=pl.ANY)
HBM_SPEC = pl.BlockSpec(memory_space=pltpu.HBM)
SEM_SPEC = pl.BlockSpec(memory_space=pltpu.SEMAPHORE)
VMEM_SPEC = pl.BlockSpec(memory_space=pltpu.VMEM)
SMEM_SPEC = pl.BlockSpec(memory_space=pltpu.SMEM)
DATAFLOW = pltpu.SideEffectType.DATAFLOW_SIDE_EFFECTING


def _rms(x):
    r = lax.rsqrt(jnp.mean(x * x, axis=-1, keepdims=True) + EPS)
    return x * r, r


def _rms_bwd(xh, r, g, dy):
    dxh = dy * g
    return r * (dxh - xh * jnp.mean(dxh * xh, axis=-1, keepdims=True))


def _shift_down(u, halo, k):
    ru = pltpu.roll(u, k, 0)
    rh = pltpu.roll(halo, k, 0)
    row = lax.broadcasted_iota(jnp.int32, rh.shape, 0)
    head = jnp.where(row < k, rh, ru[0:SUBLANES])
    return jnp.concatenate([head, ru[SUBLANES:]], axis=0)


def _shift_up(u, halo, k):
    tm = u.shape[0]
    ru = pltpu.roll(u, tm - k, 0)
    rh = pltpu.roll(halo, SUBLANES - k, 0)
    row = lax.broadcasted_iota(jnp.int32, rh.shape, 0)
    tail = jnp.where(row >= SUBLANES - k, rh, ru[tm - SUBLANES:])
    return jnp.concatenate([ru[:tm - SUBLANES], tail], axis=0)


def _causal_conv3(u, halo, w_ref):
    return (_shift_down(u, halo, 2) * w_ref[0:1, :] + _shift_down(u, halo, 1) * w_ref[1:2, :]) + u * w_ref[2:3, :]


def _dot(a, b):
    return jnp.dot(a, b, preferred_element_type=F32)


def _dot_nt(a, b):
    return lax.dot_general(a, b, (((1,), (1,)), ((), ())), preferred_element_type=F32)


def _dot_tn(a, b):
    return lax.dot_general(a, b, (((0,), (0,)), ((), ())), preferred_element_type=F32)


def _sigmoid(x):
    return 0.5 * jnp.tanh(0.5 * x) + 0.5


def _bucket_tables():
    qi = np.arange(WIN)[:, None]
    kj = np.arange(2 * WIN)[None, :]
    steps = np.clip(qi + WIN - kj, 0, WIN)
    out = []
    for d in DILATIONS:
        dist = steps * d
        dd = np.maximum(dist, 1).astype(np.float32)
        large = BUCKET_MAX_EXACT + (
            np.log(dd / np.float32(BUCKET_MAX_EXACT)) / np.float32(math.log(BUCKET_MAX_DISTANCE / BUCKET_MAX_EXACT))
            * np.float32(N_BUCKETS - BUCKET_MAX_EXACT)).astype(np.int32)
        large = np.minimum(large, N_BUCKETS - 1)
        out.append(np.where(dist < BUCKET_MAX_EXACT, dist, large).astype(np.int32))
    return np.stack(out)


def _band_mask():
    qi = lax.broadcasted_iota(jnp.int32, (WIN, 2 * WIN), 0)
    kj = lax.broadcasted_iota(jnp.int32, (WIN, 2 * WIN), 1)
    steps = qi + WIN - kj
    return (steps >= 0) & (steps <= WIN)


def _bias_fwd(rel_bias, buckets):
    present = [sorted(set(buckets[p].ravel().tolist())) for p in range(3)]

    def body(rb_ref, bk_ref, o_ref):
        band = _band_mask()
        for p in range(3):
            bk = bk_ref[p]
            for h in range(N_HEADS):
                acc = jnp.zeros((WIN, 2 * WIN), F32)
                for b in present[p]:
                    acc = jnp.where(bk == b, rb_ref[h, b], acc)
                o_ref[p, h] = jnp.where(band, acc, -jnp.inf)

    return pl.pallas_call(
        body, name="bias_fwd",
        out_shape=jax.ShapeDtypeStruct((3, N_HEADS, WIN, 2 * WIN), F32),
        in_specs=[pl.BlockSpec(memory_space=pltpu.SMEM), pl.BlockSpec(memory_space=pltpu.VMEM)],
        out_specs=pl.BlockSpec(memory_space=pltpu.VMEM),
    )(rel_bias, jnp.asarray(buckets))


def _bias_bwd(dbias, buckets):
    present = [set(buckets[p].ravel().tolist()) for p in range(3)]

    def body(db_ref, bk_ref, o_ref):
        lane = lax.broadcasted_iota(jnp.int32, (1, D_MODEL), 1)
        rows = []
        for h in range(N_HEADS):
            row = jnp.zeros((1, D_MODEL), F32)
            for b in range(N_BUCKETS):
                tot = jnp.zeros((1, 1), F32)
                for p in (p for p in range(3) if b in present[p]):
                    sel = jnp.where(bk_ref[p] == b, db_ref[p, h], 0.0)
                    tot = tot + jnp.sum(jnp.sum(sel, axis=0, keepdims=True), axis=1, keepdims=True)
                row = jnp.where(lane == b, tot, row)
            rows.append(row)
        o_ref[...] = jnp.concatenate(rows, axis=0)

    return pl.pallas_call(
        body, name="bias_bwd",
        out_shape=jax.ShapeDtypeStruct((N_HEADS, D_MODEL), F32),
        in_specs=[pl.BlockSpec(memory_space=pltpu.VMEM), pl.BlockSpec(memory_space=pltpu.VMEM)],
        out_specs=pl.BlockSpec(memory_space=pltpu.VMEM),
    )(dbias, jnp.asarray(buckets))


def _spread(val, scr_ref, out_refs, dtype):
    out_refs[0][...] = val.astype(dtype)
    n_blk = val.shape[1] // LANES
    for c in range(n_blk):
        scr_ref[c] = val[:, c * LANES:(c + 1) * LANES]
    for o_ref, d in zip(out_refs[1:], DILATIONS[1:]):
        for r in range(d):
            for c in range(n_blk):
                o_ref[r, :, c * LANES:(c + 1) * LANES] = scr_ref.at[c][pl.ds(r, TM // d, stride=d), :].astype(dtype)


def _gather_classes(blk_ref, scr_ref, d):
    n_blk = blk_ref.shape[2] // LANES
    for r in range(d):
        for c in range(n_blk):
            scr_ref.at[c][pl.ds(r, TM // d, stride=d), :] = blk_ref[r, :, c * LANES:(c + 1) * LANES].astype(F32)
    return jnp.concatenate([scr_ref[c] for c in range(n_blk)], axis=1)


def _class_specs(cols):
    return [pl.BlockSpec((TM, cols), lambda i: (i, 0))] + [
        pl.BlockSpec((d, TM // d, cols), lambda i: (0, i, 0)) for d in DILATIONS[1:]]


def _class_shapes(s, cols, dtype):
    return [jax.ShapeDtypeStruct((s, cols), dtype)] + [
        jax.ShapeDtypeStruct((d, s // d, cols), dtype) for d in DILATIONS[1:]]


def _rms_proj(x, g_mix, w_in_g, dep):
    s = x.shape[0]

    def body(x_ref, g_ref, w_ref, dep_ref, h_ref, q1, q4, q16, k1, k4, k16, v1, v4, v16, gb_ref, gc_ref, xi_ref, scr):
        xh, _ = _rms(x_ref[...])
        h = (xh * g_ref[...]).astype(BF16)
        h_ref[...] = h
        proj = jnp.concatenate([_dot(h, w_ref[j]) for j in range(N_DEV)], axis=1)
        _spread(proj[:, 0:512] * (HEAD_DIM ** -0.5), scr, (q1, q4, q16), BF16)
        _spread(proj[:, 512:1024], scr, (k1, k4, k16), BF16)
        _spread(proj[:, 1024:1536], scr, (v1, v4, v16), BF16)
        gb_ref[...] = proj[:, 1536:2048]
        gc_ref[...] = proj[:, 2048:2560]
        xi_ref[...] = proj[:, 2560:3072]

    row = lambda n: pl.BlockSpec((TM, n), lambda i: (i, 0))
    res = pl.pallas_call(
        body, name="rms_proj", grid=(s // TM,),
        out_shape=[jax.ShapeDtypeStruct((s, D_MODEL), BF16)] + _class_shapes(s, 512, BF16) * 3
        + [jax.ShapeDtypeStruct((s, 512), F32)] * 3,
        in_specs=[row(D_MODEL), _full(g_mix.shape), _full(w_in_g.shape), ANY_SPEC],
        out_specs=[row(D_MODEL)] + _class_specs(512) * 3 + [row(512)] * 3,
        scratch_shapes=[pltpu.VMEM((512 // LANES, TM, LANES), F32)],
        compiler_params=_cparams(1),
    )(x, g_mix, w_in_g, dep)
    return res[0], res[1:4], res[4:7], res[7:10], res[10], res[11], res[12]


def _pair_split(x2):
    lane = lax.broadcasted_iota(jnp.int32, x2.shape, 1)
    zero = jnp.zeros_like(x2)
    return jnp.where(lane < HEAD_DIM, x2, zero), jnp.where(lane >= HEAD_DIM, x2, zero)


def _pair_join(even, odd):
    lane = lax.broadcasted_iota(jnp.int32, (even.shape[0], LANES), 1)
    return jnp.where(lane < HEAD_DIM, even, odd)


def _swa_steps(qc, dil):
    n128 = qc.shape[1] // WIN
    nsub = min(SWA_BLOCKS, n128)
    nb = n128 // nsub
    ncls = min(dil, SWA_BLOCKS // nsub) if nb == 1 else 1
    return nsub, nb, ncls


def _swa_fwd(qc, kc, vc, bias, dil, dep):
    nsub, nb, ncls = _swa_steps(qc, dil)
    whole = nb == 1

    def body(q_ref, kp_ref, kc_ref, vp_ref, vc_ref, b_ref, dep_ref, o_ref, lse_ref, s_scr, p_scr):
        no_prev = (pl.program_id(1) == 0) & (lax.broadcasted_iota(jnp.int32, (WIN, 2 * WIN), 1) < WIN)
        pairs = [slice(a * LANES, (a + 1) * LANES) for a in range(N_HEADS // 2)]
        for c, t in [(c, t) for c in range(ncls) for t in range(nsub)]:
            i = c * nsub + t
            rows = slice(t * WIN, (t + 1) * WIN)
            alone = whole and t == 0
            cols = slice(WIN, 2 * WIN) if alone else slice(0, 2 * WIN)

            def keys(prev_ref, cur_ref, sl):
                if alone:
                    return cur_ref[c, rows, sl]
                if t == 0:
                    return jnp.concatenate([prev_ref[c, :, sl], cur_ref[c, rows, sl]], axis=0)
                return cur_ref[c, (t - 1) * WIN:(t + 1) * WIN, sl]

            for a, sl in enumerate(pairs):
                k2 = keys(kp_ref, kc_ref, sl)
                for e, qh in enumerate(_pair_split(q_ref[c, rows, sl])):
                    s_scr[i, 2 * a + e, :, cols] = _dot_nt(qh, k2)
            den, lse = [], []
            for h in range(N_HEADS):
                lg = s_scr[i, h, :, cols] + b_ref[h, :, cols]
                if t == 0 and not whole:
                    lg = jnp.where(no_prev, -jnp.inf, lg)
                m = jnp.max(lg, axis=-1, keepdims=True)
                p = jnp.exp(lg - m)
                den.append(jnp.sum(p, axis=-1, keepdims=True))
                p_scr[i, h, :, cols] = p.astype(BF16)
                lse.append(m + jnp.log(den[h]))
            for a, sl in enumerate(pairs):
                v_even, v_odd = _pair_split(keys(vp_ref, vc_ref, sl))
                o2 = _dot(p_scr[i, 2 * a, :, cols], v_even) + _dot(p_scr[i, 2 * a + 1, :, cols], v_odd)
                o_ref[c, rows, sl] = o2 / _pair_join(den[2 * a], den[2 * a + 1])
                lse_ref[c, rows, sl] = _pair_join(lse[2 * a], lse[2 * a + 1])

    cur = pl.BlockSpec((ncls, nsub * WIN, 512), lambda r, b: (r, b, 0))
    prev = pl.BlockSpec((ncls, WIN, 512), lambda r, b: (r, jnp.maximum(nsub * b - 1, 0), 0))
    wide = (ncls * nsub, N_HEADS, WIN, 2 * WIN)
    return pl.pallas_call(
        body, name=f"swa_fwd_d{dil}", grid=(dil // ncls, nb),
        out_shape=[jax.ShapeDtypeStruct(qc.shape, F32)] * 2,
        in_specs=[cur, prev, cur, prev, cur, _full(bias.shape), ANY_SPEC],
        out_specs=[cur] * 2,
        scratch_shapes=[pltpu.VMEM(wide, F32), pltpu.VMEM(wide, BF16)],
        compiler_params=_cparams(2),
    )(qc, kc, kc, vc, vc, bias, dep)


def _mix_out(branches, gb, gc, xi, x, w_sc, g_a, g_c, w_out):
    s = x.shape[0]
    tb = TM // SUBLANES

    def body(o1, l1, o4, l4, o16, l16, gb_ref, gc_ref, xi_ref, gch_ref, xih_ref, x_ref, wsc_ref,
             ga_ref, gcv_ref, wout_ref, attn_ref, lse1, lse4, lse16, mixed_ref, x1_ref, scr_a, scr_b, scr_c, scr_d):
        i = pl.program_id(0)
        la, lb, lc = l1[...], _gather_classes(l4, scr_a, 4), _gather_classes(l16, scr_b, 16)
        m_all = jnp.maximum(jnp.maximum(la, lb), lc)
        ea, eb, ec = jnp.exp(la - m_all), jnp.exp(lb - m_all), jnp.exp(lc - m_all)
        den = (ea + eb) + ec
        num = (ea * o1[...] + eb * _gather_classes(o4, scr_c, 4)) + ec * _gather_classes(o16, scr_d, 16)
        attn = num / den
        attn_ref[...] = attn
        _spread(m_all + jnp.log(den), scr_a, (lse1, lse4, lse16), F32)
        xa, _ = _rms(attn)
        u = gc_ref[...] * xi_ref[...]
        uh = jnp.where(i > 0, gch_ref[...] * xih_ref[...], 0.0)
        conv = gb_ref[...] * _causal_conv3(u, uh, wsc_ref)
        xc, _ = _rms(conv)
        mixed = jnp.concatenate([xa * ga_ref[...], xc * gcv_ref[...]], axis=1).astype(BF16)
        mixed_ref[...] = mixed
        x1_ref[...] = x_ref[...] + _dot(mixed, wout_ref[...])

    row = lambda n: pl.BlockSpec((TM, n), lambda i: (i, 0))
    halo = pl.BlockSpec((SUBLANES, 512), lambda i: (jnp.maximum(i * tb - 1, 0), 0))
    cs = _class_specs(512)
    flat = [a for br in branches for a in br]
    res = pl.pallas_call(
        body, name="mix_out", grid=(s // TM,),
        out_shape=[jax.ShapeDtypeStruct((s, 512), F32)] + _class_shapes(s, 512, F32)
        + [jax.ShapeDtypeStruct((s, D_MODEL), BF16), jax.ShapeDtypeStruct((s, D_MODEL), F32)],
        in_specs=[cs[0], cs[0], cs[1], cs[1], cs[2], cs[2], row(512), row(512), row(512), halo, halo,
                  row(D_MODEL), _full(w_sc.shape), _full(g_a.shape), _full(g_c.shape), _full(w_out.shape)],
        out_specs=[row(512)] + cs + [row(D_MODEL), row(D_MODEL)],
        scratch_shapes=[pltpu.VMEM((512 // LANES, TM, LANES), F32)] * 4,
        compiler_params=_cparams(1),
    )(*flat, gb, gc, xi, gc, xi, x, w_sc, g_a, g_c, w_out)
    return res[0], res[1:4], res[4], res[5]


def _mem_kv(mem, g_mem, w_xk, w_xv):
    def body(mem_ref, g_ref, wk_ref, wv_ref, mn_ref, k_ref, v_ref):
        xh, _ = _rms(mem_ref[...])
        mn = (xh * g_ref[...]).astype(BF16)
        mn_ref[...] = mn
        k_ref[...] = _dot(mn, wk_ref[...]).astype(BF16)
        v_ref[...] = _dot(mn, wv_ref[...]).astype(BF16)

    vm = pl.BlockSpec(memory_space=pltpu.VMEM)
    return pl.pallas_call(
        body, name="mem_kv",
        out_shape=[jax.ShapeDtypeStruct(mem.shape, BF16)] * 3,
        in_specs=[vm] * 4, out_specs=[vm] * 3,
        compiler_params=pltpu.CompilerParams(vmem_limit_bytes=VMEM_LIMIT),
    )(mem, g_mem, w_xk, w_xv)


def _xattn_fwd(x1, g, w_xq, k, v, w_xo, dep):
    s = x1.shape[0]

    def body(x1_ref, g_ref, wq_ref, k_ref, v_ref, wo_ref, dep_ref, h2_ref, q_ref, o_ref, x2_ref):
        x1v = x1_ref[...]
        xh, _ = _rms(x1v)
        h2 = (xh * g_ref[...]).astype(BF16)
        h2_ref[...] = h2
        qb = _dot(h2, wq_ref[...]).astype(BF16)
        q_ref[...] = qb
        outs = []
        for h in range(N_MEM_HEADS):
            sl = slice(h * MEM_HEAD_DIM, (h + 1) * MEM_HEAD_DIM)
            lg = _dot_nt(qb[:, sl], k_ref[:, sl]) * (MEM_HEAD_DIM ** -0.5)
            p = jnp.exp(lg - jnp.max(lg, axis=-1, keepdims=True))
            p = p / jnp.sum(p, axis=-1, keepdims=True)
            outs.append(_dot(p.astype(BF16), v_ref[:, sl]))
        o = jnp.concatenate(outs, axis=1).astype(BF16)
        o_ref[...] = o
        x2_ref[...] = x1v + _dot(o, wo_ref[...])

    row = pl.BlockSpec((TM, D_MODEL), lambda i: (i, 0))
    return pl.pallas_call(
        body, name="xattn_fwd", grid=(s // TM,),
        out_shape=[jax.ShapeDtypeStruct((s, D_MODEL), BF16)] * 3 + [jax.ShapeDtypeStruct((s, D_MODEL), F32)],
        in_specs=[row, _full(g.shape), _full(w_xq.shape), _full(k.shape), _full(v.shape), _full(w_xo.shape), ANY_SPEC],
        out_specs=[row] * 4,
        compiler_params=_cparams(1),
    )(x1, g, w_xq, k, v, w_xo, dep)


def _ffn_conv(h_ext, wup_ref, wfc_ref, bfc_ref, j):
    u = _dot_nt(h_ext, wup_ref[j])
    w = wfc_ref[j]
    c = ((pltpu.roll(u, 2, 0) * w[0:1, :] + pltpu.roll(u, 1, 0) * w[1:2, :]) + u * w[2:3, :]) + bfc_ref[j]
    return c[HALO:], u[HALO:]


def _ffn_fwd(x2, g, w_up_g, w_fc, b_fc, w_down_g, g_final, target):
    s = x2.shape[0]
    tb = TM_FFN // HALO
    half = N_DEV // 2

    def body(x_ref, xp_ref, g_ref, wup_ref, wfc_ref, bfc_ref, wd_ref, gf_ref, t_ref, h_ref, u_ref, c_ref, act_ref,
             dx3_ref, loss_ref, dgf_ref):
        i = pl.program_id(0)

        @pl.when(i == 0)
        def _():
            loss_ref[...] = jnp.zeros_like(loss_ref)
            dgf_ref[...] = jnp.zeros_like(dgf_ref)

        x2v = x_ref[...]
        gv = g_ref[...]
        h = (_rms(x2v)[0] * gv).astype(BF16)
        h_ref[...] = h
        hp = jnp.where(i > 0, _rms(xp_ref[...])[0] * gv, 0.0).astype(BF16)
        h_ext = jnp.concatenate([hp, h], axis=0)
        down = jnp.zeros((TM_FFN, D_MODEL), F32)
        for j in range(half):
            cg, ug = _ffn_conv(h_ext, wup_ref, wfc_ref, bfc_ref, j)
            cv, uv = _ffn_conv(h_ext, wup_ref, wfc_ref, bfc_ref, j + half)
            c_ref[j] = cg
            c_ref[j + half] = cv
            u_ref[j] = ug.astype(BF16)
            u_ref[j + half] = uv.astype(BF16)
            a = ((cg * _sigmoid(cg)) * cv).astype(BF16)
            act_ref[j] = a
            down = down + _dot(a, wd_ref[j])
        x3 = x2v + down
        xh, r = _rms(x3)
        gf = gf_ref[...]
        e = xh * gf - t_ref[...]
        loss_ref[...] += 0.5 * jnp.sum(jnp.sum(e * e, axis=1, keepdims=True), axis=0, keepdims=True) / D_MODEL
        dy = e * (1.0 / D_MODEL)
        dgf_ref[0:1, :] += jnp.sum(dy * xh, axis=0, keepdims=True)
        dx3_ref[...] = _rms_bwd(xh, r, gf, dy)

    row = pl.BlockSpec((TM_FFN, D_MODEL), lambda i: (i, 0))
    prev = pl.BlockSpec((HALO, D_MODEL), lambda i: (jnp.maximum(i * tb - 1, 0), 0))
    return pl.pallas_call(
        body, name="ffn_fwd", grid=(s // TM_FFN,),
        out_shape=[jax.ShapeDtypeStruct((s, D_MODEL), BF16), jax.ShapeDtypeStruct((N_DEV, s, UP_CHUNK), BF16),
                   jax.ShapeDtypeStruct((N_DEV, s, UP_CHUNK), F32), jax.ShapeDtypeStruct((half, s, UP_CHUNK), BF16),
                   jax.ShapeDtypeStruct((s, D_MODEL), F32), jax.ShapeDtypeStruct((SUBLANES, 128), F32),
                   jax.ShapeDtypeStruct((SUBLANES, D_MODEL), F32)],
        in_specs=[row, prev, _full(g.shape), _resident(w_up_g.shape), _full(w_fc.shape), _full(b_fc.shape),
                  _resident(w_down_g.shape), _full(g_final.shape), row],
        out_specs=[row, pl.BlockSpec((N_DEV, TM_FFN, UP_CHUNK), lambda i: (0, i, 0)),
                   pl.BlockSpec((N_DEV, TM_FFN, UP_CHUNK), lambda i: (0, i, 0)),
                   pl.BlockSpec((half, TM_FFN, UP_CHUNK), lambda i: (0, i, 0)), row,
                   _full((SUBLANES, 128)), _full((SUBLANES, D_MODEL))],
        compiler_params=_cparams(1),
    )(x2, x2, g, w_up_g, w_fc, b_fc, w_down_g, g_final, target)


def _ffn_bwd(dx3, up, conv, x2, g, w_up_g, w_fc, w_down_g):
    s = x2.shape[0]
    tb = TM_FFN // HALO
    last = s // HALO - 1
    n_tiles = s // TM_FFN
    half = N_DEV // 2
    n_ext = TM_FFN + HALO

    def body(dx_ref, dxn_ref, u_ref, c_ref, cn_ref, x2_ref, g_ref, wup_ref, wfc_ref, wd_ref,
             dup_ref, dx2_ref, dg_ref, dwfc_ref, dbfc_ref):
        i = pl.program_id(0)

        @pl.when(i == 0)
        def _():
            dg_ref[...] = jnp.zeros_like(dg_ref)
            dwfc_ref[...] = jnp.zeros_like(dwfc_ref)
            dbfc_ref[...] = jnp.zeros_like(dbfc_ref)

        dxv = dx_ref[...]
        dxn = jnp.where(i < n_tiles - 1, dxn_ref[...], 0.0)
        dx_ext = jnp.concatenate([dxv, dxn], axis=0).astype(BF16)
        dh = jnp.zeros((TM_FFN, D_MODEL), F32)
        for j in range(half):
            cg = jnp.concatenate([c_ref[j], cn_ref[j]], axis=0)
            cv = jnp.concatenate([c_ref[j + half], cn_ref[j + half]], axis=0)
            dact = _dot_nt(dx_ext, wd_ref[j])
            sg = _sigmoid(cg)
            silu = cg * sg
            parts = ((j + half, dact * silu), (j, (dact * cv) * (sg + silu * (1.0 - sg))))
            for jj, dc in parts:
                u = u_ref[jj].astype(F32)
                dc0, dc1, dc2 = dc[:TM_FFN], pltpu.roll(dc, n_ext - 1, 0)[:TM_FFN], pltpu.roll(dc, n_ext - 2, 0)[:TM_FFN]
                dbfc_ref[jj:jj + 1, :] += jnp.sum(dc0, axis=0, keepdims=True)
                dwfc_ref[0, jj:jj + 1, :] += jnp.sum(dc2 * u, axis=0, keepdims=True)
                dwfc_ref[1, jj:jj + 1, :] += jnp.sum(dc1 * u, axis=0, keepdims=True)
                dwfc_ref[2, jj:jj + 1, :] += jnp.sum(dc0 * u, axis=0, keepdims=True)
                w = wfc_ref[jj]
                du = ((dc0 * w[2:3, :] + dc1 * w[1:2, :]) + dc2 * w[0:1, :]).astype(BF16)
                dup_ref[jj] = du
                dh = dh + _dot(du, wup_ref[jj])
        xh, r = _rms(x2_ref[...])
        dg_ref[0:1, :] += jnp.sum(dh * xh, axis=0, keepdims=True)
        dx2_ref[...] = dxv + _rms_bwd(xh, r, g_ref[...], dh)

    row = pl.BlockSpec((TM_FFN, D_MODEL), lambda i: (i, 0))
    nxt = pl.BlockSpec((HALO, D_MODEL), lambda i: (jnp.minimum((i + 1) * tb, last), 0))
    cur_c = pl.BlockSpec((N_DEV, TM_FFN, UP_CHUNK), lambda i: (0, i, 0))
    nxt_c = pl.BlockSpec((N_DEV, HALO, UP_CHUNK), lambda i: (0, jnp.minimum((i + 1) * tb, last), 0))
    return pl.pallas_call(
        body, name="ffn_bwd", grid=(n_tiles,),
        out_shape=[jax.ShapeDtypeStruct((N_DEV, s, UP_CHUNK), BF16), jax.ShapeDtypeStruct((s, D_MODEL), F32),
                   jax.ShapeDtypeStruct((SUBLANES, D_MODEL), F32), jax.ShapeDtypeStruct((3, N_DEV, UP_CHUNK), F32),
                   jax.ShapeDtypeStruct((N_DEV, UP_CHUNK), F32)],
        in_specs=[row, nxt, cur_c, cur_c, nxt_c, row, _full(g.shape), _resident(w_up_g.shape), _full(w_fc.shape),
                  _resident(w_down_g.shape)],
        out_specs=[cur_c, row, _full((SUBLANES, D_MODEL)), _full((3, N_DEV, UP_CHUNK)), _full((N_DEV, UP_CHUNK))],
        compiler_params=_cparams(1),
    )(dx3, dx3, up, conv, conv, x2, g, w_up_g, w_fc, w_down_g)


def _xattn_bwd(dx2, o, q, k, v, w_xo, w_xq, x1, g, dep):
    s = x1.shape[0]

    def body(dx2_ref, o_ref, q_ref, k_ref, v_ref, wo_ref, wq_ref, x1_ref, g_ref, dep_ref, dq_ref, dx1_ref, dk_ref,
             dv_ref, dg_ref):
        @pl.when(pl.program_id(0) == 0)
        def _():
            dk_ref[...] = jnp.zeros_like(dk_ref)
            dv_ref[...] = jnp.zeros_like(dv_ref)
            dg_ref[...] = jnp.zeros_like(dg_ref)

        dx2v = dx2_ref[...]
        do = _dot_nt(dx2v.astype(BF16), wo_ref[...])
        dqs = []
        for h in range(N_MEM_HEADS):
            sl = slice(h * MEM_HEAD_DIM, (h + 1) * MEM_HEAD_DIM)
            qh, kh, vh = q_ref[:, sl], k_ref[:, sl], v_ref[:, sl]
            lg = _dot_nt(qh, kh) * (MEM_HEAD_DIM ** -0.5)
            p = jnp.exp(lg - jnp.max(lg, axis=-1, keepdims=True))
            p = p / jnp.sum(p, axis=-1, keepdims=True)
            doh = do[:, sl].astype(BF16)
            dp = _dot_nt(doh, vh)
            ds = (p * (dp - jnp.sum(p * dp, axis=-1, keepdims=True)) * (MEM_HEAD_DIM ** -0.5)).astype(BF16)
            dqs.append(_dot(ds, kh))
            dk_ref[:, sl] += _dot_tn(ds, qh)
            dv_ref[:, sl] += _dot_tn(p.astype(BF16), doh)
        dq = jnp.concatenate(dqs, axis=1).astype(BF16)
        dq_ref[...] = dq
        dh2 = _dot_nt(dq, wq_ref[...])
        xh, r = _rms(x1_ref[...])
        dg_ref[0:1, :] += jnp.sum(dh2 * xh, axis=0, keepdims=True)
        dx1_ref[...] = dx2v + _rms_bwd(xh, r, g_ref[...], dh2)

    row = pl.BlockSpec((TM, D_MODEL), lambda i: (i, 0))
    return pl.pallas_call(
        body, name="xattn_bwd", grid=(s // TM,),
        out_shape=[jax.ShapeDtypeStruct((s, D_MODEL), BF16), jax.ShapeDtypeStruct((s, D_MODEL), F32),
                   jax.ShapeDtypeStruct(k.shape, F32), jax.ShapeDtypeStruct(k.shape, F32),
                   jax.ShapeDtypeStruct((SUBLANES, D_MODEL), F32)],
        in_specs=[row, row, row, _full(k.shape), _full(v.shape), _full(w_xo.shape), _full(w_xq.shape), row,
                  _full(g.shape), ANY_SPEC],
        out_specs=[row, row, _full(k.shape), _full(k.shape), _full((SUBLANES, D_MODEL))],
        compiler_params=_cparams(1),
    )(dx2, o, q, k, v, w_xo, w_xq, x1, g, dep)


def _mem_kv_bwd(dk, dv, mem_n, mem, w_xk, w_xv):
    def body(dk_ref, dv_ref, mn_ref, mem_ref, wk_ref, wv_ref, dwk_ref, dwv_ref, dg_ref):
        dkb, dvb = dk_ref[...].astype(BF16), dv_ref[...].astype(BF16)
        mn = mn_ref[...]
        dwk_ref[...] = _dot_tn(mn, dkb).astype(BF16)
        dwv_ref[...] = _dot_tn(mn, dvb).astype(BF16)
        dmn = _dot_nt(dkb, wk_ref[...]) + _dot_nt(dvb, wv_ref[...])
        xh, _ = _rms(mem_ref[...])
        dg_ref[...] = jnp.zeros_like(dg_ref)
        dg_ref[0:1, :] = jnp.sum(dmn * xh, axis=0, keepdims=True)

    vm = pl.BlockSpec(memory_space=pltpu.VMEM)
    return pl.pallas_call(
        body, name="mem_kv_bwd",
        out_shape=[jax.ShapeDtypeStruct(w_xk.shape, BF16), jax.ShapeDtypeStruct(w_xv.shape, BF16),
                   jax.ShapeDtypeStruct((SUBLANES, D_MODEL), F32)],
        in_specs=[vm] * 6, out_specs=[vm] * 3,
        compiler_params=pltpu.CompilerParams(vmem_limit_bytes=VMEM_LIMIT),
    )(dk, dv, mem_n, mem, w_xk, w_xv)


def _mix_out_bwd(dx1, w_out, attn, gb, gc, xi, w_sc, g_a, g_c, dep):
    s = dx1.shape[0]
    tb = TM // SUBLANES

    def body(dx1_ref, wout_ref, attn_ref, gb_ref, gc_ref, xi_ref, gch_ref, xih_ref, wsc_ref, ga_ref, gcv_ref, dep_ref,
             da1, da4, da16, dd1, dd4, dd16, dgb_ref, dcv_ref, dga_ref, dgc_ref, dwsc_ref, scr):
        i = pl.program_id(0)

        @pl.when(i == 0)
        def _():
            dga_ref[...] = jnp.zeros_like(dga_ref)
            dgc_ref[...] = jnp.zeros_like(dgc_ref)
            dwsc_ref[...] = jnp.zeros_like(dwsc_ref)

        dmixed = _dot_nt(dx1_ref[...].astype(BF16), wout_ref[...])
        da, dcn = dmixed[:, :ATTN_W], dmixed[:, ATTN_W:]
        attn = attn_ref[...]
        xa, ra = _rms(attn)
        dga_ref[0:1, :] += jnp.sum(da * xa, axis=0, keepdims=True)
        dattn = _rms_bwd(xa, ra, ga_ref[...], da)
        _spread(dattn, scr, (da1, da4, da16), BF16)
        prod = dattn * attn
        dd = jnp.concatenate(
            [jnp.broadcast_to(jnp.sum(prod[:, h * HEAD_DIM:(h + 1) * HEAD_DIM], axis=-1, keepdims=True),
                              (TM, HEAD_DIM)) for h in range(N_HEADS)], axis=1)
        _spread(dd, scr, (dd1, dd4, dd16), F32)
        gbv = gb_ref[...]
        u = gc_ref[...] * xi_ref[...]
        uh = jnp.where(i > 0, gch_ref[...] * xih_ref[...], 0.0)
        u2, u1 = _shift_down(u, uh, 2), _shift_down(u, uh, 1)
        cv = (u2 * wsc_ref[0:1, :] + u1 * wsc_ref[1:2, :]) + u * wsc_ref[2:3, :]
        xc, rc = _rms(gbv * cv)
        dgc_ref[0:1, :] += jnp.sum(dcn * xc, axis=0, keepdims=True)
        dconv = _rms_bwd(xc, rc, gcv_ref[...], dcn)
        dgb_ref[...] = (dconv * cv).astype(BF16)
        dcv = dconv * gbv
        dcv_ref[...] = dcv
        dwsc_ref[0:1, :] += jnp.sum(dcv * u2, axis=0, keepdims=True)
        dwsc_ref[1:2, :] += jnp.sum(dcv * u1, axis=0, keepdims=True)
        dwsc_ref[2:3, :] += jnp.sum(dcv * u, axis=0, keepdims=True)

    row = lambda n: pl.BlockSpec((TM, n), lambda i: (i, 0))
    halo = pl.BlockSpec((SUBLANES, 512), lambda i: (jnp.maximum(i * tb - 1, 0), 0))
    acc = _full((SUBLANES, 512))
    res = pl.pallas_call(
        body, name="mix_out_bwd", grid=(s // TM,),
        out_shape=_class_shapes(s, 512, BF16) + _class_shapes(s, 512, F32)
        + [jax.ShapeDtypeStruct((s, 512), BF16), jax.ShapeDtypeStruct((s, 512), F32)]
        + [jax.ShapeDtypeStruct((SUBLANES, 512), F32)] * 3,
        in_specs=[row(D_MODEL), _full(w_out.shape), row(512), row(512), row(512), row(512), halo, halo,
                  _full(w_sc.shape), _full(g_a.shape), _full(g_c.shape), ANY_SPEC],
        out_specs=_class_specs(512) * 2 + [row(512)] * 2 + [acc] * 3,
        scratch_shapes=[pltpu.VMEM((512 // LANES, TM, LANES), F32)],
        compiler_params=_cparams(1),
    )(dx1, w_out, attn, gb, gc, xi, gc, xi, w_sc, g_a, g_c, dep)
    return res[0:3], res[3:6], res[6], res[7], res[8], res[9], res[10]


def _swa_bwd(qc, kc, vc, doc, lsec, ddc, bias, dil, dep):
    nsub, nb, ncls = _swa_steps(qc, dil)
    n128 = nsub * nb
    whole = nb == 1

    def body(q_ref, qn_ref, kp_ref, kc_ref, vp_ref, vc_ref, do_ref, don_ref, lse_ref, lsen_ref, dd_ref, ddn_ref,
             b_ref, dep_ref, dq_ref, dk_ref, dv_ref, db_ref, s_scr, dp_scr, sn_scr, dpn_scr, ds_scr, p_scr, dsn_scr,
             pn_scr):
        r, b = pl.program_id(0), pl.program_id(1)

        @pl.when((r == 0) & (b == 0))
        def _():
            db_ref[...] = jnp.zeros_like(db_ref)

        pairs = [slice(a * LANES, (a + 1) * LANES) for a in range(N_HEADS // 2)]
        blk = [slice(t * WIN, (t + 1) * WIN) for t in range(nsub)]
        last = blk[nsub - 1]
        cols = lambda t: slice(WIN, 2 * WIN) if whole and t == 0 else slice(0, 2 * WIN)
        of_head = lambda ref, c, rows, h: ref[c, rows, h * HEAD_DIM:h * HEAD_DIM + 1]
        no_prev = (b == 0) & (lax.broadcasted_iota(jnp.int32, (WIN, 2 * WIN), 1) < WIN)

        def keys(prev_ref, cur_ref, c, t, sl):
            if whole and t == 0:
                return cur_ref[c, blk[0], sl]
            if t == 0:
                return jnp.concatenate([prev_ref[c, :, sl], cur_ref[c, blk[0], sl]], axis=0)
            return cur_ref[c, (t - 1) * WIN:(t + 1) * WIN, sl]

        for a, sl in enumerate(pairs):
            for c, t in [(c, t) for c in range(ncls) for t in range(nsub)]:
                k2, v2 = keys(kp_ref, kc_ref, c, t, sl), keys(vp_ref, vc_ref, c, t, sl)
                q_eo = _pair_split(q_ref[c, blk[t], sl])
                do_eo = _pair_split(do_ref[c, blk[t], sl].astype(BF16))
                for e in range(2):
                    s_scr[c * nsub + t, 2 * a + e, :, cols(t)] = _dot_nt(q_eo[e], k2)
                    dp_scr[c * nsub + t, 2 * a + e, :, cols(t)] = _dot_nt(do_eo[e], v2)
            if not whole:
                qn_eo = _pair_split(qn_ref[0, :, sl])
                don_eo = _pair_split(don_ref[0, :, sl].astype(BF16))
                for e in range(2):
                    sn_scr[2 * a + e] = _dot_nt(qn_eo[e], kc_ref[0, last, sl])
                    dpn_scr[2 * a + e] = _dot_nt(don_eo[e], vc_ref[0, last, sl])
        for c, t, h in [(c, t, h) for c in range(ncls) for t in range(nsub) for h in range(N_HEADS)]:
            i, cl = c * nsub + t, cols(t)
            lg = s_scr[i, h, :, cl] + b_ref[h, :, cl]
            if t == 0 and not whole:
                lg = jnp.where(no_prev, -jnp.inf, lg)
            p = jnp.exp(lg - of_head(lse_ref, c, blk[t], h))
            ds = p * (dp_scr[i, h, :, cl] - of_head(dd_ref, c, blk[t], h))
            db_ref[h, :, cl] += ds
            ds_scr[i, h, :, cl] = ds.astype(BF16)
            p_scr[i, h, :, cl] = p.astype(BF16)
        if not whole:
            every = slice(0, WIN)
            for h in range(N_HEADS):
                lgn = jnp.where(b + 1 < nb, sn_scr[h] + b_ref[h, :, :WIN], -jnp.inf)
                pn = jnp.exp(lgn - of_head(lsen_ref, 0, every, h))
                dsn_scr[h] = (pn * (dpn_scr[h] - of_head(ddn_ref, 0, every, h))).astype(BF16)
                pn_scr[h] = pn.astype(BF16)
        for a, sl in enumerate(pairs):
            for c in range(ncls):
                q_eo = [_pair_split(q_ref[c, blk[t], sl]) for t in range(nsub)]
                do_eo = [_pair_split(do_ref[c, blk[t], sl].astype(BF16)) for t in range(nsub)]
                if not whole:
                    q_eo.append(_pair_split(qn_ref[0, :, sl]))
                    do_eo.append(_pair_split(don_ref[0, :, sl].astype(BF16)))
                for t in range(nsub):
                    i = c * nsub + t
                    k_eo = _pair_split(keys(kp_ref, kc_ref, c, t, sl))
                    dq, dk, dv = None, None, None
                    for e in range(2):
                        h = 2 * a + e
                        terms = [_dot(ds_scr[i, h, :, cols(t)], k_eo[e]),
                                 _dot_tn(ds_scr[i, h, :, WIN:], q_eo[t][e]),
                                 _dot_tn(p_scr[i, h, :, WIN:], do_eo[t][e])]
                        if t + 1 < nsub or not whole:
                            ds_next = ds_scr[i + 1, h, :, :WIN] if t + 1 < nsub else dsn_scr[h]
                            p_next = p_scr[i + 1, h, :, :WIN] if t + 1 < nsub else pn_scr[h]
                            terms[1] += _dot_tn(ds_next, q_eo[t + 1][e])
                            terms[2] += _dot_tn(p_next, do_eo[t + 1][e])
                        dq, dk, dv = terms if e == 0 else (dq + terms[0], dk + terms[1], dv + terms[2])
                    dq_ref[c, blk[t], sl] = dq.astype(BF16)
                    dk_ref[c, blk[t], sl] = dk.astype(BF16)
                    dv_ref[c, blk[t], sl] = dv.astype(BF16)

    cur = pl.BlockSpec((ncls, nsub * WIN, 512), lambda r, b: (r, b, 0))
    prev = pl.BlockSpec((ncls, WIN, 512), lambda r, b: (r, jnp.maximum(nsub * b - 1, 0), 0))
    nxt = pl.BlockSpec((ncls, WIN, 512), lambda r, b: (r, jnp.minimum(nsub * b + nsub, n128 - 1), 0))
    wide, narrow = (ncls * nsub, N_HEADS, WIN, 2 * WIN), (N_HEADS, WIN, WIN)
    return pl.pallas_call(
        body, name=f"swa_bwd_d{dil}", grid=(dil // ncls, nb),
        out_shape=[jax.ShapeDtypeStruct(qc.shape, BF16)] * 3 + [jax.ShapeDtypeStruct(bias.shape, F32)],
        in_specs=[cur, nxt, prev, cur, prev, cur, cur, nxt, cur, nxt, cur, nxt, _full(bias.shape), ANY_SPEC],
        out_specs=[cur] * 3 + [_full(bias.shape)],
        scratch_shapes=[pltpu.VMEM(wide, F32), pltpu.VMEM(wide, F32), pltpu.VMEM(narrow, F32),
                        pltpu.VMEM(narrow, F32), pltpu.VMEM(wide, BF16), pltpu.VMEM(wide, BF16),
                        pltpu.VMEM(narrow, BF16), pltpu.VMEM(narrow, BF16)],
        compiler_params=_cparams(2),
    )(qc, qc, kc, kc, vc, vc, doc, doc, lsec, lsec, ddc, ddc, bias, dep)


def _in_proj_bwd(dqs, dks, dvs, dgb, dcv, gc, xi, w_sc, w_in_g, x, g_mix, dx1):
    s = x.shape[0]
    tb = TM // SUBLANES
    last = s // SUBLANES - 1
    n_tiles = s // TM

    def body(dq1, dq4, dq16, dk1, dk4, dk16, dv1, dv4, dv16, dgb_ref, dcv_ref, dcvn_ref, gc_ref, xi_ref, wsc_ref,
             win_ref, x_ref, g_ref, dx1_ref, dproj_ref, gx_ref, dg_ref, scr_a, scr_b):
        i = pl.program_id(0)

        @pl.when(i == 0)
        def _():
            dg_ref[...] = jnp.zeros_like(dg_ref)

        d0 = dcv_ref[...]
        dn = jnp.where(i < n_tiles - 1, dcvn_ref[...], 0.0)
        du = (d0 * wsc_ref[2:3, :] + _shift_up(d0, dn, 1) * wsc_ref[1:2, :]) + _shift_up(d0, dn, 2) * wsc_ref[0:1, :]
        merge = lambda a, b4, b16: ((a[...].astype(F32) + _gather_classes(b4, scr_a, 4))
                                    + _gather_classes(b16, scr_b, 16))
        dq = merge(dq1, dq4, dq16) * (HEAD_DIM ** -0.5)
        dk = merge(dk1, dk4, dk16)
        dv = merge(dv1, dv4, dv16)
        dproj = jnp.concatenate([dq, dk, dv, dgb_ref[...].astype(F32), du * xi_ref[...], du * gc_ref[...]],
                                axis=1).astype(BF16)
        dproj_ref[...] = dproj
        dh = jnp.zeros((TM, D_MODEL), F32)
        for j in range(N_DEV):
            dh = dh + _dot_nt(dproj[:, j * IN_CHUNK:(j + 1) * IN_CHUNK], win_ref[j])
        xh, r = _rms(x_ref[...])
        dg_ref[0:1, :] += jnp.sum(dh * xh, axis=0, keepdims=True)
        gx_ref[...] = dx1_ref[...] + _rms_bwd(xh, r, g_ref[...], dh)

    row = lambda n: pl.BlockSpec((TM, n), lambda i: (i, 0))
    nxt = pl.BlockSpec((SUBLANES, 512), lambda i: (jnp.minimum((i + 1) * tb, last), 0))
    return pl.pallas_call(
        body, name="in_proj_bwd", grid=(n_tiles,),
        out_shape=[jax.ShapeDtypeStruct((s, IN_COLS), BF16), jax.ShapeDtypeStruct((s, D_MODEL), F32),
                   jax.ShapeDtypeStruct((SUBLANES, D_MODEL), F32)],
        in_specs=_class_specs(512) * 3 + [row(512), row(512), nxt, row(512), row(512), _full(w_sc.shape),
                                          _full(w_in_g.shape), row(D_MODEL), _full(g_mix.shape), row(D_MODEL)],
        out_specs=[row(IN_COLS), row(D_MODEL), _full((SUBLANES, D_MODEL))],
        scratch_shapes=[pltpu.VMEM((512 // LANES, TM, LANES), F32)] * 2,
        compiler_params=_cparams(1),
    )(*dqs, *dks, *dvs, dgb, dcv, dcv, gc, xi, w_sc, w_in_g, x, g_mix, dx1)


def _dw(a, b, dep, name, a_chunked=False, b_chunked=False, n_chunks=1, chunk_cols=None):
    ts = TS_DW if (a_chunked or b_chunked or chunk_cols) else TS_DW // 2
    if a_chunked:
        nj, s, kk = a.shape
        nn = b.shape[1]
        a_spec = pl.BlockSpec((1, ts, kk), lambda j, t: (j, t, 0))
        b_spec = pl.BlockSpec((ts, nn), lambda j, t: (t, 0))
    elif b_chunked:
        nj, s, nn = b.shape
        kk = a.shape[1]
        a_spec = pl.BlockSpec((ts, kk), lambda j, t: (t, 0))
        b_spec = pl.BlockSpec((1, ts, nn), lambda j, t: (j, t, 0))
    else:
        s, kk = a.shape
        nj, nn = (n_chunks, chunk_cols) if chunk_cols else (1, b.shape[1])
        a_spec = pl.BlockSpec((ts, kk), lambda j, t: (t, 0))
        b_spec = pl.BlockSpec((ts, nn), lambda j, t: (t, j))
    n_steps = s // ts

    def body(a_ref, b_ref, dep_ref, o_ref, acc):
        t = pl.program_id(1)

        @pl.when(t == 0)
        def _():
            acc[...] = jnp.zeros_like(acc)

        av = (a_ref[0] if a_chunked else a_ref[...]).astype(BF16)
        bv = (b_ref[0] if b_chunked else b_ref[...]).astype(BF16)
        acc[...] += _dot_tn(av, bv)

        @pl.when(t == n_steps - 1)
        def _():
            o_ref[0] = acc[...].astype(BF16)

    return pl.pallas_call(
        body, name=name, grid=(nj, n_steps),
        out_shape=jax.ShapeDtypeStruct((nj, kk, nn), BF16),
        in_specs=[a_spec, b_spec, ANY_SPEC],
        out_specs=pl.BlockSpec((1, kk, nn), lambda j, t: (j, 0, 0)),
        scratch_shapes=[pltpu.VMEM((kk, nn), F32)],
        compiler_params=_cparams(2),
    )(a, b, dep)


def _adamw_math(w, g, m, v):
    m2 = ADAM_B1 * m + (1.0 - ADAM_B1) * g
    v2 = ADAM_B2 * v + (1.0 - ADAM_B2) * (g * g)
    m_hat = m2 / (1.0 - ADAM_B1 ** ADAM_STEP)
    v_hat = v2 / (1.0 - ADAM_B2 ** ADAM_STEP)
    delta = -ADAM_LR * (m_hat / (jnp.sqrt(v_hat) + ADAM_EPS) + ADAM_WD * w)
    return delta, m2, v2


def _sum_parts(me, own, p_ref):
    g = None
    for i in range(N_DEV):
        part = jnp.where(me == i, own.astype(F32), p_ref[i].astype(F32))
        g = part if g is None else g + part
    return g


def _adamw_big(name, w, sent, parts, m, v, me_arr):
    rr, cc = w.shape
    tr = rr // 4 if rr >= 512 else rr

    def body(me_ref, w_ref, own_ref, p_ref, m_ref, v_ref, g_ref, d_ref, nm_ref, nv_ref):
        g = own_ref[0].astype(F32)
        for k in range(1, N_DEV):
            g = g + p_ref[(me_ref[0] + k) % N_DEV].astype(F32)
        g_ref[...] = g
        d_ref[...], nm_ref[...], nv_ref[...] = _adamw_math(w_ref[...], g, m_ref[...], v_ref[...])

    row = pl.BlockSpec((tr, cc), lambda i, me: (i, 0))
    return pl.pallas_call(
        body, name=name,
        grid_spec=pltpu.PrefetchScalarGridSpec(
            num_scalar_prefetch=1, grid=(rr // tr,),
            in_specs=[row, pl.BlockSpec((1, tr, cc), lambda i, me: (me[0], i, 0)),
                      pl.BlockSpec((N_DEV, tr, cc), lambda i, me: (0, i, 0)), row, row],
            out_specs=[row] * 4),
        out_shape=[jax.ShapeDtypeStruct((rr, cc), F32)] * 4,
        compiler_params=_cparams(1),
    )(me_arr, w, sent, parts, m, v)


def _small_slices():
    return [
        (slice(ROW_RELB, ROW_RELB + 8), slice(0, N_BUCKETS)),
        (slice(ROW_GMIX, ROW_GMIX + 1), slice(0, D_MODEL)),
        (slice(ROW_GAC, ROW_GAC + 1), slice(0, ATTN_W)),
        (slice(ROW_GAC, ROW_GAC + 1), slice(ATTN_W, D_MODEL)),
        (slice(ROW_GXATTN, ROW_GXATTN + 1), slice(0, D_MODEL)),
        (slice(ROW_GMEM, ROW_GMEM + 1), slice(0, D_MODEL)),
        (slice(ROW_GFFN, ROW_GFFN + 1), slice(0, D_MODEL)),
        (slice(ROW_BFC, ROW_BFC + 8), slice(0, UP_CHUNK)),
        (slice(ROW_GFINAL, ROW_GFINAL + 1), slice(0, D_MODEL)),
    ]


def _adamw_small(own, parts, wmv, me_arr):
    slices = _small_slices()
    n = len(slices)

    def body(*refs):
        me_ref, own_ref, p_ref = refs[:3]
        ins = refs[3:3 + 3 * n]
        g_ref = refs[3 + 3 * n]
        outs = refs[4 + 3 * n:]
        g = _sum_parts(me_ref[0], own_ref[...], p_ref)
        g_ref[...] = g
        for a, (rs, ls) in enumerate(slices):
            ga = g[rs, ls]
            outs[4 * a][...] = ga
            outs[4 * a + 1][...], outs[4 * a + 2][...], outs[4 * a + 3][...] = _adamw_math(
                ins[3 * a][...], ga, ins[3 * a + 1][...], ins[3 * a + 2][...])

    vm = pl.BlockSpec(memory_space=pltpu.VMEM)
    flat = [t for trip in wmv for t in trip]
    out_shape = [jax.ShapeDtypeStruct((SMALL_ROWS, D_MODEL), F32)]
    for w, _, _ in wmv:
        out_shape += [jax.ShapeDtypeStruct(w.shape, F32)] * 4
    res = pl.pallas_call(
        body, name="adamw_small", out_shape=out_shape,
        in_specs=[SMEM_SPEC] + [vm] * (2 + 3 * n), out_specs=[vm] * len(out_shape),
    )(me_arr, own, parts, *flat)
    return res[0], [res[1 + 4 * a:5 + 4 * a] for a in range(n)]


def _adamw_shards(items):
    n = len(items)

    def body(*refs):
        for a in range(n):
            w_ref, g_ref, m_ref, v_ref = refs[4 * a:4 * a + 4]
            d_ref, nm_ref, nv_ref = refs[4 * n + 3 * a:4 * n + 3 * a + 3]
            d_ref[...], nm_ref[...], nv_ref[...] = _adamw_math(w_ref[...], g_ref[...], m_ref[...], v_ref[...])

    vm = pl.BlockSpec(memory_space=pltpu.VMEM)
    out_shape = []
    for w, _, _, _ in items:
        out_shape += [jax.ShapeDtypeStruct(w.shape, F32)] * 3
    res = pl.pallas_call(
        body, name="adamw_shards", out_shape=out_shape, in_specs=[vm] * (4 * n), out_specs=[vm] * (3 * n),
    )(*[t for it in items for t in it])
    return [res[3 * a:3 * a + 3] for a in range(n)]


def _mesh_pos():
    return lax.axis_index("x"), lax.axis_index("y"), lax.axis_index("c")


def _dev_index(p):
    return 4 * p[0] + 2 * p[1] + p[2]


def _all_gather(shards):
    n = len(shards)

    def body(*refs):
        ins, outs = refs[:n], refs[n:2 * n]
        send_sems, recv_sems, loc_sems = refs[2 * n:]
        x, y, c = _mesh_pos()
        me, sib = (x, y, c), (x, y, 1 - c)
        chips = [(1 - x, y), (x, 1 - y), (1 - x, 1 - y)]

        def cp(a, k, block, to, src=None):
            dst = outs[a].at[_dev_index(block)]
            return pltpu.make_async_remote_copy(
                src_ref=dst if src is None else src, dst_ref=dst, send_sem=send_sems.at[a, k],
                recv_sem=recv_sems.at[a, k], device_id=to, device_id_type=MESH)

        mine = [pltpu.make_async_copy(ins[a], outs[a].at[_dev_index(me)], loc_sems.at[a]) for a in range(n)]
        for m_ in mine:
            m_.start()
        first = []
        for a in range(n):
            first.append(cp(a, 0, me, sib, src=ins[a]))
            first += [cp(a, 1 + j, me, (*chip, c), src=ins[a]) for j, chip in enumerate(chips)]
        for f in first:
            f.start()
        passed = []
        for a in range(n):
            for j, chip in enumerate(chips):
                cp(a, 1 + j, (*chip, c), me).wait_recv()
                fwd = cp(a, 4 + j, (*chip, c), sib)
                fwd.start()
                passed.append(fwd)
        for a in range(n):
            cp(a, 0, sib, me).wait_recv()
            for j, chip in enumerate(chips):
                cp(a, 4 + j, (*chip, 1 - c), me).wait_recv()
        for f in first + passed:
            f.wait_send()
        for m_ in mine:
            m_.wait()

    hbm = pl.BlockSpec(memory_space=pltpu.HBM)
    return pl.pallas_call(
        body, name="all_gather_weights",
        out_shape=[jax.ShapeDtypeStruct((N_DEV,) + a.shape, a.dtype) for a in shards],
        in_specs=[hbm] * n, out_specs=[hbm] * n,
        scratch_shapes=[pltpu.SemaphoreType.DMA((n, 7)), pltpu.SemaphoreType.DMA((n, 7)),
                        pltpu.SemaphoreType.DMA((n,))],
    )(*shards)


def _peers():
    x, y, c = _mesh_pos()
    return (x, y, c), [((1 - x) if k & 4 else x, (1 - y) if k & 2 else y, (1 - c) if k & 1 else c)
                       for k in range(1, 8)]


def _exchange_copy(src_ref, land_ref, whole, send_sems, recv_sems, a, k, peer, slot):
    src = src_ref if whole else src_ref.at[_dev_index(peer)]
    return pltpu.make_async_remote_copy(
        src_ref=src, dst_ref=land_ref.at[slot], send_sem=send_sems.at[7 * a + k], recv_sem=recv_sems.at[7 * a + k],
        device_id=peer, device_id_type=MESH)


def _exchange_start(name, srcs, whole, dep):
    n = len(srcs)
    lands = [lax.empty(((N_DEV,) + s.shape) if w else s.shape, s.dtype) for s, w in zip(srcs, whole)]

    def body(*refs):
        src_refs, land_refs = refs[:n], refs[n:2 * n]
        send_sems, recv_sems, token = refs[2 * n + 1], refs[2 * n + 2], refs[-1]
        me, peers = _peers()
        for a in range(n):
            for k, peer in enumerate(peers):
                _exchange_copy(src_refs[a], land_refs[a], whole[a], send_sems, recv_sems, a, k, peer,
                               _dev_index(me)).start()
        token[...] = jnp.zeros_like(token)

    res = pl.pallas_call(
        body, name=name,
        out_shape=(pltpu.SemaphoreType.DMA((7 * n,)), pltpu.SemaphoreType.DMA((7 * n,)),
                   *[pltpu.HBM(a.shape, a.dtype) for a in srcs], *[pltpu.HBM(a.shape, a.dtype) for a in lands],
                   jax.ShapeDtypeStruct((SUBLANES, 128), F32)),
        in_specs=[HBM_SPEC] * (2 * n) + [ANY_SPEC],
        out_specs=(SEM_SPEC, SEM_SPEC, *([HBM_SPEC] * (2 * n)), VMEM_SPEC),
        input_output_aliases={i: 2 + i for i in range(2 * n)},
        compiler_params=pltpu.CompilerParams(has_side_effects=DATAFLOW),
    )(*[pltpu.with_memory_space_constraint(a, pltpu.HBM) for a in srcs],
      *[pltpu.with_memory_space_constraint(a, pltpu.HBM) for a in lands], dep)
    return res[0], res[1], list(res[2:2 + n]), list(res[2 + n:2 + 2 * n]), res[-1]


def _exchange_wait(name, started, whole, after, which=None):
    send_sems, recv_sems, srcs, lands, _ = started
    which = list(range(len(srcs))) if which is None else which
    srcs, lands = [srcs[a] for a in which], [lands[a] for a in which]
    n = len(srcs)

    def body(*refs):
        src_refs, land_refs = refs[:n], refs[n:2 * n]
        send_sems, recv_sems = refs[2 * n], refs[2 * n + 1]
        _, peers = _peers()
        for i, a in enumerate(which):
            for k, peer in enumerate(peers):
                cp = _exchange_copy(src_refs[i], land_refs[i], whole[a], send_sems, recv_sems, a, k, peer,
                                    _dev_index(peer))
                cp.wait_send()
                cp.wait_recv()

    res = pl.pallas_call(
        body, name=name,
        out_shape=[pltpu.HBM(a.shape, a.dtype) for a in srcs + lands],
        in_specs=[HBM_SPEC] * (2 * n) + [SEM_SPEC, SEM_SPEC, ANY_SPEC],
        out_specs=[HBM_SPEC] * (2 * n),
        input_output_aliases={i: i for i in range(2 * n)},
        compiler_params=pltpu.CompilerParams(has_side_effects=DATAFLOW),
    )(*srcs, *lands, send_sems, recv_sems, after)
    return list(res[:n]), list(res[n:])


def _gather_start(name, shards, dep):
    n = len(shards)
    lands = [lax.empty((N_DEV,) + a.shape, a.dtype) for a in shards]

    def body(*refs):
        src_refs, land_refs = refs[:n], refs[n:2 * n]
        send_sems, recv_sems, token = refs[2 * n + 1], refs[2 * n + 2], refs[-1]
        x, y, c = _mesh_pos()
        peers = [(x, y, 1 - c), (1 - x, y, c), (x, 1 - y, c), (1 - x, 1 - y, c)]
        for a in range(n):
            for k, peer in enumerate(peers):
                pltpu.make_async_remote_copy(
                    src_ref=src_refs[a], dst_ref=land_refs[a].at[_dev_index((x, y, c))], send_sem=send_sems.at[4 * a + k],
                    recv_sem=recv_sems.at[4 * a + k], device_id=peer, device_id_type=MESH).start()
        token[...] = jnp.zeros_like(token)

    res = pl.pallas_call(
        body, name=name,
        out_shape=(pltpu.SemaphoreType.DMA((4 * n,)), pltpu.SemaphoreType.DMA((4 * n,)),
                   *[pltpu.HBM(a.shape, a.dtype) for a in shards], *[pltpu.HBM(a.shape, a.dtype) for a in lands],
                   jax.ShapeDtypeStruct((SUBLANES, 128), F32)),
        in_specs=[HBM_SPEC] * (2 * n) + [ANY_SPEC],
        out_specs=(SEM_SPEC, SEM_SPEC, *([HBM_SPEC] * (2 * n)), VMEM_SPEC),
        input_output_aliases={i: 2 + i for i in range(2 * n)},
        compiler_params=pltpu.CompilerParams(has_side_effects=DATAFLOW),
    )(*[pltpu.with_memory_space_constraint(a, pltpu.HBM) for a in shards],
      *[pltpu.with_memory_space_constraint(a, pltpu.HBM) for a in lands], dep)
    return res[0], res[1], list(res[2:2 + n]), list(res[2 + n:2 + 2 * n]), res[-1]


def _gather_forward(name, send_sems, recv_sems, lands, which, after):
    n = len(which)

    def body(*refs):
        land_refs = refs[:n]
        send_sems, recv_sems = refs[n], refs[n + 1]
        fsend, frecv, token = refs[n + 3], refs[n + 4], refs[-1]
        x, y, c = _mesh_pos()
        chips = [(1 - x, y), (x, 1 - y), (1 - x, 1 - y)]
        for i, a in enumerate(which):
            for j, chip in enumerate(chips):
                block = land_refs[i].at[_dev_index((*chip, c))]
                pltpu.make_async_remote_copy(
                    src_ref=block, dst_ref=block, send_sem=send_sems.at[4 * a + 1 + j], recv_sem=recv_sems.at[4 * a + 1 + j],
                    device_id=(*chip, c), device_id_type=MESH).wait_recv()
                pltpu.make_async_remote_copy(
                    src_ref=block, dst_ref=block, send_sem=fsend.at[3 * i + j], recv_sem=frecv.at[3 * i + j],
                    device_id=(x, y, 1 - c), device_id_type=MESH).start()
        token[...] = jnp.zeros_like(token)

    res = pl.pallas_call(
        body, name=name,
        out_shape=(pltpu.SemaphoreType.DMA((3 * n,)), pltpu.SemaphoreType.DMA((3 * n,)),
                   *[pltpu.HBM(a.shape, a.dtype) for a in lands], jax.ShapeDtypeStruct((SUBLANES, 128), F32)),
        in_specs=[HBM_SPEC] * n + [SEM_SPEC, SEM_SPEC, ANY_SPEC],
        out_specs=(SEM_SPEC, SEM_SPEC, *([HBM_SPEC] * n), VMEM_SPEC),
        input_output_aliases={i: 2 + i for i in range(n)},
        compiler_params=pltpu.CompilerParams(has_side_effects=DATAFLOW),
    )(*lands, send_sems, recv_sems, after)
    return res[0], res[1], list(res[2:2 + n]), res[-1]


def _gather_wait(name, send_sems, recv_sems, fsend, frecv, srcs, lands, which, after):
    n = len(which)

    def body(*refs):
        land_refs = refs[n:2 * n]
        send_sems, recv_sems, fsend, frecv = refs[2 * n:2 * n + 4]
        x, y, c = _mesh_pos()
        sib = (x, y, 1 - c)
        chips = [(1 - x, y), (x, 1 - y), (1 - x, 1 - y)]
        for i, a in enumerate(which):
            def cp(slot, ssem, rsem):
                block = land_refs[i].at[_dev_index(slot)]
                return pltpu.make_async_remote_copy(src_ref=block, dst_ref=block, send_sem=ssem, recv_sem=rsem,
                                                    device_id=sib, device_id_type=MESH)
            cp(sib, send_sems.at[4 * a], recv_sems.at[4 * a]).wait_recv()
            for j, chip in enumerate(chips):
                cp((*chip, 1 - c), fsend.at[3 * i + j], frecv.at[3 * i + j]).wait_recv()
            for k in range(4):
                cp(sib, send_sems.at[4 * a + k], recv_sems.at[4 * a + k]).wait_send()
            for j in range(3):
                cp(sib, fsend.at[3 * i + j], frecv.at[3 * i + j]).wait_send()

    res = pl.pallas_call(
        body, name=name,
        out_shape=[pltpu.HBM(a.shape, a.dtype) for a in srcs + lands],
        in_specs=[HBM_SPEC] * (2 * n) + [SEM_SPEC] * 4 + [ANY_SPEC],
        out_specs=[HBM_SPEC] * (2 * n),
        input_output_aliases={i: i for i in range(2 * n)},
        compiler_params=pltpu.CompilerParams(has_side_effects=DATAFLOW),
    )(*srcs, *lands, send_sems, recv_sems, fsend, frecv, after)
    return list(res[n:])


def _local_step(x, mem, target, rel_bias, g_mix, w_in_g, w_sc, g_a, g_c, g_xattn, g_mem, g_ffn, w_fc, b_fc, g_final,
                dep, forward_weights, late_weights, emit, emit_small):
    s = x.shape[0]
    buckets = _bucket_tables()
    bias = _bias_fwd(rel_bias, buckets)

    h1, qs, ks, vs, gb, gc, xi = _rms_proj(x, g_mix, w_in_g, dep)
    qs, ks, vs = ([a[0][None]] + list(a[1:]) for a in (qs, ks, vs))
    group1, group2 = ["w_out", "w_xq", "w_xk", "w_xv", "w_xo"], ["w_up", "w_down"]
    tok = forward_weights(group1, h1)
    branches = []
    for p, dil in enumerate(DILATIONS):
        o_p, lse_p = _swa_fwd(qs[p], ks[p], vs[p], bias[p], dil, tok)
        branches.append([o_p[0], lse_p[0]] if dil == 1 else [o_p, lse_p])
    lw = late_weights(group1, branches[-1][0])
    w_out, w_xq, w_xk, w_xv, w_xo = (lw[n] for n in group1)
    attn, lses, mixed, x1 = _mix_out(branches, gb, gc, xi, x, w_sc, g_a, g_c, w_out)
    tok = forward_weights(group2, x1)
    mem_n, mk, mv = _mem_kv(mem, g_mem, w_xk, w_xv)
    h2, xq, xo, x2 = _xattn_fwd(x1, g_xattn, w_xq, mk, mv, w_xo, tok)
    lw = late_weights(group2, x2)
    w_up_g, w_down_g = lw["w_up"], lw["w_down"]
    h3, up, conv, act, dx3, loss_acc, dg_final = _ffn_fwd(x2, g_ffn, w_up_g, w_fc, b_fc, w_down_g, g_final, target)

    gw_down = _dw(act, dx3, dep, "dw_down", a_chunked=True)
    dup, dx2, dg_ffn, dw_fc, db_fc = _ffn_bwd(dx3, up, conv, x2, g_ffn, w_up_g, w_fc, w_down_g)
    gw_up = _dw(dup, h3, dep, "dw_up", a_chunked=True)
    tok = emit(dict(w_down=gw_down, w_up=gw_up))
    dxq, dx1, dmk, dmv, dg_xattn = _xattn_bwd(dx2, xo, xq, mk, mv, w_xo, w_xq, x1, g_xattn, tok)
    gw_xo = _dw(xo, dx2, tok, "dw_xo")[0]
    gw_xq = _dw(h2, dxq, tok, "dw_xq")[0]
    gw_xk, gw_xv, dg_mem = _mem_kv_bwd(dmk, dmv, mem_n, mem, w_xk, w_xv)
    tok = emit(dict(w_xo=gw_xo, w_xq=gw_xq, w_xk=gw_xk, w_xv=gw_xv))
    dattns, dds, dgb, dcv, dg_a, dg_c, dw_sc = _mix_out_bwd(dx1, w_out, attn, gb, gc, xi, w_sc, g_a, g_c, tok)
    first = lambda a: [a[0][None]] + list(a[1:])
    dattns, dds, lses = first(dattns), first(dds), first(lses)
    gw_out = _dw(mixed, dx1, tok, "dw_out")[0]
    tok = emit(dict(w_out=gw_out))
    dqs, dks, dvs, dbias = [], [], [], []
    for p, dil in enumerate(DILATIONS):
        dq_p, dk_p, dv_p, db_p = _swa_bwd(qs[p], ks[p], vs[p], dattns[p], lses[p], dds[p], bias[p], dil, tok)
        dqs.append(dq_p[0] if dil == 1 else dq_p)
        dks.append(dk_p[0] if dil == 1 else dk_p)
        dvs.append(dv_p[0] if dil == 1 else dv_p)
        dbias.append(db_p)
    d_relb = _bias_bwd(jnp.stack(dbias), buckets)
    dproj, grad_x, dg_mix = _in_proj_bwd(dqs, dks, dvs, dgb, dcv, gc, xi, w_sc, w_in_g, x, g_mix, dx1)
    pad = lambda a: jnp.pad(a, ((0, 0), (0, D_MODEL - a.shape[1])))
    small = jnp.concatenate([
        d_relb, dg_mix, dg_xattn, dg_mem, dg_ffn, dg_final, jnp.concatenate([dg_a, dg_c], axis=1),
        pad(dw_sc), pad(db_fc), pad(dw_fc.reshape(3 * N_DEV, UP_CHUNK)), pad(loss_acc)], axis=0)
    tok = emit_small(small)
    gw_in = _dw(h1, dproj, tok, "dw_in", n_chunks=N_DEV, chunk_cols=IN_CHUNK)
    emit(dict(w_in=gw_in))
    return grad_x


def kernel(x, mem, rel_bias, g_mix, w_in, w_short_conv, g_attn_out, g_conv_out, w_out, g_xattn, g_mem, w_xq, w_xk, w_xv, w_xo, g_ffn, w_up, w_ffn_conv, b_ffn_conv, w_down, g_final, loss_target, m_rel_bias, m_g_mix, m_w_in, m_w_short_conv, m_g_attn_out, m_g_conv_out, m_w_out, m_g_xattn, m_g_mem, m_w_xq, m_w_xk, m_w_xv, m_w_xo, m_g_ffn, m_w_up, m_w_ffn_conv, m_b_ffn_conv, m_w_down, m_g_final, v_rel_bias, v_g_mix, v_w_in, v_w_short_conv, v_g_attn_out, v_g_conv_out, v_w_out, v_g_xattn, v_g_mem, v_w_xq, v_w_xk, v_w_xv, v_w_xo, v_g_ffn, v_w_up, v_w_ffn_conv, v_b_ffn_conv, v_w_down, v_g_final):
    me = _dev_index(_mesh_pos())
    me_arr = me.reshape(1).astype(jnp.int32)

    big_names = ["w_in", "w_out", "w_xq", "w_xk", "w_xv", "w_xo", "w_up", "w_down"]
    late_names = big_names[1:]
    big_w = dict(w_in=w_in[0], w_out=w_out[0], w_xq=w_xq[0], w_xk=w_xk[0], w_xv=w_xv[0], w_xo=w_xo[0],
                 w_up=w_up[0].T, w_down=w_down[0])
    big_m = dict(w_in=m_w_in[0], w_out=m_w_out[0], w_xq=m_w_xq[0], w_xk=m_w_xk[0], w_xv=m_w_xv[0], w_xo=m_w_xo[0],
                 w_up=m_w_up[0].T, w_down=m_w_down[0])
    big_v = dict(w_in=v_w_in[0], w_out=v_w_out[0], w_xq=v_w_xq[0], w_xk=v_w_xk[0], w_xv=v_w_xv[0], w_xo=v_w_xo[0],
                 w_up=v_w_up[0].T, w_down=v_w_down[0])
    shard_shape = {n: big_w[n].shape for n in big_names}

    w_in_g, w_sc_g, w_fc_full = _all_gather([big_w["w_in"].astype(BF16), w_short_conv[0], w_ffn_conv[0]])
    w_sc_full = w_sc_g.transpose(1, 0, 2).reshape(3, CONV_W)
    late_shards = [big_w[n].astype(BF16) for n in late_names]
    ag_send, ag_recv, ag_srcs, ag_lands, ag_token = _gather_start("gather_weights_start", late_shards, w_in_g)
    forwarded = {}

    def forward_weights(names, after):
        which = [late_names.index(n) for n in names]
        fsend, frecv, lands, token = _gather_forward("gather_" + "_".join(names) + "_forward", ag_send, ag_recv,
                                                     [ag_lands[a] for a in which], which, after)
        forwarded[tuple(names)] = (fsend, frecv, lands)
        return token

    def late_weights(names, after):
        which = [late_names.index(n) for n in names]
        fsend, frecv, lands = forwarded[tuple(names)]
        lands = _gather_wait("gather_" + "_".join(names) + "_wait", ag_send, ag_recv, fsend, frecv,
                             [ag_srcs[a] for a in which], lands, which, after)
        out = {}
        for n, a, land in zip(names, which, lands):
            full = lax.dynamic_update_index_in_dim(land, late_shards[a], me, 0)
            if n == "w_up":
                out[n] = full
            elif n == "w_down":
                out[n] = full.reshape(N_DEV // 2, UP_CHUNK, D_MODEL)
            else:
                out[n] = full.reshape(D_MODEL, D_MODEL)
        return out

    sent = []

    def emit(grads):
        names = list(grads)
        blocks = [grads[n].reshape((N_DEV,) + shard_shape[n]) for n in names]
        started = _exchange_start("scatter_" + "_".join(names) + "_start", blocks, [False] * len(names), me_arr)
        sent.append((names, started))
        return started[-1]

    def emit_small(small):
        sent_small.append((small, _exchange_start("gather_small_start", [small], [True], me_arr)))
        return sent_small[0][1][-1]

    sent_small = []
    grad_x = _local_step(
        x[0], mem[0], loss_target[0], rel_bias, g_mix, w_in_g, w_sc_full, g_attn_out, g_conv_out, g_xattn, g_mem,
        g_ffn, w_fc_full, b_ffn_conv.reshape(N_DEV, 1, UP_CHUNK), g_final.reshape(1, D_MODEL), ag_token,
        forward_weights, late_weights, emit, emit_small)

    small_g, small_started = sent_small[0]
    after = sent[-1][1][-1]
    small_parts = _exchange_wait("gather_small_wait", small_started, [True], after)[1][0]
    big_out = {}
    after = small_parts
    for names, started in sent:
        blocks, lands = _exchange_wait("scatter_" + "_".join(names) + "_wait", started, [False] * len(names), after)
        for n, block, land in zip(names, blocks, lands):
            res = _adamw_big("adamw_" + n, big_w[n], block, land, big_m[n], big_v[n], me_arr)
            big_out[n] = [(r.T if n == "w_up" else r)[None] for r in res]
            after = res[0]

    as_rows = lambda a: a.reshape(N_DEV, UP_CHUNK)
    row1 = lambda a: a.reshape(1, D_MODEL)
    small_names = ["rel_bias", "g_mix", "g_attn_out", "g_conv_out", "g_xattn", "g_mem", "g_ffn", "b_ffn_conv", "g_final"]
    wmv = [
        (rel_bias, m_rel_bias, v_rel_bias), (g_mix, m_g_mix, v_g_mix), (g_attn_out, m_g_attn_out, v_g_attn_out),
        (g_conv_out, m_g_conv_out, v_g_conv_out), (g_xattn, m_g_xattn, v_g_xattn), (g_mem, m_g_mem, v_g_mem),
        (g_ffn, m_g_ffn, v_g_ffn), (as_rows(b_ffn_conv), as_rows(m_b_ffn_conv), as_rows(v_b_ffn_conv)),
        (row1(g_final), row1(m_g_final), row1(v_g_final))]
    g_packed, small_res = _adamw_small(small_g, small_parts, wmv, me_arr)
    small_out = dict(zip(small_names, small_res))
    loss = g_packed[ROW_LOSS, 0]
    small_out["b_ffn_conv"] = [a.reshape(1, 2 * D_FF) for a in small_out["b_ffn_conv"]]
    small_out["g_final"] = [a.reshape(D_MODEL) for a in small_out["g_final"]]

    g_wsc = lax.dynamic_slice(g_packed[ROW_WSC:ROW_WSC + 3, 0:CONV_W], (0, me * HEAD_DIM), (3, HEAD_DIM))
    g_wfc = lax.dynamic_slice(g_packed[ROW_WFC:ROW_WFC + 3 * N_DEV, 0:UP_CHUNK].reshape(3, N_DEV, UP_CHUNK),
                              (0, me, 0), (3, 1, UP_CHUNK)).reshape(3, UP_CHUNK)
    shard_res = _adamw_shards([(w_short_conv[0], g_wsc, m_w_short_conv[0], v_w_short_conv[0]),
                               (w_ffn_conv[0], g_wfc, m_w_ffn_conv[0], v_w_ffn_conv[0])])
    small_out["w_short_conv"] = [g_wsc[None]] + [a[None] for a in shard_res[0]]
    small_out["w_ffn_conv"] = [g_wfc[None]] + [a[None] for a in shard_res[1]]

    order = ["rel_bias", "g_mix", "w_in", "w_short_conv", "g_attn_out", "g_conv_out", "w_out", "g_xattn", "g_mem",
             "w_xq", "w_xk", "w_xv", "w_xo", "g_ffn", "w_up", "w_ffn_conv", "b_ffn_conv", "w_down", "g_final"]
    allp = {**big_out, **small_out}
    outs = [loss, grad_x[None]]
    for kind in range(4):
        outs += [allp[n][kind] for n in order]
    return tuple(outs)
```

```python
import math

import numpy as np
import jax
import jax.numpy as jnp
from jax import lax
from jax.experimental import pallas as pl
from jax.experimental.pallas import tpu as pltpu

F32 = jnp.float32
BF16 = jnp.bfloat16
MESH = pl.DeviceIdType.MESH

N_DEV = 8
D_MODEL = 1024
ATTN_W = 512
CONV_W = 512
N_HEADS = 8
HEAD_DIM = 64
WIN = 128
DILATIONS = (1, 4, 16)
N_BUCKETS = 32
BUCKET_MAX_EXACT = 16
BUCKET_MAX_DISTANCE = 2048
N_MEM_HEADS = 4
MEM_HEAD_DIM = 256
D_FF = 2816
IN_COLS = 3072
IN_CHUNK = IN_COLS // N_DEV
UP_CHUNK = 2 * D_FF // N_DEV
FFN_CHUNKS = 4
FFN_WIDTH = 2 * D_FF // FFN_CHUNKS
EPS = 1e-6

ADAM_LR = 0.001
ADAM_B1 = 0.9
ADAM_B2 = 0.999
ADAM_EPS = 1e-08
ADAM_WD = 0.01
ADAM_STEP = 10

SUBLANES = 8
LANES = 128
HALO = 16
TM = 512
TM_FFN = 256
TS_DW = 4096
SWA_BLOCKS = 8
VMEM_LIMIT = 56 * 1024 * 1024

ROW_RELB, ROW_GMIX, ROW_GXATTN, ROW_GMEM, ROW_GFFN, ROW_GFINAL, ROW_GAC = 0, 8, 16, 24, 32, 40, 48
ROW_WSC, ROW_BFC, ROW_WFC, ROW_LOSS, SMALL_ROWS = 56, 64, 72, 96, 104


def _cparams(n_grid):
    return pltpu.CompilerParams(dimension_semantics=("arbitrary",) * n_grid, vmem_limit_bytes=VMEM_LIMIT)


def _full(shape):
    nd = len(shape)
    return pl.BlockSpec(tuple(shape), lambda *_: (0,) * nd)


def _resident(shape):
    nd = len(shape)
    return pl.BlockSpec(tuple(shape), lambda *_: (0,) * nd, pipeline_mode=pl.Buffered(1))


ANY_SPEC = pl.BlockSpec(memory_space=pl.ANY)
HBM_SPEC = pl.BlockSpec(memory_space=pltpu.HBM)
SEM_SPEC = pl.BlockSpec(memory_space=pltpu.SEMAPHORE)
VMEM_SPEC = pl.BlockSpec(memory_space=pltpu.VMEM)
SMEM_SPEC = pl.BlockSpec(memory_space=pltpu.SMEM)
DATAFLOW = pltpu.SideEffectType.DATAFLOW_SIDE_EFFECTING


def _rms(x):
    r = lax.rsqrt(jnp.mean(x * x, axis=-1, keepdims=True) + EPS)
    return x * r, r


def _rms_bwd(xh, r, g, dy):
    dxh = dy * g
    return r * (dxh - xh * jnp.mean(dxh * xh, axis=-1, keepdims=True))


def _shift_down(u, halo, k):
    ru = pltpu.roll(u, k, 0)
    rh = pltpu.roll(halo, k, 0)
    row = lax.broadcasted_iota(jnp.int32, rh.shape, 0)
    head = jnp.where(row < k, rh, ru[0:SUBLANES])
    return jnp.concatenate([head, ru[SUBLANES:]], axis=0)


def _shift_up(u, halo, k):
    tm = u.shape[0]
    ru = pltpu.roll(u, tm - k, 0)
    rh = pltpu.roll(halo, SUBLANES - k, 0)
    row = lax.broadcasted_iota(jnp.int32, rh.shape, 0)
    tail = jnp.where(row >= SUBLANES - k, rh, ru[tm - SUBLANES:])
    return jnp.concatenate([ru[:tm - SUBLANES], tail], axis=0)


def _causal_conv3(u, halo, w_ref):
    return (_shift_down(u, halo, 2) * w_ref[0:1, :] + _shift_down(u, halo, 1) * w_ref[1:2, :]) + u * w_ref[2:3, :]


def _dot(a, b):
    return jnp.dot(a, b, preferred_element_type=F32)


def _dot_nt(a, b):
    return lax.dot_general(a, b, (((1,), (1,)), ((), ())), preferred_element_type=F32)


def _dot_tn(a, b):
    return lax.dot_general(a, b, (((0,), (0,)), ((), ())), preferred_element_type=F32)


def _sigmoid(x):
    return 0.5 * jnp.tanh(0.5 * x) + 0.5


def _bucket_tables():
    qi = np.arange(WIN)[:, None]
    kj = np.arange(2 * WIN)[None, :]
    steps = np.clip(qi + WIN - kj, 0, WIN)
    out = []
    for d in DILATIONS:
        dist = steps * d
        dd = np.maximum(dist, 1).astype(np.float32)
        large = BUCKET_MAX_EXACT + (
            np.log(dd / np.float32(BUCKET_MAX_EXACT)) / np.float32(math.log(BUCKET_MAX_DISTANCE / BUCKET_MAX_EXACT))
            * np.float32(N_BUCKETS - BUCKET_MAX_EXACT)).astype(np.int32)
        large = np.minimum(large, N_BUCKETS - 1)
        out.append(np.where(dist < BUCKET_MAX_EXACT, dist, large).astype(np.int32))
    return np.stack(out)


def _band_mask():
    qi = lax.broadcasted_iota(jnp.int32, (WIN, 2 * WIN), 0)
    kj = lax.broadcasted_iota(jnp.int32, (WIN, 2 * WIN), 1)
    steps = qi + WIN - kj
    return (steps >= 0) & (steps <= WIN)


def _bias_fwd(rel_bias, buckets):
    present = [sorted(set(buckets[p].ravel().tolist())) for p in range(3)]

    def body(rb_ref, bk_ref, o_ref):
        band = _band_mask()
        for p in range(3):
            bk = bk_ref[p]
            for h in range(N_HEADS):
                acc = jnp.zeros((WIN, 2 * WIN), F32)
                for b in present[p]:
                    acc = jnp.where(bk == b, rb_ref[h, b], acc)
                o_ref[p, h] = jnp.where(band, acc, -jnp.inf)

    return pl.pallas_call(
        body, name="bias_fwd",
        out_shape=jax.ShapeDtypeStruct((3, N_HEADS, WIN, 2 * WIN), F32),
        in_specs=[pl.BlockSpec(memory_space=pltpu.SMEM), pl.BlockSpec(memory_space=pltpu.VMEM)],
        out_specs=pl.BlockSpec(memory_space=pltpu.VMEM),
    )(rel_bias, jnp.asarray(buckets))


def _bias_bwd(dbias, buckets):
    present = [set(buckets[p].ravel().tolist()) for p in range(3)]

    def body(db_ref, bk_ref, o_ref):
        lane = lax.broadcasted_iota(jnp.int32, (1, D_MODEL), 1)
        rows = []
        for h in range(N_HEADS):
            row = jnp.zeros((1, D_MODEL), F32)
            for b in range(N_BUCKETS):
                tot = jnp.zeros((1, 1), F32)
                for p in (p for p in range(3) if b in present[p]):
                    sel = jnp.where(bk_ref[p] == b, db_ref[p, h], 0.0)
                    tot = tot + jnp.sum(jnp.sum(sel, axis=0, keepdims=True), axis=1, keepdims=True)
                row = jnp.where(lane == b, tot, row)
            rows.append(row)
        o_ref[...] = jnp.concatenate(rows, axis=0)

    return pl.pallas_call(
        body, name="bias_bwd",
        out_shape=jax.ShapeDtypeStruct((N_HEADS, D_MODEL), F32),
        in_specs=[pl.BlockSpec(memory_space=pltpu.VMEM), pl.BlockSpec(memory_space=pltpu.VMEM)],
        out_specs=pl.BlockSpec(memory_space=pltpu.VMEM),
    )(dbias, jnp.asarray(buckets))


def _spread(val, scr_ref, out_refs, dtype):
    out_refs[0][...] = val.astype(dtype)
    n_blk = val.shape[1] // LANES
    for c in range(n_blk):
        scr_ref[c] = val[:, c * LANES:(c + 1) * LANES]
    for o_ref, d in zip(out_refs[1:], DILATIONS[1:]):
        for r in range(d):
            for c in range(n_blk):
                o_ref[r, :, c * LANES:(c + 1) * LANES] = scr_ref.at[c][pl.ds(r, TM // d, stride=d), :].astype(dtype)


def _gather_classes(blk_ref, scr_ref, d):
    n_blk = blk_ref.shape[2] // LANES
    for r in range(d):
        for c in range(n_blk):
            scr_ref.at[c][pl.ds(r, TM // d, stride=d), :] = blk_ref[r, :, c * LANES:(c + 1) * LANES].astype(F32)
    return jnp.concatenate([scr_ref[c] for c in range(n_blk)], axis=1)


def _class_specs(cols):
    return [pl.BlockSpec((TM, cols), lambda i: (i, 0))] + [
        pl.BlockSpec((d, TM // d, cols), lambda i: (0, i, 0)) for d in DILATIONS[1:]]


def _class_shapes(s, cols, dtype):
    return [jax.ShapeDtypeStruct((s, cols), dtype)] + [
        jax.ShapeDtypeStruct((d, s // d, cols), dtype) for d in DILATIONS[1:]]


def _rms_proj(x, g_mix, w_in_g, dep):
    s = x.shape[0]

    def body(x_ref, g_ref, w_ref, dep_ref, h_ref, q1, q4, q16, k1, k4, k16, v1, v4, v16, gb_ref, gc_ref, xi_ref, scr):
        xh, _ = _rms(x_ref[...])
        h = (xh * g_ref[...]).astype(BF16)
        h_ref[...] = h
        proj = jnp.concatenate([_dot(h, w_ref[j]) for j in range(N_DEV)], axis=1)
        _spread(proj[:, 0:512] * (HEAD_DIM ** -0.5), scr, (q1, q4, q16), BF16)
        _spread(proj[:, 512:1024], scr, (k1, k4, k16), BF16)
        _spread(proj[:, 1024:1536], scr, (v1, v4, v16), BF16)
        gb_ref[...] = proj[:, 1536:2048]
        gc_ref[...] = proj[:, 2048:2560]
        xi_ref[...] = proj[:, 2560:3072]

    row = lambda n: pl.BlockSpec((TM, n), lambda i: (i, 0))
    res = pl.pallas_call(
        body, name="rms_proj", grid=(s // TM,),
        out_shape=[jax.ShapeDtypeStruct((s, D_MODEL), BF16)] + _class_shapes(s, 512, BF16) * 3
        + [jax.ShapeDtypeStruct((s, 512), F32)] * 3,
        in_specs=[row(D_MODEL), _full(g_mix.shape), _full(w_in_g.shape), ANY_SPEC],
        out_specs=[row(D_MODEL)] + _class_specs(512) * 3 + [row(512)] * 3,
        scratch_shapes=[pltpu.VMEM((512 // LANES, TM, LANES), F32)],
        compiler_params=_cparams(1),
    )(x, g_mix, w_in_g, dep)
    return res[0], res[1:4], res[4:7], res[7:10], res[10], res[11], res[12]


def _pair_split(x2):
    lane = lax.broadcasted_iota(jnp.int32, x2.shape, 1)
    zero = jnp.zeros_like(x2)
    return jnp.where(lane < HEAD_DIM, x2, zero), jnp.where(lane >= HEAD_DIM, x2, zero)


def _pair_join(even, odd):
    lane = lax.broadcasted_iota(jnp.int32, (even.shape[0], LANES), 1)
    return jnp.where(lane < HEAD_DIM, even, odd)


def _swa_steps(qc, dil):
    n128 = qc.shape[1] // WIN
    nsub = min(SWA_BLOCKS, n128)
    nb = n128 // nsub
    ncls = min(dil, SWA_BLOCKS // nsub) if nb == 1 else 1
    return nsub, nb, ncls


def _swa_fwd(qc, kc, vc, bias, dil, dep):
    nsub, nb, ncls = _swa_steps(qc, dil)
    whole = nb == 1

    def body(q_ref, kp_ref, kc_ref, vp_ref, vc_ref, b_ref, dep_ref, o_ref, lse_ref, s_scr, p_scr):
        no_prev = (pl.program_id(1) == 0) & (lax.broadcasted_iota(jnp.int32, (WIN, 2 * WIN), 1) < WIN)
        pairs = [slice(a * LANES, (a + 1) * LANES) for a in range(N_HEADS // 2)]
        for c, t in [(c, t) for c in range(ncls) for t in range(nsub)]:
            i = c * nsub + t
            rows = slice(t * WIN, (t + 1) * WIN)
            alone = whole and t == 0
            cols = slice(WIN, 2 * WIN) if alone else slice(0, 2 * WIN)

            def keys(prev_ref, cur_ref, sl):
                if alone:
                    return cur_ref[c, rows, sl]
                if t == 0:
                    return jnp.concatenate([prev_ref[c, :, sl], cur_ref[c, rows, sl]], axis=0)
                return cur_ref[c, (t - 1) * WIN:(t + 1) * WIN, sl]

            for a, sl in enumerate(pairs):
                k2 = keys(kp_ref, kc_ref, sl)
                for e, qh in enumerate(_pair_split(q_ref[c, rows, sl])):
                    s_scr[i, 2 * a + e, :, cols] = _dot_nt(qh, k2)
            den, lse = [], []
            for h in range(N_HEADS):
                lg = s_scr[i, h, :, cols] + b_ref[h, :, cols]
                if t == 0 and not whole:
                    lg = jnp.where(no_prev, -jnp.inf, lg)
                m = jnp.max(lg, axis=-1, keepdims=True)
                p = jnp.exp(lg - m)
                den.append(jnp.sum(p, axis=-1, keepdims=True))
                p_scr[i, h, :, cols] = p.astype(BF16)
                lse.append(m + jnp.log(den[h]))
            for a, sl in enumerate(pairs):
                v_even, v_odd = _pair_split(keys(vp_ref, vc_ref, sl))
                o2 = _dot(p_scr[i, 2 * a, :, cols], v_even) + _dot(p_scr[i, 2 * a + 1, :, cols], v_odd)
                o_ref[c, rows, sl] = o2 / _pair_join(den[2 * a], den[2 * a + 1])
                lse_ref[c, rows, sl] = _pair_join(lse[2 * a], lse[2 * a + 1])

    cur = pl.BlockSpec((ncls, nsub * WIN, 512), lambda r, b: (r, b, 0))
    prev = pl.BlockSpec((ncls, WIN, 512), lambda r, b: (r, jnp.maximum(nsub * b - 1, 0), 0))
    wide = (ncls * nsub, N_HEADS, WIN, 2 * WIN)
    return pl.pallas_call(
        body, name=f"swa_fwd_d{dil}", grid=(dil // ncls, nb),
        out_shape=[jax.ShapeDtypeStruct(qc.shape, F32)] * 2,
        in_specs=[cur, prev, cur, prev, cur, _full(bias.shape), ANY_SPEC],
        out_specs=[cur] * 2,
        scratch_shapes=[pltpu.VMEM(wide, F32), pltpu.VMEM(wide, BF16)],
        compiler_params=_cparams(2),
    )(qc, kc, kc, vc, vc, bias, dep)


def _mix_out(branches, gb, gc, xi, x, w_sc, g_a, g_c, w_out):
    s = x.shape[0]
    tb = TM // SUBLANES

    def body(o1, l1, o4, l4, o16, l16, gb_ref, gc_ref, xi_ref, gch_ref, xih_ref, x_ref, wsc_ref,
             ga_ref, gcv_ref, wout_ref, attn_ref, lse1, lse4, lse16, mixed_ref, x1_ref, scr_a, scr_b, scr_c, scr_d):
        i = pl.program_id(0)
        la, lb, lc = l1[...], _gather_classes(l4, scr_a, 4), _gather_classes(l16, scr_b, 16)
        m_all = jnp.maximum(jnp.maximum(la, lb), lc)
        ea, eb, ec = jnp.exp(la - m_all), jnp.exp(lb - m_all), jnp.exp(lc - m_all)
        den = (ea + eb) + ec
        num = (ea * o1[...] + eb * _gather_classes(o4, scr_c, 4)) + ec * _gather_classes(o16, scr_d, 16)
        attn = num / den
        attn_ref[...] = attn
        _spread(m_all + jnp.log(den), scr_a, (lse1, lse4, lse16), F32)
        xa, _ = _rms(attn)
        u = gc_ref[...] * xi_ref[...]
        uh = jnp.where(i > 0, gch_ref[...] * xih_ref[...], 0.0)
        conv = gb_ref[...] * _causal_conv3(u, uh, wsc_ref)
        xc, _ = _rms(conv)
        mixed = jnp.concatenate([xa * ga_ref[...], xc * gcv_ref[...]], axis=1).astype(BF16)
        mixed_ref[...] = mixed
        x1_ref[...] = x_ref[...] + _dot(mixed, wout_ref[...])

    row = lambda n: pl.BlockSpec((TM, n), lambda i: (i, 0))
    halo = pl.BlockSpec((SUBLANES, 512), lambda i: (jnp.maximum(i * tb - 1, 0), 0))
    cs = _class_specs(512)
    flat = [a for br in branches for a in br]
    res = pl.pallas_call(
        body, name="mix_out", grid=(s // TM,),
        out_shape=[jax.ShapeDtypeStruct((s, 512), F32)] + _class_shapes(s, 512, F32)
        + [jax.ShapeDtypeStruct((s, D_MODEL), BF16), jax.ShapeDtypeStruct((s, D_MODEL), F32)],
        in_specs=[cs[0], cs[0], cs[1], cs[1], cs[2], cs[2], row(512), row(512), row(512), halo, halo,
                  row(D_MODEL), _full(w_sc.shape), _full(g_a.shape), _full(g_c.shape), _full(w_out.shape)],
        out_specs=[row(512)] + cs + [row(D_MODEL), row(D_MODEL)],
        scratch_shapes=[pltpu.VMEM((512 // LANES, TM, LANES), F32)] * 4,
        compiler_params=_cparams(1),
    )(*flat, gb, gc, xi, gc, xi, x, w_sc, g_a, g_c, w_out)
    return res[0], res[1:4], res[4], res[5]


def _mem_kv(mem, g_mem, w_xk, w_xv):
    def body(mem_ref, g_ref, wk_ref, wv_ref, mn_ref, k_ref, v_ref):
        xh, _ = _rms(mem_ref[...])
        mn = (xh * g_ref[...]).astype(BF16)
        mn_ref[...] = mn
        k_ref[...] = _dot(mn, wk_ref[...]).astype(BF16)
        v_ref[...] = _dot(mn, wv_ref[...]).astype(BF16)

    vm = pl.BlockSpec(memory_space=pltpu.VMEM)
    return pl.pallas_call(
        body, name="mem_kv",
        out_shape=[jax.ShapeDtypeStruct(mem.shape, BF16)] * 3,
        in_specs=[vm] * 4, out_specs=[vm] * 3,
        compiler_params=pltpu.CompilerParams(vmem_limit_bytes=VMEM_LIMIT),
    )(mem, g_mem, w_xk, w_xv)


def _xattn_fwd(x1, g, w_xq, k, v, w_xo, dep):
    s = x1.shape[0]

    def body(x1_ref, g_ref, wq_ref, k_ref, v_ref, wo_ref, dep_ref, h2_ref, q_ref, o_ref, x2_ref):
        x1v = x1_ref[...]
        xh, _ = _rms(x1v)
        h2 = (xh * g_ref[...]).astype(BF16)
        h2_ref[...] = h2
        qb = _dot(h2, wq_ref[...]).astype(BF16)
        q_ref[...] = qb
        outs = []
        for h in range(N_MEM_HEADS):
            sl = slice(h * MEM_HEAD_DIM, (h + 1) * MEM_HEAD_DIM)
            lg = _dot_nt(qb[:, sl], k_ref[:, sl]) * (MEM_HEAD_DIM ** -0.5)
            p = jnp.exp(lg - jnp.max(lg, axis=-1, keepdims=True))
            p = p / jnp.sum(p, axis=-1, keepdims=True)
            outs.append(_dot(p.astype(BF16), v_ref[:, sl]))
        o = jnp.concatenate(outs, axis=1).astype(BF16)
        o_ref[...] = o
        x2_ref[...] = x1v + _dot(o, wo_ref[...])

    row = pl.BlockSpec((TM, D_MODEL), lambda i: (i, 0))
    return pl.pallas_call(
        body, name="xattn_fwd", grid=(s // TM,),
        out_shape=[jax.ShapeDtypeStruct((s, D_MODEL), BF16)] * 3 + [jax.ShapeDtypeStruct((s, D_MODEL), F32)],
        in_specs=[row, _full(g.shape), _full(w_xq.shape), _full(k.shape), _full(v.shape), _full(w_xo.shape), ANY_SPEC],
        out_specs=[row] * 4,
        compiler_params=_cparams(1),
    )(x1, g, w_xq, k, v, w_xo, dep)


def _ffn_conv(h_ext, wup_ref, wfc_ref, bfc_ref, j):
    u = _dot_nt(h_ext, wup_ref[j])
    w = wfc_ref[j]
    c = ((pltpu.roll(u, 2, 0) * w[0:1, :] + pltpu.roll(u, 1, 0) * w[1:2, :]) + u * w[2:3, :]) + bfc_ref[j]
    return c[HALO:], u[HALO:]


def _ffn_fwd(x2, g, w_up_g, w_fc, b_fc, w_down_g, g_final, target):
    s = x2.shape[0]
    tb = TM_FFN // HALO
    n_ch, wid = w_up_g.shape[:2]
    half = n_ch // 2

    def body(x_ref, xp_ref, g_ref, wup_ref, wfc_ref, bfc_ref, wd_ref, gf_ref, t_ref, h_ref, u_ref, c_ref, act_ref,
             dx3_ref, loss_ref, dgf_ref):
        i = pl.program_id(0)

        @pl.when(i == 0)
        def _():
            loss_ref[...] = jnp.zeros_like(loss_ref)
            dgf_ref[...] = jnp.zeros_like(dgf_ref)

        x2v = x_ref[...]
        gv = g_ref[...]
        h = (_rms(x2v)[0] * gv).astype(BF16)
        h_ref[...] = h
        hp = jnp.where(i > 0, _rms(xp_ref[...])[0] * gv, 0.0).astype(BF16)
        h_ext = jnp.concatenate([hp, h], axis=0)
        down = jnp.zeros((TM_FFN, D_MODEL), F32)
        for j in range(half):
            cg, ug = _ffn_conv(h_ext, wup_ref, wfc_ref, bfc_ref, j)
            cv, uv = _ffn_conv(h_ext, wup_ref, wfc_ref, bfc_ref, j + half)
            c_ref[j] = cg
            c_ref[j + half] = cv
            u_ref[j] = ug.astype(BF16)
            u_ref[j + half] = uv.astype(BF16)
            a = ((cg * _sigmoid(cg)) * cv).astype(BF16)
            act_ref[j] = a
            down = down + _dot(a, wd_ref[j])
        x3 = x2v + down
        xh, r = _rms(x3)
        gf = gf_ref[...]
        e = xh * gf - t_ref[...]
        loss_ref[...] += 0.5 * jnp.sum(jnp.sum(e * e, axis=1, keepdims=True), axis=0, keepdims=True) / D_MODEL
        dy = e * (1.0 / D_MODEL)
        dgf_ref[0:1, :] += jnp.sum(dy * xh, axis=0, keepdims=True)
        dx3_ref[...] = _rms_bwd(xh, r, gf, dy)

    row = pl.BlockSpec((TM_FFN, D_MODEL), lambda i: (i, 0))
    prev = pl.BlockSpec((HALO, D_MODEL), lambda i: (jnp.maximum(i * tb - 1, 0), 0))
    return pl.pallas_call(
        body, name="ffn_fwd", grid=(s // TM_FFN,),
        out_shape=[jax.ShapeDtypeStruct((s, D_MODEL), BF16), jax.ShapeDtypeStruct((n_ch, s, wid), BF16),
                   jax.ShapeDtypeStruct((n_ch, s, wid), F32), jax.ShapeDtypeStruct((half, s, wid), BF16),
                   jax.ShapeDtypeStruct((s, D_MODEL), F32), jax.ShapeDtypeStruct((SUBLANES, 128), F32),
                   jax.ShapeDtypeStruct((SUBLANES, D_MODEL), F32)],
        in_specs=[row, prev, _full(g.shape), _resident(w_up_g.shape), _full(w_fc.shape), _full(b_fc.shape),
                  _resident(w_down_g.shape), _full(g_final.shape), row],
        out_specs=[row, pl.BlockSpec((n_ch, TM_FFN, wid), lambda i: (0, i, 0)),
                   pl.BlockSpec((n_ch, TM_FFN, wid), lambda i: (0, i, 0)),
                   pl.BlockSpec((half, TM_FFN, wid), lambda i: (0, i, 0)), row,
                   _full((SUBLANES, 128)), _full((SUBLANES, D_MODEL))],
        compiler_params=_cparams(1),
    )(x2, x2, g, w_up_g, w_fc, b_fc, w_down_g, g_final, target)


def _ffn_bwd(dx3, up, conv, x2, g, w_up_g, w_fc, w_down_g):
    s = x2.shape[0]
    tb = TM_FFN // HALO
    last = s // HALO - 1
    n_tiles = s // TM_FFN
    n_ch, wid = w_up_g.shape[:2]
    half = n_ch // 2
    n_ext = TM_FFN + HALO

    def body(dx_ref, dxn_ref, u_ref, c_ref, cn_ref, x2_ref, g_ref, wup_ref, wfc_ref, wd_ref,
             dup_ref, dx2_ref, dg_ref, dwfc_ref, dbfc_ref):
        i = pl.program_id(0)

        @pl.when(i == 0)
        def _():
            dg_ref[...] = jnp.zeros_like(dg_ref)
            dwfc_ref[...] = jnp.zeros_like(dwfc_ref)
            dbfc_ref[...] = jnp.zeros_like(dbfc_ref)

        dxv = dx_ref[...]
        dxn = jnp.where(i < n_tiles - 1, dxn_ref[...], 0.0)
        dx_ext = jnp.concatenate([dxv, dxn], axis=0).astype(BF16)
        dh = jnp.zeros((TM_FFN, D_MODEL), F32)
        for j in range(half):
            cg = jnp.concatenate([c_ref[j], cn_ref[j]], axis=0)
            cv = jnp.concatenate([c_ref[j + half], cn_ref[j + half]], axis=0)
            dact = _dot_nt(dx_ext, wd_ref[j])
            sg = _sigmoid(cg)
            silu = cg * sg
            parts = ((j + half, dact * silu), (j, (dact * cv) * (sg + silu * (1.0 - sg))))
            for jj, dc in parts:
                u = u_ref[jj].astype(F32)
                dc0, dc1, dc2 = dc[:TM_FFN], pltpu.roll(dc, n_ext - 1, 0)[:TM_FFN], pltpu.roll(dc, n_ext - 2, 0)[:TM_FFN]
                dbfc_ref[jj:jj + 1, :] += jnp.sum(dc0, axis=0, keepdims=True)
                dwfc_ref[0, jj:jj + 1, :] += jnp.sum(dc2 * u, axis=0, keepdims=True)
                dwfc_ref[1, jj:jj + 1, :] += jnp.sum(dc1 * u, axis=0, keepdims=True)
                dwfc_ref[2, jj:jj + 1, :] += jnp.sum(dc0 * u, axis=0, keepdims=True)
                w = wfc_ref[jj]
                du = ((dc0 * w[2:3, :] + dc1 * w[1:2, :]) + dc2 * w[0:1, :]).astype(BF16)
                dup_ref[jj] = du
                dh = dh + _dot(du, wup_ref[jj])
        xh, r = _rms(x2_ref[...])
        dg_ref[0:1, :] += jnp.sum(dh * xh, axis=0, keepdims=True)
        dx2_ref[...] = dxv + _rms_bwd(xh, r, g_ref[...], dh)

    row = pl.BlockSpec((TM_FFN, D_MODEL), lambda i: (i, 0))
    nxt = pl.BlockSpec((HALO, D_MODEL), lambda i: (jnp.minimum((i + 1) * tb, last), 0))
    cur_c = pl.BlockSpec((n_ch, TM_FFN, wid), lambda i: (0, i, 0))
    nxt_c = pl.BlockSpec((n_ch, HALO, wid), lambda i: (0, jnp.minimum((i + 1) * tb, last), 0))
    return pl.pallas_call(
        body, name="ffn_bwd", grid=(n_tiles,),
        out_shape=[jax.ShapeDtypeStruct((n_ch, s, wid), BF16), jax.ShapeDtypeStruct((s, D_MODEL), F32),
                   jax.ShapeDtypeStruct((SUBLANES, D_MODEL), F32), jax.ShapeDtypeStruct((3, n_ch, wid), F32),
                   jax.ShapeDtypeStruct((n_ch, wid), F32)],
        in_specs=[row, nxt, cur_c, cur_c, nxt_c, row, _full(g.shape), _resident(w_up_g.shape), _full(w_fc.shape),
                  _resident(w_down_g.shape)],
        out_specs=[cur_c, row, _full((SUBLANES, D_MODEL)), _full((3, n_ch, wid)), _full((n_ch, wid))],
        compiler_params=_cparams(1),
    )(dx3, dx3, up, conv, conv, x2, g, w_up_g, w_fc, w_down_g)


def _xattn_bwd(dx2, o, q, k, v, w_xo, w_xq, x1, g, dep):
    s = x1.shape[0]

    def body(dx2_ref, o_ref, q_ref, k_ref, v_ref, wo_ref, wq_ref, x1_ref, g_ref, dep_ref, dq_ref, dx1_ref, dk_ref,
             dv_ref, dg_ref):
        @pl.when(pl.program_id(0) == 0)
        def _():
            dk_ref[...] = jnp.zeros_like(dk_ref)
            dv_ref[...] = jnp.zeros_like(dv_ref)
            dg_ref[...] = jnp.zeros_like(dg_ref)

        dx2v = dx2_ref[...]
        do = _dot_nt(dx2v.astype(BF16), wo_ref[...])
        dqs = []
        for h in range(N_MEM_HEADS):
            sl = slice(h * MEM_HEAD_DIM, (h + 1) * MEM_HEAD_DIM)
            qh, kh, vh = q_ref[:, sl], k_ref[:, sl], v_ref[:, sl]
            lg = _dot_nt(qh, kh) * (MEM_HEAD_DIM ** -0.5)
            p = jnp.exp(lg - jnp.max(lg, axis=-1, keepdims=True))
            p = p / jnp.sum(p, axis=-1, keepdims=True)
            doh = do[:, sl].astype(BF16)
            dp = _dot_nt(doh, vh)
            ds = (p * (dp - jnp.sum(p * dp, axis=-1, keepdims=True)) * (MEM_HEAD_DIM ** -0.5)).astype(BF16)
            dqs.append(_dot(ds, kh))
            dk_ref[:, sl] += _dot_tn(ds, qh)
            dv_ref[:, sl] += _dot_tn(p.astype(BF16), doh)
        dq = jnp.concatenate(dqs, axis=1).astype(BF16)
        dq_ref[...] = dq
        dh2 = _dot_nt(dq, wq_ref[...])
        xh, r = _rms(x1_ref[...])
        dg_ref[0:1, :] += jnp.sum(dh2 * xh, axis=0, keepdims=True)
        dx1_ref[...] = dx2v + _rms_bwd(xh, r, g_ref[...], dh2)

    row = pl.BlockSpec((TM, D_MODEL), lambda i: (i, 0))
    return pl.pallas_call(
        body, name="xattn_bwd", grid=(s // TM,),
        out_shape=[jax.ShapeDtypeStruct((s, D_MODEL), BF16), jax.ShapeDtypeStruct((s, D_MODEL), F32),
                   jax.ShapeDtypeStruct(k.shape, F32), jax.ShapeDtypeStruct(k.shape, F32),
                   jax.ShapeDtypeStruct((SUBLANES, D_MODEL), F32)],
        in_specs=[row, row, row, _full(k.shape), _full(v.shape), _full(w_xo.shape), _full(w_xq.shape), row,
                  _full(g.shape), ANY_SPEC],
        out_specs=[row, row, _full(k.shape), _full(k.shape), _full((SUBLANES, D_MODEL))],
        compiler_params=_cparams(1),
    )(dx2, o, q, k, v, w_xo, w_xq, x1, g, dep)


def _mem_kv_bwd(dk, dv, mem_n, mem, w_xk, w_xv):
    def body(dk_ref, dv_ref, mn_ref, mem_ref, wk_ref, wv_ref, dwk_ref, dwv_ref, dg_ref):
        dkb, dvb = dk_ref[...].astype(BF16), dv_ref[...].astype(BF16)
        mn = mn_ref[...]
        dwk_ref[...] = _dot_tn(mn, dkb).astype(BF16)
        dwv_ref[...] = _dot_tn(mn, dvb).astype(BF16)
        dmn = _dot_nt(dkb, wk_ref[...]) + _dot_nt(dvb, wv_ref[...])
        xh, _ = _rms(mem_ref[...])
        dg_ref[...] = jnp.zeros_like(dg_ref)
        dg_ref[0:1, :] = jnp.sum(dmn * xh, axis=0, keepdims=True)

    vm = pl.BlockSpec(memory_space=pltpu.VMEM)
    return pl.pallas_call(
        body, name="mem_kv_bwd",
        out_shape=[jax.ShapeDtypeStruct(w_xk.shape, BF16), jax.ShapeDtypeStruct(w_xv.shape, BF16),
                   jax.ShapeDtypeStruct((SUBLANES, D_MODEL), F32)],
        in_specs=[vm] * 6, out_specs=[vm] * 3,
        compiler_params=pltpu.CompilerParams(vmem_limit_bytes=VMEM_LIMIT),
    )(dk, dv, mem_n, mem, w_xk, w_xv)


def _mix_out_bwd(dx1, w_out, attn, gb, gc, xi, w_sc, g_a, g_c, dep):
    s = dx1.shape[0]
    tb = TM // SUBLANES

    def body(dx1_ref, wout_ref, attn_ref, gb_ref, gc_ref, xi_ref, gch_ref, xih_ref, wsc_ref, ga_ref, gcv_ref, dep_ref,
             da1, da4, da16, dd1, dd4, dd16, dgb_ref, dcv_ref, dga_ref, dgc_ref, dwsc_ref, scr):
        i = pl.program_id(0)

        @pl.when(i == 0)
        def _():
            dga_ref[...] = jnp.zeros_like(dga_ref)
            dgc_ref[...] = jnp.zeros_like(dgc_ref)
            dwsc_ref[...] = jnp.zeros_like(dwsc_ref)

        dmixed = _dot_nt(dx1_ref[...].astype(BF16), wout_ref[...])
        da, dcn = dmixed[:, :ATTN_W], dmixed[:, ATTN_W:]
        attn = attn_ref[...]
        xa, ra = _rms(attn)
        dga_ref[0:1, :] += jnp.sum(da * xa, axis=0, keepdims=True)
        dattn = _rms_bwd(xa, ra, ga_ref[...], da)
        _spread(dattn, scr, (da1, da4, da16), BF16)
        prod = dattn * attn
        dd = jnp.concatenate(
            [jnp.broadcast_to(jnp.sum(prod[:, h * HEAD_DIM:(h + 1) * HEAD_DIM], axis=-1, keepdims=True),
                              (TM, HEAD_DIM)) for h in range(N_HEADS)], axis=1)
        _spread(dd, scr, (dd1, dd4, dd16), F32)
        gbv = gb_ref[...]
        u = gc_ref[...] * xi_ref[...]
        uh = jnp.where(i > 0, gch_ref[...] * xih_ref[...], 0.0)
        u2, u1 = _shift_down(u, uh, 2), _shift_down(u, uh, 1)
        cv = (u2 * wsc_ref[0:1, :] + u1 * wsc_ref[1:2, :]) + u * wsc_ref[2:3, :]
        xc, rc = _rms(gbv * cv)
        dgc_ref[0:1, :] += jnp.sum(dcn * xc, axis=0, keepdims=True)
        dconv = _rms_bwd(xc, rc, gcv_ref[...], dcn)
        dgb_ref[...] = (dconv * cv).astype(BF16)
        dcv = dconv * gbv
        dcv_ref[...] = dcv
        dwsc_ref[0:1, :] += jnp.sum(dcv * u2, axis=0, keepdims=True)
        dwsc_ref[1:2, :] += jnp.sum(dcv * u1, axis=0, keepdims=True)
        dwsc_ref[2:3, :] += jnp.sum(dcv * u, axis=0, keepdims=True)

    row = lambda n: pl.BlockSpec((TM, n), lambda i: (i, 0))
    halo = pl.BlockSpec((SUBLANES, 512), lambda i: (jnp.maximum(i * tb - 1, 0), 0))
    acc = _full((SUBLANES, 512))
    res = pl.pallas_call(
        body, name="mix_out_bwd", grid=(s // TM,),
        out_shape=_class_shapes(s, 512, BF16) + _class_shapes(s, 512, F32)
        + [jax.ShapeDtypeStruct((s, 512), BF16), jax.ShapeDtypeStruct((s, 512), F32)]
        + [jax.ShapeDtypeStruct((SUBLANES, 512), F32)] * 3,
        in_specs=[row(D_MODEL), _full(w_out.shape), row(512), row(512), row(512), row(512), halo, halo,
                  _full(w_sc.shape), _full(g_a.shape), _full(g_c.shape), ANY_SPEC],
        out_specs=_class_specs(512) * 2 + [row(512)] * 2 + [acc] * 3,
        scratch_shapes=[pltpu.VMEM((512 // LANES, TM, LANES), F32)],
        compiler_params=_cparams(1),
    )(dx1, w_out, attn, gb, gc, xi, gc, xi, w_sc, g_a, g_c, dep)
    return res[0:3], res[3:6], res[6], res[7], res[8], res[9], res[10]


def _swa_bwd(qc, kc, vc, doc, lsec, ddc, bias, dil, dep):
    nsub, nb, ncls = _swa_steps(qc, dil)
    n128 = nsub * nb
    whole = nb == 1

    def body(q_ref, qn_ref, kp_ref, kc_ref, vp_ref, vc_ref, do_ref, don_ref, lse_ref, lsen_ref, dd_ref, ddn_ref,
             b_ref, dep_ref, dq_ref, dk_ref, dv_ref, db_ref, s_scr, dp_scr, sn_scr, dpn_scr, ds_scr, p_scr, dsn_scr,
             pn_scr):
        r, b = pl.program_id(0), pl.program_id(1)

        @pl.when((r == 0) & (b == 0))
        def _():
            db_ref[...] = jnp.zeros_like(db_ref)

        pairs = [slice(a * LANES, (a + 1) * LANES) for a in range(N_HEADS // 2)]
        blk = [slice(t * WIN, (t + 1) * WIN) for t in range(nsub)]
        last = blk[nsub - 1]
        cols = lambda t: slice(WIN, 2 * WIN) if whole and t == 0 else slice(0, 2 * WIN)
        of_head = lambda ref, c, rows, h: ref[c, rows, h * HEAD_DIM:h * HEAD_DIM + 1]
        no_prev = (b == 0) & (lax.broadcasted_iota(jnp.int32, (WIN, 2 * WIN), 1) < WIN)

        def keys(prev_ref, cur_ref, c, t, sl):
            if whole and t == 0:
                return cur_ref[c, blk[0], sl]
            if t == 0:
                return jnp.concatenate([prev_ref[c, :, sl], cur_ref[c, blk[0], sl]], axis=0)
            return cur_ref[c, (t - 1) * WIN:(t + 1) * WIN, sl]

        for a, sl in enumerate(pairs):
            for c, t in [(c, t) for c in range(ncls) for t in range(nsub)]:
                k2, v2 = keys(kp_ref, kc_ref, c, t, sl), keys(vp_ref, vc_ref, c, t, sl)
                q_eo = _pair_split(q_ref[c, blk[t], sl])
                do_eo = _pair_split(do_ref[c, blk[t], sl].astype(BF16))
                for e in range(2):
                    s_scr[c * nsub + t, 2 * a + e, :, cols(t)] = _dot_nt(q_eo[e], k2)
                    dp_scr[c * nsub + t, 2 * a + e, :, cols(t)] = _dot_nt(do_eo[e], v2)
            if not whole:
                qn_eo = _pair_split(qn_ref[0, :, sl])
                don_eo = _pair_split(don_ref[0, :, sl].astype(BF16))
                for e in range(2):
                    sn_scr[2 * a + e] = _dot_nt(qn_eo[e], kc_ref[0, last, sl])
                    dpn_scr[2 * a + e] = _dot_nt(don_eo[e], vc_ref[0, last, sl])
        for c, t, h in [(c, t, h) for c in range(ncls) for t in range(nsub) for h in range(N_HEADS)]:
            i, cl = c * nsub + t, cols(t)
            lg = s_scr[i, h, :, cl] + b_ref[h, :, cl]
            if t == 0 and not whole:
                lg = jnp.where(no_prev, -jnp.inf, lg)
            p = jnp.exp(lg - of_head(lse_ref, c, blk[t], h))
            ds = p * (dp_scr[i, h, :, cl] - of_head(dd_ref, c, blk[t], h))
            db_ref[h, :, cl] += ds
            ds_scr[i, h, :, cl] = ds.astype(BF16)
            p_scr[i, h, :, cl] = p.astype(BF16)
        if not whole:
            every = slice(0, WIN)
            for h in range(N_HEADS):
                lgn = jnp.where(b + 1 < nb, sn_scr[h] + b_ref[h, :, :WIN], -jnp.inf)
                pn = jnp.exp(lgn - of_head(lsen_ref, 0, every, h))
                dsn_scr[h] = (pn * (dpn_scr[h] - of_head(ddn_ref, 0, every, h))).astype(BF16)
                pn_scr[h] = pn.astype(BF16)
        for a, sl in enumerate(pairs):
            for c in range(ncls):
                q_eo = [_pair_split(q_ref[c, blk[t], sl]) for t in range(nsub)]
                do_eo = [_pair_split(do_ref[c, blk[t], sl].astype(BF16)) for t in range(nsub)]
                if not whole:
                    q_eo.append(_pair_split(qn_ref[0, :, sl]))
                    do_eo.append(_pair_split(don_ref[0, :, sl].astype(BF16)))
                for t in range(nsub):
                    i = c * nsub + t
                    k_eo = _pair_split(keys(kp_ref, kc_ref, c, t, sl))
                    dq, dk, dv = None, None, None
                    for e in range(2):
                        h = 2 * a + e
                        terms = [_dot(ds_scr[i, h, :, cols(t)], k_eo[e]),
                                 _dot_tn(ds_scr[i, h, :, WIN:], q_eo[t][e]),
                                 _dot_tn(p_scr[i, h, :, WIN:], do_eo[t][e])]
                        if t + 1 < nsub or not whole:
                            ds_next = ds_scr[i + 1, h, :, :WIN] if t + 1 < nsub else dsn_scr[h]
                            p_next = p_scr[i + 1, h, :, :WIN] if t + 1 < nsub else pn_scr[h]
                            terms[1] += _dot_tn(ds_next, q_eo[t + 1][e])
                            terms[2] += _dot_tn(p_next, do_eo[t + 1][e])
                        dq, dk, dv = terms if e == 0 else (dq + terms[0], dk + terms[1], dv + terms[2])
                    dq_ref[c, blk[t], sl] = dq.astype(BF16)
                    dk_ref[c, blk[t], sl] = dk.astype(BF16)
                    dv_ref[c, blk[t], sl] = dv.astype(BF16)

    cur = pl.BlockSpec((ncls, nsub * WIN, 512), lambda r, b: (r, b, 0))
    prev = pl.BlockSpec((ncls, WIN, 512), lambda r, b: (r, jnp.maximum(nsub * b - 1, 0), 0))
    nxt = pl.BlockSpec((ncls, WIN, 512), lambda r, b: (r, jnp.minimum(nsub * b + nsub, n128 - 1), 0))
    wide, narrow = (ncls * nsub, N_HEADS, WIN, 2 * WIN), (N_HEADS, WIN, WIN)
    return pl.pallas_call(
        body, name=f"swa_bwd_d{dil}", grid=(dil // ncls, nb),
        out_shape=[jax.ShapeDtypeStruct(qc.shape, BF16)] * 3 + [jax.ShapeDtypeStruct(bias.shape, F32)],
        in_specs=[cur, nxt, prev, cur, prev, cur, cur, nxt, cur, nxt, cur, nxt, _full(bias.shape), ANY_SPEC],
        out_specs=[cur] * 3 + [_full(bias.shape)],
        scratch_shapes=[pltpu.VMEM(wide, F32), pltpu.VMEM(wide, F32), pltpu.VMEM(narrow, F32),
                        pltpu.VMEM(narrow, F32), pltpu.VMEM(wide, BF16), pltpu.VMEM(wide, BF16),
                        pltpu.VMEM(narrow, BF16), pltpu.VMEM(narrow, BF16)],
        compiler_params=_cparams(2),
    )(qc, qc, kc, kc, vc, vc, doc, doc, lsec, lsec, ddc, ddc, bias, dep)


def _in_proj_bwd(dqs, dks, dvs, dgb, dcv, gc, xi, w_sc, w_in_g, x, g_mix, dx1):
    s = x.shape[0]
    tb = TM // SUBLANES
    last = s // SUBLANES - 1
    n_tiles = s // TM

    def body(dq1, dq4, dq16, dk1, dk4, dk16, dv1, dv4, dv16, dgb_ref, dcv_ref, dcvn_ref, gc_ref, xi_ref, wsc_ref,
             win_ref, x_ref, g_ref, dx1_ref, dproj_ref, gx_ref, dg_ref, scr_a, scr_b):
        i = pl.program_id(0)

        @pl.when(i == 0)
        def _():
            dg_ref[...] = jnp.zeros_like(dg_ref)

        d0 = dcv_ref[...]
        dn = jnp.where(i < n_tiles - 1, dcvn_ref[...], 0.0)
        du = (d0 * wsc_ref[2:3, :] + _shift_up(d0, dn, 1) * wsc_ref[1:2, :]) + _shift_up(d0, dn, 2) * wsc_ref[0:1, :]
        merge = lambda a, b4, b16: ((a[...].astype(F32) + _gather_classes(b4, scr_a, 4))
                                    + _gather_classes(b16, scr_b, 16))
        dq = merge(dq1, dq4, dq16) * (HEAD_DIM ** -0.5)
        dk = merge(dk1, dk4, dk16)
        dv = merge(dv1, dv4, dv16)
        dproj = jnp.concatenate([dq, dk, dv, dgb_ref[...].astype(F32), du * xi_ref[...], du * gc_ref[...]],
                                axis=1).astype(BF16)
        dproj_ref[...] = dproj
        dh = jnp.zeros((TM, D_MODEL), F32)
        for j in range(N_DEV):
            dh = dh + _dot_nt(dproj[:, j * IN_CHUNK:(j + 1) * IN_CHUNK], win_ref[j])
        xh, r = _rms(x_ref[...])
        dg_ref[0:1, :] += jnp.sum(dh * xh, axis=0, keepdims=True)
        gx_ref[...] = dx1_ref[...] + _rms_bwd(xh, r, g_ref[...], dh)

    row = lambda n: pl.BlockSpec((TM, n), lambda i: (i, 0))
    nxt = pl.BlockSpec((SUBLANES, 512), lambda i: (jnp.minimum((i + 1) * tb, last), 0))
    return pl.pallas_call(
        body, name="in_proj_bwd", grid=(n_tiles,),
        out_shape=[jax.ShapeDtypeStruct((s, IN_COLS), BF16), jax.ShapeDtypeStruct((s, D_MODEL), F32),
                   jax.ShapeDtypeStruct((SUBLANES, D_MODEL), F32)],
        in_specs=_class_specs(512) * 3 + [row(512), row(512), nxt, row(512), row(512), _full(w_sc.shape),
                                          _full(w_in_g.shape), row(D_MODEL), _full(g_mix.shape), row(D_MODEL)],
        out_specs=[row(IN_COLS), row(D_MODEL), _full((SUBLANES, D_MODEL))],
        scratch_shapes=[pltpu.VMEM((512 // LANES, TM, LANES), F32)] * 2,
        compiler_params=_cparams(1),
    )(*dqs, *dks, *dvs, dgb, dcv, dcv, gc, xi, w_sc, w_in_g, x, g_mix, dx1)


def _dw(a, b, dep, name, a_chunked=False, b_chunked=False, n_chunks=1, chunk_cols=None):
    single = not (a_chunked or b_chunked or chunk_cols)
    wide = a_chunked and a.shape[2] > D_MODEL
    ts = TS_DW // 2 if single or wide else TS_DW
    if a_chunked:
        nj, s, kk = a.shape
        nn = b.shape[1]
        a_spec = pl.BlockSpec((1, ts, kk), lambda j, t: (j, t, 0))
        b_spec = pl.BlockSpec((ts, nn), lambda j, t: (t, 0))
    elif b_chunked:
        nj, s, nn = b.shape
        kk = a.shape[1]
        a_spec = pl.BlockSpec((ts, kk), lambda j, t: (t, 0))
        b_spec = pl.BlockSpec((1, ts, nn), lambda j, t: (j, t, 0))
    else:
        s, kk = a.shape
        nj, nn = (n_chunks, chunk_cols) if chunk_cols else (1, b.shape[1])
        a_spec = pl.BlockSpec((ts, kk), lambda j, t: (t, 0))
        b_spec = pl.BlockSpec((ts, nn), lambda j, t: (t, j))
    n_steps = s // ts

    def body(a_ref, b_ref, dep_ref, o_ref, acc):
        t = pl.program_id(1)

        @pl.when(t == 0)
        def _():
            acc[...] = jnp.zeros_like(acc)

        av = (a_ref[0] if a_chunked else a_ref[...]).astype(BF16)
        bv = (b_ref[0] if b_chunked else b_ref[...]).astype(BF16)
        acc[...] += _dot_tn(av, bv)

        @pl.when(t == n_steps - 1)
        def _():
            o_ref[0] = acc[...].astype(BF16)

    return pl.pallas_call(
        body, name=name, grid=(nj, n_steps),
        out_shape=jax.ShapeDtypeStruct((nj, kk, nn), BF16),
        in_specs=[a_spec, b_spec, ANY_SPEC],
        out_specs=pl.BlockSpec((1, kk, nn), lambda j, t: (j, 0, 0)),
        scratch_shapes=[pltpu.VMEM((kk, nn), F32)],
        compiler_params=_cparams(2),
    )(a, b, dep)


def _adamw_math(w, g, m, v):
    m2 = ADAM_B1 * m + (1.0 - ADAM_B1) * g
    v2 = ADAM_B2 * v + (1.0 - ADAM_B2) * (g * g)
    m_hat = m2 / (1.0 - ADAM_B1 ** ADAM_STEP)
    v_hat = v2 / (1.0 - ADAM_B2 ** ADAM_STEP)
    delta = -ADAM_LR * (m_hat / (jnp.sqrt(v_hat) + ADAM_EPS) + ADAM_WD * w)
    return delta, m2, v2


def _sum_parts(me, own, p_ref):
    g = None
    for i in range(N_DEV):
        part = jnp.where(me == i, own.astype(F32), p_ref[i].astype(F32))
        g = part if g is None else g + part
    return g


def _adamw_big(name, w, sent, parts, m, v, me_arr):
    rr, cc = w.shape
    tr = rr // 4 if rr >= 512 else rr

    def body(me_ref, w_ref, own_ref, p_ref, m_ref, v_ref, g_ref, d_ref, nm_ref, nv_ref):
        g = own_ref[0].astype(F32)
        for k in range(1, N_DEV):
            g = g + p_ref[(me_ref[0] + k) % N_DEV].astype(F32)
        g_ref[...] = g
        d_ref[...], nm_ref[...], nv_ref[...] = _adamw_math(w_ref[...], g, m_ref[...], v_ref[...])

    row = pl.BlockSpec((tr, cc), lambda i, me: (i, 0))
    return pl.pallas_call(
        body, name=name,
        grid_spec=pltpu.PrefetchScalarGridSpec(
            num_scalar_prefetch=1, grid=(rr // tr,),
            in_specs=[row, pl.BlockSpec((1, tr, cc), lambda i, me: (me[0], i, 0)),
                      pl.BlockSpec((N_DEV, tr, cc), lambda i, me: (0, i, 0)), row, row],
            out_specs=[row] * 4),
        out_shape=[jax.ShapeDtypeStruct((rr, cc), F32)] * 4,
        compiler_params=_cparams(1),
    )(me_arr, w, sent, parts, m, v)


def _small_slices():
    return [
        (slice(ROW_RELB, ROW_RELB + 8), slice(0, N_BUCKETS)),
        (slice(ROW_GMIX, ROW_GMIX + 1), slice(0, D_MODEL)),
        (slice(ROW_GAC, ROW_GAC + 1), slice(0, ATTN_W)),
        (slice(ROW_GAC, ROW_GAC + 1), slice(ATTN_W, D_MODEL)),
        (slice(ROW_GXATTN, ROW_GXATTN + 1), slice(0, D_MODEL)),
        (slice(ROW_GMEM, ROW_GMEM + 1), slice(0, D_MODEL)),
        (slice(ROW_GFFN, ROW_GFFN + 1), slice(0, D_MODEL)),
        (slice(ROW_BFC, ROW_BFC + 8), slice(0, UP_CHUNK)),
        (slice(ROW_GFINAL, ROW_GFINAL + 1), slice(0, D_MODEL)),
    ]


def _adamw_small(own, parts, wmv, me_arr):
    slices = _small_slices()
    n = len(slices)

    def body(*refs):
        me_ref, own_ref, p_ref = refs[:3]
        ins = refs[3:3 + 3 * n]
        g_ref = refs[3 + 3 * n]
        outs = refs[4 + 3 * n:]
        g = _sum_parts(me_ref[0], own_ref[...], p_ref)
        g_ref[...] = g
        for a, (rs, ls) in enumerate(slices):
            ga = g[rs, ls]
            outs[4 * a][...] = ga
            outs[4 * a + 1][...], outs[4 * a + 2][...], outs[4 * a + 3][...] = _adamw_math(
                ins[3 * a][...], ga, ins[3 * a + 1][...], ins[3 * a + 2][...])

    vm = pl.BlockSpec(memory_space=pltpu.VMEM)
    flat = [t for trip in wmv for t in trip]
    out_shape = [jax.ShapeDtypeStruct((SMALL_ROWS, D_MODEL), F32)]
    for w, _, _ in wmv:
        out_shape += [jax.ShapeDtypeStruct(w.shape, F32)] * 4
    res = pl.pallas_call(
        body, name="adamw_small", out_shape=out_shape,
        in_specs=[SMEM_SPEC] + [vm] * (2 + 3 * n), out_specs=[vm] * len(out_shape),
    )(me_arr, own, parts, *flat)
    return res[0], [res[1 + 4 * a:5 + 4 * a] for a in range(n)]


def _adamw_shards(items):
    n = len(items)

    def body(*refs):
        for a in range(n):
            w_ref, g_ref, m_ref, v_ref = refs[4 * a:4 * a + 4]
            d_ref, nm_ref, nv_ref = refs[4 * n + 3 * a:4 * n + 3 * a + 3]
            d_ref[...], nm_ref[...], nv_ref[...] = _adamw_math(w_ref[...], g_ref[...], m_ref[...], v_ref[...])

    vm = pl.BlockSpec(memory_space=pltpu.VMEM)
    out_shape = []
    for w, _, _, _ in items:
        out_shape += [jax.ShapeDtypeStruct(w.shape, F32)] * 3
    res = pl.pallas_call(
        body, name="adamw_shards", out_shape=out_shape, in_specs=[vm] * (4 * n), out_specs=[vm] * (3 * n),
    )(*[t for it in items for t in it])
    return [res[3 * a:3 * a + 3] for a in range(n)]


def _mesh_pos():
    return lax.axis_index("x"), lax.axis_index("y"), lax.axis_index("c")


def _dev_index(p):
    return 4 * p[0] + 2 * p[1] + p[2]


def _all_gather(shards):
    n = len(shards)

    def body(*refs):
        ins, outs = refs[:n], refs[n:2 * n]
        send_sems, recv_sems, loc_sems = refs[2 * n:]
        x, y, c = _mesh_pos()
        me, sib = (x, y, c), (x, y, 1 - c)
        chips = [(1 - x, y), (x, 1 - y), (1 - x, 1 - y)]

        def cp(a, k, block, to, src=None):
            dst = outs[a].at[_dev_index(block)]
            return pltpu.make_async_remote_copy(
                src_ref=dst if src is None else src, dst_ref=dst, send_sem=send_sems.at[a, k],
                recv_sem=recv_sems.at[a, k], device_id=to, device_id_type=MESH)

        mine = [pltpu.make_async_copy(ins[a], outs[a].at[_dev_index(me)], loc_sems.at[a]) for a in range(n)]
        for m_ in mine:
            m_.start()
        first = []
        for a in range(n):
            first.append(cp(a, 0, me, sib, src=ins[a]))
            first += [cp(a, 1 + j, me, (*chip, c), src=ins[a]) for j, chip in enumerate(chips)]
        for f in first:
            f.start()
        passed = []
        for a in range(n):
            for j, chip in enumerate(chips):
                cp(a, 1 + j, (*chip, c), me).wait_recv()
                fwd = cp(a, 4 + j, (*chip, c), sib)
                fwd.start()
                passed.append(fwd)
        for a in range(n):
            cp(a, 0, sib, me).wait_recv()
            for j, chip in enumerate(chips):
                cp(a, 4 + j, (*chip, 1 - c), me).wait_recv()
        for f in first + passed:
            f.wait_send()
        for m_ in mine:
            m_.wait()

    hbm = pl.BlockSpec(memory_space=pltpu.HBM)
    return pl.pallas_call(
        body, name="all_gather_weights",
        out_shape=[jax.ShapeDtypeStruct((N_DEV,) + a.shape, a.dtype) for a in shards],
        in_specs=[hbm] * n, out_specs=[hbm] * n,
        scratch_shapes=[pltpu.SemaphoreType.DMA((n, 7)), pltpu.SemaphoreType.DMA((n, 7)),
                        pltpu.SemaphoreType.DMA((n,))],
    )(*shards)


def _peers():
    x, y, c = _mesh_pos()
    return (x, y, c), [((1 - x) if k & 4 else x, (1 - y) if k & 2 else y, (1 - c) if k & 1 else c)
                       for k in range(1, 8)]


def _exchange_copy(src_ref, land_ref, whole, send_sems, recv_sems, a, k, peer, slot):
    src = src_ref if whole else src_ref.at[_dev_index(peer)]
    return pltpu.make_async_remote_copy(
        src_ref=src, dst_ref=land_ref.at[slot], send_sem=send_sems.at[7 * a + k], recv_sem=recv_sems.at[7 * a + k],
        device_id=peer, device_id_type=MESH)


def _exchange_start(name, srcs, whole, dep):
    n = len(srcs)
    lands = [lax.empty(((N_DEV,) + s.shape) if w else s.shape, s.dtype) for s, w in zip(srcs, whole)]

    def body(*refs):
        src_refs, land_refs = refs[:n], refs[n:2 * n]
        send_sems, recv_sems, token = refs[2 * n + 1], refs[2 * n + 2], refs[-1]
        me, peers = _peers()
        for a in range(n):
            for k, peer in enumerate(peers):
                _exchange_copy(src_refs[a], land_refs[a], whole[a], send_sems, recv_sems, a, k, peer,
                               _dev_index(me)).start()
        token[...] = jnp.zeros_like(token)

    res = pl.pallas_call(
        body, name=name,
        out_shape=(pltpu.SemaphoreType.DMA((7 * n,)), pltpu.SemaphoreType.DMA((7 * n,)),
                   *[pltpu.HBM(a.shape, a.dtype) for a in srcs], *[pltpu.HBM(a.shape, a.dtype) for a in lands],
                   jax.ShapeDtypeStruct((SUBLANES, 128), F32)),
        in_specs=[HBM_SPEC] * (2 * n) + [ANY_SPEC],
        out_specs=(SEM_SPEC, SEM_SPEC, *([HBM_SPEC] * (2 * n)), VMEM_SPEC),
        input_output_aliases={i: 2 + i for i in range(2 * n)},
        compiler_params=pltpu.CompilerParams(has_side_effects=DATAFLOW),
    )(*[pltpu.with_memory_space_constraint(a, pltpu.HBM) for a in srcs],
      *[pltpu.with_memory_space_constraint(a, pltpu.HBM) for a in lands], dep)
    return res[0], res[1], list(res[2:2 + n]), list(res[2 + n:2 + 2 * n]), res[-1]


def _exchange_wait(name, started, whole, after, which=None):
    send_sems, recv_sems, srcs, lands, _ = started
    which = list(range(len(srcs))) if which is None else which
    srcs, lands = [srcs[a] for a in which], [lands[a] for a in which]
    n = len(srcs)

    def body(*refs):
        src_refs, land_refs = refs[:n], refs[n:2 * n]
        send_sems, recv_sems = refs[2 * n], refs[2 * n + 1]
        _, peers = _peers()
        for i, a in enumerate(which):
            for k, peer in enumerate(peers):
                cp = _exchange_copy(src_refs[i], land_refs[i], whole[a], send_sems, recv_sems, a, k, peer,
                                    _dev_index(peer))
                cp.wait_send()
                cp.wait_recv()

    res = pl.pallas_call(
        body, name=name,
        out_shape=[pltpu.HBM(a.shape, a.dtype) for a in srcs + lands],
        in_specs=[HBM_SPEC] * (2 * n) + [SEM_SPEC, SEM_SPEC, ANY_SPEC],
        out_specs=[HBM_SPEC] * (2 * n),
        input_output_aliases={i: i for i in range(2 * n)},
        compiler_params=pltpu.CompilerParams(has_side_effects=DATAFLOW),
    )(*srcs, *lands, send_sems, recv_sems, after)
    return list(res[:n]), list(res[n:])


def _gather_start(name, shards, dep):
    n = len(shards)
    lands = [lax.empty((N_DEV,) + a.shape, a.dtype) for a in shards]

    def body(*refs):
        src_refs, land_refs = refs[:n], refs[n:2 * n]
        send_sems, recv_sems, token = refs[2 * n + 1], refs[2 * n + 2], refs[-1]
        x, y, c = _mesh_pos()
        peers = [(x, y, 1 - c), (1 - x, y, c), (x, 1 - y, c), (1 - x, 1 - y, c)]
        for a in range(n):
            for k, peer in enumerate(peers):
                pltpu.make_async_remote_copy(
                    src_ref=src_refs[a], dst_ref=land_refs[a].at[_dev_index((x, y, c))], send_sem=send_sems.at[4 * a + k],
                    recv_sem=recv_sems.at[4 * a + k], device_id=peer, device_id_type=MESH).start()
        token[...] = jnp.zeros_like(token)

    res = pl.pallas_call(
        body, name=name,
        out_shape=(pltpu.SemaphoreType.DMA((4 * n,)), pltpu.SemaphoreType.DMA((4 * n,)),
                   *[pltpu.HBM(a.shape, a.dtype) for a in shards], *[pltpu.HBM(a.shape, a.dtype) for a in lands],
                   jax.ShapeDtypeStruct((SUBLANES, 128), F32)),
        in_specs=[HBM_SPEC] * (2 * n) + [ANY_SPEC],
        out_specs=(SEM_SPEC, SEM_SPEC, *([HBM_SPEC] * (2 * n)), VMEM_SPEC),
        input_output_aliases={i: 2 + i for i in range(2 * n)},
        compiler_params=pltpu.CompilerParams(has_side_effects=DATAFLOW),
    )(*[pltpu.with_memory_space_constraint(a, pltpu.HBM) for a in shards],
      *[pltpu.with_memory_space_constraint(a, pltpu.HBM) for a in lands], dep)
    return res[0], res[1], list(res[2:2 + n]), list(res[2 + n:2 + 2 * n]), res[-1]


def _gather_forward(name, send_sems, recv_sems, lands, which, after):
    n = len(which)

    def body(*refs):
        land_refs = refs[:n]
        send_sems, recv_sems = refs[n], refs[n + 1]
        fsend, frecv, token = refs[n + 3], refs[n + 4], refs[-1]
        x, y, c = _mesh_pos()
        chips = [(1 - x, y), (x, 1 - y), (1 - x, 1 - y)]
        for i, a in enumerate(which):
            for j, chip in enumerate(chips):
                block = land_refs[i].at[_dev_index((*chip, c))]
                pltpu.make_async_remote_copy(
                    src_ref=block, dst_ref=block, send_sem=send_sems.at[4 * a + 1 + j], recv_sem=recv_sems.at[4 * a + 1 + j],
                    device_id=(*chip, c), device_id_type=MESH).wait_recv()
                pltpu.make_async_remote_copy(
                    src_ref=block, dst_ref=block, send_sem=fsend.at[3 * i + j], recv_sem=frecv.at[3 * i + j],
                    device_id=(x, y, 1 - c), device_id_type=MESH).start()
        token[...] = jnp.zeros_like(token)

    res = pl.pallas_call(
        body, name=name,
        out_shape=(pltpu.SemaphoreType.DMA((3 * n,)), pltpu.SemaphoreType.DMA((3 * n,)),
                   *[pltpu.HBM(a.shape, a.dtype) for a in lands], jax.ShapeDtypeStruct((SUBLANES, 128), F32)),
        in_specs=[HBM_SPEC] * n + [SEM_SPEC, SEM_SPEC, ANY_SPEC],
        out_specs=(SEM_SPEC, SEM_SPEC, *([HBM_SPEC] * n), VMEM_SPEC),
        input_output_aliases={i: 2 + i for i in range(n)},
        compiler_params=pltpu.CompilerParams(has_side_effects=DATAFLOW),
    )(*lands, send_sems, recv_sems, after)
    return res[0], res[1], list(res[2:2 + n]), res[-1]


def _gather_wait(name, send_sems, recv_sems, fsend, frecv, srcs, lands, which, after):
    n = len(which)

    def body(*refs):
        land_refs = refs[n:2 * n]
        send_sems, recv_sems, fsend, frecv = refs[2 * n:2 * n + 4]
        x, y, c = _mesh_pos()
        sib = (x, y, 1 - c)
        chips = [(1 - x, y), (x, 1 - y), (1 - x, 1 - y)]
        for i, a in enumerate(which):
            def cp(slot, ssem, rsem):
                block = land_refs[i].at[_dev_index(slot)]
                return pltpu.make_async_remote_copy(src_ref=block, dst_ref=block, send_sem=ssem, recv_sem=rsem,
                                                    device_id=sib, device_id_type=MESH)
            cp(sib, send_sems.at[4 * a], recv_sems.at[4 * a]).wait_recv()
            for j, chip in enumerate(chips):
                cp((*chip, 1 - c), fsend.at[3 * i + j], frecv.at[3 * i + j]).wait_recv()
            for k in range(4):
                cp(sib, send_sems.at[4 * a + k], recv_sems.at[4 * a + k]).wait_send()
            for j in range(3):
                cp(sib, fsend.at[3 * i + j], frecv.at[3 * i + j]).wait_send()

    res = pl.pallas_call(
        body, name=name,
        out_shape=[pltpu.HBM(a.shape, a.dtype) for a in srcs + lands],
        in_specs=[HBM_SPEC] * (2 * n) + [SEM_SPEC] * 4 + [ANY_SPEC],
        out_specs=[HBM_SPEC] * (2 * n),
        input_output_aliases={i: i for i in range(2 * n)},
        compiler_params=pltpu.CompilerParams(has_side_effects=DATAFLOW),
    )(*srcs, *lands, send_sems, recv_sems, fsend, frecv, after)
    return list(res[n:])


def _local_step(x, mem, target, rel_bias, g_mix, w_in_g, w_sc, g_a, g_c, g_xattn, g_mem, g_ffn, w_fc, b_fc, g_final,
                dep, forward_weights, late_weights, emit, emit_small):
    s = x.shape[0]
    buckets = _bucket_tables()
    bias = _bias_fwd(rel_bias, buckets)

    h1, qs, ks, vs, gb, gc, xi = _rms_proj(x, g_mix, w_in_g, dep)
    qs, ks, vs = ([a[0][None]] + list(a[1:]) for a in (qs, ks, vs))
    group1, group2 = ["w_out", "w_xq", "w_xk", "w_xv", "w_xo"], ["w_up", "w_down"]
    tok = forward_weights(group1, h1)
    branches = []
    for p, dil in enumerate(DILATIONS):
        o_p, lse_p = _swa_fwd(qs[p], ks[p], vs[p], bias[p], dil, tok)
        branches.append([o_p[0], lse_p[0]] if dil == 1 else [o_p, lse_p])
    lw = late_weights(group1, branches[-1][0])
    w_out, w_xq, w_xk, w_xv, w_xo = (lw[n] for n in group1)
    attn, lses, mixed, x1 = _mix_out(branches, gb, gc, xi, x, w_sc, g_a, g_c, w_out)
    tok = forward_weights(group2, x1)
    mem_n, mk, mv = _mem_kv(mem, g_mem, w_xk, w_xv)
    h2, xq, xo, x2 = _xattn_fwd(x1, g_xattn, w_xq, mk, mv, w_xo, tok)
    lw = late_weights(group2, x2)
    w_up_g = lw["w_up"].reshape(FFN_CHUNKS, FFN_WIDTH, D_MODEL)
    w_down_g = lw["w_down"].reshape(FFN_CHUNKS // 2, FFN_WIDTH, D_MODEL)
    pairs = lambda a: a.reshape(FFN_CHUNKS, 2, a.shape[1], UP_CHUNK).transpose(0, 2, 1, 3).reshape(
        FFN_CHUNKS, a.shape[1], FFN_WIDTH)
    w_fc, b_fc = pairs(w_fc), pairs(b_fc)
    h3, up, conv, act, dx3, loss_acc, dg_final = _ffn_fwd(x2, g_ffn, w_up_g, w_fc, b_fc, w_down_g, g_final, target)

    gw_down = _dw(act, dx3, dep, "dw_down", a_chunked=True).reshape(N_DEV // 2, UP_CHUNK, D_MODEL)
    dup, dx2, dg_ffn, dw_fc, db_fc = _ffn_bwd(dx3, up, conv, x2, g_ffn, w_up_g, w_fc, w_down_g)
    gw_up = _dw(dup, h3, dep, "dw_up", a_chunked=True).reshape(N_DEV, UP_CHUNK, D_MODEL)
    tok = emit(dict(w_down=gw_down, w_up=gw_up))
    dxq, dx1, dmk, dmv, dg_xattn = _xattn_bwd(dx2, xo, xq, mk, mv, w_xo, w_xq, x1, g_xattn, tok)
    gw_xo = _dw(xo, dx2, tok, "dw_xo")[0]
    gw_xq = _dw(h2, dxq, tok, "dw_xq")[0]
    gw_xk, gw_xv, dg_mem = _mem_kv_bwd(dmk, dmv, mem_n, mem, w_xk, w_xv)
    tok = emit(dict(w_xo=gw_xo, w_xq=gw_xq, w_xk=gw_xk, w_xv=gw_xv))
    dattns, dds, dgb, dcv, dg_a, dg_c, dw_sc = _mix_out_bwd(dx1, w_out, attn, gb, gc, xi, w_sc, g_a, g_c, tok)
    first = lambda a: [a[0][None]] + list(a[1:])
    dattns, dds, lses = first(dattns), first(dds), first(lses)
    gw_out = _dw(mixed, dx1, tok, "dw_out")[0]
    tok = emit(dict(w_out=gw_out))
    dqs, dks, dvs, dbias = [], [], [], []
    for p, dil in enumerate(DILATIONS):
        dq_p, dk_p, dv_p, db_p = _swa_bwd(qs[p], ks[p], vs[p], dattns[p], lses[p], dds[p], bias[p], dil, tok)
        dqs.append(dq_p[0] if dil == 1 else dq_p)
        dks.append(dk_p[0] if dil == 1 else dk_p)
        dvs.append(dv_p[0] if dil == 1 else dv_p)
        dbias.append(db_p)
    d_relb = _bias_bwd(jnp.stack(dbias), buckets)
    dproj, grad_x, dg_mix = _in_proj_bwd(dqs, dks, dvs, dgb, dcv, gc, xi, w_sc, w_in_g, x, g_mix, dx1)
    pad = lambda a: jnp.pad(a, ((0, 0), (0, D_MODEL - a.shape[1])))
    small = jnp.concatenate([
        d_relb, dg_mix, dg_xattn, dg_mem, dg_ffn, dg_final, jnp.concatenate([dg_a, dg_c], axis=1),
        pad(dw_sc), pad(db_fc.reshape(N_DEV, UP_CHUNK)), pad(dw_fc.reshape(3 * N_DEV, UP_CHUNK)), pad(loss_acc)],
        axis=0)
    tok = emit_small(small)
    gw_in = _dw(h1, dproj, tok, "dw_in", n_chunks=N_DEV, chunk_cols=IN_CHUNK)
    emit(dict(w_in=gw_in))
    return grad_x


def kernel(x, mem, rel_bias, g_mix, w_in, w_short_conv, g_attn_out, g_conv_out, w_out, g_xattn, g_mem, w_xq, w_xk, w_xv, w_xo, g_ffn, w_up, w_ffn_conv, b_ffn_conv, w_down, g_final, loss_target, m_rel_bias, m_g_mix, m_w_in, m_w_short_conv, m_g_attn_out, m_g_conv_out, m_w_out, m_g_xattn, m_g_mem, m_w_xq, m_w_xk, m_w_xv, m_w_xo, m_g_ffn, m_w_up, m_w_ffn_conv, m_b_ffn_conv, m_w_down, m_g_final, v_rel_bias, v_g_mix, v_w_in, v_w_short_conv, v_g_attn_out, v_g_conv_out, v_w_out, v_g_xattn, v_g_mem, v_w_xq, v_w_xk, v_w_xv, v_w_xo, v_g_ffn, v_w_up, v_w_ffn_conv, v_b_ffn_conv, v_w_down, v_g_final):
    me = _dev_index(_mesh_pos())
    me_arr = me.reshape(1).astype(jnp.int32)

    big_names = ["w_in", "w_out", "w_xq", "w_xk", "w_xv", "w_xo", "w_up", "w_down"]
    late_names = big_names[1:]
    big_w = dict(w_in=w_in[0], w_out=w_out[0], w_xq=w_xq[0], w_xk=w_xk[0], w_xv=w_xv[0], w_xo=w_xo[0],
                 w_up=w_up[0].T, w_down=w_down[0])
    big_m = dict(w_in=m_w_in[0], w_out=m_w_out[0], w_xq=m_w_xq[0], w_xk=m_w_xk[0], w_xv=m_w_xv[0], w_xo=m_w_xo[0],
                 w_up=m_w_up[0].T, w_down=m_w_down[0])
    big_v = dict(w_in=v_w_in[0], w_out=v_w_out[0], w_xq=v_w_xq[0], w_xk=v_w_xk[0], w_xv=v_w_xv[0], w_xo=v_w_xo[0],
                 w_up=v_w_up[0].T, w_down=v_w_down[0])
    shard_shape = {n: big_w[n].shape for n in big_names}

    w_in_g, w_sc_g, w_fc_full = _all_gather([big_w["w_in"].astype(BF16), w_short_conv[0], w_ffn_conv[0]])
    w_sc_full = w_sc_g.transpose(1, 0, 2).reshape(3, CONV_W)
    late_shards = [big_w[n].astype(BF16) for n in late_names]
    ag_send, ag_recv, ag_srcs, ag_lands, ag_token = _gather_start("gather_weights_start", late_shards, w_in_g)
    forwarded = {}

    def forward_weights(names, after):
        which = [late_names.index(n) for n in names]
        fsend, frecv, lands, token = _gather_forward("gather_" + "_".join(names) + "_forward", ag_send, ag_recv,
                                                     [ag_lands[a] for a in which], which, after)
        forwarded[tuple(names)] = (fsend, frecv, lands)
        return token

    def late_weights(names, after):
        which = [late_names.index(n) for n in names]
        fsend, frecv, lands = forwarded[tuple(names)]
        lands = _gather_wait("gather_" + "_".join(names) + "_wait", ag_send, ag_recv, fsend, frecv,
                             [ag_srcs[a] for a in which], lands, which, after)
        out = {}
        for n, a, land in zip(names, which, lands):
            full = lax.dynamic_update_index_in_dim(land, late_shards[a], me, 0)
            if n == "w_up":
                out[n] = full
            elif n == "w_down":
                out[n] = full.reshape(N_DEV // 2, UP_CHUNK, D_MODEL)
            else:
                out[n] = full.reshape(D_MODEL, D_MODEL)
        return out

    sent = []

    def emit(grads):
        names = list(grads)
        blocks = [grads[n].reshape((N_DEV,) + shard_shape[n]) for n in names]
        started = _exchange_start("scatter_" + "_".join(names) + "_start", blocks, [False] * len(names), me_arr)
        sent.append((names, started))
        return started[-1]

    def emit_small(small):
        sent_small.append((small, _exchange_start("gather_small_start", [small], [True], me_arr)))
        return sent_small[0][1][-1]

    sent_small = []
    grad_x = _local_step(
        x[0], mem[0], loss_target[0], rel_bias, g_mix, w_in_g, w_sc_full, g_attn_out, g_conv_out, g_xattn, g_mem,
        g_ffn, w_fc_full, b_ffn_conv.reshape(N_DEV, 1, UP_CHUNK), g_final.reshape(1, D_MODEL), ag_token,
        forward_weights, late_weights, emit, emit_small)

    small_g, small_started = sent_small[0]
    after = sent[-1][1][-1]
    small_parts = _exchange_wait("gather_small_wait", small_started, [True], after)[1][0]
    big_out = {}
    after = small_parts
    for names, started in sent:
        blocks, lands = _exchange_wait("scatter_" + "_".join(names) + "_wait", started, [False] * len(names), after)
        for n, block, land in zip(names, blocks, lands):
            res = _adamw_big("adamw_" + n, big_w[n], block, land, big_m[n], big_v[n], me_arr)
            big_out[n] = [(r.T if n == "w_up" else r)[None] for r in res]
            after = res[0]

    as_rows = lambda a: a.reshape(N_DEV, UP_CHUNK)
    row1 = lambda a: a.reshape(1, D_MODEL)
    small_names = ["rel_bias", "g_mix", "g_attn_out", "g_conv_out", "g_xattn", "g_mem", "g_ffn", "b_ffn_conv", "g_final"]
    wmv = [
        (rel_bias, m_rel_bias, v_rel_bias), (g_mix, m_g_mix, v_g_mix), (g_attn_out, m_g_attn_out, v_g_attn_out),
        (g_conv_out, m_g_conv_out, v_g_conv_out), (g_xattn, m_g_xattn, v_g_xattn), (g_mem, m_g_mem, v_g_mem),
        (g_ffn, m_g_ffn, v_g_ffn), (as_rows(b_ffn_conv), as_rows(m_b_ffn_conv), as_rows(v_b_ffn_conv)),
        (row1(g_final), row1(m_g_final), row1(v_g_final))]
    g_packed, small_res = _adamw_small(small_g, small_parts, wmv, me_arr)
    small_out = dict(zip(small_names, small_res))
    loss = g_packed[ROW_LOSS, 0]
    small_out["b_ffn_conv"] = [a.reshape(1, 2 * D_FF) for a in small_out["b_ffn_conv"]]
    small_out["g_final"] = [a.reshape(D_MODEL) for a in small_out["g_final"]]

    g_wsc = lax.dynamic_slice(g_packed[ROW_WSC:ROW_WSC + 3, 0:CONV_W], (0, me * HEAD_DIM), (3, HEAD_DIM))
    g_wfc = lax.dynamic_slice(g_packed[ROW_WFC:ROW_WFC + 3 * N_DEV, 0:UP_CHUNK].reshape(3, N_DEV, UP_CHUNK),
                              (0, me, 0), (3, 1, UP_CHUNK)).reshape(3, UP_CHUNK)
    shard_res = _adamw_shards([(w_short_conv[0], g_wsc, m_w_short_conv[0], v_w_short_conv[0]),
                               (w_ffn_conv[0], g_wfc, m_w_ffn_conv[0], v_w_ffn_conv[0])])
    small_out["w_short_conv"] = [g_wsc[None]] + [a[None] for a in shard_res[0]]
    small_out["w_ffn_conv"] = [g_wfc[None]] + [a[None] for a in shard_res[1]]

    order = ["rel_bias", "g_mix", "w_in", "w_short_conv", "g_attn_out", "g_conv_out", "w_out", "g_xattn", "g_mem",
             "w_xq", "w_xk", "w_xv", "w_xo", "g_ffn", "w_up", "w_ffn_conv", "b_ffn_conv", "w_down", "g_final"]
    allp = {**big_out, **small_out}
    outs = [loss, grad_x[None]]
    for kind in range(4):
        outs += [allp[n][kind] for n in order]
    return tuple(outs)
```

```python
import math

import numpy as np
import jax
import jax.numpy as jnp
from jax import lax
from jax.experimental import pallas as pl
from jax.experimental.pallas import tpu as pltpu

F32 = jnp.float32
BF16 = jnp.bfloat16
MESH = pl.DeviceIdType.MESH

N_DEV = 8
D_MODEL = 1024
ATTN_W = 512
CONV_W = 512
N_HEADS = 8
HEAD_DIM = 64
WIN = 128
DILATIONS = (1, 4, 16)
N_BUCKETS = 32
BUCKET_MAX_EXACT = 16
BUCKET_MAX_DISTANCE = 2048
N_MEM_HEADS = 4
MEM_HEAD_DIM = 256
D_FF = 2816
IN_COLS = 3072
IN_CHUNK = IN_COLS // N_DEV
UP_CHUNK = 2 * D_FF // N_DEV
FFN_CHUNKS = 4
FFN_WIDTH = 2 * D_FF // FFN_CHUNKS
EPS = 1e-6

ADAM_LR = 0.001
ADAM_B1 = 0.9
ADAM_B2 = 0.999
ADAM_EPS = 1e-08
ADAM_WD = 0.01
ADAM_STEP = 10

SUBLANES = 8
LANES = 128
HALO = 16
TM = 512
TM_FFN = 256
TS_DW = 4096
SWA_BLOCKS = 8
VMEM_LIMIT = 56 * 1024 * 1024

ROW_RELB, ROW_GMIX, ROW_GXATTN, ROW_GMEM, ROW_GFFN, ROW_GFINAL, ROW_GAC = 0, 8, 16, 24, 32, 40, 48
ROW_WSC, ROW_BFC, ROW_WFC, ROW_LOSS, SMALL_ROWS = 56, 64, 72, 96, 104


def _cparams(n_grid):
    return pltpu.CompilerParams(dimension_semantics=("arbitrary",) * n_grid, vmem_limit_bytes=VMEM_LIMIT)


def _full(shape):
    nd = len(shape)
    return pl.BlockSpec(tuple(shape), lambda *_: (0,) * nd)


def _resident(shape):
    nd = len(shape)
    return pl.BlockSpec(tuple(shape), lambda *_: (0,) * nd, pipeline_mode=pl.Buffered(1))


ANY_SPEC = pl.BlockSpec(memory_space=pl.ANY)
HBM_SPEC = pl.BlockSpec(memory_space=pltpu.HBM)
SEM_SPEC = pl.BlockSpec(memory_space=pltpu.SEMAPHORE)
VMEM_SPEC = pl.BlockSpec(memory_space=pltpu.VMEM)
SMEM_SPEC = pl.BlockSpec(memory_space=pltpu.SMEM)
DATAFLOW = pltpu.SideEffectType.DATAFLOW_SIDE_EFFECTING


def _rms(x):
    r = lax.rsqrt(jnp.mean(x * x, axis=-1, keepdims=True) + EPS)
    return x * r, r


def _rms_bwd(xh, r, g, dy):
    dxh = dy * g
    return r * (dxh - xh * jnp.mean(dxh * xh, axis=-1, keepdims=True))


def _shift_down(u, halo, k):
    ru = pltpu.roll(u, k, 0)
    rh = pltpu.roll(halo, k, 0)
    row = lax.broadcasted_iota(jnp.int32, rh.shape, 0)
    head = jnp.where(row < k, rh, ru[0:SUBLANES])
    return jnp.concatenate([head, ru[SUBLANES:]], axis=0)


def _shift_up(u, halo, k):
    tm = u.shape[0]
    ru = pltpu.roll(u, tm - k, 0)
    rh = pltpu.roll(halo, SUBLANES - k, 0)
    row = lax.broadcasted_iota(jnp.int32, rh.shape, 0)
    tail = jnp.where(row >= SUBLANES - k, rh, ru[tm - SUBLANES:])
    return jnp.concatenate([ru[:tm - SUBLANES], tail], axis=0)


def _causal_conv3(u, halo, w_ref):
    return (_shift_down(u, halo, 2) * w_ref[0:1, :] + _shift_down(u, halo, 1) * w_ref[1:2, :]) + u * w_ref[2:3, :]


def _dot(a, b):
    return jnp.dot(a, b, preferred_element_type=F32)


def _dot_nt(a, b):
    return lax.dot_general(a, b, (((1,), (1,)), ((), ())), preferred_element_type=F32)


def _dot_tn(a, b):
    return lax.dot_general(a, b, (((0,), (0,)), ((), ())), preferred_element_type=F32)


def _sigmoid(x):
    return 0.5 * jnp.tanh(0.5 * x) + 0.5


def _bucket_tables():
    qi = np.arange(WIN)[:, None]
    kj = np.arange(2 * WIN)[None, :]
    steps = np.clip(qi + WIN - kj, 0, WIN)
    out = []
    for d in DILATIONS:
        dist = steps * d
        dd = np.maximum(dist, 1).astype(np.float32)
        large = BUCKET_MAX_EXACT + (
            np.log(dd / np.float32(BUCKET_MAX_EXACT)) / np.float32(math.log(BUCKET_MAX_DISTANCE / BUCKET_MAX_EXACT))
            * np.float32(N_BUCKETS - BUCKET_MAX_EXACT)).astype(np.int32)
        large = np.minimum(large, N_BUCKETS - 1)
        out.append(np.where(dist < BUCKET_MAX_EXACT, dist, large).astype(np.int32))
    return np.stack(out)


def _band_mask():
    qi = lax.broadcasted_iota(jnp.int32, (WIN, 2 * WIN), 0)
    kj = lax.broadcasted_iota(jnp.int32, (WIN, 2 * WIN), 1)
    steps = qi + WIN - kj
    return (steps >= 0) & (steps <= WIN)


def _bias_fwd(rel_bias, buckets):
    present = [sorted(set(buckets[p].ravel().tolist())) for p in range(3)]

    def body(rb_ref, bk_ref, o_ref):
        band = _band_mask()
        for p in range(3):
            bk = bk_ref[p]
            for h in range(N_HEADS):
                acc = jnp.zeros((WIN, 2 * WIN), F32)
                for b in present[p]:
                    acc = jnp.where(bk == b, rb_ref[h, b], acc)
                o_ref[p, h] = jnp.where(band, acc, -jnp.inf)

    return pl.pallas_call(
        body, name="bias_fwd",
        out_shape=jax.ShapeDtypeStruct((3, N_HEADS, WIN, 2 * WIN), F32),
        in_specs=[pl.BlockSpec(memory_space=pltpu.SMEM), pl.BlockSpec(memory_space=pltpu.VMEM)],
        out_specs=pl.BlockSpec(memory_space=pltpu.VMEM),
    )(rel_bias, jnp.asarray(buckets))


def _bias_bwd(dbias, buckets):
    present = [set(buckets[p].ravel().tolist()) for p in range(3)]

    def body(db_ref, bk_ref, o_ref):
        lane = lax.broadcasted_iota(jnp.int32, (1, D_MODEL), 1)
        rows = []
        for h in range(N_HEADS):
            row = jnp.zeros((1, D_MODEL), F32)
            for b in range(N_BUCKETS):
                tot = jnp.zeros((1, 1), F32)
                for p in (p for p in range(3) if b in present[p]):
                    sel = jnp.where(bk_ref[p] == b, db_ref[p, h], 0.0)
                    tot = tot + jnp.sum(jnp.sum(sel, axis=0, keepdims=True), axis=1, keepdims=True)
                row = jnp.where(lane == b, tot, row)
            rows.append(row)
        o_ref[...] = jnp.concatenate(rows, axis=0)

    return pl.pallas_call(
        body, name="bias_bwd",
        out_shape=jax.ShapeDtypeStruct((N_HEADS, D_MODEL), F32),
        in_specs=[pl.BlockSpec(memory_space=pltpu.VMEM), pl.BlockSpec(memory_space=pltpu.VMEM)],
        out_specs=pl.BlockSpec(memory_space=pltpu.VMEM),
    )(dbias, jnp.asarray(buckets))


def _spread(val, scr_ref, out_refs, dtype):
    out_refs[0][...] = val.astype(dtype)
    n_blk = val.shape[1] // LANES
    for c in range(n_blk):
        scr_ref[c] = val[:, c * LANES:(c + 1) * LANES]
    for o_ref, d in zip(out_refs[1:], DILATIONS[1:]):
        for r in range(d):
            for c in range(n_blk):
                o_ref[r, :, c * LANES:(c + 1) * LANES] = scr_ref.at[c][pl.ds(r, TM // d, stride=d), :].astype(dtype)


def _gather_classes(blk_ref, scr_ref, d):
    n_blk = blk_ref.shape[2] // LANES
    for r in range(d):
        for c in range(n_blk):
            scr_ref.at[c][pl.ds(r, TM // d, stride=d), :] = blk_ref[r, :, c * LANES:(c + 1) * LANES].astype(F32)
    return jnp.concatenate([scr_ref[c] for c in range(n_blk)], axis=1)


def _class_specs(cols):
    return [pl.BlockSpec((TM, cols), lambda i: (i, 0))] + [
        pl.BlockSpec((d, TM // d, cols), lambda i: (0, i, 0)) for d in DILATIONS[1:]]


def _class_shapes(s, cols, dtype):
    return [jax.ShapeDtypeStruct((s, cols), dtype)] + [
        jax.ShapeDtypeStruct((d, s // d, cols), dtype) for d in DILATIONS[1:]]


def _load_w_in_pairs(w_hbm, w_scr, sems):
    @pl.when(pl.program_id(0) == 0)
    def _():
        copies = [pltpu.make_async_copy(w_hbm.at[j], w_scr.at[j // 2, :, pl.ds((j % 2) * IN_CHUNK, IN_CHUNK)],
                                        sems.at[j]) for j in range(N_DEV)]
        for copy in copies:
            copy.start()
        for copy in copies:
            copy.wait()


W_IN_PAIRS = [pltpu.VMEM((N_DEV // 2, D_MODEL, 2 * IN_CHUNK), BF16), pltpu.SemaphoreType.DMA((N_DEV,))]


def _rms_proj(x, g_mix, w_in_g, dep):
    s = x.shape[0]

    def body(x_ref, g_ref, w_hbm, dep_ref, h_ref, q1, q4, q16, k1, k4, k16, v1, v4, v16, gb_ref, gc_ref, xi_ref, scr,
             w_scr, w_sems):
        _load_w_in_pairs(w_hbm, w_scr, w_sems)
        xh, _ = _rms(x_ref[...])
        h = (xh * g_ref[...]).astype(BF16)
        h_ref[...] = h
        proj = jnp.concatenate([_dot(h, w_scr[j]) for j in range(N_DEV // 2)], axis=1)
        _spread(proj[:, 0:512] * (HEAD_DIM ** -0.5), scr, (q1, q4, q16), BF16)
        _spread(proj[:, 512:1024], scr, (k1, k4, k16), BF16)
        _spread(proj[:, 1024:1536], scr, (v1, v4, v16), BF16)
        gb_ref[...] = proj[:, 1536:2048]
        gc_ref[...] = proj[:, 2048:2560]
        xi_ref[...] = proj[:, 2560:3072]

    row = lambda n: pl.BlockSpec((TM, n), lambda i: (i, 0))
    res = pl.pallas_call(
        body, name="rms_proj", grid=(s // TM,),
        out_shape=[jax.ShapeDtypeStruct((s, D_MODEL), BF16)] + _class_shapes(s, 512, BF16) * 3
        + [jax.ShapeDtypeStruct((s, 512), F32)] * 3,
        in_specs=[row(D_MODEL), _full(g_mix.shape), ANY_SPEC, ANY_SPEC],
        out_specs=[row(D_MODEL)] + _class_specs(512) * 3 + [row(512)] * 3,
        scratch_shapes=[pltpu.VMEM((512 // LANES, TM, LANES), F32)] + W_IN_PAIRS,
        compiler_params=_cparams(1),
    )(x, g_mix, w_in_g, dep)
    return res[0], res[1:4], res[4:7], res[7:10], res[10], res[11], res[12]


def _pair_split(x2):
    lane = lax.broadcasted_iota(jnp.int32, x2.shape, 1)
    zero = jnp.zeros_like(x2)
    return jnp.where(lane < HEAD_DIM, x2, zero), jnp.where(lane >= HEAD_DIM, x2, zero)


def _pair_join(even, odd):
    lane = lax.broadcasted_iota(jnp.int32, (even.shape[0], LANES), 1)
    return jnp.where(lane < HEAD_DIM, even, odd)


def _swa_steps(qc, dil):
    n128 = qc.shape[1] // WIN
    nsub = min(SWA_BLOCKS, n128)
    nb = n128 // nsub
    ncls = min(dil, SWA_BLOCKS // nsub) if nb == 1 else 1
    return nsub, nb, ncls


def _swa_fwd(qc, kc, vc, bias, dil, dep):
    nsub, nb, ncls = _swa_steps(qc, dil)
    whole = nb == 1

    def body(q_ref, kp_ref, kc_ref, vp_ref, vc_ref, b_ref, dep_ref, o_ref, lse_ref, s_scr, p_scr):
        no_prev = (pl.program_id(1) == 0) & (lax.broadcasted_iota(jnp.int32, (WIN, 2 * WIN), 1) < WIN)
        pairs = [slice(a * LANES, (a + 1) * LANES) for a in range(N_HEADS // 2)]
        for c, t in [(c, t) for c in range(ncls) for t in range(nsub)]:
            i = c * nsub + t
            rows = slice(t * WIN, (t + 1) * WIN)
            alone = whole and t == 0
            cols = slice(WIN, 2 * WIN) if alone else slice(0, 2 * WIN)

            def keys(prev_ref, cur_ref, sl):
                if alone:
                    return cur_ref[c, rows, sl]
                if t == 0:
                    return jnp.concatenate([prev_ref[c, :, sl], cur_ref[c, rows, sl]], axis=0)
                return cur_ref[c, (t - 1) * WIN:(t + 1) * WIN, sl]

            for a, sl in enumerate(pairs):
                k2 = keys(kp_ref, kc_ref, sl)
                for e, qh in enumerate(_pair_split(q_ref[c, rows, sl])):
                    s_scr[i, 2 * a + e, :, cols] = _dot_nt(qh, k2)
            den, lse = [], []
            for h in range(N_HEADS):
                lg = s_scr[i, h, :, cols] + b_ref[h, :, cols]
                if t == 0 and not whole:
                    lg = jnp.where(no_prev, -jnp.inf, lg)
                m = jnp.max(lg, axis=-1, keepdims=True)
                p = jnp.exp(lg - m)
                den.append(jnp.sum(p, axis=-1, keepdims=True))
                p_scr[i, h, :, cols] = p.astype(BF16)
                lse.append(m + jnp.log(den[h]))
            for a, sl in enumerate(pairs):
                v_even, v_odd = _pair_split(keys(vp_ref, vc_ref, sl))
                o2 = _dot(p_scr[i, 2 * a, :, cols], v_even) + _dot(p_scr[i, 2 * a + 1, :, cols], v_odd)
                o_ref[c, rows, sl] = o2 / _pair_join(den[2 * a], den[2 * a + 1])
                lse_ref[c, rows, sl] = _pair_join(lse[2 * a], lse[2 * a + 1])

    cur = pl.BlockSpec((ncls, nsub * WIN, 512), lambda r, b: (r, b, 0))
    prev = pl.BlockSpec((ncls, WIN, 512), lambda r, b: (r, jnp.maximum(nsub * b - 1, 0), 0))
    wide = (ncls * nsub, N_HEADS, WIN, 2 * WIN)
    return pl.pallas_call(
        body, name=f"swa_fwd_d{dil}", grid=(dil // ncls, nb),
        out_shape=[jax.ShapeDtypeStruct(qc.shape, F32)] * 2,
        in_specs=[cur, prev, cur, prev, cur, _full(bias.shape), ANY_SPEC],
        out_specs=[cur] * 2,
        scratch_shapes=[pltpu.VMEM(wide, F32), pltpu.VMEM(wide, BF16)],
        compiler_params=_cparams(2),
    )(qc, kc, kc, vc, vc, bias, dep)


def _mix_out(branches, gb, gc, xi, x, w_sc, g_a, g_c, w_out):
    s = x.shape[0]
    tb = TM // SUBLANES

    def body(o1, l1, o4, l4, o16, l16, gb_ref, gc_ref, xi_ref, gch_ref, xih_ref, x_ref, wsc_ref,
             ga_ref, gcv_ref, wout_ref, attn_ref, lse1, lse4, lse16, mixed_ref, x1_ref, scr_a, scr_b, scr_c, scr_d):
        i = pl.program_id(0)
        la, lb, lc = l1[...], _gather_classes(l4, scr_a, 4), _gather_classes(l16, scr_b, 16)
        m_all = jnp.maximum(jnp.maximum(la, lb), lc)
        ea, eb, ec = jnp.exp(la - m_all), jnp.exp(lb - m_all), jnp.exp(lc - m_all)
        den = (ea + eb) + ec
        num = (ea * o1[...] + eb * _gather_classes(o4, scr_c, 4)) + ec * _gather_classes(o16, scr_d, 16)
        attn = num / den
        attn_ref[...] = attn
        _spread(m_all + jnp.log(den), scr_a, (lse1, lse4, lse16), F32)
        xa, _ = _rms(attn)
        u = gc_ref[...] * xi_ref[...]
        uh = jnp.where(i > 0, gch_ref[...] * xih_ref[...], 0.0)
        conv = gb_ref[...] * _causal_conv3(u, uh, wsc_ref)
        xc, _ = _rms(conv)
        mixed = jnp.concatenate([xa * ga_ref[...], xc * gcv_ref[...]], axis=1).astype(BF16)
        mixed_ref[...] = mixed
        x1_ref[...] = x_ref[...] + _dot(mixed, wout_ref[...])

    row = lambda n: pl.BlockSpec((TM, n), lambda i: (i, 0))
    halo = pl.BlockSpec((SUBLANES, 512), lambda i: (jnp.maximum(i * tb - 1, 0), 0))
    cs = _class_specs(512)
    flat = [a for br in branches for a in br]
    res = pl.pallas_call(
        body, name="mix_out", grid=(s // TM,),
        out_shape=[jax.ShapeDtypeStruct((s, 512), F32)] + _class_shapes(s, 512, F32)
        + [jax.ShapeDtypeStruct((s, D_MODEL), BF16), jax.ShapeDtypeStruct((s, D_MODEL), F32)],
        in_specs=[cs[0], cs[0], cs[1], cs[1], cs[2], cs[2], row(512), row(512), row(512), halo, halo,
                  row(D_MODEL), _full(w_sc.shape), _full(g_a.shape), _full(g_c.shape), _full(w_out.shape)],
        out_specs=[row(512)] + cs + [row(D_MODEL), row(D_MODEL)],
        scratch_shapes=[pltpu.VMEM((512 // LANES, TM, LANES), F32)] * 4,
        compiler_params=_cparams(1),
    )(*flat, gb, gc, xi, gc, xi, x, w_sc, g_a, g_c, w_out)
    return res[0], res[1:4], res[4], res[5]


def _mem_kv(mem, g_mem, w_xk, w_xv):
    def body(mem_ref, g_ref, wk_ref, wv_ref, mn_ref, k_ref, v_ref):
        xh, _ = _rms(mem_ref[...])
        mn = (xh * g_ref[...]).astype(BF16)
        mn_ref[...] = mn
        k_ref[...] = _dot(mn, wk_ref[...]).astype(BF16)
        v_ref[...] = _dot(mn, wv_ref[...]).astype(BF16)

    vm = pl.BlockSpec(memory_space=pltpu.VMEM)
    return pl.pallas_call(
        body, name="mem_kv",
        out_shape=[jax.ShapeDtypeStruct(mem.shape, BF16)] * 3,
        in_specs=[vm] * 4, out_specs=[vm] * 3,
        compiler_params=pltpu.CompilerParams(vmem_limit_bytes=VMEM_LIMIT),
    )(mem, g_mem, w_xk, w_xv)


def _xattn_fwd(x1, g, w_xq, k, v, w_xo, dep):
    s = x1.shape[0]

    def body(x1_ref, g_ref, wq_ref, k_ref, v_ref, wo_ref, dep_ref, h2_ref, q_ref, o_ref, x2_ref):
        x1v = x1_ref[...]
        xh, _ = _rms(x1v)
        h2 = (xh * g_ref[...]).astype(BF16)
        h2_ref[...] = h2
        qb = _dot(h2, wq_ref[...]).astype(BF16)
        q_ref[...] = qb
        outs = []
        for h in range(N_MEM_HEADS):
            sl = slice(h * MEM_HEAD_DIM, (h + 1) * MEM_HEAD_DIM)
            lg = _dot_nt(qb[:, sl], k_ref[:, sl]) * (MEM_HEAD_DIM ** -0.5)
            p = jnp.exp(lg - jnp.max(lg, axis=-1, keepdims=True))
            p = p / jnp.sum(p, axis=-1, keepdims=True)
            outs.append(_dot(p.astype(BF16), v_ref[:, sl]))
        o = jnp.concatenate(outs, axis=1).astype(BF16)
        o_ref[...] = o
        x2_ref[...] = x1v + _dot(o, wo_ref[...])

    row = pl.BlockSpec((TM, D_MODEL), lambda i: (i, 0))
    return pl.pallas_call(
        body, name="xattn_fwd", grid=(s // TM,),
        out_shape=[jax.ShapeDtypeStruct((s, D_MODEL), BF16)] * 3 + [jax.ShapeDtypeStruct((s, D_MODEL), F32)],
        in_specs=[row, _full(g.shape), _full(w_xq.shape), _full(k.shape), _full(v.shape), _full(w_xo.shape), ANY_SPEC],
        out_specs=[row] * 4,
        compiler_params=_cparams(1),
    )(x1, g, w_xq, k, v, w_xo, dep)


def _ffn_conv(h_ext, wup_ref, wfc_ref, bfc_ref, j):
    u = _dot_nt(h_ext, wup_ref[j])
    w = wfc_ref[j]
    c = ((pltpu.roll(u, 2, 0) * w[0:1, :] + pltpu.roll(u, 1, 0) * w[1:2, :]) + u * w[2:3, :]) + bfc_ref[j]
    return c[HALO:], u[HALO:]


def _ffn_fwd(x2, g, w_up_g, w_fc, b_fc, w_down_g, g_final, target):
    s = x2.shape[0]
    tb = TM_FFN // HALO
    n_ch, wid = w_up_g.shape[:2]
    half = n_ch // 2

    def body(x_ref, xp_ref, g_ref, wup_ref, wfc_ref, bfc_ref, wd_ref, gf_ref, t_ref, h_ref, u_ref, c_ref, act_ref,
             dx3_ref, loss_ref, dgf_ref):
        i = pl.program_id(0)

        @pl.when(i == 0)
        def _():
            loss_ref[...] = jnp.zeros_like(loss_ref)
            dgf_ref[...] = jnp.zeros_like(dgf_ref)

        x2v = x_ref[...]
        gv = g_ref[...]
        h = (_rms(x2v)[0] * gv).astype(BF16)
        h_ref[...] = h
        hp = jnp.where(i > 0, _rms(xp_ref[...])[0] * gv, 0.0).astype(BF16)
        h_ext = jnp.concatenate([hp, h], axis=0)
        down = jnp.zeros((TM_FFN, D_MODEL), F32)
        for j in range(half):
            cg, ug = _ffn_conv(h_ext, wup_ref, wfc_ref, bfc_ref, j)
            cv, uv = _ffn_conv(h_ext, wup_ref, wfc_ref, bfc_ref, j + half)
            c_ref[j] = cg
            c_ref[j + half] = cv
            u_ref[j] = ug.astype(BF16)
            u_ref[j + half] = uv.astype(BF16)
            a = ((cg * _sigmoid(cg)) * cv).astype(BF16)
            act_ref[j] = a
            down = down + _dot(a, wd_ref[j])
        x3 = x2v + down
        xh, r = _rms(x3)
        gf = gf_ref[...]
        e = xh * gf - t_ref[...]
        loss_ref[...] += 0.5 * jnp.sum(jnp.sum(e * e, axis=1, keepdims=True), axis=0, keepdims=True) / D_MODEL
        dy = e * (1.0 / D_MODEL)
        dgf_ref[0:1, :] += jnp.sum(dy * xh, axis=0, keepdims=True)
        dx3_ref[...] = _rms_bwd(xh, r, gf, dy)

    row = pl.BlockSpec((TM_FFN, D_MODEL), lambda i: (i, 0))
    prev = pl.BlockSpec((HALO, D_MODEL), lambda i: (jnp.maximum(i * tb - 1, 0), 0))
    return pl.pallas_call(
        body, name="ffn_fwd", grid=(s // TM_FFN,),
        out_shape=[jax.ShapeDtypeStruct((s, D_MODEL), BF16), jax.ShapeDtypeStruct((n_ch, s, wid), BF16),
                   jax.ShapeDtypeStruct((n_ch, s, wid), F32), jax.ShapeDtypeStruct((half, s, wid), BF16),
                   jax.ShapeDtypeStruct((s, D_MODEL), F32), jax.ShapeDtypeStruct((SUBLANES, 128), F32),
                   jax.ShapeDtypeStruct((SUBLANES, D_MODEL), F32)],
        in_specs=[row, prev, _full(g.shape), _resident(w_up_g.shape), _full(w_fc.shape), _full(b_fc.shape),
                  _resident(w_down_g.shape), _full(g_final.shape), row],
        out_specs=[row, pl.BlockSpec((n_ch, TM_FFN, wid), lambda i: (0, i, 0)),
                   pl.BlockSpec((n_ch, TM_FFN, wid), lambda i: (0, i, 0)),
                   pl.BlockSpec((half, TM_FFN, wid), lambda i: (0, i, 0)), row,
                   _full((SUBLANES, 128)), _full((SUBLANES, D_MODEL))],
        compiler_params=_cparams(1),
    )(x2, x2, g, w_up_g, w_fc, b_fc, w_down_g, g_final, target)


def _ffn_bwd(dx3, up, conv, x2, g, w_up_g, w_fc, w_down_g):
    s = x2.shape[0]
    tb = TM_FFN // HALO
    last = s // HALO - 1
    n_tiles = s // TM_FFN
    n_ch, wid = w_up_g.shape[:2]
    half = n_ch // 2
    n_ext = TM_FFN + HALO

    def body(dx_ref, dxn_ref, u_ref, c_ref, cn_ref, x2_ref, g_ref, wup_ref, wfc_ref, wd_ref,
             dup_ref, dx2_ref, dg_ref, dwfc_ref, dbfc_ref):
        i = pl.program_id(0)

        @pl.when(i == 0)
        def _():
            dg_ref[...] = jnp.zeros_like(dg_ref)
            dwfc_ref[...] = jnp.zeros_like(dwfc_ref)
            dbfc_ref[...] = jnp.zeros_like(dbfc_ref)

        dxv = dx_ref[...]
        dxn = jnp.where(i < n_tiles - 1, dxn_ref[...], 0.0)
        dx_ext = jnp.concatenate([dxv, dxn], axis=0).astype(BF16)
        dh = jnp.zeros((TM_FFN, D_MODEL), F32)
        for j in range(half):
            cg = jnp.concatenate([c_ref[j], cn_ref[j]], axis=0)
            cv = jnp.concatenate([c_ref[j + half], cn_ref[j + half]], axis=0)
            dact = _dot_nt(dx_ext, wd_ref[j])
            sg = _sigmoid(cg)
            silu = cg * sg
            parts = ((j + half, dact * silu), (j, (dact * cv) * (sg + silu * (1.0 - sg))))
            for jj, dc in parts:
                u = u_ref[jj].astype(F32)
                dc0, dc1, dc2 = dc[:TM_FFN], pltpu.roll(dc, n_ext - 1, 0)[:TM_FFN], pltpu.roll(dc, n_ext - 2, 0)[:TM_FFN]
                dbfc_ref[jj:jj + 1, :] += jnp.sum(dc0, axis=0, keepdims=True)
                dwfc_ref[0, jj:jj + 1, :] += jnp.sum(dc2 * u, axis=0, keepdims=True)
                dwfc_ref[1, jj:jj + 1, :] += jnp.sum(dc1 * u, axis=0, keepdims=True)
                dwfc_ref[2, jj:jj + 1, :] += jnp.sum(dc0 * u, axis=0, keepdims=True)
                w = wfc_ref[jj]
                du = ((dc0 * w[2:3, :] + dc1 * w[1:2, :]) + dc2 * w[0:1, :]).astype(BF16)
                dup_ref[jj] = du
                dh = dh + _dot(du, wup_ref[jj])
        xh, r = _rms(x2_ref[...])
        dg_ref[0:1, :] += jnp.sum(dh * xh, axis=0, keepdims=True)
        dx2_ref[...] = dxv + _rms_bwd(xh, r, g_ref[...], dh)

    row = pl.BlockSpec((TM_FFN, D_MODEL), lambda i: (i, 0))
    nxt = pl.BlockSpec((HALO, D_MODEL), lambda i: (jnp.minimum((i + 1) * tb, last), 0))
    cur_c = pl.BlockSpec((n_ch, TM_FFN, wid), lambda i: (0, i, 0))
    nxt_c = pl.BlockSpec((n_ch, HALO, wid), lambda i: (0, jnp.minimum((i + 1) * tb, last), 0))
    return pl.pallas_call(
        body, name="ffn_bwd", grid=(n_tiles,),
        out_shape=[jax.ShapeDtypeStruct((n_ch, s, wid), BF16), jax.ShapeDtypeStruct((s, D_MODEL), F32),
                   jax.ShapeDtypeStruct((SUBLANES, D_MODEL), F32), jax.ShapeDtypeStruct((3, n_ch, wid), F32),
                   jax.ShapeDtypeStruct((n_ch, wid), F32)],
        in_specs=[row, nxt, cur_c, cur_c, nxt_c, row, _full(g.shape), _resident(w_up_g.shape), _full(w_fc.shape),
                  _resident(w_down_g.shape)],
        out_specs=[cur_c, row, _full((SUBLANES, D_MODEL)), _full((3, n_ch, wid)), _full((n_ch, wid))],
        compiler_params=_cparams(1),
    )(dx3, dx3, up, conv, conv, x2, g, w_up_g, w_fc, w_down_g)


def _xattn_bwd(dx2, o, q, k, v, w_xo, w_xq, x1, g, dep):
    s = x1.shape[0]

    def body(dx2_ref, o_ref, q_ref, k_ref, v_ref, wo_ref, wq_ref, x1_ref, g_ref, dep_ref, dq_ref, dx1_ref, dk_ref,
             dv_ref, dg_ref):
        @pl.when(pl.program_id(0) == 0)
        def _():
            dk_ref[...] = jnp.zeros_like(dk_ref)
            dv_ref[...] = jnp.zeros_like(dv_ref)
            dg_ref[...] = jnp.zeros_like(dg_ref)

        dx2v = dx2_ref[...]
        do = _dot_nt(dx2v.astype(BF16), wo_ref[...])
        dqs = []
        for h in range(N_MEM_HEADS):
            sl = slice(h * MEM_HEAD_DIM, (h + 1) * MEM_HEAD_DIM)
            qh, kh, vh = q_ref[:, sl], k_ref[:, sl], v_ref[:, sl]
            lg = _dot_nt(qh, kh) * (MEM_HEAD_DIM ** -0.5)
            p = jnp.exp(lg - jnp.max(lg, axis=-1, keepdims=True))
            p = p / jnp.sum(p, axis=-1, keepdims=True)
            doh = do[:, sl].astype(BF16)
            dp = _dot_nt(doh, vh)
            ds = (p * (dp - jnp.sum(p * dp, axis=-1, keepdims=True)) * (MEM_HEAD_DIM ** -0.5)).astype(BF16)
            dqs.append(_dot(ds, kh))
            dk_ref[:, sl] += _dot_tn(ds, qh)
            dv_ref[:, sl] += _dot_tn(p.astype(BF16), doh)
        dq = jnp.concatenate(dqs, axis=1).astype(BF16)
        dq_ref[...] = dq
        dh2 = _dot_nt(dq, wq_ref[...])
        xh, r = _rms(x1_ref[...])
        dg_ref[0:1, :] += jnp.sum(dh2 * xh, axis=0, keepdims=True)
        dx1_ref[...] = dx2v + _rms_bwd(xh, r, g_ref[...], dh2)

    row = pl.BlockSpec((TM, D_MODEL), lambda i: (i, 0))
    return pl.pallas_call(
        body, name="xattn_bwd", grid=(s // TM,),
        out_shape=[jax.ShapeDtypeStruct((s, D_MODEL), BF16), jax.ShapeDtypeStruct((s, D_MODEL), F32),
                   jax.ShapeDtypeStruct(k.shape, F32), jax.ShapeDtypeStruct(k.shape, F32),
                   jax.ShapeDtypeStruct((SUBLANES, D_MODEL), F32)],
        in_specs=[row, row, row, _full(k.shape), _full(v.shape), _full(w_xo.shape), _full(w_xq.shape), row,
                  _full(g.shape), ANY_SPEC],
        out_specs=[row, row, _full(k.shape), _full(k.shape), _full((SUBLANES, D_MODEL))],
        compiler_params=_cparams(1),
    )(dx2, o, q, k, v, w_xo, w_xq, x1, g, dep)


def _mem_kv_bwd(dk, dv, mem_n, mem, w_xk, w_xv):
    def body(dk_ref, dv_ref, mn_ref, mem_ref, wk_ref, wv_ref, dwk_ref, dwv_ref, dg_ref):
        dkb, dvb = dk_ref[...].astype(BF16), dv_ref[...].astype(BF16)
        mn = mn_ref[...]
        dwk_ref[...] = _dot_tn(mn, dkb).astype(BF16)
        dwv_ref[...] = _dot_tn(mn, dvb).astype(BF16)
        dmn = _dot_nt(dkb, wk_ref[...]) + _dot_nt(dvb, wv_ref[...])
        xh, _ = _rms(mem_ref[...])
        dg_ref[...] = jnp.zeros_like(dg_ref)
        dg_ref[0:1, :] = jnp.sum(dmn * xh, axis=0, keepdims=True)

    vm = pl.BlockSpec(memory_space=pltpu.VMEM)
    return pl.pallas_call(
        body, name="mem_kv_bwd",
        out_shape=[jax.ShapeDtypeStruct(w_xk.shape, BF16), jax.ShapeDtypeStruct(w_xv.shape, BF16),
                   jax.ShapeDtypeStruct((SUBLANES, D_MODEL), F32)],
        in_specs=[vm] * 6, out_specs=[vm] * 3,
        compiler_params=pltpu.CompilerParams(vmem_limit_bytes=VMEM_LIMIT),
    )(dk, dv, mem_n, mem, w_xk, w_xv)


def _mix_out_bwd(dx1, w_out, attn, gb, gc, xi, w_sc, g_a, g_c, dep):
    s = dx1.shape[0]
    tb = TM // SUBLANES

    def body(dx1_ref, wout_ref, attn_ref, gb_ref, gc_ref, xi_ref, gch_ref, xih_ref, wsc_ref, ga_ref, gcv_ref, dep_ref,
             da1, da4, da16, dd1, dd4, dd16, dgb_ref, dcv_ref, dga_ref, dgc_ref, dwsc_ref, scr):
        i = pl.program_id(0)

        @pl.when(i == 0)
        def _():
            dga_ref[...] = jnp.zeros_like(dga_ref)
            dgc_ref[...] = jnp.zeros_like(dgc_ref)
            dwsc_ref[...] = jnp.zeros_like(dwsc_ref)

        dmixed = _dot_nt(dx1_ref[...].astype(BF16), wout_ref[...])
        da, dcn = dmixed[:, :ATTN_W], dmixed[:, ATTN_W:]
        attn = attn_ref[...]
        xa, ra = _rms(attn)
        dga_ref[0:1, :] += jnp.sum(da * xa, axis=0, keepdims=True)
        dattn = _rms_bwd(xa, ra, ga_ref[...], da)
        _spread(dattn, scr, (da1, da4, da16), BF16)
        prod = dattn * attn
        dd = jnp.concatenate(
            [jnp.broadcast_to(jnp.sum(prod[:, h * HEAD_DIM:(h + 1) * HEAD_DIM], axis=-1, keepdims=True),
                              (TM, HEAD_DIM)) for h in range(N_HEADS)], axis=1)
        _spread(dd, scr, (dd1, dd4, dd16), F32)
        gbv = gb_ref[...]
        u = gc_ref[...] * xi_ref[...]
        uh = jnp.where(i > 0, gch_ref[...] * xih_ref[...], 0.0)
        u2, u1 = _shift_down(u, uh, 2), _shift_down(u, uh, 1)
        cv = (u2 * wsc_ref[0:1, :] + u1 * wsc_ref[1:2, :]) + u * wsc_ref[2:3, :]
        xc, rc = _rms(gbv * cv)
        dgc_ref[0:1, :] += jnp.sum(dcn * xc, axis=0, keepdims=True)
        dconv = _rms_bwd(xc, rc, gcv_ref[...], dcn)
        dgb_ref[...] = (dconv * cv).astype(BF16)
        dcv = dconv * gbv
        dcv_ref[...] = dcv
        dwsc_ref[0:1, :] += jnp.sum(dcv * u2, axis=0, keepdims=True)
        dwsc_ref[1:2, :] += jnp.sum(dcv * u1, axis=0, keepdims=True)
        dwsc_ref[2:3, :] += jnp.sum(dcv * u, axis=0, keepdims=True)

    row = lambda n: pl.BlockSpec((TM, n), lambda i: (i, 0))
    halo = pl.BlockSpec((SUBLANES, 512), lambda i: (jnp.maximum(i * tb - 1, 0), 0))
    acc = _full((SUBLANES, 512))
    res = pl.pallas_call(
        body, name="mix_out_bwd", grid=(s // TM,),
        out_shape=_class_shapes(s, 512, BF16) + _class_shapes(s, 512, F32)
        + [jax.ShapeDtypeStruct((s, 512), BF16), jax.ShapeDtypeStruct((s, 512), F32)]
        + [jax.ShapeDtypeStruct((SUBLANES, 512), F32)] * 3,
        in_specs=[row(D_MODEL), _full(w_out.shape), row(512), row(512), row(512), row(512), halo, halo,
                  _full(w_sc.shape), _full(g_a.shape), _full(g_c.shape), ANY_SPEC],
        out_specs=_class_specs(512) * 2 + [row(512)] * 2 + [acc] * 3,
        scratch_shapes=[pltpu.VMEM((512 // LANES, TM, LANES), F32)],
        compiler_params=_cparams(1),
    )(dx1, w_out, attn, gb, gc, xi, gc, xi, w_sc, g_a, g_c, dep)
    return res[0:3], res[3:6], res[6], res[7], res[8], res[9], res[10]


def _swa_bwd(qc, kc, vc, doc, lsec, ddc, bias, dil, dep):
    nsub, nb, ncls = _swa_steps(qc, dil)
    n128 = nsub * nb
    whole = nb == 1

    def body(q_ref, qn_ref, kp_ref, kc_ref, vp_ref, vc_ref, do_ref, don_ref, lse_ref, lsen_ref, dd_ref, ddn_ref,
             b_ref, dep_ref, dq_ref, dk_ref, dv_ref, db_ref, s_scr, dp_scr, sn_scr, dpn_scr, ds_scr, p_scr, dsn_scr,
             pn_scr):
        r, b = pl.program_id(0), pl.program_id(1)

        @pl.when((r == 0) & (b == 0))
        def _():
            db_ref[...] = jnp.zeros_like(db_ref)

        pairs = [slice(a * LANES, (a + 1) * LANES) for a in range(N_HEADS // 2)]
        blk = [slice(t * WIN, (t + 1) * WIN) for t in range(nsub)]
        last = blk[nsub - 1]
        cols = lambda t: slice(WIN, 2 * WIN) if whole and t == 0 else slice(0, 2 * WIN)
        of_head = lambda ref, c, rows, h: ref[c, rows, h * HEAD_DIM:h * HEAD_DIM + 1]
        no_prev = (b == 0) & (lax.broadcasted_iota(jnp.int32, (WIN, 2 * WIN), 1) < WIN)

        def keys(prev_ref, cur_ref, c, t, sl):
            if whole and t == 0:
                return cur_ref[c, blk[0], sl]
            if t == 0:
                return jnp.concatenate([prev_ref[c, :, sl], cur_ref[c, blk[0], sl]], axis=0)
            return cur_ref[c, (t - 1) * WIN:(t + 1) * WIN, sl]

        for a, sl in enumerate(pairs):
            for c, t in [(c, t) for c in range(ncls) for t in range(nsub)]:
                k2, v2 = keys(kp_ref, kc_ref, c, t, sl), keys(vp_ref, vc_ref, c, t, sl)
                q_eo = _pair_split(q_ref[c, blk[t], sl])
                do_eo = _pair_split(do_ref[c, blk[t], sl].astype(BF16))
                for e in range(2):
                    s_scr[c * nsub + t, 2 * a + e, :, cols(t)] = _dot_nt(q_eo[e], k2)
                    dp_scr[c * nsub + t, 2 * a + e, :, cols(t)] = _dot_nt(do_eo[e], v2)
            if not whole:
                qn_eo = _pair_split(qn_ref[0, :, sl])
                don_eo = _pair_split(don_ref[0, :, sl].astype(BF16))
                for e in range(2):
                    sn_scr[2 * a + e] = _dot_nt(qn_eo[e], kc_ref[0, last, sl])
                    dpn_scr[2 * a + e] = _dot_nt(don_eo[e], vc_ref[0, last, sl])
        for c, t, h in [(c, t, h) for c in range(ncls) for t in range(nsub) for h in range(N_HEADS)]:
            i, cl = c * nsub + t, cols(t)
            lg = s_scr[i, h, :, cl] + b_ref[h, :, cl]
            if t == 0 and not whole:
                lg = jnp.where(no_prev, -jnp.inf, lg)
            p = jnp.exp(lg - of_head(lse_ref, c, blk[t], h))
            ds = p * (dp_scr[i, h, :, cl] - of_head(dd_ref, c, blk[t], h))
            db_ref[h, :, cl] += ds
            ds_scr[i, h, :, cl] = ds.astype(BF16)
            p_scr[i, h, :, cl] = p.astype(BF16)
        if not whole:
            every = slice(0, WIN)
            for h in range(N_HEADS):
                lgn = jnp.where(b + 1 < nb, sn_scr[h] + b_ref[h, :, :WIN], -jnp.inf)
                pn = jnp.exp(lgn - of_head(lsen_ref, 0, every, h))
                dsn_scr[h] = (pn * (dpn_scr[h] - of_head(ddn_ref, 0, every, h))).astype(BF16)
                pn_scr[h] = pn.astype(BF16)
        for a, sl in enumerate(pairs):
            for c in range(ncls):
                q_eo = [_pair_split(q_ref[c, blk[t], sl]) for t in range(nsub)]
                do_eo = [_pair_split(do_ref[c, blk[t], sl].astype(BF16)) for t in range(nsub)]
                if not whole:
                    q_eo.append(_pair_split(qn_ref[0, :, sl]))
                    do_eo.append(_pair_split(don_ref[0, :, sl].astype(BF16)))
                for t in range(nsub):
                    i = c * nsub + t
                    k_eo = _pair_split(keys(kp_ref, kc_ref, c, t, sl))
                    dq, dk, dv = None, None, None
                    for e in range(2):
                        h = 2 * a + e
                        terms = [_dot(ds_scr[i, h, :, cols(t)], k_eo[e]),
                                 _dot_tn(ds_scr[i, h, :, WIN:], q_eo[t][e]),
                                 _dot_tn(p_scr[i, h, :, WIN:], do_eo[t][e])]
                        if t + 1 < nsub or not whole:
                            ds_next = ds_scr[i + 1, h, :, :WIN] if t + 1 < nsub else dsn_scr[h]
                            p_next = p_scr[i + 1, h, :, :WIN] if t + 1 < nsub else pn_scr[h]
                            terms[1] += _dot_tn(ds_next, q_eo[t + 1][e])
                            terms[2] += _dot_tn(p_next, do_eo[t + 1][e])
                        dq, dk, dv = terms if e == 0 else (dq + terms[0], dk + terms[1], dv + terms[2])
                    dq_ref[c, blk[t], sl] = dq.astype(BF16)
                    dk_ref[c, blk[t], sl] = dk.astype(BF16)
                    dv_ref[c, blk[t], sl] = dv.astype(BF16)

    cur = pl.BlockSpec((ncls, nsub * WIN, 512), lambda r, b: (r, b, 0))
    prev = pl.BlockSpec((ncls, WIN, 512), lambda r, b: (r, jnp.maximum(nsub * b - 1, 0), 0))
    nxt = pl.BlockSpec((ncls, WIN, 512), lambda r, b: (r, jnp.minimum(nsub * b + nsub, n128 - 1), 0))
    wide, narrow = (ncls * nsub, N_HEADS, WIN, 2 * WIN), (N_HEADS, WIN, WIN)
    return pl.pallas_call(
        body, name=f"swa_bwd_d{dil}", grid=(dil // ncls, nb),
        out_shape=[jax.ShapeDtypeStruct(qc.shape, BF16)] * 3 + [jax.ShapeDtypeStruct(bias.shape, F32)],
        in_specs=[cur, nxt, prev, cur, prev, cur, cur, nxt, cur, nxt, cur, nxt, _full(bias.shape), ANY_SPEC],
        out_specs=[cur] * 3 + [_full(bias.shape)],
        scratch_shapes=[pltpu.VMEM(wide, F32), pltpu.VMEM(wide, F32), pltpu.VMEM(narrow, F32),
                        pltpu.VMEM(narrow, F32), pltpu.VMEM(wide, BF16), pltpu.VMEM(wide, BF16),
                        pltpu.VMEM(narrow, BF16), pltpu.VMEM(narrow, BF16)],
        compiler_params=_cparams(2),
    )(qc, qc, kc, kc, vc, vc, doc, doc, lsec, lsec, ddc, ddc, bias, dep)


def _in_proj_bwd(dqs, dks, dvs, dgb, dcv, gc, xi, w_sc, w_in_g, x, g_mix, dx1):
    s = x.shape[0]
    tb = TM // SUBLANES
    last = s // SUBLANES - 1
    n_tiles = s // TM

    def body(dq1, dq4, dq16, dk1, dk4, dk16, dv1, dv4, dv16, dgb_ref, dcv_ref, dcvn_ref, gc_ref, xi_ref, wsc_ref,
             w_hbm, x_ref, g_ref, dx1_ref, dproj_ref, gx_ref, dg_ref, scr_a, scr_b, w_scr, w_sems):
        i = pl.program_id(0)
        _load_w_in_pairs(w_hbm, w_scr, w_sems)

        @pl.when(i == 0)
        def _():
            dg_ref[...] = jnp.zeros_like(dg_ref)

        d0 = dcv_ref[...]
        dn = jnp.where(i < n_tiles - 1, dcvn_ref[...], 0.0)
        du = (d0 * wsc_ref[2:3, :] + _shift_up(d0, dn, 1) * wsc_ref[1:2, :]) + _shift_up(d0, dn, 2) * wsc_ref[0:1, :]
        merge = lambda a, b4, b16: ((a[...].astype(F32) + _gather_classes(b4, scr_a, 4))
                                    + _gather_classes(b16, scr_b, 16))
        dq = merge(dq1, dq4, dq16) * (HEAD_DIM ** -0.5)
        dk = merge(dk1, dk4, dk16)
        dv = merge(dv1, dv4, dv16)
        dproj = jnp.concatenate([dq, dk, dv, dgb_ref[...].astype(F32), du * xi_ref[...], du * gc_ref[...]],
                                axis=1).astype(BF16)
        dproj_ref[...] = dproj
        dh = jnp.zeros((TM, D_MODEL), F32)
        for j in range(N_DEV // 2):
            dh = dh + _dot_nt(dproj[:, 2 * j * IN_CHUNK:2 * (j + 1) * IN_CHUNK], w_scr[j])
        xh, r = _rms(x_ref[...])
        dg_ref[0:1, :] += jnp.sum(dh * xh, axis=0, keepdims=True)
        gx_ref[...] = dx1_ref[...] + _rms_bwd(xh, r, g_ref[...], dh)

    row = lambda n: pl.BlockSpec((TM, n), lambda i: (i, 0))
    nxt = pl.BlockSpec((SUBLANES, 512), lambda i: (jnp.minimum((i + 1) * tb, last), 0))
    return pl.pallas_call(
        body, name="in_proj_bwd", grid=(n_tiles,),
        out_shape=[jax.ShapeDtypeStruct((s, IN_COLS), BF16), jax.ShapeDtypeStruct((s, D_MODEL), F32),
                   jax.ShapeDtypeStruct((SUBLANES, D_MODEL), F32)],
        in_specs=_class_specs(512) * 3 + [row(512), row(512), nxt, row(512), row(512), _full(w_sc.shape),
                                          ANY_SPEC, row(D_MODEL), _full(g_mix.shape), row(D_MODEL)],
        out_specs=[row(IN_COLS), row(D_MODEL), _full((SUBLANES, D_MODEL))],
        scratch_shapes=[pltpu.VMEM((512 // LANES, TM, LANES), F32)] * 2 + W_IN_PAIRS,
        compiler_params=_cparams(1),
    )(*dqs, *dks, *dvs, dgb, dcv, dcv, gc, xi, w_sc, w_in_g, x, g_mix, dx1)


def _dw(a, b, dep, name, a_chunked=False, b_chunked=False, n_chunks=1, chunk_cols=None):
    single = not (a_chunked or b_chunked or chunk_cols)
    wide = a_chunked and a.shape[2] > D_MODEL
    ts = TS_DW // 2 if single or wide else TS_DW
    if a_chunked:
        nj, s, kk = a.shape
        nn = b.shape[1]
        a_spec = pl.BlockSpec((1, ts, kk), lambda j, t: (j, t, 0))
        b_spec = pl.BlockSpec((ts, nn), lambda j, t: (t, 0))
    elif b_chunked:
        nj, s, nn = b.shape
        kk = a.shape[1]
        a_spec = pl.BlockSpec((ts, kk), lambda j, t: (t, 0))
        b_spec = pl.BlockSpec((1, ts, nn), lambda j, t: (j, t, 0))
    else:
        s, kk = a.shape
        nj, nn = (n_chunks, chunk_cols) if chunk_cols else (1, b.shape[1])
        a_spec = pl.BlockSpec((ts, kk), lambda j, t: (t, 0))
        b_spec = pl.BlockSpec((ts, nn), lambda j, t: (t, j))
    n_steps = s // ts

    def body(a_ref, b_ref, dep_ref, o_ref, acc):
        t = pl.program_id(1)

        @pl.when(t == 0)
        def _():
            acc[...] = jnp.zeros_like(acc)

        av = (a_ref[0] if a_chunked else a_ref[...]).astype(BF16)
        bv = (b_ref[0] if b_chunked else b_ref[...]).astype(BF16)
        acc[...] += _dot_tn(av, bv)

        @pl.when(t == n_steps - 1)
        def _():
            o_ref[0] = acc[...].astype(BF16)

    return pl.pallas_call(
        body, name=name, grid=(nj, n_steps),
        out_shape=jax.ShapeDtypeStruct((nj, kk, nn), BF16),
        in_specs=[a_spec, b_spec, ANY_SPEC],
        out_specs=pl.BlockSpec((1, kk, nn), lambda j, t: (j, 0, 0)),
        scratch_shapes=[pltpu.VMEM((kk, nn), F32)],
        compiler_params=_cparams(2),
    )(a, b, dep)


def _adamw_math(w, g, m, v):
    m2 = ADAM_B1 * m + (1.0 - ADAM_B1) * g
    v2 = ADAM_B2 * v + (1.0 - ADAM_B2) * (g * g)
    m_hat = m2 / (1.0 - ADAM_B1 ** ADAM_STEP)
    v_hat = v2 / (1.0 - ADAM_B2 ** ADAM_STEP)
    delta = -ADAM_LR * (m_hat / (jnp.sqrt(v_hat) + ADAM_EPS) + ADAM_WD * w)
    return delta, m2, v2


def _sum_parts(me, own, p_ref):
    g = None
    for i in range(N_DEV):
        part = jnp.where(me == i, own.astype(F32), p_ref[i].astype(F32))
        g = part if g is None else g + part
    return g


def _adamw_big(name, w, sent, parts, m, v, me_arr):
    rr, cc = w.shape
    tr = rr // 4 if rr >= 512 else rr

    def body(me_ref, w_ref, own_ref, p_ref, m_ref, v_ref, g_ref, d_ref, nm_ref, nv_ref):
        g = own_ref[0].astype(F32)
        for k in range(1, N_DEV):
            g = g + p_ref[(me_ref[0] + k) % N_DEV].astype(F32)
        g_ref[...] = g
        d_ref[...], nm_ref[...], nv_ref[...] = _adamw_math(w_ref[...], g, m_ref[...], v_ref[...])

    row = pl.BlockSpec((tr, cc), lambda i, me: (i, 0))
    return pl.pallas_call(
        body, name=name,
        grid_spec=pltpu.PrefetchScalarGridSpec(
            num_scalar_prefetch=1, grid=(rr // tr,),
            in_specs=[row, pl.BlockSpec((1, tr, cc), lambda i, me: (me[0], i, 0)),
                      pl.BlockSpec((N_DEV, tr, cc), lambda i, me: (0, i, 0)), row, row],
            out_specs=[row] * 4),
        out_shape=[jax.ShapeDtypeStruct((rr, cc), F32)] * 4,
        compiler_params=_cparams(1),
    )(me_arr, w, sent, parts, m, v)


def _small_slices():
    return [
        (slice(ROW_RELB, ROW_RELB + 8), slice(0, N_BUCKETS)),
        (slice(ROW_GMIX, ROW_GMIX + 1), slice(0, D_MODEL)),
        (slice(ROW_GAC, ROW_GAC + 1), slice(0, ATTN_W)),
        (slice(ROW_GAC, ROW_GAC + 1), slice(ATTN_W, D_MODEL)),
        (slice(ROW_GXATTN, ROW_GXATTN + 1), slice(0, D_MODEL)),
        (slice(ROW_GMEM, ROW_GMEM + 1), slice(0, D_MODEL)),
        (slice(ROW_GFFN, ROW_GFFN + 1), slice(0, D_MODEL)),
        (slice(ROW_BFC, ROW_BFC + 8), slice(0, UP_CHUNK)),
        (slice(ROW_GFINAL, ROW_GFINAL + 1), slice(0, D_MODEL)),
    ]


def _adamw_small(own, parts, wmv, me_arr):
    slices = _small_slices()
    n = len(slices)

    def body(*refs):
        me_ref, own_ref, p_ref = refs[:3]
        ins = refs[3:3 + 3 * n]
        g_ref = refs[3 + 3 * n]
        outs = refs[4 + 3 * n:]
        g = _sum_parts(me_ref[0], own_ref[...], p_ref)
        g_ref[...] = g
        for a, (rs, ls) in enumerate(slices):
            ga = g[rs, ls]
            outs[4 * a][...] = ga
            outs[4 * a + 1][...], outs[4 * a + 2][...], outs[4 * a + 3][...] = _adamw_math(
                ins[3 * a][...], ga, ins[3 * a + 1][...], ins[3 * a + 2][...])

    vm = pl.BlockSpec(memory_space=pltpu.VMEM)
    flat = [t for trip in wmv for t in trip]
    out_shape = [jax.ShapeDtypeStruct((SMALL_ROWS, D_MODEL), F32)]
    for w, _, _ in wmv:
        out_shape += [jax.ShapeDtypeStruct(w.shape, F32)] * 4
    res = pl.pallas_call(
        body, name="adamw_small", out_shape=out_shape,
        in_specs=[SMEM_SPEC] + [vm] * (2 + 3 * n), out_specs=[vm] * len(out_shape),
    )(me_arr, own, parts, *flat)
    return res[0], [res[1 + 4 * a:5 + 4 * a] for a in range(n)]


def _adamw_shards(items):
    n = len(items)

    def body(*refs):
        for a in range(n):
            w_ref, g_ref, m_ref, v_ref = refs[4 * a:4 * a + 4]
            d_ref, nm_ref, nv_ref = refs[4 * n + 3 * a:4 * n + 3 * a + 3]
            d_ref[...], nm_ref[...], nv_ref[...] = _adamw_math(w_ref[...], g_ref[...], m_ref[...], v_ref[...])

    vm = pl.BlockSpec(memory_space=pltpu.VMEM)
    out_shape = []
    for w, _, _, _ in items:
        out_shape += [jax.ShapeDtypeStruct(w.shape, F32)] * 3
    res = pl.pallas_call(
        body, name="adamw_shards", out_shape=out_shape, in_specs=[vm] * (4 * n), out_specs=[vm] * (3 * n),
    )(*[t for it in items for t in it])
    return [res[3 * a:3 * a + 3] for a in range(n)]


def _mesh_pos():
    return lax.axis_index("x"), lax.axis_index("y"), lax.axis_index("c")


def _dev_index(p):
    return 4 * p[0] + 2 * p[1] + p[2]


def _all_gather(shards):
    n = len(shards)

    def body(*refs):
        ins, outs = refs[:n], refs[n:2 * n]
        send_sems, recv_sems, loc_sems = refs[2 * n:]
        x, y, c = _mesh_pos()
        me, sib = (x, y, c), (x, y, 1 - c)
        chips = [(1 - x, y), (x, 1 - y), (1 - x, 1 - y)]

        def cp(a, k, block, to, src=None):
            dst = outs[a].at[_dev_index(block)]
            return pltpu.make_async_remote_copy(
                src_ref=dst if src is None else src, dst_ref=dst, send_sem=send_sems.at[a, k],
                recv_sem=recv_sems.at[a, k], device_id=to, device_id_type=MESH)

        mine = [pltpu.make_async_copy(ins[a], outs[a].at[_dev_index(me)], loc_sems.at[a]) for a in range(n)]
        for m_ in mine:
            m_.start()
        first = []
        for a in range(n):
            first.append(cp(a, 0, me, sib, src=ins[a]))
            first += [cp(a, 1 + j, me, (*chip, c), src=ins[a]) for j, chip in enumerate(chips)]
        for f in first:
            f.start()
        passed = []
        for a in range(n):
            for j, chip in enumerate(chips):
                cp(a, 1 + j, (*chip, c), me).wait_recv()
                fwd = cp(a, 4 + j, (*chip, c), sib)
                fwd.start()
                passed.append(fwd)
        for a in range(n):
            cp(a, 0, sib, me).wait_recv()
            for j, chip in enumerate(chips):
                cp(a, 4 + j, (*chip, 1 - c), me).wait_recv()
        for f in first + passed:
            f.wait_send()
        for m_ in mine:
            m_.wait()

    hbm = pl.BlockSpec(memory_space=pltpu.HBM)
    return pl.pallas_call(
        body, name="all_gather_weights",
        out_shape=[jax.ShapeDtypeStruct((N_DEV,) + a.shape, a.dtype) for a in shards],
        in_specs=[hbm] * n, out_specs=[hbm] * n,
        scratch_shapes=[pltpu.SemaphoreType.DMA((n, 7)), pltpu.SemaphoreType.DMA((n, 7)),
                        pltpu.SemaphoreType.DMA((n,))],
    )(*shards)


def _peers():
    x, y, c = _mesh_pos()
    return (x, y, c), [((1 - x) if k & 4 else x, (1 - y) if k & 2 else y, (1 - c) if k & 1 else c)
                       for k in range(1, 8)]


def _exchange_copy(src_ref, land_ref, whole, send_sems, recv_sems, a, k, peer, slot):
    src = src_ref if whole else src_ref.at[_dev_index(peer)]
    return pltpu.make_async_remote_copy(
        src_ref=src, dst_ref=land_ref.at[slot], send_sem=send_sems.at[7 * a + k], recv_sem=recv_sems.at[7 * a + k],
        device_id=peer, device_id_type=MESH)


def _exchange_start(name, srcs, whole, dep):
    n = len(srcs)
    lands = [lax.empty(((N_DEV,) + s.shape) if w else s.shape, s.dtype) for s, w in zip(srcs, whole)]

    def body(*refs):
        src_refs, land_refs = refs[:n], refs[n:2 * n]
        send_sems, recv_sems, token = refs[2 * n + 1], refs[2 * n + 2], refs[-1]
        me, peers = _peers()
        for a in range(n):
            for k, peer in enumerate(peers):
                _exchange_copy(src_refs[a], land_refs[a], whole[a], send_sems, recv_sems, a, k, peer,
                               _dev_index(me)).start()
        token[...] = jnp.zeros_like(token)

    res = pl.pallas_call(
        body, name=name,
        out_shape=(pltpu.SemaphoreType.DMA((7 * n,)), pltpu.SemaphoreType.DMA((7 * n,)),
                   *[pltpu.HBM(a.shape, a.dtype) for a in srcs], *[pltpu.HBM(a.shape, a.dtype) for a in lands],
                   jax.ShapeDtypeStruct((SUBLANES, 128), F32)),
        in_specs=[HBM_SPEC] * (2 * n) + [ANY_SPEC],
        out_specs=(SEM_SPEC, SEM_SPEC, *([HBM_SPEC] * (2 * n)), VMEM_SPEC),
        input_output_aliases={i: 2 + i for i in range(2 * n)},
        compiler_params=pltpu.CompilerParams(has_side_effects=DATAFLOW),
    )(*[pltpu.with_memory_space_constraint(a, pltpu.HBM) for a in srcs],
      *[pltpu.with_memory_space_constraint(a, pltpu.HBM) for a in lands], dep)
    return res[0], res[1], list(res[2:2 + n]), list(res[2 + n:2 + 2 * n]), res[-1]


def _exchange_wait(name, started, whole, after, which=None):
    send_sems, recv_sems, srcs, lands, _ = started
    which = list(range(len(srcs))) if which is None else which
    srcs, lands = [srcs[a] for a in which], [lands[a] for a in which]
    n = len(srcs)

    def body(*refs):
        src_refs, land_refs = refs[:n], refs[n:2 * n]
        send_sems, recv_sems = refs[2 * n], refs[2 * n + 1]
        _, peers = _peers()
        for i, a in enumerate(which):
            for k, peer in enumerate(peers):
                cp = _exchange_copy(src_refs[i], land_refs[i], whole[a], send_sems, recv_sems, a, k, peer,
                                    _dev_index(peer))
                cp.wait_send()
                cp.wait_recv()

    res = pl.pallas_call(
        body, name=name,
        out_shape=[pltpu.HBM(a.shape, a.dtype) for a in srcs + lands],
        in_specs=[HBM_SPEC] * (2 * n) + [SEM_SPEC, SEM_SPEC, ANY_SPEC],
        out_specs=[HBM_SPEC] * (2 * n),
        input_output_aliases={i: i for i in range(2 * n)},
        compiler_params=pltpu.CompilerParams(has_side_effects=DATAFLOW),
    )(*srcs, *lands, send_sems, recv_sems, after)
    return list(res[:n]), list(res[n:])


def _gather_start(name, shards, dep):
    n = len(shards)
    lands = [lax.empty((N_DEV,) + a.shape, a.dtype) for a in shards]

    def body(*refs):
        src_refs, land_refs = refs[:n], refs[n:2 * n]
        send_sems, recv_sems, token = refs[2 * n + 1], refs[2 * n + 2], refs[-1]
        x, y, c = _mesh_pos()
        peers = [(x, y, 1 - c), (1 - x, y, c), (x, 1 - y, c), (1 - x, 1 - y, c)]
        for a in range(n):
            for k, peer in enumerate(peers):
                pltpu.make_async_remote_copy(
                    src_ref=src_refs[a], dst_ref=land_refs[a].at[_dev_index((x, y, c))], send_sem=send_sems.at[4 * a + k],
                    recv_sem=recv_sems.at[4 * a + k], device_id=peer, device_id_type=MESH).start()
        token[...] = jnp.zeros_like(token)

    res = pl.pallas_call(
        body, name=name,
        out_shape=(pltpu.SemaphoreType.DMA((4 * n,)), pltpu.SemaphoreType.DMA((4 * n,)),
                   *[pltpu.HBM(a.shape, a.dtype) for a in shards], *[pltpu.HBM(a.shape, a.dtype) for a in lands],
                   jax.ShapeDtypeStruct((SUBLANES, 128), F32)),
        in_specs=[HBM_SPEC] * (2 * n) + [ANY_SPEC],
        out_specs=(SEM_SPEC, SEM_SPEC, *([HBM_SPEC] * (2 * n)), VMEM_SPEC),
        input_output_aliases={i: 2 + i for i in range(2 * n)},
        compiler_params=pltpu.CompilerParams(has_side_effects=DATAFLOW),
    )(*[pltpu.with_memory_space_constraint(a, pltpu.HBM) for a in shards],
      *[pltpu.with_memory_space_constraint(a, pltpu.HBM) for a in lands], dep)
    return res[0], res[1], list(res[2:2 + n]), list(res[2 + n:2 + 2 * n]), res[-1]


def _gather_forward(name, send_sems, recv_sems, lands, which, after):
    n = len(which)

    def body(*refs):
        land_refs = refs[:n]
        send_sems, recv_sems = refs[n], refs[n + 1]
        fsend, frecv, token = refs[n + 3], refs[n + 4], refs[-1]
        x, y, c = _mesh_pos()
        chips = [(1 - x, y), (x, 1 - y), (1 - x, 1 - y)]
        for i, a in enumerate(which):
            for j, chip in enumerate(chips):
                block = land_refs[i].at[_dev_index((*chip, c))]
                pltpu.make_async_remote_copy(
                    src_ref=block, dst_ref=block, send_sem=send_sems.at[4 * a + 1 + j], recv_sem=recv_sems.at[4 * a + 1 + j],
                    device_id=(*chip, c), device_id_type=MESH).wait_recv()
                pltpu.make_async_remote_copy(
                    src_ref=block, dst_ref=block, send_sem=fsend.at[3 * i + j], recv_sem=frecv.at[3 * i + j],
                    device_id=(x, y, 1 - c), device_id_type=MESH).start()
        token[...] = jnp.zeros_like(token)

    res = pl.pallas_call(
        body, name=name,
        out_shape=(pltpu.SemaphoreType.DMA((3 * n,)), pltpu.SemaphoreType.DMA((3 * n,)),
                   *[pltpu.HBM(a.shape, a.dtype) for a in lands], jax.ShapeDtypeStruct((SUBLANES, 128), F32)),
        in_specs=[HBM_SPEC] * n + [SEM_SPEC, SEM_SPEC, ANY_SPEC],
        out_specs=(SEM_SPEC, SEM_SPEC, *([HBM_SPEC] * n), VMEM_SPEC),
        input_output_aliases={i: 2 + i for i in range(n)},
        compiler_params=pltpu.CompilerParams(has_side_effects=DATAFLOW),
    )(*lands, send_sems, recv_sems, after)
    return res[0], res[1], list(res[2:2 + n]), res[-1]


def _gather_wait(name, send_sems, recv_sems, fsend, frecv, srcs, lands, which, after):
    n = len(which)

    def body(*refs):
        land_refs = refs[n:2 * n]
        send_sems, recv_sems, fsend, frecv = refs[2 * n:2 * n + 4]
        x, y, c = _mesh_pos()
        sib = (x, y, 1 - c)
        chips = [(1 - x, y), (x, 1 - y), (1 - x, 1 - y)]
        for i, a in enumerate(which):
            def cp(slot, ssem, rsem):
                block = land_refs[i].at[_dev_index(slot)]
                return pltpu.make_async_remote_copy(src_ref=block, dst_ref=block, send_sem=ssem, recv_sem=rsem,
                                                    device_id=sib, device_id_type=MESH)
            cp(sib, send_sems.at[4 * a], recv_sems.at[4 * a]).wait_recv()
            for j, chip in enumerate(chips):
                cp((*chip, 1 - c), fsend.at[3 * i + j], frecv.at[3 * i + j]).wait_recv()
            for k in range(4):
                cp(sib, send_sems.at[4 * a + k], recv_sems.at[4 * a + k]).wait_send()
            for j in range(3):
                cp(sib, fsend.at[3 * i + j], frecv.at[3 * i + j]).wait_send()

    res = pl.pallas_call(
        body, name=name,
        out_shape=[pltpu.HBM(a.shape, a.dtype) for a in srcs + lands],
        in_specs=[HBM_SPEC] * (2 * n) + [SEM_SPEC] * 4 + [ANY_SPEC],
        out_specs=[HBM_SPEC] * (2 * n),
        input_output_aliases={i: i for i in range(2 * n)},
        compiler_params=pltpu.CompilerParams(has_side_effects=DATAFLOW),
    )(*srcs, *lands, send_sems, recv_sems, fsend, frecv, after)
    return list(res[n:])


def _local_step(x, mem, target, rel_bias, g_mix, w_in_g, w_sc, g_a, g_c, g_xattn, g_mem, g_ffn, w_fc, b_fc, g_final,
                dep, forward_weights, late_weights, emit, emit_small):
    s = x.shape[0]
    buckets = _bucket_tables()
    bias = _bias_fwd(rel_bias, buckets)

    h1, qs, ks, vs, gb, gc, xi = _rms_proj(x, g_mix, w_in_g, dep)
    qs, ks, vs = ([a[0][None]] + list(a[1:]) for a in (qs, ks, vs))
    group1, group2 = ["w_out", "w_xq", "w_xk", "w_xv", "w_xo"], ["w_up", "w_down"]
    tok = forward_weights(group1, h1)
    branches = []
    for p, dil in enumerate(DILATIONS):
        o_p, lse_p = _swa_fwd(qs[p], ks[p], vs[p], bias[p], dil, tok)
        branches.append([o_p[0], lse_p[0]] if dil == 1 else [o_p, lse_p])
    lw = late_weights(group1, branches[-1][0])
    w_out, w_xq, w_xk, w_xv, w_xo = (lw[n] for n in group1)
    attn, lses, mixed, x1 = _mix_out(branches, gb, gc, xi, x, w_sc, g_a, g_c, w_out)
    tok = forward_weights(group2, x1)
    mem_n, mk, mv = _mem_kv(mem, g_mem, w_xk, w_xv)
    h2, xq, xo, x2 = _xattn_fwd(x1, g_xattn, w_xq, mk, mv, w_xo, tok)
    lw = late_weights(group2, x2)
    w_up_g = lw["w_up"].reshape(FFN_CHUNKS, FFN_WIDTH, D_MODEL)
    w_down_g = lw["w_down"].reshape(FFN_CHUNKS // 2, FFN_WIDTH, D_MODEL)
    pairs = lambda a: a.reshape(FFN_CHUNKS, 2, a.shape[1], UP_CHUNK).transpose(0, 2, 1, 3).reshape(
        FFN_CHUNKS, a.shape[1], FFN_WIDTH)
    w_fc, b_fc = pairs(w_fc), pairs(b_fc)
    h3, up, conv, act, dx3, loss_acc, dg_final = _ffn_fwd(x2, g_ffn, w_up_g, w_fc, b_fc, w_down_g, g_final, target)

    gw_down = _dw(act, dx3, dep, "dw_down", a_chunked=True).reshape(N_DEV // 2, UP_CHUNK, D_MODEL)
    dup, dx2, dg_ffn, dw_fc, db_fc = _ffn_bwd(dx3, up, conv, x2, g_ffn, w_up_g, w_fc, w_down_g)
    gw_up = _dw(dup, h3, dep, "dw_up", a_chunked=True).reshape(N_DEV, UP_CHUNK, D_MODEL)
    tok = emit(dict(w_down=gw_down, w_up=gw_up))
    dxq, dx1, dmk, dmv, dg_xattn = _xattn_bwd(dx2, xo, xq, mk, mv, w_xo, w_xq, x1, g_xattn, tok)
    gw_xo = _dw(xo, dx2, tok, "dw_xo")[0]
    gw_xq = _dw(h2, dxq, tok, "dw_xq")[0]
    gw_xk, gw_xv, dg_mem = _mem_kv_bwd(dmk, dmv, mem_n, mem, w_xk, w_xv)
    tok = emit(dict(w_xo=gw_xo, w_xq=gw_xq, w_xk=gw_xk, w_xv=gw_xv))
    dattns, dds, dgb, dcv, dg_a, dg_c, dw_sc = _mix_out_bwd(dx1, w_out, attn, gb, gc, xi, w_sc, g_a, g_c, tok)
    first = lambda a: [a[0][None]] + list(a[1:])
    dattns, dds, lses = first(dattns), first(dds), first(lses)
    gw_out = _dw(mixed, dx1, tok, "dw_out")[0]
    tok = emit(dict(w_out=gw_out))
    dqs, dks, dvs, dbias = [], [], [], []
    for p, dil in enumerate(DILATIONS):
        dq_p, dk_p, dv_p, db_p = _swa_bwd(qs[p], ks[p], vs[p], dattns[p], lses[p], dds[p], bias[p], dil, tok)
        dqs.append(dq_p[0] if dil == 1 else dq_p)
        dks.append(dk_p[0] if dil == 1 else dk_p)
        dvs.append(dv_p[0] if dil == 1 else dv_p)
        dbias.append(db_p)
    d_relb = _bias_bwd(jnp.stack(dbias), buckets)
    dproj, grad_x, dg_mix = _in_proj_bwd(dqs, dks, dvs, dgb, dcv, gc, xi, w_sc, w_in_g, x, g_mix, dx1)
    pad = lambda a: jnp.pad(a, ((0, 0), (0, D_MODEL - a.shape[1])))
    small = jnp.concatenate([
        d_relb, dg_mix, dg_xattn, dg_mem, dg_ffn, dg_final, jnp.concatenate([dg_a, dg_c], axis=1),
        pad(dw_sc), pad(db_fc.reshape(N_DEV, UP_CHUNK)), pad(dw_fc.reshape(3 * N_DEV, UP_CHUNK)), pad(loss_acc)],
        axis=0)
    tok = emit_small(small)
    gw_in = _dw(h1, dproj, tok, "dw_in", n_chunks=N_DEV, chunk_cols=IN_CHUNK)
    emit(dict(w_in=gw_in))
    return grad_x


def kernel(x, mem, rel_bias, g_mix, w_in, w_short_conv, g_attn_out, g_conv_out, w_out, g_xattn, g_mem, w_xq, w_xk, w_xv, w_xo, g_ffn, w_up, w_ffn_conv, b_ffn_conv, w_down, g_final, loss_target, m_rel_bias, m_g_mix, m_w_in, m_w_short_conv, m_g_attn_out, m_g_conv_out, m_w_out, m_g_xattn, m_g_mem, m_w_xq, m_w_xk, m_w_xv, m_w_xo, m_g_ffn, m_w_up, m_w_ffn_conv, m_b_ffn_conv, m_w_down, m_g_final, v_rel_bias, v_g_mix, v_w_in, v_w_short_conv, v_g_attn_out, v_g_conv_out, v_w_out, v_g_xattn, v_g_mem, v_w_xq, v_w_xk, v_w_xv, v_w_xo, v_g_ffn, v_w_up, v_w_ffn_conv, v_b_ffn_conv, v_w_down, v_g_final):
    me = _dev_index(_mesh_pos())
    me_arr = me.reshape(1).astype(jnp.int32)

    big_names = ["w_in", "w_out", "w_xq", "w_xk", "w_xv", "w_xo", "w_up", "w_down"]
    late_names = big_names[1:]
    big_w = dict(w_in=w_in[0], w_out=w_out[0], w_xq=w_xq[0], w_xk=w_xk[0], w_xv=w_xv[0], w_xo=w_xo[0],
                 w_up=w_up[0].T, w_down=w_down[0])
    big_m = dict(w_in=m_w_in[0], w_out=m_w_out[0], w_xq=m_w_xq[0], w_xk=m_w_xk[0], w_xv=m_w_xv[0], w_xo=m_w_xo[0],
                 w_up=m_w_up[0].T, w_down=m_w_down[0])
    big_v = dict(w_in=v_w_in[0], w_out=v_w_out[0], w_xq=v_w_xq[0], w_xk=v_w_xk[0], w_xv=v_w_xv[0], w_xo=v_w_xo[0],
                 w_up=v_w_up[0].T, w_down=v_w_down[0])
    shard_shape = {n: big_w[n].shape for n in big_names}

    w_in_g, w_sc_g, w_fc_full = _all_gather([big_w["w_in"].astype(BF16), w_short_conv[0], w_ffn_conv[0]])
    w_sc_full = w_sc_g.transpose(1, 0, 2).reshape(3, CONV_W)
    late_shards = [big_w[n].astype(BF16) for n in late_names]
    ag_send, ag_recv, ag_srcs, ag_lands, ag_token = _gather_start("gather_weights_start", late_shards, w_in_g)
    forwarded = {}

    def forward_weights(names, after):
        which = [late_names.index(n) for n in names]
        fsend, frecv, lands, token = _gather_forward("gather_" + "_".join(names) + "_forward", ag_send, ag_recv,
                                                     [ag_lands[a] for a in which], which, after)
        forwarded[tuple(names)] = (fsend, frecv, lands)
        return token

    def late_weights(names, after):
        which = [late_names.index(n) for n in names]
        fsend, frecv, lands = forwarded[tuple(names)]
        lands = _gather_wait("gather_" + "_".join(names) + "_wait", ag_send, ag_recv, fsend, frecv,
                             [ag_srcs[a] for a in which], lands, which, after)
        out = {}
        for n, a, land in zip(names, which, lands):
            full = lax.dynamic_update_index_in_dim(land, late_shards[a], me, 0)
            if n == "w_up":
                out[n] = full
            elif n == "w_down":
                out[n] = full.reshape(N_DEV // 2, UP_CHUNK, D_MODEL)
            else:
                out[n] = full.reshape(D_MODEL, D_MODEL)
        return out

    sent = []

    def emit(grads):
        names = list(grads)
        blocks = [grads[n].reshape((N_DEV,) + shard_shape[n]) for n in names]
        started = _exchange_start("scatter_" + "_".join(names) + "_start", blocks, [False] * len(names), me_arr)
        sent.append((names, started))
        return started[-1]

    def emit_small(small):
        sent_small.append((small, _exchange_start("gather_small_start", [small], [True], me_arr)))
        return sent_small[0][1][-1]

    sent_small = []
    grad_x = _local_step(
        x[0], mem[0], loss_target[0], rel_bias, g_mix, w_in_g, w_sc_full, g_attn_out, g_conv_out, g_xattn, g_mem,
        g_ffn, w_fc_full, b_ffn_conv.reshape(N_DEV, 1, UP_CHUNK), g_final.reshape(1, D_MODEL), ag_token,
        forward_weights, late_weights, emit, emit_small)

    small_g, small_started = sent_small[0]
    after = sent[-1][1][-1]
    small_parts = _exchange_wait("gather_small_wait", small_started, [True], after)[1][0]
    big_out = {}
    after = small_parts
    for names, started in sent:
        blocks, lands = _exchange_wait("scatter_" + "_".join(names) + "_wait", started, [False] * len(names), after)
        for n, block, land in zip(names, blocks, lands):
            res = _adamw_big("adamw_" + n, big_w[n], block, land, big_m[n], big_v[n], me_arr)
            big_out[n] = [(r.T if n == "w_up" else r)[None] for r in res]
            after = res[0]

    as_rows = lambda a: a.reshape(N_DEV, UP_CHUNK)
    row1 = lambda a: a.reshape(1, D_MODEL)
    small_names = ["rel_bias", "g_mix", "g_attn_out", "g_conv_out", "g_xattn", "g_mem", "g_ffn", "b_ffn_conv", "g_final"]
    wmv = [
        (rel_bias, m_rel_bias, v_rel_bias), (g_mix, m_g_mix, v_g_mix), (g_attn_out, m_g_attn_out, v_g_attn_out),
        (g_conv_out, m_g_conv_out, v_g_conv_out), (g_xattn, m_g_xattn, v_g_xattn), (g_mem, m_g_mem, v_g_mem),
        (g_ffn, m_g_ffn, v_g_ffn), (as_rows(b_ffn_conv), as_rows(m_b_ffn_conv), as_rows(v_b_ffn_conv)),
        (row1(g_final), row1(m_g_final), row1(v_g_final))]
    g_packed, small_res = _adamw_small(small_g, small_parts, wmv, me_arr)
    small_out = dict(zip(small_names, small_res))
    loss = g_packed[ROW_LOSS, 0]
    small_out["b_ffn_conv"] = [a.reshape(1, 2 * D_FF) for a in small_out["b_ffn_conv"]]
    small_out["g_final"] = [a.reshape(D_MODEL) for a in small_out["g_final"]]

    g_wsc = lax.dynamic_slice(g_packed[ROW_WSC:ROW_WSC + 3, 0:CONV_W], (0, me * HEAD_DIM), (3, HEAD_DIM))
    g_wfc = lax.dynamic_slice(g_packed[ROW_WFC:ROW_WFC + 3 * N_DEV, 0:UP_CHUNK].reshape(3, N_DEV, UP_CHUNK),
                              (0, me, 0), (3, 1, UP_CHUNK)).reshape(3, UP_CHUNK)
    shard_res = _adamw_shards([(w_short_conv[0], g_wsc, m_w_short_conv[0], v_w_short_conv[0]),
                               (w_ffn_conv[0], g_wfc, m_w_ffn_conv[0], v_w_ffn_conv[0])])
    small_out["w_short_conv"] = [g_wsc[None]] + [a[None] for a in shard_res[0]]
    small_out["w_ffn_conv"] = [g_wfc[None]] + [a[None] for a in shard_res[1]]

    order = ["rel_bias", "g_mix", "w_in", "w_short_conv", "g_attn_out", "g_conv_out", "w_out", "g_xattn", "g_mem",
             "w_xq", "w_xk", "w_xv", "w_xo", "g_ffn", "w_up", "w_ffn_conv", "b_ffn_conv", "w_down", "g_final"]
    allp = {**big_out, **small_out}
    outs = [loss, grad_x[None]]
    for kind in range(4):
        outs += [allp[n][kind] for n in order]
    return tuple(outs)
```

```python
import math

import numpy as np
import jax
import jax.numpy as jnp
from jax import lax
from jax.experimental import pallas as pl
from jax.experimental.pallas import tpu as pltpu

F32 = jnp.float32
BF16 = jnp.bfloat16
MESH = pl.DeviceIdType.MESH

N_DEV = 8
D_MODEL = 1024
ATTN_W = 512
CONV_W = 512
N_HEADS = 8
HEAD_DIM = 64
WIN = 128
DILATIONS = (1, 4, 16)
N_BUCKETS = 32
BUCKET_MAX_EXACT = 16
BUCKET_MAX_DISTANCE = 2048
N_MEM_HEADS = 4
MEM_HEAD_DIM = 256
D_FF = 2816
IN_COLS = 3072
IN_CHUNK = IN_COLS // N_DEV
UP_CHUNK = 2 * D_FF // N_DEV
FFN_CHUNKS = 4
FFN_WIDTH = 2 * D_FF // FFN_CHUNKS
EPS = 1e-6

ADAM_LR = 0.001
ADAM_B1 = 0.9
ADAM_B2 = 0.999
ADAM_EPS = 1e-08
ADAM_WD = 0.01
ADAM_STEP = 10

SUBLANES = 8
LANES = 128
HALO = 16
TM = 512
TM_FFN = 256
TS_DW = 4096
SWA_BLOCKS = 8
VMEM_LIMIT = 56 * 1024 * 1024

ROW_RELB, ROW_GMIX, ROW_GXATTN, ROW_GMEM, ROW_GFFN, ROW_GFINAL, ROW_GAC = 0, 8, 16, 24, 32, 40, 48
ROW_WSC, ROW_BFC, ROW_WFC, ROW_LOSS, SMALL_ROWS = 56, 64, 72, 96, 104


def _cparams(n_grid):
    return pltpu.CompilerParams(dimension_semantics=("arbitrary",) * n_grid, vmem_limit_bytes=VMEM_LIMIT)


def _full(shape):
    nd = len(shape)
    return pl.BlockSpec(tuple(shape), lambda *_: (0,) * nd)


def _resident(shape):
    nd = len(shape)
    return pl.BlockSpec(tuple(shape), lambda *_: (0,) * nd, pipeline_mode=pl.Buffered(1))


ANY_SPEC = pl.BlockSpec(memory_space=pl.ANY)
HBM_SPEC = pl.BlockSpec(memory_space=pltpu.HBM)
SEM_SPEC = pl.BlockSpec(memory_space=pltpu.SEMAPHORE)
VMEM_SPEC = pl.BlockSpec(memory_space=pltpu.VMEM)
SMEM_SPEC = pl.BlockSpec(memory_space=pltpu.SMEM)
DATAFLOW = pltpu.SideEffectType.DATAFLOW_SIDE_EFFECTING


def _rms(x):
    r = lax.rsqrt(jnp.mean(x * x, axis=-1, keepdims=True) + EPS)
    return x * r, r


def _rms_bwd(xh, r, g, dy):
    dxh = dy * g
    return r * (dxh - xh * jnp.mean(dxh * xh, axis=-1, keepdims=True))


def _shift_down(u, halo, k):
    ru = pltpu.roll(u, k, 0)
    rh = pltpu.roll(halo, k, 0)
    row = lax.broadcasted_iota(jnp.int32, rh.shape, 0)
    head = jnp.where(row < k, rh, ru[0:SUBLANES])
    return jnp.concatenate([head, ru[SUBLANES:]], axis=0)


def _shift_up(u, halo, k):
    tm = u.shape[0]
    ru = pltpu.roll(u, tm - k, 0)
    rh = pltpu.roll(halo, SUBLANES - k, 0)
    row = lax.broadcasted_iota(jnp.int32, rh.shape, 0)
    tail = jnp.where(row >= SUBLANES - k, rh, ru[tm - SUBLANES:])
    return jnp.concatenate([ru[:tm - SUBLANES], tail], axis=0)


def _causal_conv3(u, halo, w_ref):
    return (_shift_down(u, halo, 2) * w_ref[0:1, :] + _shift_down(u, halo, 1) * w_ref[1:2, :]) + u * w_ref[2:3, :]


def _dot(a, b):
    return jnp.dot(a, b, preferred_element_type=F32)


def _dot_nt(a, b):
    return lax.dot_general(a, b, (((1,), (1,)), ((), ())), preferred_element_type=F32)


def _dot_tn(a, b):
    return lax.dot_general(a, b, (((0,), (0,)), ((), ())), preferred_element_type=F32)


def _sigmoid(x):
    return 0.5 * jnp.tanh(0.5 * x) + 0.5


def _bucket_tables():
    qi = np.arange(WIN)[:, None]
    kj = np.arange(2 * WIN)[None, :]
    steps = np.clip(qi + WIN - kj, 0, WIN)
    out = []
    for d in DILATIONS:
        dist = steps * d
        dd = np.maximum(dist, 1).astype(np.float32)
        large = BUCKET_MAX_EXACT + (
            np.log(dd / np.float32(BUCKET_MAX_EXACT)) / np.float32(math.log(BUCKET_MAX_DISTANCE / BUCKET_MAX_EXACT))
            * np.float32(N_BUCKETS - BUCKET_MAX_EXACT)).astype(np.int32)
        large = np.minimum(large, N_BUCKETS - 1)
        out.append(np.where(dist < BUCKET_MAX_EXACT, dist, large).astype(np.int32))
    return np.stack(out)


def _band_mask():
    qi = lax.broadcasted_iota(jnp.int32, (WIN, 2 * WIN), 0)
    kj = lax.broadcasted_iota(jnp.int32, (WIN, 2 * WIN), 1)
    steps = qi + WIN - kj
    return (steps >= 0) & (steps <= WIN)


def _bias_fwd(rel_bias, buckets):
    present = [sorted(set(buckets[p].ravel().tolist())) for p in range(3)]

    def body(rb_ref, bk_ref, o_ref):
        band = _band_mask()
        for p in range(3):
            bk = bk_ref[p]
            for h in range(N_HEADS):
                acc = jnp.zeros((WIN, 2 * WIN), F32)
                for b in present[p]:
                    acc = jnp.where(bk == b, rb_ref[h, b], acc)
                o_ref[p, h] = jnp.where(band, acc, -jnp.inf)

    return pl.pallas_call(
        body, name="bias_fwd",
        out_shape=jax.ShapeDtypeStruct((3, N_HEADS, WIN, 2 * WIN), F32),
        in_specs=[pl.BlockSpec(memory_space=pltpu.SMEM), pl.BlockSpec(memory_space=pltpu.VMEM)],
        out_specs=pl.BlockSpec(memory_space=pltpu.VMEM),
    )(rel_bias, jnp.asarray(buckets))


def _bias_bwd(dbias, buckets):
    present = [set(buckets[p].ravel().tolist()) for p in range(3)]

    def body(db_ref, bk_ref, o_ref):
        lane = lax.broadcasted_iota(jnp.int32, (1, D_MODEL), 1)
        rows = []
        for h in range(N_HEADS):
            row = jnp.zeros((1, D_MODEL), F32)
            for b in range(N_BUCKETS):
                tot = jnp.zeros((1, 1), F32)
                for p in (p for p in range(3) if b in present[p]):
                    sel = jnp.where(bk_ref[p] == b, db_ref[p, h], 0.0)
                    tot = tot + jnp.sum(jnp.sum(sel, axis=0, keepdims=True), axis=1, keepdims=True)
                row = jnp.where(lane == b, tot, row)
            rows.append(row)
        o_ref[...] = jnp.concatenate(rows, axis=0)

    return pl.pallas_call(
        body, name="bias_bwd",
        out_shape=jax.ShapeDtypeStruct((N_HEADS, D_MODEL), F32),
        in_specs=[pl.BlockSpec(memory_space=pltpu.VMEM), pl.BlockSpec(memory_space=pltpu.VMEM)],
        out_specs=pl.BlockSpec(memory_space=pltpu.VMEM),
    )(dbias, jnp.asarray(buckets))


def _spread(val, scr_ref, out_refs, dtype):
    out_refs[0][...] = val.astype(dtype)
    n_blk = val.shape[1] // LANES
    for c in range(n_blk):
        scr_ref[c] = val[:, c * LANES:(c + 1) * LANES]
    for o_ref, d in zip(out_refs[1:], DILATIONS[1:]):
        for r in range(d):
            for c in range(n_blk):
                o_ref[r, :, c * LANES:(c + 1) * LANES] = scr_ref.at[c][pl.ds(r, TM // d, stride=d), :].astype(dtype)


def _gather_classes(blk_ref, scr_ref, d):
    n_blk = blk_ref.shape[2] // LANES
    for r in range(d):
        for c in range(n_blk):
            scr_ref.at[c][pl.ds(r, TM // d, stride=d), :] = blk_ref[r, :, c * LANES:(c + 1) * LANES].astype(F32)
    return jnp.concatenate([scr_ref[c] for c in range(n_blk)], axis=1)


def _class_specs(cols):
    return [pl.BlockSpec((TM, cols), lambda i: (i, 0))] + [
        pl.BlockSpec((d, TM // d, cols), lambda i: (0, i, 0)) for d in DILATIONS[1:]]


def _class_shapes(s, cols, dtype):
    return [jax.ShapeDtypeStruct((s, cols), dtype)] + [
        jax.ShapeDtypeStruct((d, s // d, cols), dtype) for d in DILATIONS[1:]]


def _load_w_in_pairs(w_hbm, w_scr, sems):
    @pl.when(pl.program_id(0) == 0)
    def _():
        copies = [pltpu.make_async_copy(w_hbm.at[j], w_scr.at[j // 2, :, pl.ds((j % 2) * IN_CHUNK, IN_CHUNK)],
                                        sems.at[j]) for j in range(N_DEV)]
        for copy in copies:
            copy.start()
        for copy in copies:
            copy.wait()


W_IN_PAIRS = [pltpu.VMEM((N_DEV // 2, D_MODEL, 2 * IN_CHUNK), BF16), pltpu.SemaphoreType.DMA((N_DEV,))]


def _rms_proj(x, g_mix, w_in_g, dep):
    s = x.shape[0]

    def body(x_ref, g_ref, w_hbm, dep_ref, h_ref, q1, q4, q16, k1, k4, k16, v1, v4, v16, gb_ref, gc_ref, xi_ref, scr,
             w_scr, w_sems):
        _load_w_in_pairs(w_hbm, w_scr, w_sems)
        xh, _ = _rms(x_ref[...])
        h = (xh * g_ref[...]).astype(BF16)
        h_ref[...] = h
        proj = jnp.concatenate([_dot(h, w_scr[j]) for j in range(N_DEV // 2)], axis=1)
        _spread(proj[:, 0:512] * (HEAD_DIM ** -0.5), scr, (q1, q4, q16), BF16)
        _spread(proj[:, 512:1024], scr, (k1, k4, k16), BF16)
        _spread(proj[:, 1024:1536], scr, (v1, v4, v16), BF16)
        gb_ref[...] = proj[:, 1536:2048]
        gc_ref[...] = proj[:, 2048:2560]
        xi_ref[...] = proj[:, 2560:3072]

    row = lambda n: pl.BlockSpec((TM, n), lambda i: (i, 0))
    res = pl.pallas_call(
        body, name="rms_proj", grid=(s // TM,),
        out_shape=[jax.ShapeDtypeStruct((s, D_MODEL), BF16)] + _class_shapes(s, 512, BF16) * 3
        + [jax.ShapeDtypeStruct((s, 512), F32)] * 3,
        in_specs=[row(D_MODEL), _full(g_mix.shape), ANY_SPEC, ANY_SPEC],
        out_specs=[row(D_MODEL)] + _class_specs(512) * 3 + [row(512)] * 3,
        scratch_shapes=[pltpu.VMEM((512 // LANES, TM, LANES), F32)] + W_IN_PAIRS,
        compiler_params=_cparams(1),
    )(x, g_mix, w_in_g, dep)
    return res[0], res[1:4], res[4:7], res[7:10], res[10], res[11], res[12]


def _pair_split(x2):
    lane = lax.broadcasted_iota(jnp.int32, x2.shape, 1)
    zero = jnp.zeros_like(x2)
    return jnp.where(lane < HEAD_DIM, x2, zero), jnp.where(lane >= HEAD_DIM, x2, zero)


def _pair_join(even, odd):
    lane = lax.broadcasted_iota(jnp.int32, (even.shape[0], LANES), 1)
    return jnp.where(lane < HEAD_DIM, even, odd)


def _swa_steps(qc, dil):
    n128 = qc.shape[1] // WIN
    nsub = min(SWA_BLOCKS, n128)
    nb = n128 // nsub
    ncls = min(dil, SWA_BLOCKS // nsub) if nb == 1 else 1
    return nsub, nb, ncls


def _swa_fwd(qc, kc, vc, bias, dil, dep):
    nsub, nb, ncls = _swa_steps(qc, dil)
    whole = nb == 1

    def body(q_ref, kp_ref, kc_ref, vp_ref, vc_ref, b_ref, dep_ref, o_ref, lse_ref, s_scr, p_scr):
        no_prev = (pl.program_id(1) == 0) & (lax.broadcasted_iota(jnp.int32, (WIN, 2 * WIN), 1) < WIN)
        pairs = [slice(a * LANES, (a + 1) * LANES) for a in range(N_HEADS // 2)]
        for c, t in [(c, t) for c in range(ncls) for t in range(nsub)]:
            i = c * nsub + t
            rows = slice(t * WIN, (t + 1) * WIN)
            alone = whole and t == 0
            cols = slice(WIN, 2 * WIN) if alone else slice(0, 2 * WIN)

            def keys(prev_ref, cur_ref, sl):
                if alone:
                    return cur_ref[c, rows, sl]
                if t == 0:
                    return jnp.concatenate([prev_ref[c, :, sl], cur_ref[c, rows, sl]], axis=0)
                return cur_ref[c, (t - 1) * WIN:(t + 1) * WIN, sl]

            for a, sl in enumerate(pairs):
                k2 = keys(kp_ref, kc_ref, sl)
                for e, qh in enumerate(_pair_split(q_ref[c, rows, sl])):
                    s_scr[i, 2 * a + e, :, cols] = _dot_nt(qh, k2)
            den, lse = [], []
            for h in range(N_HEADS):
                lg = s_scr[i, h, :, cols] + b_ref[h, :, cols]
                if t == 0 and not whole:
                    lg = jnp.where(no_prev, -jnp.inf, lg)
                m = jnp.max(lg, axis=-1, keepdims=True)
                p = jnp.exp(lg - m)
                den.append(jnp.sum(p, axis=-1, keepdims=True))
                p_scr[i, h, :, cols] = p.astype(BF16)
                lse.append(m + jnp.log(den[h]))
            for a, sl in enumerate(pairs):
                v_even, v_odd = _pair_split(keys(vp_ref, vc_ref, sl))
                o2 = _dot(p_scr[i, 2 * a, :, cols], v_even) + _dot(p_scr[i, 2 * a + 1, :, cols], v_odd)
                o_ref[c, rows, sl] = o2 / _pair_join(den[2 * a], den[2 * a + 1])
                lse_ref[c, rows, sl] = _pair_join(lse[2 * a], lse[2 * a + 1])

    cur = pl.BlockSpec((ncls, nsub * WIN, 512), lambda r, b: (r, b, 0))
    prev = pl.BlockSpec((ncls, WIN, 512), lambda r, b: (r, jnp.maximum(nsub * b - 1, 0), 0))
    wide = (ncls * nsub, N_HEADS, WIN, 2 * WIN)
    return pl.pallas_call(
        body, name=f"swa_fwd_d{dil}", grid=(dil // ncls, nb),
        out_shape=[jax.ShapeDtypeStruct(qc.shape, F32)] * 2,
        in_specs=[cur, prev, cur, prev, cur, _full(bias.shape), ANY_SPEC],
        out_specs=[cur] * 2,
        scratch_shapes=[pltpu.VMEM(wide, F32), pltpu.VMEM(wide, BF16)],
        compiler_params=_cparams(2),
    )(qc, kc, kc, vc, vc, bias, dep)


def _mix_out(branches, gb, gc, xi, x, w_sc, g_a, g_c, w_out):
    s = x.shape[0]
    tb = TM // SUBLANES

    def body(o1, l1, o4, l4, o16, l16, gb_ref, gc_ref, xi_ref, gch_ref, xih_ref, x_ref, wsc_ref,
             ga_ref, gcv_ref, wout_ref, attn_ref, lse1, lse4, lse16, mixed_ref, x1_ref, scr_a, scr_b, scr_c, scr_d):
        i = pl.program_id(0)
        la, lb, lc = l1[...], _gather_classes(l4, scr_a, 4), _gather_classes(l16, scr_b, 16)
        m_all = jnp.maximum(jnp.maximum(la, lb), lc)
        ea, eb, ec = jnp.exp(la - m_all), jnp.exp(lb - m_all), jnp.exp(lc - m_all)
        den = (ea + eb) + ec
        num = (ea * o1[...] + eb * _gather_classes(o4, scr_c, 4)) + ec * _gather_classes(o16, scr_d, 16)
        attn = num / den
        attn_ref[...] = attn
        _spread(m_all + jnp.log(den), scr_a, (lse1, lse4, lse16), F32)
        xa, _ = _rms(attn)
        u = gc_ref[...] * xi_ref[...]
        uh = jnp.where(i > 0, gch_ref[...] * xih_ref[...], 0.0)
        conv = gb_ref[...] * _causal_conv3(u, uh, wsc_ref)
        xc, _ = _rms(conv)
        mixed = jnp.concatenate([xa * ga_ref[...], xc * gcv_ref[...]], axis=1).astype(BF16)
        mixed_ref[...] = mixed
        x1_ref[...] = x_ref[...] + _dot(mixed, wout_ref[...])

    row = lambda n: pl.BlockSpec((TM, n), lambda i: (i, 0))
    halo = pl.BlockSpec((SUBLANES, 512), lambda i: (jnp.maximum(i * tb - 1, 0), 0))
    cs = _class_specs(512)
    flat = [a for br in branches for a in br]
    res = pl.pallas_call(
        body, name="mix_out", grid=(s // TM,),
        out_shape=[jax.ShapeDtypeStruct((s, 512), F32)] + _class_shapes(s, 512, F32)
        + [jax.ShapeDtypeStruct((s, D_MODEL), BF16), jax.ShapeDtypeStruct((s, D_MODEL), F32)],
        in_specs=[cs[0], cs[0], cs[1], cs[1], cs[2], cs[2], row(512), row(512), row(512), halo, halo,
                  row(D_MODEL), _full(w_sc.shape), _full(g_a.shape), _full(g_c.shape), _full(w_out.shape)],
        out_specs=[row(512)] + cs + [row(D_MODEL), row(D_MODEL)],
        scratch_shapes=[pltpu.VMEM((512 // LANES, TM, LANES), F32)] * 4,
        compiler_params=_cparams(1),
    )(*flat, gb, gc, xi, gc, xi, x, w_sc, g_a, g_c, w_out)
    return res[0], res[1:4], res[4], res[5]


def _mem_kv(mem, g_mem, w_xk, w_xv):
    def body(mem_ref, g_ref, wk_ref, wv_ref, mn_ref, k_ref, v_ref):
        xh, _ = _rms(mem_ref[...])
        mn = (xh * g_ref[...]).astype(BF16)
        mn_ref[...] = mn
        k_ref[...] = _dot(mn, wk_ref[...]).astype(BF16)
        v_ref[...] = _dot(mn, wv_ref[...]).astype(BF16)

    vm = pl.BlockSpec(memory_space=pltpu.VMEM)
    return pl.pallas_call(
        body, name="mem_kv",
        out_shape=[jax.ShapeDtypeStruct(mem.shape, BF16)] * 3,
        in_specs=[vm] * 4, out_specs=[vm] * 3,
        compiler_params=pltpu.CompilerParams(vmem_limit_bytes=VMEM_LIMIT),
    )(mem, g_mem, w_xk, w_xv)


def _xattn_fwd(x1, g, w_xq, k, v, w_xo, dep):
    s = x1.shape[0]

    def body(x1_ref, g_ref, wq_ref, k_ref, v_ref, wo_ref, dep_ref, h2_ref, q_ref, o_ref, x2_ref):
        x1v = x1_ref[...]
        xh, _ = _rms(x1v)
        h2 = (xh * g_ref[...]).astype(BF16)
        h2_ref[...] = h2
        qb = _dot(h2, wq_ref[...]).astype(BF16)
        q_ref[...] = qb
        outs = []
        for h in range(N_MEM_HEADS):
            sl = slice(h * MEM_HEAD_DIM, (h + 1) * MEM_HEAD_DIM)
            lg = _dot_nt(qb[:, sl], k_ref[:, sl]) * (MEM_HEAD_DIM ** -0.5)
            p = jnp.exp(lg - jnp.max(lg, axis=-1, keepdims=True))
            p = p / jnp.sum(p, axis=-1, keepdims=True)
            outs.append(_dot(p.astype(BF16), v_ref[:, sl]))
        o = jnp.concatenate(outs, axis=1).astype(BF16)
        o_ref[...] = o
        x2_ref[...] = x1v + _dot(o, wo_ref[...])

    row = pl.BlockSpec((TM, D_MODEL), lambda i: (i, 0))
    return pl.pallas_call(
        body, name="xattn_fwd", grid=(s // TM,),
        out_shape=[jax.ShapeDtypeStruct((s, D_MODEL), BF16)] * 3 + [jax.ShapeDtypeStruct((s, D_MODEL), F32)],
        in_specs=[row, _full(g.shape), _full(w_xq.shape), _full(k.shape), _full(v.shape), _full(w_xo.shape), ANY_SPEC],
        out_specs=[row] * 4,
        compiler_params=_cparams(1),
    )(x1, g, w_xq, k, v, w_xo, dep)


def _ffn_conv(h_ext, wup_ref, wfc_ref, bfc_ref, j):
    u = _dot_nt(h_ext, wup_ref[j])
    w = wfc_ref[j]
    c = ((pltpu.roll(u, 2, 0) * w[0:1, :] + pltpu.roll(u, 1, 0) * w[1:2, :]) + u * w[2:3, :]) + bfc_ref[j]
    return c[HALO:], u[HALO:]


def _ffn_fwd(x2, g, w_up_g, w_fc, b_fc, w_down_g, g_final, target):
    s = x2.shape[0]
    tb = TM_FFN // HALO
    n_ch, wid = w_up_g.shape[:2]
    half = n_ch // 2

    def body(x_ref, xp_ref, g_ref, wup_ref, wfc_ref, bfc_ref, wd_ref, gf_ref, t_ref, h_ref, u_ref, c_ref, act_ref,
             dx3_ref, loss_ref, dgf_ref):
        i = pl.program_id(0)

        @pl.when(i == 0)
        def _():
            loss_ref[...] = jnp.zeros_like(loss_ref)
            dgf_ref[...] = jnp.zeros_like(dgf_ref)

        x2v = x_ref[...]
        gv = g_ref[...]
        h = (_rms(x2v)[0] * gv).astype(BF16)
        h_ref[...] = h
        hp = jnp.where(i > 0, _rms(xp_ref[...])[0] * gv, 0.0).astype(BF16)
        h_ext = jnp.concatenate([hp, h], axis=0)
        down = jnp.zeros((TM_FFN, D_MODEL), F32)
        for j in range(half):
            cg, ug = _ffn_conv(h_ext, wup_ref, wfc_ref, bfc_ref, j)
            cv, uv = _ffn_conv(h_ext, wup_ref, wfc_ref, bfc_ref, j + half)
            c_ref[j] = cg
            c_ref[j + half] = cv
            u_ref[j] = ug.astype(BF16)
            u_ref[j + half] = uv.astype(BF16)
            a = ((cg * _sigmoid(cg)) * cv).astype(BF16)
            act_ref[j] = a
            down = down + _dot(a, wd_ref[j])
        x3 = x2v + down
        xh, r = _rms(x3)
        gf = gf_ref[...]
        e = xh * gf - t_ref[...]
        loss_ref[...] += 0.5 * jnp.sum(jnp.sum(e * e, axis=1, keepdims=True), axis=0, keepdims=True) / D_MODEL
        dy = e * (1.0 / D_MODEL)
        dgf_ref[0:1, :] += jnp.sum(dy * xh, axis=0, keepdims=True)
        dx3_ref[...] = _rms_bwd(xh, r, gf, dy)

    row = pl.BlockSpec((TM_FFN, D_MODEL), lambda i: (i, 0))
    prev = pl.BlockSpec((HALO, D_MODEL), lambda i: (jnp.maximum(i * tb - 1, 0), 0))
    return pl.pallas_call(
        body, name="ffn_fwd", grid=(s // TM_FFN,),
        out_shape=[jax.ShapeDtypeStruct((s, D_MODEL), BF16), jax.ShapeDtypeStruct((n_ch, s, wid), BF16),
                   jax.ShapeDtypeStruct((n_ch, s, wid), F32), jax.ShapeDtypeStruct((half, s, wid), BF16),
                   jax.ShapeDtypeStruct((s, D_MODEL), F32), jax.ShapeDtypeStruct((SUBLANES, 128), F32),
                   jax.ShapeDtypeStruct((SUBLANES, D_MODEL), F32)],
        in_specs=[row, prev, _full(g.shape), _resident(w_up_g.shape), _full(w_fc.shape), _full(b_fc.shape),
                  _resident(w_down_g.shape), _full(g_final.shape), row],
        out_specs=[row, pl.BlockSpec((n_ch, TM_FFN, wid), lambda i: (0, i, 0)),
                   pl.BlockSpec((n_ch, TM_FFN, wid), lambda i: (0, i, 0)),
                   pl.BlockSpec((half, TM_FFN, wid), lambda i: (0, i, 0)), row,
                   _full((SUBLANES, 128)), _full((SUBLANES, D_MODEL))],
        compiler_params=_cparams(1),
    )(x2, x2, g, w_up_g, w_fc, b_fc, w_down_g, g_final, target)


def _ffn_bwd(dx3, up, conv, x2, g, w_up_g, w_fc, w_down_g):
    s = x2.shape[0]
    tb = TM_FFN // HALO
    last = s // HALO - 1
    n_tiles = s // TM_FFN
    n_ch, wid = w_up_g.shape[:2]
    half = n_ch // 2
    n_ext = TM_FFN + HALO

    def body(dx_ref, dxn_ref, u_ref, c_ref, cn_ref, x2_ref, g_ref, wup_ref, wfc_ref, wd_ref,
             dup_ref, dx2_ref, dg_ref, dwfc_ref, dbfc_ref):
        i = pl.program_id(0)

        @pl.when(i == 0)
        def _():
            dg_ref[...] = jnp.zeros_like(dg_ref)
            dwfc_ref[...] = jnp.zeros_like(dwfc_ref)
            dbfc_ref[...] = jnp.zeros_like(dbfc_ref)

        dxv = dx_ref[...]
        dxn = jnp.where(i < n_tiles - 1, dxn_ref[...], 0.0)
        dx_ext = jnp.concatenate([dxv, dxn], axis=0).astype(BF16)
        dh = jnp.zeros((TM_FFN, D_MODEL), F32)
        for j in range(half):
            cg = jnp.concatenate([c_ref[j], cn_ref[j]], axis=0)
            cv = jnp.concatenate([c_ref[j + half], cn_ref[j + half]], axis=0)
            dact = _dot_nt(dx_ext, wd_ref[j])
            sg = _sigmoid(cg)
            silu = cg * sg
            parts = ((j + half, dact * silu), (j, (dact * cv) * (sg + silu * (1.0 - sg))))
            for jj, dc in parts:
                u = u_ref[jj].astype(F32)
                dc0, dc1, dc2 = dc[:TM_FFN], pltpu.roll(dc, n_ext - 1, 0)[:TM_FFN], pltpu.roll(dc, n_ext - 2, 0)[:TM_FFN]
                dbfc_ref[jj:jj + 1, :] += jnp.sum(dc0, axis=0, keepdims=True)
                dwfc_ref[0, jj:jj + 1, :] += jnp.sum(dc2 * u, axis=0, keepdims=True)
                dwfc_ref[1, jj:jj + 1, :] += jnp.sum(dc1 * u, axis=0, keepdims=True)
                dwfc_ref[2, jj:jj + 1, :] += jnp.sum(dc0 * u, axis=0, keepdims=True)
                w = wfc_ref[jj]
                du = ((dc0 * w[2:3, :] + dc1 * w[1:2, :]) + dc2 * w[0:1, :]).astype(BF16)
                dup_ref[jj] = du
                dh = dh + _dot(du, wup_ref[jj])
        xh, r = _rms(x2_ref[...])
        dg_ref[0:1, :] += jnp.sum(dh * xh, axis=0, keepdims=True)
        dx2_ref[...] = dxv + _rms_bwd(xh, r, g_ref[...], dh)

    row = pl.BlockSpec((TM_FFN, D_MODEL), lambda i: (i, 0))
    nxt = pl.BlockSpec((HALO, D_MODEL), lambda i: (jnp.minimum((i + 1) * tb, last), 0))
    cur_c = pl.BlockSpec((n_ch, TM_FFN, wid), lambda i: (0, i, 0))
    nxt_c = pl.BlockSpec((n_ch, HALO, wid), lambda i: (0, jnp.minimum((i + 1) * tb, last), 0))
    return pl.pallas_call(
        body, name="ffn_bwd", grid=(n_tiles,),
        out_shape=[jax.ShapeDtypeStruct((n_ch, s, wid), BF16), jax.ShapeDtypeStruct((s, D_MODEL), F32),
                   jax.ShapeDtypeStruct((SUBLANES, D_MODEL), F32), jax.ShapeDtypeStruct((3, n_ch, wid), F32),
                   jax.ShapeDtypeStruct((n_ch, wid), F32)],
        in_specs=[row, nxt, cur_c, cur_c, nxt_c, row, _full(g.shape), _resident(w_up_g.shape), _full(w_fc.shape),
                  _resident(w_down_g.shape)],
        out_specs=[cur_c, row, _full((SUBLANES, D_MODEL)), _full((3, n_ch, wid)), _full((n_ch, wid))],
        compiler_params=_cparams(1),
    )(dx3, dx3, up, conv, conv, x2, g, w_up_g, w_fc, w_down_g)


def _xattn_bwd(dx2, o, q, k, v, w_xo, w_xq, x1, g, dep):
    s = x1.shape[0]

    def body(dx2_ref, o_ref, q_ref, k_ref, v_ref, wo_ref, wq_ref, x1_ref, g_ref, dep_ref, dq_ref, dx1_ref, dk_ref,
             dv_ref, dg_ref):
        @pl.when(pl.program_id(0) == 0)
        def _():
            dk_ref[...] = jnp.zeros_like(dk_ref)
            dv_ref[...] = jnp.zeros_like(dv_ref)
            dg_ref[...] = jnp.zeros_like(dg_ref)

        dx2v = dx2_ref[...]
        do = _dot_nt(dx2v.astype(BF16), wo_ref[...])
        dqs = []
        for h in range(N_MEM_HEADS):
            sl = slice(h * MEM_HEAD_DIM, (h + 1) * MEM_HEAD_DIM)
            qh, kh, vh = q_ref[:, sl], k_ref[:, sl], v_ref[:, sl]
            lg = _dot_nt(qh, kh) * (MEM_HEAD_DIM ** -0.5)
            p = jnp.exp(lg - jnp.max(lg, axis=-1, keepdims=True))
            p = p / jnp.sum(p, axis=-1, keepdims=True)
            doh = do[:, sl].astype(BF16)
            dp = _dot_nt(doh, vh)
            ds = (p * (dp - jnp.sum(p * dp, axis=-1, keepdims=True)) * (MEM_HEAD_DIM ** -0.5)).astype(BF16)
            dqs.append(_dot(ds, kh))
            dk_ref[:, sl] += _dot_tn(ds, qh)
            dv_ref[:, sl] += _dot_tn(p.astype(BF16), doh)
        dq = jnp.concatenate(dqs, axis=1).astype(BF16)
        dq_ref[...] = dq
        dh2 = _dot_nt(dq, wq_ref[...])
        xh, r = _rms(x1_ref[...])
        dg_ref[0:1, :] += jnp.sum(dh2 * xh, axis=0, keepdims=True)
        dx1_ref[...] = dx2v + _rms_bwd(xh, r, g_ref[...], dh2)

    row = pl.BlockSpec((TM, D_MODEL), lambda i: (i, 0))
    return pl.pallas_call(
        body, name="xattn_bwd", grid=(s // TM,),
        out_shape=[jax.ShapeDtypeStruct((s, D_MODEL), BF16), jax.ShapeDtypeStruct((s, D_MODEL), F32),
                   jax.ShapeDtypeStruct(k.shape, F32), jax.ShapeDtypeStruct(k.shape, F32),
                   jax.ShapeDtypeStruct((SUBLANES, D_MODEL), F32)],
        in_specs=[row, row, row, _full(k.shape), _full(v.shape), _full(w_xo.shape), _full(w_xq.shape), row,
                  _full(g.shape), ANY_SPEC],
        out_specs=[row, row, _full(k.shape), _full(k.shape), _full((SUBLANES, D_MODEL))],
        compiler_params=_cparams(1),
    )(dx2, o, q, k, v, w_xo, w_xq, x1, g, dep)


def _mem_kv_bwd(dk, dv, mem_n, mem, w_xk, w_xv):
    def body(dk_ref, dv_ref, mn_ref, mem_ref, wk_ref, wv_ref, dwk_ref, dwv_ref, dg_ref):
        dkb, dvb = dk_ref[...].astype(BF16), dv_ref[...].astype(BF16)
        mn = mn_ref[...]
        dwk_ref[...] = _dot_tn(mn, dkb).astype(BF16)
        dwv_ref[...] = _dot_tn(mn, dvb).astype(BF16)
        dmn = _dot_nt(dkb, wk_ref[...]) + _dot_nt(dvb, wv_ref[...])
        xh, _ = _rms(mem_ref[...])
        dg_ref[...] = jnp.zeros_like(dg_ref)
        dg_ref[0:1, :] = jnp.sum(dmn * xh, axis=0, keepdims=True)

    vm = pl.BlockSpec(memory_space=pltpu.VMEM)
    return pl.pallas_call(
        body, name="mem_kv_bwd",
        out_shape=[jax.ShapeDtypeStruct(w_xk.shape, BF16), jax.ShapeDtypeStruct(w_xv.shape, BF16),
                   jax.ShapeDtypeStruct((SUBLANES, D_MODEL), F32)],
        in_specs=[vm] * 6, out_specs=[vm] * 3,
        compiler_params=pltpu.CompilerParams(vmem_limit_bytes=VMEM_LIMIT),
    )(dk, dv, mem_n, mem, w_xk, w_xv)


def _mix_out_bwd(dx1, w_out, attn, gb, gc, xi, w_sc, g_a, g_c, dep):
    s = dx1.shape[0]
    tb = TM // SUBLANES

    def body(dx1_ref, wout_ref, attn_ref, gb_ref, gc_ref, xi_ref, gch_ref, xih_ref, wsc_ref, ga_ref, gcv_ref, dep_ref,
             da1, da4, da16, dd1, dd4, dd16, dgb_ref, dcv_ref, dga_ref, dgc_ref, dwsc_ref, scr):
        i = pl.program_id(0)

        @pl.when(i == 0)
        def _():
            dga_ref[...] = jnp.zeros_like(dga_ref)
            dgc_ref[...] = jnp.zeros_like(dgc_ref)
            dwsc_ref[...] = jnp.zeros_like(dwsc_ref)

        dmixed = _dot_nt(dx1_ref[...].astype(BF16), wout_ref[...])
        da, dcn = dmixed[:, :ATTN_W], dmixed[:, ATTN_W:]
        attn = attn_ref[...]
        xa, ra = _rms(attn)
        dga_ref[0:1, :] += jnp.sum(da * xa, axis=0, keepdims=True)
        dattn = _rms_bwd(xa, ra, ga_ref[...], da)
        _spread(dattn, scr, (da1, da4, da16), BF16)
        prod = dattn * attn
        dd = jnp.concatenate(
            [jnp.broadcast_to(jnp.sum(prod[:, h * HEAD_DIM:(h + 1) * HEAD_DIM], axis=-1, keepdims=True),
                              (TM, HEAD_DIM)) for h in range(N_HEADS)], axis=1)
        _spread(dd, scr, (dd1, dd4, dd16), F32)
        gbv = gb_ref[...]
        u = gc_ref[...] * xi_ref[...]
        uh = jnp.where(i > 0, gch_ref[...] * xih_ref[...], 0.0)
        u2, u1 = _shift_down(u, uh, 2), _shift_down(u, uh, 1)
        cv = (u2 * wsc_ref[0:1, :] + u1 * wsc_ref[1:2, :]) + u * wsc_ref[2:3, :]
        xc, rc = _rms(gbv * cv)
        dgc_ref[0:1, :] += jnp.sum(dcn * xc, axis=0, keepdims=True)
        dconv = _rms_bwd(xc, rc, gcv_ref[...], dcn)
        dgb_ref[...] = (dconv * cv).astype(BF16)
        dcv = dconv * gbv
        dcv_ref[...] = dcv
        dwsc_ref[0:1, :] += jnp.sum(dcv * u2, axis=0, keepdims=True)
        dwsc_ref[1:2, :] += jnp.sum(dcv * u1, axis=0, keepdims=True)
        dwsc_ref[2:3, :] += jnp.sum(dcv * u, axis=0, keepdims=True)

    row = lambda n: pl.BlockSpec((TM, n), lambda i: (i, 0))
    halo = pl.BlockSpec((SUBLANES, 512), lambda i: (jnp.maximum(i * tb - 1, 0), 0))
    acc = _full((SUBLANES, 512))
    res = pl.pallas_call(
        body, name="mix_out_bwd", grid=(s // TM,),
        out_shape=_class_shapes(s, 512, BF16) + _class_shapes(s, 512, F32)
        + [jax.ShapeDtypeStruct((s, 512), BF16), jax.ShapeDtypeStruct((s, 512), F32)]
        + [jax.ShapeDtypeStruct((SUBLANES, 512), F32)] * 3,
        in_specs=[row(D_MODEL), _full(w_out.shape), row(512), row(512), row(512), row(512), halo, halo,
                  _full(w_sc.shape), _full(g_a.shape), _full(g_c.shape), ANY_SPEC],
        out_specs=_class_specs(512) * 2 + [row(512)] * 2 + [acc] * 3,
        scratch_shapes=[pltpu.VMEM((512 // LANES, TM, LANES), F32)],
        compiler_params=_cparams(1),
    )(dx1, w_out, attn, gb, gc, xi, gc, xi, w_sc, g_a, g_c, dep)
    return res[0:3], res[3:6], res[6], res[7], res[8], res[9], res[10]


def _swa_bwd(qc, kc, vc, doc, lsec, ddc, bias, dil, dep):
    nsub, nb, ncls = _swa_steps(qc, dil)
    n128 = nsub * nb
    whole = nb == 1

    def body(q_ref, qn_ref, kp_ref, kc_ref, vp_ref, vc_ref, do_ref, don_ref, lse_ref, lsen_ref, dd_ref, ddn_ref,
             b_ref, dep_ref, dq_ref, dk_ref, dv_ref, db_ref, s_scr, dp_scr, sn_scr, dpn_scr, ds_scr, p_scr, dsn_scr,
             pn_scr):
        r, b = pl.program_id(0), pl.program_id(1)

        @pl.when((r == 0) & (b == 0))
        def _():
            db_ref[...] = jnp.zeros_like(db_ref)

        pairs = [slice(a * LANES, (a + 1) * LANES) for a in range(N_HEADS // 2)]
        blk = [slice(t * WIN, (t + 1) * WIN) for t in range(nsub)]
        last = blk[nsub - 1]
        cols = lambda t: slice(WIN, 2 * WIN) if whole and t == 0 else slice(0, 2 * WIN)
        of_head = lambda ref, c, rows, h: ref[c, rows, h * HEAD_DIM:h * HEAD_DIM + 1]
        no_prev = (b == 0) & (lax.broadcasted_iota(jnp.int32, (WIN, 2 * WIN), 1) < WIN)

        def keys(prev_ref, cur_ref, c, t, sl):
            if whole and t == 0:
                return cur_ref[c, blk[0], sl]
            if t == 0:
                return jnp.concatenate([prev_ref[c, :, sl], cur_ref[c, blk[0], sl]], axis=0)
            return cur_ref[c, (t - 1) * WIN:(t + 1) * WIN, sl]

        for a, sl in enumerate(pairs):
            for c, t in [(c, t) for c in range(ncls) for t in range(nsub)]:
                k2, v2 = keys(kp_ref, kc_ref, c, t, sl), keys(vp_ref, vc_ref, c, t, sl)
                q_eo = _pair_split(q_ref[c, blk[t], sl])
                do_eo = _pair_split(do_ref[c, blk[t], sl].astype(BF16))
                for e in range(2):
                    s_scr[c * nsub + t, 2 * a + e, :, cols(t)] = _dot_nt(q_eo[e], k2)
                    dp_scr[c * nsub + t, 2 * a + e, :, cols(t)] = _dot_nt(do_eo[e], v2)
            if not whole:
                qn_eo = _pair_split(qn_ref[0, :, sl])
                don_eo = _pair_split(don_ref[0, :, sl].astype(BF16))
                for e in range(2):
                    sn_scr[2 * a + e] = _dot_nt(qn_eo[e], kc_ref[0, last, sl])
                    dpn_scr[2 * a + e] = _dot_nt(don_eo[e], vc_ref[0, last, sl])
        for c, t, h in [(c, t, h) for c in range(ncls) for t in range(nsub) for h in range(N_HEADS)]:
            i, cl = c * nsub + t, cols(t)
            lg = s_scr[i, h, :, cl] + b_ref[h, :, cl]
            if t == 0 and not whole:
                lg = jnp.where(no_prev, -jnp.inf, lg)
            p = jnp.exp(lg - of_head(lse_ref, c, blk[t], h))
            ds = p * (dp_scr[i, h, :, cl] - of_head(dd_ref, c, blk[t], h))
            db_ref[h, :, cl] += ds
            ds_scr[i, h, :, cl] = ds.astype(BF16)
            p_scr[i, h, :, cl] = p.astype(BF16)
        if not whole:
            every = slice(0, WIN)
            for h in range(N_HEADS):
                lgn = jnp.where(b + 1 < nb, sn_scr[h] + b_ref[h, :, :WIN], -jnp.inf)
                pn = jnp.exp(lgn - of_head(lsen_ref, 0, every, h))
                dsn_scr[h] = (pn * (dpn_scr[h] - of_head(ddn_ref, 0, every, h))).astype(BF16)
                pn_scr[h] = pn.astype(BF16)
        for a, sl in enumerate(pairs):
            for c in range(ncls):
                q_eo = [_pair_split(q_ref[c, blk[t], sl]) for t in range(nsub)]
                do_eo = [_pair_split(do_ref[c, blk[t], sl].astype(BF16)) for t in range(nsub)]
                if not whole:
                    q_eo.append(_pair_split(qn_ref[0, :, sl]))
                    do_eo.append(_pair_split(don_ref[0, :, sl].astype(BF16)))
                for t in range(nsub):
                    i = c * nsub + t
                    k_eo = _pair_split(keys(kp_ref, kc_ref, c, t, sl))
                    dq, dk, dv = None, None, None
                    for e in range(2):
                        h = 2 * a + e
                        terms = [_dot(ds_scr[i, h, :, cols(t)], k_eo[e]),
                                 _dot_tn(ds_scr[i, h, :, WIN:], q_eo[t][e]),
                                 _dot_tn(p_scr[i, h, :, WIN:], do_eo[t][e])]
                        if t + 1 < nsub or not whole:
                            ds_next = ds_scr[i + 1, h, :, :WIN] if t + 1 < nsub else dsn_scr[h]
                            p_next = p_scr[i + 1, h, :, :WIN] if t + 1 < nsub else pn_scr[h]
                            terms[1] += _dot_tn(ds_next, q_eo[t + 1][e])
                            terms[2] += _dot_tn(p_next, do_eo[t + 1][e])
                        dq, dk, dv = terms if e == 0 else (dq + terms[0], dk + terms[1], dv + terms[2])
                    dq_ref[c, blk[t], sl] = dq.astype(BF16)
                    dk_ref[c, blk[t], sl] = dk.astype(BF16)
                    dv_ref[c, blk[t], sl] = dv.astype(BF16)

    cur = pl.BlockSpec((ncls, nsub * WIN, 512), lambda r, b: (r, b, 0))
    prev = pl.BlockSpec((ncls, WIN, 512), lambda r, b: (r, jnp.maximum(nsub * b - 1, 0), 0))
    nxt = pl.BlockSpec((ncls, WIN, 512), lambda r, b: (r, jnp.minimum(nsub * b + nsub, n128 - 1), 0))
    wide, narrow = (ncls * nsub, N_HEADS, WIN, 2 * WIN), (N_HEADS, WIN, WIN)
    return pl.pallas_call(
        body, name=f"swa_bwd_d{dil}", grid=(dil // ncls, nb),
        out_shape=[jax.ShapeDtypeStruct(qc.shape, BF16)] * 3 + [jax.ShapeDtypeStruct(bias.shape, F32)],
        in_specs=[cur, nxt, prev, cur, prev, cur, cur, nxt, cur, nxt, cur, nxt, _full(bias.shape), ANY_SPEC],
        out_specs=[cur] * 3 + [_full(bias.shape)],
        scratch_shapes=[pltpu.VMEM(wide, F32), pltpu.VMEM(wide, F32), pltpu.VMEM(narrow, F32),
                        pltpu.VMEM(narrow, F32), pltpu.VMEM(wide, BF16), pltpu.VMEM(wide, BF16),
                        pltpu.VMEM(narrow, BF16), pltpu.VMEM(narrow, BF16)],
        compiler_params=_cparams(2),
    )(qc, qc, kc, kc, vc, vc, doc, doc, lsec, lsec, ddc, ddc, bias, dep)


def _in_proj_bwd(dqs, dks, dvs, dgb, dcv, gc, xi, w_sc, w_in_g, x, g_mix, dx1):
    s = x.shape[0]
    tb = TM // SUBLANES
    last = s // SUBLANES - 1
    n_tiles = s // TM

    def body(dq1, dq4, dq16, dk1, dk4, dk16, dv1, dv4, dv16, dgb_ref, dcv_ref, dcvn_ref, gc_ref, xi_ref, wsc_ref,
             w_hbm, x_ref, g_ref, dx1_ref, dproj_ref, gx_ref, dg_ref, scr_a, scr_b, w_scr, w_sems):
        i = pl.program_id(0)
        _load_w_in_pairs(w_hbm, w_scr, w_sems)

        @pl.when(i == 0)
        def _():
            dg_ref[...] = jnp.zeros_like(dg_ref)

        d0 = dcv_ref[...]
        dn = jnp.where(i < n_tiles - 1, dcvn_ref[...], 0.0)
        du = (d0 * wsc_ref[2:3, :] + _shift_up(d0, dn, 1) * wsc_ref[1:2, :]) + _shift_up(d0, dn, 2) * wsc_ref[0:1, :]
        merge = lambda a, b4, b16: ((a[...].astype(F32) + _gather_classes(b4, scr_a, 4))
                                    + _gather_classes(b16, scr_b, 16))
        dq = merge(dq1, dq4, dq16) * (HEAD_DIM ** -0.5)
        dk = merge(dk1, dk4, dk16)
        dv = merge(dv1, dv4, dv16)
        dproj = jnp.concatenate([dq, dk, dv, dgb_ref[...].astype(F32), du * xi_ref[...], du * gc_ref[...]],
                                axis=1).astype(BF16)
        dproj_ref[...] = dproj
        dh = jnp.zeros((TM, D_MODEL), F32)
        for j in range(N_DEV // 2):
            dh = dh + _dot_nt(dproj[:, 2 * j * IN_CHUNK:2 * (j + 1) * IN_CHUNK], w_scr[j])
        xh, r = _rms(x_ref[...])
        dg_ref[0:1, :] += jnp.sum(dh * xh, axis=0, keepdims=True)
        gx_ref[...] = dx1_ref[...] + _rms_bwd(xh, r, g_ref[...], dh)

    row = lambda n: pl.BlockSpec((TM, n), lambda i: (i, 0))
    nxt = pl.BlockSpec((SUBLANES, 512), lambda i: (jnp.minimum((i + 1) * tb, last), 0))
    return pl.pallas_call(
        body, name="in_proj_bwd", grid=(n_tiles,),
        out_shape=[jax.ShapeDtypeStruct((s, IN_COLS), BF16), jax.ShapeDtypeStruct((s, D_MODEL), F32),
                   jax.ShapeDtypeStruct((SUBLANES, D_MODEL), F32)],
        in_specs=_class_specs(512) * 3 + [row(512), row(512), nxt, row(512), row(512), _full(w_sc.shape),
                                          ANY_SPEC, row(D_MODEL), _full(g_mix.shape), row(D_MODEL)],
        out_specs=[row(IN_COLS), row(D_MODEL), _full((SUBLANES, D_MODEL))],
        scratch_shapes=[pltpu.VMEM((512 // LANES, TM, LANES), F32)] * 2 + W_IN_PAIRS,
        compiler_params=_cparams(1),
    )(*dqs, *dks, *dvs, dgb, dcv, dcv, gc, xi, w_sc, w_in_g, x, g_mix, dx1)


def _dw(a, b, dep, name, a_chunked=False, b_chunked=False, n_chunks=1, chunk_cols=None, per_step=1):
    single = not (a_chunked or b_chunked or chunk_cols)
    wide = a_chunked and a.shape[2] > D_MODEL
    ts = TS_DW // 2 if single or wide else TS_DW
    if a_chunked:
        nj, s, kk = a.shape
        nn = b.shape[1]
        a_spec = pl.BlockSpec((1, ts, kk), lambda j, t: (j, t, 0))
        b_spec = pl.BlockSpec((ts, nn), lambda j, t: (t, 0))
    elif b_chunked:
        nj, s, nn = b.shape
        kk = a.shape[1]
        a_spec = pl.BlockSpec((ts, kk), lambda j, t: (t, 0))
        b_spec = pl.BlockSpec((1, ts, nn), lambda j, t: (j, t, 0))
    else:
        s, kk = a.shape
        nj, nn = (n_chunks // per_step, chunk_cols * per_step) if chunk_cols else (1, b.shape[1])
        a_spec = pl.BlockSpec((ts, kk), lambda j, t: (t, 0))
        b_spec = pl.BlockSpec((ts, nn), lambda j, t: (t, j))
    n_steps = s // ts

    def body(a_ref, b_ref, dep_ref, o_ref, acc):
        t = pl.program_id(1)

        @pl.when(t == 0)
        def _():
            acc[...] = jnp.zeros_like(acc)

        av = (a_ref[0] if a_chunked else a_ref[...]).astype(BF16)
        bv = (b_ref[0] if b_chunked else b_ref[...]).astype(BF16)
        acc[...] += _dot_tn(av, bv)

        @pl.when(t == n_steps - 1)
        def _():
            for q in range(per_step):
                o_ref[q] = acc[:, q * nn // per_step:(q + 1) * nn // per_step].astype(BF16)

    return pl.pallas_call(
        body, name=name, grid=(nj, n_steps),
        out_shape=jax.ShapeDtypeStruct((nj * per_step, kk, nn // per_step), BF16),
        in_specs=[a_spec, b_spec, ANY_SPEC],
        out_specs=pl.BlockSpec((per_step, kk, nn // per_step), lambda j, t: (j, 0, 0)),
        scratch_shapes=[pltpu.VMEM((kk, nn), F32)],
        compiler_params=_cparams(2),
    )(a, b, dep)


def _adamw_math(w, g, m, v):
    m2 = ADAM_B1 * m + (1.0 - ADAM_B1) * g
    v2 = ADAM_B2 * v + (1.0 - ADAM_B2) * (g * g)
    m_hat = m2 / (1.0 - ADAM_B1 ** ADAM_STEP)
    v_hat = v2 / (1.0 - ADAM_B2 ** ADAM_STEP)
    delta = -ADAM_LR * (m_hat / (jnp.sqrt(v_hat) + ADAM_EPS) + ADAM_WD * w)
    return delta, m2, v2


def _sum_parts(me, own, p_ref):
    g = None
    for i in range(N_DEV):
        part = jnp.where(me == i, own.astype(F32), p_ref[i].astype(F32))
        g = part if g is None else g + part
    return g


def _adamw_big(name, w, sent, parts, m, v, me_arr):
    rr, cc = w.shape
    tr = rr // 4 if rr >= 512 else rr

    def body(me_ref, w_ref, own_ref, p_ref, m_ref, v_ref, g_ref, d_ref, nm_ref, nv_ref):
        g = own_ref[0].astype(F32)
        for k in range(1, N_DEV):
            g = g + p_ref[(me_ref[0] + k) % N_DEV].astype(F32)
        g_ref[...] = g
        d_ref[...], nm_ref[...], nv_ref[...] = _adamw_math(w_ref[...], g, m_ref[...], v_ref[...])

    row = pl.BlockSpec((tr, cc), lambda i, me: (i, 0))
    return pl.pallas_call(
        body, name=name,
        grid_spec=pltpu.PrefetchScalarGridSpec(
            num_scalar_prefetch=1, grid=(rr // tr,),
            in_specs=[row, pl.BlockSpec((1, tr, cc), lambda i, me: (me[0], i, 0)),
                      pl.BlockSpec((N_DEV, tr, cc), lambda i, me: (0, i, 0)), row, row],
            out_specs=[row] * 4),
        out_shape=[jax.ShapeDtypeStruct((rr, cc), F32)] * 4,
        compiler_params=_cparams(1),
    )(me_arr, w, sent, parts, m, v)


def _small_slices():
    return [
        (slice(ROW_RELB, ROW_RELB + 8), slice(0, N_BUCKETS)),
        (slice(ROW_GMIX, ROW_GMIX + 1), slice(0, D_MODEL)),
        (slice(ROW_GAC, ROW_GAC + 1), slice(0, ATTN_W)),
        (slice(ROW_GAC, ROW_GAC + 1), slice(ATTN_W, D_MODEL)),
        (slice(ROW_GXATTN, ROW_GXATTN + 1), slice(0, D_MODEL)),
        (slice(ROW_GMEM, ROW_GMEM + 1), slice(0, D_MODEL)),
        (slice(ROW_GFFN, ROW_GFFN + 1), slice(0, D_MODEL)),
        (slice(ROW_BFC, ROW_BFC + 8), slice(0, UP_CHUNK)),
        (slice(ROW_GFINAL, ROW_GFINAL + 1), slice(0, D_MODEL)),
    ]


def _adamw_small(own, parts, wmv, me_arr):
    slices = _small_slices()
    n = len(slices)

    def body(*refs):
        me_ref, own_ref, p_ref = refs[:3]
        ins = refs[3:3 + 3 * n]
        g_ref = refs[3 + 3 * n]
        outs = refs[4 + 3 * n:]
        g = _sum_parts(me_ref[0], own_ref[...], p_ref)
        g_ref[...] = g
        for a, (rs, ls) in enumerate(slices):
            ga = g[rs, ls]
            outs[4 * a][...] = ga
            outs[4 * a + 1][...], outs[4 * a + 2][...], outs[4 * a + 3][...] = _adamw_math(
                ins[3 * a][...], ga, ins[3 * a + 1][...], ins[3 * a + 2][...])

    vm = pl.BlockSpec(memory_space=pltpu.VMEM)
    flat = [t for trip in wmv for t in trip]
    out_shape = [jax.ShapeDtypeStruct((SMALL_ROWS, D_MODEL), F32)]
    for w, _, _ in wmv:
        out_shape += [jax.ShapeDtypeStruct(w.shape, F32)] * 4
    res = pl.pallas_call(
        body, name="adamw_small", out_shape=out_shape,
        in_specs=[SMEM_SPEC] + [vm] * (2 + 3 * n), out_specs=[vm] * len(out_shape),
    )(me_arr, own, parts, *flat)
    return res[0], [res[1 + 4 * a:5 + 4 * a] for a in range(n)]


def _adamw_shards(items):
    n = len(items)

    def body(*refs):
        for a in range(n):
            w_ref, g_ref, m_ref, v_ref = refs[4 * a:4 * a + 4]
            d_ref, nm_ref, nv_ref = refs[4 * n + 3 * a:4 * n + 3 * a + 3]
            d_ref[...], nm_ref[...], nv_ref[...] = _adamw_math(w_ref[...], g_ref[...], m_ref[...], v_ref[...])

    vm = pl.BlockSpec(memory_space=pltpu.VMEM)
    out_shape = []
    for w, _, _, _ in items:
        out_shape += [jax.ShapeDtypeStruct(w.shape, F32)] * 3
    res = pl.pallas_call(
        body, name="adamw_shards", out_shape=out_shape, in_specs=[vm] * (4 * n), out_specs=[vm] * (3 * n),
    )(*[t for it in items for t in it])
    return [res[3 * a:3 * a + 3] for a in range(n)]


def _mesh_pos():
    return lax.axis_index("x"), lax.axis_index("y"), lax.axis_index("c")


def _dev_index(p):
    return 4 * p[0] + 2 * p[1] + p[2]


def _all_gather(shards):
    n = len(shards)

    def body(*refs):
        ins, outs = refs[:n], refs[n:2 * n]
        send_sems, recv_sems, loc_sems = refs[2 * n:]
        x, y, c = _mesh_pos()
        me, sib = (x, y, c), (x, y, 1 - c)
        chips = [(1 - x, y), (x, 1 - y), (1 - x, 1 - y)]

        def cp(a, k, block, to, src=None):
            dst = outs[a].at[_dev_index(block)]
            return pltpu.make_async_remote_copy(
                src_ref=dst if src is None else src, dst_ref=dst, send_sem=send_sems.at[a, k],
                recv_sem=recv_sems.at[a, k], device_id=to, device_id_type=MESH)

        mine = [pltpu.make_async_copy(ins[a], outs[a].at[_dev_index(me)], loc_sems.at[a]) for a in range(n)]
        for m_ in mine:
            m_.start()
        first = []
        for a in range(n):
            first.append(cp(a, 0, me, sib, src=ins[a]))
            first += [cp(a, 1 + j, me, (*chip, c), src=ins[a]) for j, chip in enumerate(chips)]
        for f in first:
            f.start()
        passed = []
        for a in range(n):
            for j, chip in enumerate(chips):
                cp(a, 1 + j, (*chip, c), me).wait_recv()
                fwd = cp(a, 4 + j, (*chip, c), sib)
                fwd.start()
                passed.append(fwd)
        for a in range(n):
            cp(a, 0, sib, me).wait_recv()
            for j, chip in enumerate(chips):
                cp(a, 4 + j, (*chip, 1 - c), me).wait_recv()
        for f in first + passed:
            f.wait_send()
        for m_ in mine:
            m_.wait()

    hbm = pl.BlockSpec(memory_space=pltpu.HBM)
    return pl.pallas_call(
        body, name="all_gather_weights",
        out_shape=[jax.ShapeDtypeStruct((N_DEV,) + a.shape, a.dtype) for a in shards],
        in_specs=[hbm] * n, out_specs=[hbm] * n,
        scratch_shapes=[pltpu.SemaphoreType.DMA((n, 7)), pltpu.SemaphoreType.DMA((n, 7)),
                        pltpu.SemaphoreType.DMA((n,))],
    )(*shards)


def _peers():
    x, y, c = _mesh_pos()
    return (x, y, c), [((1 - x) if k & 4 else x, (1 - y) if k & 2 else y, (1 - c) if k & 1 else c)
                       for k in range(1, 8)]


def _exchange_copy(src_ref, land_ref, whole, send_sems, recv_sems, a, k, peer, slot):
    src = src_ref if whole else src_ref.at[_dev_index(peer)]
    return pltpu.make_async_remote_copy(
        src_ref=src, dst_ref=land_ref.at[slot], send_sem=send_sems.at[7 * a + k], recv_sem=recv_sems.at[7 * a + k],
        device_id=peer, device_id_type=MESH)


def _exchange_start(name, srcs, whole, dep):
    n = len(srcs)
    lands = [lax.empty(((N_DEV,) + s.shape) if w else s.shape, s.dtype) for s, w in zip(srcs, whole)]

    def body(*refs):
        src_refs, land_refs = refs[:n], refs[n:2 * n]
        send_sems, recv_sems, token = refs[2 * n + 1], refs[2 * n + 2], refs[-1]
        me, peers = _peers()
        for a in range(n):
            for k, peer in enumerate(peers):
                _exchange_copy(src_refs[a], land_refs[a], whole[a], send_sems, recv_sems, a, k, peer,
                               _dev_index(me)).start()
        token[...] = jnp.zeros_like(token)

    res = pl.pallas_call(
        body, name=name,
        out_shape=(pltpu.SemaphoreType.DMA((7 * n,)), pltpu.SemaphoreType.DMA((7 * n,)),
                   *[pltpu.HBM(a.shape, a.dtype) for a in srcs], *[pltpu.HBM(a.shape, a.dtype) for a in lands],
                   jax.ShapeDtypeStruct((SUBLANES, 128), F32)),
        in_specs=[HBM_SPEC] * (2 * n) + [ANY_SPEC],
        out_specs=(SEM_SPEC, SEM_SPEC, *([HBM_SPEC] * (2 * n)), VMEM_SPEC),
        input_output_aliases={i: 2 + i for i in range(2 * n)},
        compiler_params=pltpu.CompilerParams(has_side_effects=DATAFLOW),
    )(*[pltpu.with_memory_space_constraint(a, pltpu.HBM) for a in srcs],
      *[pltpu.with_memory_space_constraint(a, pltpu.HBM) for a in lands], dep)
    return res[0], res[1], list(res[2:2 + n]), list(res[2 + n:2 + 2 * n]), res[-1]


def _exchange_wait(name, started, whole, after, which=None):
    send_sems, recv_sems, srcs, lands, _ = started
    which = list(range(len(srcs))) if which is None else which
    srcs, lands = [srcs[a] for a in which], [lands[a] for a in which]
    n = len(srcs)

    def body(*refs):
        src_refs, land_refs = refs[:n], refs[n:2 * n]
        send_sems, recv_sems = refs[2 * n], refs[2 * n + 1]
        _, peers = _peers()
        for i, a in enumerate(which):
            for k, peer in enumerate(peers):
                cp = _exchange_copy(src_refs[i], land_refs[i], whole[a], send_sems, recv_sems, a, k, peer,
                                    _dev_index(peer))
                cp.wait_send()
                cp.wait_recv()

    res = pl.pallas_call(
        body, name=name,
        out_shape=[pltpu.HBM(a.shape, a.dtype) for a in srcs + lands],
        in_specs=[HBM_SPEC] * (2 * n) + [SEM_SPEC, SEM_SPEC, ANY_SPEC],
        out_specs=[HBM_SPEC] * (2 * n),
        input_output_aliases={i: i for i in range(2 * n)},
        compiler_params=pltpu.CompilerParams(has_side_effects=DATAFLOW),
    )(*srcs, *lands, send_sems, recv_sems, after)
    return list(res[:n]), list(res[n:])


def _gather_start(name, shards, dep):
    n = len(shards)
    lands = [lax.empty((N_DEV,) + a.shape, a.dtype) for a in shards]

    def body(*refs):
        src_refs, land_refs = refs[:n], refs[n:2 * n]
        send_sems, recv_sems, token = refs[2 * n + 1], refs[2 * n + 2], refs[-1]
        x, y, c = _mesh_pos()
        peers = [(x, y, 1 - c), (1 - x, y, c), (x, 1 - y, c), (1 - x, 1 - y, c)]
        for a in range(n):
            for k, peer in enumerate(peers):
                pltpu.make_async_remote_copy(
                    src_ref=src_refs[a], dst_ref=land_refs[a].at[_dev_index((x, y, c))], send_sem=send_sems.at[4 * a + k],
                    recv_sem=recv_sems.at[4 * a + k], device_id=peer, device_id_type=MESH).start()
        token[...] = jnp.zeros_like(token)

    res = pl.pallas_call(
        body, name=name,
        out_shape=(pltpu.SemaphoreType.DMA((4 * n,)), pltpu.SemaphoreType.DMA((4 * n,)),
                   *[pltpu.HBM(a.shape, a.dtype) for a in shards], *[pltpu.HBM(a.shape, a.dtype) for a in lands],
                   jax.ShapeDtypeStruct((SUBLANES, 128), F32)),
        in_specs=[HBM_SPEC] * (2 * n) + [ANY_SPEC],
        out_specs=(SEM_SPEC, SEM_SPEC, *([HBM_SPEC] * (2 * n)), VMEM_SPEC),
        input_output_aliases={i: 2 + i for i in range(2 * n)},
        compiler_params=pltpu.CompilerParams(has_side_effects=DATAFLOW),
    )(*[pltpu.with_memory_space_constraint(a, pltpu.HBM) for a in shards],
      *[pltpu.with_memory_space_constraint(a, pltpu.HBM) for a in lands], dep)
    return res[0], res[1], list(res[2:2 + n]), list(res[2 + n:2 + 2 * n]), res[-1]


def _gather_forward(name, send_sems, recv_sems, lands, which, after):
    n = len(which)

    def body(*refs):
        land_refs = refs[:n]
        send_sems, recv_sems = refs[n], refs[n + 1]
        fsend, frecv, token = refs[n + 3], refs[n + 4], refs[-1]
        x, y, c = _mesh_pos()
        chips = [(1 - x, y), (x, 1 - y), (1 - x, 1 - y)]
        for i, a in enumerate(which):
            for j, chip in enumerate(chips):
                block = land_refs[i].at[_dev_index((*chip, c))]
                pltpu.make_async_remote_copy(
                    src_ref=block, dst_ref=block, send_sem=send_sems.at[4 * a + 1 + j], recv_sem=recv_sems.at[4 * a + 1 + j],
                    device_id=(*chip, c), device_id_type=MESH).wait_recv()
                pltpu.make_async_remote_copy(
                    src_ref=block, dst_ref=block, send_sem=fsend.at[3 * i + j], recv_sem=frecv.at[3 * i + j],
                    device_id=(x, y, 1 - c), device_id_type=MESH).start()
        token[...] = jnp.zeros_like(token)

    res = pl.pallas_call(
        body, name=name,
        out_shape=(pltpu.SemaphoreType.DMA((3 * n,)), pltpu.SemaphoreType.DMA((3 * n,)),
                   *[pltpu.HBM(a.shape, a.dtype) for a in lands], jax.ShapeDtypeStruct((SUBLANES, 128), F32)),
        in_specs=[HBM_SPEC] * n + [SEM_SPEC, SEM_SPEC, ANY_SPEC],
        out_specs=(SEM_SPEC, SEM_SPEC, *([HBM_SPEC] * n), VMEM_SPEC),
        input_output_aliases={i: 2 + i for i in range(n)},
        compiler_params=pltpu.CompilerParams(has_side_effects=DATAFLOW),
    )(*lands, send_sems, recv_sems, after)
    return res[0], res[1], list(res[2:2 + n]), res[-1]


def _gather_wait(name, send_sems, recv_sems, fsend, frecv, srcs, lands, which, after):
    n = len(which)

    def body(*refs):
        land_refs = refs[n:2 * n]
        send_sems, recv_sems, fsend, frecv = refs[2 * n:2 * n + 4]
        x, y, c = _mesh_pos()
        sib = (x, y, 1 - c)
        chips = [(1 - x, y), (x, 1 - y), (1 - x, 1 - y)]
        for i, a in enumerate(which):
            def cp(slot, ssem, rsem):
                block = land_refs[i].at[_dev_index(slot)]
                return pltpu.make_async_remote_copy(src_ref=block, dst_ref=block, send_sem=ssem, recv_sem=rsem,
                                                    device_id=sib, device_id_type=MESH)
            cp(sib, send_sems.at[4 * a], recv_sems.at[4 * a]).wait_recv()
            for j, chip in enumerate(chips):
                cp((*chip, 1 - c), fsend.at[3 * i + j], frecv.at[3 * i + j]).wait_recv()
            for k in range(4):
                cp(sib, send_sems.at[4 * a + k], recv_sems.at[4 * a + k]).wait_send()
            for j in range(3):
                cp(sib, fsend.at[3 * i + j], frecv.at[3 * i + j]).wait_send()

    res = pl.pallas_call(
        body, name=name,
        out_shape=[pltpu.HBM(a.shape, a.dtype) for a in srcs + lands],
        in_specs=[HBM_SPEC] * (2 * n) + [SEM_SPEC] * 4 + [ANY_SPEC],
        out_specs=[HBM_SPEC] * (2 * n),
        input_output_aliases={i: i for i in range(2 * n)},
        compiler_params=pltpu.CompilerParams(has_side_effects=DATAFLOW),
    )(*srcs, *lands, send_sems, recv_sems, fsend, frecv, after)
    return list(res[n:])


def _local_step(x, mem, target, rel_bias, g_mix, w_in_g, w_sc, g_a, g_c, g_xattn, g_mem, g_ffn, w_fc, b_fc, g_final,
                dep, forward_weights, late_weights, emit, emit_small):
    s = x.shape[0]
    buckets = _bucket_tables()
    bias = _bias_fwd(rel_bias, buckets)

    h1, qs, ks, vs, gb, gc, xi = _rms_proj(x, g_mix, w_in_g, dep)
    qs, ks, vs = ([a[0][None]] + list(a[1:]) for a in (qs, ks, vs))
    group1, group2 = ["w_out", "w_xq", "w_xk", "w_xv", "w_xo"], ["w_up", "w_down"]
    tok = forward_weights(group1, h1)
    branches = []
    for p, dil in enumerate(DILATIONS):
        o_p, lse_p = _swa_fwd(qs[p], ks[p], vs[p], bias[p], dil, tok)
        branches.append([o_p[0], lse_p[0]] if dil == 1 else [o_p, lse_p])
    lw = late_weights(group1, branches[-1][0])
    w_out, w_xq, w_xk, w_xv, w_xo = (lw[n] for n in group1)
    attn, lses, mixed, x1 = _mix_out(branches, gb, gc, xi, x, w_sc, g_a, g_c, w_out)
    tok = forward_weights(group2, x1)
    mem_n, mk, mv = _mem_kv(mem, g_mem, w_xk, w_xv)
    h2, xq, xo, x2 = _xattn_fwd(x1, g_xattn, w_xq, mk, mv, w_xo, tok)
    lw = late_weights(group2, x2)
    w_up_g = lw["w_up"].reshape(FFN_CHUNKS, FFN_WIDTH, D_MODEL)
    w_down_g = lw["w_down"].reshape(FFN_CHUNKS // 2, FFN_WIDTH, D_MODEL)
    pairs = lambda a: a.reshape(FFN_CHUNKS, 2, a.shape[1], UP_CHUNK).transpose(0, 2, 1, 3).reshape(
        FFN_CHUNKS, a.shape[1], FFN_WIDTH)
    w_fc, b_fc = pairs(w_fc), pairs(b_fc)
    h3, up, conv, act, dx3, loss_acc, dg_final = _ffn_fwd(x2, g_ffn, w_up_g, w_fc, b_fc, w_down_g, g_final, target)

    gw_down = _dw(act, dx3, dep, "dw_down", a_chunked=True).reshape(N_DEV // 2, UP_CHUNK, D_MODEL)
    dup, dx2, dg_ffn, dw_fc, db_fc = _ffn_bwd(dx3, up, conv, x2, g_ffn, w_up_g, w_fc, w_down_g)
    gw_up = _dw(dup, h3, dep, "dw_up", a_chunked=True).reshape(N_DEV, UP_CHUNK, D_MODEL)
    tok = emit(dict(w_down=gw_down, w_up=gw_up))
    dxq, dx1, dmk, dmv, dg_xattn = _xattn_bwd(dx2, xo, xq, mk, mv, w_xo, w_xq, x1, g_xattn, tok)
    gw_xo = _dw(xo, dx2, tok, "dw_xo")[0]
    gw_xq = _dw(h2, dxq, tok, "dw_xq")[0]
    gw_xk, gw_xv, dg_mem = _mem_kv_bwd(dmk, dmv, mem_n, mem, w_xk, w_xv)
    tok = emit(dict(w_xo=gw_xo, w_xq=gw_xq, w_xk=gw_xk, w_xv=gw_xv))
    dattns, dds, dgb, dcv, dg_a, dg_c, dw_sc = _mix_out_bwd(dx1, w_out, attn, gb, gc, xi, w_sc, g_a, g_c, tok)
    first = lambda a: [a[0][None]] + list(a[1:])
    dattns, dds, lses = first(dattns), first(dds), first(lses)
    gw_out = _dw(mixed, dx1, tok, "dw_out")[0]
    tok = emit(dict(w_out=gw_out))
    dqs, dks, dvs, dbias = [], [], [], []
    for p, dil in enumerate(DILATIONS):
        dq_p, dk_p, dv_p, db_p = _swa_bwd(qs[p], ks[p], vs[p], dattns[p], lses[p], dds[p], bias[p], dil, tok)
        dqs.append(dq_p[0] if dil == 1 else dq_p)
        dks.append(dk_p[0] if dil == 1 else dk_p)
        dvs.append(dv_p[0] if dil == 1 else dv_p)
        dbias.append(db_p)
    d_relb = _bias_bwd(jnp.stack(dbias), buckets)
    dproj, grad_x, dg_mix = _in_proj_bwd(dqs, dks, dvs, dgb, dcv, gc, xi, w_sc, w_in_g, x, g_mix, dx1)
    pad = lambda a: jnp.pad(a, ((0, 0), (0, D_MODEL - a.shape[1])))
    small = jnp.concatenate([
        d_relb, dg_mix, dg_xattn, dg_mem, dg_ffn, dg_final, jnp.concatenate([dg_a, dg_c], axis=1),
        pad(dw_sc), pad(db_fc.reshape(N_DEV, UP_CHUNK)), pad(dw_fc.reshape(3 * N_DEV, UP_CHUNK)), pad(loss_acc)],
        axis=0)
    tok = emit_small(small)
    gw_in = _dw(h1, dproj, tok, "dw_in", n_chunks=N_DEV, chunk_cols=IN_CHUNK, per_step=2)
    emit(dict(w_in=gw_in))
    return grad_x


def kernel(x, mem, rel_bias, g_mix, w_in, w_short_conv, g_attn_out, g_conv_out, w_out, g_xattn, g_mem, w_xq, w_xk, w_xv, w_xo, g_ffn, w_up, w_ffn_conv, b_ffn_conv, w_down, g_final, loss_target, m_rel_bias, m_g_mix, m_w_in, m_w_short_conv, m_g_attn_out, m_g_conv_out, m_w_out, m_g_xattn, m_g_mem, m_w_xq, m_w_xk, m_w_xv, m_w_xo, m_g_ffn, m_w_up, m_w_ffn_conv, m_b_ffn_conv, m_w_down, m_g_final, v_rel_bias, v_g_mix, v_w_in, v_w_short_conv, v_g_attn_out, v_g_conv_out, v_w_out, v_g_xattn, v_g_mem, v_w_xq, v_w_xk, v_w_xv, v_w_xo, v_g_ffn, v_w_up, v_w_ffn_conv, v_b_ffn_conv, v_w_down, v_g_final):
    me = _dev_index(_mesh_pos())
    me_arr = me.reshape(1).astype(jnp.int32)

    big_names = ["w_in", "w_out", "w_xq", "w_xk", "w_xv", "w_xo", "w_up", "w_down"]
    late_names = big_names[1:]
    big_w = dict(w_in=w_in[0], w_out=w_out[0], w_xq=w_xq[0], w_xk=w_xk[0], w_xv=w_xv[0], w_xo=w_xo[0],
                 w_up=w_up[0].T, w_down=w_down[0])
    big_m = dict(w_in=m_w_in[0], w_out=m_w_out[0], w_xq=m_w_xq[0], w_xk=m_w_xk[0], w_xv=m_w_xv[0], w_xo=m_w_xo[0],
                 w_up=m_w_up[0].T, w_down=m_w_down[0])
    big_v = dict(w_in=v_w_in[0], w_out=v_w_out[0], w_xq=v_w_xq[0], w_xk=v_w_xk[0], w_xv=v_w_xv[0], w_xo=v_w_xo[0],
                 w_up=v_w_up[0].T, w_down=v_w_down[0])
    shard_shape = {n: big_w[n].shape for n in big_names}

    w_in_g, w_sc_g, w_fc_full = _all_gather([big_w["w_in"].astype(BF16), w_short_conv[0], w_ffn_conv[0]])
    w_sc_full = w_sc_g.transpose(1, 0, 2).reshape(3, CONV_W)
    late_shards = [big_w[n].astype(BF16) for n in late_names]
    ag_send, ag_recv, ag_srcs, ag_lands, ag_token = _gather_start("gather_weights_start", late_shards, w_in_g)
    forwarded = {}

    def forward_weights(names, after):
        which = [late_names.index(n) for n in names]
        fsend, frecv, lands, token = _gather_forward("gather_" + "_".join(names) + "_forward", ag_send, ag_recv,
                                                     [ag_lands[a] for a in which], which, after)
        forwarded[tuple(names)] = (fsend, frecv, lands)
        return token

    def late_weights(names, after):
        which = [late_names.index(n) for n in names]
        fsend, frecv, lands = forwarded[tuple(names)]
        lands = _gather_wait("gather_" + "_".join(names) + "_wait", ag_send, ag_recv, fsend, frecv,
                             [ag_srcs[a] for a in which], lands, which, after)
        out = {}
        for n, a, land in zip(names, which, lands):
            full = lax.dynamic_update_index_in_dim(land, late_shards[a], me, 0)
            if n == "w_up":
                out[n] = full
            elif n == "w_down":
                out[n] = full.reshape(N_DEV // 2, UP_CHUNK, D_MODEL)
            else:
                out[n] = full.reshape(D_MODEL, D_MODEL)
        return out

    sent = []

    def emit(grads):
        names = list(grads)
        blocks = [grads[n].reshape((N_DEV,) + shard_shape[n]) for n in names]
        started = _exchange_start("scatter_" + "_".join(names) + "_start", blocks, [False] * len(names), me_arr)
        sent.append((names, started))
        return started[-1]

    def emit_small(small):
        sent_small.append((small, _exchange_start("gather_small_start", [small], [True], me_arr)))
        return sent_small[0][1][-1]

    sent_small = []
    grad_x = _local_step(
        x[0], mem[0], loss_target[0], rel_bias, g_mix, w_in_g, w_sc_full, g_attn_out, g_conv_out, g_xattn, g_mem,
        g_ffn, w_fc_full, b_ffn_conv.reshape(N_DEV, 1, UP_CHUNK), g_final.reshape(1, D_MODEL), ag_token,
        forward_weights, late_weights, emit, emit_small)

    small_g, small_started = sent_small[0]
    after = sent[-1][1][-1]
    small_parts = _exchange_wait("gather_small_wait", small_started, [True], after)[1][0]
    big_out = {}
    after = small_parts
    for names, started in sent:
        blocks, lands = _exchange_wait("scatter_" + "_".join(names) + "_wait", started, [False] * len(names), after)
        for n, block, land in zip(names, blocks, lands):
            res = _adamw_big("adamw_" + n, big_w[n], block, land, big_m[n], big_v[n], me_arr)
            big_out[n] = [(r.T if n == "w_up" else r)[None] for r in res]
            after = res[0]

    as_rows = lambda a: a.reshape(N_DEV, UP_CHUNK)
    row1 = lambda a: a.reshape(1, D_MODEL)
    small_names = ["rel_bias", "g_mix", "g_attn_out", "g_conv_out", "g_xattn", "g_mem", "g_ffn", "b_ffn_conv", "g_final"]
    wmv = [
        (rel_bias, m_rel_bias, v_rel_bias), (g_mix, m_g_mix, v_g_mix), (g_attn_out, m_g_attn_out, v_g_attn_out),
        (g_conv_out, m_g_conv_out, v_g_conv_out), (g_xattn, m_g_xattn, v_g_xattn), (g_mem, m_g_mem, v_g_mem),
        (g_ffn, m_g_ffn, v_g_ffn), (as_rows(b_ffn_conv), as_rows(m_b_ffn_conv), as_rows(v_b_ffn_conv)),
        (row1(g_final), row1(m_g_final), row1(v_g_final))]
    g_packed, small_res = _adamw_small(small_g, small_parts, wmv, me_arr)
    small_out = dict(zip(small_names, small_res))
    loss = g_packed[ROW_LOSS, 0]
    small_out["b_ffn_conv"] = [a.reshape(1, 2 * D_FF) for a in small_out["b_ffn_conv"]]
    small_out["g_final"] = [a.reshape(D_MODEL) for a in small_out["g_final"]]

    g_wsc = lax.dynamic_slice(g_packed[ROW_WSC:ROW_WSC + 3, 0:CONV_W], (0, me * HEAD_DIM), (3, HEAD_DIM))
    g_wfc = lax.dynamic_slice(g_packed[ROW_WFC:ROW_WFC + 3 * N_DEV, 0:UP_CHUNK].reshape(3, N_DEV, UP_CHUNK),
                              (0, me, 0), (3, 1, UP_CHUNK)).reshape(3, UP_CHUNK)
    shard_res = _adamw_shards([(w_short_conv[0], g_wsc, m_w_short_conv[0], v_w_short_conv[0]),
                               (w_ffn_conv[0], g_wfc, m_w_ffn_conv[0], v_w_ffn_conv[0])])
    small_out["w_short_conv"] = [g_wsc[None]] + [a[None] for a in shard_res[0]]
    small_out["w_ffn_conv"] = [g_wfc[None]] + [a[None] for a in shard_res[1]]

    order = ["rel_bias", "g_mix", "w_in", "w_short_conv", "g_attn_out", "g_conv_out", "w_out", "g_xattn", "g_mem",
             "w_xq", "w_xk", "w_xv", "w_xo", "g_ffn", "w_up", "w_ffn_conv", "b_ffn_conv", "w_down", "g_final"]
    allp = {**big_out, **small_out}
    outs = [loss, grad_x[None]]
    for kind in range(4):
        outs += [allp[n][kind] for n in order]
    return tuple(outs)
```

```python
import math

import numpy as np
import jax
import jax.numpy as jnp
from jax import lax
from jax.experimental import pallas as pl
from jax.experimental.pallas import tpu as pltpu

F32 = jnp.float32
BF16 = jnp.bfloat16
MESH = pl.DeviceIdType.MESH

N_DEV = 8
D_MODEL = 1024
ATTN_W = 512
CONV_W = 512
N_HEADS = 8
HEAD_DIM = 64
WIN = 128
DILATIONS = (1, 4, 16)
N_BUCKETS = 32
BUCKET_MAX_EXACT = 16
BUCKET_MAX_DISTANCE = 2048
N_MEM_HEADS = 4
MEM_HEAD_DIM = 256
D_FF = 2816
IN_COLS = 3072
IN_CHUNK = IN_COLS // N_DEV
UP_CHUNK = 2 * D_FF // N_DEV
FFN_CHUNKS = 4
FFN_WIDTH = 2 * D_FF // FFN_CHUNKS
EPS = 1e-6

ADAM_LR = 0.001
ADAM_B1 = 0.9
ADAM_B2 = 0.999
ADAM_EPS = 1e-08
ADAM_WD = 0.01
ADAM_STEP = 10

SUBLANES = 8
LANES = 128
HALO = 16
TM = 512
TM_FFN = 256
TS_DW = 4096
SWA_BLOCKS = 8
VMEM_LIMIT = 56 * 1024 * 1024

ROW_RELB, ROW_GMIX, ROW_GXATTN, ROW_GMEM, ROW_GFFN, ROW_GFINAL, ROW_GAC = 0, 8, 16, 24, 32, 40, 48
ROW_WSC, ROW_BFC, ROW_WFC, ROW_LOSS, SMALL_ROWS = 56, 64, 72, 96, 104


def _cparams(n_grid):
    return pltpu.CompilerParams(dimension_semantics=("arbitrary",) * n_grid, vmem_limit_bytes=VMEM_LIMIT)


def _full(shape):
    nd = len(shape)
    return pl.BlockSpec(tuple(shape), lambda *_: (0,) * nd)


def _resident(shape):
    nd = len(shape)
    return pl.BlockSpec(tuple(shape), lambda *_: (0,) * nd, pipeline_mode=pl.Buffered(1))


ANY_SPEC = pl.BlockSpec(memory_space=pl.ANY)
HBM_SPEC = pl.BlockSpec(memory_space=pltpu.HBM)
SEM_SPEC = pl.BlockSpec(memory_space=pltpu.SEMAPHORE)
VMEM_SPEC = pl.BlockSpec(memory_space=pltpu.VMEM)
SMEM_SPEC = pl.BlockSpec(memory_space=pltpu.SMEM)
DATAFLOW = pltpu.SideEffectType.DATAFLOW_SIDE_EFFECTING


def _rms(x):
    r = lax.rsqrt(jnp.mean(x * x, axis=-1, keepdims=True) + EPS)
    return x * r, r


def _rms_bwd(xh, r, g, dy):
    dxh = dy * g
    return r * (dxh - xh * jnp.mean(dxh * xh, axis=-1, keepdims=True))


def _shift_down(u, halo, k):
    ru = pltpu.roll(u, k, 0)
    rh = pltpu.roll(halo, k, 0)
    row = lax.broadcasted_iota(jnp.int32, rh.shape, 0)
    head = jnp.where(row < k, rh, ru[0:SUBLANES])
    return jnp.concatenate([head, ru[SUBLANES:]], axis=0)


def _shift_up(u, halo, k):
    tm = u.shape[0]
    ru = pltpu.roll(u, tm - k, 0)
    rh = pltpu.roll(halo, SUBLANES - k, 0)
    row = lax.broadcasted_iota(jnp.int32, rh.shape, 0)
    tail = jnp.where(row >= SUBLANES - k, rh, ru[tm - SUBLANES:])
    return jnp.concatenate([ru[:tm - SUBLANES], tail], axis=0)


def _causal_conv3(u, halo, w_ref):
    return (_shift_down(u, halo, 2) * w_ref[0:1, :] + _shift_down(u, halo, 1) * w_ref[1:2, :]) + u * w_ref[2:3, :]


def _dot(a, b):
    return jnp.dot(a, b, preferred_element_type=F32)


def _dot_nt(a, b):
    return lax.dot_general(a, b, (((1,), (1,)), ((), ())), preferred_element_type=F32)


def _dot_tn(a, b):
    return lax.dot_general(a, b, (((0,), (0,)), ((), ())), preferred_element_type=F32)


def _sigmoid(x):
    return 0.5 * jnp.tanh(0.5 * x) + 0.5


def _bucket_tables():
    qi = np.arange(WIN)[:, None]
    kj = np.arange(2 * WIN)[None, :]
    steps = np.clip(qi + WIN - kj, 0, WIN)
    out = []
    for d in DILATIONS:
        dist = steps * d
        dd = np.maximum(dist, 1).astype(np.float32)
        large = BUCKET_MAX_EXACT + (
            np.log(dd / np.float32(BUCKET_MAX_EXACT)) / np.float32(math.log(BUCKET_MAX_DISTANCE / BUCKET_MAX_EXACT))
            * np.float32(N_BUCKETS - BUCKET_MAX_EXACT)).astype(np.int32)
        large = np.minimum(large, N_BUCKETS - 1)
        out.append(np.where(dist < BUCKET_MAX_EXACT, dist, large).astype(np.int32))
    return np.stack(out)


def _band_mask():
    qi = lax.broadcasted_iota(jnp.int32, (WIN, 2 * WIN), 0)
    kj = lax.broadcasted_iota(jnp.int32, (WIN, 2 * WIN), 1)
    steps = qi + WIN - kj
    return (steps >= 0) & (steps <= WIN)


def _bias_fwd(rel_bias, buckets):
    present = [sorted(set(buckets[p].ravel().tolist())) for p in range(3)]

    def body(rb_ref, bk_ref, o_ref):
        band = _band_mask()
        for p in range(3):
            bk = bk_ref[p]
            for h in range(N_HEADS):
                acc = jnp.zeros((WIN, 2 * WIN), F32)
                for b in present[p]:
                    acc = jnp.where(bk == b, rb_ref[h, b], acc)
                o_ref[p, h] = jnp.where(band, acc, -jnp.inf)

    return pl.pallas_call(
        body, name="bias_fwd",
        out_shape=jax.ShapeDtypeStruct((3, N_HEADS, WIN, 2 * WIN), F32),
        in_specs=[pl.BlockSpec(memory_space=pltpu.SMEM), pl.BlockSpec(memory_space=pltpu.VMEM)],
        out_specs=pl.BlockSpec(memory_space=pltpu.VMEM),
    )(rel_bias, jnp.asarray(buckets))


def _bias_bwd(dbias, buckets):
    present = [set(buckets[p].ravel().tolist()) for p in range(3)]

    def body(db_ref, bk_ref, o_ref):
        lane = lax.broadcasted_iota(jnp.int32, (1, D_MODEL), 1)
        rows = []
        for h in range(N_HEADS):
            row = jnp.zeros((1, D_MODEL), F32)
            for b in range(N_BUCKETS):
                tot = jnp.zeros((1, 1), F32)
                for p in (p for p in range(3) if b in present[p]):
                    sel = jnp.where(bk_ref[p] == b, db_ref[p, h], 0.0)
                    tot = tot + jnp.sum(jnp.sum(sel, axis=0, keepdims=True), axis=1, keepdims=True)
                row = jnp.where(lane == b, tot, row)
            rows.append(row)
        o_ref[...] = jnp.concatenate(rows, axis=0)

    return pl.pallas_call(
        body, name="bias_bwd",
        out_shape=jax.ShapeDtypeStruct((N_HEADS, D_MODEL), F32),
        in_specs=[pl.BlockSpec(memory_space=pltpu.VMEM), pl.BlockSpec(memory_space=pltpu.VMEM)],
        out_specs=pl.BlockSpec(memory_space=pltpu.VMEM),
    )(dbias, jnp.asarray(buckets))


def _spread(val, scr_ref, out_refs, dtype):
    out_refs[0][...] = val.astype(dtype)
    n_blk = val.shape[1] // LANES
    for c in range(n_blk):
        scr_ref[c] = val[:, c * LANES:(c + 1) * LANES]
    for o_ref, d in zip(out_refs[1:], DILATIONS[1:]):
        for r in range(d):
            for c in range(n_blk):
                o_ref[r, :, c * LANES:(c + 1) * LANES] = scr_ref.at[c][pl.ds(r, TM // d, stride=d), :].astype(dtype)


def _gather_classes(blk_ref, scr_ref, d):
    n_blk = blk_ref.shape[2] // LANES
    for r in range(d):
        for c in range(n_blk):
            scr_ref.at[c][pl.ds(r, TM // d, stride=d), :] = blk_ref[r, :, c * LANES:(c + 1) * LANES].astype(F32)
    return jnp.concatenate([scr_ref[c] for c in range(n_blk)], axis=1)


def _class_specs(cols):
    return [pl.BlockSpec((TM, cols), lambda i: (i, 0))] + [
        pl.BlockSpec((d, TM // d, cols), lambda i: (0, i, 0)) for d in DILATIONS[1:]]


def _class_shapes(s, cols, dtype):
    return [jax.ShapeDtypeStruct((s, cols), dtype)] + [
        jax.ShapeDtypeStruct((d, s // d, cols), dtype) for d in DILATIONS[1:]]


def _load_w_in_pairs(w_hbm, w_scr, sems):
    @pl.when(pl.program_id(0) == 0)
    def _():
        copies = [pltpu.make_async_copy(w_hbm.at[j], w_scr.at[j // 2, :, pl.ds((j % 2) * IN_CHUNK, IN_CHUNK)],
                                        sems.at[j]) for j in range(N_DEV)]
        for copy in copies:
            copy.start()
        for copy in copies:
            copy.wait()


W_IN_PAIRS = [pltpu.VMEM((N_DEV // 2, D_MODEL, 2 * IN_CHUNK), BF16), pltpu.SemaphoreType.DMA((N_DEV,))]


def _rms_proj(x, g_mix, w_in_g, dep):
    s = x.shape[0]

    def body(x_ref, g_ref, w_hbm, dep_ref, h_ref, q1, q4, q16, k1, k4, k16, v1, v4, v16, gb_ref, gc_ref, xi_ref, scr,
             w_scr, w_sems):
        _load_w_in_pairs(w_hbm, w_scr, w_sems)
        xh, _ = _rms(x_ref[...])
        h = (xh * g_ref[...]).astype(BF16)
        h_ref[...] = h
        proj = jnp.concatenate([_dot(h, w_scr[j]) for j in range(N_DEV // 2)], axis=1)
        _spread(proj[:, 0:512] * (HEAD_DIM ** -0.5), scr, (q1, q4, q16), BF16)
        _spread(proj[:, 512:1024], scr, (k1, k4, k16), BF16)
        _spread(proj[:, 1024:1536], scr, (v1, v4, v16), BF16)
        gb_ref[...] = proj[:, 1536:2048]
        gc_ref[...] = proj[:, 2048:2560]
        xi_ref[...] = proj[:, 2560:3072]

    row = lambda n: pl.BlockSpec((TM, n), lambda i: (i, 0))
    res = pl.pallas_call(
        body, name="rms_proj", grid=(s // TM,),
        out_shape=[jax.ShapeDtypeStruct((s, D_MODEL), BF16)] + _class_shapes(s, 512, BF16) * 3
        + [jax.ShapeDtypeStruct((s, 512), F32)] * 3,
        in_specs=[row(D_MODEL), _full(g_mix.shape), ANY_SPEC, ANY_SPEC],
        out_specs=[row(D_MODEL)] + _class_specs(512) * 3 + [row(512)] * 3,
        scratch_shapes=[pltpu.VMEM((512 // LANES, TM, LANES), F32)] + W_IN_PAIRS,
        compiler_params=_cparams(1),
    )(x, g_mix, w_in_g, dep)
    return res[0], res[1:4], res[4:7], res[7:10], res[10], res[11], res[12]


def _pair_split(x2):
    lane = lax.broadcasted_iota(jnp.int32, x2.shape, 1)
    zero = jnp.zeros_like(x2)
    return jnp.where(lane < HEAD_DIM, x2, zero), jnp.where(lane >= HEAD_DIM, x2, zero)


def _pair_join(even, odd):
    lane = lax.broadcasted_iota(jnp.int32, (even.shape[0], LANES), 1)
    return jnp.where(lane < HEAD_DIM, even, odd)


def _swa_steps(qc, dil):
    n128 = qc.shape[1] // WIN
    nsub = min(SWA_BLOCKS, n128)
    nb = n128 // nsub
    ncls = min(dil, SWA_BLOCKS // nsub) if nb == 1 else 1
    return nsub, nb, ncls


def _swa_fwd(qc, kc, vc, bias, dil, dep):
    nsub, nb, ncls = _swa_steps(qc, dil)
    whole = nb == 1

    def body(q_ref, kp_ref, kc_ref, vp_ref, vc_ref, b_ref, dep_ref, o_ref, lse_ref, s_scr, p_scr):
        no_prev = (pl.program_id(1) == 0) & (lax.broadcasted_iota(jnp.int32, (WIN, 2 * WIN), 1) < WIN)
        pairs = [slice(a * LANES, (a + 1) * LANES) for a in range(N_HEADS // 2)]
        for c, t in [(c, t) for c in range(ncls) for t in range(nsub)]:
            i = c * nsub + t
            rows = slice(t * WIN, (t + 1) * WIN)
            alone = whole and t == 0
            cols = slice(WIN, 2 * WIN) if alone else slice(0, 2 * WIN)

            def keys(prev_ref, cur_ref, sl):
                if alone:
                    return cur_ref[c, rows, sl]
                if t == 0:
                    return jnp.concatenate([prev_ref[c, :, sl], cur_ref[c, rows, sl]], axis=0)
                return cur_ref[c, (t - 1) * WIN:(t + 1) * WIN, sl]

            for a, sl in enumerate(pairs):
                k2 = keys(kp_ref, kc_ref, sl)
                for e, qh in enumerate(_pair_split(q_ref[c, rows, sl])):
                    s_scr[i, 2 * a + e, :, cols] = _dot_nt(qh, k2)
            den, lse = [], []
            for h in range(N_HEADS):
                lg = s_scr[i, h, :, cols] + b_ref[h, :, cols]
                if t == 0 and not whole:
                    lg = jnp.where(no_prev, -jnp.inf, lg)
                m = jnp.max(lg, axis=-1, keepdims=True)
                p = jnp.exp(lg - m)
                den.append(jnp.sum(p, axis=-1, keepdims=True))
                p_scr[i, h, :, cols] = p.astype(BF16)
                lse.append(m + jnp.log(den[h]))
            for a, sl in enumerate(pairs):
                v_even, v_odd = _pair_split(keys(vp_ref, vc_ref, sl))
                o2 = _dot(p_scr[i, 2 * a, :, cols], v_even) + _dot(p_scr[i, 2 * a + 1, :, cols], v_odd)
                o_ref[c, rows, sl] = o2 / _pair_join(den[2 * a], den[2 * a + 1])
                lse_ref[c, rows, sl] = _pair_join(lse[2 * a], lse[2 * a + 1])

    cur = pl.BlockSpec((ncls, nsub * WIN, 512), lambda r, b: (r, b, 0))
    prev = pl.BlockSpec((ncls, WIN, 512), lambda r, b: (r, jnp.maximum(nsub * b - 1, 0), 0))
    wide = (ncls * nsub, N_HEADS, WIN, 2 * WIN)
    return pl.pallas_call(
        body, name=f"swa_fwd_d{dil}", grid=(dil // ncls, nb),
        out_shape=[jax.ShapeDtypeStruct(qc.shape, F32)] * 2,
        in_specs=[cur, prev, cur, prev, cur, _full(bias.shape), ANY_SPEC],
        out_specs=[cur] * 2,
        scratch_shapes=[pltpu.VMEM(wide, F32), pltpu.VMEM(wide, BF16)],
        compiler_params=_cparams(2),
    )(qc, kc, kc, vc, vc, bias, dep)


def _mix_out(branches, gb, gc, xi, x, w_sc, g_a, g_c, w_out):
    s = x.shape[0]
    tb = TM // SUBLANES

    def body(o1, l1, o4, l4, o16, l16, gb_ref, gc_ref, xi_ref, gch_ref, xih_ref, x_ref, wsc_ref,
             ga_ref, gcv_ref, wout_ref, attn_ref, lse1, lse4, lse16, mixed_ref, x1_ref, scr_a, scr_b, scr_c, scr_d):
        i = pl.program_id(0)
        la, lb, lc = l1[...], _gather_classes(l4, scr_a, 4), _gather_classes(l16, scr_b, 16)
        m_all = jnp.maximum(jnp.maximum(la, lb), lc)
        ea, eb, ec = jnp.exp(la - m_all), jnp.exp(lb - m_all), jnp.exp(lc - m_all)
        den = (ea + eb) + ec
        num = (ea * o1[...] + eb * _gather_classes(o4, scr_c, 4)) + ec * _gather_classes(o16, scr_d, 16)
        attn = num / den
        attn_ref[...] = attn
        _spread(m_all + jnp.log(den), scr_a, (lse1, lse4, lse16), F32)
        xa, _ = _rms(attn)
        u = gc_ref[...] * xi_ref[...]
        uh = jnp.where(i > 0, gch_ref[...] * xih_ref[...], 0.0)
        conv = gb_ref[...] * _causal_conv3(u, uh, wsc_ref)
        xc, _ = _rms(conv)
        mixed = jnp.concatenate([xa * ga_ref[...], xc * gcv_ref[...]], axis=1).astype(BF16)
        mixed_ref[...] = mixed
        x1_ref[...] = x_ref[...] + _dot(mixed, wout_ref[...])

    row = lambda n: pl.BlockSpec((TM, n), lambda i: (i, 0))
    halo = pl.BlockSpec((SUBLANES, 512), lambda i: (jnp.maximum(i * tb - 1, 0), 0))
    cs = _class_specs(512)
    flat = [a for br in branches for a in br]
    res = pl.pallas_call(
        body, name="mix_out", grid=(s // TM,),
        out_shape=[jax.ShapeDtypeStruct((s, 512), F32)] + _class_shapes(s, 512, F32)
        + [jax.ShapeDtypeStruct((s, D_MODEL), BF16), jax.ShapeDtypeStruct((s, D_MODEL), F32)],
        in_specs=[cs[0], cs[0], cs[1], cs[1], cs[2], cs[2], row(512), row(512), row(512), halo, halo,
                  row(D_MODEL), _full(w_sc.shape), _full(g_a.shape), _full(g_c.shape), _full(w_out.shape)],
        out_specs=[row(512)] + cs + [row(D_MODEL), row(D_MODEL)],
        scratch_shapes=[pltpu.VMEM((512 // LANES, TM, LANES), F32)] * 4,
        compiler_params=_cparams(1),
    )(*flat, gb, gc, xi, gc, xi, x, w_sc, g_a, g_c, w_out)
    return res[0], res[1:4], res[4], res[5]


def _mem_kv(mem, g_mem, w_xk, w_xv):
    def body(mem_ref, g_ref, wk_ref, wv_ref, mn_ref, k_ref, v_ref):
        xh, _ = _rms(mem_ref[...])
        mn = (xh * g_ref[...]).astype(BF16)
        mn_ref[...] = mn
        k_ref[...] = _dot(mn, wk_ref[...]).astype(BF16)
        v_ref[...] = _dot(mn, wv_ref[...]).astype(BF16)

    vm = pl.BlockSpec(memory_space=pltpu.VMEM)
    return pl.pallas_call(
        body, name="mem_kv",
        out_shape=[jax.ShapeDtypeStruct(mem.shape, BF16)] * 3,
        in_specs=[vm] * 4, out_specs=[vm] * 3,
        compiler_params=pltpu.CompilerParams(vmem_limit_bytes=VMEM_LIMIT),
    )(mem, g_mem, w_xk, w_xv)


def _xattn_fwd(x1, g, w_xq, k, v, w_xo, dep):
    s = x1.shape[0]

    def body(x1_ref, g_ref, wq_ref, k_ref, v_ref, wo_ref, dep_ref, h2_ref, q_ref, o_ref, x2_ref):
        x1v = x1_ref[...]
        xh, _ = _rms(x1v)
        h2 = (xh * g_ref[...]).astype(BF16)
        h2_ref[...] = h2
        qb = _dot(h2, wq_ref[...]).astype(BF16)
        q_ref[...] = qb
        outs = []
        for h in range(N_MEM_HEADS):
            sl = slice(h * MEM_HEAD_DIM, (h + 1) * MEM_HEAD_DIM)
            lg = _dot_nt(qb[:, sl], k_ref[:, sl]) * (MEM_HEAD_DIM ** -0.5)
            p = jnp.exp(lg - jnp.max(lg, axis=-1, keepdims=True))
            p = p / jnp.sum(p, axis=-1, keepdims=True)
            outs.append(_dot(p.astype(BF16), v_ref[:, sl]))
        o = jnp.concatenate(outs, axis=1).astype(BF16)
        o_ref[...] = o
        x2_ref[...] = x1v + _dot(o, wo_ref[...])

    row = pl.BlockSpec((TM, D_MODEL), lambda i: (i, 0))
    return pl.pallas_call(
        body, name="xattn_fwd", grid=(s // TM,),
        out_shape=[jax.ShapeDtypeStruct((s, D_MODEL), BF16)] * 3 + [jax.ShapeDtypeStruct((s, D_MODEL), F32)],
        in_specs=[row, _full(g.shape), _full(w_xq.shape), _full(k.shape), _full(v.shape), _full(w_xo.shape), ANY_SPEC],
        out_specs=[row] * 4,
        compiler_params=_cparams(1),
    )(x1, g, w_xq, k, v, w_xo, dep)


def _ffn_conv(h_ext, wup_ref, wfc_ref, bfc_ref, j):
    u = _dot_nt(h_ext, wup_ref[j])
    w = wfc_ref[j]
    c = ((pltpu.roll(u, 2, 0) * w[0:1, :] + pltpu.roll(u, 1, 0) * w[1:2, :]) + u * w[2:3, :]) + bfc_ref[j]
    return c[HALO:], u[HALO:]


def _ffn_fwd(x2, g, w_up_g, w_fc, b_fc, w_down_g, g_final, target):
    s = x2.shape[0]
    tb = TM_FFN // HALO
    n_ch, wid = w_up_g.shape[:2]
    half = n_ch // 2

    def body(x_ref, xp_ref, g_ref, wup_ref, wfc_ref, bfc_ref, wd_ref, gf_ref, t_ref, h_ref, u_ref, c_ref, act_ref,
             dx3_ref, loss_ref, dgf_ref):
        i = pl.program_id(0)

        @pl.when(i == 0)
        def _():
            loss_ref[...] = jnp.zeros_like(loss_ref)
            dgf_ref[...] = jnp.zeros_like(dgf_ref)

        x2v = x_ref[...]
        gv = g_ref[...]
        h = (_rms(x2v)[0] * gv).astype(BF16)
        h_ref[...] = h
        hp = jnp.where(i > 0, _rms(xp_ref[...])[0] * gv, 0.0).astype(BF16)
        h_ext = jnp.concatenate([hp, h], axis=0)
        down = jnp.zeros((TM_FFN, D_MODEL), F32)
        for j in range(half):
            cg, ug = _ffn_conv(h_ext, wup_ref, wfc_ref, bfc_ref, j)
            cv, uv = _ffn_conv(h_ext, wup_ref, wfc_ref, bfc_ref, j + half)
            c_ref[j] = cg
            c_ref[j + half] = cv
            u_ref[j] = ug.astype(BF16)
            u_ref[j + half] = uv.astype(BF16)
            a = ((cg * _sigmoid(cg)) * cv).astype(BF16)
            act_ref[j] = a
            down = down + _dot(a, wd_ref[j])
        x3 = x2v + down
        xh, r = _rms(x3)
        gf = gf_ref[...]
        e = xh * gf - t_ref[...]
        loss_ref[...] += 0.5 * jnp.sum(jnp.sum(e * e, axis=1, keepdims=True), axis=0, keepdims=True) / D_MODEL
        dy = e * (1.0 / D_MODEL)
        dgf_ref[0:1, :] += jnp.sum(dy * xh, axis=0, keepdims=True)
        dx3_ref[...] = _rms_bwd(xh, r, gf, dy)

    row = pl.BlockSpec((TM_FFN, D_MODEL), lambda i: (i, 0))
    prev = pl.BlockSpec((HALO, D_MODEL), lambda i: (jnp.maximum(i * tb - 1, 0), 0))
    return pl.pallas_call(
        body, name="ffn_fwd", grid=(s // TM_FFN,),
        out_shape=[jax.ShapeDtypeStruct((s, D_MODEL), BF16), jax.ShapeDtypeStruct((n_ch, s, wid), BF16),
                   jax.ShapeDtypeStruct((n_ch, s, wid), F32), jax.ShapeDtypeStruct((half, s, wid), BF16),
                   jax.ShapeDtypeStruct((s, D_MODEL), F32), jax.ShapeDtypeStruct((SUBLANES, 128), F32),
                   jax.ShapeDtypeStruct((SUBLANES, D_MODEL), F32)],
        in_specs=[row, prev, _full(g.shape), _resident(w_up_g.shape), _full(w_fc.shape), _full(b_fc.shape),
                  _resident(w_down_g.shape), _full(g_final.shape), row],
        out_specs=[row, pl.BlockSpec((n_ch, TM_FFN, wid), lambda i: (0, i, 0)),
                   pl.BlockSpec((n_ch, TM_FFN, wid), lambda i: (0, i, 0)),
                   pl.BlockSpec((half, TM_FFN, wid), lambda i: (0, i, 0)), row,
                   _full((SUBLANES, 128)), _full((SUBLANES, D_MODEL))],
        compiler_params=_cparams(1),
    )(x2, x2, g, w_up_g, w_fc, b_fc, w_down_g, g_final, target)


def _ffn_bwd(dx3, up, conv, x2, g, w_up_g, w_fc, w_down_g):
    s = x2.shape[0]
    tb = TM_FFN // HALO
    last = s // HALO - 1
    n_tiles = s // TM_FFN
    n_ch, wid = w_up_g.shape[:2]
    half = n_ch // 2
    n_ext = TM_FFN + HALO

    def body(dx_ref, dxn_ref, u_ref, c_ref, cn_ref, x2_ref, g_ref, wup_ref, wfc_ref, wd_ref,
             dup_ref, dx2_ref, dg_ref, dwfc_ref, dbfc_ref):
        i = pl.program_id(0)

        @pl.when(i == 0)
        def _():
            dg_ref[...] = jnp.zeros_like(dg_ref)
            dwfc_ref[...] = jnp.zeros_like(dwfc_ref)
            dbfc_ref[...] = jnp.zeros_like(dbfc_ref)

        dxv = dx_ref[...]
        dxn = jnp.where(i < n_tiles - 1, dxn_ref[...], 0.0)
        dx_ext = jnp.concatenate([dxv, dxn], axis=0).astype(BF16)
        dh = jnp.zeros((TM_FFN, D_MODEL), F32)
        for j in range(half):
            cg = jnp.concatenate([c_ref[j], cn_ref[j]], axis=0)
            cv = jnp.concatenate([c_ref[j + half], cn_ref[j + half]], axis=0)
            dact = _dot_nt(dx_ext, wd_ref[j])
            sg = _sigmoid(cg)
            silu = cg * sg
            parts = ((j + half, dact * silu), (j, (dact * cv) * (sg + silu * (1.0 - sg))))
            for jj, dc in parts:
                u = u_ref[jj].astype(F32)
                dc0, dc1, dc2 = dc[:TM_FFN], pltpu.roll(dc, n_ext - 1, 0)[:TM_FFN], pltpu.roll(dc, n_ext - 2, 0)[:TM_FFN]
                dbfc_ref[jj:jj + 1, :] += jnp.sum(dc0, axis=0, keepdims=True)
                dwfc_ref[0, jj:jj + 1, :] += jnp.sum(dc2 * u, axis=0, keepdims=True)
                dwfc_ref[1, jj:jj + 1, :] += jnp.sum(dc1 * u, axis=0, keepdims=True)
                dwfc_ref[2, jj:jj + 1, :] += jnp.sum(dc0 * u, axis=0, keepdims=True)
                w = wfc_ref[jj]
                du = ((dc0 * w[2:3, :] + dc1 * w[1:2, :]) + dc2 * w[0:1, :]).astype(BF16)
                dup_ref[jj] = du
                dh = dh + _dot(du, wup_ref[jj])
        xh, r = _rms(x2_ref[...])
        dg_ref[0:1, :] += jnp.sum(dh * xh, axis=0, keepdims=True)
        dx2_ref[...] = dxv + _rms_bwd(xh, r, g_ref[...], dh)

    row = pl.BlockSpec((TM_FFN, D_MODEL), lambda i: (i, 0))
    nxt = pl.BlockSpec((HALO, D_MODEL), lambda i: (jnp.minimum((i + 1) * tb, last), 0))
    cur_c = pl.BlockSpec((n_ch, TM_FFN, wid), lambda i: (0, i, 0))
    nxt_c = pl.BlockSpec((n_ch, HALO, wid), lambda i: (0, jnp.minimum((i + 1) * tb, last), 0))
    return pl.pallas_call(
        body, name="ffn_bwd", grid=(n_tiles,),
        out_shape=[jax.ShapeDtypeStruct((n_ch, s, wid), BF16), jax.ShapeDtypeStruct((s, D_MODEL), F32),
                   jax.ShapeDtypeStruct((SUBLANES, D_MODEL), F32), jax.ShapeDtypeStruct((3, n_ch, wid), F32),
                   jax.ShapeDtypeStruct((n_ch, wid), F32)],
        in_specs=[row, nxt, cur_c, cur_c, nxt_c, row, _full(g.shape), _resident(w_up_g.shape), _full(w_fc.shape),
                  _resident(w_down_g.shape)],
        out_specs=[cur_c, row, _full((SUBLANES, D_MODEL)), _full((3, n_ch, wid)), _full((n_ch, wid))],
        compiler_params=_cparams(1),
    )(dx3, dx3, up, conv, conv, x2, g, w_up_g, w_fc, w_down_g)


def _xattn_bwd(dx2, o, q, k, v, w_xo, w_xq, x1, g, dep):
    s = x1.shape[0]

    def body(dx2_ref, o_ref, q_ref, k_ref, v_ref, wo_ref, wq_ref, x1_ref, g_ref, dep_ref, dq_ref, dx1_ref, dk_ref,
             dv_ref, dg_ref):
        @pl.when(pl.program_id(0) == 0)
        def _():
            dk_ref[...] = jnp.zeros_like(dk_ref)
            dv_ref[...] = jnp.zeros_like(dv_ref)
            dg_ref[...] = jnp.zeros_like(dg_ref)

        dx2v = dx2_ref[...]
        do = _dot_nt(dx2v.astype(BF16), wo_ref[...])
        dqs = []
        for h in range(N_MEM_HEADS):
            sl = slice(h * MEM_HEAD_DIM, (h + 1) * MEM_HEAD_DIM)
            qh, kh, vh = q_ref[:, sl], k_ref[:, sl], v_ref[:, sl]
            lg = _dot_nt(qh, kh) * (MEM_HEAD_DIM ** -0.5)
            p = jnp.exp(lg - jnp.max(lg, axis=-1, keepdims=True))
            p = p / jnp.sum(p, axis=-1, keepdims=True)
            doh = do[:, sl].astype(BF16)
            dp = _dot_nt(doh, vh)
            ds = (p * (dp - jnp.sum(p * dp, axis=-1, keepdims=True)) * (MEM_HEAD_DIM ** -0.5)).astype(BF16)
            dqs.append(_dot(ds, kh))
            dk_ref[:, sl] += _dot_tn(ds, qh)
            dv_ref[:, sl] += _dot_tn(p.astype(BF16), doh)
        dq = jnp.concatenate(dqs, axis=1).astype(BF16)
        dq_ref[...] = dq
        dh2 = _dot_nt(dq, wq_ref[...])
        xh, r = _rms(x1_ref[...])
        dg_ref[0:1, :] += jnp.sum(dh2 * xh, axis=0, keepdims=True)
        dx1_ref[...] = dx2v + _rms_bwd(xh, r, g_ref[...], dh2)

    row = pl.BlockSpec((TM, D_MODEL), lambda i: (i, 0))
    return pl.pallas_call(
        body, name="xattn_bwd", grid=(s // TM,),
        out_shape=[jax.ShapeDtypeStruct((s, D_MODEL), BF16), jax.ShapeDtypeStruct((s, D_MODEL), F32),
                   jax.ShapeDtypeStruct(k.shape, F32), jax.ShapeDtypeStruct(k.shape, F32),
                   jax.ShapeDtypeStruct((SUBLANES, D_MODEL), F32)],
        in_specs=[row, row, row, _full(k.shape), _full(v.shape), _full(w_xo.shape), _full(w_xq.shape), row,
                  _full(g.shape), ANY_SPEC],
        out_specs=[row, row, _full(k.shape), _full(k.shape), _full((SUBLANES, D_MODEL))],
        compiler_params=_cparams(1),
    )(dx2, o, q, k, v, w_xo, w_xq, x1, g, dep)


def _mem_kv_bwd(dk, dv, mem_n, mem, w_xk, w_xv):
    def body(dk_ref, dv_ref, mn_ref, mem_ref, wk_ref, wv_ref, dwk_ref, dwv_ref, dg_ref):
        dkb, dvb = dk_ref[...].astype(BF16), dv_ref[...].astype(BF16)
        mn = mn_ref[...]
        dwk_ref[...] = _dot_tn(mn, dkb).astype(BF16)
        dwv_ref[...] = _dot_tn(mn, dvb).astype(BF16)
        dmn = _dot_nt(dkb, wk_ref[...]) + _dot_nt(dvb, wv_ref[...])
        xh, _ = _rms(mem_ref[...])
        dg_ref[...] = jnp.zeros_like(dg_ref)
        dg_ref[0:1, :] = jnp.sum(dmn * xh, axis=0, keepdims=True)

    vm = pl.BlockSpec(memory_space=pltpu.VMEM)
    return pl.pallas_call(
        body, name="mem_kv_bwd",
        out_shape=[jax.ShapeDtypeStruct(w_xk.shape, BF16), jax.ShapeDtypeStruct(w_xv.shape, BF16),
                   jax.ShapeDtypeStruct((SUBLANES, D_MODEL), F32)],
        in_specs=[vm] * 6, out_specs=[vm] * 3,
        compiler_params=pltpu.CompilerParams(vmem_limit_bytes=VMEM_LIMIT),
    )(dk, dv, mem_n, mem, w_xk, w_xv)


def _mix_out_bwd(dx1, w_out, attn, gb, gc, xi, w_sc, g_a, g_c, dep):
    s = dx1.shape[0]
    tb = TM // SUBLANES

    def body(dx1_ref, wout_ref, attn_ref, gb_ref, gc_ref, xi_ref, gch_ref, xih_ref, wsc_ref, ga_ref, gcv_ref, dep_ref,
             da1, da4, da16, dd1, dd4, dd16, dgb_ref, dcv_ref, dga_ref, dgc_ref, dwsc_ref, scr):
        i = pl.program_id(0)

        @pl.when(i == 0)
        def _():
            dga_ref[...] = jnp.zeros_like(dga_ref)
            dgc_ref[...] = jnp.zeros_like(dgc_ref)
            dwsc_ref[...] = jnp.zeros_like(dwsc_ref)

        dmixed = _dot_nt(dx1_ref[...].astype(BF16), wout_ref[...])
        da, dcn = dmixed[:, :ATTN_W], dmixed[:, ATTN_W:]
        attn = attn_ref[...]
        xa, ra = _rms(attn)
        dga_ref[0:1, :] += jnp.sum(da * xa, axis=0, keepdims=True)
        dattn = _rms_bwd(xa, ra, ga_ref[...], da)
        _spread(dattn, scr, (da1, da4, da16), BF16)
        prod = dattn * attn
        dd = jnp.concatenate(
            [jnp.broadcast_to(jnp.sum(prod[:, h * HEAD_DIM:(h + 1) * HEAD_DIM], axis=-1, keepdims=True),
                              (TM, HEAD_DIM)) for h in range(N_HEADS)], axis=1)
        _spread(dd, scr, (dd1, dd4, dd16), F32)
        gbv = gb_ref[...]
        u = gc_ref[...] * xi_ref[...]
        uh = jnp.where(i > 0, gch_ref[...] * xih_ref[...], 0.0)
        u2, u1 = _shift_down(u, uh, 2), _shift_down(u, uh, 1)
        cv = (u2 * wsc_ref[0:1, :] + u1 * wsc_ref[1:2, :]) + u * wsc_ref[2:3, :]
        xc, rc = _rms(gbv * cv)
        dgc_ref[0:1, :] += jnp.sum(dcn * xc, axis=0, keepdims=True)
        dconv = _rms_bwd(xc, rc, gcv_ref[...], dcn)
        dgb_ref[...] = (dconv * cv).astype(BF16)
        dcv = dconv * gbv
        dcv_ref[...] = dcv
        dwsc_ref[0:1, :] += jnp.sum(dcv * u2, axis=0, keepdims=True)
        dwsc_ref[1:2, :] += jnp.sum(dcv * u1, axis=0, keepdims=True)
        dwsc_ref[2:3, :] += jnp.sum(dcv * u, axis=0, keepdims=True)

    row = lambda n: pl.BlockSpec((TM, n), lambda i: (i, 0))
    halo = pl.BlockSpec((SUBLANES, 512), lambda i: (jnp.maximum(i * tb - 1, 0), 0))
    acc = _full((SUBLANES, 512))
    res = pl.pallas_call(
        body, name="mix_out_bwd", grid=(s // TM,),
        out_shape=_class_shapes(s, 512, BF16) + _class_shapes(s, 512, F32)
        + [jax.ShapeDtypeStruct((s, 512), BF16), jax.ShapeDtypeStruct((s, 512), F32)]
        + [jax.ShapeDtypeStruct((SUBLANES, 512), F32)] * 3,
        in_specs=[row(D_MODEL), _full(w_out.shape), row(512), row(512), row(512), row(512), halo, halo,
                  _full(w_sc.shape), _full(g_a.shape), _full(g_c.shape), ANY_SPEC],
        out_specs=_class_specs(512) * 2 + [row(512)] * 2 + [acc] * 3,
        scratch_shapes=[pltpu.VMEM((512 // LANES, TM, LANES), F32)],
        compiler_params=_cparams(1),
    )(dx1, w_out, attn, gb, gc, xi, gc, xi, w_sc, g_a, g_c, dep)
    return res[0:3], res[3:6], res[6], res[7], res[8], res[9], res[10]


def _swa_bwd(qc, kc, vc, doc, lsec, ddc, bias, dil, dep):
    nsub, nb, ncls = _swa_steps(qc, dil)
    n128 = nsub * nb
    whole = nb == 1

    def body(q_ref, qn_ref, kp_ref, kc_ref, vp_ref, vc_ref, do_ref, don_ref, lse_ref, lsen_ref, dd_ref, ddn_ref,
             b_ref, dep_ref, dq_ref, dk_ref, dv_ref, db_ref, s_scr, dp_scr, sn_scr, dpn_scr, ds_scr, p_scr, dsn_scr,
             pn_scr):
        r, b = pl.program_id(0), pl.program_id(1)

        @pl.when((r == 0) & (b == 0))
        def _():
            db_ref[...] = jnp.zeros_like(db_ref)

        pairs = [slice(a * LANES, (a + 1) * LANES) for a in range(N_HEADS // 2)]
        blk = [slice(t * WIN, (t + 1) * WIN) for t in range(nsub)]
        last = blk[nsub - 1]
        cols = lambda t: slice(WIN, 2 * WIN) if whole and t == 0 else slice(0, 2 * WIN)
        of_head = lambda ref, c, rows, h: ref[c, rows, h * HEAD_DIM:h * HEAD_DIM + 1]
        no_prev = (b == 0) & (lax.broadcasted_iota(jnp.int32, (WIN, 2 * WIN), 1) < WIN)

        def keys(prev_ref, cur_ref, c, t, sl):
            if whole and t == 0:
                return cur_ref[c, blk[0], sl]
            if t == 0:
                return jnp.concatenate([prev_ref[c, :, sl], cur_ref[c, blk[0], sl]], axis=0)
            return cur_ref[c, (t - 1) * WIN:(t + 1) * WIN, sl]

        for a, sl in enumerate(pairs):
            for c, t in [(c, t) for c in range(ncls) for t in range(nsub)]:
                k2, v2 = keys(kp_ref, kc_ref, c, t, sl), keys(vp_ref, vc_ref, c, t, sl)
                q_eo = _pair_split(q_ref[c, blk[t], sl])
                do_eo = _pair_split(do_ref[c, blk[t], sl].astype(BF16))
                for e in range(2):
                    s_scr[c * nsub + t, 2 * a + e, :, cols(t)] = _dot_nt(q_eo[e], k2)
                    dp_scr[c * nsub + t, 2 * a + e, :, cols(t)] = _dot_nt(do_eo[e], v2)
            if not whole:
                qn_eo = _pair_split(qn_ref[0, :, sl])
                don_eo = _pair_split(don_ref[0, :, sl].astype(BF16))
                for e in range(2):
                    sn_scr[2 * a + e] = _dot_nt(qn_eo[e], kc_ref[0, last, sl])
                    dpn_scr[2 * a + e] = _dot_nt(don_eo[e], vc_ref[0, last, sl])
        for c, t, h in [(c, t, h) for c in range(ncls) for t in range(nsub) for h in range(N_HEADS)]:
            i, cl = c * nsub + t, cols(t)
            lg = s_scr[i, h, :, cl] + b_ref[h, :, cl]
            if t == 0 and not whole:
                lg = jnp.where(no_prev, -jnp.inf, lg)
            p = jnp.exp(lg - of_head(lse_ref, c, blk[t], h))
            ds = p * (dp_scr[i, h, :, cl] - of_head(dd_ref, c, blk[t], h))
            db_ref[h, :, cl] += ds
            ds_scr[i, h, :, cl] = ds.astype(BF16)
            p_scr[i, h, :, cl] = p.astype(BF16)
        if not whole:
            every = slice(0, WIN)
            for h in range(N_HEADS):
                lgn = jnp.where(b + 1 < nb, sn_scr[h] + b_ref[h, :, :WIN], -jnp.inf)
                pn = jnp.exp(lgn - of_head(lsen_ref, 0, every, h))
                dsn_scr[h] = (pn * (dpn_scr[h] - of_head(ddn_ref, 0, every, h))).astype(BF16)
                pn_scr[h] = pn.astype(BF16)
        for a, sl in enumerate(pairs):
            for c in range(ncls):
                q_eo = [_pair_split(q_ref[c, blk[t], sl]) for t in range(nsub)]
                do_eo = [_pair_split(do_ref[c, blk[t], sl].astype(BF16)) for t in range(nsub)]
                if not whole:
                    q_eo.append(_pair_split(qn_ref[0, :, sl]))
                    do_eo.append(_pair_split(don_ref[0, :, sl].astype(BF16)))
                for t in range(nsub):
                    i = c * nsub + t
                    k_eo = _pair_split(keys(kp_ref, kc_ref, c, t, sl))
                    dq, dk, dv = None, None, None
                    for e in range(2):
                        h = 2 * a + e
                        terms = [_dot(ds_scr[i, h, :, cols(t)], k_eo[e]),
                                 _dot_tn(ds_scr[i, h, :, WIN:], q_eo[t][e]),
                                 _dot_tn(p_scr[i, h, :, WIN:], do_eo[t][e])]
                        if t + 1 < nsub or not whole:
                            ds_next = ds_scr[i + 1, h, :, :WIN] if t + 1 < nsub else dsn_scr[h]
                            p_next = p_scr[i + 1, h, :, :WIN] if t + 1 < nsub else pn_scr[h]
                            terms[1] += _dot_tn(ds_next, q_eo[t + 1][e])
                            terms[2] += _dot_tn(p_next, do_eo[t + 1][e])
                        dq, dk, dv = terms if e == 0 else (dq + terms[0], dk + terms[1], dv + terms[2])
                    dq_ref[c, blk[t], sl] = dq.astype(BF16)
                    dk_ref[c, blk[t], sl] = dk.astype(BF16)
                    dv_ref[c, blk[t], sl] = dv.astype(BF16)

    cur = pl.BlockSpec((ncls, nsub * WIN, 512), lambda r, b: (r, b, 0))
    prev = pl.BlockSpec((ncls, WIN, 512), lambda r, b: (r, jnp.maximum(nsub * b - 1, 0), 0))
    nxt = pl.BlockSpec((ncls, WIN, 512), lambda r, b: (r, jnp.minimum(nsub * b + nsub, n128 - 1), 0))
    wide, narrow = (ncls * nsub, N_HEADS, WIN, 2 * WIN), (N_HEADS, WIN, WIN)
    return pl.pallas_call(
        body, name=f"swa_bwd_d{dil}", grid=(dil // ncls, nb),
        out_shape=[jax.ShapeDtypeStruct(qc.shape, BF16)] * 3 + [jax.ShapeDtypeStruct(bias.shape, F32)],
        in_specs=[cur, nxt, prev, cur, prev, cur, cur, nxt, cur, nxt, cur, nxt, _full(bias.shape), ANY_SPEC],
        out_specs=[cur] * 3 + [_full(bias.shape)],
        scratch_shapes=[pltpu.VMEM(wide, F32), pltpu.VMEM(wide, F32), pltpu.VMEM(narrow, F32),
                        pltpu.VMEM(narrow, F32), pltpu.VMEM(wide, BF16), pltpu.VMEM(wide, BF16),
                        pltpu.VMEM(narrow, BF16), pltpu.VMEM(narrow, BF16)],
        compiler_params=_cparams(2),
    )(qc, qc, kc, kc, vc, vc, doc, doc, lsec, lsec, ddc, ddc, bias, dep)


def _in_proj_bwd(dqs, dks, dvs, dgb, dcv, gc, xi, w_sc, w_in_g, x, g_mix, dx1):
    s = x.shape[0]
    tb = TM // SUBLANES
    last = s // SUBLANES - 1
    n_tiles = s // TM

    def body(dq1, dq4, dq16, dk1, dk4, dk16, dv1, dv4, dv16, dgb_ref, dcv_ref, dcvn_ref, gc_ref, xi_ref, wsc_ref,
             w_hbm, x_ref, g_ref, dx1_ref, dproj_ref, gx_ref, dg_ref, scr_a, scr_b, w_scr, w_sems):
        i = pl.program_id(0)
        _load_w_in_pairs(w_hbm, w_scr, w_sems)

        @pl.when(i == 0)
        def _():
            dg_ref[...] = jnp.zeros_like(dg_ref)

        d0 = dcv_ref[...]
        dn = jnp.where(i < n_tiles - 1, dcvn_ref[...], 0.0)
        du = (d0 * wsc_ref[2:3, :] + _shift_up(d0, dn, 1) * wsc_ref[1:2, :]) + _shift_up(d0, dn, 2) * wsc_ref[0:1, :]
        merge = lambda a, b4, b16: ((a[...].astype(F32) + _gather_classes(b4, scr_a, 4))
                                    + _gather_classes(b16, scr_b, 16))
        dq = merge(dq1, dq4, dq16) * (HEAD_DIM ** -0.5)
        dk = merge(dk1, dk4, dk16)
        dv = merge(dv1, dv4, dv16)
        dproj = jnp.concatenate([dq, dk, dv, dgb_ref[...].astype(F32), du * xi_ref[...], du * gc_ref[...]],
                                axis=1).astype(BF16)
        dproj_ref[...] = dproj
        dh = jnp.zeros((TM, D_MODEL), F32)
        for j in range(N_DEV // 2):
            dh = dh + _dot_nt(dproj[:, 2 * j * IN_CHUNK:2 * (j + 1) * IN_CHUNK], w_scr[j])
        xh, r = _rms(x_ref[...])
        dg_ref[0:1, :] += jnp.sum(dh * xh, axis=0, keepdims=True)
        gx_ref[...] = dx1_ref[...] + _rms_bwd(xh, r, g_ref[...], dh)

    row = lambda n: pl.BlockSpec((TM, n), lambda i: (i, 0))
    nxt = pl.BlockSpec((SUBLANES, 512), lambda i: (jnp.minimum((i + 1) * tb, last), 0))
    return pl.pallas_call(
        body, name="in_proj_bwd", grid=(n_tiles,),
        out_shape=[jax.ShapeDtypeStruct((s, IN_COLS), BF16), jax.ShapeDtypeStruct((s, D_MODEL), F32),
                   jax.ShapeDtypeStruct((SUBLANES, D_MODEL), F32)],
        in_specs=_class_specs(512) * 3 + [row(512), row(512), nxt, row(512), row(512), _full(w_sc.shape),
                                          ANY_SPEC, row(D_MODEL), _full(g_mix.shape), row(D_MODEL)],
        out_specs=[row(IN_COLS), row(D_MODEL), _full((SUBLANES, D_MODEL))],
        scratch_shapes=[pltpu.VMEM((512 // LANES, TM, LANES), F32)] * 2 + W_IN_PAIRS,
        compiler_params=_cparams(1),
    )(*dqs, *dks, *dvs, dgb, dcv, dcv, gc, xi, w_sc, w_in_g, x, g_mix, dx1)


def _dw(a, b, dep, name, a_chunked=False, b_chunked=False, n_chunks=1, chunk_cols=None, per_step=1):
    single = not (a_chunked or b_chunked or chunk_cols)
    wide = a_chunked and a.shape[2] > D_MODEL
    ts = TS_DW // 2 if single or wide else TS_DW
    if a_chunked:
        nj, s, kk = a.shape
        nn = b.shape[1]
        a_spec = pl.BlockSpec((1, ts, kk), lambda j, t: (j, t, 0))
        b_spec = pl.BlockSpec((ts, nn), lambda j, t: (t, 0))
    elif b_chunked:
        nj, s, nn = b.shape
        kk = a.shape[1]
        a_spec = pl.BlockSpec((ts, kk), lambda j, t: (t, 0))
        b_spec = pl.BlockSpec((1, ts, nn), lambda j, t: (j, t, 0))
    else:
        s, kk = a.shape
        nj, nn = (n_chunks // per_step, chunk_cols * per_step) if chunk_cols else (1, b.shape[1])
        a_spec = pl.BlockSpec((ts, kk), lambda j, t: (t, 0))
        b_spec = pl.BlockSpec((ts, nn), lambda j, t: (t, j))
    n_steps = s // ts

    def body(a_ref, b_ref, dep_ref, o_ref, acc):
        t = pl.program_id(1)

        @pl.when(t == 0)
        def _():
            acc[...] = jnp.zeros_like(acc)

        av = (a_ref[0] if a_chunked else a_ref[...]).astype(BF16)
        bv = (b_ref[0] if b_chunked else b_ref[...]).astype(BF16)
        acc[...] += _dot_tn(av, bv)

        @pl.when(t == n_steps - 1)
        def _():
            for q in range(per_step):
                o_ref[q] = acc[:, q * nn // per_step:(q + 1) * nn // per_step].astype(BF16)

    return pl.pallas_call(
        body, name=name, grid=(nj, n_steps),
        out_shape=jax.ShapeDtypeStruct((nj * per_step, kk, nn // per_step), BF16),
        in_specs=[a_spec, b_spec, ANY_SPEC],
        out_specs=pl.BlockSpec((per_step, kk, nn // per_step), lambda j, t: (j, 0, 0)),
        scratch_shapes=[pltpu.VMEM((kk, nn), F32)],
        compiler_params=_cparams(2),
    )(a, b, dep)


def _adamw_math(w, g, m, v):
    m2 = ADAM_B1 * m + (1.0 - ADAM_B1) * g
    v2 = ADAM_B2 * v + (1.0 - ADAM_B2) * (g * g)
    m_hat = m2 / (1.0 - ADAM_B1 ** ADAM_STEP)
    v_hat = v2 / (1.0 - ADAM_B2 ** ADAM_STEP)
    delta = -ADAM_LR * (m_hat / (jnp.sqrt(v_hat) + ADAM_EPS) + ADAM_WD * w)
    return delta, m2, v2


def _sum_parts(me, own, p_ref):
    g = None
    for i in range(N_DEV):
        part = jnp.where(me == i, own.astype(F32), p_ref[i].astype(F32))
        g = part if g is None else g + part
    return g


def _adamw_big(name, w, sent, parts, m, v, me_arr):
    rr, cc = w.shape
    tr = rr // 4 if rr >= 512 else rr

    def body(me_ref, w_ref, own_ref, p_ref, m_ref, v_ref, g_ref, d_ref, nm_ref, nv_ref):
        g = own_ref[0].astype(F32)
        for k in range(1, N_DEV):
            g = g + p_ref[(me_ref[0] + k) % N_DEV].astype(F32)
        g_ref[...] = g
        d_ref[...], nm_ref[...], nv_ref[...] = _adamw_math(w_ref[...], g, m_ref[...], v_ref[...])

    row = pl.BlockSpec((tr, cc), lambda i, me: (i, 0))
    return pl.pallas_call(
        body, name=name,
        grid_spec=pltpu.PrefetchScalarGridSpec(
            num_scalar_prefetch=1, grid=(rr // tr,),
            in_specs=[row, pl.BlockSpec((1, tr, cc), lambda i, me: (me[0], i, 0)),
                      pl.BlockSpec((N_DEV, tr, cc), lambda i, me: (0, i, 0)), row, row],
            out_specs=[row] * 4),
        out_shape=[jax.ShapeDtypeStruct((rr, cc), F32)] * 4,
        compiler_params=_cparams(1),
    )(me_arr, w, sent, parts, m, v)


def _small_slices():
    return [
        (slice(ROW_RELB, ROW_RELB + 8), slice(0, N_BUCKETS)),
        (slice(ROW_GMIX, ROW_GMIX + 1), slice(0, D_MODEL)),
        (slice(ROW_GAC, ROW_GAC + 1), slice(0, ATTN_W)),
        (slice(ROW_GAC, ROW_GAC + 1), slice(ATTN_W, D_MODEL)),
        (slice(ROW_GXATTN, ROW_GXATTN + 1), slice(0, D_MODEL)),
        (slice(ROW_GMEM, ROW_GMEM + 1), slice(0, D_MODEL)),
        (slice(ROW_GFFN, ROW_GFFN + 1), slice(0, D_MODEL)),
        (slice(ROW_BFC, ROW_BFC + 8), slice(0, UP_CHUNK)),
        (slice(ROW_GFINAL, ROW_GFINAL + 1), slice(0, D_MODEL)),
    ]


def _adamw_small(own, parts, wmv, me_arr):
    slices = _small_slices()
    n = len(slices)

    def body(*refs):
        me_ref, own_ref, p_ref = refs[:3]
        ins = refs[3:3 + 3 * n]
        g_ref = refs[3 + 3 * n]
        outs = refs[4 + 3 * n:]
        g = _sum_parts(me_ref[0], own_ref[...], p_ref)
        g_ref[...] = g
        for a, (rs, ls) in enumerate(slices):
            ga = g[rs, ls]
            outs[4 * a][...] = ga
            outs[4 * a + 1][...], outs[4 * a + 2][...], outs[4 * a + 3][...] = _adamw_math(
                ins[3 * a][...], ga, ins[3 * a + 1][...], ins[3 * a + 2][...])

    vm = pl.BlockSpec(memory_space=pltpu.VMEM)
    flat = [t for trip in wmv for t in trip]
    out_shape = [jax.ShapeDtypeStruct((SMALL_ROWS, D_MODEL), F32)]
    for w, _, _ in wmv:
        out_shape += [jax.ShapeDtypeStruct(w.shape, F32)] * 4
    res = pl.pallas_call(
        body, name="adamw_small", out_shape=out_shape,
        in_specs=[SMEM_SPEC] + [vm] * (2 + 3 * n), out_specs=[vm] * len(out_shape),
    )(me_arr, own, parts, *flat)
    return res[0], [res[1 + 4 * a:5 + 4 * a] for a in range(n)]


def _adamw_shards(items):
    n = len(items)

    def body(*refs):
        for a in range(n):
            w_ref, g_ref, m_ref, v_ref = refs[4 * a:4 * a + 4]
            d_ref, nm_ref, nv_ref = refs[4 * n + 3 * a:4 * n + 3 * a + 3]
            d_ref[...], nm_ref[...], nv_ref[...] = _adamw_math(w_ref[...], g_ref[...], m_ref[...], v_ref[...])

    vm = pl.BlockSpec(memory_space=pltpu.VMEM)
    out_shape = []
    for w, _, _, _ in items:
        out_shape += [jax.ShapeDtypeStruct(w.shape, F32)] * 3
    res = pl.pallas_call(
        body, name="adamw_shards", out_shape=out_shape, in_specs=[vm] * (4 * n), out_specs=[vm] * (3 * n),
    )(*[t for it in items for t in it])
    return [res[3 * a:3 * a + 3] for a in range(n)]


def _mesh_pos():
    return lax.axis_index("x"), lax.axis_index("y"), lax.axis_index("c")


def _dev_index(p):
    return 4 * p[0] + 2 * p[1] + p[2]


def _all_gather(shards):
    n = len(shards)

    def body(*refs):
        ins, outs = refs[:n], refs[n:2 * n]
        send_sems, recv_sems, loc_sems = refs[2 * n:]
        x, y, c = _mesh_pos()
        me, sib = (x, y, c), (x, y, 1 - c)
        chips = [(1 - x, y), (x, 1 - y), (1 - x, 1 - y)]

        def cp(a, k, block, to, src=None):
            dst = outs[a].at[_dev_index(block)]
            return pltpu.make_async_remote_copy(
                src_ref=dst if src is None else src, dst_ref=dst, send_sem=send_sems.at[a, k],
                recv_sem=recv_sems.at[a, k], device_id=to, device_id_type=MESH)

        mine = [pltpu.make_async_copy(ins[a], outs[a].at[_dev_index(me)], loc_sems.at[a]) for a in range(n)]
        for m_ in mine:
            m_.start()
        first = []
        for a in range(n):
            first.append(cp(a, 0, me, sib, src=ins[a]))
            first += [cp(a, 1 + j, me, (*chip, c), src=ins[a]) for j, chip in enumerate(chips)]
        for f in first:
            f.start()
        passed = []
        for a in range(n):
            for j, chip in enumerate(chips):
                cp(a, 1 + j, (*chip, c), me).wait_recv()
                fwd = cp(a, 4 + j, (*chip, c), sib)
                fwd.start()
                passed.append(fwd)
        for a in range(n):
            cp(a, 0, sib, me).wait_recv()
            for j, chip in enumerate(chips):
                cp(a, 4 + j, (*chip, 1 - c), me).wait_recv()
        for f in first + passed:
            f.wait_send()
        for m_ in mine:
            m_.wait()

    hbm = pl.BlockSpec(memory_space=pltpu.HBM)
    return pl.pallas_call(
        body, name="all_gather_weights",
        out_shape=[jax.ShapeDtypeStruct((N_DEV,) + a.shape, a.dtype) for a in shards],
        in_specs=[hbm] * n, out_specs=[hbm] * n,
        scratch_shapes=[pltpu.SemaphoreType.DMA((n, 7)), pltpu.SemaphoreType.DMA((n, 7)),
                        pltpu.SemaphoreType.DMA((n,))],
    )(*shards)


def _peers():
    x, y, c = _mesh_pos()
    return (x, y, c), [((1 - x) if k & 4 else x, (1 - y) if k & 2 else y, (1 - c) if k & 1 else c)
                       for k in range(1, 8)]


def _exchange_copy(src_ref, land_ref, whole, send_sems, recv_sems, a, k, peer, slot):
    src = src_ref if whole else src_ref.at[_dev_index(peer)]
    return pltpu.make_async_remote_copy(
        src_ref=src, dst_ref=land_ref.at[slot], send_sem=send_sems.at[7 * a + k], recv_sem=recv_sems.at[7 * a + k],
        device_id=peer, device_id_type=MESH)


def _exchange_start(name, srcs, whole, dep):
    n = len(srcs)
    lands = [lax.empty(((N_DEV,) + s.shape) if w else s.shape, s.dtype) for s, w in zip(srcs, whole)]

    def body(*refs):
        src_refs, land_refs = refs[:n], refs[n:2 * n]
        send_sems, recv_sems, token = refs[2 * n + 1], refs[2 * n + 2], refs[-1]
        me, peers = _peers()
        for a in range(n):
            for k, peer in enumerate(peers):
                _exchange_copy(src_refs[a], land_refs[a], whole[a], send_sems, recv_sems, a, k, peer,
                               _dev_index(me)).start()
        token[...] = jnp.zeros_like(token)

    res = pl.pallas_call(
        body, name=name,
        out_shape=(pltpu.SemaphoreType.DMA((7 * n,)), pltpu.SemaphoreType.DMA((7 * n,)),
                   *[pltpu.HBM(a.shape, a.dtype) for a in srcs], *[pltpu.HBM(a.shape, a.dtype) for a in lands],
                   jax.ShapeDtypeStruct((SUBLANES, 128), F32)),
        in_specs=[HBM_SPEC] * (2 * n) + [ANY_SPEC],
        out_specs=(SEM_SPEC, SEM_SPEC, *([HBM_SPEC] * (2 * n)), VMEM_SPEC),
        input_output_aliases={i: 2 + i for i in range(2 * n)},
        compiler_params=pltpu.CompilerParams(has_side_effects=DATAFLOW),
    )(*[pltpu.with_memory_space_constraint(a, pltpu.HBM) for a in srcs],
      *[pltpu.with_memory_space_constraint(a, pltpu.HBM) for a in lands], dep)
    return res[0], res[1], list(res[2:2 + n]), list(res[2 + n:2 + 2 * n]), res[-1]


def _exchange_wait(name, started, whole, after, which=None):
    send_sems, recv_sems, srcs, lands, _ = started
    which = list(range(len(srcs))) if which is None else which
    srcs, lands = [srcs[a] for a in which], [lands[a] for a in which]
    n = len(srcs)

    def body(*refs):
        src_refs, land_refs = refs[:n], refs[n:2 * n]
        send_sems, recv_sems = refs[2 * n], refs[2 * n + 1]
        _, peers = _peers()
        for i, a in enumerate(which):
            for k, peer in enumerate(peers):
                cp = _exchange_copy(src_refs[i], land_refs[i], whole[a], send_sems, recv_sems, a, k, peer,
                                    _dev_index(peer))
                cp.wait_send()
                cp.wait_recv()

    res = pl.pallas_call(
        body, name=name,
        out_shape=[pltpu.HBM(a.shape, a.dtype) for a in srcs + lands],
        in_specs=[HBM_SPEC] * (2 * n) + [SEM_SPEC, SEM_SPEC, ANY_SPEC],
        out_specs=[HBM_SPEC] * (2 * n),
        input_output_aliases={i: i for i in range(2 * n)},
        compiler_params=pltpu.CompilerParams(has_side_effects=DATAFLOW),
    )(*srcs, *lands, send_sems, recv_sems, after)
    return list(res[:n]), list(res[n:])


def _gather_start(name, shards, dep):
    n = len(shards)
    lands = [lax.empty((N_DEV,) + a.shape, a.dtype) for a in shards]

    def body(*refs):
        src_refs, land_refs = refs[:n], refs[n:2 * n]
        send_sems, recv_sems, token = refs[2 * n + 1], refs[2 * n + 2], refs[-1]
        x, y, c = _mesh_pos()
        peers = [(x, y, 1 - c), (1 - x, y, c), (x, 1 - y, c), (1 - x, 1 - y, c)]
        for a in range(n):
            for k, peer in enumerate(peers):
                pltpu.make_async_remote_copy(
                    src_ref=src_refs[a], dst_ref=land_refs[a].at[_dev_index((x, y, c))], send_sem=send_sems.at[4 * a + k],
                    recv_sem=recv_sems.at[4 * a + k], device_id=peer, device_id_type=MESH).start()
        token[...] = jnp.zeros_like(token)

    res = pl.pallas_call(
        body, name=name,
        out_shape=(pltpu.SemaphoreType.DMA((4 * n,)), pltpu.SemaphoreType.DMA((4 * n,)),
                   *[pltpu.HBM(a.shape, a.dtype) for a in shards], *[pltpu.HBM(a.shape, a.dtype) for a in lands],
                   jax.ShapeDtypeStruct((SUBLANES, 128), F32)),
        in_specs=[HBM_SPEC] * (2 * n) + [ANY_SPEC],
        out_specs=(SEM_SPEC, SEM_SPEC, *([HBM_SPEC] * (2 * n)), VMEM_SPEC),
        input_output_aliases={i: 2 + i for i in range(2 * n)},
        compiler_params=pltpu.CompilerParams(has_side_effects=DATAFLOW),
    )(*[pltpu.with_memory_space_constraint(a, pltpu.HBM) for a in shards],
      *[pltpu.with_memory_space_constraint(a, pltpu.HBM) for a in lands], dep)
    return res[0], res[1], list(res[2:2 + n]), list(res[2 + n:2 + 2 * n]), res[-1]


def _gather_forward(name, send_sems, recv_sems, lands, which, after):
    n = len(which)

    def body(*refs):
        land_refs = refs[:n]
        send_sems, recv_sems = refs[n], refs[n + 1]
        fsend, frecv, token = refs[n + 3], refs[n + 4], refs[-1]
        x, y, c = _mesh_pos()
        chips = [(1 - x, y), (x, 1 - y), (1 - x, 1 - y)]
        for i, a in enumerate(which):
            for j, chip in enumerate(chips):
                block = land_refs[i].at[_dev_index((*chip, c))]
                pltpu.make_async_remote_copy(
                    src_ref=block, dst_ref=block, send_sem=send_sems.at[4 * a + 1 + j], recv_sem=recv_sems.at[4 * a + 1 + j],
                    device_id=(*chip, c), device_id_type=MESH).wait_recv()
                pltpu.make_async_remote_copy(
                    src_ref=block, dst_ref=block, send_sem=fsend.at[3 * i + j], recv_sem=frecv.at[3 * i + j],
                    device_id=(x, y, 1 - c), device_id_type=MESH).start()
        token[...] = jnp.zeros_like(token)

    res = pl.pallas_call(
        body, name=name,
        out_shape=(pltpu.SemaphoreType.DMA((3 * n,)), pltpu.SemaphoreType.DMA((3 * n,)),
                   *[pltpu.HBM(a.shape, a.dtype) for a in lands], jax.ShapeDtypeStruct((SUBLANES, 128), F32)),
        in_specs=[HBM_SPEC] * n + [SEM_SPEC, SEM_SPEC, ANY_SPEC],
        out_specs=(SEM_SPEC, SEM_SPEC, *([HBM_SPEC] * n), VMEM_SPEC),
        input_output_aliases={i: 2 + i for i in range(n)},
        compiler_params=pltpu.CompilerParams(has_side_effects=DATAFLOW),
    )(*lands, send_sems, recv_sems, after)
    return res[0], res[1], list(res[2:2 + n]), res[-1]


def _gather_wait(name, send_sems, recv_sems, fsend, frecv, srcs, lands, which, after):
    n = len(which)

    def body(*refs):
        land_refs = refs[n:2 * n]
        send_sems, recv_sems, fsend, frecv = refs[2 * n:2 * n + 4]
        x, y, c = _mesh_pos()
        sib = (x, y, 1 - c)
        chips = [(1 - x, y), (x, 1 - y), (1 - x, 1 - y)]
        for i, a in enumerate(which):
            def cp(slot, ssem, rsem):
                block = land_refs[i].at[_dev_index(slot)]
                return pltpu.make_async_remote_copy(src_ref=block, dst_ref=block, send_sem=ssem, recv_sem=rsem,
                                                    device_id=sib, device_id_type=MESH)
            cp(sib, send_sems.at[4 * a], recv_sems.at[4 * a]).wait_recv()
            for j, chip in enumerate(chips):
                cp((*chip, 1 - c), fsend.at[3 * i + j], frecv.at[3 * i + j]).wait_recv()
            for k in range(4):
                cp(sib, send_sems.at[4 * a + k], recv_sems.at[4 * a + k]).wait_send()
            for j in range(3):
                cp(sib, fsend.at[3 * i + j], frecv.at[3 * i + j]).wait_send()

    res = pl.pallas_call(
        body, name=name,
        out_shape=[pltpu.HBM(a.shape, a.dtype) for a in srcs + lands],
        in_specs=[HBM_SPEC] * (2 * n) + [SEM_SPEC] * 4 + [ANY_SPEC],
        out_specs=[HBM_SPEC] * (2 * n),
        input_output_aliases={i: i for i in range(2 * n)},
        compiler_params=pltpu.CompilerParams(has_side_effects=DATAFLOW),
    )(*srcs, *lands, send_sems, recv_sems, fsend, frecv, after)
    return list(res[n:])


def _local_step(x, mem, target, rel_bias, g_mix, w_in_g, w_sc, g_a, g_c, g_xattn, g_mem, g_ffn, w_fc, b_fc, g_final,
                dep, forward_weights, late_weights, emit, emit_small):
    s = x.shape[0]
    buckets = _bucket_tables()
    bias = _bias_fwd(rel_bias, buckets)

    h1, qs, ks, vs, gb, gc, xi = _rms_proj(x, g_mix, w_in_g, dep)
    qs, ks, vs = ([a[0][None]] + list(a[1:]) for a in (qs, ks, vs))
    group1, group2 = ["w_out", "w_xq", "w_xk", "w_xv", "w_xo"], ["w_up", "w_down"]
    tok = forward_weights(group1, h1)
    branches = []
    for p, dil in enumerate(DILATIONS):
        o_p, lse_p = _swa_fwd(qs[p], ks[p], vs[p], bias[p], dil, tok)
        branches.append([o_p[0], lse_p[0]] if dil == 1 else [o_p, lse_p])
    lw = late_weights(group1, branches[-1][0])
    w_out, w_xq, w_xk, w_xv, w_xo = (lw[n] for n in group1)
    attn, lses, mixed, x1 = _mix_out(branches, gb, gc, xi, x, w_sc, g_a, g_c, w_out)
    tok = forward_weights(group2, x1)
    mem_n, mk, mv = _mem_kv(mem, g_mem, w_xk, w_xv)
    h2, xq, xo, x2 = _xattn_fwd(x1, g_xattn, w_xq, mk, mv, w_xo, tok)
    lw = late_weights(group2, x2)
    w_up_g = lw["w_up"].reshape(FFN_CHUNKS, FFN_WIDTH, D_MODEL)
    w_down_g = lw["w_down"].reshape(FFN_CHUNKS // 2, FFN_WIDTH, D_MODEL)
    pairs = lambda a: a.reshape(FFN_CHUNKS, 2, a.shape[1], UP_CHUNK).transpose(0, 2, 1, 3).reshape(
        FFN_CHUNKS, a.shape[1], FFN_WIDTH)
    w_fc, b_fc = pairs(w_fc), pairs(b_fc)
    h3, up, conv, act, dx3, loss_acc, dg_final = _ffn_fwd(x2, g_ffn, w_up_g, w_fc, b_fc, w_down_g, g_final, target)

    gw_down = _dw(act, dx3, dep, "dw_down", a_chunked=True).reshape(N_DEV // 2, UP_CHUNK, D_MODEL)
    dup, dx2, dg_ffn, dw_fc, db_fc = _ffn_bwd(dx3, up, conv, x2, g_ffn, w_up_g, w_fc, w_down_g)
    gw_up = _dw(dup, h3, dep, "dw_up", a_chunked=True).reshape(N_DEV, UP_CHUNK, D_MODEL)
    tok = emit(dict(w_down=gw_down, w_up=gw_up))
    dxq, dx1, dmk, dmv, dg_xattn = _xattn_bwd(dx2, xo, xq, mk, mv, w_xo, w_xq, x1, g_xattn, tok)
    gw_xo = _dw(xo, dx2, tok, "dw_xo")[0]
    gw_xq = _dw(h2, dxq, tok, "dw_xq")[0]
    gw_xk, gw_xv, dg_mem = _mem_kv_bwd(dmk, dmv, mem_n, mem, w_xk, w_xv)
    dattns, dds, dgb, dcv, dg_a, dg_c, dw_sc = _mix_out_bwd(dx1, w_out, attn, gb, gc, xi, w_sc, g_a, g_c, tok)
    first = lambda a: [a[0][None]] + list(a[1:])
    dattns, dds, lses = first(dattns), first(dds), first(lses)
    gw_out = _dw(mixed, dx1, tok, "dw_out")[0]
    tok = emit(dict(w_xo=gw_xo, w_xq=gw_xq, w_xk=gw_xk, w_xv=gw_xv, w_out=gw_out))
    dqs, dks, dvs, dbias = [], [], [], []
    for p, dil in enumerate(DILATIONS):
        dq_p, dk_p, dv_p, db_p = _swa_bwd(qs[p], ks[p], vs[p], dattns[p], lses[p], dds[p], bias[p], dil, tok)
        dqs.append(dq_p[0] if dil == 1 else dq_p)
        dks.append(dk_p[0] if dil == 1 else dk_p)
        dvs.append(dv_p[0] if dil == 1 else dv_p)
        dbias.append(db_p)
    d_relb = _bias_bwd(jnp.stack(dbias), buckets)
    dproj, grad_x, dg_mix = _in_proj_bwd(dqs, dks, dvs, dgb, dcv, gc, xi, w_sc, w_in_g, x, g_mix, dx1)
    pad = lambda a: jnp.pad(a, ((0, 0), (0, D_MODEL - a.shape[1])))
    small = jnp.concatenate([
        d_relb, dg_mix, dg_xattn, dg_mem, dg_ffn, dg_final, jnp.concatenate([dg_a, dg_c], axis=1),
        pad(dw_sc), pad(db_fc.reshape(N_DEV, UP_CHUNK)), pad(dw_fc.reshape(3 * N_DEV, UP_CHUNK)), pad(loss_acc)],
        axis=0)
    tok = emit_small(small)
    gw_in = _dw(h1, dproj, tok, "dw_in", n_chunks=N_DEV, chunk_cols=IN_CHUNK, per_step=2)
    emit(dict(w_in=gw_in))
    return grad_x


def kernel(x, mem, rel_bias, g_mix, w_in, w_short_conv, g_attn_out, g_conv_out, w_out, g_xattn, g_mem, w_xq, w_xk, w_xv, w_xo, g_ffn, w_up, w_ffn_conv, b_ffn_conv, w_down, g_final, loss_target, m_rel_bias, m_g_mix, m_w_in, m_w_short_conv, m_g_attn_out, m_g_conv_out, m_w_out, m_g_xattn, m_g_mem, m_w_xq, m_w_xk, m_w_xv, m_w_xo, m_g_ffn, m_w_up, m_w_ffn_conv, m_b_ffn_conv, m_w_down, m_g_final, v_rel_bias, v_g_mix, v_w_in, v_w_short_conv, v_g_attn_out, v_g_conv_out, v_w_out, v_g_xattn, v_g_mem, v_w_xq, v_w_xk, v_w_xv, v_w_xo, v_g_ffn, v_w_up, v_w_ffn_conv, v_b_ffn_conv, v_w_down, v_g_final):
    me = _dev_index(_mesh_pos())
    me_arr = me.reshape(1).astype(jnp.int32)

    big_names = ["w_in", "w_out", "w_xq", "w_xk", "w_xv", "w_xo", "w_up", "w_down"]
    late_names = big_names[1:]
    big_w = dict(w_in=w_in[0], w_out=w_out[0], w_xq=w_xq[0], w_xk=w_xk[0], w_xv=w_xv[0], w_xo=w_xo[0],
                 w_up=w_up[0].T, w_down=w_down[0])
    big_m = dict(w_in=m_w_in[0], w_out=m_w_out[0], w_xq=m_w_xq[0], w_xk=m_w_xk[0], w_xv=m_w_xv[0], w_xo=m_w_xo[0],
                 w_up=m_w_up[0].T, w_down=m_w_down[0])
    big_v = dict(w_in=v_w_in[0], w_out=v_w_out[0], w_xq=v_w_xq[0], w_xk=v_w_xk[0], w_xv=v_w_xv[0], w_xo=v_w_xo[0],
                 w_up=v_w_up[0].T, w_down=v_w_down[0])
    shard_shape = {n: big_w[n].shape for n in big_names}

    w_in_g, w_sc_g, w_fc_full = _all_gather([big_w["w_in"].astype(BF16), w_short_conv[0], w_ffn_conv[0]])
    w_sc_full = w_sc_g.transpose(1, 0, 2).reshape(3, CONV_W)
    late_shards = [big_w[n].astype(BF16) for n in late_names]
    ag_send, ag_recv, ag_srcs, ag_lands, ag_token = _gather_start("gather_weights_start", late_shards, w_in_g)
    forwarded = {}

    def forward_weights(names, after):
        which = [late_names.index(n) for n in names]
        fsend, frecv, lands, token = _gather_forward("gather_" + "_".join(names) + "_forward", ag_send, ag_recv,
                                                     [ag_lands[a] for a in which], which, after)
        forwarded[tuple(names)] = (fsend, frecv, lands)
        return token

    def late_weights(names, after):
        which = [late_names.index(n) for n in names]
        fsend, frecv, lands = forwarded[tuple(names)]
        lands = _gather_wait("gather_" + "_".join(names) + "_wait", ag_send, ag_recv, fsend, frecv,
                             [ag_srcs[a] for a in which], lands, which, after)
        out = {}
        for n, a, land in zip(names, which, lands):
            full = lax.dynamic_update_index_in_dim(land, late_shards[a], me, 0)
            if n == "w_up":
                out[n] = full
            elif n == "w_down":
                out[n] = full.reshape(N_DEV // 2, UP_CHUNK, D_MODEL)
            else:
                out[n] = full.reshape(D_MODEL, D_MODEL)
        return out

    sent = []

    def emit(grads):
        names = list(grads)
        blocks = [grads[n].reshape((N_DEV,) + shard_shape[n]) for n in names]
        started = _exchange_start("scatter_" + "_".join(names) + "_start", blocks, [False] * len(names), me_arr)
        sent.append((names, started))
        return started[-1]

    def emit_small(small):
        sent_small.append((small, _exchange_start("gather_small_start", [small], [True], me_arr)))
        return sent_small[0][1][-1]

    sent_small = []
    grad_x = _local_step(
        x[0], mem[0], loss_target[0], rel_bias, g_mix, w_in_g, w_sc_full, g_attn_out, g_conv_out, g_xattn, g_mem,
        g_ffn, w_fc_full, b_ffn_conv.reshape(N_DEV, 1, UP_CHUNK), g_final.reshape(1, D_MODEL), ag_token,
        forward_weights, late_weights, emit, emit_small)

    small_g, small_started = sent_small[0]
    after = sent[-1][1][-1]
    small_parts = _exchange_wait("gather_small_wait", small_started, [True], after)[1][0]
    big_out = {}
    after = small_parts
    for names, started in sent:
        blocks, lands = _exchange_wait("scatter_" + "_".join(names) + "_wait", started, [False] * len(names), after)
        for n, block, land in zip(names, blocks, lands):
            res = _adamw_big("adamw_" + n, big_w[n], block, land, big_m[n], big_v[n], me_arr)
            big_out[n] = [(r.T if n == "w_up" else r)[None] for r in res]
            after = res[0]

    as_rows = lambda a: a.reshape(N_DEV, UP_CHUNK)
    row1 = lambda a: a.reshape(1, D_MODEL)
    small_names = ["rel_bias", "g_mix", "g_attn_out", "g_conv_out", "g_xattn", "g_mem", "g_ffn", "b_ffn_conv", "g_final"]
    wmv = [
        (rel_bias, m_rel_bias, v_rel_bias), (g_mix, m_g_mix, v_g_mix), (g_attn_out, m_g_attn_out, v_g_attn_out),
        (g_conv_out, m_g_conv_out, v_g_conv_out), (g_xattn, m_g_xattn, v_g_xattn), (g_mem, m_g_mem, v_g_mem),
        (g_ffn, m_g_ffn, v_g_ffn), (as_rows(b_ffn_conv), as_rows(m_b_ffn_conv), as_rows(v_b_ffn_conv)),
        (row1(g_final), row1(m_g_final), row1(v_g_final))]
    g_packed, small_res = _adamw_small(small_g, small_parts, wmv, me_arr)
    small_out = dict(zip(small_names, small_res))
    loss = g_packed[ROW_LOSS, 0]
    small_out["b_ffn_conv"] = [a.reshape(1, 2 * D_FF) for a in small_out["b_ffn_conv"]]
    small_out["g_final"] = [a.reshape(D_MODEL) for a in small_out["g_final"]]

    g_wsc = lax.dynamic_slice(g_packed[ROW_WSC:ROW_WSC + 3, 0:CONV_W], (0, me * HEAD_DIM), (3, HEAD_DIM))
    g_wfc = lax.dynamic_slice(g_packed[ROW_WFC:ROW_WFC + 3 * N_DEV, 0:UP_CHUNK].reshape(3, N_DEV, UP_CHUNK),
                              (0, me, 0), (3, 1, UP_CHUNK)).reshape(3, UP_CHUNK)
    shard_res = _adamw_shards([(w_short_conv[0], g_wsc, m_w_short_conv[0], v_w_short_conv[0]),
                               (w_ffn_conv[0], g_wfc, m_w_ffn_conv[0], v_w_ffn_conv[0])])
    small_out["w_short_conv"] = [g_wsc[None]] + [a[None] for a in shard_res[0]]
    small_out["w_ffn_conv"] = [g_wfc[None]] + [a[None] for a in shard_res[1]]

    order = ["rel_bias", "g_mix", "w_in", "w_short_conv", "g_attn_out", "g_conv_out", "w_out", "g_xattn", "g_mem",
             "w_xq", "w_xk", "w_xv", "w_xo", "g_ffn", "w_up", "w_ffn_conv", "b_ffn_conv", "w_down", "g_final"]
    allp = {**big_out, **small_out}
    outs = [loss, grad_x[None]]
    for kind in range(4):
        outs += [allp[n][kind] for n in order]
    return tuple(outs)
```

```python
import math

import numpy as np
import jax
import jax.numpy as jnp
from jax import lax
from jax.experimental import pallas as pl
from jax.experimental.pallas import tpu as pltpu

F32 = jnp.float32
BF16 = jnp.bfloat16
MESH = pl.DeviceIdType.MESH

N_DEV = 8
D_MODEL = 1024
ATTN_W = 512
CONV_W = 512
N_HEADS = 8
HEAD_DIM = 64
WIN = 128
DILATIONS = (1, 4, 16)
N_BUCKETS = 32
BUCKET_MAX_EXACT = 16
BUCKET_MAX_DISTANCE = 2048
N_MEM_HEADS = 4
MEM_HEAD_DIM = 256
D_FF = 2816
IN_COLS = 3072
IN_CHUNK = IN_COLS // N_DEV
UP_CHUNK = 2 * D_FF // N_DEV
FFN_CHUNKS = 4
FFN_WIDTH = 2 * D_FF // FFN_CHUNKS
EPS = 1e-6

ADAM_LR = 0.001
ADAM_B1 = 0.9
ADAM_B2 = 0.999
ADAM_EPS = 1e-08
ADAM_WD = 0.01
ADAM_STEP = 10

SUBLANES = 8
LANES = 128
HALO = 16
TM = 512
TM_FFN = 256
TS_DW = 4096
SWA_BLOCKS = 8
VMEM_LIMIT = 56 * 1024 * 1024

ROW_RELB, ROW_GMIX, ROW_GXATTN, ROW_GMEM, ROW_GFFN, ROW_GFINAL, ROW_GAC = 0, 8, 16, 24, 32, 40, 48
ROW_WSC, ROW_BFC, ROW_WFC, ROW_LOSS, SMALL_ROWS = 56, 64, 72, 96, 104


def _cparams(n_grid):
    return pltpu.CompilerParams(dimension_semantics=("arbitrary",) * n_grid, vmem_limit_bytes=VMEM_LIMIT)


def _full(shape):
    nd = len(shape)
    return pl.BlockSpec(tuple(shape), lambda *_: (0,) * nd)


def _resident(shape):
    nd = len(shape)
    return pl.BlockSpec(tuple(shape), lambda *_: (0,) * nd, pipeline_mode=pl.Buffered(1))


ANY_SPEC = pl.BlockSpec(memory_space=pl.ANY)
HBM_SPEC = pl.BlockSpec(memory_space=pltpu.HBM)
SEM_SPEC = pl.BlockSpec(memory_space=pltpu.SEMAPHORE)
VMEM_SPEC = pl.BlockSpec(memory_space=pltpu.VMEM)
SMEM_SPEC = pl.BlockSpec(memory_space=pltpu.SMEM)
DATAFLOW = pltpu.SideEffectType.DATAFLOW_SIDE_EFFECTING


def _rms(x):
    r = lax.rsqrt(jnp.mean(x * x, axis=-1, keepdims=True) + EPS)
    return x * r, r


def _rms_bwd(xh, r, g, dy):
    dxh = dy * g
    return r * (dxh - xh * jnp.mean(dxh * xh, axis=-1, keepdims=True))


def _shift_down(u, halo, k):
    ru = pltpu.roll(u, k, 0)
    rh = pltpu.roll(halo, k, 0)
    row = lax.broadcasted_iota(jnp.int32, rh.shape, 0)
    head = jnp.where(row < k, rh, ru[0:SUBLANES])
    return jnp.concatenate([head, ru[SUBLANES:]], axis=0)


def _shift_up(u, halo, k):
    tm = u.shape[0]
    ru = pltpu.roll(u, tm - k, 0)
    rh = pltpu.roll(halo, SUBLANES - k, 0)
    row = lax.broadcasted_iota(jnp.int32, rh.shape, 0)
    tail = jnp.where(row >= SUBLANES - k, rh, ru[tm - SUBLANES:])
    return jnp.concatenate([ru[:tm - SUBLANES], tail], axis=0)


def _causal_conv3(u, halo, w_ref):
    return (_shift_down(u, halo, 2) * w_ref[0:1, :] + _shift_down(u, halo, 1) * w_ref[1:2, :]) + u * w_ref[2:3, :]


def _dot(a, b):
    return jnp.dot(a, b, preferred_element_type=F32)


def _dot_nt(a, b):
    return lax.dot_general(a, b, (((1,), (1,)), ((), ())), preferred_element_type=F32)


def _dot_tn(a, b):
    return lax.dot_general(a, b, (((0,), (0,)), ((), ())), preferred_element_type=F32)


def _sigmoid(x):
    return 0.5 * jnp.tanh(0.5 * x) + 0.5


def _bucket_tables():
    qi = np.arange(WIN)[:, None]
    kj = np.arange(2 * WIN)[None, :]
    steps = np.clip(qi + WIN - kj, 0, WIN)
    out = []
    for d in DILATIONS:
        dist = steps * d
        dd = np.maximum(dist, 1).astype(np.float32)
        large = BUCKET_MAX_EXACT + (
            np.log(dd / np.float32(BUCKET_MAX_EXACT)) / np.float32(math.log(BUCKET_MAX_DISTANCE / BUCKET_MAX_EXACT))
            * np.float32(N_BUCKETS - BUCKET_MAX_EXACT)).astype(np.int32)
        large = np.minimum(large, N_BUCKETS - 1)
        out.append(np.where(dist < BUCKET_MAX_EXACT, dist, large).astype(np.int32))
    return np.stack(out)


def _band_mask():
    qi = lax.broadcasted_iota(jnp.int32, (WIN, 2 * WIN), 0)
    kj = lax.broadcasted_iota(jnp.int32, (WIN, 2 * WIN), 1)
    steps = qi + WIN - kj
    return (steps >= 0) & (steps <= WIN)


def _bias_fwd(rel_bias, buckets):
    present = [sorted(set(buckets[p].ravel().tolist())) for p in range(3)]

    def body(rb_ref, bk_ref, o_ref):
        band = _band_mask()
        for p in range(3):
            bk = bk_ref[p]
            for h in range(N_HEADS):
                acc = jnp.zeros((WIN, 2 * WIN), F32)
                for b in present[p]:
                    acc = jnp.where(bk == b, rb_ref[h, b], acc)
                o_ref[p, h] = jnp.where(band, acc, -jnp.inf)

    return pl.pallas_call(
        body, name="bias_fwd",
        out_shape=jax.ShapeDtypeStruct((3, N_HEADS, WIN, 2 * WIN), F32),
        in_specs=[pl.BlockSpec(memory_space=pltpu.SMEM), pl.BlockSpec(memory_space=pltpu.VMEM)],
        out_specs=pl.BlockSpec(memory_space=pltpu.VMEM),
    )(rel_bias, jnp.asarray(buckets))


def _bias_bwd(dbias, buckets):
    present = [set(buckets[p].ravel().tolist()) for p in range(3)]

    def body(db_ref, bk_ref, o_ref):
        lane = lax.broadcasted_iota(jnp.int32, (1, D_MODEL), 1)
        rows = []
        for h in range(N_HEADS):
            row = jnp.zeros((1, D_MODEL), F32)
            for b in range(N_BUCKETS):
                tot = jnp.zeros((1, 1), F32)
                for p in (p for p in range(3) if b in present[p]):
                    sel = jnp.where(bk_ref[p] == b, db_ref[p, h], 0.0)
                    tot = tot + jnp.sum(jnp.sum(sel, axis=0, keepdims=True), axis=1, keepdims=True)
                row = jnp.where(lane == b, tot, row)
            rows.append(row)
        o_ref[...] = jnp.concatenate(rows, axis=0)

    return pl.pallas_call(
        body, name="bias_bwd",
        out_shape=jax.ShapeDtypeStruct((N_HEADS, D_MODEL), F32),
        in_specs=[pl.BlockSpec(memory_space=pltpu.VMEM), pl.BlockSpec(memory_space=pltpu.VMEM)],
        out_specs=pl.BlockSpec(memory_space=pltpu.VMEM),
    )(dbias, jnp.asarray(buckets))


def _spread(val, scr_ref, out_refs, dtype):
    out_refs[0][...] = val.astype(dtype)
    n_blk = val.shape[1] // LANES
    for c in range(n_blk):
        scr_ref[c] = val[:, c * LANES:(c + 1) * LANES]
    for o_ref, d in zip(out_refs[1:], DILATIONS[1:]):
        for r in range(d):
            for c in range(n_blk):
                o_ref[r, :, c * LANES:(c + 1) * LANES] = scr_ref.at[c][pl.ds(r, TM // d, stride=d), :].astype(dtype)


HEAD_LANES = LANES // N_HEADS


def _head_lane(h):
    return HEAD_LANES * (h // 2) + (LANES // 2) * (h % 2)


def _narrow_heads(x):
    grp = (lax.broadcasted_iota(jnp.int32, (x.shape[0], LANES), 1) // HEAD_LANES) % (N_HEADS // 2)
    out = x[:, 0:LANES]
    for t in range(1, N_HEADS // 2):
        out = jnp.where(grp == t, x[:, t * LANES:(t + 1) * LANES], out)
    return out


def _gather_classes(blk_ref, scr_ref, d):
    n_blk = blk_ref.shape[2] // LANES
    for r in range(d):
        for c in range(n_blk):
            scr_ref.at[c][pl.ds(r, TM // d, stride=d), :] = blk_ref[r, :, c * LANES:(c + 1) * LANES].astype(F32)
    return jnp.concatenate([scr_ref[c] for c in range(n_blk)], axis=1)


def _class_specs(cols):
    return [pl.BlockSpec((TM, cols), lambda i: (i, 0))] + [
        pl.BlockSpec((d, TM // d, cols), lambda i: (0, i, 0)) for d in DILATIONS[1:]]


def _class_shapes(s, cols, dtype):
    return [jax.ShapeDtypeStruct((s, cols), dtype)] + [
        jax.ShapeDtypeStruct((d, s // d, cols), dtype) for d in DILATIONS[1:]]


def _load_w_in_pairs(w_hbm, w_scr, sems):
    @pl.when(pl.program_id(0) == 0)
    def _():
        copies = [pltpu.make_async_copy(w_hbm.at[j], w_scr.at[j // 2, :, pl.ds((j % 2) * IN_CHUNK, IN_CHUNK)],
                                        sems.at[j]) for j in range(N_DEV)]
        for copy in copies:
            copy.start()
        for copy in copies:
            copy.wait()


W_IN_PAIRS = [pltpu.VMEM((N_DEV // 2, D_MODEL, 2 * IN_CHUNK), BF16), pltpu.SemaphoreType.DMA((N_DEV,))]


def _rms_proj(x, g_mix, w_in_g, dep):
    s = x.shape[0]

    def body(x_ref, g_ref, w_hbm, dep_ref, h_ref, q1, q4, q16, k1, k4, k16, v1, v4, v16, gb_ref, gc_ref, xi_ref, scr,
             w_scr, w_sems):
        _load_w_in_pairs(w_hbm, w_scr, w_sems)
        xh, _ = _rms(x_ref[...])
        h = (xh * g_ref[...]).astype(BF16)
        h_ref[...] = h
        proj = jnp.concatenate([_dot(h, w_scr[j]) for j in range(N_DEV // 2)], axis=1)
        _spread(proj[:, 0:512] * (HEAD_DIM ** -0.5), scr, (q1, q4, q16), BF16)
        _spread(proj[:, 512:1024], scr, (k1, k4, k16), BF16)
        _spread(proj[:, 1024:1536], scr, (v1, v4, v16), BF16)
        gb_ref[...] = proj[:, 1536:2048]
        gc_ref[...] = proj[:, 2048:2560]
        xi_ref[...] = proj[:, 2560:3072]

    row = lambda n: pl.BlockSpec((TM, n), lambda i: (i, 0))
    res = pl.pallas_call(
        body, name="rms_proj", grid=(s // TM,),
        out_shape=[jax.ShapeDtypeStruct((s, D_MODEL), BF16)] + _class_shapes(s, 512, BF16) * 3
        + [jax.ShapeDtypeStruct((s, 512), F32)] * 3,
        in_specs=[row(D_MODEL), _full(g_mix.shape), ANY_SPEC, ANY_SPEC],
        out_specs=[row(D_MODEL)] + _class_specs(512) * 3 + [row(512)] * 3,
        scratch_shapes=[pltpu.VMEM((512 // LANES, TM, LANES), F32)] + W_IN_PAIRS,
        compiler_params=_cparams(1),
    )(x, g_mix, w_in_g, dep)
    return res[0], res[1:4], res[4:7], res[7:10], res[10], res[11], res[12]


def _pair_split(x2):
    lane = lax.broadcasted_iota(jnp.int32, x2.shape, 1)
    zero = jnp.zeros_like(x2)
    return jnp.where(lane < HEAD_DIM, x2, zero), jnp.where(lane >= HEAD_DIM, x2, zero)


def _pair_join(even, odd):
    lane = lax.broadcasted_iota(jnp.int32, (even.shape[0], LANES), 1)
    return jnp.where(lane < HEAD_DIM, even, odd)


def _swa_steps(qc, dil):
    n128 = qc.shape[1] // WIN
    nsub = min(SWA_BLOCKS, n128)
    nb = n128 // nsub
    ncls = min(dil, SWA_BLOCKS // nsub) if nb == 1 else 1
    return nsub, nb, ncls


def _swa_fwd(qc, kc, vc, bias, dil, dep):
    nsub, nb, ncls = _swa_steps(qc, dil)
    whole = nb == 1

    def body(q_ref, kp_ref, kc_ref, vp_ref, vc_ref, b_ref, dep_ref, o_ref, lse_ref, s_scr, p_scr):
        no_prev = (pl.program_id(1) == 0) & (lax.broadcasted_iota(jnp.int32, (WIN, 2 * WIN), 1) < WIN)
        pairs = [slice(a * LANES, (a + 1) * LANES) for a in range(N_HEADS // 2)]
        for c, t in [(c, t) for c in range(ncls) for t in range(nsub)]:
            i = c * nsub + t
            rows = slice(t * WIN, (t + 1) * WIN)
            alone = whole and t == 0
            cols = slice(WIN, 2 * WIN) if alone else slice(0, 2 * WIN)

            def keys(prev_ref, cur_ref, sl):
                if alone:
                    return cur_ref[c, rows, sl]
                if t == 0:
                    return jnp.concatenate([prev_ref[c, :, sl], cur_ref[c, rows, sl]], axis=0)
                return cur_ref[c, (t - 1) * WIN:(t + 1) * WIN, sl]

            for a, sl in enumerate(pairs):
                k2 = keys(kp_ref, kc_ref, sl)
                for e, qh in enumerate(_pair_split(q_ref[c, rows, sl])):
                    s_scr[i, 2 * a + e, :, cols] = _dot_nt(qh, k2)
            den, lse = [], []
            for h in range(N_HEADS):
                lg = s_scr[i, h, :, cols] + b_ref[h, :, cols]
                if t == 0 and not whole:
                    lg = jnp.where(no_prev, -jnp.inf, lg)
                m = jnp.max(lg, axis=-1, keepdims=True)
                p = jnp.exp(lg - m)
                den.append(jnp.sum(p, axis=-1, keepdims=True))
                p_scr[i, h, :, cols] = p.astype(BF16)
                lse.append(m + jnp.log(den[h]))
            for a, sl in enumerate(pairs):
                v_even, v_odd = _pair_split(keys(vp_ref, vc_ref, sl))
                o2 = _dot(p_scr[i, 2 * a, :, cols], v_even) + _dot(p_scr[i, 2 * a + 1, :, cols], v_odd)
                o_ref[c, rows, sl] = o2 / _pair_join(den[2 * a], den[2 * a + 1])
                lse_ref[c, rows, sl] = _pair_join(lse[2 * a], lse[2 * a + 1])

    cur = pl.BlockSpec((ncls, nsub * WIN, 512), lambda r, b: (r, b, 0))
    prev = pl.BlockSpec((ncls, WIN, 512), lambda r, b: (r, jnp.maximum(nsub * b - 1, 0), 0))
    wide = (ncls * nsub, N_HEADS, WIN, 2 * WIN)
    return pl.pallas_call(
        body, name=f"swa_fwd_d{dil}", grid=(dil // ncls, nb),
        out_shape=[jax.ShapeDtypeStruct(qc.shape, F32)] * 2,
        in_specs=[cur, prev, cur, prev, cur, _full(bias.shape), ANY_SPEC],
        out_specs=[cur] * 2,
        scratch_shapes=[pltpu.VMEM(wide, F32), pltpu.VMEM(wide, BF16)],
        compiler_params=_cparams(2),
    )(qc, kc, kc, vc, vc, bias, dep)


def _mix_out(branches, gb, gc, xi, x, w_sc, g_a, g_c, w_out):
    s = x.shape[0]
    tb = TM // SUBLANES

    def body(o1, l1, o4, l4, o16, l16, gb_ref, gc_ref, xi_ref, gch_ref, xih_ref, x_ref, wsc_ref,
             ga_ref, gcv_ref, wout_ref, attn_ref, lse1, lse4, lse16, mixed_ref, x1_ref, scr_a, scr_b, scr_c, scr_d):
        i = pl.program_id(0)
        la, lb, lc = l1[...], _gather_classes(l4, scr_a, 4), _gather_classes(l16, scr_b, 16)
        m_all = jnp.maximum(jnp.maximum(la, lb), lc)
        ea, eb, ec = jnp.exp(la - m_all), jnp.exp(lb - m_all), jnp.exp(lc - m_all)
        den = (ea + eb) + ec
        num = (ea * o1[...] + eb * _gather_classes(o4, scr_c, 4)) + ec * _gather_classes(o16, scr_d, 16)
        attn = num / den
        attn_ref[...] = attn
        _spread(_narrow_heads(m_all + jnp.log(den)), scr_a, (lse1, lse4, lse16), F32)
        xa, _ = _rms(attn)
        u = gc_ref[...] * xi_ref[...]
        uh = jnp.where(i > 0, gch_ref[...] * xih_ref[...], 0.0)
        conv = gb_ref[...] * _causal_conv3(u, uh, wsc_ref)
        xc, _ = _rms(conv)
        mixed = jnp.concatenate([xa * ga_ref[...], xc * gcv_ref[...]], axis=1).astype(BF16)
        mixed_ref[...] = mixed
        x1_ref[...] = x_ref[...] + _dot(mixed, wout_ref[...])

    row = lambda n: pl.BlockSpec((TM, n), lambda i: (i, 0))
    halo = pl.BlockSpec((SUBLANES, 512), lambda i: (jnp.maximum(i * tb - 1, 0), 0))
    cs = _class_specs(512)
    flat = [a for br in branches for a in br]
    res = pl.pallas_call(
        body, name="mix_out", grid=(s // TM,),
        out_shape=[jax.ShapeDtypeStruct((s, 512), F32)] + _class_shapes(s, LANES, F32)
        + [jax.ShapeDtypeStruct((s, D_MODEL), BF16), jax.ShapeDtypeStruct((s, D_MODEL), F32)],
        in_specs=[cs[0], cs[0], cs[1], cs[1], cs[2], cs[2], row(512), row(512), row(512), halo, halo,
                  row(D_MODEL), _full(w_sc.shape), _full(g_a.shape), _full(g_c.shape), _full(w_out.shape)],
        out_specs=[row(512)] + _class_specs(LANES) + [row(D_MODEL), row(D_MODEL)],
        scratch_shapes=[pltpu.VMEM((512 // LANES, TM, LANES), F32)] * 4,
        compiler_params=_cparams(1),
    )(*flat, gb, gc, xi, gc, xi, x, w_sc, g_a, g_c, w_out)
    return res[0], res[1:4], res[4], res[5]


def _mem_kv(mem, g_mem, w_xk, w_xv):
    def body(mem_ref, g_ref, wk_ref, wv_ref, mn_ref, k_ref, v_ref):
        xh, _ = _rms(mem_ref[...])
        mn = (xh * g_ref[...]).astype(BF16)
        mn_ref[...] = mn
        k_ref[...] = _dot(mn, wk_ref[...]).astype(BF16)
        v_ref[...] = _dot(mn, wv_ref[...]).astype(BF16)

    vm = pl.BlockSpec(memory_space=pltpu.VMEM)
    return pl.pallas_call(
        body, name="mem_kv",
        out_shape=[jax.ShapeDtypeStruct(mem.shape, BF16)] * 3,
        in_specs=[vm] * 4, out_specs=[vm] * 3,
        compiler_params=pltpu.CompilerParams(vmem_limit_bytes=VMEM_LIMIT),
    )(mem, g_mem, w_xk, w_xv)


def _xattn_fwd(x1, g, w_xq, k, v, w_xo, dep):
    s = x1.shape[0]

    def body(x1_ref, g_ref, wq_ref, k_ref, v_ref, wo_ref, dep_ref, h2_ref, q_ref, o_ref, x2_ref):
        x1v = x1_ref[...]
        xh, _ = _rms(x1v)
        h2 = (xh * g_ref[...]).astype(BF16)
        h2_ref[...] = h2
        qb = _dot(h2, wq_ref[...]).astype(BF16)
        q_ref[...] = qb
        outs = []
        for h in range(N_MEM_HEADS):
            sl = slice(h * MEM_HEAD_DIM, (h + 1) * MEM_HEAD_DIM)
            lg = _dot_nt(qb[:, sl], k_ref[:, sl]) * (MEM_HEAD_DIM ** -0.5)
            p = jnp.exp(lg - jnp.max(lg, axis=-1, keepdims=True))
            p = p / jnp.sum(p, axis=-1, keepdims=True)
            outs.append(_dot(p.astype(BF16), v_ref[:, sl]))
        o = jnp.concatenate(outs, axis=1).astype(BF16)
        o_ref[...] = o
        x2_ref[...] = x1v + _dot(o, wo_ref[...])

    row = pl.BlockSpec((TM, D_MODEL), lambda i: (i, 0))
    return pl.pallas_call(
        body, name="xattn_fwd", grid=(s // TM,),
        out_shape=[jax.ShapeDtypeStruct((s, D_MODEL), BF16)] * 3 + [jax.ShapeDtypeStruct((s, D_MODEL), F32)],
        in_specs=[row, _full(g.shape), _full(w_xq.shape), _full(k.shape), _full(v.shape), _full(w_xo.shape), ANY_SPEC],
        out_specs=[row] * 4,
        compiler_params=_cparams(1),
    )(x1, g, w_xq, k, v, w_xo, dep)


def _ffn_conv(h_ext, wup_ref, wfc_ref, bfc_ref, j):
    u = _dot_nt(h_ext, wup_ref[j])
    w = wfc_ref[j]
    c = ((pltpu.roll(u, 2, 0) * w[0:1, :] + pltpu.roll(u, 1, 0) * w[1:2, :]) + u * w[2:3, :]) + bfc_ref[j]
    return c[HALO:], u[HALO:]


def _ffn_fwd(x2, g, w_up_g, w_fc, b_fc, w_down_g, g_final, target):
    s = x2.shape[0]
    tb = TM_FFN // HALO
    n_ch, wid = w_up_g.shape[:2]
    half = n_ch // 2

    def body(x_ref, xp_ref, g_ref, wup_ref, wfc_ref, bfc_ref, wd_ref, gf_ref, t_ref, h_ref, u_ref, c_ref, act_ref,
             dx3_ref, loss_ref, dgf_ref):
        i = pl.program_id(0)

        @pl.when(i == 0)
        def _():
            loss_ref[...] = jnp.zeros_like(loss_ref)
            dgf_ref[...] = jnp.zeros_like(dgf_ref)

        x2v = x_ref[...]
        gv = g_ref[...]
        h = (_rms(x2v)[0] * gv).astype(BF16)
        h_ref[...] = h
        hp = jnp.where(i > 0, _rms(xp_ref[...])[0] * gv, 0.0).astype(BF16)
        h_ext = jnp.concatenate([hp, h], axis=0)
        down = jnp.zeros((TM_FFN, D_MODEL), F32)
        for j in range(half):
            cg, ug = _ffn_conv(h_ext, wup_ref, wfc_ref, bfc_ref, j)
            cv, uv = _ffn_conv(h_ext, wup_ref, wfc_ref, bfc_ref, j + half)
            c_ref[j] = cg
            c_ref[j + half] = cv
            u_ref[j] = ug.astype(BF16)
            u_ref[j + half] = uv.astype(BF16)
            a = ((cg * _sigmoid(cg)) * cv).astype(BF16)
            act_ref[j] = a
            down = down + _dot(a, wd_ref[j])
        x3 = x2v + down
        xh, r = _rms(x3)
        gf = gf_ref[...]
        e = xh * gf - t_ref[...]
        loss_ref[...] += 0.5 * jnp.sum(jnp.sum(e * e, axis=1, keepdims=True), axis=0, keepdims=True) / D_MODEL
        dy = e * (1.0 / D_MODEL)
        dgf_ref[0:1, :] += jnp.sum(dy * xh, axis=0, keepdims=True)
        dx3_ref[...] = _rms_bwd(xh, r, gf, dy)

    row = pl.BlockSpec((TM_FFN, D_MODEL), lambda i: (i, 0))
    prev = pl.BlockSpec((HALO, D_MODEL), lambda i: (jnp.maximum(i * tb - 1, 0), 0))
    return pl.pallas_call(
        body, name="ffn_fwd", grid=(s // TM_FFN,),
        out_shape=[jax.ShapeDtypeStruct((s, D_MODEL), BF16), jax.ShapeDtypeStruct((n_ch, s, wid), BF16),
                   jax.ShapeDtypeStruct((n_ch, s, wid), F32), jax.ShapeDtypeStruct((half, s, wid), BF16),
                   jax.ShapeDtypeStruct((s, D_MODEL), F32), jax.ShapeDtypeStruct((SUBLANES, 128), F32),
                   jax.ShapeDtypeStruct((SUBLANES, D_MODEL), F32)],
        in_specs=[row, prev, _full(g.shape), _resident(w_up_g.shape), _full(w_fc.shape), _full(b_fc.shape),
                  _resident(w_down_g.shape), _full(g_final.shape), row],
        out_specs=[row, pl.BlockSpec((n_ch, TM_FFN, wid), lambda i: (0, i, 0)),
                   pl.BlockSpec((n_ch, TM_FFN, wid), lambda i: (0, i, 0)),
                   pl.BlockSpec((half, TM_FFN, wid), lambda i: (0, i, 0)), row,
                   _full((SUBLANES, 128)), _full((SUBLANES, D_MODEL))],
        compiler_params=_cparams(1),
    )(x2, x2, g, w_up_g, w_fc, b_fc, w_down_g, g_final, target)


def _ffn_bwd(dx3, up, conv, x2, g, w_up_g, w_fc, w_down_g):
    s = x2.shape[0]
    tb = TM_FFN // HALO
    last = s // HALO - 1
    n_tiles = s // TM_FFN
    n_ch, wid = w_up_g.shape[:2]
    half = n_ch // 2
    n_ext = TM_FFN + HALO

    def body(dx_ref, dxn_ref, u_ref, c_ref, cn_ref, x2_ref, g_ref, wup_ref, wfc_ref, wd_ref,
             dup_ref, dx2_ref, dg_ref, dwfc_ref, dbfc_ref):
        i = pl.program_id(0)

        @pl.when(i == 0)
        def _():
            dg_ref[...] = jnp.zeros_like(dg_ref)
            dwfc_ref[...] = jnp.zeros_like(dwfc_ref)
            dbfc_ref[...] = jnp.zeros_like(dbfc_ref)

        dxv = dx_ref[...]
        dxn = jnp.where(i < n_tiles - 1, dxn_ref[...], 0.0)
        dx_ext = jnp.concatenate([dxv, dxn], axis=0).astype(BF16)
        dh = jnp.zeros((TM_FFN, D_MODEL), F32)
        for j in range(half):
            cg = jnp.concatenate([c_ref[j], cn_ref[j]], axis=0)
            cv = jnp.concatenate([c_ref[j + half], cn_ref[j + half]], axis=0)
            dact = _dot_nt(dx_ext, wd_ref[j])
            sg = _sigmoid(cg)
            silu = cg * sg
            parts = ((j + half, dact * silu), (j, (dact * cv) * (sg + silu * (1.0 - sg))))
            for jj, dc in parts:
                u = u_ref[jj].astype(F32)
                dc0, dc1, dc2 = dc[:TM_FFN], pltpu.roll(dc, n_ext - 1, 0)[:TM_FFN], pltpu.roll(dc, n_ext - 2, 0)[:TM_FFN]
                dbfc_ref[jj:jj + 1, :] += jnp.sum(dc0, axis=0, keepdims=True)
                dwfc_ref[0, jj:jj + 1, :] += jnp.sum(dc2 * u, axis=0, keepdims=True)
                dwfc_ref[1, jj:jj + 1, :] += jnp.sum(dc1 * u, axis=0, keepdims=True)
                dwfc_ref[2, jj:jj + 1, :] += jnp.sum(dc0 * u, axis=0, keepdims=True)
                w = wfc_ref[jj]
                du = ((dc0 * w[2:3, :] + dc1 * w[1:2, :]) + dc2 * w[0:1, :]).astype(BF16)
                dup_ref[jj] = du
                dh = dh + _dot(du, wup_ref[jj])
        xh, r = _rms(x2_ref[...])
        dg_ref[0:1, :] += jnp.sum(dh * xh, axis=0, keepdims=True)
        dx2_ref[...] = dxv + _rms_bwd(xh, r, g_ref[...], dh)

    row = pl.BlockSpec((TM_FFN, D_MODEL), lambda i: (i, 0))
    nxt = pl.BlockSpec((HALO, D_MODEL), lambda i: (jnp.minimum((i + 1) * tb, last), 0))
    cur_c = pl.BlockSpec((n_ch, TM_FFN, wid), lambda i: (0, i, 0))
    nxt_c = pl.BlockSpec((n_ch, HALO, wid), lambda i: (0, jnp.minimum((i + 1) * tb, last), 0))
    return pl.pallas_call(
        body, name="ffn_bwd", grid=(n_tiles,),
        out_shape=[jax.ShapeDtypeStruct((n_ch, s, wid), BF16), jax.ShapeDtypeStruct((s, D_MODEL), F32),
                   jax.ShapeDtypeStruct((SUBLANES, D_MODEL), F32), jax.ShapeDtypeStruct((3, n_ch, wid), F32),
                   jax.ShapeDtypeStruct((n_ch, wid), F32)],
        in_specs=[row, nxt, cur_c, cur_c, nxt_c, row, _full(g.shape), _resident(w_up_g.shape), _full(w_fc.shape),
                  _resident(w_down_g.shape)],
        out_specs=[cur_c, row, _full((SUBLANES, D_MODEL)), _full((3, n_ch, wid)), _full((n_ch, wid))],
        compiler_params=_cparams(1),
    )(dx3, dx3, up, conv, conv, x2, g, w_up_g, w_fc, w_down_g)


def _xattn_bwd(dx2, o, q, k, v, w_xo, w_xq, x1, g, dep):
    s = x1.shape[0]

    def body(dx2_ref, o_ref, q_ref, k_ref, v_ref, wo_ref, wq_ref, x1_ref, g_ref, dep_ref, dq_ref, dx1_ref, dk_ref,
             dv_ref, dg_ref):
        @pl.when(pl.program_id(0) == 0)
        def _():
            dk_ref[...] = jnp.zeros_like(dk_ref)
            dv_ref[...] = jnp.zeros_like(dv_ref)
            dg_ref[...] = jnp.zeros_like(dg_ref)

        dx2v = dx2_ref[...]
        do = _dot_nt(dx2v.astype(BF16), wo_ref[...])
        dqs = []
        for h in range(N_MEM_HEADS):
            sl = slice(h * MEM_HEAD_DIM, (h + 1) * MEM_HEAD_DIM)
            qh, kh, vh = q_ref[:, sl], k_ref[:, sl], v_ref[:, sl]
            lg = _dot_nt(qh, kh) * (MEM_HEAD_DIM ** -0.5)
            p = jnp.exp(lg - jnp.max(lg, axis=-1, keepdims=True))
            p = p / jnp.sum(p, axis=-1, keepdims=True)
            doh = do[:, sl].astype(BF16)
            dp = _dot_nt(doh, vh)
            ds = (p * (dp - jnp.sum(p * dp, axis=-1, keepdims=True)) * (MEM_HEAD_DIM ** -0.5)).astype(BF16)
            dqs.append(_dot(ds, kh))
            dk_ref[:, sl] += _dot_tn(ds, qh)
            dv_ref[:, sl] += _dot_tn(p.astype(BF16), doh)
        dq = jnp.concatenate(dqs, axis=1).astype(BF16)
        dq_ref[...] = dq
        dh2 = _dot_nt(dq, wq_ref[...])
        xh, r = _rms(x1_ref[...])
        dg_ref[0:1, :] += jnp.sum(dh2 * xh, axis=0, keepdims=True)
        dx1_ref[...] = dx2v + _rms_bwd(xh, r, g_ref[...], dh2)

    row = pl.BlockSpec((TM, D_MODEL), lambda i: (i, 0))
    return pl.pallas_call(
        body, name="xattn_bwd", grid=(s // TM,),
        out_shape=[jax.ShapeDtypeStruct((s, D_MODEL), BF16), jax.ShapeDtypeStruct((s, D_MODEL), F32),
                   jax.ShapeDtypeStruct(k.shape, F32), jax.ShapeDtypeStruct(k.shape, F32),
                   jax.ShapeDtypeStruct((SUBLANES, D_MODEL), F32)],
        in_specs=[row, row, row, _full(k.shape), _full(v.shape), _full(w_xo.shape), _full(w_xq.shape), row,
                  _full(g.shape), ANY_SPEC],
        out_specs=[row, row, _full(k.shape), _full(k.shape), _full((SUBLANES, D_MODEL))],
        compiler_params=_cparams(1),
    )(dx2, o, q, k, v, w_xo, w_xq, x1, g, dep)


def _mem_kv_bwd(dk, dv, mem_n, mem, w_xk, w_xv):
    def body(dk_ref, dv_ref, mn_ref, mem_ref, wk_ref, wv_ref, dwk_ref, dwv_ref, dg_ref):
        dkb, dvb = dk_ref[...].astype(BF16), dv_ref[...].astype(BF16)
        mn = mn_ref[...]
        dwk_ref[...] = _dot_tn(mn, dkb).astype(BF16)
        dwv_ref[...] = _dot_tn(mn, dvb).astype(BF16)
        dmn = _dot_nt(dkb, wk_ref[...]) + _dot_nt(dvb, wv_ref[...])
        xh, _ = _rms(mem_ref[...])
        dg_ref[...] = jnp.zeros_like(dg_ref)
        dg_ref[0:1, :] = jnp.sum(dmn * xh, axis=0, keepdims=True)

    vm = pl.BlockSpec(memory_space=pltpu.VMEM)
    return pl.pallas_call(
        body, name="mem_kv_bwd",
        out_shape=[jax.ShapeDtypeStruct(w_xk.shape, BF16), jax.ShapeDtypeStruct(w_xv.shape, BF16),
                   jax.ShapeDtypeStruct((SUBLANES, D_MODEL), F32)],
        in_specs=[vm] * 6, out_specs=[vm] * 3,
        compiler_params=pltpu.CompilerParams(vmem_limit_bytes=VMEM_LIMIT),
    )(dk, dv, mem_n, mem, w_xk, w_xv)


def _mix_out_bwd(dx1, w_out, attn, gb, gc, xi, w_sc, g_a, g_c, dep):
    s = dx1.shape[0]
    tb = TM // SUBLANES

    def body(dx1_ref, wout_ref, attn_ref, gb_ref, gc_ref, xi_ref, gch_ref, xih_ref, wsc_ref, ga_ref, gcv_ref, dep_ref,
             da1, da4, da16, dd1, dd4, dd16, dgb_ref, dcv_ref, dga_ref, dgc_ref, dwsc_ref, scr):
        i = pl.program_id(0)

        @pl.when(i == 0)
        def _():
            dga_ref[...] = jnp.zeros_like(dga_ref)
            dgc_ref[...] = jnp.zeros_like(dgc_ref)
            dwsc_ref[...] = jnp.zeros_like(dwsc_ref)

        dmixed = _dot_nt(dx1_ref[...].astype(BF16), wout_ref[...])
        da, dcn = dmixed[:, :ATTN_W], dmixed[:, ATTN_W:]
        attn = attn_ref[...]
        xa, ra = _rms(attn)
        dga_ref[0:1, :] += jnp.sum(da * xa, axis=0, keepdims=True)
        dattn = _rms_bwd(xa, ra, ga_ref[...], da)
        _spread(dattn, scr, (da1, da4, da16), BF16)
        prod = dattn * attn
        dd = jnp.concatenate(
            [jnp.broadcast_to(jnp.sum(prod[:, h * HEAD_DIM:(h + 1) * HEAD_DIM], axis=-1, keepdims=True),
                              (TM, HEAD_DIM)) for h in range(N_HEADS)], axis=1)
        _spread(_narrow_heads(dd), scr, (dd1, dd4, dd16), F32)
        gbv = gb_ref[...]
        u = gc_ref[...] * xi_ref[...]
        uh = jnp.where(i > 0, gch_ref[...] * xih_ref[...], 0.0)
        u2, u1 = _shift_down(u, uh, 2), _shift_down(u, uh, 1)
        cv = (u2 * wsc_ref[0:1, :] + u1 * wsc_ref[1:2, :]) + u * wsc_ref[2:3, :]
        xc, rc = _rms(gbv * cv)
        dgc_ref[0:1, :] += jnp.sum(dcn * xc, axis=0, keepdims=True)
        dconv = _rms_bwd(xc, rc, gcv_ref[...], dcn)
        dgb_ref[...] = (dconv * cv).astype(BF16)
        dcv = dconv * gbv
        dcv_ref[...] = dcv
        dwsc_ref[0:1, :] += jnp.sum(dcv * u2, axis=0, keepdims=True)
        dwsc_ref[1:2, :] += jnp.sum(dcv * u1, axis=0, keepdims=True)
        dwsc_ref[2:3, :] += jnp.sum(dcv * u, axis=0, keepdims=True)

    row = lambda n: pl.BlockSpec((TM, n), lambda i: (i, 0))
    halo = pl.BlockSpec((SUBLANES, 512), lambda i: (jnp.maximum(i * tb - 1, 0), 0))
    acc = _full((SUBLANES, 512))
    res = pl.pallas_call(
        body, name="mix_out_bwd", grid=(s // TM,),
        out_shape=_class_shapes(s, 512, BF16) + _class_shapes(s, LANES, F32)
        + [jax.ShapeDtypeStruct((s, 512), BF16), jax.ShapeDtypeStruct((s, 512), F32)]
        + [jax.ShapeDtypeStruct((SUBLANES, 512), F32)] * 3,
        in_specs=[row(D_MODEL), _full(w_out.shape), row(512), row(512), row(512), row(512), halo, halo,
                  _full(w_sc.shape), _full(g_a.shape), _full(g_c.shape), ANY_SPEC],
        out_specs=_class_specs(512) + _class_specs(LANES) + [row(512)] * 2 + [acc] * 3,
        scratch_shapes=[pltpu.VMEM((512 // LANES, TM, LANES), F32)],
        compiler_params=_cparams(1),
    )(dx1, w_out, attn, gb, gc, xi, gc, xi, w_sc, g_a, g_c, dep)
    return res[0:3], res[3:6], res[6], res[7], res[8], res[9], res[10]


def _swa_bwd(qc, kc, vc, doc, lsec, ddc, bias, dil, dep):
    nsub, nb, ncls = _swa_steps(qc, dil)
    n128 = nsub * nb
    whole = nb == 1

    def body(q_ref, qn_ref, kp_ref, kc_ref, vp_ref, vc_ref, do_ref, don_ref, lse_ref, lsen_ref, dd_ref, ddn_ref,
             b_ref, dep_ref, dq_ref, dk_ref, dv_ref, db_ref, s_scr, dp_scr, sn_scr, dpn_scr, ds_scr, p_scr, dsn_scr,
             pn_scr):
        r, b = pl.program_id(0), pl.program_id(1)

        @pl.when((r == 0) & (b == 0))
        def _():
            db_ref[...] = jnp.zeros_like(db_ref)

        pairs = [slice(a * LANES, (a + 1) * LANES) for a in range(N_HEADS // 2)]
        blk = [slice(t * WIN, (t + 1) * WIN) for t in range(nsub)]
        last = blk[nsub - 1]
        cols = lambda t: slice(WIN, 2 * WIN) if whole and t == 0 else slice(0, 2 * WIN)
        of_head = lambda ref, c, rows, h: ref[c, rows, _head_lane(h):_head_lane(h) + 1]
        no_prev = (b == 0) & (lax.broadcasted_iota(jnp.int32, (WIN, 2 * WIN), 1) < WIN)

        def keys(prev_ref, cur_ref, c, t, sl):
            if whole and t == 0:
                return cur_ref[c, blk[0], sl]
            if t == 0:
                return jnp.concatenate([prev_ref[c, :, sl], cur_ref[c, blk[0], sl]], axis=0)
            return cur_ref[c, (t - 1) * WIN:(t + 1) * WIN, sl]

        for a, sl in enumerate(pairs):
            for c, t in [(c, t) for c in range(ncls) for t in range(nsub)]:
                k2, v2 = keys(kp_ref, kc_ref, c, t, sl), keys(vp_ref, vc_ref, c, t, sl)
                q_eo = _pair_split(q_ref[c, blk[t], sl])
                do_eo = _pair_split(do_ref[c, blk[t], sl].astype(BF16))
                for e in range(2):
                    s_scr[c * nsub + t, 2 * a + e, :, cols(t)] = _dot_nt(q_eo[e], k2)
                    dp_scr[c * nsub + t, 2 * a + e, :, cols(t)] = _dot_nt(do_eo[e], v2)
            if not whole:
                qn_eo = _pair_split(qn_ref[0, :, sl])
                don_eo = _pair_split(don_ref[0, :, sl].astype(BF16))
                for e in range(2):
                    sn_scr[2 * a + e] = _dot_nt(qn_eo[e], kc_ref[0, last, sl])
                    dpn_scr[2 * a + e] = _dot_nt(don_eo[e], vc_ref[0, last, sl])
        for c, t, h in [(c, t, h) for c in range(ncls) for t in range(nsub) for h in range(N_HEADS)]:
            i, cl = c * nsub + t, cols(t)
            lg = s_scr[i, h, :, cl] + b_ref[h, :, cl]
            if t == 0 and not whole:
                lg = jnp.where(no_prev, -jnp.inf, lg)
            p = jnp.exp(lg - of_head(lse_ref, c, blk[t], h))
            ds = p * (dp_scr[i, h, :, cl] - of_head(dd_ref, c, blk[t], h))
            db_ref[h, :, cl] += ds
            ds_scr[i, h, :, cl] = ds.astype(BF16)
            p_scr[i, h, :, cl] = p.astype(BF16)
        if not whole:
            every = slice(0, WIN)
            for h in range(N_HEADS):
                lgn = jnp.where(b + 1 < nb, sn_scr[h] + b_ref[h, :, :WIN], -jnp.inf)
                pn = jnp.exp(lgn - of_head(lsen_ref, 0, every, h))
                dsn_scr[h] = (pn * (dpn_scr[h] - of_head(ddn_ref, 0, every, h))).astype(BF16)
                pn_scr[h] = pn.astype(BF16)
        for a, sl in enumerate(pairs):
            for c in range(ncls):
                q_eo = [_pair_split(q_ref[c, blk[t], sl]) for t in range(nsub)]
                do_eo = [_pair_split(do_ref[c, blk[t], sl].astype(BF16)) for t in range(nsub)]
                if not whole:
                    q_eo.append(_pair_split(qn_ref[0, :, sl]))
                    do_eo.append(_pair_split(don_ref[0, :, sl].astype(BF16)))
                for t in range(nsub):
                    i = c * nsub + t
                    k_eo = _pair_split(keys(kp_ref, kc_ref, c, t, sl))
                    dq, dk, dv = None, None, None
                    for e in range(2):
                        h = 2 * a + e
                        terms = [_dot(ds_scr[i, h, :, cols(t)], k_eo[e]),
                                 _dot_tn(ds_scr[i, h, :, WIN:], q_eo[t][e]),
                                 _dot_tn(p_scr[i, h, :, WIN:], do_eo[t][e])]
                        if t + 1 < nsub or not whole:
                            ds_next = ds_scr[i + 1, h, :, :WIN] if t + 1 < nsub else dsn_scr[h]
                            p_next = p_scr[i + 1, h, :, :WIN] if t + 1 < nsub else pn_scr[h]
                            terms[1] += _dot_tn(ds_next, q_eo[t + 1][e])
                            terms[2] += _dot_tn(p_next, do_eo[t + 1][e])
                        dq, dk, dv = terms if e == 0 else (dq + terms[0], dk + terms[1], dv + terms[2])
                    dq_ref[c, blk[t], sl] = dq.astype(BF16)
                    dk_ref[c, blk[t], sl] = dk.astype(BF16)
                    dv_ref[c, blk[t], sl] = dv.astype(BF16)

    cur = pl.BlockSpec((ncls, nsub * WIN, 512), lambda r, b: (r, b, 0))
    prev = pl.BlockSpec((ncls, WIN, 512), lambda r, b: (r, jnp.maximum(nsub * b - 1, 0), 0))
    nxt = pl.BlockSpec((ncls, WIN, 512), lambda r, b: (r, jnp.minimum(nsub * b + nsub, n128 - 1), 0))
    cur_h = pl.BlockSpec((ncls, nsub * WIN, LANES), cur.index_map)
    nxt_h = pl.BlockSpec((ncls, WIN, LANES), nxt.index_map)
    wide, narrow = (ncls * nsub, N_HEADS, WIN, 2 * WIN), (N_HEADS, WIN, WIN)
    return pl.pallas_call(
        body, name=f"swa_bwd_d{dil}", grid=(dil // ncls, nb),
        out_shape=[jax.ShapeDtypeStruct(qc.shape, BF16)] * 3 + [jax.ShapeDtypeStruct(bias.shape, F32)],
        in_specs=[cur, nxt, prev, cur, prev, cur, cur, nxt, cur_h, nxt_h, cur_h, nxt_h, _full(bias.shape),
                  ANY_SPEC],
        out_specs=[cur] * 3 + [_full(bias.shape)],
        scratch_shapes=[pltpu.VMEM(wide, F32), pltpu.VMEM(wide, F32), pltpu.VMEM(narrow, F32),
                        pltpu.VMEM(narrow, F32), pltpu.VMEM(wide, BF16), pltpu.VMEM(wide, BF16),
                        pltpu.VMEM(narrow, BF16), pltpu.VMEM(narrow, BF16)],
        compiler_params=_cparams(2),
    )(qc, qc, kc, kc, vc, vc, doc, doc, lsec, lsec, ddc, ddc, bias, dep)


def _in_proj_bwd(dqs, dks, dvs, dgb, dcv, gc, xi, w_sc, w_in_g, x, g_mix, dx1):
    s = x.shape[0]
    tb = TM // SUBLANES
    last = s // SUBLANES - 1
    n_tiles = s // TM

    def body(dq1, dq4, dq16, dk1, dk4, dk16, dv1, dv4, dv16, dgb_ref, dcv_ref, dcvn_ref, gc_ref, xi_ref, wsc_ref,
             w_hbm, x_ref, g_ref, dx1_ref, dproj_ref, gx_ref, dg_ref, scr_a, scr_b, w_scr, w_sems):
        i = pl.program_id(0)
        _load_w_in_pairs(w_hbm, w_scr, w_sems)

        @pl.when(i == 0)
        def _():
            dg_ref[...] = jnp.zeros_like(dg_ref)

        d0 = dcv_ref[...]
        dn = jnp.where(i < n_tiles - 1, dcvn_ref[...], 0.0)
        du = (d0 * wsc_ref[2:3, :] + _shift_up(d0, dn, 1) * wsc_ref[1:2, :]) + _shift_up(d0, dn, 2) * wsc_ref[0:1, :]
        merge = lambda a, b4, b16: ((a[...].astype(F32) + _gather_classes(b4, scr_a, 4))
                                    + _gather_classes(b16, scr_b, 16))
        dq = merge(dq1, dq4, dq16) * (HEAD_DIM ** -0.5)
        dk = merge(dk1, dk4, dk16)
        dv = merge(dv1, dv4, dv16)
        dproj = jnp.concatenate([dq, dk, dv, dgb_ref[...].astype(F32), du * xi_ref[...], du * gc_ref[...]],
                                axis=1).astype(BF16)
        dproj_ref[...] = dproj
        dh = jnp.zeros((TM, D_MODEL), F32)
        for j in range(N_DEV // 2):
            dh = dh + _dot_nt(dproj[:, 2 * j * IN_CHUNK:2 * (j + 1) * IN_CHUNK], w_scr[j])
        xh, r = _rms(x_ref[...])
        dg_ref[0:1, :] += jnp.sum(dh * xh, axis=0, keepdims=True)
        gx_ref[...] = dx1_ref[...] + _rms_bwd(xh, r, g_ref[...], dh)

    row = lambda n: pl.BlockSpec((TM, n), lambda i: (i, 0))
    nxt = pl.BlockSpec((SUBLANES, 512), lambda i: (jnp.minimum((i + 1) * tb, last), 0))
    return pl.pallas_call(
        body, name="in_proj_bwd", grid=(n_tiles,),
        out_shape=[jax.ShapeDtypeStruct((s, IN_COLS), BF16), jax.ShapeDtypeStruct((s, D_MODEL), F32),
                   jax.ShapeDtypeStruct((SUBLANES, D_MODEL), F32)],
        in_specs=_class_specs(512) * 3 + [row(512), row(512), nxt, row(512), row(512), _full(w_sc.shape),
                                          ANY_SPEC, row(D_MODEL), _full(g_mix.shape), row(D_MODEL)],
        out_specs=[row(IN_COLS), row(D_MODEL), _full((SUBLANES, D_MODEL))],
        scratch_shapes=[pltpu.VMEM((512 // LANES, TM, LANES), F32)] * 2 + W_IN_PAIRS,
        compiler_params=_cparams(1),
    )(*dqs, *dks, *dvs, dgb, dcv, dcv, gc, xi, w_sc, w_in_g, x, g_mix, dx1)


def _dw(a, b, dep, name, a_chunked=False, b_chunked=False, n_chunks=1, chunk_cols=None, per_step=1):
    single = not (a_chunked or b_chunked or chunk_cols)
    wide = a_chunked and a.shape[2] > D_MODEL
    ts = TS_DW // 2 if single or wide else TS_DW
    if a_chunked:
        nj, s, kk = a.shape
        nn = b.shape[1]
        a_spec = pl.BlockSpec((1, ts, kk), lambda j, t: (j, t, 0))
        b_spec = pl.BlockSpec((ts, nn), lambda j, t: (t, 0))
    elif b_chunked:
        nj, s, nn = b.shape
        kk = a.shape[1]
        a_spec = pl.BlockSpec((ts, kk), lambda j, t: (t, 0))
        b_spec = pl.BlockSpec((1, ts, nn), lambda j, t: (j, t, 0))
    else:
        s, kk = a.shape
        nj, nn = (n_chunks // per_step, chunk_cols * per_step) if chunk_cols else (1, b.shape[1])
        a_spec = pl.BlockSpec((ts, kk), lambda j, t: (t, 0))
        b_spec = pl.BlockSpec((ts, nn), lambda j, t: (t, j))
    n_steps = s // ts

    def body(a_ref, b_ref, dep_ref, o_ref, acc):
        t = pl.program_id(1)

        @pl.when(t == 0)
        def _():
            acc[...] = jnp.zeros_like(acc)

        av = (a_ref[0] if a_chunked else a_ref[...]).astype(BF16)
        bv = (b_ref[0] if b_chunked else b_ref[...]).astype(BF16)
        acc[...] += _dot_tn(av, bv)

        @pl.when(t == n_steps - 1)
        def _():
            for q in range(per_step):
                o_ref[q] = acc[:, q * nn // per_step:(q + 1) * nn // per_step].astype(BF16)

    return pl.pallas_call(
        body, name=name, grid=(nj, n_steps),
        out_shape=jax.ShapeDtypeStruct((nj * per_step, kk, nn // per_step), BF16),
        in_specs=[a_spec, b_spec, ANY_SPEC],
        out_specs=pl.BlockSpec((per_step, kk, nn // per_step), lambda j, t: (j, 0, 0)),
        scratch_shapes=[pltpu.VMEM((kk, nn), F32)],
        compiler_params=_cparams(2),
    )(a, b, dep)


def _adamw_math(w, g, m, v):
    m2 = ADAM_B1 * m + (1.0 - ADAM_B1) * g
    v2 = ADAM_B2 * v + (1.0 - ADAM_B2) * (g * g)
    m_hat = m2 / (1.0 - ADAM_B1 ** ADAM_STEP)
    v_hat = v2 / (1.0 - ADAM_B2 ** ADAM_STEP)
    delta = -ADAM_LR * (m_hat / (jnp.sqrt(v_hat) + ADAM_EPS) + ADAM_WD * w)
    return delta, m2, v2


def _sum_parts(me, own, p_ref):
    g = None
    for i in range(N_DEV):
        part = jnp.where(me == i, own.astype(F32), p_ref[i].astype(F32))
        g = part if g is None else g + part
    return g


def _adamw_big(name, w, sent, parts, m, v, me_arr):
    rr, cc = w.shape
    tr = rr // 4 if rr >= 512 else rr

    def body(me_ref, w_ref, own_ref, p_ref, m_ref, v_ref, g_ref, d_ref, nm_ref, nv_ref):
        g = own_ref[0].astype(F32)
        for k in range(1, N_DEV):
            g = g + p_ref[(me_ref[0] + k) % N_DEV].astype(F32)
        g_ref[...] = g
        d_ref[...], nm_ref[...], nv_ref[...] = _adamw_math(w_ref[...], g, m_ref[...], v_ref[...])

    row = pl.BlockSpec((tr, cc), lambda i, me: (i, 0))
    return pl.pallas_call(
        body, name=name,
        grid_spec=pltpu.PrefetchScalarGridSpec(
            num_scalar_prefetch=1, grid=(rr // tr,),
            in_specs=[row, pl.BlockSpec((1, tr, cc), lambda i, me: (me[0], i, 0)),
                      pl.BlockSpec((N_DEV, tr, cc), lambda i, me: (0, i, 0)), row, row],
            out_specs=[row] * 4),
        out_shape=[jax.ShapeDtypeStruct((rr, cc), F32)] * 4,
        compiler_params=_cparams(1),
    )(me_arr, w, sent, parts, m, v)


def _small_slices():
    return [
        (slice(ROW_RELB, ROW_RELB + 8), slice(0, N_BUCKETS)),
        (slice(ROW_GMIX, ROW_GMIX + 1), slice(0, D_MODEL)),
        (slice(ROW_GAC, ROW_GAC + 1), slice(0, ATTN_W)),
        (slice(ROW_GAC, ROW_GAC + 1), slice(ATTN_W, D_MODEL)),
        (slice(ROW_GXATTN, ROW_GXATTN + 1), slice(0, D_MODEL)),
        (slice(ROW_GMEM, ROW_GMEM + 1), slice(0, D_MODEL)),
        (slice(ROW_GFFN, ROW_GFFN + 1), slice(0, D_MODEL)),
        (slice(ROW_BFC, ROW_BFC + 8), slice(0, UP_CHUNK)),
        (slice(ROW_GFINAL, ROW_GFINAL + 1), slice(0, D_MODEL)),
    ]


def _adamw_small(own, parts, wmv, me_arr):
    slices = _small_slices()
    n = len(slices)

    def body(*refs):
        me_ref, own_ref, p_ref = refs[:3]
        ins = refs[3:3 + 3 * n]
        g_ref = refs[3 + 3 * n]
        outs = refs[4 + 3 * n:]
        g = _sum_parts(me_ref[0], own_ref[...], p_ref)
        g_ref[...] = g
        for a, (rs, ls) in enumerate(slices):
            ga = g[rs, ls]
            outs[4 * a][...] = ga
            outs[4 * a + 1][...], outs[4 * a + 2][...], outs[4 * a + 3][...] = _adamw_math(
                ins[3 * a][...], ga, ins[3 * a + 1][...], ins[3 * a + 2][...])

    vm = pl.BlockSpec(memory_space=pltpu.VMEM)
    flat = [t for trip in wmv for t in trip]
    out_shape = [jax.ShapeDtypeStruct((SMALL_ROWS, D_MODEL), F32)]
    for w, _, _ in wmv:
        out_shape += [jax.ShapeDtypeStruct(w.shape, F32)] * 4
    res = pl.pallas_call(
        body, name="adamw_small", out_shape=out_shape,
        in_specs=[SMEM_SPEC] + [vm] * (2 + 3 * n), out_specs=[vm] * len(out_shape),
    )(me_arr, own, parts, *flat)
    return res[0], [res[1 + 4 * a:5 + 4 * a] for a in range(n)]


def _adamw_shards(items):
    n = len(items)

    def body(*refs):
        for a in range(n):
            w_ref, g_ref, m_ref, v_ref = refs[4 * a:4 * a + 4]
            d_ref, nm_ref, nv_ref = refs[4 * n + 3 * a:4 * n + 3 * a + 3]
            d_ref[...], nm_ref[...], nv_ref[...] = _adamw_math(w_ref[...], g_ref[...], m_ref[...], v_ref[...])

    vm = pl.BlockSpec(memory_space=pltpu.VMEM)
    out_shape = []
    for w, _, _, _ in items:
        out_shape += [jax.ShapeDtypeStruct(w.shape, F32)] * 3
    res = pl.pallas_call(
        body, name="adamw_shards", out_shape=out_shape, in_specs=[vm] * (4 * n), out_specs=[vm] * (3 * n),
    )(*[t for it in items for t in it])
    return [res[3 * a:3 * a + 3] for a in range(n)]


def _mesh_pos():
    return lax.axis_index("x"), lax.axis_index("y"), lax.axis_index("c")


def _dev_index(p):
    return 4 * p[0] + 2 * p[1] + p[2]


def _all_gather(shards):
    n = len(shards)

    def body(*refs):
        ins, outs = refs[:n], refs[n:2 * n]
        send_sems, recv_sems, loc_sems = refs[2 * n:]
        x, y, c = _mesh_pos()
        me, sib = (x, y, c), (x, y, 1 - c)
        chips = [(1 - x, y), (x, 1 - y), (1 - x, 1 - y)]

        def cp(a, k, block, to, src=None):
            dst = outs[a].at[_dev_index(block)]
            return pltpu.make_async_remote_copy(
                src_ref=dst if src is None else src, dst_ref=dst, send_sem=send_sems.at[a, k],
                recv_sem=recv_sems.at[a, k], device_id=to, device_id_type=MESH)

        mine = [pltpu.make_async_copy(ins[a], outs[a].at[_dev_index(me)], loc_sems.at[a]) for a in range(n)]
        for m_ in mine:
            m_.start()
        first = []
        for a in range(n):
            first.append(cp(a, 0, me, sib, src=ins[a]))
            first += [cp(a, 1 + j, me, (*chip, c), src=ins[a]) for j, chip in enumerate(chips)]
        for f in first:
            f.start()
        passed = []
        for a in range(n):
            for j, chip in enumerate(chips):
                cp(a, 1 + j, (*chip, c), me).wait_recv()
                fwd = cp(a, 4 + j, (*chip, c), sib)
                fwd.start()
                passed.append(fwd)
        for a in range(n):
            cp(a, 0, sib, me).wait_recv()
            for j, chip in enumerate(chips):
                cp(a, 4 + j, (*chip, 1 - c), me).wait_recv()
        for f in first + passed:
            f.wait_send()
        for m_ in mine:
            m_.wait()

    hbm = pl.BlockSpec(memory_space=pltpu.HBM)
    return pl.pallas_call(
        body, name="all_gather_weights",
        out_shape=[jax.ShapeDtypeStruct((N_DEV,) + a.shape, a.dtype) for a in shards],
        in_specs=[hbm] * n, out_specs=[hbm] * n,
        scratch_shapes=[pltpu.SemaphoreType.DMA((n, 7)), pltpu.SemaphoreType.DMA((n, 7)),
                        pltpu.SemaphoreType.DMA((n,))],
    )(*shards)


def _peers():
    x, y, c = _mesh_pos()
    return (x, y, c), [((1 - x) if k & 4 else x, (1 - y) if k & 2 else y, (1 - c) if k & 1 else c)
                       for k in range(1, 8)]


def _exchange_copy(src_ref, land_ref, whole, send_sems, recv_sems, a, k, peer, slot):
    src = src_ref if whole else src_ref.at[_dev_index(peer)]
    return pltpu.make_async_remote_copy(
        src_ref=src, dst_ref=land_ref.at[slot], send_sem=send_sems.at[7 * a + k], recv_sem=recv_sems.at[7 * a + k],
        device_id=peer, device_id_type=MESH)


def _exchange_start(name, srcs, whole, dep):
    n = len(srcs)
    lands = [lax.empty(((N_DEV,) + s.shape) if w else s.shape, s.dtype) for s, w in zip(srcs, whole)]

    def body(*refs):
        src_refs, land_refs = refs[:n], refs[n:2 * n]
        send_sems, recv_sems, token = refs[2 * n + 1], refs[2 * n + 2], refs[-1]
        me, peers = _peers()
        for a in range(n):
            for k, peer in enumerate(peers):
                _exchange_copy(src_refs[a], land_refs[a], whole[a], send_sems, recv_sems, a, k, peer,
                               _dev_index(me)).start()
        token[...] = jnp.zeros_like(token)

    res = pl.pallas_call(
        body, name=name,
        out_shape=(pltpu.SemaphoreType.DMA((7 * n,)), pltpu.SemaphoreType.DMA((7 * n,)),
                   *[pltpu.HBM(a.shape, a.dtype) for a in srcs], *[pltpu.HBM(a.shape, a.dtype) for a in lands],
                   jax.ShapeDtypeStruct((SUBLANES, 128), F32)),
        in_specs=[HBM_SPEC] * (2 * n) + [ANY_SPEC],
        out_specs=(SEM_SPEC, SEM_SPEC, *([HBM_SPEC] * (2 * n)), VMEM_SPEC),
        input_output_aliases={i: 2 + i for i in range(2 * n)},
        compiler_params=pltpu.CompilerParams(has_side_effects=DATAFLOW),
    )(*[pltpu.with_memory_space_constraint(a, pltpu.HBM) for a in srcs],
      *[pltpu.with_memory_space_constraint(a, pltpu.HBM) for a in lands], dep)
    return res[0], res[1], list(res[2:2 + n]), list(res[2 + n:2 + 2 * n]), res[-1]


def _exchange_wait(name, started, whole, after, which=None):
    send_sems, recv_sems, srcs, lands, _ = started
    which = list(range(len(srcs))) if which is None else which
    srcs, lands = [srcs[a] for a in which], [lands[a] for a in which]
    n = len(srcs)

    def body(*refs):
        src_refs, land_refs = refs[:n], refs[n:2 * n]
        send_sems, recv_sems = refs[2 * n], refs[2 * n + 1]
        _, peers = _peers()
        for i, a in enumerate(which):
            for k, peer in enumerate(peers):
                cp = _exchange_copy(src_refs[i], land_refs[i], whole[a], send_sems, recv_sems, a, k, peer,
                                    _dev_index(peer))
                cp.wait_send()
                cp.wait_recv()

    res = pl.pallas_call(
        body, name=name,
        out_shape=[pltpu.HBM(a.shape, a.dtype) for a in srcs + lands],
        in_specs=[HBM_SPEC] * (2 * n) + [SEM_SPEC, SEM_SPEC, ANY_SPEC],
        out_specs=[HBM_SPEC] * (2 * n),
        input_output_aliases={i: i for i in range(2 * n)},
        compiler_params=pltpu.CompilerParams(has_side_effects=DATAFLOW),
    )(*srcs, *lands, send_sems, recv_sems, after)
    return list(res[:n]), list(res[n:])


def _gather_start(name, shards, dep):
    n = len(shards)
    lands = [lax.empty((N_DEV,) + a.shape, a.dtype) for a in shards]

    def body(*refs):
        src_refs, land_refs = refs[:n], refs[n:2 * n]
        send_sems, recv_sems, token = refs[2 * n + 1], refs[2 * n + 2], refs[-1]
        x, y, c = _mesh_pos()
        peers = [(x, y, 1 - c), (1 - x, y, c), (x, 1 - y, c), (1 - x, 1 - y, c)]
        for a in range(n):
            for k, peer in enumerate(peers):
                pltpu.make_async_remote_copy(
                    src_ref=src_refs[a], dst_ref=land_refs[a].at[_dev_index((x, y, c))], send_sem=send_sems.at[4 * a + k],
                    recv_sem=recv_sems.at[4 * a + k], device_id=peer, device_id_type=MESH).start()
        token[...] = jnp.zeros_like(token)

    res = pl.pallas_call(
        body, name=name,
        out_shape=(pltpu.SemaphoreType.DMA((4 * n,)), pltpu.SemaphoreType.DMA((4 * n,)),
                   *[pltpu.HBM(a.shape, a.dtype) for a in shards], *[pltpu.HBM(a.shape, a.dtype) for a in lands],
                   jax.ShapeDtypeStruct((SUBLANES, 128), F32)),
        in_specs=[HBM_SPEC] * (2 * n) + [ANY_SPEC],
        out_specs=(SEM_SPEC, SEM_SPEC, *([HBM_SPEC] * (2 * n)), VMEM_SPEC),
        input_output_aliases={i: 2 + i for i in range(2 * n)},
        compiler_params=pltpu.CompilerParams(has_side_effects=DATAFLOW),
    )(*[pltpu.with_memory_space_constraint(a, pltpu.HBM) for a in shards],
      *[pltpu.with_memory_space_constraint(a, pltpu.HBM) for a in lands], dep)
    return res[0], res[1], list(res[2:2 + n]), list(res[2 + n:2 + 2 * n]), res[-1]


def _gather_forward(name, send_sems, recv_sems, lands, which, after):
    n = len(which)

    def body(*refs):
        land_refs = refs[:n]
        send_sems, recv_sems = refs[n], refs[n + 1]
        fsend, frecv, token = refs[n + 3], refs[n + 4], refs[-1]
        x, y, c = _mesh_pos()
        chips = [(1 - x, y), (x, 1 - y), (1 - x, 1 - y)]
        for i, a in enumerate(which):
            for j, chip in enumerate(chips):
                block = land_refs[i].at[_dev_index((*chip, c))]
                pltpu.make_async_remote_copy(
                    src_ref=block, dst_ref=block, send_sem=send_sems.at[4 * a + 1 + j], recv_sem=recv_sems.at[4 * a + 1 + j],
                    device_id=(*chip, c), device_id_type=MESH).wait_recv()
                pltpu.make_async_remote_copy(
                    src_ref=block, dst_ref=block, send_sem=fsend.at[3 * i + j], recv_sem=frecv.at[3 * i + j],
                    device_id=(x, y, 1 - c), device_id_type=MESH).start()
        token[...] = jnp.zeros_like(token)

    res = pl.pallas_call(
        body, name=name,
        out_shape=(pltpu.SemaphoreType.DMA((3 * n,)), pltpu.SemaphoreType.DMA((3 * n,)),
                   *[pltpu.HBM(a.shape, a.dtype) for a in lands], jax.ShapeDtypeStruct((SUBLANES, 128), F32)),
        in_specs=[HBM_SPEC] * n + [SEM_SPEC, SEM_SPEC, ANY_SPEC],
        out_specs=(SEM_SPEC, SEM_SPEC, *([HBM_SPEC] * n), VMEM_SPEC),
        input_output_aliases={i: 2 + i for i in range(n)},
        compiler_params=pltpu.CompilerParams(has_side_effects=DATAFLOW),
    )(*lands, send_sems, recv_sems, after)
    return res[0], res[1], list(res[2:2 + n]), res[-1]


def _gather_wait(name, send_sems, recv_sems, fsend, frecv, srcs, lands, which, after):
    n = len(which)

    def body(*refs):
        land_refs = refs[n:2 * n]
        send_sems, recv_sems, fsend, frecv = refs[2 * n:2 * n + 4]
        x, y, c = _mesh_pos()
        sib = (x, y, 1 - c)
        chips = [(1 - x, y), (x, 1 - y), (1 - x, 1 - y)]
        for i, a in enumerate(which):
            def cp(slot, ssem, rsem):
                block = land_refs[i].at[_dev_index(slot)]
                return pltpu.make_async_remote_copy(src_ref=block, dst_ref=block, send_sem=ssem, recv_sem=rsem,
                                                    device_id=sib, device_id_type=MESH)
            cp(sib, send_sems.at[4 * a], recv_sems.at[4 * a]).wait_recv()
            for j, chip in enumerate(chips):
                cp((*chip, 1 - c), fsend.at[3 * i + j], frecv.at[3 * i + j]).wait_recv()
            for k in range(4):
                cp(sib, send_sems.at[4 * a + k], recv_sems.at[4 * a + k]).wait_send()
            for j in range(3):
                cp(sib, fsend.at[3 * i + j], frecv.at[3 * i + j]).wait_send()

    res = pl.pallas_call(
        body, name=name,
        out_shape=[pltpu.HBM(a.shape, a.dtype) for a in srcs + lands],
        in_specs=[HBM_SPEC] * (2 * n) + [SEM_SPEC] * 4 + [ANY_SPEC],
        out_specs=[HBM_SPEC] * (2 * n),
        input_output_aliases={i: i for i in range(2 * n)},
        compiler_params=pltpu.CompilerParams(has_side_effects=DATAFLOW),
    )(*srcs, *lands, send_sems, recv_sems, fsend, frecv, after)
    return list(res[n:])


def _local_step(x, mem, target, rel_bias, g_mix, w_in_g, w_sc, g_a, g_c, g_xattn, g_mem, g_ffn, w_fc, b_fc, g_final,
                dep, forward_weights, late_weights, emit, emit_small):
    s = x.shape[0]
    buckets = _bucket_tables()
    bias = _bias_fwd(rel_bias, buckets)

    h1, qs, ks, vs, gb, gc, xi = _rms_proj(x, g_mix, w_in_g, dep)
    qs, ks, vs = ([a[0][None]] + list(a[1:]) for a in (qs, ks, vs))
    group1, group2 = ["w_out", "w_xq", "w_xk", "w_xv", "w_xo"], ["w_up", "w_down"]
    tok = forward_weights(group1, h1)
    branches = []
    for p, dil in enumerate(DILATIONS):
        o_p, lse_p = _swa_fwd(qs[p], ks[p], vs[p], bias[p], dil, tok)
        branches.append([o_p[0], lse_p[0]] if dil == 1 else [o_p, lse_p])
    lw = late_weights(group1, branches[-1][0])
    w_out, w_xq, w_xk, w_xv, w_xo = (lw[n] for n in group1)
    attn, lses, mixed, x1 = _mix_out(branches, gb, gc, xi, x, w_sc, g_a, g_c, w_out)
    tok = forward_weights(group2, x1)
    mem_n, mk, mv = _mem_kv(mem, g_mem, w_xk, w_xv)
    h2, xq, xo, x2 = _xattn_fwd(x1, g_xattn, w_xq, mk, mv, w_xo, tok)
    lw = late_weights(group2, x2)
    w_up_g = lw["w_up"].reshape(FFN_CHUNKS, FFN_WIDTH, D_MODEL)
    w_down_g = lw["w_down"].reshape(FFN_CHUNKS // 2, FFN_WIDTH, D_MODEL)
    pairs = lambda a: a.reshape(FFN_CHUNKS, 2, a.shape[1], UP_CHUNK).transpose(0, 2, 1, 3).reshape(
        FFN_CHUNKS, a.shape[1], FFN_WIDTH)
    w_fc, b_fc = pairs(w_fc), pairs(b_fc)
    h3, up, conv, act, dx3, loss_acc, dg_final = _ffn_fwd(x2, g_ffn, w_up_g, w_fc, b_fc, w_down_g, g_final, target)

    gw_down = _dw(act, dx3, dep, "dw_down", a_chunked=True).reshape(N_DEV // 2, UP_CHUNK, D_MODEL)
    dup, dx2, dg_ffn, dw_fc, db_fc = _ffn_bwd(dx3, up, conv, x2, g_ffn, w_up_g, w_fc, w_down_g)
    gw_up = _dw(dup, h3, dep, "dw_up", a_chunked=True).reshape(N_DEV, UP_CHUNK, D_MODEL)
    tok = emit(dict(w_down=gw_down, w_up=gw_up))
    dxq, dx1, dmk, dmv, dg_xattn = _xattn_bwd(dx2, xo, xq, mk, mv, w_xo, w_xq, x1, g_xattn, tok)
    gw_xo = _dw(xo, dx2, tok, "dw_xo")[0]
    gw_xq = _dw(h2, dxq, tok, "dw_xq")[0]
    gw_xk, gw_xv, dg_mem = _mem_kv_bwd(dmk, dmv, mem_n, mem, w_xk, w_xv)
    tok = emit(dict(w_xo=gw_xo, w_xq=gw_xq, w_xk=gw_xk, w_xv=gw_xv))
    dattns, dds, dgb, dcv, dg_a, dg_c, dw_sc = _mix_out_bwd(dx1, w_out, attn, gb, gc, xi, w_sc, g_a, g_c, tok)
    first = lambda a: [a[0][None]] + list(a[1:])
    dattns, dds, lses = first(dattns), first(dds), first(lses)
    gw_out = _dw(mixed, dx1, tok, "dw_out")[0]
    tok = emit(dict(w_out=gw_out))
    dqs, dks, dvs, dbias = [], [], [], []
    for p, dil in enumerate(DILATIONS):
        dq_p, dk_p, dv_p, db_p = _swa_bwd(qs[p], ks[p], vs[p], dattns[p], lses[p], dds[p], bias[p], dil, tok)
        dqs.append(dq_p[0] if dil == 1 else dq_p)
        dks.append(dk_p[0] if dil == 1 else dk_p)
        dvs.append(dv_p[0] if dil == 1 else dv_p)
        dbias.append(db_p)
    d_relb = _bias_bwd(jnp.stack(dbias), buckets)
    dproj, grad_x, dg_mix = _in_proj_bwd(dqs, dks, dvs, dgb, dcv, gc, xi, w_sc, w_in_g, x, g_mix, dx1)
    pad = lambda a: jnp.pad(a, ((0, 0), (0, D_MODEL - a.shape[1])))
    small = jnp.concatenate([
        d_relb, dg_mix, dg_xattn, dg_mem, dg_ffn, dg_final, jnp.concatenate([dg_a, dg_c], axis=1),
        pad(dw_sc), pad(db_fc.reshape(N_DEV, UP_CHUNK)), pad(dw_fc.reshape(3 * N_DEV, UP_CHUNK)), pad(loss_acc)],
        axis=0)
    tok = emit_small(small)
    gw_in = _dw(h1, dproj, tok, "dw_in", n_chunks=N_DEV, chunk_cols=IN_CHUNK, per_step=2)
    emit(dict(w_in=gw_in))
    return grad_x


def kernel(x, mem, rel_bias, g_mix, w_in, w_short_conv, g_attn_out, g_conv_out, w_out, g_xattn, g_mem, w_xq, w_xk, w_xv, w_xo, g_ffn, w_up, w_ffn_conv, b_ffn_conv, w_down, g_final, loss_target, m_rel_bias, m_g_mix, m_w_in, m_w_short_conv, m_g_attn_out, m_g_conv_out, m_w_out, m_g_xattn, m_g_mem, m_w_xq, m_w_xk, m_w_xv, m_w_xo, m_g_ffn, m_w_up, m_w_ffn_conv, m_b_ffn_conv, m_w_down, m_g_final, v_rel_bias, v_g_mix, v_w_in, v_w_short_conv, v_g_attn_out, v_g_conv_out, v_w_out, v_g_xattn, v_g_mem, v_w_xq, v_w_xk, v_w_xv, v_w_xo, v_g_ffn, v_w_up, v_w_ffn_conv, v_b_ffn_conv, v_w_down, v_g_final):
    me = _dev_index(_mesh_pos())
    me_arr = me.reshape(1).astype(jnp.int32)

    big_names = ["w_in", "w_out", "w_xq", "w_xk", "w_xv", "w_xo", "w_up", "w_down"]
    late_names = big_names[1:]
    big_w = dict(w_in=w_in[0], w_out=w_out[0], w_xq=w_xq[0], w_xk=w_xk[0], w_xv=w_xv[0], w_xo=w_xo[0],
                 w_up=w_up[0].T, w_down=w_down[0])
    big_m = dict(w_in=m_w_in[0], w_out=m_w_out[0], w_xq=m_w_xq[0], w_xk=m_w_xk[0], w_xv=m_w_xv[0], w_xo=m_w_xo[0],
                 w_up=m_w_up[0].T, w_down=m_w_down[0])
    big_v = dict(w_in=v_w_in[0], w_out=v_w_out[0], w_xq=v_w_xq[0], w_xk=v_w_xk[0], w_xv=v_w_xv[0], w_xo=v_w_xo[0],
                 w_up=v_w_up[0].T, w_down=v_w_down[0])
    shard_shape = {n: big_w[n].shape for n in big_names}

    w_in_g, w_sc_g, w_fc_full = _all_gather([big_w["w_in"].astype(BF16), w_short_conv[0], w_ffn_conv[0]])
    w_sc_full = w_sc_g.transpose(1, 0, 2).reshape(3, CONV_W)
    late_shards = [big_w[n].astype(BF16) for n in late_names]
    ag_send, ag_recv, ag_srcs, ag_lands, ag_token = _gather_start("gather_weights_start", late_shards, w_in_g)
    forwarded = {}

    def forward_weights(names, after):
        which = [late_names.index(n) for n in names]
        fsend, frecv, lands, token = _gather_forward("gather_" + "_".join(names) + "_forward", ag_send, ag_recv,
                                                     [ag_lands[a] for a in which], which, after)
        forwarded[tuple(names)] = (fsend, frecv, lands)
        return token

    def late_weights(names, after):
        which = [late_names.index(n) for n in names]
        fsend, frecv, lands = forwarded[tuple(names)]
        lands = _gather_wait("gather_" + "_".join(names) + "_wait", ag_send, ag_recv, fsend, frecv,
                             [ag_srcs[a] for a in which], lands, which, after)
        out = {}
        for n, a, land in zip(names, which, lands):
            full = lax.dynamic_update_index_in_dim(land, late_shards[a], me, 0)
            if n == "w_up":
                out[n] = full
            elif n == "w_down":
                out[n] = full.reshape(N_DEV // 2, UP_CHUNK, D_MODEL)
            else:
                out[n] = full.reshape(D_MODEL, D_MODEL)
        return out

    sent = []

    def emit(grads):
        names = list(grads)
        blocks = [grads[n].reshape((N_DEV,) + shard_shape[n]) for n in names]
        started = _exchange_start("scatter_" + "_".join(names) + "_start", blocks, [False] * len(names), me_arr)
        sent.append((names, started))
        return started[-1]

    def emit_small(small):
        sent_small.append((small, _exchange_start("gather_small_start", [small], [True], me_arr)))
        return sent_small[0][1][-1]

    sent_small = []
    grad_x = _local_step(
        x[0], mem[0], loss_target[0], rel_bias, g_mix, w_in_g, w_sc_full, g_attn_out, g_conv_out, g_xattn, g_mem,
        g_ffn, w_fc_full, b_ffn_conv.reshape(N_DEV, 1, UP_CHUNK), g_final.reshape(1, D_MODEL), ag_token,
        forward_weights, late_weights, emit, emit_small)

    small_g, small_started = sent_small[0]
    after = sent[-1][1][-1]
    small_parts = _exchange_wait("gather_small_wait", small_started, [True], after)[1][0]
    big_out = {}
    after = small_parts
    for names, started in sent:
        blocks, lands = _exchange_wait("scatter_" + "_".join(names) + "_wait", started, [False] * len(names), after)
        for n, block, land in zip(names, blocks, lands):
            res = _adamw_big("adamw_" + n, big_w[n], block, land, big_m[n], big_v[n], me_arr)
            big_out[n] = [(r.T if n == "w_up" else r)[None] for r in res]
            after = res[0]

    as_rows = lambda a: a.reshape(N_DEV, UP_CHUNK)
    row1 = lambda a: a.reshape(1, D_MODEL)
    small_names = ["rel_bias", "g_mix", "g_attn_out", "g_conv_out", "g_xattn", "g_mem", "g_ffn", "b_ffn_conv", "g_final"]
    wmv = [
        (rel_bias, m_rel_bias, v_rel_bias), (g_mix, m_g_mix, v_g_mix), (g_attn_out, m_g_attn_out, v_g_attn_out),
        (g_conv_out, m_g_conv_out, v_g_conv_out), (g_xattn, m_g_xattn, v_g_xattn), (g_mem, m_g_mem, v_g_mem),
        (g_ffn, m_g_ffn, v_g_ffn), (as_rows(b_ffn_conv), as_rows(m_b_ffn_conv), as_rows(v_b_ffn_conv)),
        (row1(g_final), row1(m_g_final), row1(v_g_final))]
    g_packed, small_res = _adamw_small(small_g, small_parts, wmv, me_arr)
    small_out = dict(zip(small_names, small_res))
    loss = g_packed[ROW_LOSS, 0]
    small_out["b_ffn_conv"] = [a.reshape(1, 2 * D_FF) for a in small_out["b_ffn_conv"]]
    small_out["g_final"] = [a.reshape(D_MODEL) for a in small_out["g_final"]]

    g_wsc = lax.dynamic_slice(g_packed[ROW_WSC:ROW_WSC + 3, 0:CONV_W], (0, me * HEAD_DIM), (3, HEAD_DIM))
    g_wfc = lax.dynamic_slice(g_packed[ROW_WFC:ROW_WFC + 3 * N_DEV, 0:UP_CHUNK].reshape(3, N_DEV, UP_CHUNK),
                              (0, me, 0), (3, 1, UP_CHUNK)).reshape(3, UP_CHUNK)
    shard_res = _adamw_shards([(w_short_conv[0], g_wsc, m_w_short_conv[0], v_w_short_conv[0]),
                               (w_ffn_conv[0], g_wfc, m_w_ffn_conv[0], v_w_ffn_conv[0])])
    small_out["w_short_conv"] = [g_wsc[None]] + [a[None] for a in shard_res[0]]
    small_out["w_ffn_conv"] = [g_wfc[None]] + [a[None] for a in shard_res[1]]

    order = ["rel_bias", "g_mix", "w_in", "w_short_conv", "g_attn_out", "g_conv_out", "w_out", "g_xattn", "g_mem",
             "w_xq", "w_xk", "w_xv", "w_xo", "g_ffn", "w_up", "w_ffn_conv", "b_ffn_conv", "w_down", "g_final"]
    allp = {**big_out, **small_out}
    outs = [loss, grad_x[None]]
    for kind in range(4):
        outs += [allp[n][kind] for n in order]
    return tuple(outs)
```

```python
import math

import numpy as np
import jax
import jax.numpy as jnp
from jax import lax
from jax.experimental import pallas as pl
from jax.experimental.pallas import tpu as pltpu

F32 = jnp.float32
BF16 = jnp.bfloat16
MESH = pl.DeviceIdType.MESH

N_DEV = 8
D_MODEL = 1024
ATTN_W = 512
CONV_W = 512
N_HEADS = 8
HEAD_DIM = 64
WIN = 128
DILATIONS = (1, 4, 16)
N_BUCKETS = 32
BUCKET_MAX_EXACT = 16
BUCKET_MAX_DISTANCE = 2048
N_MEM_HEADS = 4
MEM_HEAD_DIM = 256
D_FF = 2816
IN_COLS = 3072
IN_CHUNK = IN_COLS // N_DEV
UP_CHUNK = 2 * D_FF // N_DEV
FFN_CHUNKS = 4
FFN_WIDTH = 2 * D_FF // FFN_CHUNKS
EPS = 1e-6

ADAM_LR = 0.001
ADAM_B1 = 0.9
ADAM_B2 = 0.999
ADAM_EPS = 1e-08
ADAM_WD = 0.01
ADAM_STEP = 10

SUBLANES = 8
LANES = 128
HALO = 16
TM = 512
TM_XATTN = 1024
TM_FFN = 256
TS_DW = 4096
SWA_BLOCKS = 8
VMEM_LIMIT = 56 * 1024 * 1024

ROW_RELB, ROW_GMIX, ROW_GXATTN, ROW_GMEM, ROW_GFFN, ROW_GFINAL, ROW_GAC = 0, 8, 16, 24, 32, 40, 48
ROW_WSC, ROW_BFC, ROW_WFC, ROW_LOSS, SMALL_ROWS = 56, 64, 72, 96, 104


def _cparams(n_grid):
    return pltpu.CompilerParams(dimension_semantics=("arbitrary",) * n_grid, vmem_limit_bytes=VMEM_LIMIT)


def _full(shape):
    nd = len(shape)
    return pl.BlockSpec(tuple(shape), lambda *_: (0,) * nd)


def _resident(shape):
    nd = len(shape)
    return pl.BlockSpec(tuple(shape), lambda *_: (0,) * nd, pipeline_mode=pl.Buffered(1))


ANY_SPEC = pl.BlockSpec(memory_space=pl.ANY)
HBM_SPEC = pl.BlockSpec(memory_space=pltpu.HBM)
SEM_SPEC = pl.BlockSpec(memory_space=pltpu.SEMAPHORE)
VMEM_SPEC = pl.BlockSpec(memory_space=pltpu.VMEM)
SMEM_SPEC = pl.BlockSpec(memory_space=pltpu.SMEM)
DATAFLOW = pltpu.SideEffectType.DATAFLOW_SIDE_EFFECTING


def _rms(x):
    r = lax.rsqrt(jnp.mean(x * x, axis=-1, keepdims=True) + EPS)
    return x * r, r


def _rms_bwd(xh, r, g, dy):
    dxh = dy * g
    return r * (dxh - xh * jnp.mean(dxh * xh, axis=-1, keepdims=True))


def _shift_down(u, halo, k):
    ru = pltpu.roll(u, k, 0)
    rh = pltpu.roll(halo, k, 0)
    row = lax.broadcasted_iota(jnp.int32, rh.shape, 0)
    head = jnp.where(row < k, rh, ru[0:SUBLANES])
    return jnp.concatenate([head, ru[SUBLANES:]], axis=0)


def _shift_up(u, halo, k):
    tm = u.shape[0]
    ru = pltpu.roll(u, tm - k, 0)
    rh = pltpu.roll(halo, SUBLANES - k, 0)
    row = lax.broadcasted_iota(jnp.int32, rh.shape, 0)
    tail = jnp.where(row >= SUBLANES - k, rh, ru[tm - SUBLANES:])
    return jnp.concatenate([ru[:tm - SUBLANES], tail], axis=0)


def _causal_conv3(u, halo, w_ref):
    return (_shift_down(u, halo, 2) * w_ref[0:1, :] + _shift_down(u, halo, 1) * w_ref[1:2, :]) + u * w_ref[2:3, :]


def _dot(a, b):
    return jnp.dot(a, b, preferred_element_type=F32)


def _dot_nt(a, b):
    return lax.dot_general(a, b, (((1,), (1,)), ((), ())), preferred_element_type=F32)


def _dot_tn(a, b):
    return lax.dot_general(a, b, (((0,), (0,)), ((), ())), preferred_element_type=F32)


def _sigmoid(x):
    return 0.5 * jnp.tanh(0.5 * x) + 0.5


def _bucket_tables():
    qi = np.arange(WIN)[:, None]
    kj = np.arange(2 * WIN)[None, :]
    steps = np.clip(qi + WIN - kj, 0, WIN)
    out = []
    for d in DILATIONS:
        dist = steps * d
        dd = np.maximum(dist, 1).astype(np.float32)
        large = BUCKET_MAX_EXACT + (
            np.log(dd / np.float32(BUCKET_MAX_EXACT)) / np.float32(math.log(BUCKET_MAX_DISTANCE / BUCKET_MAX_EXACT))
            * np.float32(N_BUCKETS - BUCKET_MAX_EXACT)).astype(np.int32)
        large = np.minimum(large, N_BUCKETS - 1)
        out.append(np.where(dist < BUCKET_MAX_EXACT, dist, large).astype(np.int32))
    return np.stack(out)


def _band_mask():
    qi = lax.broadcasted_iota(jnp.int32, (WIN, 2 * WIN), 0)
    kj = lax.broadcasted_iota(jnp.int32, (WIN, 2 * WIN), 1)
    steps = qi + WIN - kj
    return (steps >= 0) & (steps <= WIN)


def _bias_fwd(rel_bias, buckets):
    present = [sorted(set(buckets[p].ravel().tolist())) for p in range(3)]

    def body(rb_ref, bk_ref, o_ref):
        band = _band_mask()
        for p in range(3):
            bk = bk_ref[p]
            for h in range(N_HEADS):
                acc = jnp.zeros((WIN, 2 * WIN), F32)
                for b in present[p]:
                    acc = jnp.where(bk == b, rb_ref[h, b], acc)
                o_ref[p, h] = jnp.where(band, acc, -jnp.inf)

    return pl.pallas_call(
        body, name="bias_fwd",
        out_shape=jax.ShapeDtypeStruct((3, N_HEADS, WIN, 2 * WIN), F32),
        in_specs=[pl.BlockSpec(memory_space=pltpu.SMEM), pl.BlockSpec(memory_space=pltpu.VMEM)],
        out_specs=pl.BlockSpec(memory_space=pltpu.VMEM),
    )(rel_bias, jnp.asarray(buckets))


def _bias_bwd(dbias, buckets):
    present = [set(buckets[p].ravel().tolist()) for p in range(3)]

    def body(db_ref, bk_ref, o_ref):
        lane = lax.broadcasted_iota(jnp.int32, (1, D_MODEL), 1)
        rows = []
        for h in range(N_HEADS):
            row = jnp.zeros((1, D_MODEL), F32)
            for b in range(N_BUCKETS):
                tot = jnp.zeros((1, 1), F32)
                for p in (p for p in range(3) if b in present[p]):
                    sel = jnp.where(bk_ref[p] == b, db_ref[p, h], 0.0)
                    tot = tot + jnp.sum(jnp.sum(sel, axis=0, keepdims=True), axis=1, keepdims=True)
                row = jnp.where(lane == b, tot, row)
            rows.append(row)
        o_ref[...] = jnp.concatenate(rows, axis=0)

    return pl.pallas_call(
        body, name="bias_bwd",
        out_shape=jax.ShapeDtypeStruct((N_HEADS, D_MODEL), F32),
        in_specs=[pl.BlockSpec(memory_space=pltpu.VMEM), pl.BlockSpec(memory_space=pltpu.VMEM)],
        out_specs=pl.BlockSpec(memory_space=pltpu.VMEM),
    )(dbias, jnp.asarray(buckets))


def _spread(val, scr_ref, out_refs, dtype):
    out_refs[0][...] = val.astype(dtype)
    n_blk = val.shape[1] // LANES
    for c in range(n_blk):
        scr_ref[c] = val[:, c * LANES:(c + 1) * LANES]
    for o_ref, d in zip(out_refs[1:], DILATIONS[1:]):
        for r in range(d):
            for c in range(n_blk):
                o_ref[r, :, c * LANES:(c + 1) * LANES] = scr_ref.at[c][pl.ds(r, TM // d, stride=d), :].astype(dtype)


HEAD_LANES = LANES // N_HEADS


def _head_lane(h):
    return HEAD_LANES * (h // 2) + (LANES // 2) * (h % 2)


def _narrow_heads(x):
    grp = (lax.broadcasted_iota(jnp.int32, (x.shape[0], LANES), 1) // HEAD_LANES) % (N_HEADS // 2)
    out = x[:, 0:LANES]
    for t in range(1, N_HEADS // 2):
        out = jnp.where(grp == t, x[:, t * LANES:(t + 1) * LANES], out)
    return out


def _gather_classes(blk_ref, scr_ref, d):
    n_blk = blk_ref.shape[2] // LANES
    for r in range(d):
        for c in range(n_blk):
            scr_ref.at[c][pl.ds(r, TM // d, stride=d), :] = blk_ref[r, :, c * LANES:(c + 1) * LANES].astype(F32)
    return jnp.concatenate([scr_ref[c] for c in range(n_blk)], axis=1)


def _class_specs(cols):
    return [pl.BlockSpec((TM, cols), lambda i: (i, 0))] + [
        pl.BlockSpec((d, TM // d, cols), lambda i: (0, i, 0)) for d in DILATIONS[1:]]


def _class_shapes(s, cols, dtype):
    return [jax.ShapeDtypeStruct((s, cols), dtype)] + [
        jax.ShapeDtypeStruct((d, s // d, cols), dtype) for d in DILATIONS[1:]]


def _load_w_in_pairs(w_hbm, w_scr, sems):
    @pl.when(pl.program_id(0) == 0)
    def _():
        copies = [pltpu.make_async_copy(w_hbm.at[j], w_scr.at[j // 2, :, pl.ds((j % 2) * IN_CHUNK, IN_CHUNK)],
                                        sems.at[j]) for j in range(N_DEV)]
        for copy in copies:
            copy.start()
        for copy in copies:
            copy.wait()


W_IN_PAIRS = [pltpu.VMEM((N_DEV // 2, D_MODEL, 2 * IN_CHUNK), BF16), pltpu.SemaphoreType.DMA((N_DEV,))]


def _rms_proj(x, g_mix, w_in_g, dep):
    s = x.shape[0]

    def body(x_ref, g_ref, w_hbm, dep_ref, h_ref, q1, q4, q16, k1, k4, k16, v1, v4, v16, gb_ref, gc_ref, xi_ref, scr,
             w_scr, w_sems):
        _load_w_in_pairs(w_hbm, w_scr, w_sems)
        xh, _ = _rms(x_ref[...])
        h = (xh * g_ref[...]).astype(BF16)
        h_ref[...] = h
        proj = jnp.concatenate([_dot(h, w_scr[j]) for j in range(N_DEV // 2)], axis=1)
        _spread(proj[:, 0:512] * (HEAD_DIM ** -0.5), scr, (q1, q4, q16), BF16)
        _spread(proj[:, 512:1024], scr, (k1, k4, k16), BF16)
        _spread(proj[:, 1024:1536], scr, (v1, v4, v16), BF16)
        gb_ref[...] = proj[:, 1536:2048]
        gc_ref[...] = proj[:, 2048:2560]
        xi_ref[...] = proj[:, 2560:3072]

    row = lambda n: pl.BlockSpec((TM, n), lambda i: (i, 0))
    res = pl.pallas_call(
        body, name="rms_proj", grid=(s // TM,),
        out_shape=[jax.ShapeDtypeStruct((s, D_MODEL), BF16)] + _class_shapes(s, 512, BF16) * 3
        + [jax.ShapeDtypeStruct((s, 512), F32)] * 3,
        in_specs=[row(D_MODEL), _full(g_mix.shape), ANY_SPEC, ANY_SPEC],
        out_specs=[row(D_MODEL)] + _class_specs(512) * 3 + [row(512)] * 3,
        scratch_shapes=[pltpu.VMEM((512 // LANES, TM, LANES), F32)] + W_IN_PAIRS,
        compiler_params=_cparams(1),
    )(x, g_mix, w_in_g, dep)
    return res[0], res[1:4], res[4:7], res[7:10], res[10], res[11], res[12]


def _pair_split(x2):
    lane = lax.broadcasted_iota(jnp.int32, x2.shape, 1)
    zero = jnp.zeros_like(x2)
    return jnp.where(lane < HEAD_DIM, x2, zero), jnp.where(lane >= HEAD_DIM, x2, zero)


def _pair_join(even, odd):
    lane = lax.broadcasted_iota(jnp.int32, (even.shape[0], LANES), 1)
    return jnp.where(lane < HEAD_DIM, even, odd)


def _swa_steps(qc, dil):
    n128 = qc.shape[1] // WIN
    nsub = min(SWA_BLOCKS, n128)
    nb = n128 // nsub
    ncls = min(dil, SWA_BLOCKS // nsub) if nb == 1 else 1
    return nsub, nb, ncls


def _swa_fwd(qc, kc, vc, bias, dil, dep):
    nsub, nb, ncls = _swa_steps(qc, dil)
    whole = nb == 1

    def body(q_ref, kp_ref, kc_ref, vp_ref, vc_ref, b_ref, dep_ref, o_ref, lse_ref, s_scr, p_scr):
        no_prev = (pl.program_id(1) == 0) & (lax.broadcasted_iota(jnp.int32, (WIN, 2 * WIN), 1) < WIN)
        pairs = [slice(a * LANES, (a + 1) * LANES) for a in range(N_HEADS // 2)]
        for c, t in [(c, t) for c in range(ncls) for t in range(nsub)]:
            i = c * nsub + t
            rows = slice(t * WIN, (t + 1) * WIN)
            alone = whole and t == 0
            cols = slice(WIN, 2 * WIN) if alone else slice(0, 2 * WIN)

            def keys(prev_ref, cur_ref, sl):
                if alone:
                    return cur_ref[c, rows, sl]
                if t == 0:
                    return jnp.concatenate([prev_ref[c, :, sl], cur_ref[c, rows, sl]], axis=0)
                return cur_ref[c, (t - 1) * WIN:(t + 1) * WIN, sl]

            for a, sl in enumerate(pairs):
                k2 = keys(kp_ref, kc_ref, sl)
                for e, qh in enumerate(_pair_split(q_ref[c, rows, sl])):
                    s_scr[i, 2 * a + e, :, cols] = _dot_nt(qh, k2)
            den, lse = [], []
            for h in range(N_HEADS):
                lg = s_scr[i, h, :, cols] + b_ref[h, :, cols]
                if t == 0 and not whole:
                    lg = jnp.where(no_prev, -jnp.inf, lg)
                m = jnp.max(lg, axis=-1, keepdims=True)
                p = jnp.exp(lg - m)
                den.append(jnp.sum(p, axis=-1, keepdims=True))
                p_scr[i, h, :, cols] = p.astype(BF16)
                lse.append(m + jnp.log(den[h]))
            for a, sl in enumerate(pairs):
                v_even, v_odd = _pair_split(keys(vp_ref, vc_ref, sl))
                o2 = _dot(p_scr[i, 2 * a, :, cols], v_even) + _dot(p_scr[i, 2 * a + 1, :, cols], v_odd)
                o_ref[c, rows, sl] = o2 / _pair_join(den[2 * a], den[2 * a + 1])
                lse_ref[c, rows, sl] = _pair_join(lse[2 * a], lse[2 * a + 1])

    cur = pl.BlockSpec((ncls, nsub * WIN, 512), lambda r, b: (r, b, 0))
    prev = pl.BlockSpec((ncls, WIN, 512), lambda r, b: (r, jnp.maximum(nsub * b - 1, 0), 0))
    wide = (ncls * nsub, N_HEADS, WIN, 2 * WIN)
    return pl.pallas_call(
        body, name=f"swa_fwd_d{dil}", grid=(dil // ncls, nb),
        out_shape=[jax.ShapeDtypeStruct(qc.shape, F32)] * 2,
        in_specs=[cur, prev, cur, prev, cur, _full(bias.shape), ANY_SPEC],
        out_specs=[cur] * 2,
        scratch_shapes=[pltpu.VMEM(wide, F32), pltpu.VMEM(wide, BF16)],
        compiler_params=_cparams(2),
    )(qc, kc, kc, vc, vc, bias, dep)


def _mix_out(branches, gb, gc, xi, x, w_sc, g_a, g_c, w_out):
    s = x.shape[0]
    tb = TM // SUBLANES

    def body(o1, l1, o4, l4, o16, l16, gb_ref, gc_ref, xi_ref, gch_ref, xih_ref, x_ref, wsc_ref,
             ga_ref, gcv_ref, wout_ref, attn_ref, lse1, lse4, lse16, mixed_ref, x1_ref, scr_a, scr_b, scr_c, scr_d):
        i = pl.program_id(0)
        la, lb, lc = l1[...], _gather_classes(l4, scr_a, 4), _gather_classes(l16, scr_b, 16)
        m_all = jnp.maximum(jnp.maximum(la, lb), lc)
        ea, eb, ec = jnp.exp(la - m_all), jnp.exp(lb - m_all), jnp.exp(lc - m_all)
        den = (ea + eb) + ec
        num = (ea * o1[...] + eb * _gather_classes(o4, scr_c, 4)) + ec * _gather_classes(o16, scr_d, 16)
        attn = num / den
        attn_ref[...] = attn
        _spread(_narrow_heads(m_all + jnp.log(den)), scr_a, (lse1, lse4, lse16), F32)
        xa, _ = _rms(attn)
        u = gc_ref[...] * xi_ref[...]
        uh = jnp.where(i > 0, gch_ref[...] * xih_ref[...], 0.0)
        conv = gb_ref[...] * _causal_conv3(u, uh, wsc_ref)
        xc, _ = _rms(conv)
        mixed = jnp.concatenate([xa * ga_ref[...], xc * gcv_ref[...]], axis=1).astype(BF16)
        mixed_ref[...] = mixed
        x1_ref[...] = x_ref[...] + _dot(mixed, wout_ref[...])

    row = lambda n: pl.BlockSpec((TM, n), lambda i: (i, 0))
    halo = pl.BlockSpec((SUBLANES, 512), lambda i: (jnp.maximum(i * tb - 1, 0), 0))
    cs = _class_specs(512)
    flat = [a for br in branches for a in br]
    res = pl.pallas_call(
        body, name="mix_out", grid=(s // TM,),
        out_shape=[jax.ShapeDtypeStruct((s, 512), F32)] + _class_shapes(s, LANES, F32)
        + [jax.ShapeDtypeStruct((s, D_MODEL), BF16), jax.ShapeDtypeStruct((s, D_MODEL), F32)],
        in_specs=[cs[0], cs[0], cs[1], cs[1], cs[2], cs[2], row(512), row(512), row(512), halo, halo,
                  row(D_MODEL), _full(w_sc.shape), _full(g_a.shape), _full(g_c.shape), _full(w_out.shape)],
        out_specs=[row(512)] + _class_specs(LANES) + [row(D_MODEL), row(D_MODEL)],
        scratch_shapes=[pltpu.VMEM((512 // LANES, TM, LANES), F32)] * 4,
        compiler_params=_cparams(1),
    )(*flat, gb, gc, xi, gc, xi, x, w_sc, g_a, g_c, w_out)
    return res[0], res[1:4], res[4], res[5]


def _mem_kv(mem, g_mem, w_xk, w_xv):
    def body(mem_ref, g_ref, wk_ref, wv_ref, mn_ref, k_ref, v_ref):
        xh, _ = _rms(mem_ref[...])
        mn = (xh * g_ref[...]).astype(BF16)
        mn_ref[...] = mn
        k_ref[...] = _dot(mn, wk_ref[...]).astype(BF16)
        v_ref[...] = _dot(mn, wv_ref[...]).astype(BF16)

    vm = pl.BlockSpec(memory_space=pltpu.VMEM)
    return pl.pallas_call(
        body, name="mem_kv",
        out_shape=[jax.ShapeDtypeStruct(mem.shape, BF16)] * 3,
        in_specs=[vm] * 4, out_specs=[vm] * 3,
        compiler_params=pltpu.CompilerParams(vmem_limit_bytes=VMEM_LIMIT),
    )(mem, g_mem, w_xk, w_xv)


def _xattn_fwd(x1, g, w_xq, k, v, w_xo, dep):
    s = x1.shape[0]

    def body(x1_ref, g_ref, wq_ref, k_ref, v_ref, wo_ref, dep_ref, h2_ref, q_ref, o_ref, x2_ref):
        x1v = x1_ref[...]
        xh, _ = _rms(x1v)
        h2 = (xh * g_ref[...]).astype(BF16)
        h2_ref[...] = h2
        qb = _dot(h2, wq_ref[...]).astype(BF16)
        q_ref[...] = qb
        outs = []
        for h in range(N_MEM_HEADS):
            sl = slice(h * MEM_HEAD_DIM, (h + 1) * MEM_HEAD_DIM)
            lg = _dot_nt(qb[:, sl], k_ref[:, sl]) * (MEM_HEAD_DIM ** -0.5)
            p = jnp.exp(lg - jnp.max(lg, axis=-1, keepdims=True))
            p = p / jnp.sum(p, axis=-1, keepdims=True)
            outs.append(_dot(p.astype(BF16), v_ref[:, sl]))
        o = jnp.concatenate(outs, axis=1).astype(BF16)
        o_ref[...] = o
        x2_ref[...] = x1v + _dot(o, wo_ref[...])

    row = pl.BlockSpec((TM_XATTN, D_MODEL), lambda i: (i, 0))
    return pl.pallas_call(
        body, name="xattn_fwd", grid=(s // TM_XATTN,),
        out_shape=[jax.ShapeDtypeStruct((s, D_MODEL), BF16)] * 3 + [jax.ShapeDtypeStruct((s, D_MODEL), F32)],
        in_specs=[row, _full(g.shape), _full(w_xq.shape), _full(k.shape), _full(v.shape), _full(w_xo.shape), ANY_SPEC],
        out_specs=[row] * 4,
        compiler_params=_cparams(1),
    )(x1, g, w_xq, k, v, w_xo, dep)


def _ffn_conv(h_ext, wup_ref, wfc_ref, bfc_ref, j):
    u = _dot_nt(h_ext, wup_ref[j])
    w = wfc_ref[j]
    c = ((pltpu.roll(u, 2, 0) * w[0:1, :] + pltpu.roll(u, 1, 0) * w[1:2, :]) + u * w[2:3, :]) + bfc_ref[j]
    return c[HALO:], u[HALO:]


def _ffn_fwd(x2, g, w_up_g, w_fc, b_fc, w_down_g, g_final, target):
    s = x2.shape[0]
    tb = TM_FFN // HALO
    n_ch, wid = w_up_g.shape[:2]
    half = n_ch // 2

    def body(x_ref, xp_ref, g_ref, wup_ref, wfc_ref, bfc_ref, wd_ref, gf_ref, t_ref, h_ref, u_ref, c_ref, act_ref,
             dx3_ref, loss_ref, dgf_ref):
        i = pl.program_id(0)

        @pl.when(i == 0)
        def _():
            loss_ref[...] = jnp.zeros_like(loss_ref)
            dgf_ref[...] = jnp.zeros_like(dgf_ref)

        x2v = x_ref[...]
        gv = g_ref[...]
        h = (_rms(x2v)[0] * gv).astype(BF16)
        h_ref[...] = h
        hp = jnp.where(i > 0, _rms(xp_ref[...])[0] * gv, 0.0).astype(BF16)
        h_ext = jnp.concatenate([hp, h], axis=0)
        down = jnp.zeros((TM_FFN, D_MODEL), F32)
        for j in range(half):
            cg, ug = _ffn_conv(h_ext, wup_ref, wfc_ref, bfc_ref, j)
            cv, uv = _ffn_conv(h_ext, wup_ref, wfc_ref, bfc_ref, j + half)
            c_ref[j] = cg
            c_ref[j + half] = cv
            u_ref[j] = ug.astype(BF16)
            u_ref[j + half] = uv.astype(BF16)
            a = ((cg * _sigmoid(cg)) * cv).astype(BF16)
            act_ref[j] = a
            down = down + _dot(a, wd_ref[j])
        x3 = x2v + down
        xh, r = _rms(x3)
        gf = gf_ref[...]
        e = xh * gf - t_ref[...]
        loss_ref[...] += 0.5 * jnp.sum(jnp.sum(e * e, axis=1, keepdims=True), axis=0, keepdims=True) / D_MODEL
        dy = e * (1.0 / D_MODEL)
        dgf_ref[0:1, :] += jnp.sum(dy * xh, axis=0, keepdims=True)
        dx3_ref[...] = _rms_bwd(xh, r, gf, dy)

    row = pl.BlockSpec((TM_FFN, D_MODEL), lambda i: (i, 0))
    prev = pl.BlockSpec((HALO, D_MODEL), lambda i: (jnp.maximum(i * tb - 1, 0), 0))
    return pl.pallas_call(
        body, name="ffn_fwd", grid=(s // TM_FFN,),
        out_shape=[jax.ShapeDtypeStruct((s, D_MODEL), BF16), jax.ShapeDtypeStruct((n_ch, s, wid), BF16),
                   jax.ShapeDtypeStruct((n_ch, s, wid), F32), jax.ShapeDtypeStruct((half, s, wid), BF16),
                   jax.ShapeDtypeStruct((s, D_MODEL), F32), jax.ShapeDtypeStruct((SUBLANES, 128), F32),
                   jax.ShapeDtypeStruct((SUBLANES, D_MODEL), F32)],
        in_specs=[row, prev, _full(g.shape), _resident(w_up_g.shape), _full(w_fc.shape), _full(b_fc.shape),
                  _resident(w_down_g.shape), _full(g_final.shape), row],
        out_specs=[row, pl.BlockSpec((n_ch, TM_FFN, wid), lambda i: (0, i, 0)),
                   pl.BlockSpec((n_ch, TM_FFN, wid), lambda i: (0, i, 0)),
                   pl.BlockSpec((half, TM_FFN, wid), lambda i: (0, i, 0)), row,
                   _full((SUBLANES, 128)), _full((SUBLANES, D_MODEL))],
        compiler_params=_cparams(1),
    )(x2, x2, g, w_up_g, w_fc, b_fc, w_down_g, g_final, target)


def _ffn_bwd(dx3, up, conv, x2, g, w_up_g, w_fc, w_down_g):
    s = x2.shape[0]
    tb = TM_FFN // HALO
    last = s // HALO - 1
    n_tiles = s // TM_FFN
    n_ch, wid = w_up_g.shape[:2]
    half = n_ch // 2
    n_ext = TM_FFN + HALO

    def body(dx_ref, dxn_ref, u_ref, c_ref, cn_ref, x2_ref, g_ref, wup_ref, wfc_ref, wd_ref,
             dup_ref, dx2_ref, dg_ref, dwfc_ref, dbfc_ref):
        i = pl.program_id(0)

        @pl.when(i == 0)
        def _():
            dg_ref[...] = jnp.zeros_like(dg_ref)
            dwfc_ref[...] = jnp.zeros_like(dwfc_ref)
            dbfc_ref[...] = jnp.zeros_like(dbfc_ref)

        dxv = dx_ref[...]
        dxn = jnp.where(i < n_tiles - 1, dxn_ref[...], 0.0)
        dx_ext = jnp.concatenate([dxv, dxn], axis=0).astype(BF16)
        dh = jnp.zeros((TM_FFN, D_MODEL), F32)
        for j in range(half):
            cg = jnp.concatenate([c_ref[j], cn_ref[j]], axis=0)
            cv = jnp.concatenate([c_ref[j + half], cn_ref[j + half]], axis=0)
            dact = _dot_nt(dx_ext, wd_ref[j])
            sg = _sigmoid(cg)
            silu = cg * sg
            parts = ((j + half, dact * silu), (j, (dact * cv) * (sg + silu * (1.0 - sg))))
            for jj, dc in parts:
                u = u_ref[jj].astype(F32)
                dc0, dc1, dc2 = dc[:TM_FFN], pltpu.roll(dc, n_ext - 1, 0)[:TM_FFN], pltpu.roll(dc, n_ext - 2, 0)[:TM_FFN]
                dbfc_ref[jj:jj + 1, :] += jnp.sum(dc0, axis=0, keepdims=True)
                dwfc_ref[0, jj:jj + 1, :] += jnp.sum(dc2 * u, axis=0, keepdims=True)
                dwfc_ref[1, jj:jj + 1, :] += jnp.sum(dc1 * u, axis=0, keepdims=True)
                dwfc_ref[2, jj:jj + 1, :] += jnp.sum(dc0 * u, axis=0, keepdims=True)
                w = wfc_ref[jj]
                du = ((dc0 * w[2:3, :] + dc1 * w[1:2, :]) + dc2 * w[0:1, :]).astype(BF16)
                dup_ref[jj] = du
                dh = dh + _dot(du, wup_ref[jj])
        xh, r = _rms(x2_ref[...])
        dg_ref[0:1, :] += jnp.sum(dh * xh, axis=0, keepdims=True)
        dx2_ref[...] = dxv + _rms_bwd(xh, r, g_ref[...], dh)

    row = pl.BlockSpec((TM_FFN, D_MODEL), lambda i: (i, 0))
    nxt = pl.BlockSpec((HALO, D_MODEL), lambda i: (jnp.minimum((i + 1) * tb, last), 0))
    cur_c = pl.BlockSpec((n_ch, TM_FFN, wid), lambda i: (0, i, 0))
    nxt_c = pl.BlockSpec((n_ch, HALO, wid), lambda i: (0, jnp.minimum((i + 1) * tb, last), 0))
    return pl.pallas_call(
        body, name="ffn_bwd", grid=(n_tiles,),
        out_shape=[jax.ShapeDtypeStruct((n_ch, s, wid), BF16), jax.ShapeDtypeStruct((s, D_MODEL), F32),
                   jax.ShapeDtypeStruct((SUBLANES, D_MODEL), F32), jax.ShapeDtypeStruct((3, n_ch, wid), F32),
                   jax.ShapeDtypeStruct((n_ch, wid), F32)],
        in_specs=[row, nxt, cur_c, cur_c, nxt_c, row, _full(g.shape), _resident(w_up_g.shape), _full(w_fc.shape),
                  _resident(w_down_g.shape)],
        out_specs=[cur_c, row, _full((SUBLANES, D_MODEL)), _full((3, n_ch, wid)), _full((n_ch, wid))],
        compiler_params=_cparams(1),
    )(dx3, dx3, up, conv, conv, x2, g, w_up_g, w_fc, w_down_g)


def _xattn_bwd(dx2, o, q, k, v, w_xo, w_xq, x1, g, dep):
    s = x1.shape[0]

    def body(dx2_ref, o_ref, q_ref, k_ref, v_ref, wo_ref, wq_ref, x1_ref, g_ref, dep_ref, dq_ref, dx1_ref, dk_ref,
             dv_ref, dg_ref):
        @pl.when(pl.program_id(0) == 0)
        def _():
            dk_ref[...] = jnp.zeros_like(dk_ref)
            dv_ref[...] = jnp.zeros_like(dv_ref)
            dg_ref[...] = jnp.zeros_like(dg_ref)

        dx2v = dx2_ref[...]
        do = _dot_nt(dx2v.astype(BF16), wo_ref[...])
        dqs = []
        for h in range(N_MEM_HEADS):
            sl = slice(h * MEM_HEAD_DIM, (h + 1) * MEM_HEAD_DIM)
            qh, kh, vh = q_ref[:, sl], k_ref[:, sl], v_ref[:, sl]
            lg = _dot_nt(qh, kh) * (MEM_HEAD_DIM ** -0.5)
            p = jnp.exp(lg - jnp.max(lg, axis=-1, keepdims=True))
            p = p / jnp.sum(p, axis=-1, keepdims=True)
            doh = do[:, sl].astype(BF16)
            dp = _dot_nt(doh, vh)
            ds = (p * (dp - jnp.sum(p * dp, axis=-1, keepdims=True)) * (MEM_HEAD_DIM ** -0.5)).astype(BF16)
            dqs.append(_dot(ds, kh))
            dk_ref[:, sl] += _dot_tn(ds, qh)
            dv_ref[:, sl] += _dot_tn(p.astype(BF16), doh)
        dq = jnp.concatenate(dqs, axis=1).astype(BF16)
        dq_ref[...] = dq
        dh2 = _dot_nt(dq, wq_ref[...])
        xh, r = _rms(x1_ref[...])
        dg_ref[0:1, :] += jnp.sum(dh2 * xh, axis=0, keepdims=True)
        dx1_ref[...] = dx2v + _rms_bwd(xh, r, g_ref[...], dh2)

    row = pl.BlockSpec((TM_XATTN, D_MODEL), lambda i: (i, 0))
    return pl.pallas_call(
        body, name="xattn_bwd", grid=(s // TM_XATTN,),
        out_shape=[jax.ShapeDtypeStruct((s, D_MODEL), BF16), jax.ShapeDtypeStruct((s, D_MODEL), F32),
                   jax.ShapeDtypeStruct(k.shape, F32), jax.ShapeDtypeStruct(k.shape, F32),
                   jax.ShapeDtypeStruct((SUBLANES, D_MODEL), F32)],
        in_specs=[row, row, row, _full(k.shape), _full(v.shape), _full(w_xo.shape), _full(w_xq.shape), row,
                  _full(g.shape), ANY_SPEC],
        out_specs=[row, row, _full(k.shape), _full(k.shape), _full((SUBLANES, D_MODEL))],
        compiler_params=_cparams(1),
    )(dx2, o, q, k, v, w_xo, w_xq, x1, g, dep)


def _mem_kv_bwd(dk, dv, mem_n, mem, w_xk, w_xv):
    def body(dk_ref, dv_ref, mn_ref, mem_ref, wk_ref, wv_ref, dwk_ref, dwv_ref, dg_ref):
        dkb, dvb = dk_ref[...].astype(BF16), dv_ref[...].astype(BF16)
        mn = mn_ref[...]
        dwk_ref[...] = _dot_tn(mn, dkb).astype(BF16)
        dwv_ref[...] = _dot_tn(mn, dvb).astype(BF16)
        dmn = _dot_nt(dkb, wk_ref[...]) + _dot_nt(dvb, wv_ref[...])
        xh, _ = _rms(mem_ref[...])
        dg_ref[...] = jnp.zeros_like(dg_ref)
        dg_ref[0:1, :] = jnp.sum(dmn * xh, axis=0, keepdims=True)

    vm = pl.BlockSpec(memory_space=pltpu.VMEM)
    return pl.pallas_call(
        body, name="mem_kv_bwd",
        out_shape=[jax.ShapeDtypeStruct(w_xk.shape, BF16), jax.ShapeDtypeStruct(w_xv.shape, BF16),
                   jax.ShapeDtypeStruct((SUBLANES, D_MODEL), F32)],
        in_specs=[vm] * 6, out_specs=[vm] * 3,
        compiler_params=pltpu.CompilerParams(vmem_limit_bytes=VMEM_LIMIT),
    )(dk, dv, mem_n, mem, w_xk, w_xv)


def _mix_out_bwd(dx1, w_out, attn, gb, gc, xi, w_sc, g_a, g_c, dep):
    s = dx1.shape[0]
    tb = TM // SUBLANES

    def body(dx1_ref, wout_ref, attn_ref, gb_ref, gc_ref, xi_ref, gch_ref, xih_ref, wsc_ref, ga_ref, gcv_ref, dep_ref,
             da1, da4, da16, dd1, dd4, dd16, dgb_ref, dcv_ref, dga_ref, dgc_ref, dwsc_ref, scr):
        i = pl.program_id(0)

        @pl.when(i == 0)
        def _():
            dga_ref[...] = jnp.zeros_like(dga_ref)
            dgc_ref[...] = jnp.zeros_like(dgc_ref)
            dwsc_ref[...] = jnp.zeros_like(dwsc_ref)

        dmixed = _dot_nt(dx1_ref[...].astype(BF16), wout_ref[...])
        da, dcn = dmixed[:, :ATTN_W], dmixed[:, ATTN_W:]
        attn = attn_ref[...]
        xa, ra = _rms(attn)
        dga_ref[0:1, :] += jnp.sum(da * xa, axis=0, keepdims=True)
        dattn = _rms_bwd(xa, ra, ga_ref[...], da)
        _spread(dattn, scr, (da1, da4, da16), BF16)
        prod = dattn * attn
        dd = jnp.concatenate(
            [jnp.broadcast_to(jnp.sum(prod[:, h * HEAD_DIM:(h + 1) * HEAD_DIM], axis=-1, keepdims=True),
                              (TM, HEAD_DIM)) for h in range(N_HEADS)], axis=1)
        _spread(_narrow_heads(dd), scr, (dd1, dd4, dd16), F32)
        gbv = gb_ref[...]
        u = gc_ref[...] * xi_ref[...]
        uh = jnp.where(i > 0, gch_ref[...] * xih_ref[...], 0.0)
        u2, u1 = _shift_down(u, uh, 2), _shift_down(u, uh, 1)
        cv = (u2 * wsc_ref[0:1, :] + u1 * wsc_ref[1:2, :]) + u * wsc_ref[2:3, :]
        xc, rc = _rms(gbv * cv)
        dgc_ref[0:1, :] += jnp.sum(dcn * xc, axis=0, keepdims=True)
        dconv = _rms_bwd(xc, rc, gcv_ref[...], dcn)
        dgb_ref[...] = (dconv * cv).astype(BF16)
        dcv = dconv * gbv
        dcv_ref[...] = dcv
        dwsc_ref[0:1, :] += jnp.sum(dcv * u2, axis=0, keepdims=True)
        dwsc_ref[1:2, :] += jnp.sum(dcv * u1, axis=0, keepdims=True)
        dwsc_ref[2:3, :] += jnp.sum(dcv * u, axis=0, keepdims=True)

    row = lambda n: pl.BlockSpec((TM, n), lambda i: (i, 0))
    halo = pl.BlockSpec((SUBLANES, 512), lambda i: (jnp.maximum(i * tb - 1, 0), 0))
    acc = _full((SUBLANES, 512))
    res = pl.pallas_call(
        body, name="mix_out_bwd", grid=(s // TM,),
        out_shape=_class_shapes(s, 512, BF16) + _class_shapes(s, LANES, F32)
        + [jax.ShapeDtypeStruct((s, 512), BF16), jax.ShapeDtypeStruct((s, 512), F32)]
        + [jax.ShapeDtypeStruct((SUBLANES, 512), F32)] * 3,
        in_specs=[row(D_MODEL), _full(w_out.shape), row(512), row(512), row(512), row(512), halo, halo,
                  _full(w_sc.shape), _full(g_a.shape), _full(g_c.shape), ANY_SPEC],
        out_specs=_class_specs(512) + _class_specs(LANES) + [row(512)] * 2 + [acc] * 3,
        scratch_shapes=[pltpu.VMEM((512 // LANES, TM, LANES), F32)],
        compiler_params=_cparams(1),
    )(dx1, w_out, attn, gb, gc, xi, gc, xi, w_sc, g_a, g_c, dep)
    return res[0:3], res[3:6], res[6], res[7], res[8], res[9], res[10]


def _swa_bwd(qc, kc, vc, doc, lsec, ddc, bias, dil, dep):
    nsub, nb, ncls = _swa_steps(qc, dil)
    n128 = nsub * nb
    whole = nb == 1

    def body(q_ref, qn_ref, kp_ref, kc_ref, vp_ref, vc_ref, do_ref, don_ref, lse_ref, lsen_ref, dd_ref, ddn_ref,
             b_ref, dep_ref, dq_ref, dk_ref, dv_ref, db_ref, s_scr, dp_scr, sn_scr, dpn_scr, ds_scr, p_scr, dsn_scr,
             pn_scr):
        r, b = pl.program_id(0), pl.program_id(1)

        @pl.when((r == 0) & (b == 0))
        def _():
            db_ref[...] = jnp.zeros_like(db_ref)

        pairs = [slice(a * LANES, (a + 1) * LANES) for a in range(N_HEADS // 2)]
        blk = [slice(t * WIN, (t + 1) * WIN) for t in range(nsub)]
        last = blk[nsub - 1]
        cols = lambda t: slice(WIN, 2 * WIN) if whole and t == 0 else slice(0, 2 * WIN)
        of_head = lambda ref, c, rows, h: ref[c, rows, _head_lane(h):_head_lane(h) + 1]
        no_prev = (b == 0) & (lax.broadcasted_iota(jnp.int32, (WIN, 2 * WIN), 1) < WIN)

        def keys(prev_ref, cur_ref, c, t, sl):
            if whole and t == 0:
                return cur_ref[c, blk[0], sl]
            if t == 0:
                return jnp.concatenate([prev_ref[c, :, sl], cur_ref[c, blk[0], sl]], axis=0)
            return cur_ref[c, (t - 1) * WIN:(t + 1) * WIN, sl]

        for a, sl in enumerate(pairs):
            for c, t in [(c, t) for c in range(ncls) for t in range(nsub)]:
                k2, v2 = keys(kp_ref, kc_ref, c, t, sl), keys(vp_ref, vc_ref, c, t, sl)
                q_eo = _pair_split(q_ref[c, blk[t], sl])
                do_eo = _pair_split(do_ref[c, blk[t], sl].astype(BF16))
                for e in range(2):
                    s_scr[c * nsub + t, 2 * a + e, :, cols(t)] = _dot_nt(q_eo[e], k2)
                    dp_scr[c * nsub + t, 2 * a + e, :, cols(t)] = _dot_nt(do_eo[e], v2)
            if not whole:
                qn_eo = _pair_split(qn_ref[0, :, sl])
                don_eo = _pair_split(don_ref[0, :, sl].astype(BF16))
                for e in range(2):
                    sn_scr[2 * a + e] = _dot_nt(qn_eo[e], kc_ref[0, last, sl])
                    dpn_scr[2 * a + e] = _dot_nt(don_eo[e], vc_ref[0, last, sl])
        for c, t, h in [(c, t, h) for c in range(ncls) for t in range(nsub) for h in range(N_HEADS)]:
            i, cl = c * nsub + t, cols(t)
            lg = s_scr[i, h, :, cl] + b_ref[h, :, cl]
            if t == 0 and not whole:
                lg = jnp.where(no_prev, -jnp.inf, lg)
            p = jnp.exp(lg - of_head(lse_ref, c, blk[t], h))
            ds = p * (dp_scr[i, h, :, cl] - of_head(dd_ref, c, blk[t], h))
            db_ref[h, :, cl] += ds
            ds_scr[i, h, :, cl] = ds.astype(BF16)
            p_scr[i, h, :, cl] = p.astype(BF16)
        if not whole:
            every = slice(0, WIN)
            for h in range(N_HEADS):
                lgn = jnp.where(b + 1 < nb, sn_scr[h] + b_ref[h, :, :WIN], -jnp.inf)
                pn = jnp.exp(lgn - of_head(lsen_ref, 0, every, h))
                dsn_scr[h] = (pn * (dpn_scr[h] - of_head(ddn_ref, 0, every, h))).astype(BF16)
                pn_scr[h] = pn.astype(BF16)
        for a, sl in enumerate(pairs):
            for c in range(ncls):
                q_eo = [_pair_split(q_ref[c, blk[t], sl]) for t in range(nsub)]
                do_eo = [_pair_split(do_ref[c, blk[t], sl].astype(BF16)) for t in range(nsub)]
                if not whole:
                    q_eo.append(_pair_split(qn_ref[0, :, sl]))
                    do_eo.append(_pair_split(don_ref[0, :, sl].astype(BF16)))
                for t in range(nsub):
                    i = c * nsub + t
                    k_eo = _pair_split(keys(kp_ref, kc_ref, c, t, sl))
                    dq, dk, dv = None, None, None
                    for e in range(2):
                        h = 2 * a + e
                        terms = [_dot(ds_scr[i, h, :, cols(t)], k_eo[e]),
                                 _dot_tn(ds_scr[i, h, :, WIN:], q_eo[t][e]),
                                 _dot_tn(p_scr[i, h, :, WIN:], do_eo[t][e])]
                        if t + 1 < nsub or not whole:
                            ds_next = ds_scr[i + 1, h, :, :WIN] if t + 1 < nsub else dsn_scr[h]
                            p_next = p_scr[i + 1, h, :, :WIN] if t + 1 < nsub else pn_scr[h]
                            terms[1] += _dot_tn(ds_next, q_eo[t + 1][e])
                            terms[2] += _dot_tn(p_next, do_eo[t + 1][e])
                        dq, dk, dv = terms if e == 0 else (dq + terms[0], dk + terms[1], dv + terms[2])
                    dq_ref[c, blk[t], sl] = dq.astype(BF16)
                    dk_ref[c, blk[t], sl] = dk.astype(BF16)
                    dv_ref[c, blk[t], sl] = dv.astype(BF16)

    cur = pl.BlockSpec((ncls, nsub * WIN, 512), lambda r, b: (r, b, 0))
    prev = pl.BlockSpec((ncls, WIN, 512), lambda r, b: (r, jnp.maximum(nsub * b - 1, 0), 0))
    nxt = pl.BlockSpec((ncls, WIN, 512), lambda r, b: (r, jnp.minimum(nsub * b + nsub, n128 - 1), 0))
    cur_h = pl.BlockSpec((ncls, nsub * WIN, LANES), cur.index_map)
    nxt_h = pl.BlockSpec((ncls, WIN, LANES), nxt.index_map)
    wide, narrow = (ncls * nsub, N_HEADS, WIN, 2 * WIN), (N_HEADS, WIN, WIN)
    return pl.pallas_call(
        body, name=f"swa_bwd_d{dil}", grid=(dil // ncls, nb),
        out_shape=[jax.ShapeDtypeStruct(qc.shape, BF16)] * 3 + [jax.ShapeDtypeStruct(bias.shape, F32)],
        in_specs=[cur, nxt, prev, cur, prev, cur, cur, nxt, cur_h, nxt_h, cur_h, nxt_h, _full(bias.shape),
                  ANY_SPEC],
        out_specs=[cur] * 3 + [_full(bias.shape)],
        scratch_shapes=[pltpu.VMEM(wide, F32), pltpu.VMEM(wide, F32), pltpu.VMEM(narrow, F32),
                        pltpu.VMEM(narrow, F32), pltpu.VMEM(wide, BF16), pltpu.VMEM(wide, BF16),
                        pltpu.VMEM(narrow, BF16), pltpu.VMEM(narrow, BF16)],
        compiler_params=_cparams(2),
    )(qc, qc, kc, kc, vc, vc, doc, doc, lsec, lsec, ddc, ddc, bias, dep)


def _in_proj_bwd(dqs, dks, dvs, dgb, dcv, gc, xi, w_sc, w_in_g, x, g_mix, dx1):
    s = x.shape[0]
    tb = TM // SUBLANES
    last = s // SUBLANES - 1
    n_tiles = s // TM

    def body(dq1, dq4, dq16, dk1, dk4, dk16, dv1, dv4, dv16, dgb_ref, dcv_ref, dcvn_ref, gc_ref, xi_ref, wsc_ref,
             w_hbm, x_ref, g_ref, dx1_ref, dproj_ref, gx_ref, dg_ref, scr_a, scr_b, w_scr, w_sems):
        i = pl.program_id(0)
        _load_w_in_pairs(w_hbm, w_scr, w_sems)

        @pl.when(i == 0)
        def _():
            dg_ref[...] = jnp.zeros_like(dg_ref)

        d0 = dcv_ref[...]
        dn = jnp.where(i < n_tiles - 1, dcvn_ref[...], 0.0)
        du = (d0 * wsc_ref[2:3, :] + _shift_up(d0, dn, 1) * wsc_ref[1:2, :]) + _shift_up(d0, dn, 2) * wsc_ref[0:1, :]
        merge = lambda a, b4, b16: ((a[...].astype(F32) + _gather_classes(b4, scr_a, 4))
                                    + _gather_classes(b16, scr_b, 16))
        dq = merge(dq1, dq4, dq16) * (HEAD_DIM ** -0.5)
        dk = merge(dk1, dk4, dk16)
        dv = merge(dv1, dv4, dv16)
        dproj = jnp.concatenate([dq, dk, dv, dgb_ref[...].astype(F32), du * xi_ref[...], du * gc_ref[...]],
                                axis=1).astype(BF16)
        dproj_ref[...] = dproj
        dh = jnp.zeros((TM, D_MODEL), F32)
        for j in range(N_DEV // 2):
            dh = dh + _dot_nt(dproj[:, 2 * j * IN_CHUNK:2 * (j + 1) * IN_CHUNK], w_scr[j])
        xh, r = _rms(x_ref[...])
        dg_ref[0:1, :] += jnp.sum(dh * xh, axis=0, keepdims=True)
        gx_ref[...] = dx1_ref[...] + _rms_bwd(xh, r, g_ref[...], dh)

    row = lambda n: pl.BlockSpec((TM, n), lambda i: (i, 0))
    nxt = pl.BlockSpec((SUBLANES, 512), lambda i: (jnp.minimum((i + 1) * tb, last), 0))
    return pl.pallas_call(
        body, name="in_proj_bwd", grid=(n_tiles,),
        out_shape=[jax.ShapeDtypeStruct((s, IN_COLS), BF16), jax.ShapeDtypeStruct((s, D_MODEL), F32),
                   jax.ShapeDtypeStruct((SUBLANES, D_MODEL), F32)],
        in_specs=_class_specs(512) * 3 + [row(512), row(512), nxt, row(512), row(512), _full(w_sc.shape),
                                          ANY_SPEC, row(D_MODEL), _full(g_mix.shape), row(D_MODEL)],
        out_specs=[row(IN_COLS), row(D_MODEL), _full((SUBLANES, D_MODEL))],
        scratch_shapes=[pltpu.VMEM((512 // LANES, TM, LANES), F32)] * 2 + W_IN_PAIRS,
        compiler_params=_cparams(1),
    )(*dqs, *dks, *dvs, dgb, dcv, dcv, gc, xi, w_sc, w_in_g, x, g_mix, dx1)


def _dw(a, b, dep, name, a_chunked=False, b_chunked=False, n_chunks=1, chunk_cols=None, per_step=1):
    single = not (a_chunked or b_chunked or chunk_cols)
    wide = a_chunked and a.shape[2] > D_MODEL
    ts = TS_DW // 2 if single or wide else TS_DW
    if a_chunked:
        nj, s, kk = a.shape
        nn = b.shape[1]
        a_spec = pl.BlockSpec((1, ts, kk), lambda j, t: (j, t, 0))
        b_spec = pl.BlockSpec((ts, nn), lambda j, t: (t, 0))
    elif b_chunked:
        nj, s, nn = b.shape
        kk = a.shape[1]
        a_spec = pl.BlockSpec((ts, kk), lambda j, t: (t, 0))
        b_spec = pl.BlockSpec((1, ts, nn), lambda j, t: (j, t, 0))
    else:
        s, kk = a.shape
        nj, nn = (n_chunks // per_step, chunk_cols * per_step) if chunk_cols else (1, b.shape[1])
        a_spec = pl.BlockSpec((ts, kk), lambda j, t: (t, 0))
        b_spec = pl.BlockSpec((ts, nn), lambda j, t: (t, j))
    n_steps = s // ts

    def body(a_ref, b_ref, dep_ref, o_ref, acc):
        t = pl.program_id(1)

        @pl.when(t == 0)
        def _():
            acc[...] = jnp.zeros_like(acc)

        av = (a_ref[0] if a_chunked else a_ref[...]).astype(BF16)
        bv = (b_ref[0] if b_chunked else b_ref[...]).astype(BF16)
        acc[...] += _dot_tn(av, bv)

        @pl.when(t == n_steps - 1)
        def _():
            for q in range(per_step):
                o_ref[q] = acc[:, q * nn // per_step:(q + 1) * nn // per_step].astype(BF16)

    return pl.pallas_call(
        body, name=name, grid=(nj, n_steps),
        out_shape=jax.ShapeDtypeStruct((nj * per_step, kk, nn // per_step), BF16),
        in_specs=[a_spec, b_spec, ANY_SPEC],
        out_specs=pl.BlockSpec((per_step, kk, nn // per_step), lambda j, t: (j, 0, 0)),
        scratch_shapes=[pltpu.VMEM((kk, nn), F32)],
        compiler_params=_cparams(2),
    )(a, b, dep)


def _adamw_math(w, g, m, v):
    m2 = ADAM_B1 * m + (1.0 - ADAM_B1) * g
    v2 = ADAM_B2 * v + (1.0 - ADAM_B2) * (g * g)
    m_hat = m2 / (1.0 - ADAM_B1 ** ADAM_STEP)
    v_hat = v2 / (1.0 - ADAM_B2 ** ADAM_STEP)
    delta = -ADAM_LR * (m_hat / (jnp.sqrt(v_hat) + ADAM_EPS) + ADAM_WD * w)
    return delta, m2, v2


def _sum_parts(me, own, p_ref):
    g = None
    for i in range(N_DEV):
        part = jnp.where(me == i, own.astype(F32), p_ref[i].astype(F32))
        g = part if g is None else g + part
    return g


def _adamw_big(name, w, sent, parts, m, v, me_arr):
    rr, cc = w.shape
    tr = rr // 4 if rr >= 512 else rr

    def body(me_ref, w_ref, own_ref, p_ref, m_ref, v_ref, g_ref, d_ref, nm_ref, nv_ref):
        g = own_ref[0].astype(F32)
        for k in range(1, N_DEV):
            g = g + p_ref[(me_ref[0] + k) % N_DEV].astype(F32)
        g_ref[...] = g
        d_ref[...], nm_ref[...], nv_ref[...] = _adamw_math(w_ref[...], g, m_ref[...], v_ref[...])

    row = pl.BlockSpec((tr, cc), lambda i, me: (i, 0))
    return pl.pallas_call(
        body, name=name,
        grid_spec=pltpu.PrefetchScalarGridSpec(
            num_scalar_prefetch=1, grid=(rr // tr,),
            in_specs=[row, pl.BlockSpec((1, tr, cc), lambda i, me: (me[0], i, 0)),
                      pl.BlockSpec((N_DEV, tr, cc), lambda i, me: (0, i, 0)), row, row],
            out_specs=[row] * 4),
        out_shape=[jax.ShapeDtypeStruct((rr, cc), F32)] * 4,
        compiler_params=_cparams(1),
    )(me_arr, w, sent, parts, m, v)


def _small_slices():
    return [
        (slice(ROW_RELB, ROW_RELB + 8), slice(0, N_BUCKETS)),
        (slice(ROW_GMIX, ROW_GMIX + 1), slice(0, D_MODEL)),
        (slice(ROW_GAC, ROW_GAC + 1), slice(0, ATTN_W)),
        (slice(ROW_GAC, ROW_GAC + 1), slice(ATTN_W, D_MODEL)),
        (slice(ROW_GXATTN, ROW_GXATTN + 1), slice(0, D_MODEL)),
        (slice(ROW_GMEM, ROW_GMEM + 1), slice(0, D_MODEL)),
        (slice(ROW_GFFN, ROW_GFFN + 1), slice(0, D_MODEL)),
        (slice(ROW_BFC, ROW_BFC + 8), slice(0, UP_CHUNK)),
        (slice(ROW_GFINAL, ROW_GFINAL + 1), slice(0, D_MODEL)),
    ]


def _adamw_small(own, parts, wmv, me_arr):
    slices = _small_slices()
    n = len(slices)

    def body(*refs):
        me_ref, own_ref, p_ref = refs[:3]
        ins = refs[3:3 + 3 * n]
        g_ref = refs[3 + 3 * n]
        outs = refs[4 + 3 * n:]
        g = _sum_parts(me_ref[0], own_ref[...], p_ref)
        g_ref[...] = g
        for a, (rs, ls) in enumerate(slices):
            ga = g[rs, ls]
            outs[4 * a][...] = ga
            outs[4 * a + 1][...], outs[4 * a + 2][...], outs[4 * a + 3][...] = _adamw_math(
                ins[3 * a][...], ga, ins[3 * a + 1][...], ins[3 * a + 2][...])

    vm = pl.BlockSpec(memory_space=pltpu.VMEM)
    flat = [t for trip in wmv for t in trip]
    out_shape = [jax.ShapeDtypeStruct((SMALL_ROWS, D_MODEL), F32)]
    for w, _, _ in wmv:
        out_shape += [jax.ShapeDtypeStruct(w.shape, F32)] * 4
    res = pl.pallas_call(
        body, name="adamw_small", out_shape=out_shape,
        in_specs=[SMEM_SPEC] + [vm] * (2 + 3 * n), out_specs=[vm] * len(out_shape),
    )(me_arr, own, parts, *flat)
    return res[0], [res[1 + 4 * a:5 + 4 * a] for a in range(n)]


def _adamw_shards(items):
    n = len(items)

    def body(*refs):
        for a in range(n):
            w_ref, g_ref, m_ref, v_ref = refs[4 * a:4 * a + 4]
            d_ref, nm_ref, nv_ref = refs[4 * n + 3 * a:4 * n + 3 * a + 3]
            d_ref[...], nm_ref[...], nv_ref[...] = _adamw_math(w_ref[...], g_ref[...], m_ref[...], v_ref[...])

    vm = pl.BlockSpec(memory_space=pltpu.VMEM)
    out_shape = []
    for w, _, _, _ in items:
        out_shape += [jax.ShapeDtypeStruct(w.shape, F32)] * 3
    res = pl.pallas_call(
        body, name="adamw_shards", out_shape=out_shape, in_specs=[vm] * (4 * n), out_specs=[vm] * (3 * n),
    )(*[t for it in items for t in it])
    return [res[3 * a:3 * a + 3] for a in range(n)]


def _mesh_pos():
    return lax.axis_index("x"), lax.axis_index("y"), lax.axis_index("c")


def _dev_index(p):
    return 4 * p[0] + 2 * p[1] + p[2]


def _all_gather(shards):
    n = len(shards)

    def body(*refs):
        ins, outs = refs[:n], refs[n:2 * n]
        send_sems, recv_sems, loc_sems = refs[2 * n:]
        x, y, c = _mesh_pos()
        me, sib = (x, y, c), (x, y, 1 - c)
        chips = [(1 - x, y), (x, 1 - y), (1 - x, 1 - y)]

        def cp(a, k, block, to, src=None):
            dst = outs[a].at[_dev_index(block)]
            return pltpu.make_async_remote_copy(
                src_ref=dst if src is None else src, dst_ref=dst, send_sem=send_sems.at[a, k],
                recv_sem=recv_sems.at[a, k], device_id=to, device_id_type=MESH)

        mine = [pltpu.make_async_copy(ins[a], outs[a].at[_dev_index(me)], loc_sems.at[a]) for a in range(n)]
        for m_ in mine:
            m_.start()
        first = []
        for a in range(n):
            first.append(cp(a, 0, me, sib, src=ins[a]))
            first += [cp(a, 1 + j, me, (*chip, c), src=ins[a]) for j, chip in enumerate(chips)]
        for f in first:
            f.start()
        passed = []
        for a in range(n):
            for j, chip in enumerate(chips):
                cp(a, 1 + j, (*chip, c), me).wait_recv()
                fwd = cp(a, 4 + j, (*chip, c), sib)
                fwd.start()
                passed.append(fwd)
        for a in range(n):
            cp(a, 0, sib, me).wait_recv()
            for j, chip in enumerate(chips):
                cp(a, 4 + j, (*chip, 1 - c), me).wait_recv()
        for f in first + passed:
            f.wait_send()
        for m_ in mine:
            m_.wait()

    hbm = pl.BlockSpec(memory_space=pltpu.HBM)
    return pl.pallas_call(
        body, name="all_gather_weights",
        out_shape=[jax.ShapeDtypeStruct((N_DEV,) + a.shape, a.dtype) for a in shards],
        in_specs=[hbm] * n, out_specs=[hbm] * n,
        scratch_shapes=[pltpu.SemaphoreType.DMA((n, 7)), pltpu.SemaphoreType.DMA((n, 7)),
                        pltpu.SemaphoreType.DMA((n,))],
    )(*shards)


def _peers():
    x, y, c = _mesh_pos()
    return (x, y, c), [((1 - x) if k & 4 else x, (1 - y) if k & 2 else y, (1 - c) if k & 1 else c)
                       for k in range(1, 8)]


def _exchange_copy(src_ref, land_ref, whole, send_sems, recv_sems, a, k, peer, slot):
    src = src_ref if whole else src_ref.at[_dev_index(peer)]
    return pltpu.make_async_remote_copy(
        src_ref=src, dst_ref=land_ref.at[slot], send_sem=send_sems.at[7 * a + k], recv_sem=recv_sems.at[7 * a + k],
        device_id=peer, device_id_type=MESH)


def _exchange_start(name, srcs, whole, dep):
    n = len(srcs)
    lands = [lax.empty(((N_DEV,) + s.shape) if w else s.shape, s.dtype) for s, w in zip(srcs, whole)]

    def body(*refs):
        src_refs, land_refs = refs[:n], refs[n:2 * n]
        send_sems, recv_sems, token = refs[2 * n + 1], refs[2 * n + 2], refs[-1]
        me, peers = _peers()
        for a in range(n):
            for k, peer in enumerate(peers):
                _exchange_copy(src_refs[a], land_refs[a], whole[a], send_sems, recv_sems, a, k, peer,
                               _dev_index(me)).start()
        token[...] = jnp.zeros_like(token)

    res = pl.pallas_call(
        body, name=name,
        out_shape=(pltpu.SemaphoreType.DMA((7 * n,)), pltpu.SemaphoreType.DMA((7 * n,)),
                   *[pltpu.HBM(a.shape, a.dtype) for a in srcs], *[pltpu.HBM(a.shape, a.dtype) for a in lands],
                   jax.ShapeDtypeStruct((SUBLANES, 128), F32)),
        in_specs=[HBM_SPEC] * (2 * n) + [ANY_SPEC],
        out_specs=(SEM_SPEC, SEM_SPEC, *([HBM_SPEC] * (2 * n)), VMEM_SPEC),
        input_output_aliases={i: 2 + i for i in range(2 * n)},
        compiler_params=pltpu.CompilerParams(has_side_effects=DATAFLOW),
    )(*[pltpu.with_memory_space_constraint(a, pltpu.HBM) for a in srcs],
      *[pltpu.with_memory_space_constraint(a, pltpu.HBM) for a in lands], dep)
    return res[0], res[1], list(res[2:2 + n]), list(res[2 + n:2 + 2 * n]), res[-1]


def _exchange_wait(name, started, whole, after, which=None):
    send_sems, recv_sems, srcs, lands, _ = started
    which = list(range(len(srcs))) if which is None else which
    srcs, lands = [srcs[a] for a in which], [lands[a] for a in which]
    n = len(srcs)

    def body(*refs):
        src_refs, land_refs = refs[:n], refs[n:2 * n]
        send_sems, recv_sems = refs[2 * n], refs[2 * n + 1]
        _, peers = _peers()
        for i, a in enumerate(which):
            for k, peer in enumerate(peers):
                cp = _exchange_copy(src_refs[i], land_refs[i], whole[a], send_sems, recv_sems, a, k, peer,
                                    _dev_index(peer))
                cp.wait_send()
                cp.wait_recv()

    res = pl.pallas_call(
        body, name=name,
        out_shape=[pltpu.HBM(a.shape, a.dtype) for a in srcs + lands],
        in_specs=[HBM_SPEC] * (2 * n) + [SEM_SPEC, SEM_SPEC, ANY_SPEC],
        out_specs=[HBM_SPEC] * (2 * n),
        input_output_aliases={i: i for i in range(2 * n)},
        compiler_params=pltpu.CompilerParams(has_side_effects=DATAFLOW),
    )(*srcs, *lands, send_sems, recv_sems, after)
    return list(res[:n]), list(res[n:])


def _gather_start(name, shards, dep):
    n = len(shards)
    lands = [lax.empty((N_DEV,) + a.shape, a.dtype) for a in shards]

    def body(*refs):
        src_refs, land_refs = refs[:n], refs[n:2 * n]
        send_sems, recv_sems, token = refs[2 * n + 1], refs[2 * n + 2], refs[-1]
        x, y, c = _mesh_pos()
        peers = [(x, y, 1 - c), (1 - x, y, c), (x, 1 - y, c), (1 - x, 1 - y, c)]
        for a in range(n):
            for k, peer in enumerate(peers):
                pltpu.make_async_remote_copy(
                    src_ref=src_refs[a], dst_ref=land_refs[a].at[_dev_index((x, y, c))], send_sem=send_sems.at[4 * a + k],
                    recv_sem=recv_sems.at[4 * a + k], device_id=peer, device_id_type=MESH).start()
        token[...] = jnp.zeros_like(token)

    res = pl.pallas_call(
        body, name=name,
        out_shape=(pltpu.SemaphoreType.DMA((4 * n,)), pltpu.SemaphoreType.DMA((4 * n,)),
                   *[pltpu.HBM(a.shape, a.dtype) for a in shards], *[pltpu.HBM(a.shape, a.dtype) for a in lands],
                   jax.ShapeDtypeStruct((SUBLANES, 128), F32)),
        in_specs=[HBM_SPEC] * (2 * n) + [ANY_SPEC],
        out_specs=(SEM_SPEC, SEM_SPEC, *([HBM_SPEC] * (2 * n)), VMEM_SPEC),
        input_output_aliases={i: 2 + i for i in range(2 * n)},
        compiler_params=pltpu.CompilerParams(has_side_effects=DATAFLOW),
    )(*[pltpu.with_memory_space_constraint(a, pltpu.HBM) for a in shards],
      *[pltpu.with_memory_space_constraint(a, pltpu.HBM) for a in lands], dep)
    return res[0], res[1], list(res[2:2 + n]), list(res[2 + n:2 + 2 * n]), res[-1]


def _gather_forward(name, send_sems, recv_sems, lands, which, after):
    n = len(which)

    def body(*refs):
        land_refs = refs[:n]
        send_sems, recv_sems = refs[n], refs[n + 1]
        fsend, frecv, token = refs[n + 3], refs[n + 4], refs[-1]
        x, y, c = _mesh_pos()
        chips = [(1 - x, y), (x, 1 - y), (1 - x, 1 - y)]
        for i, a in enumerate(which):
            for j, chip in enumerate(chips):
                block = land_refs[i].at[_dev_index((*chip, c))]
                pltpu.make_async_remote_copy(
                    src_ref=block, dst_ref=block, send_sem=send_sems.at[4 * a + 1 + j], recv_sem=recv_sems.at[4 * a + 1 + j],
                    device_id=(*chip, c), device_id_type=MESH).wait_recv()
                pltpu.make_async_remote_copy(
                    src_ref=block, dst_ref=block, send_sem=fsend.at[3 * i + j], recv_sem=frecv.at[3 * i + j],
                    device_id=(x, y, 1 - c), device_id_type=MESH).start()
        token[...] = jnp.zeros_like(token)

    res = pl.pallas_call(
        body, name=name,
        out_shape=(pltpu.SemaphoreType.DMA((3 * n,)), pltpu.SemaphoreType.DMA((3 * n,)),
                   *[pltpu.HBM(a.shape, a.dtype) for a in lands], jax.ShapeDtypeStruct((SUBLANES, 128), F32)),
        in_specs=[HBM_SPEC] * n + [SEM_SPEC, SEM_SPEC, ANY_SPEC],
        out_specs=(SEM_SPEC, SEM_SPEC, *([HBM_SPEC] * n), VMEM_SPEC),
        input_output_aliases={i: 2 + i for i in range(n)},
        compiler_params=pltpu.CompilerParams(has_side_effects=DATAFLOW),
    )(*lands, send_sems, recv_sems, after)
    return res[0], res[1], list(res[2:2 + n]), res[-1]


def _gather_wait(name, send_sems, recv_sems, fsend, frecv, srcs, lands, which, after):
    n = len(which)

    def body(*refs):
        land_refs = refs[n:2 * n]
        send_sems, recv_sems, fsend, frecv = refs[2 * n:2 * n + 4]
        x, y, c = _mesh_pos()
        sib = (x, y, 1 - c)
        chips = [(1 - x, y), (x, 1 - y), (1 - x, 1 - y)]
        for i, a in enumerate(which):
            def cp(slot, ssem, rsem):
                block = land_refs[i].at[_dev_index(slot)]
                return pltpu.make_async_remote_copy(src_ref=block, dst_ref=block, send_sem=ssem, recv_sem=rsem,
                                                    device_id=sib, device_id_type=MESH)
            cp(sib, send_sems.at[4 * a], recv_sems.at[4 * a]).wait_recv()
            for j, chip in enumerate(chips):
                cp((*chip, 1 - c), fsend.at[3 * i + j], frecv.at[3 * i + j]).wait_recv()
            for k in range(4):
                cp(sib, send_sems.at[4 * a + k], recv_sems.at[4 * a + k]).wait_send()
            for j in range(3):
                cp(sib, fsend.at[3 * i + j], frecv.at[3 * i + j]).wait_send()

    res = pl.pallas_call(
        body, name=name,
        out_shape=[pltpu.HBM(a.shape, a.dtype) for a in srcs + lands],
        in_specs=[HBM_SPEC] * (2 * n) + [SEM_SPEC] * 4 + [ANY_SPEC],
        out_specs=[HBM_SPEC] * (2 * n),
        input_output_aliases={i: i for i in range(2 * n)},
        compiler_params=pltpu.CompilerParams(has_side_effects=DATAFLOW),
    )(*srcs, *lands, send_sems, recv_sems, fsend, frecv, after)
    return list(res[n:])


def _local_step(x, mem, target, rel_bias, g_mix, w_in_g, w_sc, g_a, g_c, g_xattn, g_mem, g_ffn, w_fc, b_fc, g_final,
                dep, forward_weights, late_weights, emit, emit_small):
    s = x.shape[0]
    buckets = _bucket_tables()
    bias = _bias_fwd(rel_bias, buckets)

    h1, qs, ks, vs, gb, gc, xi = _rms_proj(x, g_mix, w_in_g, dep)
    qs, ks, vs = ([a[0][None]] + list(a[1:]) for a in (qs, ks, vs))
    group1, group2 = ["w_out", "w_xq", "w_xk", "w_xv", "w_xo"], ["w_up", "w_down"]
    tok = forward_weights(group1, h1)
    branches = []
    for p, dil in enumerate(DILATIONS):
        o_p, lse_p = _swa_fwd(qs[p], ks[p], vs[p], bias[p], dil, tok)
        branches.append([o_p[0], lse_p[0]] if dil == 1 else [o_p, lse_p])
    lw = late_weights(group1, branches[-1][0])
    w_out, w_xq, w_xk, w_xv, w_xo = (lw[n] for n in group1)
    attn, lses, mixed, x1 = _mix_out(branches, gb, gc, xi, x, w_sc, g_a, g_c, w_out)
    tok = forward_weights(group2, x1)
    mem_n, mk, mv = _mem_kv(mem, g_mem, w_xk, w_xv)
    h2, xq, xo, x2 = _xattn_fwd(x1, g_xattn, w_xq, mk, mv, w_xo, tok)
    lw = late_weights(group2, x2)
    w_up_g = lw["w_up"].reshape(FFN_CHUNKS, FFN_WIDTH, D_MODEL)
    w_down_g = lw["w_down"].reshape(FFN_CHUNKS // 2, FFN_WIDTH, D_MODEL)
    pairs = lambda a: a.reshape(FFN_CHUNKS, 2, a.shape[1], UP_CHUNK).transpose(0, 2, 1, 3).reshape(
        FFN_CHUNKS, a.shape[1], FFN_WIDTH)
    w_fc, b_fc = pairs(w_fc), pairs(b_fc)
    h3, up, conv, act, dx3, loss_acc, dg_final = _ffn_fwd(x2, g_ffn, w_up_g, w_fc, b_fc, w_down_g, g_final, target)

    gw_down = _dw(act, dx3, dep, "dw_down", a_chunked=True).reshape(N_DEV // 2, UP_CHUNK, D_MODEL)
    dup, dx2, dg_ffn, dw_fc, db_fc = _ffn_bwd(dx3, up, conv, x2, g_ffn, w_up_g, w_fc, w_down_g)
    gw_up = _dw(dup, h3, dep, "dw_up", a_chunked=True).reshape(N_DEV, UP_CHUNK, D_MODEL)
    tok = emit(dict(w_down=gw_down, w_up=gw_up))
    dxq, dx1, dmk, dmv, dg_xattn = _xattn_bwd(dx2, xo, xq, mk, mv, w_xo, w_xq, x1, g_xattn, tok)
    gw_xo = _dw(xo, dx2, tok, "dw_xo")[0]
    gw_xq = _dw(h2, dxq, tok, "dw_xq")[0]
    gw_xk, gw_xv, dg_mem = _mem_kv_bwd(dmk, dmv, mem_n, mem, w_xk, w_xv)
    tok = emit(dict(w_xo=gw_xo, w_xq=gw_xq, w_xk=gw_xk, w_xv=gw_xv))
    dattns, dds, dgb, dcv, dg_a, dg_c, dw_sc = _mix_out_bwd(dx1, w_out, attn, gb, gc, xi, w_sc, g_a, g_c, tok)
    first = lambda a: [a[0][None]] + list(a[1:])
    dattns, dds, lses = first(dattns), first(dds), first(lses)
    gw_out = _dw(mixed, dx1, tok, "dw_out")[0]
    tok = emit(dict(w_out=gw_out))
    dqs, dks, dvs, dbias = [], [], [], []
    for p, dil in enumerate(DILATIONS):
        dq_p, dk_p, dv_p, db_p = _swa_bwd(qs[p], ks[p], vs[p], dattns[p], lses[p], dds[p], bias[p], dil, tok)
        dqs.append(dq_p[0] if dil == 1 else dq_p)
        dks.append(dk_p[0] if dil == 1 else dk_p)
        dvs.append(dv_p[0] if dil == 1 else dv_p)
        dbias.append(db_p)
    d_relb = _bias_bwd(jnp.stack(dbias), buckets)
    dproj, grad_x, dg_mix = _in_proj_bwd(dqs, dks, dvs, dgb, dcv, gc, xi, w_sc, w_in_g, x, g_mix, dx1)
    pad = lambda a: jnp.pad(a, ((0, 0), (0, D_MODEL - a.shape[1])))
    small = jnp.concatenate([
        d_relb, dg_mix, dg_xattn, dg_mem, dg_ffn, dg_final, jnp.concatenate([dg_a, dg_c], axis=1),
        pad(dw_sc), pad(db_fc.reshape(N_DEV, UP_CHUNK)), pad(dw_fc.reshape(3 * N_DEV, UP_CHUNK)), pad(loss_acc)],
        axis=0)
    tok = emit_small(small)
    gw_in = _dw(h1, dproj, tok, "dw_in", n_chunks=N_DEV, chunk_cols=IN_CHUNK, per_step=2)
    emit(dict(w_in=gw_in))
    return grad_x


def kernel(x, mem, rel_bias, g_mix, w_in, w_short_conv, g_attn_out, g_conv_out, w_out, g_xattn, g_mem, w_xq, w_xk, w_xv, w_xo, g_ffn, w_up, w_ffn_conv, b_ffn_conv, w_down, g_final, loss_target, m_rel_bias, m_g_mix, m_w_in, m_w_short_conv, m_g_attn_out, m_g_conv_out, m_w_out, m_g_xattn, m_g_mem, m_w_xq, m_w_xk, m_w_xv, m_w_xo, m_g_ffn, m_w_up, m_w_ffn_conv, m_b_ffn_conv, m_w_down, m_g_final, v_rel_bias, v_g_mix, v_w_in, v_w_short_conv, v_g_attn_out, v_g_conv_out, v_w_out, v_g_xattn, v_g_mem, v_w_xq, v_w_xk, v_w_xv, v_w_xo, v_g_ffn, v_w_up, v_w_ffn_conv, v_b_ffn_conv, v_w_down, v_g_final):
    me = _dev_index(_mesh_pos())
    me_arr = me.reshape(1).astype(jnp.int32)

    big_names = ["w_in", "w_out", "w_xq", "w_xk", "w_xv", "w_xo", "w_up", "w_down"]
    late_names = big_names[1:]
    big_w = dict(w_in=w_in[0], w_out=w_out[0], w_xq=w_xq[0], w_xk=w_xk[0], w_xv=w_xv[0], w_xo=w_xo[0],
                 w_up=w_up[0].T, w_down=w_down[0])
    big_m = dict(w_in=m_w_in[0], w_out=m_w_out[0], w_xq=m_w_xq[0], w_xk=m_w_xk[0], w_xv=m_w_xv[0], w_xo=m_w_xo[0],
                 w_up=m_w_up[0].T, w_down=m_w_down[0])
    big_v = dict(w_in=v_w_in[0], w_out=v_w_out[0], w_xq=v_w_xq[0], w_xk=v_w_xk[0], w_xv=v_w_xv[0], w_xo=v_w_xo[0],
                 w_up=v_w_up[0].T, w_down=v_w_down[0])
    shard_shape = {n: big_w[n].shape for n in big_names}

    w_in_g, w_sc_g, w_fc_full = _all_gather([big_w["w_in"].astype(BF16), w_short_conv[0], w_ffn_conv[0]])
    w_sc_full = w_sc_g.transpose(1, 0, 2).reshape(3, CONV_W)
    late_shards = [big_w[n].astype(BF16) for n in late_names]
    ag_send, ag_recv, ag_srcs, ag_lands, ag_token = _gather_start("gather_weights_start", late_shards, w_in_g)
    forwarded = {}

    def forward_weights(names, after):
        which = [late_names.index(n) for n in names]
        fsend, frecv, lands, token = _gather_forward("gather_" + "_".join(names) + "_forward", ag_send, ag_recv,
                                                     [ag_lands[a] for a in which], which, after)
        forwarded[tuple(names)] = (fsend, frecv, lands)
        return token

    def late_weights(names, after):
        which = [late_names.index(n) for n in names]
        fsend, frecv, lands = forwarded[tuple(names)]
        lands = _gather_wait("gather_" + "_".join(names) + "_wait", ag_send, ag_recv, fsend, frecv,
                             [ag_srcs[a] for a in which], lands, which, after)
        out = {}
        for n, a, land in zip(names, which, lands):
            full = lax.dynamic_update_index_in_dim(land, late_shards[a], me, 0)
            if n == "w_up":
                out[n] = full
            elif n == "w_down":
                out[n] = full.reshape(N_DEV // 2, UP_CHUNK, D_MODEL)
            else:
                out[n] = full.reshape(D_MODEL, D_MODEL)
        return out

    sent = []

    def emit(grads):
        names = list(grads)
        blocks = [grads[n].reshape((N_DEV,) + shard_shape[n]) for n in names]
        started = _exchange_start("scatter_" + "_".join(names) + "_start", blocks, [False] * len(names), me_arr)
        sent.append((names, started))
        return started[-1]

    def emit_small(small):
        sent_small.append((small, _exchange_start("gather_small_start", [small], [True], me_arr)))
        return sent_small[0][1][-1]

    sent_small = []
    grad_x = _local_step(
        x[0], mem[0], loss_target[0], rel_bias, g_mix, w_in_g, w_sc_full, g_attn_out, g_conv_out, g_xattn, g_mem,
        g_ffn, w_fc_full, b_ffn_conv.reshape(N_DEV, 1, UP_CHUNK), g_final.reshape(1, D_MODEL), ag_token,
        forward_weights, late_weights, emit, emit_small)

    small_g, small_started = sent_small[0]
    after = sent[-1][1][-1]
    small_parts = _exchange_wait("gather_small_wait", small_started, [True], after)[1][0]
    big_out = {}
    after = small_parts
    for names, started in sent:
        blocks, lands = _exchange_wait("scatter_" + "_".join(names) + "_wait", started, [False] * len(names), after)
        for n, block, land in zip(names, blocks, lands):
            res = _adamw_big("adamw_" + n, big_w[n], block, land, big_m[n], big_v[n], me_arr)
            big_out[n] = [(r.T if n == "w_up" else r)[None] for r in res]
            after = res[0]

    as_rows = lambda a: a.reshape(N_DEV, UP_CHUNK)
    row1 = lambda a: a.reshape(1, D_MODEL)
    small_names = ["rel_bias", "g_mix", "g_attn_out", "g_conv_out", "g_xattn", "g_mem", "g_ffn", "b_ffn_conv", "g_final"]
    wmv = [
        (rel_bias, m_rel_bias, v_rel_bias), (g_mix, m_g_mix, v_g_mix), (g_attn_out, m_g_attn_out, v_g_attn_out),
        (g_conv_out, m_g_conv_out, v_g_conv_out), (g_xattn, m_g_xattn, v_g_xattn), (g_mem, m_g_mem, v_g_mem),
        (g_ffn, m_g_ffn, v_g_ffn), (as_rows(b_ffn_conv), as_rows(m_b_ffn_conv), as_rows(v_b_ffn_conv)),
        (row1(g_final), row1(m_g_final), row1(v_g_final))]
    g_packed, small_res = _adamw_small(small_g, small_parts, wmv, me_arr)
    small_out = dict(zip(small_names, small_res))
    loss = g_packed[ROW_LOSS, 0]
    small_out["b_ffn_conv"] = [a.reshape(1, 2 * D_FF) for a in small_out["b_ffn_conv"]]
    small_out["g_final"] = [a.reshape(D_MODEL) for a in small_out["g_final"]]

    g_wsc = lax.dynamic_slice(g_packed[ROW_WSC:ROW_WSC + 3, 0:CONV_W], (0, me * HEAD_DIM), (3, HEAD_DIM))
    g_wfc = lax.dynamic_slice(g_packed[ROW_WFC:ROW_WFC + 3 * N_DEV, 0:UP_CHUNK].reshape(3, N_DEV, UP_CHUNK),
                              (0, me, 0), (3, 1, UP_CHUNK)).reshape(3, UP_CHUNK)
    shard_res = _adamw_shards([(w_short_conv[0], g_wsc, m_w_short_conv[0], v_w_short_conv[0]),
                               (w_ffn_conv[0], g_wfc, m_w_ffn_conv[0], v_w_ffn_conv[0])])
    small_out["w_short_conv"] = [g_wsc[None]] + [a[None] for a in shard_res[0]]
    small_out["w_ffn_conv"] = [g_wfc[None]] + [a[None] for a in shard_res[1]]

    order = ["rel_bias", "g_mix", "w_in", "w_short_conv", "g_attn_out", "g_conv_out", "w_out", "g_xattn", "g_mem",
             "w_xq", "w_xk", "w_xv", "w_xo", "g_ffn", "w_up", "w_ffn_conv", "b_ffn_conv", "w_down", "g_final"]
    allp = {**big_out, **small_out}
    outs = [loss, grad_x[None]]
    for kind in range(4):
        outs += [allp[n][kind] for n in order]
    return tuple(outs)
```

```python
import math

import numpy as np
import jax
import jax.numpy as jnp
from jax import lax
from jax.experimental import pallas as pl
from jax.experimental.pallas import tpu as pltpu

F32 = jnp.float32
BF16 = jnp.bfloat16
MESH = pl.DeviceIdType.MESH

N_DEV = 8
D_MODEL = 1024
ATTN_W = 512
CONV_W = 512
N_HEADS = 8
HEAD_DIM = 64
WIN = 128
DILATIONS = (1, 4, 16)
N_BUCKETS = 32
BUCKET_MAX_EXACT = 16
BUCKET_MAX_DISTANCE = 2048
N_MEM_HEADS = 4
MEM_HEAD_DIM = 256
D_FF = 2816
IN_COLS = 3072
IN_CHUNK = IN_COLS // N_DEV
UP_CHUNK = 2 * D_FF // N_DEV
FFN_CHUNKS = 4
FFN_WIDTH = 2 * D_FF // FFN_CHUNKS
EPS = 1e-6

ADAM_LR = 0.001
ADAM_B1 = 0.9
ADAM_B2 = 0.999
ADAM_EPS = 1e-08
ADAM_WD = 0.01
ADAM_STEP = 10

SUBLANES = 8
LANES = 128
HALO = 16
TM = 512
TM_XATTN = 1024
TM_FFN = 256
TS_DW = 4096
SWA_BLOCKS = 8
VMEM_LIMIT = 56 * 1024 * 1024

ROW_RELB, ROW_GMIX, ROW_GXATTN, ROW_GMEM, ROW_GFFN, ROW_GFINAL, ROW_GAC = 0, 8, 16, 24, 32, 40, 48
ROW_WSC, ROW_BFC, ROW_WFC, ROW_LOSS, SMALL_ROWS = 56, 64, 72, 96, 104


def _cparams(n_grid):
    return pltpu.CompilerParams(dimension_semantics=("arbitrary",) * n_grid, vmem_limit_bytes=VMEM_LIMIT)


def _full(shape):
    nd = len(shape)
    return pl.BlockSpec(tuple(shape), lambda *_: (0,) * nd)


def _resident(shape):
    nd = len(shape)
    return pl.BlockSpec(tuple(shape), lambda *_: (0,) * nd, pipeline_mode=pl.Buffered(1))


ANY_SPEC = pl.BlockSpec(memory_space=pl.ANY)
HBM_SPEC = pl.BlockSpec(memory_space=pltpu.HBM)
SEM_SPEC = pl.BlockSpec(memory_space=pltpu.SEMAPHORE)
VMEM_SPEC = pl.BlockSpec(memory_space=pltpu.VMEM)
SMEM_SPEC = pl.BlockSpec(memory_space=pltpu.SMEM)
DATAFLOW = pltpu.SideEffectType.DATAFLOW_SIDE_EFFECTING


def _rms(x):
    r = lax.rsqrt(jnp.mean(x * x, axis=-1, keepdims=True) + EPS)
    return x * r, r


def _rms_bwd(xh, r, g, dy):
    dxh = dy * g
    return r * (dxh - xh * jnp.mean(dxh * xh, axis=-1, keepdims=True))


def _shift_down(u, halo, k):
    ru = pltpu.roll(u, k, 0)
    rh = pltpu.roll(halo, k, 0)
    row = lax.broadcasted_iota(jnp.int32, rh.shape, 0)
    head = jnp.where(row < k, rh, ru[0:SUBLANES])
    return jnp.concatenate([head, ru[SUBLANES:]], axis=0)


def _shift_up(u, halo, k):
    tm = u.shape[0]
    ru = pltpu.roll(u, tm - k, 0)
    rh = pltpu.roll(halo, SUBLANES - k, 0)
    row = lax.broadcasted_iota(jnp.int32, rh.shape, 0)
    tail = jnp.where(row >= SUBLANES - k, rh, ru[tm - SUBLANES:])
    return jnp.concatenate([ru[:tm - SUBLANES], tail], axis=0)


def _causal_conv3(u, halo, w_ref):
    return (_shift_down(u, halo, 2) * w_ref[0:1, :] + _shift_down(u, halo, 1) * w_ref[1:2, :]) + u * w_ref[2:3, :]


def _dot(a, b):
    return jnp.dot(a, b, preferred_element_type=F32)


def _dot_nt(a, b):
    return lax.dot_general(a, b, (((1,), (1,)), ((), ())), preferred_element_type=F32)


def _dot_tn(a, b):
    return lax.dot_general(a, b, (((0,), (0,)), ((), ())), preferred_element_type=F32)


def _sigmoid(x):
    return 0.5 * jnp.tanh(0.5 * x) + 0.5


def _bucket_tables():
    qi = np.arange(WIN)[:, None]
    kj = np.arange(2 * WIN)[None, :]
    steps = np.clip(qi + WIN - kj, 0, WIN)
    out = []
    for d in DILATIONS:
        dist = steps * d
        dd = np.maximum(dist, 1).astype(np.float32)
        large = BUCKET_MAX_EXACT + (
            np.log(dd / np.float32(BUCKET_MAX_EXACT)) / np.float32(math.log(BUCKET_MAX_DISTANCE / BUCKET_MAX_EXACT))
            * np.float32(N_BUCKETS - BUCKET_MAX_EXACT)).astype(np.int32)
        large = np.minimum(large, N_BUCKETS - 1)
        out.append(np.where(dist < BUCKET_MAX_EXACT, dist, large).astype(np.int32))
    return np.stack(out)


def _band_mask():
    qi = lax.broadcasted_iota(jnp.int32, (WIN, 2 * WIN), 0)
    kj = lax.broadcasted_iota(jnp.int32, (WIN, 2 * WIN), 1)
    steps = qi + WIN - kj
    return (steps >= 0) & (steps <= WIN)


def _bias_fwd(rel_bias, buckets):
    present = [sorted(set(buckets[p].ravel().tolist())) for p in range(3)]

    def body(rb_ref, bk_ref, o_ref):
        band = _band_mask()
        for p in range(3):
            bk = bk_ref[p]
            for h in range(N_HEADS):
                acc = jnp.zeros((WIN, 2 * WIN), F32)
                for b in present[p]:
                    acc = jnp.where(bk == b, rb_ref[h, b], acc)
                o_ref[p, h] = jnp.where(band, acc, -jnp.inf)

    return pl.pallas_call(
        body, name="bias_fwd",
        out_shape=jax.ShapeDtypeStruct((3, N_HEADS, WIN, 2 * WIN), F32),
        in_specs=[pl.BlockSpec(memory_space=pltpu.SMEM), pl.BlockSpec(memory_space=pltpu.VMEM)],
        out_specs=pl.BlockSpec(memory_space=pltpu.VMEM),
    )(rel_bias, jnp.asarray(buckets))


def _bias_bwd(dbias, buckets):
    present = [set(buckets[p].ravel().tolist()) for p in range(3)]

    def body(db_ref, bk_ref, o_ref):
        lane = lax.broadcasted_iota(jnp.int32, (1, D_MODEL), 1)
        rows = []
        for h in range(N_HEADS):
            row = jnp.zeros((1, D_MODEL), F32)
            for b in range(N_BUCKETS):
                tot = jnp.zeros((1, 1), F32)
                for p in (p for p in range(3) if b in present[p]):
                    sel = jnp.where(bk_ref[p] == b, db_ref[p, h], 0.0)
                    tot = tot + jnp.sum(jnp.sum(sel, axis=0, keepdims=True), axis=1, keepdims=True)
                row = jnp.where(lane == b, tot, row)
            rows.append(row)
        o_ref[...] = jnp.concatenate(rows, axis=0)

    return pl.pallas_call(
        body, name="bias_bwd",
        out_shape=jax.ShapeDtypeStruct((N_HEADS, D_MODEL), F32),
        in_specs=[pl.BlockSpec(memory_space=pltpu.VMEM), pl.BlockSpec(memory_space=pltpu.VMEM)],
        out_specs=pl.BlockSpec(memory_space=pltpu.VMEM),
    )(dbias, jnp.asarray(buckets))


def _spread(val, scr_ref, out_refs, dtype):
    out_refs[0][...] = val.astype(dtype)
    n_blk = val.shape[1] // LANES
    for c in range(n_blk):
        scr_ref[c] = val[:, c * LANES:(c + 1) * LANES]
    for o_ref, d in zip(out_refs[1:], DILATIONS[1:]):
        for r in range(d):
            for c in range(n_blk):
                o_ref[r, :, c * LANES:(c + 1) * LANES] = scr_ref.at[c][pl.ds(r, TM // d, stride=d), :].astype(dtype)


HEAD_LANES = LANES // N_HEADS


def _head_lane(h):
    return HEAD_LANES * (h // 2) + (LANES // 2) * (h % 2)


def _narrow_heads(x):
    grp = (lax.broadcasted_iota(jnp.int32, (x.shape[0], LANES), 1) // HEAD_LANES) % (N_HEADS // 2)
    out = x[:, 0:LANES]
    for t in range(1, N_HEADS // 2):
        out = jnp.where(grp == t, x[:, t * LANES:(t + 1) * LANES], out)
    return out


def _gather_classes(blk_ref, scr_ref, d):
    n_blk = blk_ref.shape[2] // LANES
    for r in range(d):
        for c in range(n_blk):
            scr_ref.at[c][pl.ds(r, TM // d, stride=d), :] = blk_ref[r, :, c * LANES:(c + 1) * LANES].astype(F32)
    return jnp.concatenate([scr_ref[c] for c in range(n_blk)], axis=1)


def _class_specs(cols):
    return [pl.BlockSpec((TM, cols), lambda i: (i, 0))] + [
        pl.BlockSpec((d, TM // d, cols), lambda i: (0, i, 0)) for d in DILATIONS[1:]]


def _class_shapes(s, cols, dtype):
    return [jax.ShapeDtypeStruct((s, cols), dtype)] + [
        jax.ShapeDtypeStruct((d, s // d, cols), dtype) for d in DILATIONS[1:]]


def _load_w_in_pairs(w_hbm, w_scr, sems):
    @pl.when(pl.program_id(0) == 0)
    def _():
        copies = [pltpu.make_async_copy(w_hbm.at[j], w_scr.at[j // 2, :, pl.ds((j % 2) * IN_CHUNK, IN_CHUNK)],
                                        sems.at[j]) for j in range(N_DEV)]
        for copy in copies:
            copy.start()
        for copy in copies:
            copy.wait()


W_IN_PAIRS = [pltpu.VMEM((N_DEV // 2, D_MODEL, 2 * IN_CHUNK), BF16), pltpu.SemaphoreType.DMA((N_DEV,))]


def _rms_proj(x, g_mix, w_in_g, dep):
    s = x.shape[0]

    def body(x_ref, g_ref, w_hbm, dep_ref, h_ref, q1, q4, q16, k1, k4, k16, v1, v4, v16, gb_ref, gc_ref, xi_ref, scr,
             w_scr, w_sems):
        _load_w_in_pairs(w_hbm, w_scr, w_sems)
        xh, _ = _rms(x_ref[...])
        h = (xh * g_ref[...]).astype(BF16)
        h_ref[...] = h
        proj = jnp.concatenate([_dot(h, w_scr[j]) for j in range(N_DEV // 2)], axis=1)
        _spread(proj[:, 0:512] * (HEAD_DIM ** -0.5), scr, (q1, q4, q16), BF16)
        _spread(proj[:, 512:1024], scr, (k1, k4, k16), BF16)
        _spread(proj[:, 1024:1536], scr, (v1, v4, v16), BF16)
        gb_ref[...] = proj[:, 1536:2048]
        gc_ref[...] = proj[:, 2048:2560]
        xi_ref[...] = proj[:, 2560:3072]

    row = lambda n: pl.BlockSpec((TM, n), lambda i: (i, 0))
    res = pl.pallas_call(
        body, name="rms_proj", grid=(s // TM,),
        out_shape=[jax.ShapeDtypeStruct((s, D_MODEL), BF16)] + _class_shapes(s, 512, BF16) * 3
        + [jax.ShapeDtypeStruct((s, 512), F32)] * 3,
        in_specs=[row(D_MODEL), _full(g_mix.shape), ANY_SPEC, ANY_SPEC],
        out_specs=[row(D_MODEL)] + _class_specs(512) * 3 + [row(512)] * 3,
        scratch_shapes=[pltpu.VMEM((512 // LANES, TM, LANES), F32)] + W_IN_PAIRS,
        compiler_params=_cparams(1),
    )(x, g_mix, w_in_g, dep)
    return res[0], res[1:4], res[4:7], res[7:10], res[10], res[11], res[12]


def _pair_split(x2):
    lane = lax.broadcasted_iota(jnp.int32, x2.shape, 1)
    zero = jnp.zeros_like(x2)
    return jnp.where(lane < HEAD_DIM, x2, zero), jnp.where(lane >= HEAD_DIM, x2, zero)


def _pair_join(even, odd):
    lane = lax.broadcasted_iota(jnp.int32, (even.shape[0], LANES), 1)
    return jnp.where(lane < HEAD_DIM, even, odd)


def _swa_steps(qc, dil):
    n128 = qc.shape[1] // WIN
    nsub = min(SWA_BLOCKS, n128)
    nb = n128 // nsub
    ncls = min(dil, SWA_BLOCKS // nsub) if nb == 1 else 1
    return nsub, nb, ncls


def _swa_fwd(qc, kc, vc, bias, dil, dep):
    nsub, nb, ncls = _swa_steps(qc, dil)
    whole = nb == 1

    def body(q_ref, kp_ref, kc_ref, vp_ref, vc_ref, b_ref, dep_ref, o_ref, lse_ref, s_scr, p_scr):
        no_prev = (pl.program_id(1) == 0) & (lax.broadcasted_iota(jnp.int32, (WIN, 2 * WIN), 1) < WIN)
        pairs = [slice(a * LANES, (a + 1) * LANES) for a in range(N_HEADS // 2)]
        for c, t in [(c, t) for c in range(ncls) for t in range(nsub)]:
            i = c * nsub + t
            rows = slice(t * WIN, (t + 1) * WIN)
            alone = whole and t == 0
            cols = slice(WIN, 2 * WIN) if alone else slice(0, 2 * WIN)

            def keys(prev_ref, cur_ref, sl):
                if alone:
                    return cur_ref[c, rows, sl]
                if t == 0:
                    return jnp.concatenate([prev_ref[c, :, sl], cur_ref[c, rows, sl]], axis=0)
                return cur_ref[c, (t - 1) * WIN:(t + 1) * WIN, sl]

            for a, sl in enumerate(pairs):
                k2 = keys(kp_ref, kc_ref, sl)
                for e, qh in enumerate(_pair_split(q_ref[c, rows, sl])):
                    s_scr[i, 2 * a + e, :, cols] = _dot_nt(qh, k2)
            den, lse = [], []
            for h in range(N_HEADS):
                lg = s_scr[i, h, :, cols] + b_ref[h, :, cols]
                if t == 0 and not whole:
                    lg = jnp.where(no_prev, -jnp.inf, lg)
                m = jnp.max(lg, axis=-1, keepdims=True)
                p = jnp.exp(lg - m)
                den.append(jnp.sum(p, axis=-1, keepdims=True))
                p_scr[i, h, :, cols] = p.astype(BF16)
                lse.append(m + jnp.log(den[h]))
            for a, sl in enumerate(pairs):
                v_even, v_odd = _pair_split(keys(vp_ref, vc_ref, sl))
                o2 = _dot(p_scr[i, 2 * a, :, cols], v_even) + _dot(p_scr[i, 2 * a + 1, :, cols], v_odd)
                o_ref[c, rows, sl] = o2 / _pair_join(den[2 * a], den[2 * a + 1])
                lse_ref[c, rows, sl] = _pair_join(lse[2 * a], lse[2 * a + 1])

    cur = pl.BlockSpec((ncls, nsub * WIN, 512), lambda r, b: (r, b, 0))
    prev = pl.BlockSpec((ncls, WIN, 512), lambda r, b: (r, jnp.maximum(nsub * b - 1, 0), 0))
    wide = (ncls * nsub, N_HEADS, WIN, 2 * WIN)
    return pl.pallas_call(
        body, name=f"swa_fwd_d{dil}", grid=(dil // ncls, nb),
        out_shape=[jax.ShapeDtypeStruct(qc.shape, F32)] * 2,
        in_specs=[cur, prev, cur, prev, cur, _full(bias.shape), ANY_SPEC],
        out_specs=[cur] * 2,
        scratch_shapes=[pltpu.VMEM(wide, F32), pltpu.VMEM(wide, BF16)],
        compiler_params=_cparams(2),
    )(qc, kc, kc, vc, vc, bias, dep)


def _mix_out(branches, gb, gc, xi, x, w_sc, g_a, g_c, w_out):
    s = x.shape[0]
    tb = TM // SUBLANES

    def body(o1, l1, o4, l4, o16, l16, gb_ref, gc_ref, xi_ref, gch_ref, xih_ref, x_ref, wsc_ref,
             ga_ref, gcv_ref, wout_ref, attn_ref, lse1, lse4, lse16, mixed_ref, x1_ref, scr_a, scr_b, scr_c, scr_d):
        i = pl.program_id(0)
        la, lb, lc = l1[...], _gather_classes(l4, scr_a, 4), _gather_classes(l16, scr_b, 16)
        m_all = jnp.maximum(jnp.maximum(la, lb), lc)
        ea, eb, ec = jnp.exp(la - m_all), jnp.exp(lb - m_all), jnp.exp(lc - m_all)
        den = (ea + eb) + ec
        num = (ea * o1[...] + eb * _gather_classes(o4, scr_c, 4)) + ec * _gather_classes(o16, scr_d, 16)
        attn = num / den
        attn_ref[...] = attn
        _spread(_narrow_heads(m_all + jnp.log(den)), scr_a, (lse1, lse4, lse16), F32)
        xa, _ = _rms(attn)
        u = gc_ref[...] * xi_ref[...]
        uh = jnp.where(i > 0, gch_ref[...] * xih_ref[...], 0.0)
        conv = gb_ref[...] * _causal_conv3(u, uh, wsc_ref)
        xc, _ = _rms(conv)
        mixed = jnp.concatenate([xa * ga_ref[...], xc * gcv_ref[...]], axis=1).astype(BF16)
        mixed_ref[...] = mixed
        x1_ref[...] = x_ref[...] + _dot(mixed, wout_ref[...])

    row = lambda n: pl.BlockSpec((TM, n), lambda i: (i, 0))
    halo = pl.BlockSpec((SUBLANES, 512), lambda i: (jnp.maximum(i * tb - 1, 0), 0))
    cs = _class_specs(512)
    flat = [a for br in branches for a in br]
    res = pl.pallas_call(
        body, name="mix_out", grid=(s // TM,),
        out_shape=[jax.ShapeDtypeStruct((s, 512), F32)] + _class_shapes(s, LANES, F32)
        + [jax.ShapeDtypeStruct((s, D_MODEL), BF16), jax.ShapeDtypeStruct((s, D_MODEL), F32)],
        in_specs=[cs[0], cs[0], cs[1], cs[1], cs[2], cs[2], row(512), row(512), row(512), halo, halo,
                  row(D_MODEL), _full(w_sc.shape), _full(g_a.shape), _full(g_c.shape), _full(w_out.shape)],
        out_specs=[row(512)] + _class_specs(LANES) + [row(D_MODEL), row(D_MODEL)],
        scratch_shapes=[pltpu.VMEM((512 // LANES, TM, LANES), F32)] * 4,
        compiler_params=_cparams(1),
    )(*flat, gb, gc, xi, gc, xi, x, w_sc, g_a, g_c, w_out)
    return res[0], res[1:4], res[4], res[5]


def _mem_kv(mem, g_mem, w_xk, w_xv):
    def body(mem_ref, g_ref, wk_ref, wv_ref, mn_ref, k_ref, v_ref):
        xh, _ = _rms(mem_ref[...])
        mn = (xh * g_ref[...]).astype(BF16)
        mn_ref[...] = mn
        k_ref[...] = _dot(mn, wk_ref[...]).astype(BF16)
        v_ref[...] = _dot(mn, wv_ref[...]).astype(BF16)

    vm = pl.BlockSpec(memory_space=pltpu.VMEM)
    return pl.pallas_call(
        body, name="mem_kv",
        out_shape=[jax.ShapeDtypeStruct(mem.shape, BF16)] * 3,
        in_specs=[vm] * 4, out_specs=[vm] * 3,
        compiler_params=pltpu.CompilerParams(vmem_limit_bytes=VMEM_LIMIT),
    )(mem, g_mem, w_xk, w_xv)


def _xattn_fwd(x1, g, w_xq, k, v, w_xo, dep):
    s = x1.shape[0]

    def body(x1_ref, g_ref, wq_ref, k_ref, v_ref, wo_ref, dep_ref, h2_ref, q_ref, o_ref, x2_ref):
        x1v = x1_ref[...]
        xh, _ = _rms(x1v)
        h2 = (xh * g_ref[...]).astype(BF16)
        h2_ref[...] = h2
        qb = _dot(h2, wq_ref[...]).astype(BF16)
        q_ref[...] = qb
        outs = []
        for h in range(N_MEM_HEADS):
            sl = slice(h * MEM_HEAD_DIM, (h + 1) * MEM_HEAD_DIM)
            lg = _dot_nt(qb[:, sl], k_ref[:, sl]) * (MEM_HEAD_DIM ** -0.5)
            p = jnp.exp(lg - jnp.max(lg, axis=-1, keepdims=True))
            p = p / jnp.sum(p, axis=-1, keepdims=True)
            outs.append(_dot(p.astype(BF16), v_ref[:, sl]))
        o = jnp.concatenate(outs, axis=1).astype(BF16)
        o_ref[...] = o
        x2_ref[...] = x1v + _dot(o, wo_ref[...])

    row = pl.BlockSpec((TM_XATTN, D_MODEL), lambda i: (i, 0))
    return pl.pallas_call(
        body, name="xattn_fwd", grid=(s // TM_XATTN,),
        out_shape=[jax.ShapeDtypeStruct((s, D_MODEL), BF16)] * 3 + [jax.ShapeDtypeStruct((s, D_MODEL), F32)],
        in_specs=[row, _full(g.shape), _full(w_xq.shape), _full(k.shape), _full(v.shape), _full(w_xo.shape), ANY_SPEC],
        out_specs=[row] * 4,
        compiler_params=_cparams(1),
    )(x1, g, w_xq, k, v, w_xo, dep)


def _ffn_conv(h_ext, wup_ref, wfc_ref, bfc_ref, j):
    u = _dot_nt(h_ext, wup_ref[j])
    w = wfc_ref[j]
    c = ((pltpu.roll(u, 2, 0) * w[0:1, :] + pltpu.roll(u, 1, 0) * w[1:2, :]) + u * w[2:3, :]) + bfc_ref[j]
    return c[HALO:], u[HALO:]


def _ffn_fwd(x2, g, w_up_g, w_fc, b_fc, w_down_g, g_final, target):
    s = x2.shape[0]
    tb = TM_FFN // HALO
    n_ch, wid = w_up_g.shape[:2]
    half = n_ch // 2

    def body(x_ref, xp_ref, g_ref, wup_ref, wfc_ref, bfc_ref, wd_ref, gf_ref, t_ref, h_ref, u_ref, c_ref, act_ref,
             dx3_ref, loss_ref, dgf_ref):
        i = pl.program_id(0)

        @pl.when(i == 0)
        def _():
            loss_ref[...] = jnp.zeros_like(loss_ref)
            dgf_ref[...] = jnp.zeros_like(dgf_ref)

        x2v = x_ref[...]
        gv = g_ref[...]
        h = (_rms(x2v)[0] * gv).astype(BF16)
        h_ref[...] = h
        hp = jnp.where(i > 0, _rms(xp_ref[...])[0] * gv, 0.0).astype(BF16)
        h_ext = jnp.concatenate([hp, h], axis=0)
        down = jnp.zeros((TM_FFN, D_MODEL), F32)
        for j in range(half):
            cg, ug = _ffn_conv(h_ext, wup_ref, wfc_ref, bfc_ref, j)
            cv, uv = _ffn_conv(h_ext, wup_ref, wfc_ref, bfc_ref, j + half)
            c_ref[j] = cg
            c_ref[j + half] = cv
            u_ref[j] = ug.astype(BF16)
            u_ref[j + half] = uv.astype(BF16)
            a = ((cg * _sigmoid(cg)) * cv).astype(BF16)
            act_ref[j] = a
            down = down + _dot(a, wd_ref[j])
        x3 = x2v + down
        xh, r = _rms(x3)
        gf = gf_ref[...]
        e = xh * gf - t_ref[...]
        loss_ref[...] += 0.5 * jnp.sum(jnp.sum(e * e, axis=1, keepdims=True), axis=0, keepdims=True) / D_MODEL
        dy = e * (1.0 / D_MODEL)
        dgf_ref[0:1, :] += jnp.sum(dy * xh, axis=0, keepdims=True)
        dx3_ref[...] = _rms_bwd(xh, r, gf, dy)

    row = pl.BlockSpec((TM_FFN, D_MODEL), lambda i: (i, 0))
    prev = pl.BlockSpec((HALO, D_MODEL), lambda i: (jnp.maximum(i * tb - 1, 0), 0))
    return pl.pallas_call(
        body, name="ffn_fwd", grid=(s // TM_FFN,),
        out_shape=[jax.ShapeDtypeStruct((s, D_MODEL), BF16), jax.ShapeDtypeStruct((n_ch, s, wid), BF16),
                   jax.ShapeDtypeStruct((n_ch, s, wid), F32), jax.ShapeDtypeStruct((half, s, wid), BF16),
                   jax.ShapeDtypeStruct((s, D_MODEL), F32), jax.ShapeDtypeStruct((SUBLANES, 128), F32),
                   jax.ShapeDtypeStruct((SUBLANES, D_MODEL), F32)],
        in_specs=[row, prev, _full(g.shape), _resident(w_up_g.shape), _full(w_fc.shape), _full(b_fc.shape),
                  _resident(w_down_g.shape), _full(g_final.shape), row],
        out_specs=[row, pl.BlockSpec((n_ch, TM_FFN, wid), lambda i: (0, i, 0)),
                   pl.BlockSpec((n_ch, TM_FFN, wid), lambda i: (0, i, 0)),
                   pl.BlockSpec((half, TM_FFN, wid), lambda i: (0, i, 0)), row,
                   _full((SUBLANES, 128)), _full((SUBLANES, D_MODEL))],
        compiler_params=_cparams(1),
    )(x2, x2, g, w_up_g, w_fc, b_fc, w_down_g, g_final, target)


def _ffn_bwd(dx3, up, conv, x2, g, w_up_g, w_fc, w_down_g):
    s = x2.shape[0]
    tb = TM_FFN // HALO
    last = s // HALO - 1
    n_tiles = s // TM_FFN
    n_ch, wid = w_up_g.shape[:2]
    half = n_ch // 2
    n_ext = TM_FFN + HALO

    def body(dx_ref, dxn_ref, u_ref, c_ref, cn_ref, x2_ref, g_ref, wup_ref, wfc_ref, wd_ref,
             dup_ref, dx2_ref, dg_ref, dwfc_ref, dbfc_ref):
        i = pl.program_id(0)

        @pl.when(i == 0)
        def _():
            dg_ref[...] = jnp.zeros_like(dg_ref)
            dwfc_ref[...] = jnp.zeros_like(dwfc_ref)
            dbfc_ref[...] = jnp.zeros_like(dbfc_ref)

        dxv = dx_ref[...]
        dxn = jnp.where(i < n_tiles - 1, dxn_ref[...], 0.0)
        dx_ext = jnp.concatenate([dxv, dxn], axis=0).astype(BF16)
        dh = jnp.zeros((TM_FFN, D_MODEL), F32)
        for j in range(half):
            cg = jnp.concatenate([c_ref[j], cn_ref[j]], axis=0)
            cv = jnp.concatenate([c_ref[j + half], cn_ref[j + half]], axis=0)
            dact = _dot_nt(dx_ext, wd_ref[j])
            sg = _sigmoid(cg)
            silu = cg * sg
            parts = ((j + half, dact * silu), (j, (dact * cv) * (sg + silu * (1.0 - sg))))
            for jj, dc in parts:
                u = u_ref[jj].astype(F32)
                dc0, dc1, dc2 = dc[:TM_FFN], pltpu.roll(dc, n_ext - 1, 0)[:TM_FFN], pltpu.roll(dc, n_ext - 2, 0)[:TM_FFN]
                dbfc_ref[jj:jj + 1, :] += jnp.sum(dc0, axis=0, keepdims=True)
                dwfc_ref[0, jj:jj + 1, :] += jnp.sum(dc2 * u, axis=0, keepdims=True)
                dwfc_ref[1, jj:jj + 1, :] += jnp.sum(dc1 * u, axis=0, keepdims=True)
                dwfc_ref[2, jj:jj + 1, :] += jnp.sum(dc0 * u, axis=0, keepdims=True)
                w = wfc_ref[jj]
                du = ((dc0 * w[2:3, :] + dc1 * w[1:2, :]) + dc2 * w[0:1, :]).astype(BF16)
                dup_ref[jj] = du
                dh = dh + _dot(du, wup_ref[jj])
        xh, r = _rms(x2_ref[...])
        dg_ref[0:1, :] += jnp.sum(dh * xh, axis=0, keepdims=True)
        dx2_ref[...] = dxv + _rms_bwd(xh, r, g_ref[...], dh)

    row = pl.BlockSpec((TM_FFN, D_MODEL), lambda i: (i, 0))
    nxt = pl.BlockSpec((HALO, D_MODEL), lambda i: (jnp.minimum((i + 1) * tb, last), 0))
    cur_c = pl.BlockSpec((n_ch, TM_FFN, wid), lambda i: (0, i, 0))
    nxt_c = pl.BlockSpec((n_ch, HALO, wid), lambda i: (0, jnp.minimum((i + 1) * tb, last), 0))
    return pl.pallas_call(
        body, name="ffn_bwd", grid=(n_tiles,),
        out_shape=[jax.ShapeDtypeStruct((n_ch, s, wid), BF16), jax.ShapeDtypeStruct((s, D_MODEL), F32),
                   jax.ShapeDtypeStruct((SUBLANES, D_MODEL), F32), jax.ShapeDtypeStruct((3, n_ch, wid), F32),
                   jax.ShapeDtypeStruct((n_ch, wid), F32)],
        in_specs=[row, nxt, cur_c, cur_c, nxt_c, row, _full(g.shape), _resident(w_up_g.shape), _full(w_fc.shape),
                  _resident(w_down_g.shape)],
        out_specs=[cur_c, row, _full((SUBLANES, D_MODEL)), _full((3, n_ch, wid)), _full((n_ch, wid))],
        compiler_params=_cparams(1),
    )(dx3, dx3, up, conv, conv, x2, g, w_up_g, w_fc, w_down_g)


def _xattn_bwd(dx2, o, q, k, v, w_xo, w_xq, x1, g, dep):
    s = x1.shape[0]

    def body(dx2_ref, o_ref, q_ref, k_ref, v_ref, wo_ref, wq_ref, x1_ref, g_ref, dep_ref, dq_ref, dx1_ref, dk_ref,
             dv_ref, dg_ref):
        @pl.when(pl.program_id(0) == 0)
        def _():
            dk_ref[...] = jnp.zeros_like(dk_ref)
            dv_ref[...] = jnp.zeros_like(dv_ref)
            dg_ref[...] = jnp.zeros_like(dg_ref)

        dx2v = dx2_ref[...]
        do = _dot_nt(dx2v.astype(BF16), wo_ref[...])
        dqs = []
        for h in range(N_MEM_HEADS):
            sl = slice(h * MEM_HEAD_DIM, (h + 1) * MEM_HEAD_DIM)
            qh, kh, vh = q_ref[:, sl], k_ref[:, sl], v_ref[:, sl]
            lg = _dot_nt(qh, kh) * (MEM_HEAD_DIM ** -0.5)
            p = jnp.exp(lg - jnp.max(lg, axis=-1, keepdims=True))
            p = p / jnp.sum(p, axis=-1, keepdims=True)
            doh = do[:, sl].astype(BF16)
            dp = _dot_nt(doh, vh)
            ds = (p * (dp - jnp.sum(p * dp, axis=-1, keepdims=True)) * (MEM_HEAD_DIM ** -0.5)).astype(BF16)
            dqs.append(_dot(ds, kh))
            dk_ref[:, sl] += _dot_tn(ds, qh)
            dv_ref[:, sl] += _dot_tn(p.astype(BF16), doh)
        dq = jnp.concatenate(dqs, axis=1).astype(BF16)
        dq_ref[...] = dq
        dh2 = _dot_nt(dq, wq_ref[...])
        xh, r = _rms(x1_ref[...])
        dg_ref[0:1, :] += jnp.sum(dh2 * xh, axis=0, keepdims=True)
        dx1_ref[...] = dx2v + _rms_bwd(xh, r, g_ref[...], dh2)

    row = pl.BlockSpec((TM_XATTN, D_MODEL), lambda i: (i, 0))
    return pl.pallas_call(
        body, name="xattn_bwd", grid=(s // TM_XATTN,),
        out_shape=[jax.ShapeDtypeStruct((s, D_MODEL), BF16), jax.ShapeDtypeStruct((s, D_MODEL), F32),
                   jax.ShapeDtypeStruct(k.shape, F32), jax.ShapeDtypeStruct(k.shape, F32),
                   jax.ShapeDtypeStruct((SUBLANES, D_MODEL), F32)],
        in_specs=[row, row, row, _full(k.shape), _full(v.shape), _full(w_xo.shape), _full(w_xq.shape), row,
                  _full(g.shape), ANY_SPEC],
        out_specs=[row, row, _full(k.shape), _full(k.shape), _full((SUBLANES, D_MODEL))],
        compiler_params=_cparams(1),
    )(dx2, o, q, k, v, w_xo, w_xq, x1, g, dep)


def _mem_kv_bwd(dk, dv, mem_n, mem, w_xk, w_xv):
    def body(dk_ref, dv_ref, mn_ref, mem_ref, wk_ref, wv_ref, dwk_ref, dwv_ref, dg_ref):
        dkb, dvb = dk_ref[...].astype(BF16), dv_ref[...].astype(BF16)
        mn = mn_ref[...]
        dwk_ref[...] = _dot_tn(mn, dkb).astype(BF16)
        dwv_ref[...] = _dot_tn(mn, dvb).astype(BF16)
        dmn = _dot_nt(dkb, wk_ref[...]) + _dot_nt(dvb, wv_ref[...])
        xh, _ = _rms(mem_ref[...])
        dg_ref[...] = jnp.zeros_like(dg_ref)
        dg_ref[0:1, :] = jnp.sum(dmn * xh, axis=0, keepdims=True)

    vm = pl.BlockSpec(memory_space=pltpu.VMEM)
    return pl.pallas_call(
        body, name="mem_kv_bwd",
        out_shape=[jax.ShapeDtypeStruct(w_xk.shape, BF16), jax.ShapeDtypeStruct(w_xv.shape, BF16),
                   jax.ShapeDtypeStruct((SUBLANES, D_MODEL), F32)],
        in_specs=[vm] * 6, out_specs=[vm] * 3,
        compiler_params=pltpu.CompilerParams(vmem_limit_bytes=VMEM_LIMIT),
    )(dk, dv, mem_n, mem, w_xk, w_xv)


def _mix_out_bwd(dx1, w_out, attn, gb, gc, xi, w_sc, g_a, g_c, dep):
    s = dx1.shape[0]
    tb = TM // SUBLANES

    def body(dx1_ref, wout_ref, attn_ref, gb_ref, gc_ref, xi_ref, gch_ref, xih_ref, wsc_ref, ga_ref, gcv_ref, dep_ref,
             da1, da4, da16, dd1, dd4, dd16, dgb_ref, dcv_ref, dga_ref, dgc_ref, dwsc_ref, scr):
        i = pl.program_id(0)

        @pl.when(i == 0)
        def _():
            dga_ref[...] = jnp.zeros_like(dga_ref)
            dgc_ref[...] = jnp.zeros_like(dgc_ref)
            dwsc_ref[...] = jnp.zeros_like(dwsc_ref)

        dmixed = _dot_nt(dx1_ref[...].astype(BF16), wout_ref[...])
        da, dcn = dmixed[:, :ATTN_W], dmixed[:, ATTN_W:]
        attn = attn_ref[...]
        xa, ra = _rms(attn)
        dga_ref[0:1, :] += jnp.sum(da * xa, axis=0, keepdims=True)
        dattn = _rms_bwd(xa, ra, ga_ref[...], da)
        _spread(dattn, scr, (da1, da4, da16), BF16)
        prod = dattn * attn
        dd = jnp.concatenate(
            [jnp.broadcast_to(jnp.sum(prod[:, h * HEAD_DIM:(h + 1) * HEAD_DIM], axis=-1, keepdims=True),
                              (TM, HEAD_DIM)) for h in range(N_HEADS)], axis=1)
        _spread(_narrow_heads(dd), scr, (dd1, dd4, dd16), F32)
        gbv = gb_ref[...]
        u = gc_ref[...] * xi_ref[...]
        uh = jnp.where(i > 0, gch_ref[...] * xih_ref[...], 0.0)
        u2, u1 = _shift_down(u, uh, 2), _shift_down(u, uh, 1)
        cv = (u2 * wsc_ref[0:1, :] + u1 * wsc_ref[1:2, :]) + u * wsc_ref[2:3, :]
        xc, rc = _rms(gbv * cv)
        dgc_ref[0:1, :] += jnp.sum(dcn * xc, axis=0, keepdims=True)
        dconv = _rms_bwd(xc, rc, gcv_ref[...], dcn)
        dgb_ref[...] = (dconv * cv).astype(BF16)
        dcv = dconv * gbv
        dcv_ref[...] = dcv
        dwsc_ref[0:1, :] += jnp.sum(dcv * u2, axis=0, keepdims=True)
        dwsc_ref[1:2, :] += jnp.sum(dcv * u1, axis=0, keepdims=True)
        dwsc_ref[2:3, :] += jnp.sum(dcv * u, axis=0, keepdims=True)

    row = lambda n: pl.BlockSpec((TM, n), lambda i: (i, 0))
    halo = pl.BlockSpec((SUBLANES, 512), lambda i: (jnp.maximum(i * tb - 1, 0), 0))
    acc = _full((SUBLANES, 512))
    res = pl.pallas_call(
        body, name="mix_out_bwd", grid=(s // TM,),
        out_shape=_class_shapes(s, 512, BF16) + _class_shapes(s, LANES, F32)
        + [jax.ShapeDtypeStruct((s, 512), BF16), jax.ShapeDtypeStruct((s, 512), F32)]
        + [jax.ShapeDtypeStruct((SUBLANES, 512), F32)] * 3,
        in_specs=[row(D_MODEL), _full(w_out.shape), row(512), row(512), row(512), row(512), halo, halo,
                  _full(w_sc.shape), _full(g_a.shape), _full(g_c.shape), ANY_SPEC],
        out_specs=_class_specs(512) + _class_specs(LANES) + [row(512)] * 2 + [acc] * 3,
        scratch_shapes=[pltpu.VMEM((512 // LANES, TM, LANES), F32)],
        compiler_params=_cparams(1),
    )(dx1, w_out, attn, gb, gc, xi, gc, xi, w_sc, g_a, g_c, dep)
    return res[0:3], res[3:6], res[6], res[7], res[8], res[9], res[10]


def _swa_bwd(qc, kc, vc, doc, lsec, ddc, bias, dil, dep):
    nsub, nb, ncls = _swa_steps(qc, dil)
    n128 = nsub * nb
    whole = nb == 1

    def body(q_ref, qn_ref, kp_ref, kc_ref, vp_ref, vc_ref, do_ref, don_ref, lse_ref, lsen_ref, dd_ref, ddn_ref,
             b_ref, dep_ref, dq_ref, dk_ref, dv_ref, db_ref, s_scr, dp_scr, sn_scr, dpn_scr, ds_scr, p_scr, dsn_scr,
             pn_scr):
        r, b = pl.program_id(0), pl.program_id(1)

        @pl.when((r == 0) & (b == 0))
        def _():
            db_ref[...] = jnp.zeros_like(db_ref)

        pairs = [slice(a * LANES, (a + 1) * LANES) for a in range(N_HEADS // 2)]
        blk = [slice(t * WIN, (t + 1) * WIN) for t in range(nsub)]
        last = blk[nsub - 1]
        cols = lambda t: slice(WIN, 2 * WIN) if whole and t == 0 else slice(0, 2 * WIN)
        of_head = lambda ref, c, rows, h: ref[c, rows, _head_lane(h):_head_lane(h) + 1]
        no_prev = (b == 0) & (lax.broadcasted_iota(jnp.int32, (WIN, 2 * WIN), 1) < WIN)

        def keys(prev_ref, cur_ref, c, t, sl):
            if whole and t == 0:
                return cur_ref[c, blk[0], sl]
            if t == 0:
                return jnp.concatenate([prev_ref[c, :, sl], cur_ref[c, blk[0], sl]], axis=0)
            return cur_ref[c, (t - 1) * WIN:(t + 1) * WIN, sl]

        for a, sl in enumerate(pairs):
            for c, t in [(c, t) for c in range(ncls) for t in range(nsub)]:
                k2, v2 = keys(kp_ref, kc_ref, c, t, sl), keys(vp_ref, vc_ref, c, t, sl)
                q_eo = _pair_split(q_ref[c, blk[t], sl])
                do_eo = _pair_split(do_ref[c, blk[t], sl].astype(BF16))
                for e in range(2):
                    s_scr[c * nsub + t, 2 * a + e, :, cols(t)] = _dot_nt(q_eo[e], k2)
                    dp_scr[c * nsub + t, 2 * a + e, :, cols(t)] = _dot_nt(do_eo[e], v2)
            if not whole:
                qn_eo = _pair_split(qn_ref[0, :, sl])
                don_eo = _pair_split(don_ref[0, :, sl].astype(BF16))
                for e in range(2):
                    sn_scr[2 * a + e] = _dot_nt(qn_eo[e], kc_ref[0, last, sl])
                    dpn_scr[2 * a + e] = _dot_nt(don_eo[e], vc_ref[0, last, sl])
        for c, t, h in [(c, t, h) for c in range(ncls) for t in range(nsub) for h in range(N_HEADS)]:
            i, cl = c * nsub + t, cols(t)
            lg = s_scr[i, h, :, cl] + b_ref[h, :, cl]
            if t == 0 and not whole:
                lg = jnp.where(no_prev, -jnp.inf, lg)
            p = jnp.exp(lg - of_head(lse_ref, c, blk[t], h))
            ds = p * (dp_scr[i, h, :, cl] - of_head(dd_ref, c, blk[t], h))
            db_ref[h, :, cl] += ds
            ds_scr[i, h, :, cl] = ds.astype(BF16)
            p_scr[i, h, :, cl] = p.astype(BF16)
        if not whole:
            every = slice(0, WIN)
            for h in range(N_HEADS):
                lgn = jnp.where(b + 1 < nb, sn_scr[h] + b_ref[h, :, :WIN], -jnp.inf)
                pn = jnp.exp(lgn - of_head(lsen_ref, 0, every, h))
                dsn_scr[h] = (pn * (dpn_scr[h] - of_head(ddn_ref, 0, every, h))).astype(BF16)
                pn_scr[h] = pn.astype(BF16)
        for a, sl in enumerate(pairs):
            for c in range(ncls):
                q_eo = [_pair_split(q_ref[c, blk[t], sl]) for t in range(nsub)]
                do_eo = [_pair_split(do_ref[c, blk[t], sl].astype(BF16)) for t in range(nsub)]
                if not whole:
                    q_eo.append(_pair_split(qn_ref[0, :, sl]))
                    do_eo.append(_pair_split(don_ref[0, :, sl].astype(BF16)))
                for t in range(nsub):
                    i = c * nsub + t
                    k_eo = _pair_split(keys(kp_ref, kc_ref, c, t, sl))
                    dq, dk, dv = None, None, None
                    for e in range(2):
                        h = 2 * a + e
                        terms = [_dot(ds_scr[i, h, :, cols(t)], k_eo[e]),
                                 _dot_tn(ds_scr[i, h, :, WIN:], q_eo[t][e]),
                                 _dot_tn(p_scr[i, h, :, WIN:], do_eo[t][e])]
                        if t + 1 < nsub or not whole:
                            ds_next = ds_scr[i + 1, h, :, :WIN] if t + 1 < nsub else dsn_scr[h]
                            p_next = p_scr[i + 1, h, :, :WIN] if t + 1 < nsub else pn_scr[h]
                            terms[1] += _dot_tn(ds_next, q_eo[t + 1][e])
                            terms[2] += _dot_tn(p_next, do_eo[t + 1][e])
                        dq, dk, dv = terms if e == 0 else (dq + terms[0], dk + terms[1], dv + terms[2])
                    dq_ref[c, blk[t], sl] = dq.astype(BF16)
                    dk_ref[c, blk[t], sl] = dk.astype(BF16)
                    dv_ref[c, blk[t], sl] = dv.astype(BF16)

    cur = pl.BlockSpec((ncls, nsub * WIN, 512), lambda r, b: (r, b, 0))
    prev = pl.BlockSpec((ncls, WIN, 512), lambda r, b: (r, jnp.maximum(nsub * b - 1, 0), 0))
    nxt = pl.BlockSpec((ncls, WIN, 512), lambda r, b: (r, jnp.minimum(nsub * b + nsub, n128 - 1), 0))
    cur_h = pl.BlockSpec((ncls, nsub * WIN, LANES), cur.index_map)
    nxt_h = pl.BlockSpec((ncls, WIN, LANES), nxt.index_map)
    wide, narrow = (ncls * nsub, N_HEADS, WIN, 2 * WIN), (N_HEADS, WIN, WIN)
    return pl.pallas_call(
        body, name=f"swa_bwd_d{dil}", grid=(dil // ncls, nb),
        out_shape=[jax.ShapeDtypeStruct(qc.shape, BF16)] * 3 + [jax.ShapeDtypeStruct(bias.shape, F32)],
        in_specs=[cur, nxt, prev, cur, prev, cur, cur, nxt, cur_h, nxt_h, cur_h, nxt_h, _full(bias.shape),
                  ANY_SPEC],
        out_specs=[cur] * 3 + [_full(bias.shape)],
        scratch_shapes=[pltpu.VMEM(wide, F32), pltpu.VMEM(wide, F32), pltpu.VMEM(narrow, F32),
                        pltpu.VMEM(narrow, F32), pltpu.VMEM(wide, BF16), pltpu.VMEM(wide, BF16),
                        pltpu.VMEM(narrow, BF16), pltpu.VMEM(narrow, BF16)],
        compiler_params=_cparams(2),
    )(qc, qc, kc, kc, vc, vc, doc, doc, lsec, lsec, ddc, ddc, bias, dep)


def _in_proj_bwd(dqs, dks, dvs, dgb, dcv, gc, xi, w_sc, w_in_g, x, g_mix, dx1):
    s = x.shape[0]
    tb = TM // SUBLANES
    last = s // SUBLANES - 1
    n_tiles = s // TM

    def body(dq1, dq4, dq16, dk1, dk4, dk16, dv1, dv4, dv16, dgb_ref, dcv_ref, dcvn_ref, gc_ref, xi_ref, wsc_ref,
             w_hbm, x_ref, g_ref, dx1_ref, dproj_ref, gx_ref, dg_ref, scr_a, scr_b, w_scr, w_sems):
        i = pl.program_id(0)
        _load_w_in_pairs(w_hbm, w_scr, w_sems)

        @pl.when(i == 0)
        def _():
            dg_ref[...] = jnp.zeros_like(dg_ref)

        d0 = dcv_ref[...]
        dn = jnp.where(i < n_tiles - 1, dcvn_ref[...], 0.0)
        du = (d0 * wsc_ref[2:3, :] + _shift_up(d0, dn, 1) * wsc_ref[1:2, :]) + _shift_up(d0, dn, 2) * wsc_ref[0:1, :]
        merge = lambda a, b4, b16: ((a[...].astype(F32) + _gather_classes(b4, scr_a, 4))
                                    + _gather_classes(b16, scr_b, 16))
        dq = merge(dq1, dq4, dq16) * (HEAD_DIM ** -0.5)
        dk = merge(dk1, dk4, dk16)
        dv = merge(dv1, dv4, dv16)
        dproj = jnp.concatenate([dq, dk, dv, dgb_ref[...].astype(F32), du * xi_ref[...], du * gc_ref[...]],
                                axis=1).astype(BF16)
        dproj_ref[...] = dproj
        dh = jnp.zeros((TM, D_MODEL), F32)
        for j in range(N_DEV // 2):
            dh = dh + _dot_nt(dproj[:, 2 * j * IN_CHUNK:2 * (j + 1) * IN_CHUNK], w_scr[j])
        xh, r = _rms(x_ref[...])
        dg_ref[0:1, :] += jnp.sum(dh * xh, axis=0, keepdims=True)
        gx_ref[...] = dx1_ref[...] + _rms_bwd(xh, r, g_ref[...], dh)

    row = lambda n: pl.BlockSpec((TM, n), lambda i: (i, 0))
    nxt = pl.BlockSpec((SUBLANES, 512), lambda i: (jnp.minimum((i + 1) * tb, last), 0))
    return pl.pallas_call(
        body, name="in_proj_bwd", grid=(n_tiles,),
        out_shape=[jax.ShapeDtypeStruct((s, IN_COLS), BF16), jax.ShapeDtypeStruct((s, D_MODEL), F32),
                   jax.ShapeDtypeStruct((SUBLANES, D_MODEL), F32)],
        in_specs=_class_specs(512) * 3 + [row(512), row(512), nxt, row(512), row(512), _full(w_sc.shape),
                                          ANY_SPEC, row(D_MODEL), _full(g_mix.shape), row(D_MODEL)],
        out_specs=[row(IN_COLS), row(D_MODEL), _full((SUBLANES, D_MODEL))],
        scratch_shapes=[pltpu.VMEM((512 // LANES, TM, LANES), F32)] * 2 + W_IN_PAIRS,
        compiler_params=_cparams(1),
    )(*dqs, *dks, *dvs, dgb, dcv, dcv, gc, xi, w_sc, w_in_g, x, g_mix, dx1)


def _dw(a, b, dep, name, a_chunked=False, b_chunked=False, n_chunks=1, chunk_cols=None, per_step=1):
    single = not (a_chunked or b_chunked or chunk_cols)
    wide = a_chunked and a.shape[2] > D_MODEL
    ts = TS_DW // 4 if single else TS_DW // 2 if wide else TS_DW
    if a_chunked:
        nj, s, kk = a.shape
        nn = b.shape[1]
        a_spec = pl.BlockSpec((1, ts, kk), lambda j, t: (j, t, 0))
        b_spec = pl.BlockSpec((ts, nn), lambda j, t: (t, 0))
    elif b_chunked:
        nj, s, nn = b.shape
        kk = a.shape[1]
        a_spec = pl.BlockSpec((ts, kk), lambda j, t: (t, 0))
        b_spec = pl.BlockSpec((1, ts, nn), lambda j, t: (j, t, 0))
    else:
        s, kk = a.shape
        nj, nn = (n_chunks // per_step, chunk_cols * per_step) if chunk_cols else (1, b.shape[1])
        a_spec = pl.BlockSpec((ts, kk), lambda j, t: (t, 0))
        b_spec = pl.BlockSpec((ts, nn), lambda j, t: (t, j))
    n_steps = s // ts

    def body(a_ref, b_ref, dep_ref, o_ref, acc):
        t = pl.program_id(1)

        @pl.when(t == 0)
        def _():
            acc[...] = jnp.zeros_like(acc)

        av = (a_ref[0] if a_chunked else a_ref[...]).astype(BF16)
        bv = (b_ref[0] if b_chunked else b_ref[...]).astype(BF16)
        acc[...] += _dot_tn(av, bv)

        @pl.when(t == n_steps - 1)
        def _():
            for q in range(per_step):
                o_ref[q] = acc[:, q * nn // per_step:(q + 1) * nn // per_step].astype(BF16)

    return pl.pallas_call(
        body, name=name, grid=(nj, n_steps),
        out_shape=jax.ShapeDtypeStruct((nj * per_step, kk, nn // per_step), BF16),
        in_specs=[a_spec, b_spec, ANY_SPEC],
        out_specs=pl.BlockSpec((per_step, kk, nn // per_step), lambda j, t: (j, 0, 0)),
        scratch_shapes=[pltpu.VMEM((kk, nn), F32)],
        compiler_params=_cparams(2),
    )(a, b, dep)


def _adamw_math(w, g, m, v):
    m2 = ADAM_B1 * m + (1.0 - ADAM_B1) * g
    v2 = ADAM_B2 * v + (1.0 - ADAM_B2) * (g * g)
    m_hat = m2 / (1.0 - ADAM_B1 ** ADAM_STEP)
    v_hat = v2 / (1.0 - ADAM_B2 ** ADAM_STEP)
    delta = -ADAM_LR * (m_hat / (jnp.sqrt(v_hat) + ADAM_EPS) + ADAM_WD * w)
    return delta, m2, v2


def _sum_parts(me, own, p_ref):
    g = None
    for i in range(N_DEV):
        part = jnp.where(me == i, own.astype(F32), p_ref[i].astype(F32))
        g = part if g is None else g + part
    return g


def _adamw_big(name, w, sent, parts, m, v, me_arr):
    rr, cc = w.shape
    tr = rr // 4 if rr >= 512 else rr

    def body(me_ref, w_ref, own_ref, p_ref, m_ref, v_ref, g_ref, d_ref, nm_ref, nv_ref):
        g = own_ref[0].astype(F32)
        for k in range(1, N_DEV):
            g = g + p_ref[(me_ref[0] + k) % N_DEV].astype(F32)
        g_ref[...] = g
        d_ref[...], nm_ref[...], nv_ref[...] = _adamw_math(w_ref[...], g, m_ref[...], v_ref[...])

    row = pl.BlockSpec((tr, cc), lambda i, me: (i, 0))
    return pl.pallas_call(
        body, name=name,
        grid_spec=pltpu.PrefetchScalarGridSpec(
            num_scalar_prefetch=1, grid=(rr // tr,),
            in_specs=[row, pl.BlockSpec((1, tr, cc), lambda i, me: (me[0], i, 0)),
                      pl.BlockSpec((N_DEV, tr, cc), lambda i, me: (0, i, 0)), row, row],
            out_specs=[row] * 4),
        out_shape=[jax.ShapeDtypeStruct((rr, cc), F32)] * 4,
        compiler_params=_cparams(1),
    )(me_arr, w, sent, parts, m, v)


def _small_slices():
    return [
        (slice(ROW_RELB, ROW_RELB + 8), slice(0, N_BUCKETS)),
        (slice(ROW_GMIX, ROW_GMIX + 1), slice(0, D_MODEL)),
        (slice(ROW_GAC, ROW_GAC + 1), slice(0, ATTN_W)),
        (slice(ROW_GAC, ROW_GAC + 1), slice(ATTN_W, D_MODEL)),
        (slice(ROW_GXATTN, ROW_GXATTN + 1), slice(0, D_MODEL)),
        (slice(ROW_GMEM, ROW_GMEM + 1), slice(0, D_MODEL)),
        (slice(ROW_GFFN, ROW_GFFN + 1), slice(0, D_MODEL)),
        (slice(ROW_BFC, ROW_BFC + 8), slice(0, UP_CHUNK)),
        (slice(ROW_GFINAL, ROW_GFINAL + 1), slice(0, D_MODEL)),
    ]


def _adamw_small(own, parts, wmv, me_arr):
    slices = _small_slices()
    n = len(slices)

    def body(*refs):
        me_ref, own_ref, p_ref = refs[:3]
        ins = refs[3:3 + 3 * n]
        g_ref = refs[3 + 3 * n]
        outs = refs[4 + 3 * n:]
        g = _sum_parts(me_ref[0], own_ref[...], p_ref)
        g_ref[...] = g
        for a, (rs, ls) in enumerate(slices):
            ga = g[rs, ls]
            outs[4 * a][...] = ga
            outs[4 * a + 1][...], outs[4 * a + 2][...], outs[4 * a + 3][...] = _adamw_math(
                ins[3 * a][...], ga, ins[3 * a + 1][...], ins[3 * a + 2][...])

    vm = pl.BlockSpec(memory_space=pltpu.VMEM)
    flat = [t for trip in wmv for t in trip]
    out_shape = [jax.ShapeDtypeStruct((SMALL_ROWS, D_MODEL), F32)]
    for w, _, _ in wmv:
        out_shape += [jax.ShapeDtypeStruct(w.shape, F32)] * 4
    res = pl.pallas_call(
        body, name="adamw_small", out_shape=out_shape,
        in_specs=[SMEM_SPEC] + [vm] * (2 + 3 * n), out_specs=[vm] * len(out_shape),
    )(me_arr, own, parts, *flat)
    return res[0], [res[1 + 4 * a:5 + 4 * a] for a in range(n)]


def _adamw_shards(items):
    n = len(items)

    def body(*refs):
        for a in range(n):
            w_ref, g_ref, m_ref, v_ref = refs[4 * a:4 * a + 4]
            d_ref, nm_ref, nv_ref = refs[4 * n + 3 * a:4 * n + 3 * a + 3]
            d_ref[...], nm_ref[...], nv_ref[...] = _adamw_math(w_ref[...], g_ref[...], m_ref[...], v_ref[...])

    vm = pl.BlockSpec(memory_space=pltpu.VMEM)
    out_shape = []
    for w, _, _, _ in items:
        out_shape += [jax.ShapeDtypeStruct(w.shape, F32)] * 3
    res = pl.pallas_call(
        body, name="adamw_shards", out_shape=out_shape, in_specs=[vm] * (4 * n), out_specs=[vm] * (3 * n),
    )(*[t for it in items for t in it])
    return [res[3 * a:3 * a + 3] for a in range(n)]


def _mesh_pos():
    return lax.axis_index("x"), lax.axis_index("y"), lax.axis_index("c")


def _dev_index(p):
    return 4 * p[0] + 2 * p[1] + p[2]


def _all_gather(shards):
    n = len(shards)

    def body(*refs):
        ins, outs = refs[:n], refs[n:2 * n]
        send_sems, recv_sems, loc_sems = refs[2 * n:]
        x, y, c = _mesh_pos()
        me, sib = (x, y, c), (x, y, 1 - c)
        chips = [(1 - x, y), (x, 1 - y), (1 - x, 1 - y)]

        def cp(a, k, block, to, src=None):
            dst = outs[a].at[_dev_index(block)]
            return pltpu.make_async_remote_copy(
                src_ref=dst if src is None else src, dst_ref=dst, send_sem=send_sems.at[a, k],
                recv_sem=recv_sems.at[a, k], device_id=to, device_id_type=MESH)

        mine = [pltpu.make_async_copy(ins[a], outs[a].at[_dev_index(me)], loc_sems.at[a]) for a in range(n)]
        for m_ in mine:
            m_.start()
        first = []
        for a in range(n):
            first.append(cp(a, 0, me, sib, src=ins[a]))
            first += [cp(a, 1 + j, me, (*chip, c), src=ins[a]) for j, chip in enumerate(chips)]
        for f in first:
            f.start()
        passed = []
        for a in range(n):
            for j, chip in enumerate(chips):
                cp(a, 1 + j, (*chip, c), me).wait_recv()
                fwd = cp(a, 4 + j, (*chip, c), sib)
                fwd.start()
                passed.append(fwd)
        for a in range(n):
            cp(a, 0, sib, me).wait_recv()
            for j, chip in enumerate(chips):
                cp(a, 4 + j, (*chip, 1 - c), me).wait_recv()
        for f in first + passed:
            f.wait_send()
        for m_ in mine:
            m_.wait()

    hbm = pl.BlockSpec(memory_space=pltpu.HBM)
    return pl.pallas_call(
        body, name="all_gather_weights",
        out_shape=[jax.ShapeDtypeStruct((N_DEV,) + a.shape, a.dtype) for a in shards],
        in_specs=[hbm] * n, out_specs=[hbm] * n,
        scratch_shapes=[pltpu.SemaphoreType.DMA((n, 7)), pltpu.SemaphoreType.DMA((n, 7)),
                        pltpu.SemaphoreType.DMA((n,))],
    )(*shards)


def _peers():
    x, y, c = _mesh_pos()
    return (x, y, c), [((1 - x) if k & 4 else x, (1 - y) if k & 2 else y, (1 - c) if k & 1 else c)
                       for k in range(1, 8)]


def _exchange_copy(src_ref, land_ref, whole, send_sems, recv_sems, a, k, peer, slot):
    src = src_ref if whole else src_ref.at[_dev_index(peer)]
    return pltpu.make_async_remote_copy(
        src_ref=src, dst_ref=land_ref.at[slot], send_sem=send_sems.at[7 * a + k], recv_sem=recv_sems.at[7 * a + k],
        device_id=peer, device_id_type=MESH)


def _exchange_start(name, srcs, whole, dep):
    n = len(srcs)
    lands = [lax.empty(((N_DEV,) + s.shape) if w else s.shape, s.dtype) for s, w in zip(srcs, whole)]

    def body(*refs):
        src_refs, land_refs = refs[:n], refs[n:2 * n]
        send_sems, recv_sems, token = refs[2 * n + 1], refs[2 * n + 2], refs[-1]
        me, peers = _peers()
        for a in range(n):
            for k, peer in enumerate(peers):
                _exchange_copy(src_refs[a], land_refs[a], whole[a], send_sems, recv_sems, a, k, peer,
                               _dev_index(me)).start()
        token[...] = jnp.zeros_like(token)

    res = pl.pallas_call(
        body, name=name,
        out_shape=(pltpu.SemaphoreType.DMA((7 * n,)), pltpu.SemaphoreType.DMA((7 * n,)),
                   *[pltpu.HBM(a.shape, a.dtype) for a in srcs], *[pltpu.HBM(a.shape, a.dtype) for a in lands],
                   jax.ShapeDtypeStruct((SUBLANES, 128), F32)),
        in_specs=[HBM_SPEC] * (2 * n) + [ANY_SPEC],
        out_specs=(SEM_SPEC, SEM_SPEC, *([HBM_SPEC] * (2 * n)), VMEM_SPEC),
        input_output_aliases={i: 2 + i for i in range(2 * n)},
        compiler_params=pltpu.CompilerParams(has_side_effects=DATAFLOW),
    )(*[pltpu.with_memory_space_constraint(a, pltpu.HBM) for a in srcs],
      *[pltpu.with_memory_space_constraint(a, pltpu.HBM) for a in lands], dep)
    return res[0], res[1], list(res[2:2 + n]), list(res[2 + n:2 + 2 * n]), res[-1]


def _exchange_wait(name, started, whole, after, which=None):
    send_sems, recv_sems, srcs, lands, _ = started
    which = list(range(len(srcs))) if which is None else which
    srcs, lands = [srcs[a] for a in which], [lands[a] for a in which]
    n = len(srcs)

    def body(*refs):
        src_refs, land_refs = refs[:n], refs[n:2 * n]
        send_sems, recv_sems = refs[2 * n], refs[2 * n + 1]
        _, peers = _peers()
        for i, a in enumerate(which):
            for k, peer in enumerate(peers):
                cp = _exchange_copy(src_refs[i], land_refs[i], whole[a], send_sems, recv_sems, a, k, peer,
                                    _dev_index(peer))
                cp.wait_send()
                cp.wait_recv()

    res = pl.pallas_call(
        body, name=name,
        out_shape=[pltpu.HBM(a.shape, a.dtype) for a in srcs + lands],
        in_specs=[HBM_SPEC] * (2 * n) + [SEM_SPEC, SEM_SPEC, ANY_SPEC],
        out_specs=[HBM_SPEC] * (2 * n),
        input_output_aliases={i: i for i in range(2 * n)},
        compiler_params=pltpu.CompilerParams(has_side_effects=DATAFLOW),
    )(*srcs, *lands, send_sems, recv_sems, after)
    return list(res[:n]), list(res[n:])


def _gather_start(name, shards, dep):
    n = len(shards)
    lands = [lax.empty((N_DEV,) + a.shape, a.dtype) for a in shards]

    def body(*refs):
        src_refs, land_refs = refs[:n], refs[n:2 * n]
        send_sems, recv_sems, token = refs[2 * n + 1], refs[2 * n + 2], refs[-1]
        x, y, c = _mesh_pos()
        peers = [(x, y, 1 - c), (1 - x, y, c), (x, 1 - y, c), (1 - x, 1 - y, c)]
        for a in range(n):
            for k, peer in enumerate(peers):
                pltpu.make_async_remote_copy(
                    src_ref=src_refs[a], dst_ref=land_refs[a].at[_dev_index((x, y, c))], send_sem=send_sems.at[4 * a + k],
                    recv_sem=recv_sems.at[4 * a + k], device_id=peer, device_id_type=MESH).start()
        token[...] = jnp.zeros_like(token)

    res = pl.pallas_call(
        body, name=name,
        out_shape=(pltpu.SemaphoreType.DMA((4 * n,)), pltpu.SemaphoreType.DMA((4 * n,)),
                   *[pltpu.HBM(a.shape, a.dtype) for a in shards], *[pltpu.HBM(a.shape, a.dtype) for a in lands],
                   jax.ShapeDtypeStruct((SUBLANES, 128), F32)),
        in_specs=[HBM_SPEC] * (2 * n) + [ANY_SPEC],
        out_specs=(SEM_SPEC, SEM_SPEC, *([HBM_SPEC] * (2 * n)), VMEM_SPEC),
        input_output_aliases={i: 2 + i for i in range(2 * n)},
        compiler_params=pltpu.CompilerParams(has_side_effects=DATAFLOW),
    )(*[pltpu.with_memory_space_constraint(a, pltpu.HBM) for a in shards],
      *[pltpu.with_memory_space_constraint(a, pltpu.HBM) for a in lands], dep)
    return res[0], res[1], list(res[2:2 + n]), list(res[2 + n:2 + 2 * n]), res[-1]


def _gather_forward(name, send_sems, recv_sems, lands, which, after):
    n = len(which)

    def body(*refs):
        land_refs = refs[:n]
        send_sems, recv_sems = refs[n], refs[n + 1]
        fsend, frecv, token = refs[n + 3], refs[n + 4], refs[-1]
        x, y, c = _mesh_pos()
        chips = [(1 - x, y), (x, 1 - y), (1 - x, 1 - y)]
        for i, a in enumerate(which):
            for j, chip in enumerate(chips):
                block = land_refs[i].at[_dev_index((*chip, c))]
                pltpu.make_async_remote_copy(
                    src_ref=block, dst_ref=block, send_sem=send_sems.at[4 * a + 1 + j], recv_sem=recv_sems.at[4 * a + 1 + j],
                    device_id=(*chip, c), device_id_type=MESH).wait_recv()
                pltpu.make_async_remote_copy(
                    src_ref=block, dst_ref=block, send_sem=fsend.at[3 * i + j], recv_sem=frecv.at[3 * i + j],
                    device_id=(x, y, 1 - c), device_id_type=MESH).start()
        token[...] = jnp.zeros_like(token)

    res = pl.pallas_call(
        body, name=name,
        out_shape=(pltpu.SemaphoreType.DMA((3 * n,)), pltpu.SemaphoreType.DMA((3 * n,)),
                   *[pltpu.HBM(a.shape, a.dtype) for a in lands], jax.ShapeDtypeStruct((SUBLANES, 128), F32)),
        in_specs=[HBM_SPEC] * n + [SEM_SPEC, SEM_SPEC, ANY_SPEC],
        out_specs=(SEM_SPEC, SEM_SPEC, *([HBM_SPEC] * n), VMEM_SPEC),
        input_output_aliases={i: 2 + i for i in range(n)},
        compiler_params=pltpu.CompilerParams(has_side_effects=DATAFLOW),
    )(*lands, send_sems, recv_sems, after)
    return res[0], res[1], list(res[2:2 + n]), res[-1]


def _gather_wait(name, send_sems, recv_sems, fsend, frecv, srcs, lands, which, after):
    n = len(which)

    def body(*refs):
        land_refs = refs[n:2 * n]
        send_sems, recv_sems, fsend, frecv = refs[2 * n:2 * n + 4]
        x, y, c = _mesh_pos()
        sib = (x, y, 1 - c)
        chips = [(1 - x, y), (x, 1 - y), (1 - x, 1 - y)]
        for i, a in enumerate(which):
            def cp(slot, ssem, rsem):
                block = land_refs[i].at[_dev_index(slot)]
                return pltpu.make_async_remote_copy(src_ref=block, dst_ref=block, send_sem=ssem, recv_sem=rsem,
                                                    device_id=sib, device_id_type=MESH)
            cp(sib, send_sems.at[4 * a], recv_sems.at[4 * a]).wait_recv()
            for j, chip in enumerate(chips):
                cp((*chip, 1 - c), fsend.at[3 * i + j], frecv.at[3 * i + j]).wait_recv()
            for k in range(4):
                cp(sib, send_sems.at[4 * a + k], recv_sems.at[4 * a + k]).wait_send()
            for j in range(3):
                cp(sib, fsend.at[3 * i + j], frecv.at[3 * i + j]).wait_send()

    res = pl.pallas_call(
        body, name=name,
        out_shape=[pltpu.HBM(a.shape, a.dtype) for a in srcs + lands],
        in_specs=[HBM_SPEC] * (2 * n) + [SEM_SPEC] * 4 + [ANY_SPEC],
        out_specs=[HBM_SPEC] * (2 * n),
        input_output_aliases={i: i for i in range(2 * n)},
        compiler_params=pltpu.CompilerParams(has_side_effects=DATAFLOW),
    )(*srcs, *lands, send_sems, recv_sems, fsend, frecv, after)
    return list(res[n:])


def _local_step(x, mem, target, rel_bias, g_mix, w_in_g, w_sc, g_a, g_c, g_xattn, g_mem, g_ffn, w_fc, b_fc, g_final,
                dep, forward_weights, late_weights, emit, emit_small):
    s = x.shape[0]
    buckets = _bucket_tables()
    bias = _bias_fwd(rel_bias, buckets)

    h1, qs, ks, vs, gb, gc, xi = _rms_proj(x, g_mix, w_in_g, dep)
    qs, ks, vs = ([a[0][None]] + list(a[1:]) for a in (qs, ks, vs))
    group1, group2 = ["w_out", "w_xq", "w_xk", "w_xv", "w_xo"], ["w_up", "w_down"]
    tok = forward_weights(group1, h1)
    branches = []
    for p, dil in enumerate(DILATIONS):
        o_p, lse_p = _swa_fwd(qs[p], ks[p], vs[p], bias[p], dil, tok)
        branches.append([o_p[0], lse_p[0]] if dil == 1 else [o_p, lse_p])
    lw = late_weights(group1, branches[-1][0])
    w_out, w_xq, w_xk, w_xv, w_xo = (lw[n] for n in group1)
    attn, lses, mixed, x1 = _mix_out(branches, gb, gc, xi, x, w_sc, g_a, g_c, w_out)
    tok = forward_weights(group2, x1)
    mem_n, mk, mv = _mem_kv(mem, g_mem, w_xk, w_xv)
    h2, xq, xo, x2 = _xattn_fwd(x1, g_xattn, w_xq, mk, mv, w_xo, tok)
    lw = late_weights(group2, x2)
    w_up_g = lw["w_up"].reshape(FFN_CHUNKS, FFN_WIDTH, D_MODEL)
    w_down_g = lw["w_down"].reshape(FFN_CHUNKS // 2, FFN_WIDTH, D_MODEL)
    pairs = lambda a: a.reshape(FFN_CHUNKS, 2, a.shape[1], UP_CHUNK).transpose(0, 2, 1, 3).reshape(
        FFN_CHUNKS, a.shape[1], FFN_WIDTH)
    w_fc, b_fc = pairs(w_fc), pairs(b_fc)
    h3, up, conv, act, dx3, loss_acc, dg_final = _ffn_fwd(x2, g_ffn, w_up_g, w_fc, b_fc, w_down_g, g_final, target)

    gw_down = _dw(act, dx3, dep, "dw_down", a_chunked=True).reshape(N_DEV // 2, UP_CHUNK, D_MODEL)
    dup, dx2, dg_ffn, dw_fc, db_fc = _ffn_bwd(dx3, up, conv, x2, g_ffn, w_up_g, w_fc, w_down_g)
    gw_up = _dw(dup, h3, dep, "dw_up", a_chunked=True).reshape(N_DEV, UP_CHUNK, D_MODEL)
    tok = emit(dict(w_down=gw_down, w_up=gw_up))
    dxq, dx1, dmk, dmv, dg_xattn = _xattn_bwd(dx2, xo, xq, mk, mv, w_xo, w_xq, x1, g_xattn, tok)
    gw_xo = _dw(xo, dx2, tok, "dw_xo")[0]
    gw_xq = _dw(h2, dxq, tok, "dw_xq")[0]
    gw_xk, gw_xv, dg_mem = _mem_kv_bwd(dmk, dmv, mem_n, mem, w_xk, w_xv)
    tok = emit(dict(w_xo=gw_xo, w_xq=gw_xq, w_xk=gw_xk, w_xv=gw_xv))
    dattns, dds, dgb, dcv, dg_a, dg_c, dw_sc = _mix_out_bwd(dx1, w_out, attn, gb, gc, xi, w_sc, g_a, g_c, tok)
    first = lambda a: [a[0][None]] + list(a[1:])
    dattns, dds, lses = first(dattns), first(dds), first(lses)
    gw_out = _dw(mixed, dx1, tok, "dw_out")[0]
    tok = emit(dict(w_out=gw_out))
    dqs, dks, dvs, dbias = [], [], [], []
    for p, dil in enumerate(DILATIONS):
        dq_p, dk_p, dv_p, db_p = _swa_bwd(qs[p], ks[p], vs[p], dattns[p], lses[p], dds[p], bias[p], dil, tok)
        dqs.append(dq_p[0] if dil == 1 else dq_p)
        dks.append(dk_p[0] if dil == 1 else dk_p)
        dvs.append(dv_p[0] if dil == 1 else dv_p)
        dbias.append(db_p)
    d_relb = _bias_bwd(jnp.stack(dbias), buckets)
    dproj, grad_x, dg_mix = _in_proj_bwd(dqs, dks, dvs, dgb, dcv, gc, xi, w_sc, w_in_g, x, g_mix, dx1)
    pad = lambda a: jnp.pad(a, ((0, 0), (0, D_MODEL - a.shape[1])))
    small = jnp.concatenate([
        d_relb, dg_mix, dg_xattn, dg_mem, dg_ffn, dg_final, jnp.concatenate([dg_a, dg_c], axis=1),
        pad(dw_sc), pad(db_fc.reshape(N_DEV, UP_CHUNK)), pad(dw_fc.reshape(3 * N_DEV, UP_CHUNK)), pad(loss_acc)],
        axis=0)
    tok = emit_small(small)
    gw_in = _dw(h1, dproj, tok, "dw_in", n_chunks=N_DEV, chunk_cols=IN_CHUNK, per_step=2)
    emit(dict(w_in=gw_in))
    return grad_x


def kernel(x, mem, rel_bias, g_mix, w_in, w_short_conv, g_attn_out, g_conv_out, w_out, g_xattn, g_mem, w_xq, w_xk, w_xv, w_xo, g_ffn, w_up, w_ffn_conv, b_ffn_conv, w_down, g_final, loss_target, m_rel_bias, m_g_mix, m_w_in, m_w_short_conv, m_g_attn_out, m_g_conv_out, m_w_out, m_g_xattn, m_g_mem, m_w_xq, m_w_xk, m_w_xv, m_w_xo, m_g_ffn, m_w_up, m_w_ffn_conv, m_b_ffn_conv, m_w_down, m_g_final, v_rel_bias, v_g_mix, v_w_in, v_w_short_conv, v_g_attn_out, v_g_conv_out, v_w_out, v_g_xattn, v_g_mem, v_w_xq, v_w_xk, v_w_xv, v_w_xo, v_g_ffn, v_w_up, v_w_ffn_conv, v_b_ffn_conv, v_w_down, v_g_final):
    me = _dev_index(_mesh_pos())
    me_arr = me.reshape(1).astype(jnp.int32)

    big_names = ["w_in", "w_out", "w_xq", "w_xk", "w_xv", "w_xo", "w_up", "w_down"]
    late_names = big_names[1:]
    big_w = dict(w_in=w_in[0], w_out=w_out[0], w_xq=w_xq[0], w_xk=w_xk[0], w_xv=w_xv[0], w_xo=w_xo[0],
                 w_up=w_up[0].T, w_down=w_down[0])
    big_m = dict(w_in=m_w_in[0], w_out=m_w_out[0], w_xq=m_w_xq[0], w_xk=m_w_xk[0], w_xv=m_w_xv[0], w_xo=m_w_xo[0],
                 w_up=m_w_up[0].T, w_down=m_w_down[0])
    big_v = dict(w_in=v_w_in[0], w_out=v_w_out[0], w_xq=v_w_xq[0], w_xk=v_w_xk[0], w_xv=v_w_xv[0], w_xo=v_w_xo[0],
                 w_up=v_w_up[0].T, w_down=v_w_down[0])
    shard_shape = {n: big_w[n].shape for n in big_names}

    w_in_g, w_sc_g, w_fc_full = _all_gather([big_w["w_in"].astype(BF16), w_short_conv[0], w_ffn_conv[0]])
    w_sc_full = w_sc_g.transpose(1, 0, 2).reshape(3, CONV_W)
    late_shards = [big_w[n].astype(BF16) for n in late_names]
    ag_send, ag_recv, ag_srcs, ag_lands, ag_token = _gather_start("gather_weights_start", late_shards, w_in_g)
    forwarded = {}

    def forward_weights(names, after):
        which = [late_names.index(n) for n in names]
        fsend, frecv, lands, token = _gather_forward("gather_" + "_".join(names) + "_forward", ag_send, ag_recv,
                                                     [ag_lands[a] for a in which], which, after)
        forwarded[tuple(names)] = (fsend, frecv, lands)
        return token

    def late_weights(names, after):
        which = [late_names.index(n) for n in names]
        fsend, frecv, lands = forwarded[tuple(names)]
        lands = _gather_wait("gather_" + "_".join(names) + "_wait", ag_send, ag_recv, fsend, frecv,
                             [ag_srcs[a] for a in which], lands, which, after)
        out = {}
        for n, a, land in zip(names, which, lands):
            full = lax.dynamic_update_index_in_dim(land, late_shards[a], me, 0)
            if n == "w_up":
                out[n] = full
            elif n == "w_down":
                out[n] = full.reshape(N_DEV // 2, UP_CHUNK, D_MODEL)
            else:
                out[n] = full.reshape(D_MODEL, D_MODEL)
        return out

    sent = []

    def emit(grads):
        names = list(grads)
        blocks = [grads[n].reshape((N_DEV,) + shard_shape[n]) for n in names]
        started = _exchange_start("scatter_" + "_".join(names) + "_start", blocks, [False] * len(names), me_arr)
        sent.append((names, started))
        return started[-1]

    def emit_small(small):
        sent_small.append((small, _exchange_start("gather_small_start", [small], [True], me_arr)))
        return sent_small[0][1][-1]

    sent_small = []
    grad_x = _local_step(
        x[0], mem[0], loss_target[0], rel_bias, g_mix, w_in_g, w_sc_full, g_attn_out, g_conv_out, g_xattn, g_mem,
        g_ffn, w_fc_full, b_ffn_conv.reshape(N_DEV, 1, UP_CHUNK), g_final.reshape(1, D_MODEL), ag_token,
        forward_weights, late_weights, emit, emit_small)

    small_g, small_started = sent_small[0]
    after = sent[-1][1][-1]
    small_parts = _exchange_wait("gather_small_wait", small_started, [True], after)[1][0]
    big_out = {}
    after = small_parts
    for names, started in sent:
        blocks, lands = _exchange_wait("scatter_" + "_".join(names) + "_wait", started, [False] * len(names), after)
        for n, block, land in zip(names, blocks, lands):
            res = _adamw_big("adamw_" + n, big_w[n], block, land, big_m[n], big_v[n], me_arr)
            big_out[n] = [(r.T if n == "w_up" else r)[None] for r in res]
            after = res[0]

    as_rows = lambda a: a.reshape(N_DEV, UP_CHUNK)
    row1 = lambda a: a.reshape(1, D_MODEL)
    small_names = ["rel_bias", "g_mix", "g_attn_out", "g_conv_out", "g_xattn", "g_mem", "g_ffn", "b_ffn_conv", "g_final"]
    wmv = [
        (rel_bias, m_rel_bias, v_rel_bias), (g_mix, m_g_mix, v_g_mix), (g_attn_out, m_g_attn_out, v_g_attn_out),
        (g_conv_out, m_g_conv_out, v_g_conv_out), (g_xattn, m_g_xattn, v_g_xattn), (g_mem, m_g_mem, v_g_mem),
        (g_ffn, m_g_ffn, v_g_ffn), (as_rows(b_ffn_conv), as_rows(m_b_ffn_conv), as_rows(v_b_ffn_conv)),
        (row1(g_final), row1(m_g_final), row1(v_g_final))]
    g_packed, small_res = _adamw_small(small_g, small_parts, wmv, me_arr)
    small_out = dict(zip(small_names, small_res))
    loss = g_packed[ROW_LOSS, 0]
    small_out["b_ffn_conv"] = [a.reshape(1, 2 * D_FF) for a in small_out["b_ffn_conv"]]
    small_out["g_final"] = [a.reshape(D_MODEL) for a in small_out["g_final"]]

    g_wsc = lax.dynamic_slice(g_packed[ROW_WSC:ROW_WSC + 3, 0:CONV_W], (0, me * HEAD_DIM), (3, HEAD_DIM))
    g_wfc = lax.dynamic_slice(g_packed[ROW_WFC:ROW_WFC + 3 * N_DEV, 0:UP_CHUNK].reshape(3, N_DEV, UP_CHUNK),
                              (0, me, 0), (3, 1, UP_CHUNK)).reshape(3, UP_CHUNK)
    shard_res = _adamw_shards([(w_short_conv[0], g_wsc, m_w_short_conv[0], v_w_short_conv[0]),
                               (w_ffn_conv[0], g_wfc, m_w_ffn_conv[0], v_w_ffn_conv[0])])
    small_out["w_short_conv"] = [g_wsc[None]] + [a[None] for a in shard_res[0]]
    small_out["w_ffn_conv"] = [g_wfc[None]] + [a[None] for a in shard_res[1]]

    order = ["rel_bias", "g_mix", "w_in", "w_short_conv", "g_attn_out", "g_conv_out", "w_out", "g_xattn", "g_mem",
             "w_xq", "w_xk", "w_xv", "w_xo", "g_ffn", "w_up", "w_ffn_conv", "b_ffn_conv", "w_down", "g_final"]
    allp = {**big_out, **small_out}
    outs = [loss, grad_x[None]]
    for kind in range(4):
        outs += [allp[n][kind] for n in order]
    return tuple(outs)
```

```python
import math

import numpy as np
import jax
import jax.numpy as jnp
from jax import lax
from jax.experimental import pallas as pl
from jax.experimental.pallas import tpu as pltpu

F32 = jnp.float32
BF16 = jnp.bfloat16
MESH = pl.DeviceIdType.MESH

N_DEV = 8
D_MODEL = 1024
ATTN_W = 512
CONV_W = 512
N_HEADS = 8
HEAD_DIM = 64
WIN = 128
DILATIONS = (1, 4, 16)
N_BUCKETS = 32
BUCKET_MAX_EXACT = 16
BUCKET_MAX_DISTANCE = 2048
N_MEM_HEADS = 4
MEM_HEAD_DIM = 256
D_FF = 2816
IN_COLS = 3072
IN_CHUNK = IN_COLS // N_DEV
UP_CHUNK = 2 * D_FF // N_DEV
FFN_CHUNKS = 4
FFN_WIDTH = 2 * D_FF // FFN_CHUNKS
EPS = 1e-6

ADAM_LR = 0.001
ADAM_B1 = 0.9
ADAM_B2 = 0.999
ADAM_EPS = 1e-08
ADAM_WD = 0.01
ADAM_STEP = 10

SUBLANES = 8
LANES = 128
HALO = 16
TM = 512
TM_XATTN = 1024
TM_FFN = 256
TS_DW = 4096
SWA_BLOCKS = 8
VMEM_LIMIT = 56 * 1024 * 1024

ROW_RELB, ROW_GMIX, ROW_GXATTN, ROW_GMEM, ROW_GFFN, ROW_GFINAL, ROW_GAC = 0, 8, 16, 24, 32, 40, 48
ROW_WSC, ROW_BFC, ROW_WFC, ROW_LOSS, SMALL_ROWS = 56, 64, 72, 96, 104


def _cparams(n_grid):
    return pltpu.CompilerParams(dimension_semantics=("arbitrary",) * n_grid, vmem_limit_bytes=VMEM_LIMIT)


def _full(shape):
    nd = len(shape)
    return pl.BlockSpec(tuple(shape), lambda *_: (0,) * nd)


def _resident(shape):
    nd = len(shape)
    return pl.BlockSpec(tuple(shape), lambda *_: (0,) * nd, pipeline_mode=pl.Buffered(1))


ANY_SPEC = pl.BlockSpec(memory_space=pl.ANY)
HBM_SPEC = pl.BlockSpec(memory_space=pltpu.HBM)
SEM_SPEC = pl.BlockSpec(memory_space=pltpu.SEMAPHORE)
VMEM_SPEC = pl.BlockSpec(memory_space=pltpu.VMEM)
SMEM_SPEC = pl.BlockSpec(memory_space=pltpu.SMEM)
DATAFLOW = pltpu.SideEffectType.DATAFLOW_SIDE_EFFECTING


def _rms(x):
    r = lax.rsqrt(jnp.mean(x * x, axis=-1, keepdims=True) + EPS)
    return x * r, r


def _rms_bwd(xh, r, g, dy):
    dxh = dy * g
    return r * (dxh - xh * jnp.mean(dxh * xh, axis=-1, keepdims=True))


def _shift_down(u, halo, k):
    ru = pltpu.roll(u, k, 0)
    rh = pltpu.roll(halo, k, 0)
    row = lax.broadcasted_iota(jnp.int32, rh.shape, 0)
    head = jnp.where(row < k, rh, ru[0:SUBLANES])
    return jnp.concatenate([head, ru[SUBLANES:]], axis=0)


def _shift_up(u, halo, k):
    tm = u.shape[0]
    ru = pltpu.roll(u, tm - k, 0)
    rh = pltpu.roll(halo, SUBLANES - k, 0)
    row = lax.broadcasted_iota(jnp.int32, rh.shape, 0)
    tail = jnp.where(row >= SUBLANES - k, rh, ru[tm - SUBLANES:])
    return jnp.concatenate([ru[:tm - SUBLANES], tail], axis=0)


def _causal_conv3(u, halo, w_ref):
    return (_shift_down(u, halo, 2) * w_ref[0:1, :] + _shift_down(u, halo, 1) * w_ref[1:2, :]) + u * w_ref[2:3, :]


def _dot(a, b):
    return jnp.dot(a, b, preferred_element_type=F32)


def _dot_nt(a, b):
    return lax.dot_general(a, b, (((1,), (1,)), ((), ())), preferred_element_type=F32)


def _dot_tn(a, b):
    return lax.dot_general(a, b, (((0,), (0,)), ((), ())), preferred_element_type=F32)


def _sigmoid(x):
    return 0.5 * jnp.tanh(0.5 * x) + 0.5


def _bucket_tables():
    qi = np.arange(WIN)[:, None]
    kj = np.arange(2 * WIN)[None, :]
    steps = np.clip(qi + WIN - kj, 0, WIN)
    out = []
    for d in DILATIONS:
        dist = steps * d
        dd = np.maximum(dist, 1).astype(np.float32)
        large = BUCKET_MAX_EXACT + (
            np.log(dd / np.float32(BUCKET_MAX_EXACT)) / np.float32(math.log(BUCKET_MAX_DISTANCE / BUCKET_MAX_EXACT))
            * np.float32(N_BUCKETS - BUCKET_MAX_EXACT)).astype(np.int32)
        large = np.minimum(large, N_BUCKETS - 1)
        out.append(np.where(dist < BUCKET_MAX_EXACT, dist, large).astype(np.int32))
    return np.stack(out)


def _band_mask():
    qi = lax.broadcasted_iota(jnp.int32, (WIN, 2 * WIN), 0)
    kj = lax.broadcasted_iota(jnp.int32, (WIN, 2 * WIN), 1)
    steps = qi + WIN - kj
    return (steps >= 0) & (steps <= WIN)


def _bias_fwd(rel_bias, buckets):
    present = [sorted(set(buckets[p].ravel().tolist())) for p in range(3)]

    def body(rb_ref, bk_ref, o_ref):
        band = _band_mask()
        for p in range(3):
            bk = bk_ref[p]
            for h in range(N_HEADS):
                acc = jnp.zeros((WIN, 2 * WIN), F32)
                for b in present[p]:
                    acc = jnp.where(bk == b, rb_ref[h, b], acc)
                o_ref[p, h] = jnp.where(band, acc, -jnp.inf)

    return pl.pallas_call(
        body, name="bias_fwd",
        out_shape=jax.ShapeDtypeStruct((3, N_HEADS, WIN, 2 * WIN), F32),
        in_specs=[pl.BlockSpec(memory_space=pltpu.SMEM), pl.BlockSpec(memory_space=pltpu.VMEM)],
        out_specs=pl.BlockSpec(memory_space=pltpu.VMEM),
    )(rel_bias, jnp.asarray(buckets))


def _bias_bwd(dbias, buckets):
    present = [set(buckets[p].ravel().tolist()) for p in range(3)]

    def body(db_ref, bk_ref, o_ref):
        lane = lax.broadcasted_iota(jnp.int32, (1, D_MODEL), 1)
        rows = []
        for h in range(N_HEADS):
            row = jnp.zeros((1, D_MODEL), F32)
            for b in range(N_BUCKETS):
                tot = jnp.zeros((1, 1), F32)
                for p in (p for p in range(3) if b in present[p]):
                    sel = jnp.where(bk_ref[p] == b, db_ref[p, h], 0.0)
                    tot = tot + jnp.sum(jnp.sum(sel, axis=0, keepdims=True), axis=1, keepdims=True)
                row = jnp.where(lane == b, tot, row)
            rows.append(row)
        o_ref[...] = jnp.concatenate(rows, axis=0)

    return pl.pallas_call(
        body, name="bias_bwd",
        out_shape=jax.ShapeDtypeStruct((N_HEADS, D_MODEL), F32),
        in_specs=[pl.BlockSpec(memory_space=pltpu.VMEM), pl.BlockSpec(memory_space=pltpu.VMEM)],
        out_specs=pl.BlockSpec(memory_space=pltpu.VMEM),
    )(dbias, jnp.asarray(buckets))


def _spread(val, scr_ref, out_refs, dtype):
    out_refs[0][...] = val.astype(dtype)
    n_blk = val.shape[1] // LANES
    for c in range(n_blk):
        scr_ref[c] = val[:, c * LANES:(c + 1) * LANES]
    for o_ref, d in zip(out_refs[1:], DILATIONS[1:]):
        for r in range(d):
            for c in range(n_blk):
                o_ref[r, :, c * LANES:(c + 1) * LANES] = scr_ref.at[c][pl.ds(r, TM // d, stride=d), :].astype(dtype)


HEAD_LANES = LANES // N_HEADS


def _head_lane(h):
    return HEAD_LANES * (h // 2) + (LANES // 2) * (h % 2)


def _narrow_heads(x):
    grp = (lax.broadcasted_iota(jnp.int32, (x.shape[0], LANES), 1) // HEAD_LANES) % (N_HEADS // 2)
    out = x[:, 0:LANES]
    for t in range(1, N_HEADS // 2):
        out = jnp.where(grp == t, x[:, t * LANES:(t + 1) * LANES], out)
    return out


def _gather_classes(blk_ref, scr_ref, d):
    n_blk = blk_ref.shape[2] // LANES
    for r in range(d):
        for c in range(n_blk):
            scr_ref.at[c][pl.ds(r, TM // d, stride=d), :] = blk_ref[r, :, c * LANES:(c + 1) * LANES].astype(F32)
    return jnp.concatenate([scr_ref[c] for c in range(n_blk)], axis=1)


def _class_specs(cols):
    return [pl.BlockSpec((TM, cols), lambda i: (i, 0))] + [
        pl.BlockSpec((d, TM // d, cols), lambda i: (0, i, 0)) for d in DILATIONS[1:]]


def _class_shapes(s, cols, dtype):
    return [jax.ShapeDtypeStruct((s, cols), dtype)] + [
        jax.ShapeDtypeStruct((d, s // d, cols), dtype) for d in DILATIONS[1:]]


def _load_w_in_pairs(w_hbm, w_scr, sems):
    @pl.when(pl.program_id(0) == 0)
    def _():
        copies = [pltpu.make_async_copy(w_hbm.at[j], w_scr.at[j // 2, :, pl.ds((j % 2) * IN_CHUNK, IN_CHUNK)],
                                        sems.at[j]) for j in range(N_DEV)]
        for copy in copies:
            copy.start()
        for copy in copies:
            copy.wait()


W_IN_PAIRS = [pltpu.VMEM((N_DEV // 2, D_MODEL, 2 * IN_CHUNK), BF16), pltpu.SemaphoreType.DMA((N_DEV,))]


def _rms_proj(x, g_mix, w_in_g, dep):
    s = x.shape[0]

    def body(x_ref, g_ref, w_hbm, dep_ref, h_ref, q1, q4, q16, k1, k4, k16, v1, v4, v16, gb_ref, gc_ref, xi_ref, scr,
             w_scr, w_sems):
        _load_w_in_pairs(w_hbm, w_scr, w_sems)
        xh, _ = _rms(x_ref[...])
        h = (xh * g_ref[...]).astype(BF16)
        h_ref[...] = h
        proj = jnp.concatenate([_dot(h, w_scr[j]) for j in range(N_DEV // 2)], axis=1)
        _spread(proj[:, 0:512] * (HEAD_DIM ** -0.5), scr, (q1, q4, q16), BF16)
        _spread(proj[:, 512:1024], scr, (k1, k4, k16), BF16)
        _spread(proj[:, 1024:1536], scr, (v1, v4, v16), BF16)
        gb_ref[...] = proj[:, 1536:2048]
        gc_ref[...] = proj[:, 2048:2560]
        xi_ref[...] = proj[:, 2560:3072]

    row = lambda n: pl.BlockSpec((TM, n), lambda i: (i, 0))
    res = pl.pallas_call(
        body, name="rms_proj", grid=(s // TM,),
        out_shape=[jax.ShapeDtypeStruct((s, D_MODEL), BF16)] + _class_shapes(s, 512, BF16) * 3
        + [jax.ShapeDtypeStruct((s, 512), F32)] * 3,
        in_specs=[row(D_MODEL), _full(g_mix.shape), ANY_SPEC, ANY_SPEC],
        out_specs=[row(D_MODEL)] + _class_specs(512) * 3 + [row(512)] * 3,
        scratch_shapes=[pltpu.VMEM((512 // LANES, TM, LANES), F32)] + W_IN_PAIRS,
        compiler_params=_cparams(1),
    )(x, g_mix, w_in_g, dep)
    return res[0], res[1:4], res[4:7], res[7:10], res[10], res[11], res[12]


def _pair_split(x2):
    lane = lax.broadcasted_iota(jnp.int32, x2.shape, 1)
    zero = jnp.zeros_like(x2)
    return jnp.where(lane < HEAD_DIM, x2, zero), jnp.where(lane >= HEAD_DIM, x2, zero)


def _pair_join(even, odd):
    lane = lax.broadcasted_iota(jnp.int32, (even.shape[0], LANES), 1)
    return jnp.where(lane < HEAD_DIM, even, odd)


def _swa_steps(qc, dil):
    n128 = qc.shape[1] // WIN
    nsub = min(SWA_BLOCKS, n128)
    nb = n128 // nsub
    ncls = min(dil, SWA_BLOCKS // nsub) if nb == 1 else 1
    return nsub, nb, ncls


def _swa_fwd(qc, kc, vc, bias, dil, dep):
    nsub, nb, ncls = _swa_steps(qc, dil)
    whole = nb == 1

    def body(q_ref, kp_ref, kc_ref, vp_ref, vc_ref, b_ref, dep_ref, o_ref, lse_ref, s_scr, p_scr):
        no_prev = (pl.program_id(1) == 0) & (lax.broadcasted_iota(jnp.int32, (WIN, 2 * WIN), 1) < WIN)
        pairs = [slice(a * LANES, (a + 1) * LANES) for a in range(N_HEADS // 2)]
        for c, t in [(c, t) for c in range(ncls) for t in range(nsub)]:
            i = c * nsub + t
            rows = slice(t * WIN, (t + 1) * WIN)
            alone = whole and t == 0
            cols = slice(WIN, 2 * WIN) if alone else slice(0, 2 * WIN)

            def keys(prev_ref, cur_ref, sl):
                if alone:
                    return cur_ref[c, rows, sl]
                if t == 0:
                    return jnp.concatenate([prev_ref[c, :, sl], cur_ref[c, rows, sl]], axis=0)
                return cur_ref[c, (t - 1) * WIN:(t + 1) * WIN, sl]

            for a, sl in enumerate(pairs):
                k2 = keys(kp_ref, kc_ref, sl)
                for e, qh in enumerate(_pair_split(q_ref[c, rows, sl])):
                    s_scr[i, 2 * a + e, :, cols] = _dot_nt(qh, k2)
            den, lse = [], []
            for h in range(N_HEADS):
                lg = s_scr[i, h, :, cols] + b_ref[h, :, cols]
                if t == 0 and not whole:
                    lg = jnp.where(no_prev, -jnp.inf, lg)
                m = jnp.max(lg, axis=-1, keepdims=True)
                p = jnp.exp(lg - m)
                den.append(jnp.sum(p, axis=-1, keepdims=True))
                p_scr[i, h, :, cols] = p.astype(BF16)
                lse.append(m + jnp.log(den[h]))
            for a, sl in enumerate(pairs):
                v_even, v_odd = _pair_split(keys(vp_ref, vc_ref, sl))
                o2 = _dot(p_scr[i, 2 * a, :, cols], v_even) + _dot(p_scr[i, 2 * a + 1, :, cols], v_odd)
                o_ref[c, rows, sl] = o2 / _pair_join(den[2 * a], den[2 * a + 1])
                lse_ref[c, rows, sl] = _pair_join(lse[2 * a], lse[2 * a + 1])

    cur = pl.BlockSpec((ncls, nsub * WIN, 512), lambda r, b: (r, b, 0))
    prev = pl.BlockSpec((ncls, WIN, 512), lambda r, b: (r, jnp.maximum(nsub * b - 1, 0), 0))
    wide = (ncls * nsub, N_HEADS, WIN, 2 * WIN)
    return pl.pallas_call(
        body, name=f"swa_fwd_d{dil}", grid=(dil // ncls, nb),
        out_shape=[jax.ShapeDtypeStruct(qc.shape, F32)] * 2,
        in_specs=[cur, prev, cur, prev, cur, _full(bias.shape), ANY_SPEC],
        out_specs=[cur] * 2,
        scratch_shapes=[pltpu.VMEM(wide, F32), pltpu.VMEM(wide, BF16)],
        compiler_params=_cparams(2),
    )(qc, kc, kc, vc, vc, bias, dep)


def _mix_out(branches, gb, gc, xi, x, w_sc, g_a, g_c, w_out):
    s = x.shape[0]
    tb = TM // SUBLANES

    def body(o1, l1, o4, l4, o16, l16, gb_ref, gc_ref, xi_ref, gch_ref, xih_ref, x_ref, wsc_ref,
             ga_ref, gcv_ref, wout_ref, attn_ref, lse1, lse4, lse16, mixed_ref, x1_ref, scr_a, scr_b, scr_c, scr_d):
        i = pl.program_id(0)
        la, lb, lc = l1[...], _gather_classes(l4, scr_a, 4), _gather_classes(l16, scr_b, 16)
        m_all = jnp.maximum(jnp.maximum(la, lb), lc)
        ea, eb, ec = jnp.exp(la - m_all), jnp.exp(lb - m_all), jnp.exp(lc - m_all)
        den = (ea + eb) + ec
        num = (ea * o1[...] + eb * _gather_classes(o4, scr_c, 4)) + ec * _gather_classes(o16, scr_d, 16)
        attn = num / den
        attn_ref[...] = attn
        _spread(_narrow_heads(m_all + jnp.log(den)), scr_a, (lse1, lse4, lse16), F32)
        xa, _ = _rms(attn)
        u = gc_ref[...] * xi_ref[...]
        uh = jnp.where(i > 0, gch_ref[...] * xih_ref[...], 0.0)
        conv = gb_ref[...] * _causal_conv3(u, uh, wsc_ref)
        xc, _ = _rms(conv)
        mixed = jnp.concatenate([xa * ga_ref[...], xc * gcv_ref[...]], axis=1).astype(BF16)
        mixed_ref[...] = mixed
        x1_ref[...] = x_ref[...] + _dot(mixed, wout_ref[...])

    row = lambda n: pl.BlockSpec((TM, n), lambda i: (i, 0))
    halo = pl.BlockSpec((SUBLANES, 512), lambda i: (jnp.maximum(i * tb - 1, 0), 0))
    cs = _class_specs(512)
    flat = [a for br in branches for a in br]
    res = pl.pallas_call(
        body, name="mix_out", grid=(s // TM,),
        out_shape=[jax.ShapeDtypeStruct((s, 512), F32)] + _class_shapes(s, LANES, F32)
        + [jax.ShapeDtypeStruct((s, D_MODEL), BF16), jax.ShapeDtypeStruct((s, D_MODEL), F32)],
        in_specs=[cs[0], cs[0], cs[1], cs[1], cs[2], cs[2], row(512), row(512), row(512), halo, halo,
                  row(D_MODEL), _full(w_sc.shape), _full(g_a.shape), _full(g_c.shape), _full(w_out.shape)],
        out_specs=[row(512)] + _class_specs(LANES) + [row(D_MODEL), row(D_MODEL)],
        scratch_shapes=[pltpu.VMEM((512 // LANES, TM, LANES), F32)] * 4,
        compiler_params=_cparams(1),
    )(*flat, gb, gc, xi, gc, xi, x, w_sc, g_a, g_c, w_out)
    return res[0], res[1:4], res[4], res[5]


def _mem_kv(mem, g_mem, w_xk, w_xv):
    def body(mem_ref, g_ref, wk_ref, wv_ref, mn_ref, k_ref, v_ref):
        xh, _ = _rms(mem_ref[...])
        mn = (xh * g_ref[...]).astype(BF16)
        mn_ref[...] = mn
        k_ref[...] = _dot(mn, wk_ref[...]).astype(BF16)
        v_ref[...] = _dot(mn, wv_ref[...]).astype(BF16)

    vm = pl.BlockSpec(memory_space=pltpu.VMEM)
    return pl.pallas_call(
        body, name="mem_kv",
        out_shape=[jax.ShapeDtypeStruct(mem.shape, BF16)] * 3,
        in_specs=[vm] * 4, out_specs=[vm] * 3,
        compiler_params=pltpu.CompilerParams(vmem_limit_bytes=VMEM_LIMIT),
    )(mem, g_mem, w_xk, w_xv)


def _xattn_fwd(x1, g, w_xq, k, v, w_xo, dep):
    s = x1.shape[0]

    def body(x1_ref, g_ref, wq_ref, k_ref, v_ref, wo_ref, dep_ref, h2_ref, q_ref, o_ref, x2_ref):
        x1v = x1_ref[...]
        xh, _ = _rms(x1v)
        h2 = (xh * g_ref[...]).astype(BF16)
        h2_ref[...] = h2
        qb = _dot(h2, wq_ref[...]).astype(BF16)
        q_ref[...] = qb
        outs = []
        for h in range(N_MEM_HEADS):
            sl = slice(h * MEM_HEAD_DIM, (h + 1) * MEM_HEAD_DIM)
            lg = _dot_nt(qb[:, sl], k_ref[:, sl]) * (MEM_HEAD_DIM ** -0.5)
            p = jnp.exp(lg - jnp.max(lg, axis=-1, keepdims=True))
            p = p / jnp.sum(p, axis=-1, keepdims=True)
            outs.append(_dot(p.astype(BF16), v_ref[:, sl]))
        o = jnp.concatenate(outs, axis=1).astype(BF16)
        o_ref[...] = o
        x2_ref[...] = x1v + _dot(o, wo_ref[...])

    row = pl.BlockSpec((TM_XATTN, D_MODEL), lambda i: (i, 0))
    return pl.pallas_call(
        body, name="xattn_fwd", grid=(s // TM_XATTN,),
        out_shape=[jax.ShapeDtypeStruct((s, D_MODEL), BF16)] * 3 + [jax.ShapeDtypeStruct((s, D_MODEL), F32)],
        in_specs=[row, _full(g.shape), _full(w_xq.shape), _full(k.shape), _full(v.shape), _full(w_xo.shape), ANY_SPEC],
        out_specs=[row] * 4,
        compiler_params=_cparams(1),
    )(x1, g, w_xq, k, v, w_xo, dep)


def _ffn_conv(h_ext, wup_ref, wfc_ref, bfc_ref, j):
    u = _dot_nt(h_ext, wup_ref[j])
    w = wfc_ref[j]
    c = ((pltpu.roll(u, 2, 0) * w[0:1, :] + pltpu.roll(u, 1, 0) * w[1:2, :]) + u * w[2:3, :]) + bfc_ref[j]
    return c[HALO:], u[HALO:]


def _ffn_fwd(x2, g, w_up_g, w_fc, b_fc, w_down_g, g_final, target):
    s = x2.shape[0]
    tb = TM_FFN // HALO
    n_ch, wid = w_up_g.shape[:2]
    half = n_ch // 2

    def body(x_ref, xp_ref, g_ref, wup_ref, wfc_ref, bfc_ref, wd_ref, gf_ref, t_ref, h_ref, u_ref, c_ref, act_ref,
             dx3_ref, loss_ref, dgf_ref):
        i = pl.program_id(0)

        @pl.when(i == 0)
        def _():
            loss_ref[...] = jnp.zeros_like(loss_ref)
            dgf_ref[...] = jnp.zeros_like(dgf_ref)

        x2v = x_ref[...]
        gv = g_ref[...]
        h = (_rms(x2v)[0] * gv).astype(BF16)
        h_ref[...] = h
        hp = jnp.where(i > 0, _rms(xp_ref[...])[0] * gv, 0.0).astype(BF16)
        h_ext = jnp.concatenate([hp, h], axis=0)
        down = jnp.zeros((TM_FFN, D_MODEL), F32)
        for j in range(half):
            cg, ug = _ffn_conv(h_ext, wup_ref, wfc_ref, bfc_ref, j)
            cv, uv = _ffn_conv(h_ext, wup_ref, wfc_ref, bfc_ref, j + half)
            c_ref[j] = cg
            c_ref[j + half] = cv
            u_ref[j] = ug.astype(BF16)
            u_ref[j + half] = uv.astype(BF16)
            a = ((cg * _sigmoid(cg)) * cv).astype(BF16)
            act_ref[j] = a
            down = down + _dot(a, wd_ref[j])
        x3 = x2v + down
        xh, r = _rms(x3)
        gf = gf_ref[...]
        e = xh * gf - t_ref[...]
        loss_ref[...] += 0.5 * jnp.sum(jnp.sum(e * e, axis=1, keepdims=True), axis=0, keepdims=True) / D_MODEL
        dy = e * (1.0 / D_MODEL)
        dgf_ref[0:1, :] += jnp.sum(dy * xh, axis=0, keepdims=True)
        dx3_ref[...] = _rms_bwd(xh, r, gf, dy)

    row = pl.BlockSpec((TM_FFN, D_MODEL), lambda i: (i, 0))
    prev = pl.BlockSpec((HALO, D_MODEL), lambda i: (jnp.maximum(i * tb - 1, 0), 0))
    return pl.pallas_call(
        body, name="ffn_fwd", grid=(s // TM_FFN,),
        out_shape=[jax.ShapeDtypeStruct((s, D_MODEL), BF16), jax.ShapeDtypeStruct((n_ch, s, wid), BF16),
                   jax.ShapeDtypeStruct((n_ch, s, wid), F32), jax.ShapeDtypeStruct((half, s, wid), BF16),
                   jax.ShapeDtypeStruct((s, D_MODEL), F32), jax.ShapeDtypeStruct((SUBLANES, 128), F32),
                   jax.ShapeDtypeStruct((SUBLANES, D_MODEL), F32)],
        in_specs=[row, prev, _full(g.shape), _resident(w_up_g.shape), _full(w_fc.shape), _full(b_fc.shape),
                  _resident(w_down_g.shape), _full(g_final.shape), row],
        out_specs=[row, pl.BlockSpec((n_ch, TM_FFN, wid), lambda i: (0, i, 0)),
                   pl.BlockSpec((n_ch, TM_FFN, wid), lambda i: (0, i, 0)),
                   pl.BlockSpec((half, TM_FFN, wid), lambda i: (0, i, 0)), row,
                   _full((SUBLANES, 128)), _full((SUBLANES, D_MODEL))],
        compiler_params=_cparams(1),
    )(x2, x2, g, w_up_g, w_fc, b_fc, w_down_g, g_final, target)


def _ffn_bwd(dx3, up, conv, x2, g, w_up_g, w_fc, w_down_g):
    s = x2.shape[0]
    tb = TM_FFN // HALO
    last = s // HALO - 1
    n_tiles = s // TM_FFN
    n_ch, wid = w_up_g.shape[:2]
    half = n_ch // 2
    n_ext = TM_FFN + HALO

    def body(dx_ref, dxn_ref, u_ref, c_ref, cn_ref, x2_ref, g_ref, wup_ref, wfc_ref, wd_ref,
             dup_ref, dx2_ref, dg_ref, dwfc_ref, dbfc_ref):
        i = pl.program_id(0)

        @pl.when(i == 0)
        def _():
            dg_ref[...] = jnp.zeros_like(dg_ref)
            dwfc_ref[...] = jnp.zeros_like(dwfc_ref)
            dbfc_ref[...] = jnp.zeros_like(dbfc_ref)

        dxv = dx_ref[...]
        dxn = jnp.where(i < n_tiles - 1, dxn_ref[...], 0.0)
        dx_ext = jnp.concatenate([dxv, dxn], axis=0).astype(BF16)
        dh = jnp.zeros((TM_FFN, D_MODEL), F32)
        for j in range(half):
            cg = jnp.concatenate([c_ref[j], cn_ref[j]], axis=0)
            cv = jnp.concatenate([c_ref[j + half], cn_ref[j + half]], axis=0)
            dact = _dot_nt(dx_ext, wd_ref[j])
            sg = _sigmoid(cg)
            silu = cg * sg
            parts = ((j + half, dact * silu), (j, (dact * cv) * (sg + silu * (1.0 - sg))))
            for jj, dc in parts:
                u = u_ref[jj].astype(F32)
                dc0, dc1, dc2 = dc[:TM_FFN], pltpu.roll(dc, n_ext - 1, 0)[:TM_FFN], pltpu.roll(dc, n_ext - 2, 0)[:TM_FFN]
                dbfc_ref[jj:jj + 1, :] += jnp.sum(dc0, axis=0, keepdims=True)
                dwfc_ref[0, jj:jj + 1, :] += jnp.sum(dc2 * u, axis=0, keepdims=True)
                dwfc_ref[1, jj:jj + 1, :] += jnp.sum(dc1 * u, axis=0, keepdims=True)
                dwfc_ref[2, jj:jj + 1, :] += jnp.sum(dc0 * u, axis=0, keepdims=True)
                w = wfc_ref[jj]
                du = ((dc0 * w[2:3, :] + dc1 * w[1:2, :]) + dc2 * w[0:1, :]).astype(BF16)
                dup_ref[jj] = du
                dh = dh + _dot(du, wup_ref[jj])
        xh, r = _rms(x2_ref[...])
        dg_ref[0:1, :] += jnp.sum(dh * xh, axis=0, keepdims=True)
        dx2_ref[...] = dxv + _rms_bwd(xh, r, g_ref[...], dh)

    row = pl.BlockSpec((TM_FFN, D_MODEL), lambda i: (i, 0))
    nxt = pl.BlockSpec((HALO, D_MODEL), lambda i: (jnp.minimum((i + 1) * tb, last), 0))
    cur_c = pl.BlockSpec((n_ch, TM_FFN, wid), lambda i: (0, i, 0))
    nxt_c = pl.BlockSpec((n_ch, HALO, wid), lambda i: (0, jnp.minimum((i + 1) * tb, last), 0))
    return pl.pallas_call(
        body, name="ffn_bwd", grid=(n_tiles,),
        out_shape=[jax.ShapeDtypeStruct((n_ch, s, wid), BF16), jax.ShapeDtypeStruct((s, D_MODEL), F32),
                   jax.ShapeDtypeStruct((SUBLANES, D_MODEL), F32), jax.ShapeDtypeStruct((3, n_ch, wid), F32),
                   jax.ShapeDtypeStruct((n_ch, wid), F32)],
        in_specs=[row, nxt, cur_c, cur_c, nxt_c, row, _full(g.shape), _resident(w_up_g.shape), _full(w_fc.shape),
                  _resident(w_down_g.shape)],
        out_specs=[cur_c, row, _full((SUBLANES, D_MODEL)), _full((3, n_ch, wid)), _full((n_ch, wid))],
        compiler_params=_cparams(1),
    )(dx3, dx3, up, conv, conv, x2, g, w_up_g, w_fc, w_down_g)


def _xattn_bwd(dx2, o, q, k, v, w_xo, w_xq, x1, g, dep):
    s = x1.shape[0]

    def body(dx2_ref, o_ref, q_ref, k_ref, v_ref, wo_ref, wq_ref, x1_ref, g_ref, dep_ref, dq_ref, dx1_ref, dk_ref,
             dv_ref, dg_ref):
        @pl.when(pl.program_id(0) == 0)
        def _():
            dk_ref[...] = jnp.zeros_like(dk_ref)
            dv_ref[...] = jnp.zeros_like(dv_ref)
            dg_ref[...] = jnp.zeros_like(dg_ref)

        dx2v = dx2_ref[...]
        do = _dot_nt(dx2v.astype(BF16), wo_ref[...])
        dqs = []
        for h in range(N_MEM_HEADS):
            sl = slice(h * MEM_HEAD_DIM, (h + 1) * MEM_HEAD_DIM)
            qh, kh, vh = q_ref[:, sl], k_ref[:, sl], v_ref[:, sl]
            lg = _dot_nt(qh, kh) * (MEM_HEAD_DIM ** -0.5)
            p = jnp.exp(lg - jnp.max(lg, axis=-1, keepdims=True))
            p = p / jnp.sum(p, axis=-1, keepdims=True)
            doh = do[:, sl].astype(BF16)
            dp = _dot_nt(doh, vh)
            ds = (p * (dp - jnp.sum(p * dp, axis=-1, keepdims=True)) * (MEM_HEAD_DIM ** -0.5)).astype(BF16)
            dqs.append(_dot(ds, kh))
            dk_ref[:, sl] += _dot_tn(ds, qh)
            dv_ref[:, sl] += _dot_tn(p.astype(BF16), doh)
        dq = jnp.concatenate(dqs, axis=1).astype(BF16)
        dq_ref[...] = dq
        dh2 = _dot_nt(dq, wq_ref[...])
        xh, r = _rms(x1_ref[...])
        dg_ref[0:1, :] += jnp.sum(dh2 * xh, axis=0, keepdims=True)
        dx1_ref[...] = dx2v + _rms_bwd(xh, r, g_ref[...], dh2)

    row = pl.BlockSpec((TM_XATTN, D_MODEL), lambda i: (i, 0))
    return pl.pallas_call(
        body, name="xattn_bwd", grid=(s // TM_XATTN,),
        out_shape=[jax.ShapeDtypeStruct((s, D_MODEL), BF16), jax.ShapeDtypeStruct((s, D_MODEL), F32),
                   jax.ShapeDtypeStruct(k.shape, F32), jax.ShapeDtypeStruct(k.shape, F32),
                   jax.ShapeDtypeStruct((SUBLANES, D_MODEL), F32)],
        in_specs=[row, row, row, _full(k.shape), _full(v.shape), _full(w_xo.shape), _full(w_xq.shape), row,
                  _full(g.shape), ANY_SPEC],
        out_specs=[row, row, _full(k.shape), _full(k.shape), _full((SUBLANES, D_MODEL))],
        compiler_params=_cparams(1),
    )(dx2, o, q, k, v, w_xo, w_xq, x1, g, dep)


def _mem_kv_bwd(dk, dv, mem_n, mem, w_xk, w_xv):
    def body(dk_ref, dv_ref, mn_ref, mem_ref, wk_ref, wv_ref, dwk_ref, dwv_ref, dg_ref):
        dkb, dvb = dk_ref[...].astype(BF16), dv_ref[...].astype(BF16)
        mn = mn_ref[...]
        dwk_ref[...] = _dot_tn(mn, dkb).astype(BF16)
        dwv_ref[...] = _dot_tn(mn, dvb).astype(BF16)
        dmn = _dot_nt(dkb, wk_ref[...]) + _dot_nt(dvb, wv_ref[...])
        xh, _ = _rms(mem_ref[...])
        dg_ref[...] = jnp.zeros_like(dg_ref)
        dg_ref[0:1, :] = jnp.sum(dmn * xh, axis=0, keepdims=True)

    vm = pl.BlockSpec(memory_space=pltpu.VMEM)
    return pl.pallas_call(
        body, name="mem_kv_bwd",
        out_shape=[jax.ShapeDtypeStruct(w_xk.shape, BF16), jax.ShapeDtypeStruct(w_xv.shape, BF16),
                   jax.ShapeDtypeStruct((SUBLANES, D_MODEL), F32)],
        in_specs=[vm] * 6, out_specs=[vm] * 3,
        compiler_params=pltpu.CompilerParams(vmem_limit_bytes=VMEM_LIMIT),
    )(dk, dv, mem_n, mem, w_xk, w_xv)


def _mix_out_bwd(dx1, w_out, attn, gb, gc, xi, w_sc, g_a, g_c, dep):
    s = dx1.shape[0]
    tb = TM // SUBLANES

    def body(dx1_ref, wout_ref, attn_ref, gb_ref, gc_ref, xi_ref, gch_ref, xih_ref, wsc_ref, ga_ref, gcv_ref, dep_ref,
             da1, da4, da16, dd1, dd4, dd16, dgb_ref, dcv_ref, dga_ref, dgc_ref, dwsc_ref, scr):
        i = pl.program_id(0)

        @pl.when(i == 0)
        def _():
            dga_ref[...] = jnp.zeros_like(dga_ref)
            dgc_ref[...] = jnp.zeros_like(dgc_ref)
            dwsc_ref[...] = jnp.zeros_like(dwsc_ref)

        dmixed = _dot_nt(dx1_ref[...].astype(BF16), wout_ref[...])
        da, dcn = dmixed[:, :ATTN_W], dmixed[:, ATTN_W:]
        attn = attn_ref[...]
        xa, ra = _rms(attn)
        dga_ref[0:1, :] += jnp.sum(da * xa, axis=0, keepdims=True)
        dattn = _rms_bwd(xa, ra, ga_ref[...], da)
        _spread(dattn, scr, (da1, da4, da16), BF16)
        prod = dattn * attn
        dd = jnp.concatenate(
            [jnp.broadcast_to(jnp.sum(prod[:, h * HEAD_DIM:(h + 1) * HEAD_DIM], axis=-1, keepdims=True),
                              (TM, HEAD_DIM)) for h in range(N_HEADS)], axis=1)
        _spread(_narrow_heads(dd), scr, (dd1, dd4, dd16), F32)
        gbv = gb_ref[...]
        u = gc_ref[...] * xi_ref[...]
        uh = jnp.where(i > 0, gch_ref[...] * xih_ref[...], 0.0)
        u2, u1 = _shift_down(u, uh, 2), _shift_down(u, uh, 1)
        cv = (u2 * wsc_ref[0:1, :] + u1 * wsc_ref[1:2, :]) + u * wsc_ref[2:3, :]
        xc, rc = _rms(gbv * cv)
        dgc_ref[0:1, :] += jnp.sum(dcn * xc, axis=0, keepdims=True)
        dconv = _rms_bwd(xc, rc, gcv_ref[...], dcn)
        dgb_ref[...] = (dconv * cv).astype(BF16)
        dcv = dconv * gbv
        dcv_ref[...] = dcv
        dwsc_ref[0:1, :] += jnp.sum(dcv * u2, axis=0, keepdims=True)
        dwsc_ref[1:2, :] += jnp.sum(dcv * u1, axis=0, keepdims=True)
        dwsc_ref[2:3, :] += jnp.sum(dcv * u, axis=0, keepdims=True)

    row = lambda n: pl.BlockSpec((TM, n), lambda i: (i, 0))
    halo = pl.BlockSpec((SUBLANES, 512), lambda i: (jnp.maximum(i * tb - 1, 0), 0))
    acc = _full((SUBLANES, 512))
    res = pl.pallas_call(
        body, name="mix_out_bwd", grid=(s // TM,),
        out_shape=_class_shapes(s, 512, BF16) + _class_shapes(s, LANES, F32)
        + [jax.ShapeDtypeStruct((s, 512), BF16), jax.ShapeDtypeStruct((s, 512), F32)]
        + [jax.ShapeDtypeStruct((SUBLANES, 512), F32)] * 3,
        in_specs=[row(D_MODEL), _full(w_out.shape), row(512), row(512), row(512), row(512), halo, halo,
                  _full(w_sc.shape), _full(g_a.shape), _full(g_c.shape), ANY_SPEC],
        out_specs=_class_specs(512) + _class_specs(LANES) + [row(512)] * 2 + [acc] * 3,
        scratch_shapes=[pltpu.VMEM((512 // LANES, TM, LANES), F32)],
        compiler_params=_cparams(1),
    )(dx1, w_out, attn, gb, gc, xi, gc, xi, w_sc, g_a, g_c, dep)
    return res[0:3], res[3:6], res[6], res[7], res[8], res[9], res[10]


def _swa_bwd(qc, kc, vc, doc, lsec, ddc, bias, dil, dep):
    nsub, nb, ncls = _swa_steps(qc, dil)
    n128 = nsub * nb
    whole = nb == 1

    def body(q_ref, qn_ref, kp_ref, kc_ref, vp_ref, vc_ref, do_ref, don_ref, lse_ref, lsen_ref, dd_ref, ddn_ref,
             b_ref, dep_ref, dq_ref, dk_ref, dv_ref, db_ref, s_scr, dp_scr, sn_scr, dpn_scr, ds_scr, p_scr, dsn_scr,
             pn_scr):
        r, b = pl.program_id(0), pl.program_id(1)

        @pl.when((r == 0) & (b == 0))
        def _():
            db_ref[...] = jnp.zeros_like(db_ref)

        pairs = [slice(a * LANES, (a + 1) * LANES) for a in range(N_HEADS // 2)]
        blk = [slice(t * WIN, (t + 1) * WIN) for t in range(nsub)]
        last = blk[nsub - 1]
        cols = lambda t: slice(WIN, 2 * WIN) if whole and t == 0 else slice(0, 2 * WIN)
        of_head = lambda ref, c, rows, h: ref[c, rows, _head_lane(h):_head_lane(h) + 1]
        no_prev = (b == 0) & (lax.broadcasted_iota(jnp.int32, (WIN, 2 * WIN), 1) < WIN)

        def keys(prev_ref, cur_ref, c, t, sl):
            if whole and t == 0:
                return cur_ref[c, blk[0], sl]
            if t == 0:
                return jnp.concatenate([prev_ref[c, :, sl], cur_ref[c, blk[0], sl]], axis=0)
            return cur_ref[c, (t - 1) * WIN:(t + 1) * WIN, sl]

        for a, sl in enumerate(pairs):
            for c, t in [(c, t) for c in range(ncls) for t in range(nsub)]:
                k2, v2 = keys(kp_ref, kc_ref, c, t, sl), keys(vp_ref, vc_ref, c, t, sl)
                q_eo = _pair_split(q_ref[c, blk[t], sl])
                do_eo = _pair_split(do_ref[c, blk[t], sl].astype(BF16))
                for e in range(2):
                    s_scr[c * nsub + t, 2 * a + e, :, cols(t)] = _dot_nt(q_eo[e], k2)
                    dp_scr[c * nsub + t, 2 * a + e, :, cols(t)] = _dot_nt(do_eo[e], v2)
            if not whole:
                qn_eo = _pair_split(qn_ref[0, :, sl])
                don_eo = _pair_split(don_ref[0, :, sl].astype(BF16))
                for e in range(2):
                    sn_scr[2 * a + e] = _dot_nt(qn_eo[e], kc_ref[0, last, sl])
                    dpn_scr[2 * a + e] = _dot_nt(don_eo[e], vc_ref[0, last, sl])
        for c, t, h in [(c, t, h) for c in range(ncls) for t in range(nsub) for h in range(N_HEADS)]:
            i, cl = c * nsub + t, cols(t)
            lg = s_scr[i, h, :, cl] + b_ref[h, :, cl]
            if t == 0 and not whole:
                lg = jnp.where(no_prev, -jnp.inf, lg)
            p = jnp.exp(lg - of_head(lse_ref, c, blk[t], h))
            ds = p * (dp_scr[i, h, :, cl] - of_head(dd_ref, c, blk[t], h))
            db_ref[h, :, cl] += ds
            ds_scr[i, h, :, cl] = ds.astype(BF16)
            p_scr[i, h, :, cl] = p.astype(BF16)
        if not whole:
            every = slice(0, WIN)
            for h in range(N_HEADS):
                lgn = jnp.where(b + 1 < nb, sn_scr[h] + b_ref[h, :, :WIN], -jnp.inf)
                pn = jnp.exp(lgn - of_head(lsen_ref, 0, every, h))
                dsn_scr[h] = (pn * (dpn_scr[h] - of_head(ddn_ref, 0, every, h))).astype(BF16)
                pn_scr[h] = pn.astype(BF16)
        for a, sl in enumerate(pairs):
            for c in range(ncls):
                q_eo = [_pair_split(q_ref[c, blk[t], sl]) for t in range(nsub)]
                do_eo = [_pair_split(do_ref[c, blk[t], sl].astype(BF16)) for t in range(nsub)]
                if not whole:
                    q_eo.append(_pair_split(qn_ref[0, :, sl]))
                    do_eo.append(_pair_split(don_ref[0, :, sl].astype(BF16)))
                for t in range(nsub):
                    i = c * nsub + t
                    k_eo = _pair_split(keys(kp_ref, kc_ref, c, t, sl))
                    dq, dk, dv = None, None, None
                    for e in range(2):
                        h = 2 * a + e
                        terms = [_dot(ds_scr[i, h, :, cols(t)], k_eo[e]),
                                 _dot_tn(ds_scr[i, h, :, WIN:], q_eo[t][e]),
                                 _dot_tn(p_scr[i, h, :, WIN:], do_eo[t][e])]
                        if t + 1 < nsub or not whole:
                            ds_next = ds_scr[i + 1, h, :, :WIN] if t + 1 < nsub else dsn_scr[h]
                            p_next = p_scr[i + 1, h, :, :WIN] if t + 1 < nsub else pn_scr[h]
                            terms[1] += _dot_tn(ds_next, q_eo[t + 1][e])
                            terms[2] += _dot_tn(p_next, do_eo[t + 1][e])
                        dq, dk, dv = terms if e == 0 else (dq + terms[0], dk + terms[1], dv + terms[2])
                    dq_ref[c, blk[t], sl] = dq.astype(BF16)
                    dk_ref[c, blk[t], sl] = dk.astype(BF16)
                    dv_ref[c, blk[t], sl] = dv.astype(BF16)

    cur = pl.BlockSpec((ncls, nsub * WIN, 512), lambda r, b: (r, b, 0))
    prev = pl.BlockSpec((ncls, WIN, 512), lambda r, b: (r, jnp.maximum(nsub * b - 1, 0), 0))
    nxt = pl.BlockSpec((ncls, WIN, 512), lambda r, b: (r, jnp.minimum(nsub * b + nsub, n128 - 1), 0))
    cur_h = pl.BlockSpec((ncls, nsub * WIN, LANES), cur.index_map)
    nxt_h = pl.BlockSpec((ncls, WIN, LANES), nxt.index_map)
    wide, narrow = (ncls * nsub, N_HEADS, WIN, 2 * WIN), (N_HEADS, WIN, WIN)
    return pl.pallas_call(
        body, name=f"swa_bwd_d{dil}", grid=(dil // ncls, nb),
        out_shape=[jax.ShapeDtypeStruct(qc.shape, BF16)] * 3 + [jax.ShapeDtypeStruct(bias.shape, F32)],
        in_specs=[cur, nxt, prev, cur, prev, cur, cur, nxt, cur_h, nxt_h, cur_h, nxt_h, _full(bias.shape),
                  ANY_SPEC],
        out_specs=[cur] * 3 + [_full(bias.shape)],
        scratch_shapes=[pltpu.VMEM(wide, F32), pltpu.VMEM(wide, F32), pltpu.VMEM(narrow, F32),
                        pltpu.VMEM(narrow, F32), pltpu.VMEM(wide, BF16), pltpu.VMEM(wide, BF16),
                        pltpu.VMEM(narrow, BF16), pltpu.VMEM(narrow, BF16)],
        compiler_params=_cparams(2),
    )(qc, qc, kc, kc, vc, vc, doc, doc, lsec, lsec, ddc, ddc, bias, dep)


def _in_proj_bwd(dqs, dks, dvs, dgb, dcv, gc, xi, w_sc, w_in_g, x, g_mix, dx1):
    s = x.shape[0]
    tb = TM // SUBLANES
    last = s // SUBLANES - 1
    n_tiles = s // TM

    def body(dq1, dq4, dq16, dk1, dk4, dk16, dv1, dv4, dv16, dgb_ref, dcv_ref, dcvn_ref, gc_ref, xi_ref, wsc_ref,
             w_hbm, x_ref, g_ref, dx1_ref, dproj_ref, gx_ref, dg_ref, scr_a, scr_b, w_scr, w_sems):
        i = pl.program_id(0)
        _load_w_in_pairs(w_hbm, w_scr, w_sems)

        @pl.when(i == 0)
        def _():
            dg_ref[...] = jnp.zeros_like(dg_ref)

        d0 = dcv_ref[...]
        dn = jnp.where(i < n_tiles - 1, dcvn_ref[...], 0.0)
        du = (d0 * wsc_ref[2:3, :] + _shift_up(d0, dn, 1) * wsc_ref[1:2, :]) + _shift_up(d0, dn, 2) * wsc_ref[0:1, :]
        merge = lambda a, b4, b16: ((a[...].astype(F32) + _gather_classes(b4, scr_a, 4))
                                    + _gather_classes(b16, scr_b, 16))
        dq = merge(dq1, dq4, dq16) * (HEAD_DIM ** -0.5)
        dk = merge(dk1, dk4, dk16)
        dv = merge(dv1, dv4, dv16)
        dproj = jnp.concatenate([dq, dk, dv, dgb_ref[...].astype(F32), du * xi_ref[...], du * gc_ref[...]],
                                axis=1).astype(BF16)
        dproj_ref[...] = dproj
        dh = jnp.zeros((TM, D_MODEL), F32)
        for j in range(N_DEV // 2):
            dh = dh + _dot_nt(dproj[:, 2 * j * IN_CHUNK:2 * (j + 1) * IN_CHUNK], w_scr[j])
        xh, r = _rms(x_ref[...])
        dg_ref[0:1, :] += jnp.sum(dh * xh, axis=0, keepdims=True)
        gx_ref[...] = dx1_ref[...] + _rms_bwd(xh, r, g_ref[...], dh)

    row = lambda n: pl.BlockSpec((TM, n), lambda i: (i, 0))
    nxt = pl.BlockSpec((SUBLANES, 512), lambda i: (jnp.minimum((i + 1) * tb, last), 0))
    return pl.pallas_call(
        body, name="in_proj_bwd", grid=(n_tiles,),
        out_shape=[jax.ShapeDtypeStruct((s, IN_COLS), BF16), jax.ShapeDtypeStruct((s, D_MODEL), F32),
                   jax.ShapeDtypeStruct((SUBLANES, D_MODEL), F32)],
        in_specs=_class_specs(512) * 3 + [row(512), row(512), nxt, row(512), row(512), _full(w_sc.shape),
                                          ANY_SPEC, row(D_MODEL), _full(g_mix.shape), row(D_MODEL)],
        out_specs=[row(IN_COLS), row(D_MODEL), _full((SUBLANES, D_MODEL))],
        scratch_shapes=[pltpu.VMEM((512 // LANES, TM, LANES), F32)] * 2 + W_IN_PAIRS,
        compiler_params=_cparams(1),
    )(*dqs, *dks, *dvs, dgb, dcv, dcv, gc, xi, w_sc, w_in_g, x, g_mix, dx1)


def _dw(a, b, dep, name, a_chunked=False, b_chunked=False, n_chunks=1, chunk_cols=None, per_step=1):
    single = not (a_chunked or b_chunked or chunk_cols)
    wide = a_chunked and a.shape[2] > D_MODEL
    ts = TS_DW // 4 if single or wide else TS_DW
    if a_chunked:
        nj, s, kk = a.shape
        nn = b.shape[1]
        a_spec = pl.BlockSpec((1, ts, kk), lambda j, t: (j, t, 0))
        b_spec = pl.BlockSpec((ts, nn), lambda j, t: (t, 0))
    elif b_chunked:
        nj, s, nn = b.shape
        kk = a.shape[1]
        a_spec = pl.BlockSpec((ts, kk), lambda j, t: (t, 0))
        b_spec = pl.BlockSpec((1, ts, nn), lambda j, t: (j, t, 0))
    else:
        s, kk = a.shape
        nj, nn = (n_chunks // per_step, chunk_cols * per_step) if chunk_cols else (1, b.shape[1])
        a_spec = pl.BlockSpec((ts, kk), lambda j, t: (t, 0))
        b_spec = pl.BlockSpec((ts, nn), lambda j, t: (t, j))
    n_steps = s // ts

    def body(a_ref, b_ref, dep_ref, o_ref, acc):
        t = pl.program_id(1)

        @pl.when(t == 0)
        def _():
            acc[...] = jnp.zeros_like(acc)

        av = (a_ref[0] if a_chunked else a_ref[...]).astype(BF16)
        bv = (b_ref[0] if b_chunked else b_ref[...]).astype(BF16)
        acc[...] += _dot_tn(av, bv)

        @pl.when(t == n_steps - 1)
        def _():
            for q in range(per_step):
                o_ref[q] = acc[:, q * nn // per_step:(q + 1) * nn // per_step].astype(BF16)

    return pl.pallas_call(
        body, name=name, grid=(nj, n_steps),
        out_shape=jax.ShapeDtypeStruct((nj * per_step, kk, nn // per_step), BF16),
        in_specs=[a_spec, b_spec, ANY_SPEC],
        out_specs=pl.BlockSpec((per_step, kk, nn // per_step), lambda j, t: (j, 0, 0)),
        scratch_shapes=[pltpu.VMEM((kk, nn), F32)],
        compiler_params=_cparams(2),
    )(a, b, dep)


def _adamw_math(w, g, m, v):
    m2 = ADAM_B1 * m + (1.0 - ADAM_B1) * g
    v2 = ADAM_B2 * v + (1.0 - ADAM_B2) * (g * g)
    m_hat = m2 / (1.0 - ADAM_B1 ** ADAM_STEP)
    v_hat = v2 / (1.0 - ADAM_B2 ** ADAM_STEP)
    delta = -ADAM_LR * (m_hat / (jnp.sqrt(v_hat) + ADAM_EPS) + ADAM_WD * w)
    return delta, m2, v2


def _sum_parts(me, own, p_ref):
    g = None
    for i in range(N_DEV):
        part = jnp.where(me == i, own.astype(F32), p_ref[i].astype(F32))
        g = part if g is None else g + part
    return g


def _adamw_big(name, w, sent, parts, m, v, me_arr):
    rr, cc = w.shape
    tr = rr // 4 if rr >= 512 else rr

    def body(me_ref, w_ref, own_ref, p_ref, m_ref, v_ref, g_ref, d_ref, nm_ref, nv_ref):
        g = own_ref[0].astype(F32)
        for k in range(1, N_DEV):
            g = g + p_ref[(me_ref[0] + k) % N_DEV].astype(F32)
        g_ref[...] = g
        d_ref[...], nm_ref[...], nv_ref[...] = _adamw_math(w_ref[...], g, m_ref[...], v_ref[...])

    row = pl.BlockSpec((tr, cc), lambda i, me: (i, 0))
    return pl.pallas_call(
        body, name=name,
        grid_spec=pltpu.PrefetchScalarGridSpec(
            num_scalar_prefetch=1, grid=(rr // tr,),
            in_specs=[row, pl.BlockSpec((1, tr, cc), lambda i, me: (me[0], i, 0)),
                      pl.BlockSpec((N_DEV, tr, cc), lambda i, me: (0, i, 0)), row, row],
            out_specs=[row] * 4),
        out_shape=[jax.ShapeDtypeStruct((rr, cc), F32)] * 4,
        compiler_params=_cparams(1),
    )(me_arr, w, sent, parts, m, v)


def _small_slices():
    return [
        (slice(ROW_RELB, ROW_RELB + 8), slice(0, N_BUCKETS)),
        (slice(ROW_GMIX, ROW_GMIX + 1), slice(0, D_MODEL)),
        (slice(ROW_GAC, ROW_GAC + 1), slice(0, ATTN_W)),
        (slice(ROW_GAC, ROW_GAC + 1), slice(ATTN_W, D_MODEL)),
        (slice(ROW_GXATTN, ROW_GXATTN + 1), slice(0, D_MODEL)),
        (slice(ROW_GMEM, ROW_GMEM + 1), slice(0, D_MODEL)),
        (slice(ROW_GFFN, ROW_GFFN + 1), slice(0, D_MODEL)),
        (slice(ROW_BFC, ROW_BFC + 8), slice(0, UP_CHUNK)),
        (slice(ROW_GFINAL, ROW_GFINAL + 1), slice(0, D_MODEL)),
    ]


def _adamw_small(own, parts, wmv, me_arr):
    slices = _small_slices()
    n = len(slices)

    def body(*refs):
        me_ref, own_ref, p_ref = refs[:3]
        ins = refs[3:3 + 3 * n]
        g_ref = refs[3 + 3 * n]
        outs = refs[4 + 3 * n:]
        g = _sum_parts(me_ref[0], own_ref[...], p_ref)
        g_ref[...] = g
        for a, (rs, ls) in enumerate(slices):
            ga = g[rs, ls]
            outs[4 * a][...] = ga
            outs[4 * a + 1][...], outs[4 * a + 2][...], outs[4 * a + 3][...] = _adamw_math(
                ins[3 * a][...], ga, ins[3 * a + 1][...], ins[3 * a + 2][...])

    vm = pl.BlockSpec(memory_space=pltpu.VMEM)
    flat = [t for trip in wmv for t in trip]
    out_shape = [jax.ShapeDtypeStruct((SMALL_ROWS, D_MODEL), F32)]
    for w, _, _ in wmv:
        out_shape += [jax.ShapeDtypeStruct(w.shape, F32)] * 4
    res = pl.pallas_call(
        body, name="adamw_small", out_shape=out_shape,
        in_specs=[SMEM_SPEC] + [vm] * (2 + 3 * n), out_specs=[vm] * len(out_shape),
    )(me_arr, own, parts, *flat)
    return res[0], [res[1 + 4 * a:5 + 4 * a] for a in range(n)]


def _adamw_shards(items):
    n = len(items)

    def body(*refs):
        for a in range(n):
            w_ref, g_ref, m_ref, v_ref = refs[4 * a:4 * a + 4]
            d_ref, nm_ref, nv_ref = refs[4 * n + 3 * a:4 * n + 3 * a + 3]
            d_ref[...], nm_ref[...], nv_ref[...] = _adamw_math(w_ref[...], g_ref[...], m_ref[...], v_ref[...])

    vm = pl.BlockSpec(memory_space=pltpu.VMEM)
    out_shape = []
    for w, _, _, _ in items:
        out_shape += [jax.ShapeDtypeStruct(w.shape, F32)] * 3
    res = pl.pallas_call(
        body, name="adamw_shards", out_shape=out_shape, in_specs=[vm] * (4 * n), out_specs=[vm] * (3 * n),
    )(*[t for it in items for t in it])
    return [res[3 * a:3 * a + 3] for a in range(n)]


def _mesh_pos():
    return lax.axis_index("x"), lax.axis_index("y"), lax.axis_index("c")


def _dev_index(p):
    return 4 * p[0] + 2 * p[1] + p[2]


def _all_gather(shards):
    n = len(shards)

    def body(*refs):
        ins, outs = refs[:n], refs[n:2 * n]
        send_sems, recv_sems, loc_sems = refs[2 * n:]
        x, y, c = _mesh_pos()
        me, sib = (x, y, c), (x, y, 1 - c)
        chips = [(1 - x, y), (x, 1 - y), (1 - x, 1 - y)]

        def cp(a, k, block, to, src=None):
            dst = outs[a].at[_dev_index(block)]
            return pltpu.make_async_remote_copy(
                src_ref=dst if src is None else src, dst_ref=dst, send_sem=send_sems.at[a, k],
                recv_sem=recv_sems.at[a, k], device_id=to, device_id_type=MESH)

        mine = [pltpu.make_async_copy(ins[a], outs[a].at[_dev_index(me)], loc_sems.at[a]) for a in range(n)]
        for m_ in mine:
            m_.start()
        first = []
        for a in range(n):
            first.append(cp(a, 0, me, sib, src=ins[a]))
            first += [cp(a, 1 + j, me, (*chip, c), src=ins[a]) for j, chip in enumerate(chips)]
        for f in first:
            f.start()
        passed = []
        for a in range(n):
            for j, chip in enumerate(chips):
                cp(a, 1 + j, (*chip, c), me).wait_recv()
                fwd = cp(a, 4 + j, (*chip, c), sib)
                fwd.start()
                passed.append(fwd)
        for a in range(n):
            cp(a, 0, sib, me).wait_recv()
            for j, chip in enumerate(chips):
                cp(a, 4 + j, (*chip, 1 - c), me).wait_recv()
        for f in first + passed:
            f.wait_send()
        for m_ in mine:
            m_.wait()

    hbm = pl.BlockSpec(memory_space=pltpu.HBM)
    return pl.pallas_call(
        body, name="all_gather_weights",
        out_shape=[jax.ShapeDtypeStruct((N_DEV,) + a.shape, a.dtype) for a in shards],
        in_specs=[hbm] * n, out_specs=[hbm] * n,
        scratch_shapes=[pltpu.SemaphoreType.DMA((n, 7)), pltpu.SemaphoreType.DMA((n, 7)),
                        pltpu.SemaphoreType.DMA((n,))],
    )(*shards)


def _peers():
    x, y, c = _mesh_pos()
    return (x, y, c), [((1 - x) if k & 4 else x, (1 - y) if k & 2 else y, (1 - c) if k & 1 else c)
                       for k in range(1, 8)]


def _exchange_copy(src_ref, land_ref, whole, send_sems, recv_sems, a, k, peer, slot):
    src = src_ref if whole else src_ref.at[_dev_index(peer)]
    return pltpu.make_async_remote_copy(
        src_ref=src, dst_ref=land_ref.at[slot], send_sem=send_sems.at[7 * a + k], recv_sem=recv_sems.at[7 * a + k],
        device_id=peer, device_id_type=MESH)


def _exchange_start(name, srcs, whole, dep):
    n = len(srcs)
    lands = [lax.empty(((N_DEV,) + s.shape) if w else s.shape, s.dtype) for s, w in zip(srcs, whole)]

    def body(*refs):
        src_refs, land_refs = refs[:n], refs[n:2 * n]
        send_sems, recv_sems, token = refs[2 * n + 1], refs[2 * n + 2], refs[-1]
        me, peers = _peers()
        for a in range(n):
            for k, peer in enumerate(peers):
                _exchange_copy(src_refs[a], land_refs[a], whole[a], send_sems, recv_sems, a, k, peer,
                               _dev_index(me)).start()
        token[...] = jnp.zeros_like(token)

    res = pl.pallas_call(
        body, name=name,
        out_shape=(pltpu.SemaphoreType.DMA((7 * n,)), pltpu.SemaphoreType.DMA((7 * n,)),
                   *[pltpu.HBM(a.shape, a.dtype) for a in srcs], *[pltpu.HBM(a.shape, a.dtype) for a in lands],
                   jax.ShapeDtypeStruct((SUBLANES, 128), F32)),
        in_specs=[HBM_SPEC] * (2 * n) + [ANY_SPEC],
        out_specs=(SEM_SPEC, SEM_SPEC, *([HBM_SPEC] * (2 * n)), VMEM_SPEC),
        input_output_aliases={i: 2 + i for i in range(2 * n)},
        compiler_params=pltpu.CompilerParams(has_side_effects=DATAFLOW),
    )(*[pltpu.with_memory_space_constraint(a, pltpu.HBM) for a in srcs],
      *[pltpu.with_memory_space_constraint(a, pltpu.HBM) for a in lands], dep)
    return res[0], res[1], list(res[2:2 + n]), list(res[2 + n:2 + 2 * n]), res[-1]


def _exchange_wait(name, started, whole, after, which=None):
    send_sems, recv_sems, srcs, lands, _ = started
    which = list(range(len(srcs))) if which is None else which
    srcs, lands = [srcs[a] for a in which], [lands[a] for a in which]
    n = len(srcs)

    def body(*refs):
        src_refs, land_refs = refs[:n], refs[n:2 * n]
        send_sems, recv_sems = refs[2 * n], refs[2 * n + 1]
        _, peers = _peers()
        for i, a in enumerate(which):
            for k, peer in enumerate(peers):
                cp = _exchange_copy(src_refs[i], land_refs[i], whole[a], send_sems, recv_sems, a, k, peer,
                                    _dev_index(peer))
                cp.wait_send()
                cp.wait_recv()

    res = pl.pallas_call(
        body, name=name,
        out_shape=[pltpu.HBM(a.shape, a.dtype) for a in srcs + lands],
        in_specs=[HBM_SPEC] * (2 * n) + [SEM_SPEC, SEM_SPEC, ANY_SPEC],
        out_specs=[HBM_SPEC] * (2 * n),
        input_output_aliases={i: i for i in range(2 * n)},
        compiler_params=pltpu.CompilerParams(has_side_effects=DATAFLOW),
    )(*srcs, *lands, send_sems, recv_sems, after)
    return list(res[:n]), list(res[n:])


def _gather_start(name, shards, dep):
    n = len(shards)
    lands = [lax.empty((N_DEV,) + a.shape, a.dtype) for a in shards]

    def body(*refs):
        src_refs, land_refs = refs[:n], refs[n:2 * n]
        send_sems, recv_sems, token = refs[2 * n + 1], refs[2 * n + 2], refs[-1]
        x, y, c = _mesh_pos()
        peers = [(x, y, 1 - c), (1 - x, y, c), (x, 1 - y, c), (1 - x, 1 - y, c)]
        for a in range(n):
            for k, peer in enumerate(peers):
                pltpu.make_async_remote_copy(
                    src_ref=src_refs[a], dst_ref=land_refs[a].at[_dev_index((x, y, c))], send_sem=send_sems.at[4 * a + k],
                    recv_sem=recv_sems.at[4 * a + k], device_id=peer, device_id_type=MESH).start()
        token[...] = jnp.zeros_like(token)

    res = pl.pallas_call(
        body, name=name,
        out_shape=(pltpu.SemaphoreType.DMA((4 * n,)), pltpu.SemaphoreType.DMA((4 * n,)),
                   *[pltpu.HBM(a.shape, a.dtype) for a in shards], *[pltpu.HBM(a.shape, a.dtype) for a in lands],
                   jax.ShapeDtypeStruct((SUBLANES, 128), F32)),
        in_specs=[HBM_SPEC] * (2 * n) + [ANY_SPEC],
        out_specs=(SEM_SPEC, SEM_SPEC, *([HBM_SPEC] * (2 * n)), VMEM_SPEC),
        input_output_aliases={i: 2 + i for i in range(2 * n)},
        compiler_params=pltpu.CompilerParams(has_side_effects=DATAFLOW),
    )(*[pltpu.with_memory_space_constraint(a, pltpu.HBM) for a in shards],
      *[pltpu.with_memory_space_constraint(a, pltpu.HBM) for a in lands], dep)
    return res[0], res[1], list(res[2:2 + n]), list(res[2 + n:2 + 2 * n]), res[-1]


def _gather_forward(name, send_sems, recv_sems, lands, which, after):
    n = len(which)

    def body(*refs):
        land_refs = refs[:n]
        send_sems, recv_sems = refs[n], refs[n + 1]
        fsend, frecv, token = refs[n + 3], refs[n + 4], refs[-1]
        x, y, c = _mesh_pos()
        chips = [(1 - x, y), (x, 1 - y), (1 - x, 1 - y)]
        for i, a in enumerate(which):
            for j, chip in enumerate(chips):
                block = land_refs[i].at[_dev_index((*chip, c))]
                pltpu.make_async_remote_copy(
                    src_ref=block, dst_ref=block, send_sem=send_sems.at[4 * a + 1 + j], recv_sem=recv_sems.at[4 * a + 1 + j],
                    device_id=(*chip, c), device_id_type=MESH).wait_recv()
                pltpu.make_async_remote_copy(
                    src_ref=block, dst_ref=block, send_sem=fsend.at[3 * i + j], recv_sem=frecv.at[3 * i + j],
                    device_id=(x, y, 1 - c), device_id_type=MESH).start()
        token[...] = jnp.zeros_like(token)

    res = pl.pallas_call(
        body, name=name,
        out_shape=(pltpu.SemaphoreType.DMA((3 * n,)), pltpu.SemaphoreType.DMA((3 * n,)),
                   *[pltpu.HBM(a.shape, a.dtype) for a in lands], jax.ShapeDtypeStruct((SUBLANES, 128), F32)),
        in_specs=[HBM_SPEC] * n + [SEM_SPEC, SEM_SPEC, ANY_SPEC],
        out_specs=(SEM_SPEC, SEM_SPEC, *([HBM_SPEC] * n), VMEM_SPEC),
        input_output_aliases={i: 2 + i for i in range(n)},
        compiler_params=pltpu.CompilerParams(has_side_effects=DATAFLOW),
    )(*lands, send_sems, recv_sems, after)
    return res[0], res[1], list(res[2:2 + n]), res[-1]


def _gather_wait(name, send_sems, recv_sems, fsend, frecv, srcs, lands, which, after):
    n = len(which)

    def body(*refs):
        land_refs = refs[n:2 * n]
        send_sems, recv_sems, fsend, frecv = refs[2 * n:2 * n + 4]
        x, y, c = _mesh_pos()
        sib = (x, y, 1 - c)
        chips = [(1 - x, y), (x, 1 - y), (1 - x, 1 - y)]
        for i, a in enumerate(which):
            def cp(slot, ssem, rsem):
                block = land_refs[i].at[_dev_index(slot)]
                return pltpu.make_async_remote_copy(src_ref=block, dst_ref=block, send_sem=ssem, recv_sem=rsem,
                                                    device_id=sib, device_id_type=MESH)
            cp(sib, send_sems.at[4 * a], recv_sems.at[4 * a]).wait_recv()
            for j, chip in enumerate(chips):
                cp((*chip, 1 - c), fsend.at[3 * i + j], frecv.at[3 * i + j]).wait_recv()
            for k in range(4):
                cp(sib, send_sems.at[4 * a + k], recv_sems.at[4 * a + k]).wait_send()
            for j in range(3):
                cp(sib, fsend.at[3 * i + j], frecv.at[3 * i + j]).wait_send()

    res = pl.pallas_call(
        body, name=name,
        out_shape=[pltpu.HBM(a.shape, a.dtype) for a in srcs + lands],
        in_specs=[HBM_SPEC] * (2 * n) + [SEM_SPEC] * 4 + [ANY_SPEC],
        out_specs=[HBM_SPEC] * (2 * n),
        input_output_aliases={i: i for i in range(2 * n)},
        compiler_params=pltpu.CompilerParams(has_side_effects=DATAFLOW),
    )(*srcs, *lands, send_sems, recv_sems, fsend, frecv, after)
    return list(res[n:])


def _local_step(x, mem, target, rel_bias, g_mix, w_in_g, w_sc, g_a, g_c, g_xattn, g_mem, g_ffn, w_fc, b_fc, g_final,
                dep, forward_weights, late_weights, emit, emit_small):
    s = x.shape[0]
    buckets = _bucket_tables()
    bias = _bias_fwd(rel_bias, buckets)

    h1, qs, ks, vs, gb, gc, xi = _rms_proj(x, g_mix, w_in_g, dep)
    qs, ks, vs = ([a[0][None]] + list(a[1:]) for a in (qs, ks, vs))
    group1, group2 = ["w_out", "w_xq", "w_xk", "w_xv", "w_xo"], ["w_up", "w_down"]
    tok = forward_weights(group1, h1)
    branches = []
    for p, dil in enumerate(DILATIONS):
        o_p, lse_p = _swa_fwd(qs[p], ks[p], vs[p], bias[p], dil, tok)
        branches.append([o_p[0], lse_p[0]] if dil == 1 else [o_p, lse_p])
    lw = late_weights(group1, branches[-1][0])
    w_out, w_xq, w_xk, w_xv, w_xo = (lw[n] for n in group1)
    attn, lses, mixed, x1 = _mix_out(branches, gb, gc, xi, x, w_sc, g_a, g_c, w_out)
    tok = forward_weights(group2, x1)
    mem_n, mk, mv = _mem_kv(mem, g_mem, w_xk, w_xv)
    h2, xq, xo, x2 = _xattn_fwd(x1, g_xattn, w_xq, mk, mv, w_xo, tok)
    lw = late_weights(group2, x2)
    w_up_g = lw["w_up"].reshape(FFN_CHUNKS, FFN_WIDTH, D_MODEL)
    w_down_g = lw["w_down"].reshape(FFN_CHUNKS // 2, FFN_WIDTH, D_MODEL)
    pairs = lambda a: a.reshape(FFN_CHUNKS, 2, a.shape[1], UP_CHUNK).transpose(0, 2, 1, 3).reshape(
        FFN_CHUNKS, a.shape[1], FFN_WIDTH)
    w_fc, b_fc = pairs(w_fc), pairs(b_fc)
    h3, up, conv, act, dx3, loss_acc, dg_final = _ffn_fwd(x2, g_ffn, w_up_g, w_fc, b_fc, w_down_g, g_final, target)

    gw_down = _dw(act, dx3, dep, "dw_down", a_chunked=True).reshape(N_DEV // 2, UP_CHUNK, D_MODEL)
    dup, dx2, dg_ffn, dw_fc, db_fc = _ffn_bwd(dx3, up, conv, x2, g_ffn, w_up_g, w_fc, w_down_g)
    gw_up = _dw(dup, h3, dep, "dw_up", a_chunked=True).reshape(N_DEV, UP_CHUNK, D_MODEL)
    tok = emit(dict(w_down=gw_down, w_up=gw_up))
    dxq, dx1, dmk, dmv, dg_xattn = _xattn_bwd(dx2, xo, xq, mk, mv, w_xo, w_xq, x1, g_xattn, tok)
    gw_xo = _dw(xo, dx2, tok, "dw_xo")[0]
    gw_xq = _dw(h2, dxq, tok, "dw_xq")[0]
    gw_xk, gw_xv, dg_mem = _mem_kv_bwd(dmk, dmv, mem_n, mem, w_xk, w_xv)
    tok = emit(dict(w_xo=gw_xo, w_xq=gw_xq, w_xk=gw_xk, w_xv=gw_xv))
    dattns, dds, dgb, dcv, dg_a, dg_c, dw_sc = _mix_out_bwd(dx1, w_out, attn, gb, gc, xi, w_sc, g_a, g_c, tok)
    first = lambda a: [a[0][None]] + list(a[1:])
    dattns, dds, lses = first(dattns), first(dds), first(lses)
    gw_out = _dw(mixed, dx1, tok, "dw_out")[0]
    tok = emit(dict(w_out=gw_out))
    dqs, dks, dvs, dbias = [], [], [], []
    for p, dil in enumerate(DILATIONS):
        dq_p, dk_p, dv_p, db_p = _swa_bwd(qs[p], ks[p], vs[p], dattns[p], lses[p], dds[p], bias[p], dil, tok)
        dqs.append(dq_p[0] if dil == 1 else dq_p)
        dks.append(dk_p[0] if dil == 1 else dk_p)
        dvs.append(dv_p[0] if dil == 1 else dv_p)
        dbias.append(db_p)
    d_relb = _bias_bwd(jnp.stack(dbias), buckets)
    dproj, grad_x, dg_mix = _in_proj_bwd(dqs, dks, dvs, dgb, dcv, gc, xi, w_sc, w_in_g, x, g_mix, dx1)
    pad = lambda a: jnp.pad(a, ((0, 0), (0, D_MODEL - a.shape[1])))
    small = jnp.concatenate([
        d_relb, dg_mix, dg_xattn, dg_mem, dg_ffn, dg_final, jnp.concatenate([dg_a, dg_c], axis=1),
        pad(dw_sc), pad(db_fc.reshape(N_DEV, UP_CHUNK)), pad(dw_fc.reshape(3 * N_DEV, UP_CHUNK)), pad(loss_acc)],
        axis=0)
    tok = emit_small(small)
    gw_in = _dw(h1, dproj, tok, "dw_in", n_chunks=N_DEV, chunk_cols=IN_CHUNK, per_step=2)
    emit(dict(w_in=gw_in))
    return grad_x


def kernel(x, mem, rel_bias, g_mix, w_in, w_short_conv, g_attn_out, g_conv_out, w_out, g_xattn, g_mem, w_xq, w_xk, w_xv, w_xo, g_ffn, w_up, w_ffn_conv, b_ffn_conv, w_down, g_final, loss_target, m_rel_bias, m_g_mix, m_w_in, m_w_short_conv, m_g_attn_out, m_g_conv_out, m_w_out, m_g_xattn, m_g_mem, m_w_xq, m_w_xk, m_w_xv, m_w_xo, m_g_ffn, m_w_up, m_w_ffn_conv, m_b_ffn_conv, m_w_down, m_g_final, v_rel_bias, v_g_mix, v_w_in, v_w_short_conv, v_g_attn_out, v_g_conv_out, v_w_out, v_g_xattn, v_g_mem, v_w_xq, v_w_xk, v_w_xv, v_w_xo, v_g_ffn, v_w_up, v_w_ffn_conv, v_b_ffn_conv, v_w_down, v_g_final):
    me = _dev_index(_mesh_pos())
    me_arr = me.reshape(1).astype(jnp.int32)

    big_names = ["w_in", "w_out", "w_xq", "w_xk", "w_xv", "w_xo", "w_up", "w_down"]
    late_names = big_names[1:]
    big_w = dict(w_in=w_in[0], w_out=w_out[0], w_xq=w_xq[0], w_xk=w_xk[0], w_xv=w_xv[0], w_xo=w_xo[0],
                 w_up=w_up[0].T, w_down=w_down[0])
    big_m = dict(w_in=m_w_in[0], w_out=m_w_out[0], w_xq=m_w_xq[0], w_xk=m_w_xk[0], w_xv=m_w_xv[0], w_xo=m_w_xo[0],
                 w_up=m_w_up[0].T, w_down=m_w_down[0])
    big_v = dict(w_in=v_w_in[0], w_out=v_w_out[0], w_xq=v_w_xq[0], w_xk=v_w_xk[0], w_xv=v_w_xv[0], w_xo=v_w_xo[0],
                 w_up=v_w_up[0].T, w_down=v_w_down[0])
    shard_shape = {n: big_w[n].shape for n in big_names}

    w_in_g, w_sc_g, w_fc_full = _all_gather([big_w["w_in"].astype(BF16), w_short_conv[0], w_ffn_conv[0]])
    w_sc_full = w_sc_g.transpose(1, 0, 2).reshape(3, CONV_W)
    late_shards = [big_w[n].astype(BF16) for n in late_names]
    ag_send, ag_recv, ag_srcs, ag_lands, ag_token = _gather_start("gather_weights_start", late_shards, w_in_g)
    forwarded = {}

    def forward_weights(names, after):
        which = [late_names.index(n) for n in names]
        fsend, frecv, lands, token = _gather_forward("gather_" + "_".join(names) + "_forward", ag_send, ag_recv,
                                                     [ag_lands[a] for a in which], which, after)
        forwarded[tuple(names)] = (fsend, frecv, lands)
        return token

    def late_weights(names, after):
        which = [late_names.index(n) for n in names]
        fsend, frecv, lands = forwarded[tuple(names)]
        lands = _gather_wait("gather_" + "_".join(names) + "_wait", ag_send, ag_recv, fsend, frecv,
                             [ag_srcs[a] for a in which], lands, which, after)
        out = {}
        for n, a, land in zip(names, which, lands):
            full = lax.dynamic_update_index_in_dim(land, late_shards[a], me, 0)
            if n == "w_up":
                out[n] = full
            elif n == "w_down":
                out[n] = full.reshape(N_DEV // 2, UP_CHUNK, D_MODEL)
            else:
                out[n] = full.reshape(D_MODEL, D_MODEL)
        return out

    sent = []

    def emit(grads):
        names = list(grads)
        blocks = [grads[n].reshape((N_DEV,) + shard_shape[n]) for n in names]
        started = _exchange_start("scatter_" + "_".join(names) + "_start", blocks, [False] * len(names), me_arr)
        sent.append((names, started))
        return started[-1]

    def emit_small(small):
        sent_small.append((small, _exchange_start("gather_small_start", [small], [True], me_arr)))
        return sent_small[0][1][-1]

    sent_small = []
    grad_x = _local_step(
        x[0], mem[0], loss_target[0], rel_bias, g_mix, w_in_g, w_sc_full, g_attn_out, g_conv_out, g_xattn, g_mem,
        g_ffn, w_fc_full, b_ffn_conv.reshape(N_DEV, 1, UP_CHUNK), g_final.reshape(1, D_MODEL), ag_token,
        forward_weights, late_weights, emit, emit_small)

    small_g, small_started = sent_small[0]
    after = sent[-1][1][-1]
    small_parts = _exchange_wait("gather_small_wait", small_started, [True], after)[1][0]
    big_out = {}
    after = small_parts
    for names, started in sent:
        blocks, lands = _exchange_wait("scatter_" + "_".join(names) + "_wait", started, [False] * len(names), after)
        for n, block, land in zip(names, blocks, lands):
            res = _adamw_big("adamw_" + n, big_w[n], block, land, big_m[n], big_v[n], me_arr)
            big_out[n] = [(r.T if n == "w_up" else r)[None] for r in res]
            after = res[0]

    as_rows = lambda a: a.reshape(N_DEV, UP_CHUNK)
    row1 = lambda a: a.reshape(1, D_MODEL)
    small_names = ["rel_bias", "g_mix", "g_attn_out", "g_conv_out", "g_xattn", "g_mem", "g_ffn", "b_ffn_conv", "g_final"]
    wmv = [
        (rel_bias, m_rel_bias, v_rel_bias), (g_mix, m_g_mix, v_g_mix), (g_attn_out, m_g_attn_out, v_g_attn_out),
        (g_conv_out, m_g_conv_out, v_g_conv_out), (g_xattn, m_g_xattn, v_g_xattn), (g_mem, m_g_mem, v_g_mem),
        (g_ffn, m_g_ffn, v_g_ffn), (as_rows(b_ffn_conv), as_rows(m_b_ffn_conv), as_rows(v_b_ffn_conv)),
        (row1(g_final), row1(m_g_final), row1(v_g_final))]
    g_packed, small_res = _adamw_small(small_g, small_parts, wmv, me_arr)
    small_out = dict(zip(small_names, small_res))
    loss = g_packed[ROW_LOSS, 0]
    small_out["b_ffn_conv"] = [a.reshape(1, 2 * D_FF) for a in small_out["b_ffn_conv"]]
    small_out["g_final"] = [a.reshape(D_MODEL) for a in small_out["g_final"]]

    g_wsc = lax.dynamic_slice(g_packed[ROW_WSC:ROW_WSC + 3, 0:CONV_W], (0, me * HEAD_DIM), (3, HEAD_DIM))
    g_wfc = lax.dynamic_slice(g_packed[ROW_WFC:ROW_WFC + 3 * N_DEV, 0:UP_CHUNK].reshape(3, N_DEV, UP_CHUNK),
                              (0, me, 0), (3, 1, UP_CHUNK)).reshape(3, UP_CHUNK)
    shard_res = _adamw_shards([(w_short_conv[0], g_wsc, m_w_short_conv[0], v_w_short_conv[0]),
                               (w_ffn_conv[0], g_wfc, m_w_ffn_conv[0], v_w_ffn_conv[0])])
    small_out["w_short_conv"] = [g_wsc[None]] + [a[None] for a in shard_res[0]]
    small_out["w_ffn_conv"] = [g_wfc[None]] + [a[None] for a in shard_res[1]]

    order = ["rel_bias", "g_mix", "w_in", "w_short_conv", "g_attn_out", "g_conv_out", "w_out", "g_xattn", "g_mem",
             "w_xq", "w_xk", "w_xv", "w_xo", "g_ffn", "w_up", "w_ffn_conv", "b_ffn_conv", "w_down", "g_final"]
    allp = {**big_out, **small_out}
    outs = [loss, grad_x[None]]
    for kind in range(4):
        outs += [allp[n][kind] for n in order]
    return tuple(outs)
```

```python
import math

import numpy as np
import jax
import jax.numpy as jnp
from jax import lax
from jax.experimental import pallas as pl
from jax.experimental.pallas import tpu as pltpu

F32 = jnp.float32
BF16 = jnp.bfloat16
MESH = pl.DeviceIdType.MESH

N_DEV = 8
D_MODEL = 1024
ATTN_W = 512
CONV_W = 512
N_HEADS = 8
HEAD_DIM = 64
WIN = 128
DILATIONS = (1, 4, 16)
N_BUCKETS = 32
BUCKET_MAX_EXACT = 16
BUCKET_MAX_DISTANCE = 2048
N_MEM_HEADS = 4
MEM_HEAD_DIM = 256
D_FF = 2816
IN_COLS = 3072
IN_CHUNK = IN_COLS // N_DEV
UP_CHUNK = 2 * D_FF // N_DEV
FFN_CHUNKS = 4
FFN_WIDTH = 2 * D_FF // FFN_CHUNKS
EPS = 1e-6

ADAM_LR = 0.001
ADAM_B1 = 0.9
ADAM_B2 = 0.999
ADAM_EPS = 1e-08
ADAM_WD = 0.01
ADAM_STEP = 10

SUBLANES = 8
LANES = 128
HALO = 16
TM = 512
TM_XATTN = 1024
TM_FFN = 256
TS_DW = 4096
SWA_BLOCKS = 8
VMEM_LIMIT = 56 * 1024 * 1024

ROW_RELB, ROW_GMIX, ROW_GXATTN, ROW_GMEM, ROW_GFFN, ROW_GFINAL, ROW_GAC = 0, 8, 16, 24, 32, 40, 48
ROW_WSC, ROW_BFC, ROW_WFC, ROW_LOSS, SMALL_ROWS = 56, 64, 72, 96, 104


def _cparams(n_grid):
    return pltpu.CompilerParams(dimension_semantics=("arbitrary",) * n_grid, vmem_limit_bytes=VMEM_LIMIT)


def _full(shape):
    nd = len(shape)
    return pl.BlockSpec(tuple(shape), lambda *_: (0,) * nd)


def _resident(shape):
    nd = len(shape)
    return pl.BlockSpec(tuple(shape), lambda *_: (0,) * nd, pipeline_mode=pl.Buffered(1))


ANY_SPEC = pl.BlockSpec(memory_space=pl.ANY)
HBM_SPEC = pl.BlockSpec(memory_space=pltpu.HBM)
SEM_SPEC = pl.BlockSpec(memory_space=pltpu.SEMAPHORE)
VMEM_SPEC = pl.BlockSpec(memory_space=pltpu.VMEM)
SMEM_SPEC = pl.BlockSpec(memory_space=pltpu.SMEM)
DATAFLOW = pltpu.SideEffectType.DATAFLOW_SIDE_EFFECTING


def _rms(x):
    r = lax.rsqrt(jnp.mean(x * x, axis=-1, keepdims=True) + EPS)
    return x * r, r


def _rms_bwd(xh, r, g, dy):
    dxh = dy * g
    return r * (dxh - xh * jnp.mean(dxh * xh, axis=-1, keepdims=True))


def _shift_down(u, halo, k):
    ru = pltpu.roll(u, k, 0)
    rh = pltpu.roll(halo, k, 0)
    row = lax.broadcasted_iota(jnp.int32, rh.shape, 0)
    head = jnp.where(row < k, rh, ru[0:SUBLANES])
    return jnp.concatenate([head, ru[SUBLANES:]], axis=0)


def _shift_up(u, halo, k):
    tm = u.shape[0]
    ru = pltpu.roll(u, tm - k, 0)
    rh = pltpu.roll(halo, SUBLANES - k, 0)
    row = lax.broadcasted_iota(jnp.int32, rh.shape, 0)
    tail = jnp.where(row >= SUBLANES - k, rh, ru[tm - SUBLANES:])
    return jnp.concatenate([ru[:tm - SUBLANES], tail], axis=0)


def _causal_conv3(u, halo, w_ref):
    return (_shift_down(u, halo, 2) * w_ref[0:1, :] + _shift_down(u, halo, 1) * w_ref[1:2, :]) + u * w_ref[2:3, :]


def _dot(a, b):
    return jnp.dot(a, b, preferred_element_type=F32)


def _dot_nt(a, b):
    return lax.dot_general(a, b, (((1,), (1,)), ((), ())), preferred_element_type=F32)


def _dot_tn(a, b):
    return lax.dot_general(a, b, (((0,), (0,)), ((), ())), preferred_element_type=F32)


def _sigmoid(x):
    return 0.5 * jnp.tanh(0.5 * x) + 0.5


def _bucket_tables():
    qi = np.arange(WIN)[:, None]
    kj = np.arange(2 * WIN)[None, :]
    steps = np.clip(qi + WIN - kj, 0, WIN)
    out = []
    for d in DILATIONS:
        dist = steps * d
        dd = np.maximum(dist, 1).astype(np.float32)
        large = BUCKET_MAX_EXACT + (
            np.log(dd / np.float32(BUCKET_MAX_EXACT)) / np.float32(math.log(BUCKET_MAX_DISTANCE / BUCKET_MAX_EXACT))
            * np.float32(N_BUCKETS - BUCKET_MAX_EXACT)).astype(np.int32)
        large = np.minimum(large, N_BUCKETS - 1)
        out.append(np.where(dist < BUCKET_MAX_EXACT, dist, large).astype(np.int32))
    return np.stack(out)


def _band_mask():
    qi = lax.broadcasted_iota(jnp.int32, (WIN, 2 * WIN), 0)
    kj = lax.broadcasted_iota(jnp.int32, (WIN, 2 * WIN), 1)
    steps = qi + WIN - kj
    return (steps >= 0) & (steps <= WIN)


def _bias_fwd(rel_bias, buckets):
    present = [sorted(set(buckets[p].ravel().tolist())) for p in range(3)]

    def body(rb_ref, bk_ref, o_ref):
        band = _band_mask()
        for p in range(3):
            bk = bk_ref[p]
            for h in range(N_HEADS):
                acc = jnp.zeros((WIN, 2 * WIN), F32)
                for b in present[p]:
                    acc = jnp.where(bk == b, rb_ref[h, b], acc)
                o_ref[p, h] = jnp.where(band, acc, -jnp.inf)

    return pl.pallas_call(
        body, name="bias_fwd",
        out_shape=jax.ShapeDtypeStruct((3, N_HEADS, WIN, 2 * WIN), F32),
        in_specs=[pl.BlockSpec(memory_space=pltpu.SMEM), pl.BlockSpec(memory_space=pltpu.VMEM)],
        out_specs=pl.BlockSpec(memory_space=pltpu.VMEM),
    )(rel_bias, jnp.asarray(buckets))


def _bias_bwd(dbias, buckets):
    present = [set(buckets[p].ravel().tolist()) for p in range(3)]

    def body(db_ref, bk_ref, o_ref):
        lane = lax.broadcasted_iota(jnp.int32, (1, D_MODEL), 1)
        rows = []
        for h in range(N_HEADS):
            row = jnp.zeros((1, D_MODEL), F32)
            for b in range(N_BUCKETS):
                tot = jnp.zeros((1, 1), F32)
                for p in (p for p in range(3) if b in present[p]):
                    sel = jnp.where(bk_ref[p] == b, db_ref[p, h], 0.0)
                    tot = tot + jnp.sum(jnp.sum(sel, axis=0, keepdims=True), axis=1, keepdims=True)
                row = jnp.where(lane == b, tot, row)
            rows.append(row)
        o_ref[...] = jnp.concatenate(rows, axis=0)

    return pl.pallas_call(
        body, name="bias_bwd",
        out_shape=jax.ShapeDtypeStruct((N_HEADS, D_MODEL), F32),
        in_specs=[pl.BlockSpec(memory_space=pltpu.VMEM), pl.BlockSpec(memory_space=pltpu.VMEM)],
        out_specs=pl.BlockSpec(memory_space=pltpu.VMEM),
    )(dbias, jnp.asarray(buckets))


def _spread(val, scr_ref, out_refs, dtype):
    out_refs[0][...] = val.astype(dtype)
    n_blk = val.shape[1] // LANES
    for c in range(n_blk):
        scr_ref[c] = val[:, c * LANES:(c + 1) * LANES]
    for o_ref, d in zip(out_refs[1:], DILATIONS[1:]):
        for r in range(d):
            for c in range(n_blk):
                o_ref[r, :, c * LANES:(c + 1) * LANES] = scr_ref.at[c][pl.ds(r, TM // d, stride=d), :].astype(dtype)


HEAD_LANES = LANES // N_HEADS


def _head_lane(h):
    return HEAD_LANES * (h // 2) + (LANES // 2) * (h % 2)


def _narrow_heads(x):
    grp = (lax.broadcasted_iota(jnp.int32, (x.shape[0], LANES), 1) // HEAD_LANES) % (N_HEADS // 2)
    out = x[:, 0:LANES]
    for t in range(1, N_HEADS // 2):
        out = jnp.where(grp == t, x[:, t * LANES:(t + 1) * LANES], out)
    return out


def _gather_classes(blk_ref, scr_ref, d):
    n_blk = blk_ref.shape[2] // LANES
    for r in range(d):
        for c in range(n_blk):
            scr_ref.at[c][pl.ds(r, TM // d, stride=d), :] = blk_ref[r, :, c * LANES:(c + 1) * LANES].astype(F32)
    return jnp.concatenate([scr_ref[c] for c in range(n_blk)], axis=1)


def _class_specs(cols):
    return [pl.BlockSpec((TM, cols), lambda i: (i, 0))] + [
        pl.BlockSpec((d, TM // d, cols), lambda i: (0, i, 0)) for d in DILATIONS[1:]]


def _class_shapes(s, cols, dtype):
    return [jax.ShapeDtypeStruct((s, cols), dtype)] + [
        jax.ShapeDtypeStruct((d, s // d, cols), dtype) for d in DILATIONS[1:]]


def _load_w_in_pairs(w_hbm, w_scr, sems):
    @pl.when(pl.program_id(0) == 0)
    def _():
        copies = [pltpu.make_async_copy(w_hbm.at[j], w_scr.at[j // 2, :, pl.ds((j % 2) * IN_CHUNK, IN_CHUNK)],
                                        sems.at[j]) for j in range(N_DEV)]
        for copy in copies:
            copy.start()
        for copy in copies:
            copy.wait()


W_IN_PAIRS = [pltpu.VMEM((N_DEV // 2, D_MODEL, 2 * IN_CHUNK), BF16), pltpu.SemaphoreType.DMA((N_DEV,))]


def _rms_proj(x, g_mix, w_in_g, dep):
    s = x.shape[0]

    def body(x_ref, g_ref, w_hbm, dep_ref, h_ref, q1, q4, q16, k1, k4, k16, v1, v4, v16, gb_ref, gc_ref, xi_ref, scr,
             w_scr, w_sems):
        _load_w_in_pairs(w_hbm, w_scr, w_sems)
        xh, _ = _rms(x_ref[...])
        h = (xh * g_ref[...]).astype(BF16)
        h_ref[...] = h
        proj = jnp.concatenate([_dot(h, w_scr[j]) for j in range(N_DEV // 2)], axis=1)
        _spread(proj[:, 0:512] * (HEAD_DIM ** -0.5), scr, (q1, q4, q16), BF16)
        _spread(proj[:, 512:1024], scr, (k1, k4, k16), BF16)
        _spread(proj[:, 1024:1536], scr, (v1, v4, v16), BF16)
        gb_ref[...] = proj[:, 1536:2048]
        gc_ref[...] = proj[:, 2048:2560]
        xi_ref[...] = proj[:, 2560:3072]

    row = lambda n: pl.BlockSpec((TM, n), lambda i: (i, 0))
    res = pl.pallas_call(
        body, name="rms_proj", grid=(s // TM,),
        out_shape=[jax.ShapeDtypeStruct((s, D_MODEL), BF16)] + _class_shapes(s, 512, BF16) * 3
        + [jax.ShapeDtypeStruct((s, 512), F32)] * 3,
        in_specs=[row(D_MODEL), _full(g_mix.shape), ANY_SPEC, ANY_SPEC],
        out_specs=[row(D_MODEL)] + _class_specs(512) * 3 + [row(512)] * 3,
        scratch_shapes=[pltpu.VMEM((512 // LANES, TM, LANES), F32)] + W_IN_PAIRS,
        compiler_params=_cparams(1),
    )(x, g_mix, w_in_g, dep)
    return res[0], res[1:4], res[4:7], res[7:10], res[10], res[11], res[12]


def _pair_split(x2):
    lane = lax.broadcasted_iota(jnp.int32, x2.shape, 1)
    zero = jnp.zeros_like(x2)
    return jnp.where(lane < HEAD_DIM, x2, zero), jnp.where(lane >= HEAD_DIM, x2, zero)


def _pair_join(even, odd):
    lane = lax.broadcasted_iota(jnp.int32, (even.shape[0], LANES), 1)
    return jnp.where(lane < HEAD_DIM, even, odd)


def _swa_steps(qc, dil):
    n128 = qc.shape[1] // WIN
    nsub = min(SWA_BLOCKS, n128)
    nb = n128 // nsub
    ncls = min(dil, SWA_BLOCKS // nsub) if nb == 1 else 1
    return nsub, nb, ncls


def _swa_fwd(qc, kc, vc, bias, dil, dep):
    nsub, nb, ncls = _swa_steps(qc, dil)
    whole = nb == 1

    def body(q_ref, kp_ref, kc_ref, vp_ref, vc_ref, b_ref, dep_ref, o_ref, lse_ref, s_scr, p_scr):
        no_prev = (pl.program_id(1) == 0) & (lax.broadcasted_iota(jnp.int32, (WIN, 2 * WIN), 1) < WIN)
        pairs = [slice(a * LANES, (a + 1) * LANES) for a in range(N_HEADS // 2)]
        for c, t in [(c, t) for c in range(ncls) for t in range(nsub)]:
            i = c * nsub + t
            rows = slice(t * WIN, (t + 1) * WIN)
            alone = whole and t == 0
            cols = slice(WIN, 2 * WIN) if alone else slice(0, 2 * WIN)

            def keys(prev_ref, cur_ref, sl):
                if alone:
                    return cur_ref[c, rows, sl]
                if t == 0:
                    return jnp.concatenate([prev_ref[c, :, sl], cur_ref[c, rows, sl]], axis=0)
                return cur_ref[c, (t - 1) * WIN:(t + 1) * WIN, sl]

            for a, sl in enumerate(pairs):
                k2 = keys(kp_ref, kc_ref, sl)
                for e, qh in enumerate(_pair_split(q_ref[c, rows, sl])):
                    s_scr[i, 2 * a + e, :, cols] = _dot_nt(qh, k2)
            den, lse = [], []
            for h in range(N_HEADS):
                lg = s_scr[i, h, :, cols] + b_ref[h, :, cols]
                if t == 0 and not whole:
                    lg = jnp.where(no_prev, -jnp.inf, lg)
                m = jnp.max(lg, axis=-1, keepdims=True)
                p = jnp.exp(lg - m)
                den.append(jnp.sum(p, axis=-1, keepdims=True))
                p_scr[i, h, :, cols] = p.astype(BF16)
                lse.append(m + jnp.log(den[h]))
            for a, sl in enumerate(pairs):
                v_even, v_odd = _pair_split(keys(vp_ref, vc_ref, sl))
                o2 = _dot(p_scr[i, 2 * a, :, cols], v_even) + _dot(p_scr[i, 2 * a + 1, :, cols], v_odd)
                o_ref[c, rows, sl] = o2 / _pair_join(den[2 * a], den[2 * a + 1])
                lse_ref[c, rows, sl] = _pair_join(lse[2 * a], lse[2 * a + 1])

    cur = pl.BlockSpec((ncls, nsub * WIN, 512), lambda r, b: (r, b, 0))
    prev = pl.BlockSpec((ncls, WIN, 512), lambda r, b: (r, jnp.maximum(nsub * b - 1, 0), 0))
    wide = (ncls * nsub, N_HEADS, WIN, 2 * WIN)
    return pl.pallas_call(
        body, name=f"swa_fwd_d{dil}", grid=(dil // ncls, nb),
        out_shape=[jax.ShapeDtypeStruct(qc.shape, F32)] * 2,
        in_specs=[cur, prev, cur, prev, cur, _full(bias.shape), ANY_SPEC],
        out_specs=[cur] * 2,
        scratch_shapes=[pltpu.VMEM(wide, F32), pltpu.VMEM(wide, BF16)],
        compiler_params=_cparams(2),
    )(qc, kc, kc, vc, vc, bias, dep)


RING = 3


def _ring_fetch(i, n_steps, srcs, bufs, sems):
    def copies(step):
        slot = step % RING
        return [pltpu.make_async_copy(src(step), buf.at[slot], sem.at[slot])
                for src, buf, sem in zip(srcs, bufs, sems)]

    @pl.when(i == 0)
    def _():
        for step in range(RING - 1):
            for copy in copies(step):
                copy.start()

    @pl.when(i + RING - 1 < n_steps)
    def _():
        for copy in copies(i + RING - 1):
            copy.start()

    for copy in copies(i):
        copy.wait()
    return i % RING


def _mix_out(branches, gb, gc, xi, x, w_sc, g_a, g_c, w_out):
    s = x.shape[0]
    tb = TM // SUBLANES

    def body(o1_hbm, l1_hbm, o4_hbm, l4_hbm, o16_hbm, l16_hbm, gb_ref, gc_ref, xi_ref, gch_ref, xih_ref, x_ref,
             wsc_ref, ga_ref, gcv_ref, wout_ref, attn_ref, lse1, lse4, lse16, mixed_ref, x1_ref, scr_a, scr_b, scr_c,
             scr_d, *ring):
        i = pl.program_id(0)
        rows = lambda ref, d: (lambda step: ref.at[pl.ds(step * TM, TM), :] if d == 1 else
                               ref.at[:, pl.ds(step * (TM // d), TM // d), :])
        srcs = [rows(ref, d) for ref, d in ((o1_hbm, 1), (l1_hbm, 1), (o4_hbm, 4), (l4_hbm, 4), (o16_hbm, 16),
                                            (l16_hbm, 16))]
        bufs, sems = ring[:len(srcs)], ring[len(srcs):]
        slot = _ring_fetch(i, s // TM, srcs, bufs, sems)
        o1, l1, o4, l4, o16, l16 = [buf.at[slot] for buf in bufs]
        la, lb, lc = l1[...], _gather_classes(l4, scr_a, 4), _gather_classes(l16, scr_b, 16)
        m_all = jnp.maximum(jnp.maximum(la, lb), lc)
        ea, eb, ec = jnp.exp(la - m_all), jnp.exp(lb - m_all), jnp.exp(lc - m_all)
        den = (ea + eb) + ec
        num = (ea * o1[...] + eb * _gather_classes(o4, scr_c, 4)) + ec * _gather_classes(o16, scr_d, 16)
        attn = num / den
        attn_ref[...] = attn
        _spread(_narrow_heads(m_all + jnp.log(den)), scr_a, (lse1, lse4, lse16), F32)
        xa, _ = _rms(attn)
        u = gc_ref[...] * xi_ref[...]
        uh = jnp.where(i > 0, gch_ref[...] * xih_ref[...], 0.0)
        conv = gb_ref[...] * _causal_conv3(u, uh, wsc_ref)
        xc, _ = _rms(conv)
        mixed = jnp.concatenate([xa * ga_ref[...], xc * gcv_ref[...]], axis=1).astype(BF16)
        mixed_ref[...] = mixed
        x1_ref[...] = x_ref[...] + _dot(mixed, wout_ref[...])

    row = lambda n: pl.BlockSpec((TM, n), lambda i: (i, 0))
    halo = pl.BlockSpec((SUBLANES, 512), lambda i: (jnp.maximum(i * tb - 1, 0), 0))
    cs = _class_specs(512)
    flat = [a for br in branches for a in br]
    res = pl.pallas_call(
        body, name="mix_out", grid=(s // TM,),
        out_shape=[jax.ShapeDtypeStruct((s, 512), F32)] + _class_shapes(s, LANES, F32)
        + [jax.ShapeDtypeStruct((s, D_MODEL), BF16), jax.ShapeDtypeStruct((s, D_MODEL), F32)],
        in_specs=[ANY_SPEC] * 6 + [row(512), row(512), row(512), halo, halo,
                                   row(D_MODEL), _full(w_sc.shape), _full(g_a.shape), _full(g_c.shape),
                                   _full(w_out.shape)],
        out_specs=[row(512)] + _class_specs(LANES) + [row(D_MODEL), row(D_MODEL)],
        scratch_shapes=[pltpu.VMEM((512 // LANES, TM, LANES), F32)] * 4
        + [pltpu.VMEM((RING,) + tuple(c.block_shape), F32) for c in cs for _ in range(2)]
        + [pltpu.SemaphoreType.DMA((RING,))] * 6,
        compiler_params=_cparams(1),
    )(*flat, gb, gc, xi, gc, xi, x, w_sc, g_a, g_c, w_out)
    return res[0], res[1:4], res[4], res[5]


def _mem_kv(mem, g_mem, w_xk, w_xv):
    def body(mem_ref, g_ref, wk_ref, wv_ref, mn_ref, k_ref, v_ref):
        xh, _ = _rms(mem_ref[...])
        mn = (xh * g_ref[...]).astype(BF16)
        mn_ref[...] = mn
        k_ref[...] = _dot(mn, wk_ref[...]).astype(BF16)
        v_ref[...] = _dot(mn, wv_ref[...]).astype(BF16)

    vm = pl.BlockSpec(memory_space=pltpu.VMEM)
    return pl.pallas_call(
        body, name="mem_kv",
        out_shape=[jax.ShapeDtypeStruct(mem.shape, BF16)] * 3,
        in_specs=[vm] * 4, out_specs=[vm] * 3,
        compiler_params=pltpu.CompilerParams(vmem_limit_bytes=VMEM_LIMIT),
    )(mem, g_mem, w_xk, w_xv)


def _xattn_fwd(x1, g, w_xq, k, v, w_xo, dep):
    s = x1.shape[0]

    def body(x1_ref, g_ref, wq_ref, k_ref, v_ref, wo_ref, dep_ref, h2_ref, q_ref, o_ref, x2_ref):
        x1v = x1_ref[...]
        xh, _ = _rms(x1v)
        h2 = (xh * g_ref[...]).astype(BF16)
        h2_ref[...] = h2
        qb = _dot(h2, wq_ref[...]).astype(BF16)
        q_ref[...] = qb
        outs = []
        for h in range(N_MEM_HEADS):
            sl = slice(h * MEM_HEAD_DIM, (h + 1) * MEM_HEAD_DIM)
            lg = _dot_nt(qb[:, sl], k_ref[:, sl]) * (MEM_HEAD_DIM ** -0.5)
            p = jnp.exp(lg - jnp.max(lg, axis=-1, keepdims=True))
            p = p / jnp.sum(p, axis=-1, keepdims=True)
            outs.append(_dot(p.astype(BF16), v_ref[:, sl]))
        o = jnp.concatenate(outs, axis=1).astype(BF16)
        o_ref[...] = o
        x2_ref[...] = x1v + _dot(o, wo_ref[...])

    row = pl.BlockSpec((TM_XATTN, D_MODEL), lambda i: (i, 0))
    return pl.pallas_call(
        body, name="xattn_fwd", grid=(s // TM_XATTN,),
        out_shape=[jax.ShapeDtypeStruct((s, D_MODEL), BF16)] * 3 + [jax.ShapeDtypeStruct((s, D_MODEL), F32)],
        in_specs=[row, _full(g.shape), _full(w_xq.shape), _full(k.shape), _full(v.shape), _full(w_xo.shape), ANY_SPEC],
        out_specs=[row] * 4,
        compiler_params=_cparams(1),
    )(x1, g, w_xq, k, v, w_xo, dep)


def _ffn_conv(h_ext, wup_ref, wfc_ref, bfc_ref, j):
    u = _dot_nt(h_ext, wup_ref[j])
    w = wfc_ref[j]
    c = ((pltpu.roll(u, 2, 0) * w[0:1, :] + pltpu.roll(u, 1, 0) * w[1:2, :]) + u * w[2:3, :]) + bfc_ref[j]
    return c[HALO:], u[HALO:]


def _ffn_fwd(x2, g, w_up_g, w_fc, b_fc, w_down_g, g_final, target):
    s = x2.shape[0]
    tb = TM_FFN // HALO
    n_ch, wid = w_up_g.shape[:2]
    half = n_ch // 2

    def body(x_ref, xp_ref, g_ref, wup_ref, wfc_ref, bfc_ref, wd_ref, gf_ref, t_ref, h_ref, u_ref, c_ref, act_ref,
             dx3_ref, loss_ref, dgf_ref):
        i = pl.program_id(0)

        @pl.when(i == 0)
        def _():
            loss_ref[...] = jnp.zeros_like(loss_ref)
            dgf_ref[...] = jnp.zeros_like(dgf_ref)

        x2v = x_ref[...]
        gv = g_ref[...]
        h = (_rms(x2v)[0] * gv).astype(BF16)
        h_ref[...] = h
        hp = jnp.where(i > 0, _rms(xp_ref[...])[0] * gv, 0.0).astype(BF16)
        h_ext = jnp.concatenate([hp, h], axis=0)
        down = jnp.zeros((TM_FFN, D_MODEL), F32)
        for j in range(half):
            cg, ug = _ffn_conv(h_ext, wup_ref, wfc_ref, bfc_ref, j)
            cv, uv = _ffn_conv(h_ext, wup_ref, wfc_ref, bfc_ref, j + half)
            c_ref[j] = cg
            c_ref[j + half] = cv
            u_ref[j] = ug.astype(BF16)
            u_ref[j + half] = uv.astype(BF16)
            a = ((cg * _sigmoid(cg)) * cv).astype(BF16)
            act_ref[j] = a
            down = down + _dot(a, wd_ref[j])
        x3 = x2v + down
        xh, r = _rms(x3)
        gf = gf_ref[...]
        e = xh * gf - t_ref[...]
        loss_ref[...] += 0.5 * jnp.sum(jnp.sum(e * e, axis=1, keepdims=True), axis=0, keepdims=True) / D_MODEL
        dy = e * (1.0 / D_MODEL)
        dgf_ref[0:1, :] += jnp.sum(dy * xh, axis=0, keepdims=True)
        dx3_ref[...] = _rms_bwd(xh, r, gf, dy)

    row = pl.BlockSpec((TM_FFN, D_MODEL), lambda i: (i, 0))
    prev = pl.BlockSpec((HALO, D_MODEL), lambda i: (jnp.maximum(i * tb - 1, 0), 0))
    return pl.pallas_call(
        body, name="ffn_fwd", grid=(s // TM_FFN,),
        out_shape=[jax.ShapeDtypeStruct((s, D_MODEL), BF16), jax.ShapeDtypeStruct((n_ch, s, wid), BF16),
                   jax.ShapeDtypeStruct((n_ch, s, wid), F32), jax.ShapeDtypeStruct((half, s, wid), BF16),
                   jax.ShapeDtypeStruct((s, D_MODEL), F32), jax.ShapeDtypeStruct((SUBLANES, 128), F32),
                   jax.ShapeDtypeStruct((SUBLANES, D_MODEL), F32)],
        in_specs=[row, prev, _full(g.shape), _resident(w_up_g.shape), _full(w_fc.shape), _full(b_fc.shape),
                  _resident(w_down_g.shape), _full(g_final.shape), row],
        out_specs=[row, pl.BlockSpec((n_ch, TM_FFN, wid), lambda i: (0, i, 0)),
                   pl.BlockSpec((n_ch, TM_FFN, wid), lambda i: (0, i, 0)),
                   pl.BlockSpec((half, TM_FFN, wid), lambda i: (0, i, 0)), row,
                   _full((SUBLANES, 128)), _full((SUBLANES, D_MODEL))],
        compiler_params=_cparams(1),
    )(x2, x2, g, w_up_g, w_fc, b_fc, w_down_g, g_final, target)


def _ffn_bwd(dx3, up, conv, x2, g, w_up_g, w_fc, w_down_g):
    s = x2.shape[0]
    tb = TM_FFN // HALO
    last = s // HALO - 1
    n_tiles = s // TM_FFN
    n_ch, wid = w_up_g.shape[:2]
    half = n_ch // 2
    n_ext = TM_FFN + HALO

    def body(dx_ref, dxn_ref, u_ref, c_ref, cn_ref, x2_ref, g_ref, wup_ref, wfc_ref, wd_ref,
             dup_ref, dx2_ref, dg_ref, dwfc_ref, dbfc_ref):
        i = pl.program_id(0)

        @pl.when(i == 0)
        def _():
            dg_ref[...] = jnp.zeros_like(dg_ref)
            dwfc_ref[...] = jnp.zeros_like(dwfc_ref)
            dbfc_ref[...] = jnp.zeros_like(dbfc_ref)

        dxv = dx_ref[...]
        dxn = jnp.where(i < n_tiles - 1, dxn_ref[...], 0.0)
        dx_ext = jnp.concatenate([dxv, dxn], axis=0).astype(BF16)
        dh = jnp.zeros((TM_FFN, D_MODEL), F32)
        for j in range(half):
            cg = jnp.concatenate([c_ref[j], cn_ref[j]], axis=0)
            cv = jnp.concatenate([c_ref[j + half], cn_ref[j + half]], axis=0)
            dact = _dot_nt(dx_ext, wd_ref[j])
            sg = _sigmoid(cg)
            silu = cg * sg
            parts = ((j + half, dact * silu), (j, (dact * cv) * (sg + silu * (1.0 - sg))))
            for jj, dc in parts:
                u = u_ref[jj].astype(F32)
                dc0, dc1, dc2 = dc[:TM_FFN], pltpu.roll(dc, n_ext - 1, 0)[:TM_FFN], pltpu.roll(dc, n_ext - 2, 0)[:TM_FFN]
                dbfc_ref[jj:jj + 1, :] += jnp.sum(dc0, axis=0, keepdims=True)
                dwfc_ref[0, jj:jj + 1, :] += jnp.sum(dc2 * u, axis=0, keepdims=True)
                dwfc_ref[1, jj:jj + 1, :] += jnp.sum(dc1 * u, axis=0, keepdims=True)
                dwfc_ref[2, jj:jj + 1, :] += jnp.sum(dc0 * u, axis=0, keepdims=True)
                w = wfc_ref[jj]
                du = ((dc0 * w[2:3, :] + dc1 * w[1:2, :]) + dc2 * w[0:1, :]).astype(BF16)
                dup_ref[jj] = du
                dh = dh + _dot(du, wup_ref[jj])
        xh, r = _rms(x2_ref[...])
        dg_ref[0:1, :] += jnp.sum(dh * xh, axis=0, keepdims=True)
        dx2_ref[...] = dxv + _rms_bwd(xh, r, g_ref[...], dh)

    row = pl.BlockSpec((TM_FFN, D_MODEL), lambda i: (i, 0))
    nxt = pl.BlockSpec((HALO, D_MODEL), lambda i: (jnp.minimum((i + 1) * tb, last), 0))
    cur_c = pl.BlockSpec((n_ch, TM_FFN, wid), lambda i: (0, i, 0))
    nxt_c = pl.BlockSpec((n_ch, HALO, wid), lambda i: (0, jnp.minimum((i + 1) * tb, last), 0))
    return pl.pallas_call(
        body, name="ffn_bwd", grid=(n_tiles,),
        out_shape=[jax.ShapeDtypeStruct((n_ch, s, wid), BF16), jax.ShapeDtypeStruct((s, D_MODEL), F32),
                   jax.ShapeDtypeStruct((SUBLANES, D_MODEL), F32), jax.ShapeDtypeStruct((3, n_ch, wid), F32),
                   jax.ShapeDtypeStruct((n_ch, wid), F32)],
        in_specs=[row, nxt, cur_c, cur_c, nxt_c, row, _full(g.shape), _resident(w_up_g.shape), _full(w_fc.shape),
                  _resident(w_down_g.shape)],
        out_specs=[cur_c, row, _full((SUBLANES, D_MODEL)), _full((3, n_ch, wid)), _full((n_ch, wid))],
        compiler_params=_cparams(1),
    )(dx3, dx3, up, conv, conv, x2, g, w_up_g, w_fc, w_down_g)


def _xattn_bwd(dx2, o, q, k, v, w_xo, w_xq, x1, g, dep):
    s = x1.shape[0]

    def body(dx2_ref, o_ref, q_ref, k_ref, v_ref, wo_ref, wq_ref, x1_ref, g_ref, dep_ref, dq_ref, dx1_ref, dk_ref,
             dv_ref, dg_ref):
        @pl.when(pl.program_id(0) == 0)
        def _():
            dk_ref[...] = jnp.zeros_like(dk_ref)
            dv_ref[...] = jnp.zeros_like(dv_ref)
            dg_ref[...] = jnp.zeros_like(dg_ref)

        dx2v = dx2_ref[...]
        do = _dot_nt(dx2v.astype(BF16), wo_ref[...])
        dqs = []
        for h in range(N_MEM_HEADS):
            sl = slice(h * MEM_HEAD_DIM, (h + 1) * MEM_HEAD_DIM)
            qh, kh, vh = q_ref[:, sl], k_ref[:, sl], v_ref[:, sl]
            lg = _dot_nt(qh, kh) * (MEM_HEAD_DIM ** -0.5)
            p = jnp.exp(lg - jnp.max(lg, axis=-1, keepdims=True))
            p = p / jnp.sum(p, axis=-1, keepdims=True)
            doh = do[:, sl].astype(BF16)
            dp = _dot_nt(doh, vh)
            ds = (p * (dp - jnp.sum(p * dp, axis=-1, keepdims=True)) * (MEM_HEAD_DIM ** -0.5)).astype(BF16)
            dqs.append(_dot(ds, kh))
            dk_ref[:, sl] += _dot_tn(ds, qh)
            dv_ref[:, sl] += _dot_tn(p.astype(BF16), doh)
        dq = jnp.concatenate(dqs, axis=1).astype(BF16)
        dq_ref[...] = dq
        dh2 = _dot_nt(dq, wq_ref[...])
        xh, r = _rms(x1_ref[...])
        dg_ref[0:1, :] += jnp.sum(dh2 * xh, axis=0, keepdims=True)
        dx1_ref[...] = dx2v + _rms_bwd(xh, r, g_ref[...], dh2)

    row = pl.BlockSpec((TM_XATTN, D_MODEL), lambda i: (i, 0))
    return pl.pallas_call(
        body, name="xattn_bwd", grid=(s // TM_XATTN,),
        out_shape=[jax.ShapeDtypeStruct((s, D_MODEL), BF16), jax.ShapeDtypeStruct((s, D_MODEL), F32),
                   jax.ShapeDtypeStruct(k.shape, F32), jax.ShapeDtypeStruct(k.shape, F32),
                   jax.ShapeDtypeStruct((SUBLANES, D_MODEL), F32)],
        in_specs=[row, row, row, _full(k.shape), _full(v.shape), _full(w_xo.shape), _full(w_xq.shape), row,
                  _full(g.shape), ANY_SPEC],
        out_specs=[row, row, _full(k.shape), _full(k.shape), _full((SUBLANES, D_MODEL))],
        compiler_params=_cparams(1),
    )(dx2, o, q, k, v, w_xo, w_xq, x1, g, dep)


def _mem_kv_bwd(dk, dv, mem_n, mem, w_xk, w_xv):
    def body(dk_ref, dv_ref, mn_ref, mem_ref, wk_ref, wv_ref, dwk_ref, dwv_ref, dg_ref):
        dkb, dvb = dk_ref[...].astype(BF16), dv_ref[...].astype(BF16)
        mn = mn_ref[...]
        dwk_ref[...] = _dot_tn(mn, dkb).astype(BF16)
        dwv_ref[...] = _dot_tn(mn, dvb).astype(BF16)
        dmn = _dot_nt(dkb, wk_ref[...]) + _dot_nt(dvb, wv_ref[...])
        xh, _ = _rms(mem_ref[...])
        dg_ref[...] = jnp.zeros_like(dg_ref)
        dg_ref[0:1, :] = jnp.sum(dmn * xh, axis=0, keepdims=True)

    vm = pl.BlockSpec(memory_space=pltpu.VMEM)
    return pl.pallas_call(
        body, name="mem_kv_bwd",
        out_shape=[jax.ShapeDtypeStruct(w_xk.shape, BF16), jax.ShapeDtypeStruct(w_xv.shape, BF16),
                   jax.ShapeDtypeStruct((SUBLANES, D_MODEL), F32)],
        in_specs=[vm] * 6, out_specs=[vm] * 3,
        compiler_params=pltpu.CompilerParams(vmem_limit_bytes=VMEM_LIMIT),
    )(dk, dv, mem_n, mem, w_xk, w_xv)


def _mix_out_bwd(dx1, w_out, attn, gb, gc, xi, w_sc, g_a, g_c, dep):
    s = dx1.shape[0]
    tb = TM // SUBLANES

    def body(dx1_ref, wout_ref, attn_ref, gb_ref, gc_ref, xi_ref, gch_ref, xih_ref, wsc_ref, ga_ref, gcv_ref, dep_ref,
             da1, da4, da16, dd1, dd4, dd16, dgb_ref, dcv_ref, dga_ref, dgc_ref, dwsc_ref, scr):
        i = pl.program_id(0)

        @pl.when(i == 0)
        def _():
            dga_ref[...] = jnp.zeros_like(dga_ref)
            dgc_ref[...] = jnp.zeros_like(dgc_ref)
            dwsc_ref[...] = jnp.zeros_like(dwsc_ref)

        dmixed = _dot_nt(dx1_ref[...].astype(BF16), wout_ref[...])
        da, dcn = dmixed[:, :ATTN_W], dmixed[:, ATTN_W:]
        attn = attn_ref[...]
        xa, ra = _rms(attn)
        dga_ref[0:1, :] += jnp.sum(da * xa, axis=0, keepdims=True)
        dattn = _rms_bwd(xa, ra, ga_ref[...], da)
        _spread(dattn, scr, (da1, da4, da16), BF16)
        prod = dattn * attn
        dd = jnp.concatenate(
            [jnp.broadcast_to(jnp.sum(prod[:, h * HEAD_DIM:(h + 1) * HEAD_DIM], axis=-1, keepdims=True),
                              (TM, HEAD_DIM)) for h in range(N_HEADS)], axis=1)
        _spread(_narrow_heads(dd), scr, (dd1, dd4, dd16), F32)
        gbv = gb_ref[...]
        u = gc_ref[...] * xi_ref[...]
        uh = jnp.where(i > 0, gch_ref[...] * xih_ref[...], 0.0)
        u2, u1 = _shift_down(u, uh, 2), _shift_down(u, uh, 1)
        cv = (u2 * wsc_ref[0:1, :] + u1 * wsc_ref[1:2, :]) + u * wsc_ref[2:3, :]
        xc, rc = _rms(gbv * cv)
        dgc_ref[0:1, :] += jnp.sum(dcn * xc, axis=0, keepdims=True)
        dconv = _rms_bwd(xc, rc, gcv_ref[...], dcn)
        dgb_ref[...] = (dconv * cv).astype(BF16)
        dcv = dconv * gbv
        dcv_ref[...] = dcv
        dwsc_ref[0:1, :] += jnp.sum(dcv * u2, axis=0, keepdims=True)
        dwsc_ref[1:2, :] += jnp.sum(dcv * u1, axis=0, keepdims=True)
        dwsc_ref[2:3, :] += jnp.sum(dcv * u, axis=0, keepdims=True)

    row = lambda n: pl.BlockSpec((TM, n), lambda i: (i, 0))
    halo = pl.BlockSpec((SUBLANES, 512), lambda i: (jnp.maximum(i * tb - 1, 0), 0))
    acc = _full((SUBLANES, 512))
    res = pl.pallas_call(
        body, name="mix_out_bwd", grid=(s // TM,),
        out_shape=_class_shapes(s, 512, BF16) + _class_shapes(s, LANES, F32)
        + [jax.ShapeDtypeStruct((s, 512), BF16), jax.ShapeDtypeStruct((s, 512), F32)]
        + [jax.ShapeDtypeStruct((SUBLANES, 512), F32)] * 3,
        in_specs=[row(D_MODEL), _full(w_out.shape), row(512), row(512), row(512), row(512), halo, halo,
                  _full(w_sc.shape), _full(g_a.shape), _full(g_c.shape), ANY_SPEC],
        out_specs=_class_specs(512) + _class_specs(LANES) + [row(512)] * 2 + [acc] * 3,
        scratch_shapes=[pltpu.VMEM((512 // LANES, TM, LANES), F32)],
        compiler_params=_cparams(1),
    )(dx1, w_out, attn, gb, gc, xi, gc, xi, w_sc, g_a, g_c, dep)
    return res[0:3], res[3:6], res[6], res[7], res[8], res[9], res[10]


def _swa_bwd(qc, kc, vc, doc, lsec, ddc, bias, dil, dep):
    nsub, nb, ncls = _swa_steps(qc, dil)
    n128 = nsub * nb
    whole = nb == 1

    def body(q_ref, qn_ref, kp_ref, kc_ref, vp_ref, vc_ref, do_ref, don_ref, lse_ref, lsen_ref, dd_ref, ddn_ref,
             b_ref, dep_ref, dq_ref, dk_ref, dv_ref, db_ref, s_scr, dp_scr, sn_scr, dpn_scr, ds_scr, p_scr, dsn_scr,
             pn_scr):
        r, b = pl.program_id(0), pl.program_id(1)

        @pl.when((r == 0) & (b == 0))
        def _():
            db_ref[...] = jnp.zeros_like(db_ref)

        pairs = [slice(a * LANES, (a + 1) * LANES) for a in range(N_HEADS // 2)]
        blk = [slice(t * WIN, (t + 1) * WIN) for t in range(nsub)]
        last = blk[nsub - 1]
        cols = lambda t: slice(WIN, 2 * WIN) if whole and t == 0 else slice(0, 2 * WIN)
        of_head = lambda ref, c, rows, h: ref[c, rows, _head_lane(h):_head_lane(h) + 1]
        no_prev = (b == 0) & (lax.broadcasted_iota(jnp.int32, (WIN, 2 * WIN), 1) < WIN)

        def keys(prev_ref, cur_ref, c, t, sl):
            if whole and t == 0:
                return cur_ref[c, blk[0], sl]
            if t == 0:
                return jnp.concatenate([prev_ref[c, :, sl], cur_ref[c, blk[0], sl]], axis=0)
            return cur_ref[c, (t - 1) * WIN:(t + 1) * WIN, sl]

        for a, sl in enumerate(pairs):
            for c, t in [(c, t) for c in range(ncls) for t in range(nsub)]:
                k2, v2 = keys(kp_ref, kc_ref, c, t, sl), keys(vp_ref, vc_ref, c, t, sl)
                q_eo = _pair_split(q_ref[c, blk[t], sl])
                do_eo = _pair_split(do_ref[c, blk[t], sl].astype(BF16))
                for e in range(2):
                    s_scr[c * nsub + t, 2 * a + e, :, cols(t)] = _dot_nt(q_eo[e], k2)
                    dp_scr[c * nsub + t, 2 * a + e, :, cols(t)] = _dot_nt(do_eo[e], v2)
            if not whole:
                qn_eo = _pair_split(qn_ref[0, :, sl])
                don_eo = _pair_split(don_ref[0, :, sl].astype(BF16))
                for e in range(2):
                    sn_scr[2 * a + e] = _dot_nt(qn_eo[e], kc_ref[0, last, sl])
                    dpn_scr[2 * a + e] = _dot_nt(don_eo[e], vc_ref[0, last, sl])
        for c, t, h in [(c, t, h) for c in range(ncls) for t in range(nsub) for h in range(N_HEADS)]:
            i, cl = c * nsub + t, cols(t)
            lg = s_scr[i, h, :, cl] + b_ref[h, :, cl]
            if t == 0 and not whole:
                lg = jnp.where(no_prev, -jnp.inf, lg)
            p = jnp.exp(lg - of_head(lse_ref, c, blk[t], h))
            ds = p * (dp_scr[i, h, :, cl] - of_head(dd_ref, c, blk[t], h))
            db_ref[h, :, cl] += ds
            ds_scr[i, h, :, cl] = ds.astype(BF16)
            p_scr[i, h, :, cl] = p.astype(BF16)
        if not whole:
            every = slice(0, WIN)
            for h in range(N_HEADS):
                lgn = jnp.where(b + 1 < nb, sn_scr[h] + b_ref[h, :, :WIN], -jnp.inf)
                pn = jnp.exp(lgn - of_head(lsen_ref, 0, every, h))
                dsn_scr[h] = (pn * (dpn_scr[h] - of_head(ddn_ref, 0, every, h))).astype(BF16)
                pn_scr[h] = pn.astype(BF16)
        for a, sl in enumerate(pairs):
            for c in range(ncls):
                q_eo = [_pair_split(q_ref[c, blk[t], sl]) for t in range(nsub)]
                do_eo = [_pair_split(do_ref[c, blk[t], sl].astype(BF16)) for t in range(nsub)]
                if not whole:
                    q_eo.append(_pair_split(qn_ref[0, :, sl]))
                    do_eo.append(_pair_split(don_ref[0, :, sl].astype(BF16)))
                for t in range(nsub):
                    i = c * nsub + t
                    k_eo = _pair_split(keys(kp_ref, kc_ref, c, t, sl))
                    dq, dk, dv = None, None, None
                    for e in range(2):
                        h = 2 * a + e
                        terms = [_dot(ds_scr[i, h, :, cols(t)], k_eo[e]),
                                 _dot_tn(ds_scr[i, h, :, WIN:], q_eo[t][e]),
                                 _dot_tn(p_scr[i, h, :, WIN:], do_eo[t][e])]
                        if t + 1 < nsub or not whole:
                            ds_next = ds_scr[i + 1, h, :, :WIN] if t + 1 < nsub else dsn_scr[h]
                            p_next = p_scr[i + 1, h, :, :WIN] if t + 1 < nsub else pn_scr[h]
                            terms[1] += _dot_tn(ds_next, q_eo[t + 1][e])
                            terms[2] += _dot_tn(p_next, do_eo[t + 1][e])
                        dq, dk, dv = terms if e == 0 else (dq + terms[0], dk + terms[1], dv + terms[2])
                    dq_ref[c, blk[t], sl] = dq.astype(BF16)
                    dk_ref[c, blk[t], sl] = dk.astype(BF16)
                    dv_ref[c, blk[t], sl] = dv.astype(BF16)

    cur = pl.BlockSpec((ncls, nsub * WIN, 512), lambda r, b: (r, b, 0))
    prev = pl.BlockSpec((ncls, WIN, 512), lambda r, b: (r, jnp.maximum(nsub * b - 1, 0), 0))
    nxt = pl.BlockSpec((ncls, WIN, 512), lambda r, b: (r, jnp.minimum(nsub * b + nsub, n128 - 1), 0))
    cur_h = pl.BlockSpec((ncls, nsub * WIN, LANES), cur.index_map)
    nxt_h = pl.BlockSpec((ncls, WIN, LANES), nxt.index_map)
    wide, narrow = (ncls * nsub, N_HEADS, WIN, 2 * WIN), (N_HEADS, WIN, WIN)
    return pl.pallas_call(
        body, name=f"swa_bwd_d{dil}", grid=(dil // ncls, nb),
        out_shape=[jax.ShapeDtypeStruct(qc.shape, BF16)] * 3 + [jax.ShapeDtypeStruct(bias.shape, F32)],
        in_specs=[cur, nxt, prev, cur, prev, cur, cur, nxt, cur_h, nxt_h, cur_h, nxt_h, _full(bias.shape),
                  ANY_SPEC],
        out_specs=[cur] * 3 + [_full(bias.shape)],
        scratch_shapes=[pltpu.VMEM(wide, F32), pltpu.VMEM(wide, F32), pltpu.VMEM(narrow, F32),
                        pltpu.VMEM(narrow, F32), pltpu.VMEM(wide, BF16), pltpu.VMEM(wide, BF16),
                        pltpu.VMEM(narrow, BF16), pltpu.VMEM(narrow, BF16)],
        compiler_params=_cparams(2),
    )(qc, qc, kc, kc, vc, vc, doc, doc, lsec, lsec, ddc, ddc, bias, dep)


def _in_proj_bwd(dqs, dks, dvs, dgb, dcv, gc, xi, w_sc, w_in_g, x, g_mix, dx1):
    s = x.shape[0]
    tb = TM // SUBLANES
    last = s // SUBLANES - 1
    n_tiles = s // TM

    def body(dq1, dq4, dq16, dk1, dk4, dk16, dv1, dv4, dv16, dgb_ref, dcv_ref, dcvn_ref, gc_ref, xi_ref, wsc_ref,
             w_hbm, x_ref, g_ref, dx1_ref, dproj_ref, gx_ref, dg_ref, scr_a, scr_b, w_scr, w_sems):
        i = pl.program_id(0)
        _load_w_in_pairs(w_hbm, w_scr, w_sems)

        @pl.when(i == 0)
        def _():
            dg_ref[...] = jnp.zeros_like(dg_ref)

        d0 = dcv_ref[...]
        dn = jnp.where(i < n_tiles - 1, dcvn_ref[...], 0.0)
        du = (d0 * wsc_ref[2:3, :] + _shift_up(d0, dn, 1) * wsc_ref[1:2, :]) + _shift_up(d0, dn, 2) * wsc_ref[0:1, :]
        merge = lambda a, b4, b16: ((a[...].astype(F32) + _gather_classes(b4, scr_a, 4))
                                    + _gather_classes(b16, scr_b, 16))
        dq = merge(dq1, dq4, dq16) * (HEAD_DIM ** -0.5)
        dk = merge(dk1, dk4, dk16)
        dv = merge(dv1, dv4, dv16)
        dproj = jnp.concatenate([dq, dk, dv, dgb_ref[...].astype(F32), du * xi_ref[...], du * gc_ref[...]],
                                axis=1).astype(BF16)
        dproj_ref[...] = dproj
        dh = jnp.zeros((TM, D_MODEL), F32)
        for j in range(N_DEV // 2):
            dh = dh + _dot_nt(dproj[:, 2 * j * IN_CHUNK:2 * (j + 1) * IN_CHUNK], w_scr[j])
        xh, r = _rms(x_ref[...])
        dg_ref[0:1, :] += jnp.sum(dh * xh, axis=0, keepdims=True)
        gx_ref[...] = dx1_ref[...] + _rms_bwd(xh, r, g_ref[...], dh)

    row = lambda n: pl.BlockSpec((TM, n), lambda i: (i, 0))
    nxt = pl.BlockSpec((SUBLANES, 512), lambda i: (jnp.minimum((i + 1) * tb, last), 0))
    return pl.pallas_call(
        body, name="in_proj_bwd", grid=(n_tiles,),
        out_shape=[jax.ShapeDtypeStruct((s, IN_COLS), BF16), jax.ShapeDtypeStruct((s, D_MODEL), F32),
                   jax.ShapeDtypeStruct((SUBLANES, D_MODEL), F32)],
        in_specs=_class_specs(512) * 3 + [row(512), row(512), nxt, row(512), row(512), _full(w_sc.shape),
                                          ANY_SPEC, row(D_MODEL), _full(g_mix.shape), row(D_MODEL)],
        out_specs=[row(IN_COLS), row(D_MODEL), _full((SUBLANES, D_MODEL))],
        scratch_shapes=[pltpu.VMEM((512 // LANES, TM, LANES), F32)] * 2 + W_IN_PAIRS,
        compiler_params=_cparams(1),
    )(*dqs, *dks, *dvs, dgb, dcv, dcv, gc, xi, w_sc, w_in_g, x, g_mix, dx1)


def _dw(a, b, dep, name, a_chunked=False, b_chunked=False, n_chunks=1, chunk_cols=None, per_step=1):
    single = not (a_chunked or b_chunked or chunk_cols)
    wide = a_chunked and a.shape[2] > D_MODEL
    ts = TS_DW // 4 if single else TS_DW // 2 if wide else TS_DW
    if a_chunked:
        nj, s, kk = a.shape
        nn = b.shape[1]
        a_spec = pl.BlockSpec((1, ts, kk), lambda j, t: (j, t, 0))
        b_spec = pl.BlockSpec((ts, nn), lambda j, t: (t, 0))
    elif b_chunked:
        nj, s, nn = b.shape
        kk = a.shape[1]
        a_spec = pl.BlockSpec((ts, kk), lambda j, t: (t, 0))
        b_spec = pl.BlockSpec((1, ts, nn), lambda j, t: (j, t, 0))
    else:
        s, kk = a.shape
        nj, nn = (n_chunks // per_step, chunk_cols * per_step) if chunk_cols else (1, b.shape[1])
        a_spec = pl.BlockSpec((ts, kk), lambda j, t: (t, 0))
        b_spec = pl.BlockSpec((ts, nn), lambda j, t: (t, j))
    n_steps = s // ts

    def body(a_ref, b_ref, dep_ref, o_ref, acc):
        t = pl.program_id(1)

        @pl.when(t == 0)
        def _():
            acc[...] = jnp.zeros_like(acc)

        av = (a_ref[0] if a_chunked else a_ref[...]).astype(BF16)
        bv = (b_ref[0] if b_chunked else b_ref[...]).astype(BF16)
        acc[...] += _dot_tn(av, bv)

        @pl.when(t == n_steps - 1)
        def _():
            for q in range(per_step):
                o_ref[q] = acc[:, q * nn // per_step:(q + 1) * nn // per_step].astype(BF16)

    return pl.pallas_call(
        body, name=name, grid=(nj, n_steps),
        out_shape=jax.ShapeDtypeStruct((nj * per_step, kk, nn // per_step), BF16),
        in_specs=[a_spec, b_spec, ANY_SPEC],
        out_specs=pl.BlockSpec((per_step, kk, nn // per_step), lambda j, t: (j, 0, 0)),
        scratch_shapes=[pltpu.VMEM((kk, nn), F32)],
        compiler_params=_cparams(2),
    )(a, b, dep)


def _adamw_math(w, g, m, v):
    m2 = ADAM_B1 * m + (1.0 - ADAM_B1) * g
    v2 = ADAM_B2 * v + (1.0 - ADAM_B2) * (g * g)
    m_hat = m2 / (1.0 - ADAM_B1 ** ADAM_STEP)
    v_hat = v2 / (1.0 - ADAM_B2 ** ADAM_STEP)
    delta = -ADAM_LR * (m_hat / (jnp.sqrt(v_hat) + ADAM_EPS) + ADAM_WD * w)
    return delta, m2, v2


def _sum_parts(me, own, p_ref):
    g = None
    for i in range(N_DEV):
        part = jnp.where(me == i, own.astype(F32), p_ref[i].astype(F32))
        g = part if g is None else g + part
    return g


def _adamw_big(name, w, sent, parts, m, v, me_arr):
    rr, cc = w.shape
    tr = rr // 4 if rr >= 512 else rr

    def body(me_ref, w_ref, own_ref, p_ref, m_ref, v_ref, g_ref, d_ref, nm_ref, nv_ref):
        g = own_ref[0].astype(F32)
        for k in range(1, N_DEV):
            g = g + p_ref[(me_ref[0] + k) % N_DEV].astype(F32)
        g_ref[...] = g
        d_ref[...], nm_ref[...], nv_ref[...] = _adamw_math(w_ref[...], g, m_ref[...], v_ref[...])

    row = pl.BlockSpec((tr, cc), lambda i, me: (i, 0))
    return pl.pallas_call(
        body, name=name,
        grid_spec=pltpu.PrefetchScalarGridSpec(
            num_scalar_prefetch=1, grid=(rr // tr,),
            in_specs=[row, pl.BlockSpec((1, tr, cc), lambda i, me: (me[0], i, 0)),
                      pl.BlockSpec((N_DEV, tr, cc), lambda i, me: (0, i, 0)), row, row],
            out_specs=[row] * 4),
        out_shape=[jax.ShapeDtypeStruct((rr, cc), F32)] * 4,
        compiler_params=_cparams(1),
    )(me_arr, w, sent, parts, m, v)


def _small_slices():
    return [
        (slice(ROW_RELB, ROW_RELB + 8), slice(0, N_BUCKETS)),
        (slice(ROW_GMIX, ROW_GMIX + 1), slice(0, D_MODEL)),
        (slice(ROW_GAC, ROW_GAC + 1), slice(0, ATTN_W)),
        (slice(ROW_GAC, ROW_GAC + 1), slice(ATTN_W, D_MODEL)),
        (slice(ROW_GXATTN, ROW_GXATTN + 1), slice(0, D_MODEL)),
        (slice(ROW_GMEM, ROW_GMEM + 1), slice(0, D_MODEL)),
        (slice(ROW_GFFN, ROW_GFFN + 1), slice(0, D_MODEL)),
        (slice(ROW_BFC, ROW_BFC + 8), slice(0, UP_CHUNK)),
        (slice(ROW_GFINAL, ROW_GFINAL + 1), slice(0, D_MODEL)),
    ]


def _adamw_small(own, parts, wmv, me_arr):
    slices = _small_slices()
    n = len(slices)

    def body(*refs):
        me_ref, own_ref, p_ref = refs[:3]
        ins = refs[3:3 + 3 * n]
        g_ref = refs[3 + 3 * n]
        outs = refs[4 + 3 * n:]
        g = _sum_parts(me_ref[0], own_ref[...], p_ref)
        g_ref[...] = g
        for a, (rs, ls) in enumerate(slices):
            ga = g[rs, ls]
            outs[4 * a][...] = ga
            outs[4 * a + 1][...], outs[4 * a + 2][...], outs[4 * a + 3][...] = _adamw_math(
                ins[3 * a][...], ga, ins[3 * a + 1][...], ins[3 * a + 2][...])

    vm = pl.BlockSpec(memory_space=pltpu.VMEM)
    flat = [t for trip in wmv for t in trip]
    out_shape = [jax.ShapeDtypeStruct((SMALL_ROWS, D_MODEL), F32)]
    for w, _, _ in wmv:
        out_shape += [jax.ShapeDtypeStruct(w.shape, F32)] * 4
    res = pl.pallas_call(
        body, name="adamw_small", out_shape=out_shape,
        in_specs=[SMEM_SPEC] + [vm] * (2 + 3 * n), out_specs=[vm] * len(out_shape),
    )(me_arr, own, parts, *flat)
    return res[0], [res[1 + 4 * a:5 + 4 * a] for a in range(n)]


def _adamw_shards(items):
    n = len(items)

    def body(*refs):
        for a in range(n):
            w_ref, g_ref, m_ref, v_ref = refs[4 * a:4 * a + 4]
            d_ref, nm_ref, nv_ref = refs[4 * n + 3 * a:4 * n + 3 * a + 3]
            d_ref[...], nm_ref[...], nv_ref[...] = _adamw_math(w_ref[...], g_ref[...], m_ref[...], v_ref[...])

    vm = pl.BlockSpec(memory_space=pltpu.VMEM)
    out_shape = []
    for w, _, _, _ in items:
        out_shape += [jax.ShapeDtypeStruct(w.shape, F32)] * 3
    res = pl.pallas_call(
        body, name="adamw_shards", out_shape=out_shape, in_specs=[vm] * (4 * n), out_specs=[vm] * (3 * n),
    )(*[t for it in items for t in it])
    return [res[3 * a:3 * a + 3] for a in range(n)]


def _mesh_pos():
    return lax.axis_index("x"), lax.axis_index("y"), lax.axis_index("c")


def _dev_index(p):
    return 4 * p[0] + 2 * p[1] + p[2]


def _all_gather(shards):
    n = len(shards)

    def body(*refs):
        ins, outs = refs[:n], refs[n:2 * n]
        send_sems, recv_sems, loc_sems = refs[2 * n:]
        x, y, c = _mesh_pos()
        me, sib = (x, y, c), (x, y, 1 - c)
        chips = [(1 - x, y), (x, 1 - y), (1 - x, 1 - y)]

        def cp(a, k, block, to, src=None):
            dst = outs[a].at[_dev_index(block)]
            return pltpu.make_async_remote_copy(
                src_ref=dst if src is None else src, dst_ref=dst, send_sem=send_sems.at[a, k],
                recv_sem=recv_sems.at[a, k], device_id=to, device_id_type=MESH)

        mine = [pltpu.make_async_copy(ins[a], outs[a].at[_dev_index(me)], loc_sems.at[a]) for a in range(n)]
        for m_ in mine:
            m_.start()
        first = []
        for a in range(n):
            first.append(cp(a, 0, me, sib, src=ins[a]))
            first += [cp(a, 1 + j, me, (*chip, c), src=ins[a]) for j, chip in enumerate(chips)]
        for f in first:
            f.start()
        passed = []
        for a in range(n):
            for j, chip in enumerate(chips):
                cp(a, 1 + j, (*chip, c), me).wait_recv()
                fwd = cp(a, 4 + j, (*chip, c), sib)
                fwd.start()
                passed.append(fwd)
        for a in range(n):
            cp(a, 0, sib, me).wait_recv()
            for j, chip in enumerate(chips):
                cp(a, 4 + j, (*chip, 1 - c), me).wait_recv()
        for f in first + passed:
            f.wait_send()
        for m_ in mine:
            m_.wait()

    hbm = pl.BlockSpec(memory_space=pltpu.HBM)
    return pl.pallas_call(
        body, name="all_gather_weights",
        out_shape=[jax.ShapeDtypeStruct((N_DEV,) + a.shape, a.dtype) for a in shards],
        in_specs=[hbm] * n, out_specs=[hbm] * n,
        scratch_shapes=[pltpu.SemaphoreType.DMA((n, 7)), pltpu.SemaphoreType.DMA((n, 7)),
                        pltpu.SemaphoreType.DMA((n,))],
    )(*shards)


def _peers():
    x, y, c = _mesh_pos()
    return (x, y, c), [((1 - x) if k & 4 else x, (1 - y) if k & 2 else y, (1 - c) if k & 1 else c)
                       for k in range(1, 8)]


def _exchange_copy(src_ref, land_ref, whole, send_sems, recv_sems, a, k, peer, slot):
    src = src_ref if whole else src_ref.at[_dev_index(peer)]
    return pltpu.make_async_remote_copy(
        src_ref=src, dst_ref=land_ref.at[slot], send_sem=send_sems.at[7 * a + k], recv_sem=recv_sems.at[7 * a + k],
        device_id=peer, device_id_type=MESH)


def _exchange_start(name, srcs, whole, dep):
    n = len(srcs)
    lands = [lax.empty(((N_DEV,) + s.shape) if w else s.shape, s.dtype) for s, w in zip(srcs, whole)]

    def body(*refs):
        src_refs, land_refs = refs[:n], refs[n:2 * n]
        send_sems, recv_sems, token = refs[2 * n + 1], refs[2 * n + 2], refs[-1]
        me, peers = _peers()
        for a in range(n):
            for k, peer in enumerate(peers):
                _exchange_copy(src_refs[a], land_refs[a], whole[a], send_sems, recv_sems, a, k, peer,
                               _dev_index(me)).start()
        token[...] = jnp.zeros_like(token)

    res = pl.pallas_call(
        body, name=name,
        out_shape=(pltpu.SemaphoreType.DMA((7 * n,)), pltpu.SemaphoreType.DMA((7 * n,)),
                   *[pltpu.HBM(a.shape, a.dtype) for a in srcs], *[pltpu.HBM(a.shape, a.dtype) for a in lands],
                   jax.ShapeDtypeStruct((SUBLANES, 128), F32)),
        in_specs=[HBM_SPEC] * (2 * n) + [ANY_SPEC],
        out_specs=(SEM_SPEC, SEM_SPEC, *([HBM_SPEC] * (2 * n)), VMEM_SPEC),
        input_output_aliases={i: 2 + i for i in range(2 * n)},
        compiler_params=pltpu.CompilerParams(has_side_effects=DATAFLOW),
    )(*[pltpu.with_memory_space_constraint(a, pltpu.HBM) for a in srcs],
      *[pltpu.with_memory_space_constraint(a, pltpu.HBM) for a in lands], dep)
    return res[0], res[1], list(res[2:2 + n]), list(res[2 + n:2 + 2 * n]), res[-1]


def _exchange_wait(name, started, whole, after, which=None):
    send_sems, recv_sems, srcs, lands, _ = started
    which = list(range(len(srcs))) if which is None else which
    srcs, lands = [srcs[a] for a in which], [lands[a] for a in which]
    n = len(srcs)

    def body(*refs):
        src_refs, land_refs = refs[:n], refs[n:2 * n]
        send_sems, recv_sems = refs[2 * n], refs[2 * n + 1]
        _, peers = _peers()
        for i, a in enumerate(which):
            for k, peer in enumerate(peers):
                cp = _exchange_copy(src_refs[i], land_refs[i], whole[a], send_sems, recv_sems, a, k, peer,
                                    _dev_index(peer))
                cp.wait_send()
                cp.wait_recv()

    res = pl.pallas_call(
        body, name=name,
        out_shape=[pltpu.HBM(a.shape, a.dtype) for a in srcs + lands],
        in_specs=[HBM_SPEC] * (2 * n) + [SEM_SPEC, SEM_SPEC, ANY_SPEC],
        out_specs=[HBM_SPEC] * (2 * n),
        input_output_aliases={i: i for i in range(2 * n)},
        compiler_params=pltpu.CompilerParams(has_side_effects=DATAFLOW),
    )(*srcs, *lands, send_sems, recv_sems, after)
    return list(res[:n]), list(res[n:])


def _gather_start(name, shards, dep):
    n = len(shards)
    lands = [lax.empty((N_DEV,) + a.shape, a.dtype) for a in shards]

    def body(*refs):
        src_refs, land_refs = refs[:n], refs[n:2 * n]
        send_sems, recv_sems, token = refs[2 * n + 1], refs[2 * n + 2], refs[-1]
        x, y, c = _mesh_pos()
        peers = [(x, y, 1 - c), (1 - x, y, c), (x, 1 - y, c), (1 - x, 1 - y, c)]
        for a in range(n):
            for k, peer in enumerate(peers):
                pltpu.make_async_remote_copy(
                    src_ref=src_refs[a], dst_ref=land_refs[a].at[_dev_index((x, y, c))], send_sem=send_sems.at[4 * a + k],
                    recv_sem=recv_sems.at[4 * a + k], device_id=peer, device_id_type=MESH).start()
        token[...] = jnp.zeros_like(token)

    res = pl.pallas_call(
        body, name=name,
        out_shape=(pltpu.SemaphoreType.DMA((4 * n,)), pltpu.SemaphoreType.DMA((4 * n,)),
                   *[pltpu.HBM(a.shape, a.dtype) for a in shards], *[pltpu.HBM(a.shape, a.dtype) for a in lands],
                   jax.ShapeDtypeStruct((SUBLANES, 128), F32)),
        in_specs=[HBM_SPEC] * (2 * n) + [ANY_SPEC],
        out_specs=(SEM_SPEC, SEM_SPEC, *([HBM_SPEC] * (2 * n)), VMEM_SPEC),
        input_output_aliases={i: 2 + i for i in range(2 * n)},
        compiler_params=pltpu.CompilerParams(has_side_effects=DATAFLOW),
    )(*[pltpu.with_memory_space_constraint(a, pltpu.HBM) for a in shards],
      *[pltpu.with_memory_space_constraint(a, pltpu.HBM) for a in lands], dep)
    return res[0], res[1], list(res[2:2 + n]), list(res[2 + n:2 + 2 * n]), res[-1]


def _gather_forward(name, send_sems, recv_sems, lands, which, after):
    n = len(which)

    def body(*refs):
        land_refs = refs[:n]
        send_sems, recv_sems = refs[n], refs[n + 1]
        fsend, frecv, token = refs[n + 3], refs[n + 4], refs[-1]
        x, y, c = _mesh_pos()
        chips = [(1 - x, y), (x, 1 - y), (1 - x, 1 - y)]
        for i, a in enumerate(which):
            for j, chip in enumerate(chips):
                block = land_refs[i].at[_dev_index((*chip, c))]
                pltpu.make_async_remote_copy(
                    src_ref=block, dst_ref=block, send_sem=send_sems.at[4 * a + 1 + j], recv_sem=recv_sems.at[4 * a + 1 + j],
                    device_id=(*chip, c), device_id_type=MESH).wait_recv()
                pltpu.make_async_remote_copy(
                    src_ref=block, dst_ref=block, send_sem=fsend.at[3 * i + j], recv_sem=frecv.at[3 * i + j],
                    device_id=(x, y, 1 - c), device_id_type=MESH).start()
        token[...] = jnp.zeros_like(token)

    res = pl.pallas_call(
        body, name=name,
        out_shape=(pltpu.SemaphoreType.DMA((3 * n,)), pltpu.SemaphoreType.DMA((3 * n,)),
                   *[pltpu.HBM(a.shape, a.dtype) for a in lands], jax.ShapeDtypeStruct((SUBLANES, 128), F32)),
        in_specs=[HBM_SPEC] * n + [SEM_SPEC, SEM_SPEC, ANY_SPEC],
        out_specs=(SEM_SPEC, SEM_SPEC, *([HBM_SPEC] * n), VMEM_SPEC),
        input_output_aliases={i: 2 + i for i in range(n)},
        compiler_params=pltpu.CompilerParams(has_side_effects=DATAFLOW),
    )(*lands, send_sems, recv_sems, after)
    return res[0], res[1], list(res[2:2 + n]), res[-1]


def _gather_wait(name, send_sems, recv_sems, fsend, frecv, srcs, lands, which, after):
    n = len(which)

    def body(*refs):
        land_refs = refs[n:2 * n]
        send_sems, recv_sems, fsend, frecv = refs[2 * n:2 * n + 4]
        x, y, c = _mesh_pos()
        sib = (x, y, 1 - c)
        chips = [(1 - x, y), (x, 1 - y), (1 - x, 1 - y)]
        for i, a in enumerate(which):
            def cp(slot, ssem, rsem):
                block = land_refs[i].at[_dev_index(slot)]
                return pltpu.make_async_remote_copy(src_ref=block, dst_ref=block, send_sem=ssem, recv_sem=rsem,
                                                    device_id=sib, device_id_type=MESH)
            cp(sib, send_sems.at[4 * a], recv_sems.at[4 * a]).wait_recv()
            for j, chip in enumerate(chips):
                cp((*chip, 1 - c), fsend.at[3 * i + j], frecv.at[3 * i + j]).wait_recv()
            for k in range(4):
                cp(sib, send_sems.at[4 * a + k], recv_sems.at[4 * a + k]).wait_send()
            for j in range(3):
                cp(sib, fsend.at[3 * i + j], frecv.at[3 * i + j]).wait_send()

    res = pl.pallas_call(
        body, name=name,
        out_shape=[pltpu.HBM(a.shape, a.dtype) for a in srcs + lands],
        in_specs=[HBM_SPEC] * (2 * n) + [SEM_SPEC] * 4 + [ANY_SPEC],
        out_specs=[HBM_SPEC] * (2 * n),
        input_output_aliases={i: i for i in range(2 * n)},
        compiler_params=pltpu.CompilerParams(has_side_effects=DATAFLOW),
    )(*srcs, *lands, send_sems, recv_sems, fsend, frecv, after)
    return list(res[n:])


def _local_step(x, mem, target, rel_bias, g_mix, w_in_g, w_sc, g_a, g_c, g_xattn, g_mem, g_ffn, w_fc, b_fc, g_final,
                dep, forward_weights, late_weights, emit, emit_small):
    s = x.shape[0]
    buckets = _bucket_tables()
    bias = _bias_fwd(rel_bias, buckets)

    h1, qs, ks, vs, gb, gc, xi = _rms_proj(x, g_mix, w_in_g, dep)
    qs, ks, vs = ([a[0][None]] + list(a[1:]) for a in (qs, ks, vs))
    group1, group2 = ["w_out", "w_xq", "w_xk", "w_xv", "w_xo"], ["w_up", "w_down"]
    tok = forward_weights(group1, h1)
    branches = []
    for p, dil in enumerate(DILATIONS):
        o_p, lse_p = _swa_fwd(qs[p], ks[p], vs[p], bias[p], dil, tok)
        branches.append([o_p[0], lse_p[0]] if dil == 1 else [o_p, lse_p])
    lw = late_weights(group1, branches[-1][0])
    w_out, w_xq, w_xk, w_xv, w_xo = (lw[n] for n in group1)
    attn, lses, mixed, x1 = _mix_out(branches, gb, gc, xi, x, w_sc, g_a, g_c, w_out)
    tok = forward_weights(group2, x1)
    mem_n, mk, mv = _mem_kv(mem, g_mem, w_xk, w_xv)
    h2, xq, xo, x2 = _xattn_fwd(x1, g_xattn, w_xq, mk, mv, w_xo, tok)
    lw = late_weights(group2, x2)
    w_up_g = lw["w_up"].reshape(FFN_CHUNKS, FFN_WIDTH, D_MODEL)
    w_down_g = lw["w_down"].reshape(FFN_CHUNKS // 2, FFN_WIDTH, D_MODEL)
    pairs = lambda a: a.reshape(FFN_CHUNKS, 2, a.shape[1], UP_CHUNK).transpose(0, 2, 1, 3).reshape(
        FFN_CHUNKS, a.shape[1], FFN_WIDTH)
    w_fc, b_fc = pairs(w_fc), pairs(b_fc)
    h3, up, conv, act, dx3, loss_acc, dg_final = _ffn_fwd(x2, g_ffn, w_up_g, w_fc, b_fc, w_down_g, g_final, target)

    gw_down = _dw(act, dx3, dep, "dw_down", a_chunked=True).reshape(N_DEV // 2, UP_CHUNK, D_MODEL)
    dup, dx2, dg_ffn, dw_fc, db_fc = _ffn_bwd(dx3, up, conv, x2, g_ffn, w_up_g, w_fc, w_down_g)
    gw_up = _dw(dup, h3, dep, "dw_up", a_chunked=True).reshape(N_DEV, UP_CHUNK, D_MODEL)
    tok = emit(dict(w_down=gw_down, w_up=gw_up))
    dxq, dx1, dmk, dmv, dg_xattn = _xattn_bwd(dx2, xo, xq, mk, mv, w_xo, w_xq, x1, g_xattn, tok)
    gw_xo = _dw(xo, dx2, tok, "dw_xo")[0]
    gw_xq = _dw(h2, dxq, tok, "dw_xq")[0]
    gw_xk, gw_xv, dg_mem = _mem_kv_bwd(dmk, dmv, mem_n, mem, w_xk, w_xv)
    tok = emit(dict(w_xo=gw_xo, w_xq=gw_xq, w_xk=gw_xk, w_xv=gw_xv))
    dattns, dds, dgb, dcv, dg_a, dg_c, dw_sc = _mix_out_bwd(dx1, w_out, attn, gb, gc, xi, w_sc, g_a, g_c, tok)
    first = lambda a: [a[0][None]] + list(a[1:])
    dattns, dds, lses = first(dattns), first(dds), first(lses)
    gw_out = _dw(mixed, dx1, tok, "dw_out")[0]
    tok = emit(dict(w_out=gw_out))
    dqs, dks, dvs, dbias = [], [], [], []
    for p, dil in enumerate(DILATIONS):
        dq_p, dk_p, dv_p, db_p = _swa_bwd(qs[p], ks[p], vs[p], dattns[p], lses[p], dds[p], bias[p], dil, tok)
        dqs.append(dq_p[0] if dil == 1 else dq_p)
        dks.append(dk_p[0] if dil == 1 else dk_p)
        dvs.append(dv_p[0] if dil == 1 else dv_p)
        dbias.append(db_p)
    d_relb = _bias_bwd(jnp.stack(dbias), buckets)
    dproj, grad_x, dg_mix = _in_proj_bwd(dqs, dks, dvs, dgb, dcv, gc, xi, w_sc, w_in_g, x, g_mix, dx1)
    pad = lambda a: jnp.pad(a, ((0, 0), (0, D_MODEL - a.shape[1])))
    small = jnp.concatenate([
        d_relb, dg_mix, dg_xattn, dg_mem, dg_ffn, dg_final, jnp.concatenate([dg_a, dg_c], axis=1),
        pad(dw_sc), pad(db_fc.reshape(N_DEV, UP_CHUNK)), pad(dw_fc.reshape(3 * N_DEV, UP_CHUNK)), pad(loss_acc)],
        axis=0)
    tok = emit_small(small)
    gw_in = _dw(h1, dproj, tok, "dw_in", n_chunks=N_DEV, chunk_cols=IN_CHUNK, per_step=2)
    emit(dict(w_in=gw_in))
    return grad_x


def kernel(x, mem, rel_bias, g_mix, w_in, w_short_conv, g_attn_out, g_conv_out, w_out, g_xattn, g_mem, w_xq, w_xk, w_xv, w_xo, g_ffn, w_up, w_ffn_conv, b_ffn_conv, w_down, g_final, loss_target, m_rel_bias, m_g_mix, m_w_in, m_w_short_conv, m_g_attn_out, m_g_conv_out, m_w_out, m_g_xattn, m_g_mem, m_w_xq, m_w_xk, m_w_xv, m_w_xo, m_g_ffn, m_w_up, m_w_ffn_conv, m_b_ffn_conv, m_w_down, m_g_final, v_rel_bias, v_g_mix, v_w_in, v_w_short_conv, v_g_attn_out, v_g_conv_out, v_w_out, v_g_xattn, v_g_mem, v_w_xq, v_w_xk, v_w_xv, v_w_xo, v_g_ffn, v_w_up, v_w_ffn_conv, v_b_ffn_conv, v_w_down, v_g_final):
    me = _dev_index(_mesh_pos())
    me_arr = me.reshape(1).astype(jnp.int32)

    big_names = ["w_in", "w_out", "w_xq", "w_xk", "w_xv", "w_xo", "w_up", "w_down"]
    late_names = big_names[1:]
    big_w = dict(w_in=w_in[0], w_out=w_out[0], w_xq=w_xq[0], w_xk=w_xk[0], w_xv=w_xv[0], w_xo=w_xo[0],
                 w_up=w_up[0].T, w_down=w_down[0])
    big_m = dict(w_in=m_w_in[0], w_out=m_w_out[0], w_xq=m_w_xq[0], w_xk=m_w_xk[0], w_xv=m_w_xv[0], w_xo=m_w_xo[0],
                 w_up=m_w_up[0].T, w_down=m_w_down[0])
    big_v = dict(w_in=v_w_in[0], w_out=v_w_out[0], w_xq=v_w_xq[0], w_xk=v_w_xk[0], w_xv=v_w_xv[0], w_xo=v_w_xo[0],
                 w_up=v_w_up[0].T, w_down=v_w_down[0])
    shard_shape = {n: big_w[n].shape for n in big_names}

    w_in_g, w_sc_g, w_fc_full = _all_gather([big_w["w_in"].astype(BF16), w_short_conv[0], w_ffn_conv[0]])
    w_sc_full = w_sc_g.transpose(1, 0, 2).reshape(3, CONV_W)
    late_shards = [big_w[n].astype(BF16) for n in late_names]
    ag_send, ag_recv, ag_srcs, ag_lands, ag_token = _gather_start("gather_weights_start", late_shards, w_in_g)
    forwarded = {}

    def forward_weights(names, after):
        which = [late_names.index(n) for n in names]
        fsend, frecv, lands, token = _gather_forward("gather_" + "_".join(names) + "_forward", ag_send, ag_recv,
                                                     [ag_lands[a] for a in which], which, after)
        forwarded[tuple(names)] = (fsend, frecv, lands)
        return token

    def late_weights(names, after):
        which = [late_names.index(n) for n in names]
        fsend, frecv, lands = forwarded[tuple(names)]
        lands = _gather_wait("gather_" + "_".join(names) + "_wait", ag_send, ag_recv, fsend, frecv,
                             [ag_srcs[a] for a in which], lands, which, after)
        out = {}
        for n, a, land in zip(names, which, lands):
            full = lax.dynamic_update_index_in_dim(land, late_shards[a], me, 0)
            if n == "w_up":
                out[n] = full
            elif n == "w_down":
                out[n] = full.reshape(N_DEV // 2, UP_CHUNK, D_MODEL)
            else:
                out[n] = full.reshape(D_MODEL, D_MODEL)
        return out

    sent = []

    def emit(grads):
        names = list(grads)
        blocks = [grads[n].reshape((N_DEV,) + shard_shape[n]) for n in names]
        started = _exchange_start("scatter_" + "_".join(names) + "_start", blocks, [False] * len(names), me_arr)
        sent.append((names, started))
        return started[-1]

    def emit_small(small):
        sent_small.append((small, _exchange_start("gather_small_start", [small], [True], me_arr)))
        return sent_small[0][1][-1]

    sent_small = []
    grad_x = _local_step(
        x[0], mem[0], loss_target[0], rel_bias, g_mix, w_in_g, w_sc_full, g_attn_out, g_conv_out, g_xattn, g_mem,
        g_ffn, w_fc_full, b_ffn_conv.reshape(N_DEV, 1, UP_CHUNK), g_final.reshape(1, D_MODEL), ag_token,
        forward_weights, late_weights, emit, emit_small)

    small_g, small_started = sent_small[0]
    after = sent[-1][1][-1]
    small_parts = _exchange_wait("gather_small_wait", small_started, [True], after)[1][0]
    big_out = {}
    after = small_parts
    for names, started in sent:
        blocks, lands = _exchange_wait("scatter_" + "_".join(names) + "_wait", started, [False] * len(names), after)
        for n, block, land in zip(names, blocks, lands):
            res = _adamw_big("adamw_" + n, big_w[n], block, land, big_m[n], big_v[n], me_arr)
            big_out[n] = [(r.T if n == "w_up" else r)[None] for r in res]
            after = res[0]

    as_rows = lambda a: a.reshape(N_DEV, UP_CHUNK)
    row1 = lambda a: a.reshape(1, D_MODEL)
    small_names = ["rel_bias", "g_mix", "g_attn_out", "g_conv_out", "g_xattn", "g_mem", "g_ffn", "b_ffn_conv", "g_final"]
    wmv = [
        (rel_bias, m_rel_bias, v_rel_bias), (g_mix, m_g_mix, v_g_mix), (g_attn_out, m_g_attn_out, v_g_attn_out),
        (g_conv_out, m_g_conv_out, v_g_conv_out), (g_xattn, m_g_xattn, v_g_xattn), (g_mem, m_g_mem, v_g_mem),
        (g_ffn, m_g_ffn, v_g_ffn), (as_rows(b_ffn_conv), as_rows(m_b_ffn_conv), as_rows(v_b_ffn_conv)),
        (row1(g_final), row1(m_g_final), row1(v_g_final))]
    g_packed, small_res = _adamw_small(small_g, small_parts, wmv, me_arr)
    small_out = dict(zip(small_names, small_res))
    loss = g_packed[ROW_LOSS, 0]
    small_out["b_ffn_conv"] = [a.reshape(1, 2 * D_FF) for a in small_out["b_ffn_conv"]]
    small_out["g_final"] = [a.reshape(D_MODEL) for a in small_out["g_final"]]

    g_wsc = lax.dynamic_slice(g_packed[ROW_WSC:ROW_WSC + 3, 0:CONV_W], (0, me * HEAD_DIM), (3, HEAD_DIM))
    g_wfc = lax.dynamic_slice(g_packed[ROW_WFC:ROW_WFC + 3 * N_DEV, 0:UP_CHUNK].reshape(3, N_DEV, UP_CHUNK),
                              (0, me, 0), (3, 1, UP_CHUNK)).reshape(3, UP_CHUNK)
    shard_res = _adamw_shards([(w_short_conv[0], g_wsc, m_w_short_conv[0], v_w_short_conv[0]),
                               (w_ffn_conv[0], g_wfc, m_w_ffn_conv[0], v_w_ffn_conv[0])])
    small_out["w_short_conv"] = [g_wsc[None]] + [a[None] for a in shard_res[0]]
    small_out["w_ffn_conv"] = [g_wfc[None]] + [a[None] for a in shard_res[1]]

    order = ["rel_bias", "g_mix", "w_in", "w_short_conv", "g_attn_out", "g_conv_out", "w_out", "g_xattn", "g_mem",
             "w_xq", "w_xk", "w_xv", "w_xo", "g_ffn", "w_up", "w_ffn_conv", "b_ffn_conv", "w_down", "g_final"]
    allp = {**big_out, **small_out}
    outs = [loss, grad_x[None]]
    for kind in range(4):
        outs += [allp[n][kind] for n in order]
    return tuple(outs)
```

```python
import math

import numpy as np
import jax
import jax.numpy as jnp
from jax import lax
from jax.experimental import pallas as pl
from jax.experimental.pallas import tpu as pltpu

F32 = jnp.float32
BF16 = jnp.bfloat16
MESH = pl.DeviceIdType.MESH

N_DEV = 8
D_MODEL = 1024
ATTN_W = 512
CONV_W = 512
N_HEADS = 8
HEAD_DIM = 64
WIN = 128
DILATIONS = (1, 4, 16)
N_BUCKETS = 32
BUCKET_MAX_EXACT = 16
BUCKET_MAX_DISTANCE = 2048
N_MEM_HEADS = 4
MEM_HEAD_DIM = 256
D_FF = 2816
IN_COLS = 3072
IN_CHUNK = IN_COLS // N_DEV
UP_CHUNK = 2 * D_FF // N_DEV
FFN_CHUNKS = 4
FFN_WIDTH = 2 * D_FF // FFN_CHUNKS
EPS = 1e-6

ADAM_LR = 0.001
ADAM_B1 = 0.9
ADAM_B2 = 0.999
ADAM_EPS = 1e-08
ADAM_WD = 0.01
ADAM_STEP = 10

SUBLANES = 8
LANES = 128
HALO = 16
TM = 512
TM_XATTN = 1024
TM_FFN = 256
TS_DW = 4096
SWA_BLOCKS = 8
VMEM_LIMIT = 56 * 1024 * 1024

ROW_RELB, ROW_GMIX, ROW_GXATTN, ROW_GMEM, ROW_GFFN, ROW_GFINAL, ROW_GAC = 0, 8, 16, 24, 32, 40, 48
ROW_WSC, ROW_BFC, ROW_WFC, ROW_LOSS, SMALL_ROWS = 56, 64, 72, 96, 104


def _cparams(n_grid):
    return pltpu.CompilerParams(dimension_semantics=("arbitrary",) * n_grid, vmem_limit_bytes=VMEM_LIMIT)


def _full(shape):
    nd = len(shape)
    return pl.BlockSpec(tuple(shape), lambda *_: (0,) * nd)


def _resident(shape):
    nd = len(shape)
    return pl.BlockSpec(tuple(shape), lambda *_: (0,) * nd, pipeline_mode=pl.Buffered(1))


ANY_SPEC = pl.BlockSpec(memory_space=pl.ANY)
HBM_SPEC = pl.BlockSpec(memory_space=pltpu.HBM)
SEM_SPEC = pl.BlockSpec(memory_space=pltpu.SEMAPHORE)
VMEM_SPEC = pl.BlockSpec(memory_space=pltpu.VMEM)
SMEM_SPEC = pl.BlockSpec(memory_space=pltpu.SMEM)
DATAFLOW = pltpu.SideEffectType.DATAFLOW_SIDE_EFFECTING


def _rms(x):
    r = lax.rsqrt(jnp.mean(x * x, axis=-1, keepdims=True) + EPS)
    return x * r, r


def _rms_bwd(xh, r, g, dy):
    dxh = dy * g
    return r * (dxh - xh * jnp.mean(dxh * xh, axis=-1, keepdims=True))


def _shift_down(u, halo, k):
    ru = pltpu.roll(u, k, 0)
    rh = pltpu.roll(halo, k, 0)
    row = lax.broadcasted_iota(jnp.int32, rh.shape, 0)
    head = jnp.where(row < k, rh, ru[0:SUBLANES])
    return jnp.concatenate([head, ru[SUBLANES:]], axis=0)


def _shift_up(u, halo, k):
    tm = u.shape[0]
    ru = pltpu.roll(u, tm - k, 0)
    rh = pltpu.roll(halo, SUBLANES - k, 0)
    row = lax.broadcasted_iota(jnp.int32, rh.shape, 0)
    tail = jnp.where(row >= SUBLANES - k, rh, ru[tm - SUBLANES:])
    return jnp.concatenate([ru[:tm - SUBLANES], tail], axis=0)


def _causal_conv3(u, halo, w_ref):
    return (_shift_down(u, halo, 2) * w_ref[0:1, :] + _shift_down(u, halo, 1) * w_ref[1:2, :]) + u * w_ref[2:3, :]


def _dot(a, b):
    return jnp.dot(a, b, preferred_element_type=F32)


def _dot_nt(a, b):
    return lax.dot_general(a, b, (((1,), (1,)), ((), ())), preferred_element_type=F32)


def _dot_tn(a, b):
    return lax.dot_general(a, b, (((0,), (0,)), ((), ())), preferred_element_type=F32)


def _sigmoid(x):
    return 0.5 * jnp.tanh(0.5 * x) + 0.5


def _bucket_tables():
    qi = np.arange(WIN)[:, None]
    kj = np.arange(2 * WIN)[None, :]
    steps = np.clip(qi + WIN - kj, 0, WIN)
    out = []
    for d in DILATIONS:
        dist = steps * d
        dd = np.maximum(dist, 1).astype(np.float32)
        large = BUCKET_MAX_EXACT + (
            np.log(dd / np.float32(BUCKET_MAX_EXACT)) / np.float32(math.log(BUCKET_MAX_DISTANCE / BUCKET_MAX_EXACT))
            * np.float32(N_BUCKETS - BUCKET_MAX_EXACT)).astype(np.int32)
        large = np.minimum(large, N_BUCKETS - 1)
        out.append(np.where(dist < BUCKET_MAX_EXACT, dist, large).astype(np.int32))
    return np.stack(out)


def _band_mask():
    qi = lax.broadcasted_iota(jnp.int32, (WIN, 2 * WIN), 0)
    kj = lax.broadcasted_iota(jnp.int32, (WIN, 2 * WIN), 1)
    steps = qi + WIN - kj
    return (steps >= 0) & (steps <= WIN)


def _bias_fwd(rel_bias, buckets):
    present = [sorted(set(buckets[p].ravel().tolist())) for p in range(3)]

    def body(rb_ref, bk_ref, o_ref):
        band = _band_mask()
        for p in range(3):
            bk = bk_ref[p]
            for h in range(N_HEADS):
                acc = jnp.zeros((WIN, 2 * WIN), F32)
                for b in present[p]:
                    acc = jnp.where(bk == b, rb_ref[h, b], acc)
                o_ref[p, h] = jnp.where(band, acc, -jnp.inf)

    return pl.pallas_call(
        body, name="bias_fwd",
        out_shape=jax.ShapeDtypeStruct((3, N_HEADS, WIN, 2 * WIN), F32),
        in_specs=[pl.BlockSpec(memory_space=pltpu.SMEM), pl.BlockSpec(memory_space=pltpu.VMEM)],
        out_specs=pl.BlockSpec(memory_space=pltpu.VMEM),
    )(rel_bias, jnp.asarray(buckets))


def _bias_bwd(dbias, buckets):
    present = [set(buckets[p].ravel().tolist()) for p in range(3)]

    def body(db_ref, bk_ref, o_ref):
        lane = lax.broadcasted_iota(jnp.int32, (1, D_MODEL), 1)
        rows = []
        for h in range(N_HEADS):
            row = jnp.zeros((1, D_MODEL), F32)
            for b in range(N_BUCKETS):
                tot = jnp.zeros((1, 1), F32)
                for p in (p for p in range(3) if b in present[p]):
                    sel = jnp.where(bk_ref[p] == b, db_ref[p, h], 0.0)
                    tot = tot + jnp.sum(jnp.sum(sel, axis=0, keepdims=True), axis=1, keepdims=True)
                row = jnp.where(lane == b, tot, row)
            rows.append(row)
        o_ref[...] = jnp.concatenate(rows, axis=0)

    return pl.pallas_call(
        body, name="bias_bwd",
        out_shape=jax.ShapeDtypeStruct((N_HEADS, D_MODEL), F32),
        in_specs=[pl.BlockSpec(memory_space=pltpu.VMEM), pl.BlockSpec(memory_space=pltpu.VMEM)],
        out_specs=pl.BlockSpec(memory_space=pltpu.VMEM),
    )(dbias, jnp.asarray(buckets))


def _spread(val, scr_ref, out_refs, dtype):
    out_refs[0][...] = val.astype(dtype)
    n_blk = val.shape[1] // LANES
    for c in range(n_blk):
        scr_ref[c] = val[:, c * LANES:(c + 1) * LANES]
    for o_ref, d in zip(out_refs[1:], DILATIONS[1:]):
        for r in range(d):
            for c in range(n_blk):
                o_ref[r, :, c * LANES:(c + 1) * LANES] = scr_ref.at[c][pl.ds(r, TM // d, stride=d), :].astype(dtype)


HEAD_LANES = LANES // N_HEADS


def _head_lane(h):
    return HEAD_LANES * (h // 2) + (LANES // 2) * (h % 2)


def _narrow_heads(x):
    grp = (lax.broadcasted_iota(jnp.int32, (x.shape[0], LANES), 1) // HEAD_LANES) % (N_HEADS // 2)
    out = x[:, 0:LANES]
    for t in range(1, N_HEADS // 2):
        out = jnp.where(grp == t, x[:, t * LANES:(t + 1) * LANES], out)
    return out


def _gather_classes(blk_ref, scr_ref, d):
    n_blk = blk_ref.shape[2] // LANES
    for r in range(d):
        for c in range(n_blk):
            scr_ref.at[c][pl.ds(r, TM // d, stride=d), :] = blk_ref[r, :, c * LANES:(c + 1) * LANES].astype(F32)
    return jnp.concatenate([scr_ref[c] for c in range(n_blk)], axis=1)


def _class_specs(cols):
    return [pl.BlockSpec((TM, cols), lambda i: (i, 0))] + [
        pl.BlockSpec((d, TM // d, cols), lambda i: (0, i, 0)) for d in DILATIONS[1:]]


def _class_shapes(s, cols, dtype):
    return [jax.ShapeDtypeStruct((s, cols), dtype)] + [
        jax.ShapeDtypeStruct((d, s // d, cols), dtype) for d in DILATIONS[1:]]


def _load_w_in_pairs(w_hbm, w_scr, sems):
    @pl.when(pl.program_id(0) == 0)
    def _():
        copies = [pltpu.make_async_copy(w_hbm.at[j], w_scr.at[j // 2, :, pl.ds((j % 2) * IN_CHUNK, IN_CHUNK)],
                                        sems.at[j]) for j in range(N_DEV)]
        for copy in copies:
            copy.start()
        for copy in copies:
            copy.wait()


W_IN_PAIRS = [pltpu.VMEM((N_DEV // 2, D_MODEL, 2 * IN_CHUNK), BF16), pltpu.SemaphoreType.DMA((N_DEV,))]


def _rms_proj(x, g_mix, w_in_g, dep):
    s = x.shape[0]

    def body(x_ref, g_ref, w_hbm, dep_ref, h_ref, q1, q4, q16, k1, k4, k16, v1, v4, v16, gb_ref, gc_ref, xi_ref, scr,
             w_scr, w_sems):
        _load_w_in_pairs(w_hbm, w_scr, w_sems)
        xh, _ = _rms(x_ref[...])
        h = (xh * g_ref[...]).astype(BF16)
        h_ref[...] = h
        proj = jnp.concatenate([_dot(h, w_scr[j]) for j in range(N_DEV // 2)], axis=1)
        _spread(proj[:, 0:512] * (HEAD_DIM ** -0.5), scr, (q1, q4, q16), BF16)
        _spread(proj[:, 512:1024], scr, (k1, k4, k16), BF16)
        _spread(proj[:, 1024:1536], scr, (v1, v4, v16), BF16)
        gb_ref[...] = proj[:, 1536:2048]
        gc_ref[...] = proj[:, 2048:2560]
        xi_ref[...] = proj[:, 2560:3072]

    row = lambda n: pl.BlockSpec((TM, n), lambda i: (i, 0))
    res = pl.pallas_call(
        body, name="rms_proj", grid=(s // TM,),
        out_shape=[jax.ShapeDtypeStruct((s, D_MODEL), BF16)] + _class_shapes(s, 512, BF16) * 3
        + [jax.ShapeDtypeStruct((s, 512), F32)] * 3,
        in_specs=[row(D_MODEL), _full(g_mix.shape), ANY_SPEC, ANY_SPEC],
        out_specs=[row(D_MODEL)] + _class_specs(512) * 3 + [row(512)] * 3,
        scratch_shapes=[pltpu.VMEM((512 // LANES, TM, LANES), F32)] + W_IN_PAIRS,
        compiler_params=_cparams(1),
    )(x, g_mix, w_in_g, dep)
    return res[0], res[1:4], res[4:7], res[7:10], res[10], res[11], res[12]


def _pair_split(x2):
    lane = lax.broadcasted_iota(jnp.int32, x2.shape, 1)
    zero = jnp.zeros_like(x2)
    return jnp.where(lane < HEAD_DIM, x2, zero), jnp.where(lane >= HEAD_DIM, x2, zero)


def _pair_join(even, odd):
    lane = lax.broadcasted_iota(jnp.int32, (even.shape[0], LANES), 1)
    return jnp.where(lane < HEAD_DIM, even, odd)


def _swa_steps(qc, dil):
    n128 = qc.shape[1] // WIN
    nsub = min(SWA_BLOCKS, n128)
    nb = n128 // nsub
    ncls = min(dil, SWA_BLOCKS // nsub) if nb == 1 else 1
    return nsub, nb, ncls


def _swa_fwd(qc, kc, vc, bias, dil, dep):
    nsub, nb, ncls = _swa_steps(qc, dil)
    whole = nb == 1

    def body(q_ref, kp_ref, kc_ref, vp_ref, vc_ref, b_ref, dep_ref, o_ref, lse_ref, s_scr, p_scr):
        no_prev = (pl.program_id(1) == 0) & (lax.broadcasted_iota(jnp.int32, (WIN, 2 * WIN), 1) < WIN)
        pairs = [slice(a * LANES, (a + 1) * LANES) for a in range(N_HEADS // 2)]
        for c, t in [(c, t) for c in range(ncls) for t in range(nsub)]:
            i = c * nsub + t
            rows = slice(t * WIN, (t + 1) * WIN)
            alone = whole and t == 0
            cols = slice(WIN, 2 * WIN) if alone else slice(0, 2 * WIN)

            def keys(prev_ref, cur_ref, sl):
                if alone:
                    return cur_ref[c, rows, sl]
                if t == 0:
                    return jnp.concatenate([prev_ref[c, :, sl], cur_ref[c, rows, sl]], axis=0)
                return cur_ref[c, (t - 1) * WIN:(t + 1) * WIN, sl]

            for a, sl in enumerate(pairs):
                k2 = keys(kp_ref, kc_ref, sl)
                for e, qh in enumerate(_pair_split(q_ref[c, rows, sl])):
                    s_scr[i, 2 * a + e, :, cols] = _dot_nt(qh, k2)
            den, lse = [], []
            for h in range(N_HEADS):
                lg = s_scr[i, h, :, cols] + b_ref[h, :, cols]
                if t == 0 and not whole:
                    lg = jnp.where(no_prev, -jnp.inf, lg)
                m = jnp.max(lg, axis=-1, keepdims=True)
                p = jnp.exp(lg - m)
                den.append(jnp.sum(p, axis=-1, keepdims=True))
                p_scr[i, h, :, cols] = p.astype(BF16)
                lse.append(m + jnp.log(den[h]))
            for a, sl in enumerate(pairs):
                v_even, v_odd = _pair_split(keys(vp_ref, vc_ref, sl))
                o2 = _dot(p_scr[i, 2 * a, :, cols], v_even) + _dot(p_scr[i, 2 * a + 1, :, cols], v_odd)
                o_ref[c, rows, sl] = o2 / _pair_join(den[2 * a], den[2 * a + 1])
                lse_ref[c, rows, sl] = _pair_join(lse[2 * a], lse[2 * a + 1])

    cur = pl.BlockSpec((ncls, nsub * WIN, 512), lambda r, b: (r, b, 0))
    prev = pl.BlockSpec((ncls, WIN, 512), lambda r, b: (r, jnp.maximum(nsub * b - 1, 0), 0))
    wide = (ncls * nsub, N_HEADS, WIN, 2 * WIN)
    return pl.pallas_call(
        body, name=f"swa_fwd_d{dil}", grid=(dil // ncls, nb),
        out_shape=[jax.ShapeDtypeStruct(qc.shape, F32)] * 2,
        in_specs=[cur, prev, cur, prev, cur, _full(bias.shape), ANY_SPEC],
        out_specs=[cur] * 2,
        scratch_shapes=[pltpu.VMEM(wide, F32), pltpu.VMEM(wide, BF16)],
        compiler_params=_cparams(2),
    )(qc, kc, kc, vc, vc, bias, dep)


RING = 3


def _ring_fetch(i, n_steps, srcs, bufs, sems):
    def copies(step):
        slot = step % RING
        return [pltpu.make_async_copy(src(step), buf.at[slot], sem.at[slot])
                for src, buf, sem in zip(srcs, bufs, sems)]

    @pl.when(i == 0)
    def _():
        for step in range(RING - 1):
            for copy in copies(step):
                copy.start()

    @pl.when(i + RING - 1 < n_steps)
    def _():
        for copy in copies(i + RING - 1):
            copy.start()

    for copy in copies(i):
        copy.wait()
    return i % RING


def _mix_out(branches, gb, gc, xi, x, w_sc, g_a, g_c, w_out):
    s = x.shape[0]
    tb = TM // SUBLANES

    def body(o1_hbm, l1_hbm, o4_hbm, l4_hbm, o16_hbm, l16_hbm, gb_hbm, gc_hbm, xi_hbm, gch_ref, xih_ref, x_hbm,
             wsc_ref, ga_ref, gcv_ref, wout_ref, attn_ref, lse1, lse4, lse16, mixed_ref, x1_ref, scr_a, scr_b, scr_c,
             scr_d, *ring):
        i = pl.program_id(0)
        rows = lambda ref, d: (lambda step: ref.at[pl.ds(step * TM, TM), :] if d == 1 else
                               ref.at[:, pl.ds(step * (TM // d), TM // d), :])
        srcs = [rows(ref, d) for ref, d in ((o1_hbm, 1), (l1_hbm, 1), (o4_hbm, 4), (l4_hbm, 4), (o16_hbm, 16),
                                            (l16_hbm, 16), (gb_hbm, 1), (gc_hbm, 1), (xi_hbm, 1), (x_hbm, 1))]
        bufs, sems = ring[:len(srcs)], ring[len(srcs):]
        slot = _ring_fetch(i, s // TM, srcs, bufs, sems)
        o1, l1, o4, l4, o16, l16, gb_ref, gc_ref, xi_ref, x_ref = [buf.at[slot] for buf in bufs]
        la, lb, lc = l1[...], _gather_classes(l4, scr_a, 4), _gather_classes(l16, scr_b, 16)
        m_all = jnp.maximum(jnp.maximum(la, lb), lc)
        ea, eb, ec = jnp.exp(la - m_all), jnp.exp(lb - m_all), jnp.exp(lc - m_all)
        den = (ea + eb) + ec
        num = (ea * o1[...] + eb * _gather_classes(o4, scr_c, 4)) + ec * _gather_classes(o16, scr_d, 16)
        attn = num / den
        attn_ref[...] = attn
        _spread(_narrow_heads(m_all + jnp.log(den)), scr_a, (lse1, lse4, lse16), F32)
        xa, _ = _rms(attn)
        u = gc_ref[...] * xi_ref[...]
        uh = jnp.where(i > 0, gch_ref[...] * xih_ref[...], 0.0)
        conv = gb_ref[...] * _causal_conv3(u, uh, wsc_ref)
        xc, _ = _rms(conv)
        mixed = jnp.concatenate([xa * ga_ref[...], xc * gcv_ref[...]], axis=1).astype(BF16)
        mixed_ref[...] = mixed
        x1_ref[...] = x_ref[...] + _dot(mixed, wout_ref[...])

    row = lambda n: pl.BlockSpec((TM, n), lambda i: (i, 0))
    halo = pl.BlockSpec((SUBLANES, 512), lambda i: (jnp.maximum(i * tb - 1, 0), 0))
    cs = _class_specs(512)
    flat = [a for br in branches for a in br]
    res = pl.pallas_call(
        body, name="mix_out", grid=(s // TM,),
        out_shape=[jax.ShapeDtypeStruct((s, 512), F32)] + _class_shapes(s, LANES, F32)
        + [jax.ShapeDtypeStruct((s, D_MODEL), BF16), jax.ShapeDtypeStruct((s, D_MODEL), F32)],
        in_specs=[ANY_SPEC] * 9 + [halo, halo, ANY_SPEC, _full(w_sc.shape), _full(g_a.shape), _full(g_c.shape),
                                   _full(w_out.shape)],
        out_specs=[row(512)] + _class_specs(LANES) + [row(D_MODEL), row(D_MODEL)],
        scratch_shapes=[pltpu.VMEM((512 // LANES, TM, LANES), F32)] * 4
        + [pltpu.VMEM((RING,) + tuple(c.block_shape), F32) for c in cs for _ in range(2)]
        + [pltpu.VMEM((RING, TM, 512), F32)] * 3 + [pltpu.VMEM((RING, TM, D_MODEL), F32)]
        + [pltpu.SemaphoreType.DMA((RING,))] * 10,
        compiler_params=_cparams(1),
    )(*flat, gb, gc, xi, gc, xi, x, w_sc, g_a, g_c, w_out)
    return res[0], res[1:4], res[4], res[5]


def _mem_kv(mem, g_mem, w_xk, w_xv):
    def body(mem_ref, g_ref, wk_ref, wv_ref, mn_ref, k_ref, v_ref):
        xh, _ = _rms(mem_ref[...])
        mn = (xh * g_ref[...]).astype(BF16)
        mn_ref[...] = mn
        k_ref[...] = _dot(mn, wk_ref[...]).astype(BF16)
        v_ref[...] = _dot(mn, wv_ref[...]).astype(BF16)

    vm = pl.BlockSpec(memory_space=pltpu.VMEM)
    return pl.pallas_call(
        body, name="mem_kv",
        out_shape=[jax.ShapeDtypeStruct(mem.shape, BF16)] * 3,
        in_specs=[vm] * 4, out_specs=[vm] * 3,
        compiler_params=pltpu.CompilerParams(vmem_limit_bytes=VMEM_LIMIT),
    )(mem, g_mem, w_xk, w_xv)


def _xattn_fwd(x1, g, w_xq, k, v, w_xo, dep):
    s = x1.shape[0]

    def body(x1_ref, g_ref, wq_ref, k_ref, v_ref, wo_ref, dep_ref, h2_ref, q_ref, o_ref, x2_ref):
        x1v = x1_ref[...]
        xh, _ = _rms(x1v)
        h2 = (xh * g_ref[...]).astype(BF16)
        h2_ref[...] = h2
        qb = _dot(h2, wq_ref[...]).astype(BF16)
        q_ref[...] = qb
        outs = []
        for h in range(N_MEM_HEADS):
            sl = slice(h * MEM_HEAD_DIM, (h + 1) * MEM_HEAD_DIM)
            lg = _dot_nt(qb[:, sl], k_ref[:, sl]) * (MEM_HEAD_DIM ** -0.5)
            p = jnp.exp(lg - jnp.max(lg, axis=-1, keepdims=True))
            p = p / jnp.sum(p, axis=-1, keepdims=True)
            outs.append(_dot(p.astype(BF16), v_ref[:, sl]))
        o = jnp.concatenate(outs, axis=1).astype(BF16)
        o_ref[...] = o
        x2_ref[...] = x1v + _dot(o, wo_ref[...])

    row = pl.BlockSpec((TM_XATTN, D_MODEL), lambda i: (i, 0))
    return pl.pallas_call(
        body, name="xattn_fwd", grid=(s // TM_XATTN,),
        out_shape=[jax.ShapeDtypeStruct((s, D_MODEL), BF16)] * 3 + [jax.ShapeDtypeStruct((s, D_MODEL), F32)],
        in_specs=[row, _full(g.shape), _full(w_xq.shape), _full(k.shape), _full(v.shape), _full(w_xo.shape), ANY_SPEC],
        out_specs=[row] * 4,
        compiler_params=_cparams(1),
    )(x1, g, w_xq, k, v, w_xo, dep)


def _ffn_conv(h_ext, wup_ref, wfc_ref, bfc_ref, j):
    u = _dot_nt(h_ext, wup_ref[j])
    w = wfc_ref[j]
    c = ((pltpu.roll(u, 2, 0) * w[0:1, :] + pltpu.roll(u, 1, 0) * w[1:2, :]) + u * w[2:3, :]) + bfc_ref[j]
    return c[HALO:], u[HALO:]


def _ffn_fwd(x2, g, w_up_g, w_fc, b_fc, w_down_g, g_final, target):
    s = x2.shape[0]
    tb = TM_FFN // HALO
    n_ch, wid = w_up_g.shape[:2]
    half = n_ch // 2

    def body(x_ref, xp_ref, g_ref, wup_ref, wfc_ref, bfc_ref, wd_ref, gf_ref, t_ref, h_ref, u_ref, c_ref, act_ref,
             dx3_ref, loss_ref, dgf_ref):
        i = pl.program_id(0)

        @pl.when(i == 0)
        def _():
            loss_ref[...] = jnp.zeros_like(loss_ref)
            dgf_ref[...] = jnp.zeros_like(dgf_ref)

        x2v = x_ref[...]
        gv = g_ref[...]
        h = (_rms(x2v)[0] * gv).astype(BF16)
        h_ref[...] = h
        hp = jnp.where(i > 0, _rms(xp_ref[...])[0] * gv, 0.0).astype(BF16)
        h_ext = jnp.concatenate([hp, h], axis=0)
        down = jnp.zeros((TM_FFN, D_MODEL), F32)
        for j in range(half):
            cg, ug = _ffn_conv(h_ext, wup_ref, wfc_ref, bfc_ref, j)
            cv, uv = _ffn_conv(h_ext, wup_ref, wfc_ref, bfc_ref, j + half)
            c_ref[j] = cg
            c_ref[j + half] = cv
            u_ref[j] = ug.astype(BF16)
            u_ref[j + half] = uv.astype(BF16)
            a = ((cg * _sigmoid(cg)) * cv).astype(BF16)
            act_ref[j] = a
            down = down + _dot(a, wd_ref[j])
        x3 = x2v + down
        xh, r = _rms(x3)
        gf = gf_ref[...]
        e = xh * gf - t_ref[...]
        loss_ref[...] += 0.5 * jnp.sum(jnp.sum(e * e, axis=1, keepdims=True), axis=0, keepdims=True) / D_MODEL
        dy = e * (1.0 / D_MODEL)
        dgf_ref[0:1, :] += jnp.sum(dy * xh, axis=0, keepdims=True)
        dx3_ref[...] = _rms_bwd(xh, r, gf, dy)

    row = pl.BlockSpec((TM_FFN, D_MODEL), lambda i: (i, 0))
    prev = pl.BlockSpec((HALO, D_MODEL), lambda i: (jnp.maximum(i * tb - 1, 0), 0))
    return pl.pallas_call(
        body, name="ffn_fwd", grid=(s // TM_FFN,),
        out_shape=[jax.ShapeDtypeStruct((s, D_MODEL), BF16), jax.ShapeDtypeStruct((n_ch, s, wid), BF16),
                   jax.ShapeDtypeStruct((n_ch, s, wid), F32), jax.ShapeDtypeStruct((half, s, wid), BF16),
                   jax.ShapeDtypeStruct((s, D_MODEL), F32), jax.ShapeDtypeStruct((SUBLANES, 128), F32),
                   jax.ShapeDtypeStruct((SUBLANES, D_MODEL), F32)],
        in_specs=[row, prev, _full(g.shape), _resident(w_up_g.shape), _full(w_fc.shape), _full(b_fc.shape),
                  _resident(w_down_g.shape), _full(g_final.shape), row],
        out_specs=[row, pl.BlockSpec((n_ch, TM_FFN, wid), lambda i: (0, i, 0)),
                   pl.BlockSpec((n_ch, TM_FFN, wid), lambda i: (0, i, 0)),
                   pl.BlockSpec((half, TM_FFN, wid), lambda i: (0, i, 0)), row,
                   _full((SUBLANES, 128)), _full((SUBLANES, D_MODEL))],
        compiler_params=_cparams(1),
    )(x2, x2, g, w_up_g, w_fc, b_fc, w_down_g, g_final, target)


def _ffn_bwd(dx3, up, conv, x2, g, w_up_g, w_fc, w_down_g):
    s = x2.shape[0]
    tb = TM_FFN // HALO
    last = s // HALO - 1
    n_tiles = s // TM_FFN
    n_ch, wid = w_up_g.shape[:2]
    half = n_ch // 2
    n_ext = TM_FFN + HALO

    def body(dx_ref, dxn_ref, u_ref, c_ref, cn_ref, x2_ref, g_ref, wup_ref, wfc_ref, wd_ref,
             dup_ref, dx2_ref, dg_ref, dwfc_ref, dbfc_ref):
        i = pl.program_id(0)

        @pl.when(i == 0)
        def _():
            dg_ref[...] = jnp.zeros_like(dg_ref)
            dwfc_ref[...] = jnp.zeros_like(dwfc_ref)
            dbfc_ref[...] = jnp.zeros_like(dbfc_ref)

        dxv = dx_ref[...]
        dxn = jnp.where(i < n_tiles - 1, dxn_ref[...], 0.0)
        dx_ext = jnp.concatenate([dxv, dxn], axis=0).astype(BF16)
        dh = jnp.zeros((TM_FFN, D_MODEL), F32)
        for j in range(half):
            cg = jnp.concatenate([c_ref[j], cn_ref[j]], axis=0)
            cv = jnp.concatenate([c_ref[j + half], cn_ref[j + half]], axis=0)
            dact = _dot_nt(dx_ext, wd_ref[j])
            sg = _sigmoid(cg)
            silu = cg * sg
            parts = ((j + half, dact * silu), (j, (dact * cv) * (sg + silu * (1.0 - sg))))
            for jj, dc in parts:
                u = u_ref[jj].astype(F32)
                dc0, dc1, dc2 = dc[:TM_FFN], pltpu.roll(dc, n_ext - 1, 0)[:TM_FFN], pltpu.roll(dc, n_ext - 2, 0)[:TM_FFN]
                dbfc_ref[jj:jj + 1, :] += jnp.sum(dc0, axis=0, keepdims=True)
                dwfc_ref[0, jj:jj + 1, :] += jnp.sum(dc2 * u, axis=0, keepdims=True)
                dwfc_ref[1, jj:jj + 1, :] += jnp.sum(dc1 * u, axis=0, keepdims=True)
                dwfc_ref[2, jj:jj + 1, :] += jnp.sum(dc0 * u, axis=0, keepdims=True)
                w = wfc_ref[jj]
                du = ((dc0 * w[2:3, :] + dc1 * w[1:2, :]) + dc2 * w[0:1, :]).astype(BF16)
                dup_ref[jj] = du
                dh = dh + _dot(du, wup_ref[jj])
        xh, r = _rms(x2_ref[...])
        dg_ref[0:1, :] += jnp.sum(dh * xh, axis=0, keepdims=True)
        dx2_ref[...] = dxv + _rms_bwd(xh, r, g_ref[...], dh)

    row = pl.BlockSpec((TM_FFN, D_MODEL), lambda i: (i, 0))
    nxt = pl.BlockSpec((HALO, D_MODEL), lambda i: (jnp.minimum((i + 1) * tb, last), 0))
    cur_c = pl.BlockSpec((n_ch, TM_FFN, wid), lambda i: (0, i, 0))
    nxt_c = pl.BlockSpec((n_ch, HALO, wid), lambda i: (0, jnp.minimum((i + 1) * tb, last), 0))
    return pl.pallas_call(
        body, name="ffn_bwd", grid=(n_tiles,),
        out_shape=[jax.ShapeDtypeStruct((n_ch, s, wid), BF16), jax.ShapeDtypeStruct((s, D_MODEL), F32),
                   jax.ShapeDtypeStruct((SUBLANES, D_MODEL), F32), jax.ShapeDtypeStruct((3, n_ch, wid), F32),
                   jax.ShapeDtypeStruct((n_ch, wid), F32)],
        in_specs=[row, nxt, cur_c, cur_c, nxt_c, row, _full(g.shape), _resident(w_up_g.shape), _full(w_fc.shape),
                  _resident(w_down_g.shape)],
        out_specs=[cur_c, row, _full((SUBLANES, D_MODEL)), _full((3, n_ch, wid)), _full((n_ch, wid))],
        compiler_params=_cparams(1),
    )(dx3, dx3, up, conv, conv, x2, g, w_up_g, w_fc, w_down_g)


def _xattn_bwd(dx2, o, q, k, v, w_xo, w_xq, x1, g, dep):
    s = x1.shape[0]

    def body(dx2_ref, o_ref, q_ref, k_ref, v_ref, wo_ref, wq_ref, x1_ref, g_ref, dep_ref, dq_ref, dx1_ref, dk_ref,
             dv_ref, dg_ref):
        @pl.when(pl.program_id(0) == 0)
        def _():
            dk_ref[...] = jnp.zeros_like(dk_ref)
            dv_ref[...] = jnp.zeros_like(dv_ref)
            dg_ref[...] = jnp.zeros_like(dg_ref)

        dx2v = dx2_ref[...]
        do = _dot_nt(dx2v.astype(BF16), wo_ref[...])
        dqs = []
        for h in range(N_MEM_HEADS):
            sl = slice(h * MEM_HEAD_DIM, (h + 1) * MEM_HEAD_DIM)
            qh, kh, vh = q_ref[:, sl], k_ref[:, sl], v_ref[:, sl]
            lg = _dot_nt(qh, kh) * (MEM_HEAD_DIM ** -0.5)
            p = jnp.exp(lg - jnp.max(lg, axis=-1, keepdims=True))
            p = p / jnp.sum(p, axis=-1, keepdims=True)
            doh = do[:, sl].astype(BF16)
            dp = _dot_nt(doh, vh)
            ds = (p * (dp - jnp.sum(p * dp, axis=-1, keepdims=True)) * (MEM_HEAD_DIM ** -0.5)).astype(BF16)
            dqs.append(_dot(ds, kh))
            dk_ref[:, sl] += _dot_tn(ds, qh)
            dv_ref[:, sl] += _dot_tn(p.astype(BF16), doh)
        dq = jnp.concatenate(dqs, axis=1).astype(BF16)
        dq_ref[...] = dq
        dh2 = _dot_nt(dq, wq_ref[...])
        xh, r = _rms(x1_ref[...])
        dg_ref[0:1, :] += jnp.sum(dh2 * xh, axis=0, keepdims=True)
        dx1_ref[...] = dx2v + _rms_bwd(xh, r, g_ref[...], dh2)

    row = pl.BlockSpec((TM_XATTN, D_MODEL), lambda i: (i, 0))
    return pl.pallas_call(
        body, name="xattn_bwd", grid=(s // TM_XATTN,),
        out_shape=[jax.ShapeDtypeStruct((s, D_MODEL), BF16), jax.ShapeDtypeStruct((s, D_MODEL), F32),
                   jax.ShapeDtypeStruct(k.shape, F32), jax.ShapeDtypeStruct(k.shape, F32),
                   jax.ShapeDtypeStruct((SUBLANES, D_MODEL), F32)],
        in_specs=[row, row, row, _full(k.shape), _full(v.shape), _full(w_xo.shape), _full(w_xq.shape), row,
                  _full(g.shape), ANY_SPEC],
        out_specs=[row, row, _full(k.shape), _full(k.shape), _full((SUBLANES, D_MODEL))],
        compiler_params=_cparams(1),
    )(dx2, o, q, k, v, w_xo, w_xq, x1, g, dep)


def _mem_kv_bwd(dk, dv, mem_n, mem, w_xk, w_xv):
    def body(dk_ref, dv_ref, mn_ref, mem_ref, wk_ref, wv_ref, dwk_ref, dwv_ref, dg_ref):
        dkb, dvb = dk_ref[...].astype(BF16), dv_ref[...].astype(BF16)
        mn = mn_ref[...]
        dwk_ref[...] = _dot_tn(mn, dkb).astype(BF16)
        dwv_ref[...] = _dot_tn(mn, dvb).astype(BF16)
        dmn = _dot_nt(dkb, wk_ref[...]) + _dot_nt(dvb, wv_ref[...])
        xh, _ = _rms(mem_ref[...])
        dg_ref[...] = jnp.zeros_like(dg_ref)
        dg_ref[0:1, :] = jnp.sum(dmn * xh, axis=0, keepdims=True)

    vm = pl.BlockSpec(memory_space=pltpu.VMEM)
    return pl.pallas_call(
        body, name="mem_kv_bwd",
        out_shape=[jax.ShapeDtypeStruct(w_xk.shape, BF16), jax.ShapeDtypeStruct(w_xv.shape, BF16),
                   jax.ShapeDtypeStruct((SUBLANES, D_MODEL), F32)],
        in_specs=[vm] * 6, out_specs=[vm] * 3,
        compiler_params=pltpu.CompilerParams(vmem_limit_bytes=VMEM_LIMIT),
    )(dk, dv, mem_n, mem, w_xk, w_xv)


def _mix_out_bwd(dx1, w_out, attn, gb, gc, xi, w_sc, g_a, g_c, dep):
    s = dx1.shape[0]
    tb = TM // SUBLANES

    def body(dx1_ref, wout_ref, attn_ref, gb_ref, gc_ref, xi_ref, gch_ref, xih_ref, wsc_ref, ga_ref, gcv_ref, dep_ref,
             da1, da4, da16, dd1, dd4, dd16, dgb_ref, dcv_ref, dga_ref, dgc_ref, dwsc_ref, scr):
        i = pl.program_id(0)

        @pl.when(i == 0)
        def _():
            dga_ref[...] = jnp.zeros_like(dga_ref)
            dgc_ref[...] = jnp.zeros_like(dgc_ref)
            dwsc_ref[...] = jnp.zeros_like(dwsc_ref)

        dmixed = _dot_nt(dx1_ref[...].astype(BF16), wout_ref[...])
        da, dcn = dmixed[:, :ATTN_W], dmixed[:, ATTN_W:]
        attn = attn_ref[...]
        xa, ra = _rms(attn)
        dga_ref[0:1, :] += jnp.sum(da * xa, axis=0, keepdims=True)
        dattn = _rms_bwd(xa, ra, ga_ref[...], da)
        _spread(dattn, scr, (da1, da4, da16), BF16)
        prod = dattn * attn
        dd = jnp.concatenate(
            [jnp.broadcast_to(jnp.sum(prod[:, h * HEAD_DIM:(h + 1) * HEAD_DIM], axis=-1, keepdims=True),
                              (TM, HEAD_DIM)) for h in range(N_HEADS)], axis=1)
        _spread(_narrow_heads(dd), scr, (dd1, dd4, dd16), F32)
        gbv = gb_ref[...]
        u = gc_ref[...] * xi_ref[...]
        uh = jnp.where(i > 0, gch_ref[...] * xih_ref[...], 0.0)
        u2, u1 = _shift_down(u, uh, 2), _shift_down(u, uh, 1)
        cv = (u2 * wsc_ref[0:1, :] + u1 * wsc_ref[1:2, :]) + u * wsc_ref[2:3, :]
        xc, rc = _rms(gbv * cv)
        dgc_ref[0:1, :] += jnp.sum(dcn * xc, axis=0, keepdims=True)
        dconv = _rms_bwd(xc, rc, gcv_ref[...], dcn)
        dgb_ref[...] = (dconv * cv).astype(BF16)
        dcv = dconv * gbv
        dcv_ref[...] = dcv
        dwsc_ref[0:1, :] += jnp.sum(dcv * u2, axis=0, keepdims=True)
        dwsc_ref[1:2, :] += jnp.sum(dcv * u1, axis=0, keepdims=True)
        dwsc_ref[2:3, :] += jnp.sum(dcv * u, axis=0, keepdims=True)

    row = lambda n: pl.BlockSpec((TM, n), lambda i: (i, 0))
    halo = pl.BlockSpec((SUBLANES, 512), lambda i: (jnp.maximum(i * tb - 1, 0), 0))
    acc = _full((SUBLANES, 512))
    res = pl.pallas_call(
        body, name="mix_out_bwd", grid=(s // TM,),
        out_shape=_class_shapes(s, 512, BF16) + _class_shapes(s, LANES, F32)
        + [jax.ShapeDtypeStruct((s, 512), BF16), jax.ShapeDtypeStruct((s, 512), F32)]
        + [jax.ShapeDtypeStruct((SUBLANES, 512), F32)] * 3,
        in_specs=[row(D_MODEL), _full(w_out.shape), row(512), row(512), row(512), row(512), halo, halo,
                  _full(w_sc.shape), _full(g_a.shape), _full(g_c.shape), ANY_SPEC],
        out_specs=_class_specs(512) + _class_specs(LANES) + [row(512)] * 2 + [acc] * 3,
        scratch_shapes=[pltpu.VMEM((512 // LANES, TM, LANES), F32)],
        compiler_params=_cparams(1),
    )(dx1, w_out, attn, gb, gc, xi, gc, xi, w_sc, g_a, g_c, dep)
    return res[0:3], res[3:6], res[6], res[7], res[8], res[9], res[10]


def _swa_bwd(qc, kc, vc, doc, lsec, ddc, bias, dil, dep):
    nsub, nb, ncls = _swa_steps(qc, dil)
    n128 = nsub * nb
    whole = nb == 1

    def body(q_ref, qn_ref, kp_ref, kc_ref, vp_ref, vc_ref, do_ref, don_ref, lse_ref, lsen_ref, dd_ref, ddn_ref,
             b_ref, dep_ref, dq_ref, dk_ref, dv_ref, db_ref, s_scr, dp_scr, sn_scr, dpn_scr, ds_scr, p_scr, dsn_scr,
             pn_scr):
        r, b = pl.program_id(0), pl.program_id(1)

        @pl.when((r == 0) & (b == 0))
        def _():
            db_ref[...] = jnp.zeros_like(db_ref)

        pairs = [slice(a * LANES, (a + 1) * LANES) for a in range(N_HEADS // 2)]
        blk = [slice(t * WIN, (t + 1) * WIN) for t in range(nsub)]
        last = blk[nsub - 1]
        cols = lambda t: slice(WIN, 2 * WIN) if whole and t == 0 else slice(0, 2 * WIN)
        of_head = lambda ref, c, rows, h: ref[c, rows, _head_lane(h):_head_lane(h) + 1]
        no_prev = (b == 0) & (lax.broadcasted_iota(jnp.int32, (WIN, 2 * WIN), 1) < WIN)

        def keys(prev_ref, cur_ref, c, t, sl):
            if whole and t == 0:
                return cur_ref[c, blk[0], sl]
            if t == 0:
                return jnp.concatenate([prev_ref[c, :, sl], cur_ref[c, blk[0], sl]], axis=0)
            return cur_ref[c, (t - 1) * WIN:(t + 1) * WIN, sl]

        for a, sl in enumerate(pairs):
            for c, t in [(c, t) for c in range(ncls) for t in range(nsub)]:
                k2, v2 = keys(kp_ref, kc_ref, c, t, sl), keys(vp_ref, vc_ref, c, t, sl)
                q_eo = _pair_split(q_ref[c, blk[t], sl])
                do_eo = _pair_split(do_ref[c, blk[t], sl].astype(BF16))
                for e in range(2):
                    s_scr[c * nsub + t, 2 * a + e, :, cols(t)] = _dot_nt(q_eo[e], k2)
                    dp_scr[c * nsub + t, 2 * a + e, :, cols(t)] = _dot_nt(do_eo[e], v2)
            if not whole:
                qn_eo = _pair_split(qn_ref[0, :, sl])
                don_eo = _pair_split(don_ref[0, :, sl].astype(BF16))
                for e in range(2):
                    sn_scr[2 * a + e] = _dot_nt(qn_eo[e], kc_ref[0, last, sl])
                    dpn_scr[2 * a + e] = _dot_nt(don_eo[e], vc_ref[0, last, sl])
        for c, t, h in [(c, t, h) for c in range(ncls) for t in range(nsub) for h in range(N_HEADS)]:
            i, cl = c * nsub + t, cols(t)
            lg = s_scr[i, h, :, cl] + b_ref[h, :, cl]
            if t == 0 and not whole:
                lg = jnp.where(no_prev, -jnp.inf, lg)
            p = jnp.exp(lg - of_head(lse_ref, c, blk[t], h))
            ds = p * (dp_scr[i, h, :, cl] - of_head(dd_ref, c, blk[t], h))
            db_ref[h, :, cl] += ds
            ds_scr[i, h, :, cl] = ds.astype(BF16)
            p_scr[i, h, :, cl] = p.astype(BF16)
        if not whole:
            every = slice(0, WIN)
            for h in range(N_HEADS):
                lgn = jnp.where(b + 1 < nb, sn_scr[h] + b_ref[h, :, :WIN], -jnp.inf)
                pn = jnp.exp(lgn - of_head(lsen_ref, 0, every, h))
                dsn_scr[h] = (pn * (dpn_scr[h] - of_head(ddn_ref, 0, every, h))).astype(BF16)
                pn_scr[h] = pn.astype(BF16)
        for a, sl in enumerate(pairs):
            for c in range(ncls):
                q_eo = [_pair_split(q_ref[c, blk[t], sl]) for t in range(nsub)]
                do_eo = [_pair_split(do_ref[c, blk[t], sl].astype(BF16)) for t in range(nsub)]
                if not whole:
                    q_eo.append(_pair_split(qn_ref[0, :, sl]))
                    do_eo.append(_pair_split(don_ref[0, :, sl].astype(BF16)))
                for t in range(nsub):
                    i = c * nsub + t
                    k_eo = _pair_split(keys(kp_ref, kc_ref, c, t, sl))
                    dq, dk, dv = None, None, None
                    for e in range(2):
                        h = 2 * a + e
                        terms = [_dot(ds_scr[i, h, :, cols(t)], k_eo[e]),
                                 _dot_tn(ds_scr[i, h, :, WIN:], q_eo[t][e]),
                                 _dot_tn(p_scr[i, h, :, WIN:], do_eo[t][e])]
                        if t + 1 < nsub or not whole:
                            ds_next = ds_scr[i + 1, h, :, :WIN] if t + 1 < nsub else dsn_scr[h]
                            p_next = p_scr[i + 1, h, :, :WIN] if t + 1 < nsub else pn_scr[h]
                            terms[1] += _dot_tn(ds_next, q_eo[t + 1][e])
                            terms[2] += _dot_tn(p_next, do_eo[t + 1][e])
                        dq, dk, dv = terms if e == 0 else (dq + terms[0], dk + terms[1], dv + terms[2])
                    dq_ref[c, blk[t], sl] = dq.astype(BF16)
                    dk_ref[c, blk[t], sl] = dk.astype(BF16)
                    dv_ref[c, blk[t], sl] = dv.astype(BF16)

    cur = pl.BlockSpec((ncls, nsub * WIN, 512), lambda r, b: (r, b, 0))
    prev = pl.BlockSpec((ncls, WIN, 512), lambda r, b: (r, jnp.maximum(nsub * b - 1, 0), 0))
    nxt = pl.BlockSpec((ncls, WIN, 512), lambda r, b: (r, jnp.minimum(nsub * b + nsub, n128 - 1), 0))
    cur_h = pl.BlockSpec((ncls, nsub * WIN, LANES), cur.index_map)
    nxt_h = pl.BlockSpec((ncls, WIN, LANES), nxt.index_map)
    wide, narrow = (ncls * nsub, N_HEADS, WIN, 2 * WIN), (N_HEADS, WIN, WIN)
    return pl.pallas_call(
        body, name=f"swa_bwd_d{dil}", grid=(dil // ncls, nb),
        out_shape=[jax.ShapeDtypeStruct(qc.shape, BF16)] * 3 + [jax.ShapeDtypeStruct(bias.shape, F32)],
        in_specs=[cur, nxt, prev, cur, prev, cur, cur, nxt, cur_h, nxt_h, cur_h, nxt_h, _full(bias.shape),
                  ANY_SPEC],
        out_specs=[cur] * 3 + [_full(bias.shape)],
        scratch_shapes=[pltpu.VMEM(wide, F32), pltpu.VMEM(wide, F32), pltpu.VMEM(narrow, F32),
                        pltpu.VMEM(narrow, F32), pltpu.VMEM(wide, BF16), pltpu.VMEM(wide, BF16),
                        pltpu.VMEM(narrow, BF16), pltpu.VMEM(narrow, BF16)],
        compiler_params=_cparams(2),
    )(qc, qc, kc, kc, vc, vc, doc, doc, lsec, lsec, ddc, ddc, bias, dep)


def _in_proj_bwd(dqs, dks, dvs, dgb, dcv, gc, xi, w_sc, w_in_g, x, g_mix, dx1):
    s = x.shape[0]
    tb = TM // SUBLANES
    last = s // SUBLANES - 1
    n_tiles = s // TM

    def body(dq1, dq4, dq16, dk1, dk4, dk16, dv1, dv4, dv16, dgb_ref, dcv_ref, dcvn_ref, gc_ref, xi_ref, wsc_ref,
             w_hbm, x_ref, g_ref, dx1_ref, dproj_ref, gx_ref, dg_ref, scr_a, scr_b, w_scr, w_sems):
        i = pl.program_id(0)
        _load_w_in_pairs(w_hbm, w_scr, w_sems)

        @pl.when(i == 0)
        def _():
            dg_ref[...] = jnp.zeros_like(dg_ref)

        d0 = dcv_ref[...]
        dn = jnp.where(i < n_tiles - 1, dcvn_ref[...], 0.0)
        du = (d0 * wsc_ref[2:3, :] + _shift_up(d0, dn, 1) * wsc_ref[1:2, :]) + _shift_up(d0, dn, 2) * wsc_ref[0:1, :]
        merge = lambda a, b4, b16: ((a[...].astype(F32) + _gather_classes(b4, scr_a, 4))
                                    + _gather_classes(b16, scr_b, 16))
        dq = merge(dq1, dq4, dq16) * (HEAD_DIM ** -0.5)
        dk = merge(dk1, dk4, dk16)
        dv = merge(dv1, dv4, dv16)
        dproj = jnp.concatenate([dq, dk, dv, dgb_ref[...].astype(F32), du * xi_ref[...], du * gc_ref[...]],
                                axis=1).astype(BF16)
        dproj_ref[...] = dproj
        dh = jnp.zeros((TM, D_MODEL), F32)
        for j in range(N_DEV // 2):
            dh = dh + _dot_nt(dproj[:, 2 * j * IN_CHUNK:2 * (j + 1) * IN_CHUNK], w_scr[j])
        xh, r = _rms(x_ref[...])
        dg_ref[0:1, :] += jnp.sum(dh * xh, axis=0, keepdims=True)
        gx_ref[...] = dx1_ref[...] + _rms_bwd(xh, r, g_ref[...], dh)

    row = lambda n: pl.BlockSpec((TM, n), lambda i: (i, 0))
    nxt = pl.BlockSpec((SUBLANES, 512), lambda i: (jnp.minimum((i + 1) * tb, last), 0))
    return pl.pallas_call(
        body, name="in_proj_bwd", grid=(n_tiles,),
        out_shape=[jax.ShapeDtypeStruct((s, IN_COLS), BF16), jax.ShapeDtypeStruct((s, D_MODEL), F32),
                   jax.ShapeDtypeStruct((SUBLANES, D_MODEL), F32)],
        in_specs=_class_specs(512) * 3 + [row(512), row(512), nxt, row(512), row(512), _full(w_sc.shape),
                                          ANY_SPEC, row(D_MODEL), _full(g_mix.shape), row(D_MODEL)],
        out_specs=[row(IN_COLS), row(D_MODEL), _full((SUBLANES, D_MODEL))],
        scratch_shapes=[pltpu.VMEM((512 // LANES, TM, LANES), F32)] * 2 + W_IN_PAIRS,
        compiler_params=_cparams(1),
    )(*dqs, *dks, *dvs, dgb, dcv, dcv, gc, xi, w_sc, w_in_g, x, g_mix, dx1)


def _dw(a, b, dep, name, a_chunked=False, b_chunked=False, n_chunks=1, chunk_cols=None, per_step=1):
    single = not (a_chunked or b_chunked or chunk_cols)
    wide = a_chunked and a.shape[2] > D_MODEL
    ts = TS_DW // 4 if single else TS_DW // 2 if wide else TS_DW
    if a_chunked:
        nj, s, kk = a.shape
        nn = b.shape[1]
        a_spec = pl.BlockSpec((1, ts, kk), lambda j, t: (j, t, 0))
        b_spec = pl.BlockSpec((ts, nn), lambda j, t: (t, 0))
    elif b_chunked:
        nj, s, nn = b.shape
        kk = a.shape[1]
        a_spec = pl.BlockSpec((ts, kk), lambda j, t: (t, 0))
        b_spec = pl.BlockSpec((1, ts, nn), lambda j, t: (j, t, 0))
    else:
        s, kk = a.shape
        nj, nn = (n_chunks // per_step, chunk_cols * per_step) if chunk_cols else (1, b.shape[1])
        a_spec = pl.BlockSpec((ts, kk), lambda j, t: (t, 0))
        b_spec = pl.BlockSpec((ts, nn), lambda j, t: (t, j))
    n_steps = s // ts

    def body(a_ref, b_ref, dep_ref, o_ref, acc):
        t = pl.program_id(1)

        @pl.when(t == 0)
        def _():
            acc[...] = jnp.zeros_like(acc)

        av = (a_ref[0] if a_chunked else a_ref[...]).astype(BF16)
        bv = (b_ref[0] if b_chunked else b_ref[...]).astype(BF16)
        acc[...] += _dot_tn(av, bv)

        @pl.when(t == n_steps - 1)
        def _():
            for q in range(per_step):
                o_ref[q] = acc[:, q * nn // per_step:(q + 1) * nn // per_step].astype(BF16)

    return pl.pallas_call(
        body, name=name, grid=(nj, n_steps),
        out_shape=jax.ShapeDtypeStruct((nj * per_step, kk, nn // per_step), BF16),
        in_specs=[a_spec, b_spec, ANY_SPEC],
        out_specs=pl.BlockSpec((per_step, kk, nn // per_step), lambda j, t: (j, 0, 0)),
        scratch_shapes=[pltpu.VMEM((kk, nn), F32)],
        compiler_params=_cparams(2),
    )(a, b, dep)


def _adamw_math(w, g, m, v):
    m2 = ADAM_B1 * m + (1.0 - ADAM_B1) * g
    v2 = ADAM_B2 * v + (1.0 - ADAM_B2) * (g * g)
    m_hat = m2 / (1.0 - ADAM_B1 ** ADAM_STEP)
    v_hat = v2 / (1.0 - ADAM_B2 ** ADAM_STEP)
    delta = -ADAM_LR * (m_hat / (jnp.sqrt(v_hat) + ADAM_EPS) + ADAM_WD * w)
    return delta, m2, v2


def _sum_parts(me, own, p_ref):
    g = None
    for i in range(N_DEV):
        part = jnp.where(me == i, own.astype(F32), p_ref[i].astype(F32))
        g = part if g is None else g + part
    return g


def _adamw_big(name, w, sent, parts, m, v, me_arr):
    rr, cc = w.shape
    tr = rr // 4 if rr >= 512 else rr

    def body(me_ref, w_ref, own_ref, p_ref, m_ref, v_ref, g_ref, d_ref, nm_ref, nv_ref):
        g = own_ref[0].astype(F32)
        for k in range(1, N_DEV):
            g = g + p_ref[(me_ref[0] + k) % N_DEV].astype(F32)
        g_ref[...] = g
        d_ref[...], nm_ref[...], nv_ref[...] = _adamw_math(w_ref[...], g, m_ref[...], v_ref[...])

    row = pl.BlockSpec((tr, cc), lambda i, me: (i, 0))
    return pl.pallas_call(
        body, name=name,
        grid_spec=pltpu.PrefetchScalarGridSpec(
            num_scalar_prefetch=1, grid=(rr // tr,),
            in_specs=[row, pl.BlockSpec((1, tr, cc), lambda i, me: (me[0], i, 0)),
                      pl.BlockSpec((N_DEV, tr, cc), lambda i, me: (0, i, 0)), row, row],
            out_specs=[row] * 4),
        out_shape=[jax.ShapeDtypeStruct((rr, cc), F32)] * 4,
        compiler_params=_cparams(1),
    )(me_arr, w, sent, parts, m, v)


def _small_slices():
    return [
        (slice(ROW_RELB, ROW_RELB + 8), slice(0, N_BUCKETS)),
        (slice(ROW_GMIX, ROW_GMIX + 1), slice(0, D_MODEL)),
        (slice(ROW_GAC, ROW_GAC + 1), slice(0, ATTN_W)),
        (slice(ROW_GAC, ROW_GAC + 1), slice(ATTN_W, D_MODEL)),
        (slice(ROW_GXATTN, ROW_GXATTN + 1), slice(0, D_MODEL)),
        (slice(ROW_GMEM, ROW_GMEM + 1), slice(0, D_MODEL)),
        (slice(ROW_GFFN, ROW_GFFN + 1), slice(0, D_MODEL)),
        (slice(ROW_BFC, ROW_BFC + 8), slice(0, UP_CHUNK)),
        (slice(ROW_GFINAL, ROW_GFINAL + 1), slice(0, D_MODEL)),
    ]


def _adamw_small(own, parts, wmv, me_arr):
    slices = _small_slices()
    n = len(slices)

    def body(*refs):
        me_ref, own_ref, p_ref = refs[:3]
        ins = refs[3:3 + 3 * n]
        g_ref = refs[3 + 3 * n]
        outs = refs[4 + 3 * n:]
        g = _sum_parts(me_ref[0], own_ref[...], p_ref)
        g_ref[...] = g
        for a, (rs, ls) in enumerate(slices):
            ga = g[rs, ls]
            outs[4 * a][...] = ga
            outs[4 * a + 1][...], outs[4 * a + 2][...], outs[4 * a + 3][...] = _adamw_math(
                ins[3 * a][...], ga, ins[3 * a + 1][...], ins[3 * a + 2][...])

    vm = pl.BlockSpec(memory_space=pltpu.VMEM)
    flat = [t for trip in wmv for t in trip]
    out_shape = [jax.ShapeDtypeStruct((SMALL_ROWS, D_MODEL), F32)]
    for w, _, _ in wmv:
        out_shape += [jax.ShapeDtypeStruct(w.shape, F32)] * 4
    res = pl.pallas_call(
        body, name="adamw_small", out_shape=out_shape,
        in_specs=[SMEM_SPEC] + [vm] * (2 + 3 * n), out_specs=[vm] * len(out_shape),
    )(me_arr, own, parts, *flat)
    return res[0], [res[1 + 4 * a:5 + 4 * a] for a in range(n)]


def _adamw_shards(items):
    n = len(items)

    def body(*refs):
        for a in range(n):
            w_ref, g_ref, m_ref, v_ref = refs[4 * a:4 * a + 4]
            d_ref, nm_ref, nv_ref = refs[4 * n + 3 * a:4 * n + 3 * a + 3]
            d_ref[...], nm_ref[...], nv_ref[...] = _adamw_math(w_ref[...], g_ref[...], m_ref[...], v_ref[...])

    vm = pl.BlockSpec(memory_space=pltpu.VMEM)
    out_shape = []
    for w, _, _, _ in items:
        out_shape += [jax.ShapeDtypeStruct(w.shape, F32)] * 3
    res = pl.pallas_call(
        body, name="adamw_shards", out_shape=out_shape, in_specs=[vm] * (4 * n), out_specs=[vm] * (3 * n),
    )(*[t for it in items for t in it])
    return [res[3 * a:3 * a + 3] for a in range(n)]


def _mesh_pos():
    return lax.axis_index("x"), lax.axis_index("y"), lax.axis_index("c")


def _dev_index(p):
    return 4 * p[0] + 2 * p[1] + p[2]


def _all_gather(shards):
    n = len(shards)

    def body(*refs):
        ins, outs = refs[:n], refs[n:2 * n]
        send_sems, recv_sems, loc_sems = refs[2 * n:]
        x, y, c = _mesh_pos()
        me, sib = (x, y, c), (x, y, 1 - c)
        chips = [(1 - x, y), (x, 1 - y), (1 - x, 1 - y)]

        def cp(a, k, block, to, src=None):
            dst = outs[a].at[_dev_index(block)]
            return pltpu.make_async_remote_copy(
                src_ref=dst if src is None else src, dst_ref=dst, send_sem=send_sems.at[a, k],
                recv_sem=recv_sems.at[a, k], device_id=to, device_id_type=MESH)

        mine = [pltpu.make_async_copy(ins[a], outs[a].at[_dev_index(me)], loc_sems.at[a]) for a in range(n)]
        for m_ in mine:
            m_.start()
        first = []
        for a in range(n):
            first.append(cp(a, 0, me, sib, src=ins[a]))
            first += [cp(a, 1 + j, me, (*chip, c), src=ins[a]) for j, chip in enumerate(chips)]
        for f in first:
            f.start()
        passed = []
        for a in range(n):
            for j, chip in enumerate(chips):
                cp(a, 1 + j, (*chip, c), me).wait_recv()
                fwd = cp(a, 4 + j, (*chip, c), sib)
                fwd.start()
                passed.append(fwd)
        for a in range(n):
            cp(a, 0, sib, me).wait_recv()
            for j, chip in enumerate(chips):
                cp(a, 4 + j, (*chip, 1 - c), me).wait_recv()
        for f in first + passed:
            f.wait_send()
        for m_ in mine:
            m_.wait()

    hbm = pl.BlockSpec(memory_space=pltpu.HBM)
    return pl.pallas_call(
        body, name="all_gather_weights",
        out_shape=[jax.ShapeDtypeStruct((N_DEV,) + a.shape, a.dtype) for a in shards],
        in_specs=[hbm] * n, out_specs=[hbm] * n,
        scratch_shapes=[pltpu.SemaphoreType.DMA((n, 7)), pltpu.SemaphoreType.DMA((n, 7)),
                        pltpu.SemaphoreType.DMA((n,))],
    )(*shards)


def _peers():
    x, y, c = _mesh_pos()
    return (x, y, c), [((1 - x) if k & 4 else x, (1 - y) if k & 2 else y, (1 - c) if k & 1 else c)
                       for k in range(1, 8)]


def _exchange_copy(src_ref, land_ref, whole, send_sems, recv_sems, a, k, peer, slot):
    src = src_ref if whole else src_ref.at[_dev_index(peer)]
    return pltpu.make_async_remote_copy(
        src_ref=src, dst_ref=land_ref.at[slot], send_sem=send_sems.at[7 * a + k], recv_sem=recv_sems.at[7 * a + k],
        device_id=peer, device_id_type=MESH)


def _exchange_start(name, srcs, whole, dep):
    n = len(srcs)
    lands = [lax.empty(((N_DEV,) + s.shape) if w else s.shape, s.dtype) for s, w in zip(srcs, whole)]

    def body(*refs):
        src_refs, land_refs = refs[:n], refs[n:2 * n]
        send_sems, recv_sems, token = refs[2 * n + 1], refs[2 * n + 2], refs[-1]
        me, peers = _peers()
        for a in range(n):
            for k, peer in enumerate(peers):
                _exchange_copy(src_refs[a], land_refs[a], whole[a], send_sems, recv_sems, a, k, peer,
                               _dev_index(me)).start()
        token[...] = jnp.zeros_like(token)

    res = pl.pallas_call(
        body, name=name,
        out_shape=(pltpu.SemaphoreType.DMA((7 * n,)), pltpu.SemaphoreType.DMA((7 * n,)),
                   *[pltpu.HBM(a.shape, a.dtype) for a in srcs], *[pltpu.HBM(a.shape, a.dtype) for a in lands],
                   jax.ShapeDtypeStruct((SUBLANES, 128), F32)),
        in_specs=[HBM_SPEC] * (2 * n) + [ANY_SPEC],
        out_specs=(SEM_SPEC, SEM_SPEC, *([HBM_SPEC] * (2 * n)), VMEM_SPEC),
        input_output_aliases={i: 2 + i for i in range(2 * n)},
        compiler_params=pltpu.CompilerParams(has_side_effects=DATAFLOW),
    )(*[pltpu.with_memory_space_constraint(a, pltpu.HBM) for a in srcs],
      *[pltpu.with_memory_space_constraint(a, pltpu.HBM) for a in lands], dep)
    return res[0], res[1], list(res[2:2 + n]), list(res[2 + n:2 + 2 * n]), res[-1]


def _exchange_wait(name, started, whole, after, which=None):
    send_sems, recv_sems, srcs, lands, _ = started
    which = list(range(len(srcs))) if which is None else which
    srcs, lands = [srcs[a] for a in which], [lands[a] for a in which]
    n = len(srcs)

    def body(*refs):
        src_refs, land_refs = refs[:n], refs[n:2 * n]
        send_sems, recv_sems = refs[2 * n], refs[2 * n + 1]
        _, peers = _peers()
        for i, a in enumerate(which):
            for k, peer in enumerate(peers):
                cp = _exchange_copy(src_refs[i], land_refs[i], whole[a], send_sems, recv_sems, a, k, peer,
                                    _dev_index(peer))
                cp.wait_send()
                cp.wait_recv()

    res = pl.pallas_call(
        body, name=name,
        out_shape=[pltpu.HBM(a.shape, a.dtype) for a in srcs + lands],
        in_specs=[HBM_SPEC] * (2 * n) + [SEM_SPEC, SEM_SPEC, ANY_SPEC],
        out_specs=[HBM_SPEC] * (2 * n),
        input_output_aliases={i: i for i in range(2 * n)},
        compiler_params=pltpu.CompilerParams(has_side_effects=DATAFLOW),
    )(*srcs, *lands, send_sems, recv_sems, after)
    return list(res[:n]), list(res[n:])


def _gather_start(name, shards, dep):
    n = len(shards)
    lands = [lax.empty((N_DEV,) + a.shape, a.dtype) for a in shards]

    def body(*refs):
        src_refs, land_refs = refs[:n], refs[n:2 * n]
        send_sems, recv_sems, token = refs[2 * n + 1], refs[2 * n + 2], refs[-1]
        x, y, c = _mesh_pos()
        peers = [(x, y, 1 - c), (1 - x, y, c), (x, 1 - y, c), (1 - x, 1 - y, c)]
        for a in range(n):
            for k, peer in enumerate(peers):
                pltpu.make_async_remote_copy(
                    src_ref=src_refs[a], dst_ref=land_refs[a].at[_dev_index((x, y, c))], send_sem=send_sems.at[4 * a + k],
                    recv_sem=recv_sems.at[4 * a + k], device_id=peer, device_id_type=MESH).start()
        token[...] = jnp.zeros_like(token)

    res = pl.pallas_call(
        body, name=name,
        out_shape=(pltpu.SemaphoreType.DMA((4 * n,)), pltpu.SemaphoreType.DMA((4 * n,)),
                   *[pltpu.HBM(a.shape, a.dtype) for a in shards], *[pltpu.HBM(a.shape, a.dtype) for a in lands],
                   jax.ShapeDtypeStruct((SUBLANES, 128), F32)),
        in_specs=[HBM_SPEC] * (2 * n) + [ANY_SPEC],
        out_specs=(SEM_SPEC, SEM_SPEC, *([HBM_SPEC] * (2 * n)), VMEM_SPEC),
        input_output_aliases={i: 2 + i for i in range(2 * n)},
        compiler_params=pltpu.CompilerParams(has_side_effects=DATAFLOW),
    )(*[pltpu.with_memory_space_constraint(a, pltpu.HBM) for a in shards],
      *[pltpu.with_memory_space_constraint(a, pltpu.HBM) for a in lands], dep)
    return res[0], res[1], list(res[2:2 + n]), list(res[2 + n:2 + 2 * n]), res[-1]


def _gather_forward(name, send_sems, recv_sems, lands, which, after):
    n = len(which)

    def body(*refs):
        land_refs = refs[:n]
        send_sems, recv_sems = refs[n], refs[n + 1]
        fsend, frecv, token = refs[n + 3], refs[n + 4], refs[-1]
        x, y, c = _mesh_pos()
        chips = [(1 - x, y), (x, 1 - y), (1 - x, 1 - y)]
        for i, a in enumerate(which):
            for j, chip in enumerate(chips):
                block = land_refs[i].at[_dev_index((*chip, c))]
                pltpu.make_async_remote_copy(
                    src_ref=block, dst_ref=block, send_sem=send_sems.at[4 * a + 1 + j], recv_sem=recv_sems.at[4 * a + 1 + j],
                    device_id=(*chip, c), device_id_type=MESH).wait_recv()
                pltpu.make_async_remote_copy(
                    src_ref=block, dst_ref=block, send_sem=fsend.at[3 * i + j], recv_sem=frecv.at[3 * i + j],
                    device_id=(x, y, 1 - c), device_id_type=MESH).start()
        token[...] = jnp.zeros_like(token)

    res = pl.pallas_call(
        body, name=name,
        out_shape=(pltpu.SemaphoreType.DMA((3 * n,)), pltpu.SemaphoreType.DMA((3 * n,)),
                   *[pltpu.HBM(a.shape, a.dtype) for a in lands], jax.ShapeDtypeStruct((SUBLANES, 128), F32)),
        in_specs=[HBM_SPEC] * n + [SEM_SPEC, SEM_SPEC, ANY_SPEC],
        out_specs=(SEM_SPEC, SEM_SPEC, *([HBM_SPEC] * n), VMEM_SPEC),
        input_output_aliases={i: 2 + i for i in range(n)},
        compiler_params=pltpu.CompilerParams(has_side_effects=DATAFLOW),
    )(*lands, send_sems, recv_sems, after)
    return res[0], res[1], list(res[2:2 + n]), res[-1]


def _gather_wait(name, send_sems, recv_sems, fsend, frecv, srcs, lands, which, after):
    n = len(which)

    def body(*refs):
        land_refs = refs[n:2 * n]
        send_sems, recv_sems, fsend, frecv = refs[2 * n:2 * n + 4]
        x, y, c = _mesh_pos()
        sib = (x, y, 1 - c)
        chips = [(1 - x, y), (x, 1 - y), (1 - x, 1 - y)]
        for i, a in enumerate(which):
            def cp(slot, ssem, rsem):
                block = land_refs[i].at[_dev_index(slot)]
                return pltpu.make_async_remote_copy(src_ref=block, dst_ref=block, send_sem=ssem, recv_sem=rsem,
                                                    device_id=sib, device_id_type=MESH)
            cp(sib, send_sems.at[4 * a], recv_sems.at[4 * a]).wait_recv()
            for j, chip in enumerate(chips):
                cp((*chip, 1 - c), fsend.at[3 * i + j], frecv.at[3 * i + j]).wait_recv()
            for k in range(4):
                cp(sib, send_sems.at[4 * a + k], recv_sems.at[4 * a + k]).wait_send()
            for j in range(3):
                cp(sib, fsend.at[3 * i + j], frecv.at[3 * i + j]).wait_send()

    res = pl.pallas_call(
        body, name=name,
        out_shape=[pltpu.HBM(a.shape, a.dtype) for a in srcs + lands],
        in_specs=[HBM_SPEC] * (2 * n) + [SEM_SPEC] * 4 + [ANY_SPEC],
        out_specs=[HBM_SPEC] * (2 * n),
        input_output_aliases={i: i for i in range(2 * n)},
        compiler_params=pltpu.CompilerParams(has_side_effects=DATAFLOW),
    )(*srcs, *lands, send_sems, recv_sems, fsend, frecv, after)
    return list(res[n:])


def _local_step(x, mem, target, rel_bias, g_mix, w_in_g, w_sc, g_a, g_c, g_xattn, g_mem, g_ffn, w_fc, b_fc, g_final,
                dep, forward_weights, late_weights, emit, emit_small):
    s = x.shape[0]
    buckets = _bucket_tables()
    bias = _bias_fwd(rel_bias, buckets)

    h1, qs, ks, vs, gb, gc, xi = _rms_proj(x, g_mix, w_in_g, dep)
    qs, ks, vs = ([a[0][None]] + list(a[1:]) for a in (qs, ks, vs))
    group1, group2 = ["w_out", "w_xq", "w_xk", "w_xv", "w_xo"], ["w_up", "w_down"]
    tok = forward_weights(group1, h1)
    branches = []
    for p, dil in enumerate(DILATIONS):
        o_p, lse_p = _swa_fwd(qs[p], ks[p], vs[p], bias[p], dil, tok)
        branches.append([o_p[0], lse_p[0]] if dil == 1 else [o_p, lse_p])
    lw = late_weights(group1, branches[-1][0])
    w_out, w_xq, w_xk, w_xv, w_xo = (lw[n] for n in group1)
    attn, lses, mixed, x1 = _mix_out(branches, gb, gc, xi, x, w_sc, g_a, g_c, w_out)
    tok = forward_weights(group2, x1)
    mem_n, mk, mv = _mem_kv(mem, g_mem, w_xk, w_xv)
    h2, xq, xo, x2 = _xattn_fwd(x1, g_xattn, w_xq, mk, mv, w_xo, tok)
    lw = late_weights(group2, x2)
    w_up_g = lw["w_up"].reshape(FFN_CHUNKS, FFN_WIDTH, D_MODEL)
    w_down_g = lw["w_down"].reshape(FFN_CHUNKS // 2, FFN_WIDTH, D_MODEL)
    pairs = lambda a: a.reshape(FFN_CHUNKS, 2, a.shape[1], UP_CHUNK).transpose(0, 2, 1, 3).reshape(
        FFN_CHUNKS, a.shape[1], FFN_WIDTH)
    w_fc, b_fc = pairs(w_fc), pairs(b_fc)
    h3, up, conv, act, dx3, loss_acc, dg_final = _ffn_fwd(x2, g_ffn, w_up_g, w_fc, b_fc, w_down_g, g_final, target)

    gw_down = _dw(act, dx3, dep, "dw_down", a_chunked=True).reshape(N_DEV // 2, UP_CHUNK, D_MODEL)
    dup, dx2, dg_ffn, dw_fc, db_fc = _ffn_bwd(dx3, up, conv, x2, g_ffn, w_up_g, w_fc, w_down_g)
    gw_up = _dw(dup, h3, dep, "dw_up", a_chunked=True).reshape(N_DEV, UP_CHUNK, D_MODEL)
    tok = emit(dict(w_down=gw_down, w_up=gw_up))
    dxq, dx1, dmk, dmv, dg_xattn = _xattn_bwd(dx2, xo, xq, mk, mv, w_xo, w_xq, x1, g_xattn, tok)
    gw_xo = _dw(xo, dx2, tok, "dw_xo")[0]
    gw_xq = _dw(h2, dxq, tok, "dw_xq")[0]
    gw_xk, gw_xv, dg_mem = _mem_kv_bwd(dmk, dmv, mem_n, mem, w_xk, w_xv)
    tok = emit(dict(w_xo=gw_xo, w_xq=gw_xq, w_xk=gw_xk, w_xv=gw_xv))
    dattns, dds, dgb, dcv, dg_a, dg_c, dw_sc = _mix_out_bwd(dx1, w_out, attn, gb, gc, xi, w_sc, g_a, g_c, tok)
    first = lambda a: [a[0][None]] + list(a[1:])
    dattns, dds, lses = first(dattns), first(dds), first(lses)
    gw_out = _dw(mixed, dx1, tok, "dw_out")[0]
    tok = emit(dict(w_out=gw_out))
    dqs, dks, dvs, dbias = [], [], [], []
    for p, dil in enumerate(DILATIONS):
        dq_p, dk_p, dv_p, db_p = _swa_bwd(qs[p], ks[p], vs[p], dattns[p], lses[p], dds[p], bias[p], dil, tok)
        dqs.append(dq_p[0] if dil == 1 else dq_p)
        dks.append(dk_p[0] if dil == 1 else dk_p)
        dvs.append(dv_p[0] if dil == 1 else dv_p)
        dbias.append(db_p)
    d_relb = _bias_bwd(jnp.stack(dbias), buckets)
    dproj, grad_x, dg_mix = _in_proj_bwd(dqs, dks, dvs, dgb, dcv, gc, xi, w_sc, w_in_g, x, g_mix, dx1)
    pad = lambda a: jnp.pad(a, ((0, 0), (0, D_MODEL - a.shape[1])))
    small = jnp.concatenate([
        d_relb, dg_mix, dg_xattn, dg_mem, dg_ffn, dg_final, jnp.concatenate([dg_a, dg_c], axis=1),
        pad(dw_sc), pad(db_fc.reshape(N_DEV, UP_CHUNK)), pad(dw_fc.reshape(3 * N_DEV, UP_CHUNK)), pad(loss_acc)],
        axis=0)
    tok = emit_small(small)
    gw_in = _dw(h1, dproj, tok, "dw_in", n_chunks=N_DEV, chunk_cols=IN_CHUNK, per_step=2)
    emit(dict(w_in=gw_in))
    return grad_x


def kernel(x, mem, rel_bias, g_mix, w_in, w_short_conv, g_attn_out, g_conv_out, w_out, g_xattn, g_mem, w_xq, w_xk, w_xv, w_xo, g_ffn, w_up, w_ffn_conv, b_ffn_conv, w_down, g_final, loss_target, m_rel_bias, m_g_mix, m_w_in, m_w_short_conv, m_g_attn_out, m_g_conv_out, m_w_out, m_g_xattn, m_g_mem, m_w_xq, m_w_xk, m_w_xv, m_w_xo, m_g_ffn, m_w_up, m_w_ffn_conv, m_b_ffn_conv, m_w_down, m_g_final, v_rel_bias, v_g_mix, v_w_in, v_w_short_conv, v_g_attn_out, v_g_conv_out, v_w_out, v_g_xattn, v_g_mem, v_w_xq, v_w_xk, v_w_xv, v_w_xo, v_g_ffn, v_w_up, v_w_ffn_conv, v_b_ffn_conv, v_w_down, v_g_final):
    me = _dev_index(_mesh_pos())
    me_arr = me.reshape(1).astype(jnp.int32)

    big_names = ["w_in", "w_out", "w_xq", "w_xk", "w_xv", "w_xo", "w_up", "w_down"]
    late_names = big_names[1:]
    big_w = dict(w_in=w_in[0], w_out=w_out[0], w_xq=w_xq[0], w_xk=w_xk[0], w_xv=w_xv[0], w_xo=w_xo[0],
                 w_up=w_up[0].T, w_down=w_down[0])
    big_m = dict(w_in=m_w_in[0], w_out=m_w_out[0], w_xq=m_w_xq[0], w_xk=m_w_xk[0], w_xv=m_w_xv[0], w_xo=m_w_xo[0],
                 w_up=m_w_up[0].T, w_down=m_w_down[0])
    big_v = dict(w_in=v_w_in[0], w_out=v_w_out[0], w_xq=v_w_xq[0], w_xk=v_w_xk[0], w_xv=v_w_xv[0], w_xo=v_w_xo[0],
                 w_up=v_w_up[0].T, w_down=v_w_down[0])
    shard_shape = {n: big_w[n].shape for n in big_names}

    w_in_g, w_sc_g, w_fc_full = _all_gather([big_w["w_in"].astype(BF16), w_short_conv[0], w_ffn_conv[0]])
    w_sc_full = w_sc_g.transpose(1, 0, 2).reshape(3, CONV_W)
    late_shards = [big_w[n].astype(BF16) for n in late_names]
    ag_send, ag_recv, ag_srcs, ag_lands, ag_token = _gather_start("gather_weights_start", late_shards, w_in_g)
    forwarded = {}

    def forward_weights(names, after):
        which = [late_names.index(n) for n in names]
        fsend, frecv, lands, token = _gather_forward("gather_" + "_".join(names) + "_forward", ag_send, ag_recv,
                                                     [ag_lands[a] for a in which], which, after)
        forwarded[tuple(names)] = (fsend, frecv, lands)
        return token

    def late_weights(names, after):
        which = [late_names.index(n) for n in names]
        fsend, frecv, lands = forwarded[tuple(names)]
        lands = _gather_wait("gather_" + "_".join(names) + "_wait", ag_send, ag_recv, fsend, frecv,
                             [ag_srcs[a] for a in which], lands, which, after)
        out = {}
        for n, a, land in zip(names, which, lands):
            full = lax.dynamic_update_index_in_dim(land, late_shards[a], me, 0)
            if n == "w_up":
                out[n] = full
            elif n == "w_down":
                out[n] = full.reshape(N_DEV // 2, UP_CHUNK, D_MODEL)
            else:
                out[n] = full.reshape(D_MODEL, D_MODEL)
        return out

    sent = []

    def emit(grads):
        names = list(grads)
        blocks = [grads[n].reshape((N_DEV,) + shard_shape[n]) for n in names]
        started = _exchange_start("scatter_" + "_".join(names) + "_start", blocks, [False] * len(names), me_arr)
        sent.append((names, started))
        return started[-1]

    def emit_small(small):
        sent_small.append((small, _exchange_start("gather_small_start", [small], [True], me_arr)))
        return sent_small[0][1][-1]

    sent_small = []
    grad_x = _local_step(
        x[0], mem[0], loss_target[0], rel_bias, g_mix, w_in_g, w_sc_full, g_attn_out, g_conv_out, g_xattn, g_mem,
        g_ffn, w_fc_full, b_ffn_conv.reshape(N_DEV, 1, UP_CHUNK), g_final.reshape(1, D_MODEL), ag_token,
        forward_weights, late_weights, emit, emit_small)

    small_g, small_started = sent_small[0]
    after = sent[-1][1][-1]
    small_parts = _exchange_wait("gather_small_wait", small_started, [True], after)[1][0]
    big_out = {}
    after = small_parts
    for names, started in sent:
        blocks, lands = _exchange_wait("scatter_" + "_".join(names) + "_wait", started, [False] * len(names), after)
        for n, block, land in zip(names, blocks, lands):
            res = _adamw_big("adamw_" + n, big_w[n], block, land, big_m[n], big_v[n], me_arr)
            big_out[n] = [(r.T if n == "w_up" else r)[None] for r in res]
            after = res[0]

    as_rows = lambda a: a.reshape(N_DEV, UP_CHUNK)
    row1 = lambda a: a.reshape(1, D_MODEL)
    small_names = ["rel_bias", "g_mix", "g_attn_out", "g_conv_out", "g_xattn", "g_mem", "g_ffn", "b_ffn_conv", "g_final"]
    wmv = [
        (rel_bias, m_rel_bias, v_rel_bias), (g_mix, m_g_mix, v_g_mix), (g_attn_out, m_g_attn_out, v_g_attn_out),
        (g_conv_out, m_g_conv_out, v_g_conv_out), (g_xattn, m_g_xattn, v_g_xattn), (g_mem, m_g_mem, v_g_mem),
        (g_ffn, m_g_ffn, v_g_ffn), (as_rows(b_ffn_conv), as_rows(m_b_ffn_conv), as_rows(v_b_ffn_conv)),
        (row1(g_final), row1(m_g_final), row1(v_g_final))]
    g_packed, small_res = _adamw_small(small_g, small_parts, wmv, me_arr)
    small_out = dict(zip(small_names, small_res))
    loss = g_packed[ROW_LOSS, 0]
    small_out["b_ffn_conv"] = [a.reshape(1, 2 * D_FF) for a in small_out["b_ffn_conv"]]
    small_out["g_final"] = [a.reshape(D_MODEL) for a in small_out["g_final"]]

    g_wsc = lax.dynamic_slice(g_packed[ROW_WSC:ROW_WSC + 3, 0:CONV_W], (0, me * HEAD_DIM), (3, HEAD_DIM))
    g_wfc = lax.dynamic_slice(g_packed[ROW_WFC:ROW_WFC + 3 * N_DEV, 0:UP_CHUNK].reshape(3, N_DEV, UP_CHUNK),
                              (0, me, 0), (3, 1, UP_CHUNK)).reshape(3, UP_CHUNK)
    shard_res = _adamw_shards([(w_short_conv[0], g_wsc, m_w_short_conv[0], v_w_short_conv[0]),
                               (w_ffn_conv[0], g_wfc, m_w_ffn_conv[0], v_w_ffn_conv[0])])
    small_out["w_short_conv"] = [g_wsc[None]] + [a[None] for a in shard_res[0]]
    small_out["w_ffn_conv"] = [g_wfc[None]] + [a[None] for a in shard_res[1]]

    order = ["rel_bias", "g_mix", "w_in", "w_short_conv", "g_attn_out", "g_conv_out", "w_out", "g_xattn", "g_mem",
             "w_xq", "w_xk", "w_xv", "w_xo", "g_ffn", "w_up", "w_ffn_conv", "b_ffn_conv", "w_down", "g_final"]
    allp = {**big_out, **small_out}
    outs = [loss, grad_x[None]]
    for kind in range(4):
        outs += [allp[n][kind] for n in order]
    return tuple(outs)
```
